```python
import jax, jax.numpy as jnp
from jax import lax
import numpy as np

D_MODEL = 1024
BATCH = 8
SEQ = 2048
DEPTH = 1

CHUNK = 64
PLE_DIM = 256
EPS = 1e-6
GMLP_GROUPS = 8
GMLP_GROUP_DIM = 128
GMLP_WIDTH = GMLP_GROUPS * GMLP_GROUP_DIM
GMLP_BLOCK = 128
FOX_HEADS = 16
FOX_HEAD_DIM = 64
FOX_WIDTH = FOX_HEADS * FOX_HEAD_DIM
Q_BLOCK = 128
D_FF = 2816
CONV_WIDTH = 3
N_BRANCH = 2
IN_COLS = 2 * GMLP_WIDTH + 3 * FOX_WIDTH + FOX_HEADS + N_BRANCH * D_MODEL

kernel_name = "hybrid_gmlp_fox_convffn_block"


def rmsnorm(x, g):
    x32 = x.astype(jnp.float32)
    y = x32 * lax.rsqrt(jnp.mean(x32 * x32, axis=-1, keepdims=True) + EPS)
    return (y * g.astype(jnp.float32)).astype(x.dtype)


def layernorm(x, g, b):
    x32 = x.astype(jnp.float32)
    mu = jnp.mean(x32, axis=-1, keepdims=True)
    xc = x32 - mu
    y = xc * lax.rsqrt(jnp.mean(xc * xc, axis=-1, keepdims=True) + EPS)
    return (y * g.astype(jnp.float32) + b.astype(jnp.float32)).astype(x.dtype)


def gmlp_spatial_gating(z_u, z_v, ln_g, ln_b, w_s, b_s):
    B, S, _ = z_u.shape
    v = layernorm(z_v, ln_g, ln_b)
    v = v.reshape(B, S // GMLP_BLOCK, GMLP_BLOCK, GMLP_GROUPS, GMLP_GROUP_DIM)
    pos = jnp.arange(GMLP_BLOCK)
    mask = (pos[None, :] // CHUNK) <= (pos[:, None] // CHUNK)
    w = jnp.where(mask[None], w_s, jnp.zeros_like(w_s))
    mixed = jnp.einsum('gts,bnsgc->bntgc', w, v) + b_s.T[None, None, :, :, None]
    return z_u * mixed.reshape(B, S, GMLP_WIDTH)


def forgetting_attention(q, k, v, f_logit, b_f):
    B, S, _ = q.shape
    def heads(t):
        return t.reshape(B, S, FOX_HEADS, FOX_HEAD_DIM).transpose(0, 2, 1, 3)
    q, k, v = heads(q), heads(k), heads(v)
    log_f = jax.nn.log_sigmoid(f_logit.astype(jnp.float32) + b_f.astype(jnp.float32))
    cum = jnp.cumsum(log_f, axis=1).transpose(0, 2, 1)
    scale = FOX_HEAD_DIM ** -0.5
    outs = []
    for i in range(S // Q_BLOCK):
        lo, hi = i * Q_BLOCK, (i + 1) * Q_BLOCK
        s = jnp.einsum('bhqd,bhkd->bhqk', q[:, :, lo:hi], k[:, :, :hi]).astype(jnp.float32) * scale
        s = s + cum[:, :, lo:hi, None] - cum[:, :, None, :hi]
        qpos = jnp.arange(lo, hi)
        kpos = jnp.arange(hi)
        s = jnp.where(kpos[None, :] <= qpos[:, None], s, -1e30)
        prob = jax.nn.softmax(s, axis=-1).astype(v.dtype)
        outs.append(jnp.einsum('bhqk,bhkd->bhqd', prob, v[:, :, :hi]))
    o = jnp.concatenate(outs, axis=2)
    return o.transpose(0, 2, 1, 3).reshape(B, S, FOX_WIDTH)


def causal_depthwise_conv(u, w, b):
    S = u.shape[1]
    up = jnp.pad(u, ((0, 0), (CONV_WIDTH - 1, 0), (0, 0)))
    out = b + w[0] * up[:, 0:S]
    for j in range(1, CONV_WIDTH):
        out = out + w[j] * up[:, j:j + S]
    return out


def _fwd_setup_inputs(seed: int = 0) -> dict:
    key = jax.random.key(seed)
    ks = jax.random.split(key, 24)
    f32 = jnp.float32
    def nrm(k, shape, scale):
        return jax.random.normal(k, shape, f32) * scale
    L = DEPTH
    return {
        "x": nrm(ks[0], (BATCH, SEQ, D_MODEL), 1.0),
        "p": nrm(ks[1], (DEPTH, BATCH, SEQ, PLE_DIM), 1.0),
        "norm_mix_g": 1.0 + nrm(ks[2], (L, D_MODEL), 0.02),
        "w_in": nrm(ks[3], (L, D_MODEL, IN_COLS), D_MODEL ** -0.5),
        "b_f": 2.0 + nrm(ks[4], (L, FOX_HEADS), 0.5),
        "gmlp_ln_g": 1.0 + nrm(ks[5], (L, GMLP_WIDTH), 0.02),
        "gmlp_ln_b": nrm(ks[6], (L, GMLP_WIDTH), 0.02),
        "gmlp_w_s": nrm(ks[7], (L, GMLP_GROUPS, GMLP_BLOCK, GMLP_BLOCK), GMLP_BLOCK ** -0.5),
        "gmlp_b_s": 1.0 + nrm(ks[8], (L, GMLP_GROUPS, GMLP_BLOCK), 0.1),
        "w_branch_a": nrm(ks[9], (L, GMLP_WIDTH, D_MODEL), GMLP_WIDTH ** -0.5),
        "w_branch_b": nrm(ks[10], (L, FOX_WIDTH, D_MODEL), FOX_WIDTH ** -0.5),
        "w_out": nrm(ks[11], (L, D_MODEL, D_MODEL), D_MODEL ** -0.5),
        "norm_ffn_g": 1.0 + nrm(ks[12], (L, D_MODEL), 0.02),
        "w_up": nrm(ks[13], (L, D_MODEL, 2 * D_FF), D_MODEL ** -0.5),
        "conv_w": nrm(ks[14], (L, CONV_WIDTH, 2 * D_FF), CONV_WIDTH ** -0.5),
        "conv_b": nrm(ks[15], (L, 2 * D_FF), 0.02),
        "w_down": nrm(ks[16], (L, D_FF, D_MODEL), D_FF ** -0.5),
        "norm_ple_g": 1.0 + nrm(ks[17], (L, D_MODEL), 0.02),
        "w_ple": nrm(ks[18], (L, PLE_DIM, D_MODEL), PLE_DIM ** -0.5),
        "w_ple_gate": nrm(ks[19], (L, D_MODEL, D_MODEL), D_MODEL ** -0.5),
        "norm_final_g": 1.0 + nrm(ks[20], (D_MODEL,), 0.02),
    }


def _fwd_reference(x, p, norm_mix_g, w_in, b_f, gmlp_ln_g, gmlp_ln_b, gmlp_w_s, gmlp_b_s,
              w_branch_a, w_branch_b, w_out, norm_ffn_g, w_up, conv_w, conv_b, w_down,
              norm_ple_g, w_ple, w_ple_gate, norm_final_g):
    o1 = 2 * GMLP_WIDTH
    o2 = o1 + 3 * FOX_WIDTH
    o3 = o2 + FOX_HEADS
    for i in range(DEPTH):
        h = rmsnorm(x, norm_mix_g[i])
        z = jnp.einsum('bsd,dc->bsc', h, w_in[i])
        uv = jax.nn.gelu(z[..., :o1])
        z_u, z_v = uv[..., :GMLP_WIDTH], uv[..., GMLP_WIDTH:]
        q = z[..., o1:o1 + FOX_WIDTH]
        k = z[..., o1 + FOX_WIDTH:o1 + 2 * FOX_WIDTH]
        v = z[..., o1 + 2 * FOX_WIDTH:o2]
        f_logit = z[..., o2:o3]
        gates = jax.nn.sigmoid(z[..., o3:])
        gate_a, gate_b = gates[..., :D_MODEL], gates[..., D_MODEL:]

        a = gmlp_spatial_gating(z_u, z_v, gmlp_ln_g[i], gmlp_ln_b[i], gmlp_w_s[i], gmlp_b_s[i])
        b = forgetting_attention(q, k, v, f_logit, b_f[i])
        y_a = jnp.einsum('bsc,cd->bsd', a, w_branch_a[i])
        y_b = jnp.einsum('bsc,cd->bsd', b, w_branch_b[i])
        merged = gate_a * y_a + gate_b * y_b
        x = x + jnp.einsum('bsd,de->bse', merged, w_out[i])

        h2 = rmsnorm(x, norm_ffn_g[i])
        up = jnp.einsum('bsd,df->bsf', h2, w_up[i])
        up = causal_depthwise_conv(up, conv_w[i], conv_b[i])
        act = jax.nn.gelu(up[..., :D_FF]) * up[..., D_FF:]
        x = x + jnp.einsum('bsf,fd->bsd', act, w_down[i])

        h3 = rmsnorm(x, norm_ple_g[i])
        ple = jnp.einsum('bse,ed->bsd', p[i], w_ple[i])
        x = x + ple * jax.nn.sigmoid(jnp.einsum('bsd,de->bse', h3, w_ple_gate[i]))
    return rmsnorm(x, norm_final_g)


import jax as _jax
import jax.numpy as _jnp

TWIN_FORMAT = 'train_step'
FWD_PARAMS = ['x', 'p', 'norm_mix_g', 'w_in', 'b_f', 'gmlp_ln_g', 'gmlp_ln_b', 'gmlp_w_s', 'gmlp_b_s', 'w_branch_a', 'w_branch_b', 'w_out', 'norm_ffn_g', 'w_up', 'conv_w', 'conv_b', 'w_down', 'norm_ple_g', 'w_ple', 'w_ple_gate', 'norm_final_g']
TWIN_WEIGHTS = ['norm_mix_g', 'w_in', 'b_f', 'gmlp_ln_g', 'gmlp_ln_b', 'gmlp_w_s', 'gmlp_b_s', 'w_branch_a', 'w_branch_b', 'w_out', 'norm_ffn_g', 'w_up', 'conv_w', 'conv_b', 'w_down', 'norm_ple_g', 'w_ple', 'w_ple_gate', 'norm_final_g']
TWIN_DIFF_INPUT = 'x'
TWIN_INPUTS = ['x', 'p', 'norm_mix_g', 'w_in', 'b_f', 'gmlp_ln_g', 'gmlp_ln_b', 'gmlp_w_s', 'gmlp_b_s', 'w_branch_a', 'w_branch_b', 'w_out', 'norm_ffn_g', 'w_up', 'conv_w', 'conv_b', 'w_down', 'norm_ple_g', 'w_ple', 'w_ple_gate', 'norm_final_g', 'loss_target', 'm_norm_mix_g', 'm_w_in', 'm_b_f', 'm_gmlp_ln_g', 'm_gmlp_ln_b', 'm_gmlp_w_s', 'm_gmlp_b_s', 'm_w_branch_a', 'm_w_branch_b', 'm_w_out', 'm_norm_ffn_g', 'm_w_up', 'm_conv_w', 'm_conv_b', 'm_w_down', 'm_norm_ple_g', 'm_w_ple', 'm_w_ple_gate', 'm_norm_final_g', 'v_norm_mix_g', 'v_w_in', 'v_b_f', 'v_gmlp_ln_g', 'v_gmlp_ln_b', 'v_gmlp_w_s', 'v_gmlp_b_s', 'v_w_branch_a', 'v_w_branch_b', 'v_w_out', 'v_norm_ffn_g', 'v_w_up', 'v_conv_w', 'v_conv_b', 'v_w_down', 'v_norm_ple_g', 'v_w_ple', 'v_w_ple_gate', 'v_norm_final_g']
TWIN_OUTPUTS = ['loss', 'grad_x', 'grad_norm_mix_g', 'grad_w_in', 'grad_b_f', 'grad_gmlp_ln_g', 'grad_gmlp_ln_b', 'grad_gmlp_w_s', 'grad_gmlp_b_s', 'grad_w_branch_a', 'grad_w_branch_b', 'grad_w_out', 'grad_norm_ffn_g', 'grad_w_up', 'grad_conv_w', 'grad_conv_b', 'grad_w_down', 'grad_norm_ple_g', 'grad_w_ple', 'grad_w_ple_gate', 'grad_norm_final_g', 'delta_norm_mix_g', 'delta_w_in', 'delta_b_f', 'delta_gmlp_ln_g', 'delta_gmlp_ln_b', 'delta_gmlp_w_s', 'delta_gmlp_b_s', 'delta_w_branch_a', 'delta_w_branch_b', 'delta_w_out', 'delta_norm_ffn_g', 'delta_w_up', 'delta_conv_w', 'delta_conv_b', 'delta_w_down', 'delta_norm_ple_g', 'delta_w_ple', 'delta_w_ple_gate', 'delta_norm_final_g', 'new_m_norm_mix_g', 'new_m_w_in', 'new_m_b_f', 'new_m_gmlp_ln_g', 'new_m_gmlp_ln_b', 'new_m_gmlp_w_s', 'new_m_gmlp_b_s', 'new_m_w_branch_a', 'new_m_w_branch_b', 'new_m_w_out', 'new_m_norm_ffn_g', 'new_m_w_up', 'new_m_conv_w', 'new_m_conv_b', 'new_m_w_down', 'new_m_norm_ple_g', 'new_m_w_ple', 'new_m_w_ple_gate', 'new_m_norm_final_g', 'new_v_norm_mix_g', 'new_v_w_in', 'new_v_b_f', 'new_v_gmlp_ln_g', 'new_v_gmlp_ln_b', 'new_v_gmlp_w_s', 'new_v_gmlp_b_s', 'new_v_w_branch_a', 'new_v_w_branch_b', 'new_v_w_out', 'new_v_norm_ffn_g', 'new_v_w_up', 'new_v_conv_w', 'new_v_conv_b', 'new_v_w_down', 'new_v_norm_ple_g', 'new_v_w_ple', 'new_v_w_ple_gate', 'new_v_norm_final_g']
TWIN_LEAF_KINDS = {'loss': 'loss', 'grad_x': 'grad_x', 'grad_norm_mix_g': 'grad_w', 'grad_w_in': 'grad_w', 'grad_b_f': 'grad_w', 'grad_gmlp_ln_g': 'grad_w', 'grad_gmlp_ln_b': 'grad_w', 'grad_gmlp_w_s': 'grad_w', 'grad_gmlp_b_s': 'grad_w', 'grad_w_branch_a': 'grad_w', 'grad_w_branch_b': 'grad_w', 'grad_w_out': 'grad_w', 'grad_norm_ffn_g': 'grad_w', 'grad_w_up': 'grad_w', 'grad_conv_w': 'grad_w', 'grad_conv_b': 'grad_w', 'grad_w_down': 'grad_w', 'grad_norm_ple_g': 'grad_w', 'grad_w_ple': 'grad_w', 'grad_w_ple_gate': 'grad_w', 'grad_norm_final_g': 'grad_w', 'delta_norm_mix_g': 'delta_w', 'delta_w_in': 'delta_w', 'delta_b_f': 'delta_w', 'delta_gmlp_ln_g': 'delta_w', 'delta_gmlp_ln_b': 'delta_w', 'delta_gmlp_w_s': 'delta_w', 'delta_gmlp_b_s': 'delta_w', 'delta_w_branch_a': 'delta_w', 'delta_w_branch_b': 'delta_w', 'delta_w_out': 'delta_w', 'delta_norm_ffn_g': 'delta_w', 'delta_w_up': 'delta_w', 'delta_conv_w': 'delta_w', 'delta_conv_b': 'delta_w', 'delta_w_down': 'delta_w', 'delta_norm_ple_g': 'delta_w', 'delta_w_ple': 'delta_w', 'delta_w_ple_gate': 'delta_w', 'delta_norm_final_g': 'delta_w', 'new_m_norm_mix_g': 'new_m', 'new_m_w_in': 'new_m', 'new_m_b_f': 'new_m', 'new_m_gmlp_ln_g': 'new_m', 'new_m_gmlp_ln_b': 'new_m', 'new_m_gmlp_w_s': 'new_m', 'new_m_gmlp_b_s': 'new_m', 'new_m_w_branch_a': 'new_m', 'new_m_w_branch_b': 'new_m', 'new_m_w_out': 'new_m', 'new_m_norm_ffn_g': 'new_m', 'new_m_w_up': 'new_m', 'new_m_conv_w': 'new_m', 'new_m_conv_b': 'new_m', 'new_m_w_down': 'new_m', 'new_m_norm_ple_g': 'new_m', 'new_m_w_ple': 'new_m', 'new_m_w_ple_gate': 'new_m', 'new_m_norm_final_g': 'new_m', 'new_v_norm_mix_g': 'new_v', 'new_v_w_in': 'new_v', 'new_v_b_f': 'new_v', 'new_v_gmlp_ln_g': 'new_v', 'new_v_gmlp_ln_b': 'new_v', 'new_v_gmlp_w_s': 'new_v', 'new_v_gmlp_b_s': 'new_v', 'new_v_w_branch_a': 'new_v', 'new_v_w_branch_b': 'new_v', 'new_v_w_out': 'new_v', 'new_v_norm_ffn_g': 'new_v', 'new_v_w_up': 'new_v', 'new_v_conv_w': 'new_v', 'new_v_conv_b': 'new_v', 'new_v_w_down': 'new_v', 'new_v_norm_ple_g': 'new_v', 'new_v_w_ple': 'new_v', 'new_v_w_ple_gate': 'new_v', 'new_v_norm_final_g': 'new_v'}


def _forward(args):
    return _fwd_reference(*[args[k] for k in FWD_PARAMS])


def _output_shape():
    out = _jax.eval_shape(lambda: _forward(_fwd_setup_inputs(0)))
    return out.shape, out.dtype

N_MICROBATCH = 1
ADAM_LR = 0.001
ADAM_B1 = 0.9
ADAM_B2 = 0.999
ADAM_EPS = 1e-08
ADAM_WD = 0.01
ADAM_STEP = 10
PER_EXAMPLE_BATCH_AXIS = {'x': 0, 'p': 1, 'loss_target': 0}
SHARED_INPUTS = []
_WEIGHT_DTYPES = {'norm_mix_g': _jnp.float32, 'w_in': _jnp.float32, 'b_f': _jnp.float32, 'gmlp_ln_g': _jnp.float32, 'gmlp_ln_b': _jnp.float32, 'gmlp_w_s': _jnp.float32, 'gmlp_b_s': _jnp.float32, 'w_branch_a': _jnp.float32, 'w_branch_b': _jnp.float32, 'w_out': _jnp.float32, 'norm_ffn_g': _jnp.float32, 'w_up': _jnp.float32, 'conv_w': _jnp.float32, 'conv_b': _jnp.float32, 'w_down': _jnp.float32, 'norm_ple_g': _jnp.float32, 'w_ple': _jnp.float32, 'w_ple_gate': _jnp.float32, 'norm_final_g': _jnp.float32}
MOMENT_SCALE = {'norm_mix_g': 8.633929e-02, 'w_in': 3.178509e-02, 'b_f': 1.817029e-01, 'gmlp_ln_g': 3.543209e-02, 'gmlp_ln_b': 3.653301e-02, 'gmlp_w_s': 3.588814e-02, 'gmlp_b_s': 4.178612e-02, 'w_branch_a': 5.531593e-02, 'w_branch_b': 2.755894e-02, 'w_out': 6.164639e-02, 'norm_ffn_g': 8.486139e-02, 'w_up': 3.625565e-02, 'conv_w': 3.842585e-02, 'conv_b': 3.642360e-02, 'w_down': 5.913411e-02, 'norm_ple_g': 2.008144e-02, 'w_ple': 4.827149e-02, 'w_ple_gate': 1.896445e-02, 'norm_final_g': 1.601485e+01}


def _to_microbatches(a, axis):
    t = _jnp.moveaxis(a, axis, 0)
    t = t.reshape((N_MICROBATCH, t.shape[0] // N_MICROBATCH) + t.shape[1:])
    return _jnp.moveaxis(t, 1, axis + 1)


def setup_inputs(seed: int = 0) -> dict:
    inp = _fwd_setup_inputs(seed)
    key = _jax.random.fold_in(_jax.random.key(seed), 7919)
    shape, _ = _output_shape()
    out = dict(inp)
    out["loss_target"] = _jax.random.normal(_jax.random.fold_in(key, 0), shape, _jnp.float32)
    for i, name in enumerate(TWIN_WEIGHTS):
        w = inp[name].astype(_jnp.float32)
        if MOMENT_SCALE is None:
            s = _jnp.sqrt(_jnp.mean(_jnp.square(w)) + 1e-30)
        else:
            s = MOMENT_SCALE[name]
        km, kv = _jax.random.split(_jax.random.fold_in(key, i + 1))
        out[name] = w
        out["m_" + name] = s * _jax.random.normal(km, w.shape, _jnp.float32)
        out["v_" + name] = (s * s) * _jax.random.uniform(kv, w.shape, _jnp.float32, 0.5, 1.5)
    if N_MICROBATCH > 1:
        for name, axis in PER_EXAMPLE_BATCH_AXIS.items():
            out[name] = _to_microbatches(out[name], axis)
    return {'x': out['x'], 'p': out['p'], 'norm_mix_g': out['norm_mix_g'], 'w_in': out['w_in'], 'b_f': out['b_f'], 'gmlp_ln_g': out['gmlp_ln_g'], 'gmlp_ln_b': out['gmlp_ln_b'], 'gmlp_w_s': out['gmlp_w_s'], 'gmlp_b_s': out['gmlp_b_s'], 'w_branch_a': out['w_branch_a'], 'w_branch_b': out['w_branch_b'], 'w_out': out['w_out'], 'norm_ffn_g': out['norm_ffn_g'], 'w_up': out['w_up'], 'conv_w': out['conv_w'], 'conv_b': out['conv_b'], 'w_down': out['w_down'], 'norm_ple_g': out['norm_ple_g'], 'w_ple': out['w_ple'], 'w_ple_gate': out['w_ple_gate'], 'norm_final_g': out['norm_final_g'], 'loss_target': out['loss_target'], 'm_norm_mix_g': out['m_norm_mix_g'], 'm_w_in': out['m_w_in'], 'm_b_f': out['m_b_f'], 'm_gmlp_ln_g': out['m_gmlp_ln_g'], 'm_gmlp_ln_b': out['m_gmlp_ln_b'], 'm_gmlp_w_s': out['m_gmlp_w_s'], 'm_gmlp_b_s': out['m_gmlp_b_s'], 'm_w_branch_a': out['m_w_branch_a'], 'm_w_branch_b': out['m_w_branch_b'], 'm_w_out': out['m_w_out'], 'm_norm_ffn_g': out['m_norm_ffn_g'], 'm_w_up': out['m_w_up'], 'm_conv_w': out['m_conv_w'], 'm_conv_b': out['m_conv_b'], 'm_w_down': out['m_w_down'], 'm_norm_ple_g': out['m_norm_ple_g'], 'm_w_ple': out['m_w_ple'], 'm_w_ple_gate': out['m_w_ple_gate'], 'm_norm_final_g': out['m_norm_final_g'], 'v_norm_mix_g': out['v_norm_mix_g'], 'v_w_in': out['v_w_in'], 'v_b_f': out['v_b_f'], 'v_gmlp_ln_g': out['v_gmlp_ln_g'], 'v_gmlp_ln_b': out['v_gmlp_ln_b'], 'v_gmlp_w_s': out['v_gmlp_w_s'], 'v_gmlp_b_s': out['v_gmlp_b_s'], 'v_w_branch_a': out['v_w_branch_a'], 'v_w_branch_b': out['v_w_branch_b'], 'v_w_out': out['v_w_out'], 'v_norm_ffn_g': out['v_norm_ffn_g'], 'v_w_up': out['v_w_up'], 'v_conv_w': out['v_conv_w'], 'v_conv_b': out['v_conv_b'], 'v_w_down': out['v_w_down'], 'v_norm_ple_g': out['v_norm_ple_g'], 'v_w_ple': out['v_w_ple'], 'v_w_ple_gate': out['v_w_ple_gate'], 'v_norm_final_g': out['v_norm_final_g']}


def _loss(weights, diff, rest, loss_target):
    with _jax.named_scope("forward"):
        args = {**rest, TWIN_DIFF_INPUT: diff, **{k: w.astype(_WEIGHT_DTYPES[k]) for k, w in weights.items()}}
        y = _forward(args)
    with _jax.named_scope("loss_head"):
        err = _jnp.square(y.astype(_jnp.float32) - loss_target)
        return 0.5 * _jnp.sum(_jnp.mean(err, axis=-1)) if err.ndim else 0.5 * err


def _adamw(w, g, m, v):
    m = ADAM_B1 * m + (1.0 - ADAM_B1) * g
    v = ADAM_B2 * v + (1.0 - ADAM_B2) * _jnp.square(g)
    m_hat = m / (1.0 - ADAM_B1 ** ADAM_STEP)
    v_hat = v / (1.0 - ADAM_B2 ** ADAM_STEP)
    delta = -ADAM_LR * (m_hat / (_jnp.sqrt(v_hat) + ADAM_EPS) + ADAM_WD * w)
    return delta, m, v


def reference(x, p, norm_mix_g, w_in, b_f, gmlp_ln_g, gmlp_ln_b, gmlp_w_s, gmlp_b_s, w_branch_a, w_branch_b, w_out, norm_ffn_g, w_up, conv_w, conv_b, w_down, norm_ple_g, w_ple, w_ple_gate, norm_final_g, loss_target, m_norm_mix_g, m_w_in, m_b_f, m_gmlp_ln_g, m_gmlp_ln_b, m_gmlp_w_s, m_gmlp_b_s, m_w_branch_a, m_w_branch_b, m_w_out, m_norm_ffn_g, m_w_up, m_conv_w, m_conv_b, m_w_down, m_norm_ple_g, m_w_ple, m_w_ple_gate, m_norm_final_g, v_norm_mix_g, v_w_in, v_b_f, v_gmlp_ln_g, v_gmlp_ln_b, v_gmlp_w_s, v_gmlp_b_s, v_w_branch_a, v_w_branch_b, v_w_out, v_norm_ffn_g, v_w_up, v_conv_w, v_conv_b, v_w_down, v_norm_ple_g, v_w_ple, v_w_ple_gate, v_norm_final_g):
    given = dict(x=x, p=p, norm_mix_g=norm_mix_g, w_in=w_in, b_f=b_f, gmlp_ln_g=gmlp_ln_g, gmlp_ln_b=gmlp_ln_b, gmlp_w_s=gmlp_w_s, gmlp_b_s=gmlp_b_s, w_branch_a=w_branch_a, w_branch_b=w_branch_b, w_out=w_out, norm_ffn_g=norm_ffn_g, w_up=w_up, conv_w=conv_w, conv_b=conv_b, w_down=w_down, norm_ple_g=norm_ple_g, w_ple=w_ple, w_ple_gate=w_ple_gate, norm_final_g=norm_final_g, loss_target=loss_target, m_norm_mix_g=m_norm_mix_g, m_w_in=m_w_in, m_b_f=m_b_f, m_gmlp_ln_g=m_gmlp_ln_g, m_gmlp_ln_b=m_gmlp_ln_b, m_gmlp_w_s=m_gmlp_w_s, m_gmlp_b_s=m_gmlp_b_s, m_w_branch_a=m_w_branch_a, m_w_branch_b=m_w_branch_b, m_w_out=m_w_out, m_norm_ffn_g=m_norm_ffn_g, m_w_up=m_w_up, m_conv_w=m_conv_w, m_conv_b=m_conv_b, m_w_down=m_w_down, m_norm_ple_g=m_norm_ple_g, m_w_ple=m_w_ple, m_w_ple_gate=m_w_ple_gate, m_norm_final_g=m_norm_final_g, v_norm_mix_g=v_norm_mix_g, v_w_in=v_w_in, v_b_f=v_b_f, v_gmlp_ln_g=v_gmlp_ln_g, v_gmlp_ln_b=v_gmlp_ln_b, v_gmlp_w_s=v_gmlp_w_s, v_gmlp_b_s=v_gmlp_b_s, v_w_branch_a=v_w_branch_a, v_w_branch_b=v_w_branch_b, v_w_out=v_w_out, v_norm_ffn_g=v_norm_ffn_g, v_w_up=v_w_up, v_conv_w=v_conv_w, v_conv_b=v_conv_b, v_w_down=v_w_down, v_norm_ple_g=v_norm_ple_g, v_w_ple=v_w_ple, v_w_ple_gate=v_w_ple_gate, v_norm_final_g=v_norm_final_g)
    weights = {n: given[n] for n in TWIN_WEIGHTS}
    shared = {n: given[n] for n in SHARED_INPUTS}
    per_example = {n: given[n] for n in ['x', 'p']}
    grad_fn = _jax.value_and_grad(_loss, argnums=(0, 1))

    def one_microbatch(ex, loss_target):
        ex = dict(ex)
        diff = ex.pop(TWIN_DIFF_INPUT)
        return grad_fn(weights, diff, {**shared, **ex}, loss_target)

    if N_MICROBATCH == 1:
        loss, (grad_w, grad_x) = one_microbatch(per_example, given["loss_target"])
    else:
        def body(carry, xs):
            loss_sum, grad_sum = carry
            l_k, (gw_k, gx_k) = one_microbatch(xs[0], xs[1])
            with _jax.named_scope("update"):
                return (loss_sum + l_k, _jax.tree.map(_jnp.add, grad_sum, gw_k)), gx_k

        init = (_jnp.zeros((), _jnp.float32), _jax.tree.map(_jnp.zeros_like, weights))
        (loss, grad_w), grad_x = _jax.lax.scan(body, init, (per_example, given["loss_target"]))
    with _jax.named_scope("update"):
        delta_w, new_m, new_v = {}, {}, {}
        for n in TWIN_WEIGHTS:
            delta_w[n], new_m[n], new_v[n] = _adamw(weights[n], grad_w[n], given["m_" + n], given["v_" + n])
    return (loss, grad_x, *[grad_w[n] for n in TWIN_WEIGHTS], *[delta_w[n] for n in TWIN_WEIGHTS],
            *[new_m[n] for n in TWIN_WEIGHTS], *[new_v[n] for n in TWIN_WEIGHTS])
```

```python
import functools
import math

import jax
import jax.numpy as jnp
from jax import lax
from jax.experimental import pallas as pl
from jax.experimental.pallas import tpu as pltpu

F32 = jnp.float32
BF16 = jnp.bfloat16

D_MODEL = 1024
EPS = 1e-6
CHUNK = 64
GMLP_GROUPS = 8
GMLP_BLOCK = 128
GMLP_WIDTH = 1024
FOX_HEADS = 16
FOX_HEAD_DIM = 64
FOX_WIDTH = 1024
HEAD_PAIRS = FOX_HEADS // 2
ATT_BLOCK = 128
D_FF = 2816
PLE_DIM = 256
LANES = 128
N_CHIPS = 4

ADAM_LR = 0.001
ADAM_B1 = 0.9
ADAM_B2 = 0.999
ADAM_EPS = 1e-08
ADAM_WD = 0.01
ADAM_STEP = 10

VMEM_LIMIT = 56 * 1024 * 1024
MESH = pl.DeviceIdType.MESH

_NN = (((1,), (0,)), ((), ()))
_NT = (((1,), (1,)), ((), ()))
_TN = (((0,), (0,)), ((), ()))


def _params(**kw):
    return pltpu.CompilerParams(vmem_limit_bytes=VMEM_LIMIT, **kw)


def _tile(dim, pref):
    if dim <= pref:
        return dim
    t = (pref // LANES) * LANES
    while t >= LANES:
        if dim % t == 0:
            return t
        t -= LANES
    return dim


def _dot(a, b, dn):
    return lax.dot_general(a.astype(BF16), b.astype(BF16), dn, preferred_element_type=F32)


def _gelu(x):
    c = math.sqrt(2.0 / math.pi)
    t = jnp.tanh(c * (x + 0.044715 * x * x * x))
    return 0.5 * x * (1.0 + t)


def _gelu_and_grad(x):
    c = math.sqrt(2.0 / math.pi)
    x2 = x * x
    t = jnp.tanh(c * (x + 0.044715 * x2 * x))
    g = 0.5 * x * (1.0 + t)
    dg = 0.5 * (1.0 + t) + 0.5 * x * (1.0 - t * t) * c * (1.0 + 3.0 * 0.044715 * x2)
    return g, dg


def _sigmoid(x):
    return 1.0 / (1.0 + jnp.exp(-x))


def _mm(a, b, *, mode, out_dtype, name, add=None, tm=512, tn=512):
    if mode == "nn":
        m, k = a.shape
        k2, n = b.shape
    elif mode == "nt":
        m, k = a.shape
        n, k2 = b.shape
    else:
        k, m = a.shape
        k2, n = b.shape
    assert k == k2, (name, a.shape, b.shape)
    tm = _tile(m, tm)
    tn = _tile(n, tn)
    dn = {"nn": _NN, "nt": _NT, "tn": _TN}[mode]

    def body(a_ref, b_ref, *rest):
        o_ref = rest[-1]
        acc = _dot(a_ref[...], b_ref[...], dn)
        if add is not None:
            acc = acc + rest[0][...].astype(F32)
        o_ref[...] = acc.astype(o_ref.dtype)

    a_spec = pl.BlockSpec((k, tm), lambda i, j: (0, i)) if mode == "tn" else pl.BlockSpec((tm, k), lambda i, j: (i, 0))
    b_spec = pl.BlockSpec((tn, k), lambda i, j: (j, 0)) if mode == "nt" else pl.BlockSpec((k, tn), lambda i, j: (0, j))
    o_spec = pl.BlockSpec((tm, tn), lambda i, j: (i, j))
    in_specs = [a_spec, b_spec]
    args = [a, b]
    if add is not None:
        in_specs.append(o_spec)
        args.append(add)
    return pl.pallas_call(
        body, name=name, grid=(m // tm, n // tn), in_specs=in_specs, out_specs=o_spec,
        out_shape=jax.ShapeDtypeStruct((m, n), out_dtype), compiler_params=_params(),
    )(*args)


def _rms_fwd(x, g, *, name, tm=256):
    s, d = x.shape
    tm = _tile(s, tm)

    def body(x_ref, g_ref, h_ref):
        xv = x_ref[...]
        r = lax.rsqrt(jnp.mean(xv * xv, axis=-1, keepdims=True) + EPS)
        h_ref[...] = (xv * r * g_ref[...]).astype(h_ref.dtype)

    return pl.pallas_call(
        body, name=name, grid=(s // tm,),
        in_specs=[pl.BlockSpec((tm, d), lambda i: (i, 0)), pl.BlockSpec((1, d), lambda i: (0, 0))],
        out_specs=pl.BlockSpec((tm, d), lambda i: (i, 0)),
        out_shape=jax.ShapeDtypeStruct((s, d), BF16), compiler_params=_params(),
    )(x, g)


def _rms_bwd(x, g, dh, dres, *, name, tm=256):
    s, d = x.shape
    tm = _tile(s, tm)

    def body(x_ref, g_ref, dh_ref, dres_ref, dx_ref, dg_ref):
        xv = x_ref[...]
        r = lax.rsqrt(jnp.mean(xv * xv, axis=-1, keepdims=True) + EPS)
        xhat = xv * r
        dhv = dh_ref[...].astype(F32)
        dyg = dhv * g_ref[...]
        dx = r * (dyg - xhat * jnp.mean(dyg * xhat, axis=-1, keepdims=True))
        dx_ref[...] = dres_ref[...] + dx

        @pl.when(pl.program_id(0) == 0)
        def _():
            dg_ref[...] = jnp.zeros_like(dg_ref)

        dg_ref[...] += jnp.sum(dhv * xhat, axis=0, keepdims=True)

    row = pl.BlockSpec((tm, d), lambda i: (i, 0))
    vec = pl.BlockSpec((1, d), lambda i: (0, 0))
    return pl.pallas_call(
        body, name=name, grid=(s // tm,), in_specs=[row, vec, row, row], out_specs=[row, vec],
        out_shape=[jax.ShapeDtypeStruct((s, d), F32), jax.ShapeDtypeStruct((1, d), F32)],
        compiler_params=_params(),
    )(x, g, dh, dres)


def _gmlp_mask():
    t = lax.broadcasted_iota(jnp.int32, (GMLP_BLOCK, GMLP_BLOCK), 0)
    s_ = lax.broadcasted_iota(jnp.int32, (GMLP_BLOCK, GMLP_BLOCK), 1)
    return (s_ // CHUNK) <= (t // CHUNK)


def _gmlp_norm(zv, ln_g, ln_b):
    vv, dvv = _gelu_and_grad(zv)
    mu = jnp.mean(vv, axis=-1, keepdims=True)
    xc = vv - mu
    rstd = lax.rsqrt(jnp.mean(xc * xc, axis=-1, keepdims=True) + EPS)
    vhat = xc * rstd
    return vhat * ln_g + ln_b, vhat, rstd, dvv


def _gmlp_fwd(z_uv, ln_g, ln_b, w_s, b_s_t, *, name):
    s = z_uv.shape[0]
    w = GMLP_WIDTH
    gd = w // GMLP_GROUPS

    def body(z_ref, lg_ref, lb_ref, ws_ref, bs_ref, a_ref):
        u = _gelu(z_ref[:, :w].astype(F32))
        vn, _, _, _ = _gmlp_norm(z_ref[:, w:].astype(F32), lg_ref[...], lb_ref[...])
        mask = _gmlp_mask()
        for g in range(GMLP_GROUPS):
            wm = jnp.where(mask, ws_ref[g], 0.0)
            mixed = _dot(wm, vn[:, g * gd:(g + 1) * gd], _NN) + bs_ref[:, g:g + 1]
            a_ref[:, g * gd:(g + 1) * gd] = (u[:, g * gd:(g + 1) * gd] * mixed).astype(a_ref.dtype)

    full = lambda shape: pl.BlockSpec(shape, lambda i: (0,) * len(shape))
    return pl.pallas_call(
        body, name=name, grid=(s // GMLP_BLOCK,),
        in_specs=[pl.BlockSpec((GMLP_BLOCK, 2 * w), lambda i: (i, 0)), full((1, w)), full((1, w)),
                  full((GMLP_GROUPS, GMLP_BLOCK, GMLP_BLOCK)), full((GMLP_BLOCK, LANES))],
        out_specs=pl.BlockSpec((GMLP_BLOCK, w), lambda i: (i, 0)),
        out_shape=jax.ShapeDtypeStruct((s, w), BF16), compiler_params=_params(),
    )(z_uv, ln_g, ln_b, w_s, b_s_t)


def _gmlp_bwd(z_uv, da, ln_g, ln_b, w_s, b_s_t, *, name):
    s = z_uv.shape[0]
    w = GMLP_WIDTH
    gd = w // GMLP_GROUPS

    def body(z_ref, da_ref, lg_ref, lb_ref, ws_ref, bs_ref, dz_ref, dws_ref, dbs_ref, dlg_ref, dlb_ref):
        @pl.when(pl.program_id(0) == 0)
        def _():
            dws_ref[...] = jnp.zeros_like(dws_ref)
            dbs_ref[...] = jnp.zeros_like(dbs_ref)
            dlg_ref[...] = jnp.zeros_like(dlg_ref)
            dlb_ref[...] = jnp.zeros_like(dlb_ref)

        u, du_dz = _gelu_and_grad(z_ref[:, :w].astype(F32))
        lg = lg_ref[...]
        vn, vhat, rstd, dvv_dz = _gmlp_norm(z_ref[:, w:].astype(F32), lg, lb_ref[...])
        dav = da_ref[...].astype(F32)
        mask = _gmlp_mask()
        lane = lax.broadcasted_iota(jnp.int32, (GMLP_BLOCK, LANES), 1)
        dvn_parts = []
        dbs = jnp.zeros((GMLP_BLOCK, LANES), F32)
        for g in range(GMLP_GROUPS):
            sl = slice(g * gd, (g + 1) * gd)
            wm = jnp.where(mask, ws_ref[g], 0.0)
            vn_g = vn[:, sl]
            mixed = _dot(wm, vn_g, _NN) + bs_ref[:, g:g + 1]
            dmixed = dav[:, sl] * u[:, sl]
            dz_ref[:, sl] = (dav[:, sl] * mixed * du_dz[:, sl]).astype(dz_ref.dtype)
            dvn_parts.append(_dot(wm, dmixed, _TN))
            dws_ref[g] += jnp.where(mask, _dot(dmixed, vn_g, _NT), 0.0)
            dbs = dbs + jnp.where(lane == g, jnp.sum(dmixed, axis=-1, keepdims=True), 0.0)
        dbs_ref[...] += dbs
        dvn = jnp.concatenate(dvn_parts, axis=-1)
        dlg_ref[...] += jnp.sum(dvn * vhat, axis=0, keepdims=True)
        dlb_ref[...] += jnp.sum(dvn, axis=0, keepdims=True)
        dyg = dvn * lg
        dvv = rstd * (dyg - jnp.mean(dyg, axis=-1, keepdims=True)
                      - vhat * jnp.mean(dyg * vhat, axis=-1, keepdims=True))
        dz_ref[:, w:] = (dvv * dvv_dz).astype(dz_ref.dtype)

    full = lambda shape: pl.BlockSpec(shape, lambda i: (0,) * len(shape))
    return pl.pallas_call(
        body, name=name, grid=(s // GMLP_BLOCK,),
        in_specs=[pl.BlockSpec((GMLP_BLOCK, 2 * w), lambda i: (i, 0)),
                  pl.BlockSpec((GMLP_BLOCK, w), lambda i: (i, 0)), full((1, w)), full((1, w)),
                  full((GMLP_GROUPS, GMLP_BLOCK, GMLP_BLOCK)), full((GMLP_BLOCK, LANES))],
        out_specs=[pl.BlockSpec((GMLP_BLOCK, 2 * w), lambda i: (i, 0)),
                   full((GMLP_GROUPS, GMLP_BLOCK, GMLP_BLOCK)), full((GMLP_BLOCK, LANES)),
                   full((1, w)), full((1, w))],
        out_shape=[jax.ShapeDtypeStruct((s, 2 * w), BF16),
                   jax.ShapeDtypeStruct((GMLP_GROUPS, GMLP_BLOCK, GMLP_BLOCK), F32),
                   jax.ShapeDtypeStruct((GMLP_BLOCK, LANES), F32),
                   jax.ShapeDtypeStruct((1, w), F32), jax.ShapeDtypeStruct((1, w), F32)],
        compiler_params=_params(),
    )(z_uv, da, ln_g, ln_b, w_s, b_s_t)


def _tri(lower):
    r = lax.broadcasted_iota(jnp.int32, (ATT_BLOCK, ATT_BLOCK), 0)
    c = lax.broadcasted_iota(jnp.int32, (ATT_BLOCK, ATT_BLOCK), 1)
    return jnp.where((c <= r) if lower else (c >= r), 1.0, 0.0).astype(F32)


def _log_sigmoid(x):
    return jnp.minimum(x, 0.0) - jnp.log(1.0 + jnp.exp(-jnp.abs(x)))


def _fox_cum(f, b_f, *, name):
    s = f.shape[0]
    nb = s // ATT_BLOCK

    def body(f_ref, b_ref, cb_ref, ct_ref, carry):
        @pl.when(pl.program_id(0) == 0)
        def _():
            carry[...] = jnp.zeros_like(carry)

        lf = _log_sigmoid(f_ref[...] + b_ref[...])
        cum = lax.dot_general(_tri(True), lf, _NN, precision=lax.Precision.HIGHEST,
                              preferred_element_type=F32) + carry[...]
        carry[...] = cum[ATT_BLOCK - 1:ATT_BLOCK, :]
        for h in range(FOX_HEADS):
            cb_ref[h] = jnp.broadcast_to(cum[:, h:h + 1], (ATT_BLOCK, LANES))
        ct_ref[...] = cum.T

    return pl.pallas_call(
        body, name=name, grid=(nb,),
        in_specs=[pl.BlockSpec((ATT_BLOCK, LANES), lambda i: (i, 0)), pl.BlockSpec((1, LANES), lambda i: (0, 0))],
        out_specs=[pl.BlockSpec((FOX_HEADS, ATT_BLOCK, LANES), lambda i: (0, i, 0)),
                   pl.BlockSpec((LANES, ATT_BLOCK), lambda i: (0, i))],
        out_shape=[jax.ShapeDtypeStruct((FOX_HEADS, s, LANES), F32), jax.ShapeDtypeStruct((LANES, s), F32)],
        scratch_shapes=[pltpu.VMEM((1, LANES), F32)], compiler_params=_params(),
    )(f, b_f)


def _fox_dlogit(dcum_t, f, b_f, *, name):
    s = f.shape[0]
    nb = s // ATT_BLOCK

    def body(dc_ref, f_ref, b_ref, df_ref, db_ref, carry):
        @pl.when(pl.program_id(0) == 0)
        def _():
            carry[...] = jnp.zeros_like(carry)
            db_ref[...] = jnp.zeros_like(db_ref)

        d = dc_ref[...].T
        dlog = lax.dot_general(_tri(False), d, _NN, precision=lax.Precision.HIGHEST,
                               preferred_element_type=F32) + carry[...]
        carry[...] = dlog[0:1, :]
        df = dlog * (1.0 - _sigmoid(f_ref[...] + b_ref[...]))
        df_ref[...] = df
        db_ref[...] += jnp.sum(df, axis=0, keepdims=True)

    rev = lambda i: nb - 1 - i
    return pl.pallas_call(
        body, name=name, grid=(nb,),
        in_specs=[pl.BlockSpec((LANES, ATT_BLOCK), lambda i: (0, rev(i))),
                  pl.BlockSpec((ATT_BLOCK, LANES), lambda i: (rev(i), 0)),
                  pl.BlockSpec((1, LANES), lambda i: (0, 0))],
        out_specs=[pl.BlockSpec((ATT_BLOCK, LANES), lambda i: (rev(i), 0)),
                   pl.BlockSpec((1, LANES), lambda i: (0, 0))],
        out_shape=[jax.ShapeDtypeStruct((s, LANES), F32), jax.ShapeDtypeStruct((1, LANES), F32)],
        scratch_shapes=[pltpu.VMEM((1, LANES), F32)], compiler_params=_params(),
    )(dcum_t, f, b_f)


def _causal(qi, ki):
    r = lax.broadcasted_iota(jnp.int32, (ATT_BLOCK, ATT_BLOCK), 0) + qi * ATT_BLOCK
    c = lax.broadcasted_iota(jnp.int32, (ATT_BLOCK, ATT_BLOCK), 1) + ki * ATT_BLOCK
    return c <= r


def _head_mask():
    return lax.broadcasted_iota(jnp.int32, (1, LANES), 1) < FOX_HEAD_DIM


def _attn_fwd(qkv, cum_b, cum_r, *, name):
    s = qkv.shape[0]
    nq = s // ATT_BLOCK
    scale = FOX_HEAD_DIM ** -0.5
    npair = HEAD_PAIRS

    def body(q_ref, k_ref, v_ref, cq_ref, ck_ref, o_ref, l_ref):
        qi = pl.program_id(1)
        m0 = _head_mask()
        q2 = q_ref[...]
        zero = jnp.zeros_like(q2)
        qs = (jnp.where(m0, q2, zero), jnp.where(m0, zero, q2))
        cqs = (cq_ref[0], cq_ref[1])

        def step(ki, carry, masked):
            off = pl.multiple_of(ki * ATT_BLOCK, ATT_BLOCK)
            k2 = k_ref[pl.ds(off, ATT_BLOCK), :]
            v2 = v_ref[pl.ds(off, ATT_BLOCK), :]
            out = []
            for hh in range(2):
                m, l, acc = carry[hh]
                sc = _dot(qs[hh], k2, _NT) * scale + (cqs[hh] - ck_ref[hh:hh + 1, pl.ds(off, ATT_BLOCK)])
                if masked:
                    sc = jnp.where(_causal(qi, ki), sc, -1e30)
                m_new = jnp.maximum(m, jnp.max(sc, axis=-1, keepdims=True))
                alpha = jnp.exp(m - m_new)
                p = jnp.exp(sc - m_new)
                l = alpha * l + jnp.sum(p, axis=-1, keepdims=True)
                acc = alpha * acc + _dot(p, v2, _NN)
                out.append((m_new, l, acc))
            return tuple(out)

        init = tuple((jnp.full((ATT_BLOCK, 1), -1e30, F32), jnp.zeros((ATT_BLOCK, 1), F32),
                      jnp.zeros((ATT_BLOCK, LANES), F32)) for _ in range(2))
        carry = lax.fori_loop(0, qi, lambda ki, c: step(ki, c, False), init)
        (ma, la, acca), (mb, lb, accb) = step(qi, carry, True)
        o_ref[...] = jnp.where(m0, acca / la, accb / lb).astype(o_ref.dtype)
        l_ref[0] = jnp.broadcast_to(ma + jnp.log(la), (ATT_BLOCK, LANES))
        l_ref[1] = jnp.broadcast_to(mb + jnp.log(lb), (ATT_BLOCK, LANES))

    stat = pl.BlockSpec((None, 2, ATT_BLOCK, LANES), lambda j, i: (j, 0, i, 0))
    row = pl.BlockSpec((None, 2, s), lambda j, i: (j, 0, 0))
    return pl.pallas_call(
        body, name=name, grid=(npair, nq),
        in_specs=[pl.BlockSpec((ATT_BLOCK, LANES), lambda j, i: (i, j)),
                  pl.BlockSpec((s, LANES), lambda j, i: (0, npair + j)),
                  pl.BlockSpec((s, LANES), lambda j, i: (0, 2 * npair + j)),
                  stat, row],
        out_specs=[pl.BlockSpec((ATT_BLOCK, LANES), lambda j, i: (i, j)), stat],
        out_shape=[jax.ShapeDtypeStruct((s, FOX_WIDTH), BF16),
                   jax.ShapeDtypeStruct((npair, 2, s, LANES), F32)],
        compiler_params=_params(),
    )(qkv, qkv, qkv, cum_b, cum_r)


def _attn_delta(qkv, do, lse_b, cum_b, cum_r, *, name):
    s = qkv.shape[0]
    nq = s // ATT_BLOCK
    scale = FOX_HEAD_DIM ** -0.5
    npair = HEAD_PAIRS

    def body(q_ref, k_ref, v_ref, do_ref, l_ref, cq_ref, ck_ref, d_ref):
        qi = pl.program_id(1)
        m0 = _head_mask()
        q2 = q_ref[...]
        do2 = do_ref[...]
        qs = (jnp.where(m0, q2, jnp.zeros_like(q2)), jnp.where(m0, jnp.zeros_like(q2), q2))
        dos = (jnp.where(m0, do2, jnp.zeros_like(do2)), jnp.where(m0, jnp.zeros_like(do2), do2))

        def step(ki, carry, masked):
            off = pl.multiple_of(ki * ATT_BLOCK, ATT_BLOCK)
            k2 = k_ref[pl.ds(off, ATT_BLOCK), :]
            v2 = v_ref[pl.ds(off, ATT_BLOCK), :]
            out = []
            for hh in range(2):
                sc = _dot(qs[hh], k2, _NT) * scale + (cq_ref[hh] - ck_ref[hh:hh + 1, pl.ds(off, ATT_BLOCK)])
                p = jnp.exp(sc - l_ref[hh])
                if masked:
                    p = jnp.where(_causal(qi, ki), p, 0.0)
                out.append(carry[hh] + jnp.sum(p * _dot(dos[hh], v2, _NT), axis=-1, keepdims=True))
            return tuple(out)

        init = (jnp.zeros((ATT_BLOCK, 1), F32), jnp.zeros((ATT_BLOCK, 1), F32))
        carry = lax.fori_loop(0, qi, lambda ki, c: step(ki, c, False), init)
        da, db = step(qi, carry, True)
        d_ref[0] = jnp.broadcast_to(da, (ATT_BLOCK, LANES))
        d_ref[1] = jnp.broadcast_to(db, (ATT_BLOCK, LANES))

    stat = pl.BlockSpec((None, 2, ATT_BLOCK, LANES), lambda j, i: (j, 0, i, 0))
    return pl.pallas_call(
        body, name=name, grid=(npair, nq),
        in_specs=[pl.BlockSpec((ATT_BLOCK, LANES), lambda j, i: (i, j)),
                  pl.BlockSpec((s, LANES), lambda j, i: (0, npair + j)),
                  pl.BlockSpec((s, LANES), lambda j, i: (0, 2 * npair + j)),
                  pl.BlockSpec((ATT_BLOCK, LANES), lambda j, i: (i, j)),
                  stat, stat, pl.BlockSpec((None, 2, s), lambda j, i: (j, 0, 0))],
        out_specs=stat,
        out_shape=jax.ShapeDtypeStruct((npair, 2, s, LANES), F32), compiler_params=_params(),
    )(qkv, qkv, qkv, do, lse_b, cum_b, cum_r)


def _attn_bwd(qkv, do, lse_b, delta_b, cum_b, cum_r, *, name):
    s = qkv.shape[0]
    nq = s // ATT_BLOCK
    scale = FOX_HEAD_DIM ** -0.5
    npair = HEAD_PAIRS

    def body(q_ref, k_ref, v_ref, do_ref, l_ref, dl_ref, cq_ref, ck_ref, dq_ref, dk_ref, dv_ref, dc_ref):
        ki = pl.program_id(1)
        m0 = _head_mask()
        k2 = k_ref[...]
        v2 = v_ref[...]
        koff = pl.multiple_of(ki * ATT_BLOCK, ATT_BLOCK)

        @pl.when(ki == 0)
        def _():
            dq_ref[...] = jnp.zeros_like(dq_ref)

        def step(qi, carry, masked):
            off = pl.multiple_of(qi * ATT_BLOCK, ATT_BLOCK)
            q2 = q_ref[pl.ds(off, ATT_BLOCK), :]
            do2 = do_ref[pl.ds(off, ATT_BLOCK), :]
            qzero = jnp.zeros_like(q2)
            dzero = jnp.zeros_like(do2)
            out = []
            dqs = []
            for hh in range(2):
                dk_acc, dv_acc, dc_acc = carry[hh]
                keep = m0 if hh == 0 else jnp.logical_not(m0)
                qh = jnp.where(keep, q2, qzero)
                doh = jnp.where(keep, do2, dzero)
                sc = _dot(qh, k2, _NT) * scale + (cq_ref[hh, pl.ds(off, ATT_BLOCK), :]
                                                 - ck_ref[hh:hh + 1, pl.ds(koff, ATT_BLOCK)])
                p = jnp.exp(sc - l_ref[hh, pl.ds(off, ATT_BLOCK), :])
                if masked:
                    p = jnp.where(_causal(qi, ki), p, 0.0)
                dp = _dot(doh, v2, _NT)
                ds = p * (dp - dl_ref[hh, pl.ds(off, ATT_BLOCK), :])
                dv_acc = dv_acc + _dot(p, do2, _TN)
                dk_acc = dk_acc + _dot(ds, q2, _TN)
                dc_acc = dc_acc - jnp.sum(ds, axis=0, keepdims=True)
                dqs.append(_dot(ds, k2, _NN))
                out.append((dk_acc, dv_acc, dc_acc))
            dq_ref[pl.ds(off, ATT_BLOCK), :] += jnp.where(m0, dqs[0], dqs[1]) * scale
            return tuple(out)

        init = tuple((jnp.zeros((ATT_BLOCK, LANES), F32), jnp.zeros((ATT_BLOCK, LANES), F32),
                      jnp.zeros((1, ATT_BLOCK), F32)) for _ in range(2))
        carry = step(ki, init, True)
        (dka, dva, dca), (dkb, dvb, dcb) = lax.fori_loop(ki + 1, nq, lambda qi, c: step(qi, c, False), carry)
        dk_ref[...] = (jnp.where(m0, dka, dkb) * scale).astype(dk_ref.dtype)
        dv_ref[...] = jnp.where(m0, dva, dvb).astype(dv_ref.dtype)
        dc_ref[0:1, :] = dca
        dc_ref[1:2, :] = dcb

    stat = pl.BlockSpec((None, 2, s, LANES), lambda j, i: (j, 0, 0, 0))
    colfull = lambda base: pl.BlockSpec((s, LANES), lambda j, i: (0, base + j))
    colblk = lambda base: pl.BlockSpec((ATT_BLOCK, LANES), lambda j, i: (i, base + j))
    return pl.pallas_call(
        body, name=name, grid=(npair, nq),
        in_specs=[colfull(0), colblk(npair), colblk(2 * npair), colfull(0), stat, stat, stat,
                  pl.BlockSpec((None, 2, s), lambda j, i: (j, 0, 0))],
        out_specs=[colfull(0), colblk(0), colblk(0), pl.BlockSpec((None, 2, ATT_BLOCK), lambda j, i: (j, 0, i))],
        out_shape=[jax.ShapeDtypeStruct((s, FOX_WIDTH), F32), jax.ShapeDtypeStruct((s, FOX_WIDTH), BF16),
                   jax.ShapeDtypeStruct((s, FOX_WIDTH), BF16), jax.ShapeDtypeStruct((npair, 2, s), F32)],
        compiler_params=_params(),
    )(qkv, qkv, qkv, do, lse_b, delta_b, cum_b, cum_r)


def _merge_fwd(zg, ya, yb, *, name, tm=256):
    s, d = ya.shape
    tm = _tile(s, tm)

    def body(zg_ref, ya_ref, yb_ref, m_ref):
        ga = _sigmoid(zg_ref[:, :d].astype(F32))
        gb = _sigmoid(zg_ref[:, d:].astype(F32))
        m_ref[...] = (ga * ya_ref[...].astype(F32) + gb * yb_ref[...].astype(F32)).astype(m_ref.dtype)

    row = pl.BlockSpec((tm, d), lambda i: (i, 0))
    row2 = pl.BlockSpec((tm, 2 * d), lambda i: (i, 0))
    return pl.pallas_call(
        body, name=name, grid=(s // tm,), in_specs=[row2, row, row], out_specs=row,
        out_shape=jax.ShapeDtypeStruct((s, d), BF16), compiler_params=_params(),
    )(zg, ya, yb)


def _merge_bwd(dm, zg, ya, yb, *, name, tm=256):
    s, d = ya.shape
    tm = _tile(s, tm)

    def body(dm_ref, zg_ref, ya_ref, yb_ref, dzg_ref, dya_ref, dyb_ref):
        dmv = dm_ref[...].astype(F32)
        ga = _sigmoid(zg_ref[:, :d].astype(F32))
        gb = _sigmoid(zg_ref[:, d:].astype(F32))
        dzg_ref[:, :d] = (dmv * ya_ref[...].astype(F32) * ga * (1.0 - ga)).astype(dzg_ref.dtype)
        dzg_ref[:, d:] = (dmv * yb_ref[...].astype(F32) * gb * (1.0 - gb)).astype(dzg_ref.dtype)
        dya_ref[...] = (dmv * ga).astype(dya_ref.dtype)
        dyb_ref[...] = (dmv * gb).astype(dyb_ref.dtype)

    row = pl.BlockSpec((tm, d), lambda i: (i, 0))
    row2 = pl.BlockSpec((tm, 2 * d), lambda i: (i, 0))
    return pl.pallas_call(
        body, name=name, grid=(s // tm,), in_specs=[row, row2, row, row], out_specs=[row2, row, row],
        out_shape=[jax.ShapeDtypeStruct((s, 2 * d), BF16), jax.ShapeDtypeStruct((s, d), BF16),
                   jax.ShapeDtypeStruct((s, d), BF16)],
        compiler_params=_params(),
    )(dm, zg, ya, yb)


def _shift_down(u, k, row):
    return jnp.where(row >= k, pltpu.roll(u, k, 0), 0.0)


def _shift_up(u, k, row):
    n = u.shape[0]
    return jnp.where(row < n - k, pltpu.roll(u, n - k, 0), 0.0)


def _conv_act_fwd(up_a, up_b, cw_a, cw_b, cb_a, cb_b, *, name, tc=128):
    s, f = up_a.shape
    tc = _tile(f, tc)

    def body(ua_ref, ub_ref, wa_ref, wb_ref, ba_ref, bb_ref, act_ref):
        row = lax.broadcasted_iota(jnp.int32, (s, tc), 0)

        def conv(u_ref, w_ref, b_ref):
            u = u_ref[...].astype(F32)
            return (b_ref[...] + w_ref[0:1, :] * _shift_down(u, 2, row)
                    + w_ref[1:2, :] * _shift_down(u, 1, row) + w_ref[2:3, :] * u)

        ca = conv(ua_ref, wa_ref, ba_ref)
        cb = conv(ub_ref, wb_ref, bb_ref)
        act_ref[...] = (_gelu(ca) * cb).astype(act_ref.dtype)

    col = pl.BlockSpec((s, tc), lambda j: (0, j))
    w3 = pl.BlockSpec((3, tc), lambda j: (0, j))
    b1 = pl.BlockSpec((1, tc), lambda j: (0, j))
    return pl.pallas_call(
        body, name=name, grid=(f // tc,), in_specs=[col, col, w3, w3, b1, b1], out_specs=col,
        out_shape=jax.ShapeDtypeStruct((s, f), BF16), compiler_params=_params(),
    )(up_a, up_b, cw_a, cw_b, cb_a, cb_b)


def _conv_act_bwd(up_a, up_b, dact, cw_a, cw_b, cb_a, cb_b, *, name, tc=128):
    s, f = up_a.shape
    tc = _tile(f, tc)

    def body(ua_ref, ub_ref, da_ref, wa_ref, wb_ref, ba_ref, bb_ref, dua_ref, dub_ref, dwa_ref, dwb_ref):
        row = lax.broadcasted_iota(jnp.int32, (s, tc), 0)

        def conv(u_ref, w_ref, b_ref):
            u = u_ref[...].astype(F32)
            u1 = _shift_down(u, 1, row)
            u2 = _shift_down(u, 2, row)
            return u, u1, u2, b_ref[...] + w_ref[0:1, :] * u2 + w_ref[1:2, :] * u1 + w_ref[2:3, :] * u

        def back(dc, taps, w_ref, du_ref, dw_ref):
            u, u1, u2 = taps
            dw_ref[0:1, :] = jnp.sum(dc * u2, axis=0, keepdims=True)
            dw_ref[1:2, :] = jnp.sum(dc * u1, axis=0, keepdims=True)
            dw_ref[2:3, :] = jnp.sum(dc * u, axis=0, keepdims=True)
            dw_ref[3:4, :] = jnp.sum(dc, axis=0, keepdims=True)
            du = (w_ref[2:3, :] * dc + w_ref[1:2, :] * _shift_up(dc, 1, row)
                  + w_ref[0:1, :] * _shift_up(dc, 2, row))
            du_ref[...] = du.astype(du_ref.dtype)

        ua, ua1, ua2, ca = conv(ua_ref, wa_ref, ba_ref)
        ub, ub1, ub2, cb = conv(ub_ref, wb_ref, bb_ref)
        g, dg = _gelu_and_grad(ca)
        dact_v = da_ref[...].astype(F32)
        back(dact_v * cb * dg, (ua, ua1, ua2), wa_ref, dua_ref, dwa_ref)
        back(dact_v * g, (ub, ub1, ub2), wb_ref, dub_ref, dwb_ref)

    col = pl.BlockSpec((s, tc), lambda j: (0, j))
    w3 = pl.BlockSpec((3, tc), lambda j: (0, j))
    w4 = pl.BlockSpec((4, tc), lambda j: (0, j))
    b1 = pl.BlockSpec((1, tc), lambda j: (0, j))
    return pl.pallas_call(
        body, name=name, grid=(f // tc,), in_specs=[col, col, col, w3, w3, b1, b1],
        out_specs=[col, col, w4, w4],
        out_shape=[jax.ShapeDtypeStruct((s, f), BF16), jax.ShapeDtypeStruct((s, f), BF16),
                   jax.ShapeDtypeStruct((4, f), F32), jax.ShapeDtypeStruct((4, f), F32)],
        compiler_params=_params(),
    )(up_a, up_b, dact, cw_a, cw_b, cb_a, cb_b)


def _ple_final(x2, ple, zp, target, g_final, *, name, tm=256):
    s, d = x2.shape
    tm = _tile(s, tm)

    def body(x_ref, ple_ref, zp_ref, t_ref, g_ref, dx_ref, dple_ref, dzp_ref, dg_ref, loss_ref):
        @pl.when(pl.program_id(0) == 0)
        def _():
            dg_ref[...] = jnp.zeros_like(dg_ref)
            loss_ref[...] = jnp.zeros_like(loss_ref)

        gp = _sigmoid(zp_ref[...].astype(F32))
        plev = ple_ref[...].astype(F32)
        x3 = x_ref[...] + plev * gp
        r = lax.rsqrt(jnp.mean(x3 * x3, axis=-1, keepdims=True) + EPS)
        xhat = x3 * r
        gv = g_ref[...]
        diff = xhat * gv - t_ref[...]
        loss_ref[...] += 0.5 * jnp.sum(jnp.mean(diff * diff, axis=-1, keepdims=True), axis=0, keepdims=True)
        dy = diff * (1.0 / d)
        dg_ref[...] += jnp.sum(dy * xhat, axis=0, keepdims=True)
        dyg = dy * gv
        dx3 = r * (dyg - xhat * jnp.mean(dyg * xhat, axis=-1, keepdims=True))
        dx_ref[...] = dx3
        dple_ref[...] = (dx3 * gp).astype(dple_ref.dtype)
        dzp_ref[...] = (dx3 * plev * gp * (1.0 - gp)).astype(dzp_ref.dtype)

    row = pl.BlockSpec((tm, d), lambda i: (i, 0))
    vec = pl.BlockSpec((1, d), lambda i: (0, 0))
    return pl.pallas_call(
        body, name=name, grid=(s // tm,), in_specs=[row, row, row, row, vec],
        out_specs=[row, row, row, vec, pl.BlockSpec((1, LANES), lambda i: (0, 0))],
        out_shape=[jax.ShapeDtypeStruct((s, d), F32), jax.ShapeDtypeStruct((s, d), BF16),
                   jax.ShapeDtypeStruct((s, d), BF16), jax.ShapeDtypeStruct((1, d), F32),
                   jax.ShapeDtypeStruct((1, LANES), F32)],
        compiler_params=_params(),
    )(x2, ple, zp, target, g_final)


def _device_step(x, p, target, w):
    s = x.shape[0]
    g = {}

    h = _rms_fwd(x, w["norm_mix_g"], name="rms_mix")
    z_uv = _mm(h, w["w_uv"], mode="nn", out_dtype=BF16, name="proj_uv", tm=1024)
    qkv = _mm(h, w["w_qkv"], mode="nn", out_dtype=BF16, name="proj_qkv", tm=1024)
    zg = _mm(h, w["w_g"], mode="nn", out_dtype=BF16, name="proj_gate", tm=1024)
    f = _mm(h, w["w_f"], mode="nn", out_dtype=F32, name="proj_f", tm=1024)

    a = _gmlp_fwd(z_uv, w["gmlp_ln_g"], w["gmlp_ln_b"], w["gmlp_w_s"], w["gmlp_b_s_t"], name="gmlp_fwd")

    cum_b, cum_t = _fox_cum(f, w["b_f"], name="fox_cum")
    cum_b = cum_b.reshape(HEAD_PAIRS, 2, s, LANES)
    cum_r = cum_t[:FOX_HEADS].reshape(HEAD_PAIRS, 2, s)
    b, lse_b = _attn_fwd(qkv, cum_b, cum_r, name="attn_fwd")

    ya = _mm(a, w["w_branch_a"], mode="nn", out_dtype=BF16, name="branch_a", tm=1024)
    yb = _mm(b, w["w_branch_b"], mode="nn", out_dtype=BF16, name="branch_b", tm=1024)
    merged = _merge_fwd(zg, ya, yb, name="merge_fwd")
    x1 = _mm(merged, w["w_out"], mode="nn", out_dtype=F32, name="proj_out", add=x, tm=1024)

    h2 = _rms_fwd(x1, w["norm_ffn_g"], name="rms_ffn")
    up_a = _mm(h2, w["w_up_a"], mode="nn", out_dtype=BF16, name="up_a", tm=1024)
    up_b = _mm(h2, w["w_up_b"], mode="nn", out_dtype=BF16, name="up_b", tm=1024)
    cw, cb = w["conv_w"], w["conv_b"]
    conv_args = (cw[:, :D_FF], cw[:, D_FF:], cb[:, :D_FF], cb[:, D_FF:])
    act = _conv_act_fwd(up_a, up_b, *conv_args, name="conv_act_fwd")
    x2 = _mm(act, w["w_down"], mode="nn", out_dtype=F32, name="down", add=x1, tm=512)

    h3 = _rms_fwd(x2, w["norm_ple_g"], name="rms_ple")
    ple = _mm(p, w["w_ple"], mode="nn", out_dtype=BF16, name="ple_proj", tm=1024)
    zp = _mm(h3, w["w_ple_gate"], mode="nn", out_dtype=BF16, name="ple_gate", tm=1024)
    dx3, dple, dzp, g["norm_final_g"], loss = _ple_final(x2, ple, zp, target, w["norm_final_g"], name="ple_final")

    g["w_ple"] = _mm(p, dple, mode="tn", out_dtype=BF16, name="dw_ple")
    g["w_ple_gate"] = _mm(h3, dzp, mode="tn", out_dtype=BF16, name="dw_ple_gate")
    dh3 = _mm(dzp, w["w_ple_gate"], mode="nt", out_dtype=BF16, name="dh3")
    dx2, g["norm_ple_g"] = _rms_bwd(x2, w["norm_ple_g"], dh3, dx3, name="rms_ple_bwd")

    g["w_down"] = _mm(act, dx2, mode="tn", out_dtype=BF16, name="dw_down")
    dact = _mm(dx2, w["w_down"], mode="nt", out_dtype=BF16, name="dact")
    dup_a, dup_b, dcw_a, dcw_b = _conv_act_bwd(up_a, up_b, dact, *conv_args, name="conv_act_bwd")
    g["conv_w"] = jnp.concatenate([dcw_a[:3], dcw_b[:3]], axis=1)
    g["conv_b"] = jnp.concatenate([dcw_a[3:], dcw_b[3:]], axis=1)
    g["w_up_a"] = _mm(h2, dup_a, mode="tn", out_dtype=BF16, name="dw_up_a")
    g["w_up_b"] = _mm(h2, dup_b, mode="tn", out_dtype=BF16, name="dw_up_b")
    dh2 = _mm(dup_a, w["w_up_a"], mode="nt", out_dtype=F32, name="dh2_a")
    dh2 = _mm(dup_b, w["w_up_b"], mode="nt", out_dtype=BF16, name="dh2_b", add=dh2)
    dx1, g["norm_ffn_g"] = _rms_bwd(x1, w["norm_ffn_g"], dh2, dx2, name="rms_ffn_bwd")

    g["w_out"] = _mm(merged, dx1, mode="tn", out_dtype=BF16, name="dw_out")
    dmerged = _mm(dx1, w["w_out"], mode="nt", out_dtype=BF16, name="dmerged")
    dzg, dya, dyb = _merge_bwd(dmerged, zg, ya, yb, name="merge_bwd")
    g["w_branch_a"] = _mm(a, dya, mode="tn", out_dtype=BF16, name="dw_branch_a")
    g["w_branch_b"] = _mm(b, dyb, mode="tn", out_dtype=BF16, name="dw_branch_b")
    da = _mm(dya, w["w_branch_a"], mode="nt", out_dtype=BF16, name="da")
    db = _mm(dyb, w["w_branch_b"], mode="nt", out_dtype=BF16, name="db")

    dz_uv, g["gmlp_w_s"], dbs_t, g["gmlp_ln_g"], g["gmlp_ln_b"] = _gmlp_bwd(
        z_uv, da, w["gmlp_ln_g"], w["gmlp_ln_b"], w["gmlp_w_s"], w["gmlp_b_s_t"], name="gmlp_bwd")
    g["gmlp_b_s"] = dbs_t[:, :GMLP_GROUPS].T

    delta_b = _attn_delta(qkv, db, lse_b, cum_b, cum_r, name="attn_delta")
    dq, dk, dv, dcum_r = _attn_bwd(qkv, db, lse_b, delta_b, cum_b, cum_r, name="attn_bwd")
    dcum_t = jnp.pad(dcum_r.reshape(FOX_HEADS, s), ((0, LANES - FOX_HEADS), (0, 0)))
    df, g["b_f"] = _fox_dlogit(dcum_t, f, w["b_f"], name="fox_dlogit")
    dqkv = jnp.concatenate([dq.astype(BF16), dk, dv], axis=1)

    g["w_uv"] = _mm(h, dz_uv, mode="tn", out_dtype=BF16, name="dw_uv")
    g["w_qkv"] = _mm(h, dqkv, mode="tn", out_dtype=BF16, name="dw_qkv")
    g["w_f"] = _mm(h, df, mode="tn", out_dtype=BF16, name="dw_f")
    g["w_g"] = _mm(h, dzg, mode="tn", out_dtype=BF16, name="dw_g")
    dh = _mm(dz_uv, w["w_uv"], mode="nt", out_dtype=F32, name="dh_uv")
    dh = _mm(dqkv, w["w_qkv"], mode="nt", out_dtype=F32, name="dh_qkv", add=dh)
    dh = _mm(df, w["w_f"], mode="nt", out_dtype=F32, name="dh_f", add=dh)
    dh = _mm(dzg, w["w_g"], mode="nt", out_dtype=BF16, name="dh_g", add=dh)
    dx0, g["norm_mix_g"] = _rms_bwd(x, w["norm_mix_g"], dh, dx1, name="rms_mix_bwd")
    return loss, dx0, g


def _coords():
    return lax.axis_index("x"), lax.axis_index("y"), lax.axis_index("c")


def _other_chips(x, y):
    return [(1 - x, y), (x, 1 - y), (1 - x, 1 - y)]


def _remote(src, dst, send_sem, recv_sem, dev):
    return pltpu.make_async_remote_copy(src_ref=src, dst_ref=dst, send_sem=send_sem, recv_sem=recv_sem,
                                        device_id=dev, device_id_type=MESH)


_ANY = pl.BlockSpec(memory_space=pl.ANY)


def _gather_weights(halved, whole, *, name):
    nh, n = len(halved), len(halved) + len(whole)
    arrays = list(halved) + list(whole)

    def body(*refs):
        ins, outs = refs[:n], refs[n:2 * n]
        send_sems, recv_sems, local_sems = refs[2 * n:]
        x, y, c = _coords()
        me, sib = 2 * x + y, (x, y, 1 - c)
        chips = _other_chips(x, y)

        def half(i, which):
            h = ins[i].shape[0] // 2
            return pl.ds(pl.multiple_of(which * h, 16), h)

        local, sends = [], []
        for i in range(n):
            cp = pltpu.make_async_copy(ins[i], outs[i].at[me], local_sems.at[i])
            cp.start()
            local.append(cp)
            src, dst = (ins[i].at[half(i, c)], outs[i].at[me, half(i, c)]) if i < nh else (ins[i], outs[i].at[me])
            for k, (cx, cy) in enumerate(chips):
                cp = _remote(src, dst, send_sems.at[i, k], recv_sems.at[i, k], (cx, cy, c))
                cp.start()
                sends.append(cp)
        for i in range(n):
            for k, (cx, cy) in enumerate(chips):
                got = outs[i].at[2 * cx + cy, half(i, c)] if i < nh else outs[i].at[2 * cx + cy]
                _remote(got, got, send_sems.at[i, k], recv_sems.at[i, k], sib).wait_recv()
                if i < nh:
                    cp = _remote(got, got, send_sems.at[i, 3 + k], recv_sems.at[i, 3 + k], sib)
                    cp.start()
                    sends.append(cp)
        for i in range(nh):
            for k, (cx, cy) in enumerate(chips):
                got = outs[i].at[2 * cx + cy, half(i, 1 - c)]
                _remote(got, got, send_sems.at[i, 3 + k], recv_sems.at[i, 3 + k], sib).wait_recv()
        for cp in sends:
            cp.wait_send()
        for cp in local:
            cp.wait()

    return pl.pallas_call(
        body, name=name, in_specs=[_ANY] * n, out_specs=[_ANY] * n,
        out_shape=[jax.ShapeDtypeStruct((N_CHIPS,) + a.shape, a.dtype) for a in arrays],
        scratch_shapes=[pltpu.SemaphoreType.DMA((n, 6)), pltpu.SemaphoreType.DMA((n, 6)),
                        pltpu.SemaphoreType.DMA((n,))],
        compiler_params=_params(),
    )(*arrays)


def _pair_exchange(gs, *, name):
    n = len(gs)

    def body(*refs):
        ins, outs = refs[:n], refs[n:2 * n]
        send_sems, recv_sems = refs[2 * n:]
        x, y, c = _coords()
        copies = []
        for i in range(n):
            for j in range(N_CHIPS):
                cp = _remote(ins[i].at[j, 1 - c], outs[i].at[j], send_sems.at[i, j], recv_sems.at[i, j], (x, y, 1 - c))
                cp.start()
                copies.append(cp)
        for cp in copies:
            cp.wait()

    return pl.pallas_call(
        body, name=name, in_specs=[_ANY] * n, out_specs=[_ANY] * n,
        out_shape=[jax.ShapeDtypeStruct((N_CHIPS,) + a.shape[2:], a.dtype) for a in gs],
        scratch_shapes=[pltpu.SemaphoreType.DMA((n, N_CHIPS)), pltpu.SemaphoreType.DMA((n, N_CHIPS))],
        compiler_params=_params(),
    )(*gs)


def _chip_exchange(ss, *, name):
    n = len(ss)

    def body(*refs):
        ins, outs = refs[:n], refs[n:2 * n]
        send_sems, recv_sems, local_sems = refs[2 * n:]
        x, y, c = _coords()
        me = 2 * x + y
        chips = _other_chips(x, y)
        local, sends = [], []
        for i in range(n):
            cp = pltpu.make_async_copy(ins[i].at[me], outs[i].at[me], local_sems.at[i])
            cp.start()
            local.append(cp)
            for k, (cx, cy) in enumerate(chips):
                cp = _remote(ins[i].at[2 * cx + cy], outs[i].at[me], send_sems.at[i, k], recv_sems.at[i, k], (cx, cy, c))
                cp.start()
                sends.append(cp)
        for i in range(n):
            for k, (cx, cy) in enumerate(chips):
                got = outs[i].at[2 * cx + cy]
                _remote(got, got, send_sems.at[i, k], recv_sems.at[i, k], (cx, cy, c)).wait_recv()
        for cp in sends:
            cp.wait_send()
        for cp in local:
            cp.wait()

    return pl.pallas_call(
        body, name=name, in_specs=[_ANY] * n, out_specs=[_ANY] * n,
        out_shape=[jax.ShapeDtypeStruct(a.shape, a.dtype) for a in ss],
        scratch_shapes=[pltpu.SemaphoreType.DMA((n, 3)), pltpu.SemaphoreType.DMA((n, 3)),
                        pltpu.SemaphoreType.DMA((n,))],
        compiler_params=_params(),
    )(*ss)


def _pair_share(hs, *, name):
    n = len(hs)

    def body(*refs):
        ins, outs = refs[:n], refs[n:2 * n]
        send_sems, recv_sems, local_sems = refs[2 * n:]
        x, y, c = _coords()
        local, copies = [], []
        for i in range(n):
            cp = pltpu.make_async_copy(ins[i], outs[i].at[c], local_sems.at[i])
            cp.start()
            local.append(cp)
            cp = _remote(ins[i], outs[i].at[c], send_sems.at[i], recv_sems.at[i], (x, y, 1 - c))
            cp.start()
            copies.append(cp)
        for cp in copies:
            cp.wait()
        for cp in local:
            cp.wait()

    return pl.pallas_call(
        body, name=name, in_specs=[_ANY] * n, out_specs=[_ANY] * n,
        out_shape=[jax.ShapeDtypeStruct((2,) + a.shape, a.dtype) for a in hs],
        scratch_shapes=[pltpu.SemaphoreType.DMA((n,)), pltpu.SemaphoreType.DMA((n,)),
                        pltpu.SemaphoreType.DMA((n,))],
        compiler_params=_params(),
    )(*hs)


def _all_exchange(vec, *, name):
    def body(v_ref, o_ref, send_sems, recv_sems, local_sem):
        x, y, c = _coords()
        me = 4 * x + 2 * y + c
        local = pltpu.make_async_copy(v_ref, o_ref.at[me], local_sem)
        local.start()
        copies = []
        k = 0
        for dx in (0, 1):
            for dy in (0, 1):
                for dc in (0, 1):
                    if dx or dy or dc:
                        peer = (1 - x if dx else x, 1 - y if dy else y, 1 - c if dc else c)
                        cp = _remote(v_ref, o_ref.at[me], send_sems.at[k], recv_sems.at[k], peer)
                        cp.start()
                        copies.append(cp)
                        k += 1
        for cp in copies:
            cp.wait()
        local.wait()

    return pl.pallas_call(
        body, name=name, in_specs=[_ANY], out_specs=_ANY,
        out_shape=jax.ShapeDtypeStruct((8,) + vec.shape, vec.dtype),
        scratch_shapes=[pltpu.SemaphoreType.DMA((7,)), pltpu.SemaphoreType.DMA((7,)), pltpu.SemaphoreType.DMA(())],
        compiler_params=_params(),
    )(vec)


def _rtile(r, pref, mult):
    t = (min(r, pref) // mult) * mult
    while t >= mult:
        if r % t == 0:
            return t
        t -= mult
    return r


def _pair_add(g, recv, core, *, name):
    _, _, r2, cols = g.shape
    tr = _rtile(r2, 256, 16)

    def body(c_ref, g_ref, r_ref, o_ref):
        o_ref[...] = (g_ref[...].astype(F32) + r_ref[...].astype(F32)).astype(o_ref.dtype)

    blk = pl.BlockSpec((None, tr, cols), lambda j, i, c_ref: (j, i, 0))
    return pl.pallas_call(
        body, name=name,
        grid_spec=pltpu.PrefetchScalarGridSpec(
            num_scalar_prefetch=1, grid=(N_CHIPS, r2 // tr),
            in_specs=[pl.BlockSpec((None, None, tr, cols), lambda j, i, c_ref: (j, c_ref[0], i, 0)), blk],
            out_specs=blk),
        out_shape=jax.ShapeDtypeStruct(recv.shape, recv.dtype), compiler_params=_params(),
    )(core, g, recv)


def _sum_slots(a, out_dtype, *, name):
    n, r, cols = a.shape
    tr = _rtile(r, 256, 16)

    def body(a_ref, o_ref):
        acc = a_ref[0].astype(F32)
        for j in range(1, n):
            acc = acc + a_ref[j].astype(F32)
        o_ref[...] = acc.astype(o_ref.dtype)

    return pl.pallas_call(
        body, name=name, grid=(r // tr,),
        in_specs=[pl.BlockSpec((n, tr, cols), lambda i: (0, i, 0))],
        out_specs=pl.BlockSpec((tr, cols), lambda i: (i, 0)),
        out_shape=jax.ShapeDtypeStruct((r, cols), out_dtype), compiler_params=_params(),
    )(a)


def _adamw(w, g, m, v, *, name, rows=256):
    r, cols = w.shape
    tr = _rtile(r, rows, 8)
    c1 = 1.0 / (1.0 - ADAM_B1 ** ADAM_STEP)
    c2 = 1.0 / (1.0 - ADAM_B2 ** ADAM_STEP)

    def body(w_ref, g_ref, m_ref, v_ref, d_ref, nm_ref, nv_ref):
        gv = g_ref[...]
        nm = ADAM_B1 * m_ref[...] + (1.0 - ADAM_B1) * gv
        nv = ADAM_B2 * v_ref[...] + (1.0 - ADAM_B2) * gv * gv
        nm_ref[...] = nm
        nv_ref[...] = nv
        d_ref[...] = -ADAM_LR * ((nm * c1) / (jnp.sqrt(nv * c2) + ADAM_EPS) + ADAM_WD * w_ref[...])

    blk = pl.BlockSpec((tr, cols), lambda i: (i, 0))
    shape = jax.ShapeDtypeStruct((r, cols), F32)
    return pl.pallas_call(
        body, name=name, grid=(r // tr,), in_specs=[blk] * 4, out_specs=[blk] * 3,
        out_shape=[shape] * 3, compiler_params=_params(),
    )(w, g, m, v)


_BIG = (("w_in", 1), ("w_branch_a", 0), ("w_branch_b", 0), ("w_out", 0), ("w_up", 1), ("w_down", 0),
        ("w_ple", 1), ("w_ple_gate", 0))
_SMALL = ("norm_mix_g", "b_f", "gmlp_ln_g", "gmlp_ln_b", "gmlp_w_s", "gmlp_b_s", "norm_ffn_g", "conv_b",
          "norm_ple_g", "norm_final_g")
_WEIGHTS = ("norm_mix_g", "w_in", "b_f", "gmlp_ln_g", "gmlp_ln_b", "gmlp_w_s", "gmlp_b_s", "w_branch_a",
            "w_branch_b", "w_out", "norm_ffn_g", "w_up", "conv_w", "conv_b", "w_down", "norm_ple_g", "w_ple",
            "w_ple_gate", "norm_final_g")
_PACK_ROWS = 8


def _pack(arrays):
    parts = []
    for a in arrays:
        flat = a.reshape(-1)
        unit = _PACK_ROWS * LANES
        flat = jnp.pad(flat, (0, (-flat.shape[0]) % unit))
        parts.append(flat.reshape(-1, LANES))
    return jnp.concatenate(parts, axis=0)


def _unpack(packed, shapes):
    out, row = [], 0
    for shp in shapes:
        size = math.prod(shp)
        rows = -(-size // (_PACK_ROWS * LANES)) * _PACK_ROWS
        out.append(packed[row:row + rows].reshape(-1)[:size].reshape(shp))
        row += rows
    return out


def _assemble(gathered, axis):
    n, r, cols = gathered.shape
    if axis == 0:
        return gathered.reshape(n * r, cols)
    return gathered.transpose(1, 0, 2).reshape(r, n * cols)


def _to_chunks(full, axis):
    if axis == 0:
        r, cols = full.shape[0] // N_CHIPS, full.shape[1]
        chunks = full.reshape(N_CHIPS, r, cols)
    else:
        r, cols = full.shape[0], full.shape[1] // N_CHIPS
        chunks = full.reshape(r, N_CHIPS, cols).transpose(1, 0, 2)
    return chunks.reshape(N_CHIPS, 2, r // 2, cols)


def kernel(x, p, norm_mix_g, w_in, b_f, gmlp_ln_g, gmlp_ln_b, gmlp_w_s, gmlp_b_s, w_branch_a, w_branch_b, w_out, norm_ffn_g, w_up, conv_w, conv_b, w_down, norm_ple_g, w_ple, w_ple_gate, norm_final_g, loss_target, m_norm_mix_g, m_w_in, m_b_f, m_gmlp_ln_g, m_gmlp_ln_b, m_gmlp_w_s, m_gmlp_b_s, m_w_branch_a, m_w_branch_b, m_w_out, m_norm_ffn_g, m_w_up, m_conv_w, m_conv_b, m_w_down, m_norm_ple_g, m_w_ple, m_w_ple_gate, m_norm_final_g, v_norm_mix_g, v_w_in, v_b_f, v_gmlp_ln_g, v_gmlp_ln_b, v_gmlp_w_s, v_gmlp_b_s, v_w_branch_a, v_w_branch_b, v_w_out, v_norm_ffn_g, v_w_up, v_conv_w, v_conv_b, v_w_down, v_norm_ple_g, v_w_ple, v_w_ple_gate, v_norm_final_g):
    args = dict(locals())
    wt = {n: args[n] for n in _WEIGHTS}
    mom = {n: args["m_" + n] for n in _WEIGHTS}
    var = {n: args["v_" + n] for n in _WEIGHTS}
    chip = 2 * lax.axis_index("x") + lax.axis_index("y")
    core = lax.axis_index("c").astype(jnp.int32).reshape(1)

    shards = [wt[n][0].astype(BF16) for n, _ in _BIG]
    gathered = _gather_weights(shards, [conv_w[0]], name="gather_weights")
    full = {n: _assemble(gathered[i], axis) for i, (n, axis) in enumerate(_BIG)}
    o1 = 2 * GMLP_WIDTH
    o2 = o1 + 3 * FOX_WIDTH
    o3 = o2 + FOX_HEADS
    fpad = ((0, 0), (0, LANES - FOX_HEADS))
    w = {
        "w_uv": full["w_in"][:, :o1], "w_qkv": full["w_in"][:, o1:o2],
        "w_f": jnp.pad(full["w_in"][:, o2:o3], fpad), "w_g": full["w_in"][:, o3:],
        "w_branch_a": full["w_branch_a"], "w_branch_b": full["w_branch_b"], "w_out": full["w_out"],
        "w_up_a": full["w_up"][:, :D_FF], "w_up_b": full["w_up"][:, D_FF:], "w_down": full["w_down"],
        "w_ple": full["w_ple"], "w_ple_gate": full["w_ple_gate"],
        "conv_w": _assemble(gathered[len(_BIG)], 1), "conv_b": conv_b,
        "norm_mix_g": norm_mix_g, "norm_ffn_g": norm_ffn_g, "norm_ple_g": norm_ple_g,
        "norm_final_g": norm_final_g.reshape(1, D_MODEL), "b_f": jnp.pad(b_f, fpad),
        "gmlp_ln_g": gmlp_ln_g, "gmlp_ln_b": gmlp_ln_b, "gmlp_w_s": gmlp_w_s[0],
        "gmlp_b_s_t": jnp.pad(gmlp_b_s[0].T, ((0, 0), (0, LANES - GMLP_GROUPS))),
    }

    loss, grad_x, g = _device_step(x[0], p[0, 0], loss_target[0], w)

    gfull = dict(g)
    gfull["w_in"] = jnp.concatenate([g["w_uv"], g["w_qkv"], g["w_f"][:, :FOX_HEADS], g["w_g"]], axis=1)
    gfull["w_up"] = jnp.concatenate([g["w_up_a"], g["w_up_b"]], axis=1)
    chunks = [_to_chunks(gfull[n], axis) for n, axis in _BIG]
    from_sibling = _pair_exchange(chunks, name="grad_pair_exchange")
    pair_sums = [_pair_add(chunks[i], from_sibling[i], core, name="grad_pair_add_" + n) for i, (n, _) in enumerate(_BIG)]
    from_chips = _chip_exchange(pair_sums, name="grad_chip_exchange")
    halves = [_sum_slots(from_chips[i], F32, name="grad_chip_sum_" + n) for i, (n, _) in enumerate(_BIG)]
    shared = _pair_share(halves, name="grad_pair_share")
    grads = {n: shared[i].reshape(wt[n].shape) for i, (n, _) in enumerate(_BIG)}

    small_g = [g[n] if n != "b_f" else g[n][:, :FOX_HEADS] for n in _SMALL]
    vec = _pack(small_g + [g["conv_w"]])
    vec = _sum_slots(_all_exchange(vec, name="small_exchange"), F32, name="small_sum")
    small_rows = _pack([wt[n] for n in _SMALL]).shape[0]
    for n, a in zip(_SMALL, _unpack(vec[:small_rows], [wt[n].shape for n in _SMALL])):
        grads[n] = a
    conv_w_grad = _unpack(vec[small_rows:], [(3, 2 * D_FF)])[0]
    grads["conv_w"] = lax.dynamic_slice_in_dim(conv_w_grad, chip * conv_w.shape[2], conv_w.shape[2], axis=1).reshape(conv_w.shape)

    delta, new_m, new_v = {}, {}, {}
    for n in [n for n, _ in _BIG] + ["conv_w"]:
        shp = wt[n].shape
        outs = _adamw(wt[n].reshape(shp[-2:]), grads[n].reshape(shp[-2:]), mom[n].reshape(shp[-2:]),
                      var[n].reshape(shp[-2:]), name="adamw_" + n)
        delta[n], new_m[n], new_v[n] = (o.reshape(shp) for o in outs)
    outs = _adamw(_pack([wt[n] for n in _SMALL]), vec[:small_rows], _pack([mom[n] for n in _SMALL]),
                  _pack([var[n] for n in _SMALL]), name="adamw_small", rows=2048)
    for d, o in zip((delta, new_m, new_v), outs):
        for n, a in zip(_SMALL, _unpack(o, [wt[n].shape for n in _SMALL])):
            d[n] = a

    total_loss = lax.psum(loss[0, 0], ("x", "y", "c"))
    return (total_loss, grad_x.reshape(x.shape), *[grads[n] for n in _WEIGHTS], *[delta[n] for n in _WEIGHTS],
            *[new_m[n] for n in _WEIGHTS], *[new_v[n] for n in _WEIGHTS])
```

```python
import functools
import math

import jax
import jax.numpy as jnp
from jax import lax
from jax.experimental import pallas as pl
from jax.experimental.pallas import tpu as pltpu

F32 = jnp.float32
BF16 = jnp.bfloat16

D_MODEL = 1024
EPS = 1e-6
CHUNK = 64
GMLP_GROUPS = 8
GMLP_BLOCK = 128
GMLP_WIDTH = 1024
FOX_HEADS = 16
FOX_HEAD_DIM = 64
FOX_WIDTH = 1024
HEAD_PAIRS = FOX_HEADS // 2
ATT_BLOCK = 128
D_FF = 2816
PLE_DIM = 256
LANES = 128
N_CHIPS = 4

ADAM_LR = 0.001
ADAM_B1 = 0.9
ADAM_B2 = 0.999
ADAM_EPS = 1e-08
ADAM_WD = 0.01
ADAM_STEP = 10

VMEM_LIMIT = 56 * 1024 * 1024
MESH = pl.DeviceIdType.MESH

_NN = (((1,), (0,)), ((), ()))
_NT = (((1,), (1,)), ((), ()))
_TN = (((0,), (0,)), ((), ()))


def _params(**kw):
    return pltpu.CompilerParams(vmem_limit_bytes=VMEM_LIMIT, **kw)


def _tile(dim, pref):
    if dim <= pref:
        return dim
    t = (pref // LANES) * LANES
    while t >= LANES:
        if dim % t == 0:
            return t
        t -= LANES
    return dim


def _dot(a, b, dn):
    return lax.dot_general(a.astype(BF16), b.astype(BF16), dn, preferred_element_type=F32)


def _gelu(x):
    c = math.sqrt(2.0 / math.pi)
    t = jnp.tanh(c * (x + 0.044715 * x * x * x))
    return 0.5 * x * (1.0 + t)


def _gelu_and_grad(x):
    c = math.sqrt(2.0 / math.pi)
    x2 = x * x
    t = jnp.tanh(c * (x + 0.044715 * x2 * x))
    g = 0.5 * x * (1.0 + t)
    dg = 0.5 * (1.0 + t) + 0.5 * x * (1.0 - t * t) * c * (1.0 + 3.0 * 0.044715 * x2)
    return g, dg


def _sigmoid(x):
    return 1.0 / (1.0 + jnp.exp(-x))


def _mm(a, b, *, mode, out_dtype, name, add=None, tm=512, tn=512):
    if mode == "nn":
        m, k = a.shape
        k2, n = b.shape
    elif mode == "nt":
        m, k = a.shape
        n, k2 = b.shape
    else:
        k, m = a.shape
        k2, n = b.shape
    assert k == k2, (name, a.shape, b.shape)
    tm = _tile(m, tm)
    tn = _tile(n, tn)
    dn = {"nn": _NN, "nt": _NT, "tn": _TN}[mode]

    def body(a_ref, b_ref, *rest):
        o_ref = rest[-1]
        acc = _dot(a_ref[...], b_ref[...], dn)
        if add is not None:
            acc = acc + rest[0][...].astype(F32)
        o_ref[...] = acc.astype(o_ref.dtype)

    a_spec = pl.BlockSpec((k, tm), lambda i, j: (0, i)) if mode == "tn" else pl.BlockSpec((tm, k), lambda i, j: (i, 0))
    b_spec = pl.BlockSpec((tn, k), lambda i, j: (j, 0)) if mode == "nt" else pl.BlockSpec((k, tn), lambda i, j: (0, j))
    o_spec = pl.BlockSpec((tm, tn), lambda i, j: (i, j))
    in_specs = [a_spec, b_spec]
    args = [a, b]
    if add is not None:
        in_specs.append(o_spec)
        args.append(add)
    return pl.pallas_call(
        body, name=name, grid=(m // tm, n // tn), in_specs=in_specs, out_specs=o_spec,
        out_shape=jax.ShapeDtypeStruct((m, n), out_dtype), compiler_params=_params(),
    )(*args)


def _rms_fwd(x, g, *, name, tm=256):
    s, d = x.shape
    tm = _tile(s, tm)

    def body(x_ref, g_ref, h_ref):
        xv = x_ref[...]
        r = lax.rsqrt(jnp.mean(xv * xv, axis=-1, keepdims=True) + EPS)
        h_ref[...] = (xv * r * g_ref[...]).astype(h_ref.dtype)

    return pl.pallas_call(
        body, name=name, grid=(s // tm,),
        in_specs=[pl.BlockSpec((tm, d), lambda i: (i, 0)), pl.BlockSpec((1, d), lambda i: (0, 0))],
        out_specs=pl.BlockSpec((tm, d), lambda i: (i, 0)),
        out_shape=jax.ShapeDtypeStruct((s, d), BF16), compiler_params=_params(),
    )(x, g)


def _rms_bwd(x, g, dh, dres, *, name, tm=256):
    s, d = x.shape
    tm = _tile(s, tm)

    def body(x_ref, g_ref, dh_ref, dres_ref, dx_ref, dg_ref):
        xv = x_ref[...]
        r = lax.rsqrt(jnp.mean(xv * xv, axis=-1, keepdims=True) + EPS)
        xhat = xv * r
        dhv = dh_ref[...].astype(F32)
        dyg = dhv * g_ref[...]
        dx = r * (dyg - xhat * jnp.mean(dyg * xhat, axis=-1, keepdims=True))
        dx_ref[...] = dres_ref[...] + dx

        @pl.when(pl.program_id(0) == 0)
        def _():
            dg_ref[...] = jnp.zeros_like(dg_ref)

        dg_ref[...] += jnp.sum(dhv * xhat, axis=0, keepdims=True)

    row = pl.BlockSpec((tm, d), lambda i: (i, 0))
    vec = pl.BlockSpec((1, d), lambda i: (0, 0))
    return pl.pallas_call(
        body, name=name, grid=(s // tm,), in_specs=[row, vec, row, row], out_specs=[row, vec],
        out_shape=[jax.ShapeDtypeStruct((s, d), F32), jax.ShapeDtypeStruct((1, d), F32)],
        compiler_params=_params(),
    )(x, g, dh, dres)


def _gmlp_mask():
    t = lax.broadcasted_iota(jnp.int32, (GMLP_BLOCK, GMLP_BLOCK), 0)
    s_ = lax.broadcasted_iota(jnp.int32, (GMLP_BLOCK, GMLP_BLOCK), 1)
    return (s_ // CHUNK) <= (t // CHUNK)


def _gmlp_norm(zv, ln_g, ln_b):
    vv, dvv = _gelu_and_grad(zv)
    mu = jnp.mean(vv, axis=-1, keepdims=True)
    xc = vv - mu
    rstd = lax.rsqrt(jnp.mean(xc * xc, axis=-1, keepdims=True) + EPS)
    vhat = xc * rstd
    return vhat * ln_g + ln_b, vhat, rstd, dvv


def _gmlp_fwd(z_uv, ln_g, ln_b, w_s, b_s_t, *, name):
    s = z_uv.shape[0]
    w = GMLP_WIDTH
    gd = w // GMLP_GROUPS

    def body(z_ref, lg_ref, lb_ref, ws_ref, bs_ref, a_ref):
        u = _gelu(z_ref[:, :w].astype(F32))
        vn, _, _, _ = _gmlp_norm(z_ref[:, w:].astype(F32), lg_ref[...], lb_ref[...])
        mask = _gmlp_mask()
        for g in range(GMLP_GROUPS):
            wm = jnp.where(mask, ws_ref[g], 0.0)
            mixed = _dot(wm, vn[:, g * gd:(g + 1) * gd], _NN) + bs_ref[:, g:g + 1]
            a_ref[:, g * gd:(g + 1) * gd] = (u[:, g * gd:(g + 1) * gd] * mixed).astype(a_ref.dtype)

    full = lambda shape: pl.BlockSpec(shape, lambda i: (0,) * len(shape))
    return pl.pallas_call(
        body, name=name, grid=(s // GMLP_BLOCK,),
        in_specs=[pl.BlockSpec((GMLP_BLOCK, 2 * w), lambda i: (i, 0)), full((1, w)), full((1, w)),
                  full((GMLP_GROUPS, GMLP_BLOCK, GMLP_BLOCK)), full((GMLP_BLOCK, LANES))],
        out_specs=pl.BlockSpec((GMLP_BLOCK, w), lambda i: (i, 0)),
        out_shape=jax.ShapeDtypeStruct((s, w), BF16), compiler_params=_params(),
    )(z_uv, ln_g, ln_b, w_s, b_s_t)


def _gmlp_bwd(z_uv, da, ln_g, ln_b, w_s, b_s_t, *, name):
    s = z_uv.shape[0]
    w = GMLP_WIDTH
    gd = w // GMLP_GROUPS

    def body(z_ref, da_ref, lg_ref, lb_ref, ws_ref, bs_ref, dz_ref, dws_ref, dbs_ref, dlg_ref, dlb_ref):
        @pl.when(pl.program_id(0) == 0)
        def _():
            dws_ref[...] = jnp.zeros_like(dws_ref)
            dbs_ref[...] = jnp.zeros_like(dbs_ref)
            dlg_ref[...] = jnp.zeros_like(dlg_ref)
            dlb_ref[...] = jnp.zeros_like(dlb_ref)

        u, du_dz = _gelu_and_grad(z_ref[:, :w].astype(F32))
        lg = lg_ref[...]
        vn, vhat, rstd, dvv_dz = _gmlp_norm(z_ref[:, w:].astype(F32), lg, lb_ref[...])
        dav = da_ref[...].astype(F32)
        mask = _gmlp_mask()
        lane = lax.broadcasted_iota(jnp.int32, (GMLP_BLOCK, LANES), 1)
        dvn_parts = []
        dbs = jnp.zeros((GMLP_BLOCK, LANES), F32)
        for g in range(GMLP_GROUPS):
            sl = slice(g * gd, (g + 1) * gd)
            wm = jnp.where(mask, ws_ref[g], 0.0)
            vn_g = vn[:, sl]
            mixed = _dot(wm, vn_g, _NN) + bs_ref[:, g:g + 1]
            dmixed = dav[:, sl] * u[:, sl]
            dz_ref[:, sl] = (dav[:, sl] * mixed * du_dz[:, sl]).astype(dz_ref.dtype)
            dvn_parts.append(_dot(wm, dmixed, _TN))
            dws_ref[g] += jnp.where(mask, _dot(dmixed, vn_g, _NT), 0.0)
            dbs = dbs + jnp.where(lane == g, jnp.sum(dmixed, axis=-1, keepdims=True), 0.0)
        dbs_ref[...] += dbs
        dvn = jnp.concatenate(dvn_parts, axis=-1)
        dlg_ref[...] += jnp.sum(dvn * vhat, axis=0, keepdims=True)
        dlb_ref[...] += jnp.sum(dvn, axis=0, keepdims=True)
        dyg = dvn * lg
        dvv = rstd * (dyg - jnp.mean(dyg, axis=-1, keepdims=True)
                      - vhat * jnp.mean(dyg * vhat, axis=-1, keepdims=True))
        dz_ref[:, w:] = (dvv * dvv_dz).astype(dz_ref.dtype)

    full = lambda shape: pl.BlockSpec(shape, lambda i: (0,) * len(shape))
    return pl.pallas_call(
        body, name=name, grid=(s // GMLP_BLOCK,),
        in_specs=[pl.BlockSpec((GMLP_BLOCK, 2 * w), lambda i: (i, 0)),
                  pl.BlockSpec((GMLP_BLOCK, w), lambda i: (i, 0)), full((1, w)), full((1, w)),
                  full((GMLP_GROUPS, GMLP_BLOCK, GMLP_BLOCK)), full((GMLP_BLOCK, LANES))],
        out_specs=[pl.BlockSpec((GMLP_BLOCK, 2 * w), lambda i: (i, 0)),
                   full((GMLP_GROUPS, GMLP_BLOCK, GMLP_BLOCK)), full((GMLP_BLOCK, LANES)),
                   full((1, w)), full((1, w))],
        out_shape=[jax.ShapeDtypeStruct((s, 2 * w), BF16),
                   jax.ShapeDtypeStruct((GMLP_GROUPS, GMLP_BLOCK, GMLP_BLOCK), F32),
                   jax.ShapeDtypeStruct((GMLP_BLOCK, LANES), F32),
                   jax.ShapeDtypeStruct((1, w), F32), jax.ShapeDtypeStruct((1, w), F32)],
        compiler_params=_params(),
    )(z_uv, da, ln_g, ln_b, w_s, b_s_t)


def _tri(lower):
    r = lax.broadcasted_iota(jnp.int32, (ATT_BLOCK, ATT_BLOCK), 0)
    c = lax.broadcasted_iota(jnp.int32, (ATT_BLOCK, ATT_BLOCK), 1)
    return jnp.where((c <= r) if lower else (c >= r), 1.0, 0.0).astype(F32)


def _log_sigmoid(x):
    return jnp.minimum(x, 0.0) - jnp.log(1.0 + jnp.exp(-jnp.abs(x)))


def _fox_cum(f, b_f, *, name):
    s = f.shape[0]
    nb = s // ATT_BLOCK

    def body(f_ref, b_ref, cb_ref, ct_ref, carry):
        @pl.when(pl.program_id(0) == 0)
        def _():
            carry[...] = jnp.zeros_like(carry)

        lf = _log_sigmoid(f_ref[...] + b_ref[...])
        cum = lax.dot_general(_tri(True), lf, _NN, precision=lax.Precision.HIGHEST,
                              preferred_element_type=F32) + carry[...]
        carry[...] = cum[ATT_BLOCK - 1:ATT_BLOCK, :]
        for h in range(FOX_HEADS):
            cb_ref[h] = jnp.broadcast_to(cum[:, h:h + 1], (ATT_BLOCK, LANES))
        ct_ref[...] = cum.T

    return pl.pallas_call(
        body, name=name, grid=(nb,),
        in_specs=[pl.BlockSpec((ATT_BLOCK, LANES), lambda i: (i, 0)), pl.BlockSpec((1, LANES), lambda i: (0, 0))],
        out_specs=[pl.BlockSpec((FOX_HEADS, ATT_BLOCK, LANES), lambda i: (0, i, 0)),
                   pl.BlockSpec((LANES, ATT_BLOCK), lambda i: (0, i))],
        out_shape=[jax.ShapeDtypeStruct((FOX_HEADS, s, LANES), F32), jax.ShapeDtypeStruct((LANES, s), F32)],
        scratch_shapes=[pltpu.VMEM((1, LANES), F32)], compiler_params=_params(),
    )(f, b_f)


def _fox_dlogit(dcum_t, f, b_f, *, name):
    s = f.shape[0]
    nb = s // ATT_BLOCK

    def body(dc_ref, f_ref, b_ref, df_ref, db_ref, carry):
        @pl.when(pl.program_id(0) == 0)
        def _():
            carry[...] = jnp.zeros_like(carry)
            db_ref[...] = jnp.zeros_like(db_ref)

        d = dc_ref[...].T
        dlog = lax.dot_general(_tri(False), d, _NN, precision=lax.Precision.HIGHEST,
                               preferred_element_type=F32) + carry[...]
        carry[...] = dlog[0:1, :]
        df = dlog * (1.0 - _sigmoid(f_ref[...] + b_ref[...]))
        df_ref[...] = df
        db_ref[...] += jnp.sum(df, axis=0, keepdims=True)

    rev = lambda i: nb - 1 - i
    return pl.pallas_call(
        body, name=name, grid=(nb,),
        in_specs=[pl.BlockSpec((LANES, ATT_BLOCK), lambda i: (0, rev(i))),
                  pl.BlockSpec((ATT_BLOCK, LANES), lambda i: (rev(i), 0)),
                  pl.BlockSpec((1, LANES), lambda i: (0, 0))],
        out_specs=[pl.BlockSpec((ATT_BLOCK, LANES), lambda i: (rev(i), 0)),
                   pl.BlockSpec((1, LANES), lambda i: (0, 0))],
        out_shape=[jax.ShapeDtypeStruct((s, LANES), F32), jax.ShapeDtypeStruct((1, LANES), F32)],
        scratch_shapes=[pltpu.VMEM((1, LANES), F32)], compiler_params=_params(),
    )(dcum_t, f, b_f)


def _causal(qi, ki):
    r = lax.broadcasted_iota(jnp.int32, (ATT_BLOCK, ATT_BLOCK), 0) + qi * ATT_BLOCK
    c = lax.broadcasted_iota(jnp.int32, (ATT_BLOCK, ATT_BLOCK), 1) + ki * ATT_BLOCK
    return c <= r


def _head_mask():
    return lax.broadcasted_iota(jnp.int32, (1, LANES), 1) < FOX_HEAD_DIM


def _attn_fwd(qkv, cum_b, cum_r, *, name):
    s = qkv.shape[0]
    nq = s // ATT_BLOCK
    scale = FOX_HEAD_DIM ** -0.5
    npair = HEAD_PAIRS

    def body(q_ref, k_ref, v_ref, cq_ref, ck_ref, o_ref, l_ref):
        qi = pl.program_id(1)
        m0 = _head_mask()
        q2 = q_ref[...]
        zero = jnp.zeros_like(q2)
        qs = (jnp.where(m0, q2, zero), jnp.where(m0, zero, q2))
        cqs = (cq_ref[0], cq_ref[1])

        def step(ki, carry, masked):
            off = pl.multiple_of(ki * ATT_BLOCK, ATT_BLOCK)
            k2 = k_ref[pl.ds(off, ATT_BLOCK), :]
            v2 = v_ref[pl.ds(off, ATT_BLOCK), :]
            out = []
            for hh in range(2):
                m, l, acc = carry[hh]
                sc = _dot(qs[hh], k2, _NT) * scale + (cqs[hh] - ck_ref[hh:hh + 1, pl.ds(off, ATT_BLOCK)])
                if masked:
                    sc = jnp.where(_causal(qi, ki), sc, -1e30)
                m_new = jnp.maximum(m, jnp.max(sc, axis=-1, keepdims=True))
                alpha = jnp.exp(m - m_new)
                p = jnp.exp(sc - m_new)
                l = alpha * l + jnp.sum(p, axis=-1, keepdims=True)
                acc = alpha * acc + _dot(p, v2, _NN)
                out.append((m_new, l, acc))
            return tuple(out)

        init = tuple((jnp.full((ATT_BLOCK, 1), -1e30, F32), jnp.zeros((ATT_BLOCK, 1), F32),
                      jnp.zeros((ATT_BLOCK, LANES), F32)) for _ in range(2))
        carry = lax.fori_loop(0, qi, lambda ki, c: step(ki, c, False), init)
        (ma, la, acca), (mb, lb, accb) = step(qi, carry, True)
        o_ref[...] = jnp.where(m0, acca / la, accb / lb).astype(o_ref.dtype)
        l_ref[0] = jnp.broadcast_to(ma + jnp.log(la), (ATT_BLOCK, LANES))
        l_ref[1] = jnp.broadcast_to(mb + jnp.log(lb), (ATT_BLOCK, LANES))

    stat = pl.BlockSpec((None, 2, ATT_BLOCK, LANES), lambda j, i: (j, 0, i, 0))
    row = pl.BlockSpec((None, 2, s), lambda j, i: (j, 0, 0))
    return pl.pallas_call(
        body, name=name, grid=(npair, nq),
        in_specs=[pl.BlockSpec((ATT_BLOCK, LANES), lambda j, i: (i, j)),
                  pl.BlockSpec((s, LANES), lambda j, i: (0, npair + j)),
                  pl.BlockSpec((s, LANES), lambda j, i: (0, 2 * npair + j)),
                  stat, row],
        out_specs=[pl.BlockSpec((ATT_BLOCK, LANES), lambda j, i: (i, j)), stat],
        out_shape=[jax.ShapeDtypeStruct((s, FOX_WIDTH), BF16),
                   jax.ShapeDtypeStruct((npair, 2, s, LANES), F32)],
        compiler_params=_params(),
    )(qkv, qkv, qkv, cum_b, cum_r)


def _attn_delta(qkv, do, lse_b, cum_b, cum_r, *, name):
    s = qkv.shape[0]
    nq = s // ATT_BLOCK
    scale = FOX_HEAD_DIM ** -0.5
    npair = HEAD_PAIRS

    def body(q_ref, k_ref, v_ref, do_ref, l_ref, cq_ref, ck_ref, d_ref):
        qi = pl.program_id(1)
        m0 = _head_mask()
        q2 = q_ref[...]
        do2 = do_ref[...]
        qs = (jnp.where(m0, q2, jnp.zeros_like(q2)), jnp.where(m0, jnp.zeros_like(q2), q2))
        dos = (jnp.where(m0, do2, jnp.zeros_like(do2)), jnp.where(m0, jnp.zeros_like(do2), do2))

        def step(ki, carry, masked):
            off = pl.multiple_of(ki * ATT_BLOCK, ATT_BLOCK)
            k2 = k_ref[pl.ds(off, ATT_BLOCK), :]
            v2 = v_ref[pl.ds(off, ATT_BLOCK), :]
            out = []
            for hh in range(2):
                sc = _dot(qs[hh], k2, _NT) * scale + (cq_ref[hh] - ck_ref[hh:hh + 1, pl.ds(off, ATT_BLOCK)])
                p = jnp.exp(sc - l_ref[hh])
                if masked:
                    p = jnp.where(_causal(qi, ki), p, 0.0)
                out.append(carry[hh] + jnp.sum(p * _dot(dos[hh], v2, _NT), axis=-1, keepdims=True))
            return tuple(out)

        init = (jnp.zeros((ATT_BLOCK, 1), F32), jnp.zeros((ATT_BLOCK, 1), F32))
        carry = lax.fori_loop(0, qi, lambda ki, c: step(ki, c, False), init)
        da, db = step(qi, carry, True)
        d_ref[0] = jnp.broadcast_to(da, (ATT_BLOCK, LANES))
        d_ref[1] = jnp.broadcast_to(db, (ATT_BLOCK, LANES))

    stat = pl.BlockSpec((None, 2, ATT_BLOCK, LANES), lambda j, i: (j, 0, i, 0))
    return pl.pallas_call(
        body, name=name, grid=(npair, nq),
        in_specs=[pl.BlockSpec((ATT_BLOCK, LANES), lambda j, i: (i, j)),
                  pl.BlockSpec((s, LANES), lambda j, i: (0, npair + j)),
                  pl.BlockSpec((s, LANES), lambda j, i: (0, 2 * npair + j)),
                  pl.BlockSpec((ATT_BLOCK, LANES), lambda j, i: (i, j)),
                  stat, stat, pl.BlockSpec((None, 2, s), lambda j, i: (j, 0, 0))],
        out_specs=stat,
        out_shape=jax.ShapeDtypeStruct((npair, 2, s, LANES), F32), compiler_params=_params(),
    )(qkv, qkv, qkv, do, lse_b, cum_b, cum_r)


def _attn_bwd(qkv, do, lse_b, delta_b, cum_b, cum_r, *, name):
    s = qkv.shape[0]
    nq = s // ATT_BLOCK
    scale = FOX_HEAD_DIM ** -0.5
    npair = HEAD_PAIRS

    def body(q_ref, k_ref, v_ref, do_ref, l_ref, dl_ref, cq_ref, ck_ref, dq_ref, dk_ref, dv_ref, dc_ref):
        ki = pl.program_id(1)
        m0 = _head_mask()
        k2 = k_ref[...]
        v2 = v_ref[...]
        koff = pl.multiple_of(ki * ATT_BLOCK, ATT_BLOCK)

        @pl.when(ki == 0)
        def _():
            dq_ref[...] = jnp.zeros_like(dq_ref)

        def step(qi, carry, masked):
            off = pl.multiple_of(qi * ATT_BLOCK, ATT_BLOCK)
            q2 = q_ref[pl.ds(off, ATT_BLOCK), :]
            do2 = do_ref[pl.ds(off, ATT_BLOCK), :]
            qzero = jnp.zeros_like(q2)
            dzero = jnp.zeros_like(do2)
            out = []
            dqs = []
            for hh in range(2):
                dk_acc, dv_acc, dc_acc = carry[hh]
                keep = m0 if hh == 0 else jnp.logical_not(m0)
                qh = jnp.where(keep, q2, qzero)
                doh = jnp.where(keep, do2, dzero)
                sc = _dot(qh, k2, _NT) * scale + (cq_ref[hh, pl.ds(off, ATT_BLOCK), :]
                                                 - ck_ref[hh:hh + 1, pl.ds(koff, ATT_BLOCK)])
                p = jnp.exp(sc - l_ref[hh, pl.ds(off, ATT_BLOCK), :])
                if masked:
                    p = jnp.where(_causal(qi, ki), p, 0.0)
                dp = _dot(doh, v2, _NT)
                ds = p * (dp - dl_ref[hh, pl.ds(off, ATT_BLOCK), :])
                dv_acc = dv_acc + _dot(p, do2, _TN)
                dk_acc = dk_acc + _dot(ds, q2, _TN)
                dc_acc = dc_acc - jnp.sum(ds, axis=0, keepdims=True)
                dqs.append(_dot(ds, k2, _NN))
                out.append((dk_acc, dv_acc, dc_acc))
            dq_ref[pl.ds(off, ATT_BLOCK), :] += jnp.where(m0, dqs[0], dqs[1]) * scale
            return tuple(out)

        init = tuple((jnp.zeros((ATT_BLOCK, LANES), F32), jnp.zeros((ATT_BLOCK, LANES), F32),
                      jnp.zeros((1, ATT_BLOCK), F32)) for _ in range(2))
        carry = step(ki, init, True)
        (dka, dva, dca), (dkb, dvb, dcb) = lax.fori_loop(ki + 1, nq, lambda qi, c: step(qi, c, False), carry)
        dk_ref[...] = (jnp.where(m0, dka, dkb) * scale).astype(dk_ref.dtype)
        dv_ref[...] = jnp.where(m0, dva, dvb).astype(dv_ref.dtype)
        dc_ref[0:1, :] = dca
        dc_ref[1:2, :] = dcb

    stat = pl.BlockSpec((None, 2, s, LANES), lambda j, i: (j, 0, 0, 0))
    colfull = lambda base: pl.BlockSpec((s, LANES), lambda j, i: (0, base + j))
    colblk = lambda base: pl.BlockSpec((ATT_BLOCK, LANES), lambda j, i: (i, base + j))
    return pl.pallas_call(
        body, name=name, grid=(npair, nq),
        in_specs=[colfull(0), colblk(npair), colblk(2 * npair), colfull(0), stat, stat, stat,
                  pl.BlockSpec((None, 2, s), lambda j, i: (j, 0, 0))],
        out_specs=[colfull(0), colblk(0), colblk(0), pl.BlockSpec((None, 2, ATT_BLOCK), lambda j, i: (j, 0, i))],
        out_shape=[jax.ShapeDtypeStruct((s, FOX_WIDTH), F32), jax.ShapeDtypeStruct((s, FOX_WIDTH), BF16),
                   jax.ShapeDtypeStruct((s, FOX_WIDTH), BF16), jax.ShapeDtypeStruct((npair, 2, s), F32)],
        compiler_params=_params(),
    )(qkv, qkv, qkv, do, lse_b, delta_b, cum_b, cum_r)


ATT_TQ = 256
ATT_TK = 256
ATT_SCALE = FOX_HEAD_DIM ** -0.5
assert ATT_SCALE == 0.125 and ATT_TQ == ATT_TK


def _causal_t(qi, ki):
    kpos = lax.broadcasted_iota(jnp.int32, (ATT_TK, ATT_TQ), 0) + ki * ATT_TK
    qpos = lax.broadcasted_iota(jnp.int32, (ATT_TK, ATT_TQ), 1) + qi * ATT_TQ
    return kpos <= qpos


def _row_mask():
    return lax.broadcasted_iota(jnp.int32, (LANES, 1), 0) < FOX_HEAD_DIM


def _lane_tile(a, width):
    return a if a.shape[1] == width else jnp.tile(a, (1, width // a.shape[1]))


def _attn_fwd_t(qkv, q_t, v_t, cum_b, cum_r, *, name):
    s = qkv.shape[0]
    nq = s // ATT_TQ
    npair = HEAD_PAIRS

    def body(k_ref, qt_ref, vt_ref, cq_ref, ck_ref, o_ref, ot_ref, l_ref):
        qi = pl.program_id(1)
        rows = _row_mask()
        qt = qt_ref[...] * ATT_SCALE
        zero = jnp.zeros_like(qt)
        qts = (jnp.where(rows, qt, zero), jnp.where(rows, zero, qt))

        def step(ki, carry, masked):
            off = pl.multiple_of(ki * ATT_TK, ATT_TK)
            k2 = k_ref[pl.ds(off, ATT_TK), :]
            vt = vt_ref[:, pl.ds(off, ATT_TK)]
            out = []
            for hh in range(2):
                m, l, acc = carry[hh]
                bias = cq_ref[hh:hh + 1, :] - _lane_tile(ck_ref[hh, pl.ds(off, ATT_TK), :], ATT_TQ)
                sc = _dot(k2, qts[hh], _NN) + bias
                if masked:
                    sc = jnp.where(_causal_t(qi, ki), sc, -1e30)
                m_new = jnp.maximum(m, jnp.max(sc, axis=0, keepdims=True))
                alpha = jnp.exp(m - m_new)
                p = jnp.exp(sc - m_new)
                l = alpha * l + jnp.sum(p, axis=0, keepdims=True)
                p_hi = p.astype(BF16)
                p_lo = (p - p_hi.astype(F32)).astype(BF16)
                acc = alpha * acc + (_dot(vt, p_hi, _NN) + _dot(vt, p_lo, _NN))
                out.append((m_new, l, acc))
            return tuple(out)

        init = tuple((jnp.full((1, ATT_TQ), -1e30, F32), jnp.zeros((1, ATT_TQ), F32),
                      jnp.zeros((LANES, ATT_TQ), F32)) for _ in range(2))
        carry = lax.fori_loop(0, qi, lambda ki, c: step(ki, c, False), init)
        (ma, la, acca), (mb, lb, accb) = step(qi, carry, True)
        ot = jnp.where(rows, acca / la, accb / lb)
        ot_ref[...] = ot
        o_ref[...] = ot.T.astype(o_ref.dtype)
        l_ref[0:1, :] = ma + jnp.log(la)
        l_ref[1:2, :] = mb + jnp.log(lb)

    row = pl.BlockSpec((None, 2, ATT_TQ), lambda j, i: (j, 0, i))
    return pl.pallas_call(
        body, name=name, grid=(npair, nq),
        in_specs=[pl.BlockSpec((s, LANES), lambda j, i: (0, npair + j)),
                  pl.BlockSpec((LANES, ATT_TQ), lambda j, i: (j, i)),
                  pl.BlockSpec((LANES, s), lambda j, i: (j, 0)),
                  row, pl.BlockSpec((None, 2, s, LANES), lambda j, i: (j, 0, 0, 0))],
        out_specs=[pl.BlockSpec((ATT_TQ, LANES), lambda j, i: (i, j)),
                   pl.BlockSpec((LANES, ATT_TQ), lambda j, i: (j, i)), row],
        out_shape=[jax.ShapeDtypeStruct((s, FOX_WIDTH), BF16), jax.ShapeDtypeStruct((FOX_WIDTH, s), F32),
                   jax.ShapeDtypeStruct((npair, 2, s), F32)],
        compiler_params=_params(),
    )(qkv, q_t, v_t, cum_r, cum_b)


def _attn_delta_t(do_t, o_t, *, name):
    s = o_t.shape[1]
    ts = _tile(s, 512)

    def body(do_ref, o_ref, d_ref):
        prod = do_ref[...].astype(F32) * o_ref[...]
        d_ref[0:1, :] = jnp.sum(prod[:FOX_HEAD_DIM], axis=0, keepdims=True)
        d_ref[1:2, :] = jnp.sum(prod[FOX_HEAD_DIM:], axis=0, keepdims=True)

    blk = pl.BlockSpec((LANES, ts), lambda j, i: (j, i))
    return pl.pallas_call(
        body, name=name, grid=(HEAD_PAIRS, s // ts), in_specs=[blk, blk],
        out_specs=pl.BlockSpec((None, 2, ts), lambda j, i: (j, 0, i)),
        out_shape=jax.ShapeDtypeStruct((HEAD_PAIRS, 2, s), F32), compiler_params=_params(),
    )(do_t, o_t)


def _attn_bwd_t(qkv, q_t, k_t, do, do_t, lse, delta, cum_b, cum_r, *, name):
    s = qkv.shape[0]
    nq = s // ATT_TQ
    npair = HEAD_PAIRS

    def body(q_ref, k_ref, v_ref, qt_ref, kt_ref, do_ref, dot_ref, l_ref, dl_ref, cq_ref, ck_ref,
             dqt_ref, dk_ref, dv_ref, dc_ref):
        ki = pl.program_id(1)
        m0 = _head_mask()
        rows = _row_mask()
        k2 = k_ref[...]
        v2 = v_ref[...]
        kt = kt_ref[...]
        ks = k2 * ATT_SCALE
        kz, vz = jnp.zeros_like(k2), jnp.zeros_like(v2)
        khs = (jnp.where(m0, ks, kz), jnp.where(m0, kz, ks))
        vhs = (jnp.where(m0, v2, vz), jnp.where(m0, vz, v2))
        cks = tuple(_lane_tile(ck_ref[hh], ATT_TQ) for hh in range(2))

        @pl.when(ki == 0)
        def _():
            dqt_ref[...] = jnp.zeros_like(dqt_ref)

        def step(qi, carry, masked):
            off = pl.multiple_of(qi * ATT_TQ, ATT_TQ)
            q2 = q_ref[pl.ds(off, ATT_TQ), :]
            do2 = do_ref[pl.ds(off, ATT_TQ), :]
            qt = qt_ref[:, pl.ds(off, ATT_TQ)]
            dot_ = dot_ref[:, pl.ds(off, ATT_TQ)]
            out, dqs = [], []
            for hh in range(2):
                dk_acc, dv_acc, dc_acc = carry[hh]
                sc = _dot(khs[hh], qt, _NN) + (cq_ref[hh:hh + 1, pl.ds(off, ATT_TQ)] - cks[hh])
                p = jnp.exp(sc - l_ref[hh:hh + 1, pl.ds(off, ATT_TQ)])
                if masked:
                    p = jnp.where(_causal_t(qi, ki), p, 0.0)
                dp = _dot(vhs[hh], dot_, _NN)
                ds = p * (dp - dl_ref[hh:hh + 1, pl.ds(off, ATT_TQ)])
                dc_acc = dc_acc - jnp.sum(ds, axis=1, keepdims=True)
                dss = (ds * ATT_SCALE).astype(BF16)
                dv_acc = dv_acc + _dot(p, do2, _NN)
                dk_acc = dk_acc + _dot(dss, q2, _NN)
                dqs.append(_dot(kt, dss, _NN))
                out.append((dk_acc, dv_acc, dc_acc))
            dqt_ref[:, pl.ds(off, ATT_TQ)] += jnp.where(rows, dqs[0], dqs[1])
            return tuple(out)

        init = tuple((jnp.zeros((ATT_TK, LANES), F32), jnp.zeros((ATT_TK, LANES), F32),
                      jnp.zeros((ATT_TK, 1), F32)) for _ in range(2))
        carry = step(ki, init, True)
        (dka, dva, dca), (dkb, dvb, dcb) = lax.fori_loop(ki + 1, nq, lambda qi, c: step(qi, c, False), carry)
        dk_ref[...] = jnp.where(m0, dka, dkb).astype(dk_ref.dtype)
        dv_ref[...] = jnp.where(m0, dva, dvb).astype(dv_ref.dtype)
        dc_ref[0] = jnp.broadcast_to(dca, (ATT_TK, LANES))
        dc_ref[1] = jnp.broadcast_to(dcb, (ATT_TK, LANES))

    colfull = lambda base: pl.BlockSpec((s, LANES), lambda j, i: (0, base + j))
    colblk = lambda base: pl.BlockSpec((ATT_TK, LANES), lambda j, i: (i, base + j))
    rowfull = pl.BlockSpec((LANES, s), lambda j, i: (j, 0))
    stat = pl.BlockSpec((None, 2, s), lambda j, i: (j, 0, 0))
    bcast = pl.BlockSpec((None, 2, ATT_TK, LANES), lambda j, i: (j, 0, i, 0))
    return pl.pallas_call(
        body, name=name, grid=(npair, nq),
        in_specs=[colfull(0), colblk(npair), colblk(2 * npair), rowfull,
                  pl.BlockSpec((LANES, ATT_TK), lambda j, i: (j, i)), colfull(0), rowfull,
                  stat, stat, stat, bcast],
        out_specs=[rowfull, colblk(0), colblk(0), bcast],
        out_shape=[jax.ShapeDtypeStruct((FOX_WIDTH, s), F32), jax.ShapeDtypeStruct((s, FOX_WIDTH), BF16),
                   jax.ShapeDtypeStruct((s, FOX_WIDTH), BF16), jax.ShapeDtypeStruct((npair, 2, s, LANES), F32)],
        compiler_params=_params(),
    )(qkv, qkv, qkv, q_t, k_t, do, do_t, lse, delta, cum_r, cum_b)


def _merge_fwd(zg, ya, yb, *, name, tm=256):
    s, d = ya.shape
    tm = _tile(s, tm)

    def body(zg_ref, ya_ref, yb_ref, m_ref):
        ga = _sigmoid(zg_ref[:, :d].astype(F32))
        gb = _sigmoid(zg_ref[:, d:].astype(F32))
        m_ref[...] = (ga * ya_ref[...].astype(F32) + gb * yb_ref[...].astype(F32)).astype(m_ref.dtype)

    row = pl.BlockSpec((tm, d), lambda i: (i, 0))
    row2 = pl.BlockSpec((tm, 2 * d), lambda i: (i, 0))
    return pl.pallas_call(
        body, name=name, grid=(s // tm,), in_specs=[row2, row, row], out_specs=row,
        out_shape=jax.ShapeDtypeStruct((s, d), BF16), compiler_params=_params(),
    )(zg, ya, yb)


def _merge_bwd(dm, zg, ya, yb, *, name, tm=256):
    s, d = ya.shape
    tm = _tile(s, tm)

    def body(dm_ref, zg_ref, ya_ref, yb_ref, dzg_ref, dya_ref, dyb_ref):
        dmv = dm_ref[...].astype(F32)
        ga = _sigmoid(zg_ref[:, :d].astype(F32))
        gb = _sigmoid(zg_ref[:, d:].astype(F32))
        dzg_ref[:, :d] = (dmv * ya_ref[...].astype(F32) * ga * (1.0 - ga)).astype(dzg_ref.dtype)
        dzg_ref[:, d:] = (dmv * yb_ref[...].astype(F32) * gb * (1.0 - gb)).astype(dzg_ref.dtype)
        dya_ref[...] = (dmv * ga).astype(dya_ref.dtype)
        dyb_ref[...] = (dmv * gb).astype(dyb_ref.dtype)

    row = pl.BlockSpec((tm, d), lambda i: (i, 0))
    row2 = pl.BlockSpec((tm, 2 * d), lambda i: (i, 0))
    return pl.pallas_call(
        body, name=name, grid=(s // tm,), in_specs=[row, row2, row, row], out_specs=[row2, row, row],
        out_shape=[jax.ShapeDtypeStruct((s, 2 * d), BF16), jax.ShapeDtypeStruct((s, d), BF16),
                   jax.ShapeDtypeStruct((s, d), BF16)],
        compiler_params=_params(),
    )(dm, zg, ya, yb)


def _shift_down(u, k, row):
    return jnp.where(row >= k, pltpu.roll(u, k, 0), 0.0)


def _shift_up(u, k, row):
    n = u.shape[0]
    return jnp.where(row < n - k, pltpu.roll(u, n - k, 0), 0.0)


def _conv_act_fwd(up_a, up_b, cw_a, cw_b, cb_a, cb_b, *, name, tc=128):
    s, f = up_a.shape
    tc = _tile(f, tc)

    def body(ua_ref, ub_ref, wa_ref, wb_ref, ba_ref, bb_ref, act_ref):
        row = lax.broadcasted_iota(jnp.int32, (s, tc), 0)

        def conv(u_ref, w_ref, b_ref):
            u = u_ref[...].astype(F32)
            return (b_ref[...] + w_ref[0:1, :] * _shift_down(u, 2, row)
                    + w_ref[1:2, :] * _shift_down(u, 1, row) + w_ref[2:3, :] * u)

        ca = conv(ua_ref, wa_ref, ba_ref)
        cb = conv(ub_ref, wb_ref, bb_ref)
        act_ref[...] = (_gelu(ca) * cb).astype(act_ref.dtype)

    col = pl.BlockSpec((s, tc), lambda j: (0, j))
    w3 = pl.BlockSpec((3, tc), lambda j: (0, j))
    b1 = pl.BlockSpec((1, tc), lambda j: (0, j))
    return pl.pallas_call(
        body, name=name, grid=(f // tc,), in_specs=[col, col, w3, w3, b1, b1], out_specs=col,
        out_shape=jax.ShapeDtypeStruct((s, f), BF16), compiler_params=_params(),
    )(up_a, up_b, cw_a, cw_b, cb_a, cb_b)


def _conv_act_bwd(up_a, up_b, dact, cw_a, cw_b, cb_a, cb_b, *, name, tc=128):
    s, f = up_a.shape
    tc = _tile(f, tc)

    def body(ua_ref, ub_ref, da_ref, wa_ref, wb_ref, ba_ref, bb_ref, dua_ref, dub_ref, dwa_ref, dwb_ref):
        row = lax.broadcasted_iota(jnp.int32, (s, tc), 0)

        def conv(u_ref, w_ref, b_ref):
            u = u_ref[...].astype(F32)
            u1 = _shift_down(u, 1, row)
            u2 = _shift_down(u, 2, row)
            return u, u1, u2, b_ref[...] + w_ref[0:1, :] * u2 + w_ref[1:2, :] * u1 + w_ref[2:3, :] * u

        def back(dc, taps, w_ref, du_ref, dw_ref):
            u, u1, u2 = taps
            dw_ref[0:1, :] = jnp.sum(dc * u2, axis=0, keepdims=True)
            dw_ref[1:2, :] = jnp.sum(dc * u1, axis=0, keepdims=True)
            dw_ref[2:3, :] = jnp.sum(dc * u, axis=0, keepdims=True)
            dw_ref[3:4, :] = jnp.sum(dc, axis=0, keepdims=True)
            du = (w_ref[2:3, :] * dc + w_ref[1:2, :] * _shift_up(dc, 1, row)
                  + w_ref[0:1, :] * _shift_up(dc, 2, row))
            du_ref[...] = du.astype(du_ref.dtype)

        ua, ua1, ua2, ca = conv(ua_ref, wa_ref, ba_ref)
        ub, ub1, ub2, cb = conv(ub_ref, wb_ref, bb_ref)
        g, dg = _gelu_and_grad(ca)
        dact_v = da_ref[...].astype(F32)
        back(dact_v * cb * dg, (ua, ua1, ua2), wa_ref, dua_ref, dwa_ref)
        back(dact_v * g, (ub, ub1, ub2), wb_ref, dub_ref, dwb_ref)

    col = pl.BlockSpec((s, tc), lambda j: (0, j))
    w3 = pl.BlockSpec((3, tc), lambda j: (0, j))
    w4 = pl.BlockSpec((4, tc), lambda j: (0, j))
    b1 = pl.BlockSpec((1, tc), lambda j: (0, j))
    return pl.pallas_call(
        body, name=name, grid=(f // tc,), in_specs=[col, col, col, w3, w3, b1, b1],
        out_specs=[col, col, w4, w4],
        out_shape=[jax.ShapeDtypeStruct((s, f), BF16), jax.ShapeDtypeStruct((s, f), BF16),
                   jax.ShapeDtypeStruct((4, f), F32), jax.ShapeDtypeStruct((4, f), F32)],
        compiler_params=_params(),
    )(up_a, up_b, dact, cw_a, cw_b, cb_a, cb_b)


def _ple_final(x2, ple, zp, target, g_final, *, name, tm=256):
    s, d = x2.shape
    tm = _tile(s, tm)

    def body(x_ref, ple_ref, zp_ref, t_ref, g_ref, dx_ref, dple_ref, dzp_ref, dg_ref, loss_ref):
        @pl.when(pl.program_id(0) == 0)
        def _():
            dg_ref[...] = jnp.zeros_like(dg_ref)
            loss_ref[...] = jnp.zeros_like(loss_ref)

        gp = _sigmoid(zp_ref[...].astype(F32))
        plev = ple_ref[...].astype(F32)
        x3 = x_ref[...] + plev * gp
        r = lax.rsqrt(jnp.mean(x3 * x3, axis=-1, keepdims=True) + EPS)
        xhat = x3 * r
        gv = g_ref[...]
        diff = xhat * gv - t_ref[...]
        loss_ref[...] += 0.5 * jnp.sum(jnp.mean(diff * diff, axis=-1, keepdims=True), axis=0, keepdims=True)
        dy = diff * (1.0 / d)
        dg_ref[...] += jnp.sum(dy * xhat, axis=0, keepdims=True)
        dyg = dy * gv
        dx3 = r * (dyg - xhat * jnp.mean(dyg * xhat, axis=-1, keepdims=True))
        dx_ref[...] = dx3
        dple_ref[...] = (dx3 * gp).astype(dple_ref.dtype)
        dzp_ref[...] = (dx3 * plev * gp * (1.0 - gp)).astype(dzp_ref.dtype)

    row = pl.BlockSpec((tm, d), lambda i: (i, 0))
    vec = pl.BlockSpec((1, d), lambda i: (0, 0))
    return pl.pallas_call(
        body, name=name, grid=(s // tm,), in_specs=[row, row, row, row, vec],
        out_specs=[row, row, row, vec, pl.BlockSpec((1, LANES), lambda i: (0, 0))],
        out_shape=[jax.ShapeDtypeStruct((s, d), F32), jax.ShapeDtypeStruct((s, d), BF16),
                   jax.ShapeDtypeStruct((s, d), BF16), jax.ShapeDtypeStruct((1, d), F32),
                   jax.ShapeDtypeStruct((1, LANES), F32)],
        compiler_params=_params(),
    )(x2, ple, zp, target, g_final)


def _device_step(x, p, target, w):
    s = x.shape[0]
    g = {}

    h = _rms_fwd(x, w["norm_mix_g"], name="rms_mix")
    z_uv = _mm(h, w["w_uv"], mode="nn", out_dtype=BF16, name="proj_uv", tm=1024)
    qkv = _mm(h, w["w_qkv"], mode="nn", out_dtype=BF16, name="proj_qkv", tm=1024)
    zg = _mm(h, w["w_g"], mode="nn", out_dtype=BF16, name="proj_gate", tm=1024)
    f = _mm(h, w["w_f"], mode="nn", out_dtype=F32, name="proj_f", tm=1024)

    a = _gmlp_fwd(z_uv, w["gmlp_ln_g"], w["gmlp_ln_b"], w["gmlp_w_s"], w["gmlp_b_s_t"], name="gmlp_fwd")

    cum_b, cum_t = _fox_cum(f, w["b_f"], name="fox_cum")
    cum_b = cum_b.reshape(HEAD_PAIRS, 2, s, LANES)
    cum_r = cum_t[:FOX_HEADS].reshape(HEAD_PAIRS, 2, s)
    q_t, k_t, v_t = (qkv[:, i * FOX_WIDTH:(i + 1) * FOX_WIDTH].T for i in range(3))
    b, o_t, lse = _attn_fwd_t(qkv, q_t, v_t, cum_b, cum_r, name="attn_fwd")

    ya = _mm(a, w["w_branch_a"], mode="nn", out_dtype=BF16, name="branch_a", tm=1024)
    yb = _mm(b, w["w_branch_b"], mode="nn", out_dtype=BF16, name="branch_b", tm=1024)
    merged = _merge_fwd(zg, ya, yb, name="merge_fwd")
    x1 = _mm(merged, w["w_out"], mode="nn", out_dtype=F32, name="proj_out", add=x, tm=1024)

    h2 = _rms_fwd(x1, w["norm_ffn_g"], name="rms_ffn")
    up_a = _mm(h2, w["w_up_a"], mode="nn", out_dtype=BF16, name="up_a", tm=1024)
    up_b = _mm(h2, w["w_up_b"], mode="nn", out_dtype=BF16, name="up_b", tm=1024)
    cw, cb = w["conv_w"], w["conv_b"]
    conv_args = (cw[:, :D_FF], cw[:, D_FF:], cb[:, :D_FF], cb[:, D_FF:])
    act = _conv_act_fwd(up_a, up_b, *conv_args, name="conv_act_fwd")
    x2 = _mm(act, w["w_down"], mode="nn", out_dtype=F32, name="down", add=x1, tm=512)

    h3 = _rms_fwd(x2, w["norm_ple_g"], name="rms_ple")
    ple = _mm(p, w["w_ple"], mode="nn", out_dtype=BF16, name="ple_proj", tm=1024)
    zp = _mm(h3, w["w_ple_gate"], mode="nn", out_dtype=BF16, name="ple_gate", tm=1024)
    dx3, dple, dzp, g["norm_final_g"], loss = _ple_final(x2, ple, zp, target, w["norm_final_g"], name="ple_final")

    g["w_ple"] = _mm(p, dple, mode="tn", out_dtype=BF16, name="dw_ple")
    g["w_ple_gate"] = _mm(h3, dzp, mode="tn", out_dtype=BF16, name="dw_ple_gate")
    dh3 = _mm(dzp, w["w_ple_gate"], mode="nt", out_dtype=BF16, name="dh3")
    dx2, g["norm_ple_g"] = _rms_bwd(x2, w["norm_ple_g"], dh3, dx3, name="rms_ple_bwd")

    g["w_down"] = _mm(act, dx2, mode="tn", out_dtype=BF16, name="dw_down")
    dact = _mm(dx2, w["w_down"], mode="nt", out_dtype=BF16, name="dact")
    dup_a, dup_b, dcw_a, dcw_b = _conv_act_bwd(up_a, up_b, dact, *conv_args, name="conv_act_bwd")
    g["conv_w"] = jnp.concatenate([dcw_a[:3], dcw_b[:3]], axis=1)
    g["conv_b"] = jnp.concatenate([dcw_a[3:], dcw_b[3:]], axis=1)
    g["w_up_a"] = _mm(h2, dup_a, mode="tn", out_dtype=BF16, name="dw_up_a")
    g["w_up_b"] = _mm(h2, dup_b, mode="tn", out_dtype=BF16, name="dw_up_b")
    dh2 = _mm(dup_a, w["w_up_a"], mode="nt", out_dtype=F32, name="dh2_a")
    dh2 = _mm(dup_b, w["w_up_b"], mode="nt", out_dtype=BF16, name="dh2_b", add=dh2)
    dx1, g["norm_ffn_g"] = _rms_bwd(x1, w["norm_ffn_g"], dh2, dx2, name="rms_ffn_bwd")

    g["w_out"] = _mm(merged, dx1, mode="tn", out_dtype=BF16, name="dw_out")
    dmerged = _mm(dx1, w["w_out"], mode="nt", out_dtype=BF16, name="dmerged")
    dzg, dya, dyb = _merge_bwd(dmerged, zg, ya, yb, name="merge_bwd")
    g["w_branch_a"] = _mm(a, dya, mode="tn", out_dtype=BF16, name="dw_branch_a")
    g["w_branch_b"] = _mm(b, dyb, mode="tn", out_dtype=BF16, name="dw_branch_b")
    da = _mm(dya, w["w_branch_a"], mode="nt", out_dtype=BF16, name="da")
    db = _mm(dyb, w["w_branch_b"], mode="nt", out_dtype=BF16, name="db")

    dz_uv, g["gmlp_w_s"], dbs_t, g["gmlp_ln_g"], g["gmlp_ln_b"] = _gmlp_bwd(
        z_uv, da, w["gmlp_ln_g"], w["gmlp_ln_b"], w["gmlp_w_s"], w["gmlp_b_s_t"], name="gmlp_bwd")
    g["gmlp_b_s"] = dbs_t[:, :GMLP_GROUPS].T

    db_t = db.T
    delta = _attn_delta_t(db_t, o_t, name="attn_delta")
    dq_t, dk, dv, dcum_b = _attn_bwd_t(qkv, q_t, k_t, db, db_t, lse, delta, cum_b, cum_r, name="attn_bwd")
    dcum_t = jnp.pad(dcum_b[..., 0].reshape(FOX_HEADS, s), ((0, LANES - FOX_HEADS), (0, 0)))
    df, g["b_f"] = _fox_dlogit(dcum_t, f, w["b_f"], name="fox_dlogit")
    dqkv = jnp.concatenate([dq_t.T.astype(BF16), dk, dv], axis=1)

    g["w_uv"] = _mm(h, dz_uv, mode="tn", out_dtype=BF16, name="dw_uv")
    g["w_qkv"] = _mm(h, dqkv, mode="tn", out_dtype=BF16, name="dw_qkv")
    g["w_f"] = _mm(h, df, mode="tn", out_dtype=BF16, name="dw_f")
    g["w_g"] = _mm(h, dzg, mode="tn", out_dtype=BF16, name="dw_g")
    dh = _mm(dz_uv, w["w_uv"], mode="nt", out_dtype=F32, name="dh_uv")
    dh = _mm(dqkv, w["w_qkv"], mode="nt", out_dtype=F32, name="dh_qkv", add=dh)
    dh = _mm(df, w["w_f"], mode="nt", out_dtype=F32, name="dh_f", add=dh)
    dh = _mm(dzg, w["w_g"], mode="nt", out_dtype=BF16, name="dh_g", add=dh)
    dx0, g["norm_mix_g"] = _rms_bwd(x, w["norm_mix_g"], dh, dx1, name="rms_mix_bwd")
    return loss, dx0, g


def _coords():
    return lax.axis_index("x"), lax.axis_index("y"), lax.axis_index("c")


def _other_chips(x, y):
    return [(1 - x, y), (x, 1 - y), (1 - x, 1 - y)]


def _remote(src, dst, send_sem, recv_sem, dev):
    return pltpu.make_async_remote_copy(src_ref=src, dst_ref=dst, send_sem=send_sem, recv_sem=recv_sem,
                                        device_id=dev, device_id_type=MESH)


_ANY = pl.BlockSpec(memory_space=pl.ANY)


def _gather_weights(halved, whole, *, name):
    nh, n = len(halved), len(halved) + len(whole)
    arrays = list(halved) + list(whole)

    def body(*refs):
        ins, outs = refs[:n], refs[n:2 * n]
        send_sems, recv_sems, local_sems = refs[2 * n:]
        x, y, c = _coords()
        me, sib = 2 * x + y, (x, y, 1 - c)
        chips = _other_chips(x, y)

        def half(i, which):
            h = ins[i].shape[0] // 2
            return pl.ds(pl.multiple_of(which * h, 16), h)

        local, sends = [], []
        for i in range(n):
            cp = pltpu.make_async_copy(ins[i], outs[i].at[me], local_sems.at[i])
            cp.start()
            local.append(cp)
            src, dst = (ins[i].at[half(i, c)], outs[i].at[me, half(i, c)]) if i < nh else (ins[i], outs[i].at[me])
            for k, (cx, cy) in enumerate(chips):
                cp = _remote(src, dst, send_sems.at[i, k], recv_sems.at[i, k], (cx, cy, c))
                cp.start()
                sends.append(cp)
        for i in range(n):
            for k, (cx, cy) in enumerate(chips):
                got = outs[i].at[2 * cx + cy, half(i, c)] if i < nh else outs[i].at[2 * cx + cy]
                _remote(got, got, send_sems.at[i, k], recv_sems.at[i, k], sib).wait_recv()
                if i < nh:
                    cp = _remote(got, got, send_sems.at[i, 3 + k], recv_sems.at[i, 3 + k], sib)
                    cp.start()
                    sends.append(cp)
        for i in range(nh):
            for k, (cx, cy) in enumerate(chips):
                got = outs[i].at[2 * cx + cy, half(i, 1 - c)]
                _remote(got, got, send_sems.at[i, 3 + k], recv_sems.at[i, 3 + k], sib).wait_recv()
        for cp in sends:
            cp.wait_send()
        for cp in local:
            cp.wait()

    return pl.pallas_call(
        body, name=name, in_specs=[_ANY] * n, out_specs=[_ANY] * n,
        out_shape=[jax.ShapeDtypeStruct((N_CHIPS,) + a.shape, a.dtype) for a in arrays],
        scratch_shapes=[pltpu.SemaphoreType.DMA((n, 6)), pltpu.SemaphoreType.DMA((n, 6)),
                        pltpu.SemaphoreType.DMA((n,))],
        compiler_params=_params(),
    )(*arrays)


def _pair_exchange(gs, *, name):
    n = len(gs)

    def body(*refs):
        ins, outs = refs[:n], refs[n:2 * n]
        send_sems, recv_sems = refs[2 * n:]
        x, y, c = _coords()
        copies = []
        for i in range(n):
            for j in range(N_CHIPS):
                cp = _remote(ins[i].at[j, 1 - c], outs[i].at[j], send_sems.at[i, j], recv_sems.at[i, j], (x, y, 1 - c))
                cp.start()
                copies.append(cp)
        for cp in copies:
            cp.wait()

    return pl.pallas_call(
        body, name=name, in_specs=[_ANY] * n, out_specs=[_ANY] * n,
        out_shape=[jax.ShapeDtypeStruct((N_CHIPS,) + a.shape[2:], a.dtype) for a in gs],
        scratch_shapes=[pltpu.SemaphoreType.DMA((n, N_CHIPS)), pltpu.SemaphoreType.DMA((n, N_CHIPS))],
        compiler_params=_params(),
    )(*gs)


def _chip_exchange(ss, *, name):
    n = len(ss)

    def body(*refs):
        ins, outs = refs[:n], refs[n:2 * n]
        send_sems, recv_sems, local_sems = refs[2 * n:]
        x, y, c = _coords()
        me = 2 * x + y
        chips = _other_chips(x, y)
        local, sends = [], []
        for i in range(n):
            cp = pltpu.make_async_copy(ins[i].at[me], outs[i].at[me], local_sems.at[i])
            cp.start()
            local.append(cp)
            for k, (cx, cy) in enumerate(chips):
                cp = _remote(ins[i].at[2 * cx + cy], outs[i].at[me], send_sems.at[i, k], recv_sems.at[i, k], (cx, cy, c))
                cp.start()
                sends.append(cp)
        for i in range(n):
            for k, (cx, cy) in enumerate(chips):
                got = outs[i].at[2 * cx + cy]
                _remote(got, got, send_sems.at[i, k], recv_sems.at[i, k], (cx, cy, c)).wait_recv()
        for cp in sends:
            cp.wait_send()
        for cp in local:
            cp.wait()

    return pl.pallas_call(
        body, name=name, in_specs=[_ANY] * n, out_specs=[_ANY] * n,
        out_shape=[jax.ShapeDtypeStruct(a.shape, a.dtype) for a in ss],
        scratch_shapes=[pltpu.SemaphoreType.DMA((n, 3)), pltpu.SemaphoreType.DMA((n, 3)),
                        pltpu.SemaphoreType.DMA((n,))],
        compiler_params=_params(),
    )(*ss)


def _pair_share(hs, *, name):
    n = len(hs)

    def body(*refs):
        ins, outs = refs[:n], refs[n:2 * n]
        send_sems, recv_sems, local_sems = refs[2 * n:]
        x, y, c = _coords()
        local, copies = [], []
        for i in range(n):
            cp = pltpu.make_async_copy(ins[i], outs[i].at[c], local_sems.at[i])
            cp.start()
            local.append(cp)
            cp = _remote(ins[i], outs[i].at[c], send_sems.at[i], recv_sems.at[i], (x, y, 1 - c))
            cp.start()
            copies.append(cp)
        for cp in copies:
            cp.wait()
        for cp in local:
            cp.wait()

    return pl.pallas_call(
        body, name=name, in_specs=[_ANY] * n, out_specs=[_ANY] * n,
        out_shape=[jax.ShapeDtypeStruct((2,) + a.shape, a.dtype) for a in hs],
        scratch_shapes=[pltpu.SemaphoreType.DMA((n,)), pltpu.SemaphoreType.DMA((n,)),
                        pltpu.SemaphoreType.DMA((n,))],
        compiler_params=_params(),
    )(*hs)


def _all_exchange(vec, *, name):
    def body(v_ref, o_ref, send_sems, recv_sems, local_sem):
        x, y, c = _coords()
        me = 4 * x + 2 * y + c
        local = pltpu.make_async_copy(v_ref, o_ref.at[me], local_sem)
        local.start()
        copies = []
        k = 0
        for dx in (0, 1):
            for dy in (0, 1):
                for dc in (0, 1):
                    if dx or dy or dc:
                        peer = (1 - x if dx else x, 1 - y if dy else y, 1 - c if dc else c)
                        cp = _remote(v_ref, o_ref.at[me], send_sems.at[k], recv_sems.at[k], peer)
                        cp.start()
                        copies.append(cp)
                        k += 1
        for cp in copies:
            cp.wait()
        local.wait()

    return pl.pallas_call(
        body, name=name, in_specs=[_ANY], out_specs=_ANY,
        out_shape=jax.ShapeDtypeStruct((8,) + vec.shape, vec.dtype),
        scratch_shapes=[pltpu.SemaphoreType.DMA((7,)), pltpu.SemaphoreType.DMA((7,)), pltpu.SemaphoreType.DMA(())],
        compiler_params=_params(),
    )(vec)


def _rtile(r, pref, mult):
    t = (min(r, pref) // mult) * mult
    while t >= mult:
        if r % t == 0:
            return t
        t -= mult
    return r


def _pair_add(g, recv, core, *, name):
    _, _, r2, cols = g.shape
    tr = _rtile(r2, 256, 16)

    def body(c_ref, g_ref, r_ref, o_ref):
        o_ref[...] = (g_ref[...].astype(F32) + r_ref[...].astype(F32)).astype(o_ref.dtype)

    blk = pl.BlockSpec((None, tr, cols), lambda j, i, c_ref: (j, i, 0))
    return pl.pallas_call(
        body, name=name,
        grid_spec=pltpu.PrefetchScalarGridSpec(
            num_scalar_prefetch=1, grid=(N_CHIPS, r2 // tr),
            in_specs=[pl.BlockSpec((None, None, tr, cols), lambda j, i, c_ref: (j, c_ref[0], i, 0)), blk],
            out_specs=blk),
        out_shape=jax.ShapeDtypeStruct(recv.shape, recv.dtype), compiler_params=_params(),
    )(core, g, recv)


def _sum_slots(a, out_dtype, *, name):
    n, r, cols = a.shape
    tr = _rtile(r, 256, 16)

    def body(a_ref, o_ref):
        acc = a_ref[0].astype(F32)
        for j in range(1, n):
            acc = acc + a_ref[j].astype(F32)
        o_ref[...] = acc.astype(o_ref.dtype)

    return pl.pallas_call(
        body, name=name, grid=(r // tr,),
        in_specs=[pl.BlockSpec((n, tr, cols), lambda i: (0, i, 0))],
        out_specs=pl.BlockSpec((tr, cols), lambda i: (i, 0)),
        out_shape=jax.ShapeDtypeStruct((r, cols), out_dtype), compiler_params=_params(),
    )(a)


def _adamw(w, g, m, v, *, name, rows=256):
    r, cols = w.shape
    tr = _rtile(r, rows, 8)
    c1 = 1.0 / (1.0 - ADAM_B1 ** ADAM_STEP)
    c2 = 1.0 / (1.0 - ADAM_B2 ** ADAM_STEP)

    def body(w_ref, g_ref, m_ref, v_ref, d_ref, nm_ref, nv_ref):
        gv = g_ref[...]
        nm = ADAM_B1 * m_ref[...] + (1.0 - ADAM_B1) * gv
        nv = ADAM_B2 * v_ref[...] + (1.0 - ADAM_B2) * gv * gv
        nm_ref[...] = nm
        nv_ref[...] = nv
        d_ref[...] = -ADAM_LR * ((nm * c1) / (jnp.sqrt(nv * c2) + ADAM_EPS) + ADAM_WD * w_ref[...])

    blk = pl.BlockSpec((tr, cols), lambda i: (i, 0))
    shape = jax.ShapeDtypeStruct((r, cols), F32)
    return pl.pallas_call(
        body, name=name, grid=(r // tr,), in_specs=[blk] * 4, out_specs=[blk] * 3,
        out_shape=[shape] * 3, compiler_params=_params(),
    )(w, g, m, v)


_BIG = (("w_in", 1), ("w_branch_a", 0), ("w_branch_b", 0), ("w_out", 0), ("w_up", 1), ("w_down", 0),
        ("w_ple", 1), ("w_ple_gate", 0))
_SMALL = ("norm_mix_g", "b_f", "gmlp_ln_g", "gmlp_ln_b", "gmlp_w_s", "gmlp_b_s", "norm_ffn_g", "conv_b",
          "norm_ple_g", "norm_final_g")
_WEIGHTS = ("norm_mix_g", "w_in", "b_f", "gmlp_ln_g", "gmlp_ln_b", "gmlp_w_s", "gmlp_b_s", "w_branch_a",
            "w_branch_b", "w_out", "norm_ffn_g", "w_up", "conv_w", "conv_b", "w_down", "norm_ple_g", "w_ple",
            "w_ple_gate", "norm_final_g")
_PACK_ROWS = 8


def _pack(arrays):
    parts = []
    for a in arrays:
        flat = a.reshape(-1)
        unit = _PACK_ROWS * LANES
        flat = jnp.pad(flat, (0, (-flat.shape[0]) % unit))
        parts.append(flat.reshape(-1, LANES))
    return jnp.concatenate(parts, axis=0)


def _unpack(packed, shapes):
    out, row = [], 0
    for shp in shapes:
        size = math.prod(shp)
        rows = -(-size // (_PACK_ROWS * LANES)) * _PACK_ROWS
        out.append(packed[row:row + rows].reshape(-1)[:size].reshape(shp))
        row += rows
    return out


def _assemble(gathered, axis):
    n, r, cols = gathered.shape
    if axis == 0:
        return gathered.reshape(n * r, cols)
    return gathered.transpose(1, 0, 2).reshape(r, n * cols)


def _to_chunks(full, axis):
    if axis == 0:
        r, cols = full.shape[0] // N_CHIPS, full.shape[1]
        chunks = full.reshape(N_CHIPS, r, cols)
    else:
        r, cols = full.shape[0], full.shape[1] // N_CHIPS
        chunks = full.reshape(r, N_CHIPS, cols).transpose(1, 0, 2)
    return chunks.reshape(N_CHIPS, 2, r // 2, cols)


def kernel(x, p, norm_mix_g, w_in, b_f, gmlp_ln_g, gmlp_ln_b, gmlp_w_s, gmlp_b_s, w_branch_a, w_branch_b, w_out, norm_ffn_g, w_up, conv_w, conv_b, w_down, norm_ple_g, w_ple, w_ple_gate, norm_final_g, loss_target, m_norm_mix_g, m_w_in, m_b_f, m_gmlp_ln_g, m_gmlp_ln_b, m_gmlp_w_s, m_gmlp_b_s, m_w_branch_a, m_w_branch_b, m_w_out, m_norm_ffn_g, m_w_up, m_conv_w, m_conv_b, m_w_down, m_norm_ple_g, m_w_ple, m_w_ple_gate, m_norm_final_g, v_norm_mix_g, v_w_in, v_b_f, v_gmlp_ln_g, v_gmlp_ln_b, v_gmlp_w_s, v_gmlp_b_s, v_w_branch_a, v_w_branch_b, v_w_out, v_norm_ffn_g, v_w_up, v_conv_w, v_conv_b, v_w_down, v_norm_ple_g, v_w_ple, v_w_ple_gate, v_norm_final_g):
    args = dict(locals())
    wt = {n: args[n] for n in _WEIGHTS}
    mom = {n: args["m_" + n] for n in _WEIGHTS}
    var = {n: args["v_" + n] for n in _WEIGHTS}
    chip = 2 * lax.axis_index("x") + lax.axis_index("y")
    core = lax.axis_index("c").astype(jnp.int32).reshape(1)

    shards = [wt[n][0].astype(BF16) for n, _ in _BIG]
    gathered = _gather_weights(shards, [conv_w[0]], name="gather_weights")
    full = {n: _assemble(gathered[i], axis) for i, (n, axis) in enumerate(_BIG)}
    o1 = 2 * GMLP_WIDTH
    o2 = o1 + 3 * FOX_WIDTH
    o3 = o2 + FOX_HEADS
    fpad = ((0, 0), (0, LANES - FOX_HEADS))
    w = {
        "w_uv": full["w_in"][:, :o1], "w_qkv": full["w_in"][:, o1:o2],
        "w_f": jnp.pad(full["w_in"][:, o2:o3], fpad), "w_g": full["w_in"][:, o3:],
        "w_branch_a": full["w_branch_a"], "w_branch_b": full["w_branch_b"], "w_out": full["w_out"],
        "w_up_a": full["w_up"][:, :D_FF], "w_up_b": full["w_up"][:, D_FF:], "w_down": full["w_down"],
        "w_ple": full["w_ple"], "w_ple_gate": full["w_ple_gate"],
        "conv_w": _assemble(gathered[len(_BIG)], 1), "conv_b": conv_b,
        "norm_mix_g": norm_mix_g, "norm_ffn_g": norm_ffn_g, "norm_ple_g": norm_ple_g,
        "norm_final_g": norm_final_g.reshape(1, D_MODEL), "b_f": jnp.pad(b_f, fpad),
        "gmlp_ln_g": gmlp_ln_g, "gmlp_ln_b": gmlp_ln_b, "gmlp_w_s": gmlp_w_s[0],
        "gmlp_b_s_t": jnp.pad(gmlp_b_s[0].T, ((0, 0), (0, LANES - GMLP_GROUPS))),
    }

    loss, grad_x, g = _device_step(x[0], p[0, 0], loss_target[0], w)

    gfull = dict(g)
    gfull["w_in"] = jnp.concatenate([g["w_uv"], g["w_qkv"], g["w_f"][:, :FOX_HEADS], g["w_g"]], axis=1)
    gfull["w_up"] = jnp.concatenate([g["w_up_a"], g["w_up_b"]], axis=1)
    chunks = [_to_chunks(gfull[n], axis) for n, axis in _BIG]
    from_sibling = _pair_exchange(chunks, name="grad_pair_exchange")
    pair_sums = [_pair_add(chunks[i], from_sibling[i], core, name="grad_pair_add_" + n) for i, (n, _) in enumerate(_BIG)]
    from_chips = _chip_exchange(pair_sums, name="grad_chip_exchange")
    halves = [_sum_slots(from_chips[i], F32, name="grad_chip_sum_" + n) for i, (n, _) in enumerate(_BIG)]
    shared = _pair_share(halves, name="grad_pair_share")
    grads = {n: shared[i].reshape(wt[n].shape) for i, (n, _) in enumerate(_BIG)}

    small_g = [g[n] if n != "b_f" else g[n][:, :FOX_HEADS] for n in _SMALL]
    vec = _pack(small_g + [g["conv_w"]])
    vec = _sum_slots(_all_exchange(vec, name="small_exchange"), F32, name="small_sum")
    small_rows = _pack([wt[n] for n in _SMALL]).shape[0]
    for n, a in zip(_SMALL, _unpack(vec[:small_rows], [wt[n].shape for n in _SMALL])):
        grads[n] = a
    conv_w_grad = _unpack(vec[small_rows:], [(3, 2 * D_FF)])[0]
    grads["conv_w"] = lax.dynamic_slice_in_dim(conv_w_grad, chip * conv_w.shape[2], conv_w.shape[2], axis=1).reshape(conv_w.shape)

    delta, new_m, new_v = {}, {}, {}
    for n in [n for n, _ in _BIG] + ["conv_w"]:
        shp = wt[n].shape
        outs = _adamw(wt[n].reshape(shp[-2:]), grads[n].reshape(shp[-2:]), mom[n].reshape(shp[-2:]),
                      var[n].reshape(shp[-2:]), name="adamw_" + n)
        delta[n], new_m[n], new_v[n] = (o.reshape(shp) for o in outs)
    outs = _adamw(_pack([wt[n] for n in _SMALL]), vec[:small_rows], _pack([mom[n] for n in _SMALL]),
                  _pack([var[n] for n in _SMALL]), name="adamw_small", rows=2048)
    for d, o in zip((delta, new_m, new_v), outs):
        for n, a in zip(_SMALL, _unpack(o, [wt[n].shape for n in _SMALL])):
            d[n] = a

    total_loss = lax.psum(loss[0, 0], ("x", "y", "c"))
    return (total_loss, grad_x.reshape(x.shape), *[grads[n] for n in _WEIGHTS], *[delta[n] for n in _WEIGHTS],
            *[new_m[n] for n in _WEIGHTS], *[new_v[n] for n in _WEIGHTS])
```

```python
import functools
import math

import jax
import jax.numpy as jnp
from jax import lax
from jax.experimental import pallas as pl
from jax.experimental.pallas import tpu as pltpu

F32 = jnp.float32
BF16 = jnp.bfloat16

D_MODEL = 1024
EPS = 1e-6
CHUNK = 64
GMLP_GROUPS = 8
GMLP_BLOCK = 128
GMLP_WIDTH = 1024
FOX_HEADS = 16
FOX_HEAD_DIM = 64
FOX_WIDTH = 1024
HEAD_PAIRS = FOX_HEADS // 2
ATT_BLOCK = 128
D_FF = 2816
PLE_DIM = 256
LANES = 128
N_CHIPS = 4

ADAM_LR = 0.001
ADAM_B1 = 0.9
ADAM_B2 = 0.999
ADAM_EPS = 1e-08
ADAM_WD = 0.01
ADAM_STEP = 10

VMEM_LIMIT = 56 * 1024 * 1024
MESH = pl.DeviceIdType.MESH

_NN = (((1,), (0,)), ((), ()))
_NT = (((1,), (1,)), ((), ()))
_TN = (((0,), (0,)), ((), ()))


def _params(**kw):
    return pltpu.CompilerParams(vmem_limit_bytes=VMEM_LIMIT, **kw)


def _tile(dim, pref):
    if dim <= pref:
        return dim
    t = (pref // LANES) * LANES
    while t >= LANES:
        if dim % t == 0:
            return t
        t -= LANES
    return dim


def _dot(a, b, dn):
    return lax.dot_general(a.astype(BF16), b.astype(BF16), dn, preferred_element_type=F32)


def _gelu(x):
    c = math.sqrt(2.0 / math.pi)
    t = jnp.tanh(c * (x + 0.044715 * x * x * x))
    return 0.5 * x * (1.0 + t)


def _gelu_and_grad(x):
    c = math.sqrt(2.0 / math.pi)
    x2 = x * x
    t = jnp.tanh(c * (x + 0.044715 * x2 * x))
    g = 0.5 * x * (1.0 + t)
    dg = 0.5 * (1.0 + t) + 0.5 * x * (1.0 - t * t) * c * (1.0 + 3.0 * 0.044715 * x2)
    return g, dg


def _sigmoid(x):
    return 1.0 / (1.0 + jnp.exp(-x))


def _mm(a, b, *, mode, out_dtype, name, add=None, tm=512, tn=512):
    if mode == "nn":
        m, k = a.shape
        k2, n = b.shape
    elif mode == "nt":
        m, k = a.shape
        n, k2 = b.shape
    else:
        k, m = a.shape
        k2, n = b.shape
    assert k == k2, (name, a.shape, b.shape)
    tm = _tile(m, tm)
    tn = _tile(n, tn)
    dn = {"nn": _NN, "nt": _NT, "tn": _TN}[mode]

    def body(a_ref, b_ref, *rest):
        o_ref = rest[-1]
        acc = _dot(a_ref[...], b_ref[...], dn)
        if add is not None:
            acc = acc + rest[0][...].astype(F32)
        o_ref[...] = acc.astype(o_ref.dtype)

    a_spec = pl.BlockSpec((k, tm), lambda i, j: (0, i)) if mode == "tn" else pl.BlockSpec((tm, k), lambda i, j: (i, 0))
    b_spec = pl.BlockSpec((tn, k), lambda i, j: (j, 0)) if mode == "nt" else pl.BlockSpec((k, tn), lambda i, j: (0, j))
    o_spec = pl.BlockSpec((tm, tn), lambda i, j: (i, j))
    in_specs = [a_spec, b_spec]
    args = [a, b]
    if add is not None:
        in_specs.append(o_spec)
        args.append(add)
    return pl.pallas_call(
        body, name=name, grid=(m // tm, n // tn), in_specs=in_specs, out_specs=o_spec,
        out_shape=jax.ShapeDtypeStruct((m, n), out_dtype), compiler_params=_params(),
    )(*args)


def _rms_fwd(x, g, *, name, tm=256):
    s, d = x.shape
    tm = _tile(s, tm)

    def body(x_ref, g_ref, h_ref):
        xv = x_ref[...]
        r = lax.rsqrt(jnp.mean(xv * xv, axis=-1, keepdims=True) + EPS)
        h_ref[...] = (xv * r * g_ref[...]).astype(h_ref.dtype)

    return pl.pallas_call(
        body, name=name, grid=(s // tm,),
        in_specs=[pl.BlockSpec((tm, d), lambda i: (i, 0)), pl.BlockSpec((1, d), lambda i: (0, 0))],
        out_specs=pl.BlockSpec((tm, d), lambda i: (i, 0)),
        out_shape=jax.ShapeDtypeStruct((s, d), BF16), compiler_params=_params(),
    )(x, g)


def _rms_bwd(x, g, dh, dres, *, name, tm=256):
    s, d = x.shape
    tm = _tile(s, tm)

    def body(x_ref, g_ref, dh_ref, dres_ref, dx_ref, dg_ref):
        xv = x_ref[...]
        r = lax.rsqrt(jnp.mean(xv * xv, axis=-1, keepdims=True) + EPS)
        xhat = xv * r
        dhv = dh_ref[...].astype(F32)
        dyg = dhv * g_ref[...]
        dx = r * (dyg - xhat * jnp.mean(dyg * xhat, axis=-1, keepdims=True))
        dx_ref[...] = dres_ref[...] + dx

        @pl.when(pl.program_id(0) == 0)
        def _():
            dg_ref[...] = jnp.zeros_like(dg_ref)

        dg_ref[...] += jnp.sum(dhv * xhat, axis=0, keepdims=True)

    row = pl.BlockSpec((tm, d), lambda i: (i, 0))
    vec = pl.BlockSpec((1, d), lambda i: (0, 0))
    return pl.pallas_call(
        body, name=name, grid=(s // tm,), in_specs=[row, vec, row, row], out_specs=[row, vec],
        out_shape=[jax.ShapeDtypeStruct((s, d), F32), jax.ShapeDtypeStruct((1, d), F32)],
        compiler_params=_params(),
    )(x, g, dh, dres)


def _gmlp_mask():
    t = lax.broadcasted_iota(jnp.int32, (GMLP_BLOCK, GMLP_BLOCK), 0)
    s_ = lax.broadcasted_iota(jnp.int32, (GMLP_BLOCK, GMLP_BLOCK), 1)
    return (s_ // CHUNK) <= (t // CHUNK)


def _gmlp_norm(zv, ln_g, ln_b):
    vv, dvv = _gelu_and_grad(zv)
    mu = jnp.mean(vv, axis=-1, keepdims=True)
    xc = vv - mu
    rstd = lax.rsqrt(jnp.mean(xc * xc, axis=-1, keepdims=True) + EPS)
    vhat = xc * rstd
    return vhat * ln_g + ln_b, vhat, rstd, dvv


def _gmlp_fwd(z_uv, ln_g, ln_b, w_s, b_s_t, *, name):
    s = z_uv.shape[0]
    w = GMLP_WIDTH
    gd = w // GMLP_GROUPS

    def body(z_ref, lg_ref, lb_ref, ws_ref, bs_ref, a_ref):
        u = _gelu(z_ref[:, :w].astype(F32))
        vn, _, _, _ = _gmlp_norm(z_ref[:, w:].astype(F32), lg_ref[...], lb_ref[...])
        mask = _gmlp_mask()
        for g in range(GMLP_GROUPS):
            wm = jnp.where(mask, ws_ref[g], 0.0)
            mixed = _dot(wm, vn[:, g * gd:(g + 1) * gd], _NN) + bs_ref[:, g:g + 1]
            a_ref[:, g * gd:(g + 1) * gd] = (u[:, g * gd:(g + 1) * gd] * mixed).astype(a_ref.dtype)

    full = lambda shape: pl.BlockSpec(shape, lambda i: (0,) * len(shape))
    return pl.pallas_call(
        body, name=name, grid=(s // GMLP_BLOCK,),
        in_specs=[pl.BlockSpec((GMLP_BLOCK, 2 * w), lambda i: (i, 0)), full((1, w)), full((1, w)),
                  full((GMLP_GROUPS, GMLP_BLOCK, GMLP_BLOCK)), full((GMLP_BLOCK, LANES))],
        out_specs=pl.BlockSpec((GMLP_BLOCK, w), lambda i: (i, 0)),
        out_shape=jax.ShapeDtypeStruct((s, w), BF16), compiler_params=_params(),
    )(z_uv, ln_g, ln_b, w_s, b_s_t)


def _gmlp_bwd(z_uv, da, ln_g, ln_b, w_s, b_s_t, *, name):
    s = z_uv.shape[0]
    w = GMLP_WIDTH
    gd = w // GMLP_GROUPS

    def body(z_ref, da_ref, lg_ref, lb_ref, ws_ref, bs_ref, dz_ref, dws_ref, dbs_ref, dlg_ref, dlb_ref):
        @pl.when(pl.program_id(0) == 0)
        def _():
            dws_ref[...] = jnp.zeros_like(dws_ref)
            dbs_ref[...] = jnp.zeros_like(dbs_ref)
            dlg_ref[...] = jnp.zeros_like(dlg_ref)
            dlb_ref[...] = jnp.zeros_like(dlb_ref)

        u, du_dz = _gelu_and_grad(z_ref[:, :w].astype(F32))
        lg = lg_ref[...]
        vn, vhat, rstd, dvv_dz = _gmlp_norm(z_ref[:, w:].astype(F32), lg, lb_ref[...])
        dav = da_ref[...].astype(F32)
        mask = _gmlp_mask()
        lane = lax.broadcasted_iota(jnp.int32, (GMLP_BLOCK, LANES), 1)
        dvn_parts = []
        dbs = jnp.zeros((GMLP_BLOCK, LANES), F32)
        for g in range(GMLP_GROUPS):
            sl = slice(g * gd, (g + 1) * gd)
            wm = jnp.where(mask, ws_ref[g], 0.0)
            vn_g = vn[:, sl]
            mixed = _dot(wm, vn_g, _NN) + bs_ref[:, g:g + 1]
            dmixed = dav[:, sl] * u[:, sl]
            dz_ref[:, sl] = (dav[:, sl] * mixed * du_dz[:, sl]).astype(dz_ref.dtype)
            dvn_parts.append(_dot(wm, dmixed, _TN))
            dws_ref[g] += jnp.where(mask, _dot(dmixed, vn_g, _NT), 0.0)
            dbs = dbs + jnp.where(lane == g, jnp.sum(dmixed, axis=-1, keepdims=True), 0.0)
        dbs_ref[...] += dbs
        dvn = jnp.concatenate(dvn_parts, axis=-1)
        dlg_ref[...] += jnp.sum(dvn * vhat, axis=0, keepdims=True)
        dlb_ref[...] += jnp.sum(dvn, axis=0, keepdims=True)
        dyg = dvn * lg
        dvv = rstd * (dyg - jnp.mean(dyg, axis=-1, keepdims=True)
                      - vhat * jnp.mean(dyg * vhat, axis=-1, keepdims=True))
        dz_ref[:, w:] = (dvv * dvv_dz).astype(dz_ref.dtype)

    full = lambda shape: pl.BlockSpec(shape, lambda i: (0,) * len(shape))
    return pl.pallas_call(
        body, name=name, grid=(s // GMLP_BLOCK,),
        in_specs=[pl.BlockSpec((GMLP_BLOCK, 2 * w), lambda i: (i, 0)),
                  pl.BlockSpec((GMLP_BLOCK, w), lambda i: (i, 0)), full((1, w)), full((1, w)),
                  full((GMLP_GROUPS, GMLP_BLOCK, GMLP_BLOCK)), full((GMLP_BLOCK, LANES))],
        out_specs=[pl.BlockSpec((GMLP_BLOCK, 2 * w), lambda i: (i, 0)),
                   full((GMLP_GROUPS, GMLP_BLOCK, GMLP_BLOCK)), full((GMLP_BLOCK, LANES)),
                   full((1, w)), full((1, w))],
        out_shape=[jax.ShapeDtypeStruct((s, 2 * w), BF16),
                   jax.ShapeDtypeStruct((GMLP_GROUPS, GMLP_BLOCK, GMLP_BLOCK), F32),
                   jax.ShapeDtypeStruct((GMLP_BLOCK, LANES), F32),
                   jax.ShapeDtypeStruct((1, w), F32), jax.ShapeDtypeStruct((1, w), F32)],
        compiler_params=_params(),
    )(z_uv, da, ln_g, ln_b, w_s, b_s_t)


def _tri(lower):
    r = lax.broadcasted_iota(jnp.int32, (ATT_BLOCK, ATT_BLOCK), 0)
    c = lax.broadcasted_iota(jnp.int32, (ATT_BLOCK, ATT_BLOCK), 1)
    return jnp.where((c <= r) if lower else (c >= r), 1.0, 0.0).astype(F32)


def _log_sigmoid(x):
    return jnp.minimum(x, 0.0) - jnp.log(1.0 + jnp.exp(-jnp.abs(x)))


def _fox_cum(f, b_f, *, name):
    s = f.shape[0]
    nb = s // ATT_BLOCK

    def body(f_ref, b_ref, cb_ref, ct_ref, carry):
        @pl.when(pl.program_id(0) == 0)
        def _():
            carry[...] = jnp.zeros_like(carry)

        lf = _log_sigmoid(f_ref[...] + b_ref[...])
        cum = lax.dot_general(_tri(True), lf, _NN, precision=lax.Precision.HIGHEST,
                              preferred_element_type=F32) + carry[...]
        carry[...] = cum[ATT_BLOCK - 1:ATT_BLOCK, :]
        for h in range(FOX_HEADS):
            cb_ref[h] = jnp.broadcast_to(cum[:, h:h + 1], (ATT_BLOCK, LANES))
        ct_ref[...] = cum.T

    return pl.pallas_call(
        body, name=name, grid=(nb,),
        in_specs=[pl.BlockSpec((ATT_BLOCK, LANES), lambda i: (i, 0)), pl.BlockSpec((1, LANES), lambda i: (0, 0))],
        out_specs=[pl.BlockSpec((FOX_HEADS, ATT_BLOCK, LANES), lambda i: (0, i, 0)),
                   pl.BlockSpec((LANES, ATT_BLOCK), lambda i: (0, i))],
        out_shape=[jax.ShapeDtypeStruct((FOX_HEADS, s, LANES), F32), jax.ShapeDtypeStruct((LANES, s), F32)],
        scratch_shapes=[pltpu.VMEM((1, LANES), F32)], compiler_params=_params(),
    )(f, b_f)


def _fox_dlogit(dcum_t, f, b_f, *, name):
    s = f.shape[0]
    nb = s // ATT_BLOCK

    def body(dc_ref, f_ref, b_ref, df_ref, db_ref, carry):
        @pl.when(pl.program_id(0) == 0)
        def _():
            carry[...] = jnp.zeros_like(carry)
            db_ref[...] = jnp.zeros_like(db_ref)

        d = dc_ref[...].T
        dlog = lax.dot_general(_tri(False), d, _NN, precision=lax.Precision.HIGHEST,
                               preferred_element_type=F32) + carry[...]
        carry[...] = dlog[0:1, :]
        df = dlog * (1.0 - _sigmoid(f_ref[...] + b_ref[...]))
        df_ref[...] = df
        db_ref[...] += jnp.sum(df, axis=0, keepdims=True)

    rev = lambda i: nb - 1 - i
    return pl.pallas_call(
        body, name=name, grid=(nb,),
        in_specs=[pl.BlockSpec((LANES, ATT_BLOCK), lambda i: (0, rev(i))),
                  pl.BlockSpec((ATT_BLOCK, LANES), lambda i: (rev(i), 0)),
                  pl.BlockSpec((1, LANES), lambda i: (0, 0))],
        out_specs=[pl.BlockSpec((ATT_BLOCK, LANES), lambda i: (rev(i), 0)),
                   pl.BlockSpec((1, LANES), lambda i: (0, 0))],
        out_shape=[jax.ShapeDtypeStruct((s, LANES), F32), jax.ShapeDtypeStruct((1, LANES), F32)],
        scratch_shapes=[pltpu.VMEM((1, LANES), F32)], compiler_params=_params(),
    )(dcum_t, f, b_f)


def _causal(qi, ki):
    r = lax.broadcasted_iota(jnp.int32, (ATT_BLOCK, ATT_BLOCK), 0) + qi * ATT_BLOCK
    c = lax.broadcasted_iota(jnp.int32, (ATT_BLOCK, ATT_BLOCK), 1) + ki * ATT_BLOCK
    return c <= r


def _head_mask():
    return lax.broadcasted_iota(jnp.int32, (1, LANES), 1) < FOX_HEAD_DIM


def _attn_fwd(qkv, cum_b, cum_r, *, name):
    s = qkv.shape[0]
    nq = s // ATT_BLOCK
    scale = FOX_HEAD_DIM ** -0.5
    npair = HEAD_PAIRS

    def body(q_ref, k_ref, v_ref, cq_ref, ck_ref, o_ref, l_ref):
        qi = pl.program_id(1)
        m0 = _head_mask()
        q2 = q_ref[...]
        zero = jnp.zeros_like(q2)
        qs = (jnp.where(m0, q2, zero), jnp.where(m0, zero, q2))
        cqs = (cq_ref[0], cq_ref[1])

        def step(ki, carry, masked):
            off = pl.multiple_of(ki * ATT_BLOCK, ATT_BLOCK)
            k2 = k_ref[pl.ds(off, ATT_BLOCK), :]
            v2 = v_ref[pl.ds(off, ATT_BLOCK), :]
            out = []
            for hh in range(2):
                m, l, acc = carry[hh]
                sc = _dot(qs[hh], k2, _NT) * scale + (cqs[hh] - ck_ref[hh:hh + 1, pl.ds(off, ATT_BLOCK)])
                if masked:
                    sc = jnp.where(_causal(qi, ki), sc, -1e30)
                m_new = jnp.maximum(m, jnp.max(sc, axis=-1, keepdims=True))
                alpha = jnp.exp(m - m_new)
                p = jnp.exp(sc - m_new)
                l = alpha * l + jnp.sum(p, axis=-1, keepdims=True)
                acc = alpha * acc + _dot(p, v2, _NN)
                out.append((m_new, l, acc))
            return tuple(out)

        init = tuple((jnp.full((ATT_BLOCK, 1), -1e30, F32), jnp.zeros((ATT_BLOCK, 1), F32),
                      jnp.zeros((ATT_BLOCK, LANES), F32)) for _ in range(2))
        carry = lax.fori_loop(0, qi, lambda ki, c: step(ki, c, False), init)
        (ma, la, acca), (mb, lb, accb) = step(qi, carry, True)
        o_ref[...] = jnp.where(m0, acca / la, accb / lb).astype(o_ref.dtype)
        l_ref[0] = jnp.broadcast_to(ma + jnp.log(la), (ATT_BLOCK, LANES))
        l_ref[1] = jnp.broadcast_to(mb + jnp.log(lb), (ATT_BLOCK, LANES))

    stat = pl.BlockSpec((None, 2, ATT_BLOCK, LANES), lambda j, i: (j, 0, i, 0))
    row = pl.BlockSpec((None, 2, s), lambda j, i: (j, 0, 0))
    return pl.pallas_call(
        body, name=name, grid=(npair, nq),
        in_specs=[pl.BlockSpec((ATT_BLOCK, LANES), lambda j, i: (i, j)),
                  pl.BlockSpec((s, LANES), lambda j, i: (0, npair + j)),
                  pl.BlockSpec((s, LANES), lambda j, i: (0, 2 * npair + j)),
                  stat, row],
        out_specs=[pl.BlockSpec((ATT_BLOCK, LANES), lambda j, i: (i, j)), stat],
        out_shape=[jax.ShapeDtypeStruct((s, FOX_WIDTH), BF16),
                   jax.ShapeDtypeStruct((npair, 2, s, LANES), F32)],
        compiler_params=_params(),
    )(qkv, qkv, qkv, cum_b, cum_r)


def _attn_delta(qkv, do, lse_b, cum_b, cum_r, *, name):
    s = qkv.shape[0]
    nq = s // ATT_BLOCK
    scale = FOX_HEAD_DIM ** -0.5
    npair = HEAD_PAIRS

    def body(q_ref, k_ref, v_ref, do_ref, l_ref, cq_ref, ck_ref, d_ref):
        qi = pl.program_id(1)
        m0 = _head_mask()
        q2 = q_ref[...]
        do2 = do_ref[...]
        qs = (jnp.where(m0, q2, jnp.zeros_like(q2)), jnp.where(m0, jnp.zeros_like(q2), q2))
        dos = (jnp.where(m0, do2, jnp.zeros_like(do2)), jnp.where(m0, jnp.zeros_like(do2), do2))

        def step(ki, carry, masked):
            off = pl.multiple_of(ki * ATT_BLOCK, ATT_BLOCK)
            k2 = k_ref[pl.ds(off, ATT_BLOCK), :]
            v2 = v_ref[pl.ds(off, ATT_BLOCK), :]
            out = []
            for hh in range(2):
                sc = _dot(qs[hh], k2, _NT) * scale + (cq_ref[hh] - ck_ref[hh:hh + 1, pl.ds(off, ATT_BLOCK)])
                p = jnp.exp(sc - l_ref[hh])
                if masked:
                    p = jnp.where(_causal(qi, ki), p, 0.0)
                out.append(carry[hh] + jnp.sum(p * _dot(dos[hh], v2, _NT), axis=-1, keepdims=True))
            return tuple(out)

        init = (jnp.zeros((ATT_BLOCK, 1), F32), jnp.zeros((ATT_BLOCK, 1), F32))
        carry = lax.fori_loop(0, qi, lambda ki, c: step(ki, c, False), init)
        da, db = step(qi, carry, True)
        d_ref[0] = jnp.broadcast_to(da, (ATT_BLOCK, LANES))
        d_ref[1] = jnp.broadcast_to(db, (ATT_BLOCK, LANES))

    stat = pl.BlockSpec((None, 2, ATT_BLOCK, LANES), lambda j, i: (j, 0, i, 0))
    return pl.pallas_call(
        body, name=name, grid=(npair, nq),
        in_specs=[pl.BlockSpec((ATT_BLOCK, LANES), lambda j, i: (i, j)),
                  pl.BlockSpec((s, LANES), lambda j, i: (0, npair + j)),
                  pl.BlockSpec((s, LANES), lambda j, i: (0, 2 * npair + j)),
                  pl.BlockSpec((ATT_BLOCK, LANES), lambda j, i: (i, j)),
                  stat, stat, pl.BlockSpec((None, 2, s), lambda j, i: (j, 0, 0))],
        out_specs=stat,
        out_shape=jax.ShapeDtypeStruct((npair, 2, s, LANES), F32), compiler_params=_params(),
    )(qkv, qkv, qkv, do, lse_b, cum_b, cum_r)


def _attn_bwd(qkv, do, lse_b, delta_b, cum_b, cum_r, *, name):
    s = qkv.shape[0]
    nq = s // ATT_BLOCK
    scale = FOX_HEAD_DIM ** -0.5
    npair = HEAD_PAIRS

    def body(q_ref, k_ref, v_ref, do_ref, l_ref, dl_ref, cq_ref, ck_ref, dq_ref, dk_ref, dv_ref, dc_ref):
        ki = pl.program_id(1)
        m0 = _head_mask()
        k2 = k_ref[...]
        v2 = v_ref[...]
        koff = pl.multiple_of(ki * ATT_BLOCK, ATT_BLOCK)

        @pl.when(ki == 0)
        def _():
            dq_ref[...] = jnp.zeros_like(dq_ref)

        def step(qi, carry, masked):
            off = pl.multiple_of(qi * ATT_BLOCK, ATT_BLOCK)
            q2 = q_ref[pl.ds(off, ATT_BLOCK), :]
            do2 = do_ref[pl.ds(off, ATT_BLOCK), :]
            qzero = jnp.zeros_like(q2)
            dzero = jnp.zeros_like(do2)
            out = []
            dqs = []
            for hh in range(2):
                dk_acc, dv_acc, dc_acc = carry[hh]
                keep = m0 if hh == 0 else jnp.logical_not(m0)
                qh = jnp.where(keep, q2, qzero)
                doh = jnp.where(keep, do2, dzero)
                sc = _dot(qh, k2, _NT) * scale + (cq_ref[hh, pl.ds(off, ATT_BLOCK), :]
                                                 - ck_ref[hh:hh + 1, pl.ds(koff, ATT_BLOCK)])
                p = jnp.exp(sc - l_ref[hh, pl.ds(off, ATT_BLOCK), :])
                if masked:
                    p = jnp.where(_causal(qi, ki), p, 0.0)
                dp = _dot(doh, v2, _NT)
                ds = p * (dp - dl_ref[hh, pl.ds(off, ATT_BLOCK), :])
                dv_acc = dv_acc + _dot(p, do2, _TN)
                dk_acc = dk_acc + _dot(ds, q2, _TN)
                dc_acc = dc_acc - jnp.sum(ds, axis=0, keepdims=True)
                dqs.append(_dot(ds, k2, _NN))
                out.append((dk_acc, dv_acc, dc_acc))
            dq_ref[pl.ds(off, ATT_BLOCK), :] += jnp.where(m0, dqs[0], dqs[1]) * scale
            return tuple(out)

        init = tuple((jnp.zeros((ATT_BLOCK, LANES), F32), jnp.zeros((ATT_BLOCK, LANES), F32),
                      jnp.zeros((1, ATT_BLOCK), F32)) for _ in range(2))
        carry = step(ki, init, True)
        (dka, dva, dca), (dkb, dvb, dcb) = lax.fori_loop(ki + 1, nq, lambda qi, c: step(qi, c, False), carry)
        dk_ref[...] = (jnp.where(m0, dka, dkb) * scale).astype(dk_ref.dtype)
        dv_ref[...] = jnp.where(m0, dva, dvb).astype(dv_ref.dtype)
        dc_ref[0:1, :] = dca
        dc_ref[1:2, :] = dcb

    stat = pl.BlockSpec((None, 2, s, LANES), lambda j, i: (j, 0, 0, 0))
    colfull = lambda base: pl.BlockSpec((s, LANES), lambda j, i: (0, base + j))
    colblk = lambda base: pl.BlockSpec((ATT_BLOCK, LANES), lambda j, i: (i, base + j))
    return pl.pallas_call(
        body, name=name, grid=(npair, nq),
        in_specs=[colfull(0), colblk(npair), colblk(2 * npair), colfull(0), stat, stat, stat,
                  pl.BlockSpec((None, 2, s), lambda j, i: (j, 0, 0))],
        out_specs=[colfull(0), colblk(0), colblk(0), pl.BlockSpec((None, 2, ATT_BLOCK), lambda j, i: (j, 0, i))],
        out_shape=[jax.ShapeDtypeStruct((s, FOX_WIDTH), F32), jax.ShapeDtypeStruct((s, FOX_WIDTH), BF16),
                   jax.ShapeDtypeStruct((s, FOX_WIDTH), BF16), jax.ShapeDtypeStruct((npair, 2, s), F32)],
        compiler_params=_params(),
    )(qkv, qkv, qkv, do, lse_b, delta_b, cum_b, cum_r)


ATT_TQ = 256
ATT_TK = 256
ATT_SCALE = FOX_HEAD_DIM ** -0.5
assert ATT_SCALE == 0.125 and ATT_TQ == ATT_TK


def _causal_t(qi, ki):
    kpos = lax.broadcasted_iota(jnp.int32, (ATT_TK, ATT_TQ), 0) + ki * ATT_TK
    qpos = lax.broadcasted_iota(jnp.int32, (ATT_TK, ATT_TQ), 1) + qi * ATT_TQ
    return kpos <= qpos


def _row_mask():
    return lax.broadcasted_iota(jnp.int32, (LANES, 1), 0) < FOX_HEAD_DIM


def _lane_tile(a, width):
    return a if a.shape[1] == width else jnp.tile(a, (1, width // a.shape[1]))


def _attn_fwd_t(qkv, q_t, v_t, cum_b, cum_r, *, name):
    s = qkv.shape[0]
    nq = s // ATT_TQ
    npair = HEAD_PAIRS

    def body(k_ref, qt_ref, vt_ref, cq_ref, ck_ref, o_ref, ot_ref, l_ref):
        qi = pl.program_id(1)
        rows = _row_mask()
        qt = qt_ref[...] * ATT_SCALE
        zero = jnp.zeros_like(qt)
        qts = (jnp.where(rows, qt, zero), jnp.where(rows, zero, qt))

        def step(ki, carry, masked):
            off = pl.multiple_of(ki * ATT_TK, ATT_TK)
            k2 = k_ref[pl.ds(off, ATT_TK), :]
            vt = vt_ref[:, pl.ds(off, ATT_TK)]
            out = []
            for hh in range(2):
                m, l, acc = carry[hh]
                bias = cq_ref[hh:hh + 1, :] - _lane_tile(ck_ref[hh, pl.ds(off, ATT_TK), :], ATT_TQ)
                sc = _dot(k2, qts[hh], _NN) + bias
                if masked:
                    sc = jnp.where(_causal_t(qi, ki), sc, -1e30)
                m_new = jnp.maximum(m, jnp.max(sc, axis=0, keepdims=True))
                alpha = jnp.exp(m - m_new)
                p = jnp.exp(sc - m_new)
                l = alpha * l + jnp.sum(p, axis=0, keepdims=True)
                p_hi = p.astype(BF16)
                p_lo = (p - p_hi.astype(F32)).astype(BF16)
                acc = alpha * acc + (_dot(vt, p_hi, _NN) + _dot(vt, p_lo, _NN))
                out.append((m_new, l, acc))
            return tuple(out)

        init = tuple((jnp.full((1, ATT_TQ), -1e30, F32), jnp.zeros((1, ATT_TQ), F32),
                      jnp.zeros((LANES, ATT_TQ), F32)) for _ in range(2))
        carry = lax.fori_loop(0, qi, lambda ki, c: step(ki, c, False), init)
        (ma, la, acca), (mb, lb, accb) = step(qi, carry, True)
        ot = jnp.where(rows, acca / la, accb / lb)
        ot_ref[...] = ot
        o_ref[...] = ot.T.astype(o_ref.dtype)
        l_ref[0:1, :] = ma + jnp.log(la)
        l_ref[1:2, :] = mb + jnp.log(lb)

    row = pl.BlockSpec((None, 2, ATT_TQ), lambda j, i: (j, 0, i))
    return pl.pallas_call(
        body, name=name, grid=(npair, nq),
        in_specs=[pl.BlockSpec((s, LANES), lambda j, i: (0, npair + j)),
                  pl.BlockSpec((LANES, ATT_TQ), lambda j, i: (j, i)),
                  pl.BlockSpec((LANES, s), lambda j, i: (j, 0)),
                  row, pl.BlockSpec((None, 2, s, LANES), lambda j, i: (j, 0, 0, 0))],
        out_specs=[pl.BlockSpec((ATT_TQ, LANES), lambda j, i: (i, j)),
                   pl.BlockSpec((LANES, ATT_TQ), lambda j, i: (j, i)), row],
        out_shape=[jax.ShapeDtypeStruct((s, FOX_WIDTH), BF16), jax.ShapeDtypeStruct((FOX_WIDTH, s), F32),
                   jax.ShapeDtypeStruct((npair, 2, s), F32)],
        compiler_params=_params(),
    )(qkv, q_t, v_t, cum_r, cum_b)


def _attn_delta_t(do_t, o_t, *, name):
    s = o_t.shape[1]
    ts = _tile(s, 512)

    def body(do_ref, o_ref, d_ref):
        prod = do_ref[...].astype(F32) * o_ref[...]
        d_ref[0:1, :] = jnp.sum(prod[:FOX_HEAD_DIM], axis=0, keepdims=True)
        d_ref[1:2, :] = jnp.sum(prod[FOX_HEAD_DIM:], axis=0, keepdims=True)

    blk = pl.BlockSpec((LANES, ts), lambda j, i: (j, i))
    return pl.pallas_call(
        body, name=name, grid=(HEAD_PAIRS, s // ts), in_specs=[blk, blk],
        out_specs=pl.BlockSpec((None, 2, ts), lambda j, i: (j, 0, i)),
        out_shape=jax.ShapeDtypeStruct((HEAD_PAIRS, 2, s), F32), compiler_params=_params(),
    )(do_t, o_t)


def _attn_bwd_t(qkv, q_t, k_t, do, do_t, lse, delta, cum_b, cum_r, *, name):
    s = qkv.shape[0]
    nq = s // ATT_TQ
    npair = HEAD_PAIRS

    def body(q_ref, k_ref, v_ref, qt_ref, kt_ref, do_ref, dot_ref, l_ref, dl_ref, cq_ref, ck_ref,
             dqt_ref, dk_ref, dv_ref, dc_ref):
        ki = pl.program_id(1)
        m0 = _head_mask()
        rows = _row_mask()
        k2 = k_ref[...]
        v2 = v_ref[...]
        kt = kt_ref[...]
        ks = k2 * ATT_SCALE
        kz, vz = jnp.zeros_like(k2), jnp.zeros_like(v2)
        khs = (jnp.where(m0, ks, kz), jnp.where(m0, kz, ks))
        vhs = (jnp.where(m0, v2, vz), jnp.where(m0, vz, v2))
        cks = tuple(_lane_tile(ck_ref[hh], ATT_TQ) for hh in range(2))

        @pl.when(ki == 0)
        def _():
            dqt_ref[...] = jnp.zeros_like(dqt_ref)

        def step(qi, carry, masked):
            off = pl.multiple_of(qi * ATT_TQ, ATT_TQ)
            q2 = q_ref[pl.ds(off, ATT_TQ), :]
            do2 = do_ref[pl.ds(off, ATT_TQ), :]
            qt = qt_ref[:, pl.ds(off, ATT_TQ)]
            dot_ = dot_ref[:, pl.ds(off, ATT_TQ)]
            out, dqs = [], []
            for hh in range(2):
                dk_acc, dv_acc, dc_acc = carry[hh]
                sc = _dot(khs[hh], qt, _NN) + (cq_ref[hh:hh + 1, pl.ds(off, ATT_TQ)] - cks[hh])
                p = jnp.exp(sc - l_ref[hh:hh + 1, pl.ds(off, ATT_TQ)])
                if masked:
                    p = jnp.where(_causal_t(qi, ki), p, 0.0)
                dp = _dot(vhs[hh], dot_, _NN)
                ds = p * (dp - dl_ref[hh:hh + 1, pl.ds(off, ATT_TQ)])
                dc_acc = dc_acc - jnp.sum(ds, axis=1, keepdims=True)
                dss = (ds * ATT_SCALE).astype(BF16)
                dv_acc = dv_acc + _dot(p, do2, _NN)
                dk_acc = dk_acc + _dot(dss, q2, _NN)
                dqs.append(_dot(kt, dss, _NN))
                out.append((dk_acc, dv_acc, dc_acc))
            dqt_ref[:, pl.ds(off, ATT_TQ)] += jnp.where(rows, dqs[0], dqs[1])
            return tuple(out)

        init = tuple((jnp.zeros((ATT_TK, LANES), F32), jnp.zeros((ATT_TK, LANES), F32),
                      jnp.zeros((ATT_TK, 1), F32)) for _ in range(2))
        carry = step(ki, init, True)
        (dka, dva, dca), (dkb, dvb, dcb) = lax.fori_loop(ki + 1, nq, lambda qi, c: step(qi, c, False), carry)
        dk_ref[...] = jnp.where(m0, dka, dkb).astype(dk_ref.dtype)
        dv_ref[...] = jnp.where(m0, dva, dvb).astype(dv_ref.dtype)
        dc_ref[0] = jnp.broadcast_to(dca, (ATT_TK, LANES))
        dc_ref[1] = jnp.broadcast_to(dcb, (ATT_TK, LANES))

    colfull = lambda base: pl.BlockSpec((s, LANES), lambda j, i: (0, base + j))
    colblk = lambda base: pl.BlockSpec((ATT_TK, LANES), lambda j, i: (i, base + j))
    rowfull = pl.BlockSpec((LANES, s), lambda j, i: (j, 0))
    stat = pl.BlockSpec((None, 2, s), lambda j, i: (j, 0, 0))
    bcast = pl.BlockSpec((None, 2, ATT_TK, LANES), lambda j, i: (j, 0, i, 0))
    return pl.pallas_call(
        body, name=name, grid=(npair, nq),
        in_specs=[colfull(0), colblk(npair), colblk(2 * npair), rowfull,
                  pl.BlockSpec((LANES, ATT_TK), lambda j, i: (j, i)), colfull(0), rowfull,
                  stat, stat, stat, bcast],
        out_specs=[rowfull, colblk(0), colblk(0), bcast],
        out_shape=[jax.ShapeDtypeStruct((FOX_WIDTH, s), F32), jax.ShapeDtypeStruct((s, FOX_WIDTH), BF16),
                   jax.ShapeDtypeStruct((s, FOX_WIDTH), BF16), jax.ShapeDtypeStruct((npair, 2, s, LANES), F32)],
        compiler_params=_params(),
    )(qkv, qkv, qkv, q_t, k_t, do, do_t, lse, delta, cum_r, cum_b)


def _merge_fwd(zg, ya, yb, *, name, tm=256):
    s, d = ya.shape
    tm = _tile(s, tm)

    def body(zg_ref, ya_ref, yb_ref, m_ref):
        ga = _sigmoid(zg_ref[:, :d].astype(F32))
        gb = _sigmoid(zg_ref[:, d:].astype(F32))
        m_ref[...] = (ga * ya_ref[...].astype(F32) + gb * yb_ref[...].astype(F32)).astype(m_ref.dtype)

    row = pl.BlockSpec((tm, d), lambda i: (i, 0))
    row2 = pl.BlockSpec((tm, 2 * d), lambda i: (i, 0))
    return pl.pallas_call(
        body, name=name, grid=(s // tm,), in_specs=[row2, row, row], out_specs=row,
        out_shape=jax.ShapeDtypeStruct((s, d), BF16), compiler_params=_params(),
    )(zg, ya, yb)


def _merge_bwd(dm, zg, ya, yb, *, name, tm=256):
    s, d = ya.shape
    tm = _tile(s, tm)

    def body(dm_ref, zg_ref, ya_ref, yb_ref, dzg_ref, dya_ref, dyb_ref):
        dmv = dm_ref[...].astype(F32)
        ga = _sigmoid(zg_ref[:, :d].astype(F32))
        gb = _sigmoid(zg_ref[:, d:].astype(F32))
        dzg_ref[:, :d] = (dmv * ya_ref[...].astype(F32) * ga * (1.0 - ga)).astype(dzg_ref.dtype)
        dzg_ref[:, d:] = (dmv * yb_ref[...].astype(F32) * gb * (1.0 - gb)).astype(dzg_ref.dtype)
        dya_ref[...] = (dmv * ga).astype(dya_ref.dtype)
        dyb_ref[...] = (dmv * gb).astype(dyb_ref.dtype)

    row = pl.BlockSpec((tm, d), lambda i: (i, 0))
    row2 = pl.BlockSpec((tm, 2 * d), lambda i: (i, 0))
    return pl.pallas_call(
        body, name=name, grid=(s // tm,), in_specs=[row, row2, row, row], out_specs=[row2, row, row],
        out_shape=[jax.ShapeDtypeStruct((s, 2 * d), BF16), jax.ShapeDtypeStruct((s, d), BF16),
                   jax.ShapeDtypeStruct((s, d), BF16)],
        compiler_params=_params(),
    )(dm, zg, ya, yb)


def _shift_down(u, k, row):
    return jnp.where(row >= k, pltpu.roll(u, k, 0), 0.0)


def _shift_up(u, k, row):
    n = u.shape[0]
    return jnp.where(row < n - k, pltpu.roll(u, n - k, 0), 0.0)


def _conv_act_fwd(up_a, up_b, cw_a, cw_b, cb_a, cb_b, *, name, tc=128):
    s, f = up_a.shape
    tc = _tile(f, tc)

    def body(ua_ref, ub_ref, wa_ref, wb_ref, ba_ref, bb_ref, act_ref):
        row = lax.broadcasted_iota(jnp.int32, (s, tc), 0)

        def conv(u_ref, w_ref, b_ref):
            u = u_ref[...].astype(F32)
            return (b_ref[...] + w_ref[0:1, :] * _shift_down(u, 2, row)
                    + w_ref[1:2, :] * _shift_down(u, 1, row) + w_ref[2:3, :] * u)

        ca = conv(ua_ref, wa_ref, ba_ref)
        cb = conv(ub_ref, wb_ref, bb_ref)
        act_ref[...] = (_gelu(ca) * cb).astype(act_ref.dtype)

    col = pl.BlockSpec((s, tc), lambda j: (0, j))
    w3 = pl.BlockSpec((3, tc), lambda j: (0, j))
    b1 = pl.BlockSpec((1, tc), lambda j: (0, j))
    return pl.pallas_call(
        body, name=name, grid=(f // tc,), in_specs=[col, col, w3, w3, b1, b1], out_specs=col,
        out_shape=jax.ShapeDtypeStruct((s, f), BF16), compiler_params=_params(),
    )(up_a, up_b, cw_a, cw_b, cb_a, cb_b)


def _conv_act_bwd(up_a, up_b, dact, cw_a, cw_b, cb_a, cb_b, *, name, tc=128):
    s, f = up_a.shape
    tc = _tile(f, tc)

    def body(ua_ref, ub_ref, da_ref, wa_ref, wb_ref, ba_ref, bb_ref, dua_ref, dub_ref, dwa_ref, dwb_ref):
        row = lax.broadcasted_iota(jnp.int32, (s, tc), 0)

        def conv(u_ref, w_ref, b_ref):
            u = u_ref[...].astype(F32)
            u1 = _shift_down(u, 1, row)
            u2 = _shift_down(u, 2, row)
            return u, u1, u2, b_ref[...] + w_ref[0:1, :] * u2 + w_ref[1:2, :] * u1 + w_ref[2:3, :] * u

        def back(dc, taps, w_ref, du_ref, dw_ref):
            u, u1, u2 = taps
            dw_ref[0:1, :] = jnp.sum(dc * u2, axis=0, keepdims=True)
            dw_ref[1:2, :] = jnp.sum(dc * u1, axis=0, keepdims=True)
            dw_ref[2:3, :] = jnp.sum(dc * u, axis=0, keepdims=True)
            dw_ref[3:4, :] = jnp.sum(dc, axis=0, keepdims=True)
            du = (w_ref[2:3, :] * dc + w_ref[1:2, :] * _shift_up(dc, 1, row)
                  + w_ref[0:1, :] * _shift_up(dc, 2, row))
            du_ref[...] = du.astype(du_ref.dtype)

        ua, ua1, ua2, ca = conv(ua_ref, wa_ref, ba_ref)
        ub, ub1, ub2, cb = conv(ub_ref, wb_ref, bb_ref)
        g, dg = _gelu_and_grad(ca)
        dact_v = da_ref[...].astype(F32)
        back(dact_v * cb * dg, (ua, ua1, ua2), wa_ref, dua_ref, dwa_ref)
        back(dact_v * g, (ub, ub1, ub2), wb_ref, dub_ref, dwb_ref)

    col = pl.BlockSpec((s, tc), lambda j: (0, j))
    w3 = pl.BlockSpec((3, tc), lambda j: (0, j))
    w4 = pl.BlockSpec((4, tc), lambda j: (0, j))
    b1 = pl.BlockSpec((1, tc), lambda j: (0, j))
    return pl.pallas_call(
        body, name=name, grid=(f // tc,), in_specs=[col, col, col, w3, w3, b1, b1],
        out_specs=[col, col, w4, w4],
        out_shape=[jax.ShapeDtypeStruct((s, f), BF16), jax.ShapeDtypeStruct((s, f), BF16),
                   jax.ShapeDtypeStruct((4, f), F32), jax.ShapeDtypeStruct((4, f), F32)],
        compiler_params=_params(),
    )(up_a, up_b, dact, cw_a, cw_b, cb_a, cb_b)


def _ple_final(x2, ple, zp, target, g_final, *, name, tm=256):
    s, d = x2.shape
    tm = _tile(s, tm)

    def body(x_ref, ple_ref, zp_ref, t_ref, g_ref, dx_ref, dple_ref, dzp_ref, dg_ref, loss_ref):
        @pl.when(pl.program_id(0) == 0)
        def _():
            dg_ref[...] = jnp.zeros_like(dg_ref)
            loss_ref[...] = jnp.zeros_like(loss_ref)

        gp = _sigmoid(zp_ref[...].astype(F32))
        plev = ple_ref[...].astype(F32)
        x3 = x_ref[...] + plev * gp
        r = lax.rsqrt(jnp.mean(x3 * x3, axis=-1, keepdims=True) + EPS)
        xhat = x3 * r
        gv = g_ref[...]
        diff = xhat * gv - t_ref[...]
        loss_ref[...] += 0.5 * jnp.sum(jnp.mean(diff * diff, axis=-1, keepdims=True), axis=0, keepdims=True)
        dy = diff * (1.0 / d)
        dg_ref[...] += jnp.sum(dy * xhat, axis=0, keepdims=True)
        dyg = dy * gv
        dx3 = r * (dyg - xhat * jnp.mean(dyg * xhat, axis=-1, keepdims=True))
        dx_ref[...] = dx3
        dple_ref[...] = (dx3 * gp).astype(dple_ref.dtype)
        dzp_ref[...] = (dx3 * plev * gp * (1.0 - gp)).astype(dzp_ref.dtype)

    row = pl.BlockSpec((tm, d), lambda i: (i, 0))
    vec = pl.BlockSpec((1, d), lambda i: (0, 0))
    return pl.pallas_call(
        body, name=name, grid=(s // tm,), in_specs=[row, row, row, row, vec],
        out_specs=[row, row, row, vec, pl.BlockSpec((1, LANES), lambda i: (0, 0))],
        out_shape=[jax.ShapeDtypeStruct((s, d), F32), jax.ShapeDtypeStruct((s, d), BF16),
                   jax.ShapeDtypeStruct((s, d), BF16), jax.ShapeDtypeStruct((1, d), F32),
                   jax.ShapeDtypeStruct((1, LANES), F32)],
        compiler_params=_params(),
    )(x2, ple, zp, target, g_final)


def _device_step(x, p, target, w):
    s = x.shape[0]
    g = {}

    h = _rms_fwd(x, w["norm_mix_g"], name="rms_mix")
    z_uv = _mm(h, w["w_uv"], mode="nn", out_dtype=BF16, name="proj_uv", tm=1024)
    qkv = _mm(h, w["w_qkv"], mode="nn", out_dtype=BF16, name="proj_qkv", tm=1024)
    zg = _mm(h, w["w_g"], mode="nn", out_dtype=BF16, name="proj_gate", tm=1024)
    f = _mm(h, w["w_f"], mode="nn", out_dtype=F32, name="proj_f", tm=1024)

    a = _gmlp_fwd(z_uv, w["gmlp_ln_g"], w["gmlp_ln_b"], w["gmlp_w_s"], w["gmlp_b_s_t"], name="gmlp_fwd")

    cum_b, cum_t = _fox_cum(f, w["b_f"], name="fox_cum")
    cum_b = cum_b.reshape(HEAD_PAIRS, 2, s, LANES)
    cum_r = cum_t[:FOX_HEADS].reshape(HEAD_PAIRS, 2, s)
    q_t, k_t, v_t = (qkv[:, i * FOX_WIDTH:(i + 1) * FOX_WIDTH].T for i in range(3))
    b, o_t, lse = _attn_fwd_t(qkv, q_t, v_t, cum_b, cum_r, name="attn_fwd")

    ya = _mm(a, w["w_branch_a"], mode="nn", out_dtype=BF16, name="branch_a", tm=1024)
    yb = _mm(b, w["w_branch_b"], mode="nn", out_dtype=BF16, name="branch_b", tm=1024)
    merged = _merge_fwd(zg, ya, yb, name="merge_fwd")
    x1 = _mm(merged, w["w_out"], mode="nn", out_dtype=F32, name="proj_out", add=x, tm=1024)

    h2 = _rms_fwd(x1, w["norm_ffn_g"], name="rms_ffn")
    up_a = _mm(h2, w["w_up_a"], mode="nn", out_dtype=BF16, name="up_a", tm=1024)
    up_b = _mm(h2, w["w_up_b"], mode="nn", out_dtype=BF16, name="up_b", tm=1024)
    cw, cb = w["conv_w"], w["conv_b"]
    conv_args = (cw[:, :D_FF], cw[:, D_FF:], cb[:, :D_FF], cb[:, D_FF:])
    act = _conv_act_fwd(up_a, up_b, *conv_args, name="conv_act_fwd")
    x2 = _mm(act, w["w_down"], mode="nn", out_dtype=F32, name="down", add=x1, tm=512)

    h3 = _rms_fwd(x2, w["norm_ple_g"], name="rms_ple")
    ple = _mm(p, w["w_ple"], mode="nn", out_dtype=BF16, name="ple_proj", tm=1024)
    zp = _mm(h3, w["w_ple_gate"], mode="nn", out_dtype=BF16, name="ple_gate", tm=1024)
    dx3, dple, dzp, g["norm_final_g"], loss = _ple_final(x2, ple, zp, target, w["norm_final_g"], name="ple_final")

    g["w_ple"] = _mm(p, dple, mode="tn", out_dtype=BF16, name="dw_ple")
    g["w_ple_gate"] = _mm(h3, dzp, mode="tn", out_dtype=BF16, name="dw_ple_gate")
    dh3 = _mm(dzp, w["w_ple_gate"], mode="nt", out_dtype=BF16, name="dh3")
    dx2, g["norm_ple_g"] = _rms_bwd(x2, w["norm_ple_g"], dh3, dx3, name="rms_ple_bwd")

    g["w_down"] = _mm(act, dx2, mode="tn", out_dtype=BF16, name="dw_down")
    dact = _mm(dx2, w["w_down"], mode="nt", out_dtype=BF16, name="dact")
    dup_a, dup_b, dcw_a, dcw_b = _conv_act_bwd(up_a, up_b, dact, *conv_args, name="conv_act_bwd")
    g["conv_w"] = jnp.concatenate([dcw_a[:3], dcw_b[:3]], axis=1)
    g["conv_b"] = jnp.concatenate([dcw_a[3:], dcw_b[3:]], axis=1)
    g["w_up_a"] = _mm(h2, dup_a, mode="tn", out_dtype=BF16, name="dw_up_a")
    g["w_up_b"] = _mm(h2, dup_b, mode="tn", out_dtype=BF16, name="dw_up_b")
    dh2 = _mm(dup_a, w["w_up_a"], mode="nt", out_dtype=F32, name="dh2_a")
    dh2 = _mm(dup_b, w["w_up_b"], mode="nt", out_dtype=BF16, name="dh2_b", add=dh2)
    dx1, g["norm_ffn_g"] = _rms_bwd(x1, w["norm_ffn_g"], dh2, dx2, name="rms_ffn_bwd")

    g["w_out"] = _mm(merged, dx1, mode="tn", out_dtype=BF16, name="dw_out")
    dmerged = _mm(dx1, w["w_out"], mode="nt", out_dtype=BF16, name="dmerged")
    dzg, dya, dyb = _merge_bwd(dmerged, zg, ya, yb, name="merge_bwd")
    g["w_branch_a"] = _mm(a, dya, mode="tn", out_dtype=BF16, name="dw_branch_a")
    g["w_branch_b"] = _mm(b, dyb, mode="tn", out_dtype=BF16, name="dw_branch_b")
    da = _mm(dya, w["w_branch_a"], mode="nt", out_dtype=BF16, name="da")
    db = _mm(dyb, w["w_branch_b"], mode="nt", out_dtype=BF16, name="db")

    dz_uv, g["gmlp_w_s"], dbs_t, g["gmlp_ln_g"], g["gmlp_ln_b"] = _gmlp_bwd(
        z_uv, da, w["gmlp_ln_g"], w["gmlp_ln_b"], w["gmlp_w_s"], w["gmlp_b_s_t"], name="gmlp_bwd")
    g["gmlp_b_s"] = dbs_t[:, :GMLP_GROUPS].T

    db_t = db.T
    delta = _attn_delta_t(db_t, o_t, name="attn_delta")
    dq_t, dk, dv, dcum_b = _attn_bwd_t(qkv, q_t, k_t, db, db_t, lse, delta, cum_b, cum_r, name="attn_bwd")
    dcum_t = jnp.pad(dcum_b[..., 0].reshape(FOX_HEADS, s), ((0, LANES - FOX_HEADS), (0, 0)))
    df, g["b_f"] = _fox_dlogit(dcum_t, f, w["b_f"], name="fox_dlogit")
    dqkv = jnp.concatenate([dq_t.T.astype(BF16), dk, dv], axis=1)

    g["w_uv"] = _mm(h, dz_uv, mode="tn", out_dtype=BF16, name="dw_uv")
    g["w_qkv"] = _mm(h, dqkv, mode="tn", out_dtype=BF16, name="dw_qkv")
    g["w_f"] = _mm(h, df, mode="tn", out_dtype=BF16, name="dw_f")
    g["w_g"] = _mm(h, dzg, mode="tn", out_dtype=BF16, name="dw_g")
    dh = _mm(dz_uv, w["w_uv"], mode="nt", out_dtype=F32, name="dh_uv")
    dh = _mm(dqkv, w["w_qkv"], mode="nt", out_dtype=F32, name="dh_qkv", add=dh)
    dh = _mm(df, w["w_f"], mode="nt", out_dtype=F32, name="dh_f", add=dh)
    dh = _mm(dzg, w["w_g"], mode="nt", out_dtype=BF16, name="dh_g", add=dh)
    dx0, g["norm_mix_g"] = _rms_bwd(x, w["norm_mix_g"], dh, dx1, name="rms_mix_bwd")
    return loss, dx0, g


def _coords():
    return lax.axis_index("x"), lax.axis_index("y"), lax.axis_index("c")


def _other_chips(x, y):
    return [(1 - x, y), (x, 1 - y), (1 - x, 1 - y)]


def _remote(src, dst, send_sem, recv_sem, dev):
    return pltpu.make_async_remote_copy(src_ref=src, dst_ref=dst, send_sem=send_sem, recv_sem=recv_sem,
                                        device_id=dev, device_id_type=MESH)


_ANY = pl.BlockSpec(memory_space=pl.ANY)


def _gather_weights(halved, whole, *, name):
    nh, n = len(halved), len(halved) + len(whole)
    arrays = list(halved) + list(whole)

    def body(*refs):
        ins, outs = refs[:n], refs[n:2 * n]
        send_sems, recv_sems = refs[2 * n:]
        x, y, c = _coords()
        me, sib = 2 * x + y, (x, y, 1 - c)
        chips = _other_chips(x, y)

        def half(i, which):
            h = ins[i].shape[0] // 2
            return pl.ds(pl.multiple_of(which * h, 16), h)

        sends = []
        for i in range(n):
            src, dst = (ins[i].at[half(i, c)], outs[i].at[me, half(i, c)]) if i < nh else (ins[i], outs[i].at[me])
            for k, (cx, cy) in enumerate(chips):
                cp = _remote(src, dst, send_sems.at[i, k], recv_sems.at[i, k], (cx, cy, c))
                cp.start()
                sends.append(cp)
        for i in range(n):
            for k, (cx, cy) in enumerate(chips):
                got = outs[i].at[2 * cx + cy, half(i, c)] if i < nh else outs[i].at[2 * cx + cy]
                _remote(got, got, send_sems.at[i, k], recv_sems.at[i, k], sib).wait_recv()
                if i < nh:
                    cp = _remote(got, got, send_sems.at[i, 3 + k], recv_sems.at[i, 3 + k], sib)
                    cp.start()
                    sends.append(cp)
        for i in range(nh):
            for k, (cx, cy) in enumerate(chips):
                got = outs[i].at[2 * cx + cy, half(i, 1 - c)]
                _remote(got, got, send_sems.at[i, 3 + k], recv_sems.at[i, 3 + k], sib).wait_recv()
        for cp in sends:
            cp.wait_send()

    outs = pl.pallas_call(
        body, name=name, in_specs=[_ANY] * n, out_specs=[_ANY] * n,
        out_shape=[jax.ShapeDtypeStruct((N_CHIPS,) + a.shape, a.dtype) for a in arrays],
        scratch_shapes=[pltpu.SemaphoreType.DMA((n, 6)), pltpu.SemaphoreType.DMA((n, 6))],
        compiler_params=_params(),
    )(*arrays)
    chip = 2 * lax.axis_index("x") + lax.axis_index("y")
    return [lax.dynamic_update_index_in_dim(o, a, chip, 0) for o, a in zip(outs, arrays)]


def _pair_exchange(gs, *, name):
    n = len(gs)

    def body(*refs):
        ins, outs = refs[:n], refs[n:2 * n]
        send_sems, recv_sems = refs[2 * n:]
        x, y, c = _coords()
        copies = []
        for i in range(n):
            for j in range(N_CHIPS):
                cp = _remote(ins[i].at[j, 1 - c], outs[i].at[j], send_sems.at[i, j], recv_sems.at[i, j], (x, y, 1 - c))
                cp.start()
                copies.append(cp)
        for cp in copies:
            cp.wait()

    return pl.pallas_call(
        body, name=name, in_specs=[_ANY] * n, out_specs=[_ANY] * n,
        out_shape=[jax.ShapeDtypeStruct((N_CHIPS,) + a.shape[2:], a.dtype) for a in gs],
        scratch_shapes=[pltpu.SemaphoreType.DMA((n, N_CHIPS)), pltpu.SemaphoreType.DMA((n, N_CHIPS))],
        compiler_params=_params(),
    )(*gs)


def _chip_exchange(ss, *, name):
    n = len(ss)

    def body(*refs):
        ins, outs = refs[:n], refs[n:2 * n]
        send_sems, recv_sems = refs[2 * n:]
        x, y, c = _coords()
        me = 2 * x + y
        chips = _other_chips(x, y)
        sends = []
        for i in range(n):
            for k, (cx, cy) in enumerate(chips):
                cp = _remote(ins[i].at[2 * cx + cy], outs[i].at[me], send_sems.at[i, k], recv_sems.at[i, k], (cx, cy, c))
                cp.start()
                sends.append(cp)
        for i in range(n):
            for k, (cx, cy) in enumerate(chips):
                got = outs[i].at[2 * cx + cy]
                _remote(got, got, send_sems.at[i, k], recv_sems.at[i, k], (cx, cy, c)).wait_recv()
        for cp in sends:
            cp.wait_send()

    return pl.pallas_call(
        body, name=name, in_specs=[_ANY] * n, out_specs=[_ANY] * n,
        out_shape=[jax.ShapeDtypeStruct(a.shape, a.dtype) for a in ss],
        scratch_shapes=[pltpu.SemaphoreType.DMA((n, 3)), pltpu.SemaphoreType.DMA((n, 3))],
        compiler_params=_params(),
    )(*ss)


def _pair_share(hs, *, name):
    n = len(hs)

    def body(*refs):
        ins, outs = refs[:n], refs[n:2 * n]
        send_sems, recv_sems = refs[2 * n:]
        x, y, c = _coords()
        copies = []
        for i in range(n):
            cp = _remote(ins[i], outs[i], send_sems.at[i], recv_sems.at[i], (x, y, 1 - c))
            cp.start()
            copies.append(cp)
        for cp in copies:
            cp.wait()

    return pl.pallas_call(
        body, name=name, in_specs=[_ANY] * n, out_specs=[_ANY] * n,
        out_shape=[jax.ShapeDtypeStruct(a.shape, a.dtype) for a in hs],
        scratch_shapes=[pltpu.SemaphoreType.DMA((n,)), pltpu.SemaphoreType.DMA((n,))],
        compiler_params=_params(),
    )(*hs)


def _all_exchange(vec, *, name):
    def body(v_ref, o_ref, send_sems, recv_sems, local_sem):
        x, y, c = _coords()
        me = 4 * x + 2 * y + c
        local = pltpu.make_async_copy(v_ref, o_ref.at[me], local_sem)
        local.start()
        copies = []
        k = 0
        for dx in (0, 1):
            for dy in (0, 1):
                for dc in (0, 1):
                    if dx or dy or dc:
                        peer = (1 - x if dx else x, 1 - y if dy else y, 1 - c if dc else c)
                        cp = _remote(v_ref, o_ref.at[me], send_sems.at[k], recv_sems.at[k], peer)
                        cp.start()
                        copies.append(cp)
                        k += 1
        for cp in copies:
            cp.wait()
        local.wait()

    return pl.pallas_call(
        body, name=name, in_specs=[_ANY], out_specs=_ANY,
        out_shape=jax.ShapeDtypeStruct((8,) + vec.shape, vec.dtype),
        scratch_shapes=[pltpu.SemaphoreType.DMA((7,)), pltpu.SemaphoreType.DMA((7,)), pltpu.SemaphoreType.DMA(())],
        compiler_params=_params(),
    )(vec)


def _rtile(r, pref, mult):
    t = (min(r, pref) // mult) * mult
    while t >= mult:
        if r % t == 0:
            return t
        t -= mult
    return r


def _pair_add(g, recv, core, *, name):
    _, _, r2, cols = g.shape
    tr = _rtile(r2, 256, 16)

    def body(c_ref, g_ref, r_ref, o_ref):
        o_ref[...] = (g_ref[...].astype(F32) + r_ref[...].astype(F32)).astype(o_ref.dtype)

    blk = pl.BlockSpec((None, tr, cols), lambda j, i, c_ref: (j, i, 0))
    return pl.pallas_call(
        body, name=name,
        grid_spec=pltpu.PrefetchScalarGridSpec(
            num_scalar_prefetch=1, grid=(N_CHIPS, r2 // tr),
            in_specs=[pl.BlockSpec((None, None, tr, cols), lambda j, i, c_ref: (j, c_ref[0], i, 0)), blk],
            out_specs=blk),
        out_shape=jax.ShapeDtypeStruct(recv.shape, recv.dtype), compiler_params=_params(),
    )(core, g, recv)


def _sum_slots(a, out_dtype, *, name):
    n, r, cols = a.shape
    tr = _rtile(r, 256, 16)

    def body(a_ref, o_ref):
        acc = a_ref[0].astype(F32)
        for j in range(1, n):
            acc = acc + a_ref[j].astype(F32)
        o_ref[...] = acc.astype(o_ref.dtype)

    return pl.pallas_call(
        body, name=name, grid=(r // tr,),
        in_specs=[pl.BlockSpec((n, tr, cols), lambda i: (0, i, 0))],
        out_specs=pl.BlockSpec((tr, cols), lambda i: (i, 0)),
        out_shape=jax.ShapeDtypeStruct((r, cols), out_dtype), compiler_params=_params(),
    )(a)


def _chip_sum(own, recv, chip, *, name):
    _, r2, cols = own.shape
    tr = _rtile(r2, 256, 16)

    def body(chip_ref, own_ref, *rest):
        o_ref = rest[-1]
        acc = None
        for j in range(N_CHIPS):
            term = jnp.where(chip_ref[0] == j, own_ref[...], rest[j][...]).astype(F32)
            acc = term if acc is None else acc + term
        o_ref[...] = acc

    def slot(j):
        return pl.BlockSpec((None, tr, cols),
                            lambda i, chip_ref: (jnp.where(chip_ref[0] == j, (j + 1) % N_CHIPS, j), i, 0))

    return pl.pallas_call(
        body, name=name,
        grid_spec=pltpu.PrefetchScalarGridSpec(
            num_scalar_prefetch=1, grid=(r2 // tr,),
            in_specs=[pl.BlockSpec((None, tr, cols), lambda i, chip_ref: (chip_ref[0], i, 0))]
                     + [slot(j) for j in range(N_CHIPS)],
            out_specs=pl.BlockSpec((tr, cols), lambda i, chip_ref: (i, 0))),
        out_shape=jax.ShapeDtypeStruct((r2, cols), F32), compiler_params=_params(),
    )(chip, own, *([recv] * N_CHIPS))


def _adam_update(w, gv, m, v):
    c1 = 1.0 / (1.0 - ADAM_B1 ** ADAM_STEP)
    c2 = 1.0 / (1.0 - ADAM_B2 ** ADAM_STEP)
    nm = ADAM_B1 * m + (1.0 - ADAM_B1) * gv
    nv = ADAM_B2 * v + (1.0 - ADAM_B2) * gv * gv
    return -ADAM_LR * ((nm * c1) / (jnp.sqrt(nv * c2) + ADAM_EPS) + ADAM_WD * w), nm, nv


def _adamw_halves(w, g_mine, g_other, m, v, core, *, name):
    r, cols = w.shape
    r2 = r // 2
    tr = _rtile(r2, 256, 8)
    nt = r2 // tr

    def body(core_ref, w_ref, gm_ref, go_ref, m_ref, v_ref, g_ref, d_ref, nm_ref, nv_ref):
        gv = jnp.where(pl.program_id(0) == core_ref[0], gm_ref[...], go_ref[...])
        g_ref[...] = gv
        d_ref[...], nm_ref[...], nv_ref[...] = _adam_update(w_ref[...], gv, m_ref[...], v_ref[...])

    full = pl.BlockSpec((tr, cols), lambda hf, i, core_ref: (hf * nt + i, 0))
    half = pl.BlockSpec((tr, cols), lambda hf, i, core_ref: (i, 0))
    shape = jax.ShapeDtypeStruct((r, cols), F32)
    return pl.pallas_call(
        body, name=name,
        grid_spec=pltpu.PrefetchScalarGridSpec(
            num_scalar_prefetch=1, grid=(2, nt), in_specs=[full, half, half, full, full], out_specs=[full] * 4),
        out_shape=[shape] * 4, compiler_params=_params(),
    )(core, w, g_mine, g_other, m, v)


def _adamw(w, g, m, v, *, name, rows=256):
    r, cols = w.shape
    tr = _rtile(r, rows, 8)

    def body(w_ref, g_ref, m_ref, v_ref, d_ref, nm_ref, nv_ref):
        d_ref[...], nm_ref[...], nv_ref[...] = _adam_update(w_ref[...], g_ref[...], m_ref[...], v_ref[...])

    blk = pl.BlockSpec((tr, cols), lambda i: (i, 0))
    shape = jax.ShapeDtypeStruct((r, cols), F32)
    return pl.pallas_call(
        body, name=name, grid=(r // tr,), in_specs=[blk] * 4, out_specs=[blk] * 3,
        out_shape=[shape] * 3, compiler_params=_params(),
    )(w, g, m, v)


_BIG = (("w_in", 1), ("w_branch_a", 0), ("w_branch_b", 0), ("w_out", 0), ("w_up", 1), ("w_down", 0),
        ("w_ple", 1), ("w_ple_gate", 0))
_SMALL = ("norm_mix_g", "b_f", "gmlp_ln_g", "gmlp_ln_b", "gmlp_w_s", "gmlp_b_s", "norm_ffn_g", "conv_b",
          "norm_ple_g", "norm_final_g")
_WEIGHTS = ("norm_mix_g", "w_in", "b_f", "gmlp_ln_g", "gmlp_ln_b", "gmlp_w_s", "gmlp_b_s", "w_branch_a",
            "w_branch_b", "w_out", "norm_ffn_g", "w_up", "conv_w", "conv_b", "w_down", "norm_ple_g", "w_ple",
            "w_ple_gate", "norm_final_g")
_PACK_ROWS = 8


def _pack(arrays):
    parts = []
    for a in arrays:
        flat = a.reshape(-1)
        unit = _PACK_ROWS * LANES
        flat = jnp.pad(flat, (0, (-flat.shape[0]) % unit))
        parts.append(flat.reshape(-1, LANES))
    return jnp.concatenate(parts, axis=0)


def _unpack(packed, shapes):
    out, row = [], 0
    for shp in shapes:
        size = math.prod(shp)
        rows = -(-size // (_PACK_ROWS * LANES)) * _PACK_ROWS
        out.append(packed[row:row + rows].reshape(-1)[:size].reshape(shp))
        row += rows
    return out


def _assemble(gathered, axis):
    n, r, cols = gathered.shape
    if axis == 0:
        return gathered.reshape(n * r, cols)
    return gathered.transpose(1, 0, 2).reshape(r, n * cols)


def _to_chunks(full, axis):
    if axis == 0:
        r, cols = full.shape[0] // N_CHIPS, full.shape[1]
        chunks = full.reshape(N_CHIPS, r, cols)
    else:
        r, cols = full.shape[0], full.shape[1] // N_CHIPS
        chunks = full.reshape(r, N_CHIPS, cols).transpose(1, 0, 2)
    return chunks.reshape(N_CHIPS, 2, r // 2, cols)


def kernel(x, p, norm_mix_g, w_in, b_f, gmlp_ln_g, gmlp_ln_b, gmlp_w_s, gmlp_b_s, w_branch_a, w_branch_b, w_out, norm_ffn_g, w_up, conv_w, conv_b, w_down, norm_ple_g, w_ple, w_ple_gate, norm_final_g, loss_target, m_norm_mix_g, m_w_in, m_b_f, m_gmlp_ln_g, m_gmlp_ln_b, m_gmlp_w_s, m_gmlp_b_s, m_w_branch_a, m_w_branch_b, m_w_out, m_norm_ffn_g, m_w_up, m_conv_w, m_conv_b, m_w_down, m_norm_ple_g, m_w_ple, m_w_ple_gate, m_norm_final_g, v_norm_mix_g, v_w_in, v_b_f, v_gmlp_ln_g, v_gmlp_ln_b, v_gmlp_w_s, v_gmlp_b_s, v_w_branch_a, v_w_branch_b, v_w_out, v_norm_ffn_g, v_w_up, v_conv_w, v_conv_b, v_w_down, v_norm_ple_g, v_w_ple, v_w_ple_gate, v_norm_final_g):
    args = dict(locals())
    wt = {n: args[n] for n in _WEIGHTS}
    mom = {n: args["m_" + n] for n in _WEIGHTS}
    var = {n: args["v_" + n] for n in _WEIGHTS}
    chip = 2 * lax.axis_index("x") + lax.axis_index("y")
    core = lax.axis_index("c").astype(jnp.int32).reshape(1)

    shards = [wt[n][0].astype(BF16) for n, _ in _BIG]
    gathered = _gather_weights(shards, [conv_w[0]], name="gather_weights")
    full = {n: _assemble(gathered[i], axis) for i, (n, axis) in enumerate(_BIG)}
    o1 = 2 * GMLP_WIDTH
    o2 = o1 + 3 * FOX_WIDTH
    o3 = o2 + FOX_HEADS
    fpad = ((0, 0), (0, LANES - FOX_HEADS))
    w = {
        "w_uv": full["w_in"][:, :o1], "w_qkv": full["w_in"][:, o1:o2],
        "w_f": jnp.pad(full["w_in"][:, o2:o3], fpad), "w_g": full["w_in"][:, o3:],
        "w_branch_a": full["w_branch_a"], "w_branch_b": full["w_branch_b"], "w_out": full["w_out"],
        "w_up_a": full["w_up"][:, :D_FF], "w_up_b": full["w_up"][:, D_FF:], "w_down": full["w_down"],
        "w_ple": full["w_ple"], "w_ple_gate": full["w_ple_gate"],
        "conv_w": _assemble(gathered[len(_BIG)], 1), "conv_b": conv_b,
        "norm_mix_g": norm_mix_g, "norm_ffn_g": norm_ffn_g, "norm_ple_g": norm_ple_g,
        "norm_final_g": norm_final_g.reshape(1, D_MODEL), "b_f": jnp.pad(b_f, fpad),
        "gmlp_ln_g": gmlp_ln_g, "gmlp_ln_b": gmlp_ln_b, "gmlp_w_s": gmlp_w_s[0],
        "gmlp_b_s_t": jnp.pad(gmlp_b_s[0].T, ((0, 0), (0, LANES - GMLP_GROUPS))),
    }

    loss, grad_x, g = _device_step(x[0], p[0, 0], loss_target[0], w)

    gfull = dict(g)
    gfull["w_in"] = jnp.concatenate([g["w_uv"], g["w_qkv"], g["w_f"][:, :FOX_HEADS], g["w_g"]], axis=1)
    gfull["w_up"] = jnp.concatenate([g["w_up_a"], g["w_up_b"]], axis=1)
    chunks = [_to_chunks(gfull[n], axis) for n, axis in _BIG]
    from_sibling = _pair_exchange(chunks, name="grad_pair_exchange")
    pair_sums = [_pair_add(chunks[i], from_sibling[i], core, name="grad_pair_add_" + n) for i, (n, _) in enumerate(_BIG)]
    from_chips = _chip_exchange(pair_sums, name="grad_chip_exchange")
    chip1 = chip.astype(jnp.int32).reshape(1)
    halves = [_chip_sum(pair_sums[i], from_chips[i], chip1, name="grad_chip_sum_" + n) for i, (n, _) in enumerate(_BIG)]
    other_halves = _pair_share(halves, name="grad_pair_share")
    grads = {}

    small_g = [g[n] if n != "b_f" else g[n][:, :FOX_HEADS] for n in _SMALL]
    vec = _pack(small_g + [g["conv_w"]])
    vec = _sum_slots(_all_exchange(vec, name="small_exchange"), F32, name="small_sum")
    small_rows = _pack([wt[n] for n in _SMALL]).shape[0]
    for n, a in zip(_SMALL, _unpack(vec[:small_rows], [wt[n].shape for n in _SMALL])):
        grads[n] = a
    conv_w_grad = _unpack(vec[small_rows:], [(3, 2 * D_FF)])[0]
    grads["conv_w"] = lax.dynamic_slice_in_dim(conv_w_grad, chip * conv_w.shape[2], conv_w.shape[2], axis=1).reshape(conv_w.shape)

    delta, new_m, new_v = {}, {}, {}
    for i, (n, _) in enumerate(_BIG):
        shp = wt[n].shape
        outs = _adamw_halves(wt[n].reshape(shp[-2:]), halves[i], other_halves[i], mom[n].reshape(shp[-2:]),
                             var[n].reshape(shp[-2:]), core, name="adamw_" + n)
        grads[n], delta[n], new_m[n], new_v[n] = (o.reshape(shp) for o in outs)
    shp = conv_w.shape
    outs = _adamw(conv_w.reshape(shp[-2:]), grads["conv_w"].reshape(shp[-2:]), m_conv_w.reshape(shp[-2:]),
                  v_conv_w.reshape(shp[-2:]), name="adamw_conv_w")
    delta["conv_w"], new_m["conv_w"], new_v["conv_w"] = (o.reshape(shp) for o in outs)
    outs = _adamw(_pack([wt[n] for n in _SMALL]), vec[:small_rows], _pack([mom[n] for n in _SMALL]),
                  _pack([var[n] for n in _SMALL]), name="adamw_small", rows=2048)
    for d, o in zip((delta, new_m, new_v), outs):
        for n, a in zip(_SMALL, _unpack(o, [wt[n].shape for n in _SMALL])):
            d[n] = a

    total_loss = lax.psum(loss[0, 0], ("x", "y", "c"))
    return (total_loss, grad_x.reshape(x.shape), *[grads[n] for n in _WEIGHTS], *[delta[n] for n in _WEIGHTS],
            *[new_m[n] for n in _WEIGHTS], *[new_v[n] for n in _WEIGHTS])
```

```python
import functools
import math

import jax
import jax.numpy as jnp
from jax import lax
from jax.experimental import pallas as pl
from jax.experimental.pallas import tpu as pltpu

F32 = jnp.float32
BF16 = jnp.bfloat16

D_MODEL = 1024
EPS = 1e-6
CHUNK = 64
GMLP_GROUPS = 8
GMLP_BLOCK = 128
GMLP_WIDTH = 1024
FOX_HEADS = 16
FOX_HEAD_DIM = 64
FOX_WIDTH = 1024
HEAD_PAIRS = FOX_HEADS // 2
ATT_BLOCK = 128
D_FF = 2816
PLE_DIM = 256
LANES = 128
N_CHIPS = 4

ADAM_LR = 0.001
ADAM_B1 = 0.9
ADAM_B2 = 0.999
ADAM_EPS = 1e-08
ADAM_WD = 0.01
ADAM_STEP = 10

VMEM_LIMIT = 56 * 1024 * 1024
MESH = pl.DeviceIdType.MESH

_NN = (((1,), (0,)), ((), ()))
_NT = (((1,), (1,)), ((), ()))
_TN = (((0,), (0,)), ((), ()))


def _params(**kw):
    return pltpu.CompilerParams(vmem_limit_bytes=VMEM_LIMIT, **kw)


def _tile(dim, pref):
    if dim <= pref:
        return dim
    t = (pref // LANES) * LANES
    while t >= LANES:
        if dim % t == 0:
            return t
        t -= LANES
    return dim


def _dot(a, b, dn):
    return lax.dot_general(a.astype(BF16), b.astype(BF16), dn, preferred_element_type=F32)


def _gelu(x):
    c = math.sqrt(2.0 / math.pi)
    t = jnp.tanh(c * (x + 0.044715 * x * x * x))
    return 0.5 * x * (1.0 + t)


def _gelu_and_grad(x):
    c = math.sqrt(2.0 / math.pi)
    x2 = x * x
    t = jnp.tanh(c * (x + 0.044715 * x2 * x))
    g = 0.5 * x * (1.0 + t)
    dg = 0.5 * (1.0 + t) + 0.5 * x * (1.0 - t * t) * c * (1.0 + 3.0 * 0.044715 * x2)
    return g, dg


def _sigmoid(x):
    return 1.0 / (1.0 + jnp.exp(-x))


def _mm(a, b, *, mode, out_dtype, name, add=None, tm=512, tn=512, dep=None):
    if mode == "nn":
        m, k = a.shape
        k2, n = b.shape
    elif mode == "nt":
        m, k = a.shape
        n, k2 = b.shape
    else:
        k, m = a.shape
        k2, n = b.shape
    assert k == k2, (name, a.shape, b.shape)
    tm = _tile(m, tm)
    tn = _tile(n, tn)
    dn = {"nn": _NN, "nt": _NT, "tn": _TN}[mode]

    def body(a_ref, b_ref, *rest):
        o_ref = rest[-1]
        acc = _dot(a_ref[...], b_ref[...], dn)
        if add is not None:
            acc = acc + rest[0][...].astype(F32)
        o_ref[...] = acc.astype(o_ref.dtype)

    a_spec = pl.BlockSpec((k, tm), lambda i, j: (0, i)) if mode == "tn" else pl.BlockSpec((tm, k), lambda i, j: (i, 0))
    b_spec = pl.BlockSpec((tn, k), lambda i, j: (j, 0)) if mode == "nt" else pl.BlockSpec((k, tn), lambda i, j: (0, j))
    o_spec = pl.BlockSpec((tm, tn), lambda i, j: (i, j))
    in_specs = [a_spec, b_spec]
    args = [a, b]
    if add is not None:
        in_specs.append(o_spec)
        args.append(add)
    if dep is not None:
        in_specs.append(pl.BlockSpec(memory_space=pl.ANY))
        args.append(dep)
    return pl.pallas_call(
        body, name=name, grid=(m // tm, n // tn), in_specs=in_specs, out_specs=o_spec,
        out_shape=jax.ShapeDtypeStruct((m, n), out_dtype), compiler_params=_params(),
    )(*args)


def _rms_fwd(x, g, *, name, tm=256):
    s, d = x.shape
    tm = _tile(s, tm)

    def body(x_ref, g_ref, h_ref):
        xv = x_ref[...]
        r = lax.rsqrt(jnp.mean(xv * xv, axis=-1, keepdims=True) + EPS)
        h_ref[...] = (xv * r * g_ref[...]).astype(h_ref.dtype)

    return pl.pallas_call(
        body, name=name, grid=(s // tm,),
        in_specs=[pl.BlockSpec((tm, d), lambda i: (i, 0)), pl.BlockSpec((1, d), lambda i: (0, 0))],
        out_specs=pl.BlockSpec((tm, d), lambda i: (i, 0)),
        out_shape=jax.ShapeDtypeStruct((s, d), BF16), compiler_params=_params(),
    )(x, g)


def _rms_bwd(x, g, dh, dres, *, name, tm=256):
    s, d = x.shape
    tm = _tile(s, tm)

    def body(x_ref, g_ref, dh_ref, dres_ref, dx_ref, dg_ref):
        xv = x_ref[...]
        r = lax.rsqrt(jnp.mean(xv * xv, axis=-1, keepdims=True) + EPS)
        xhat = xv * r
        dhv = dh_ref[...].astype(F32)
        dyg = dhv * g_ref[...]
        dx = r * (dyg - xhat * jnp.mean(dyg * xhat, axis=-1, keepdims=True))
        dx_ref[...] = dres_ref[...] + dx

        @pl.when(pl.program_id(0) == 0)
        def _():
            dg_ref[...] = jnp.zeros_like(dg_ref)

        dg_ref[...] += jnp.sum(dhv * xhat, axis=0, keepdims=True)

    row = pl.BlockSpec((tm, d), lambda i: (i, 0))
    vec = pl.BlockSpec((1, d), lambda i: (0, 0))
    return pl.pallas_call(
        body, name=name, grid=(s // tm,), in_specs=[row, vec, row, row], out_specs=[row, vec],
        out_shape=[jax.ShapeDtypeStruct((s, d), F32), jax.ShapeDtypeStruct((1, d), F32)],
        compiler_params=_params(),
    )(x, g, dh, dres)


def _gmlp_mask():
    t = lax.broadcasted_iota(jnp.int32, (GMLP_BLOCK, GMLP_BLOCK), 0)
    s_ = lax.broadcasted_iota(jnp.int32, (GMLP_BLOCK, GMLP_BLOCK), 1)
    return (s_ // CHUNK) <= (t // CHUNK)


def _gmlp_norm(zv, ln_g, ln_b):
    vv, dvv = _gelu_and_grad(zv)
    mu = jnp.mean(vv, axis=-1, keepdims=True)
    xc = vv - mu
    rstd = lax.rsqrt(jnp.mean(xc * xc, axis=-1, keepdims=True) + EPS)
    vhat = xc * rstd
    return vhat * ln_g + ln_b, vhat, rstd, dvv


def _gmlp_fwd(z_uv, ln_g, ln_b, w_s, b_s_t, *, name):
    s = z_uv.shape[0]
    w = GMLP_WIDTH
    gd = w // GMLP_GROUPS

    def body(z_ref, lg_ref, lb_ref, ws_ref, bs_ref, a_ref):
        u = _gelu(z_ref[:, :w].astype(F32))
        vn, _, _, _ = _gmlp_norm(z_ref[:, w:].astype(F32), lg_ref[...], lb_ref[...])
        mask = _gmlp_mask()
        for g in range(GMLP_GROUPS):
            wm = jnp.where(mask, ws_ref[g], 0.0)
            mixed = _dot(wm, vn[:, g * gd:(g + 1) * gd], _NN) + bs_ref[:, g:g + 1]
            a_ref[:, g * gd:(g + 1) * gd] = (u[:, g * gd:(g + 1) * gd] * mixed).astype(a_ref.dtype)

    full = lambda shape: pl.BlockSpec(shape, lambda i: (0,) * len(shape))
    return pl.pallas_call(
        body, name=name, grid=(s // GMLP_BLOCK,),
        in_specs=[pl.BlockSpec((GMLP_BLOCK, 2 * w), lambda i: (i, 0)), full((1, w)), full((1, w)),
                  full((GMLP_GROUPS, GMLP_BLOCK, GMLP_BLOCK)), full((GMLP_BLOCK, LANES))],
        out_specs=pl.BlockSpec((GMLP_BLOCK, w), lambda i: (i, 0)),
        out_shape=jax.ShapeDtypeStruct((s, w), BF16), compiler_params=_params(),
    )(z_uv, ln_g, ln_b, w_s, b_s_t)


def _gmlp_bwd(z_uv, da, ln_g, ln_b, w_s, b_s_t, *, name):
    s = z_uv.shape[0]
    w = GMLP_WIDTH
    gd = w // GMLP_GROUPS

    def body(z_ref, da_ref, lg_ref, lb_ref, ws_ref, bs_ref, dz_ref, dws_ref, dbs_ref, dlg_ref, dlb_ref):
        @pl.when(pl.program_id(0) == 0)
        def _():
            dws_ref[...] = jnp.zeros_like(dws_ref)
            dbs_ref[...] = jnp.zeros_like(dbs_ref)
            dlg_ref[...] = jnp.zeros_like(dlg_ref)
            dlb_ref[...] = jnp.zeros_like(dlb_ref)

        u, du_dz = _gelu_and_grad(z_ref[:, :w].astype(F32))
        lg = lg_ref[...]
        vn, vhat, rstd, dvv_dz = _gmlp_norm(z_ref[:, w:].astype(F32), lg, lb_ref[...])
        dav = da_ref[...].astype(F32)
        mask = _gmlp_mask()
        lane = lax.broadcasted_iota(jnp.int32, (GMLP_BLOCK, LANES), 1)
        dvn_parts = []
        dbs = jnp.zeros((GMLP_BLOCK, LANES), F32)
        for g in range(GMLP_GROUPS):
            sl = slice(g * gd, (g + 1) * gd)
            wm = jnp.where(mask, ws_ref[g], 0.0)
            vn_g = vn[:, sl]
            mixed = _dot(wm, vn_g, _NN) + bs_ref[:, g:g + 1]
            dmixed = dav[:, sl] * u[:, sl]
            dz_ref[:, sl] = (dav[:, sl] * mixed * du_dz[:, sl]).astype(dz_ref.dtype)
            dvn_parts.append(_dot(wm, dmixed, _TN))
            dws_ref[g] += jnp.where(mask, _dot(dmixed, vn_g, _NT), 0.0)
            dbs = dbs + jnp.where(lane == g, jnp.sum(dmixed, axis=-1, keepdims=True), 0.0)
        dbs_ref[...] += dbs
        dvn = jnp.concatenate(dvn_parts, axis=-1)
        dlg_ref[...] += jnp.sum(dvn * vhat, axis=0, keepdims=True)
        dlb_ref[...] += jnp.sum(dvn, axis=0, keepdims=True)
        dyg = dvn * lg
        dvv = rstd * (dyg - jnp.mean(dyg, axis=-1, keepdims=True)
                      - vhat * jnp.mean(dyg * vhat, axis=-1, keepdims=True))
        dz_ref[:, w:] = (dvv * dvv_dz).astype(dz_ref.dtype)

    full = lambda shape: pl.BlockSpec(shape, lambda i: (0,) * len(shape))
    return pl.pallas_call(
        body, name=name, grid=(s // GMLP_BLOCK,),
        in_specs=[pl.BlockSpec((GMLP_BLOCK, 2 * w), lambda i: (i, 0)),
                  pl.BlockSpec((GMLP_BLOCK, w), lambda i: (i, 0)), full((1, w)), full((1, w)),
                  full((GMLP_GROUPS, GMLP_BLOCK, GMLP_BLOCK)), full((GMLP_BLOCK, LANES))],
        out_specs=[pl.BlockSpec((GMLP_BLOCK, 2 * w), lambda i: (i, 0)),
                   full((GMLP_GROUPS, GMLP_BLOCK, GMLP_BLOCK)), full((GMLP_BLOCK, LANES)),
                   full((1, w)), full((1, w))],
        out_shape=[jax.ShapeDtypeStruct((s, 2 * w), BF16),
                   jax.ShapeDtypeStruct((GMLP_GROUPS, GMLP_BLOCK, GMLP_BLOCK), F32),
                   jax.ShapeDtypeStruct((GMLP_BLOCK, LANES), F32),
                   jax.ShapeDtypeStruct((1, w), F32), jax.ShapeDtypeStruct((1, w), F32)],
        compiler_params=_params(),
    )(z_uv, da, ln_g, ln_b, w_s, b_s_t)


def _tri(lower):
    r = lax.broadcasted_iota(jnp.int32, (ATT_BLOCK, ATT_BLOCK), 0)
    c = lax.broadcasted_iota(jnp.int32, (ATT_BLOCK, ATT_BLOCK), 1)
    return jnp.where((c <= r) if lower else (c >= r), 1.0, 0.0).astype(F32)


def _log_sigmoid(x):
    return jnp.minimum(x, 0.0) - jnp.log(1.0 + jnp.exp(-jnp.abs(x)))


def _fox_cum(f, b_f, *, name):
    s = f.shape[0]
    nb = s // ATT_BLOCK

    def body(f_ref, b_ref, cb_ref, ct_ref, carry):
        @pl.when(pl.program_id(0) == 0)
        def _():
            carry[...] = jnp.zeros_like(carry)

        lf = _log_sigmoid(f_ref[...] + b_ref[...])
        cum = lax.dot_general(_tri(True), lf, _NN, precision=lax.Precision.HIGHEST,
                              preferred_element_type=F32) + carry[...]
        carry[...] = cum[ATT_BLOCK - 1:ATT_BLOCK, :]
        for h in range(FOX_HEADS):
            cb_ref[h] = jnp.broadcast_to(cum[:, h:h + 1], (ATT_BLOCK, LANES))
        ct_ref[...] = cum.T

    return pl.pallas_call(
        body, name=name, grid=(nb,),
        in_specs=[pl.BlockSpec((ATT_BLOCK, LANES), lambda i: (i, 0)), pl.BlockSpec((1, LANES), lambda i: (0, 0))],
        out_specs=[pl.BlockSpec((FOX_HEADS, ATT_BLOCK, LANES), lambda i: (0, i, 0)),
                   pl.BlockSpec((LANES, ATT_BLOCK), lambda i: (0, i))],
        out_shape=[jax.ShapeDtypeStruct((FOX_HEADS, s, LANES), F32), jax.ShapeDtypeStruct((LANES, s), F32)],
        scratch_shapes=[pltpu.VMEM((1, LANES), F32)], compiler_params=_params(),
    )(f, b_f)


def _fox_dlogit(dcum_t, f, b_f, *, name):
    s = f.shape[0]
    nb = s // ATT_BLOCK

    def body(dc_ref, f_ref, b_ref, df_ref, db_ref, carry):
        @pl.when(pl.program_id(0) == 0)
        def _():
            carry[...] = jnp.zeros_like(carry)
            db_ref[...] = jnp.zeros_like(db_ref)

        d = dc_ref[...].T
        dlog = lax.dot_general(_tri(False), d, _NN, precision=lax.Precision.HIGHEST,
                               preferred_element_type=F32) + carry[...]
        carry[...] = dlog[0:1, :]
        df = dlog * (1.0 - _sigmoid(f_ref[...] + b_ref[...]))
        df_ref[...] = df
        db_ref[...] += jnp.sum(df, axis=0, keepdims=True)

    rev = lambda i: nb - 1 - i
    return pl.pallas_call(
        body, name=name, grid=(nb,),
        in_specs=[pl.BlockSpec((LANES, ATT_BLOCK), lambda i: (0, rev(i))),
                  pl.BlockSpec((ATT_BLOCK, LANES), lambda i: (rev(i), 0)),
                  pl.BlockSpec((1, LANES), lambda i: (0, 0))],
        out_specs=[pl.BlockSpec((ATT_BLOCK, LANES), lambda i: (rev(i), 0)),
                   pl.BlockSpec((1, LANES), lambda i: (0, 0))],
        out_shape=[jax.ShapeDtypeStruct((s, LANES), F32), jax.ShapeDtypeStruct((1, LANES), F32)],
        scratch_shapes=[pltpu.VMEM((1, LANES), F32)], compiler_params=_params(),
    )(dcum_t, f, b_f)


def _causal(qi, ki):
    r = lax.broadcasted_iota(jnp.int32, (ATT_BLOCK, ATT_BLOCK), 0) + qi * ATT_BLOCK
    c = lax.broadcasted_iota(jnp.int32, (ATT_BLOCK, ATT_BLOCK), 1) + ki * ATT_BLOCK
    return c <= r


def _head_mask():
    return lax.broadcasted_iota(jnp.int32, (1, LANES), 1) < FOX_HEAD_DIM


def _attn_fwd(qkv, cum_b, cum_r, *, name):
    s = qkv.shape[0]
    nq = s // ATT_BLOCK
    scale = FOX_HEAD_DIM ** -0.5
    npair = HEAD_PAIRS

    def body(q_ref, k_ref, v_ref, cq_ref, ck_ref, o_ref, l_ref):
        qi = pl.program_id(1)
        m0 = _head_mask()
        q2 = q_ref[...]
        zero = jnp.zeros_like(q2)
        qs = (jnp.where(m0, q2, zero), jnp.where(m0, zero, q2))
        cqs = (cq_ref[0], cq_ref[1])

        def step(ki, carry, masked):
            off = pl.multiple_of(ki * ATT_BLOCK, ATT_BLOCK)
            k2 = k_ref[pl.ds(off, ATT_BLOCK), :]
            v2 = v_ref[pl.ds(off, ATT_BLOCK), :]
            out = []
            for hh in range(2):
                m, l, acc = carry[hh]
                sc = _dot(qs[hh], k2, _NT) * scale + (cqs[hh] - ck_ref[hh:hh + 1, pl.ds(off, ATT_BLOCK)])
                if masked:
                    sc = jnp.where(_causal(qi, ki), sc, -1e30)
                m_new = jnp.maximum(m, jnp.max(sc, axis=-1, keepdims=True))
                alpha = jnp.exp(m - m_new)
                p = jnp.exp(sc - m_new)
                l = alpha * l + jnp.sum(p, axis=-1, keepdims=True)
                acc = alpha * acc + _dot(p, v2, _NN)
                out.append((m_new, l, acc))
            return tuple(out)

        init = tuple((jnp.full((ATT_BLOCK, 1), -1e30, F32), jnp.zeros((ATT_BLOCK, 1), F32),
                      jnp.zeros((ATT_BLOCK, LANES), F32)) for _ in range(2))
        carry = lax.fori_loop(0, qi, lambda ki, c: step(ki, c, False), init)
        (ma, la, acca), (mb, lb, accb) = step(qi, carry, True)
        o_ref[...] = jnp.where(m0, acca / la, accb / lb).astype(o_ref.dtype)
        l_ref[0] = jnp.broadcast_to(ma + jnp.log(la), (ATT_BLOCK, LANES))
        l_ref[1] = jnp.broadcast_to(mb + jnp.log(lb), (ATT_BLOCK, LANES))

    stat = pl.BlockSpec((None, 2, ATT_BLOCK, LANES), lambda j, i: (j, 0, i, 0))
    row = pl.BlockSpec((None, 2, s), lambda j, i: (j, 0, 0))
    return pl.pallas_call(
        body, name=name, grid=(npair, nq),
        in_specs=[pl.BlockSpec((ATT_BLOCK, LANES), lambda j, i: (i, j)),
                  pl.BlockSpec((s, LANES), lambda j, i: (0, npair + j)),
                  pl.BlockSpec((s, LANES), lambda j, i: (0, 2 * npair + j)),
                  stat, row],
        out_specs=[pl.BlockSpec((ATT_BLOCK, LANES), lambda j, i: (i, j)), stat],
        out_shape=[jax.ShapeDtypeStruct((s, FOX_WIDTH), BF16),
                   jax.ShapeDtypeStruct((npair, 2, s, LANES), F32)],
        compiler_params=_params(),
    )(qkv, qkv, qkv, cum_b, cum_r)


def _attn_delta(qkv, do, lse_b, cum_b, cum_r, *, name):
    s = qkv.shape[0]
    nq = s // ATT_BLOCK
    scale = FOX_HEAD_DIM ** -0.5
    npair = HEAD_PAIRS

    def body(q_ref, k_ref, v_ref, do_ref, l_ref, cq_ref, ck_ref, d_ref):
        qi = pl.program_id(1)
        m0 = _head_mask()
        q2 = q_ref[...]
        do2 = do_ref[...]
        qs = (jnp.where(m0, q2, jnp.zeros_like(q2)), jnp.where(m0, jnp.zeros_like(q2), q2))
        dos = (jnp.where(m0, do2, jnp.zeros_like(do2)), jnp.where(m0, jnp.zeros_like(do2), do2))

        def step(ki, carry, masked):
            off = pl.multiple_of(ki * ATT_BLOCK, ATT_BLOCK)
            k2 = k_ref[pl.ds(off, ATT_BLOCK), :]
            v2 = v_ref[pl.ds(off, ATT_BLOCK), :]
            out = []
            for hh in range(2):
                sc = _dot(qs[hh], k2, _NT) * scale + (cq_ref[hh] - ck_ref[hh:hh + 1, pl.ds(off, ATT_BLOCK)])
                p = jnp.exp(sc - l_ref[hh])
                if masked:
                    p = jnp.where(_causal(qi, ki), p, 0.0)
                out.append(carry[hh] + jnp.sum(p * _dot(dos[hh], v2, _NT), axis=-1, keepdims=True))
            return tuple(out)

        init = (jnp.zeros((ATT_BLOCK, 1), F32), jnp.zeros((ATT_BLOCK, 1), F32))
        carry = lax.fori_loop(0, qi, lambda ki, c: step(ki, c, False), init)
        da, db = step(qi, carry, True)
        d_ref[0] = jnp.broadcast_to(da, (ATT_BLOCK, LANES))
        d_ref[1] = jnp.broadcast_to(db, (ATT_BLOCK, LANES))

    stat = pl.BlockSpec((None, 2, ATT_BLOCK, LANES), lambda j, i: (j, 0, i, 0))
    return pl.pallas_call(
        body, name=name, grid=(npair, nq),
        in_specs=[pl.BlockSpec((ATT_BLOCK, LANES), lambda j, i: (i, j)),
                  pl.BlockSpec((s, LANES), lambda j, i: (0, npair + j)),
                  pl.BlockSpec((s, LANES), lambda j, i: (0, 2 * npair + j)),
                  pl.BlockSpec((ATT_BLOCK, LANES), lambda j, i: (i, j)),
                  stat, stat, pl.BlockSpec((None, 2, s), lambda j, i: (j, 0, 0))],
        out_specs=stat,
        out_shape=jax.ShapeDtypeStruct((npair, 2, s, LANES), F32), compiler_params=_params(),
    )(qkv, qkv, qkv, do, lse_b, cum_b, cum_r)


def _attn_bwd(qkv, do, lse_b, delta_b, cum_b, cum_r, *, name):
    s = qkv.shape[0]
    nq = s // ATT_BLOCK
    scale = FOX_HEAD_DIM ** -0.5
    npair = HEAD_PAIRS

    def body(q_ref, k_ref, v_ref, do_ref, l_ref, dl_ref, cq_ref, ck_ref, dq_ref, dk_ref, dv_ref, dc_ref):
        ki = pl.program_id(1)
        m0 = _head_mask()
        k2 = k_ref[...]
        v2 = v_ref[...]
        koff = pl.multiple_of(ki * ATT_BLOCK, ATT_BLOCK)

        @pl.when(ki == 0)
        def _():
            dq_ref[...] = jnp.zeros_like(dq_ref)

        def step(qi, carry, masked):
            off = pl.multiple_of(qi * ATT_BLOCK, ATT_BLOCK)
            q2 = q_ref[pl.ds(off, ATT_BLOCK), :]
            do2 = do_ref[pl.ds(off, ATT_BLOCK), :]
            qzero = jnp.zeros_like(q2)
            dzero = jnp.zeros_like(do2)
            out = []
            dqs = []
            for hh in range(2):
                dk_acc, dv_acc, dc_acc = carry[hh]
                keep = m0 if hh == 0 else jnp.logical_not(m0)
                qh = jnp.where(keep, q2, qzero)
                doh = jnp.where(keep, do2, dzero)
                sc = _dot(qh, k2, _NT) * scale + (cq_ref[hh, pl.ds(off, ATT_BLOCK), :]
                                                 - ck_ref[hh:hh + 1, pl.ds(koff, ATT_BLOCK)])
                p = jnp.exp(sc - l_ref[hh, pl.ds(off, ATT_BLOCK), :])
                if masked:
                    p = jnp.where(_causal(qi, ki), p, 0.0)
                dp = _dot(doh, v2, _NT)
                ds = p * (dp - dl_ref[hh, pl.ds(off, ATT_BLOCK), :])
                dv_acc = dv_acc + _dot(p, do2, _TN)
                dk_acc = dk_acc + _dot(ds, q2, _TN)
                dc_acc = dc_acc - jnp.sum(ds, axis=0, keepdims=True)
                dqs.append(_dot(ds, k2, _NN))
                out.append((dk_acc, dv_acc, dc_acc))
            dq_ref[pl.ds(off, ATT_BLOCK), :] += jnp.where(m0, dqs[0], dqs[1]) * scale
            return tuple(out)

        init = tuple((jnp.zeros((ATT_BLOCK, LANES), F32), jnp.zeros((ATT_BLOCK, LANES), F32),
                      jnp.zeros((1, ATT_BLOCK), F32)) for _ in range(2))
        carry = step(ki, init, True)
        (dka, dva, dca), (dkb, dvb, dcb) = lax.fori_loop(ki + 1, nq, lambda qi, c: step(qi, c, False), carry)
        dk_ref[...] = (jnp.where(m0, dka, dkb) * scale).astype(dk_ref.dtype)
        dv_ref[...] = jnp.where(m0, dva, dvb).astype(dv_ref.dtype)
        dc_ref[0:1, :] = dca
        dc_ref[1:2, :] = dcb

    stat = pl.BlockSpec((None, 2, s, LANES), lambda j, i: (j, 0, 0, 0))
    colfull = lambda base: pl.BlockSpec((s, LANES), lambda j, i: (0, base + j))
    colblk = lambda base: pl.BlockSpec((ATT_BLOCK, LANES), lambda j, i: (i, base + j))
    return pl.pallas_call(
        body, name=name, grid=(npair, nq),
        in_specs=[colfull(0), colblk(npair), colblk(2 * npair), colfull(0), stat, stat, stat,
                  pl.BlockSpec((None, 2, s), lambda j, i: (j, 0, 0))],
        out_specs=[colfull(0), colblk(0), colblk(0), pl.BlockSpec((None, 2, ATT_BLOCK), lambda j, i: (j, 0, i))],
        out_shape=[jax.ShapeDtypeStruct((s, FOX_WIDTH), F32), jax.ShapeDtypeStruct((s, FOX_WIDTH), BF16),
                   jax.ShapeDtypeStruct((s, FOX_WIDTH), BF16), jax.ShapeDtypeStruct((npair, 2, s), F32)],
        compiler_params=_params(),
    )(qkv, qkv, qkv, do, lse_b, delta_b, cum_b, cum_r)


ATT_TQ = 256
ATT_TK = 256
ATT_SCALE = FOX_HEAD_DIM ** -0.5
assert ATT_SCALE == 0.125 and ATT_TQ == ATT_TK


def _causal_t(qi, ki):
    kpos = lax.broadcasted_iota(jnp.int32, (ATT_TK, ATT_TQ), 0) + ki * ATT_TK
    qpos = lax.broadcasted_iota(jnp.int32, (ATT_TK, ATT_TQ), 1) + qi * ATT_TQ
    return kpos <= qpos


def _row_mask():
    return lax.broadcasted_iota(jnp.int32, (LANES, 1), 0) < FOX_HEAD_DIM


def _lane_tile(a, width):
    return a if a.shape[1] == width else jnp.tile(a, (1, width // a.shape[1]))


def _attn_fwd_t(qkv, q_t, v_t, cum_b, cum_r, *, name):
    s = qkv.shape[0]
    nq = s // ATT_TQ
    npair = HEAD_PAIRS

    def body(k_ref, qt_ref, vt_ref, cq_ref, ck_ref, o_ref, ot_ref, l_ref):
        qi = pl.program_id(1)
        rows = _row_mask()
        qt = qt_ref[...] * ATT_SCALE
        zero = jnp.zeros_like(qt)
        qts = (jnp.where(rows, qt, zero), jnp.where(rows, zero, qt))

        def step(ki, carry, masked):
            off = pl.multiple_of(ki * ATT_TK, ATT_TK)
            k2 = k_ref[pl.ds(off, ATT_TK), :]
            vt = vt_ref[:, pl.ds(off, ATT_TK)]
            out = []
            for hh in range(2):
                m, l, acc = carry[hh]
                bias = cq_ref[hh:hh + 1, :] - _lane_tile(ck_ref[hh, pl.ds(off, ATT_TK), :], ATT_TQ)
                sc = _dot(k2, qts[hh], _NN) + bias
                if masked:
                    sc = jnp.where(_causal_t(qi, ki), sc, -1e30)
                m_new = jnp.maximum(m, jnp.max(sc, axis=0, keepdims=True))
                alpha = jnp.exp(m - m_new)
                p = jnp.exp(sc - m_new)
                l = alpha * l + jnp.sum(p, axis=0, keepdims=True)
                p_hi = p.astype(BF16)
                p_lo = (p - p_hi.astype(F32)).astype(BF16)
                acc = alpha * acc + (_dot(vt, p_hi, _NN) + _dot(vt, p_lo, _NN))
                out.append((m_new, l, acc))
            return tuple(out)

        init = tuple((jnp.full((1, ATT_TQ), -1e30, F32), jnp.zeros((1, ATT_TQ), F32),
                      jnp.zeros((LANES, ATT_TQ), F32)) for _ in range(2))
        carry = lax.fori_loop(0, qi, lambda ki, c: step(ki, c, False), init)
        (ma, la, acca), (mb, lb, accb) = step(qi, carry, True)
        ot = jnp.where(rows, acca / la, accb / lb)
        ot_ref[...] = ot
        o_ref[...] = ot.T.astype(o_ref.dtype)
        l_ref[0:1, :] = ma + jnp.log(la)
        l_ref[1:2, :] = mb + jnp.log(lb)

    row = pl.BlockSpec((None, 2, ATT_TQ), lambda j, i: (j, 0, i))
    return pl.pallas_call(
        body, name=name, grid=(npair, nq),
        in_specs=[pl.BlockSpec((s, LANES), lambda j, i: (0, npair + j)),
                  pl.BlockSpec((LANES, ATT_TQ), lambda j, i: (j, i)),
                  pl.BlockSpec((LANES, s), lambda j, i: (j, 0)),
                  row, pl.BlockSpec((None, 2, s, LANES), lambda j, i: (j, 0, 0, 0))],
        out_specs=[pl.BlockSpec((ATT_TQ, LANES), lambda j, i: (i, j)),
                   pl.BlockSpec((LANES, ATT_TQ), lambda j, i: (j, i)), row],
        out_shape=[jax.ShapeDtypeStruct((s, FOX_WIDTH), BF16), jax.ShapeDtypeStruct((FOX_WIDTH, s), F32),
                   jax.ShapeDtypeStruct((npair, 2, s), F32)],
        compiler_params=_params(),
    )(qkv, q_t, v_t, cum_r, cum_b)


def _attn_delta_t(do_t, o_t, *, name):
    s = o_t.shape[1]
    ts = _tile(s, 512)

    def body(do_ref, o_ref, d_ref):
        prod = do_ref[...].astype(F32) * o_ref[...]
        d_ref[0:1, :] = jnp.sum(prod[:FOX_HEAD_DIM], axis=0, keepdims=True)
        d_ref[1:2, :] = jnp.sum(prod[FOX_HEAD_DIM:], axis=0, keepdims=True)

    blk = pl.BlockSpec((LANES, ts), lambda j, i: (j, i))
    return pl.pallas_call(
        body, name=name, grid=(HEAD_PAIRS, s // ts), in_specs=[blk, blk],
        out_specs=pl.BlockSpec((None, 2, ts), lambda j, i: (j, 0, i)),
        out_shape=jax.ShapeDtypeStruct((HEAD_PAIRS, 2, s), F32), compiler_params=_params(),
    )(do_t, o_t)


def _attn_bwd_t(qkv, q_t, k_t, do, do_t, lse, delta, cum_b, cum_r, *, name):
    s = qkv.shape[0]
    nq = s // ATT_TQ
    npair = HEAD_PAIRS

    def body(q_ref, k_ref, v_ref, qt_ref, kt_ref, do_ref, dot_ref, l_ref, dl_ref, cq_ref, ck_ref,
             dqt_ref, dk_ref, dv_ref, dc_ref):
        ki = pl.program_id(1)
        m0 = _head_mask()
        rows = _row_mask()
        k2 = k_ref[...]
        v2 = v_ref[...]
        kt = kt_ref[...]
        ks = k2 * ATT_SCALE
        kz, vz = jnp.zeros_like(k2), jnp.zeros_like(v2)
        khs = (jnp.where(m0, ks, kz), jnp.where(m0, kz, ks))
        vhs = (jnp.where(m0, v2, vz), jnp.where(m0, vz, v2))
        cks = tuple(_lane_tile(ck_ref[hh], ATT_TQ) for hh in range(2))

        @pl.when(ki == 0)
        def _():
            dqt_ref[...] = jnp.zeros_like(dqt_ref)

        def step(qi, carry, masked):
            off = pl.multiple_of(qi * ATT_TQ, ATT_TQ)
            q2 = q_ref[pl.ds(off, ATT_TQ), :]
            do2 = do_ref[pl.ds(off, ATT_TQ), :]
            qt = qt_ref[:, pl.ds(off, ATT_TQ)]
            dot_ = dot_ref[:, pl.ds(off, ATT_TQ)]
            out, dqs = [], []
            for hh in range(2):
                dk_acc, dv_acc, dc_acc = carry[hh]
                sc = _dot(khs[hh], qt, _NN) + (cq_ref[hh:hh + 1, pl.ds(off, ATT_TQ)] - cks[hh])
                p = jnp.exp(sc - l_ref[hh:hh + 1, pl.ds(off, ATT_TQ)])
                if masked:
                    p = jnp.where(_causal_t(qi, ki), p, 0.0)
                dp = _dot(vhs[hh], dot_, _NN)
                ds = p * (dp - dl_ref[hh:hh + 1, pl.ds(off, ATT_TQ)])
                dc_acc = dc_acc - jnp.sum(ds, axis=1, keepdims=True)
                dss = (ds * ATT_SCALE).astype(BF16)
                dv_acc = dv_acc + _dot(p, do2, _NN)
                dk_acc = dk_acc + _dot(dss, q2, _NN)
                dqs.append(_dot(kt, dss, _NN))
                out.append((dk_acc, dv_acc, dc_acc))
            dqt_ref[:, pl.ds(off, ATT_TQ)] += jnp.where(rows, dqs[0], dqs[1])
            return tuple(out)

        init = tuple((jnp.zeros((ATT_TK, LANES), F32), jnp.zeros((ATT_TK, LANES), F32),
                      jnp.zeros((ATT_TK, 1), F32)) for _ in range(2))
        carry = step(ki, init, True)
        (dka, dva, dca), (dkb, dvb, dcb) = lax.fori_loop(ki + 1, nq, lambda qi, c: step(qi, c, False), carry)
        dk_ref[...] = jnp.where(m0, dka, dkb).astype(dk_ref.dtype)
        dv_ref[...] = jnp.where(m0, dva, dvb).astype(dv_ref.dtype)
        dc_ref[0] = jnp.broadcast_to(dca, (ATT_TK, LANES))
        dc_ref[1] = jnp.broadcast_to(dcb, (ATT_TK, LANES))

    colfull = lambda base: pl.BlockSpec((s, LANES), lambda j, i: (0, base + j))
    colblk = lambda base: pl.BlockSpec((ATT_TK, LANES), lambda j, i: (i, base + j))
    rowfull = pl.BlockSpec((LANES, s), lambda j, i: (j, 0))
    stat = pl.BlockSpec((None, 2, s), lambda j, i: (j, 0, 0))
    bcast = pl.BlockSpec((None, 2, ATT_TK, LANES), lambda j, i: (j, 0, i, 0))
    return pl.pallas_call(
        body, name=name, grid=(npair, nq),
        in_specs=[colfull(0), colblk(npair), colblk(2 * npair), rowfull,
                  pl.BlockSpec((LANES, ATT_TK), lambda j, i: (j, i)), colfull(0), rowfull,
                  stat, stat, stat, bcast],
        out_specs=[rowfull, colblk(0), colblk(0), bcast],
        out_shape=[jax.ShapeDtypeStruct((FOX_WIDTH, s), F32), jax.ShapeDtypeStruct((s, FOX_WIDTH), BF16),
                   jax.ShapeDtypeStruct((s, FOX_WIDTH), BF16), jax.ShapeDtypeStruct((npair, 2, s, LANES), F32)],
        compiler_params=_params(),
    )(qkv, qkv, qkv, q_t, k_t, do, do_t, lse, delta, cum_r, cum_b)


def _merge_fwd(zg, ya, yb, *, name, tm=256):
    s, d = ya.shape
    tm = _tile(s, tm)

    def body(zg_ref, ya_ref, yb_ref, m_ref):
        ga = _sigmoid(zg_ref[:, :d].astype(F32))
        gb = _sigmoid(zg_ref[:, d:].astype(F32))
        m_ref[...] = (ga * ya_ref[...].astype(F32) + gb * yb_ref[...].astype(F32)).astype(m_ref.dtype)

    row = pl.BlockSpec((tm, d), lambda i: (i, 0))
    row2 = pl.BlockSpec((tm, 2 * d), lambda i: (i, 0))
    return pl.pallas_call(
        body, name=name, grid=(s // tm,), in_specs=[row2, row, row], out_specs=row,
        out_shape=jax.ShapeDtypeStruct((s, d), BF16), compiler_params=_params(),
    )(zg, ya, yb)


def _merge_bwd(dm, zg, ya, yb, *, name, tm=256):
    s, d = ya.shape
    tm = _tile(s, tm)

    def body(dm_ref, zg_ref, ya_ref, yb_ref, dzg_ref, dya_ref, dyb_ref):
        dmv = dm_ref[...].astype(F32)
        ga = _sigmoid(zg_ref[:, :d].astype(F32))
        gb = _sigmoid(zg_ref[:, d:].astype(F32))
        dzg_ref[:, :d] = (dmv * ya_ref[...].astype(F32) * ga * (1.0 - ga)).astype(dzg_ref.dtype)
        dzg_ref[:, d:] = (dmv * yb_ref[...].astype(F32) * gb * (1.0 - gb)).astype(dzg_ref.dtype)
        dya_ref[...] = (dmv * ga).astype(dya_ref.dtype)
        dyb_ref[...] = (dmv * gb).astype(dyb_ref.dtype)

    row = pl.BlockSpec((tm, d), lambda i: (i, 0))
    row2 = pl.BlockSpec((tm, 2 * d), lambda i: (i, 0))
    return pl.pallas_call(
        body, name=name, grid=(s // tm,), in_specs=[row, row2, row, row], out_specs=[row2, row, row],
        out_shape=[jax.ShapeDtypeStruct((s, 2 * d), BF16), jax.ShapeDtypeStruct((s, d), BF16),
                   jax.ShapeDtypeStruct((s, d), BF16)],
        compiler_params=_params(),
    )(dm, zg, ya, yb)


def _shift_down(u, k, row):
    return jnp.where(row >= k, pltpu.roll(u, k, 0), 0.0)


def _shift_up(u, k, row):
    n = u.shape[0]
    return jnp.where(row < n - k, pltpu.roll(u, n - k, 0), 0.0)


def _conv_act_fwd(up_a, up_b, cw_a, cw_b, cb_a, cb_b, *, name, tc=128):
    s, f = up_a.shape
    tc = _tile(f, tc)

    def body(ua_ref, ub_ref, wa_ref, wb_ref, ba_ref, bb_ref, act_ref):
        row = lax.broadcasted_iota(jnp.int32, (s, tc), 0)

        def conv(u_ref, w_ref, b_ref):
            u = u_ref[...].astype(F32)
            return (b_ref[...] + w_ref[0:1, :] * _shift_down(u, 2, row)
                    + w_ref[1:2, :] * _shift_down(u, 1, row) + w_ref[2:3, :] * u)

        ca = conv(ua_ref, wa_ref, ba_ref)
        cb = conv(ub_ref, wb_ref, bb_ref)
        act_ref[...] = (_gelu(ca) * cb).astype(act_ref.dtype)

    col = pl.BlockSpec((s, tc), lambda j: (0, j))
    w3 = pl.BlockSpec((3, tc), lambda j: (0, j))
    b1 = pl.BlockSpec((1, tc), lambda j: (0, j))
    return pl.pallas_call(
        body, name=name, grid=(f // tc,), in_specs=[col, col, w3, w3, b1, b1], out_specs=col,
        out_shape=jax.ShapeDtypeStruct((s, f), BF16), compiler_params=_params(),
    )(up_a, up_b, cw_a, cw_b, cb_a, cb_b)


def _conv_act_bwd(up_a, up_b, dact, cw_a, cw_b, cb_a, cb_b, *, name, tc=128):
    s, f = up_a.shape
    tc = _tile(f, tc)

    def body(ua_ref, ub_ref, da_ref, wa_ref, wb_ref, ba_ref, bb_ref, dua_ref, dub_ref, dwa_ref, dwb_ref):
        row = lax.broadcasted_iota(jnp.int32, (s, tc), 0)

        def conv(u_ref, w_ref, b_ref):
            u = u_ref[...].astype(F32)
            u1 = _shift_down(u, 1, row)
            u2 = _shift_down(u, 2, row)
            return u, u1, u2, b_ref[...] + w_ref[0:1, :] * u2 + w_ref[1:2, :] * u1 + w_ref[2:3, :] * u

        def back(dc, taps, w_ref, du_ref, dw_ref):
            u, u1, u2 = taps
            dw_ref[0:1, :] = jnp.sum(dc * u2, axis=0, keepdims=True)
            dw_ref[1:2, :] = jnp.sum(dc * u1, axis=0, keepdims=True)
            dw_ref[2:3, :] = jnp.sum(dc * u, axis=0, keepdims=True)
            dw_ref[3:4, :] = jnp.sum(dc, axis=0, keepdims=True)
            du = (w_ref[2:3, :] * dc + w_ref[1:2, :] * _shift_up(dc, 1, row)
                  + w_ref[0:1, :] * _shift_up(dc, 2, row))
            du_ref[...] = du.astype(du_ref.dtype)

        ua, ua1, ua2, ca = conv(ua_ref, wa_ref, ba_ref)
        ub, ub1, ub2, cb = conv(ub_ref, wb_ref, bb_ref)
        g, dg = _gelu_and_grad(ca)
        dact_v = da_ref[...].astype(F32)
        back(dact_v * cb * dg, (ua, ua1, ua2), wa_ref, dua_ref, dwa_ref)
        back(dact_v * g, (ub, ub1, ub2), wb_ref, dub_ref, dwb_ref)

    col = pl.BlockSpec((s, tc), lambda j: (0, j))
    w3 = pl.BlockSpec((3, tc), lambda j: (0, j))
    w4 = pl.BlockSpec((4, tc), lambda j: (0, j))
    b1 = pl.BlockSpec((1, tc), lambda j: (0, j))
    return pl.pallas_call(
        body, name=name, grid=(f // tc,), in_specs=[col, col, col, w3, w3, b1, b1],
        out_specs=[col, col, w4, w4],
        out_shape=[jax.ShapeDtypeStruct((s, f), BF16), jax.ShapeDtypeStruct((s, f), BF16),
                   jax.ShapeDtypeStruct((4, f), F32), jax.ShapeDtypeStruct((4, f), F32)],
        compiler_params=_params(),
    )(up_a, up_b, dact, cw_a, cw_b, cb_a, cb_b)


def _ple_final(x2, ple, zp, target, g_final, *, name, tm=256):
    s, d = x2.shape
    tm = _tile(s, tm)

    def body(x_ref, ple_ref, zp_ref, t_ref, g_ref, dx_ref, dple_ref, dzp_ref, dg_ref, loss_ref):
        @pl.when(pl.program_id(0) == 0)
        def _():
            dg_ref[...] = jnp.zeros_like(dg_ref)
            loss_ref[...] = jnp.zeros_like(loss_ref)

        gp = _sigmoid(zp_ref[...].astype(F32))
        plev = ple_ref[...].astype(F32)
        x3 = x_ref[...] + plev * gp
        r = lax.rsqrt(jnp.mean(x3 * x3, axis=-1, keepdims=True) + EPS)
        xhat = x3 * r
        gv = g_ref[...]
        diff = xhat * gv - t_ref[...]
        loss_ref[...] += 0.5 * jnp.sum(jnp.mean(diff * diff, axis=-1, keepdims=True), axis=0, keepdims=True)
        dy = diff * (1.0 / d)
        dg_ref[...] += jnp.sum(dy * xhat, axis=0, keepdims=True)
        dyg = dy * gv
        dx3 = r * (dyg - xhat * jnp.mean(dyg * xhat, axis=-1, keepdims=True))
        dx_ref[...] = dx3
        dple_ref[...] = (dx3 * gp).astype(dple_ref.dtype)
        dzp_ref[...] = (dx3 * plev * gp * (1.0 - gp)).astype(dzp_ref.dtype)

    row = pl.BlockSpec((tm, d), lambda i: (i, 0))
    vec = pl.BlockSpec((1, d), lambda i: (0, 0))
    return pl.pallas_call(
        body, name=name, grid=(s // tm,), in_specs=[row, row, row, row, vec],
        out_specs=[row, row, row, vec, pl.BlockSpec((1, LANES), lambda i: (0, 0))],
        out_shape=[jax.ShapeDtypeStruct((s, d), F32), jax.ShapeDtypeStruct((s, d), BF16),
                   jax.ShapeDtypeStruct((s, d), BF16), jax.ShapeDtypeStruct((1, d), F32),
                   jax.ShapeDtypeStruct((1, LANES), F32)],
        compiler_params=_params(),
    )(x2, ple, zp, target, g_final)


def _device_step(x, p, target, w, get_w_in=None, get_w_rest=None, on_grads_ffn=None, on_grads_mix=None):
    s = x.shape[0]
    g = {}
    w = dict(w)

    h = _rms_fwd(x, w["norm_mix_g"], name="rms_mix")
    if get_w_in is not None:
        w.update(get_w_in(h))
    z_uv = _mm(h, w["w_uv"], mode="nn", out_dtype=BF16, name="proj_uv", tm=1024)
    qkv = _mm(h, w["w_qkv"], mode="nn", out_dtype=BF16, name="proj_qkv", tm=1024)
    zg = _mm(h, w["w_g"], mode="nn", out_dtype=BF16, name="proj_gate", tm=1024)
    f = _mm(h, w["w_f"], mode="nn", out_dtype=F32, name="proj_f", tm=1024)

    a = _gmlp_fwd(z_uv, w["gmlp_ln_g"], w["gmlp_ln_b"], w["gmlp_w_s"], w["gmlp_b_s_t"], name="gmlp_fwd")

    cum_b, cum_t = _fox_cum(f, w["b_f"], name="fox_cum")
    cum_b = cum_b.reshape(HEAD_PAIRS, 2, s, LANES)
    cum_r = cum_t[:FOX_HEADS].reshape(HEAD_PAIRS, 2, s)
    q_t, k_t, v_t = (qkv[:, i * FOX_WIDTH:(i + 1) * FOX_WIDTH].T for i in range(3))
    b, o_t, lse = _attn_fwd_t(qkv, q_t, v_t, cum_b, cum_r, name="attn_fwd")
    if get_w_rest is not None:
        w.update(get_w_rest(b))

    ya = _mm(a, w["w_branch_a"], mode="nn", out_dtype=BF16, name="branch_a", tm=1024)
    yb = _mm(b, w["w_branch_b"], mode="nn", out_dtype=BF16, name="branch_b", tm=1024)
    merged = _merge_fwd(zg, ya, yb, name="merge_fwd")
    x1 = _mm(merged, w["w_out"], mode="nn", out_dtype=F32, name="proj_out", add=x, tm=1024)

    h2 = _rms_fwd(x1, w["norm_ffn_g"], name="rms_ffn")
    up_a = _mm(h2, w["w_up_a"], mode="nn", out_dtype=BF16, name="up_a", tm=1024)
    up_b = _mm(h2, w["w_up_b"], mode="nn", out_dtype=BF16, name="up_b", tm=1024)
    cw, cb = w["conv_w"], w["conv_b"]
    conv_args = (cw[:, :D_FF], cw[:, D_FF:], cb[:, :D_FF], cb[:, D_FF:])
    act = _conv_act_fwd(up_a, up_b, *conv_args, name="conv_act_fwd")
    x2 = _mm(act, w["w_down"], mode="nn", out_dtype=F32, name="down", add=x1, tm=512)

    h3 = _rms_fwd(x2, w["norm_ple_g"], name="rms_ple")
    ple = _mm(p, w["w_ple"], mode="nn", out_dtype=BF16, name="ple_proj", tm=1024)
    zp = _mm(h3, w["w_ple_gate"], mode="nn", out_dtype=BF16, name="ple_gate", tm=1024)
    dx3, dple, dzp, g["norm_final_g"], loss = _ple_final(x2, ple, zp, target, w["norm_final_g"], name="ple_final")

    g["w_ple"] = _mm(p, dple, mode="tn", out_dtype=BF16, name="dw_ple")
    g["w_ple_gate"] = _mm(h3, dzp, mode="tn", out_dtype=BF16, name="dw_ple_gate")
    dh3 = _mm(dzp, w["w_ple_gate"], mode="nt", out_dtype=BF16, name="dh3")
    dx2, g["norm_ple_g"] = _rms_bwd(x2, w["norm_ple_g"], dh3, dx3, name="rms_ple_bwd")

    g["w_down"] = _mm(act, dx2, mode="tn", out_dtype=BF16, name="dw_down")
    dact = _mm(dx2, w["w_down"], mode="nt", out_dtype=BF16, name="dact")
    dup_a, dup_b, dcw_a, dcw_b = _conv_act_bwd(up_a, up_b, dact, *conv_args, name="conv_act_bwd")
    g["conv_w"] = jnp.concatenate([dcw_a[:3], dcw_b[:3]], axis=1)
    g["conv_b"] = jnp.concatenate([dcw_a[3:], dcw_b[3:]], axis=1)
    g["w_up_a"] = _mm(h2, dup_a, mode="tn", out_dtype=BF16, name="dw_up_a")
    g["w_up_b"] = _mm(h2, dup_b, mode="tn", out_dtype=BF16, name="dw_up_b")
    dh2 = _mm(dup_a, w["w_up_a"], mode="nt", out_dtype=F32, name="dh2_a")
    dh2 = _mm(dup_b, w["w_up_b"], mode="nt", out_dtype=BF16, name="dh2_b", add=dh2)
    dx1, g["norm_ffn_g"] = _rms_bwd(x1, w["norm_ffn_g"], dh2, dx2, name="rms_ffn_bwd")
    dep = on_grads_ffn(g) if on_grads_ffn is not None else None

    g["w_out"] = _mm(merged, dx1, mode="tn", out_dtype=BF16, name="dw_out", dep=dep)
    dmerged = _mm(dx1, w["w_out"], mode="nt", out_dtype=BF16, name="dmerged")
    dzg, dya, dyb = _merge_bwd(dmerged, zg, ya, yb, name="merge_bwd")
    g["w_branch_a"] = _mm(a, dya, mode="tn", out_dtype=BF16, name="dw_branch_a")
    g["w_branch_b"] = _mm(b, dyb, mode="tn", out_dtype=BF16, name="dw_branch_b")
    da = _mm(dya, w["w_branch_a"], mode="nt", out_dtype=BF16, name="da")
    db = _mm(dyb, w["w_branch_b"], mode="nt", out_dtype=BF16, name="db")

    dz_uv, g["gmlp_w_s"], dbs_t, g["gmlp_ln_g"], g["gmlp_ln_b"] = _gmlp_bwd(
        z_uv, da, w["gmlp_ln_g"], w["gmlp_ln_b"], w["gmlp_w_s"], w["gmlp_b_s_t"], name="gmlp_bwd")
    g["gmlp_b_s"] = dbs_t[:, :GMLP_GROUPS].T

    db_t = db.T
    delta = _attn_delta_t(db_t, o_t, name="attn_delta")
    dq_t, dk, dv, dcum_b = _attn_bwd_t(qkv, q_t, k_t, db, db_t, lse, delta, cum_b, cum_r, name="attn_bwd")
    dcum_t = jnp.pad(dcum_b[..., 0].reshape(FOX_HEADS, s), ((0, LANES - FOX_HEADS), (0, 0)))
    df, g["b_f"] = _fox_dlogit(dcum_t, f, w["b_f"], name="fox_dlogit")
    dqkv = jnp.concatenate([dq_t.T.astype(BF16), dk, dv], axis=1)

    g["w_uv"] = _mm(h, dz_uv, mode="tn", out_dtype=BF16, name="dw_uv")
    g["w_qkv"] = _mm(h, dqkv, mode="tn", out_dtype=BF16, name="dw_qkv")
    g["w_f"] = _mm(h, df, mode="tn", out_dtype=BF16, name="dw_f")
    g["w_g"] = _mm(h, dzg, mode="tn", out_dtype=BF16, name="dw_g")
    dep = on_grads_mix(g) if on_grads_mix is not None else None
    dh = _mm(dz_uv, w["w_uv"], mode="nt", out_dtype=F32, name="dh_uv", dep=dep)
    dh = _mm(dqkv, w["w_qkv"], mode="nt", out_dtype=F32, name="dh_qkv", add=dh)
    dh = _mm(df, w["w_f"], mode="nt", out_dtype=F32, name="dh_f", add=dh)
    dh = _mm(dzg, w["w_g"], mode="nt", out_dtype=BF16, name="dh_g", add=dh)
    dx0, g["norm_mix_g"] = _rms_bwd(x, w["norm_mix_g"], dh, dx1, name="rms_mix_bwd")
    return loss, dx0, g


def _coords():
    return lax.axis_index("x"), lax.axis_index("y"), lax.axis_index("c")


def _other_chips(x, y):
    return [(1 - x, y), (x, 1 - y), (1 - x, 1 - y)]


def _remote(src, dst, send_sem, recv_sem, dev):
    return pltpu.make_async_remote_copy(src_ref=src, dst_ref=dst, send_sem=send_sem, recv_sem=recv_sem,
                                        device_id=dev, device_id_type=MESH)


_ANY = pl.BlockSpec(memory_space=pl.ANY)


def _gather_weights(halved, whole, *, name):
    nh, n = len(halved), len(halved) + len(whole)
    arrays = list(halved) + list(whole)

    def body(*refs):
        ins, outs = refs[:n], refs[n:2 * n]
        send_sems, recv_sems = refs[2 * n:]
        x, y, c = _coords()
        me, sib = 2 * x + y, (x, y, 1 - c)
        chips = _other_chips(x, y)

        def half(i, which):
            h = ins[i].shape[0] // 2
            return pl.ds(pl.multiple_of(which * h, 16), h)

        sends = []
        for i in range(n):
            src, dst = (ins[i].at[half(i, c)], outs[i].at[me, half(i, c)]) if i < nh else (ins[i], outs[i].at[me])
            for k, (cx, cy) in enumerate(chips):
                cp = _remote(src, dst, send_sems.at[i, k], recv_sems.at[i, k], (cx, cy, c))
                cp.start()
                sends.append(cp)
        for i in range(n):
            for k, (cx, cy) in enumerate(chips):
                got = outs[i].at[2 * cx + cy, half(i, c)] if i < nh else outs[i].at[2 * cx + cy]
                _remote(got, got, send_sems.at[i, k], recv_sems.at[i, k], sib).wait_recv()
                if i < nh:
                    cp = _remote(got, got, send_sems.at[i, 3 + k], recv_sems.at[i, 3 + k], sib)
                    cp.start()
                    sends.append(cp)
        for i in range(nh):
            for k, (cx, cy) in enumerate(chips):
                got = outs[i].at[2 * cx + cy, half(i, 1 - c)]
                _remote(got, got, send_sems.at[i, 3 + k], recv_sems.at[i, 3 + k], sib).wait_recv()
        for cp in sends:
            cp.wait_send()

    outs = pl.pallas_call(
        body, name=name, in_specs=[_ANY] * n, out_specs=[_ANY] * n,
        out_shape=[jax.ShapeDtypeStruct((N_CHIPS,) + a.shape, a.dtype) for a in arrays],
        scratch_shapes=[pltpu.SemaphoreType.DMA((n, 6)), pltpu.SemaphoreType.DMA((n, 6))],
        compiler_params=_params(),
    )(*arrays)
    chip = 2 * lax.axis_index("x") + lax.axis_index("y")
    return [lax.dynamic_update_index_in_dim(o, a, chip, 0) for o, a in zip(outs, arrays)]


def _pair_exchange(gs, *, name):
    n = len(gs)

    def body(*refs):
        ins, outs = refs[:n], refs[n:2 * n]
        send_sems, recv_sems = refs[2 * n:]
        x, y, c = _coords()
        copies = []
        for i in range(n):
            for j in range(N_CHIPS):
                cp = _remote(ins[i].at[j, 1 - c], outs[i].at[j], send_sems.at[i, j], recv_sems.at[i, j], (x, y, 1 - c))
                cp.start()
                copies.append(cp)
        for cp in copies:
            cp.wait()

    return pl.pallas_call(
        body, name=name, in_specs=[_ANY] * n, out_specs=[_ANY] * n,
        out_shape=[jax.ShapeDtypeStruct((N_CHIPS,) + a.shape[2:], a.dtype) for a in gs],
        scratch_shapes=[pltpu.SemaphoreType.DMA((n, N_CHIPS)), pltpu.SemaphoreType.DMA((n, N_CHIPS))],
        compiler_params=_params(),
    )(*gs)


def _chip_exchange(ss, *, name):
    n = len(ss)

    def body(*refs):
        ins, outs = refs[:n], refs[n:2 * n]
        send_sems, recv_sems = refs[2 * n:]
        x, y, c = _coords()
        me = 2 * x + y
        chips = _other_chips(x, y)
        sends = []
        for i in range(n):
            for k, (cx, cy) in enumerate(chips):
                cp = _remote(ins[i].at[2 * cx + cy], outs[i].at[me], send_sems.at[i, k], recv_sems.at[i, k], (cx, cy, c))
                cp.start()
                sends.append(cp)
        for i in range(n):
            for k, (cx, cy) in enumerate(chips):
                got = outs[i].at[2 * cx + cy]
                _remote(got, got, send_sems.at[i, k], recv_sems.at[i, k], (cx, cy, c)).wait_recv()
        for cp in sends:
            cp.wait_send()

    return pl.pallas_call(
        body, name=name, in_specs=[_ANY] * n, out_specs=[_ANY] * n,
        out_shape=[jax.ShapeDtypeStruct(a.shape, a.dtype) for a in ss],
        scratch_shapes=[pltpu.SemaphoreType.DMA((n, 3)), pltpu.SemaphoreType.DMA((n, 3))],
        compiler_params=_params(),
    )(*ss)


def _pair_share(hs, *, name):
    n = len(hs)

    def body(*refs):
        ins, outs = refs[:n], refs[n:2 * n]
        send_sems, recv_sems = refs[2 * n:]
        x, y, c = _coords()
        copies = []
        for i in range(n):
            cp = _remote(ins[i], outs[i], send_sems.at[i], recv_sems.at[i], (x, y, 1 - c))
            cp.start()
            copies.append(cp)
        for cp in copies:
            cp.wait()

    return pl.pallas_call(
        body, name=name, in_specs=[_ANY] * n, out_specs=[_ANY] * n,
        out_shape=[jax.ShapeDtypeStruct(a.shape, a.dtype) for a in hs],
        scratch_shapes=[pltpu.SemaphoreType.DMA((n,)), pltpu.SemaphoreType.DMA((n,))],
        compiler_params=_params(),
    )(*hs)


def _all_exchange(vec, *, name):
    def body(v_ref, o_ref, send_sems, recv_sems, local_sem):
        x, y, c = _coords()
        me = 4 * x + 2 * y + c
        local = pltpu.make_async_copy(v_ref, o_ref.at[me], local_sem)
        local.start()
        copies = []
        k = 0
        for dx in (0, 1):
            for dy in (0, 1):
                for dc in (0, 1):
                    if dx or dy or dc:
                        peer = (1 - x if dx else x, 1 - y if dy else y, 1 - c if dc else c)
                        cp = _remote(v_ref, o_ref.at[me], send_sems.at[k], recv_sems.at[k], peer)
                        cp.start()
                        copies.append(cp)
                        k += 1
        for cp in copies:
            cp.wait()
        local.wait()

    return pl.pallas_call(
        body, name=name, in_specs=[_ANY], out_specs=_ANY,
        out_shape=jax.ShapeDtypeStruct((8,) + vec.shape, vec.dtype),
        scratch_shapes=[pltpu.SemaphoreType.DMA((7,)), pltpu.SemaphoreType.DMA((7,)), pltpu.SemaphoreType.DMA(())],
        compiler_params=_params(),
    )(vec)


_HBM = pl.BlockSpec(memory_space=pltpu.HBM)
_SEM = pl.BlockSpec(memory_space=pltpu.SEMAPHORE)
_EFFECT = pltpu.SideEffectType.DATAFLOW_SIDE_EFFECTING


def _copies_start(srcs, lands, plan, n_copies, *, name):
    ns, n = len(srcs), len(srcs) + len(lands)

    def body(*refs):
        send_sems, recv_sems = refs[n], refs[n + 1]
        token = refs[-1]
        for k, (src, dst, dev) in enumerate(plan(refs[:ns], refs[ns:n])):
            _remote(src, dst, send_sems.at[k], recv_sems.at[k], dev).start()
        token[...] = jnp.zeros_like(token)

    arrays = list(srcs) + list(lands)
    outs = pl.pallas_call(
        body, name=name,
        out_shape=(pltpu.SemaphoreType.DMA((n_copies,)), pltpu.SemaphoreType.DMA((n_copies,)),
                   *[pltpu.HBM(a.shape, a.dtype) for a in arrays], jax.ShapeDtypeStruct((8, LANES), F32)),
        in_specs=[_HBM] * n, out_specs=(_SEM, _SEM, *[_HBM] * n, pl.BlockSpec(memory_space=pltpu.VMEM)),
        input_output_aliases={i: 2 + i for i in range(n)},
        compiler_params=pltpu.CompilerParams(has_side_effects=_EFFECT),
    )(*[pltpu.with_memory_space_constraint(a, pltpu.HBM) for a in arrays])
    return outs[0], outs[1], list(outs[2:2 + ns]), list(outs[2 + ns:2 + n]), outs[-1]


def _copies_wait(send_sems, recv_sems, srcs, lands, plan, first, after, *, name):
    ns, n = len(srcs), len(srcs) + len(lands)

    def body(*refs):
        send, recv = refs[n], refs[n + 1]
        for k, (src, dst, dev) in enumerate(plan(refs[:ns], refs[ns:n])):
            cp = _remote(src, dst, send.at[first + k], recv.at[first + k], dev)
            cp.wait_send()
            cp.wait_recv()

    arrays = list(srcs) + list(lands)
    outs = pl.pallas_call(
        body, name=name, out_shape=tuple(pltpu.HBM(a.shape, a.dtype) for a in arrays),
        in_specs=[_HBM] * n + [_SEM, _SEM] + [_ANY] * len(after), out_specs=tuple([_HBM] * n),
        input_output_aliases={i: i for i in range(n)},
        compiler_params=pltpu.CompilerParams(has_side_effects=_EFFECT),
    )(*arrays, send_sems, recv_sems, *after)
    return list(outs[:ns]), list(outs[ns:])


def _gather_plan(halved):
    def plan(srcs, lands):
        x, y, c = _coords()
        me = 2 * x + y
        out = []
        for i, (src, land) in enumerate(zip(srcs, lands)):
            if halved[i]:
                h = src.shape[0] // 2
                rows = pl.ds(pl.multiple_of(c * h, 16), h)
                src, dst = src.at[rows], land.at[me, rows]
            else:
                dst = land.at[me]
            out += [(src, dst, (cx, cy, c)) for cx, cy in _other_chips(x, y)]
        return out
    return plan


def _forward_halves(lands, *, name):
    n = len(lands)

    def body(*refs):
        ins, outs = refs[:n], refs[n:2 * n]
        send_sems, recv_sems = refs[2 * n:]
        x, y, c = _coords()
        copies = []
        for i in range(n):
            h = ins[i].shape[1] // 2
            rows = pl.ds(pl.multiple_of(c * h, 16), h)
            for k, (cx, cy) in enumerate(_other_chips(x, y)):
                cp = _remote(ins[i].at[2 * cx + cy, rows], outs[i].at[2 * cx + cy, rows],
                             send_sems.at[i, k], recv_sems.at[i, k], (x, y, 1 - c))
                cp.start()
                copies.append(cp)
        for cp in copies:
            cp.wait()

    return pl.pallas_call(
        body, name=name, in_specs=[_ANY] * n, out_specs=[_ANY] * n,
        out_shape=[jax.ShapeDtypeStruct(a.shape, a.dtype) for a in lands],
        input_output_aliases={i: i for i in range(n)},
        scratch_shapes=[pltpu.SemaphoreType.DMA((n, 3)), pltpu.SemaphoreType.DMA((n, 3))],
        compiler_params=_params(),
    )(*lands)


def _chip_plan(srcs, lands):
    x, y, c = _coords()
    me = 2 * x + y
    out = []
    for src, land in zip(srcs, lands):
        out += [(src.at[2 * cx + cy], land.at[me], (cx, cy, c)) for cx, cy in _other_chips(x, y)]
    return out


def _rtile(r, pref, mult):
    t = (min(r, pref) // mult) * mult
    while t >= mult:
        if r % t == 0:
            return t
        t -= mult
    return r


def _pair_add(g, recv, core, *, name):
    _, _, r2, cols = g.shape
    tr = _rtile(r2, 256, 16)

    def body(c_ref, g_ref, r_ref, o_ref):
        o_ref[...] = (g_ref[...].astype(F32) + r_ref[...].astype(F32)).astype(o_ref.dtype)

    blk = pl.BlockSpec((None, tr, cols), lambda j, i, c_ref: (j, i, 0))
    return pl.pallas_call(
        body, name=name,
        grid_spec=pltpu.PrefetchScalarGridSpec(
            num_scalar_prefetch=1, grid=(N_CHIPS, r2 // tr),
            in_specs=[pl.BlockSpec((None, None, tr, cols), lambda j, i, c_ref: (j, c_ref[0], i, 0)), blk],
            out_specs=blk),
        out_shape=jax.ShapeDtypeStruct(recv.shape, recv.dtype), compiler_params=_params(),
    )(core, g, recv)


def _sum_slots(a, out_dtype, *, name):
    n, r, cols = a.shape
    tr = _rtile(r, 256, 16)

    def body(a_ref, o_ref):
        acc = a_ref[0].astype(F32)
        for j in range(1, n):
            acc = acc + a_ref[j].astype(F32)
        o_ref[...] = acc.astype(o_ref.dtype)

    return pl.pallas_call(
        body, name=name, grid=(r // tr,),
        in_specs=[pl.BlockSpec((n, tr, cols), lambda i: (0, i, 0))],
        out_specs=pl.BlockSpec((tr, cols), lambda i: (i, 0)),
        out_shape=jax.ShapeDtypeStruct((r, cols), out_dtype), compiler_params=_params(),
    )(a)


def _chip_sum(own, recv, chip, *, name):
    _, r2, cols = own.shape
    tr = _rtile(r2, 256, 16)

    def body(chip_ref, own_ref, *rest):
        o_ref = rest[-1]
        acc = None
        for j in range(N_CHIPS):
            term = jnp.where(chip_ref[0] == j, own_ref[...], rest[j][...]).astype(F32)
            acc = term if acc is None else acc + term
        o_ref[...] = acc

    def slot(j):
        return pl.BlockSpec((None, tr, cols),
                            lambda i, chip_ref: (jnp.where(chip_ref[0] == j, (j + 1) % N_CHIPS, j), i, 0))

    return pl.pallas_call(
        body, name=name,
        grid_spec=pltpu.PrefetchScalarGridSpec(
            num_scalar_prefetch=1, grid=(r2 // tr,),
            in_specs=[pl.BlockSpec((None, tr, cols), lambda i, chip_ref: (chip_ref[0], i, 0))]
                     + [slot(j) for j in range(N_CHIPS)],
            out_specs=pl.BlockSpec((tr, cols), lambda i, chip_ref: (i, 0))),
        out_shape=jax.ShapeDtypeStruct((r2, cols), F32), compiler_params=_params(),
    )(chip, own, *([recv] * N_CHIPS))


def _adam_update(w, gv, m, v):
    c1 = 1.0 / (1.0 - ADAM_B1 ** ADAM_STEP)
    c2 = 1.0 / (1.0 - ADAM_B2 ** ADAM_STEP)
    nm = ADAM_B1 * m + (1.0 - ADAM_B1) * gv
    nv = ADAM_B2 * v + (1.0 - ADAM_B2) * gv * gv
    return -ADAM_LR * ((nm * c1) / (jnp.sqrt(nv * c2) + ADAM_EPS) + ADAM_WD * w), nm, nv


def _adamw_halves(w, g_mine, g_other, m, v, core, *, name):
    r, cols = w.shape
    r2 = r // 2
    tr = _rtile(r2, 256, 8)
    nt = r2 // tr

    def body(core_ref, w_ref, gm_ref, go_ref, m_ref, v_ref, g_ref, d_ref, nm_ref, nv_ref):
        gv = jnp.where(pl.program_id(0) == core_ref[0], gm_ref[...], go_ref[...])
        g_ref[...] = gv
        d_ref[...], nm_ref[...], nv_ref[...] = _adam_update(w_ref[...], gv, m_ref[...], v_ref[...])

    full = pl.BlockSpec((tr, cols), lambda hf, i, core_ref: (hf * nt + i, 0))
    half = pl.BlockSpec((tr, cols), lambda hf, i, core_ref: (i, 0))
    shape = jax.ShapeDtypeStruct((r, cols), F32)
    return pl.pallas_call(
        body, name=name,
        grid_spec=pltpu.PrefetchScalarGridSpec(
            num_scalar_prefetch=1, grid=(2, nt), in_specs=[full, half, half, full, full], out_specs=[full] * 4),
        out_shape=[shape] * 4, compiler_params=_params(),
    )(core, w, g_mine, g_other, m, v)


def _adamw(w, g, m, v, *, name, rows=256):
    r, cols = w.shape
    tr = _rtile(r, rows, 8)

    def body(w_ref, g_ref, m_ref, v_ref, d_ref, nm_ref, nv_ref):
        d_ref[...], nm_ref[...], nv_ref[...] = _adam_update(w_ref[...], g_ref[...], m_ref[...], v_ref[...])

    blk = pl.BlockSpec((tr, cols), lambda i: (i, 0))
    shape = jax.ShapeDtypeStruct((r, cols), F32)
    return pl.pallas_call(
        body, name=name, grid=(r // tr,), in_specs=[blk] * 4, out_specs=[blk] * 3,
        out_shape=[shape] * 3, compiler_params=_params(),
    )(w, g, m, v)


_BIG = (("w_in", 1), ("w_branch_a", 0), ("w_branch_b", 0), ("w_out", 0), ("w_up", 1), ("w_down", 0),
        ("w_ple", 1), ("w_ple_gate", 0))
_SMALL = ("norm_mix_g", "b_f", "gmlp_ln_g", "gmlp_ln_b", "gmlp_w_s", "gmlp_b_s", "norm_ffn_g", "conv_b",
          "norm_ple_g", "norm_final_g")
_WEIGHTS = ("norm_mix_g", "w_in", "b_f", "gmlp_ln_g", "gmlp_ln_b", "gmlp_w_s", "gmlp_b_s", "w_branch_a",
            "w_branch_b", "w_out", "norm_ffn_g", "w_up", "conv_w", "conv_b", "w_down", "norm_ple_g", "w_ple",
            "w_ple_gate", "norm_final_g")
_PACK_ROWS = 8


def _pack(arrays):
    parts = []
    for a in arrays:
        flat = a.reshape(-1)
        unit = _PACK_ROWS * LANES
        flat = jnp.pad(flat, (0, (-flat.shape[0]) % unit))
        parts.append(flat.reshape(-1, LANES))
    return jnp.concatenate(parts, axis=0)


def _unpack(packed, shapes):
    out, row = [], 0
    for shp in shapes:
        size = math.prod(shp)
        rows = -(-size // (_PACK_ROWS * LANES)) * _PACK_ROWS
        out.append(packed[row:row + rows].reshape(-1)[:size].reshape(shp))
        row += rows
    return out


def _assemble(gathered, axis):
    n, r, cols = gathered.shape
    if axis == 0:
        return gathered.reshape(n * r, cols)
    return gathered.transpose(1, 0, 2).reshape(r, n * cols)


def _to_chunks(full, axis):
    if axis == 0:
        r, cols = full.shape[0] // N_CHIPS, full.shape[1]
        chunks = full.reshape(N_CHIPS, r, cols)
    else:
        r, cols = full.shape[0], full.shape[1] // N_CHIPS
        chunks = full.reshape(r, N_CHIPS, cols).transpose(1, 0, 2)
    return chunks.reshape(N_CHIPS, 2, r // 2, cols)


def kernel(x, p, norm_mix_g, w_in, b_f, gmlp_ln_g, gmlp_ln_b, gmlp_w_s, gmlp_b_s, w_branch_a, w_branch_b, w_out, norm_ffn_g, w_up, conv_w, conv_b, w_down, norm_ple_g, w_ple, w_ple_gate, norm_final_g, loss_target, m_norm_mix_g, m_w_in, m_b_f, m_gmlp_ln_g, m_gmlp_ln_b, m_gmlp_w_s, m_gmlp_b_s, m_w_branch_a, m_w_branch_b, m_w_out, m_norm_ffn_g, m_w_up, m_conv_w, m_conv_b, m_w_down, m_norm_ple_g, m_w_ple, m_w_ple_gate, m_norm_final_g, v_norm_mix_g, v_w_in, v_b_f, v_gmlp_ln_g, v_gmlp_ln_b, v_gmlp_w_s, v_gmlp_b_s, v_w_branch_a, v_w_branch_b, v_w_out, v_norm_ffn_g, v_w_up, v_conv_w, v_conv_b, v_w_down, v_norm_ple_g, v_w_ple, v_w_ple_gate, v_norm_final_g):
    args = dict(locals())
    wt = {n: args[n] for n in _WEIGHTS}
    mom = {n: args["m_" + n] for n in _WEIGHTS}
    var = {n: args["v_" + n] for n in _WEIGHTS}
    chip = 2 * lax.axis_index("x") + lax.axis_index("y")
    core = lax.axis_index("c").astype(jnp.int32).reshape(1)

    chip1 = chip.astype(jnp.int32).reshape(1)
    axis_of = dict(_BIG)
    names = [n for n, _ in _BIG]
    put_mine = lambda land, mine: lax.dynamic_update_index_in_dim(land, mine, chip, 0)

    shards = [wt[n][0].astype(BF16) for n in names] + [conv_w[0]]
    halved = [True] * len(names) + [False]
    lands = [lax.empty((N_CHIPS,) + a.shape, a.dtype) for a in shards]
    send_sems, recv_sems, srcs, lands, _ = _copies_start(shards, lands, _gather_plan(halved), 3 * len(shards),
                                                         name="gather_start")
    o1 = 2 * GMLP_WIDTH
    o2 = o1 + 3 * FOX_WIDTH
    o3 = o2 + FOX_HEADS
    fpad = ((0, 0), (0, LANES - FOX_HEADS))
    w = {
        "conv_b": conv_b, "norm_mix_g": norm_mix_g, "norm_ffn_g": norm_ffn_g, "norm_ple_g": norm_ple_g,
        "norm_final_g": norm_final_g.reshape(1, D_MODEL), "b_f": jnp.pad(b_f, fpad),
        "gmlp_ln_g": gmlp_ln_g, "gmlp_ln_b": gmlp_ln_b, "gmlp_w_s": gmlp_w_s[0],
        "gmlp_b_s_t": jnp.pad(gmlp_b_s[0].T, ((0, 0), (0, LANES - GMLP_GROUPS))),
    }

    def get_w_in(after):
        _, got = _copies_wait(send_sems, recv_sems, srcs[:1], lands[:1], _gather_plan(halved[:1]), 0, [after],
                              name="gather_wait_in")
        got = _forward_halves(got, name="gather_forward_in")
        full = _assemble(put_mine(got[0], shards[0]), 1)
        return {"w_uv": full[:, :o1], "w_qkv": full[:, o1:o2], "w_f": jnp.pad(full[:, o2:o3], fpad),
                "w_g": full[:, o3:]}

    def get_w_rest(after):
        _, got = _copies_wait(send_sems, recv_sems, srcs[1:], lands[1:], _gather_plan(halved[1:]), 3, [after],
                              name="gather_wait_rest")
        got = list(_forward_halves(got[:-1], name="gather_forward_rest")) + got[-1:]
        full = {n: _assemble(put_mine(got[i], shards[1 + i]), axis_of[n]) for i, n in enumerate(names[1:])}
        return {"w_branch_a": full["w_branch_a"], "w_branch_b": full["w_branch_b"], "w_out": full["w_out"],
                "w_up_a": full["w_up"][:, :D_FF], "w_up_b": full["w_up"][:, D_FF:], "w_down": full["w_down"],
                "w_ple": full["w_ple"], "w_ple_gate": full["w_ple_gate"],
                "conv_w": _assemble(put_mine(got[-1], shards[-1]), 1)}

    grads, delta, new_m, new_v = {}, {}, {}, {}
    pending = {}

    def reduce_start(group, gfull, tag):
        chunks = [_to_chunks(gfull[n], axis_of[n]) for n in group]
        from_sibling = _pair_exchange(chunks, name="grad_pair_exchange_" + tag)
        pair_sums = [_pair_add(chunks[i], from_sibling[i], core, name="grad_pair_add_" + n) for i, n in enumerate(group)]
        empty = [lax.empty(a.shape, a.dtype) for a in pair_sums]
        ssem, rsem, own, recv, token = _copies_start(pair_sums, empty, _chip_plan, 3 * len(group),
                                                     name="grad_chip_start_" + tag)
        pending[tag] = (ssem, rsem, own, recv)
        return token

    def reduce_finish(group, tag, after):
        ssem, rsem, own, recv = pending[tag]
        own, recv = _copies_wait(ssem, rsem, own, recv, _chip_plan, 0, after, name="grad_chip_wait_" + tag)
        halves = [_chip_sum(own[i], recv[i], chip1, name="grad_chip_sum_" + n) for i, n in enumerate(group)]
        other_halves = _pair_share(halves, name="grad_pair_share_" + tag)
        for i, n in enumerate(group):
            shp = wt[n].shape
            outs = _adamw_halves(wt[n].reshape(shp[-2:]), halves[i], other_halves[i], mom[n].reshape(shp[-2:]),
                                 var[n].reshape(shp[-2:]), core, name="adamw_" + n)
            grads[n], delta[n], new_m[n], new_v[n] = (o.reshape(shp) for o in outs)
        return new_v[group[-1]]

    ffn_group = ("w_up", "w_down", "w_ple", "w_ple_gate")
    mix_group = ("w_in", "w_branch_a", "w_branch_b", "w_out")

    def on_grads_ffn(g):
        gfull = dict(g)
        gfull["w_up"] = jnp.concatenate([g["w_up_a"], g["w_up_b"]], axis=1)
        return reduce_start(ffn_group, gfull, "ffn")

    def on_grads_mix(g):
        gfull = dict(g)
        gfull["w_in"] = jnp.concatenate([g["w_uv"], g["w_qkv"], g["w_f"][:, :FOX_HEADS], g["w_g"]], axis=1)
        token = reduce_start(mix_group, gfull, "mix")
        pending["ffn_done"] = reduce_finish(ffn_group, "ffn", [token])
        return token

    loss, grad_x, g = _device_step(x[0], p[0, 0], loss_target[0], w, get_w_in, get_w_rest, on_grads_ffn, on_grads_mix)

    small_g = [g[n] if n != "b_f" else g[n][:, :FOX_HEADS] for n in _SMALL]
    vec = _pack(small_g + [g["conv_w"]])
    vec = _sum_slots(_all_exchange(vec, name="small_exchange"), F32, name="small_sum")
    small_rows = _pack([wt[n] for n in _SMALL]).shape[0]
    for n, a in zip(_SMALL, _unpack(vec[:small_rows], [wt[n].shape for n in _SMALL])):
        grads[n] = a
    conv_w_grad = _unpack(vec[small_rows:], [(3, 2 * D_FF)])[0]
    grads["conv_w"] = lax.dynamic_slice_in_dim(conv_w_grad, chip * conv_w.shape[2], conv_w.shape[2], axis=1).reshape(conv_w.shape)

    reduce_finish(mix_group, "mix", [grad_x, pending["ffn_done"], vec])
    shp = conv_w.shape
    outs = _adamw(conv_w.reshape(shp[-2:]), grads["conv_w"].reshape(shp[-2:]), m_conv_w.reshape(shp[-2:]),
                  v_conv_w.reshape(shp[-2:]), name="adamw_conv_w")
    delta["conv_w"], new_m["conv_w"], new_v["conv_w"] = (o.reshape(shp) for o in outs)
    outs = _adamw(_pack([wt[n] for n in _SMALL]), vec[:small_rows], _pack([mom[n] for n in _SMALL]),
                  _pack([var[n] for n in _SMALL]), name="adamw_small", rows=2048)
    for d, o in zip((delta, new_m, new_v), outs):
        for n, a in zip(_SMALL, _unpack(o, [wt[n].shape for n in _SMALL])):
            d[n] = a

    total_loss = lax.psum(loss[0, 0], ("x", "y", "c"))
    return (total_loss, grad_x.reshape(x.shape), *[grads[n] for n in _WEIGHTS], *[delta[n] for n in _WEIGHTS],
            *[new_m[n] for n in _WEIGHTS], *[new_v[n] for n in _WEIGHTS])
```

```python
import functools
import math

import jax
import jax.numpy as jnp
from jax import lax
from jax.experimental import pallas as pl
from jax.experimental.pallas import tpu as pltpu

F32 = jnp.float32
BF16 = jnp.bfloat16

D_MODEL = 1024
EPS = 1e-6
CHUNK = 64
GMLP_GROUPS = 8
GMLP_BLOCK = 128
GMLP_WIDTH = 1024
FOX_HEADS = 16
FOX_HEAD_DIM = 64
FOX_WIDTH = 1024
HEAD_PAIRS = FOX_HEADS // 2
ATT_BLOCK = 128
D_FF = 2816
PLE_DIM = 256
LANES = 128
N_CHIPS = 4

ADAM_LR = 0.001
ADAM_B1 = 0.9
ADAM_B2 = 0.999
ADAM_EPS = 1e-08
ADAM_WD = 0.01
ADAM_STEP = 10

VMEM_LIMIT = 56 * 1024 * 1024
MESH = pl.DeviceIdType.MESH

_NN = (((1,), (0,)), ((), ()))
_NT = (((1,), (1,)), ((), ()))
_TN = (((0,), (0,)), ((), ()))


def _params(**kw):
    return pltpu.CompilerParams(vmem_limit_bytes=VMEM_LIMIT, **kw)


def _tile(dim, pref):
    if dim <= pref:
        return dim
    t = (pref // LANES) * LANES
    while t >= LANES:
        if dim % t == 0:
            return t
        t -= LANES
    return dim


def _dot(a, b, dn):
    return lax.dot_general(a.astype(BF16), b.astype(BF16), dn, preferred_element_type=F32)


def _gelu(x):
    c = math.sqrt(2.0 / math.pi)
    t = jnp.tanh(c * (x + 0.044715 * x * x * x))
    return 0.5 * x * (1.0 + t)


def _gelu_and_grad(x):
    c = math.sqrt(2.0 / math.pi)
    x2 = x * x
    t = jnp.tanh(c * (x + 0.044715 * x2 * x))
    g = 0.5 * x * (1.0 + t)
    dg = 0.5 * (1.0 + t) + 0.5 * x * (1.0 - t * t) * c * (1.0 + 3.0 * 0.044715 * x2)
    return g, dg


def _sigmoid(x):
    return 1.0 / (1.0 + jnp.exp(-x))


def _mm(a, b, *, mode, out_dtype, name, add=None, tm=512, tn=512, dep=None):
    if mode == "nn":
        m, k = a.shape
        k2, n = b.shape
    elif mode == "nt":
        m, k = a.shape
        n, k2 = b.shape
    else:
        k, m = a.shape
        k2, n = b.shape
    assert k == k2, (name, a.shape, b.shape)
    tm = _tile(m, tm)
    tn = _tile(n, tn)
    dn = {"nn": _NN, "nt": _NT, "tn": _TN}[mode]

    def body(a_ref, b_ref, *rest):
        o_ref = rest[-1]
        acc = _dot(a_ref[...], b_ref[...], dn)
        if add is not None:
            acc = acc + rest[0][...].astype(F32)
        o_ref[...] = acc.astype(o_ref.dtype)

    a_spec = pl.BlockSpec((k, tm), lambda i, j: (0, i)) if mode == "tn" else pl.BlockSpec((tm, k), lambda i, j: (i, 0))
    b_spec = pl.BlockSpec((tn, k), lambda i, j: (j, 0)) if mode == "nt" else pl.BlockSpec((k, tn), lambda i, j: (0, j))
    o_spec = pl.BlockSpec((tm, tn), lambda i, j: (i, j))
    in_specs = [a_spec, b_spec]
    args = [a, b]
    if add is not None:
        in_specs.append(o_spec)
        args.append(add)
    if dep is not None:
        in_specs.append(pl.BlockSpec(memory_space=pl.ANY))
        args.append(dep)
    return pl.pallas_call(
        body, name=name, grid=(m // tm, n // tn), in_specs=in_specs, out_specs=o_spec,
        out_shape=jax.ShapeDtypeStruct((m, n), out_dtype), compiler_params=_params(),
    )(*args)


def _rms_fwd(x, g, *, name, tm=256):
    s, d = x.shape
    tm = _tile(s, tm)

    def body(x_ref, g_ref, h_ref):
        xv = x_ref[...]
        r = lax.rsqrt(jnp.mean(xv * xv, axis=-1, keepdims=True) + EPS)
        h_ref[...] = (xv * r * g_ref[...]).astype(h_ref.dtype)

    return pl.pallas_call(
        body, name=name, grid=(s // tm,),
        in_specs=[pl.BlockSpec((tm, d), lambda i: (i, 0)), pl.BlockSpec((1, d), lambda i: (0, 0))],
        out_specs=pl.BlockSpec((tm, d), lambda i: (i, 0)),
        out_shape=jax.ShapeDtypeStruct((s, d), BF16), compiler_params=_params(),
    )(x, g)


def _rms_bwd(x, g, dh, dres, *, name, tm=256):
    s, d = x.shape
    tm = _tile(s, tm)

    def body(x_ref, g_ref, dh_ref, dres_ref, dx_ref, dxb_ref, dg_ref):
        xv = x_ref[...]
        r = lax.rsqrt(jnp.mean(xv * xv, axis=-1, keepdims=True) + EPS)
        xhat = xv * r
        dhv = dh_ref[...].astype(F32)
        dyg = dhv * g_ref[...]
        dx = dres_ref[...] + r * (dyg - xhat * jnp.mean(dyg * xhat, axis=-1, keepdims=True))
        dx_ref[...] = dx
        dxb_ref[...] = dx.astype(dxb_ref.dtype)

        @pl.when(pl.program_id(0) == 0)
        def _():
            dg_ref[...] = jnp.zeros_like(dg_ref)

        dg_ref[...] += jnp.sum(dhv * xhat, axis=0, keepdims=True)

    row = pl.BlockSpec((tm, d), lambda i: (i, 0))
    vec = pl.BlockSpec((1, d), lambda i: (0, 0))
    return pl.pallas_call(
        body, name=name, grid=(s // tm,), in_specs=[row, vec, row, row], out_specs=[row, row, vec],
        out_shape=[jax.ShapeDtypeStruct((s, d), F32), jax.ShapeDtypeStruct((s, d), BF16),
                   jax.ShapeDtypeStruct((1, d), F32)],
        compiler_params=_params(),
    )(x, g, dh, dres)


def _gmlp_mask():
    t = lax.broadcasted_iota(jnp.int32, (GMLP_BLOCK, GMLP_BLOCK), 0)
    s_ = lax.broadcasted_iota(jnp.int32, (GMLP_BLOCK, GMLP_BLOCK), 1)
    return (s_ // CHUNK) <= (t // CHUNK)


def _gmlp_norm(zv, ln_g, ln_b):
    vv, dvv = _gelu_and_grad(zv)
    mu = jnp.mean(vv, axis=-1, keepdims=True)
    xc = vv - mu
    rstd = lax.rsqrt(jnp.mean(xc * xc, axis=-1, keepdims=True) + EPS)
    vhat = xc * rstd
    return vhat * ln_g + ln_b, vhat, rstd, dvv


def _gmlp_fwd(z_uv, ln_g, ln_b, w_s, b_s_t, *, name):
    s = z_uv.shape[0]
    w = GMLP_WIDTH
    gd = w // GMLP_GROUPS

    def body(z_ref, lg_ref, lb_ref, ws_ref, bs_ref, a_ref):
        u = _gelu(z_ref[:, :w].astype(F32))
        vn, _, _, _ = _gmlp_norm(z_ref[:, w:].astype(F32), lg_ref[...], lb_ref[...])
        mask = _gmlp_mask()
        for g in range(GMLP_GROUPS):
            wm = jnp.where(mask, ws_ref[g], 0.0)
            mixed = _dot(wm, vn[:, g * gd:(g + 1) * gd], _NN) + bs_ref[:, g:g + 1]
            a_ref[:, g * gd:(g + 1) * gd] = (u[:, g * gd:(g + 1) * gd] * mixed).astype(a_ref.dtype)

    full = lambda shape: pl.BlockSpec(shape, lambda i: (0,) * len(shape))
    return pl.pallas_call(
        body, name=name, grid=(s // GMLP_BLOCK,),
        in_specs=[pl.BlockSpec((GMLP_BLOCK, 2 * w), lambda i: (i, 0)), full((1, w)), full((1, w)),
                  full((GMLP_GROUPS, GMLP_BLOCK, GMLP_BLOCK)), full((GMLP_BLOCK, LANES))],
        out_specs=pl.BlockSpec((GMLP_BLOCK, w), lambda i: (i, 0)),
        out_shape=jax.ShapeDtypeStruct((s, w), BF16), compiler_params=_params(),
    )(z_uv, ln_g, ln_b, w_s, b_s_t)


def _gmlp_bwd(z_uv, da, ln_g, ln_b, w_s, b_s_t, *, name):
    s = z_uv.shape[0]
    w = GMLP_WIDTH
    gd = w // GMLP_GROUPS

    def body(z_ref, da_ref, lg_ref, lb_ref, ws_ref, bs_ref, dz_ref, dws_ref, dbs_ref, dlg_ref, dlb_ref):
        @pl.when(pl.program_id(0) == 0)
        def _():
            dws_ref[...] = jnp.zeros_like(dws_ref)
            dbs_ref[...] = jnp.zeros_like(dbs_ref)
            dlg_ref[...] = jnp.zeros_like(dlg_ref)
            dlb_ref[...] = jnp.zeros_like(dlb_ref)

        u, du_dz = _gelu_and_grad(z_ref[:, :w].astype(F32))
        lg = lg_ref[...]
        vn, vhat, rstd, dvv_dz = _gmlp_norm(z_ref[:, w:].astype(F32), lg, lb_ref[...])
        dav = da_ref[...].astype(F32)
        mask = _gmlp_mask()
        lane = lax.broadcasted_iota(jnp.int32, (GMLP_BLOCK, LANES), 1)
        dvn_parts = []
        dbs = jnp.zeros((GMLP_BLOCK, LANES), F32)
        for g in range(GMLP_GROUPS):
            sl = slice(g * gd, (g + 1) * gd)
            wm = jnp.where(mask, ws_ref[g], 0.0)
            vn_g = vn[:, sl]
            mixed = _dot(wm, vn_g, _NN) + bs_ref[:, g:g + 1]
            dmixed = dav[:, sl] * u[:, sl]
            dz_ref[:, sl] = (dav[:, sl] * mixed * du_dz[:, sl]).astype(dz_ref.dtype)
            dvn_parts.append(_dot(wm, dmixed, _TN))
            dws_ref[g] += jnp.where(mask, _dot(dmixed, vn_g, _NT), 0.0)
            dbs = dbs + jnp.where(lane == g, jnp.sum(dmixed, axis=-1, keepdims=True), 0.0)
        dbs_ref[...] += dbs
        dvn = jnp.concatenate(dvn_parts, axis=-1)
        dlg_ref[...] += jnp.sum(dvn * vhat, axis=0, keepdims=True)
        dlb_ref[...] += jnp.sum(dvn, axis=0, keepdims=True)
        dyg = dvn * lg
        dvv = rstd * (dyg - jnp.mean(dyg, axis=-1, keepdims=True)
                      - vhat * jnp.mean(dyg * vhat, axis=-1, keepdims=True))
        dz_ref[:, w:] = (dvv * dvv_dz).astype(dz_ref.dtype)

    full = lambda shape: pl.BlockSpec(shape, lambda i: (0,) * len(shape))
    return pl.pallas_call(
        body, name=name, grid=(s // GMLP_BLOCK,),
        in_specs=[pl.BlockSpec((GMLP_BLOCK, 2 * w), lambda i: (i, 0)),
                  pl.BlockSpec((GMLP_BLOCK, w), lambda i: (i, 0)), full((1, w)), full((1, w)),
                  full((GMLP_GROUPS, GMLP_BLOCK, GMLP_BLOCK)), full((GMLP_BLOCK, LANES))],
        out_specs=[pl.BlockSpec((GMLP_BLOCK, 2 * w), lambda i: (i, 0)),
                   full((GMLP_GROUPS, GMLP_BLOCK, GMLP_BLOCK)), full((GMLP_BLOCK, LANES)),
                   full((1, w)), full((1, w))],
        out_shape=[jax.ShapeDtypeStruct((s, 2 * w), BF16),
                   jax.ShapeDtypeStruct((GMLP_GROUPS, GMLP_BLOCK, GMLP_BLOCK), F32),
                   jax.ShapeDtypeStruct((GMLP_BLOCK, LANES), F32),
                   jax.ShapeDtypeStruct((1, w), F32), jax.ShapeDtypeStruct((1, w), F32)],
        compiler_params=_params(),
    )(z_uv, da, ln_g, ln_b, w_s, b_s_t)


def _tri(lower):
    r = lax.broadcasted_iota(jnp.int32, (ATT_BLOCK, ATT_BLOCK), 0)
    c = lax.broadcasted_iota(jnp.int32, (ATT_BLOCK, ATT_BLOCK), 1)
    return jnp.where((c <= r) if lower else (c >= r), 1.0, 0.0).astype(F32)


def _log_sigmoid(x):
    return jnp.minimum(x, 0.0) - jnp.log(1.0 + jnp.exp(-jnp.abs(x)))


def _fox_cum(f, b_f, *, name):
    s = f.shape[0]
    nb = s // ATT_BLOCK

    def body(f_ref, b_ref, cb_ref, ct_ref, carry):
        @pl.when(pl.program_id(0) == 0)
        def _():
            carry[...] = jnp.zeros_like(carry)

        lf = _log_sigmoid(f_ref[...] + b_ref[...])
        cum = lax.dot_general(_tri(True), lf, _NN, precision=lax.Precision.HIGHEST,
                              preferred_element_type=F32) + carry[...]
        carry[...] = cum[ATT_BLOCK - 1:ATT_BLOCK, :]
        for h in range(FOX_HEADS):
            cb_ref[h] = jnp.broadcast_to(cum[:, h:h + 1], (ATT_BLOCK, LANES))
        ct_ref[...] = cum.T

    return pl.pallas_call(
        body, name=name, grid=(nb,),
        in_specs=[pl.BlockSpec((ATT_BLOCK, LANES), lambda i: (i, 0)), pl.BlockSpec((1, LANES), lambda i: (0, 0))],
        out_specs=[pl.BlockSpec((FOX_HEADS, ATT_BLOCK, LANES), lambda i: (0, i, 0)),
                   pl.BlockSpec((LANES, ATT_BLOCK), lambda i: (0, i))],
        out_shape=[jax.ShapeDtypeStruct((FOX_HEADS, s, LANES), F32), jax.ShapeDtypeStruct((LANES, s), F32)],
        scratch_shapes=[pltpu.VMEM((1, LANES), F32)], compiler_params=_params(),
    )(f, b_f)


def _fox_dlogit(dcum_t, f, b_f, *, name):
    s = f.shape[0]
    nb = s // ATT_BLOCK

    def body(dc_ref, f_ref, b_ref, df_ref, db_ref, carry):
        @pl.when(pl.program_id(0) == 0)
        def _():
            carry[...] = jnp.zeros_like(carry)
            db_ref[...] = jnp.zeros_like(db_ref)

        d = dc_ref[...].T
        dlog = lax.dot_general(_tri(False), d, _NN, precision=lax.Precision.HIGHEST,
                               preferred_element_type=F32) + carry[...]
        carry[...] = dlog[0:1, :]
        df = dlog * (1.0 - _sigmoid(f_ref[...] + b_ref[...]))
        df_ref[...] = df
        db_ref[...] += jnp.sum(df, axis=0, keepdims=True)

    rev = lambda i: nb - 1 - i
    return pl.pallas_call(
        body, name=name, grid=(nb,),
        in_specs=[pl.BlockSpec((LANES, ATT_BLOCK), lambda i: (0, rev(i))),
                  pl.BlockSpec((ATT_BLOCK, LANES), lambda i: (rev(i), 0)),
                  pl.BlockSpec((1, LANES), lambda i: (0, 0))],
        out_specs=[pl.BlockSpec((ATT_BLOCK, LANES), lambda i: (rev(i), 0)),
                   pl.BlockSpec((1, LANES), lambda i: (0, 0))],
        out_shape=[jax.ShapeDtypeStruct((s, LANES), F32), jax.ShapeDtypeStruct((1, LANES), F32)],
        scratch_shapes=[pltpu.VMEM((1, LANES), F32)], compiler_params=_params(),
    )(dcum_t, f, b_f)


def _causal(qi, ki):
    r = lax.broadcasted_iota(jnp.int32, (ATT_BLOCK, ATT_BLOCK), 0) + qi * ATT_BLOCK
    c = lax.broadcasted_iota(jnp.int32, (ATT_BLOCK, ATT_BLOCK), 1) + ki * ATT_BLOCK
    return c <= r


def _head_mask():
    return lax.broadcasted_iota(jnp.int32, (1, LANES), 1) < FOX_HEAD_DIM


def _attn_fwd(qkv, cum_b, cum_r, *, name):
    s = qkv.shape[0]
    nq = s // ATT_BLOCK
    scale = FOX_HEAD_DIM ** -0.5
    npair = HEAD_PAIRS

    def body(q_ref, k_ref, v_ref, cq_ref, ck_ref, o_ref, l_ref):
        qi = pl.program_id(1)
        m0 = _head_mask()
        q2 = q_ref[...]
        zero = jnp.zeros_like(q2)
        qs = (jnp.where(m0, q2, zero), jnp.where(m0, zero, q2))
        cqs = (cq_ref[0], cq_ref[1])

        def step(ki, carry, masked):
            off = pl.multiple_of(ki * ATT_BLOCK, ATT_BLOCK)
            k2 = k_ref[pl.ds(off, ATT_BLOCK), :]
            v2 = v_ref[pl.ds(off, ATT_BLOCK), :]
            out = []
            for hh in range(2):
                m, l, acc = carry[hh]
                sc = _dot(qs[hh], k2, _NT) * scale + (cqs[hh] - ck_ref[hh:hh + 1, pl.ds(off, ATT_BLOCK)])
                if masked:
                    sc = jnp.where(_causal(qi, ki), sc, -1e30)
                m_new = jnp.maximum(m, jnp.max(sc, axis=-1, keepdims=True))
                alpha = jnp.exp(m - m_new)
                p = jnp.exp(sc - m_new)
                l = alpha * l + jnp.sum(p, axis=-1, keepdims=True)
                acc = alpha * acc + _dot(p, v2, _NN)
                out.append((m_new, l, acc))
            return tuple(out)

        init = tuple((jnp.full((ATT_BLOCK, 1), -1e30, F32), jnp.zeros((ATT_BLOCK, 1), F32),
                      jnp.zeros((ATT_BLOCK, LANES), F32)) for _ in range(2))
        carry = lax.fori_loop(0, qi, lambda ki, c: step(ki, c, False), init)
        (ma, la, acca), (mb, lb, accb) = step(qi, carry, True)
        o_ref[...] = jnp.where(m0, acca / la, accb / lb).astype(o_ref.dtype)
        l_ref[0] = jnp.broadcast_to(ma + jnp.log(la), (ATT_BLOCK, LANES))
        l_ref[1] = jnp.broadcast_to(mb + jnp.log(lb), (ATT_BLOCK, LANES))

    stat = pl.BlockSpec((None, 2, ATT_BLOCK, LANES), lambda j, i: (j, 0, i, 0))
    row = pl.BlockSpec((None, 2, s), lambda j, i: (j, 0, 0))
    return pl.pallas_call(
        body, name=name, grid=(npair, nq),
        in_specs=[pl.BlockSpec((ATT_BLOCK, LANES), lambda j, i: (i, j)),
                  pl.BlockSpec((s, LANES), lambda j, i: (0, npair + j)),
                  pl.BlockSpec((s, LANES), lambda j, i: (0, 2 * npair + j)),
                  stat, row],
        out_specs=[pl.BlockSpec((ATT_BLOCK, LANES), lambda j, i: (i, j)), stat],
        out_shape=[jax.ShapeDtypeStruct((s, FOX_WIDTH), BF16),
                   jax.ShapeDtypeStruct((npair, 2, s, LANES), F32)],
        compiler_params=_params(),
    )(qkv, qkv, qkv, cum_b, cum_r)


def _attn_delta(qkv, do, lse_b, cum_b, cum_r, *, name):
    s = qkv.shape[0]
    nq = s // ATT_BLOCK
    scale = FOX_HEAD_DIM ** -0.5
    npair = HEAD_PAIRS

    def body(q_ref, k_ref, v_ref, do_ref, l_ref, cq_ref, ck_ref, d_ref):
        qi = pl.program_id(1)
        m0 = _head_mask()
        q2 = q_ref[...]
        do2 = do_ref[...]
        qs = (jnp.where(m0, q2, jnp.zeros_like(q2)), jnp.where(m0, jnp.zeros_like(q2), q2))
        dos = (jnp.where(m0, do2, jnp.zeros_like(do2)), jnp.where(m0, jnp.zeros_like(do2), do2))

        def step(ki, carry, masked):
            off = pl.multiple_of(ki * ATT_BLOCK, ATT_BLOCK)
            k2 = k_ref[pl.ds(off, ATT_BLOCK), :]
            v2 = v_ref[pl.ds(off, ATT_BLOCK), :]
            out = []
            for hh in range(2):
                sc = _dot(qs[hh], k2, _NT) * scale + (cq_ref[hh] - ck_ref[hh:hh + 1, pl.ds(off, ATT_BLOCK)])
                p = jnp.exp(sc - l_ref[hh])
                if masked:
                    p = jnp.where(_causal(qi, ki), p, 0.0)
                out.append(carry[hh] + jnp.sum(p * _dot(dos[hh], v2, _NT), axis=-1, keepdims=True))
            return tuple(out)

        init = (jnp.zeros((ATT_BLOCK, 1), F32), jnp.zeros((ATT_BLOCK, 1), F32))
        carry = lax.fori_loop(0, qi, lambda ki, c: step(ki, c, False), init)
        da, db = step(qi, carry, True)
        d_ref[0] = jnp.broadcast_to(da, (ATT_BLOCK, LANES))
        d_ref[1] = jnp.broadcast_to(db, (ATT_BLOCK, LANES))

    stat = pl.BlockSpec((None, 2, ATT_BLOCK, LANES), lambda j, i: (j, 0, i, 0))
    return pl.pallas_call(
        body, name=name, grid=(npair, nq),
        in_specs=[pl.BlockSpec((ATT_BLOCK, LANES), lambda j, i: (i, j)),
                  pl.BlockSpec((s, LANES), lambda j, i: (0, npair + j)),
                  pl.BlockSpec((s, LANES), lambda j, i: (0, 2 * npair + j)),
                  pl.BlockSpec((ATT_BLOCK, LANES), lambda j, i: (i, j)),
                  stat, stat, pl.BlockSpec((None, 2, s), lambda j, i: (j, 0, 0))],
        out_specs=stat,
        out_shape=jax.ShapeDtypeStruct((npair, 2, s, LANES), F32), compiler_params=_params(),
    )(qkv, qkv, qkv, do, lse_b, cum_b, cum_r)


def _attn_bwd(qkv, do, lse_b, delta_b, cum_b, cum_r, *, name):
    s = qkv.shape[0]
    nq = s // ATT_BLOCK
    scale = FOX_HEAD_DIM ** -0.5
    npair = HEAD_PAIRS

    def body(q_ref, k_ref, v_ref, do_ref, l_ref, dl_ref, cq_ref, ck_ref, dq_ref, dk_ref, dv_ref, dc_ref):
        ki = pl.program_id(1)
        m0 = _head_mask()
        k2 = k_ref[...]
        v2 = v_ref[...]
        koff = pl.multiple_of(ki * ATT_BLOCK, ATT_BLOCK)

        @pl.when(ki == 0)
        def _():
            dq_ref[...] = jnp.zeros_like(dq_ref)

        def step(qi, carry, masked):
            off = pl.multiple_of(qi * ATT_BLOCK, ATT_BLOCK)
            q2 = q_ref[pl.ds(off, ATT_BLOCK), :]
            do2 = do_ref[pl.ds(off, ATT_BLOCK), :]
            qzero = jnp.zeros_like(q2)
            dzero = jnp.zeros_like(do2)
            out = []
            dqs = []
            for hh in range(2):
                dk_acc, dv_acc, dc_acc = carry[hh]
                keep = m0 if hh == 0 else jnp.logical_not(m0)
                qh = jnp.where(keep, q2, qzero)
                doh = jnp.where(keep, do2, dzero)
                sc = _dot(qh, k2, _NT) * scale + (cq_ref[hh, pl.ds(off, ATT_BLOCK), :]
                                                 - ck_ref[hh:hh + 1, pl.ds(koff, ATT_BLOCK)])
                p = jnp.exp(sc - l_ref[hh, pl.ds(off, ATT_BLOCK), :])
                if masked:
                    p = jnp.where(_causal(qi, ki), p, 0.0)
                dp = _dot(doh, v2, _NT)
                ds = p * (dp - dl_ref[hh, pl.ds(off, ATT_BLOCK), :])
                dv_acc = dv_acc + _dot(p, do2, _TN)
                dk_acc = dk_acc + _dot(ds, q2, _TN)
                dc_acc = dc_acc - jnp.sum(ds, axis=0, keepdims=True)
                dqs.append(_dot(ds, k2, _NN))
                out.append((dk_acc, dv_acc, dc_acc))
            dq_ref[pl.ds(off, ATT_BLOCK), :] += jnp.where(m0, dqs[0], dqs[1]) * scale
            return tuple(out)

        init = tuple((jnp.zeros((ATT_BLOCK, LANES), F32), jnp.zeros((ATT_BLOCK, LANES), F32),
                      jnp.zeros((1, ATT_BLOCK), F32)) for _ in range(2))
        carry = step(ki, init, True)
        (dka, dva, dca), (dkb, dvb, dcb) = lax.fori_loop(ki + 1, nq, lambda qi, c: step(qi, c, False), carry)
        dk_ref[...] = (jnp.where(m0, dka, dkb) * scale).astype(dk_ref.dtype)
        dv_ref[...] = jnp.where(m0, dva, dvb).astype(dv_ref.dtype)
        dc_ref[0:1, :] = dca
        dc_ref[1:2, :] = dcb

    stat = pl.BlockSpec((None, 2, s, LANES), lambda j, i: (j, 0, 0, 0))
    colfull = lambda base: pl.BlockSpec((s, LANES), lambda j, i: (0, base + j))
    colblk = lambda base: pl.BlockSpec((ATT_BLOCK, LANES), lambda j, i: (i, base + j))
    return pl.pallas_call(
        body, name=name, grid=(npair, nq),
        in_specs=[colfull(0), colblk(npair), colblk(2 * npair), colfull(0), stat, stat, stat,
                  pl.BlockSpec((None, 2, s), lambda j, i: (j, 0, 0))],
        out_specs=[colfull(0), colblk(0), colblk(0), pl.BlockSpec((None, 2, ATT_BLOCK), lambda j, i: (j, 0, i))],
        out_shape=[jax.ShapeDtypeStruct((s, FOX_WIDTH), F32), jax.ShapeDtypeStruct((s, FOX_WIDTH), BF16),
                   jax.ShapeDtypeStruct((s, FOX_WIDTH), BF16), jax.ShapeDtypeStruct((npair, 2, s), F32)],
        compiler_params=_params(),
    )(qkv, qkv, qkv, do, lse_b, delta_b, cum_b, cum_r)


ATT_TQ = 256
ATT_TK = 256
ATT_SCALE = FOX_HEAD_DIM ** -0.5
assert ATT_SCALE == 0.125 and ATT_TQ == ATT_TK


def _causal_t(qi, ki):
    kpos = lax.broadcasted_iota(jnp.int32, (ATT_TK, ATT_TQ), 0) + ki * ATT_TK
    qpos = lax.broadcasted_iota(jnp.int32, (ATT_TK, ATT_TQ), 1) + qi * ATT_TQ
    return kpos <= qpos


def _row_mask():
    return lax.broadcasted_iota(jnp.int32, (LANES, 1), 0) < FOX_HEAD_DIM


def _lane_tile(a, width):
    return a if a.shape[1] == width else jnp.tile(a, (1, width // a.shape[1]))


def _attn_fwd_t(qkv, q_t, v_t, cum_b, cum_r, *, name):
    s = qkv.shape[0]
    nq = s // ATT_TQ
    npair = HEAD_PAIRS

    def body(k_ref, qt_ref, vt_ref, cq_ref, ck_ref, o_ref, ot_ref, l_ref):
        qi = pl.program_id(1)
        rows = _row_mask()
        qt = qt_ref[...] * ATT_SCALE
        zero = jnp.zeros_like(qt)
        qts = (jnp.where(rows, qt, zero), jnp.where(rows, zero, qt))

        def step(ki, carry, masked):
            off = pl.multiple_of(ki * ATT_TK, ATT_TK)
            k2 = k_ref[pl.ds(off, ATT_TK), :]
            vt = vt_ref[:, pl.ds(off, ATT_TK)]
            out = []
            for hh in range(2):
                m, l, acc = carry[hh]
                bias = cq_ref[hh:hh + 1, :] - _lane_tile(ck_ref[hh, pl.ds(off, ATT_TK), :], ATT_TQ)
                sc = _dot(k2, qts[hh], _NN) + bias
                if masked:
                    sc = jnp.where(_causal_t(qi, ki), sc, -1e30)
                m_new = jnp.maximum(m, jnp.max(sc, axis=0, keepdims=True))
                alpha = jnp.exp(m - m_new)
                p = jnp.exp(sc - m_new)
                l = alpha * l + jnp.sum(p, axis=0, keepdims=True)
                p_hi = p.astype(BF16)
                p_lo = (p - p_hi.astype(F32)).astype(BF16)
                acc = alpha * acc + (_dot(vt, p_hi, _NN) + _dot(vt, p_lo, _NN))
                out.append((m_new, l, acc))
            return tuple(out)

        init = tuple((jnp.full((1, ATT_TQ), -1e30, F32), jnp.zeros((1, ATT_TQ), F32),
                      jnp.zeros((LANES, ATT_TQ), F32)) for _ in range(2))
        carry = lax.fori_loop(0, qi, lambda ki, c: step(ki, c, False), init)
        (ma, la, acca), (mb, lb, accb) = step(qi, carry, True)
        ot = jnp.where(rows, acca / la, accb / lb)
        ot_ref[...] = ot
        o_ref[...] = ot.T.astype(o_ref.dtype)
        l_ref[0:1, :] = ma + jnp.log(la)
        l_ref[1:2, :] = mb + jnp.log(lb)

    row = pl.BlockSpec((None, 2, ATT_TQ), lambda j, i: (j, 0, i))
    return pl.pallas_call(
        body, name=name, grid=(npair, nq),
        in_specs=[pl.BlockSpec((s, LANES), lambda j, i: (0, npair + j)),
                  pl.BlockSpec((LANES, ATT_TQ), lambda j, i: (j, i)),
                  pl.BlockSpec((LANES, s), lambda j, i: (j, 0)),
                  row, pl.BlockSpec((None, 2, s, LANES), lambda j, i: (j, 0, 0, 0))],
        out_specs=[pl.BlockSpec((ATT_TQ, LANES), lambda j, i: (i, j)),
                   pl.BlockSpec((LANES, ATT_TQ), lambda j, i: (j, i)), row],
        out_shape=[jax.ShapeDtypeStruct((s, FOX_WIDTH), BF16), jax.ShapeDtypeStruct((FOX_WIDTH, s), F32),
                   jax.ShapeDtypeStruct((npair, 2, s), F32)],
        compiler_params=_params(),
    )(qkv, q_t, v_t, cum_r, cum_b)


def _attn_delta_t(do_t, o_t, *, name):
    s = o_t.shape[1]
    ts = _tile(s, 512)

    def body(do_ref, o_ref, d_ref):
        prod = do_ref[...].astype(F32) * o_ref[...]
        d_ref[0:1, :] = jnp.sum(prod[:FOX_HEAD_DIM], axis=0, keepdims=True)
        d_ref[1:2, :] = jnp.sum(prod[FOX_HEAD_DIM:], axis=0, keepdims=True)

    blk = pl.BlockSpec((LANES, ts), lambda j, i: (j, i))
    return pl.pallas_call(
        body, name=name, grid=(HEAD_PAIRS, s // ts), in_specs=[blk, blk],
        out_specs=pl.BlockSpec((None, 2, ts), lambda j, i: (j, 0, i)),
        out_shape=jax.ShapeDtypeStruct((HEAD_PAIRS, 2, s), F32), compiler_params=_params(),
    )(do_t, o_t)


def _attn_bwd_t(qkv, q_t, k_t, do, do_t, lse, delta, cum_b, cum_r, *, name):
    s = qkv.shape[0]
    nq = s // ATT_TQ
    npair = HEAD_PAIRS

    def body(q_ref, k_ref, v_ref, qt_ref, kt_ref, do_ref, dot_ref, l_ref, dl_ref, cq_ref, ck_ref,
             dqt_ref, dk_ref, dv_ref, dc_ref):
        ki = pl.program_id(1)
        m0 = _head_mask()
        rows = _row_mask()
        k2 = k_ref[...]
        v2 = v_ref[...]
        kt = kt_ref[...]
        ks = k2 * ATT_SCALE
        kz, vz = jnp.zeros_like(k2), jnp.zeros_like(v2)
        khs = (jnp.where(m0, ks, kz), jnp.where(m0, kz, ks))
        vhs = (jnp.where(m0, v2, vz), jnp.where(m0, vz, v2))
        cks = tuple(_lane_tile(ck_ref[hh], ATT_TQ) for hh in range(2))

        @pl.when(ki == 0)
        def _():
            dqt_ref[...] = jnp.zeros_like(dqt_ref)

        def step(qi, carry, masked):
            off = pl.multiple_of(qi * ATT_TQ, ATT_TQ)
            q2 = q_ref[pl.ds(off, ATT_TQ), :]
            do2 = do_ref[pl.ds(off, ATT_TQ), :]
            qt = qt_ref[:, pl.ds(off, ATT_TQ)]
            dot_ = dot_ref[:, pl.ds(off, ATT_TQ)]
            out, dqs = [], []
            for hh in range(2):
                dk_acc, dv_acc, dc_acc = carry[hh]
                sc = _dot(khs[hh], qt, _NN) + (cq_ref[hh:hh + 1, pl.ds(off, ATT_TQ)] - cks[hh])
                p = jnp.exp(sc - l_ref[hh:hh + 1, pl.ds(off, ATT_TQ)])
                if masked:
                    p = jnp.where(_causal_t(qi, ki), p, 0.0)
                dp = _dot(vhs[hh], dot_, _NN)
                ds = p * (dp - dl_ref[hh:hh + 1, pl.ds(off, ATT_TQ)])
                dc_acc = dc_acc - jnp.sum(ds, axis=1, keepdims=True)
                dss = (ds * ATT_SCALE).astype(BF16)
                dv_acc = dv_acc + _dot(p, do2, _NN)
                dk_acc = dk_acc + _dot(dss, q2, _NN)
                dqs.append(_dot(kt, dss, _NN))
                out.append((dk_acc, dv_acc, dc_acc))
            dqt_ref[:, pl.ds(off, ATT_TQ)] += jnp.where(rows, dqs[0], dqs[1])
            return tuple(out)

        init = tuple((jnp.zeros((ATT_TK, LANES), F32), jnp.zeros((ATT_TK, LANES), F32),
                      jnp.zeros((ATT_TK, 1), F32)) for _ in range(2))
        carry = step(ki, init, True)
        (dka, dva, dca), (dkb, dvb, dcb) = lax.fori_loop(ki + 1, nq, lambda qi, c: step(qi, c, False), carry)
        dk_ref[...] = jnp.where(m0, dka, dkb).astype(dk_ref.dtype)
        dv_ref[...] = jnp.where(m0, dva, dvb).astype(dv_ref.dtype)
        dc_ref[0] = jnp.broadcast_to(dca, (ATT_TK, LANES))
        dc_ref[1] = jnp.broadcast_to(dcb, (ATT_TK, LANES))

    colfull = lambda base: pl.BlockSpec((s, LANES), lambda j, i: (0, base + j))
    colblk = lambda base: pl.BlockSpec((ATT_TK, LANES), lambda j, i: (i, base + j))
    rowfull = pl.BlockSpec((LANES, s), lambda j, i: (j, 0))
    stat = pl.BlockSpec((None, 2, s), lambda j, i: (j, 0, 0))
    bcast = pl.BlockSpec((None, 2, ATT_TK, LANES), lambda j, i: (j, 0, i, 0))
    return pl.pallas_call(
        body, name=name, grid=(npair, nq),
        in_specs=[colfull(0), colblk(npair), colblk(2 * npair), rowfull,
                  pl.BlockSpec((LANES, ATT_TK), lambda j, i: (j, i)), colfull(0), rowfull,
                  stat, stat, stat, bcast],
        out_specs=[rowfull, colblk(0), colblk(0), bcast],
        out_shape=[jax.ShapeDtypeStruct((FOX_WIDTH, s), F32), jax.ShapeDtypeStruct((s, FOX_WIDTH), BF16),
                   jax.ShapeDtypeStruct((s, FOX_WIDTH), BF16), jax.ShapeDtypeStruct((npair, 2, s, LANES), F32)],
        compiler_params=_params(),
    )(qkv, qkv, qkv, q_t, k_t, do, do_t, lse, delta, cum_r, cum_b)


def _merge_fwd(zg, ya, yb, *, name, tm=256):
    s, d = ya.shape
    tm = _tile(s, tm)

    def body(zg_ref, ya_ref, yb_ref, m_ref):
        ga = _sigmoid(zg_ref[:, :d].astype(F32))
        gb = _sigmoid(zg_ref[:, d:].astype(F32))
        m_ref[...] = (ga * ya_ref[...].astype(F32) + gb * yb_ref[...].astype(F32)).astype(m_ref.dtype)

    row = pl.BlockSpec((tm, d), lambda i: (i, 0))
    row2 = pl.BlockSpec((tm, 2 * d), lambda i: (i, 0))
    return pl.pallas_call(
        body, name=name, grid=(s // tm,), in_specs=[row2, row, row], out_specs=row,
        out_shape=jax.ShapeDtypeStruct((s, d), BF16), compiler_params=_params(),
    )(zg, ya, yb)


def _merge_bwd(dm, zg, ya, yb, *, name, tm=256):
    s, d = ya.shape
    tm = _tile(s, tm)

    def body(dm_ref, zg_ref, ya_ref, yb_ref, dzg_ref, dya_ref, dyb_ref):
        dmv = dm_ref[...].astype(F32)
        ga = _sigmoid(zg_ref[:, :d].astype(F32))
        gb = _sigmoid(zg_ref[:, d:].astype(F32))
        dzg_ref[:, :d] = (dmv * ya_ref[...].astype(F32) * ga * (1.0 - ga)).astype(dzg_ref.dtype)
        dzg_ref[:, d:] = (dmv * yb_ref[...].astype(F32) * gb * (1.0 - gb)).astype(dzg_ref.dtype)
        dya_ref[...] = (dmv * ga).astype(dya_ref.dtype)
        dyb_ref[...] = (dmv * gb).astype(dyb_ref.dtype)

    row = pl.BlockSpec((tm, d), lambda i: (i, 0))
    row2 = pl.BlockSpec((tm, 2 * d), lambda i: (i, 0))
    return pl.pallas_call(
        body, name=name, grid=(s // tm,), in_specs=[row, row2, row, row], out_specs=[row2, row, row],
        out_shape=[jax.ShapeDtypeStruct((s, 2 * d), BF16), jax.ShapeDtypeStruct((s, d), BF16),
                   jax.ShapeDtypeStruct((s, d), BF16)],
        compiler_params=_params(),
    )(dm, zg, ya, yb)


def _shift_down(u, k, row):
    return jnp.where(row >= k, pltpu.roll(u, k, 0), 0.0)


def _shift_up(u, k, row):
    n = u.shape[0]
    return jnp.where(row < n - k, pltpu.roll(u, n - k, 0), 0.0)


def _conv_act_fwd(up_a, up_b, cw_a, cw_b, cb_a, cb_b, *, name, tc=128):
    s, f = up_a.shape
    tc = _tile(f, tc)

    def body(ua_ref, ub_ref, wa_ref, wb_ref, ba_ref, bb_ref, act_ref):
        row = lax.broadcasted_iota(jnp.int32, (s, tc), 0)

        def conv(u_ref, w_ref, b_ref):
            u = u_ref[...].astype(F32)
            return (b_ref[...] + w_ref[0:1, :] * _shift_down(u, 2, row)
                    + w_ref[1:2, :] * _shift_down(u, 1, row) + w_ref[2:3, :] * u)

        ca = conv(ua_ref, wa_ref, ba_ref)
        cb = conv(ub_ref, wb_ref, bb_ref)
        act_ref[...] = (_gelu(ca) * cb).astype(act_ref.dtype)

    col = pl.BlockSpec((s, tc), lambda j: (0, j))
    w3 = pl.BlockSpec((3, tc), lambda j: (0, j))
    b1 = pl.BlockSpec((1, tc), lambda j: (0, j))
    return pl.pallas_call(
        body, name=name, grid=(f // tc,), in_specs=[col, col, w3, w3, b1, b1], out_specs=col,
        out_shape=jax.ShapeDtypeStruct((s, f), BF16), compiler_params=_params(),
    )(up_a, up_b, cw_a, cw_b, cb_a, cb_b)


def _conv_act_bwd(up_a, up_b, dact, cw_a, cw_b, cb_a, cb_b, *, name, tc=128):
    s, f = up_a.shape
    tc = _tile(f, tc)

    def body(ua_ref, ub_ref, da_ref, wa_ref, wb_ref, ba_ref, bb_ref, dua_ref, dub_ref, dwa_ref, dwb_ref):
        row = lax.broadcasted_iota(jnp.int32, (s, tc), 0)

        def conv(u_ref, w_ref, b_ref):
            u = u_ref[...].astype(F32)
            u1 = _shift_down(u, 1, row)
            u2 = _shift_down(u, 2, row)
            return u, u1, u2, b_ref[...] + w_ref[0:1, :] * u2 + w_ref[1:2, :] * u1 + w_ref[2:3, :] * u

        def back(dc, taps, w_ref, du_ref, dw_ref):
            u, u1, u2 = taps
            dw_ref[0:1, :] = jnp.sum(dc * u2, axis=0, keepdims=True)
            dw_ref[1:2, :] = jnp.sum(dc * u1, axis=0, keepdims=True)
            dw_ref[2:3, :] = jnp.sum(dc * u, axis=0, keepdims=True)
            dw_ref[3:4, :] = jnp.sum(dc, axis=0, keepdims=True)
            du = (w_ref[2:3, :] * dc + w_ref[1:2, :] * _shift_up(dc, 1, row)
                  + w_ref[0:1, :] * _shift_up(dc, 2, row))
            du_ref[...] = du.astype(du_ref.dtype)

        ua, ua1, ua2, ca = conv(ua_ref, wa_ref, ba_ref)
        ub, ub1, ub2, cb = conv(ub_ref, wb_ref, bb_ref)
        g, dg = _gelu_and_grad(ca)
        dact_v = da_ref[...].astype(F32)
        back(dact_v * cb * dg, (ua, ua1, ua2), wa_ref, dua_ref, dwa_ref)
        back(dact_v * g, (ub, ub1, ub2), wb_ref, dub_ref, dwb_ref)

    col = pl.BlockSpec((s, tc), lambda j: (0, j))
    w3 = pl.BlockSpec((3, tc), lambda j: (0, j))
    w4 = pl.BlockSpec((4, tc), lambda j: (0, j))
    b1 = pl.BlockSpec((1, tc), lambda j: (0, j))
    return pl.pallas_call(
        body, name=name, grid=(f // tc,), in_specs=[col, col, col, w3, w3, b1, b1],
        out_specs=[col, col, w4, w4],
        out_shape=[jax.ShapeDtypeStruct((s, f), BF16), jax.ShapeDtypeStruct((s, f), BF16),
                   jax.ShapeDtypeStruct((4, f), F32), jax.ShapeDtypeStruct((4, f), F32)],
        compiler_params=_params(),
    )(up_a, up_b, dact, cw_a, cw_b, cb_a, cb_b)


def _ple_final(x2, ple, zp, target, g_final, *, name, tm=256):
    s, d = x2.shape
    tm = _tile(s, tm)

    def body(x_ref, ple_ref, zp_ref, t_ref, g_ref, dx_ref, dple_ref, dzp_ref, dg_ref, loss_ref):
        @pl.when(pl.program_id(0) == 0)
        def _():
            dg_ref[...] = jnp.zeros_like(dg_ref)
            loss_ref[...] = jnp.zeros_like(loss_ref)

        gp = _sigmoid(zp_ref[...].astype(F32))
        plev = ple_ref[...].astype(F32)
        x3 = x_ref[...] + plev * gp
        r = lax.rsqrt(jnp.mean(x3 * x3, axis=-1, keepdims=True) + EPS)
        xhat = x3 * r
        gv = g_ref[...]
        diff = xhat * gv - t_ref[...]
        loss_ref[...] += 0.5 * jnp.sum(jnp.mean(diff * diff, axis=-1, keepdims=True), axis=0, keepdims=True)
        dy = diff * (1.0 / d)
        dg_ref[...] += jnp.sum(dy * xhat, axis=0, keepdims=True)
        dyg = dy * gv
        dx3 = r * (dyg - xhat * jnp.mean(dyg * xhat, axis=-1, keepdims=True))
        dx_ref[...] = dx3
        dple_ref[...] = (dx3 * gp).astype(dple_ref.dtype)
        dzp_ref[...] = (dx3 * plev * gp * (1.0 - gp)).astype(dzp_ref.dtype)

    row = pl.BlockSpec((tm, d), lambda i: (i, 0))
    vec = pl.BlockSpec((1, d), lambda i: (0, 0))
    return pl.pallas_call(
        body, name=name, grid=(s // tm,), in_specs=[row, row, row, row, vec],
        out_specs=[row, row, row, vec, pl.BlockSpec((1, LANES), lambda i: (0, 0))],
        out_shape=[jax.ShapeDtypeStruct((s, d), F32), jax.ShapeDtypeStruct((s, d), BF16),
                   jax.ShapeDtypeStruct((s, d), BF16), jax.ShapeDtypeStruct((1, d), F32),
                   jax.ShapeDtypeStruct((1, LANES), F32)],
        compiler_params=_params(),
    )(x2, ple, zp, target, g_final)


def _device_step(x, p, target, w, get_w_in=None, get_w_rest=None, on_grads_ffn=None, on_grads_mix=None):
    s = x.shape[0]
    g = {}
    w = dict(w)

    h = _rms_fwd(x, w["norm_mix_g"], name="rms_mix")
    if get_w_in is not None:
        w.update(get_w_in(h))
    z_uv = _mm(h, w["w_uv"], mode="nn", out_dtype=BF16, name="proj_uv", tm=1024)
    qkv = _mm(h, w["w_qkv"], mode="nn", out_dtype=BF16, name="proj_qkv", tm=1024)
    zg = _mm(h, w["w_g"], mode="nn", out_dtype=BF16, name="proj_gate", tm=1024)
    f = _mm(h, w["w_f"], mode="nn", out_dtype=F32, name="proj_f", tm=1024)

    a = _gmlp_fwd(z_uv, w["gmlp_ln_g"], w["gmlp_ln_b"], w["gmlp_w_s"], w["gmlp_b_s_t"], name="gmlp_fwd")

    cum_b, cum_t = _fox_cum(f, w["b_f"], name="fox_cum")
    cum_b = cum_b.reshape(HEAD_PAIRS, 2, s, LANES)
    cum_r = cum_t[:FOX_HEADS].reshape(HEAD_PAIRS, 2, s)
    q_t, k_t, v_t = (qkv[:, i * FOX_WIDTH:(i + 1) * FOX_WIDTH].T for i in range(3))
    b, o_t, lse = _attn_fwd_t(qkv, q_t, v_t, cum_b, cum_r, name="attn_fwd")
    if get_w_rest is not None:
        w.update(get_w_rest(b))

    ya = _mm(a, w["w_branch_a"], mode="nn", out_dtype=BF16, name="branch_a", tm=1024)
    yb = _mm(b, w["w_branch_b"], mode="nn", out_dtype=BF16, name="branch_b", tm=1024)
    merged = _merge_fwd(zg, ya, yb, name="merge_fwd")
    x1 = _mm(merged, w["w_out"], mode="nn", out_dtype=F32, name="proj_out", add=x, tm=1024)

    h2 = _rms_fwd(x1, w["norm_ffn_g"], name="rms_ffn")
    up_a = _mm(h2, w["w_up_a"], mode="nn", out_dtype=BF16, name="up_a", tm=1024, tn=D_FF // 2)
    up_b = _mm(h2, w["w_up_b"], mode="nn", out_dtype=BF16, name="up_b", tm=1024, tn=D_FF // 2)
    cw, cb = w["conv_w"], w["conv_b"]
    conv_args = (cw[:, :D_FF], cw[:, D_FF:], cb[:, :D_FF], cb[:, D_FF:])
    act = _conv_act_fwd(up_a, up_b, *conv_args, name="conv_act_fwd")
    x2 = _mm(act, w["w_down"], mode="nn", out_dtype=F32, name="down", add=x1, tm=512)

    h3 = _rms_fwd(x2, w["norm_ple_g"], name="rms_ple")
    ple = _mm(p, w["w_ple"], mode="nn", out_dtype=BF16, name="ple_proj", tm=1024)
    zp = _mm(h3, w["w_ple_gate"], mode="nn", out_dtype=BF16, name="ple_gate", tm=1024)
    dx3, dple, dzp, g["norm_final_g"], loss = _ple_final(x2, ple, zp, target, w["norm_final_g"], name="ple_final")

    g["w_ple"] = _mm(p, dple, mode="tn", out_dtype=BF16, name="dw_ple")
    g["w_ple_gate"] = _mm(h3, dzp, mode="tn", out_dtype=BF16, name="dw_ple_gate")
    dh3 = _mm(dzp, w["w_ple_gate"], mode="nt", out_dtype=BF16, name="dh3")
    dx2, dx2_b, g["norm_ple_g"] = _rms_bwd(x2, w["norm_ple_g"], dh3, dx3, name="rms_ple_bwd")

    g["w_down"] = _mm(act, dx2_b, mode="tn", out_dtype=BF16, name="dw_down", tm=D_FF // 2)
    dact = _mm(dx2_b, w["w_down"], mode="nt", out_dtype=BF16, name="dact", tn=D_FF // 2)
    dup_a, dup_b, dcw_a, dcw_b = _conv_act_bwd(up_a, up_b, dact, *conv_args, name="conv_act_bwd")
    g["conv_w"] = jnp.concatenate([dcw_a[:3], dcw_b[:3]], axis=1)
    g["conv_b"] = jnp.concatenate([dcw_a[3:], dcw_b[3:]], axis=1)
    g["w_up_a"] = _mm(h2, dup_a, mode="tn", out_dtype=BF16, name="dw_up_a", tn=D_FF // 2)
    g["w_up_b"] = _mm(h2, dup_b, mode="tn", out_dtype=BF16, name="dw_up_b", tn=D_FF // 2)
    dh2 = _mm(dup_a, w["w_up_a"], mode="nt", out_dtype=F32, name="dh2_a")
    dh2 = _mm(dup_b, w["w_up_b"], mode="nt", out_dtype=BF16, name="dh2_b", add=dh2)
    dx1, dx1_b, g["norm_ffn_g"] = _rms_bwd(x1, w["norm_ffn_g"], dh2, dx2, name="rms_ffn_bwd")
    dep = on_grads_ffn(g) if on_grads_ffn is not None else None

    g["w_out"] = _mm(merged, dx1_b, mode="tn", out_dtype=BF16, name="dw_out", dep=dep)
    dmerged = _mm(dx1_b, w["w_out"], mode="nt", out_dtype=BF16, name="dmerged")
    dzg, dya, dyb = _merge_bwd(dmerged, zg, ya, yb, name="merge_bwd")
    g["w_branch_a"] = _mm(a, dya, mode="tn", out_dtype=BF16, name="dw_branch_a")
    g["w_branch_b"] = _mm(b, dyb, mode="tn", out_dtype=BF16, name="dw_branch_b")
    da = _mm(dya, w["w_branch_a"], mode="nt", out_dtype=BF16, name="da")
    db = _mm(dyb, w["w_branch_b"], mode="nt", out_dtype=BF16, name="db")

    dz_uv, g["gmlp_w_s"], dbs_t, g["gmlp_ln_g"], g["gmlp_ln_b"] = _gmlp_bwd(
        z_uv, da, w["gmlp_ln_g"], w["gmlp_ln_b"], w["gmlp_w_s"], w["gmlp_b_s_t"], name="gmlp_bwd")
    g["gmlp_b_s"] = dbs_t[:, :GMLP_GROUPS].T

    db_t = db.T
    delta = _attn_delta_t(db_t, o_t, name="attn_delta")
    dq_t, dk, dv, dcum_b = _attn_bwd_t(qkv, q_t, k_t, db, db_t, lse, delta, cum_b, cum_r, name="attn_bwd")
    dcum_t = jnp.pad(dcum_b[..., 0].reshape(FOX_HEADS, s), ((0, LANES - FOX_HEADS), (0, 0)))
    df, g["b_f"] = _fox_dlogit(dcum_t, f, w["b_f"], name="fox_dlogit")
    dqkv = jnp.concatenate([dq_t.T.astype(BF16), dk, dv], axis=1)

    g["w_uv"] = _mm(h, dz_uv, mode="tn", out_dtype=BF16, name="dw_uv")
    g["w_qkv"] = _mm(h, dqkv, mode="tn", out_dtype=BF16, name="dw_qkv")
    g["w_f"] = _mm(h, df, mode="tn", out_dtype=BF16, name="dw_f")
    g["w_g"] = _mm(h, dzg, mode="tn", out_dtype=BF16, name="dw_g")
    dep = on_grads_mix(g) if on_grads_mix is not None else None
    dh = _mm(dz_uv, w["w_uv"], mode="nt", out_dtype=F32, name="dh_uv", dep=dep)
    dh = _mm(dqkv, w["w_qkv"], mode="nt", out_dtype=F32, name="dh_qkv", add=dh)
    dh = _mm(df, w["w_f"], mode="nt", out_dtype=F32, name="dh_f", add=dh)
    dh = _mm(dzg, w["w_g"], mode="nt", out_dtype=BF16, name="dh_g", add=dh)
    dx0, _, g["norm_mix_g"] = _rms_bwd(x, w["norm_mix_g"], dh, dx1, name="rms_mix_bwd")
    return loss, dx0, g


def _coords():
    return lax.axis_index("x"), lax.axis_index("y"), lax.axis_index("c")


def _other_chips(x, y):
    return [(1 - x, y), (x, 1 - y), (1 - x, 1 - y)]


def _remote(src, dst, send_sem, recv_sem, dev):
    return pltpu.make_async_remote_copy(src_ref=src, dst_ref=dst, send_sem=send_sem, recv_sem=recv_sem,
                                        device_id=dev, device_id_type=MESH)


_ANY = pl.BlockSpec(memory_space=pl.ANY)


def _gather_weights(halved, whole, *, name):
    nh, n = len(halved), len(halved) + len(whole)
    arrays = list(halved) + list(whole)

    def body(*refs):
        ins, outs = refs[:n], refs[n:2 * n]
        send_sems, recv_sems = refs[2 * n:]
        x, y, c = _coords()
        me, sib = 2 * x + y, (x, y, 1 - c)
        chips = _other_chips(x, y)

        def half(i, which):
            h = ins[i].shape[0] // 2
            return pl.ds(pl.multiple_of(which * h, 16), h)

        sends = []
        for i in range(n):
            src, dst = (ins[i].at[half(i, c)], outs[i].at[me, half(i, c)]) if i < nh else (ins[i], outs[i].at[me])
            for k, (cx, cy) in enumerate(chips):
                cp = _remote(src, dst, send_sems.at[i, k], recv_sems.at[i, k], (cx, cy, c))
                cp.start()
                sends.append(cp)
        for i in range(n):
            for k, (cx, cy) in enumerate(chips):
                got = outs[i].at[2 * cx + cy, half(i, c)] if i < nh else outs[i].at[2 * cx + cy]
                _remote(got, got, send_sems.at[i, k], recv_sems.at[i, k], sib).wait_recv()
                if i < nh:
                    cp = _remote(got, got, send_sems.at[i, 3 + k], recv_sems.at[i, 3 + k], sib)
                    cp.start()
                    sends.append(cp)
        for i in range(nh):
            for k, (cx, cy) in enumerate(chips):
                got = outs[i].at[2 * cx + cy, half(i, 1 - c)]
                _remote(got, got, send_sems.at[i, 3 + k], recv_sems.at[i, 3 + k], sib).wait_recv()
        for cp in sends:
            cp.wait_send()

    outs = pl.pallas_call(
        body, name=name, in_specs=[_ANY] * n, out_specs=[_ANY] * n,
        out_shape=[jax.ShapeDtypeStruct((N_CHIPS,) + a.shape, a.dtype) for a in arrays],
        scratch_shapes=[pltpu.SemaphoreType.DMA((n, 6)), pltpu.SemaphoreType.DMA((n, 6))],
        compiler_params=_params(),
    )(*arrays)
    chip = 2 * lax.axis_index("x") + lax.axis_index("y")
    return [lax.dynamic_update_index_in_dim(o, a, chip, 0) for o, a in zip(outs, arrays)]


def _pair_exchange(gs, *, name):
    n = len(gs)

    def body(*refs):
        ins, outs = refs[:n], refs[n:2 * n]
        send_sems, recv_sems = refs[2 * n:]
        x, y, c = _coords()
        copies = []
        for i in range(n):
            for j in range(N_CHIPS):
                cp = _remote(ins[i].at[j, 1 - c], outs[i].at[j], send_sems.at[i, j], recv_sems.at[i, j], (x, y, 1 - c))
                cp.start()
                copies.append(cp)
        for cp in copies:
            cp.wait()

    return pl.pallas_call(
        body, name=name, in_specs=[_ANY] * n, out_specs=[_ANY] * n,
        out_shape=[jax.ShapeDtypeStruct((N_CHIPS,) + a.shape[2:], a.dtype) for a in gs],
        scratch_shapes=[pltpu.SemaphoreType.DMA((n, N_CHIPS)), pltpu.SemaphoreType.DMA((n, N_CHIPS))],
        compiler_params=_params(),
    )(*gs)


def _chip_exchange(ss, *, name):
    n = len(ss)

    def body(*refs):
        ins, outs = refs[:n], refs[n:2 * n]
        send_sems, recv_sems = refs[2 * n:]
        x, y, c = _coords()
        me = 2 * x + y
        chips = _other_chips(x, y)
        sends = []
        for i in range(n):
            for k, (cx, cy) in enumerate(chips):
                cp = _remote(ins[i].at[2 * cx + cy], outs[i].at[me], send_sems.at[i, k], recv_sems.at[i, k], (cx, cy, c))
                cp.start()
                sends.append(cp)
        for i in range(n):
            for k, (cx, cy) in enumerate(chips):
                got = outs[i].at[2 * cx + cy]
                _remote(got, got, send_sems.at[i, k], recv_sems.at[i, k], (cx, cy, c)).wait_recv()
        for cp in sends:
            cp.wait_send()

    return pl.pallas_call(
        body, name=name, in_specs=[_ANY] * n, out_specs=[_ANY] * n,
        out_shape=[jax.ShapeDtypeStruct(a.shape, a.dtype) for a in ss],
        scratch_shapes=[pltpu.SemaphoreType.DMA((n, 3)), pltpu.SemaphoreType.DMA((n, 3))],
        compiler_params=_params(),
    )(*ss)


def _pair_share(hs, *, name):
    n = len(hs)

    def body(*refs):
        ins, outs = refs[:n], refs[n:2 * n]
        send_sems, recv_sems = refs[2 * n:]
        x, y, c = _coords()
        copies = []
        for i in range(n):
            cp = _remote(ins[i], outs[i], send_sems.at[i], recv_sems.at[i], (x, y, 1 - c))
            cp.start()
            copies.append(cp)
        for cp in copies:
            cp.wait()

    return pl.pallas_call(
        body, name=name, in_specs=[_ANY] * n, out_specs=[_ANY] * n,
        out_shape=[jax.ShapeDtypeStruct(a.shape, a.dtype) for a in hs],
        scratch_shapes=[pltpu.SemaphoreType.DMA((n,)), pltpu.SemaphoreType.DMA((n,))],
        compiler_params=_params(),
    )(*hs)


def _all_exchange(vec, *, name):
    def body(v_ref, o_ref, send_sems, recv_sems, local_sem):
        x, y, c = _coords()
        me = 4 * x + 2 * y + c
        local = pltpu.make_async_copy(v_ref, o_ref.at[me], local_sem)
        local.start()
        copies = []
        k = 0
        for dx in (0, 1):
            for dy in (0, 1):
                for dc in (0, 1):
                    if dx or dy or dc:
                        peer = (1 - x if dx else x, 1 - y if dy else y, 1 - c if dc else c)
                        cp = _remote(v_ref, o_ref.at[me], send_sems.at[k], recv_sems.at[k], peer)
                        cp.start()
                        copies.append(cp)
                        k += 1
        for cp in copies:
            cp.wait()
        local.wait()

    return pl.pallas_call(
        body, name=name, in_specs=[_ANY], out_specs=_ANY,
        out_shape=jax.ShapeDtypeStruct((8,) + vec.shape, vec.dtype),
        scratch_shapes=[pltpu.SemaphoreType.DMA((7,)), pltpu.SemaphoreType.DMA((7,)), pltpu.SemaphoreType.DMA(())],
        compiler_params=_params(),
    )(vec)


_HBM = pl.BlockSpec(memory_space=pltpu.HBM)
_SEM = pl.BlockSpec(memory_space=pltpu.SEMAPHORE)
_EFFECT = pltpu.SideEffectType.DATAFLOW_SIDE_EFFECTING


def _copies_start(srcs, lands, plan, n_copies, *, name):
    ns, n = len(srcs), len(srcs) + len(lands)

    def body(*refs):
        send_sems, recv_sems = refs[n], refs[n + 1]
        token = refs[-1]
        for k, (src, dst, dev) in enumerate(plan(refs[:ns], refs[ns:n])):
            _remote(src, dst, send_sems.at[k], recv_sems.at[k], dev).start()
        token[...] = jnp.zeros_like(token)

    arrays = list(srcs) + list(lands)
    outs = pl.pallas_call(
        body, name=name,
        out_shape=(pltpu.SemaphoreType.DMA((n_copies,)), pltpu.SemaphoreType.DMA((n_copies,)),
                   *[pltpu.HBM(a.shape, a.dtype) for a in arrays], jax.ShapeDtypeStruct((8, LANES), F32)),
        in_specs=[_HBM] * n, out_specs=(_SEM, _SEM, *[_HBM] * n, pl.BlockSpec(memory_space=pltpu.VMEM)),
        input_output_aliases={i: 2 + i for i in range(n)},
        compiler_params=pltpu.CompilerParams(has_side_effects=_EFFECT),
    )(*[pltpu.with_memory_space_constraint(a, pltpu.HBM) for a in arrays])
    return outs[0], outs[1], list(outs[2:2 + ns]), list(outs[2 + ns:2 + n]), outs[-1]


def _copies_wait(send_sems, recv_sems, srcs, lands, plan, first, after, *, name):
    ns, n = len(srcs), len(srcs) + len(lands)

    def body(*refs):
        send, recv = refs[n], refs[n + 1]
        for k, (src, dst, dev) in enumerate(plan(refs[:ns], refs[ns:n])):
            cp = _remote(src, dst, send.at[first + k], recv.at[first + k], dev)
            cp.wait_send()
            cp.wait_recv()

    arrays = list(srcs) + list(lands)
    outs = pl.pallas_call(
        body, name=name, out_shape=tuple(pltpu.HBM(a.shape, a.dtype) for a in arrays),
        in_specs=[_HBM] * n + [_SEM, _SEM] + [_ANY] * len(after), out_specs=tuple([_HBM] * n),
        input_output_aliases={i: i for i in range(n)},
        compiler_params=pltpu.CompilerParams(has_side_effects=_EFFECT),
    )(*arrays, send_sems, recv_sems, *after)
    return list(outs[:ns]), list(outs[ns:])


def _gather_plan(halved):
    def plan(srcs, lands):
        x, y, c = _coords()
        me = 2 * x + y
        out = []
        for i, (src, land) in enumerate(zip(srcs, lands)):
            if halved[i]:
                h = src.shape[0] // 2
                rows = pl.ds(pl.multiple_of(c * h, 16), h)
                src, dst = src.at[rows], land.at[me, rows]
            else:
                dst = land.at[me]
            out += [(src, dst, (cx, cy, c)) for cx, cy in _other_chips(x, y)]
        return out
    return plan


def _forward_halves(lands, *, name):
    n = len(lands)

    def body(*refs):
        ins, outs = refs[:n], refs[n:2 * n]
        send_sems, recv_sems = refs[2 * n:]
        x, y, c = _coords()
        copies = []
        for i in range(n):
            h = ins[i].shape[1] // 2
            rows = pl.ds(pl.multiple_of(c * h, 16), h)
            for k, (cx, cy) in enumerate(_other_chips(x, y)):
                cp = _remote(ins[i].at[2 * cx + cy, rows], outs[i].at[2 * cx + cy, rows],
                             send_sems.at[i, k], recv_sems.at[i, k], (x, y, 1 - c))
                cp.start()
                copies.append(cp)
        for cp in copies:
            cp.wait()

    return pl.pallas_call(
        body, name=name, in_specs=[_ANY] * n, out_specs=[_ANY] * n,
        out_shape=[jax.ShapeDtypeStruct(a.shape, a.dtype) for a in lands],
        input_output_aliases={i: i for i in range(n)},
        scratch_shapes=[pltpu.SemaphoreType.DMA((n, 3)), pltpu.SemaphoreType.DMA((n, 3))],
        compiler_params=_params(),
    )(*lands)


def _all_plan(srcs, lands):
    x, y, c = _coords()
    me = 4 * x + 2 * y + c
    out = []
    for src, land in zip(srcs, lands):
        for dx in (0, 1):
            for dy in (0, 1):
                for dc in (0, 1):
                    if dx or dy or dc:
                        out.append((src, land.at[me], (1 - x if dx else x, 1 - y if dy else y, 1 - c if dc else c)))
    return out


def _chip_plan(srcs, lands):
    x, y, c = _coords()
    me = 2 * x + y
    out = []
    for src, land in zip(srcs, lands):
        out += [(src.at[2 * cx + cy], land.at[me], (cx, cy, c)) for cx, cy in _other_chips(x, y)]
    return out


def _rtile(r, pref, mult):
    t = (min(r, pref) // mult) * mult
    while t >= mult:
        if r % t == 0:
            return t
        t -= mult
    return r


def _pair_add(g, recv, core, *, name):
    _, _, r2, cols = g.shape
    tr = _rtile(r2, 256, 16)

    def body(c_ref, g_ref, r_ref, o_ref):
        o_ref[...] = (g_ref[...].astype(F32) + r_ref[...].astype(F32)).astype(o_ref.dtype)

    blk = pl.BlockSpec((None, tr, cols), lambda j, i, c_ref: (j, i, 0))
    return pl.pallas_call(
        body, name=name,
        grid_spec=pltpu.PrefetchScalarGridSpec(
            num_scalar_prefetch=1, grid=(N_CHIPS, r2 // tr),
            in_specs=[pl.BlockSpec((None, None, tr, cols), lambda j, i, c_ref: (j, c_ref[0], i, 0)), blk],
            out_specs=blk),
        out_shape=jax.ShapeDtypeStruct(recv.shape, recv.dtype), compiler_params=_params(),
    )(core, g, recv)


def _sum_slots(a, out_dtype, *, name):
    n, r, cols = a.shape
    tr = _rtile(r, 256, 16)

    def body(a_ref, o_ref):
        acc = a_ref[0].astype(F32)
        for j in range(1, n):
            acc = acc + a_ref[j].astype(F32)
        o_ref[...] = acc.astype(o_ref.dtype)

    return pl.pallas_call(
        body, name=name, grid=(r // tr,),
        in_specs=[pl.BlockSpec((n, tr, cols), lambda i: (0, i, 0))],
        out_specs=pl.BlockSpec((tr, cols), lambda i: (i, 0)),
        out_shape=jax.ShapeDtypeStruct((r, cols), out_dtype), compiler_params=_params(),
    )(a)


def _chip_sum(own, recv, chip, *, name):
    _, r2, cols = own.shape
    tr = _rtile(r2, 256, 16)

    def body(chip_ref, own_ref, *rest):
        o_ref = rest[-1]
        acc = None
        for j in range(N_CHIPS):
            term = jnp.where(chip_ref[0] == j, own_ref[...], rest[j][...]).astype(F32)
            acc = term if acc is None else acc + term
        o_ref[...] = acc

    def slot(j):
        return pl.BlockSpec((None, tr, cols),
                            lambda i, chip_ref: (jnp.where(chip_ref[0] == j, (j + 1) % N_CHIPS, j), i, 0))

    return pl.pallas_call(
        body, name=name,
        grid_spec=pltpu.PrefetchScalarGridSpec(
            num_scalar_prefetch=1, grid=(r2 // tr,),
            in_specs=[pl.BlockSpec((None, tr, cols), lambda i, chip_ref: (chip_ref[0], i, 0))]
                     + [slot(j) for j in range(N_CHIPS)],
            out_specs=pl.BlockSpec((tr, cols), lambda i, chip_ref: (i, 0))),
        out_shape=jax.ShapeDtypeStruct((r2, cols), F32), compiler_params=_params(),
    )(chip, own, *([recv] * N_CHIPS))


def _adam_update(w, gv, m, v):
    c1 = 1.0 / (1.0 - ADAM_B1 ** ADAM_STEP)
    c2 = 1.0 / (1.0 - ADAM_B2 ** ADAM_STEP)
    nm = ADAM_B1 * m + (1.0 - ADAM_B1) * gv
    nv = ADAM_B2 * v + (1.0 - ADAM_B2) * gv * gv
    return -ADAM_LR * ((nm * c1) / (jnp.sqrt(nv * c2) + ADAM_EPS) + ADAM_WD * w), nm, nv


def _adamw_halves(w, g_mine, g_other, m, v, core, *, name):
    r, cols = w.shape
    r2 = r // 2
    tr = _rtile(r2, 256, 8)
    nt = r2 // tr

    def body(core_ref, w_ref, gm_ref, go_ref, m_ref, v_ref, g_ref, d_ref, nm_ref, nv_ref):
        gv = jnp.where(pl.program_id(0) == core_ref[0], gm_ref[...], go_ref[...])
        g_ref[...] = gv
        d_ref[...], nm_ref[...], nv_ref[...] = _adam_update(w_ref[...], gv, m_ref[...], v_ref[...])

    full = pl.BlockSpec((tr, cols), lambda hf, i, core_ref: (hf * nt + i, 0))
    half = pl.BlockSpec((tr, cols), lambda hf, i, core_ref: (i, 0))
    shape = jax.ShapeDtypeStruct((r, cols), F32)
    return pl.pallas_call(
        body, name=name,
        grid_spec=pltpu.PrefetchScalarGridSpec(
            num_scalar_prefetch=1, grid=(2, nt), in_specs=[full, half, half, full, full], out_specs=[full] * 4),
        out_shape=[shape] * 4, compiler_params=_params(),
    )(core, w, g_mine, g_other, m, v)


def _adamw(w, g, m, v, *, name, rows=256):
    r, cols = w.shape
    tr = _rtile(r, rows, 8)

    def body(w_ref, g_ref, m_ref, v_ref, d_ref, nm_ref, nv_ref):
        d_ref[...], nm_ref[...], nv_ref[...] = _adam_update(w_ref[...], g_ref[...], m_ref[...], v_ref[...])

    blk = pl.BlockSpec((tr, cols), lambda i: (i, 0))
    shape = jax.ShapeDtypeStruct((r, cols), F32)
    return pl.pallas_call(
        body, name=name, grid=(r // tr,), in_specs=[blk] * 4, out_specs=[blk] * 3,
        out_shape=[shape] * 3, compiler_params=_params(),
    )(w, g, m, v)


_BIG = (("w_in", 1), ("w_branch_a", 0), ("w_branch_b", 0), ("w_out", 0), ("w_up", 1), ("w_down", 0),
        ("w_ple", 1), ("w_ple_gate", 0))
_SMALL = ("b_f", "gmlp_ln_g", "gmlp_ln_b", "gmlp_w_s", "gmlp_b_s", "norm_ffn_g", "conv_b", "norm_ple_g",
          "norm_final_g", "norm_mix_g")
_WEIGHTS = ("norm_mix_g", "w_in", "b_f", "gmlp_ln_g", "gmlp_ln_b", "gmlp_w_s", "gmlp_b_s", "w_branch_a",
            "w_branch_b", "w_out", "norm_ffn_g", "w_up", "conv_w", "conv_b", "w_down", "norm_ple_g", "w_ple",
            "w_ple_gate", "norm_final_g")
_PACK_ROWS = 8


def _pack(arrays):
    parts = []
    for a in arrays:
        flat = a.reshape(-1)
        unit = _PACK_ROWS * LANES
        flat = jnp.pad(flat, (0, (-flat.shape[0]) % unit))
        parts.append(flat.reshape(-1, LANES))
    return jnp.concatenate(parts, axis=0)


def _unpack(packed, shapes):
    out, row = [], 0
    for shp in shapes:
        size = math.prod(shp)
        rows = -(-size // (_PACK_ROWS * LANES)) * _PACK_ROWS
        out.append(packed[row:row + rows].reshape(-1)[:size].reshape(shp))
        row += rows
    return out


def _assemble(gathered, axis):
    n, r, cols = gathered.shape
    if axis == 0:
        return gathered.reshape(n * r, cols)
    return gathered.transpose(1, 0, 2).reshape(r, n * cols)


def _to_chunks(full, axis):
    if axis == 0:
        r, cols = full.shape[0] // N_CHIPS, full.shape[1]
        chunks = full.reshape(N_CHIPS, r, cols)
    else:
        r, cols = full.shape[0], full.shape[1] // N_CHIPS
        chunks = full.reshape(r, N_CHIPS, cols).transpose(1, 0, 2)
    return chunks.reshape(N_CHIPS, 2, r // 2, cols)


def kernel(x, p, norm_mix_g, w_in, b_f, gmlp_ln_g, gmlp_ln_b, gmlp_w_s, gmlp_b_s, w_branch_a, w_branch_b, w_out, norm_ffn_g, w_up, conv_w, conv_b, w_down, norm_ple_g, w_ple, w_ple_gate, norm_final_g, loss_target, m_norm_mix_g, m_w_in, m_b_f, m_gmlp_ln_g, m_gmlp_ln_b, m_gmlp_w_s, m_gmlp_b_s, m_w_branch_a, m_w_branch_b, m_w_out, m_norm_ffn_g, m_w_up, m_conv_w, m_conv_b, m_w_down, m_norm_ple_g, m_w_ple, m_w_ple_gate, m_norm_final_g, v_norm_mix_g, v_w_in, v_b_f, v_gmlp_ln_g, v_gmlp_ln_b, v_gmlp_w_s, v_gmlp_b_s, v_w_branch_a, v_w_branch_b, v_w_out, v_norm_ffn_g, v_w_up, v_conv_w, v_conv_b, v_w_down, v_norm_ple_g, v_w_ple, v_w_ple_gate, v_norm_final_g):
    args = dict(locals())
    wt = {n: args[n] for n in _WEIGHTS}
    mom = {n: args["m_" + n] for n in _WEIGHTS}
    var = {n: args["v_" + n] for n in _WEIGHTS}
    chip = 2 * lax.axis_index("x") + lax.axis_index("y")
    core = lax.axis_index("c").astype(jnp.int32).reshape(1)

    chip1 = chip.astype(jnp.int32).reshape(1)
    device = 2 * chip + lax.axis_index("c")
    axis_of = dict(_BIG)
    names = [n for n, _ in _BIG]
    put_mine = lambda land, mine: lax.dynamic_update_index_in_dim(land, mine, chip, 0)

    shards = [wt[n][0].astype(BF16) for n in names] + [conv_w[0]]
    halved = [True] * len(names) + [False]
    lands = [lax.empty((N_CHIPS,) + a.shape, a.dtype) for a in shards]
    send_sems, recv_sems, srcs, lands, _ = _copies_start(shards, lands, _gather_plan(halved), 3 * len(shards),
                                                         name="gather_start")
    o1 = 2 * GMLP_WIDTH
    o2 = o1 + 3 * FOX_WIDTH
    o3 = o2 + FOX_HEADS
    fpad = ((0, 0), (0, LANES - FOX_HEADS))
    w = {
        "conv_b": conv_b, "norm_mix_g": norm_mix_g, "norm_ffn_g": norm_ffn_g, "norm_ple_g": norm_ple_g,
        "norm_final_g": norm_final_g.reshape(1, D_MODEL), "b_f": jnp.pad(b_f, fpad),
        "gmlp_ln_g": gmlp_ln_g, "gmlp_ln_b": gmlp_ln_b, "gmlp_w_s": gmlp_w_s[0],
        "gmlp_b_s_t": jnp.pad(gmlp_b_s[0].T, ((0, 0), (0, LANES - GMLP_GROUPS))),
    }

    def get_w_in(after):
        _, got = _copies_wait(send_sems, recv_sems, srcs[:1], lands[:1], _gather_plan(halved[:1]), 0, [after],
                              name="gather_wait_in")
        got = _forward_halves(got, name="gather_forward_in")
        full = _assemble(put_mine(got[0], shards[0]), 1)
        return {"w_uv": full[:, :o1], "w_qkv": full[:, o1:o2], "w_f": jnp.pad(full[:, o2:o3], fpad),
                "w_g": full[:, o3:]}

    def get_w_rest(after):
        _, got = _copies_wait(send_sems, recv_sems, srcs[1:], lands[1:], _gather_plan(halved[1:]), 3, [after],
                              name="gather_wait_rest")
        got = list(_forward_halves(got[:-1], name="gather_forward_rest")) + got[-1:]
        full = {n: _assemble(put_mine(got[i], shards[1 + i]), axis_of[n]) for i, n in enumerate(names[1:])}
        return {"w_branch_a": full["w_branch_a"], "w_branch_b": full["w_branch_b"], "w_out": full["w_out"],
                "w_up_a": full["w_up"][:, :D_FF], "w_up_b": full["w_up"][:, D_FF:], "w_down": full["w_down"],
                "w_ple": full["w_ple"], "w_ple_gate": full["w_ple_gate"],
                "conv_w": _assemble(put_mine(got[-1], shards[-1]), 1)}

    grads, delta, new_m, new_v = {}, {}, {}, {}
    pending = {}

    def reduce_start(group, gfull, tag):
        chunks = [_to_chunks(gfull[n], axis_of[n]) for n in group]
        from_sibling = _pair_exchange(chunks, name="grad_pair_exchange_" + tag)
        pair_sums = [_pair_add(chunks[i], from_sibling[i], core, name="grad_pair_add_" + n) for i, n in enumerate(group)]
        empty = [lax.empty(a.shape, a.dtype) for a in pair_sums]
        ssem, rsem, own, recv, token = _copies_start(pair_sums, empty, _chip_plan, 3 * len(group),
                                                     name="grad_chip_start_" + tag)
        pending[tag] = (ssem, rsem, own, recv)
        return token

    def reduce_finish(group, tag, after):
        ssem, rsem, own, recv = pending[tag]
        own, recv = _copies_wait(ssem, rsem, own, recv, _chip_plan, 0, after, name="grad_chip_wait_" + tag)
        halves = [_chip_sum(own[i], recv[i], chip1, name="grad_chip_sum_" + n) for i, n in enumerate(group)]
        other_halves = _pair_share(halves, name="grad_pair_share_" + tag)
        for i, n in enumerate(group):
            shp = wt[n].shape
            outs = _adamw_halves(wt[n].reshape(shp[-2:]), halves[i], other_halves[i], mom[n].reshape(shp[-2:]),
                                 var[n].reshape(shp[-2:]), core, name="adamw_" + n)
            grads[n], delta[n], new_m[n], new_v[n] = (o.reshape(shp) for o in outs)
        return new_v[group[-1]]

    ffn_group = ("w_up", "w_down", "w_ple", "w_ple_gate")
    mix_group = ("w_in", "w_branch_a", "w_branch_b", "w_out")

    def on_grads_ffn(g):
        gfull = dict(g)
        gfull["w_up"] = jnp.concatenate([g["w_up_a"], g["w_up_b"]], axis=1)
        return reduce_start(ffn_group, gfull, "ffn")

    def on_grads_mix(g):
        early = [g[n] if n != "b_f" else g[n][:, :FOX_HEADS] for n in _SMALL[:-1]] + [g["conv_w"]]
        vec = _pack(early)
        ssem, rsem, own, recv, small_token = _copies_start(
            [vec], [lax.empty((8,) + vec.shape, F32)], _all_plan, 7, name="small_start")
        pending["small"] = (ssem, rsem, own, recv)
        gfull = dict(g)
        gfull["w_in"] = jnp.concatenate([g["w_uv"], g["w_qkv"], g["w_f"][:, :FOX_HEADS], g["w_g"]], axis=1)
        token = reduce_start(mix_group, gfull, "mix")
        pending["ffn_done"] = reduce_finish(ffn_group, "ffn", [token])
        return token + small_token

    loss, grad_x, g = _device_step(x[0], p[0, 0], loss_target[0], w, get_w_in, get_w_rest, on_grads_ffn, on_grads_mix)

    ssem, rsem, own, recv = pending["small"]
    own, recv = _copies_wait(ssem, rsem, own, recv, _all_plan, 0, [grad_x], name="small_wait")
    vec_early = _sum_slots(lax.dynamic_update_index_in_dim(recv[0], own[0], device, 0), F32, name="small_sum")
    vec_late = _pack([g["norm_mix_g"]])
    vec_late = _sum_slots(_all_exchange(vec_late, name="small_exchange_late"), F32, name="small_sum_late")
    early_rows = _pack([wt[n] for n in _SMALL[:-1]]).shape[0]
    vec = jnp.concatenate([vec_early[:early_rows], vec_late], axis=0)
    for n, a in zip(_SMALL, _unpack(vec, [wt[n].shape for n in _SMALL])):
        grads[n] = a
    conv_w_grad = _unpack(vec_early[early_rows:], [(3, 2 * D_FF)])[0]
    grads["conv_w"] = lax.dynamic_slice_in_dim(conv_w_grad, chip * conv_w.shape[2], conv_w.shape[2], axis=1).reshape(conv_w.shape)

    reduce_finish(mix_group, "mix", [grad_x, pending["ffn_done"], vec])
    shp = conv_w.shape
    outs = _adamw(conv_w.reshape(shp[-2:]), grads["conv_w"].reshape(shp[-2:]), m_conv_w.reshape(shp[-2:]),
                  v_conv_w.reshape(shp[-2:]), name="adamw_conv_w")
    delta["conv_w"], new_m["conv_w"], new_v["conv_w"] = (o.reshape(shp) for o in outs)
    outs = _adamw(_pack([wt[n] for n in _SMALL]), vec, _pack([mom[n] for n in _SMALL]),
                  _pack([var[n] for n in _SMALL]), name="adamw_small", rows=2048)
    for d, o in zip((delta, new_m, new_v), outs):
        for n, a in zip(_SMALL, _unpack(o, [wt[n].shape for n in _SMALL])):
            d[n] = a

    total_loss = lax.psum(loss[0, 0], ("x", "y", "c"))
    return (total_loss, grad_x.reshape(x.shape), *[grads[n] for n in _WEIGHTS], *[delta[n] for n in _WEIGHTS],
            *[new_m[n] for n in _WEIGHTS], *[new_v[n] for n in _WEIGHTS])
```

```python
import functools
import math

import jax
import jax.numpy as jnp
from jax import lax
from jax.experimental import pallas as pl
from jax.experimental.pallas import tpu as pltpu

F32 = jnp.float32
BF16 = jnp.bfloat16

D_MODEL = 1024
EPS = 1e-6
CHUNK = 64
GMLP_GROUPS = 8
GMLP_BLOCK = 128
GMLP_WIDTH = 1024
FOX_HEADS = 16
FOX_HEAD_DIM = 64
FOX_WIDTH = 1024
HEAD_PAIRS = FOX_HEADS // 2
ATT_BLOCK = 128
D_FF = 2816
PLE_DIM = 256
LANES = 128
N_CHIPS = 4

ADAM_LR = 0.001
ADAM_B1 = 0.9
ADAM_B2 = 0.999
ADAM_EPS = 1e-08
ADAM_WD = 0.01
ADAM_STEP = 10

VMEM_LIMIT = 56 * 1024 * 1024
MESH = pl.DeviceIdType.MESH

_NN = (((1,), (0,)), ((), ()))
_NT = (((1,), (1,)), ((), ()))
_TN = (((0,), (0,)), ((), ()))


def _params(**kw):
    return pltpu.CompilerParams(vmem_limit_bytes=VMEM_LIMIT, **kw)


def _tile(dim, pref):
    if dim <= pref:
        return dim
    t = (pref // LANES) * LANES
    while t >= LANES:
        if dim % t == 0:
            return t
        t -= LANES
    return dim


def _dot(a, b, dn):
    return lax.dot_general(a.astype(BF16), b.astype(BF16), dn, preferred_element_type=F32)


def _gelu(x):
    c = math.sqrt(2.0 / math.pi)
    t = jnp.tanh(c * (x + 0.044715 * x * x * x))
    return 0.5 * x * (1.0 + t)


def _gelu_and_grad(x):
    c = math.sqrt(2.0 / math.pi)
    x2 = x * x
    t = jnp.tanh(c * (x + 0.044715 * x2 * x))
    g = 0.5 * x * (1.0 + t)
    dg = 0.5 * (1.0 + t) + 0.5 * x * (1.0 - t * t) * c * (1.0 + 3.0 * 0.044715 * x2)
    return g, dg


def _sigmoid(x):
    return 1.0 / (1.0 + jnp.exp(-x))


def _mm(a, b, *, mode, out_dtype, name, add=None, tm=512, tn=512, dep=None):
    if mode == "nn":
        m, k = a.shape
        k2, n = b.shape
    elif mode == "nt":
        m, k = a.shape
        n, k2 = b.shape
    else:
        k, m = a.shape
        k2, n = b.shape
    assert k == k2, (name, a.shape, b.shape)
    tm = _tile(m, tm)
    tn = _tile(n, tn)
    dn = {"nn": _NN, "nt": _NT, "tn": _TN}[mode]

    def body(a_ref, b_ref, *rest):
        o_ref = rest[-1]
        acc = _dot(a_ref[...], b_ref[...], dn)
        if add is not None:
            acc = acc + rest[0][...].astype(F32)
        o_ref[...] = acc.astype(o_ref.dtype)

    a_spec = pl.BlockSpec((k, tm), lambda i, j: (0, i)) if mode == "tn" else pl.BlockSpec((tm, k), lambda i, j: (i, 0))
    b_spec = pl.BlockSpec((tn, k), lambda i, j: (j, 0)) if mode == "nt" else pl.BlockSpec((k, tn), lambda i, j: (0, j))
    o_spec = pl.BlockSpec((tm, tn), lambda i, j: (i, j))
    in_specs = [a_spec, b_spec]
    args = [a, b]
    if add is not None:
        in_specs.append(o_spec)
        args.append(add)
    if dep is not None:
        in_specs.append(pl.BlockSpec(memory_space=pl.ANY))
        args.append(dep)
    return pl.pallas_call(
        body, name=name, grid=(m // tm, n // tn), in_specs=in_specs, out_specs=o_spec,
        out_shape=jax.ShapeDtypeStruct((m, n), out_dtype), compiler_params=_params(),
    )(*args)


def _rms_fwd(x, g, *, name, tm=256):
    s, d = x.shape
    tm = _tile(s, tm)

    def body(x_ref, g_ref, h_ref):
        xv = x_ref[...]
        r = lax.rsqrt(jnp.mean(xv * xv, axis=-1, keepdims=True) + EPS)
        h_ref[...] = (xv * r * g_ref[...]).astype(h_ref.dtype)

    return pl.pallas_call(
        body, name=name, grid=(s // tm,),
        in_specs=[pl.BlockSpec((tm, d), lambda i: (i, 0)), pl.BlockSpec((1, d), lambda i: (0, 0))],
        out_specs=pl.BlockSpec((tm, d), lambda i: (i, 0)),
        out_shape=jax.ShapeDtypeStruct((s, d), BF16), compiler_params=_params(),
    )(x, g)


def _rms_bwd(x, g, dh, dres, *, name, tm=256):
    s, d = x.shape
    tm = _tile(s, tm)

    def body(x_ref, g_ref, dh_ref, dres_ref, dx_ref, dxb_ref, dg_ref):
        xv = x_ref[...]
        r = lax.rsqrt(jnp.mean(xv * xv, axis=-1, keepdims=True) + EPS)
        xhat = xv * r
        dhv = dh_ref[...].astype(F32)
        dyg = dhv * g_ref[...]
        dx = dres_ref[...] + r * (dyg - xhat * jnp.mean(dyg * xhat, axis=-1, keepdims=True))
        dx_ref[...] = dx
        dxb_ref[...] = dx.astype(dxb_ref.dtype)

        @pl.when(pl.program_id(0) == 0)
        def _():
            dg_ref[...] = jnp.zeros_like(dg_ref)

        dg_ref[...] += jnp.sum(dhv * xhat, axis=0, keepdims=True)

    row = pl.BlockSpec((tm, d), lambda i: (i, 0))
    vec = pl.BlockSpec((1, d), lambda i: (0, 0))
    return pl.pallas_call(
        body, name=name, grid=(s // tm,), in_specs=[row, vec, row, row], out_specs=[row, row, vec],
        out_shape=[jax.ShapeDtypeStruct((s, d), F32), jax.ShapeDtypeStruct((s, d), BF16),
                   jax.ShapeDtypeStruct((1, d), F32)],
        compiler_params=_params(),
    )(x, g, dh, dres)


def _gmlp_mask():
    t = lax.broadcasted_iota(jnp.int32, (GMLP_BLOCK, GMLP_BLOCK), 0)
    s_ = lax.broadcasted_iota(jnp.int32, (GMLP_BLOCK, GMLP_BLOCK), 1)
    return (s_ // CHUNK) <= (t // CHUNK)


def _gmlp_norm(zv, ln_g, ln_b):
    vv, dvv = _gelu_and_grad(zv)
    mu = jnp.mean(vv, axis=-1, keepdims=True)
    xc = vv - mu
    rstd = lax.rsqrt(jnp.mean(xc * xc, axis=-1, keepdims=True) + EPS)
    vhat = xc * rstd
    return vhat * ln_g + ln_b, vhat, rstd, dvv


def _gmlp_fwd(z_uv, ln_g, ln_b, w_s, b_s_t, *, name):
    s = z_uv.shape[0]
    w = GMLP_WIDTH
    gd = w // GMLP_GROUPS

    def body(z_ref, lg_ref, lb_ref, ws_ref, bs_ref, a_ref):
        u = _gelu(z_ref[:, :w].astype(F32))
        vn, _, _, _ = _gmlp_norm(z_ref[:, w:].astype(F32), lg_ref[...], lb_ref[...])
        mask = _gmlp_mask()
        for g in range(GMLP_GROUPS):
            wm = jnp.where(mask, ws_ref[g], 0.0)
            mixed = _dot(wm, vn[:, g * gd:(g + 1) * gd], _NN) + bs_ref[:, g:g + 1]
            a_ref[:, g * gd:(g + 1) * gd] = (u[:, g * gd:(g + 1) * gd] * mixed).astype(a_ref.dtype)

    full = lambda shape: pl.BlockSpec(shape, lambda i: (0,) * len(shape))
    return pl.pallas_call(
        body, name=name, grid=(s // GMLP_BLOCK,),
        in_specs=[pl.BlockSpec((GMLP_BLOCK, 2 * w), lambda i: (i, 0)), full((1, w)), full((1, w)),
                  full((GMLP_GROUPS, GMLP_BLOCK, GMLP_BLOCK)), full((GMLP_BLOCK, LANES))],
        out_specs=pl.BlockSpec((GMLP_BLOCK, w), lambda i: (i, 0)),
        out_shape=jax.ShapeDtypeStruct((s, w), BF16), compiler_params=_params(),
    )(z_uv, ln_g, ln_b, w_s, b_s_t)


def _gmlp_bwd(z_uv, da, ln_g, ln_b, w_s, b_s_t, *, name):
    s = z_uv.shape[0]
    w = GMLP_WIDTH
    gd = w // GMLP_GROUPS

    def body(z_ref, da_ref, lg_ref, lb_ref, ws_ref, bs_ref, dz_ref, dws_ref, dbs_ref, dlg_ref, dlb_ref):
        @pl.when(pl.program_id(0) == 0)
        def _():
            dws_ref[...] = jnp.zeros_like(dws_ref)
            dbs_ref[...] = jnp.zeros_like(dbs_ref)
            dlg_ref[...] = jnp.zeros_like(dlg_ref)
            dlb_ref[...] = jnp.zeros_like(dlb_ref)

        u, du_dz = _gelu_and_grad(z_ref[:, :w].astype(F32))
        lg = lg_ref[...]
        vn, vhat, rstd, dvv_dz = _gmlp_norm(z_ref[:, w:].astype(F32), lg, lb_ref[...])
        dav = da_ref[...].astype(F32)
        mask = _gmlp_mask()
        lane = lax.broadcasted_iota(jnp.int32, (GMLP_BLOCK, LANES), 1)
        dvn_parts = []
        dbs = jnp.zeros((GMLP_BLOCK, LANES), F32)
        for g in range(GMLP_GROUPS):
            sl = slice(g * gd, (g + 1) * gd)
            wm = jnp.where(mask, ws_ref[g], 0.0)
            vn_g = vn[:, sl]
            mixed = _dot(wm, vn_g, _NN) + bs_ref[:, g:g + 1]
            dmixed = dav[:, sl] * u[:, sl]
            dz_ref[:, sl] = (dav[:, sl] * mixed * du_dz[:, sl]).astype(dz_ref.dtype)
            dvn_parts.append(_dot(wm, dmixed, _TN))
            dws_ref[g] += jnp.where(mask, _dot(dmixed, vn_g, _NT), 0.0)
            dbs = dbs + jnp.where(lane == g, jnp.sum(dmixed, axis=-1, keepdims=True), 0.0)
        dbs_ref[...] += dbs
        dvn = jnp.concatenate(dvn_parts, axis=-1)
        dlg_ref[...] += jnp.sum(dvn * vhat, axis=0, keepdims=True)
        dlb_ref[...] += jnp.sum(dvn, axis=0, keepdims=True)
        dyg = dvn * lg
        dvv = rstd * (dyg - jnp.mean(dyg, axis=-1, keepdims=True)
                      - vhat * jnp.mean(dyg * vhat, axis=-1, keepdims=True))
        dz_ref[:, w:] = (dvv * dvv_dz).astype(dz_ref.dtype)

    full = lambda shape: pl.BlockSpec(shape, lambda i: (0,) * len(shape))
    return pl.pallas_call(
        body, name=name, grid=(s // GMLP_BLOCK,),
        in_specs=[pl.BlockSpec((GMLP_BLOCK, 2 * w), lambda i: (i, 0)),
                  pl.BlockSpec((GMLP_BLOCK, w), lambda i: (i, 0)), full((1, w)), full((1, w)),
                  full((GMLP_GROUPS, GMLP_BLOCK, GMLP_BLOCK)), full((GMLP_BLOCK, LANES))],
        out_specs=[pl.BlockSpec((GMLP_BLOCK, 2 * w), lambda i: (i, 0)),
                   full((GMLP_GROUPS, GMLP_BLOCK, GMLP_BLOCK)), full((GMLP_BLOCK, LANES)),
                   full((1, w)), full((1, w))],
        out_shape=[jax.ShapeDtypeStruct((s, 2 * w), BF16),
                   jax.ShapeDtypeStruct((GMLP_GROUPS, GMLP_BLOCK, GMLP_BLOCK), F32),
                   jax.ShapeDtypeStruct((GMLP_BLOCK, LANES), F32),
                   jax.ShapeDtypeStruct((1, w), F32), jax.ShapeDtypeStruct((1, w), F32)],
        compiler_params=_params(),
    )(z_uv, da, ln_g, ln_b, w_s, b_s_t)


def _tri(lower):
    r = lax.broadcasted_iota(jnp.int32, (ATT_BLOCK, ATT_BLOCK), 0)
    c = lax.broadcasted_iota(jnp.int32, (ATT_BLOCK, ATT_BLOCK), 1)
    return jnp.where((c <= r) if lower else (c >= r), 1.0, 0.0).astype(F32)


def _log_sigmoid(x):
    return jnp.minimum(x, 0.0) - jnp.log(1.0 + jnp.exp(-jnp.abs(x)))


def _fox_cum(f, b_f, *, name):
    s = f.shape[0]
    nb = s // ATT_BLOCK

    def body(f_ref, b_ref, cb_ref, ct_ref, carry):
        @pl.when(pl.program_id(0) == 0)
        def _():
            carry[...] = jnp.zeros_like(carry)

        lf = _log_sigmoid(f_ref[...] + b_ref[...])
        cum = lax.dot_general(_tri(True), lf, _NN, precision=lax.Precision.HIGHEST,
                              preferred_element_type=F32) + carry[...]
        carry[...] = cum[ATT_BLOCK - 1:ATT_BLOCK, :]
        for h in range(FOX_HEADS):
            cb_ref[h] = jnp.broadcast_to(cum[:, h:h + 1], (ATT_BLOCK, LANES))
        ct_ref[...] = cum.T

    return pl.pallas_call(
        body, name=name, grid=(nb,),
        in_specs=[pl.BlockSpec((ATT_BLOCK, LANES), lambda i: (i, 0)), pl.BlockSpec((1, LANES), lambda i: (0, 0))],
        out_specs=[pl.BlockSpec((FOX_HEADS, ATT_BLOCK, LANES), lambda i: (0, i, 0)),
                   pl.BlockSpec((LANES, ATT_BLOCK), lambda i: (0, i))],
        out_shape=[jax.ShapeDtypeStruct((FOX_HEADS, s, LANES), F32), jax.ShapeDtypeStruct((LANES, s), F32)],
        scratch_shapes=[pltpu.VMEM((1, LANES), F32)], compiler_params=_params(),
    )(f, b_f)


def _fox_dlogit(dcum_t, f, b_f, *, name):
    s = f.shape[0]
    nb = s // ATT_BLOCK

    def body(dc_ref, f_ref, b_ref, df_ref, db_ref, carry):
        @pl.when(pl.program_id(0) == 0)
        def _():
            carry[...] = jnp.zeros_like(carry)
            db_ref[...] = jnp.zeros_like(db_ref)

        d = dc_ref[...].T
        dlog = lax.dot_general(_tri(False), d, _NN, precision=lax.Precision.HIGHEST,
                               preferred_element_type=F32) + carry[...]
        carry[...] = dlog[0:1, :]
        df = dlog * (1.0 - _sigmoid(f_ref[...] + b_ref[...]))
        df_ref[...] = df
        db_ref[...] += jnp.sum(df, axis=0, keepdims=True)

    rev = lambda i: nb - 1 - i
    return pl.pallas_call(
        body, name=name, grid=(nb,),
        in_specs=[pl.BlockSpec((LANES, ATT_BLOCK), lambda i: (0, rev(i))),
                  pl.BlockSpec((ATT_BLOCK, LANES), lambda i: (rev(i), 0)),
                  pl.BlockSpec((1, LANES), lambda i: (0, 0))],
        out_specs=[pl.BlockSpec((ATT_BLOCK, LANES), lambda i: (rev(i), 0)),
                   pl.BlockSpec((1, LANES), lambda i: (0, 0))],
        out_shape=[jax.ShapeDtypeStruct((s, LANES), F32), jax.ShapeDtypeStruct((1, LANES), F32)],
        scratch_shapes=[pltpu.VMEM((1, LANES), F32)], compiler_params=_params(),
    )(dcum_t, f, b_f)


def _causal(qi, ki):
    r = lax.broadcasted_iota(jnp.int32, (ATT_BLOCK, ATT_BLOCK), 0) + qi * ATT_BLOCK
    c = lax.broadcasted_iota(jnp.int32, (ATT_BLOCK, ATT_BLOCK), 1) + ki * ATT_BLOCK
    return c <= r


def _head_mask():
    return lax.broadcasted_iota(jnp.int32, (1, LANES), 1) < FOX_HEAD_DIM


def _attn_fwd(qkv, cum_b, cum_r, *, name):
    s = qkv.shape[0]
    nq = s // ATT_BLOCK
    scale = FOX_HEAD_DIM ** -0.5
    npair = HEAD_PAIRS

    def body(q_ref, k_ref, v_ref, cq_ref, ck_ref, o_ref, l_ref):
        qi = pl.program_id(1)
        m0 = _head_mask()
        q2 = q_ref[...]
        zero = jnp.zeros_like(q2)
        qs = (jnp.where(m0, q2, zero), jnp.where(m0, zero, q2))
        cqs = (cq_ref[0], cq_ref[1])

        def step(ki, carry, masked):
            off = pl.multiple_of(ki * ATT_BLOCK, ATT_BLOCK)
            k2 = k_ref[pl.ds(off, ATT_BLOCK), :]
            v2 = v_ref[pl.ds(off, ATT_BLOCK), :]
            out = []
            for hh in range(2):
                m, l, acc = carry[hh]
                sc = _dot(qs[hh], k2, _NT) * scale + (cqs[hh] - ck_ref[hh:hh + 1, pl.ds(off, ATT_BLOCK)])
                if masked:
                    sc = jnp.where(_causal(qi, ki), sc, -1e30)
                m_new = jnp.maximum(m, jnp.max(sc, axis=-1, keepdims=True))
                alpha = jnp.exp(m - m_new)
                p = jnp.exp(sc - m_new)
                l = alpha * l + jnp.sum(p, axis=-1, keepdims=True)
                acc = alpha * acc + _dot(p, v2, _NN)
                out.append((m_new, l, acc))
            return tuple(out)

        init = tuple((jnp.full((ATT_BLOCK, 1), -1e30, F32), jnp.zeros((ATT_BLOCK, 1), F32),
                      jnp.zeros((ATT_BLOCK, LANES), F32)) for _ in range(2))
        carry = lax.fori_loop(0, qi, lambda ki, c: step(ki, c, False), init)
        (ma, la, acca), (mb, lb, accb) = step(qi, carry, True)
        o_ref[...] = jnp.where(m0, acca / la, accb / lb).astype(o_ref.dtype)
        l_ref[0] = jnp.broadcast_to(ma + jnp.log(la), (ATT_BLOCK, LANES))
        l_ref[1] = jnp.broadcast_to(mb + jnp.log(lb), (ATT_BLOCK, LANES))

    stat = pl.BlockSpec((None, 2, ATT_BLOCK, LANES), lambda j, i: (j, 0, i, 0))
    row = pl.BlockSpec((None, 2, s), lambda j, i: (j, 0, 0))
    return pl.pallas_call(
        body, name=name, grid=(npair, nq),
        in_specs=[pl.BlockSpec((ATT_BLOCK, LANES), lambda j, i: (i, j)),
                  pl.BlockSpec((s, LANES), lambda j, i: (0, npair + j)),
                  pl.BlockSpec((s, LANES), lambda j, i: (0, 2 * npair + j)),
                  stat, row],
        out_specs=[pl.BlockSpec((ATT_BLOCK, LANES), lambda j, i: (i, j)), stat],
        out_shape=[jax.ShapeDtypeStruct((s, FOX_WIDTH), BF16),
                   jax.ShapeDtypeStruct((npair, 2, s, LANES), F32)],
        compiler_params=_params(),
    )(qkv, qkv, qkv, cum_b, cum_r)


def _attn_delta(qkv, do, lse_b, cum_b, cum_r, *, name):
    s = qkv.shape[0]
    nq = s // ATT_BLOCK
    scale = FOX_HEAD_DIM ** -0.5
    npair = HEAD_PAIRS

    def body(q_ref, k_ref, v_ref, do_ref, l_ref, cq_ref, ck_ref, d_ref):
        qi = pl.program_id(1)
        m0 = _head_mask()
        q2 = q_ref[...]
        do2 = do_ref[...]
        qs = (jnp.where(m0, q2, jnp.zeros_like(q2)), jnp.where(m0, jnp.zeros_like(q2), q2))
        dos = (jnp.where(m0, do2, jnp.zeros_like(do2)), jnp.where(m0, jnp.zeros_like(do2), do2))

        def step(ki, carry, masked):
            off = pl.multiple_of(ki * ATT_BLOCK, ATT_BLOCK)
            k2 = k_ref[pl.ds(off, ATT_BLOCK), :]
            v2 = v_ref[pl.ds(off, ATT_BLOCK), :]
            out = []
            for hh in range(2):
                sc = _dot(qs[hh], k2, _NT) * scale + (cq_ref[hh] - ck_ref[hh:hh + 1, pl.ds(off, ATT_BLOCK)])
                p = jnp.exp(sc - l_ref[hh])
                if masked:
                    p = jnp.where(_causal(qi, ki), p, 0.0)
                out.append(carry[hh] + jnp.sum(p * _dot(dos[hh], v2, _NT), axis=-1, keepdims=True))
            return tuple(out)

        init = (jnp.zeros((ATT_BLOCK, 1), F32), jnp.zeros((ATT_BLOCK, 1), F32))
        carry = lax.fori_loop(0, qi, lambda ki, c: step(ki, c, False), init)
        da, db = step(qi, carry, True)
        d_ref[0] = jnp.broadcast_to(da, (ATT_BLOCK, LANES))
        d_ref[1] = jnp.broadcast_to(db, (ATT_BLOCK, LANES))

    stat = pl.BlockSpec((None, 2, ATT_BLOCK, LANES), lambda j, i: (j, 0, i, 0))
    return pl.pallas_call(
        body, name=name, grid=(npair, nq),
        in_specs=[pl.BlockSpec((ATT_BLOCK, LANES), lambda j, i: (i, j)),
                  pl.BlockSpec((s, LANES), lambda j, i: (0, npair + j)),
                  pl.BlockSpec((s, LANES), lambda j, i: (0, 2 * npair + j)),
                  pl.BlockSpec((ATT_BLOCK, LANES), lambda j, i: (i, j)),
                  stat, stat, pl.BlockSpec((None, 2, s), lambda j, i: (j, 0, 0))],
        out_specs=stat,
        out_shape=jax.ShapeDtypeStruct((npair, 2, s, LANES), F32), compiler_params=_params(),
    )(qkv, qkv, qkv, do, lse_b, cum_b, cum_r)


def _attn_bwd(qkv, do, lse_b, delta_b, cum_b, cum_r, *, name):
    s = qkv.shape[0]
    nq = s // ATT_BLOCK
    scale = FOX_HEAD_DIM ** -0.5
    npair = HEAD_PAIRS

    def body(q_ref, k_ref, v_ref, do_ref, l_ref, dl_ref, cq_ref, ck_ref, dq_ref, dk_ref, dv_ref, dc_ref):
        ki = pl.program_id(1)
        m0 = _head_mask()
        k2 = k_ref[...]
        v2 = v_ref[...]
        koff = pl.multiple_of(ki * ATT_BLOCK, ATT_BLOCK)

        @pl.when(ki == 0)
        def _():
            dq_ref[...] = jnp.zeros_like(dq_ref)

        def step(qi, carry, masked):
            off = pl.multiple_of(qi * ATT_BLOCK, ATT_BLOCK)
            q2 = q_ref[pl.ds(off, ATT_BLOCK), :]
            do2 = do_ref[pl.ds(off, ATT_BLOCK), :]
            qzero = jnp.zeros_like(q2)
            dzero = jnp.zeros_like(do2)
            out = []
            dqs = []
            for hh in range(2):
                dk_acc, dv_acc, dc_acc = carry[hh]
                keep = m0 if hh == 0 else jnp.logical_not(m0)
                qh = jnp.where(keep, q2, qzero)
                doh = jnp.where(keep, do2, dzero)
                sc = _dot(qh, k2, _NT) * scale + (cq_ref[hh, pl.ds(off, ATT_BLOCK), :]
                                                 - ck_ref[hh:hh + 1, pl.ds(koff, ATT_BLOCK)])
                p = jnp.exp(sc - l_ref[hh, pl.ds(off, ATT_BLOCK), :])
                if masked:
                    p = jnp.where(_causal(qi, ki), p, 0.0)
                dp = _dot(doh, v2, _NT)
                ds = p * (dp - dl_ref[hh, pl.ds(off, ATT_BLOCK), :])
                dv_acc = dv_acc + _dot(p, do2, _TN)
                dk_acc = dk_acc + _dot(ds, q2, _TN)
                dc_acc = dc_acc - jnp.sum(ds, axis=0, keepdims=True)
                dqs.append(_dot(ds, k2, _NN))
                out.append((dk_acc, dv_acc, dc_acc))
            dq_ref[pl.ds(off, ATT_BLOCK), :] += jnp.where(m0, dqs[0], dqs[1]) * scale
            return tuple(out)

        init = tuple((jnp.zeros((ATT_BLOCK, LANES), F32), jnp.zeros((ATT_BLOCK, LANES), F32),
                      jnp.zeros((1, ATT_BLOCK), F32)) for _ in range(2))
        carry = step(ki, init, True)
        (dka, dva, dca), (dkb, dvb, dcb) = lax.fori_loop(ki + 1, nq, lambda qi, c: step(qi, c, False), carry)
        dk_ref[...] = (jnp.where(m0, dka, dkb) * scale).astype(dk_ref.dtype)
        dv_ref[...] = jnp.where(m0, dva, dvb).astype(dv_ref.dtype)
        dc_ref[0:1, :] = dca
        dc_ref[1:2, :] = dcb

    stat = pl.BlockSpec((None, 2, s, LANES), lambda j, i: (j, 0, 0, 0))
    colfull = lambda base: pl.BlockSpec((s, LANES), lambda j, i: (0, base + j))
    colblk = lambda base: pl.BlockSpec((ATT_BLOCK, LANES), lambda j, i: (i, base + j))
    return pl.pallas_call(
        body, name=name, grid=(npair, nq),
        in_specs=[colfull(0), colblk(npair), colblk(2 * npair), colfull(0), stat, stat, stat,
                  pl.BlockSpec((None, 2, s), lambda j, i: (j, 0, 0))],
        out_specs=[colfull(0), colblk(0), colblk(0), pl.BlockSpec((None, 2, ATT_BLOCK), lambda j, i: (j, 0, i))],
        out_shape=[jax.ShapeDtypeStruct((s, FOX_WIDTH), F32), jax.ShapeDtypeStruct((s, FOX_WIDTH), BF16),
                   jax.ShapeDtypeStruct((s, FOX_WIDTH), BF16), jax.ShapeDtypeStruct((npair, 2, s), F32)],
        compiler_params=_params(),
    )(qkv, qkv, qkv, do, lse_b, delta_b, cum_b, cum_r)


ATT_TQ = 256
ATT_TK = 256
ATT_SCALE = FOX_HEAD_DIM ** -0.5
assert ATT_SCALE == 0.125 and ATT_TQ == ATT_TK


def _causal_t(qi, ki):
    kpos = lax.broadcasted_iota(jnp.int32, (ATT_TK, ATT_TQ), 0) + ki * ATT_TK
    qpos = lax.broadcasted_iota(jnp.int32, (ATT_TK, ATT_TQ), 1) + qi * ATT_TQ
    return kpos <= qpos


def _row_mask():
    return lax.broadcasted_iota(jnp.int32, (LANES, 1), 0) < FOX_HEAD_DIM


def _lane_tile(a, width):
    return a if a.shape[1] == width else jnp.tile(a, (1, width // a.shape[1]))


def _transpose_bf16(a):
    return a.astype(F32).T.astype(BF16)


def _attn_fwd_t(qkv, cum_b, cum_r, *, name):
    s = qkv.shape[0]
    nq = s // ATT_TQ
    npair = HEAD_PAIRS

    def body(q_ref, k_ref, v_ref, cq_ref, ck_ref, o_ref, ot_ref, l_ref, vt_ref):
        qi = pl.program_id(1)
        rows = _row_mask()

        @pl.when(qi == 0)
        def _():
            vt_ref[...] = _transpose_bf16(v_ref[...])

        qt = _transpose_bf16(q_ref[...]) * ATT_SCALE
        zero = jnp.zeros_like(qt)
        qts = (jnp.where(rows, qt, zero), jnp.where(rows, zero, qt))

        def step(ki, carry, masked):
            off = pl.multiple_of(ki * ATT_TK, ATT_TK)
            k2 = k_ref[pl.ds(off, ATT_TK), :]
            vt = vt_ref[:, pl.ds(off, ATT_TK)]
            out = []
            for hh in range(2):
                m, l, acc = carry[hh]
                bias = cq_ref[hh:hh + 1, :] - _lane_tile(ck_ref[hh, pl.ds(off, ATT_TK), :], ATT_TQ)
                sc = _dot(k2, qts[hh], _NN) + bias
                if masked:
                    sc = jnp.where(_causal_t(qi, ki), sc, -1e30)
                m_new = jnp.maximum(m, jnp.max(sc, axis=0, keepdims=True))
                alpha = jnp.exp(m - m_new)
                p = jnp.exp(sc - m_new)
                l = alpha * l + jnp.sum(p, axis=0, keepdims=True)
                p_hi = p.astype(BF16)
                p_lo = (p - p_hi.astype(F32)).astype(BF16)
                acc = alpha * acc + (_dot(vt, p_hi, _NN) + _dot(vt, p_lo, _NN))
                out.append((m_new, l, acc))
            return tuple(out)

        init = tuple((jnp.full((1, ATT_TQ), -1e30, F32), jnp.zeros((1, ATT_TQ), F32),
                      jnp.zeros((LANES, ATT_TQ), F32)) for _ in range(2))
        carry = lax.fori_loop(0, qi, lambda ki, c: step(ki, c, False), init)
        (ma, la, acca), (mb, lb, accb) = step(qi, carry, True)
        ot = jnp.where(rows, acca / la, accb / lb)
        ot_ref[...] = ot
        o_ref[...] = ot.T.astype(o_ref.dtype)
        l_ref[0:1, :] = ma + jnp.log(la)
        l_ref[1:2, :] = mb + jnp.log(lb)

    row = pl.BlockSpec((None, 2, ATT_TQ), lambda j, i: (j, 0, i))
    return pl.pallas_call(
        body, name=name, grid=(npair, nq),
        in_specs=[pl.BlockSpec((ATT_TQ, LANES), lambda j, i: (i, j)),
                  pl.BlockSpec((s, LANES), lambda j, i: (0, npair + j)),
                  pl.BlockSpec((s, LANES), lambda j, i: (0, 2 * npair + j)),
                  row, pl.BlockSpec((None, 2, s, LANES), lambda j, i: (j, 0, 0, 0))],
        out_specs=[pl.BlockSpec((ATT_TQ, LANES), lambda j, i: (i, j)),
                   pl.BlockSpec((LANES, ATT_TQ), lambda j, i: (j, i)), row],
        out_shape=[jax.ShapeDtypeStruct((s, FOX_WIDTH), BF16), jax.ShapeDtypeStruct((FOX_WIDTH, s), F32),
                   jax.ShapeDtypeStruct((npair, 2, s), F32)],
        scratch_shapes=[pltpu.VMEM((LANES, s), BF16)],
        compiler_params=_params(),
    )(qkv, qkv, qkv, cum_r, cum_b)


def _attn_delta_t(do_t, o_t, *, name):
    s = o_t.shape[1]
    ts = _tile(s, 512)

    def body(do_ref, o_ref, d_ref):
        prod = do_ref[...].astype(F32) * o_ref[...]
        d_ref[0:1, :] = jnp.sum(prod[:FOX_HEAD_DIM], axis=0, keepdims=True)
        d_ref[1:2, :] = jnp.sum(prod[FOX_HEAD_DIM:], axis=0, keepdims=True)

    blk = pl.BlockSpec((LANES, ts), lambda j, i: (j, i))
    return pl.pallas_call(
        body, name=name, grid=(HEAD_PAIRS, s // ts), in_specs=[blk, blk],
        out_specs=pl.BlockSpec((None, 2, ts), lambda j, i: (j, 0, i)),
        out_shape=jax.ShapeDtypeStruct((HEAD_PAIRS, 2, s), F32), compiler_params=_params(),
    )(do_t, o_t)


def _attn_bwd_t(qkv, do, o_t, lse, cum_b, cum_r, *, name):
    s = qkv.shape[0]
    nq = s // ATT_TQ
    npair = HEAD_PAIRS

    def body(q_ref, k_ref, v_ref, do_ref, ot_ref, l_ref, cq_ref, ck_ref, dq_ref, dk_ref, dv_ref, dc_ref,
             qt_ref, dot_ref, dqt_ref, dl_ref):
        ki = pl.program_id(1)
        m0 = _head_mask()
        rows = _row_mask()
        k2 = k_ref[...]
        v2 = v_ref[...]
        kt = _transpose_bf16(k2)
        ks = k2 * ATT_SCALE
        kz, vz = jnp.zeros_like(k2), jnp.zeros_like(v2)
        khs = (jnp.where(m0, ks, kz), jnp.where(m0, kz, ks))
        vhs = (jnp.where(m0, v2, vz), jnp.where(m0, vz, v2))
        cks = tuple(_lane_tile(ck_ref[hh], ATT_TQ) for hh in range(2))

        @pl.when(ki == 0)
        def _():
            dqt_ref[...] = jnp.zeros_like(dqt_ref)
            qt_ref[...] = _transpose_bf16(q_ref[...])
            do_t = do_ref[...].astype(F32).T
            dot_ref[...] = do_t.astype(BF16)
            prod = do_t * ot_ref[...]
            dl_ref[0:1, :] = jnp.sum(prod[:FOX_HEAD_DIM], axis=0, keepdims=True)
            dl_ref[1:2, :] = jnp.sum(prod[FOX_HEAD_DIM:], axis=0, keepdims=True)

        def step(qi, carry, masked):
            off = pl.multiple_of(qi * ATT_TQ, ATT_TQ)
            q2 = q_ref[pl.ds(off, ATT_TQ), :]
            do2 = do_ref[pl.ds(off, ATT_TQ), :]
            qt = qt_ref[:, pl.ds(off, ATT_TQ)]
            dot_ = dot_ref[:, pl.ds(off, ATT_TQ)]
            out, dqs = [], []
            for hh in range(2):
                dk_acc, dv_acc, dc_acc = carry[hh]
                sc = _dot(khs[hh], qt, _NN) + (cq_ref[hh:hh + 1, pl.ds(off, ATT_TQ)] - cks[hh])
                p = jnp.exp(sc - l_ref[hh:hh + 1, pl.ds(off, ATT_TQ)])
                if masked:
                    p = jnp.where(_causal_t(qi, ki), p, 0.0)
                dp = _dot(vhs[hh], dot_, _NN)
                ds = p * (dp - dl_ref[hh:hh + 1, pl.ds(off, ATT_TQ)])
                dc_acc = dc_acc - jnp.sum(ds, axis=1, keepdims=True)
                dss = (ds * ATT_SCALE).astype(BF16)
                dv_acc = dv_acc + _dot(p, do2, _NN)
                dk_acc = dk_acc + _dot(dss, q2, _NN)
                dqs.append(_dot(kt, dss, _NN))
                out.append((dk_acc, dv_acc, dc_acc))
            dqt_ref[:, pl.ds(off, ATT_TQ)] += jnp.where(rows, dqs[0], dqs[1])
            return tuple(out)

        init = tuple((jnp.zeros((ATT_TK, LANES), F32), jnp.zeros((ATT_TK, LANES), F32),
                      jnp.zeros((ATT_TK, 1), F32)) for _ in range(2))
        carry = step(ki, init, True)
        (dka, dva, dca), (dkb, dvb, dcb) = lax.fori_loop(ki + 1, nq, lambda qi, c: step(qi, c, False), carry)
        dk_ref[...] = jnp.where(m0, dka, dkb).astype(dk_ref.dtype)
        dv_ref[...] = jnp.where(m0, dva, dvb).astype(dv_ref.dtype)
        dc_ref[0] = jnp.broadcast_to(dca, (ATT_TK, LANES))
        dc_ref[1] = jnp.broadcast_to(dcb, (ATT_TK, LANES))

        @pl.when(ki == nq - 1)
        def _():
            dq_ref[...] = dqt_ref[...].T.astype(dq_ref.dtype)

    colfull = lambda base: pl.BlockSpec((s, LANES), lambda j, i: (0, base + j))
    colblk = lambda base: pl.BlockSpec((ATT_TK, LANES), lambda j, i: (i, base + j))
    stat = pl.BlockSpec((None, 2, s), lambda j, i: (j, 0, 0))
    bcast = pl.BlockSpec((None, 2, ATT_TK, LANES), lambda j, i: (j, 0, i, 0))
    grad = jax.ShapeDtypeStruct((s, FOX_WIDTH), BF16)
    return pl.pallas_call(
        body, name=name, grid=(npair, nq),
        in_specs=[colfull(0), colblk(npair), colblk(2 * npair), colfull(0),
                  pl.BlockSpec((LANES, s), lambda j, i: (j, 0)), stat, stat, bcast],
        out_specs=[colfull(0), colblk(0), colblk(0), bcast],
        out_shape=[grad, grad, grad, jax.ShapeDtypeStruct((npair, 2, s, LANES), F32)],
        scratch_shapes=[pltpu.VMEM((LANES, s), BF16), pltpu.VMEM((LANES, s), BF16), pltpu.VMEM((LANES, s), F32),
                        pltpu.VMEM((2, s), F32)],
        compiler_params=_params(),
    )(qkv, qkv, qkv, do, o_t, lse, cum_r, cum_b)


def _merge_fwd(zg, ya, yb, *, name, tm=256):
    s, d = ya.shape
    tm = _tile(s, tm)

    def body(zg_ref, ya_ref, yb_ref, m_ref):
        ga = _sigmoid(zg_ref[:, :d].astype(F32))
        gb = _sigmoid(zg_ref[:, d:].astype(F32))
        m_ref[...] = (ga * ya_ref[...].astype(F32) + gb * yb_ref[...].astype(F32)).astype(m_ref.dtype)

    row = pl.BlockSpec((tm, d), lambda i: (i, 0))
    row2 = pl.BlockSpec((tm, 2 * d), lambda i: (i, 0))
    return pl.pallas_call(
        body, name=name, grid=(s // tm,), in_specs=[row2, row, row], out_specs=row,
        out_shape=jax.ShapeDtypeStruct((s, d), BF16), compiler_params=_params(),
    )(zg, ya, yb)


def _merge_bwd(dm, zg, ya, yb, *, name, tm=256):
    s, d = ya.shape
    tm = _tile(s, tm)

    def body(dm_ref, zg_ref, ya_ref, yb_ref, dzg_ref, dya_ref, dyb_ref):
        dmv = dm_ref[...].astype(F32)
        ga = _sigmoid(zg_ref[:, :d].astype(F32))
        gb = _sigmoid(zg_ref[:, d:].astype(F32))
        dzg_ref[:, :d] = (dmv * ya_ref[...].astype(F32) * ga * (1.0 - ga)).astype(dzg_ref.dtype)
        dzg_ref[:, d:] = (dmv * yb_ref[...].astype(F32) * gb * (1.0 - gb)).astype(dzg_ref.dtype)
        dya_ref[...] = (dmv * ga).astype(dya_ref.dtype)
        dyb_ref[...] = (dmv * gb).astype(dyb_ref.dtype)

    row = pl.BlockSpec((tm, d), lambda i: (i, 0))
    row2 = pl.BlockSpec((tm, 2 * d), lambda i: (i, 0))
    return pl.pallas_call(
        body, name=name, grid=(s // tm,), in_specs=[row, row2, row, row], out_specs=[row2, row, row],
        out_shape=[jax.ShapeDtypeStruct((s, 2 * d), BF16), jax.ShapeDtypeStruct((s, d), BF16),
                   jax.ShapeDtypeStruct((s, d), BF16)],
        compiler_params=_params(),
    )(dm, zg, ya, yb)


def _shift_down(u, k, row):
    return jnp.where(row >= k, pltpu.roll(u, k, 0), 0.0)


def _shift_up(u, k, row):
    n = u.shape[0]
    return jnp.where(row < n - k, pltpu.roll(u, n - k, 0), 0.0)


def _conv_act_fwd(up_a, up_b, cw_a, cw_b, cb_a, cb_b, *, name, tc=128):
    s, f = up_a.shape
    tc = _tile(f, tc)

    def body(ua_ref, ub_ref, wa_ref, wb_ref, ba_ref, bb_ref, act_ref):
        row = lax.broadcasted_iota(jnp.int32, (s, tc), 0)

        def conv(u_ref, w_ref, b_ref):
            u = u_ref[...].astype(F32)
            return (b_ref[...] + w_ref[0:1, :] * _shift_down(u, 2, row)
                    + w_ref[1:2, :] * _shift_down(u, 1, row) + w_ref[2:3, :] * u)

        ca = conv(ua_ref, wa_ref, ba_ref)
        cb = conv(ub_ref, wb_ref, bb_ref)
        act_ref[...] = (_gelu(ca) * cb).astype(act_ref.dtype)

    col = pl.BlockSpec((s, tc), lambda j: (0, j))
    w3 = pl.BlockSpec((3, tc), lambda j: (0, j))
    b1 = pl.BlockSpec((1, tc), lambda j: (0, j))
    return pl.pallas_call(
        body, name=name, grid=(f // tc,), in_specs=[col, col, w3, w3, b1, b1], out_specs=col,
        out_shape=jax.ShapeDtypeStruct((s, f), BF16), compiler_params=_params(),
    )(up_a, up_b, cw_a, cw_b, cb_a, cb_b)


def _conv_act_bwd(up_a, up_b, dact, cw_a, cw_b, cb_a, cb_b, *, name, tc=128):
    s, f = up_a.shape
    tc = _tile(f, tc)

    def body(ua_ref, ub_ref, da_ref, wa_ref, wb_ref, ba_ref, bb_ref, dua_ref, dub_ref, dwa_ref, dwb_ref):
        row = lax.broadcasted_iota(jnp.int32, (s, tc), 0)

        def conv(u_ref, w_ref, b_ref):
            u = u_ref[...].astype(F32)
            u1 = _shift_down(u, 1, row)
            u2 = _shift_down(u, 2, row)
            return u, u1, u2, b_ref[...] + w_ref[0:1, :] * u2 + w_ref[1:2, :] * u1 + w_ref[2:3, :] * u

        def back(dc, taps, w_ref, du_ref, dw_ref):
            u, u1, u2 = taps
            dw_ref[0:1, :] = jnp.sum(dc * u2, axis=0, keepdims=True)
            dw_ref[1:2, :] = jnp.sum(dc * u1, axis=0, keepdims=True)
            dw_ref[2:3, :] = jnp.sum(dc * u, axis=0, keepdims=True)
            dw_ref[3:4, :] = jnp.sum(dc, axis=0, keepdims=True)
            du = (w_ref[2:3, :] * dc + w_ref[1:2, :] * _shift_up(dc, 1, row)
                  + w_ref[0:1, :] * _shift_up(dc, 2, row))
            du_ref[...] = du.astype(du_ref.dtype)

        ua, ua1, ua2, ca = conv(ua_ref, wa_ref, ba_ref)
        ub, ub1, ub2, cb = conv(ub_ref, wb_ref, bb_ref)
        g, dg = _gelu_and_grad(ca)
        dact_v = da_ref[...].astype(F32)
        back(dact_v * cb * dg, (ua, ua1, ua2), wa_ref, dua_ref, dwa_ref)
        back(dact_v * g, (ub, ub1, ub2), wb_ref, dub_ref, dwb_ref)

    col = pl.BlockSpec((s, tc), lambda j: (0, j))
    w3 = pl.BlockSpec((3, tc), lambda j: (0, j))
    w4 = pl.BlockSpec((4, tc), lambda j: (0, j))
    b1 = pl.BlockSpec((1, tc), lambda j: (0, j))
    return pl.pallas_call(
        body, name=name, grid=(f // tc,), in_specs=[col, col, col, w3, w3, b1, b1],
        out_specs=[col, col, w4, w4],
        out_shape=[jax.ShapeDtypeStruct((s, f), BF16), jax.ShapeDtypeStruct((s, f), BF16),
                   jax.ShapeDtypeStruct((4, f), F32), jax.ShapeDtypeStruct((4, f), F32)],
        compiler_params=_params(),
    )(up_a, up_b, dact, cw_a, cw_b, cb_a, cb_b)


def _ple_final(x2, ple, zp, target, g_final, *, name, tm=256):
    s, d = x2.shape
    tm = _tile(s, tm)

    def body(x_ref, ple_ref, zp_ref, t_ref, g_ref, dx_ref, dple_ref, dzp_ref, dg_ref, loss_ref):
        @pl.when(pl.program_id(0) == 0)
        def _():
            dg_ref[...] = jnp.zeros_like(dg_ref)
            loss_ref[...] = jnp.zeros_like(loss_ref)

        gp = _sigmoid(zp_ref[...].astype(F32))
        plev = ple_ref[...].astype(F32)
        x3 = x_ref[...] + plev * gp
        r = lax.rsqrt(jnp.mean(x3 * x3, axis=-1, keepdims=True) + EPS)
        xhat = x3 * r
        gv = g_ref[...]
        diff = xhat * gv - t_ref[...]
        loss_ref[...] += 0.5 * jnp.sum(jnp.mean(diff * diff, axis=-1, keepdims=True), axis=0, keepdims=True)
        dy = diff * (1.0 / d)
        dg_ref[...] += jnp.sum(dy * xhat, axis=0, keepdims=True)
        dyg = dy * gv
        dx3 = r * (dyg - xhat * jnp.mean(dyg * xhat, axis=-1, keepdims=True))
        dx_ref[...] = dx3
        dple_ref[...] = (dx3 * gp).astype(dple_ref.dtype)
        dzp_ref[...] = (dx3 * plev * gp * (1.0 - gp)).astype(dzp_ref.dtype)

    row = pl.BlockSpec((tm, d), lambda i: (i, 0))
    vec = pl.BlockSpec((1, d), lambda i: (0, 0))
    return pl.pallas_call(
        body, name=name, grid=(s // tm,), in_specs=[row, row, row, row, vec],
        out_specs=[row, row, row, vec, pl.BlockSpec((1, LANES), lambda i: (0, 0))],
        out_shape=[jax.ShapeDtypeStruct((s, d), F32), jax.ShapeDtypeStruct((s, d), BF16),
                   jax.ShapeDtypeStruct((s, d), BF16), jax.ShapeDtypeStruct((1, d), F32),
                   jax.ShapeDtypeStruct((1, LANES), F32)],
        compiler_params=_params(),
    )(x2, ple, zp, target, g_final)


def _device_step(x, p, target, w, get_w_in=None, get_w_rest=None, on_grads_ffn=None, on_grads_mix=None):
    s = x.shape[0]
    g = {}
    w = dict(w)

    h = _rms_fwd(x, w["norm_mix_g"], name="rms_mix")
    if get_w_in is not None:
        w.update(get_w_in(h))
    z_uv = _mm(h, w["w_uv"], mode="nn", out_dtype=BF16, name="proj_uv", tm=1024)
    qkv = _mm(h, w["w_qkv"], mode="nn", out_dtype=BF16, name="proj_qkv", tm=1024)
    zg = _mm(h, w["w_g"], mode="nn", out_dtype=BF16, name="proj_gate", tm=1024)
    f = _mm(h, w["w_f"], mode="nn", out_dtype=F32, name="proj_f", tm=1024)

    a = _gmlp_fwd(z_uv, w["gmlp_ln_g"], w["gmlp_ln_b"], w["gmlp_w_s"], w["gmlp_b_s_t"], name="gmlp_fwd")

    cum_b, cum_t = _fox_cum(f, w["b_f"], name="fox_cum")
    cum_b = cum_b.reshape(HEAD_PAIRS, 2, s, LANES)
    cum_r = cum_t[:FOX_HEADS].reshape(HEAD_PAIRS, 2, s)
    b, o_t, lse = _attn_fwd_t(qkv, cum_b, cum_r, name="attn_fwd")
    if get_w_rest is not None:
        w.update(get_w_rest(b))

    ya = _mm(a, w["w_branch_a"], mode="nn", out_dtype=BF16, name="branch_a", tm=1024)
    yb = _mm(b, w["w_branch_b"], mode="nn", out_dtype=BF16, name="branch_b", tm=1024)
    merged = _merge_fwd(zg, ya, yb, name="merge_fwd")
    x1 = _mm(merged, w["w_out"], mode="nn", out_dtype=F32, name="proj_out", add=x, tm=1024)

    h2 = _rms_fwd(x1, w["norm_ffn_g"], name="rms_ffn")
    up_a = _mm(h2, w["w_up_a"], mode="nn", out_dtype=BF16, name="up_a", tm=1024, tn=D_FF // 2)
    up_b = _mm(h2, w["w_up_b"], mode="nn", out_dtype=BF16, name="up_b", tm=1024, tn=D_FF // 2)
    cw, cb = w["conv_w"], w["conv_b"]
    conv_args = (cw[:, :D_FF], cw[:, D_FF:], cb[:, :D_FF], cb[:, D_FF:])
    act = _conv_act_fwd(up_a, up_b, *conv_args, name="conv_act_fwd")
    x2 = _mm(act, w["w_down"], mode="nn", out_dtype=F32, name="down", add=x1, tm=512)

    h3 = _rms_fwd(x2, w["norm_ple_g"], name="rms_ple")
    ple = _mm(p, w["w_ple"], mode="nn", out_dtype=BF16, name="ple_proj", tm=1024)
    zp = _mm(h3, w["w_ple_gate"], mode="nn", out_dtype=BF16, name="ple_gate", tm=1024)
    dx3, dple, dzp, g["norm_final_g"], loss = _ple_final(x2, ple, zp, target, w["norm_final_g"], name="ple_final")

    g["w_ple"] = _mm(p, dple, mode="tn", out_dtype=BF16, name="dw_ple")
    g["w_ple_gate"] = _mm(h3, dzp, mode="tn", out_dtype=BF16, name="dw_ple_gate")
    dh3 = _mm(dzp, w["w_ple_gate"], mode="nt", out_dtype=BF16, name="dh3")
    dx2, dx2_b, g["norm_ple_g"] = _rms_bwd(x2, w["norm_ple_g"], dh3, dx3, name="rms_ple_bwd")

    g["w_down"] = _mm(act, dx2_b, mode="tn", out_dtype=BF16, name="dw_down", tm=D_FF // 2)
    dact = _mm(dx2_b, w["w_down"], mode="nt", out_dtype=BF16, name="dact", tn=D_FF // 2)
    dup_a, dup_b, dcw_a, dcw_b = _conv_act_bwd(up_a, up_b, dact, *conv_args, name="conv_act_bwd")
    g["conv_w"] = jnp.concatenate([dcw_a[:3], dcw_b[:3]], axis=1)
    g["conv_b"] = jnp.concatenate([dcw_a[3:], dcw_b[3:]], axis=1)
    g["w_up_a"] = _mm(h2, dup_a, mode="tn", out_dtype=BF16, name="dw_up_a", tn=D_FF // 2)
    g["w_up_b"] = _mm(h2, dup_b, mode="tn", out_dtype=BF16, name="dw_up_b", tn=D_FF // 2)
    dh2 = _mm(dup_a, w["w_up_a"], mode="nt", out_dtype=F32, name="dh2_a")
    dh2 = _mm(dup_b, w["w_up_b"], mode="nt", out_dtype=BF16, name="dh2_b", add=dh2)
    dx1, dx1_b, g["norm_ffn_g"] = _rms_bwd(x1, w["norm_ffn_g"], dh2, dx2, name="rms_ffn_bwd")
    dep = on_grads_ffn(g) if on_grads_ffn is not None else None

    g["w_out"] = _mm(merged, dx1_b, mode="tn", out_dtype=BF16, name="dw_out")
    dmerged = _mm(dx1_b, w["w_out"], mode="nt", out_dtype=BF16, name="dmerged", dep=dep)
    dzg, dya, dyb = _merge_bwd(dmerged, zg, ya, yb, name="merge_bwd")
    g["w_branch_a"] = _mm(a, dya, mode="tn", out_dtype=BF16, name="dw_branch_a")
    g["w_branch_b"] = _mm(b, dyb, mode="tn", out_dtype=BF16, name="dw_branch_b")
    da = _mm(dya, w["w_branch_a"], mode="nt", out_dtype=BF16, name="da")
    db = _mm(dyb, w["w_branch_b"], mode="nt", out_dtype=BF16, name="db")

    dz_uv, g["gmlp_w_s"], dbs_t, g["gmlp_ln_g"], g["gmlp_ln_b"] = _gmlp_bwd(
        z_uv, da, w["gmlp_ln_g"], w["gmlp_ln_b"], w["gmlp_w_s"], w["gmlp_b_s_t"], name="gmlp_bwd")
    g["gmlp_b_s"] = dbs_t[:, :GMLP_GROUPS].T

    dq, dk, dv, dcum_b = _attn_bwd_t(qkv, db, o_t, lse, cum_b, cum_r, name="attn_bwd")
    dcum_t = jnp.pad(dcum_b[..., 0].reshape(FOX_HEADS, s), ((0, LANES - FOX_HEADS), (0, 0)))
    df, g["b_f"] = _fox_dlogit(dcum_t, f, w["b_f"], name="fox_dlogit")
    dqkv = jnp.concatenate([dq, dk, dv], axis=1)

    g["w_uv"] = _mm(h, dz_uv, mode="tn", out_dtype=BF16, name="dw_uv")
    g["w_qkv"] = _mm(h, dqkv, mode="tn", out_dtype=BF16, name="dw_qkv")
    g["w_f"] = _mm(h, df, mode="tn", out_dtype=BF16, name="dw_f")
    g["w_g"] = _mm(h, dzg, mode="tn", out_dtype=BF16, name="dw_g")
    dep = on_grads_mix(g) if on_grads_mix is not None else None
    dh = _mm(dz_uv, w["w_uv"], mode="nt", out_dtype=F32, name="dh_uv", dep=dep)
    dh = _mm(dqkv, w["w_qkv"], mode="nt", out_dtype=F32, name="dh_qkv", add=dh)
    dh = _mm(df, w["w_f"], mode="nt", out_dtype=F32, name="dh_f", add=dh)
    dh = _mm(dzg, w["w_g"], mode="nt", out_dtype=BF16, name="dh_g", add=dh)
    dx0, _, g["norm_mix_g"] = _rms_bwd(x, w["norm_mix_g"], dh, dx1, name="rms_mix_bwd")
    return loss, dx0, g


def _coords():
    return lax.axis_index("x"), lax.axis_index("y"), lax.axis_index("c")


def _other_chips(x, y):
    return [(1 - x, y), (x, 1 - y), (1 - x, 1 - y)]


def _remote(src, dst, send_sem, recv_sem, dev):
    return pltpu.make_async_remote_copy(src_ref=src, dst_ref=dst, send_sem=send_sem, recv_sem=recv_sem,
                                        device_id=dev, device_id_type=MESH)


_ANY = pl.BlockSpec(memory_space=pl.ANY)


def _gather_weights(halved, whole, *, name):
    nh, n = len(halved), len(halved) + len(whole)
    arrays = list(halved) + list(whole)

    def body(*refs):
        ins, outs = refs[:n], refs[n:2 * n]
        send_sems, recv_sems = refs[2 * n:]
        x, y, c = _coords()
        me, sib = 2 * x + y, (x, y, 1 - c)
        chips = _other_chips(x, y)

        def half(i, which):
            h = ins[i].shape[0] // 2
            return pl.ds(pl.multiple_of(which * h, 16), h)

        sends = []
        for i in range(n):
            src, dst = (ins[i].at[half(i, c)], outs[i].at[me, half(i, c)]) if i < nh else (ins[i], outs[i].at[me])
            for k, (cx, cy) in enumerate(chips):
                cp = _remote(src, dst, send_sems.at[i, k], recv_sems.at[i, k], (cx, cy, c))
                cp.start()
                sends.append(cp)
        for i in range(n):
            for k, (cx, cy) in enumerate(chips):
                got = outs[i].at[2 * cx + cy, half(i, c)] if i < nh else outs[i].at[2 * cx + cy]
                _remote(got, got, send_sems.at[i, k], recv_sems.at[i, k], sib).wait_recv()
                if i < nh:
                    cp = _remote(got, got, send_sems.at[i, 3 + k], recv_sems.at[i, 3 + k], sib)
                    cp.start()
                    sends.append(cp)
        for i in range(nh):
            for k, (cx, cy) in enumerate(chips):
                got = outs[i].at[2 * cx + cy, half(i, 1 - c)]
                _remote(got, got, send_sems.at[i, 3 + k], recv_sems.at[i, 3 + k], sib).wait_recv()
        for cp in sends:
            cp.wait_send()

    outs = pl.pallas_call(
        body, name=name, in_specs=[_ANY] * n, out_specs=[_ANY] * n,
        out_shape=[jax.ShapeDtypeStruct((N_CHIPS,) + a.shape, a.dtype) for a in arrays],
        scratch_shapes=[pltpu.SemaphoreType.DMA((n, 6)), pltpu.SemaphoreType.DMA((n, 6))],
        compiler_params=_params(),
    )(*arrays)
    chip = 2 * lax.axis_index("x") + lax.axis_index("y")
    return [lax.dynamic_update_index_in_dim(o, a, chip, 0) for o, a in zip(outs, arrays)]


def _pair_exchange(gs, *, name):
    n = len(gs)

    def body(*refs):
        ins, outs = refs[:n], refs[n:2 * n]
        send_sems, recv_sems = refs[2 * n:]
        x, y, c = _coords()
        copies = []
        for i in range(n):
            for j in range(N_CHIPS):
                cp = _remote(ins[i].at[j, 1 - c], outs[i].at[j], send_sems.at[i, j], recv_sems.at[i, j], (x, y, 1 - c))
                cp.start()
                copies.append(cp)
        for cp in copies:
            cp.wait()

    return pl.pallas_call(
        body, name=name, in_specs=[_ANY] * n, out_specs=[_ANY] * n,
        out_shape=[jax.ShapeDtypeStruct((N_CHIPS,) + a.shape[2:], a.dtype) for a in gs],
        scratch_shapes=[pltpu.SemaphoreType.DMA((n, N_CHIPS)), pltpu.SemaphoreType.DMA((n, N_CHIPS))],
        compiler_params=_params(),
    )(*gs)


def _chip_exchange(ss, *, name):
    n = len(ss)

    def body(*refs):
        ins, outs = refs[:n], refs[n:2 * n]
        send_sems, recv_sems = refs[2 * n:]
        x, y, c = _coords()
        me = 2 * x + y
        chips = _other_chips(x, y)
        sends = []
        for i in range(n):
            for k, (cx, cy) in enumerate(chips):
                cp = _remote(ins[i].at[2 * cx + cy], outs[i].at[me], send_sems.at[i, k], recv_sems.at[i, k], (cx, cy, c))
                cp.start()
                sends.append(cp)
        for i in range(n):
            for k, (cx, cy) in enumerate(chips):
                got = outs[i].at[2 * cx + cy]
                _remote(got, got, send_sems.at[i, k], recv_sems.at[i, k], (cx, cy, c)).wait_recv()
        for cp in sends:
            cp.wait_send()

    return pl.pallas_call(
        body, name=name, in_specs=[_ANY] * n, out_specs=[_ANY] * n,
        out_shape=[jax.ShapeDtypeStruct(a.shape, a.dtype) for a in ss],
        scratch_shapes=[pltpu.SemaphoreType.DMA((n, 3)), pltpu.SemaphoreType.DMA((n, 3))],
        compiler_params=_params(),
    )(*ss)


def _pair_share(hs, *, name):
    n = len(hs)

    def body(*refs):
        ins, outs = refs[:n], refs[n:2 * n]
        send_sems, recv_sems = refs[2 * n:]
        x, y, c = _coords()
        copies = []
        for i in range(n):
            cp = _remote(ins[i], outs[i], send_sems.at[i], recv_sems.at[i], (x, y, 1 - c))
            cp.start()
            copies.append(cp)
        for cp in copies:
            cp.wait()

    return pl.pallas_call(
        body, name=name, in_specs=[_ANY] * n, out_specs=[_ANY] * n,
        out_shape=[jax.ShapeDtypeStruct(a.shape, a.dtype) for a in hs],
        scratch_shapes=[pltpu.SemaphoreType.DMA((n,)), pltpu.SemaphoreType.DMA((n,))],
        compiler_params=_params(),
    )(*hs)


def _all_exchange(vec, *, name):
    def body(v_ref, o_ref, send_sems, recv_sems, local_sem):
        x, y, c = _coords()
        me = 4 * x + 2 * y + c
        local = pltpu.make_async_copy(v_ref, o_ref.at[me], local_sem)
        local.start()
        copies = []
        k = 0
        for dx in (0, 1):
            for dy in (0, 1):
                for dc in (0, 1):
                    if dx or dy or dc:
                        peer = (1 - x if dx else x, 1 - y if dy else y, 1 - c if dc else c)
                        cp = _remote(v_ref, o_ref.at[me], send_sems.at[k], recv_sems.at[k], peer)
                        cp.start()
                        copies.append(cp)
                        k += 1
        for cp in copies:
            cp.wait()
        local.wait()

    return pl.pallas_call(
        body, name=name, in_specs=[_ANY], out_specs=_ANY,
        out_shape=jax.ShapeDtypeStruct((8,) + vec.shape, vec.dtype),
        scratch_shapes=[pltpu.SemaphoreType.DMA((7,)), pltpu.SemaphoreType.DMA((7,)), pltpu.SemaphoreType.DMA(())],
        compiler_params=_params(),
    )(vec)


_HBM = pl.BlockSpec(memory_space=pltpu.HBM)
_SEM = pl.BlockSpec(memory_space=pltpu.SEMAPHORE)
_EFFECT = pltpu.SideEffectType.DATAFLOW_SIDE_EFFECTING


def _copies_start(srcs, lands, plan, n_copies, *, name):
    ns, n = len(srcs), len(srcs) + len(lands)

    def body(*refs):
        send_sems, recv_sems = refs[n], refs[n + 1]
        token = refs[-1]
        for k, (src, dst, dev) in enumerate(plan(refs[:ns], refs[ns:n])):
            _remote(src, dst, send_sems.at[k], recv_sems.at[k], dev).start()
        token[...] = jnp.zeros_like(token)

    arrays = list(srcs) + list(lands)
    outs = pl.pallas_call(
        body, name=name,
        out_shape=(pltpu.SemaphoreType.DMA((n_copies,)), pltpu.SemaphoreType.DMA((n_copies,)),
                   *[pltpu.HBM(a.shape, a.dtype) for a in arrays], jax.ShapeDtypeStruct((8, LANES), F32)),
        in_specs=[_HBM] * n, out_specs=(_SEM, _SEM, *[_HBM] * n, pl.BlockSpec(memory_space=pltpu.VMEM)),
        input_output_aliases={i: 2 + i for i in range(n)},
        compiler_params=pltpu.CompilerParams(has_side_effects=_EFFECT),
    )(*[pltpu.with_memory_space_constraint(a, pltpu.HBM) for a in arrays])
    return outs[0], outs[1], list(outs[2:2 + ns]), list(outs[2 + ns:2 + n]), outs[-1]


def _copies_wait(send_sems, recv_sems, srcs, lands, plan, first, after, *, name):
    ns, n = len(srcs), len(srcs) + len(lands)

    def body(*refs):
        send, recv = refs[n], refs[n + 1]
        for k, (src, dst, dev) in enumerate(plan(refs[:ns], refs[ns:n])):
            cp = _remote(src, dst, send.at[first + k], recv.at[first + k], dev)
            cp.wait_send()
            cp.wait_recv()

    arrays = list(srcs) + list(lands)
    outs = pl.pallas_call(
        body, name=name, out_shape=tuple(pltpu.HBM(a.shape, a.dtype) for a in arrays),
        in_specs=[_HBM] * n + [_SEM, _SEM] + [_ANY] * len(after), out_specs=tuple([_HBM] * n),
        input_output_aliases={i: i for i in range(n)},
        compiler_params=pltpu.CompilerParams(has_side_effects=_EFFECT),
    )(*arrays, send_sems, recv_sems, *after)
    return list(outs[:ns]), list(outs[ns:])


def _gather_plan(halved):
    def plan(srcs, lands):
        x, y, c = _coords()
        me = 2 * x + y
        out = []
        for i, (src, land) in enumerate(zip(srcs, lands)):
            if halved[i]:
                h = src.shape[0] // 2
                rows = pl.ds(pl.multiple_of(c * h, 16), h)
                src, dst = src.at[rows], land.at[me, rows]
            else:
                dst = land.at[me]
            out += [(src, dst, (cx, cy, c)) for cx, cy in _other_chips(x, y)]
        return out
    return plan


def _forward_halves(lands, *, name):
    n = len(lands)

    def body(*refs):
        ins, outs = refs[:n], refs[n:2 * n]
        send_sems, recv_sems = refs[2 * n:]
        x, y, c = _coords()
        copies = []
        for i in range(n):
            h = ins[i].shape[1] // 2
            rows = pl.ds(pl.multiple_of(c * h, 16), h)
            for k, (cx, cy) in enumerate(_other_chips(x, y)):
                cp = _remote(ins[i].at[2 * cx + cy, rows], outs[i].at[2 * cx + cy, rows],
                             send_sems.at[i, k], recv_sems.at[i, k], (x, y, 1 - c))
                cp.start()
                copies.append(cp)
        for cp in copies:
            cp.wait()

    return pl.pallas_call(
        body, name=name, in_specs=[_ANY] * n, out_specs=[_ANY] * n,
        out_shape=[jax.ShapeDtypeStruct(a.shape, a.dtype) for a in lands],
        input_output_aliases={i: i for i in range(n)},
        scratch_shapes=[pltpu.SemaphoreType.DMA((n, 3)), pltpu.SemaphoreType.DMA((n, 3))],
        compiler_params=_params(),
    )(*lands)


def _all_plan(srcs, lands):
    x, y, c = _coords()
    me = 4 * x + 2 * y + c
    out = []
    for src, land in zip(srcs, lands):
        for dx in (0, 1):
            for dy in (0, 1):
                for dc in (0, 1):
                    if dx or dy or dc:
                        out.append((src, land.at[me], (1 - x if dx else x, 1 - y if dy else y, 1 - c if dc else c)))
    return out


def _chip_plan(srcs, lands):
    x, y, c = _coords()
    me = 2 * x + y
    out = []
    for src, land in zip(srcs, lands):
        out += [(src.at[2 * cx + cy], land.at[me], (cx, cy, c)) for cx, cy in _other_chips(x, y)]
    return out


def _rtile(r, pref, mult):
    t = (min(r, pref) // mult) * mult
    while t >= mult:
        if r % t == 0:
            return t
        t -= mult
    return r


def _pair_add(g, recv, core, *, name):
    _, _, r2, cols = g.shape
    tr = _rtile(r2, 256, 16)

    def body(c_ref, g_ref, r_ref, o_ref):
        o_ref[...] = (g_ref[...].astype(F32) + r_ref[...].astype(F32)).astype(o_ref.dtype)

    blk = pl.BlockSpec((None, tr, cols), lambda j, i, c_ref: (j, i, 0))
    return pl.pallas_call(
        body, name=name,
        grid_spec=pltpu.PrefetchScalarGridSpec(
            num_scalar_prefetch=1, grid=(N_CHIPS, r2 // tr),
            in_specs=[pl.BlockSpec((None, None, tr, cols), lambda j, i, c_ref: (j, c_ref[0], i, 0)), blk],
            out_specs=blk),
        out_shape=jax.ShapeDtypeStruct(recv.shape, recv.dtype), compiler_params=_params(),
    )(core, g, recv)


def _sum_slots(a, out_dtype, *, name):
    n, r, cols = a.shape
    tr = _rtile(r, 256, 16)

    def body(a_ref, o_ref):
        acc = a_ref[0].astype(F32)
        for j in range(1, n):
            acc = acc + a_ref[j].astype(F32)
        o_ref[...] = acc.astype(o_ref.dtype)

    return pl.pallas_call(
        body, name=name, grid=(r // tr,),
        in_specs=[pl.BlockSpec((n, tr, cols), lambda i: (0, i, 0))],
        out_specs=pl.BlockSpec((tr, cols), lambda i: (i, 0)),
        out_shape=jax.ShapeDtypeStruct((r, cols), out_dtype), compiler_params=_params(),
    )(a)


def _chip_sum(own, recv, chip, *, name):
    _, r2, cols = own.shape
    tr = _rtile(r2, 256, 16)

    def body(chip_ref, own_ref, *rest):
        o_ref = rest[-1]
        acc = None
        for j in range(N_CHIPS):
            term = jnp.where(chip_ref[0] == j, own_ref[...], rest[j][...]).astype(F32)
            acc = term if acc is None else acc + term
        o_ref[...] = acc

    def slot(j):
        return pl.BlockSpec((None, tr, cols),
                            lambda i, chip_ref: (jnp.where(chip_ref[0] == j, (j + 1) % N_CHIPS, j), i, 0))

    return pl.pallas_call(
        body, name=name,
        grid_spec=pltpu.PrefetchScalarGridSpec(
            num_scalar_prefetch=1, grid=(r2 // tr,),
            in_specs=[pl.BlockSpec((None, tr, cols), lambda i, chip_ref: (chip_ref[0], i, 0))]
                     + [slot(j) for j in range(N_CHIPS)],
            out_specs=pl.BlockSpec((tr, cols), lambda i, chip_ref: (i, 0))),
        out_shape=jax.ShapeDtypeStruct((r2, cols), F32), compiler_params=_params(),
    )(chip, own, *([recv] * N_CHIPS))


def _adam_update(w, gv, m, v):
    c1 = 1.0 / (1.0 - ADAM_B1 ** ADAM_STEP)
    c2 = 1.0 / (1.0 - ADAM_B2 ** ADAM_STEP)
    nm = ADAM_B1 * m + (1.0 - ADAM_B1) * gv
    nv = ADAM_B2 * v + (1.0 - ADAM_B2) * gv * gv
    return -ADAM_LR * ((nm * c1) / (jnp.sqrt(nv * c2) + ADAM_EPS) + ADAM_WD * w), nm, nv


def _adamw_halves(w, g_mine, g_other, m, v, core, *, name):
    r, cols = w.shape
    r2 = r // 2
    tr = _rtile(r2, 256, 8)
    nt = r2 // tr

    def body(core_ref, w_ref, gm_ref, go_ref, m_ref, v_ref, g_ref, d_ref, nm_ref, nv_ref):
        gv = jnp.where(pl.program_id(0) == core_ref[0], gm_ref[...], go_ref[...])
        g_ref[...] = gv
        d_ref[...], nm_ref[...], nv_ref[...] = _adam_update(w_ref[...], gv, m_ref[...], v_ref[...])

    full = pl.BlockSpec((tr, cols), lambda hf, i, core_ref: (hf * nt + i, 0))
    half = pl.BlockSpec((tr, cols), lambda hf, i, core_ref: (i, 0))
    shape = jax.ShapeDtypeStruct((r, cols), F32)
    return pl.pallas_call(
        body, name=name,
        grid_spec=pltpu.PrefetchScalarGridSpec(
            num_scalar_prefetch=1, grid=(2, nt), in_specs=[full, half, half, full, full], out_specs=[full] * 4),
        out_shape=[shape] * 4, compiler_params=_params(),
    )(core, w, g_mine, g_other, m, v)


def _adamw(w, g, m, v, *, name, rows=256):
    r, cols = w.shape
    tr = _rtile(r, rows, 8)

    def body(w_ref, g_ref, m_ref, v_ref, d_ref, nm_ref, nv_ref):
        d_ref[...], nm_ref[...], nv_ref[...] = _adam_update(w_ref[...], g_ref[...], m_ref[...], v_ref[...])

    blk = pl.BlockSpec((tr, cols), lambda i: (i, 0))
    shape = jax.ShapeDtypeStruct((r, cols), F32)
    return pl.pallas_call(
        body, name=name, grid=(r // tr,), in_specs=[blk] * 4, out_specs=[blk] * 3,
        out_shape=[shape] * 3, compiler_params=_params(),
    )(w, g, m, v)


_BIG = (("w_in", 1), ("w_branch_a", 0), ("w_branch_b", 0), ("w_out", 0), ("w_up", 1), ("w_down", 0),
        ("w_ple", 1), ("w_ple_gate", 0))
_SMALL = ("b_f", "gmlp_ln_g", "gmlp_ln_b", "gmlp_w_s", "gmlp_b_s", "norm_ffn_g", "conv_b", "norm_ple_g",
          "norm_final_g", "norm_mix_g")
_WEIGHTS = ("norm_mix_g", "w_in", "b_f", "gmlp_ln_g", "gmlp_ln_b", "gmlp_w_s", "gmlp_b_s", "w_branch_a",
            "w_branch_b", "w_out", "norm_ffn_g", "w_up", "conv_w", "conv_b", "w_down", "norm_ple_g", "w_ple",
            "w_ple_gate", "norm_final_g")
_PACK_ROWS = 8


def _pack(arrays):
    parts = []
    for a in arrays:
        flat = a.reshape(-1)
        unit = _PACK_ROWS * LANES
        flat = jnp.pad(flat, (0, (-flat.shape[0]) % unit))
        parts.append(flat.reshape(-1, LANES))
    return jnp.concatenate(parts, axis=0)


def _unpack(packed, shapes):
    out, row = [], 0
    for shp in shapes:
        size = math.prod(shp)
        rows = -(-size // (_PACK_ROWS * LANES)) * _PACK_ROWS
        out.append(packed[row:row + rows].reshape(-1)[:size].reshape(shp))
        row += rows
    return out


def _take_cols(parts, lo, hi):
    out, start = [], 0
    for a in parts:
        width = a.shape[1]
        a0, a1 = max(lo, start) - start, min(hi, start + width) - start
        if a1 > a0:
            out.append(a if (a0, a1) == (0, width) else a[:, a0:a1])
        start += width
    return out[0] if len(out) == 1 else jnp.concatenate(out, axis=1)


def _assemble(gathered, axis):
    n, r, cols = gathered.shape
    if axis == 0:
        return gathered.reshape(n * r, cols)
    return _take_cols([gathered[j] for j in range(n)], 0, n * cols)


def _to_chunks(parts, axis):
    rows, total = parts[0].shape[0], sum(a.shape[1] for a in parts)
    if axis == 0:
        r, cols = rows // N_CHIPS, total
        chunks = _take_cols(parts, 0, total).reshape(N_CHIPS, r, cols)
    else:
        r, cols = rows, total // N_CHIPS
        chunks = jnp.stack([_take_cols(parts, j * cols, (j + 1) * cols) for j in range(N_CHIPS)])
    return chunks.reshape(N_CHIPS, 2, r // 2, cols)


def kernel(x, p, norm_mix_g, w_in, b_f, gmlp_ln_g, gmlp_ln_b, gmlp_w_s, gmlp_b_s, w_branch_a, w_branch_b, w_out, norm_ffn_g, w_up, conv_w, conv_b, w_down, norm_ple_g, w_ple, w_ple_gate, norm_final_g, loss_target, m_norm_mix_g, m_w_in, m_b_f, m_gmlp_ln_g, m_gmlp_ln_b, m_gmlp_w_s, m_gmlp_b_s, m_w_branch_a, m_w_branch_b, m_w_out, m_norm_ffn_g, m_w_up, m_conv_w, m_conv_b, m_w_down, m_norm_ple_g, m_w_ple, m_w_ple_gate, m_norm_final_g, v_norm_mix_g, v_w_in, v_b_f, v_gmlp_ln_g, v_gmlp_ln_b, v_gmlp_w_s, v_gmlp_b_s, v_w_branch_a, v_w_branch_b, v_w_out, v_norm_ffn_g, v_w_up, v_conv_w, v_conv_b, v_w_down, v_norm_ple_g, v_w_ple, v_w_ple_gate, v_norm_final_g):
    args = dict(locals())
    wt = {n: args[n] for n in _WEIGHTS}
    mom = {n: args["m_" + n] for n in _WEIGHTS}
    var = {n: args["v_" + n] for n in _WEIGHTS}
    chip = 2 * lax.axis_index("x") + lax.axis_index("y")
    core = lax.axis_index("c").astype(jnp.int32).reshape(1)

    chip1 = chip.astype(jnp.int32).reshape(1)
    device = 2 * chip + lax.axis_index("c")
    axis_of = dict(_BIG)
    names = [n for n, _ in _BIG]
    put_mine = lambda land, mine: lax.dynamic_update_index_in_dim(land, mine, chip, 0)

    shards = [wt[n][0].astype(BF16) for n in names] + [conv_w[0]]
    halved = [True] * len(names) + [False]
    lands = [lax.empty((N_CHIPS,) + a.shape, a.dtype) for a in shards]
    send_sems, recv_sems, srcs, lands, _ = _copies_start(shards, lands, _gather_plan(halved), 3 * len(shards),
                                                         name="gather_start")
    o1 = 2 * GMLP_WIDTH
    o2 = o1 + 3 * FOX_WIDTH
    o3 = o2 + FOX_HEADS
    fpad = ((0, 0), (0, LANES - FOX_HEADS))
    w = {
        "conv_b": conv_b, "norm_mix_g": norm_mix_g, "norm_ffn_g": norm_ffn_g, "norm_ple_g": norm_ple_g,
        "norm_final_g": norm_final_g.reshape(1, D_MODEL), "b_f": jnp.pad(b_f, fpad),
        "gmlp_ln_g": gmlp_ln_g, "gmlp_ln_b": gmlp_ln_b, "gmlp_w_s": gmlp_w_s[0],
        "gmlp_b_s_t": jnp.pad(gmlp_b_s[0].T, ((0, 0), (0, LANES - GMLP_GROUPS))),
    }

    def get_w_in(after):
        _, got = _copies_wait(send_sems, recv_sems, srcs[:1], lands[:1], _gather_plan(halved[:1]), 0, [after],
                              name="gather_wait_in")
        got = _forward_halves(got, name="gather_forward_in")
        slots = put_mine(got[0], shards[0])
        slots = [slots[j] for j in range(N_CHIPS)]
        return {"w_uv": _take_cols(slots, 0, o1), "w_qkv": _take_cols(slots, o1, o2),
                "w_f": jnp.pad(_take_cols(slots, o2, o3), fpad), "w_g": _take_cols(slots, o3, o3 + 2 * D_MODEL)}

    def get_w_rest(after):
        _, got = _copies_wait(send_sems, recv_sems, srcs[1:], lands[1:], _gather_plan(halved[1:]), 3, [after],
                              name="gather_wait_rest")
        got = list(_forward_halves(got[:-1], name="gather_forward_rest")) + got[-1:]
        slots = {n: put_mine(got[i], shards[1 + i]) for i, n in enumerate(names[1:])}
        full = {n: _assemble(slots[n], axis_of[n]) for n in names[1:] if n != "w_up"}
        up = [slots["w_up"][j] for j in range(N_CHIPS)]
        return {"w_branch_a": full["w_branch_a"], "w_branch_b": full["w_branch_b"], "w_out": full["w_out"],
                "w_up_a": _take_cols(up, 0, D_FF), "w_up_b": _take_cols(up, D_FF, 2 * D_FF),
                "w_down": full["w_down"], "w_ple": full["w_ple"], "w_ple_gate": full["w_ple_gate"],
                "conv_w": _assemble(put_mine(got[-1], shards[-1]), 1)}

    grads, delta, new_m, new_v = {}, {}, {}, {}
    pending = {}

    def reduce_start(group, gfull, tag):
        chunks = [_to_chunks(gfull[n] if isinstance(gfull[n], list) else [gfull[n]], axis_of[n]) for n in group]
        from_sibling = _pair_exchange(chunks, name="grad_pair_exchange_" + tag)
        pair_sums = [_pair_add(chunks[i], from_sibling[i], core, name="grad_pair_add_" + n) for i, n in enumerate(group)]
        empty = [lax.empty(a.shape, a.dtype) for a in pair_sums]
        ssem, rsem, own, recv, token = _copies_start(pair_sums, empty, _chip_plan, 3 * len(group),
                                                     name="grad_chip_start_" + tag)
        pending[tag] = (ssem, rsem, own, recv)
        return token

    def reduce_finish(group, tag, after):
        ssem, rsem, own, recv = pending[tag]
        own, recv = _copies_wait(ssem, rsem, own, recv, _chip_plan, 0, after, name="grad_chip_wait_" + tag)
        halves = [_chip_sum(own[i], recv[i], chip1, name="grad_chip_sum_" + n) for i, n in enumerate(group)]
        other_halves = _pair_share(halves, name="grad_pair_share_" + tag)
        for i, n in enumerate(group):
            shp = wt[n].shape
            outs = _adamw_halves(wt[n].reshape(shp[-2:]), halves[i], other_halves[i], mom[n].reshape(shp[-2:]),
                                 var[n].reshape(shp[-2:]), core, name="adamw_" + n)
            grads[n], delta[n], new_m[n], new_v[n] = (o.reshape(shp) for o in outs)
        return new_v[group[-1]]

    ffn_group = ("w_up", "w_down", "w_ple", "w_ple_gate")
    mix_group = ("w_in", "w_branch_a", "w_branch_b", "w_out")

    def on_grads_ffn(g):
        gfull = dict(g)
        gfull["w_up"] = [g["w_up_a"], g["w_up_b"]]
        return reduce_start(ffn_group, gfull, "ffn")

    def on_grads_mix(g):
        early = [g[n] if n != "b_f" else g[n][:, :FOX_HEADS] for n in _SMALL[:-1]] + [g["conv_w"]]
        vec = _pack(early)
        ssem, rsem, own, recv, small_token = _copies_start(
            [vec], [lax.empty((8,) + vec.shape, F32)], _all_plan, 7, name="small_start")
        pending["small"] = (ssem, rsem, own, recv)
        gfull = dict(g)
        gfull["w_in"] = [g["w_uv"], g["w_qkv"], g["w_f"][:, :FOX_HEADS], g["w_g"]]
        token = reduce_start(mix_group, gfull, "mix")
        pending["ffn_done"] = reduce_finish(ffn_group, "ffn", [token])
        return token + small_token

    loss, grad_x, g = _device_step(x[0], p[0, 0], loss_target[0], w, get_w_in, get_w_rest, on_grads_ffn, on_grads_mix)

    ssem, rsem, own, recv = pending["small"]
    own, recv = _copies_wait(ssem, rsem, own, recv, _all_plan, 0, [grad_x, pending["ffn_done"]], name="small_wait")
    vec_early = _sum_slots(lax.dynamic_update_index_in_dim(recv[0], own[0], device, 0), F32, name="small_sum")
    vec_late = _pack([g["norm_mix_g"]])
    vec_late = _sum_slots(_all_exchange(vec_late, name="small_exchange_late"), F32, name="small_sum_late")
    early_rows = _pack([wt[n] for n in _SMALL[:-1]]).shape[0]
    vec = jnp.concatenate([vec_early[:early_rows], vec_late], axis=0)
    for n, a in zip(_SMALL, _unpack(vec, [wt[n].shape for n in _SMALL])):
        grads[n] = a
    conv_w_grad = _unpack(vec_early[early_rows:], [(3, 2 * D_FF)])[0]
    grads["conv_w"] = lax.dynamic_slice_in_dim(conv_w_grad, chip * conv_w.shape[2], conv_w.shape[2], axis=1).reshape(conv_w.shape)

    reduce_finish(mix_group, "mix", [grad_x, pending["ffn_done"], vec])
    shp = conv_w.shape
    outs = _adamw(conv_w.reshape(shp[-2:]), grads["conv_w"].reshape(shp[-2:]), m_conv_w.reshape(shp[-2:]),
                  v_conv_w.reshape(shp[-2:]), name="adamw_conv_w")
    delta["conv_w"], new_m["conv_w"], new_v["conv_w"] = (o.reshape(shp) for o in outs)
    outs = _adamw(_pack([wt[n] for n in _SMALL]), vec, _pack([mom[n] for n in _SMALL]),
                  _pack([var[n] for n in _SMALL]), name="adamw_small", rows=2048)
    for d, o in zip((delta, new_m, new_v), outs):
        for n, a in zip(_SMALL, _unpack(o, [wt[n].shape for n in _SMALL])):
            d[n] = a

    total_loss = lax.psum(loss[0, 0], ("x", "y", "c"))
    return (total_loss, grad_x.reshape(x.shape), *[grads[n] for n in _WEIGHTS], *[delta[n] for n in _WEIGHTS],
            *[new_m[n] for n in _WEIGHTS], *[new_v[n] for n in _WEIGHTS])
```

```python
import functools
import math

import jax
import jax.numpy as jnp
from jax import lax
from jax.experimental import pallas as pl
from jax.experimental.pallas import tpu as pltpu

F32 = jnp.float32
BF16 = jnp.bfloat16

D_MODEL = 1024
EPS = 1e-6
CHUNK = 64
GMLP_GROUPS = 8
GMLP_BLOCK = 128
GMLP_WIDTH = 1024
FOX_HEADS = 16
FOX_HEAD_DIM = 64
FOX_WIDTH = 1024
HEAD_PAIRS = FOX_HEADS // 2
ATT_BLOCK = 128
D_FF = 2816
PLE_DIM = 256
LANES = 128
BF16_TILE_ROWS = 16
N_CHIPS = 4

ADAM_LR = 0.001
ADAM_B1 = 0.9
ADAM_B2 = 0.999
ADAM_EPS = 1e-08
ADAM_WD = 0.01
ADAM_STEP = 10

VMEM_LIMIT = 56 * 1024 * 1024
MESH = pl.DeviceIdType.MESH

_NN = (((1,), (0,)), ((), ()))
_NT = (((1,), (1,)), ((), ()))
_TN = (((0,), (0,)), ((), ()))


def _params(**kw):
    return pltpu.CompilerParams(vmem_limit_bytes=VMEM_LIMIT, **kw)


def _tile(dim, pref):
    if dim <= pref:
        return dim
    t = (pref // LANES) * LANES
    while t >= LANES:
        if dim % t == 0:
            return t
        t -= LANES
    return dim


def _dot(a, b, dn):
    return lax.dot_general(a.astype(BF16), b.astype(BF16), dn, preferred_element_type=F32)


def _gelu(x):
    c = math.sqrt(2.0 / math.pi)
    t = jnp.tanh(c * (x + 0.044715 * x * x * x))
    return 0.5 * x * (1.0 + t)


def _gelu_and_grad(x):
    c = math.sqrt(2.0 / math.pi)
    x2 = x * x
    t = jnp.tanh(c * (x + 0.044715 * x2 * x))
    g = 0.5 * x * (1.0 + t)
    dg = 0.5 * (1.0 + t) + 0.5 * x * (1.0 - t * t) * c * (1.0 + 3.0 * 0.044715 * x2)
    return g, dg


def _sigmoid(x):
    return 1.0 / (1.0 + jnp.exp(-x))


def _mm(a, b, *, mode, out_dtype, name, add=None, tm=512, tn=512, dep=None):
    if mode == "nn":
        m, k = a.shape
        k2, n = b.shape
    elif mode == "nt":
        m, k = a.shape
        n, k2 = b.shape
    else:
        k, m = a.shape
        k2, n = b.shape
    assert k == k2, (name, a.shape, b.shape)
    tm = _tile(m, tm)
    tn = _tile(n, tn)
    dn = {"nn": _NN, "nt": _NT, "tn": _TN}[mode]

    def body(a_ref, b_ref, *rest):
        o_ref = rest[-1]
        acc = _dot(a_ref[...], b_ref[...], dn)
        if add is not None:
            acc = acc + rest[0][...].astype(F32)
        o_ref[...] = acc.astype(o_ref.dtype)

    a_spec = pl.BlockSpec((k, tm), lambda i, j: (0, i)) if mode == "tn" else pl.BlockSpec((tm, k), lambda i, j: (i, 0))
    b_spec = pl.BlockSpec((tn, k), lambda i, j: (j, 0)) if mode == "nt" else pl.BlockSpec((k, tn), lambda i, j: (0, j))
    o_spec = pl.BlockSpec((tm, tn), lambda i, j: (i, j))
    in_specs = [a_spec, b_spec]
    args = [a, b]
    if add is not None:
        in_specs.append(o_spec)
        args.append(add)
    if dep is not None:
        in_specs.append(pl.BlockSpec(memory_space=pl.ANY))
        args.append(dep)
    return pl.pallas_call(
        body, name=name, grid=(m // tm, n // tn), in_specs=in_specs, out_specs=o_spec,
        out_shape=jax.ShapeDtypeStruct((m, n), out_dtype), compiler_params=_params(),
    )(*args)


def _rms_fwd(x, g, *, name, tm=256, dep=None):
    s, d = x.shape
    tm = _tile(s, tm)

    def body(x_ref, g_ref, *rest):
        h_ref = rest[-1]
        xv = x_ref[...]
        r = lax.rsqrt(jnp.mean(xv * xv, axis=-1, keepdims=True) + EPS)
        h_ref[...] = (xv * r * g_ref[...]).astype(h_ref.dtype)

    deps = [] if dep is None else [dep]
    return pl.pallas_call(
        body, name=name, grid=(s // tm,),
        in_specs=[pl.BlockSpec((tm, d), lambda i: (i, 0)), pl.BlockSpec((1, d), lambda i: (0, 0))]
                 + [pl.BlockSpec(memory_space=pl.ANY)] * len(deps),
        out_specs=pl.BlockSpec((tm, d), lambda i: (i, 0)),
        out_shape=jax.ShapeDtypeStruct((s, d), BF16), compiler_params=_params(),
    )(x, g, *deps)


def _rms_bwd(x, g, dh, dres, *, name, tm=256):
    s, d = x.shape
    tm = _tile(s, tm)

    def body(x_ref, g_ref, dh_ref, dres_ref, dx_ref, dxb_ref, dg_ref):
        xv = x_ref[...]
        r = lax.rsqrt(jnp.mean(xv * xv, axis=-1, keepdims=True) + EPS)
        xhat = xv * r
        dhv = dh_ref[...].astype(F32)
        dyg = dhv * g_ref[...]
        dx = dres_ref[...] + r * (dyg - xhat * jnp.mean(dyg * xhat, axis=-1, keepdims=True))
        dx_ref[...] = dx
        dxb_ref[...] = dx.astype(dxb_ref.dtype)

        @pl.when(pl.program_id(0) == 0)
        def _():
            dg_ref[...] = jnp.zeros_like(dg_ref)

        dg_ref[...] += jnp.sum(dhv * xhat, axis=0, keepdims=True)

    row = pl.BlockSpec((tm, d), lambda i: (i, 0))
    vec = pl.BlockSpec((1, d), lambda i: (0, 0))
    return pl.pallas_call(
        body, name=name, grid=(s // tm,), in_specs=[row, vec, row, row], out_specs=[row, row, vec],
        out_shape=[jax.ShapeDtypeStruct((s, d), F32), jax.ShapeDtypeStruct((s, d), BF16),
                   jax.ShapeDtypeStruct((1, d), F32)],
        compiler_params=_params(),
    )(x, g, dh, dres)


def _gmlp_mask():
    t = lax.broadcasted_iota(jnp.int32, (GMLP_BLOCK, GMLP_BLOCK), 0)
    s_ = lax.broadcasted_iota(jnp.int32, (GMLP_BLOCK, GMLP_BLOCK), 1)
    return (s_ // CHUNK) <= (t // CHUNK)


def _gmlp_norm(zv, ln_g, ln_b):
    vv, dvv = _gelu_and_grad(zv)
    mu = jnp.mean(vv, axis=-1, keepdims=True)
    xc = vv - mu
    rstd = lax.rsqrt(jnp.mean(xc * xc, axis=-1, keepdims=True) + EPS)
    vhat = xc * rstd
    return vhat * ln_g + ln_b, vhat, rstd, dvv


def _gmlp_fwd(z_uv, ln_g, ln_b, w_s, b_s_t, *, name):
    s = z_uv.shape[0]
    w = GMLP_WIDTH
    gd = w // GMLP_GROUPS

    def body(z_ref, lg_ref, lb_ref, ws_ref, bs_ref, a_ref):
        u = _gelu(z_ref[:, :w].astype(F32))
        vn, _, _, _ = _gmlp_norm(z_ref[:, w:].astype(F32), lg_ref[...], lb_ref[...])
        mask = _gmlp_mask()
        for g in range(GMLP_GROUPS):
            wm = jnp.where(mask, ws_ref[g], 0.0)
            mixed = _dot(wm, vn[:, g * gd:(g + 1) * gd], _NN) + bs_ref[:, g:g + 1]
            a_ref[:, g * gd:(g + 1) * gd] = (u[:, g * gd:(g + 1) * gd] * mixed).astype(a_ref.dtype)

    full = lambda shape: pl.BlockSpec(shape, lambda i: (0,) * len(shape))
    return pl.pallas_call(
        body, name=name, grid=(s // GMLP_BLOCK,),
        in_specs=[pl.BlockSpec((GMLP_BLOCK, 2 * w), lambda i: (i, 0)), full((1, w)), full((1, w)),
                  full((GMLP_GROUPS, GMLP_BLOCK, GMLP_BLOCK)), full((GMLP_BLOCK, LANES))],
        out_specs=pl.BlockSpec((GMLP_BLOCK, w), lambda i: (i, 0)),
        out_shape=jax.ShapeDtypeStruct((s, w), BF16), compiler_params=_params(),
    )(z_uv, ln_g, ln_b, w_s, b_s_t)


def _gmlp_bwd(z_uv, da, ln_g, ln_b, w_s, b_s_t, *, name):
    s = z_uv.shape[0]
    w = GMLP_WIDTH
    gd = w // GMLP_GROUPS

    def body(z_ref, da_ref, lg_ref, lb_ref, ws_ref, bs_ref, dz_ref, dws_ref, dbs_ref, dlg_ref, dlb_ref):
        @pl.when(pl.program_id(0) == 0)
        def _():
            dws_ref[...] = jnp.zeros_like(dws_ref)
            dbs_ref[...] = jnp.zeros_like(dbs_ref)
            dlg_ref[...] = jnp.zeros_like(dlg_ref)
            dlb_ref[...] = jnp.zeros_like(dlb_ref)

        u, du_dz = _gelu_and_grad(z_ref[:, :w].astype(F32))
        lg = lg_ref[...]
        vn, vhat, rstd, dvv_dz = _gmlp_norm(z_ref[:, w:].astype(F32), lg, lb_ref[...])
        dav = da_ref[...].astype(F32)
        mask = _gmlp_mask()
        lane = lax.broadcasted_iota(jnp.int32, (GMLP_BLOCK, LANES), 1)
        dvn_parts = []
        dbs = jnp.zeros((GMLP_BLOCK, LANES), F32)
        for g in range(GMLP_GROUPS):
            sl = slice(g * gd, (g + 1) * gd)
            wm = jnp.where(mask, ws_ref[g], 0.0)
            vn_g = vn[:, sl]
            mixed = _dot(wm, vn_g, _NN) + bs_ref[:, g:g + 1]
            dmixed = dav[:, sl] * u[:, sl]
            dz_ref[:, sl] = (dav[:, sl] * mixed * du_dz[:, sl]).astype(dz_ref.dtype)
            dvn_parts.append(_dot(wm, dmixed, _TN))
            dws_ref[g] += jnp.where(mask, _dot(dmixed, vn_g, _NT), 0.0)
            dbs = dbs + jnp.where(lane == g, jnp.sum(dmixed, axis=-1, keepdims=True), 0.0)
        dbs_ref[...] += dbs
        dvn = jnp.concatenate(dvn_parts, axis=-1)
        dlg_ref[...] += jnp.sum(dvn * vhat, axis=0, keepdims=True)
        dlb_ref[...] += jnp.sum(dvn, axis=0, keepdims=True)
        dyg = dvn * lg
        dvv = rstd * (dyg - jnp.mean(dyg, axis=-1, keepdims=True)
                      - vhat * jnp.mean(dyg * vhat, axis=-1, keepdims=True))
        dz_ref[:, w:] = (dvv * dvv_dz).astype(dz_ref.dtype)

    full = lambda shape: pl.BlockSpec(shape, lambda i: (0,) * len(shape))
    return pl.pallas_call(
        body, name=name, grid=(s // GMLP_BLOCK,),
        in_specs=[pl.BlockSpec((GMLP_BLOCK, 2 * w), lambda i: (i, 0)),
                  pl.BlockSpec((GMLP_BLOCK, w), lambda i: (i, 0)), full((1, w)), full((1, w)),
                  full((GMLP_GROUPS, GMLP_BLOCK, GMLP_BLOCK)), full((GMLP_BLOCK, LANES))],
        out_specs=[pl.BlockSpec((GMLP_BLOCK, 2 * w), lambda i: (i, 0)),
                   full((GMLP_GROUPS, GMLP_BLOCK, GMLP_BLOCK)), full((GMLP_BLOCK, LANES)),
                   full((1, w)), full((1, w))],
        out_shape=[jax.ShapeDtypeStruct((s, 2 * w), BF16),
                   jax.ShapeDtypeStruct((GMLP_GROUPS, GMLP_BLOCK, GMLP_BLOCK), F32),
                   jax.ShapeDtypeStruct((GMLP_BLOCK, LANES), F32),
                   jax.ShapeDtypeStruct((1, w), F32), jax.ShapeDtypeStruct((1, w), F32)],
        compiler_params=_params(),
    )(z_uv, da, ln_g, ln_b, w_s, b_s_t)


def _tri(lower):
    r = lax.broadcasted_iota(jnp.int32, (ATT_BLOCK, ATT_BLOCK), 0)
    c = lax.broadcasted_iota(jnp.int32, (ATT_BLOCK, ATT_BLOCK), 1)
    return jnp.where((c <= r) if lower else (c >= r), 1.0, 0.0).astype(F32)


def _log_sigmoid(x):
    return jnp.minimum(x, 0.0) - jnp.log(1.0 + jnp.exp(-jnp.abs(x)))


def _fox_cum(f, b_f, *, name):
    s = f.shape[0]
    nb = s // ATT_BLOCK

    def body(f_ref, b_ref, cb_ref, ct_ref, carry):
        @pl.when(pl.program_id(0) == 0)
        def _():
            carry[...] = jnp.zeros_like(carry)

        lf = _log_sigmoid(f_ref[...] + b_ref[...])
        cum = lax.dot_general(_tri(True), lf, _NN, precision=lax.Precision.HIGHEST,
                              preferred_element_type=F32) + carry[...]
        carry[...] = cum[ATT_BLOCK - 1:ATT_BLOCK, :]
        for h in range(FOX_HEADS):
            cb_ref[h] = jnp.broadcast_to(cum[:, h:h + 1], (ATT_BLOCK, LANES))
        ct_ref[...] = cum.T

    return pl.pallas_call(
        body, name=name, grid=(nb,),
        in_specs=[pl.BlockSpec((ATT_BLOCK, LANES), lambda i: (i, 0)), pl.BlockSpec((1, LANES), lambda i: (0, 0))],
        out_specs=[pl.BlockSpec((FOX_HEADS, ATT_BLOCK, LANES), lambda i: (0, i, 0)),
                   pl.BlockSpec((LANES, ATT_BLOCK), lambda i: (0, i))],
        out_shape=[jax.ShapeDtypeStruct((FOX_HEADS, s, LANES), F32), jax.ShapeDtypeStruct((LANES, s), F32)],
        scratch_shapes=[pltpu.VMEM((1, LANES), F32)], compiler_params=_params(),
    )(f, b_f)


def _fox_dlogit(dcum_t, f, b_f, *, name):
    s = f.shape[0]
    nb = s // ATT_BLOCK

    def body(dc_ref, f_ref, b_ref, df_ref, db_ref, carry):
        @pl.when(pl.program_id(0) == 0)
        def _():
            carry[...] = jnp.zeros_like(carry)
            db_ref[...] = jnp.zeros_like(db_ref)

        d = dc_ref[...].T
        dlog = lax.dot_general(_tri(False), d, _NN, precision=lax.Precision.HIGHEST,
                               preferred_element_type=F32) + carry[...]
        carry[...] = dlog[0:1, :]
        df = dlog * (1.0 - _sigmoid(f_ref[...] + b_ref[...]))
        df_ref[...] = df
        db_ref[...] += jnp.sum(df, axis=0, keepdims=True)

    rev = lambda i: nb - 1 - i
    return pl.pallas_call(
        body, name=name, grid=(nb,),
        in_specs=[pl.BlockSpec((LANES, ATT_BLOCK), lambda i: (0, rev(i))),
                  pl.BlockSpec((ATT_BLOCK, LANES), lambda i: (rev(i), 0)),
                  pl.BlockSpec((1, LANES), lambda i: (0, 0))],
        out_specs=[pl.BlockSpec((ATT_BLOCK, LANES), lambda i: (rev(i), 0)),
                   pl.BlockSpec((1, LANES), lambda i: (0, 0))],
        out_shape=[jax.ShapeDtypeStruct((s, LANES), F32), jax.ShapeDtypeStruct((1, LANES), F32)],
        scratch_shapes=[pltpu.VMEM((1, LANES), F32)], compiler_params=_params(),
    )(dcum_t, f, b_f)


def _causal(qi, ki):
    r = lax.broadcasted_iota(jnp.int32, (ATT_BLOCK, ATT_BLOCK), 0) + qi * ATT_BLOCK
    c = lax.broadcasted_iota(jnp.int32, (ATT_BLOCK, ATT_BLOCK), 1) + ki * ATT_BLOCK
    return c <= r


def _head_mask():
    return lax.broadcasted_iota(jnp.int32, (1, LANES), 1) < FOX_HEAD_DIM


def _attn_fwd(qkv, cum_b, cum_r, *, name):
    s = qkv.shape[0]
    nq = s // ATT_BLOCK
    scale = FOX_HEAD_DIM ** -0.5
    npair = HEAD_PAIRS

    def body(q_ref, k_ref, v_ref, cq_ref, ck_ref, o_ref, l_ref):
        qi = pl.program_id(1)
        m0 = _head_mask()
        q2 = q_ref[...]
        zero = jnp.zeros_like(q2)
        qs = (jnp.where(m0, q2, zero), jnp.where(m0, zero, q2))
        cqs = (cq_ref[0], cq_ref[1])

        def step(ki, carry, masked):
            off = pl.multiple_of(ki * ATT_BLOCK, ATT_BLOCK)
            k2 = k_ref[pl.ds(off, ATT_BLOCK), :]
            v2 = v_ref[pl.ds(off, ATT_BLOCK), :]
            out = []
            for hh in range(2):
                m, l, acc = carry[hh]
                sc = _dot(qs[hh], k2, _NT) * scale + (cqs[hh] - ck_ref[hh:hh + 1, pl.ds(off, ATT_BLOCK)])
                if masked:
                    sc = jnp.where(_causal(qi, ki), sc, -1e30)
                m_new = jnp.maximum(m, jnp.max(sc, axis=-1, keepdims=True))
                alpha = jnp.exp(m - m_new)
                p = jnp.exp(sc - m_new)
                l = alpha * l + jnp.sum(p, axis=-1, keepdims=True)
                acc = alpha * acc + _dot(p, v2, _NN)
                out.append((m_new, l, acc))
            return tuple(out)

        init = tuple((jnp.full((ATT_BLOCK, 1), -1e30, F32), jnp.zeros((ATT_BLOCK, 1), F32),
                      jnp.zeros((ATT_BLOCK, LANES), F32)) for _ in range(2))
        carry = lax.fori_loop(0, qi, lambda ki, c: step(ki, c, False), init)
        (ma, la, acca), (mb, lb, accb) = step(qi, carry, True)
        o_ref[...] = jnp.where(m0, acca / la, accb / lb).astype(o_ref.dtype)
        l_ref[0] = jnp.broadcast_to(ma + jnp.log(la), (ATT_BLOCK, LANES))
        l_ref[1] = jnp.broadcast_to(mb + jnp.log(lb), (ATT_BLOCK, LANES))

    stat = pl.BlockSpec((None, 2, ATT_BLOCK, LANES), lambda j, i: (j, 0, i, 0))
    row = pl.BlockSpec((None, 2, s), lambda j, i: (j, 0, 0))
    return pl.pallas_call(
        body, name=name, grid=(npair, nq),
        in_specs=[pl.BlockSpec((ATT_BLOCK, LANES), lambda j, i: (i, j)),
                  pl.BlockSpec((s, LANES), lambda j, i: (0, npair + j)),
                  pl.BlockSpec((s, LANES), lambda j, i: (0, 2 * npair + j)),
                  stat, row],
        out_specs=[pl.BlockSpec((ATT_BLOCK, LANES), lambda j, i: (i, j)), stat],
        out_shape=[jax.ShapeDtypeStruct((s, FOX_WIDTH), BF16),
                   jax.ShapeDtypeStruct((npair, 2, s, LANES), F32)],
        compiler_params=_params(),
    )(qkv, qkv, qkv, cum_b, cum_r)


def _attn_delta(qkv, do, lse_b, cum_b, cum_r, *, name):
    s = qkv.shape[0]
    nq = s // ATT_BLOCK
    scale = FOX_HEAD_DIM ** -0.5
    npair = HEAD_PAIRS

    def body(q_ref, k_ref, v_ref, do_ref, l_ref, cq_ref, ck_ref, d_ref):
        qi = pl.program_id(1)
        m0 = _head_mask()
        q2 = q_ref[...]
        do2 = do_ref[...]
        qs = (jnp.where(m0, q2, jnp.zeros_like(q2)), jnp.where(m0, jnp.zeros_like(q2), q2))
        dos = (jnp.where(m0, do2, jnp.zeros_like(do2)), jnp.where(m0, jnp.zeros_like(do2), do2))

        def step(ki, carry, masked):
            off = pl.multiple_of(ki * ATT_BLOCK, ATT_BLOCK)
            k2 = k_ref[pl.ds(off, ATT_BLOCK), :]
            v2 = v_ref[pl.ds(off, ATT_BLOCK), :]
            out = []
            for hh in range(2):
                sc = _dot(qs[hh], k2, _NT) * scale + (cq_ref[hh] - ck_ref[hh:hh + 1, pl.ds(off, ATT_BLOCK)])
                p = jnp.exp(sc - l_ref[hh])
                if masked:
                    p = jnp.where(_causal(qi, ki), p, 0.0)
                out.append(carry[hh] + jnp.sum(p * _dot(dos[hh], v2, _NT), axis=-1, keepdims=True))
            return tuple(out)

        init = (jnp.zeros((ATT_BLOCK, 1), F32), jnp.zeros((ATT_BLOCK, 1), F32))
        carry = lax.fori_loop(0, qi, lambda ki, c: step(ki, c, False), init)
        da, db = step(qi, carry, True)
        d_ref[0] = jnp.broadcast_to(da, (ATT_BLOCK, LANES))
        d_ref[1] = jnp.broadcast_to(db, (ATT_BLOCK, LANES))

    stat = pl.BlockSpec((None, 2, ATT_BLOCK, LANES), lambda j, i: (j, 0, i, 0))
    return pl.pallas_call(
        body, name=name, grid=(npair, nq),
        in_specs=[pl.BlockSpec((ATT_BLOCK, LANES), lambda j, i: (i, j)),
                  pl.BlockSpec((s, LANES), lambda j, i: (0, npair + j)),
                  pl.BlockSpec((s, LANES), lambda j, i: (0, 2 * npair + j)),
                  pl.BlockSpec((ATT_BLOCK, LANES), lambda j, i: (i, j)),
                  stat, stat, pl.BlockSpec((None, 2, s), lambda j, i: (j, 0, 0))],
        out_specs=stat,
        out_shape=jax.ShapeDtypeStruct((npair, 2, s, LANES), F32), compiler_params=_params(),
    )(qkv, qkv, qkv, do, lse_b, cum_b, cum_r)


def _attn_bwd(qkv, do, lse_b, delta_b, cum_b, cum_r, *, name):
    s = qkv.shape[0]
    nq = s // ATT_BLOCK
    scale = FOX_HEAD_DIM ** -0.5
    npair = HEAD_PAIRS

    def body(q_ref, k_ref, v_ref, do_ref, l_ref, dl_ref, cq_ref, ck_ref, dq_ref, dk_ref, dv_ref, dc_ref):
        ki = pl.program_id(1)
        m0 = _head_mask()
        k2 = k_ref[...]
        v2 = v_ref[...]
        koff = pl.multiple_of(ki * ATT_BLOCK, ATT_BLOCK)

        @pl.when(ki == 0)
        def _():
            dq_ref[...] = jnp.zeros_like(dq_ref)

        def step(qi, carry, masked):
            off = pl.multiple_of(qi * ATT_BLOCK, ATT_BLOCK)
            q2 = q_ref[pl.ds(off, ATT_BLOCK), :]
            do2 = do_ref[pl.ds(off, ATT_BLOCK), :]
            qzero = jnp.zeros_like(q2)
            dzero = jnp.zeros_like(do2)
            out = []
            dqs = []
            for hh in range(2):
                dk_acc, dv_acc, dc_acc = carry[hh]
                keep = m0 if hh == 0 else jnp.logical_not(m0)
                qh = jnp.where(keep, q2, qzero)
                doh = jnp.where(keep, do2, dzero)
                sc = _dot(qh, k2, _NT) * scale + (cq_ref[hh, pl.ds(off, ATT_BLOCK), :]
                                                 - ck_ref[hh:hh + 1, pl.ds(koff, ATT_BLOCK)])
                p = jnp.exp(sc - l_ref[hh, pl.ds(off, ATT_BLOCK), :])
                if masked:
                    p = jnp.where(_causal(qi, ki), p, 0.0)
                dp = _dot(doh, v2, _NT)
                ds = p * (dp - dl_ref[hh, pl.ds(off, ATT_BLOCK), :])
                dv_acc = dv_acc + _dot(p, do2, _TN)
                dk_acc = dk_acc + _dot(ds, q2, _TN)
                dc_acc = dc_acc - jnp.sum(ds, axis=0, keepdims=True)
                dqs.append(_dot(ds, k2, _NN))
                out.append((dk_acc, dv_acc, dc_acc))
            dq_ref[pl.ds(off, ATT_BLOCK), :] += jnp.where(m0, dqs[0], dqs[1]) * scale
            return tuple(out)

        init = tuple((jnp.zeros((ATT_BLOCK, LANES), F32), jnp.zeros((ATT_BLOCK, LANES), F32),
                      jnp.zeros((1, ATT_BLOCK), F32)) for _ in range(2))
        carry = step(ki, init, True)
        (dka, dva, dca), (dkb, dvb, dcb) = lax.fori_loop(ki + 1, nq, lambda qi, c: step(qi, c, False), carry)
        dk_ref[...] = (jnp.where(m0, dka, dkb) * scale).astype(dk_ref.dtype)
        dv_ref[...] = jnp.where(m0, dva, dvb).astype(dv_ref.dtype)
        dc_ref[0:1, :] = dca
        dc_ref[1:2, :] = dcb

    stat = pl.BlockSpec((None, 2, s, LANES), lambda j, i: (j, 0, 0, 0))
    colfull = lambda base: pl.BlockSpec((s, LANES), lambda j, i: (0, base + j))
    colblk = lambda base: pl.BlockSpec((ATT_BLOCK, LANES), lambda j, i: (i, base + j))
    return pl.pallas_call(
        body, name=name, grid=(npair, nq),
        in_specs=[colfull(0), colblk(npair), colblk(2 * npair), colfull(0), stat, stat, stat,
                  pl.BlockSpec((None, 2, s), lambda j, i: (j, 0, 0))],
        out_specs=[colfull(0), colblk(0), colblk(0), pl.BlockSpec((None, 2, ATT_BLOCK), lambda j, i: (j, 0, i))],
        out_shape=[jax.ShapeDtypeStruct((s, FOX_WIDTH), F32), jax.ShapeDtypeStruct((s, FOX_WIDTH), BF16),
                   jax.ShapeDtypeStruct((s, FOX_WIDTH), BF16), jax.ShapeDtypeStruct((npair, 2, s), F32)],
        compiler_params=_params(),
    )(qkv, qkv, qkv, do, lse_b, delta_b, cum_b, cum_r)


ATT_TQ = 256
ATT_TK = 256
ATT_SCALE = FOX_HEAD_DIM ** -0.5
assert ATT_SCALE == 0.125 and ATT_TQ == ATT_TK


def _causal_t(qi, ki):
    kpos = lax.broadcasted_iota(jnp.int32, (ATT_TK, ATT_TQ), 0) + ki * ATT_TK
    qpos = lax.broadcasted_iota(jnp.int32, (ATT_TK, ATT_TQ), 1) + qi * ATT_TQ
    return kpos <= qpos


def _row_mask():
    return lax.broadcasted_iota(jnp.int32, (LANES, 1), 0) < FOX_HEAD_DIM


def _lane_tile(a, width):
    return a if a.shape[1] == width else jnp.tile(a, (1, width // a.shape[1]))


def _transpose_bf16(a):
    return a.astype(F32).T.astype(BF16)


def _attn_fwd_t(qkv, cum_b, cum_r, *, name):
    s = qkv.shape[0]
    nq = s // ATT_TQ
    npair = HEAD_PAIRS

    def body(q_ref, k_ref, v_ref, cq_ref, ck_ref, o_ref, ot_ref, l_ref, vt_ref):
        qi = pl.program_id(1)
        rows = _row_mask()

        @pl.when(qi == 0)
        def _():
            vt_ref[...] = _transpose_bf16(v_ref[...])

        qt = _transpose_bf16(q_ref[...]) * ATT_SCALE
        zero = jnp.zeros_like(qt)
        qts = (jnp.where(rows, qt, zero), jnp.where(rows, zero, qt))

        def step(ki, carry, masked):
            off = pl.multiple_of(ki * ATT_TK, ATT_TK)
            k2 = k_ref[pl.ds(off, ATT_TK), :]
            vt = vt_ref[:, pl.ds(off, ATT_TK)]
            out = []
            for hh in range(2):
                m, l, acc = carry[hh]
                bias = cq_ref[hh:hh + 1, :] - _lane_tile(ck_ref[hh, pl.ds(off, ATT_TK), :], ATT_TQ)
                sc = _dot(k2, qts[hh], _NN) + bias
                if masked:
                    sc = jnp.where(_causal_t(qi, ki), sc, -1e30)
                m_new = jnp.maximum(m, jnp.max(sc, axis=0, keepdims=True))
                alpha = jnp.exp(m - m_new)
                p = jnp.exp(sc - m_new)
                l = alpha * l + jnp.sum(p, axis=0, keepdims=True)
                p_hi = p.astype(BF16)
                p_lo = (p - p_hi.astype(F32)).astype(BF16)
                acc = alpha * acc + (_dot(vt, p_hi, _NN) + _dot(vt, p_lo, _NN))
                out.append((m_new, l, acc))
            return tuple(out)

        init = tuple((jnp.full((1, ATT_TQ), -1e30, F32), jnp.zeros((1, ATT_TQ), F32),
                      jnp.zeros((LANES, ATT_TQ), F32)) for _ in range(2))
        carry = lax.fori_loop(0, qi, lambda ki, c: step(ki, c, False), init)
        (ma, la, acca), (mb, lb, accb) = step(qi, carry, True)
        ot = jnp.where(rows, acca / la, accb / lb)
        ot_ref[...] = ot
        o_ref[...] = ot.T.astype(o_ref.dtype)
        l_ref[0:1, :] = ma + jnp.log(la)
        l_ref[1:2, :] = mb + jnp.log(lb)

    row = pl.BlockSpec((None, 2, ATT_TQ), lambda j, i: (j, 0, i))
    return pl.pallas_call(
        body, name=name, grid=(npair, nq),
        in_specs=[pl.BlockSpec((ATT_TQ, LANES), lambda j, i: (i, j)),
                  pl.BlockSpec((s, LANES), lambda j, i: (0, npair + j)),
                  pl.BlockSpec((s, LANES), lambda j, i: (0, 2 * npair + j)),
                  row, pl.BlockSpec((None, 2, s, LANES), lambda j, i: (j, 0, 0, 0))],
        out_specs=[pl.BlockSpec((ATT_TQ, LANES), lambda j, i: (i, j)),
                   pl.BlockSpec((LANES, ATT_TQ), lambda j, i: (j, i)), row],
        out_shape=[jax.ShapeDtypeStruct((s, FOX_WIDTH), BF16), jax.ShapeDtypeStruct((FOX_WIDTH, s), F32),
                   jax.ShapeDtypeStruct((npair, 2, s), F32)],
        scratch_shapes=[pltpu.VMEM((LANES, s), BF16)],
        compiler_params=_params(),
    )(qkv, qkv, qkv, cum_r, cum_b)


def _attn_delta_t(do_t, o_t, *, name):
    s = o_t.shape[1]
    ts = _tile(s, 512)

    def body(do_ref, o_ref, d_ref):
        prod = do_ref[...].astype(F32) * o_ref[...]
        d_ref[0:1, :] = jnp.sum(prod[:FOX_HEAD_DIM], axis=0, keepdims=True)
        d_ref[1:2, :] = jnp.sum(prod[FOX_HEAD_DIM:], axis=0, keepdims=True)

    blk = pl.BlockSpec((LANES, ts), lambda j, i: (j, i))
    return pl.pallas_call(
        body, name=name, grid=(HEAD_PAIRS, s // ts), in_specs=[blk, blk],
        out_specs=pl.BlockSpec((None, 2, ts), lambda j, i: (j, 0, i)),
        out_shape=jax.ShapeDtypeStruct((HEAD_PAIRS, 2, s), F32), compiler_params=_params(),
    )(do_t, o_t)


def _attn_bwd_t(qkv, do, o_t, lse, cum_b, cum_r, *, name):
    s = qkv.shape[0]
    nq = s // ATT_TQ
    npair = HEAD_PAIRS

    def body(q_ref, k_ref, v_ref, do_ref, ot_ref, l_ref, cq_ref, ck_ref, dq_ref, dk_ref, dv_ref, dc_ref,
             qt_ref, dot_ref, dqt_ref, dl_ref):
        ki = pl.program_id(1)
        m0 = _head_mask()
        rows = _row_mask()
        k2 = k_ref[...]
        v2 = v_ref[...]
        kt = _transpose_bf16(k2)
        ks = k2 * ATT_SCALE
        kz, vz = jnp.zeros_like(k2), jnp.zeros_like(v2)
        khs = (jnp.where(m0, ks, kz), jnp.where(m0, kz, ks))
        vhs = (jnp.where(m0, v2, vz), jnp.where(m0, vz, v2))
        cks = tuple(_lane_tile(ck_ref[hh], ATT_TQ) for hh in range(2))

        @pl.when(ki == 0)
        def _():
            dqt_ref[...] = jnp.zeros_like(dqt_ref)
            qt_ref[...] = _transpose_bf16(q_ref[...])
            do_t = do_ref[...].astype(F32).T
            dot_ref[...] = do_t.astype(BF16)
            prod = do_t * ot_ref[...]
            dl_ref[0:1, :] = jnp.sum(prod[:FOX_HEAD_DIM], axis=0, keepdims=True)
            dl_ref[1:2, :] = jnp.sum(prod[FOX_HEAD_DIM:], axis=0, keepdims=True)

        def step(qi, carry, masked):
            off = pl.multiple_of(qi * ATT_TQ, ATT_TQ)
            q2 = q_ref[pl.ds(off, ATT_TQ), :]
            do2 = do_ref[pl.ds(off, ATT_TQ), :]
            qt = qt_ref[:, pl.ds(off, ATT_TQ)]
            dot_ = dot_ref[:, pl.ds(off, ATT_TQ)]
            out, dqs = [], []
            for hh in range(2):
                dk_acc, dv_acc, dc_acc = carry[hh]
                sc = _dot(khs[hh], qt, _NN) + (cq_ref[hh:hh + 1, pl.ds(off, ATT_TQ)] - cks[hh])
                p = jnp.exp(sc - l_ref[hh:hh + 1, pl.ds(off, ATT_TQ)])
                if masked:
                    p = jnp.where(_causal_t(qi, ki), p, 0.0)
                dp = _dot(vhs[hh], dot_, _NN)
                ds = p * (dp - dl_ref[hh:hh + 1, pl.ds(off, ATT_TQ)])
                dc_acc = dc_acc - jnp.sum(ds, axis=1, keepdims=True)
                dss = (ds * ATT_SCALE).astype(BF16)
                dv_acc = dv_acc + _dot(p, do2, _NN)
                dk_acc = dk_acc + _dot(dss, q2, _NN)
                dqs.append(_dot(kt, dss, _NN))
                out.append((dk_acc, dv_acc, dc_acc))
            dqt_ref[:, pl.ds(off, ATT_TQ)] += jnp.where(rows, dqs[0], dqs[1])
            return tuple(out)

        init = tuple((jnp.zeros((ATT_TK, LANES), F32), jnp.zeros((ATT_TK, LANES), F32),
                      jnp.zeros((ATT_TK, 1), F32)) for _ in range(2))
        carry = step(ki, init, True)
        (dka, dva, dca), (dkb, dvb, dcb) = lax.fori_loop(ki + 1, nq, lambda qi, c: step(qi, c, False), carry)
        dk_ref[...] = jnp.where(m0, dka, dkb).astype(dk_ref.dtype)
        dv_ref[...] = jnp.where(m0, dva, dvb).astype(dv_ref.dtype)
        dc_ref[0] = jnp.broadcast_to(dca, (ATT_TK, LANES))
        dc_ref[1] = jnp.broadcast_to(dcb, (ATT_TK, LANES))

        @pl.when(ki == nq - 1)
        def _():
            dq_ref[...] = dqt_ref[...].T.astype(dq_ref.dtype)

    colfull = lambda base: pl.BlockSpec((s, LANES), lambda j, i: (0, base + j))
    colblk = lambda base: pl.BlockSpec((ATT_TK, LANES), lambda j, i: (i, base + j))
    stat = pl.BlockSpec((None, 2, s), lambda j, i: (j, 0, 0))
    bcast = pl.BlockSpec((None, 2, ATT_TK, LANES), lambda j, i: (j, 0, i, 0))
    grad = jax.ShapeDtypeStruct((s, FOX_WIDTH), BF16)
    return pl.pallas_call(
        body, name=name, grid=(npair, nq),
        in_specs=[colfull(0), colblk(npair), colblk(2 * npair), colfull(0),
                  pl.BlockSpec((LANES, s), lambda j, i: (j, 0)), stat, stat, bcast],
        out_specs=[colfull(0), colblk(0), colblk(0), bcast],
        out_shape=[grad, grad, grad, jax.ShapeDtypeStruct((npair, 2, s, LANES), F32)],
        scratch_shapes=[pltpu.VMEM((LANES, s), BF16), pltpu.VMEM((LANES, s), BF16), pltpu.VMEM((LANES, s), F32),
                        pltpu.VMEM((2, s), F32)],
        compiler_params=_params(),
    )(qkv, qkv, qkv, do, o_t, lse, cum_r, cum_b)


def _merge_fwd(zg, ya, yb, *, name, tm=256):
    s, d = ya.shape
    tm = _tile(s, tm)

    def body(zg_ref, ya_ref, yb_ref, m_ref):
        ga = _sigmoid(zg_ref[:, :d].astype(F32))
        gb = _sigmoid(zg_ref[:, d:].astype(F32))
        m_ref[...] = (ga * ya_ref[...].astype(F32) + gb * yb_ref[...].astype(F32)).astype(m_ref.dtype)

    row = pl.BlockSpec((tm, d), lambda i: (i, 0))
    row2 = pl.BlockSpec((tm, 2 * d), lambda i: (i, 0))
    return pl.pallas_call(
        body, name=name, grid=(s // tm,), in_specs=[row2, row, row], out_specs=row,
        out_shape=jax.ShapeDtypeStruct((s, d), BF16), compiler_params=_params(),
    )(zg, ya, yb)


def _merge_bwd(dm, zg, ya, yb, *, name, tm=256):
    s, d = ya.shape
    tm = _tile(s, tm)

    def body(dm_ref, zg_ref, ya_ref, yb_ref, dzg_ref, dya_ref, dyb_ref):
        dmv = dm_ref[...].astype(F32)
        ga = _sigmoid(zg_ref[:, :d].astype(F32))
        gb = _sigmoid(zg_ref[:, d:].astype(F32))
        dzg_ref[:, :d] = (dmv * ya_ref[...].astype(F32) * ga * (1.0 - ga)).astype(dzg_ref.dtype)
        dzg_ref[:, d:] = (dmv * yb_ref[...].astype(F32) * gb * (1.0 - gb)).astype(dzg_ref.dtype)
        dya_ref[...] = (dmv * ga).astype(dya_ref.dtype)
        dyb_ref[...] = (dmv * gb).astype(dyb_ref.dtype)

    row = pl.BlockSpec((tm, d), lambda i: (i, 0))
    row2 = pl.BlockSpec((tm, 2 * d), lambda i: (i, 0))
    return pl.pallas_call(
        body, name=name, grid=(s // tm,), in_specs=[row, row2, row, row], out_specs=[row2, row, row],
        out_shape=[jax.ShapeDtypeStruct((s, 2 * d), BF16), jax.ShapeDtypeStruct((s, d), BF16),
                   jax.ShapeDtypeStruct((s, d), BF16)],
        compiler_params=_params(),
    )(dm, zg, ya, yb)


def _shift_down(u, k, row):
    return jnp.where(row >= k, pltpu.roll(u, k, 0), 0.0)


def _shift_up(u, k, row):
    n = u.shape[0]
    return jnp.where(row < n - k, pltpu.roll(u, n - k, 0), 0.0)


def _conv_act_fwd(up_a, up_b, cw_a, cw_b, cb_a, cb_b, *, name, tc=128):
    s, f = up_a.shape
    tc = _tile(f, tc)

    def body(ua_ref, ub_ref, wa_ref, wb_ref, ba_ref, bb_ref, act_ref):
        row = lax.broadcasted_iota(jnp.int32, (s, tc), 0)

        def conv(u_ref, w_ref, b_ref):
            u = u_ref[...].astype(F32)
            return (b_ref[...] + w_ref[0:1, :] * _shift_down(u, 2, row)
                    + w_ref[1:2, :] * _shift_down(u, 1, row) + w_ref[2:3, :] * u)

        ca = conv(ua_ref, wa_ref, ba_ref)
        cb = conv(ub_ref, wb_ref, bb_ref)
        act_ref[...] = (_gelu(ca) * cb).astype(act_ref.dtype)

    col = pl.BlockSpec((s, tc), lambda j: (0, j))
    w3 = pl.BlockSpec((3, tc), lambda j: (0, j))
    b1 = pl.BlockSpec((1, tc), lambda j: (0, j))
    return pl.pallas_call(
        body, name=name, grid=(f // tc,), in_specs=[col, col, w3, w3, b1, b1], out_specs=col,
        out_shape=jax.ShapeDtypeStruct((s, f), BF16), compiler_params=_params(),
    )(up_a, up_b, cw_a, cw_b, cb_a, cb_b)


def _conv_act_bwd(up_a, up_b, dact, cw_a, cw_b, cb_a, cb_b, *, name, tc=128):
    s, f = up_a.shape
    tc = _tile(f, tc)

    def body(ua_ref, ub_ref, da_ref, wa_ref, wb_ref, ba_ref, bb_ref, dua_ref, dub_ref, dwa_ref, dwb_ref):
        row = lax.broadcasted_iota(jnp.int32, (s, tc), 0)

        def conv(u_ref, w_ref, b_ref):
            u = u_ref[...].astype(F32)
            u1 = _shift_down(u, 1, row)
            u2 = _shift_down(u, 2, row)
            return u, u1, u2, b_ref[...] + w_ref[0:1, :] * u2 + w_ref[1:2, :] * u1 + w_ref[2:3, :] * u

        def back(dc, taps, w_ref, du_ref, dw_ref):
            u, u1, u2 = taps
            dw_ref[0:1, :] = jnp.sum(dc * u2, axis=0, keepdims=True)
            dw_ref[1:2, :] = jnp.sum(dc * u1, axis=0, keepdims=True)
            dw_ref[2:3, :] = jnp.sum(dc * u, axis=0, keepdims=True)
            dw_ref[3:4, :] = jnp.sum(dc, axis=0, keepdims=True)
            du = (w_ref[2:3, :] * dc + w_ref[1:2, :] * _shift_up(dc, 1, row)
                  + w_ref[0:1, :] * _shift_up(dc, 2, row))
            du_ref[...] = du.astype(du_ref.dtype)

        ua, ua1, ua2, ca = conv(ua_ref, wa_ref, ba_ref)
        ub, ub1, ub2, cb = conv(ub_ref, wb_ref, bb_ref)
        g, dg = _gelu_and_grad(ca)
        dact_v = da_ref[...].astype(F32)
        back(dact_v * cb * dg, (ua, ua1, ua2), wa_ref, dua_ref, dwa_ref)
        back(dact_v * g, (ub, ub1, ub2), wb_ref, dub_ref, dwb_ref)

    col = pl.BlockSpec((s, tc), lambda j: (0, j))
    w3 = pl.BlockSpec((3, tc), lambda j: (0, j))
    w4 = pl.BlockSpec((4, tc), lambda j: (0, j))
    b1 = pl.BlockSpec((1, tc), lambda j: (0, j))
    return pl.pallas_call(
        body, name=name, grid=(f // tc,), in_specs=[col, col, col, w3, w3, b1, b1],
        out_specs=[col, col, w4, w4],
        out_shape=[jax.ShapeDtypeStruct((s, f), BF16), jax.ShapeDtypeStruct((s, f), BF16),
                   jax.ShapeDtypeStruct((4, f), F32), jax.ShapeDtypeStruct((4, f), F32)],
        compiler_params=_params(),
    )(up_a, up_b, dact, cw_a, cw_b, cb_a, cb_b)


def _ple_final(x2, ple, zp, target, g_final, *, name, tm=256):
    s, d = x2.shape
    tm = _tile(s, tm)

    def body(x_ref, ple_ref, zp_ref, t_ref, g_ref, dx_ref, dple_ref, dzp_ref, dg_ref, loss_ref):
        @pl.when(pl.program_id(0) == 0)
        def _():
            dg_ref[...] = jnp.zeros_like(dg_ref)
            loss_ref[...] = jnp.zeros_like(loss_ref)

        gp = _sigmoid(zp_ref[...].astype(F32))
        plev = ple_ref[...].astype(F32)
        x3 = x_ref[...] + plev * gp
        r = lax.rsqrt(jnp.mean(x3 * x3, axis=-1, keepdims=True) + EPS)
        xhat = x3 * r
        gv = g_ref[...]
        diff = xhat * gv - t_ref[...]
        loss_ref[...] += 0.5 * jnp.sum(jnp.mean(diff * diff, axis=-1, keepdims=True), axis=0, keepdims=True)
        dy = diff * (1.0 / d)
        dg_ref[...] += jnp.sum(dy * xhat, axis=0, keepdims=True)
        dyg = dy * gv
        dx3 = r * (dyg - xhat * jnp.mean(dyg * xhat, axis=-1, keepdims=True))
        dx_ref[...] = dx3
        dple_ref[...] = (dx3 * gp).astype(dple_ref.dtype)
        dzp_ref[...] = (dx3 * plev * gp * (1.0 - gp)).astype(dzp_ref.dtype)

    row = pl.BlockSpec((tm, d), lambda i: (i, 0))
    vec = pl.BlockSpec((1, d), lambda i: (0, 0))
    return pl.pallas_call(
        body, name=name, grid=(s // tm,), in_specs=[row, row, row, row, vec],
        out_specs=[row, row, row, vec, pl.BlockSpec((1, LANES), lambda i: (0, 0))],
        out_shape=[jax.ShapeDtypeStruct((s, d), F32), jax.ShapeDtypeStruct((s, d), BF16),
                   jax.ShapeDtypeStruct((s, d), BF16), jax.ShapeDtypeStruct((1, d), F32),
                   jax.ShapeDtypeStruct((1, LANES), F32)],
        compiler_params=_params(),
    )(x2, ple, zp, target, g_final)


def _device_step(x, p, target, w, get_w_in=None, get_w_rest=None, on_grads_ffn=None, on_grads_mix=None):
    s = x.shape[0]
    g = {}
    w = dict(w)

    h = _rms_fwd(x, w["norm_mix_g"], name="rms_mix", dep=w.get("first_dep"))
    if get_w_in is not None:
        w.update(get_w_in(h))
    z_uv = _mm(h, w["w_uv_t"], mode="nt", out_dtype=BF16, name="proj_uv", tm=1024)
    qkv = _mm(h, w["w_qkv_t"], mode="nt", out_dtype=BF16, name="proj_qkv", tm=1024)
    zg = _mm(h, w["w_g_t"], mode="nt", out_dtype=BF16, name="proj_gate", tm=1024)
    f = _mm(h, w["w_f_t"], mode="nt", out_dtype=F32, name="proj_f", tm=1024)

    a = _gmlp_fwd(z_uv, w["gmlp_ln_g"], w["gmlp_ln_b"], w["gmlp_w_s"], w["gmlp_b_s_t"], name="gmlp_fwd")

    cum_b, cum_t = _fox_cum(f, w["b_f"], name="fox_cum")
    cum_b = cum_b.reshape(HEAD_PAIRS, 2, s, LANES)
    cum_r = cum_t[:FOX_HEADS].reshape(HEAD_PAIRS, 2, s)
    b, o_t, lse = _attn_fwd_t(qkv, cum_b, cum_r, name="attn_fwd")
    if get_w_rest is not None:
        w.update(get_w_rest(b))

    ya = _mm(a, w["w_branch_a"], mode="nn", out_dtype=BF16, name="branch_a", tm=1024)
    yb = _mm(b, w["w_branch_b"], mode="nn", out_dtype=BF16, name="branch_b", tm=1024)
    merged = _merge_fwd(zg, ya, yb, name="merge_fwd")
    x1 = _mm(merged, w["w_out"], mode="nn", out_dtype=F32, name="proj_out", add=x, tm=1024)

    h2 = _rms_fwd(x1, w["norm_ffn_g"], name="rms_ffn")
    up_a = _mm(h2, w["w_up_a"], mode="nn", out_dtype=BF16, name="up_a", tm=1024, tn=D_FF // 2)
    up_b = _mm(h2, w["w_up_b"], mode="nn", out_dtype=BF16, name="up_b", tm=1024, tn=D_FF // 2)
    cw, cb = w["conv_w"], w["conv_b"]
    conv_args = (cw[:, :D_FF], cw[:, D_FF:], cb[:, :D_FF], cb[:, D_FF:])
    act = _conv_act_fwd(up_a, up_b, *conv_args, name="conv_act_fwd")
    x2 = _mm(act, w["w_down"], mode="nn", out_dtype=F32, name="down", add=x1, tm=512)

    h3 = _rms_fwd(x2, w["norm_ple_g"], name="rms_ple")
    ple = _mm(p, w["w_ple"], mode="nn", out_dtype=BF16, name="ple_proj", tm=1024)
    zp = _mm(h3, w["w_ple_gate"], mode="nn", out_dtype=BF16, name="ple_gate", tm=1024)
    dx3, dple, dzp, g["norm_final_g"], loss = _ple_final(x2, ple, zp, target, w["norm_final_g"], name="ple_final")

    g["w_ple"] = _mm(p, dple, mode="tn", out_dtype=BF16, name="dw_ple")
    g["w_ple_gate"] = _mm(h3, dzp, mode="tn", out_dtype=BF16, name="dw_ple_gate")
    dh3 = _mm(dzp, w["w_ple_gate"], mode="nt", out_dtype=BF16, name="dh3")
    dx2, dx2_b, g["norm_ple_g"] = _rms_bwd(x2, w["norm_ple_g"], dh3, dx3, name="rms_ple_bwd")

    g["w_down"] = _mm(act, dx2_b, mode="tn", out_dtype=BF16, name="dw_down", tm=D_FF // 2)
    dact = _mm(dx2_b, w["w_down"], mode="nt", out_dtype=BF16, name="dact", tn=D_FF // 2)
    dup_a, dup_b, dcw_a, dcw_b = _conv_act_bwd(up_a, up_b, dact, *conv_args, name="conv_act_bwd")
    g["conv_w"] = jnp.concatenate([dcw_a[:3], dcw_b[:3]], axis=1)
    g["conv_b"] = jnp.concatenate([dcw_a[3:], dcw_b[3:]], axis=1)
    g["w_up_a"] = _mm(h2, dup_a, mode="tn", out_dtype=BF16, name="dw_up_a", tn=D_FF // 2)
    g["w_up_b"] = _mm(h2, dup_b, mode="tn", out_dtype=BF16, name="dw_up_b", tn=D_FF // 2)
    dh2 = _mm(dup_a, w["w_up_a"], mode="nt", out_dtype=F32, name="dh2_a")
    dh2 = _mm(dup_b, w["w_up_b"], mode="nt", out_dtype=BF16, name="dh2_b", add=dh2)
    dx1, dx1_b, g["norm_ffn_g"] = _rms_bwd(x1, w["norm_ffn_g"], dh2, dx2, name="rms_ffn_bwd")
    dep = on_grads_ffn(g) if on_grads_ffn is not None else None

    g["w_out"] = _mm(merged, dx1_b, mode="tn", out_dtype=BF16, name="dw_out")
    dmerged = _mm(dx1_b, w["w_out"], mode="nt", out_dtype=BF16, name="dmerged", dep=dep)
    dzg, dya, dyb = _merge_bwd(dmerged, zg, ya, yb, name="merge_bwd")
    g["w_branch_a"] = _mm(a, dya, mode="tn", out_dtype=BF16, name="dw_branch_a")
    g["w_branch_b"] = _mm(b, dyb, mode="tn", out_dtype=BF16, name="dw_branch_b")
    da = _mm(dya, w["w_branch_a"], mode="nt", out_dtype=BF16, name="da")
    db = _mm(dyb, w["w_branch_b"], mode="nt", out_dtype=BF16, name="db")

    dz_uv, g["gmlp_w_s"], dbs_t, g["gmlp_ln_g"], g["gmlp_ln_b"] = _gmlp_bwd(
        z_uv, da, w["gmlp_ln_g"], w["gmlp_ln_b"], w["gmlp_w_s"], w["gmlp_b_s_t"], name="gmlp_bwd")
    g["gmlp_b_s"] = dbs_t[:, :GMLP_GROUPS].T

    dq, dk, dv, dcum_b = _attn_bwd_t(qkv, db, o_t, lse, cum_b, cum_r, name="attn_bwd")
    dcum_t = jnp.pad(dcum_b[..., 0].reshape(FOX_HEADS, s), ((0, LANES - FOX_HEADS), (0, 0)))
    df, g["b_f"] = _fox_dlogit(dcum_t, f, w["b_f"], name="fox_dlogit")
    dqkv = jnp.concatenate([dq, dk, dv], axis=1)

    g["w_uv_t"] = _mm(dz_uv, h, mode="tn", out_dtype=BF16, name="dw_uv")
    g["w_qkv_t"] = _mm(dqkv, h, mode="tn", out_dtype=BF16, name="dw_qkv")
    g["w_f_t"] = _mm(df, h, mode="tn", out_dtype=BF16, name="dw_f")
    g["w_g_t"] = _mm(dzg, h, mode="tn", out_dtype=BF16, name="dw_g")
    dep = on_grads_mix(g) if on_grads_mix is not None else None
    dh = _mm(dz_uv, w["w_uv_t"], mode="nn", out_dtype=F32, name="dh_uv", dep=dep)
    dh = _mm(dqkv, w["w_qkv_t"], mode="nn", out_dtype=F32, name="dh_qkv", add=dh)
    dh = _mm(df, w["w_f_t"], mode="nn", out_dtype=F32, name="dh_f", add=dh)
    dh = _mm(dzg, w["w_g_t"], mode="nn", out_dtype=BF16, name="dh_g", add=dh)
    dx0, _, g["norm_mix_g"] = _rms_bwd(x, w["norm_mix_g"], dh, dx1, name="rms_mix_bwd")
    return loss, dx0, g


def _coords():
    return lax.axis_index("x"), lax.axis_index("y"), lax.axis_index("c")


def _other_chips(x, y):
    return [(1 - x, y), (x, 1 - y), (1 - x, 1 - y)]


def _remote(src, dst, send_sem, recv_sem, dev):
    return pltpu.make_async_remote_copy(src_ref=src, dst_ref=dst, send_sem=send_sem, recv_sem=recv_sem,
                                        device_id=dev, device_id_type=MESH)


_ANY = pl.BlockSpec(memory_space=pl.ANY)


def _gather_weights(halved, whole, *, name):
    nh, n = len(halved), len(halved) + len(whole)
    arrays = list(halved) + list(whole)

    def body(*refs):
        ins, outs = refs[:n], refs[n:2 * n]
        send_sems, recv_sems = refs[2 * n:]
        x, y, c = _coords()
        me, sib = 2 * x + y, (x, y, 1 - c)
        chips = _other_chips(x, y)

        def half(i, which):
            h = ins[i].shape[0] // 2
            return pl.ds(pl.multiple_of(which * h, 16), h)

        sends = []
        for i in range(n):
            src, dst = (ins[i].at[half(i, c)], outs[i].at[me, half(i, c)]) if i < nh else (ins[i], outs[i].at[me])
            for k, (cx, cy) in enumerate(chips):
                cp = _remote(src, dst, send_sems.at[i, k], recv_sems.at[i, k], (cx, cy, c))
                cp.start()
                sends.append(cp)
        for i in range(n):
            for k, (cx, cy) in enumerate(chips):
                got = outs[i].at[2 * cx + cy, half(i, c)] if i < nh else outs[i].at[2 * cx + cy]
                _remote(got, got, send_sems.at[i, k], recv_sems.at[i, k], sib).wait_recv()
                if i < nh:
                    cp = _remote(got, got, send_sems.at[i, 3 + k], recv_sems.at[i, 3 + k], sib)
                    cp.start()
                    sends.append(cp)
        for i in range(nh):
            for k, (cx, cy) in enumerate(chips):
                got = outs[i].at[2 * cx + cy, half(i, 1 - c)]
                _remote(got, got, send_sems.at[i, 3 + k], recv_sems.at[i, 3 + k], sib).wait_recv()
        for cp in sends:
            cp.wait_send()

    outs = pl.pallas_call(
        body, name=name, in_specs=[_ANY] * n, out_specs=[_ANY] * n,
        out_shape=[jax.ShapeDtypeStruct((N_CHIPS,) + a.shape, a.dtype) for a in arrays],
        scratch_shapes=[pltpu.SemaphoreType.DMA((n, 6)), pltpu.SemaphoreType.DMA((n, 6))],
        compiler_params=_params(),
    )(*arrays)
    chip = 2 * lax.axis_index("x") + lax.axis_index("y")
    return [lax.dynamic_update_index_in_dim(o, a, chip, 0) for o, a in zip(outs, arrays)]


def _pair_exchange(gs, *, name):
    n = len(gs)

    def body(*refs):
        ins, outs = refs[:n], refs[n:2 * n]
        send_sems, recv_sems = refs[2 * n:]
        x, y, c = _coords()
        copies = []
        for i in range(n):
            for j in range(N_CHIPS):
                cp = _remote(ins[i].at[j, 1 - c], outs[i].at[j], send_sems.at[i, j], recv_sems.at[i, j], (x, y, 1 - c))
                cp.start()
                copies.append(cp)
        for cp in copies:
            cp.wait()

    return pl.pallas_call(
        body, name=name, in_specs=[_ANY] * n, out_specs=[_ANY] * n,
        out_shape=[jax.ShapeDtypeStruct((N_CHIPS,) + a.shape[2:], a.dtype) for a in gs],
        scratch_shapes=[pltpu.SemaphoreType.DMA((n, N_CHIPS)), pltpu.SemaphoreType.DMA((n, N_CHIPS))],
        compiler_params=_params(),
    )(*gs)


def _chip_exchange(ss, *, name):
    n = len(ss)

    def body(*refs):
        ins, outs = refs[:n], refs[n:2 * n]
        send_sems, recv_sems = refs[2 * n:]
        x, y, c = _coords()
        me = 2 * x + y
        chips = _other_chips(x, y)
        sends = []
        for i in range(n):
            for k, (cx, cy) in enumerate(chips):
                cp = _remote(ins[i].at[2 * cx + cy], outs[i].at[me], send_sems.at[i, k], recv_sems.at[i, k], (cx, cy, c))
                cp.start()
                sends.append(cp)
        for i in range(n):
            for k, (cx, cy) in enumerate(chips):
                got = outs[i].at[2 * cx + cy]
                _remote(got, got, send_sems.at[i, k], recv_sems.at[i, k], (cx, cy, c)).wait_recv()
        for cp in sends:
            cp.wait_send()

    return pl.pallas_call(
        body, name=name, in_specs=[_ANY] * n, out_specs=[_ANY] * n,
        out_shape=[jax.ShapeDtypeStruct(a.shape, a.dtype) for a in ss],
        scratch_shapes=[pltpu.SemaphoreType.DMA((n, 3)), pltpu.SemaphoreType.DMA((n, 3))],
        compiler_params=_params(),
    )(*ss)


def _pair_share(hs, *, name):
    n = len(hs)

    def body(*refs):
        ins, outs = refs[:n], refs[n:2 * n]
        send_sems, recv_sems = refs[2 * n:]
        x, y, c = _coords()
        copies = []
        for i in range(n):
            cp = _remote(ins[i], outs[i], send_sems.at[i], recv_sems.at[i], (x, y, 1 - c))
            cp.start()
            copies.append(cp)
        for cp in copies:
            cp.wait()

    return pl.pallas_call(
        body, name=name, in_specs=[_ANY] * n, out_specs=[_ANY] * n,
        out_shape=[jax.ShapeDtypeStruct(a.shape, a.dtype) for a in hs],
        scratch_shapes=[pltpu.SemaphoreType.DMA((n,)), pltpu.SemaphoreType.DMA((n,))],
        compiler_params=_params(),
    )(*hs)


def _all_exchange(vec, *, name):
    def body(v_ref, o_ref, send_sems, recv_sems, local_sem):
        x, y, c = _coords()
        me = 4 * x + 2 * y + c
        local = pltpu.make_async_copy(v_ref, o_ref.at[me], local_sem)
        local.start()
        copies = []
        k = 0
        for dx in (0, 1):
            for dy in (0, 1):
                for dc in (0, 1):
                    if dx or dy or dc:
                        peer = (1 - x if dx else x, 1 - y if dy else y, 1 - c if dc else c)
                        cp = _remote(v_ref, o_ref.at[me], send_sems.at[k], recv_sems.at[k], peer)
                        cp.start()
                        copies.append(cp)
                        k += 1
        for cp in copies:
            cp.wait()
        local.wait()

    return pl.pallas_call(
        body, name=name, in_specs=[_ANY], out_specs=_ANY,
        out_shape=jax.ShapeDtypeStruct((8,) + vec.shape, vec.dtype),
        scratch_shapes=[pltpu.SemaphoreType.DMA((7,)), pltpu.SemaphoreType.DMA((7,)), pltpu.SemaphoreType.DMA(())],
        compiler_params=_params(),
    )(vec)


_HBM = pl.BlockSpec(memory_space=pltpu.HBM)
_SEM = pl.BlockSpec(memory_space=pltpu.SEMAPHORE)
_EFFECT = pltpu.SideEffectType.DATAFLOW_SIDE_EFFECTING


def _copies_start(srcs, lands, plan, n_copies, *, name, after=()):
    ns, n = len(srcs), len(srcs) + len(lands)
    na = len(after)

    def body(*refs):
        send_sems, recv_sems = refs[n + na], refs[n + na + 1]
        token = refs[-1]
        for k, (src, dst, dev) in enumerate(plan(refs[:ns], refs[ns:n])):
            _remote(src, dst, send_sems.at[k], recv_sems.at[k], dev).start()
        token[...] = jnp.zeros_like(token)

    arrays = list(srcs) + list(lands)
    outs = pl.pallas_call(
        body, name=name,
        out_shape=(pltpu.SemaphoreType.DMA((n_copies,)), pltpu.SemaphoreType.DMA((n_copies,)),
                   *[pltpu.HBM(a.shape, a.dtype) for a in arrays], jax.ShapeDtypeStruct((8, LANES), F32)),
        in_specs=[_HBM] * n + [_ANY] * na,
        out_specs=(_SEM, _SEM, *[_HBM] * n, pl.BlockSpec(memory_space=pltpu.VMEM)),
        input_output_aliases={i: 2 + i for i in range(n)},
        compiler_params=pltpu.CompilerParams(has_side_effects=_EFFECT),
    )(*[pltpu.with_memory_space_constraint(a, pltpu.HBM) for a in arrays], *after)
    return outs[0], outs[1], list(outs[2:2 + ns]), list(outs[2 + ns:2 + n]), outs[-1]


def _copies_wait(send_sems, recv_sems, srcs, lands, plan, first, after, *, name):
    ns, n = len(srcs), len(srcs) + len(lands)

    def body(*refs):
        send, recv = refs[n], refs[n + 1]
        for k, (src, dst, dev) in enumerate(plan(refs[:ns], refs[ns:n])):
            cp = _remote(src, dst, send.at[first + k], recv.at[first + k], dev)
            cp.wait_send()
            cp.wait_recv()

    arrays = list(srcs) + list(lands)
    outs = pl.pallas_call(
        body, name=name, out_shape=tuple(pltpu.HBM(a.shape, a.dtype) for a in arrays),
        in_specs=[_HBM] * n + [_SEM, _SEM] + [_ANY] * len(after), out_specs=tuple([_HBM] * n),
        input_output_aliases={i: i for i in range(n)},
        compiler_params=pltpu.CompilerParams(has_side_effects=_EFFECT),
    )(*arrays, send_sems, recv_sems, *after)
    return list(outs[:ns]), list(outs[ns:])


def _gather_plan(halved):
    def plan(srcs, lands):
        x, y, c = _coords()
        me = 2 * x + y
        out = []
        for i, (src, land) in enumerate(zip(srcs, lands)):
            if halved[i]:
                h = src.shape[0] // 2
                rows = pl.ds(pl.multiple_of(c * h, 16), h)
                src, dst = src.at[rows], land.at[me, rows]
            else:
                dst = land.at[me]
            out += [(src, dst, (cx, cy, c)) for cx, cy in _other_chips(x, y)]
        return out
    return plan


def _forward_halves(lands, *, name):
    n = len(lands)

    def body(*refs):
        ins, outs = refs[:n], refs[n:2 * n]
        send_sems, recv_sems = refs[2 * n:]
        x, y, c = _coords()
        copies = []
        for i in range(n):
            h = ins[i].shape[1] // 2
            rows = pl.ds(pl.multiple_of(c * h, 16), h)
            for k, (cx, cy) in enumerate(_other_chips(x, y)):
                cp = _remote(ins[i].at[2 * cx + cy, rows], outs[i].at[2 * cx + cy, rows],
                             send_sems.at[i, k], recv_sems.at[i, k], (x, y, 1 - c))
                cp.start()
                copies.append(cp)
        for cp in copies:
            cp.wait()

    return pl.pallas_call(
        body, name=name, in_specs=[_ANY] * n, out_specs=[_ANY] * n,
        out_shape=[jax.ShapeDtypeStruct(a.shape, a.dtype) for a in lands],
        input_output_aliases={i: i for i in range(n)},
        scratch_shapes=[pltpu.SemaphoreType.DMA((n, 3)), pltpu.SemaphoreType.DMA((n, 3))],
        compiler_params=_params(),
    )(*lands)


def _all_plan(srcs, lands):
    x, y, c = _coords()
    me = 4 * x + 2 * y + c
    out = []
    for src, land in zip(srcs, lands):
        for dx in (0, 1):
            for dy in (0, 1):
                for dc in (0, 1):
                    if dx or dy or dc:
                        out.append((src, land.at[me], (1 - x if dx else x, 1 - y if dy else y, 1 - c if dc else c)))
    return out


def _chip_plan(srcs, lands):
    x, y, c = _coords()
    me = 2 * x + y
    out = []
    for src, land in zip(srcs, lands):
        out += [(src.at[2 * cx + cy], land.at[me], (cx, cy, c)) for cx, cy in _other_chips(x, y)]
    return out


ROW_BLOCK_BYTES = 2 * 1024 * 1024


def _rtile(r, pref, mult, row_bytes=None):
    if row_bytes is not None:
        pref = max(pref, ROW_BLOCK_BYTES // row_bytes)
    t = (min(r, pref) // mult) * mult
    while t >= mult:
        if r % t == 0:
            return t
        t -= mult
    return r


def _pair_add(g, recv, core, *, name):
    _, _, r2, cols = g.shape
    tr = _rtile(r2, 256, 16, row_bytes=2 * cols)

    def body(c_ref, g_ref, r_ref, o_ref):
        o_ref[...] = (g_ref[...].astype(F32) + r_ref[...].astype(F32)).astype(o_ref.dtype)

    blk = pl.BlockSpec((None, tr, cols), lambda j, i, c_ref: (j, i, 0))
    return pl.pallas_call(
        body, name=name,
        grid_spec=pltpu.PrefetchScalarGridSpec(
            num_scalar_prefetch=1, grid=(N_CHIPS, r2 // tr),
            in_specs=[pl.BlockSpec((None, None, tr, cols), lambda j, i, c_ref: (j, c_ref[0], i, 0)), blk],
            out_specs=blk),
        out_shape=jax.ShapeDtypeStruct(recv.shape, recv.dtype), compiler_params=_params(),
    )(core, g, recv)


def _sum_slots(a, out_dtype, *, name):
    n, r, cols = a.shape
    tr = _rtile(r, 256, 16)

    def body(a_ref, o_ref):
        acc = a_ref[0].astype(F32)
        for j in range(1, n):
            acc = acc + a_ref[j].astype(F32)
        o_ref[...] = acc.astype(o_ref.dtype)

    return pl.pallas_call(
        body, name=name, grid=(r // tr,),
        in_specs=[pl.BlockSpec((n, tr, cols), lambda i: (0, i, 0))],
        out_specs=pl.BlockSpec((tr, cols), lambda i: (i, 0)),
        out_shape=jax.ShapeDtypeStruct((r, cols), out_dtype), compiler_params=_params(),
    )(a)


def _chip_sum(own, recv, chip, *, name):
    _, r2, cols = own.shape
    tr = _rtile(r2, 256, 16, row_bytes=2 * cols)

    def body(chip_ref, own_ref, *rest):
        o_ref = rest[-1]
        acc = None
        for j in range(N_CHIPS):
            term = jnp.where(chip_ref[0] == j, own_ref[...], rest[j][...]).astype(F32)
            acc = term if acc is None else acc + term
        o_ref[...] = acc

    def slot(j):
        return pl.BlockSpec((None, tr, cols),
                            lambda i, chip_ref: (jnp.where(chip_ref[0] == j, (j + 1) % N_CHIPS, j), i, 0))

    return pl.pallas_call(
        body, name=name,
        grid_spec=pltpu.PrefetchScalarGridSpec(
            num_scalar_prefetch=1, grid=(r2 // tr,),
            in_specs=[pl.BlockSpec((None, tr, cols), lambda i, chip_ref: (chip_ref[0], i, 0))]
                     + [slot(j) for j in range(N_CHIPS)],
            out_specs=pl.BlockSpec((tr, cols), lambda i, chip_ref: (i, 0))),
        out_shape=jax.ShapeDtypeStruct((r2, cols), F32), compiler_params=_params(),
    )(chip, own, *([recv] * N_CHIPS))


def _adam_update(w, gv, m, v):
    c1 = 1.0 / (1.0 - ADAM_B1 ** ADAM_STEP)
    c2 = 1.0 / (1.0 - ADAM_B2 ** ADAM_STEP)
    nm = ADAM_B1 * m + (1.0 - ADAM_B1) * gv
    nv = ADAM_B2 * v + (1.0 - ADAM_B2) * gv * gv
    return -ADAM_LR * ((nm * c1) / (jnp.sqrt(nv * c2) + ADAM_EPS) + ADAM_WD * w), nm, nv


def _adamw_halves(w, g_mine, g_other, m, v, core, *, name):
    r, cols = w.shape
    r2 = r // 2
    tr = _rtile(r2, 256, 8, row_bytes=4 * cols)
    nt = r2 // tr

    def body(core_ref, w_ref, gm_ref, go_ref, m_ref, v_ref, g_ref, d_ref, nm_ref, nv_ref):
        gv = jnp.where(pl.program_id(0) == core_ref[0], gm_ref[...], go_ref[...])
        g_ref[...] = gv
        d_ref[...], nm_ref[...], nv_ref[...] = _adam_update(w_ref[...], gv, m_ref[...], v_ref[...])

    full = pl.BlockSpec((tr, cols), lambda hf, i, core_ref: (hf * nt + i, 0))
    half = pl.BlockSpec((tr, cols), lambda hf, i, core_ref: (i, 0))
    shape = jax.ShapeDtypeStruct((r, cols), F32)
    return pl.pallas_call(
        body, name=name,
        grid_spec=pltpu.PrefetchScalarGridSpec(
            num_scalar_prefetch=1, grid=(2, nt), in_specs=[full, half, half, full, full], out_specs=[full] * 4),
        out_shape=[shape] * 4, compiler_params=_params(),
    )(core, w, g_mine, g_other, m, v)


def _adamw_split_rows(w, g_mine, g_other, m, v, core, *, name, tc=256):
    r, cols = w.shape
    r2 = g_mine.shape[0]
    tc = _tile(cols, tc)

    def body(core_ref, w_ref, gm_ref, go_ref, m_ref, v_ref, g_ref, d_ref, nm_ref, nv_ref):
        mine_first = core_ref[0] == 0
        for lo, hi, first in ((0, r2, True), (r2, r, False)):
            n = hi - lo
            gm, go = gm_ref[0:n, :], go_ref[0:n, :]
            gv = jnp.where(mine_first, gm, go) if first else jnp.where(mine_first, go, gm)
            g_ref[lo:hi, :] = gv
            d_ref[lo:hi, :], nm_ref[lo:hi, :], nv_ref[lo:hi, :] = _adam_update(
                w_ref[lo:hi, :], gv, m_ref[lo:hi, :], v_ref[lo:hi, :])

    full = pl.BlockSpec((r, tc), lambda j, core_ref: (0, j))
    half = pl.BlockSpec((r2, tc), lambda j, core_ref: (0, j))
    shape = jax.ShapeDtypeStruct((r, cols), F32)
    return pl.pallas_call(
        body, name=name,
        grid_spec=pltpu.PrefetchScalarGridSpec(
            num_scalar_prefetch=1, grid=(cols // tc,), in_specs=[full, half, half, full, full],
            out_specs=[full] * 4),
        out_shape=[shape] * 4, compiler_params=_params(),
    )(core, w, g_mine, g_other, m, v)


def _adamw(w, g, m, v, *, name, rows=256):
    r, cols = w.shape
    tr = _rtile(r, rows, 8)

    def body(w_ref, g_ref, m_ref, v_ref, d_ref, nm_ref, nv_ref):
        d_ref[...], nm_ref[...], nv_ref[...] = _adam_update(w_ref[...], g_ref[...], m_ref[...], v_ref[...])

    blk = pl.BlockSpec((tr, cols), lambda i: (i, 0))
    shape = jax.ShapeDtypeStruct((r, cols), F32)
    return pl.pallas_call(
        body, name=name, grid=(r // tr,), in_specs=[blk] * 4, out_specs=[blk] * 3,
        out_shape=[shape] * 3, compiler_params=_params(),
    )(w, g, m, v)


_BIG = (("w_in", 1), ("w_branch_a", 0), ("w_branch_b", 0), ("w_out", 0), ("w_up", 1), ("w_down", 0),
        ("w_ple", 1), ("w_ple_gate", 0))
_SMALL = ("b_f", "gmlp_ln_g", "gmlp_ln_b", "gmlp_w_s", "gmlp_b_s", "norm_ffn_g", "conv_b", "norm_ple_g",
          "norm_final_g", "norm_mix_g")
_WEIGHTS = ("norm_mix_g", "w_in", "b_f", "gmlp_ln_g", "gmlp_ln_b", "gmlp_w_s", "gmlp_b_s", "w_branch_a",
            "w_branch_b", "w_out", "norm_ffn_g", "w_up", "conv_w", "conv_b", "w_down", "norm_ple_g", "w_ple",
            "w_ple_gate", "norm_final_g")
_PACK_ROWS = 8


def _pack(arrays):
    parts = []
    for a in arrays:
        flat = a.reshape(-1)
        unit = _PACK_ROWS * LANES
        flat = jnp.pad(flat, (0, (-flat.shape[0]) % unit))
        parts.append(flat.reshape(-1, LANES))
    return jnp.concatenate(parts, axis=0)


def _unpack(packed, shapes):
    out, row = [], 0
    for shp in shapes:
        size = math.prod(shp)
        rows = -(-size // (_PACK_ROWS * LANES)) * _PACK_ROWS
        out.append(packed[row:row + rows].reshape(-1)[:size].reshape(shp))
        row += rows
    return out


def _take_cols(parts, lo, hi):
    out, start = [], 0
    for a in parts:
        width = a.shape[1]
        a0, a1 = max(lo, start) - start, min(hi, start + width) - start
        if a1 > a0:
            out.append(a if (a0, a1) == (0, width) else a[:, a0:a1])
        start += width
    return out[0] if len(out) == 1 else jnp.concatenate(out, axis=1)


def _take_rows(parts, lo, hi):
    out, start = [], 0
    for a in parts:
        height = a.shape[0]
        a0, a1 = max(lo, start) - start, min(hi, start + height) - start
        if a1 > a0:
            out.append(a if (a0, a1) == (0, height) else a[a0:a1])
        start += height
    return out[0] if len(out) == 1 else jnp.concatenate(out, axis=0)


def _assemble(gathered, axis):
    n, r, cols = gathered.shape
    if axis == 0:
        return gathered.reshape(n * r, cols)
    return _take_cols([gathered[j] for j in range(n)], 0, n * cols)


def _to_chunks(parts, axis):
    rows, total = parts[0].shape[0], sum(a.shape[1] for a in parts)
    if axis == 0:
        r, cols = rows // N_CHIPS, total
        chunks = _take_cols(parts, 0, total).reshape(N_CHIPS, r, cols)
    else:
        r, cols = rows, total // N_CHIPS
        chunks = jnp.stack([_take_cols(parts, j * cols, (j + 1) * cols) for j in range(N_CHIPS)])
    return chunks.reshape(N_CHIPS, 2, r // 2, cols)


def kernel(x, p, norm_mix_g, w_in, b_f, gmlp_ln_g, gmlp_ln_b, gmlp_w_s, gmlp_b_s, w_branch_a, w_branch_b, w_out, norm_ffn_g, w_up, conv_w, conv_b, w_down, norm_ple_g, w_ple, w_ple_gate, norm_final_g, loss_target, m_norm_mix_g, m_w_in, m_b_f, m_gmlp_ln_g, m_gmlp_ln_b, m_gmlp_w_s, m_gmlp_b_s, m_w_branch_a, m_w_branch_b, m_w_out, m_norm_ffn_g, m_w_up, m_conv_w, m_conv_b, m_w_down, m_norm_ple_g, m_w_ple, m_w_ple_gate, m_norm_final_g, v_norm_mix_g, v_w_in, v_b_f, v_gmlp_ln_g, v_gmlp_ln_b, v_gmlp_w_s, v_gmlp_b_s, v_w_branch_a, v_w_branch_b, v_w_out, v_norm_ffn_g, v_w_up, v_conv_w, v_conv_b, v_w_down, v_norm_ple_g, v_w_ple, v_w_ple_gate, v_norm_final_g):
    args = dict(locals())
    wt = {n: args[n] for n in _WEIGHTS}
    mom = {n: args["m_" + n] for n in _WEIGHTS}
    var = {n: args["v_" + n] for n in _WEIGHTS}
    chip = 2 * lax.axis_index("x") + lax.axis_index("y")
    core = lax.axis_index("c").astype(jnp.int32).reshape(1)

    chip1 = chip.astype(jnp.int32).reshape(1)
    device = 2 * chip + lax.axis_index("c")
    axis_of = dict(_BIG)
    names = [n for n, _ in _BIG]
    put_mine = lambda land, mine: lax.dynamic_update_index_in_dim(land, mine, chip, 0)

    in_cols = w_in.shape[2]
    lane_rows = D_MODEL // LANES
    in_lin = lambda a: jnp.swapaxes(a[0], 0, 1).reshape(in_cols * lane_rows, LANES)
    shard_in = in_lin(w_in).astype(BF16)
    sems_in = _copies_start([shard_in], [lax.empty((N_CHIPS,) + shard_in.shape, BF16)], _gather_plan([True]), 3,
                            name="gather_start_in")
    shards = [wt[n][0].astype(BF16) for n in names[1:]] + [conv_w[0]]
    halved = [True] * len(names[1:]) + [False]
    lands = [lax.empty((N_CHIPS,) + a.shape, a.dtype) for a in shards]
    send_sems, recv_sems, srcs, lands, _ = _copies_start(shards, lands, _gather_plan(halved), 3 * len(shards),
                                                         name="gather_start_rest", after=[sems_in[4]])
    o1 = 2 * GMLP_WIDTH
    o2 = o1 + 3 * FOX_WIDTH
    o3 = o2 + FOX_HEADS
    fpad = ((0, 0), (0, LANES - FOX_HEADS))
    w = {
        "conv_b": conv_b, "norm_mix_g": norm_mix_g, "norm_ffn_g": norm_ffn_g, "norm_ple_g": norm_ple_g,
        "norm_final_g": norm_final_g.reshape(1, D_MODEL), "b_f": jnp.pad(b_f, fpad),
        "gmlp_ln_g": gmlp_ln_g, "gmlp_ln_b": gmlp_ln_b, "gmlp_w_s": gmlp_w_s[0],
        "gmlp_b_s_t": jnp.pad(gmlp_b_s[0].T, ((0, 0), (0, LANES - GMLP_GROUPS))),
        "first_dep": sems_in[4],
    }

    def get_w_in(after):
        _, got = _copies_wait(sems_in[0], sems_in[1], sems_in[2], sems_in[3], _gather_plan([True]), 0, [after],
                              name="gather_wait_in")
        got = _forward_halves(got, name="gather_forward_in")
        full_t = put_mine(got[0], shard_in).reshape(N_CHIPS * in_cols, D_MODEL)
        return {"w_uv_t": full_t[:o1], "w_qkv_t": full_t[o1:o2],
                "w_f_t": jnp.pad(full_t[o2:o3], ((0, LANES - FOX_HEADS), (0, 0))), "w_g_t": full_t[o3:]}

    def get_w_rest(after):
        _, got = _copies_wait(send_sems, recv_sems, srcs, lands, _gather_plan(halved), 0, [after],
                              name="gather_wait_rest")
        got = list(_forward_halves(got[:-1], name="gather_forward_rest")) + got[-1:]
        slots = {n: put_mine(got[i], shards[i]) for i, n in enumerate(names[1:])}
        full = {n: _assemble(slots[n], axis_of[n]) for n in names[1:] if n != "w_up"}
        up = [slots["w_up"][j] for j in range(N_CHIPS)]
        return {"w_branch_a": full["w_branch_a"], "w_branch_b": full["w_branch_b"], "w_out": full["w_out"],
                "w_up_a": _take_cols(up, 0, D_FF), "w_up_b": _take_cols(up, D_FF, 2 * D_FF),
                "w_down": full["w_down"], "w_ple": full["w_ple"], "w_ple_gate": full["w_ple_gate"],
                "conv_w": _assemble(put_mine(got[-1], shards[-1]), 1)}

    grads, delta, new_m, new_v = {}, {}, {}, {}
    pending = {}

    def to_chunks(n, gr):
        if n == "w_in":
            return jnp.concatenate(gr, axis=0).reshape(N_CHIPS, 2, in_cols * lane_rows // 2, LANES)
        return _to_chunks(gr if isinstance(gr, list) else [gr], axis_of[n])

    def reduce_start(group, gfull, tag):
        chunks = [to_chunks(n, gfull[n]) for n in group]
        from_sibling = _pair_exchange(chunks, name="grad_pair_exchange_" + tag)
        pair_sums = [_pair_add(chunks[i], from_sibling[i], core, name="grad_pair_add_" + n) for i, n in enumerate(group)]
        empty = [lax.empty(a.shape, a.dtype) for a in pair_sums]
        ssem, rsem, own, recv, token = _copies_start(pair_sums, empty, _chip_plan, 3 * len(group),
                                                     name="grad_chip_start_" + tag)
        pending[tag] = (ssem, rsem, own, recv)
        return token

    def reduce_finish(group, tag, after):
        ssem, rsem, own, recv = pending[tag]
        own, recv = _copies_wait(ssem, rsem, own, recv, _chip_plan, 0, after, name="grad_chip_wait_" + tag)
        halves = [_chip_sum(own[i], recv[i], chip1, name="grad_chip_sum_" + n) for i, n in enumerate(group)]
        other_halves = _pair_share(halves, name="grad_pair_share_" + tag)
        for i, n in enumerate(group):
            shp = wt[n].shape
            if n == "w_in":
                outs = _adamw_halves(in_lin(wt[n]), halves[i], other_halves[i], in_lin(mom[n]), in_lin(var[n]), core,
                                     name="adamw_" + n)
                outs = [jnp.swapaxes(o.reshape(in_cols, D_MODEL), 0, 1) for o in outs]
            else:
                outs = _adamw_halves(wt[n].reshape(shp[-2:]), halves[i], other_halves[i], mom[n].reshape(shp[-2:]),
                                     var[n].reshape(shp[-2:]), core, name="adamw_" + n)
            grads[n], delta[n], new_m[n], new_v[n] = (o.reshape(shp) for o in outs)
        return new_v[group[-1]]

    ffn_group = ("w_up", "w_down", "w_ple", "w_ple_gate")
    mix_group = ("w_in", "w_branch_a", "w_branch_b", "w_out")

    def on_grads_ffn(g):
        gfull = dict(g)
        gfull["w_up"] = [g["w_up_a"], g["w_up_b"]]
        return reduce_start(ffn_group, gfull, "ffn")

    def on_grads_mix(g):
        early = [g[n] if n != "b_f" else g[n][:, :FOX_HEADS] for n in _SMALL[:-1]] + [g["conv_w"]]
        vec = _pack(early)
        ssem, rsem, own, recv, small_token = _copies_start(
            [vec], [lax.empty((8,) + vec.shape, F32)], _all_plan, 7, name="small_start")
        pending["small"] = (ssem, rsem, own, recv)
        gfull = dict(g)
        gfull["w_in"] = [g["w_uv_t"], g["w_qkv_t"], g["w_f_t"][:FOX_HEADS], g["w_g_t"]]
        token = reduce_start(mix_group, gfull, "mix")
        pending["ffn_done"] = reduce_finish(ffn_group, "ffn", [token])
        return token + small_token

    loss, grad_x, g = _device_step(x[0], p[0, 0], loss_target[0], w, get_w_in, get_w_rest, on_grads_ffn, on_grads_mix)

    ssem, rsem, own, recv = pending["small"]
    own, recv = _copies_wait(ssem, rsem, own, recv, _all_plan, 0, [grad_x, pending["ffn_done"]], name="small_wait")
    vec_early = _sum_slots(lax.dynamic_update_index_in_dim(recv[0], own[0], device, 0), F32, name="small_sum")
    vec_late = _pack([g["norm_mix_g"]])
    vec_late = _sum_slots(_all_exchange(vec_late, name="small_exchange_late"), F32, name="small_sum_late")
    early_rows = _pack([wt[n] for n in _SMALL[:-1]]).shape[0]
    vec = jnp.concatenate([vec_early[:early_rows], vec_late], axis=0)
    for n, a in zip(_SMALL, _unpack(vec, [wt[n].shape for n in _SMALL])):
        grads[n] = a
    conv_w_grad = _unpack(vec_early[early_rows:], [(3, 2 * D_FF)])[0]
    grads["conv_w"] = lax.dynamic_slice_in_dim(conv_w_grad, chip * conv_w.shape[2], conv_w.shape[2], axis=1).reshape(conv_w.shape)

    reduce_finish(mix_group, "mix", [grad_x, pending["ffn_done"], vec])
    shp = conv_w.shape
    outs = _adamw(conv_w.reshape(shp[-2:]), grads["conv_w"].reshape(shp[-2:]), m_conv_w.reshape(shp[-2:]),
                  v_conv_w.reshape(shp[-2:]), name="adamw_conv_w")
    delta["conv_w"], new_m["conv_w"], new_v["conv_w"] = (o.reshape(shp) for o in outs)
    outs = _adamw(_pack([wt[n] for n in _SMALL]), vec, _pack([mom[n] for n in _SMALL]),
                  _pack([var[n] for n in _SMALL]), name="adamw_small", rows=2048)
    for d, o in zip((delta, new_m, new_v), outs):
        for n, a in zip(_SMALL, _unpack(o, [wt[n].shape for n in _SMALL])):
            d[n] = a

    total_loss = lax.psum(loss[0, 0], ("x", "y", "c"))
    return (total_loss, grad_x.reshape(x.shape), *[grads[n] for n in _WEIGHTS], *[delta[n] for n in _WEIGHTS],
            *[new_m[n] for n in _WEIGHTS], *[new_v[n] for n in _WEIGHTS])
```

```python
import functools
import math

import jax
import jax.numpy as jnp
from jax import lax
from jax.experimental import pallas as pl
from jax.experimental.pallas import tpu as pltpu

F32 = jnp.float32
BF16 = jnp.bfloat16

D_MODEL = 1024
EPS = 1e-6
CHUNK = 64
GMLP_GROUPS = 8
GMLP_BLOCK = 128
GMLP_WIDTH = 1024
FOX_HEADS = 16
FOX_HEAD_DIM = 64
FOX_WIDTH = 1024
HEAD_PAIRS = FOX_HEADS // 2
ATT_BLOCK = 128
D_FF = 2816
PLE_DIM = 256
LANES = 128
BF16_TILE_ROWS = 16
N_CHIPS = 4

ADAM_LR = 0.001
ADAM_B1 = 0.9
ADAM_B2 = 0.999
ADAM_EPS = 1e-08
ADAM_WD = 0.01
ADAM_STEP = 10

VMEM_LIMIT = 56 * 1024 * 1024
MESH = pl.DeviceIdType.MESH

_NN = (((1,), (0,)), ((), ()))
_NT = (((1,), (1,)), ((), ()))
_TN = (((0,), (0,)), ((), ()))


def _params(**kw):
    return pltpu.CompilerParams(vmem_limit_bytes=VMEM_LIMIT, **kw)


def _tile(dim, pref):
    if dim <= pref:
        return dim
    t = (pref // LANES) * LANES
    while t >= LANES:
        if dim % t == 0:
            return t
        t -= LANES
    return dim


def _dot(a, b, dn):
    return lax.dot_general(a.astype(BF16), b.astype(BF16), dn, preferred_element_type=F32)


def _gelu(x):
    c = math.sqrt(2.0 / math.pi)
    t = jnp.tanh(c * (x + 0.044715 * x * x * x))
    return 0.5 * x * (1.0 + t)


def _gelu_and_grad(x):
    c = math.sqrt(2.0 / math.pi)
    x2 = x * x
    t = jnp.tanh(c * (x + 0.044715 * x2 * x))
    g = 0.5 * x * (1.0 + t)
    dg = 0.5 * (1.0 + t) + 0.5 * x * (1.0 - t * t) * c * (1.0 + 3.0 * 0.044715 * x2)
    return g, dg


def _sigmoid(x):
    return 1.0 / (1.0 + jnp.exp(-x))


def _mm(a, b, *, mode, out_dtype, name, add=None, tm=512, tn=512, dep=None):
    if mode == "nn":
        m, k = a.shape
        k2, n = b.shape
    elif mode == "nt":
        m, k = a.shape
        n, k2 = b.shape
    else:
        k, m = a.shape
        k2, n = b.shape
    assert k == k2, (name, a.shape, b.shape)
    tm = _tile(m, tm)
    tn = _tile(n, tn)
    dn = {"nn": _NN, "nt": _NT, "tn": _TN}[mode]

    def body(a_ref, b_ref, *rest):
        o_ref = rest[-1]
        acc = _dot(a_ref[...], b_ref[...], dn)
        if add is not None:
            acc = acc + rest[0][...].astype(F32)
        o_ref[...] = acc.astype(o_ref.dtype)

    a_spec = pl.BlockSpec((k, tm), lambda i, j: (0, i)) if mode == "tn" else pl.BlockSpec((tm, k), lambda i, j: (i, 0))
    b_spec = pl.BlockSpec((tn, k), lambda i, j: (j, 0)) if mode == "nt" else pl.BlockSpec((k, tn), lambda i, j: (0, j))
    o_spec = pl.BlockSpec((tm, tn), lambda i, j: (i, j))
    in_specs = [a_spec, b_spec]
    args = [a, b]
    if add is not None:
        in_specs.append(o_spec)
        args.append(add)
    if dep is not None:
        in_specs.append(pl.BlockSpec(memory_space=pl.ANY))
        args.append(dep)
    return pl.pallas_call(
        body, name=name, grid=(m // tm, n // tn), in_specs=in_specs, out_specs=o_spec,
        out_shape=jax.ShapeDtypeStruct((m, n), out_dtype), compiler_params=_params(),
    )(*args)


def _mm_nt_sum(pairs, *, out_dtype, name, tm=256, dep=None):
    m, n = pairs[0][0].shape[0], pairs[0][1].shape[0]
    tm = _tile(m, tm)
    np_ = len(pairs)

    def body(*refs):
        o_ref = refs[-1] if dep is None else refs[-1]
        acc = None
        for p in range(np_):
            part = _dot(refs[2 * p][...], refs[2 * p + 1][...], _NT)
            acc = part if acc is None else acc + part
        o_ref[...] = acc.astype(o_ref.dtype)

    in_specs, args = [], []
    for a, b in pairs:
        assert a.shape[0] == m and b.shape[0] == n and a.shape[1] == b.shape[1], (name, a.shape, b.shape)
        in_specs += [pl.BlockSpec((tm, a.shape[1]), lambda i: (i, 0)), pl.BlockSpec(b.shape, lambda i: (0, 0))]
        args += [a, b]
    if dep is not None:
        in_specs.append(pl.BlockSpec(memory_space=pl.ANY))
        args.append(dep)
    return pl.pallas_call(
        body, name=name, grid=(m // tm,), in_specs=in_specs, out_specs=pl.BlockSpec((tm, n), lambda i: (i, 0)),
        out_shape=jax.ShapeDtypeStruct((m, n), out_dtype), compiler_params=_params(),
    )(*args)


def _rms_fwd(x, g, *, name, tm=256, dep=None):
    s, d = x.shape
    tm = _tile(s, tm)

    def body(x_ref, g_ref, *rest):
        h_ref = rest[-1]
        xv = x_ref[...]
        r = lax.rsqrt(jnp.mean(xv * xv, axis=-1, keepdims=True) + EPS)
        h_ref[...] = (xv * r * g_ref[...]).astype(h_ref.dtype)

    deps = [] if dep is None else [dep]
    return pl.pallas_call(
        body, name=name, grid=(s // tm,),
        in_specs=[pl.BlockSpec((tm, d), lambda i: (i, 0)), pl.BlockSpec((1, d), lambda i: (0, 0))]
                 + [pl.BlockSpec(memory_space=pl.ANY)] * len(deps),
        out_specs=pl.BlockSpec((tm, d), lambda i: (i, 0)),
        out_shape=jax.ShapeDtypeStruct((s, d), BF16), compiler_params=_params(),
    )(x, g, *deps)


def _rms_bwd(x, g, dh, dres, *, name, tm=256):
    s, d = x.shape
    tm = _tile(s, tm)

    def body(x_ref, g_ref, dh_ref, dres_ref, dx_ref, dxb_ref, dg_ref):
        xv = x_ref[...]
        r = lax.rsqrt(jnp.mean(xv * xv, axis=-1, keepdims=True) + EPS)
        xhat = xv * r
        dhv = dh_ref[...].astype(F32)
        dyg = dhv * g_ref[...]
        dx = dres_ref[...] + r * (dyg - xhat * jnp.mean(dyg * xhat, axis=-1, keepdims=True))
        dx_ref[...] = dx
        dxb_ref[...] = dx.astype(dxb_ref.dtype)

        @pl.when(pl.program_id(0) == 0)
        def _():
            dg_ref[...] = jnp.zeros_like(dg_ref)

        dg_ref[...] += jnp.sum(dhv * xhat, axis=0, keepdims=True)

    row = pl.BlockSpec((tm, d), lambda i: (i, 0))
    vec = pl.BlockSpec((1, d), lambda i: (0, 0))
    return pl.pallas_call(
        body, name=name, grid=(s // tm,), in_specs=[row, vec, row, row], out_specs=[row, row, vec],
        out_shape=[jax.ShapeDtypeStruct((s, d), F32), jax.ShapeDtypeStruct((s, d), BF16),
                   jax.ShapeDtypeStruct((1, d), F32)],
        compiler_params=_params(),
    )(x, g, dh, dres)


def _gmlp_mask():
    t = lax.broadcasted_iota(jnp.int32, (GMLP_BLOCK, GMLP_BLOCK), 0)
    s_ = lax.broadcasted_iota(jnp.int32, (GMLP_BLOCK, GMLP_BLOCK), 1)
    return (s_ // CHUNK) <= (t // CHUNK)


def _gmlp_norm(zv, ln_g, ln_b):
    vv, dvv = _gelu_and_grad(zv)
    mu = jnp.mean(vv, axis=-1, keepdims=True)
    xc = vv - mu
    rstd = lax.rsqrt(jnp.mean(xc * xc, axis=-1, keepdims=True) + EPS)
    vhat = xc * rstd
    return vhat * ln_g + ln_b, vhat, rstd, dvv


def _gmlp_fwd(z_uv, ln_g, ln_b, w_s, b_s_t, *, name):
    s = z_uv.shape[0]
    w = GMLP_WIDTH
    gd = w // GMLP_GROUPS

    def body(z_ref, lg_ref, lb_ref, ws_ref, bs_ref, a_ref):
        u = _gelu(z_ref[:, :w].astype(F32))
        vn, _, _, _ = _gmlp_norm(z_ref[:, w:].astype(F32), lg_ref[...], lb_ref[...])
        mask = _gmlp_mask()
        for g in range(GMLP_GROUPS):
            wm = jnp.where(mask, ws_ref[g], 0.0)
            mixed = _dot(wm, vn[:, g * gd:(g + 1) * gd], _NN) + bs_ref[:, g:g + 1]
            a_ref[:, g * gd:(g + 1) * gd] = (u[:, g * gd:(g + 1) * gd] * mixed).astype(a_ref.dtype)

    full = lambda shape: pl.BlockSpec(shape, lambda i: (0,) * len(shape))
    return pl.pallas_call(
        body, name=name, grid=(s // GMLP_BLOCK,),
        in_specs=[pl.BlockSpec((GMLP_BLOCK, 2 * w), lambda i: (i, 0)), full((1, w)), full((1, w)),
                  full((GMLP_GROUPS, GMLP_BLOCK, GMLP_BLOCK)), full((GMLP_BLOCK, LANES))],
        out_specs=pl.BlockSpec((GMLP_BLOCK, w), lambda i: (i, 0)),
        out_shape=jax.ShapeDtypeStruct((s, w), BF16), compiler_params=_params(),
    )(z_uv, ln_g, ln_b, w_s, b_s_t)


def _gmlp_bwd(z_uv, da, ln_g, ln_b, w_s, b_s_t, *, name):
    s = z_uv.shape[0]
    w = GMLP_WIDTH
    gd = w // GMLP_GROUPS

    def body(z_ref, da_ref, lg_ref, lb_ref, ws_ref, bs_ref, dz_ref, dws_ref, dbs_ref, dlg_ref, dlb_ref):
        @pl.when(pl.program_id(0) == 0)
        def _():
            dws_ref[...] = jnp.zeros_like(dws_ref)
            dbs_ref[...] = jnp.zeros_like(dbs_ref)
            dlg_ref[...] = jnp.zeros_like(dlg_ref)
            dlb_ref[...] = jnp.zeros_like(dlb_ref)

        u, du_dz = _gelu_and_grad(z_ref[:, :w].astype(F32))
        lg = lg_ref[...]
        vn, vhat, rstd, dvv_dz = _gmlp_norm(z_ref[:, w:].astype(F32), lg, lb_ref[...])
        dav = da_ref[...].astype(F32)
        mask = _gmlp_mask()
        lane = lax.broadcasted_iota(jnp.int32, (GMLP_BLOCK, LANES), 1)
        dvn_parts = []
        dbs = jnp.zeros((GMLP_BLOCK, LANES), F32)
        for g in range(GMLP_GROUPS):
            sl = slice(g * gd, (g + 1) * gd)
            wm = jnp.where(mask, ws_ref[g], 0.0)
            vn_g = vn[:, sl]
            mixed = _dot(wm, vn_g, _NN) + bs_ref[:, g:g + 1]
            dmixed = dav[:, sl] * u[:, sl]
            dz_ref[:, sl] = (dav[:, sl] * mixed * du_dz[:, sl]).astype(dz_ref.dtype)
            dvn_parts.append(_dot(wm, dmixed, _TN))
            dws_ref[g] += jnp.where(mask, _dot(dmixed, vn_g, _NT), 0.0)
            dbs = dbs + jnp.where(lane == g, jnp.sum(dmixed, axis=-1, keepdims=True), 0.0)
        dbs_ref[...] += dbs
        dvn = jnp.concatenate(dvn_parts, axis=-1)
        dlg_ref[...] += jnp.sum(dvn * vhat, axis=0, keepdims=True)
        dlb_ref[...] += jnp.sum(dvn, axis=0, keepdims=True)
        dyg = dvn * lg
        dvv = rstd * (dyg - jnp.mean(dyg, axis=-1, keepdims=True)
                      - vhat * jnp.mean(dyg * vhat, axis=-1, keepdims=True))
        dz_ref[:, w:] = (dvv * dvv_dz).astype(dz_ref.dtype)

    full = lambda shape: pl.BlockSpec(shape, lambda i: (0,) * len(shape))
    return pl.pallas_call(
        body, name=name, grid=(s // GMLP_BLOCK,),
        in_specs=[pl.BlockSpec((GMLP_BLOCK, 2 * w), lambda i: (i, 0)),
                  pl.BlockSpec((GMLP_BLOCK, w), lambda i: (i, 0)), full((1, w)), full((1, w)),
                  full((GMLP_GROUPS, GMLP_BLOCK, GMLP_BLOCK)), full((GMLP_BLOCK, LANES))],
        out_specs=[pl.BlockSpec((GMLP_BLOCK, 2 * w), lambda i: (i, 0)),
                   full((GMLP_GROUPS, GMLP_BLOCK, GMLP_BLOCK)), full((GMLP_BLOCK, LANES)),
                   full((1, w)), full((1, w))],
        out_shape=[jax.ShapeDtypeStruct((s, 2 * w), BF16),
                   jax.ShapeDtypeStruct((GMLP_GROUPS, GMLP_BLOCK, GMLP_BLOCK), F32),
                   jax.ShapeDtypeStruct((GMLP_BLOCK, LANES), F32),
                   jax.ShapeDtypeStruct((1, w), F32), jax.ShapeDtypeStruct((1, w), F32)],
        compiler_params=_params(),
    )(z_uv, da, ln_g, ln_b, w_s, b_s_t)


def _tri(lower):
    r = lax.broadcasted_iota(jnp.int32, (ATT_BLOCK, ATT_BLOCK), 0)
    c = lax.broadcasted_iota(jnp.int32, (ATT_BLOCK, ATT_BLOCK), 1)
    return jnp.where((c <= r) if lower else (c >= r), 1.0, 0.0).astype(F32)


def _log_sigmoid(x):
    return jnp.minimum(x, 0.0) - jnp.log(1.0 + jnp.exp(-jnp.abs(x)))


def _fox_cum(f, b_f, *, name):
    s = f.shape[0]
    nb = s // ATT_BLOCK

    def body(f_ref, b_ref, cb_ref, ct_ref, carry):
        @pl.when(pl.program_id(0) == 0)
        def _():
            carry[...] = jnp.zeros_like(carry)

        lf = _log_sigmoid(f_ref[...] + b_ref[...])
        cum = lax.dot_general(_tri(True), lf, _NN, precision=lax.Precision.HIGHEST,
                              preferred_element_type=F32) + carry[...]
        carry[...] = cum[ATT_BLOCK - 1:ATT_BLOCK, :]
        for h in range(FOX_HEADS):
            cb_ref[h] = jnp.broadcast_to(cum[:, h:h + 1], (ATT_BLOCK, LANES))
        ct_ref[...] = cum.T

    return pl.pallas_call(
        body, name=name, grid=(nb,),
        in_specs=[pl.BlockSpec((ATT_BLOCK, LANES), lambda i: (i, 0)), pl.BlockSpec((1, LANES), lambda i: (0, 0))],
        out_specs=[pl.BlockSpec((FOX_HEADS, ATT_BLOCK, LANES), lambda i: (0, i, 0)),
                   pl.BlockSpec((LANES, ATT_BLOCK), lambda i: (0, i))],
        out_shape=[jax.ShapeDtypeStruct((FOX_HEADS, s, LANES), F32), jax.ShapeDtypeStruct((LANES, s), F32)],
        scratch_shapes=[pltpu.VMEM((1, LANES), F32)], compiler_params=_params(),
    )(f, b_f)


def _fox_dlogit(dcum_t, f, b_f, *, name):
    s = f.shape[0]
    nb = s // ATT_BLOCK

    def body(dc_ref, f_ref, b_ref, df_ref, db_ref, carry):
        @pl.when(pl.program_id(0) == 0)
        def _():
            carry[...] = jnp.zeros_like(carry)
            db_ref[...] = jnp.zeros_like(db_ref)

        d = dc_ref[...].T
        dlog = lax.dot_general(_tri(False), d, _NN, precision=lax.Precision.HIGHEST,
                               preferred_element_type=F32) + carry[...]
        carry[...] = dlog[0:1, :]
        df = dlog * (1.0 - _sigmoid(f_ref[...] + b_ref[...]))
        df_ref[...] = df
        db_ref[...] += jnp.sum(df, axis=0, keepdims=True)

    rev = lambda i: nb - 1 - i
    return pl.pallas_call(
        body, name=name, grid=(nb,),
        in_specs=[pl.BlockSpec((LANES, ATT_BLOCK), lambda i: (0, rev(i))),
                  pl.BlockSpec((ATT_BLOCK, LANES), lambda i: (rev(i), 0)),
                  pl.BlockSpec((1, LANES), lambda i: (0, 0))],
        out_specs=[pl.BlockSpec((ATT_BLOCK, LANES), lambda i: (rev(i), 0)),
                   pl.BlockSpec((1, LANES), lambda i: (0, 0))],
        out_shape=[jax.ShapeDtypeStruct((s, LANES), F32), jax.ShapeDtypeStruct((1, LANES), F32)],
        scratch_shapes=[pltpu.VMEM((1, LANES), F32)], compiler_params=_params(),
    )(dcum_t, f, b_f)


def _causal(qi, ki):
    r = lax.broadcasted_iota(jnp.int32, (ATT_BLOCK, ATT_BLOCK), 0) + qi * ATT_BLOCK
    c = lax.broadcasted_iota(jnp.int32, (ATT_BLOCK, ATT_BLOCK), 1) + ki * ATT_BLOCK
    return c <= r


def _head_mask():
    return lax.broadcasted_iota(jnp.int32, (1, LANES), 1) < FOX_HEAD_DIM


def _attn_fwd(qkv, cum_b, cum_r, *, name):
    s = qkv.shape[0]
    nq = s // ATT_BLOCK
    scale = FOX_HEAD_DIM ** -0.5
    npair = HEAD_PAIRS

    def body(q_ref, k_ref, v_ref, cq_ref, ck_ref, o_ref, l_ref):
        qi = pl.program_id(1)
        m0 = _head_mask()
        q2 = q_ref[...]
        zero = jnp.zeros_like(q2)
        qs = (jnp.where(m0, q2, zero), jnp.where(m0, zero, q2))
        cqs = (cq_ref[0], cq_ref[1])

        def step(ki, carry, masked):
            off = pl.multiple_of(ki * ATT_BLOCK, ATT_BLOCK)
            k2 = k_ref[pl.ds(off, ATT_BLOCK), :]
            v2 = v_ref[pl.ds(off, ATT_BLOCK), :]
            out = []
            for hh in range(2):
                m, l, acc = carry[hh]
                sc = _dot(qs[hh], k2, _NT) * scale + (cqs[hh] - ck_ref[hh:hh + 1, pl.ds(off, ATT_BLOCK)])
                if masked:
                    sc = jnp.where(_causal(qi, ki), sc, -1e30)
                m_new = jnp.maximum(m, jnp.max(sc, axis=-1, keepdims=True))
                alpha = jnp.exp(m - m_new)
                p = jnp.exp(sc - m_new)
                l = alpha * l + jnp.sum(p, axis=-1, keepdims=True)
                acc = alpha * acc + _dot(p, v2, _NN)
                out.append((m_new, l, acc))
            return tuple(out)

        init = tuple((jnp.full((ATT_BLOCK, 1), -1e30, F32), jnp.zeros((ATT_BLOCK, 1), F32),
                      jnp.zeros((ATT_BLOCK, LANES), F32)) for _ in range(2))
        carry = lax.fori_loop(0, qi, lambda ki, c: step(ki, c, False), init)
        (ma, la, acca), (mb, lb, accb) = step(qi, carry, True)
        o_ref[...] = jnp.where(m0, acca / la, accb / lb).astype(o_ref.dtype)
        l_ref[0] = jnp.broadcast_to(ma + jnp.log(la), (ATT_BLOCK, LANES))
        l_ref[1] = jnp.broadcast_to(mb + jnp.log(lb), (ATT_BLOCK, LANES))

    stat = pl.BlockSpec((None, 2, ATT_BLOCK, LANES), lambda j, i: (j, 0, i, 0))
    row = pl.BlockSpec((None, 2, s), lambda j, i: (j, 0, 0))
    return pl.pallas_call(
        body, name=name, grid=(npair, nq),
        in_specs=[pl.BlockSpec((ATT_BLOCK, LANES), lambda j, i: (i, j)),
                  pl.BlockSpec((s, LANES), lambda j, i: (0, npair + j)),
                  pl.BlockSpec((s, LANES), lambda j, i: (0, 2 * npair + j)),
                  stat, row],
        out_specs=[pl.BlockSpec((ATT_BLOCK, LANES), lambda j, i: (i, j)), stat],
        out_shape=[jax.ShapeDtypeStruct((s, FOX_WIDTH), BF16),
                   jax.ShapeDtypeStruct((npair, 2, s, LANES), F32)],
        compiler_params=_params(),
    )(qkv, qkv, qkv, cum_b, cum_r)


def _attn_delta(qkv, do, lse_b, cum_b, cum_r, *, name):
    s = qkv.shape[0]
    nq = s // ATT_BLOCK
    scale = FOX_HEAD_DIM ** -0.5
    npair = HEAD_PAIRS

    def body(q_ref, k_ref, v_ref, do_ref, l_ref, cq_ref, ck_ref, d_ref):
        qi = pl.program_id(1)
        m0 = _head_mask()
        q2 = q_ref[...]
        do2 = do_ref[...]
        qs = (jnp.where(m0, q2, jnp.zeros_like(q2)), jnp.where(m0, jnp.zeros_like(q2), q2))
        dos = (jnp.where(m0, do2, jnp.zeros_like(do2)), jnp.where(m0, jnp.zeros_like(do2), do2))

        def step(ki, carry, masked):
            off = pl.multiple_of(ki * ATT_BLOCK, ATT_BLOCK)
            k2 = k_ref[pl.ds(off, ATT_BLOCK), :]
            v2 = v_ref[pl.ds(off, ATT_BLOCK), :]
            out = []
            for hh in range(2):
                sc = _dot(qs[hh], k2, _NT) * scale + (cq_ref[hh] - ck_ref[hh:hh + 1, pl.ds(off, ATT_BLOCK)])
                p = jnp.exp(sc - l_ref[hh])
                if masked:
                    p = jnp.where(_causal(qi, ki), p, 0.0)
                out.append(carry[hh] + jnp.sum(p * _dot(dos[hh], v2, _NT), axis=-1, keepdims=True))
            return tuple(out)

        init = (jnp.zeros((ATT_BLOCK, 1), F32), jnp.zeros((ATT_BLOCK, 1), F32))
        carry = lax.fori_loop(0, qi, lambda ki, c: step(ki, c, False), init)
        da, db = step(qi, carry, True)
        d_ref[0] = jnp.broadcast_to(da, (ATT_BLOCK, LANES))
        d_ref[1] = jnp.broadcast_to(db, (ATT_BLOCK, LANES))

    stat = pl.BlockSpec((None, 2, ATT_BLOCK, LANES), lambda j, i: (j, 0, i, 0))
    return pl.pallas_call(
        body, name=name, grid=(npair, nq),
        in_specs=[pl.BlockSpec((ATT_BLOCK, LANES), lambda j, i: (i, j)),
                  pl.BlockSpec((s, LANES), lambda j, i: (0, npair + j)),
                  pl.BlockSpec((s, LANES), lambda j, i: (0, 2 * npair + j)),
                  pl.BlockSpec((ATT_BLOCK, LANES), lambda j, i: (i, j)),
                  stat, stat, pl.BlockSpec((None, 2, s), lambda j, i: (j, 0, 0))],
        out_specs=stat,
        out_shape=jax.ShapeDtypeStruct((npair, 2, s, LANES), F32), compiler_params=_params(),
    )(qkv, qkv, qkv, do, lse_b, cum_b, cum_r)


def _attn_bwd(qkv, do, lse_b, delta_b, cum_b, cum_r, *, name):
    s = qkv.shape[0]
    nq = s // ATT_BLOCK
    scale = FOX_HEAD_DIM ** -0.5
    npair = HEAD_PAIRS

    def body(q_ref, k_ref, v_ref, do_ref, l_ref, dl_ref, cq_ref, ck_ref, dq_ref, dk_ref, dv_ref, dc_ref):
        ki = pl.program_id(1)
        m0 = _head_mask()
        k2 = k_ref[...]
        v2 = v_ref[...]
        koff = pl.multiple_of(ki * ATT_BLOCK, ATT_BLOCK)

        @pl.when(ki == 0)
        def _():
            dq_ref[...] = jnp.zeros_like(dq_ref)

        def step(qi, carry, masked):
            off = pl.multiple_of(qi * ATT_BLOCK, ATT_BLOCK)
            q2 = q_ref[pl.ds(off, ATT_BLOCK), :]
            do2 = do_ref[pl.ds(off, ATT_BLOCK), :]
            qzero = jnp.zeros_like(q2)
            dzero = jnp.zeros_like(do2)
            out = []
            dqs = []
            for hh in range(2):
                dk_acc, dv_acc, dc_acc = carry[hh]
                keep = m0 if hh == 0 else jnp.logical_not(m0)
                qh = jnp.where(keep, q2, qzero)
                doh = jnp.where(keep, do2, dzero)
                sc = _dot(qh, k2, _NT) * scale + (cq_ref[hh, pl.ds(off, ATT_BLOCK), :]
                                                 - ck_ref[hh:hh + 1, pl.ds(koff, ATT_BLOCK)])
                p = jnp.exp(sc - l_ref[hh, pl.ds(off, ATT_BLOCK), :])
                if masked:
                    p = jnp.where(_causal(qi, ki), p, 0.0)
                dp = _dot(doh, v2, _NT)
                ds = p * (dp - dl_ref[hh, pl.ds(off, ATT_BLOCK), :])
                dv_acc = dv_acc + _dot(p, do2, _TN)
                dk_acc = dk_acc + _dot(ds, q2, _TN)
                dc_acc = dc_acc - jnp.sum(ds, axis=0, keepdims=True)
                dqs.append(_dot(ds, k2, _NN))
                out.append((dk_acc, dv_acc, dc_acc))
            dq_ref[pl.ds(off, ATT_BLOCK), :] += jnp.where(m0, dqs[0], dqs[1]) * scale
            return tuple(out)

        init = tuple((jnp.zeros((ATT_BLOCK, LANES), F32), jnp.zeros((ATT_BLOCK, LANES), F32),
                      jnp.zeros((1, ATT_BLOCK), F32)) for _ in range(2))
        carry = step(ki, init, True)
        (dka, dva, dca), (dkb, dvb, dcb) = lax.fori_loop(ki + 1, nq, lambda qi, c: step(qi, c, False), carry)
        dk_ref[...] = (jnp.where(m0, dka, dkb) * scale).astype(dk_ref.dtype)
        dv_ref[...] = jnp.where(m0, dva, dvb).astype(dv_ref.dtype)
        dc_ref[0:1, :] = dca
        dc_ref[1:2, :] = dcb

    stat = pl.BlockSpec((None, 2, s, LANES), lambda j, i: (j, 0, 0, 0))
    colfull = lambda base: pl.BlockSpec((s, LANES), lambda j, i: (0, base + j))
    colblk = lambda base: pl.BlockSpec((ATT_BLOCK, LANES), lambda j, i: (i, base + j))
    return pl.pallas_call(
        body, name=name, grid=(npair, nq),
        in_specs=[colfull(0), colblk(npair), colblk(2 * npair), colfull(0), stat, stat, stat,
                  pl.BlockSpec((None, 2, s), lambda j, i: (j, 0, 0))],
        out_specs=[colfull(0), colblk(0), colblk(0), pl.BlockSpec((None, 2, ATT_BLOCK), lambda j, i: (j, 0, i))],
        out_shape=[jax.ShapeDtypeStruct((s, FOX_WIDTH), F32), jax.ShapeDtypeStruct((s, FOX_WIDTH), BF16),
                   jax.ShapeDtypeStruct((s, FOX_WIDTH), BF16), jax.ShapeDtypeStruct((npair, 2, s), F32)],
        compiler_params=_params(),
    )(qkv, qkv, qkv, do, lse_b, delta_b, cum_b, cum_r)


ATT_TQ = 256
ATT_TK = 256
ATT_SCALE = FOX_HEAD_DIM ** -0.5
assert ATT_SCALE == 0.125 and ATT_TQ == ATT_TK


def _causal_t(qi, ki):
    kpos = lax.broadcasted_iota(jnp.int32, (ATT_TK, ATT_TQ), 0) + ki * ATT_TK
    qpos = lax.broadcasted_iota(jnp.int32, (ATT_TK, ATT_TQ), 1) + qi * ATT_TQ
    return kpos <= qpos


def _row_mask():
    return lax.broadcasted_iota(jnp.int32, (LANES, 1), 0) < FOX_HEAD_DIM


def _lane_tile(a, width):
    return a if a.shape[1] == width else jnp.tile(a, (1, width // a.shape[1]))


def _transpose_bf16(a):
    return a.astype(F32).T.astype(BF16)


def _attn_fwd_t(qkv, cum_b, cum_r, *, name):
    s = qkv.shape[0]
    nq = s // ATT_TQ
    npair = HEAD_PAIRS

    def body(q_ref, k_ref, v_ref, cq_ref, ck_ref, o_ref, ot_ref, l_ref, vt_ref):
        qi = pl.program_id(1)
        rows = _row_mask()

        @pl.when(qi == 0)
        def _():
            vt_ref[...] = _transpose_bf16(v_ref[...])

        qt = _transpose_bf16(q_ref[...]) * ATT_SCALE
        zero = jnp.zeros_like(qt)
        qts = (jnp.where(rows, qt, zero), jnp.where(rows, zero, qt))

        def step(ki, carry, masked):
            off = pl.multiple_of(ki * ATT_TK, ATT_TK)
            k2 = k_ref[pl.ds(off, ATT_TK), :]
            vt = vt_ref[:, pl.ds(off, ATT_TK)]
            out = []
            for hh in range(2):
                m, l, acc = carry[hh]
                bias = cq_ref[hh:hh + 1, :] - _lane_tile(ck_ref[hh, pl.ds(off, ATT_TK), :], ATT_TQ)
                sc = _dot(k2, qts[hh], _NN) + bias
                if masked:
                    sc = jnp.where(_causal_t(qi, ki), sc, -1e30)
                m_new = jnp.maximum(m, jnp.max(sc, axis=0, keepdims=True))
                alpha = jnp.exp(m - m_new)
                p = jnp.exp(sc - m_new)
                l = alpha * l + jnp.sum(p, axis=0, keepdims=True)
                p_hi = p.astype(BF16)
                p_lo = (p - p_hi.astype(F32)).astype(BF16)
                acc = alpha * acc + (_dot(vt, p_hi, _NN) + _dot(vt, p_lo, _NN))
                out.append((m_new, l, acc))
            return tuple(out)

        init = tuple((jnp.full((1, ATT_TQ), -1e30, F32), jnp.zeros((1, ATT_TQ), F32),
                      jnp.zeros((LANES, ATT_TQ), F32)) for _ in range(2))
        carry = lax.fori_loop(0, qi, lambda ki, c: step(ki, c, False), init)
        (ma, la, acca), (mb, lb, accb) = step(qi, carry, True)
        ot = jnp.where(rows, acca / la, accb / lb)
        ot_ref[...] = ot
        o_ref[...] = ot.T.astype(o_ref.dtype)
        l_ref[0:1, :] = ma + jnp.log(la)
        l_ref[1:2, :] = mb + jnp.log(lb)

    row = pl.BlockSpec((None, 2, ATT_TQ), lambda j, i: (j, 0, i))
    return pl.pallas_call(
        body, name=name, grid=(npair, nq),
        in_specs=[pl.BlockSpec((ATT_TQ, LANES), lambda j, i: (i, j)),
                  pl.BlockSpec((s, LANES), lambda j, i: (0, npair + j)),
                  pl.BlockSpec((s, LANES), lambda j, i: (0, 2 * npair + j)),
                  row, pl.BlockSpec((None, 2, s, LANES), lambda j, i: (j, 0, 0, 0))],
        out_specs=[pl.BlockSpec((ATT_TQ, LANES), lambda j, i: (i, j)),
                   pl.BlockSpec((LANES, ATT_TQ), lambda j, i: (j, i)), row],
        out_shape=[jax.ShapeDtypeStruct((s, FOX_WIDTH), BF16), jax.ShapeDtypeStruct((FOX_WIDTH, s), F32),
                   jax.ShapeDtypeStruct((npair, 2, s), F32)],
        scratch_shapes=[pltpu.VMEM((LANES, s), BF16)],
        compiler_params=_params(),
    )(qkv, qkv, qkv, cum_r, cum_b)


def _attn_delta_t(do_t, o_t, *, name):
    s = o_t.shape[1]
    ts = _tile(s, 512)

    def body(do_ref, o_ref, d_ref):
        prod = do_ref[...].astype(F32) * o_ref[...]
        d_ref[0:1, :] = jnp.sum(prod[:FOX_HEAD_DIM], axis=0, keepdims=True)
        d_ref[1:2, :] = jnp.sum(prod[FOX_HEAD_DIM:], axis=0, keepdims=True)

    blk = pl.BlockSpec((LANES, ts), lambda j, i: (j, i))
    return pl.pallas_call(
        body, name=name, grid=(HEAD_PAIRS, s // ts), in_specs=[blk, blk],
        out_specs=pl.BlockSpec((None, 2, ts), lambda j, i: (j, 0, i)),
        out_shape=jax.ShapeDtypeStruct((HEAD_PAIRS, 2, s), F32), compiler_params=_params(),
    )(do_t, o_t)


def _attn_bwd_t(qkv, do, o_t, lse, cum_b, cum_r, *, name):
    s = qkv.shape[0]
    nq = s // ATT_TQ
    npair = HEAD_PAIRS

    def body(q_ref, k_ref, v_ref, do_ref, ot_ref, l_ref, cq_ref, ck_ref, dq_ref, dk_ref, dv_ref, dc_ref,
             qt_ref, dot_ref, dqt_ref, dl_ref):
        ki = pl.program_id(1)
        m0 = _head_mask()
        rows = _row_mask()
        k2 = k_ref[...]
        v2 = v_ref[...]
        kt = _transpose_bf16(k2)
        ks = k2 * ATT_SCALE
        kz, vz = jnp.zeros_like(k2), jnp.zeros_like(v2)
        khs = (jnp.where(m0, ks, kz), jnp.where(m0, kz, ks))
        vhs = (jnp.where(m0, v2, vz), jnp.where(m0, vz, v2))
        cks = tuple(_lane_tile(ck_ref[hh], ATT_TQ) for hh in range(2))

        @pl.when(ki == 0)
        def _():
            dqt_ref[...] = jnp.zeros_like(dqt_ref)
            qt_ref[...] = _transpose_bf16(q_ref[...])
            do_t = do_ref[...].astype(F32).T
            dot_ref[...] = do_t.astype(BF16)
            prod = do_t * ot_ref[...]
            dl_ref[0:1, :] = jnp.sum(prod[:FOX_HEAD_DIM], axis=0, keepdims=True)
            dl_ref[1:2, :] = jnp.sum(prod[FOX_HEAD_DIM:], axis=0, keepdims=True)

        def step(qi, carry, masked):
            off = pl.multiple_of(qi * ATT_TQ, ATT_TQ)
            q2 = q_ref[pl.ds(off, ATT_TQ), :]
            do2 = do_ref[pl.ds(off, ATT_TQ), :]
            qt = qt_ref[:, pl.ds(off, ATT_TQ)]
            dot_ = dot_ref[:, pl.ds(off, ATT_TQ)]
            out, dqs = [], []
            for hh in range(2):
                dk_acc, dv_acc, dc_acc = carry[hh]
                sc = _dot(khs[hh], qt, _NN) + (cq_ref[hh:hh + 1, pl.ds(off, ATT_TQ)] - cks[hh])
                p = jnp.exp(sc - l_ref[hh:hh + 1, pl.ds(off, ATT_TQ)])
                if masked:
                    p = jnp.where(_causal_t(qi, ki), p, 0.0)
                dp = _dot(vhs[hh], dot_, _NN)
                ds = p * (dp - dl_ref[hh:hh + 1, pl.ds(off, ATT_TQ)])
                dc_acc = dc_acc - jnp.sum(ds, axis=1, keepdims=True)
                dss = (ds * ATT_SCALE).astype(BF16)
                dv_acc = dv_acc + _dot(p, do2, _NN)
                dk_acc = dk_acc + _dot(dss, q2, _NN)
                dqs.append(_dot(kt, dss, _NN))
                out.append((dk_acc, dv_acc, dc_acc))
            dqt_ref[:, pl.ds(off, ATT_TQ)] += jnp.where(rows, dqs[0], dqs[1])
            return tuple(out)

        init = tuple((jnp.zeros((ATT_TK, LANES), F32), jnp.zeros((ATT_TK, LANES), F32),
                      jnp.zeros((ATT_TK, 1), F32)) for _ in range(2))
        carry = step(ki, init, True)
        (dka, dva, dca), (dkb, dvb, dcb) = lax.fori_loop(ki + 1, nq, lambda qi, c: step(qi, c, False), carry)
        dk_ref[...] = jnp.where(m0, dka, dkb).astype(dk_ref.dtype)
        dv_ref[...] = jnp.where(m0, dva, dvb).astype(dv_ref.dtype)
        dc_ref[0] = jnp.broadcast_to(dca, (ATT_TK, LANES))
        dc_ref[1] = jnp.broadcast_to(dcb, (ATT_TK, LANES))

        @pl.when(ki == nq - 1)
        def _():
            dq_ref[...] = dqt_ref[...].T.astype(dq_ref.dtype)

    colfull = lambda base: pl.BlockSpec((s, LANES), lambda j, i: (0, base + j))
    colblk = lambda base: pl.BlockSpec((ATT_TK, LANES), lambda j, i: (i, base + j))
    stat = pl.BlockSpec((None, 2, s), lambda j, i: (j, 0, 0))
    bcast = pl.BlockSpec((None, 2, ATT_TK, LANES), lambda j, i: (j, 0, i, 0))
    grad = jax.ShapeDtypeStruct((s, FOX_WIDTH), BF16)
    return pl.pallas_call(
        body, name=name, grid=(npair, nq),
        in_specs=[colfull(0), colblk(npair), colblk(2 * npair), colfull(0),
                  pl.BlockSpec((LANES, s), lambda j, i: (j, 0)), stat, stat, bcast],
        out_specs=[colfull(0), colblk(0), colblk(0), bcast],
        out_shape=[grad, grad, grad, jax.ShapeDtypeStruct((npair, 2, s, LANES), F32)],
        scratch_shapes=[pltpu.VMEM((LANES, s), BF16), pltpu.VMEM((LANES, s), BF16), pltpu.VMEM((LANES, s), F32),
                        pltpu.VMEM((2, s), F32)],
        compiler_params=_params(),
    )(qkv, qkv, qkv, do, o_t, lse, cum_r, cum_b)


def _merge_fwd(zg, ya, yb, *, name, tm=256):
    s, d = ya.shape
    tm = _tile(s, tm)

    def body(zg_ref, ya_ref, yb_ref, m_ref):
        ga = _sigmoid(zg_ref[:, :d].astype(F32))
        gb = _sigmoid(zg_ref[:, d:].astype(F32))
        m_ref[...] = (ga * ya_ref[...].astype(F32) + gb * yb_ref[...].astype(F32)).astype(m_ref.dtype)

    row = pl.BlockSpec((tm, d), lambda i: (i, 0))
    row2 = pl.BlockSpec((tm, 2 * d), lambda i: (i, 0))
    return pl.pallas_call(
        body, name=name, grid=(s // tm,), in_specs=[row2, row, row], out_specs=row,
        out_shape=jax.ShapeDtypeStruct((s, d), BF16), compiler_params=_params(),
    )(zg, ya, yb)


def _merge_bwd(dm, zg, ya, yb, *, name, tm=256):
    s, d = ya.shape
    tm = _tile(s, tm)

    def body(dm_ref, zg_ref, ya_ref, yb_ref, dzg_ref, dya_ref, dyb_ref):
        dmv = dm_ref[...].astype(F32)
        ga = _sigmoid(zg_ref[:, :d].astype(F32))
        gb = _sigmoid(zg_ref[:, d:].astype(F32))
        dzg_ref[:, :d] = (dmv * ya_ref[...].astype(F32) * ga * (1.0 - ga)).astype(dzg_ref.dtype)
        dzg_ref[:, d:] = (dmv * yb_ref[...].astype(F32) * gb * (1.0 - gb)).astype(dzg_ref.dtype)
        dya_ref[...] = (dmv * ga).astype(dya_ref.dtype)
        dyb_ref[...] = (dmv * gb).astype(dyb_ref.dtype)

    row = pl.BlockSpec((tm, d), lambda i: (i, 0))
    row2 = pl.BlockSpec((tm, 2 * d), lambda i: (i, 0))
    return pl.pallas_call(
        body, name=name, grid=(s // tm,), in_specs=[row, row2, row, row], out_specs=[row2, row, row],
        out_shape=[jax.ShapeDtypeStruct((s, 2 * d), BF16), jax.ShapeDtypeStruct((s, d), BF16),
                   jax.ShapeDtypeStruct((s, d), BF16)],
        compiler_params=_params(),
    )(dm, zg, ya, yb)


def _shift_down(u, k, row):
    return jnp.where(row >= k, pltpu.roll(u, k, 0), 0.0)


def _shift_up(u, k, row):
    n = u.shape[0]
    return jnp.where(row < n - k, pltpu.roll(u, n - k, 0), 0.0)


def _conv_act_fwd(up_a, up_b, cw_a, cw_b, cb_a, cb_b, *, name, tc=128):
    s, f = up_a.shape
    tc = _tile(f, tc)

    def body(ua_ref, ub_ref, wa_ref, wb_ref, ba_ref, bb_ref, act_ref):
        row = lax.broadcasted_iota(jnp.int32, (s, tc), 0)

        def conv(u_ref, w_ref, b_ref):
            u = u_ref[...].astype(F32)
            return (b_ref[...] + w_ref[0:1, :] * _shift_down(u, 2, row)
                    + w_ref[1:2, :] * _shift_down(u, 1, row) + w_ref[2:3, :] * u)

        ca = conv(ua_ref, wa_ref, ba_ref)
        cb = conv(ub_ref, wb_ref, bb_ref)
        act_ref[...] = (_gelu(ca) * cb).astype(act_ref.dtype)

    col = pl.BlockSpec((s, tc), lambda j: (0, j))
    w3 = pl.BlockSpec((3, tc), lambda j: (0, j))
    b1 = pl.BlockSpec((1, tc), lambda j: (0, j))
    return pl.pallas_call(
        body, name=name, grid=(f // tc,), in_specs=[col, col, w3, w3, b1, b1], out_specs=col,
        out_shape=jax.ShapeDtypeStruct((s, f), BF16), compiler_params=_params(),
    )(up_a, up_b, cw_a, cw_b, cb_a, cb_b)


def _conv_act_bwd(up_a, up_b, dact, cw_a, cw_b, cb_a, cb_b, *, name, tc=128):
    s, f = up_a.shape
    tc = _tile(f, tc)

    def body(ua_ref, ub_ref, da_ref, wa_ref, wb_ref, ba_ref, bb_ref, dua_ref, dub_ref, dwa_ref, dwb_ref):
        row = lax.broadcasted_iota(jnp.int32, (s, tc), 0)

        def conv(u_ref, w_ref, b_ref):
            u = u_ref[...].astype(F32)
            u1 = _shift_down(u, 1, row)
            u2 = _shift_down(u, 2, row)
            return u, u1, u2, b_ref[...] + w_ref[0:1, :] * u2 + w_ref[1:2, :] * u1 + w_ref[2:3, :] * u

        def back(dc, taps, w_ref, du_ref, dw_ref):
            u, u1, u2 = taps
            dw_ref[0:1, :] = jnp.sum(dc * u2, axis=0, keepdims=True)
            dw_ref[1:2, :] = jnp.sum(dc * u1, axis=0, keepdims=True)
            dw_ref[2:3, :] = jnp.sum(dc * u, axis=0, keepdims=True)
            dw_ref[3:4, :] = jnp.sum(dc, axis=0, keepdims=True)
            du = (w_ref[2:3, :] * dc + w_ref[1:2, :] * _shift_up(dc, 1, row)
                  + w_ref[0:1, :] * _shift_up(dc, 2, row))
            du_ref[...] = du.astype(du_ref.dtype)

        ua, ua1, ua2, ca = conv(ua_ref, wa_ref, ba_ref)
        ub, ub1, ub2, cb = conv(ub_ref, wb_ref, bb_ref)
        g, dg = _gelu_and_grad(ca)
        dact_v = da_ref[...].astype(F32)
        back(dact_v * cb * dg, (ua, ua1, ua2), wa_ref, dua_ref, dwa_ref)
        back(dact_v * g, (ub, ub1, ub2), wb_ref, dub_ref, dwb_ref)

    col = pl.BlockSpec((s, tc), lambda j: (0, j))
    w3 = pl.BlockSpec((3, tc), lambda j: (0, j))
    w4 = pl.BlockSpec((4, tc), lambda j: (0, j))
    b1 = pl.BlockSpec((1, tc), lambda j: (0, j))
    return pl.pallas_call(
        body, name=name, grid=(f // tc,), in_specs=[col, col, col, w3, w3, b1, b1],
        out_specs=[col, col, w4, w4],
        out_shape=[jax.ShapeDtypeStruct((s, f), BF16), jax.ShapeDtypeStruct((s, f), BF16),
                   jax.ShapeDtypeStruct((4, f), F32), jax.ShapeDtypeStruct((4, f), F32)],
        compiler_params=_params(),
    )(up_a, up_b, dact, cw_a, cw_b, cb_a, cb_b)


def _ple_final(x2, ple, zp, target, g_final, *, name, tm=256):
    s, d = x2.shape
    tm = _tile(s, tm)

    def body(x_ref, ple_ref, zp_ref, t_ref, g_ref, dx_ref, dple_ref, dzp_ref, dg_ref, loss_ref):
        @pl.when(pl.program_id(0) == 0)
        def _():
            dg_ref[...] = jnp.zeros_like(dg_ref)
            loss_ref[...] = jnp.zeros_like(loss_ref)

        gp = _sigmoid(zp_ref[...].astype(F32))
        plev = ple_ref[...].astype(F32)
        x3 = x_ref[...] + plev * gp
        r = lax.rsqrt(jnp.mean(x3 * x3, axis=-1, keepdims=True) + EPS)
        xhat = x3 * r
        gv = g_ref[...]
        diff = xhat * gv - t_ref[...]
        loss_ref[...] += 0.5 * jnp.sum(jnp.mean(diff * diff, axis=-1, keepdims=True), axis=0, keepdims=True)
        dy = diff * (1.0 / d)
        dg_ref[...] += jnp.sum(dy * xhat, axis=0, keepdims=True)
        dyg = dy * gv
        dx3 = r * (dyg - xhat * jnp.mean(dyg * xhat, axis=-1, keepdims=True))
        dx_ref[...] = dx3
        dple_ref[...] = (dx3 * gp).astype(dple_ref.dtype)
        dzp_ref[...] = (dx3 * plev * gp * (1.0 - gp)).astype(dzp_ref.dtype)

    row = pl.BlockSpec((tm, d), lambda i: (i, 0))
    vec = pl.BlockSpec((1, d), lambda i: (0, 0))
    return pl.pallas_call(
        body, name=name, grid=(s // tm,), in_specs=[row, row, row, row, vec],
        out_specs=[row, row, row, vec, pl.BlockSpec((1, LANES), lambda i: (0, 0))],
        out_shape=[jax.ShapeDtypeStruct((s, d), F32), jax.ShapeDtypeStruct((s, d), BF16),
                   jax.ShapeDtypeStruct((s, d), BF16), jax.ShapeDtypeStruct((1, d), F32),
                   jax.ShapeDtypeStruct((1, LANES), F32)],
        compiler_params=_params(),
    )(x2, ple, zp, target, g_final)


def _device_step(x, p, target, w, get_w_in=None, get_w_rest=None, on_grads_ffn=None, on_grads_mix=None):
    s = x.shape[0]
    g = {}
    w = dict(w)

    h = _rms_fwd(x, w["norm_mix_g"], name="rms_mix", dep=w.get("first_dep"))
    if get_w_in is not None:
        w.update(get_w_in(h))
    z_uv = _mm(h, w["w_uv"], mode="nn", out_dtype=BF16, name="proj_uv", tm=1024, dep=w.get("proj_dep"))
    qkv = _mm(h, w["w_qkv"], mode="nn", out_dtype=BF16, name="proj_qkv", tm=1024)
    zg = _mm(h, w["w_g"], mode="nn", out_dtype=BF16, name="proj_gate", tm=1024)
    f = _mm(h, w["w_f"], mode="nn", out_dtype=F32, name="proj_f", tm=1024)

    a = _gmlp_fwd(z_uv, w["gmlp_ln_g"], w["gmlp_ln_b"], w["gmlp_w_s"], w["gmlp_b_s_t"], name="gmlp_fwd")

    cum_b, cum_t = _fox_cum(f, w["b_f"], name="fox_cum")
    cum_b = cum_b.reshape(HEAD_PAIRS, 2, s, LANES)
    cum_r = cum_t[:FOX_HEADS].reshape(HEAD_PAIRS, 2, s)
    b, o_t, lse = _attn_fwd_t(qkv, cum_b, cum_r, name="attn_fwd")
    if get_w_rest is not None:
        w.update(get_w_rest(b))

    ya = _mm(a, w["w_branch_a"], mode="nn", out_dtype=BF16, name="branch_a", tm=1024)
    yb = _mm(b, w["w_branch_b"], mode="nn", out_dtype=BF16, name="branch_b", tm=1024)
    merged = _merge_fwd(zg, ya, yb, name="merge_fwd")
    x1 = _mm(merged, w["w_out"], mode="nn", out_dtype=F32, name="proj_out", add=x, tm=1024)

    h2 = _rms_fwd(x1, w["norm_ffn_g"], name="rms_ffn")
    up_a = _mm(h2, w["w_up_a"], mode="nn", out_dtype=BF16, name="up_a", tm=1024, tn=D_FF // 2)
    up_b = _mm(h2, w["w_up_b"], mode="nn", out_dtype=BF16, name="up_b", tm=1024, tn=D_FF // 2)
    cw, cb = w["conv_w"], w["conv_b"]
    conv_args = (cw[:, :D_FF], cw[:, D_FF:], cb[:, :D_FF], cb[:, D_FF:])
    act = _conv_act_fwd(up_a, up_b, *conv_args, name="conv_act_fwd")
    x2 = _mm(act, w["w_down"], mode="nn", out_dtype=F32, name="down", add=x1, tm=512)

    h3 = _rms_fwd(x2, w["norm_ple_g"], name="rms_ple")
    ple = _mm(p, w["w_ple"], mode="nn", out_dtype=BF16, name="ple_proj", tm=1024)
    zp = _mm(h3, w["w_ple_gate"], mode="nn", out_dtype=BF16, name="ple_gate", tm=1024)
    dx3, dple, dzp, g["norm_final_g"], loss = _ple_final(x2, ple, zp, target, w["norm_final_g"], name="ple_final")

    g["w_ple"] = _mm(p, dple, mode="tn", out_dtype=BF16, name="dw_ple")
    g["w_ple_gate"] = _mm(h3, dzp, mode="tn", out_dtype=BF16, name="dw_ple_gate")
    dh3 = _mm(dzp, w["w_ple_gate"], mode="nt", out_dtype=BF16, name="dh3")
    dx2, dx2_b, g["norm_ple_g"] = _rms_bwd(x2, w["norm_ple_g"], dh3, dx3, name="rms_ple_bwd")

    g["w_down"] = _mm(act, dx2_b, mode="tn", out_dtype=BF16, name="dw_down", tm=D_FF // 2)
    dact = _mm(dx2_b, w["w_down"], mode="nt", out_dtype=BF16, name="dact", tn=D_FF // 2)
    dup_a, dup_b, dcw_a, dcw_b = _conv_act_bwd(up_a, up_b, dact, *conv_args, name="conv_act_bwd")
    g["conv_w"] = jnp.concatenate([dcw_a[:3], dcw_b[:3]], axis=1)
    g["conv_b"] = jnp.concatenate([dcw_a[3:], dcw_b[3:]], axis=1)
    g["w_up_a"] = _mm(h2, dup_a, mode="tn", out_dtype=BF16, name="dw_up_a", tn=D_FF // 2)
    g["w_up_b"] = _mm(h2, dup_b, mode="tn", out_dtype=BF16, name="dw_up_b", tn=D_FF // 2)
    dh2 = _mm_nt_sum([(dup_a, w["w_up_a"]), (dup_b, w["w_up_b"])], out_dtype=BF16, name="dh2")
    dx1, dx1_b, g["norm_ffn_g"] = _rms_bwd(x1, w["norm_ffn_g"], dh2, dx2, name="rms_ffn_bwd")
    dep = on_grads_ffn(g) if on_grads_ffn is not None else None

    g["w_out"] = _mm(merged, dx1_b, mode="tn", out_dtype=BF16, name="dw_out")
    dmerged = _mm(dx1_b, w["w_out"], mode="nt", out_dtype=BF16, name="dmerged", dep=dep)
    dzg, dya, dyb = _merge_bwd(dmerged, zg, ya, yb, name="merge_bwd")
    g["w_branch_a"] = _mm(a, dya, mode="tn", out_dtype=BF16, name="dw_branch_a")
    g["w_branch_b"] = _mm(b, dyb, mode="tn", out_dtype=BF16, name="dw_branch_b")
    da = _mm(dya, w["w_branch_a"], mode="nt", out_dtype=BF16, name="da")
    db = _mm(dyb, w["w_branch_b"], mode="nt", out_dtype=BF16, name="db")

    dz_uv, g["gmlp_w_s"], dbs_t, g["gmlp_ln_g"], g["gmlp_ln_b"] = _gmlp_bwd(
        z_uv, da, w["gmlp_ln_g"], w["gmlp_ln_b"], w["gmlp_w_s"], w["gmlp_b_s_t"], name="gmlp_bwd")
    g["gmlp_b_s"] = dbs_t[:, :GMLP_GROUPS].T

    dq, dk, dv, dcum_b = _attn_bwd_t(qkv, db, o_t, lse, cum_b, cum_r, name="attn_bwd")
    dcum_t = jnp.pad(dcum_b[..., 0].reshape(FOX_HEADS, s), ((0, LANES - FOX_HEADS), (0, 0)))
    df, g["b_f"] = _fox_dlogit(dcum_t, f, w["b_f"], name="fox_dlogit")
    dqkv = jnp.concatenate([dq, dk, dv], axis=1)

    g["w_uv"] = _mm(h, dz_uv, mode="tn", out_dtype=BF16, name="dw_uv")
    g["w_qkv"] = _mm(h, dqkv, mode="tn", out_dtype=BF16, name="dw_qkv")
    g["w_f"] = _mm(h, df, mode="tn", out_dtype=BF16, name="dw_f")
    g["w_g"] = _mm(h, dzg, mode="tn", out_dtype=BF16, name="dw_g")
    dep = on_grads_mix(g) if on_grads_mix is not None else None
    dh = _mm_nt_sum([(dz_uv, w["w_uv"]), (dqkv, w["w_qkv"]), (df, w["w_f"]), (dzg, w["w_g"])],
                    out_dtype=BF16, name="dh", dep=dep)
    dx0, _, g["norm_mix_g"] = _rms_bwd(x, w["norm_mix_g"], dh, dx1, name="rms_mix_bwd")
    return loss, dx0, g


def _coords():
    return lax.axis_index("x"), lax.axis_index("y"), lax.axis_index("c")


def _other_chips(x, y):
    return [(1 - x, y), (x, 1 - y), (1 - x, 1 - y)]


def _remote(src, dst, send_sem, recv_sem, dev):
    return pltpu.make_async_remote_copy(src_ref=src, dst_ref=dst, send_sem=send_sem, recv_sem=recv_sem,
                                        device_id=dev, device_id_type=MESH)


_ANY = pl.BlockSpec(memory_space=pl.ANY)


def _gather_weights(halved, whole, *, name):
    nh, n = len(halved), len(halved) + len(whole)
    arrays = list(halved) + list(whole)

    def body(*refs):
        ins, outs = refs[:n], refs[n:2 * n]
        send_sems, recv_sems = refs[2 * n:]
        x, y, c = _coords()
        me, sib = 2 * x + y, (x, y, 1 - c)
        chips = _other_chips(x, y)

        def half(i, which):
            h = ins[i].shape[0] // 2
            return pl.ds(pl.multiple_of(which * h, 16), h)

        sends = []
        for i in range(n):
            src, dst = (ins[i].at[half(i, c)], outs[i].at[me, half(i, c)]) if i < nh else (ins[i], outs[i].at[me])
            for k, (cx, cy) in enumerate(chips):
                cp = _remote(src, dst, send_sems.at[i, k], recv_sems.at[i, k], (cx, cy, c))
                cp.start()
                sends.append(cp)
        for i in range(n):
            for k, (cx, cy) in enumerate(chips):
                got = outs[i].at[2 * cx + cy, half(i, c)] if i < nh else outs[i].at[2 * cx + cy]
                _remote(got, got, send_sems.at[i, k], recv_sems.at[i, k], sib).wait_recv()
                if i < nh:
                    cp = _remote(got, got, send_sems.at[i, 3 + k], recv_sems.at[i, 3 + k], sib)
                    cp.start()
                    sends.append(cp)
        for i in range(nh):
            for k, (cx, cy) in enumerate(chips):
                got = outs[i].at[2 * cx + cy, half(i, 1 - c)]
                _remote(got, got, send_sems.at[i, 3 + k], recv_sems.at[i, 3 + k], sib).wait_recv()
        for cp in sends:
            cp.wait_send()

    outs = pl.pallas_call(
        body, name=name, in_specs=[_ANY] * n, out_specs=[_ANY] * n,
        out_shape=[jax.ShapeDtypeStruct((N_CHIPS,) + a.shape, a.dtype) for a in arrays],
        scratch_shapes=[pltpu.SemaphoreType.DMA((n, 6)), pltpu.SemaphoreType.DMA((n, 6))],
        compiler_params=_params(),
    )(*arrays)
    chip = 2 * lax.axis_index("x") + lax.axis_index("y")
    return [lax.dynamic_update_index_in_dim(o, a, chip, 0) for o, a in zip(outs, arrays)]


def _pair_exchange(gs, *, name):
    n = len(gs)

    def body(*refs):
        ins, outs = refs[:n], refs[n:2 * n]
        send_sems, recv_sems = refs[2 * n:]
        x, y, c = _coords()
        copies = []
        for i in range(n):
            for j in range(N_CHIPS):
                cp = _remote(ins[i].at[j, 1 - c], outs[i].at[j], send_sems.at[i, j], recv_sems.at[i, j], (x, y, 1 - c))
                cp.start()
                copies.append(cp)
        for cp in copies:
            cp.wait()

    return pl.pallas_call(
        body, name=name, in_specs=[_ANY] * n, out_specs=[_ANY] * n,
        out_shape=[jax.ShapeDtypeStruct((N_CHIPS,) + a.shape[2:], a.dtype) for a in gs],
        scratch_shapes=[pltpu.SemaphoreType.DMA((n, N_CHIPS)), pltpu.SemaphoreType.DMA((n, N_CHIPS))],
        compiler_params=_params(),
    )(*gs)


def _chip_exchange(ss, *, name):
    n = len(ss)

    def body(*refs):
        ins, outs = refs[:n], refs[n:2 * n]
        send_sems, recv_sems = refs[2 * n:]
        x, y, c = _coords()
        me = 2 * x + y
        chips = _other_chips(x, y)
        sends = []
        for i in range(n):
            for k, (cx, cy) in enumerate(chips):
                cp = _remote(ins[i].at[2 * cx + cy], outs[i].at[me], send_sems.at[i, k], recv_sems.at[i, k], (cx, cy, c))
                cp.start()
                sends.append(cp)
        for i in range(n):
            for k, (cx, cy) in enumerate(chips):
                got = outs[i].at[2 * cx + cy]
                _remote(got, got, send_sems.at[i, k], recv_sems.at[i, k], (cx, cy, c)).wait_recv()
        for cp in sends:
            cp.wait_send()

    return pl.pallas_call(
        body, name=name, in_specs=[_ANY] * n, out_specs=[_ANY] * n,
        out_shape=[jax.ShapeDtypeStruct(a.shape, a.dtype) for a in ss],
        scratch_shapes=[pltpu.SemaphoreType.DMA((n, 3)), pltpu.SemaphoreType.DMA((n, 3))],
        compiler_params=_params(),
    )(*ss)


def _pair_share(hs, *, name):
    n = len(hs)

    def body(*refs):
        ins, outs = refs[:n], refs[n:2 * n]
        send_sems, recv_sems = refs[2 * n:]
        x, y, c = _coords()
        copies = []
        for i in range(n):
            cp = _remote(ins[i], outs[i], send_sems.at[i], recv_sems.at[i], (x, y, 1 - c))
            cp.start()
            copies.append(cp)
        for cp in copies:
            cp.wait()

    return pl.pallas_call(
        body, name=name, in_specs=[_ANY] * n, out_specs=[_ANY] * n,
        out_shape=[jax.ShapeDtypeStruct(a.shape, a.dtype) for a in hs],
        scratch_shapes=[pltpu.SemaphoreType.DMA((n,)), pltpu.SemaphoreType.DMA((n,))],
        compiler_params=_params(),
    )(*hs)


def _all_exchange(vec, *, name):
    def body(v_ref, o_ref, send_sems, recv_sems, local_sem):
        x, y, c = _coords()
        me = 4 * x + 2 * y + c
        local = pltpu.make_async_copy(v_ref, o_ref.at[me], local_sem)
        local.start()
        copies = []
        k = 0
        for dx in (0, 1):
            for dy in (0, 1):
                for dc in (0, 1):
                    if dx or dy or dc:
                        peer = (1 - x if dx else x, 1 - y if dy else y, 1 - c if dc else c)
                        cp = _remote(v_ref, o_ref.at[me], send_sems.at[k], recv_sems.at[k], peer)
                        cp.start()
                        copies.append(cp)
                        k += 1
        for cp in copies:
            cp.wait()
        local.wait()

    return pl.pallas_call(
        body, name=name, in_specs=[_ANY], out_specs=_ANY,
        out_shape=jax.ShapeDtypeStruct((8,) + vec.shape, vec.dtype),
        scratch_shapes=[pltpu.SemaphoreType.DMA((7,)), pltpu.SemaphoreType.DMA((7,)), pltpu.SemaphoreType.DMA(())],
        compiler_params=_params(),
    )(vec)


_HBM = pl.BlockSpec(memory_space=pltpu.HBM)
_SEM = pl.BlockSpec(memory_space=pltpu.SEMAPHORE)
_EFFECT = pltpu.SideEffectType.DATAFLOW_SIDE_EFFECTING


def _copies_start(srcs, lands, plan, n_copies, *, name, after=()):
    ns, n = len(srcs), len(srcs) + len(lands)
    na = len(after)

    def body(*refs):
        send_sems, recv_sems = refs[n + na], refs[n + na + 1]
        token = refs[-1]
        for k, (src, dst, dev) in enumerate(plan(refs[:ns], refs[ns:n])):
            _remote(src, dst, send_sems.at[k], recv_sems.at[k], dev).start()
        token[...] = jnp.zeros_like(token)

    arrays = list(srcs) + list(lands)
    outs = pl.pallas_call(
        body, name=name,
        out_shape=(pltpu.SemaphoreType.DMA((n_copies,)), pltpu.SemaphoreType.DMA((n_copies,)),
                   *[pltpu.HBM(a.shape, a.dtype) for a in arrays], jax.ShapeDtypeStruct((8, LANES), F32)),
        in_specs=[_HBM] * n + [_ANY] * na,
        out_specs=(_SEM, _SEM, *[_HBM] * n, pl.BlockSpec(memory_space=pltpu.VMEM)),
        input_output_aliases={i: 2 + i for i in range(n)},
        compiler_params=pltpu.CompilerParams(has_side_effects=_EFFECT),
    )(*[pltpu.with_memory_space_constraint(a, pltpu.HBM) for a in arrays], *after)
    return outs[0], outs[1], list(outs[2:2 + ns]), list(outs[2 + ns:2 + n]), outs[-1]


def _copies_wait(send_sems, recv_sems, srcs, lands, plan, first, after, *, name):
    ns, n = len(srcs), len(srcs) + len(lands)

    def body(*refs):
        send, recv = refs[n], refs[n + 1]
        for k, (src, dst, dev) in enumerate(plan(refs[:ns], refs[ns:n])):
            cp = _remote(src, dst, send.at[first + k], recv.at[first + k], dev)
            cp.wait_send()
            cp.wait_recv()

    arrays = list(srcs) + list(lands)
    outs = pl.pallas_call(
        body, name=name, out_shape=tuple(pltpu.HBM(a.shape, a.dtype) for a in arrays),
        in_specs=[_HBM] * n + [_SEM, _SEM] + [_ANY] * len(after), out_specs=tuple([_HBM] * n),
        input_output_aliases={i: i for i in range(n)},
        compiler_params=pltpu.CompilerParams(has_side_effects=_EFFECT),
    )(*arrays, send_sems, recv_sems, *after)
    return list(outs[:ns]), list(outs[ns:])


def _gather_plan(halved):
    def plan(srcs, lands):
        x, y, c = _coords()
        me = 2 * x + y
        out = []
        for i, (src, land) in enumerate(zip(srcs, lands)):
            if halved[i]:
                h = src.shape[0] // 2
                rows = pl.ds(pl.multiple_of(c * h, 16), h)
                src, dst = src.at[rows], land.at[me, rows]
            else:
                dst = land.at[me]
            out += [(src, dst, (cx, cy, c)) for cx, cy in _other_chips(x, y)]
        return out
    return plan


def _forward_halves(lands, *, name):
    n = len(lands)

    def body(*refs):
        ins, outs = refs[:n], refs[n:2 * n]
        send_sems, recv_sems = refs[2 * n:]
        x, y, c = _coords()
        copies = []
        for i in range(n):
            h = ins[i].shape[1] // 2
            rows = pl.ds(pl.multiple_of(c * h, 16), h)
            for k, (cx, cy) in enumerate(_other_chips(x, y)):
                cp = _remote(ins[i].at[2 * cx + cy, rows], outs[i].at[2 * cx + cy, rows],
                             send_sems.at[i, k], recv_sems.at[i, k], (x, y, 1 - c))
                cp.start()
                copies.append(cp)
        for cp in copies:
            cp.wait()

    return pl.pallas_call(
        body, name=name, in_specs=[_ANY] * n, out_specs=[_ANY] * n,
        out_shape=[jax.ShapeDtypeStruct(a.shape, a.dtype) for a in lands],
        input_output_aliases={i: i for i in range(n)},
        scratch_shapes=[pltpu.SemaphoreType.DMA((n, 3)), pltpu.SemaphoreType.DMA((n, 3))],
        compiler_params=_params(),
    )(*lands)


def _all_plan(srcs, lands):
    x, y, c = _coords()
    me = 4 * x + 2 * y + c
    out = []
    for src, land in zip(srcs, lands):
        for dx in (0, 1):
            for dy in (0, 1):
                for dc in (0, 1):
                    if dx or dy or dc:
                        out.append((src, land.at[me], (1 - x if dx else x, 1 - y if dy else y, 1 - c if dc else c)))
    return out


def _chip_plan(srcs, lands):
    x, y, c = _coords()
    me = 2 * x + y
    out = []
    for src, land in zip(srcs, lands):
        out += [(src.at[2 * cx + cy], land.at[me], (cx, cy, c)) for cx, cy in _other_chips(x, y)]
    return out


ROW_BLOCK_BYTES = 2 * 1024 * 1024


def _rtile(r, pref, mult, row_bytes=None):
    if row_bytes is not None:
        pref = max(pref, ROW_BLOCK_BYTES // row_bytes)
    t = (min(r, pref) // mult) * mult
    while t >= mult:
        if r % t == 0:
            return t
        t -= mult
    return r


def _pair_add(g, recv, core, *, name):
    _, _, r2, cols = g.shape
    tr = _rtile(r2, 256, 16, row_bytes=2 * cols)

    def body(c_ref, g_ref, r_ref, o_ref):
        o_ref[...] = (g_ref[...].astype(F32) + r_ref[...].astype(F32)).astype(o_ref.dtype)

    blk = pl.BlockSpec((None, tr, cols), lambda j, i, c_ref: (j, i, 0))
    return pl.pallas_call(
        body, name=name,
        grid_spec=pltpu.PrefetchScalarGridSpec(
            num_scalar_prefetch=1, grid=(N_CHIPS, r2 // tr),
            in_specs=[pl.BlockSpec((None, None, tr, cols), lambda j, i, c_ref: (j, c_ref[0], i, 0)), blk],
            out_specs=blk),
        out_shape=jax.ShapeDtypeStruct(recv.shape, recv.dtype), compiler_params=_params(),
    )(core, g, recv)


def _sum_slots(a, out_dtype, *, name):
    n, r, cols = a.shape
    tr = _rtile(r, 256, 16)

    def body(a_ref, o_ref):
        acc = a_ref[0].astype(F32)
        for j in range(1, n):
            acc = acc + a_ref[j].astype(F32)
        o_ref[...] = acc.astype(o_ref.dtype)

    return pl.pallas_call(
        body, name=name, grid=(r // tr,),
        in_specs=[pl.BlockSpec((n, tr, cols), lambda i: (0, i, 0))],
        out_specs=pl.BlockSpec((tr, cols), lambda i: (i, 0)),
        out_shape=jax.ShapeDtypeStruct((r, cols), out_dtype), compiler_params=_params(),
    )(a)


def _chip_sum(own, recv, chip, *, name):
    _, r2, cols = own.shape
    tr = _rtile(r2, 256, 16, row_bytes=2 * cols)

    def body(chip_ref, own_ref, *rest):
        o_ref = rest[-1]
        acc = None
        for j in range(N_CHIPS):
            term = jnp.where(chip_ref[0] == j, own_ref[...], rest[j][...]).astype(F32)
            acc = term if acc is None else acc + term
        o_ref[...] = acc

    def slot(j):
        return pl.BlockSpec((None, tr, cols),
                            lambda i, chip_ref: (jnp.where(chip_ref[0] == j, (j + 1) % N_CHIPS, j), i, 0))

    return pl.pallas_call(
        body, name=name,
        grid_spec=pltpu.PrefetchScalarGridSpec(
            num_scalar_prefetch=1, grid=(r2 // tr,),
            in_specs=[pl.BlockSpec((None, tr, cols), lambda i, chip_ref: (chip_ref[0], i, 0))]
                     + [slot(j) for j in range(N_CHIPS)],
            out_specs=pl.BlockSpec((tr, cols), lambda i, chip_ref: (i, 0))),
        out_shape=jax.ShapeDtypeStruct((r2, cols), F32), compiler_params=_params(),
    )(chip, own, *([recv] * N_CHIPS))


def _adam_update(w, gv, m, v):
    c1 = 1.0 / (1.0 - ADAM_B1 ** ADAM_STEP)
    c2 = 1.0 / (1.0 - ADAM_B2 ** ADAM_STEP)
    nm = ADAM_B1 * m + (1.0 - ADAM_B1) * gv
    nv = ADAM_B2 * v + (1.0 - ADAM_B2) * gv * gv
    return -ADAM_LR * ((nm * c1) / (jnp.sqrt(nv * c2) + ADAM_EPS) + ADAM_WD * w), nm, nv


def _adamw_halves(w, g_mine, g_other, m, v, core, *, name):
    r, cols = w.shape
    r2 = r // 2
    tr = _rtile(r2, 256, 8, row_bytes=4 * cols)
    nt = r2 // tr

    def body(core_ref, w_ref, gm_ref, go_ref, m_ref, v_ref, g_ref, d_ref, nm_ref, nv_ref):
        gv = jnp.where(pl.program_id(0) == core_ref[0], gm_ref[...], go_ref[...])
        g_ref[...] = gv
        d_ref[...], nm_ref[...], nv_ref[...] = _adam_update(w_ref[...], gv, m_ref[...], v_ref[...])

    full = pl.BlockSpec((tr, cols), lambda hf, i, core_ref: (hf * nt + i, 0))
    half = pl.BlockSpec((tr, cols), lambda hf, i, core_ref: (i, 0))
    shape = jax.ShapeDtypeStruct((r, cols), F32)
    return pl.pallas_call(
        body, name=name,
        grid_spec=pltpu.PrefetchScalarGridSpec(
            num_scalar_prefetch=1, grid=(2, nt), in_specs=[full, half, half, full, full], out_specs=[full] * 4),
        out_shape=[shape] * 4, compiler_params=_params(),
    )(core, w, g_mine, g_other, m, v)


def _adamw_split_rows(w, g_mine, g_other, m, v, core, *, name, tc=256):
    r, cols = w.shape
    r2 = g_mine.shape[0]
    tc = _tile(cols, tc)

    def body(core_ref, w_ref, gm_ref, go_ref, m_ref, v_ref, g_ref, d_ref, nm_ref, nv_ref):
        mine_first = core_ref[0] == 0
        for lo, hi, first in ((0, r2, True), (r2, r, False)):
            n = hi - lo
            gm, go = gm_ref[0:n, :], go_ref[0:n, :]
            gv = jnp.where(mine_first, gm, go) if first else jnp.where(mine_first, go, gm)
            g_ref[lo:hi, :] = gv
            d_ref[lo:hi, :], nm_ref[lo:hi, :], nv_ref[lo:hi, :] = _adam_update(
                w_ref[lo:hi, :], gv, m_ref[lo:hi, :], v_ref[lo:hi, :])

    full = pl.BlockSpec((r, tc), lambda j, core_ref: (0, j))
    half = pl.BlockSpec((r2, tc), lambda j, core_ref: (0, j))
    shape = jax.ShapeDtypeStruct((r, cols), F32)
    return pl.pallas_call(
        body, name=name,
        grid_spec=pltpu.PrefetchScalarGridSpec(
            num_scalar_prefetch=1, grid=(cols // tc,), in_specs=[full, half, half, full, full],
            out_specs=[full] * 4),
        out_shape=[shape] * 4, compiler_params=_params(),
    )(core, w, g_mine, g_other, m, v)


def _adamw(w, g, m, v, *, name, rows=256):
    r, cols = w.shape
    tr = _rtile(r, rows, 8)

    def body(w_ref, g_ref, m_ref, v_ref, d_ref, nm_ref, nv_ref):
        d_ref[...], nm_ref[...], nv_ref[...] = _adam_update(w_ref[...], g_ref[...], m_ref[...], v_ref[...])

    blk = pl.BlockSpec((tr, cols), lambda i: (i, 0))
    shape = jax.ShapeDtypeStruct((r, cols), F32)
    return pl.pallas_call(
        body, name=name, grid=(r // tr,), in_specs=[blk] * 4, out_specs=[blk] * 3,
        out_shape=[shape] * 3, compiler_params=_params(),
    )(w, g, m, v)


_BIG = (("w_in", 1), ("w_branch_a", 0), ("w_branch_b", 0), ("w_out", 0), ("w_up", 1), ("w_down", 0),
        ("w_ple", 1), ("w_ple_gate", 0))
_SMALL = ("b_f", "gmlp_ln_g", "gmlp_ln_b", "gmlp_w_s", "gmlp_b_s", "norm_ffn_g", "conv_b", "norm_ple_g",
          "norm_final_g", "norm_mix_g")
_WEIGHTS = ("norm_mix_g", "w_in", "b_f", "gmlp_ln_g", "gmlp_ln_b", "gmlp_w_s", "gmlp_b_s", "w_branch_a",
            "w_branch_b", "w_out", "norm_ffn_g", "w_up", "conv_w", "conv_b", "w_down", "norm_ple_g", "w_ple",
            "w_ple_gate", "norm_final_g")
_PACK_ROWS = 8


def _pack(arrays):
    parts = []
    for a in arrays:
        flat = a.reshape(-1)
        unit = _PACK_ROWS * LANES
        flat = jnp.pad(flat, (0, (-flat.shape[0]) % unit))
        parts.append(flat.reshape(-1, LANES))
    return jnp.concatenate(parts, axis=0)


def _unpack(packed, shapes):
    out, row = [], 0
    for shp in shapes:
        size = math.prod(shp)
        rows = -(-size // (_PACK_ROWS * LANES)) * _PACK_ROWS
        out.append(packed[row:row + rows].reshape(-1)[:size].reshape(shp))
        row += rows
    return out


def _take_cols(parts, lo, hi):
    out, start = [], 0
    for a in parts:
        width = a.shape[1]
        a0, a1 = max(lo, start) - start, min(hi, start + width) - start
        if a1 > a0:
            out.append(a if (a0, a1) == (0, width) else a[:, a0:a1])
        start += width
    return out[0] if len(out) == 1 else jnp.concatenate(out, axis=1)


def _take_rows(parts, lo, hi):
    out, start = [], 0
    for a in parts:
        height = a.shape[0]
        a0, a1 = max(lo, start) - start, min(hi, start + height) - start
        if a1 > a0:
            out.append(a if (a0, a1) == (0, height) else a[a0:a1])
        start += height
    return out[0] if len(out) == 1 else jnp.concatenate(out, axis=0)


def _assemble(gathered, axis):
    n, r, cols = gathered.shape
    if axis == 0:
        return gathered.reshape(n * r, cols)
    return _take_cols([gathered[j] for j in range(n)], 0, n * cols)


def _to_chunks(parts, axis):
    rows, total = parts[0].shape[0], sum(a.shape[1] for a in parts)
    if axis == 0:
        r, cols = rows // N_CHIPS, total
        chunks = _take_cols(parts, 0, total).reshape(N_CHIPS, r, cols)
    else:
        r, cols = rows, total // N_CHIPS
        chunks = jnp.stack([_take_cols(parts, j * cols, (j + 1) * cols) for j in range(N_CHIPS)])
    return chunks.reshape(N_CHIPS, 2, r // 2, cols)


def kernel(x, p, norm_mix_g, w_in, b_f, gmlp_ln_g, gmlp_ln_b, gmlp_w_s, gmlp_b_s, w_branch_a, w_branch_b, w_out, norm_ffn_g, w_up, conv_w, conv_b, w_down, norm_ple_g, w_ple, w_ple_gate, norm_final_g, loss_target, m_norm_mix_g, m_w_in, m_b_f, m_gmlp_ln_g, m_gmlp_ln_b, m_gmlp_w_s, m_gmlp_b_s, m_w_branch_a, m_w_branch_b, m_w_out, m_norm_ffn_g, m_w_up, m_conv_w, m_conv_b, m_w_down, m_norm_ple_g, m_w_ple, m_w_ple_gate, m_norm_final_g, v_norm_mix_g, v_w_in, v_b_f, v_gmlp_ln_g, v_gmlp_ln_b, v_gmlp_w_s, v_gmlp_b_s, v_w_branch_a, v_w_branch_b, v_w_out, v_norm_ffn_g, v_w_up, v_conv_w, v_conv_b, v_w_down, v_norm_ple_g, v_w_ple, v_w_ple_gate, v_norm_final_g):
    args = dict(locals())
    wt = {n: args[n] for n in _WEIGHTS}
    mom = {n: args["m_" + n] for n in _WEIGHTS}
    var = {n: args["v_" + n] for n in _WEIGHTS}
    chip = 2 * lax.axis_index("x") + lax.axis_index("y")
    core = lax.axis_index("c").astype(jnp.int32).reshape(1)

    chip1 = chip.astype(jnp.int32).reshape(1)
    device = 2 * chip + lax.axis_index("c")
    axis_of = dict(_BIG)
    names = [n for n, _ in _BIG]
    put_mine = lambda land, mine: lax.dynamic_update_index_in_dim(land, mine, chip, 0)

    shard_in = w_in[0].astype(BF16)
    sems_in = _copies_start([shard_in], [lax.empty((N_CHIPS,) + shard_in.shape, BF16)], _gather_plan([True]), 3,
                            name="gather_start_in")
    shards = [wt[n][0].astype(BF16) for n in names[1:]] + [conv_w[0]]
    halved = [True] * len(names[1:]) + [False]
    lands = [lax.empty((N_CHIPS,) + a.shape, a.dtype) for a in shards]
    send_sems, recv_sems, srcs, lands, rest_token = _copies_start(
        shards, lands, _gather_plan(halved), 3 * len(shards), name="gather_start_rest", after=[sems_in[4]])
    o1 = 2 * GMLP_WIDTH
    o2 = o1 + 3 * FOX_WIDTH
    o3 = o2 + FOX_HEADS
    fpad = ((0, 0), (0, LANES - FOX_HEADS))
    w = {
        "conv_b": conv_b, "norm_mix_g": norm_mix_g, "norm_ffn_g": norm_ffn_g, "norm_ple_g": norm_ple_g,
        "norm_final_g": norm_final_g.reshape(1, D_MODEL), "b_f": jnp.pad(b_f, fpad),
        "gmlp_ln_g": gmlp_ln_g, "gmlp_ln_b": gmlp_ln_b, "gmlp_w_s": gmlp_w_s[0],
        "gmlp_b_s_t": jnp.pad(gmlp_b_s[0].T, ((0, 0), (0, LANES - GMLP_GROUPS))),
        "first_dep": sems_in[4], "proj_dep": rest_token,
    }

    def get_w_in(after):
        _, got = _copies_wait(sems_in[0], sems_in[1], sems_in[2], sems_in[3], _gather_plan([True]), 0, [after],
                              name="gather_wait_in")
        got = _forward_halves(got, name="gather_forward_in")
        slots = put_mine(got[0], shard_in)
        slots = [slots[j] for j in range(N_CHIPS)]
        return {"w_uv": _take_cols(slots, 0, o1), "w_qkv": _take_cols(slots, o1, o2),
                "w_f": jnp.pad(_take_cols(slots, o2, o3), fpad), "w_g": _take_cols(slots, o3, o3 + 2 * D_MODEL)}

    def get_w_rest(after):
        _, got = _copies_wait(send_sems, recv_sems, srcs, lands, _gather_plan(halved), 0, [after],
                              name="gather_wait_rest")
        got = list(_forward_halves(got[:-1], name="gather_forward_rest")) + got[-1:]
        slots = {n: put_mine(got[i], shards[i]) for i, n in enumerate(names[1:])}
        full = {n: _assemble(slots[n], axis_of[n]) for n in names[1:] if n != "w_up"}
        up = [slots["w_up"][j] for j in range(N_CHIPS)]
        return {"w_branch_a": full["w_branch_a"], "w_branch_b": full["w_branch_b"], "w_out": full["w_out"],
                "w_up_a": _take_cols(up, 0, D_FF), "w_up_b": _take_cols(up, D_FF, 2 * D_FF),
                "w_down": full["w_down"], "w_ple": full["w_ple"], "w_ple_gate": full["w_ple_gate"],
                "conv_w": _assemble(put_mine(got[-1], shards[-1]), 1)}

    grads, delta, new_m, new_v = {}, {}, {}, {}
    pending = {}

    def to_chunks(n, gr):
        return _to_chunks(gr if isinstance(gr, list) else [gr], axis_of[n])

    def reduce_start(group, gfull, tag):
        chunks = [to_chunks(n, gfull[n]) for n in group]
        from_sibling = _pair_exchange(chunks, name="grad_pair_exchange_" + tag)
        pair_sums = [_pair_add(chunks[i], from_sibling[i], core, name="grad_pair_add_" + n) for i, n in enumerate(group)]
        empty = [lax.empty(a.shape, a.dtype) for a in pair_sums]
        ssem, rsem, own, recv, token = _copies_start(pair_sums, empty, _chip_plan, 3 * len(group),
                                                     name="grad_chip_start_" + tag)
        pending[tag] = (ssem, rsem, own, recv)
        return token

    def reduce_finish(group, tag, after):
        ssem, rsem, own, recv = pending[tag]
        own, recv = _copies_wait(ssem, rsem, own, recv, _chip_plan, 0, after, name="grad_chip_wait_" + tag)
        halves = [_chip_sum(own[i], recv[i], chip1, name="grad_chip_sum_" + n) for i, n in enumerate(group)]
        other_halves = _pair_share(halves, name="grad_pair_share_" + tag)
        for i, n in enumerate(group):
            shp = wt[n].shape
            outs = _adamw_halves(wt[n].reshape(shp[-2:]), halves[i], other_halves[i], mom[n].reshape(shp[-2:]),
                                 var[n].reshape(shp[-2:]), core, name="adamw_" + n)
            grads[n], delta[n], new_m[n], new_v[n] = (o.reshape(shp) for o in outs)
        return new_v[group[-1]]

    ffn_group = ("w_up", "w_down", "w_ple", "w_ple_gate")
    mix_group = ("w_in", "w_branch_a", "w_branch_b", "w_out")

    def on_grads_ffn(g):
        gfull = dict(g)
        gfull["w_up"] = [g["w_up_a"], g["w_up_b"]]
        return reduce_start(ffn_group, gfull, "ffn")

    def on_grads_mix(g):
        early = [g[n] if n != "b_f" else g[n][:, :FOX_HEADS] for n in _SMALL[:-1]] + [g["conv_w"]]
        vec = _pack(early)
        ssem, rsem, own, recv, small_token = _copies_start(
            [vec], [lax.empty((8,) + vec.shape, F32)], _all_plan, 7, name="small_start")
        pending["small"] = (ssem, rsem, own, recv)
        gfull = dict(g)
        gfull["w_in"] = [g["w_uv"], g["w_qkv"], g["w_f"][:, :FOX_HEADS], g["w_g"]]
        token = reduce_start(mix_group, gfull, "mix")
        pending["ffn_done"] = reduce_finish(ffn_group, "ffn", [token])
        return token + small_token

    loss, grad_x, g = _device_step(x[0], p[0, 0], loss_target[0], w, get_w_in, get_w_rest, on_grads_ffn, on_grads_mix)

    ssem, rsem, own, recv = pending["small"]
    own, recv = _copies_wait(ssem, rsem, own, recv, _all_plan, 0, [grad_x, pending["ffn_done"]], name="small_wait")
    vec_early = _sum_slots(lax.dynamic_update_index_in_dim(recv[0], own[0], device, 0), F32, name="small_sum")
    vec_late = _pack([g["norm_mix_g"]])
    vec_late = _sum_slots(_all_exchange(vec_late, name="small_exchange_late"), F32, name="small_sum_late")
    early_rows = _pack([wt[n] for n in _SMALL[:-1]]).shape[0]
    vec = jnp.concatenate([vec_early[:early_rows], vec_late], axis=0)
    for n, a in zip(_SMALL, _unpack(vec, [wt[n].shape for n in _SMALL])):
        grads[n] = a
    conv_w_grad = _unpack(vec_early[early_rows:], [(3, 2 * D_FF)])[0]
    grads["conv_w"] = lax.dynamic_slice_in_dim(conv_w_grad, chip * conv_w.shape[2], conv_w.shape[2], axis=1).reshape(conv_w.shape)

    reduce_finish(mix_group, "mix", [grad_x, pending["ffn_done"], vec])
    shp = conv_w.shape
    outs = _adamw(conv_w.reshape(shp[-2:]), grads["conv_w"].reshape(shp[-2:]), m_conv_w.reshape(shp[-2:]),
                  v_conv_w.reshape(shp[-2:]), name="adamw_conv_w")
    delta["conv_w"], new_m["conv_w"], new_v["conv_w"] = (o.reshape(shp) for o in outs)
    outs = _adamw(_pack([wt[n] for n in _SMALL]), vec, _pack([mom[n] for n in _SMALL]),
                  _pack([var[n] for n in _SMALL]), name="adamw_small", rows=2048)
    for d, o in zip((delta, new_m, new_v), outs):
        for n, a in zip(_SMALL, _unpack(o, [wt[n].shape for n in _SMALL])):
            d[n] = a

    total_loss = lax.psum(loss[0, 0], ("x", "y", "c"))
    return (total_loss, grad_x.reshape(x.shape), *[grads[n] for n in _WEIGHTS], *[delta[n] for n in _WEIGHTS],
            *[new_m[n] for n in _WEIGHTS], *[new_v[n] for n in _WEIGHTS])
```

```python
import functools
import math

import jax
import jax.numpy as jnp
from jax import lax
from jax.experimental import pallas as pl
from jax.experimental.pallas import tpu as pltpu

F32 = jnp.float32
BF16 = jnp.bfloat16

D_MODEL = 1024
EPS = 1e-6
CHUNK = 64
GMLP_GROUPS = 8
GMLP_BLOCK = 128
GMLP_WIDTH = 1024
FOX_HEADS = 16
FOX_HEAD_DIM = 64
FOX_WIDTH = 1024
HEAD_PAIRS = FOX_HEADS // 2
ATT_BLOCK = 128
D_FF = 2816
PLE_DIM = 256
LANES = 128
BF16_TILE_ROWS = 16
N_CHIPS = 4

ADAM_LR = 0.001
ADAM_B1 = 0.9
ADAM_B2 = 0.999
ADAM_EPS = 1e-08
ADAM_WD = 0.01
ADAM_STEP = 10

VMEM_LIMIT = 56 * 1024 * 1024
MESH = pl.DeviceIdType.MESH

_NN = (((1,), (0,)), ((), ()))
_NT = (((1,), (1,)), ((), ()))
_TN = (((0,), (0,)), ((), ()))


def _params(**kw):
    return pltpu.CompilerParams(vmem_limit_bytes=VMEM_LIMIT, **kw)


def _tile(dim, pref):
    if dim <= pref:
        return dim
    t = (pref // LANES) * LANES
    while t >= LANES:
        if dim % t == 0:
            return t
        t -= LANES
    return dim


def _dot(a, b, dn):
    return lax.dot_general(a.astype(BF16), b.astype(BF16), dn, preferred_element_type=F32)


def _gelu(x):
    c = math.sqrt(2.0 / math.pi)
    t = jnp.tanh(c * (x + 0.044715 * x * x * x))
    return 0.5 * x * (1.0 + t)


def _gelu_and_grad(x):
    c = math.sqrt(2.0 / math.pi)
    x2 = x * x
    t = jnp.tanh(c * (x + 0.044715 * x2 * x))
    g = 0.5 * x * (1.0 + t)
    dg = 0.5 * (1.0 + t) + 0.5 * x * (1.0 - t * t) * c * (1.0 + 3.0 * 0.044715 * x2)
    return g, dg


def _sigmoid(x):
    return 1.0 / (1.0 + jnp.exp(-x))


def _mm(a, b, *, mode, out_dtype, name, add=None, tm=512, tn=512, dep=None):
    if mode == "nn":
        m, k = a.shape
        k2, n = b.shape
    elif mode == "nt":
        m, k = a.shape
        n, k2 = b.shape
    else:
        k, m = a.shape
        k2, n = b.shape
    assert k == k2, (name, a.shape, b.shape)
    tm = _tile(m, tm)
    tn = _tile(n, tn)
    dn = {"nn": _NN, "nt": _NT, "tn": _TN}[mode]

    def body(a_ref, b_ref, *rest):
        o_ref = rest[-1]
        acc = _dot(a_ref[...], b_ref[...], dn)
        if add is not None:
            acc = acc + rest[0][...].astype(F32)
        o_ref[...] = acc.astype(o_ref.dtype)

    a_spec = pl.BlockSpec((k, tm), lambda i, j: (0, i)) if mode == "tn" else pl.BlockSpec((tm, k), lambda i, j: (i, 0))
    b_spec = pl.BlockSpec((tn, k), lambda i, j: (j, 0)) if mode == "nt" else pl.BlockSpec((k, tn), lambda i, j: (0, j))
    o_spec = pl.BlockSpec((tm, tn), lambda i, j: (i, j))
    in_specs = [a_spec, b_spec]
    args = [a, b]
    if add is not None:
        in_specs.append(o_spec)
        args.append(add)
    if dep is not None:
        in_specs.append(pl.BlockSpec(memory_space=pl.ANY))
        args.append(dep)
    return pl.pallas_call(
        body, name=name, grid=(m // tm, n // tn), in_specs=in_specs, out_specs=o_spec,
        out_shape=jax.ShapeDtypeStruct((m, n), out_dtype), compiler_params=_params(),
    )(*args)


def _mm_nt_sum(pairs, *, out_dtype, name, tm=256, dep=None):
    m, n = pairs[0][0].shape[0], pairs[0][1].shape[0]
    tm = _tile(m, tm)
    np_ = len(pairs)

    def body(*refs):
        o_ref = refs[-1] if dep is None else refs[-1]
        acc = None
        for p in range(np_):
            part = _dot(refs[2 * p][...], refs[2 * p + 1][...], _NT)
            acc = part if acc is None else acc + part
        o_ref[...] = acc.astype(o_ref.dtype)

    in_specs, args = [], []
    for a, b in pairs:
        assert a.shape[0] == m and b.shape[0] == n and a.shape[1] == b.shape[1], (name, a.shape, b.shape)
        in_specs += [pl.BlockSpec((tm, a.shape[1]), lambda i: (i, 0)), pl.BlockSpec(b.shape, lambda i: (0, 0))]
        args += [a, b]
    if dep is not None:
        in_specs.append(pl.BlockSpec(memory_space=pl.ANY))
        args.append(dep)
    return pl.pallas_call(
        body, name=name, grid=(m // tm,), in_specs=in_specs, out_specs=pl.BlockSpec((tm, n), lambda i: (i, 0)),
        out_shape=jax.ShapeDtypeStruct((m, n), out_dtype), compiler_params=_params(),
    )(*args)


def _rms_fwd(x, g, *, name, tm=256, dep=None):
    s, d = x.shape
    tm = _tile(s, tm)

    def body(x_ref, g_ref, *rest):
        h_ref = rest[-1]
        xv = x_ref[...]
        r = lax.rsqrt(jnp.mean(xv * xv, axis=-1, keepdims=True) + EPS)
        h_ref[...] = (xv * r * g_ref[...]).astype(h_ref.dtype)

    deps = [] if dep is None else [dep]
    return pl.pallas_call(
        body, name=name, grid=(s // tm,),
        in_specs=[pl.BlockSpec((tm, d), lambda i: (i, 0)), pl.BlockSpec((1, d), lambda i: (0, 0))]
                 + [pl.BlockSpec(memory_space=pl.ANY)] * len(deps),
        out_specs=pl.BlockSpec((tm, d), lambda i: (i, 0)),
        out_shape=jax.ShapeDtypeStruct((s, d), BF16), compiler_params=_params(),
    )(x, g, *deps)


def _rms_bwd(x, g, dh, dres, *, name, tm=256):
    s, d = x.shape
    tm = _tile(s, tm)

    def body(x_ref, g_ref, dh_ref, dres_ref, dx_ref, dxb_ref, dg_ref):
        xv = x_ref[...]
        r = lax.rsqrt(jnp.mean(xv * xv, axis=-1, keepdims=True) + EPS)
        xhat = xv * r
        dhv = dh_ref[...].astype(F32)
        dyg = dhv * g_ref[...]
        dx = dres_ref[...] + r * (dyg - xhat * jnp.mean(dyg * xhat, axis=-1, keepdims=True))
        dx_ref[...] = dx
        dxb_ref[...] = dx.astype(dxb_ref.dtype)

        @pl.when(pl.program_id(0) == 0)
        def _():
            dg_ref[...] = jnp.zeros_like(dg_ref)

        dg_ref[...] += jnp.sum(dhv * xhat, axis=0, keepdims=True)

    row = pl.BlockSpec((tm, d), lambda i: (i, 0))
    vec = pl.BlockSpec((1, d), lambda i: (0, 0))
    return pl.pallas_call(
        body, name=name, grid=(s // tm,), in_specs=[row, vec, row, row], out_specs=[row, row, vec],
        out_shape=[jax.ShapeDtypeStruct((s, d), F32), jax.ShapeDtypeStruct((s, d), BF16),
                   jax.ShapeDtypeStruct((1, d), F32)],
        compiler_params=_params(),
    )(x, g, dh, dres)


def _gmlp_mask():
    t = lax.broadcasted_iota(jnp.int32, (GMLP_BLOCK, GMLP_BLOCK), 0)
    s_ = lax.broadcasted_iota(jnp.int32, (GMLP_BLOCK, GMLP_BLOCK), 1)
    return (s_ // CHUNK) <= (t // CHUNK)


def _gmlp_norm(zv, ln_g, ln_b):
    vv, dvv = _gelu_and_grad(zv)
    mu = jnp.mean(vv, axis=-1, keepdims=True)
    xc = vv - mu
    rstd = lax.rsqrt(jnp.mean(xc * xc, axis=-1, keepdims=True) + EPS)
    vhat = xc * rstd
    return vhat * ln_g + ln_b, vhat, rstd, dvv


def _gmlp_fwd(z_uv, ln_g, ln_b, w_s, b_s_t, *, name):
    s = z_uv.shape[0]
    w = GMLP_WIDTH
    gd = w // GMLP_GROUPS

    def body(z_ref, lg_ref, lb_ref, ws_ref, bs_ref, a_ref):
        u = _gelu(z_ref[:, :w].astype(F32))
        vn, _, _, _ = _gmlp_norm(z_ref[:, w:].astype(F32), lg_ref[...], lb_ref[...])
        mask = _gmlp_mask()
        for g in range(GMLP_GROUPS):
            wm = jnp.where(mask, ws_ref[g], 0.0)
            mixed = _dot(wm, vn[:, g * gd:(g + 1) * gd], _NN) + bs_ref[:, g:g + 1]
            a_ref[:, g * gd:(g + 1) * gd] = (u[:, g * gd:(g + 1) * gd] * mixed).astype(a_ref.dtype)

    full = lambda shape: pl.BlockSpec(shape, lambda i: (0,) * len(shape))
    return pl.pallas_call(
        body, name=name, grid=(s // GMLP_BLOCK,),
        in_specs=[pl.BlockSpec((GMLP_BLOCK, 2 * w), lambda i: (i, 0)), full((1, w)), full((1, w)),
                  full((GMLP_GROUPS, GMLP_BLOCK, GMLP_BLOCK)), full((GMLP_BLOCK, LANES))],
        out_specs=pl.BlockSpec((GMLP_BLOCK, w), lambda i: (i, 0)),
        out_shape=jax.ShapeDtypeStruct((s, w), BF16), compiler_params=_params(),
    )(z_uv, ln_g, ln_b, w_s, b_s_t)


def _gmlp_bwd(z_uv, da, ln_g, ln_b, w_s, b_s_t, *, name):
    s = z_uv.shape[0]
    w = GMLP_WIDTH
    gd = w // GMLP_GROUPS

    def body(z_ref, da_ref, lg_ref, lb_ref, ws_ref, bs_ref, dz_ref, dws_ref, dbs_ref, dlg_ref, dlb_ref):
        @pl.when(pl.program_id(0) == 0)
        def _():
            dws_ref[...] = jnp.zeros_like(dws_ref)
            dbs_ref[...] = jnp.zeros_like(dbs_ref)
            dlg_ref[...] = jnp.zeros_like(dlg_ref)
            dlb_ref[...] = jnp.zeros_like(dlb_ref)

        u, du_dz = _gelu_and_grad(z_ref[:, :w].astype(F32))
        lg = lg_ref[...]
        vn, vhat, rstd, dvv_dz = _gmlp_norm(z_ref[:, w:].astype(F32), lg, lb_ref[...])
        dav = da_ref[...].astype(F32)
        mask = _gmlp_mask()
        lane = lax.broadcasted_iota(jnp.int32, (GMLP_BLOCK, LANES), 1)
        dvn_parts = []
        dbs = jnp.zeros((GMLP_BLOCK, LANES), F32)
        for g in range(GMLP_GROUPS):
            sl = slice(g * gd, (g + 1) * gd)
            wm = jnp.where(mask, ws_ref[g], 0.0)
            vn_g = vn[:, sl]
            mixed = _dot(wm, vn_g, _NN) + bs_ref[:, g:g + 1]
            dmixed = dav[:, sl] * u[:, sl]
            dz_ref[:, sl] = (dav[:, sl] * mixed * du_dz[:, sl]).astype(dz_ref.dtype)
            dvn_parts.append(_dot(wm, dmixed, _TN))
            dws_ref[g] += jnp.where(mask, _dot(dmixed, vn_g, _NT), 0.0)
            dbs = dbs + jnp.where(lane == g, jnp.sum(dmixed, axis=-1, keepdims=True), 0.0)
        dbs_ref[...] += dbs
        dvn = jnp.concatenate(dvn_parts, axis=-1)
        dlg_ref[...] += jnp.sum(dvn * vhat, axis=0, keepdims=True)
        dlb_ref[...] += jnp.sum(dvn, axis=0, keepdims=True)
        dyg = dvn * lg
        dvv = rstd * (dyg - jnp.mean(dyg, axis=-1, keepdims=True)
                      - vhat * jnp.mean(dyg * vhat, axis=-1, keepdims=True))
        dz_ref[:, w:] = (dvv * dvv_dz).astype(dz_ref.dtype)

    full = lambda shape: pl.BlockSpec(shape, lambda i: (0,) * len(shape))
    return pl.pallas_call(
        body, name=name, grid=(s // GMLP_BLOCK,),
        in_specs=[pl.BlockSpec((GMLP_BLOCK, 2 * w), lambda i: (i, 0)),
                  pl.BlockSpec((GMLP_BLOCK, w), lambda i: (i, 0)), full((1, w)), full((1, w)),
                  full((GMLP_GROUPS, GMLP_BLOCK, GMLP_BLOCK)), full((GMLP_BLOCK, LANES))],
        out_specs=[pl.BlockSpec((GMLP_BLOCK, 2 * w), lambda i: (i, 0)),
                   full((GMLP_GROUPS, GMLP_BLOCK, GMLP_BLOCK)), full((GMLP_BLOCK, LANES)),
                   full((1, w)), full((1, w))],
        out_shape=[jax.ShapeDtypeStruct((s, 2 * w), BF16),
                   jax.ShapeDtypeStruct((GMLP_GROUPS, GMLP_BLOCK, GMLP_BLOCK), F32),
                   jax.ShapeDtypeStruct((GMLP_BLOCK, LANES), F32),
                   jax.ShapeDtypeStruct((1, w), F32), jax.ShapeDtypeStruct((1, w), F32)],
        compiler_params=_params(),
    )(z_uv, da, ln_g, ln_b, w_s, b_s_t)


def _tri(lower):
    r = lax.broadcasted_iota(jnp.int32, (ATT_BLOCK, ATT_BLOCK), 0)
    c = lax.broadcasted_iota(jnp.int32, (ATT_BLOCK, ATT_BLOCK), 1)
    return jnp.where((c <= r) if lower else (c >= r), 1.0, 0.0).astype(F32)


def _log_sigmoid(x):
    return jnp.minimum(x, 0.0) - jnp.log(1.0 + jnp.exp(-jnp.abs(x)))


def _fox_cum(f, b_f, *, name):
    s = f.shape[0]
    nb = s // ATT_BLOCK

    def body(f_ref, b_ref, cb_ref, ct_ref, carry):
        @pl.when(pl.program_id(0) == 0)
        def _():
            carry[...] = jnp.zeros_like(carry)

        lf = _log_sigmoid(f_ref[...] + b_ref[...])
        cum = lax.dot_general(_tri(True), lf, _NN, precision=lax.Precision.HIGHEST,
                              preferred_element_type=F32) + carry[...]
        carry[...] = cum[ATT_BLOCK - 1:ATT_BLOCK, :]
        for h in range(FOX_HEADS):
            cb_ref[h] = jnp.broadcast_to(cum[:, h:h + 1], (ATT_BLOCK, LANES))
        ct_ref[...] = cum.T

    return pl.pallas_call(
        body, name=name, grid=(nb,),
        in_specs=[pl.BlockSpec((ATT_BLOCK, LANES), lambda i: (i, 0)), pl.BlockSpec((1, LANES), lambda i: (0, 0))],
        out_specs=[pl.BlockSpec((FOX_HEADS, ATT_BLOCK, LANES), lambda i: (0, i, 0)),
                   pl.BlockSpec((LANES, ATT_BLOCK), lambda i: (0, i))],
        out_shape=[jax.ShapeDtypeStruct((FOX_HEADS, s, LANES), F32), jax.ShapeDtypeStruct((LANES, s), F32)],
        scratch_shapes=[pltpu.VMEM((1, LANES), F32)], compiler_params=_params(),
    )(f, b_f)


def _fox_dlogit(dcum_t, f, b_f, *, name):
    s = f.shape[0]
    nb = s // ATT_BLOCK

    def body(dc_ref, f_ref, b_ref, df_ref, db_ref, carry):
        @pl.when(pl.program_id(0) == 0)
        def _():
            carry[...] = jnp.zeros_like(carry)
            db_ref[...] = jnp.zeros_like(db_ref)

        d = dc_ref[...].T
        dlog = lax.dot_general(_tri(False), d, _NN, precision=lax.Precision.HIGHEST,
                               preferred_element_type=F32) + carry[...]
        carry[...] = dlog[0:1, :]
        df = dlog * (1.0 - _sigmoid(f_ref[...] + b_ref[...]))
        df_ref[...] = df
        db_ref[...] += jnp.sum(df, axis=0, keepdims=True)

    rev = lambda i: nb - 1 - i
    return pl.pallas_call(
        body, name=name, grid=(nb,),
        in_specs=[pl.BlockSpec((LANES, ATT_BLOCK), lambda i: (0, rev(i))),
                  pl.BlockSpec((ATT_BLOCK, LANES), lambda i: (rev(i), 0)),
                  pl.BlockSpec((1, LANES), lambda i: (0, 0))],
        out_specs=[pl.BlockSpec((ATT_BLOCK, LANES), lambda i: (rev(i), 0)),
                   pl.BlockSpec((1, LANES), lambda i: (0, 0))],
        out_shape=[jax.ShapeDtypeStruct((s, LANES), F32), jax.ShapeDtypeStruct((1, LANES), F32)],
        scratch_shapes=[pltpu.VMEM((1, LANES), F32)], compiler_params=_params(),
    )(dcum_t, f, b_f)


def _causal(qi, ki):
    r = lax.broadcasted_iota(jnp.int32, (ATT_BLOCK, ATT_BLOCK), 0) + qi * ATT_BLOCK
    c = lax.broadcasted_iota(jnp.int32, (ATT_BLOCK, ATT_BLOCK), 1) + ki * ATT_BLOCK
    return c <= r


def _head_mask():
    return lax.broadcasted_iota(jnp.int32, (1, LANES), 1) < FOX_HEAD_DIM


def _attn_fwd(qkv, cum_b, cum_r, *, name):
    s = qkv.shape[0]
    nq = s // ATT_BLOCK
    scale = FOX_HEAD_DIM ** -0.5
    npair = HEAD_PAIRS

    def body(q_ref, k_ref, v_ref, cq_ref, ck_ref, o_ref, l_ref):
        qi = pl.program_id(1)
        m0 = _head_mask()
        q2 = q_ref[...]
        zero = jnp.zeros_like(q2)
        qs = (jnp.where(m0, q2, zero), jnp.where(m0, zero, q2))
        cqs = (cq_ref[0], cq_ref[1])

        def step(ki, carry, masked):
            off = pl.multiple_of(ki * ATT_BLOCK, ATT_BLOCK)
            k2 = k_ref[pl.ds(off, ATT_BLOCK), :]
            v2 = v_ref[pl.ds(off, ATT_BLOCK), :]
            out = []
            for hh in range(2):
                m, l, acc = carry[hh]
                sc = _dot(qs[hh], k2, _NT) * scale + (cqs[hh] - ck_ref[hh:hh + 1, pl.ds(off, ATT_BLOCK)])
                if masked:
                    sc = jnp.where(_causal(qi, ki), sc, -1e30)
                m_new = jnp.maximum(m, jnp.max(sc, axis=-1, keepdims=True))
                alpha = jnp.exp(m - m_new)
                p = jnp.exp(sc - m_new)
                l = alpha * l + jnp.sum(p, axis=-1, keepdims=True)
                acc = alpha * acc + _dot(p, v2, _NN)
                out.append((m_new, l, acc))
            return tuple(out)

        init = tuple((jnp.full((ATT_BLOCK, 1), -1e30, F32), jnp.zeros((ATT_BLOCK, 1), F32),
                      jnp.zeros((ATT_BLOCK, LANES), F32)) for _ in range(2))
        carry = lax.fori_loop(0, qi, lambda ki, c: step(ki, c, False), init)
        (ma, la, acca), (mb, lb, accb) = step(qi, carry, True)
        o_ref[...] = jnp.where(m0, acca / la, accb / lb).astype(o_ref.dtype)
        l_ref[0] = jnp.broadcast_to(ma + jnp.log(la), (ATT_BLOCK, LANES))
        l_ref[1] = jnp.broadcast_to(mb + jnp.log(lb), (ATT_BLOCK, LANES))

    stat = pl.BlockSpec((None, 2, ATT_BLOCK, LANES), lambda j, i: (j, 0, i, 0))
    row = pl.BlockSpec((None, 2, s), lambda j, i: (j, 0, 0))
    return pl.pallas_call(
        body, name=name, grid=(npair, nq),
        in_specs=[pl.BlockSpec((ATT_BLOCK, LANES), lambda j, i: (i, j)),
                  pl.BlockSpec((s, LANES), lambda j, i: (0, npair + j)),
                  pl.BlockSpec((s, LANES), lambda j, i: (0, 2 * npair + j)),
                  stat, row],
        out_specs=[pl.BlockSpec((ATT_BLOCK, LANES), lambda j, i: (i, j)), stat],
        out_shape=[jax.ShapeDtypeStruct((s, FOX_WIDTH), BF16),
                   jax.ShapeDtypeStruct((npair, 2, s, LANES), F32)],
        compiler_params=_params(),
    )(qkv, qkv, qkv, cum_b, cum_r)


def _attn_delta(qkv, do, lse_b, cum_b, cum_r, *, name):
    s = qkv.shape[0]
    nq = s // ATT_BLOCK
    scale = FOX_HEAD_DIM ** -0.5
    npair = HEAD_PAIRS

    def body(q_ref, k_ref, v_ref, do_ref, l_ref, cq_ref, ck_ref, d_ref):
        qi = pl.program_id(1)
        m0 = _head_mask()
        q2 = q_ref[...]
        do2 = do_ref[...]
        qs = (jnp.where(m0, q2, jnp.zeros_like(q2)), jnp.where(m0, jnp.zeros_like(q2), q2))
        dos = (jnp.where(m0, do2, jnp.zeros_like(do2)), jnp.where(m0, jnp.zeros_like(do2), do2))

        def step(ki, carry, masked):
            off = pl.multiple_of(ki * ATT_BLOCK, ATT_BLOCK)
            k2 = k_ref[pl.ds(off, ATT_BLOCK), :]
            v2 = v_ref[pl.ds(off, ATT_BLOCK), :]
            out = []
            for hh in range(2):
                sc = _dot(qs[hh], k2, _NT) * scale + (cq_ref[hh] - ck_ref[hh:hh + 1, pl.ds(off, ATT_BLOCK)])
                p = jnp.exp(sc - l_ref[hh])
                if masked:
                    p = jnp.where(_causal(qi, ki), p, 0.0)
                out.append(carry[hh] + jnp.sum(p * _dot(dos[hh], v2, _NT), axis=-1, keepdims=True))
            return tuple(out)

        init = (jnp.zeros((ATT_BLOCK, 1), F32), jnp.zeros((ATT_BLOCK, 1), F32))
        carry = lax.fori_loop(0, qi, lambda ki, c: step(ki, c, False), init)
        da, db = step(qi, carry, True)
        d_ref[0] = jnp.broadcast_to(da, (ATT_BLOCK, LANES))
        d_ref[1] = jnp.broadcast_to(db, (ATT_BLOCK, LANES))

    stat = pl.BlockSpec((None, 2, ATT_BLOCK, LANES), lambda j, i: (j, 0, i, 0))
    return pl.pallas_call(
        body, name=name, grid=(npair, nq),
        in_specs=[pl.BlockSpec((ATT_BLOCK, LANES), lambda j, i: (i, j)),
                  pl.BlockSpec((s, LANES), lambda j, i: (0, npair + j)),
                  pl.BlockSpec((s, LANES), lambda j, i: (0, 2 * npair + j)),
                  pl.BlockSpec((ATT_BLOCK, LANES), lambda j, i: (i, j)),
                  stat, stat, pl.BlockSpec((None, 2, s), lambda j, i: (j, 0, 0))],
        out_specs=stat,
        out_shape=jax.ShapeDtypeStruct((npair, 2, s, LANES), F32), compiler_params=_params(),
    )(qkv, qkv, qkv, do, lse_b, cum_b, cum_r)


def _attn_bwd(qkv, do, lse_b, delta_b, cum_b, cum_r, *, name):
    s = qkv.shape[0]
    nq = s // ATT_BLOCK
    scale = FOX_HEAD_DIM ** -0.5
    npair = HEAD_PAIRS

    def body(q_ref, k_ref, v_ref, do_ref, l_ref, dl_ref, cq_ref, ck_ref, dq_ref, dk_ref, dv_ref, dc_ref):
        ki = pl.program_id(1)
        m0 = _head_mask()
        k2 = k_ref[...]
        v2 = v_ref[...]
        koff = pl.multiple_of(ki * ATT_BLOCK, ATT_BLOCK)

        @pl.when(ki == 0)
        def _():
            dq_ref[...] = jnp.zeros_like(dq_ref)

        def step(qi, carry, masked):
            off = pl.multiple_of(qi * ATT_BLOCK, ATT_BLOCK)
            q2 = q_ref[pl.ds(off, ATT_BLOCK), :]
            do2 = do_ref[pl.ds(off, ATT_BLOCK), :]
            qzero = jnp.zeros_like(q2)
            dzero = jnp.zeros_like(do2)
            out = []
            dqs = []
            for hh in range(2):
                dk_acc, dv_acc, dc_acc = carry[hh]
                keep = m0 if hh == 0 else jnp.logical_not(m0)
                qh = jnp.where(keep, q2, qzero)
                doh = jnp.where(keep, do2, dzero)
                sc = _dot(qh, k2, _NT) * scale + (cq_ref[hh, pl.ds(off, ATT_BLOCK), :]
                                                 - ck_ref[hh:hh + 1, pl.ds(koff, ATT_BLOCK)])
                p = jnp.exp(sc - l_ref[hh, pl.ds(off, ATT_BLOCK), :])
                if masked:
                    p = jnp.where(_causal(qi, ki), p, 0.0)
                dp = _dot(doh, v2, _NT)
                ds = p * (dp - dl_ref[hh, pl.ds(off, ATT_BLOCK), :])
                dv_acc = dv_acc + _dot(p, do2, _TN)
                dk_acc = dk_acc + _dot(ds, q2, _TN)
                dc_acc = dc_acc - jnp.sum(ds, axis=0, keepdims=True)
                dqs.append(_dot(ds, k2, _NN))
                out.append((dk_acc, dv_acc, dc_acc))
            dq_ref[pl.ds(off, ATT_BLOCK), :] += jnp.where(m0, dqs[0], dqs[1]) * scale
            return tuple(out)

        init = tuple((jnp.zeros((ATT_BLOCK, LANES), F32), jnp.zeros((ATT_BLOCK, LANES), F32),
                      jnp.zeros((1, ATT_BLOCK), F32)) for _ in range(2))
        carry = step(ki, init, True)
        (dka, dva, dca), (dkb, dvb, dcb) = lax.fori_loop(ki + 1, nq, lambda qi, c: step(qi, c, False), carry)
        dk_ref[...] = (jnp.where(m0, dka, dkb) * scale).astype(dk_ref.dtype)
        dv_ref[...] = jnp.where(m0, dva, dvb).astype(dv_ref.dtype)
        dc_ref[0:1, :] = dca
        dc_ref[1:2, :] = dcb

    stat = pl.BlockSpec((None, 2, s, LANES), lambda j, i: (j, 0, 0, 0))
    colfull = lambda base: pl.BlockSpec((s, LANES), lambda j, i: (0, base + j))
    colblk = lambda base: pl.BlockSpec((ATT_BLOCK, LANES), lambda j, i: (i, base + j))
    return pl.pallas_call(
        body, name=name, grid=(npair, nq),
        in_specs=[colfull(0), colblk(npair), colblk(2 * npair), colfull(0), stat, stat, stat,
                  pl.BlockSpec((None, 2, s), lambda j, i: (j, 0, 0))],
        out_specs=[colfull(0), colblk(0), colblk(0), pl.BlockSpec((None, 2, ATT_BLOCK), lambda j, i: (j, 0, i))],
        out_shape=[jax.ShapeDtypeStruct((s, FOX_WIDTH), F32), jax.ShapeDtypeStruct((s, FOX_WIDTH), BF16),
                   jax.ShapeDtypeStruct((s, FOX_WIDTH), BF16), jax.ShapeDtypeStruct((npair, 2, s), F32)],
        compiler_params=_params(),
    )(qkv, qkv, qkv, do, lse_b, delta_b, cum_b, cum_r)


ATT_TQ = 256
ATT_TK = 256
ATT_SCALE = FOX_HEAD_DIM ** -0.5
assert ATT_SCALE == 0.125 and ATT_TQ == ATT_TK


def _causal_t(qi, ki):
    kpos = lax.broadcasted_iota(jnp.int32, (ATT_TK, ATT_TQ), 0) + ki * ATT_TK
    qpos = lax.broadcasted_iota(jnp.int32, (ATT_TK, ATT_TQ), 1) + qi * ATT_TQ
    return kpos <= qpos


def _row_mask():
    return lax.broadcasted_iota(jnp.int32, (LANES, 1), 0) < FOX_HEAD_DIM


def _lane_tile(a, width):
    return a if a.shape[1] == width else jnp.tile(a, (1, width // a.shape[1]))


def _transpose_bf16(a):
    return a.astype(F32).T.astype(BF16)


def _attn_fwd_t(qkv, cum_b, cum_r, *, name):
    s = qkv.shape[0]
    nq = s // ATT_TQ
    npair = HEAD_PAIRS

    def body(q_ref, k_ref, v_ref, cq_ref, ck_ref, o_ref, ot_ref, l_ref, vt_ref):
        qi = pl.program_id(1)
        rows = _row_mask()

        @pl.when(qi == 0)
        def _():
            vt_ref[...] = _transpose_bf16(v_ref[...])

        qt = _transpose_bf16(q_ref[...]) * ATT_SCALE
        zero = jnp.zeros_like(qt)
        qts = (jnp.where(rows, qt, zero), jnp.where(rows, zero, qt))

        def step(ki, carry, masked):
            off = pl.multiple_of(ki * ATT_TK, ATT_TK)
            k2 = k_ref[pl.ds(off, ATT_TK), :]
            vt = vt_ref[:, pl.ds(off, ATT_TK)]
            out = []
            for hh in range(2):
                m, l, acc = carry[hh]
                bias = cq_ref[hh:hh + 1, :] - _lane_tile(ck_ref[hh, pl.ds(off, ATT_TK), :], ATT_TQ)
                sc = _dot(k2, qts[hh], _NN) + bias
                if masked:
                    sc = jnp.where(_causal_t(qi, ki), sc, -1e30)
                m_new = jnp.maximum(m, jnp.max(sc, axis=0, keepdims=True))
                alpha = jnp.exp(m - m_new)
                p = jnp.exp(sc - m_new)
                l = alpha * l + jnp.sum(p, axis=0, keepdims=True)
                p_hi = p.astype(BF16)
                p_lo = (p - p_hi.astype(F32)).astype(BF16)
                acc = alpha * acc + (_dot(vt, p_hi, _NN) + _dot(vt, p_lo, _NN))
                out.append((m_new, l, acc))
            return tuple(out)

        init = tuple((jnp.full((1, ATT_TQ), -1e30, F32), jnp.zeros((1, ATT_TQ), F32),
                      jnp.zeros((LANES, ATT_TQ), F32)) for _ in range(2))
        carry = lax.fori_loop(0, qi, lambda ki, c: step(ki, c, False), init)
        (ma, la, acca), (mb, lb, accb) = step(qi, carry, True)
        ot = jnp.where(rows, acca / la, accb / lb)
        ot_ref[...] = ot
        o_ref[...] = ot.T.astype(o_ref.dtype)
        l_ref[0:1, :] = ma + jnp.log(la)
        l_ref[1:2, :] = mb + jnp.log(lb)

    row = pl.BlockSpec((None, 2, ATT_TQ), lambda j, i: (j, 0, i))
    return pl.pallas_call(
        body, name=name, grid=(npair, nq),
        in_specs=[pl.BlockSpec((ATT_TQ, LANES), lambda j, i: (i, j)),
                  pl.BlockSpec((s, LANES), lambda j, i: (0, npair + j)),
                  pl.BlockSpec((s, LANES), lambda j, i: (0, 2 * npair + j)),
                  row, pl.BlockSpec((None, 2, s, LANES), lambda j, i: (j, 0, 0, 0))],
        out_specs=[pl.BlockSpec((ATT_TQ, LANES), lambda j, i: (i, j)),
                   pl.BlockSpec((LANES, ATT_TQ), lambda j, i: (j, i)), row],
        out_shape=[jax.ShapeDtypeStruct((s, FOX_WIDTH), BF16), jax.ShapeDtypeStruct((FOX_WIDTH, s), F32),
                   jax.ShapeDtypeStruct((npair, 2, s), F32)],
        scratch_shapes=[pltpu.VMEM((LANES, s), BF16)],
        compiler_params=_params(),
    )(qkv, qkv, qkv, cum_r, cum_b)


def _attn_delta_t(do_t, o_t, *, name):
    s = o_t.shape[1]
    ts = _tile(s, 512)

    def body(do_ref, o_ref, d_ref):
        prod = do_ref[...].astype(F32) * o_ref[...]
        d_ref[0:1, :] = jnp.sum(prod[:FOX_HEAD_DIM], axis=0, keepdims=True)
        d_ref[1:2, :] = jnp.sum(prod[FOX_HEAD_DIM:], axis=0, keepdims=True)

    blk = pl.BlockSpec((LANES, ts), lambda j, i: (j, i))
    return pl.pallas_call(
        body, name=name, grid=(HEAD_PAIRS, s // ts), in_specs=[blk, blk],
        out_specs=pl.BlockSpec((None, 2, ts), lambda j, i: (j, 0, i)),
        out_shape=jax.ShapeDtypeStruct((HEAD_PAIRS, 2, s), F32), compiler_params=_params(),
    )(do_t, o_t)


def _attn_bwd_t(qkv, do, o_t, lse, cum_b, cum_r, *, name):
    s = qkv.shape[0]
    nq = s // ATT_TQ
    npair = HEAD_PAIRS

    def body(q_ref, k_ref, v_ref, do_ref, ot_ref, l_ref, cq_ref, ck_ref, dq_ref, dk_ref, dv_ref, dc_ref,
             qt_ref, dot_ref, dqt_ref, dl_ref):
        ki = pl.program_id(1)
        m0 = _head_mask()
        rows = _row_mask()
        k2 = k_ref[...]
        v2 = v_ref[...]
        kt = _transpose_bf16(k2)
        ks = k2 * ATT_SCALE
        kz, vz = jnp.zeros_like(k2), jnp.zeros_like(v2)
        khs = (jnp.where(m0, ks, kz), jnp.where(m0, kz, ks))
        vhs = (jnp.where(m0, v2, vz), jnp.where(m0, vz, v2))
        cks = tuple(_lane_tile(ck_ref[hh], ATT_TQ) for hh in range(2))

        @pl.when(ki == 0)
        def _():
            dqt_ref[...] = jnp.zeros_like(dqt_ref)
            qt_ref[...] = _transpose_bf16(q_ref[...])
            do_t = do_ref[...].astype(F32).T
            dot_ref[...] = do_t.astype(BF16)
            prod = do_t * ot_ref[...]
            dl_ref[0:1, :] = jnp.sum(prod[:FOX_HEAD_DIM], axis=0, keepdims=True)
            dl_ref[1:2, :] = jnp.sum(prod[FOX_HEAD_DIM:], axis=0, keepdims=True)

        def step(qi, carry, masked):
            off = pl.multiple_of(qi * ATT_TQ, ATT_TQ)
            q2 = q_ref[pl.ds(off, ATT_TQ), :]
            do2 = do_ref[pl.ds(off, ATT_TQ), :]
            qt = qt_ref[:, pl.ds(off, ATT_TQ)]
            dot_ = dot_ref[:, pl.ds(off, ATT_TQ)]
            out, dqs = [], []
            for hh in range(2):
                dk_acc, dv_acc, dc_acc = carry[hh]
                sc = _dot(khs[hh], qt, _NN) + (cq_ref[hh:hh + 1, pl.ds(off, ATT_TQ)] - cks[hh])
                p = jnp.exp(sc - l_ref[hh:hh + 1, pl.ds(off, ATT_TQ)])
                if masked:
                    p = jnp.where(_causal_t(qi, ki), p, 0.0)
                dp = _dot(vhs[hh], dot_, _NN)
                ds = p * (dp - dl_ref[hh:hh + 1, pl.ds(off, ATT_TQ)])
                dc_acc = dc_acc - jnp.sum(ds, axis=1, keepdims=True)
                dss = (ds * ATT_SCALE).astype(BF16)
                dv_acc = dv_acc + _dot(p, do2, _NN)
                dk_acc = dk_acc + _dot(dss, q2, _NN)
                dqs.append(_dot(kt, dss, _NN))
                out.append((dk_acc, dv_acc, dc_acc))
            dqt_ref[:, pl.ds(off, ATT_TQ)] += jnp.where(rows, dqs[0], dqs[1])
            return tuple(out)

        init = tuple((jnp.zeros((ATT_TK, LANES), F32), jnp.zeros((ATT_TK, LANES), F32),
                      jnp.zeros((ATT_TK, 1), F32)) for _ in range(2))
        carry = step(ki, init, True)
        (dka, dva, dca), (dkb, dvb, dcb) = lax.fori_loop(ki + 1, nq, lambda qi, c: step(qi, c, False), carry)
        dk_ref[...] = jnp.where(m0, dka, dkb).astype(dk_ref.dtype)
        dv_ref[...] = jnp.where(m0, dva, dvb).astype(dv_ref.dtype)
        dc_ref[0] = jnp.broadcast_to(dca, (ATT_TK, LANES))
        dc_ref[1] = jnp.broadcast_to(dcb, (ATT_TK, LANES))

        @pl.when(ki == nq - 1)
        def _():
            dq_ref[...] = dqt_ref[...].T.astype(dq_ref.dtype)

    colfull = lambda base: pl.BlockSpec((s, LANES), lambda j, i: (0, base + j))
    colblk = lambda base: pl.BlockSpec((ATT_TK, LANES), lambda j, i: (i, base + j))
    stat = pl.BlockSpec((None, 2, s), lambda j, i: (j, 0, 0))
    bcast = pl.BlockSpec((None, 2, ATT_TK, LANES), lambda j, i: (j, 0, i, 0))
    grad = jax.ShapeDtypeStruct((s, FOX_WIDTH), BF16)
    return pl.pallas_call(
        body, name=name, grid=(npair, nq),
        in_specs=[colfull(0), colblk(npair), colblk(2 * npair), colfull(0),
                  pl.BlockSpec((LANES, s), lambda j, i: (j, 0)), stat, stat, bcast],
        out_specs=[colfull(0), colblk(0), colblk(0), bcast],
        out_shape=[grad, grad, grad, jax.ShapeDtypeStruct((npair, 2, s, LANES), F32)],
        scratch_shapes=[pltpu.VMEM((LANES, s), BF16), pltpu.VMEM((LANES, s), BF16), pltpu.VMEM((LANES, s), F32),
                        pltpu.VMEM((2, s), F32)],
        compiler_params=_params(),
    )(qkv, qkv, qkv, do, o_t, lse, cum_r, cum_b)


def _merge_fwd(zg, ya, yb, *, name, tm=256):
    s, d = ya.shape
    tm = _tile(s, tm)

    def body(zg_ref, ya_ref, yb_ref, m_ref):
        ga = _sigmoid(zg_ref[:, :d].astype(F32))
        gb = _sigmoid(zg_ref[:, d:].astype(F32))
        m_ref[...] = (ga * ya_ref[...].astype(F32) + gb * yb_ref[...].astype(F32)).astype(m_ref.dtype)

    row = pl.BlockSpec((tm, d), lambda i: (i, 0))
    row2 = pl.BlockSpec((tm, 2 * d), lambda i: (i, 0))
    return pl.pallas_call(
        body, name=name, grid=(s // tm,), in_specs=[row2, row, row], out_specs=row,
        out_shape=jax.ShapeDtypeStruct((s, d), BF16), compiler_params=_params(),
    )(zg, ya, yb)


def _merge_bwd(dm, zg, ya, yb, *, name, tm=256):
    s, d = ya.shape
    tm = _tile(s, tm)

    def body(dm_ref, zg_ref, ya_ref, yb_ref, dzg_ref, dya_ref, dyb_ref):
        dmv = dm_ref[...].astype(F32)
        ga = _sigmoid(zg_ref[:, :d].astype(F32))
        gb = _sigmoid(zg_ref[:, d:].astype(F32))
        dzg_ref[:, :d] = (dmv * ya_ref[...].astype(F32) * ga * (1.0 - ga)).astype(dzg_ref.dtype)
        dzg_ref[:, d:] = (dmv * yb_ref[...].astype(F32) * gb * (1.0 - gb)).astype(dzg_ref.dtype)
        dya_ref[...] = (dmv * ga).astype(dya_ref.dtype)
        dyb_ref[...] = (dmv * gb).astype(dyb_ref.dtype)

    row = pl.BlockSpec((tm, d), lambda i: (i, 0))
    row2 = pl.BlockSpec((tm, 2 * d), lambda i: (i, 0))
    return pl.pallas_call(
        body, name=name, grid=(s // tm,), in_specs=[row, row2, row, row], out_specs=[row2, row, row],
        out_shape=[jax.ShapeDtypeStruct((s, 2 * d), BF16), jax.ShapeDtypeStruct((s, d), BF16),
                   jax.ShapeDtypeStruct((s, d), BF16)],
        compiler_params=_params(),
    )(dm, zg, ya, yb)


def _shift_down(u, k, row):
    return jnp.where(row >= k, pltpu.roll(u, k, 0), 0.0)


def _shift_up(u, k, row):
    n = u.shape[0]
    return jnp.where(row < n - k, pltpu.roll(u, n - k, 0), 0.0)


def _conv_act_fwd(up_a, up_b, cw_a, cw_b, cb_a, cb_b, *, name, tc=128):
    s, f = up_a.shape
    tc = _tile(f, tc)

    def body(ua_ref, ub_ref, wa_ref, wb_ref, ba_ref, bb_ref, act_ref):
        row = lax.broadcasted_iota(jnp.int32, (s, tc), 0)

        def conv(u_ref, w_ref, b_ref):
            u = u_ref[...].astype(F32)
            return (b_ref[...] + w_ref[0:1, :] * _shift_down(u, 2, row)
                    + w_ref[1:2, :] * _shift_down(u, 1, row) + w_ref[2:3, :] * u)

        ca = conv(ua_ref, wa_ref, ba_ref)
        cb = conv(ub_ref, wb_ref, bb_ref)
        act_ref[...] = (_gelu(ca) * cb).astype(act_ref.dtype)

    col = pl.BlockSpec((s, tc), lambda j: (0, j))
    w3 = pl.BlockSpec((3, tc), lambda j: (0, j))
    b1 = pl.BlockSpec((1, tc), lambda j: (0, j))
    return pl.pallas_call(
        body, name=name, grid=(f // tc,), in_specs=[col, col, w3, w3, b1, b1], out_specs=col,
        out_shape=jax.ShapeDtypeStruct((s, f), BF16), compiler_params=_params(),
    )(up_a, up_b, cw_a, cw_b, cb_a, cb_b)


def _conv_act_bwd(up_a, up_b, dact, cw_a, cw_b, cb_a, cb_b, *, name, tc=128):
    s, f = up_a.shape
    tc = _tile(f, tc)

    def body(ua_ref, ub_ref, da_ref, wa_ref, wb_ref, ba_ref, bb_ref, dua_ref, dub_ref, dwa_ref, dwb_ref):
        row = lax.broadcasted_iota(jnp.int32, (s, tc), 0)

        def conv(u_ref, w_ref, b_ref):
            u = u_ref[...].astype(F32)
            u1 = _shift_down(u, 1, row)
            u2 = _shift_down(u, 2, row)
            return u, u1, u2, b_ref[...] + w_ref[0:1, :] * u2 + w_ref[1:2, :] * u1 + w_ref[2:3, :] * u

        def back(dc, taps, w_ref, du_ref, dw_ref):
            u, u1, u2 = taps
            dw_ref[0:1, :] = jnp.sum(dc * u2, axis=0, keepdims=True)
            dw_ref[1:2, :] = jnp.sum(dc * u1, axis=0, keepdims=True)
            dw_ref[2:3, :] = jnp.sum(dc * u, axis=0, keepdims=True)
            dw_ref[3:4, :] = jnp.sum(dc, axis=0, keepdims=True)
            du = (w_ref[2:3, :] * dc + w_ref[1:2, :] * _shift_up(dc, 1, row)
                  + w_ref[0:1, :] * _shift_up(dc, 2, row))
            du_ref[...] = du.astype(du_ref.dtype)

        ua, ua1, ua2, ca = conv(ua_ref, wa_ref, ba_ref)
        ub, ub1, ub2, cb = conv(ub_ref, wb_ref, bb_ref)
        g, dg = _gelu_and_grad(ca)
        dact_v = da_ref[...].astype(F32)
        back(dact_v * cb * dg, (ua, ua1, ua2), wa_ref, dua_ref, dwa_ref)
        back(dact_v * g, (ub, ub1, ub2), wb_ref, dub_ref, dwb_ref)

    col = pl.BlockSpec((s, tc), lambda j: (0, j))
    w3 = pl.BlockSpec((3, tc), lambda j: (0, j))
    w4 = pl.BlockSpec((4, tc), lambda j: (0, j))
    b1 = pl.BlockSpec((1, tc), lambda j: (0, j))
    return pl.pallas_call(
        body, name=name, grid=(f // tc,), in_specs=[col, col, col, w3, w3, b1, b1],
        out_specs=[col, col, w4, w4],
        out_shape=[jax.ShapeDtypeStruct((s, f), BF16), jax.ShapeDtypeStruct((s, f), BF16),
                   jax.ShapeDtypeStruct((4, f), F32), jax.ShapeDtypeStruct((4, f), F32)],
        compiler_params=_params(),
    )(up_a, up_b, dact, cw_a, cw_b, cb_a, cb_b)


def _ple_final(x2, ple, zp, target, g_final, *, name, tm=256):
    s, d = x2.shape
    tm = _tile(s, tm)

    def body(x_ref, ple_ref, zp_ref, t_ref, g_ref, dx_ref, dple_ref, dzp_ref, dg_ref, loss_ref):
        @pl.when(pl.program_id(0) == 0)
        def _():
            dg_ref[...] = jnp.zeros_like(dg_ref)
            loss_ref[...] = jnp.zeros_like(loss_ref)

        gp = _sigmoid(zp_ref[...].astype(F32))
        plev = ple_ref[...].astype(F32)
        x3 = x_ref[...] + plev * gp
        r = lax.rsqrt(jnp.mean(x3 * x3, axis=-1, keepdims=True) + EPS)
        xhat = x3 * r
        gv = g_ref[...]
        diff = xhat * gv - t_ref[...]
        loss_ref[...] += 0.5 * jnp.sum(jnp.mean(diff * diff, axis=-1, keepdims=True), axis=0, keepdims=True)
        dy = diff * (1.0 / d)
        dg_ref[...] += jnp.sum(dy * xhat, axis=0, keepdims=True)
        dyg = dy * gv
        dx3 = r * (dyg - xhat * jnp.mean(dyg * xhat, axis=-1, keepdims=True))
        dx_ref[...] = dx3
        dple_ref[...] = (dx3 * gp).astype(dple_ref.dtype)
        dzp_ref[...] = (dx3 * plev * gp * (1.0 - gp)).astype(dzp_ref.dtype)

    row = pl.BlockSpec((tm, d), lambda i: (i, 0))
    vec = pl.BlockSpec((1, d), lambda i: (0, 0))
    return pl.pallas_call(
        body, name=name, grid=(s // tm,), in_specs=[row, row, row, row, vec],
        out_specs=[row, row, row, vec, pl.BlockSpec((1, LANES), lambda i: (0, 0))],
        out_shape=[jax.ShapeDtypeStruct((s, d), F32), jax.ShapeDtypeStruct((s, d), BF16),
                   jax.ShapeDtypeStruct((s, d), BF16), jax.ShapeDtypeStruct((1, d), F32),
                   jax.ShapeDtypeStruct((1, LANES), F32)],
        compiler_params=_params(),
    )(x2, ple, zp, target, g_final)


def _device_step(x, p, target, w, get_w_in=None, get_w_rest=None, on_grads_ffn=None, on_grads_mix=None):
    s = x.shape[0]
    g = {}
    w = dict(w)

    h = _rms_fwd(x, w["norm_mix_g"], name="rms_mix", dep=w.get("first_dep"))
    if get_w_in is not None:
        w.update(get_w_in(h))
    z_uv = _mm(h, w["w_uv"], mode="nn", out_dtype=BF16, name="proj_uv", tm=1024, dep=w.get("proj_dep"))
    qkv = _mm(h, w["w_qkv"], mode="nn", out_dtype=BF16, name="proj_qkv", tm=1024)
    zg = _mm(h, w["w_g"], mode="nn", out_dtype=BF16, name="proj_gate", tm=1024)
    f = _mm(h, w["w_f"], mode="nn", out_dtype=F32, name="proj_f", tm=1024)

    a = _gmlp_fwd(z_uv, w["gmlp_ln_g"], w["gmlp_ln_b"], w["gmlp_w_s"], w["gmlp_b_s_t"], name="gmlp_fwd")

    cum_b, cum_t = _fox_cum(f, w["b_f"], name="fox_cum")
    cum_b = cum_b.reshape(HEAD_PAIRS, 2, s, LANES)
    cum_r = cum_t[:FOX_HEADS].reshape(HEAD_PAIRS, 2, s)
    b, o_t, lse = _attn_fwd_t(qkv, cum_b, cum_r, name="attn_fwd")
    if get_w_rest is not None:
        w.update(get_w_rest(b))

    ya = _mm(a, w["w_branch_a"], mode="nn", out_dtype=BF16, name="branch_a", tm=1024)
    yb = _mm(b, w["w_branch_b"], mode="nn", out_dtype=BF16, name="branch_b", tm=1024)
    merged = _merge_fwd(zg, ya, yb, name="merge_fwd")
    x1 = _mm(merged, w["w_out"], mode="nn", out_dtype=F32, name="proj_out", add=x, tm=1024)

    h2 = _rms_fwd(x1, w["norm_ffn_g"], name="rms_ffn")
    up_a = _mm(h2, w["w_up_a"], mode="nn", out_dtype=BF16, name="up_a", tm=1024, tn=D_FF // 2)
    up_b = _mm(h2, w["w_up_b"], mode="nn", out_dtype=BF16, name="up_b", tm=1024, tn=D_FF // 2)
    cw, cb = w["conv_w"], w["conv_b"]
    conv_args = (cw[:, :D_FF], cw[:, D_FF:], cb[:, :D_FF], cb[:, D_FF:])
    act = _conv_act_fwd(up_a, up_b, *conv_args, name="conv_act_fwd")
    x2 = _mm(act, w["w_down"], mode="nn", out_dtype=F32, name="down", add=x1, tm=512)

    h3 = _rms_fwd(x2, w["norm_ple_g"], name="rms_ple")
    ple = _mm(p, w["w_ple"], mode="nn", out_dtype=BF16, name="ple_proj", tm=1024)
    zp = _mm(h3, w["w_ple_gate"], mode="nn", out_dtype=BF16, name="ple_gate", tm=1024)
    dx3, dple, dzp, g["norm_final_g"], loss = _ple_final(x2, ple, zp, target, w["norm_final_g"], name="ple_final")

    g["w_ple"] = _mm(p, dple, mode="tn", out_dtype=BF16, name="dw_ple")
    g["w_ple_gate"] = _mm(h3, dzp, mode="tn", out_dtype=BF16, name="dw_ple_gate")
    dh3 = _mm(dzp, w["w_ple_gate"], mode="nt", out_dtype=BF16, name="dh3")
    dx2, dx2_b, g["norm_ple_g"] = _rms_bwd(x2, w["norm_ple_g"], dh3, dx3, name="rms_ple_bwd")

    g["w_down"] = _mm(act, dx2_b, mode="tn", out_dtype=BF16, name="dw_down", tm=D_FF // 2)
    dact = _mm(dx2_b, w["w_down"], mode="nt", out_dtype=BF16, name="dact", tn=D_FF // 2)
    dup_a, dup_b, dcw_a, dcw_b = _conv_act_bwd(up_a, up_b, dact, *conv_args, name="conv_act_bwd")
    g["conv_w"] = jnp.concatenate([dcw_a[:3], dcw_b[:3]], axis=1)
    g["conv_b"] = jnp.concatenate([dcw_a[3:], dcw_b[3:]], axis=1)
    g["w_up_a"] = _mm(h2, dup_a, mode="tn", out_dtype=BF16, name="dw_up_a", tn=D_FF // 2)
    g["w_up_b"] = _mm(h2, dup_b, mode="tn", out_dtype=BF16, name="dw_up_b", tn=D_FF // 2)
    dh2 = _mm_nt_sum([(dup_a, w["w_up_a"]), (dup_b, w["w_up_b"])], out_dtype=BF16, name="dh2")
    dx1, dx1_b, g["norm_ffn_g"] = _rms_bwd(x1, w["norm_ffn_g"], dh2, dx2, name="rms_ffn_bwd")
    dep = on_grads_ffn(g) if on_grads_ffn is not None else None

    g["w_out"] = _mm(merged, dx1_b, mode="tn", out_dtype=BF16, name="dw_out")
    dmerged = _mm(dx1_b, w["w_out"], mode="nt", out_dtype=BF16, name="dmerged", dep=dep)
    dzg, dya, dyb = _merge_bwd(dmerged, zg, ya, yb, name="merge_bwd")
    g["w_branch_a"] = _mm(a, dya, mode="tn", out_dtype=BF16, name="dw_branch_a")
    g["w_branch_b"] = _mm(b, dyb, mode="tn", out_dtype=BF16, name="dw_branch_b")
    da = _mm(dya, w["w_branch_a"], mode="nt", out_dtype=BF16, name="da")
    db = _mm(dyb, w["w_branch_b"], mode="nt", out_dtype=BF16, name="db")

    dz_uv, g["gmlp_w_s"], dbs_t, g["gmlp_ln_g"], g["gmlp_ln_b"] = _gmlp_bwd(
        z_uv, da, w["gmlp_ln_g"], w["gmlp_ln_b"], w["gmlp_w_s"], w["gmlp_b_s_t"], name="gmlp_bwd")
    g["gmlp_b_s"] = dbs_t[:, :GMLP_GROUPS].T

    dq, dk, dv, dcum_b = _attn_bwd_t(qkv, db, o_t, lse, cum_b, cum_r, name="attn_bwd")
    dcum_t = jnp.pad(dcum_b[..., 0].reshape(FOX_HEADS, s), ((0, LANES - FOX_HEADS), (0, 0)))
    df, g["b_f"] = _fox_dlogit(dcum_t, f, w["b_f"], name="fox_dlogit")
    dqkv = jnp.concatenate([dq, dk, dv], axis=1)

    g["w_uv"] = _mm(h, dz_uv, mode="tn", out_dtype=BF16, name="dw_uv")
    g["w_qkv"] = _mm(h, dqkv, mode="tn", out_dtype=BF16, name="dw_qkv")
    g["w_f"] = _mm(h, df, mode="tn", out_dtype=BF16, name="dw_f")
    g["w_g"] = _mm(h, dzg, mode="tn", out_dtype=BF16, name="dw_g")
    dep = on_grads_mix(g) if on_grads_mix is not None else None
    dh = _mm_nt_sum([(dz_uv, w["w_uv"]), (dqkv, w["w_qkv"]), (df, w["w_f"]), (dzg, w["w_g"])],
                    out_dtype=BF16, name="dh", dep=dep)
    dx0, _, g["norm_mix_g"] = _rms_bwd(x, w["norm_mix_g"], dh, dx1, name="rms_mix_bwd")
    return loss, dx0, g


def _coords():
    return lax.axis_index("x"), lax.axis_index("y"), lax.axis_index("c")


def _other_chips(x, y):
    return [(1 - x, y), (x, 1 - y), (1 - x, 1 - y)]


def _remote(src, dst, send_sem, recv_sem, dev):
    return pltpu.make_async_remote_copy(src_ref=src, dst_ref=dst, send_sem=send_sem, recv_sem=recv_sem,
                                        device_id=dev, device_id_type=MESH)


_ANY = pl.BlockSpec(memory_space=pl.ANY)


def _gather_weights(halved, whole, *, name):
    nh, n = len(halved), len(halved) + len(whole)
    arrays = list(halved) + list(whole)

    def body(*refs):
        ins, outs = refs[:n], refs[n:2 * n]
        send_sems, recv_sems = refs[2 * n:]
        x, y, c = _coords()
        me, sib = 2 * x + y, (x, y, 1 - c)
        chips = _other_chips(x, y)

        def half(i, which):
            h = ins[i].shape[0] // 2
            return pl.ds(pl.multiple_of(which * h, 16), h)

        sends = []
        for i in range(n):
            src, dst = (ins[i].at[half(i, c)], outs[i].at[me, half(i, c)]) if i < nh else (ins[i], outs[i].at[me])
            for k, (cx, cy) in enumerate(chips):
                cp = _remote(src, dst, send_sems.at[i, k], recv_sems.at[i, k], (cx, cy, c))
                cp.start()
                sends.append(cp)
        for i in range(n):
            for k, (cx, cy) in enumerate(chips):
                got = outs[i].at[2 * cx + cy, half(i, c)] if i < nh else outs[i].at[2 * cx + cy]
                _remote(got, got, send_sems.at[i, k], recv_sems.at[i, k], sib).wait_recv()
                if i < nh:
                    cp = _remote(got, got, send_sems.at[i, 3 + k], recv_sems.at[i, 3 + k], sib)
                    cp.start()
                    sends.append(cp)
        for i in range(nh):
            for k, (cx, cy) in enumerate(chips):
                got = outs[i].at[2 * cx + cy, half(i, 1 - c)]
                _remote(got, got, send_sems.at[i, 3 + k], recv_sems.at[i, 3 + k], sib).wait_recv()
        for cp in sends:
            cp.wait_send()

    outs = pl.pallas_call(
        body, name=name, in_specs=[_ANY] * n, out_specs=[_ANY] * n,
        out_shape=[jax.ShapeDtypeStruct((N_CHIPS,) + a.shape, a.dtype) for a in arrays],
        scratch_shapes=[pltpu.SemaphoreType.DMA((n, 6)), pltpu.SemaphoreType.DMA((n, 6))],
        compiler_params=_params(),
    )(*arrays)
    chip = 2 * lax.axis_index("x") + lax.axis_index("y")
    return [lax.dynamic_update_index_in_dim(o, a, chip, 0) for o, a in zip(outs, arrays)]


def _pair_exchange(gs, *, name):
    n = len(gs)

    def body(*refs):
        ins, outs = refs[:n], refs[n:2 * n]
        send_sems, recv_sems = refs[2 * n:]
        x, y, c = _coords()
        copies = []
        for i in range(n):
            for j in range(N_CHIPS):
                cp = _remote(ins[i].at[j, 1 - c], outs[i].at[j], send_sems.at[i, j], recv_sems.at[i, j], (x, y, 1 - c))
                cp.start()
                copies.append(cp)
        for cp in copies:
            cp.wait()

    return pl.pallas_call(
        body, name=name, in_specs=[_ANY] * n, out_specs=[_ANY] * n,
        out_shape=[jax.ShapeDtypeStruct((N_CHIPS,) + a.shape[2:], a.dtype) for a in gs],
        scratch_shapes=[pltpu.SemaphoreType.DMA((n, N_CHIPS)), pltpu.SemaphoreType.DMA((n, N_CHIPS))],
        compiler_params=_params(),
    )(*gs)


def _chip_exchange(ss, *, name):
    n = len(ss)

    def body(*refs):
        ins, outs = refs[:n], refs[n:2 * n]
        send_sems, recv_sems = refs[2 * n:]
        x, y, c = _coords()
        me = 2 * x + y
        chips = _other_chips(x, y)
        sends = []
        for i in range(n):
            for k, (cx, cy) in enumerate(chips):
                cp = _remote(ins[i].at[2 * cx + cy], outs[i].at[me], send_sems.at[i, k], recv_sems.at[i, k], (cx, cy, c))
                cp.start()
                sends.append(cp)
        for i in range(n):
            for k, (cx, cy) in enumerate(chips):
                got = outs[i].at[2 * cx + cy]
                _remote(got, got, send_sems.at[i, k], recv_sems.at[i, k], (cx, cy, c)).wait_recv()
        for cp in sends:
            cp.wait_send()

    return pl.pallas_call(
        body, name=name, in_specs=[_ANY] * n, out_specs=[_ANY] * n,
        out_shape=[jax.ShapeDtypeStruct(a.shape, a.dtype) for a in ss],
        scratch_shapes=[pltpu.SemaphoreType.DMA((n, 3)), pltpu.SemaphoreType.DMA((n, 3))],
        compiler_params=_params(),
    )(*ss)


def _pair_share(hs, *, name):
    n = len(hs)

    def body(*refs):
        ins, outs = refs[:n], refs[n:2 * n]
        send_sems, recv_sems = refs[2 * n:]
        x, y, c = _coords()
        copies = []
        for i in range(n):
            cp = _remote(ins[i], outs[i], send_sems.at[i], recv_sems.at[i], (x, y, 1 - c))
            cp.start()
            copies.append(cp)
        for cp in copies:
            cp.wait()

    return pl.pallas_call(
        body, name=name, in_specs=[_ANY] * n, out_specs=[_ANY] * n,
        out_shape=[jax.ShapeDtypeStruct(a.shape, a.dtype) for a in hs],
        scratch_shapes=[pltpu.SemaphoreType.DMA((n,)), pltpu.SemaphoreType.DMA((n,))],
        compiler_params=_params(),
    )(*hs)


def _all_exchange(vec, *, name):
    def body(v_ref, o_ref, send_sems, recv_sems, local_sem):
        x, y, c = _coords()
        me = 4 * x + 2 * y + c
        local = pltpu.make_async_copy(v_ref, o_ref.at[me], local_sem)
        local.start()
        copies = []
        k = 0
        for dx in (0, 1):
            for dy in (0, 1):
                for dc in (0, 1):
                    if dx or dy or dc:
                        peer = (1 - x if dx else x, 1 - y if dy else y, 1 - c if dc else c)
                        cp = _remote(v_ref, o_ref.at[me], send_sems.at[k], recv_sems.at[k], peer)
                        cp.start()
                        copies.append(cp)
                        k += 1
        for cp in copies:
            cp.wait()
        local.wait()

    return pl.pallas_call(
        body, name=name, in_specs=[_ANY], out_specs=_ANY,
        out_shape=jax.ShapeDtypeStruct((8,) + vec.shape, vec.dtype),
        scratch_shapes=[pltpu.SemaphoreType.DMA((7,)), pltpu.SemaphoreType.DMA((7,)), pltpu.SemaphoreType.DMA(())],
        compiler_params=_params(),
    )(vec)


_HBM = pl.BlockSpec(memory_space=pltpu.HBM)
_SEM = pl.BlockSpec(memory_space=pltpu.SEMAPHORE)
_EFFECT = pltpu.SideEffectType.DATAFLOW_SIDE_EFFECTING


def _copies_start(srcs, lands, plan, n_copies, *, name, after=()):
    ns, n = len(srcs), len(srcs) + len(lands)
    na = len(after)

    def body(*refs):
        send_sems, recv_sems = refs[n + na], refs[n + na + 1]
        token = refs[-1]
        for k, (src, dst, dev) in enumerate(plan(refs[:ns], refs[ns:n])):
            _remote(src, dst, send_sems.at[k], recv_sems.at[k], dev).start()
        token[...] = jnp.zeros_like(token)

    arrays = list(srcs) + list(lands)
    outs = pl.pallas_call(
        body, name=name,
        out_shape=(pltpu.SemaphoreType.DMA((n_copies,)), pltpu.SemaphoreType.DMA((n_copies,)),
                   *[pltpu.HBM(a.shape, a.dtype) for a in arrays], jax.ShapeDtypeStruct((8, LANES), F32)),
        in_specs=[_HBM] * n + [_ANY] * na,
        out_specs=(_SEM, _SEM, *[_HBM] * n, pl.BlockSpec(memory_space=pltpu.VMEM)),
        input_output_aliases={i: 2 + i for i in range(n)},
        compiler_params=pltpu.CompilerParams(has_side_effects=_EFFECT),
    )(*[pltpu.with_memory_space_constraint(a, pltpu.HBM) for a in arrays], *after)
    return outs[0], outs[1], list(outs[2:2 + ns]), list(outs[2 + ns:2 + n]), outs[-1]


def _copies_wait(send_sems, recv_sems, srcs, lands, plan, first, after, *, name):
    ns, n = len(srcs), len(srcs) + len(lands)

    def body(*refs):
        send, recv = refs[n], refs[n + 1]
        for k, (src, dst, dev) in enumerate(plan(refs[:ns], refs[ns:n])):
            cp = _remote(src, dst, send.at[first + k], recv.at[first + k], dev)
            cp.wait_send()
            cp.wait_recv()

    arrays = list(srcs) + list(lands)
    outs = pl.pallas_call(
        body, name=name, out_shape=tuple(pltpu.HBM(a.shape, a.dtype) for a in arrays),
        in_specs=[_HBM] * n + [_SEM, _SEM] + [_ANY] * len(after), out_specs=tuple([_HBM] * n),
        input_output_aliases={i: i for i in range(n)},
        compiler_params=pltpu.CompilerParams(has_side_effects=_EFFECT),
    )(*arrays, send_sems, recv_sems, *after)
    return list(outs[:ns]), list(outs[ns:])


def _gather_plan(halved):
    def plan(srcs, lands):
        x, y, c = _coords()
        me = 2 * x + y
        out = []
        for i, (src, land) in enumerate(zip(srcs, lands)):
            if halved[i]:
                h = src.shape[0] // 2
                rows = pl.ds(pl.multiple_of(c * h, 16), h)
                src, dst = src.at[rows], land.at[me, rows]
            else:
                dst = land.at[me]
            out += [(src, dst, (cx, cy, c)) for cx, cy in _other_chips(x, y)]
        return out
    return plan


def _forward_halves(lands, *, name):
    n = len(lands)

    def body(*refs):
        ins, outs = refs[:n], refs[n:2 * n]
        send_sems, recv_sems = refs[2 * n:]
        x, y, c = _coords()
        copies = []
        for i in range(n):
            h = ins[i].shape[1] // 2
            rows = pl.ds(pl.multiple_of(c * h, 16), h)
            for k, (cx, cy) in enumerate(_other_chips(x, y)):
                cp = _remote(ins[i].at[2 * cx + cy, rows], outs[i].at[2 * cx + cy, rows],
                             send_sems.at[i, k], recv_sems.at[i, k], (x, y, 1 - c))
                cp.start()
                copies.append(cp)
        for cp in copies:
            cp.wait()

    return pl.pallas_call(
        body, name=name, in_specs=[_ANY] * n, out_specs=[_ANY] * n,
        out_shape=[jax.ShapeDtypeStruct(a.shape, a.dtype) for a in lands],
        input_output_aliases={i: i for i in range(n)},
        scratch_shapes=[pltpu.SemaphoreType.DMA((n, 3)), pltpu.SemaphoreType.DMA((n, 3))],
        compiler_params=_params(),
    )(*lands)


def _all_plan(srcs, lands):
    x, y, c = _coords()
    me = 4 * x + 2 * y + c
    out = []
    for src, land in zip(srcs, lands):
        for dx in (0, 1):
            for dy in (0, 1):
                for dc in (0, 1):
                    if dx or dy or dc:
                        out.append((src, land.at[me], (1 - x if dx else x, 1 - y if dy else y, 1 - c if dc else c)))
    return out


def _chip_plan(srcs, lands):
    x, y, c = _coords()
    me = 2 * x + y
    out = []
    for src, land in zip(srcs, lands):
        out += [(src.at[2 * cx + cy], land.at[me], (cx, cy, c)) for cx, cy in _other_chips(x, y)]
    return out


ROW_BLOCK_BYTES = 2 * 1024 * 1024


def _rtile(r, pref, mult, row_bytes=None):
    if row_bytes is not None:
        pref = max(pref, ROW_BLOCK_BYTES // row_bytes)
    t = (min(r, pref) // mult) * mult
    while t >= mult:
        if r % t == 0:
            return t
        t -= mult
    return r


def _pair_add(g, recv, core, *, name):
    _, _, r2, cols = g.shape
    tr = _rtile(r2, 256, 16, row_bytes=2 * cols)

    def body(c_ref, g_ref, r_ref, o_ref):
        o_ref[...] = (g_ref[...].astype(F32) + r_ref[...].astype(F32)).astype(o_ref.dtype)

    blk = pl.BlockSpec((None, tr, cols), lambda j, i, c_ref: (j, i, 0))
    return pl.pallas_call(
        body, name=name,
        grid_spec=pltpu.PrefetchScalarGridSpec(
            num_scalar_prefetch=1, grid=(N_CHIPS, r2 // tr),
            in_specs=[pl.BlockSpec((None, None, tr, cols), lambda j, i, c_ref: (j, c_ref[0], i, 0)), blk],
            out_specs=blk),
        out_shape=jax.ShapeDtypeStruct(recv.shape, recv.dtype), compiler_params=_params(),
    )(core, g, recv)


def _sum_slots(a, out_dtype, *, name):
    n, r, cols = a.shape
    tr = _rtile(r, 256, 16)

    def body(a_ref, o_ref):
        acc = a_ref[0].astype(F32)
        for j in range(1, n):
            acc = acc + a_ref[j].astype(F32)
        o_ref[...] = acc.astype(o_ref.dtype)

    return pl.pallas_call(
        body, name=name, grid=(r // tr,),
        in_specs=[pl.BlockSpec((n, tr, cols), lambda i: (0, i, 0))],
        out_specs=pl.BlockSpec((tr, cols), lambda i: (i, 0)),
        out_shape=jax.ShapeDtypeStruct((r, cols), out_dtype), compiler_params=_params(),
    )(a)


def _chip_sum(own, recv, chip, *, name):
    _, r2, cols = own.shape
    tr = _rtile(r2, 256, 16, row_bytes=2 * cols)

    def body(chip_ref, own_ref, *rest):
        o_ref = rest[-1]
        acc = None
        for j in range(N_CHIPS):
            term = jnp.where(chip_ref[0] == j, own_ref[...], rest[j][...]).astype(F32)
            acc = term if acc is None else acc + term
        o_ref[...] = acc

    def slot(j):
        return pl.BlockSpec((None, tr, cols),
                            lambda i, chip_ref: (jnp.where(chip_ref[0] == j, (j + 1) % N_CHIPS, j), i, 0))

    return pl.pallas_call(
        body, name=name,
        grid_spec=pltpu.PrefetchScalarGridSpec(
            num_scalar_prefetch=1, grid=(r2 // tr,),
            in_specs=[pl.BlockSpec((None, tr, cols), lambda i, chip_ref: (chip_ref[0], i, 0))]
                     + [slot(j) for j in range(N_CHIPS)],
            out_specs=pl.BlockSpec((tr, cols), lambda i, chip_ref: (i, 0))),
        out_shape=jax.ShapeDtypeStruct((r2, cols), F32), compiler_params=_params(),
    )(chip, own, *([recv] * N_CHIPS))


def _adam_update(w, gv, m, v):
    c1 = 1.0 / (1.0 - ADAM_B1 ** ADAM_STEP)
    c2 = 1.0 / (1.0 - ADAM_B2 ** ADAM_STEP)
    nm = ADAM_B1 * m + (1.0 - ADAM_B1) * gv
    nv = ADAM_B2 * v + (1.0 - ADAM_B2) * gv * gv
    return -ADAM_LR * ((nm * c1) / (jnp.sqrt(nv * c2) + ADAM_EPS) + ADAM_WD * w), nm, nv


def _adamw_halves(w, g_mine, g_other, m, v, core, *, name):
    r, cols = w.shape
    r2 = r // 2
    tr = _rtile(r2, 256, 8, row_bytes=4 * cols)
    nt = r2 // tr

    def body(core_ref, w_ref, gm_ref, go_ref, m_ref, v_ref, g_ref, d_ref, nm_ref, nv_ref):
        gv = jnp.where(pl.program_id(0) == core_ref[0], gm_ref[...], go_ref[...])
        g_ref[...] = gv
        d_ref[...], nm_ref[...], nv_ref[...] = _adam_update(w_ref[...], gv, m_ref[...], v_ref[...])

    full = pl.BlockSpec((tr, cols), lambda hf, i, core_ref: (hf * nt + i, 0))
    half = pl.BlockSpec((tr, cols), lambda hf, i, core_ref: (i, 0))
    shape = jax.ShapeDtypeStruct((r, cols), F32)
    return pl.pallas_call(
        body, name=name,
        grid_spec=pltpu.PrefetchScalarGridSpec(
            num_scalar_prefetch=1, grid=(2, nt), in_specs=[full, half, half, full, full], out_specs=[full] * 4),
        out_shape=[shape] * 4, compiler_params=_params(),
    )(core, w, g_mine, g_other, m, v)


def _adamw_split_rows(w, g_mine, g_other, m, v, core, *, name, tc=256):
    r, cols = w.shape
    r2 = g_mine.shape[0]
    tc = _tile(cols, tc)

    def body(core_ref, w_ref, gm_ref, go_ref, m_ref, v_ref, g_ref, d_ref, nm_ref, nv_ref):
        mine_first = core_ref[0] == 0
        for lo, hi, first in ((0, r2, True), (r2, r, False)):
            n = hi - lo
            gm, go = gm_ref[0:n, :], go_ref[0:n, :]
            gv = jnp.where(mine_first, gm, go) if first else jnp.where(mine_first, go, gm)
            g_ref[lo:hi, :] = gv
            d_ref[lo:hi, :], nm_ref[lo:hi, :], nv_ref[lo:hi, :] = _adam_update(
                w_ref[lo:hi, :], gv, m_ref[lo:hi, :], v_ref[lo:hi, :])

    full = pl.BlockSpec((r, tc), lambda j, core_ref: (0, j))
    half = pl.BlockSpec((r2, tc), lambda j, core_ref: (0, j))
    shape = jax.ShapeDtypeStruct((r, cols), F32)
    return pl.pallas_call(
        body, name=name,
        grid_spec=pltpu.PrefetchScalarGridSpec(
            num_scalar_prefetch=1, grid=(cols // tc,), in_specs=[full, half, half, full, full],
            out_specs=[full] * 4),
        out_shape=[shape] * 4, compiler_params=_params(),
    )(core, w, g_mine, g_other, m, v)


def _adamw(w, g, m, v, *, name, rows=256):
    r, cols = w.shape
    tr = _rtile(r, rows, 8)

    def body(w_ref, g_ref, m_ref, v_ref, d_ref, nm_ref, nv_ref):
        d_ref[...], nm_ref[...], nv_ref[...] = _adam_update(w_ref[...], g_ref[...], m_ref[...], v_ref[...])

    blk = pl.BlockSpec((tr, cols), lambda i: (i, 0))
    shape = jax.ShapeDtypeStruct((r, cols), F32)
    return pl.pallas_call(
        body, name=name, grid=(r // tr,), in_specs=[blk] * 4, out_specs=[blk] * 3,
        out_shape=[shape] * 3, compiler_params=_params(),
    )(w, g, m, v)


_BIG = (("w_in", 1), ("w_branch_a", 0), ("w_branch_b", 0), ("w_out", 0), ("w_up", 1), ("w_down", 0),
        ("w_ple", 1), ("w_ple_gate", 0))
_SMALL = ("b_f", "gmlp_ln_g", "gmlp_ln_b", "gmlp_w_s", "gmlp_b_s", "norm_ffn_g", "conv_b", "norm_ple_g",
          "norm_final_g", "norm_mix_g")
_WEIGHTS = ("norm_mix_g", "w_in", "b_f", "gmlp_ln_g", "gmlp_ln_b", "gmlp_w_s", "gmlp_b_s", "w_branch_a",
            "w_branch_b", "w_out", "norm_ffn_g", "w_up", "conv_w", "conv_b", "w_down", "norm_ple_g", "w_ple",
            "w_ple_gate", "norm_final_g")
_PACK_ROWS = 8


def _pack(arrays):
    parts = []
    for a in arrays:
        flat = a.reshape(-1)
        unit = _PACK_ROWS * LANES
        flat = jnp.pad(flat, (0, (-flat.shape[0]) % unit))
        parts.append(flat.reshape(-1, LANES))
    return jnp.concatenate(parts, axis=0)


def _unpack(packed, shapes):
    out, row = [], 0
    for shp in shapes:
        size = math.prod(shp)
        rows = -(-size // (_PACK_ROWS * LANES)) * _PACK_ROWS
        out.append(packed[row:row + rows].reshape(-1)[:size].reshape(shp))
        row += rows
    return out


def _take_cols(parts, lo, hi):
    out, start = [], 0
    for a in parts:
        width = a.shape[1]
        a0, a1 = max(lo, start) - start, min(hi, start + width) - start
        if a1 > a0:
            out.append(a if (a0, a1) == (0, width) else a[:, a0:a1])
        start += width
    return out[0] if len(out) == 1 else jnp.concatenate(out, axis=1)


def _take_rows(parts, lo, hi):
    out, start = [], 0
    for a in parts:
        height = a.shape[0]
        a0, a1 = max(lo, start) - start, min(hi, start + height) - start
        if a1 > a0:
            out.append(a if (a0, a1) == (0, height) else a[a0:a1])
        start += height
    return out[0] if len(out) == 1 else jnp.concatenate(out, axis=0)


def _assemble(gathered, axis):
    n, r, cols = gathered.shape
    if axis == 0:
        return gathered.reshape(n * r, cols)
    return _take_cols([gathered[j] for j in range(n)], 0, n * cols)


def _to_chunks(parts, axis):
    rows, total = parts[0].shape[0], sum(a.shape[1] for a in parts)
    if axis == 0:
        r, cols = rows // N_CHIPS, total
        chunks = _take_cols(parts, 0, total).reshape(N_CHIPS, r, cols)
    else:
        r, cols = rows, total // N_CHIPS
        chunks = jnp.stack([_take_cols(parts, j * cols, (j + 1) * cols) for j in range(N_CHIPS)])
    return chunks.reshape(N_CHIPS, 2, r // 2, cols)


def kernel(x, p, norm_mix_g, w_in, b_f, gmlp_ln_g, gmlp_ln_b, gmlp_w_s, gmlp_b_s, w_branch_a, w_branch_b, w_out, norm_ffn_g, w_up, conv_w, conv_b, w_down, norm_ple_g, w_ple, w_ple_gate, norm_final_g, loss_target, m_norm_mix_g, m_w_in, m_b_f, m_gmlp_ln_g, m_gmlp_ln_b, m_gmlp_w_s, m_gmlp_b_s, m_w_branch_a, m_w_branch_b, m_w_out, m_norm_ffn_g, m_w_up, m_conv_w, m_conv_b, m_w_down, m_norm_ple_g, m_w_ple, m_w_ple_gate, m_norm_final_g, v_norm_mix_g, v_w_in, v_b_f, v_gmlp_ln_g, v_gmlp_ln_b, v_gmlp_w_s, v_gmlp_b_s, v_w_branch_a, v_w_branch_b, v_w_out, v_norm_ffn_g, v_w_up, v_conv_w, v_conv_b, v_w_down, v_norm_ple_g, v_w_ple, v_w_ple_gate, v_norm_final_g):
    args = dict(locals())
    wt = {n: args[n] for n in _WEIGHTS}
    mom = {n: args["m_" + n] for n in _WEIGHTS}
    var = {n: args["v_" + n] for n in _WEIGHTS}
    chip = 2 * lax.axis_index("x") + lax.axis_index("y")
    core = lax.axis_index("c").astype(jnp.int32).reshape(1)

    chip1 = chip.astype(jnp.int32).reshape(1)
    device = 2 * chip + lax.axis_index("c")
    axis_of = dict(_BIG)
    names = [n for n, _ in _BIG]
    put_mine = lambda land, mine: lax.dynamic_update_index_in_dim(land, mine, chip, 0)

    shard_in = w_in[0].astype(BF16)
    sems_in = _copies_start([shard_in], [lax.empty((N_CHIPS,) + shard_in.shape, BF16)], _gather_plan([True]), 3,
                            name="gather_start_in")
    shards = [wt[n][0].astype(BF16) for n in names[1:]] + [conv_w[0]]
    halved = [True] * len(names[1:]) + [False]
    lands = [lax.empty((N_CHIPS,) + a.shape, a.dtype) for a in shards]
    send_sems, recv_sems, srcs, lands, rest_token = _copies_start(
        shards, lands, _gather_plan(halved), 3 * len(shards), name="gather_start_rest", after=[sems_in[4]])
    o1 = 2 * GMLP_WIDTH
    o2 = o1 + 3 * FOX_WIDTH
    o3 = o2 + FOX_HEADS
    fpad = ((0, 0), (0, LANES - FOX_HEADS))
    w = {
        "conv_b": conv_b, "norm_mix_g": norm_mix_g, "norm_ffn_g": norm_ffn_g, "norm_ple_g": norm_ple_g,
        "norm_final_g": norm_final_g.reshape(1, D_MODEL), "b_f": jnp.pad(b_f, fpad),
        "gmlp_ln_g": gmlp_ln_g, "gmlp_ln_b": gmlp_ln_b, "gmlp_w_s": gmlp_w_s[0],
        "gmlp_b_s_t": jnp.pad(gmlp_b_s[0].T, ((0, 0), (0, LANES - GMLP_GROUPS))),
        "first_dep": rest_token,
    }

    def get_w_in(after):
        _, got = _copies_wait(sems_in[0], sems_in[1], sems_in[2], sems_in[3], _gather_plan([True]), 0, [after],
                              name="gather_wait_in")
        got = _forward_halves(got, name="gather_forward_in")
        slots = put_mine(got[0], shard_in)
        slots = [slots[j] for j in range(N_CHIPS)]
        return {"w_uv": _take_cols(slots, 0, o1), "w_qkv": _take_cols(slots, o1, o2),
                "w_f": jnp.pad(_take_cols(slots, o2, o3), fpad), "w_g": _take_cols(slots, o3, o3 + 2 * D_MODEL)}

    def get_w_rest(after):
        _, got = _copies_wait(send_sems, recv_sems, srcs, lands, _gather_plan(halved), 0, [after],
                              name="gather_wait_rest")
        got = list(_forward_halves(got[:-1], name="gather_forward_rest")) + got[-1:]
        slots = {n: put_mine(got[i], shards[i]) for i, n in enumerate(names[1:])}
        full = {n: _assemble(slots[n], axis_of[n]) for n in names[1:] if n != "w_up"}
        up = [slots["w_up"][j] for j in range(N_CHIPS)]
        return {"w_branch_a": full["w_branch_a"], "w_branch_b": full["w_branch_b"], "w_out": full["w_out"],
                "w_up_a": _take_cols(up, 0, D_FF), "w_up_b": _take_cols(up, D_FF, 2 * D_FF),
                "w_down": full["w_down"], "w_ple": full["w_ple"], "w_ple_gate": full["w_ple_gate"],
                "conv_w": _assemble(put_mine(got[-1], shards[-1]), 1)}

    grads, delta, new_m, new_v = {}, {}, {}, {}
    pending = {}

    def to_chunks(n, gr):
        return _to_chunks(gr if isinstance(gr, list) else [gr], axis_of[n])

    def reduce_start(group, gfull, tag):
        chunks = [to_chunks(n, gfull[n]) for n in group]
        from_sibling = _pair_exchange(chunks, name="grad_pair_exchange_" + tag)
        pair_sums = [_pair_add(chunks[i], from_sibling[i], core, name="grad_pair_add_" + n) for i, n in enumerate(group)]
        empty = [lax.empty(a.shape, a.dtype) for a in pair_sums]
        ssem, rsem, own, recv, token = _copies_start(pair_sums, empty, _chip_plan, 3 * len(group),
                                                     name="grad_chip_start_" + tag)
        pending[tag] = (ssem, rsem, own, recv)
        return token

    def reduce_finish(group, tag, after):
        ssem, rsem, own, recv = pending[tag]
        own, recv = _copies_wait(ssem, rsem, own, recv, _chip_plan, 0, after, name="grad_chip_wait_" + tag)
        halves = [_chip_sum(own[i], recv[i], chip1, name="grad_chip_sum_" + n) for i, n in enumerate(group)]
        other_halves = _pair_share(halves, name="grad_pair_share_" + tag)
        for i, n in enumerate(group):
            shp = wt[n].shape
            outs = _adamw_halves(wt[n].reshape(shp[-2:]), halves[i], other_halves[i], mom[n].reshape(shp[-2:]),
                                 var[n].reshape(shp[-2:]), core, name="adamw_" + n)
            grads[n], delta[n], new_m[n], new_v[n] = (o.reshape(shp) for o in outs)
        return new_v[group[-1]]

    ffn_group = ("w_up", "w_down", "w_ple", "w_ple_gate")
    mix_group = ("w_in", "w_branch_a", "w_branch_b", "w_out")

    def on_grads_ffn(g):
        gfull = dict(g)
        gfull["w_up"] = [g["w_up_a"], g["w_up_b"]]
        return reduce_start(ffn_group, gfull, "ffn")

    def on_grads_mix(g):
        early = [g[n] if n != "b_f" else g[n][:, :FOX_HEADS] for n in _SMALL[:-1]] + [g["conv_w"]]
        vec = _pack(early)
        ssem, rsem, own, recv, small_token = _copies_start(
            [vec], [lax.empty((8,) + vec.shape, F32)], _all_plan, 7, name="small_start")
        pending["small"] = (ssem, rsem, own, recv)
        gfull = dict(g)
        gfull["w_in"] = [g["w_uv"], g["w_qkv"], g["w_f"][:, :FOX_HEADS], g["w_g"]]
        token = reduce_start(mix_group, gfull, "mix")
        pending["ffn_done"] = reduce_finish(ffn_group, "ffn", [token])
        return token + small_token

    loss, grad_x, g = _device_step(x[0], p[0, 0], loss_target[0], w, get_w_in, get_w_rest, on_grads_ffn, on_grads_mix)

    ssem, rsem, own, recv = pending["small"]
    own, recv = _copies_wait(ssem, rsem, own, recv, _all_plan, 0, [grad_x, pending["ffn_done"]], name="small_wait")
    vec_early = _sum_slots(lax.dynamic_update_index_in_dim(recv[0], own[0], device, 0), F32, name="small_sum")
    vec_late = _pack([g["norm_mix_g"]])
    vec_late = _sum_slots(_all_exchange(vec_late, name="small_exchange_late"), F32, name="small_sum_late")
    early_rows = _pack([wt[n] for n in _SMALL[:-1]]).shape[0]
    vec = jnp.concatenate([vec_early[:early_rows], vec_late], axis=0)
    for n, a in zip(_SMALL, _unpack(vec, [wt[n].shape for n in _SMALL])):
        grads[n] = a
    conv_w_grad = _unpack(vec_early[early_rows:], [(3, 2 * D_FF)])[0]
    grads["conv_w"] = lax.dynamic_slice_in_dim(conv_w_grad, chip * conv_w.shape[2], conv_w.shape[2], axis=1).reshape(conv_w.shape)

    reduce_finish(mix_group, "mix", [grad_x, pending["ffn_done"], vec])
    shp = conv_w.shape
    outs = _adamw(conv_w.reshape(shp[-2:]), grads["conv_w"].reshape(shp[-2:]), m_conv_w.reshape(shp[-2:]),
                  v_conv_w.reshape(shp[-2:]), name="adamw_conv_w")
    delta["conv_w"], new_m["conv_w"], new_v["conv_w"] = (o.reshape(shp) for o in outs)
    outs = _adamw(_pack([wt[n] for n in _SMALL]), vec, _pack([mom[n] for n in _SMALL]),
                  _pack([var[n] for n in _SMALL]), name="adamw_small", rows=2048)
    for d, o in zip((delta, new_m, new_v), outs):
        for n, a in zip(_SMALL, _unpack(o, [wt[n].shape for n in _SMALL])):
            d[n] = a

    total_loss = lax.psum(loss[0, 0], ("x", "y", "c"))
    return (total_loss, grad_x.reshape(x.shape), *[grads[n] for n in _WEIGHTS], *[delta[n] for n in _WEIGHTS],
            *[new_m[n] for n in _WEIGHTS], *[new_v[n] for n in _WEIGHTS])
```

```python
import functools
import math

import jax
import jax.numpy as jnp
from jax import lax
from jax.experimental import pallas as pl
from jax.experimental.pallas import tpu as pltpu

F32 = jnp.float32
BF16 = jnp.bfloat16

D_MODEL = 1024
EPS = 1e-6
CHUNK = 64
GMLP_GROUPS = 8
GMLP_BLOCK = 128
GMLP_WIDTH = 1024
FOX_HEADS = 16
FOX_HEAD_DIM = 64
FOX_WIDTH = 1024
HEAD_PAIRS = FOX_HEADS // 2
ATT_BLOCK = 128
D_FF = 2816
PLE_DIM = 256
LANES = 128
BF16_TILE_ROWS = 16
N_CHIPS = 4

ADAM_LR = 0.001
ADAM_B1 = 0.9
ADAM_B2 = 0.999
ADAM_EPS = 1e-08
ADAM_WD = 0.01
ADAM_STEP = 10

VMEM_LIMIT = 56 * 1024 * 1024
MESH = pl.DeviceIdType.MESH

_NN = (((1,), (0,)), ((), ()))
_NT = (((1,), (1,)), ((), ()))
_TN = (((0,), (0,)), ((), ()))


def _params(**kw):
    return pltpu.CompilerParams(vmem_limit_bytes=VMEM_LIMIT, **kw)


def _tile(dim, pref):
    if dim <= pref:
        return dim
    t = (pref // LANES) * LANES
    while t >= LANES:
        if dim % t == 0:
            return t
        t -= LANES
    return dim


def _dot(a, b, dn):
    return lax.dot_general(a.astype(BF16), b.astype(BF16), dn, preferred_element_type=F32)


def _gelu(x):
    c = math.sqrt(2.0 / math.pi)
    t = jnp.tanh(c * (x + 0.044715 * x * x * x))
    return 0.5 * x * (1.0 + t)


def _gelu_and_grad(x):
    c = math.sqrt(2.0 / math.pi)
    x2 = x * x
    t = jnp.tanh(c * (x + 0.044715 * x2 * x))
    g = 0.5 * x * (1.0 + t)
    dg = 0.5 * (1.0 + t) + 0.5 * x * (1.0 - t * t) * c * (1.0 + 3.0 * 0.044715 * x2)
    return g, dg


def _sigmoid(x):
    return 1.0 / (1.0 + jnp.exp(-x))


def _mm(a, b, *, mode, out_dtype, name, add=None, tm=512, tn=512, dep=None):
    if mode == "nn":
        m, k = a.shape
        k2, n = b.shape
    elif mode == "nt":
        m, k = a.shape
        n, k2 = b.shape
    else:
        k, m = a.shape
        k2, n = b.shape
    assert k == k2, (name, a.shape, b.shape)
    tm = _tile(m, tm)
    tn = _tile(n, tn)
    dn = {"nn": _NN, "nt": _NT, "tn": _TN}[mode]

    def body(a_ref, b_ref, *rest):
        o_ref = rest[-1]
        acc = _dot(a_ref[...], b_ref[...], dn)
        if add is not None:
            acc = acc + rest[0][...].astype(F32)
        o_ref[...] = acc.astype(o_ref.dtype)

    a_spec = pl.BlockSpec((k, tm), lambda i, j: (0, i)) if mode == "tn" else pl.BlockSpec((tm, k), lambda i, j: (i, 0))
    b_spec = pl.BlockSpec((tn, k), lambda i, j: (j, 0)) if mode == "nt" else pl.BlockSpec((k, tn), lambda i, j: (0, j))
    o_spec = pl.BlockSpec((tm, tn), lambda i, j: (i, j))
    in_specs = [a_spec, b_spec]
    args = [a, b]
    if add is not None:
        in_specs.append(o_spec)
        args.append(add)
    if dep is not None:
        in_specs.append(pl.BlockSpec(memory_space=pl.ANY))
        args.append(dep)
    return pl.pallas_call(
        body, name=name, grid=(m // tm, n // tn), in_specs=in_specs, out_specs=o_spec,
        out_shape=jax.ShapeDtypeStruct((m, n), out_dtype), compiler_params=_params(),
    )(*args)


def _mm_nt_sum(pairs, *, out_dtype, name, tm=256, dep=None):
    m, n = pairs[0][0].shape[0], pairs[0][1].shape[0]
    tm = _tile(m, tm)
    np_ = len(pairs)

    def body(*refs):
        o_ref = refs[-1] if dep is None else refs[-1]
        acc = None
        for p in range(np_):
            part = _dot(refs[2 * p][...], refs[2 * p + 1][...], _NT)
            acc = part if acc is None else acc + part
        o_ref[...] = acc.astype(o_ref.dtype)

    in_specs, args = [], []
    for a, b in pairs:
        assert a.shape[0] == m and b.shape[0] == n and a.shape[1] == b.shape[1], (name, a.shape, b.shape)
        in_specs += [pl.BlockSpec((tm, a.shape[1]), lambda i: (i, 0)), pl.BlockSpec(b.shape, lambda i: (0, 0))]
        args += [a, b]
    if dep is not None:
        in_specs.append(pl.BlockSpec(memory_space=pl.ANY))
        args.append(dep)
    return pl.pallas_call(
        body, name=name, grid=(m // tm,), in_specs=in_specs, out_specs=pl.BlockSpec((tm, n), lambda i: (i, 0)),
        out_shape=jax.ShapeDtypeStruct((m, n), out_dtype), compiler_params=_params(),
    )(*args)


def _rms_fwd(x, g, *, name, tm=256, dep=None):
    s, d = x.shape
    tm = _tile(s, tm)

    def body(x_ref, g_ref, *rest):
        h_ref = rest[-1]
        xv = x_ref[...]
        r = lax.rsqrt(jnp.mean(xv * xv, axis=-1, keepdims=True) + EPS)
        h_ref[...] = (xv * r * g_ref[...]).astype(h_ref.dtype)

    deps = [] if dep is None else [dep]
    return pl.pallas_call(
        body, name=name, grid=(s // tm,),
        in_specs=[pl.BlockSpec((tm, d), lambda i: (i, 0)), pl.BlockSpec((1, d), lambda i: (0, 0))]
                 + [pl.BlockSpec(memory_space=pl.ANY)] * len(deps),
        out_specs=pl.BlockSpec((tm, d), lambda i: (i, 0)),
        out_shape=jax.ShapeDtypeStruct((s, d), BF16), compiler_params=_params(),
    )(x, g, *deps)


def _rms_bwd(x, g, dh, dres, *, name, tm=256):
    s, d = x.shape
    tm = _tile(s, tm)

    def body(x_ref, g_ref, dh_ref, dres_ref, dx_ref, dxb_ref, dg_ref):
        xv = x_ref[...]
        r = lax.rsqrt(jnp.mean(xv * xv, axis=-1, keepdims=True) + EPS)
        xhat = xv * r
        dhv = dh_ref[...].astype(F32)
        dyg = dhv * g_ref[...]
        dx = dres_ref[...] + r * (dyg - xhat * jnp.mean(dyg * xhat, axis=-1, keepdims=True))
        dx_ref[...] = dx
        dxb_ref[...] = dx.astype(dxb_ref.dtype)

        @pl.when(pl.program_id(0) == 0)
        def _():
            dg_ref[...] = jnp.zeros_like(dg_ref)

        dg_ref[...] += jnp.sum(dhv * xhat, axis=0, keepdims=True)

    row = pl.BlockSpec((tm, d), lambda i: (i, 0))
    vec = pl.BlockSpec((1, d), lambda i: (0, 0))
    return pl.pallas_call(
        body, name=name, grid=(s // tm,), in_specs=[row, vec, row, row], out_specs=[row, row, vec],
        out_shape=[jax.ShapeDtypeStruct((s, d), F32), jax.ShapeDtypeStruct((s, d), BF16),
                   jax.ShapeDtypeStruct((1, d), F32)],
        compiler_params=_params(),
    )(x, g, dh, dres)


def _gmlp_mask():
    t = lax.broadcasted_iota(jnp.int32, (GMLP_BLOCK, GMLP_BLOCK), 0)
    s_ = lax.broadcasted_iota(jnp.int32, (GMLP_BLOCK, GMLP_BLOCK), 1)
    return (s_ // CHUNK) <= (t // CHUNK)


def _gmlp_norm(zv, ln_g, ln_b):
    vv, dvv = _gelu_and_grad(zv)
    mu = jnp.mean(vv, axis=-1, keepdims=True)
    xc = vv - mu
    rstd = lax.rsqrt(jnp.mean(xc * xc, axis=-1, keepdims=True) + EPS)
    vhat = xc * rstd
    return vhat * ln_g + ln_b, vhat, rstd, dvv


def _gmlp_fwd(z_uv, ln_g, ln_b, w_s, b_s_t, *, name):
    s = z_uv.shape[0]
    w = GMLP_WIDTH
    gd = w // GMLP_GROUPS

    def body(z_ref, lg_ref, lb_ref, ws_ref, bs_ref, a_ref):
        u = _gelu(z_ref[:, :w].astype(F32))
        vn, _, _, _ = _gmlp_norm(z_ref[:, w:].astype(F32), lg_ref[...], lb_ref[...])
        mask = _gmlp_mask()
        for g in range(GMLP_GROUPS):
            wm = jnp.where(mask, ws_ref[g], 0.0)
            mixed = _dot(wm, vn[:, g * gd:(g + 1) * gd], _NN) + bs_ref[:, g:g + 1]
            a_ref[:, g * gd:(g + 1) * gd] = (u[:, g * gd:(g + 1) * gd] * mixed).astype(a_ref.dtype)

    full = lambda shape: pl.BlockSpec(shape, lambda i: (0,) * len(shape))
    return pl.pallas_call(
        body, name=name, grid=(s // GMLP_BLOCK,),
        in_specs=[pl.BlockSpec((GMLP_BLOCK, 2 * w), lambda i: (i, 0)), full((1, w)), full((1, w)),
                  full((GMLP_GROUPS, GMLP_BLOCK, GMLP_BLOCK)), full((GMLP_BLOCK, LANES))],
        out_specs=pl.BlockSpec((GMLP_BLOCK, w), lambda i: (i, 0)),
        out_shape=jax.ShapeDtypeStruct((s, w), BF16), compiler_params=_params(),
    )(z_uv, ln_g, ln_b, w_s, b_s_t)


def _gmlp_bwd(z_uv, da, ln_g, ln_b, w_s, b_s_t, *, name):
    s = z_uv.shape[0]
    w = GMLP_WIDTH
    gd = w // GMLP_GROUPS

    def body(z_ref, da_ref, lg_ref, lb_ref, ws_ref, bs_ref, dz_ref, dws_ref, dbs_ref, dlg_ref, dlb_ref):
        @pl.when(pl.program_id(0) == 0)
        def _():
            dws_ref[...] = jnp.zeros_like(dws_ref)
            dbs_ref[...] = jnp.zeros_like(dbs_ref)
            dlg_ref[...] = jnp.zeros_like(dlg_ref)
            dlb_ref[...] = jnp.zeros_like(dlb_ref)

        u, du_dz = _gelu_and_grad(z_ref[:, :w].astype(F32))
        lg = lg_ref[...]
        vn, vhat, rstd, dvv_dz = _gmlp_norm(z_ref[:, w:].astype(F32), lg, lb_ref[...])
        dav = da_ref[...].astype(F32)
        mask = _gmlp_mask()
        lane = lax.broadcasted_iota(jnp.int32, (GMLP_BLOCK, LANES), 1)
        dvn_parts = []
        dbs = jnp.zeros((GMLP_BLOCK, LANES), F32)
        for g in range(GMLP_GROUPS):
            sl = slice(g * gd, (g + 1) * gd)
            wm = jnp.where(mask, ws_ref[g], 0.0)
            vn_g = vn[:, sl]
            mixed = _dot(wm, vn_g, _NN) + bs_ref[:, g:g + 1]
            dmixed = dav[:, sl] * u[:, sl]
            dz_ref[:, sl] = (dav[:, sl] * mixed * du_dz[:, sl]).astype(dz_ref.dtype)
            dvn_parts.append(_dot(wm, dmixed, _TN))
            dws_ref[g] += jnp.where(mask, _dot(dmixed, vn_g, _NT), 0.0)
            dbs = dbs + jnp.where(lane == g, jnp.sum(dmixed, axis=-1, keepdims=True), 0.0)
        dbs_ref[...] += dbs
        dvn = jnp.concatenate(dvn_parts, axis=-1)
        dlg_ref[...] += jnp.sum(dvn * vhat, axis=0, keepdims=True)
        dlb_ref[...] += jnp.sum(dvn, axis=0, keepdims=True)
        dyg = dvn * lg
        dvv = rstd * (dyg - jnp.mean(dyg, axis=-1, keepdims=True)
                      - vhat * jnp.mean(dyg * vhat, axis=-1, keepdims=True))
        dz_ref[:, w:] = (dvv * dvv_dz).astype(dz_ref.dtype)

    full = lambda shape: pl.BlockSpec(shape, lambda i: (0,) * len(shape))
    return pl.pallas_call(
        body, name=name, grid=(s // GMLP_BLOCK,),
        in_specs=[pl.BlockSpec((GMLP_BLOCK, 2 * w), lambda i: (i, 0)),
                  pl.BlockSpec((GMLP_BLOCK, w), lambda i: (i, 0)), full((1, w)), full((1, w)),
                  full((GMLP_GROUPS, GMLP_BLOCK, GMLP_BLOCK)), full((GMLP_BLOCK, LANES))],
        out_specs=[pl.BlockSpec((GMLP_BLOCK, 2 * w), lambda i: (i, 0)),
                   full((GMLP_GROUPS, GMLP_BLOCK, GMLP_BLOCK)), full((GMLP_BLOCK, LANES)),
                   full((1, w)), full((1, w))],
        out_shape=[jax.ShapeDtypeStruct((s, 2 * w), BF16),
                   jax.ShapeDtypeStruct((GMLP_GROUPS, GMLP_BLOCK, GMLP_BLOCK), F32),
                   jax.ShapeDtypeStruct((GMLP_BLOCK, LANES), F32),
                   jax.ShapeDtypeStruct((1, w), F32), jax.ShapeDtypeStruct((1, w), F32)],
        compiler_params=_params(),
    )(z_uv, da, ln_g, ln_b, w_s, b_s_t)


def _tri(lower):
    r = lax.broadcasted_iota(jnp.int32, (ATT_BLOCK, ATT_BLOCK), 0)
    c = lax.broadcasted_iota(jnp.int32, (ATT_BLOCK, ATT_BLOCK), 1)
    return jnp.where((c <= r) if lower else (c >= r), 1.0, 0.0).astype(F32)


def _log_sigmoid(x):
    return jnp.minimum(x, 0.0) - jnp.log(1.0 + jnp.exp(-jnp.abs(x)))


def _fox_cum(f, b_f, *, name):
    s = f.shape[0]
    nb = s // ATT_BLOCK

    def body(f_ref, b_ref, cb_ref, ct_ref, carry):
        @pl.when(pl.program_id(0) == 0)
        def _():
            carry[...] = jnp.zeros_like(carry)

        lf = _log_sigmoid(f_ref[...] + b_ref[...])
        cum = lax.dot_general(_tri(True), lf, _NN, precision=lax.Precision.HIGHEST,
                              preferred_element_type=F32) + carry[...]
        carry[...] = cum[ATT_BLOCK - 1:ATT_BLOCK, :]
        for h in range(FOX_HEADS):
            cb_ref[h] = jnp.broadcast_to(cum[:, h:h + 1], (ATT_BLOCK, LANES))
        ct_ref[...] = cum.T

    return pl.pallas_call(
        body, name=name, grid=(nb,),
        in_specs=[pl.BlockSpec((ATT_BLOCK, LANES), lambda i: (i, 0)), pl.BlockSpec((1, LANES), lambda i: (0, 0))],
        out_specs=[pl.BlockSpec((FOX_HEADS, ATT_BLOCK, LANES), lambda i: (0, i, 0)),
                   pl.BlockSpec((LANES, ATT_BLOCK), lambda i: (0, i))],
        out_shape=[jax.ShapeDtypeStruct((FOX_HEADS, s, LANES), F32), jax.ShapeDtypeStruct((LANES, s), F32)],
        scratch_shapes=[pltpu.VMEM((1, LANES), F32)], compiler_params=_params(),
    )(f, b_f)


def _fox_dlogit(dcum_t, f, b_f, *, name):
    s = f.shape[0]
    nb = s // ATT_BLOCK

    def body(dc_ref, f_ref, b_ref, df_ref, db_ref, carry):
        @pl.when(pl.program_id(0) == 0)
        def _():
            carry[...] = jnp.zeros_like(carry)
            db_ref[...] = jnp.zeros_like(db_ref)

        d = dc_ref[...].T
        dlog = lax.dot_general(_tri(False), d, _NN, precision=lax.Precision.HIGHEST,
                               preferred_element_type=F32) + carry[...]
        carry[...] = dlog[0:1, :]
        df = dlog * (1.0 - _sigmoid(f_ref[...] + b_ref[...]))
        df_ref[...] = df
        db_ref[...] += jnp.sum(df, axis=0, keepdims=True)

    rev = lambda i: nb - 1 - i
    return pl.pallas_call(
        body, name=name, grid=(nb,),
        in_specs=[pl.BlockSpec((LANES, ATT_BLOCK), lambda i: (0, rev(i))),
                  pl.BlockSpec((ATT_BLOCK, LANES), lambda i: (rev(i), 0)),
                  pl.BlockSpec((1, LANES), lambda i: (0, 0))],
        out_specs=[pl.BlockSpec((ATT_BLOCK, LANES), lambda i: (rev(i), 0)),
                   pl.BlockSpec((1, LANES), lambda i: (0, 0))],
        out_shape=[jax.ShapeDtypeStruct((s, LANES), F32), jax.ShapeDtypeStruct((1, LANES), F32)],
        scratch_shapes=[pltpu.VMEM((1, LANES), F32)], compiler_params=_params(),
    )(dcum_t, f, b_f)


def _causal(qi, ki):
    r = lax.broadcasted_iota(jnp.int32, (ATT_BLOCK, ATT_BLOCK), 0) + qi * ATT_BLOCK
    c = lax.broadcasted_iota(jnp.int32, (ATT_BLOCK, ATT_BLOCK), 1) + ki * ATT_BLOCK
    return c <= r


def _head_mask():
    return lax.broadcasted_iota(jnp.int32, (1, LANES), 1) < FOX_HEAD_DIM


def _attn_fwd(qkv, cum_b, cum_r, *, name):
    s = qkv.shape[0]
    nq = s // ATT_BLOCK
    scale = FOX_HEAD_DIM ** -0.5
    npair = HEAD_PAIRS

    def body(q_ref, k_ref, v_ref, cq_ref, ck_ref, o_ref, l_ref):
        qi = pl.program_id(1)
        m0 = _head_mask()
        q2 = q_ref[...]
        zero = jnp.zeros_like(q2)
        qs = (jnp.where(m0, q2, zero), jnp.where(m0, zero, q2))
        cqs = (cq_ref[0], cq_ref[1])

        def step(ki, carry, masked):
            off = pl.multiple_of(ki * ATT_BLOCK, ATT_BLOCK)
            k2 = k_ref[pl.ds(off, ATT_BLOCK), :]
            v2 = v_ref[pl.ds(off, ATT_BLOCK), :]
            out = []
            for hh in range(2):
                m, l, acc = carry[hh]
                sc = _dot(qs[hh], k2, _NT) * scale + (cqs[hh] - ck_ref[hh:hh + 1, pl.ds(off, ATT_BLOCK)])
                if masked:
                    sc = jnp.where(_causal(qi, ki), sc, -1e30)
                m_new = jnp.maximum(m, jnp.max(sc, axis=-1, keepdims=True))
                alpha = jnp.exp(m - m_new)
                p = jnp.exp(sc - m_new)
                l = alpha * l + jnp.sum(p, axis=-1, keepdims=True)
                acc = alpha * acc + _dot(p, v2, _NN)
                out.append((m_new, l, acc))
            return tuple(out)

        init = tuple((jnp.full((ATT_BLOCK, 1), -1e30, F32), jnp.zeros((ATT_BLOCK, 1), F32),
                      jnp.zeros((ATT_BLOCK, LANES), F32)) for _ in range(2))
        carry = lax.fori_loop(0, qi, lambda ki, c: step(ki, c, False), init)
        (ma, la, acca), (mb, lb, accb) = step(qi, carry, True)
        o_ref[...] = jnp.where(m0, acca / la, accb / lb).astype(o_ref.dtype)
        l_ref[0] = jnp.broadcast_to(ma + jnp.log(la), (ATT_BLOCK, LANES))
        l_ref[1] = jnp.broadcast_to(mb + jnp.log(lb), (ATT_BLOCK, LANES))

    stat = pl.BlockSpec((None, 2, ATT_BLOCK, LANES), lambda j, i: (j, 0, i, 0))
    row = pl.BlockSpec((None, 2, s), lambda j, i: (j, 0, 0))
    return pl.pallas_call(
        body, name=name, grid=(npair, nq),
        in_specs=[pl.BlockSpec((ATT_BLOCK, LANES), lambda j, i: (i, j)),
                  pl.BlockSpec((s, LANES), lambda j, i: (0, npair + j)),
                  pl.BlockSpec((s, LANES), lambda j, i: (0, 2 * npair + j)),
                  stat, row],
        out_specs=[pl.BlockSpec((ATT_BLOCK, LANES), lambda j, i: (i, j)), stat],
        out_shape=[jax.ShapeDtypeStruct((s, FOX_WIDTH), BF16),
                   jax.ShapeDtypeStruct((npair, 2, s, LANES), F32)],
        compiler_params=_params(),
    )(qkv, qkv, qkv, cum_b, cum_r)


def _attn_delta(qkv, do, lse_b, cum_b, cum_r, *, name):
    s = qkv.shape[0]
    nq = s // ATT_BLOCK
    scale = FOX_HEAD_DIM ** -0.5
    npair = HEAD_PAIRS

    def body(q_ref, k_ref, v_ref, do_ref, l_ref, cq_ref, ck_ref, d_ref):
        qi = pl.program_id(1)
        m0 = _head_mask()
        q2 = q_ref[...]
        do2 = do_ref[...]
        qs = (jnp.where(m0, q2, jnp.zeros_like(q2)), jnp.where(m0, jnp.zeros_like(q2), q2))
        dos = (jnp.where(m0, do2, jnp.zeros_like(do2)), jnp.where(m0, jnp.zeros_like(do2), do2))

        def step(ki, carry, masked):
            off = pl.multiple_of(ki * ATT_BLOCK, ATT_BLOCK)
            k2 = k_ref[pl.ds(off, ATT_BLOCK), :]
            v2 = v_ref[pl.ds(off, ATT_BLOCK), :]
            out = []
            for hh in range(2):
                sc = _dot(qs[hh], k2, _NT) * scale + (cq_ref[hh] - ck_ref[hh:hh + 1, pl.ds(off, ATT_BLOCK)])
                p = jnp.exp(sc - l_ref[hh])
                if masked:
                    p = jnp.where(_causal(qi, ki), p, 0.0)
                out.append(carry[hh] + jnp.sum(p * _dot(dos[hh], v2, _NT), axis=-1, keepdims=True))
            return tuple(out)

        init = (jnp.zeros((ATT_BLOCK, 1), F32), jnp.zeros((ATT_BLOCK, 1), F32))
        carry = lax.fori_loop(0, qi, lambda ki, c: step(ki, c, False), init)
        da, db = step(qi, carry, True)
        d_ref[0] = jnp.broadcast_to(da, (ATT_BLOCK, LANES))
        d_ref[1] = jnp.broadcast_to(db, (ATT_BLOCK, LANES))

    stat = pl.BlockSpec((None, 2, ATT_BLOCK, LANES), lambda j, i: (j, 0, i, 0))
    return pl.pallas_call(
        body, name=name, grid=(npair, nq),
        in_specs=[pl.BlockSpec((ATT_BLOCK, LANES), lambda j, i: (i, j)),
                  pl.BlockSpec((s, LANES), lambda j, i: (0, npair + j)),
                  pl.BlockSpec((s, LANES), lambda j, i: (0, 2 * npair + j)),
                  pl.BlockSpec((ATT_BLOCK, LANES), lambda j, i: (i, j)),
                  stat, stat, pl.BlockSpec((None, 2, s), lambda j, i: (j, 0, 0))],
        out_specs=stat,
        out_shape=jax.ShapeDtypeStruct((npair, 2, s, LANES), F32), compiler_params=_params(),
    )(qkv, qkv, qkv, do, lse_b, cum_b, cum_r)


def _attn_bwd(qkv, do, lse_b, delta_b, cum_b, cum_r, *, name):
    s = qkv.shape[0]
    nq = s // ATT_BLOCK
    scale = FOX_HEAD_DIM ** -0.5
    npair = HEAD_PAIRS

    def body(q_ref, k_ref, v_ref, do_ref, l_ref, dl_ref, cq_ref, ck_ref, dq_ref, dk_ref, dv_ref, dc_ref):
        ki = pl.program_id(1)
        m0 = _head_mask()
        k2 = k_ref[...]
        v2 = v_ref[...]
        koff = pl.multiple_of(ki * ATT_BLOCK, ATT_BLOCK)

        @pl.when(ki == 0)
        def _():
            dq_ref[...] = jnp.zeros_like(dq_ref)

        def step(qi, carry, masked):
            off = pl.multiple_of(qi * ATT_BLOCK, ATT_BLOCK)
            q2 = q_ref[pl.ds(off, ATT_BLOCK), :]
            do2 = do_ref[pl.ds(off, ATT_BLOCK), :]
            qzero = jnp.zeros_like(q2)
            dzero = jnp.zeros_like(do2)
            out = []
            dqs = []
            for hh in range(2):
                dk_acc, dv_acc, dc_acc = carry[hh]
                keep = m0 if hh == 0 else jnp.logical_not(m0)
                qh = jnp.where(keep, q2, qzero)
                doh = jnp.where(keep, do2, dzero)
                sc = _dot(qh, k2, _NT) * scale + (cq_ref[hh, pl.ds(off, ATT_BLOCK), :]
                                                 - ck_ref[hh:hh + 1, pl.ds(koff, ATT_BLOCK)])
                p = jnp.exp(sc - l_ref[hh, pl.ds(off, ATT_BLOCK), :])
                if masked:
                    p = jnp.where(_causal(qi, ki), p, 0.0)
                dp = _dot(doh, v2, _NT)
                ds = p * (dp - dl_ref[hh, pl.ds(off, ATT_BLOCK), :])
                dv_acc = dv_acc + _dot(p, do2, _TN)
                dk_acc = dk_acc + _dot(ds, q2, _TN)
                dc_acc = dc_acc - jnp.sum(ds, axis=0, keepdims=True)
                dqs.append(_dot(ds, k2, _NN))
                out.append((dk_acc, dv_acc, dc_acc))
            dq_ref[pl.ds(off, ATT_BLOCK), :] += jnp.where(m0, dqs[0], dqs[1]) * scale
            return tuple(out)

        init = tuple((jnp.zeros((ATT_BLOCK, LANES), F32), jnp.zeros((ATT_BLOCK, LANES), F32),
                      jnp.zeros((1, ATT_BLOCK), F32)) for _ in range(2))
        carry = step(ki, init, True)
        (dka, dva, dca), (dkb, dvb, dcb) = lax.fori_loop(ki + 1, nq, lambda qi, c: step(qi, c, False), carry)
        dk_ref[...] = (jnp.where(m0, dka, dkb) * scale).astype(dk_ref.dtype)
        dv_ref[...] = jnp.where(m0, dva, dvb).astype(dv_ref.dtype)
        dc_ref[0:1, :] = dca
        dc_ref[1:2, :] = dcb

    stat = pl.BlockSpec((None, 2, s, LANES), lambda j, i: (j, 0, 0, 0))
    colfull = lambda base: pl.BlockSpec((s, LANES), lambda j, i: (0, base + j))
    colblk = lambda base: pl.BlockSpec((ATT_BLOCK, LANES), lambda j, i: (i, base + j))
    return pl.pallas_call(
        body, name=name, grid=(npair, nq),
        in_specs=[colfull(0), colblk(npair), colblk(2 * npair), colfull(0), stat, stat, stat,
                  pl.BlockSpec((None, 2, s), lambda j, i: (j, 0, 0))],
        out_specs=[colfull(0), colblk(0), colblk(0), pl.BlockSpec((None, 2, ATT_BLOCK), lambda j, i: (j, 0, i))],
        out_shape=[jax.ShapeDtypeStruct((s, FOX_WIDTH), F32), jax.ShapeDtypeStruct((s, FOX_WIDTH), BF16),
                   jax.ShapeDtypeStruct((s, FOX_WIDTH), BF16), jax.ShapeDtypeStruct((npair, 2, s), F32)],
        compiler_params=_params(),
    )(qkv, qkv, qkv, do, lse_b, delta_b, cum_b, cum_r)


ATT_TQ = 256
ATT_TK = 256
ATT_SCALE = FOX_HEAD_DIM ** -0.5
assert ATT_SCALE == 0.125 and ATT_TQ == ATT_TK


def _causal_t(qi, ki):
    kpos = lax.broadcasted_iota(jnp.int32, (ATT_TK, ATT_TQ), 0) + ki * ATT_TK
    qpos = lax.broadcasted_iota(jnp.int32, (ATT_TK, ATT_TQ), 1) + qi * ATT_TQ
    return kpos <= qpos


def _row_mask():
    return lax.broadcasted_iota(jnp.int32, (LANES, 1), 0) < FOX_HEAD_DIM


def _lane_tile(a, width):
    return a if a.shape[1] == width else jnp.tile(a, (1, width // a.shape[1]))


def _transpose_bf16(a):
    return a.astype(F32).T.astype(BF16)


def _attn_fwd_t(qkv, cum_b, cum_r, *, name):
    s = qkv.shape[0]
    nq = s // ATT_TQ
    npair = HEAD_PAIRS

    def body(q_ref, k_ref, v_ref, cq_ref, ck_ref, o_ref, ot_ref, l_ref, vt_ref):
        qi = pl.program_id(1)
        rows = _row_mask()

        @pl.when(qi == 0)
        def _():
            vt_ref[...] = _transpose_bf16(v_ref[...])

        qt = _transpose_bf16(q_ref[...]) * ATT_SCALE
        zero = jnp.zeros_like(qt)
        qts = (jnp.where(rows, qt, zero), jnp.where(rows, zero, qt))

        def step(ki, carry, masked):
            off = pl.multiple_of(ki * ATT_TK, ATT_TK)
            k2 = k_ref[pl.ds(off, ATT_TK), :]
            vt = vt_ref[:, pl.ds(off, ATT_TK)]
            out = []
            for hh in range(2):
                m, l, acc = carry[hh]
                bias = cq_ref[hh:hh + 1, :] - _lane_tile(ck_ref[hh, pl.ds(off, ATT_TK), :], ATT_TQ)
                sc = _dot(k2, qts[hh], _NN) + bias
                if masked:
                    sc = jnp.where(_causal_t(qi, ki), sc, -1e30)
                m_new = jnp.maximum(m, jnp.max(sc, axis=0, keepdims=True))
                alpha = jnp.exp(m - m_new)
                p = jnp.exp(sc - m_new)
                l = alpha * l + jnp.sum(p, axis=0, keepdims=True)
                p_hi = p.astype(BF16)
                p_lo = (p - p_hi.astype(F32)).astype(BF16)
                acc = alpha * acc + (_dot(vt, p_hi, _NN) + _dot(vt, p_lo, _NN))
                out.append((m_new, l, acc))
            return tuple(out)

        init = tuple((jnp.full((1, ATT_TQ), -1e30, F32), jnp.zeros((1, ATT_TQ), F32),
                      jnp.zeros((LANES, ATT_TQ), F32)) for _ in range(2))
        carry = lax.fori_loop(0, qi // 2, lambda kk, c: step(2 * kk + 1, step(2 * kk, c, False), False), init)
        carry = lax.cond(qi % 2 == 1, lambda c: step(qi - 1, c, False), lambda c: c, carry)
        (ma, la, acca), (mb, lb, accb) = step(qi, carry, True)
        ot = jnp.where(rows, acca / la, accb / lb)
        ot_ref[...] = ot
        o_ref[...] = ot.T.astype(o_ref.dtype)
        l_ref[0:1, :] = ma + jnp.log(la)
        l_ref[1:2, :] = mb + jnp.log(lb)

    row = pl.BlockSpec((None, 2, ATT_TQ), lambda j, i: (j, 0, i))
    return pl.pallas_call(
        body, name=name, grid=(npair, nq),
        in_specs=[pl.BlockSpec((ATT_TQ, LANES), lambda j, i: (i, j)),
                  pl.BlockSpec((s, LANES), lambda j, i: (0, npair + j)),
                  pl.BlockSpec((s, LANES), lambda j, i: (0, 2 * npair + j)),
                  row, pl.BlockSpec((None, 2, s, LANES), lambda j, i: (j, 0, 0, 0))],
        out_specs=[pl.BlockSpec((ATT_TQ, LANES), lambda j, i: (i, j)),
                   pl.BlockSpec((LANES, ATT_TQ), lambda j, i: (j, i)), row],
        out_shape=[jax.ShapeDtypeStruct((s, FOX_WIDTH), BF16), jax.ShapeDtypeStruct((FOX_WIDTH, s), F32),
                   jax.ShapeDtypeStruct((npair, 2, s), F32)],
        scratch_shapes=[pltpu.VMEM((LANES, s), BF16)],
        compiler_params=_params(),
    )(qkv, qkv, qkv, cum_r, cum_b)


def _attn_delta_t(do_t, o_t, *, name):
    s = o_t.shape[1]
    ts = _tile(s, 512)

    def body(do_ref, o_ref, d_ref):
        prod = do_ref[...].astype(F32) * o_ref[...]
        d_ref[0:1, :] = jnp.sum(prod[:FOX_HEAD_DIM], axis=0, keepdims=True)
        d_ref[1:2, :] = jnp.sum(prod[FOX_HEAD_DIM:], axis=0, keepdims=True)

    blk = pl.BlockSpec((LANES, ts), lambda j, i: (j, i))
    return pl.pallas_call(
        body, name=name, grid=(HEAD_PAIRS, s // ts), in_specs=[blk, blk],
        out_specs=pl.BlockSpec((None, 2, ts), lambda j, i: (j, 0, i)),
        out_shape=jax.ShapeDtypeStruct((HEAD_PAIRS, 2, s), F32), compiler_params=_params(),
    )(do_t, o_t)


def _attn_bwd_t(qkv, do, o_t, lse, cum_b, cum_r, *, name):
    s = qkv.shape[0]
    nq = s // ATT_TQ
    npair = HEAD_PAIRS

    def body(q_ref, k_ref, v_ref, do_ref, ot_ref, l_ref, cq_ref, ck_ref, dq_ref, dk_ref, dv_ref, dc_ref,
             qt_ref, dot_ref, dqt_ref, dl_ref):
        ki = pl.program_id(1)
        m0 = _head_mask()
        rows = _row_mask()
        k2 = k_ref[...]
        v2 = v_ref[...]
        kt = _transpose_bf16(k2)
        ks = k2 * ATT_SCALE
        kz, vz = jnp.zeros_like(k2), jnp.zeros_like(v2)
        khs = (jnp.where(m0, ks, kz), jnp.where(m0, kz, ks))
        vhs = (jnp.where(m0, v2, vz), jnp.where(m0, vz, v2))
        cks = tuple(_lane_tile(ck_ref[hh], ATT_TQ) for hh in range(2))

        @pl.when(ki == 0)
        def _():
            dqt_ref[...] = jnp.zeros_like(dqt_ref)
            qt_ref[...] = _transpose_bf16(q_ref[...])
            do_t = do_ref[...].astype(F32).T
            dot_ref[...] = do_t.astype(BF16)
            prod = do_t * ot_ref[...]
            dl_ref[0:1, :] = jnp.sum(prod[:FOX_HEAD_DIM], axis=0, keepdims=True)
            dl_ref[1:2, :] = jnp.sum(prod[FOX_HEAD_DIM:], axis=0, keepdims=True)

        def step(qi, carry, masked):
            off = pl.multiple_of(qi * ATT_TQ, ATT_TQ)
            q2 = q_ref[pl.ds(off, ATT_TQ), :]
            do2 = do_ref[pl.ds(off, ATT_TQ), :]
            qt = qt_ref[:, pl.ds(off, ATT_TQ)]
            dot_ = dot_ref[:, pl.ds(off, ATT_TQ)]
            out, dqs = [], []
            for hh in range(2):
                dk_acc, dv_acc, dc_acc = carry[hh]
                sc = _dot(khs[hh], qt, _NN) + (cq_ref[hh:hh + 1, pl.ds(off, ATT_TQ)] - cks[hh])
                p = jnp.exp(sc - l_ref[hh:hh + 1, pl.ds(off, ATT_TQ)])
                if masked:
                    p = jnp.where(_causal_t(qi, ki), p, 0.0)
                dp = _dot(vhs[hh], dot_, _NN)
                ds = p * (dp - dl_ref[hh:hh + 1, pl.ds(off, ATT_TQ)])
                dc_acc = dc_acc - jnp.sum(ds, axis=1, keepdims=True)
                dss = (ds * ATT_SCALE).astype(BF16)
                dv_acc = dv_acc + _dot(p, do2, _NN)
                dk_acc = dk_acc + _dot(dss, q2, _NN)
                dqs.append(_dot(kt, dss, _NN))
                out.append((dk_acc, dv_acc, dc_acc))
            dqt_ref[:, pl.ds(off, ATT_TQ)] += jnp.where(rows, dqs[0], dqs[1])
            return tuple(out)

        init = tuple((jnp.zeros((ATT_TK, LANES), F32), jnp.zeros((ATT_TK, LANES), F32),
                      jnp.zeros((ATT_TK, 1), F32)) for _ in range(2))
        carry = step(ki, init, True)
        rest = nq - 1 - ki
        carry = lax.fori_loop(
            0, rest // 2, lambda t, c: step(ki + 2 + 2 * t, step(ki + 1 + 2 * t, c, False), False), carry)
        carry = lax.cond(rest % 2 == 1, lambda c: step(nq - 1, c, False), lambda c: c, carry)
        (dka, dva, dca), (dkb, dvb, dcb) = carry
        dk_ref[...] = jnp.where(m0, dka, dkb).astype(dk_ref.dtype)
        dv_ref[...] = jnp.where(m0, dva, dvb).astype(dv_ref.dtype)
        dc_ref[0] = jnp.broadcast_to(dca, (ATT_TK, LANES))
        dc_ref[1] = jnp.broadcast_to(dcb, (ATT_TK, LANES))

        @pl.when(ki == nq - 1)
        def _():
            dq_ref[...] = dqt_ref[...].T.astype(dq_ref.dtype)

    colfull = lambda base: pl.BlockSpec((s, LANES), lambda j, i: (0, base + j))
    colblk = lambda base: pl.BlockSpec((ATT_TK, LANES), lambda j, i: (i, base + j))
    stat = pl.BlockSpec((None, 2, s), lambda j, i: (j, 0, 0))
    bcast = pl.BlockSpec((None, 2, ATT_TK, LANES), lambda j, i: (j, 0, i, 0))
    grad = jax.ShapeDtypeStruct((s, FOX_WIDTH), BF16)
    return pl.pallas_call(
        body, name=name, grid=(npair, nq),
        in_specs=[colfull(0), colblk(npair), colblk(2 * npair), colfull(0),
                  pl.BlockSpec((LANES, s), lambda j, i: (j, 0)), stat, stat, bcast],
        out_specs=[colfull(0), colblk(0), colblk(0), bcast],
        out_shape=[grad, grad, grad, jax.ShapeDtypeStruct((npair, 2, s, LANES), F32)],
        scratch_shapes=[pltpu.VMEM((LANES, s), BF16), pltpu.VMEM((LANES, s), BF16), pltpu.VMEM((LANES, s), F32),
                        pltpu.VMEM((2, s), F32)],
        compiler_params=_params(),
    )(qkv, qkv, qkv, do, o_t, lse, cum_r, cum_b)


def _merge_fwd(zg, ya, yb, *, name, tm=256):
    s, d = ya.shape
    tm = _tile(s, tm)

    def body(zg_ref, ya_ref, yb_ref, m_ref):
        ga = _sigmoid(zg_ref[:, :d].astype(F32))
        gb = _sigmoid(zg_ref[:, d:].astype(F32))
        m_ref[...] = (ga * ya_ref[...].astype(F32) + gb * yb_ref[...].astype(F32)).astype(m_ref.dtype)

    row = pl.BlockSpec((tm, d), lambda i: (i, 0))
    row2 = pl.BlockSpec((tm, 2 * d), lambda i: (i, 0))
    return pl.pallas_call(
        body, name=name, grid=(s // tm,), in_specs=[row2, row, row], out_specs=row,
        out_shape=jax.ShapeDtypeStruct((s, d), BF16), compiler_params=_params(),
    )(zg, ya, yb)


def _merge_bwd(dm, zg, ya, yb, *, name, tm=256):
    s, d = ya.shape
    tm = _tile(s, tm)

    def body(dm_ref, zg_ref, ya_ref, yb_ref, dzg_ref, dya_ref, dyb_ref):
        dmv = dm_ref[...].astype(F32)
        ga = _sigmoid(zg_ref[:, :d].astype(F32))
        gb = _sigmoid(zg_ref[:, d:].astype(F32))
        dzg_ref[:, :d] = (dmv * ya_ref[...].astype(F32) * ga * (1.0 - ga)).astype(dzg_ref.dtype)
        dzg_ref[:, d:] = (dmv * yb_ref[...].astype(F32) * gb * (1.0 - gb)).astype(dzg_ref.dtype)
        dya_ref[...] = (dmv * ga).astype(dya_ref.dtype)
        dyb_ref[...] = (dmv * gb).astype(dyb_ref.dtype)

    row = pl.BlockSpec((tm, d), lambda i: (i, 0))
    row2 = pl.BlockSpec((tm, 2 * d), lambda i: (i, 0))
    return pl.pallas_call(
        body, name=name, grid=(s // tm,), in_specs=[row, row2, row, row], out_specs=[row2, row, row],
        out_shape=[jax.ShapeDtypeStruct((s, 2 * d), BF16), jax.ShapeDtypeStruct((s, d), BF16),
                   jax.ShapeDtypeStruct((s, d), BF16)],
        compiler_params=_params(),
    )(dm, zg, ya, yb)


def _shift_down(u, k, row):
    return jnp.where(row >= k, pltpu.roll(u, k, 0), 0.0)


def _shift_up(u, k, row):
    n = u.shape[0]
    return jnp.where(row < n - k, pltpu.roll(u, n - k, 0), 0.0)


def _conv_act_fwd(up_a, up_b, cw_a, cw_b, cb_a, cb_b, *, name, tc=128):
    s, f = up_a.shape
    tc = _tile(f, tc)

    def body(ua_ref, ub_ref, wa_ref, wb_ref, ba_ref, bb_ref, act_ref):
        row = lax.broadcasted_iota(jnp.int32, (s, tc), 0)

        def conv(u_ref, w_ref, b_ref):
            u = u_ref[...].astype(F32)
            return (b_ref[...] + w_ref[0:1, :] * _shift_down(u, 2, row)
                    + w_ref[1:2, :] * _shift_down(u, 1, row) + w_ref[2:3, :] * u)

        ca = conv(ua_ref, wa_ref, ba_ref)
        cb = conv(ub_ref, wb_ref, bb_ref)
        act_ref[...] = (_gelu(ca) * cb).astype(act_ref.dtype)

    col = pl.BlockSpec((s, tc), lambda j: (0, j))
    w3 = pl.BlockSpec((3, tc), lambda j: (0, j))
    b1 = pl.BlockSpec((1, tc), lambda j: (0, j))
    return pl.pallas_call(
        body, name=name, grid=(f // tc,), in_specs=[col, col, w3, w3, b1, b1], out_specs=col,
        out_shape=jax.ShapeDtypeStruct((s, f), BF16), compiler_params=_params(),
    )(up_a, up_b, cw_a, cw_b, cb_a, cb_b)


def _conv_act_bwd(up_a, up_b, dact, cw_a, cw_b, cb_a, cb_b, *, name, tc=128):
    s, f = up_a.shape
    tc = _tile(f, tc)

    def body(ua_ref, ub_ref, da_ref, wa_ref, wb_ref, ba_ref, bb_ref, dua_ref, dub_ref, dwa_ref, dwb_ref):
        row = lax.broadcasted_iota(jnp.int32, (s, tc), 0)

        def conv(u_ref, w_ref, b_ref):
            u = u_ref[...].astype(F32)
            u1 = _shift_down(u, 1, row)
            u2 = _shift_down(u, 2, row)
            return u, u1, u2, b_ref[...] + w_ref[0:1, :] * u2 + w_ref[1:2, :] * u1 + w_ref[2:3, :] * u

        def back(dc, taps, w_ref, du_ref, dw_ref):
            u, u1, u2 = taps
            dw_ref[0:1, :] = jnp.sum(dc * u2, axis=0, keepdims=True)
            dw_ref[1:2, :] = jnp.sum(dc * u1, axis=0, keepdims=True)
            dw_ref[2:3, :] = jnp.sum(dc * u, axis=0, keepdims=True)
            dw_ref[3:4, :] = jnp.sum(dc, axis=0, keepdims=True)
            du = (w_ref[2:3, :] * dc + w_ref[1:2, :] * _shift_up(dc, 1, row)
                  + w_ref[0:1, :] * _shift_up(dc, 2, row))
            du_ref[...] = du.astype(du_ref.dtype)

        ua, ua1, ua2, ca = conv(ua_ref, wa_ref, ba_ref)
        ub, ub1, ub2, cb = conv(ub_ref, wb_ref, bb_ref)
        g, dg = _gelu_and_grad(ca)
        dact_v = da_ref[...].astype(F32)
        back(dact_v * cb * dg, (ua, ua1, ua2), wa_ref, dua_ref, dwa_ref)
        back(dact_v * g, (ub, ub1, ub2), wb_ref, dub_ref, dwb_ref)

    col = pl.BlockSpec((s, tc), lambda j: (0, j))
    w3 = pl.BlockSpec((3, tc), lambda j: (0, j))
    w4 = pl.BlockSpec((4, tc), lambda j: (0, j))
    b1 = pl.BlockSpec((1, tc), lambda j: (0, j))
    return pl.pallas_call(
        body, name=name, grid=(f // tc,), in_specs=[col, col, col, w3, w3, b1, b1],
        out_specs=[col, col, w4, w4],
        out_shape=[jax.ShapeDtypeStruct((s, f), BF16), jax.ShapeDtypeStruct((s, f), BF16),
                   jax.ShapeDtypeStruct((4, f), F32), jax.ShapeDtypeStruct((4, f), F32)],
        compiler_params=_params(),
    )(up_a, up_b, dact, cw_a, cw_b, cb_a, cb_b)


def _ple_final(x2, ple, zp, target, g_final, *, name, tm=256):
    s, d = x2.shape
    tm = _tile(s, tm)

    def body(x_ref, ple_ref, zp_ref, t_ref, g_ref, dx_ref, dple_ref, dzp_ref, dg_ref, loss_ref):
        @pl.when(pl.program_id(0) == 0)
        def _():
            dg_ref[...] = jnp.zeros_like(dg_ref)
            loss_ref[...] = jnp.zeros_like(loss_ref)

        gp = _sigmoid(zp_ref[...].astype(F32))
        plev = ple_ref[...].astype(F32)
        x3 = x_ref[...] + plev * gp
        r = lax.rsqrt(jnp.mean(x3 * x3, axis=-1, keepdims=True) + EPS)
        xhat = x3 * r
        gv = g_ref[...]
        diff = xhat * gv - t_ref[...]
        loss_ref[...] += 0.5 * jnp.sum(jnp.mean(diff * diff, axis=-1, keepdims=True), axis=0, keepdims=True)
        dy = diff * (1.0 / d)
        dg_ref[...] += jnp.sum(dy * xhat, axis=0, keepdims=True)
        dyg = dy * gv
        dx3 = r * (dyg - xhat * jnp.mean(dyg * xhat, axis=-1, keepdims=True))
        dx_ref[...] = dx3
        dple_ref[...] = (dx3 * gp).astype(dple_ref.dtype)
        dzp_ref[...] = (dx3 * plev * gp * (1.0 - gp)).astype(dzp_ref.dtype)

    row = pl.BlockSpec((tm, d), lambda i: (i, 0))
    vec = pl.BlockSpec((1, d), lambda i: (0, 0))
    return pl.pallas_call(
        body, name=name, grid=(s // tm,), in_specs=[row, row, row, row, vec],
        out_specs=[row, row, row, vec, pl.BlockSpec((1, LANES), lambda i: (0, 0))],
        out_shape=[jax.ShapeDtypeStruct((s, d), F32), jax.ShapeDtypeStruct((s, d), BF16),
                   jax.ShapeDtypeStruct((s, d), BF16), jax.ShapeDtypeStruct((1, d), F32),
                   jax.ShapeDtypeStruct((1, LANES), F32)],
        compiler_params=_params(),
    )(x2, ple, zp, target, g_final)


def _device_step(x, p, target, w, get_w_in=None, get_w_rest=None, on_grads_ffn=None, on_grads_mix=None):
    s = x.shape[0]
    g = {}
    w = dict(w)

    h = _rms_fwd(x, w["norm_mix_g"], name="rms_mix", dep=w.get("first_dep"))
    if get_w_in is not None:
        w.update(get_w_in(h))
    z_uv = _mm(h, w["w_uv"], mode="nn", out_dtype=BF16, name="proj_uv", tm=1024, dep=w.get("proj_dep"))
    qkv = _mm(h, w["w_qkv"], mode="nn", out_dtype=BF16, name="proj_qkv", tm=1024)
    zg = _mm(h, w["w_g"], mode="nn", out_dtype=BF16, name="proj_gate", tm=1024)
    f = _mm(h, w["w_f"], mode="nn", out_dtype=F32, name="proj_f", tm=1024)

    a = _gmlp_fwd(z_uv, w["gmlp_ln_g"], w["gmlp_ln_b"], w["gmlp_w_s"], w["gmlp_b_s_t"], name="gmlp_fwd")

    cum_b, cum_t = _fox_cum(f, w["b_f"], name="fox_cum")
    cum_b = cum_b.reshape(HEAD_PAIRS, 2, s, LANES)
    cum_r = cum_t[:FOX_HEADS].reshape(HEAD_PAIRS, 2, s)
    b, o_t, lse = _attn_fwd_t(qkv, cum_b, cum_r, name="attn_fwd")
    if get_w_rest is not None:
        w.update(get_w_rest(b))

    ya = _mm(a, w["w_branch_a"], mode="nn", out_dtype=BF16, name="branch_a", tm=1024)
    yb = _mm(b, w["w_branch_b"], mode="nn", out_dtype=BF16, name="branch_b", tm=1024)
    merged = _merge_fwd(zg, ya, yb, name="merge_fwd")
    x1 = _mm(merged, w["w_out"], mode="nn", out_dtype=F32, name="proj_out", add=x, tm=1024)

    h2 = _rms_fwd(x1, w["norm_ffn_g"], name="rms_ffn")
    up_a = _mm(h2, w["w_up_a"], mode="nn", out_dtype=BF16, name="up_a", tm=1024, tn=D_FF // 2)
    up_b = _mm(h2, w["w_up_b"], mode="nn", out_dtype=BF16, name="up_b", tm=1024, tn=D_FF // 2)
    cw, cb = w["conv_w"], w["conv_b"]
    conv_args = (cw[:, :D_FF], cw[:, D_FF:], cb[:, :D_FF], cb[:, D_FF:])
    act = _conv_act_fwd(up_a, up_b, *conv_args, name="conv_act_fwd")
    x2 = _mm(act, w["w_down"], mode="nn", out_dtype=F32, name="down", add=x1, tm=512)

    h3 = _rms_fwd(x2, w["norm_ple_g"], name="rms_ple")
    ple = _mm(p, w["w_ple"], mode="nn", out_dtype=BF16, name="ple_proj", tm=1024)
    zp = _mm(h3, w["w_ple_gate"], mode="nn", out_dtype=BF16, name="ple_gate", tm=1024)
    dx3, dple, dzp, g["norm_final_g"], loss = _ple_final(x2, ple, zp, target, w["norm_final_g"], name="ple_final")

    g["w_ple"] = _mm(p, dple, mode="tn", out_dtype=BF16, name="dw_ple")
    g["w_ple_gate"] = _mm(h3, dzp, mode="tn", out_dtype=BF16, name="dw_ple_gate")
    dh3 = _mm(dzp, w["w_ple_gate"], mode="nt", out_dtype=BF16, name="dh3")
    dx2, dx2_b, g["norm_ple_g"] = _rms_bwd(x2, w["norm_ple_g"], dh3, dx3, name="rms_ple_bwd")

    g["w_down"] = _mm(act, dx2_b, mode="tn", out_dtype=BF16, name="dw_down", tm=D_FF // 2)
    dact = _mm(dx2_b, w["w_down"], mode="nt", out_dtype=BF16, name="dact", tn=D_FF // 2)
    dup_a, dup_b, dcw_a, dcw_b = _conv_act_bwd(up_a, up_b, dact, *conv_args, name="conv_act_bwd")
    g["conv_w"] = jnp.concatenate([dcw_a[:3], dcw_b[:3]], axis=1)
    g["conv_b"] = jnp.concatenate([dcw_a[3:], dcw_b[3:]], axis=1)
    g["w_up_a"] = _mm(h2, dup_a, mode="tn", out_dtype=BF16, name="dw_up_a", tn=D_FF // 2)
    g["w_up_b"] = _mm(h2, dup_b, mode="tn", out_dtype=BF16, name="dw_up_b", tn=D_FF // 2)
    dh2 = _mm_nt_sum([(dup_a, w["w_up_a"]), (dup_b, w["w_up_b"])], out_dtype=BF16, name="dh2")
    dx1, dx1_b, g["norm_ffn_g"] = _rms_bwd(x1, w["norm_ffn_g"], dh2, dx2, name="rms_ffn_bwd")
    dep = on_grads_ffn(g) if on_grads_ffn is not None else None

    g["w_out"] = _mm(merged, dx1_b, mode="tn", out_dtype=BF16, name="dw_out")
    dmerged = _mm(dx1_b, w["w_out"], mode="nt", out_dtype=BF16, name="dmerged", dep=dep)
    dzg, dya, dyb = _merge_bwd(dmerged, zg, ya, yb, name="merge_bwd")
    g["w_branch_a"] = _mm(a, dya, mode="tn", out_dtype=BF16, name="dw_branch_a")
    g["w_branch_b"] = _mm(b, dyb, mode="tn", out_dtype=BF16, name="dw_branch_b")
    da = _mm(dya, w["w_branch_a"], mode="nt", out_dtype=BF16, name="da")
    db = _mm(dyb, w["w_branch_b"], mode="nt", out_dtype=BF16, name="db")

    dz_uv, g["gmlp_w_s"], dbs_t, g["gmlp_ln_g"], g["gmlp_ln_b"] = _gmlp_bwd(
        z_uv, da, w["gmlp_ln_g"], w["gmlp_ln_b"], w["gmlp_w_s"], w["gmlp_b_s_t"], name="gmlp_bwd")
    g["gmlp_b_s"] = dbs_t[:, :GMLP_GROUPS].T

    dq, dk, dv, dcum_b = _attn_bwd_t(qkv, db, o_t, lse, cum_b, cum_r, name="attn_bwd")
    dcum_t = jnp.pad(dcum_b[..., 0].reshape(FOX_HEADS, s), ((0, LANES - FOX_HEADS), (0, 0)))
    df, g["b_f"] = _fox_dlogit(dcum_t, f, w["b_f"], name="fox_dlogit")
    dqkv = jnp.concatenate([dq, dk, dv], axis=1)

    g["w_uv"] = _mm(h, dz_uv, mode="tn", out_dtype=BF16, name="dw_uv")
    g["w_qkv"] = _mm(h, dqkv, mode="tn", out_dtype=BF16, name="dw_qkv")
    g["w_f"] = _mm(h, df, mode="tn", out_dtype=BF16, name="dw_f")
    g["w_g"] = _mm(h, dzg, mode="tn", out_dtype=BF16, name="dw_g")
    dep = on_grads_mix(g) if on_grads_mix is not None else None
    dh = _mm_nt_sum([(dz_uv, w["w_uv"]), (dqkv, w["w_qkv"]), (df, w["w_f"]), (dzg, w["w_g"])],
                    out_dtype=BF16, name="dh", dep=dep)
    dx0, _, g["norm_mix_g"] = _rms_bwd(x, w["norm_mix_g"], dh, dx1, name="rms_mix_bwd")
    return loss, dx0, g


def _coords():
    return lax.axis_index("x"), lax.axis_index("y"), lax.axis_index("c")


def _other_chips(x, y):
    return [(1 - x, y), (x, 1 - y), (1 - x, 1 - y)]


def _remote(src, dst, send_sem, recv_sem, dev):
    return pltpu.make_async_remote_copy(src_ref=src, dst_ref=dst, send_sem=send_sem, recv_sem=recv_sem,
                                        device_id=dev, device_id_type=MESH)


_ANY = pl.BlockSpec(memory_space=pl.ANY)


def _gather_weights(halved, whole, *, name):
    nh, n = len(halved), len(halved) + len(whole)
    arrays = list(halved) + list(whole)

    def body(*refs):
        ins, outs = refs[:n], refs[n:2 * n]
        send_sems, recv_sems = refs[2 * n:]
        x, y, c = _coords()
        me, sib = 2 * x + y, (x, y, 1 - c)
        chips = _other_chips(x, y)

        def half(i, which):
            h = ins[i].shape[0] // 2
            return pl.ds(pl.multiple_of(which * h, 16), h)

        sends = []
        for i in range(n):
            src, dst = (ins[i].at[half(i, c)], outs[i].at[me, half(i, c)]) if i < nh else (ins[i], outs[i].at[me])
            for k, (cx, cy) in enumerate(chips):
                cp = _remote(src, dst, send_sems.at[i, k], recv_sems.at[i, k], (cx, cy, c))
                cp.start()
                sends.append(cp)
        for i in range(n):
            for k, (cx, cy) in enumerate(chips):
                got = outs[i].at[2 * cx + cy, half(i, c)] if i < nh else outs[i].at[2 * cx + cy]
                _remote(got, got, send_sems.at[i, k], recv_sems.at[i, k], sib).wait_recv()
                if i < nh:
                    cp = _remote(got, got, send_sems.at[i, 3 + k], recv_sems.at[i, 3 + k], sib)
                    cp.start()
                    sends.append(cp)
        for i in range(nh):
            for k, (cx, cy) in enumerate(chips):
                got = outs[i].at[2 * cx + cy, half(i, 1 - c)]
                _remote(got, got, send_sems.at[i, 3 + k], recv_sems.at[i, 3 + k], sib).wait_recv()
        for cp in sends:
            cp.wait_send()

    outs = pl.pallas_call(
        body, name=name, in_specs=[_ANY] * n, out_specs=[_ANY] * n,
        out_shape=[jax.ShapeDtypeStruct((N_CHIPS,) + a.shape, a.dtype) for a in arrays],
        scratch_shapes=[pltpu.SemaphoreType.DMA((n, 6)), pltpu.SemaphoreType.DMA((n, 6))],
        compiler_params=_params(),
    )(*arrays)
    chip = 2 * lax.axis_index("x") + lax.axis_index("y")
    return [lax.dynamic_update_index_in_dim(o, a, chip, 0) for o, a in zip(outs, arrays)]


def _pair_exchange(gs, *, name):
    n = len(gs)

    def body(*refs):
        ins, outs = refs[:n], refs[n:2 * n]
        send_sems, recv_sems = refs[2 * n:]
        x, y, c = _coords()
        copies = []
        for i in range(n):
            for j in range(N_CHIPS):
                cp = _remote(ins[i].at[j, 1 - c], outs[i].at[j], send_sems.at[i, j], recv_sems.at[i, j], (x, y, 1 - c))
                cp.start()
                copies.append(cp)
        for cp in copies:
            cp.wait()

    return pl.pallas_call(
        body, name=name, in_specs=[_ANY] * n, out_specs=[_ANY] * n,
        out_shape=[jax.ShapeDtypeStruct((N_CHIPS,) + a.shape[2:], a.dtype) for a in gs],
        scratch_shapes=[pltpu.SemaphoreType.DMA((n, N_CHIPS)), pltpu.SemaphoreType.DMA((n, N_CHIPS))],
        compiler_params=_params(),
    )(*gs)


def _chip_exchange(ss, *, name):
    n = len(ss)

    def body(*refs):
        ins, outs = refs[:n], refs[n:2 * n]
        send_sems, recv_sems = refs[2 * n:]
        x, y, c = _coords()
        me = 2 * x + y
        chips = _other_chips(x, y)
        sends = []
        for i in range(n):
            for k, (cx, cy) in enumerate(chips):
                cp = _remote(ins[i].at[2 * cx + cy], outs[i].at[me], send_sems.at[i, k], recv_sems.at[i, k], (cx, cy, c))
                cp.start()
                sends.append(cp)
        for i in range(n):
            for k, (cx, cy) in enumerate(chips):
                got = outs[i].at[2 * cx + cy]
                _remote(got, got, send_sems.at[i, k], recv_sems.at[i, k], (cx, cy, c)).wait_recv()
        for cp in sends:
            cp.wait_send()

    return pl.pallas_call(
        body, name=name, in_specs=[_ANY] * n, out_specs=[_ANY] * n,
        out_shape=[jax.ShapeDtypeStruct(a.shape, a.dtype) for a in ss],
        scratch_shapes=[pltpu.SemaphoreType.DMA((n, 3)), pltpu.SemaphoreType.DMA((n, 3))],
        compiler_params=_params(),
    )(*ss)


def _pair_share(hs, *, name):
    n = len(hs)

    def body(*refs):
        ins, outs = refs[:n], refs[n:2 * n]
        send_sems, recv_sems = refs[2 * n:]
        x, y, c = _coords()
        copies = []
        for i in range(n):
            cp = _remote(ins[i], outs[i], send_sems.at[i], recv_sems.at[i], (x, y, 1 - c))
            cp.start()
            copies.append(cp)
        for cp in copies:
            cp.wait()

    return pl.pallas_call(
        body, name=name, in_specs=[_ANY] * n, out_specs=[_ANY] * n,
        out_shape=[jax.ShapeDtypeStruct(a.shape, a.dtype) for a in hs],
        scratch_shapes=[pltpu.SemaphoreType.DMA((n,)), pltpu.SemaphoreType.DMA((n,))],
        compiler_params=_params(),
    )(*hs)


def _all_exchange(vec, *, name):
    def body(v_ref, o_ref, send_sems, recv_sems, local_sem):
        x, y, c = _coords()
        me = 4 * x + 2 * y + c
        local = pltpu.make_async_copy(v_ref, o_ref.at[me], local_sem)
        local.start()
        copies = []
        k = 0
        for dx in (0, 1):
            for dy in (0, 1):
                for dc in (0, 1):
                    if dx or dy or dc:
                        peer = (1 - x if dx else x, 1 - y if dy else y, 1 - c if dc else c)
                        cp = _remote(v_ref, o_ref.at[me], send_sems.at[k], recv_sems.at[k], peer)
                        cp.start()
                        copies.append(cp)
                        k += 1
        for cp in copies:
            cp.wait()
        local.wait()

    return pl.pallas_call(
        body, name=name, in_specs=[_ANY], out_specs=_ANY,
        out_shape=jax.ShapeDtypeStruct((8,) + vec.shape, vec.dtype),
        scratch_shapes=[pltpu.SemaphoreType.DMA((7,)), pltpu.SemaphoreType.DMA((7,)), pltpu.SemaphoreType.DMA(())],
        compiler_params=_params(),
    )(vec)


_HBM = pl.BlockSpec(memory_space=pltpu.HBM)
_SEM = pl.BlockSpec(memory_space=pltpu.SEMAPHORE)
_EFFECT = pltpu.SideEffectType.DATAFLOW_SIDE_EFFECTING


def _copies_start(srcs, lands, plan, n_copies, *, name, after=()):
    ns, n = len(srcs), len(srcs) + len(lands)
    na = len(after)

    def body(*refs):
        send_sems, recv_sems = refs[n + na], refs[n + na + 1]
        token = refs[-1]
        for k, (src, dst, dev) in enumerate(plan(refs[:ns], refs[ns:n])):
            _remote(src, dst, send_sems.at[k], recv_sems.at[k], dev).start()
        token[...] = jnp.zeros_like(token)

    arrays = list(srcs) + list(lands)
    outs = pl.pallas_call(
        body, name=name,
        out_shape=(pltpu.SemaphoreType.DMA((n_copies,)), pltpu.SemaphoreType.DMA((n_copies,)),
                   *[pltpu.HBM(a.shape, a.dtype) for a in arrays], jax.ShapeDtypeStruct((8, LANES), F32)),
        in_specs=[_HBM] * n + [_ANY] * na,
        out_specs=(_SEM, _SEM, *[_HBM] * n, pl.BlockSpec(memory_space=pltpu.VMEM)),
        input_output_aliases={i: 2 + i for i in range(n)},
        compiler_params=pltpu.CompilerParams(has_side_effects=_EFFECT),
    )(*[pltpu.with_memory_space_constraint(a, pltpu.HBM) for a in arrays], *after)
    return outs[0], outs[1], list(outs[2:2 + ns]), list(outs[2 + ns:2 + n]), outs[-1]


def _copies_wait(send_sems, recv_sems, srcs, lands, plan, first, after, *, name):
    ns, n = len(srcs), len(srcs) + len(lands)

    def body(*refs):
        send, recv = refs[n], refs[n + 1]
        for k, (src, dst, dev) in enumerate(plan(refs[:ns], refs[ns:n])):
            cp = _remote(src, dst, send.at[first + k], recv.at[first + k], dev)
            cp.wait_send()
            cp.wait_recv()

    arrays = list(srcs) + list(lands)
    outs = pl.pallas_call(
        body, name=name, out_shape=tuple(pltpu.HBM(a.shape, a.dtype) for a in arrays),
        in_specs=[_HBM] * n + [_SEM, _SEM] + [_ANY] * len(after), out_specs=tuple([_HBM] * n),
        input_output_aliases={i: i for i in range(n)},
        compiler_params=pltpu.CompilerParams(has_side_effects=_EFFECT),
    )(*arrays, send_sems, recv_sems, *after)
    return list(outs[:ns]), list(outs[ns:])


def _gather_plan(halved):
    def plan(srcs, lands):
        x, y, c = _coords()
        me = 2 * x + y
        out = []
        for i, (src, land) in enumerate(zip(srcs, lands)):
            if halved[i]:
                h = src.shape[0] // 2
                rows = pl.ds(pl.multiple_of(c * h, 16), h)
                src, dst = src.at[rows], land.at[me, rows]
            else:
                dst = land.at[me]
            out += [(src, dst, (cx, cy, c)) for cx, cy in _other_chips(x, y)]
        return out
    return plan


def _forward_halves(lands, *, name):
    n = len(lands)

    def body(*refs):
        ins, outs = refs[:n], refs[n:2 * n]
        send_sems, recv_sems = refs[2 * n:]
        x, y, c = _coords()
        copies = []
        for i in range(n):
            h = ins[i].shape[1] // 2
            rows = pl.ds(pl.multiple_of(c * h, 16), h)
            for k, (cx, cy) in enumerate(_other_chips(x, y)):
                cp = _remote(ins[i].at[2 * cx + cy, rows], outs[i].at[2 * cx + cy, rows],
                             send_sems.at[i, k], recv_sems.at[i, k], (x, y, 1 - c))
                cp.start()
                copies.append(cp)
        for cp in copies:
            cp.wait()

    return pl.pallas_call(
        body, name=name, in_specs=[_ANY] * n, out_specs=[_ANY] * n,
        out_shape=[jax.ShapeDtypeStruct(a.shape, a.dtype) for a in lands],
        input_output_aliases={i: i for i in range(n)},
        scratch_shapes=[pltpu.SemaphoreType.DMA((n, 3)), pltpu.SemaphoreType.DMA((n, 3))],
        compiler_params=_params(),
    )(*lands)


def _all_plan(srcs, lands):
    x, y, c = _coords()
    me = 4 * x + 2 * y + c
    out = []
    for src, land in zip(srcs, lands):
        for dx in (0, 1):
            for dy in (0, 1):
                for dc in (0, 1):
                    if dx or dy or dc:
                        out.append((src, land.at[me], (1 - x if dx else x, 1 - y if dy else y, 1 - c if dc else c)))
    return out


def _chip_plan(srcs, lands):
    x, y, c = _coords()
    me = 2 * x + y
    out = []
    for src, land in zip(srcs, lands):
        out += [(src.at[2 * cx + cy], land.at[me], (cx, cy, c)) for cx, cy in _other_chips(x, y)]
    return out


ROW_BLOCK_BYTES = 2 * 1024 * 1024


def _rtile(r, pref, mult, row_bytes=None):
    if row_bytes is not None:
        pref = max(pref, ROW_BLOCK_BYTES // row_bytes)
    t = (min(r, pref) // mult) * mult
    while t >= mult:
        if r % t == 0:
            return t
        t -= mult
    return r


def _pair_add(g, recv, core, *, name):
    _, _, r2, cols = g.shape
    tr = _rtile(r2, 256, 16, row_bytes=2 * cols)

    def body(c_ref, g_ref, r_ref, o_ref):
        o_ref[...] = (g_ref[...].astype(F32) + r_ref[...].astype(F32)).astype(o_ref.dtype)

    blk = pl.BlockSpec((None, tr, cols), lambda j, i, c_ref: (j, i, 0))
    return pl.pallas_call(
        body, name=name,
        grid_spec=pltpu.PrefetchScalarGridSpec(
            num_scalar_prefetch=1, grid=(N_CHIPS, r2 // tr),
            in_specs=[pl.BlockSpec((None, None, tr, cols), lambda j, i, c_ref: (j, c_ref[0], i, 0)), blk],
            out_specs=blk),
        out_shape=jax.ShapeDtypeStruct(recv.shape, recv.dtype), compiler_params=_params(),
    )(core, g, recv)


def _sum_slots(a, out_dtype, *, name):
    n, r, cols = a.shape
    tr = _rtile(r, 256, 16)

    def body(a_ref, o_ref):
        acc = a_ref[0].astype(F32)
        for j in range(1, n):
            acc = acc + a_ref[j].astype(F32)
        o_ref[...] = acc.astype(o_ref.dtype)

    return pl.pallas_call(
        body, name=name, grid=(r // tr,),
        in_specs=[pl.BlockSpec((n, tr, cols), lambda i: (0, i, 0))],
        out_specs=pl.BlockSpec((tr, cols), lambda i: (i, 0)),
        out_shape=jax.ShapeDtypeStruct((r, cols), out_dtype), compiler_params=_params(),
    )(a)


def _chip_sum(own, recv, chip, *, name):
    _, r2, cols = own.shape
    tr = _rtile(r2, 256, 16, row_bytes=2 * cols)

    def body(chip_ref, own_ref, *rest):
        o_ref = rest[-1]
        acc = None
        for j in range(N_CHIPS):
            term = jnp.where(chip_ref[0] == j, own_ref[...], rest[j][...]).astype(F32)
            acc = term if acc is None else acc + term
        o_ref[...] = acc

    def slot(j):
        return pl.BlockSpec((None, tr, cols),
                            lambda i, chip_ref: (jnp.where(chip_ref[0] == j, (j + 1) % N_CHIPS, j), i, 0))

    return pl.pallas_call(
        body, name=name,
        grid_spec=pltpu.PrefetchScalarGridSpec(
            num_scalar_prefetch=1, grid=(r2 // tr,),
            in_specs=[pl.BlockSpec((None, tr, cols), lambda i, chip_ref: (chip_ref[0], i, 0))]
                     + [slot(j) for j in range(N_CHIPS)],
            out_specs=pl.BlockSpec((tr, cols), lambda i, chip_ref: (i, 0))),
        out_shape=jax.ShapeDtypeStruct((r2, cols), F32), compiler_params=_params(),
    )(chip, own, *([recv] * N_CHIPS))


def _adam_update(w, gv, m, v):
    c1 = 1.0 / (1.0 - ADAM_B1 ** ADAM_STEP)
    c2 = 1.0 / (1.0 - ADAM_B2 ** ADAM_STEP)
    nm = ADAM_B1 * m + (1.0 - ADAM_B1) * gv
    nv = ADAM_B2 * v + (1.0 - ADAM_B2) * gv * gv
    return -ADAM_LR * ((nm * c1) / (jnp.sqrt(nv * c2) + ADAM_EPS) + ADAM_WD * w), nm, nv


def _adamw_halves(w, g_mine, g_other, m, v, core, *, name):
    r, cols = w.shape
    r2 = r // 2
    tr = _rtile(r2, 256, 8, row_bytes=4 * cols)
    nt = r2 // tr

    def body(core_ref, w_ref, gm_ref, go_ref, m_ref, v_ref, g_ref, d_ref, nm_ref, nv_ref):
        gv = jnp.where(pl.program_id(0) == core_ref[0], gm_ref[...], go_ref[...])
        g_ref[...] = gv
        d_ref[...], nm_ref[...], nv_ref[...] = _adam_update(w_ref[...], gv, m_ref[...], v_ref[...])

    full = pl.BlockSpec((tr, cols), lambda hf, i, core_ref: (hf * nt + i, 0))
    half = pl.BlockSpec((tr, cols), lambda hf, i, core_ref: (i, 0))
    shape = jax.ShapeDtypeStruct((r, cols), F32)
    return pl.pallas_call(
        body, name=name,
        grid_spec=pltpu.PrefetchScalarGridSpec(
            num_scalar_prefetch=1, grid=(2, nt), in_specs=[full, half, half, full, full], out_specs=[full] * 4),
        out_shape=[shape] * 4, compiler_params=_params(),
    )(core, w, g_mine, g_other, m, v)


def _adamw_split_rows(w, g_mine, g_other, m, v, core, *, name, tc=256):
    r, cols = w.shape
    r2 = g_mine.shape[0]
    tc = _tile(cols, tc)

    def body(core_ref, w_ref, gm_ref, go_ref, m_ref, v_ref, g_ref, d_ref, nm_ref, nv_ref):
        mine_first = core_ref[0] == 0
        for lo, hi, first in ((0, r2, True), (r2, r, False)):
            n = hi - lo
            gm, go = gm_ref[0:n, :], go_ref[0:n, :]
            gv = jnp.where(mine_first, gm, go) if first else jnp.where(mine_first, go, gm)
            g_ref[lo:hi, :] = gv
            d_ref[lo:hi, :], nm_ref[lo:hi, :], nv_ref[lo:hi, :] = _adam_update(
                w_ref[lo:hi, :], gv, m_ref[lo:hi, :], v_ref[lo:hi, :])

    full = pl.BlockSpec((r, tc), lambda j, core_ref: (0, j))
    half = pl.BlockSpec((r2, tc), lambda j, core_ref: (0, j))
    shape = jax.ShapeDtypeStruct((r, cols), F32)
    return pl.pallas_call(
        body, name=name,
        grid_spec=pltpu.PrefetchScalarGridSpec(
            num_scalar_prefetch=1, grid=(cols // tc,), in_specs=[full, half, half, full, full],
            out_specs=[full] * 4),
        out_shape=[shape] * 4, compiler_params=_params(),
    )(core, w, g_mine, g_other, m, v)


def _adamw(w, g, m, v, *, name, rows=256):
    r, cols = w.shape
    tr = _rtile(r, rows, 8)

    def body(w_ref, g_ref, m_ref, v_ref, d_ref, nm_ref, nv_ref):
        d_ref[...], nm_ref[...], nv_ref[...] = _adam_update(w_ref[...], g_ref[...], m_ref[...], v_ref[...])

    blk = pl.BlockSpec((tr, cols), lambda i: (i, 0))
    shape = jax.ShapeDtypeStruct((r, cols), F32)
    return pl.pallas_call(
        body, name=name, grid=(r // tr,), in_specs=[blk] * 4, out_specs=[blk] * 3,
        out_shape=[shape] * 3, compiler_params=_params(),
    )(w, g, m, v)


_BIG = (("w_in", 1), ("w_branch_a", 0), ("w_branch_b", 0), ("w_out", 0), ("w_up", 1), ("w_down", 0),
        ("w_ple", 1), ("w_ple_gate", 0))
_SMALL = ("b_f", "gmlp_ln_g", "gmlp_ln_b", "gmlp_w_s", "gmlp_b_s", "norm_ffn_g", "conv_b", "norm_ple_g",
          "norm_final_g", "norm_mix_g")
_WEIGHTS = ("norm_mix_g", "w_in", "b_f", "gmlp_ln_g", "gmlp_ln_b", "gmlp_w_s", "gmlp_b_s", "w_branch_a",
            "w_branch_b", "w_out", "norm_ffn_g", "w_up", "conv_w", "conv_b", "w_down", "norm_ple_g", "w_ple",
            "w_ple_gate", "norm_final_g")
_PACK_ROWS = 8


def _pack(arrays):
    parts = []
    for a in arrays:
        flat = a.reshape(-1)
        unit = _PACK_ROWS * LANES
        flat = jnp.pad(flat, (0, (-flat.shape[0]) % unit))
        parts.append(flat.reshape(-1, LANES))
    return jnp.concatenate(parts, axis=0)


def _unpack(packed, shapes):
    out, row = [], 0
    for shp in shapes:
        size = math.prod(shp)
        rows = -(-size // (_PACK_ROWS * LANES)) * _PACK_ROWS
        out.append(packed[row:row + rows].reshape(-1)[:size].reshape(shp))
        row += rows
    return out


def _take_cols(parts, lo, hi):
    out, start = [], 0
    for a in parts:
        width = a.shape[1]
        a0, a1 = max(lo, start) - start, min(hi, start + width) - start
        if a1 > a0:
            out.append(a if (a0, a1) == (0, width) else a[:, a0:a1])
        start += width
    return out[0] if len(out) == 1 else jnp.concatenate(out, axis=1)


def _take_rows(parts, lo, hi):
    out, start = [], 0
    for a in parts:
        height = a.shape[0]
        a0, a1 = max(lo, start) - start, min(hi, start + height) - start
        if a1 > a0:
            out.append(a if (a0, a1) == (0, height) else a[a0:a1])
        start += height
    return out[0] if len(out) == 1 else jnp.concatenate(out, axis=0)


def _assemble(gathered, axis):
    n, r, cols = gathered.shape
    if axis == 0:
        return gathered.reshape(n * r, cols)
    return _take_cols([gathered[j] for j in range(n)], 0, n * cols)


def _to_chunks(parts, axis):
    rows, total = parts[0].shape[0], sum(a.shape[1] for a in parts)
    if axis == 0:
        r, cols = rows // N_CHIPS, total
        chunks = _take_cols(parts, 0, total).reshape(N_CHIPS, r, cols)
    else:
        r, cols = rows, total // N_CHIPS
        chunks = jnp.stack([_take_cols(parts, j * cols, (j + 1) * cols) for j in range(N_CHIPS)])
    return chunks.reshape(N_CHIPS, 2, r // 2, cols)


def kernel(x, p, norm_mix_g, w_in, b_f, gmlp_ln_g, gmlp_ln_b, gmlp_w_s, gmlp_b_s, w_branch_a, w_branch_b, w_out, norm_ffn_g, w_up, conv_w, conv_b, w_down, norm_ple_g, w_ple, w_ple_gate, norm_final_g, loss_target, m_norm_mix_g, m_w_in, m_b_f, m_gmlp_ln_g, m_gmlp_ln_b, m_gmlp_w_s, m_gmlp_b_s, m_w_branch_a, m_w_branch_b, m_w_out, m_norm_ffn_g, m_w_up, m_conv_w, m_conv_b, m_w_down, m_norm_ple_g, m_w_ple, m_w_ple_gate, m_norm_final_g, v_norm_mix_g, v_w_in, v_b_f, v_gmlp_ln_g, v_gmlp_ln_b, v_gmlp_w_s, v_gmlp_b_s, v_w_branch_a, v_w_branch_b, v_w_out, v_norm_ffn_g, v_w_up, v_conv_w, v_conv_b, v_w_down, v_norm_ple_g, v_w_ple, v_w_ple_gate, v_norm_final_g):
    args = dict(locals())
    wt = {n: args[n] for n in _WEIGHTS}
    mom = {n: args["m_" + n] for n in _WEIGHTS}
    var = {n: args["v_" + n] for n in _WEIGHTS}
    chip = 2 * lax.axis_index("x") + lax.axis_index("y")
    core = lax.axis_index("c").astype(jnp.int32).reshape(1)

    chip1 = chip.astype(jnp.int32).reshape(1)
    device = 2 * chip + lax.axis_index("c")
    axis_of = dict(_BIG)
    names = [n for n, _ in _BIG]
    put_mine = lambda land, mine: lax.dynamic_update_index_in_dim(land, mine, chip, 0)

    shard_in = w_in[0].astype(BF16)
    sems_in = _copies_start([shard_in], [lax.empty((N_CHIPS,) + shard_in.shape, BF16)], _gather_plan([True]), 3,
                            name="gather_start_in")
    shards = [wt[n][0].astype(BF16) for n in names[1:]] + [conv_w[0]]
    halved = [True] * len(names[1:]) + [False]
    lands = [lax.empty((N_CHIPS,) + a.shape, a.dtype) for a in shards]
    send_sems, recv_sems, srcs, lands, rest_token = _copies_start(
        shards, lands, _gather_plan(halved), 3 * len(shards), name="gather_start_rest", after=[sems_in[4]])
    o1 = 2 * GMLP_WIDTH
    o2 = o1 + 3 * FOX_WIDTH
    o3 = o2 + FOX_HEADS
    fpad = ((0, 0), (0, LANES - FOX_HEADS))
    w = {
        "conv_b": conv_b, "norm_mix_g": norm_mix_g, "norm_ffn_g": norm_ffn_g, "norm_ple_g": norm_ple_g,
        "norm_final_g": norm_final_g.reshape(1, D_MODEL), "b_f": jnp.pad(b_f, fpad),
        "gmlp_ln_g": gmlp_ln_g, "gmlp_ln_b": gmlp_ln_b, "gmlp_w_s": gmlp_w_s[0],
        "gmlp_b_s_t": jnp.pad(gmlp_b_s[0].T, ((0, 0), (0, LANES - GMLP_GROUPS))),
        "first_dep": rest_token,
    }

    def get_w_in(after):
        _, got = _copies_wait(sems_in[0], sems_in[1], sems_in[2], sems_in[3], _gather_plan([True]), 0, [after],
                              name="gather_wait_in")
        got = _forward_halves(got, name="gather_forward_in")
        slots = put_mine(got[0], shard_in)
        slots = [slots[j] for j in range(N_CHIPS)]
        return {"w_uv": _take_cols(slots, 0, o1), "w_qkv": _take_cols(slots, o1, o2),
                "w_f": jnp.pad(_take_cols(slots, o2, o3), fpad), "w_g": _take_cols(slots, o3, o3 + 2 * D_MODEL)}

    def get_w_rest(after):
        _, got = _copies_wait(send_sems, recv_sems, srcs, lands, _gather_plan(halved), 0, [after],
                              name="gather_wait_rest")
        got = list(_forward_halves(got[:-1], name="gather_forward_rest")) + got[-1:]
        slots = {n: put_mine(got[i], shards[i]) for i, n in enumerate(names[1:])}
        full = {n: _assemble(slots[n], axis_of[n]) for n in names[1:] if n != "w_up"}
        up = [slots["w_up"][j] for j in range(N_CHIPS)]
        return {"w_branch_a": full["w_branch_a"], "w_branch_b": full["w_branch_b"], "w_out": full["w_out"],
                "w_up_a": _take_cols(up, 0, D_FF), "w_up_b": _take_cols(up, D_FF, 2 * D_FF),
                "w_down": full["w_down"], "w_ple": full["w_ple"], "w_ple_gate": full["w_ple_gate"],
                "conv_w": _assemble(put_mine(got[-1], shards[-1]), 1)}

    grads, delta, new_m, new_v = {}, {}, {}, {}
    pending = {}

    def to_chunks(n, gr):
        return _to_chunks(gr if isinstance(gr, list) else [gr], axis_of[n])

    def reduce_start(group, gfull, tag):
        chunks = [to_chunks(n, gfull[n]) for n in group]
        from_sibling = _pair_exchange(chunks, name="grad_pair_exchange_" + tag)
        pair_sums = [_pair_add(chunks[i], from_sibling[i], core, name="grad_pair_add_" + n) for i, n in enumerate(group)]
        empty = [lax.empty(a.shape, a.dtype) for a in pair_sums]
        ssem, rsem, own, recv, token = _copies_start(pair_sums, empty, _chip_plan, 3 * len(group),
                                                     name="grad_chip_start_" + tag)
        pending[tag] = (ssem, rsem, own, recv)
        return token

    def reduce_finish(group, tag, after):
        ssem, rsem, own, recv = pending[tag]
        own, recv = _copies_wait(ssem, rsem, own, recv, _chip_plan, 0, after, name="grad_chip_wait_" + tag)
        halves = [_chip_sum(own[i], recv[i], chip1, name="grad_chip_sum_" + n) for i, n in enumerate(group)]
        other_halves = _pair_share(halves, name="grad_pair_share_" + tag)
        for i, n in enumerate(group):
            shp = wt[n].shape
            outs = _adamw_halves(wt[n].reshape(shp[-2:]), halves[i], other_halves[i], mom[n].reshape(shp[-2:]),
                                 var[n].reshape(shp[-2:]), core, name="adamw_" + n)
            grads[n], delta[n], new_m[n], new_v[n] = (o.reshape(shp) for o in outs)
        return new_v[group[-1]]

    ffn_group = ("w_up", "w_down", "w_ple", "w_ple_gate")
    mix_group = ("w_in", "w_branch_a", "w_branch_b", "w_out")

    def on_grads_ffn(g):
        gfull = dict(g)
        gfull["w_up"] = [g["w_up_a"], g["w_up_b"]]
        return reduce_start(ffn_group, gfull, "ffn")

    def on_grads_mix(g):
        early = [g[n] if n != "b_f" else g[n][:, :FOX_HEADS] for n in _SMALL[:-1]] + [g["conv_w"]]
        vec = _pack(early)
        ssem, rsem, own, recv, small_token = _copies_start(
            [vec], [lax.empty((8,) + vec.shape, F32)], _all_plan, 7, name="small_start")
        pending["small"] = (ssem, rsem, own, recv)
        gfull = dict(g)
        gfull["w_in"] = [g["w_uv"], g["w_qkv"], g["w_f"][:, :FOX_HEADS], g["w_g"]]
        token = reduce_start(mix_group, gfull, "mix")
        pending["ffn_done"] = reduce_finish(ffn_group, "ffn", [token])
        return token + small_token

    loss, grad_x, g = _device_step(x[0], p[0, 0], loss_target[0], w, get_w_in, get_w_rest, on_grads_ffn, on_grads_mix)

    mix_done = reduce_finish(mix_group, "mix", [grad_x, pending["ffn_done"]])
    ssem, rsem, own, recv = pending["small"]
    own, recv = _copies_wait(ssem, rsem, own, recv, _all_plan, 0, [mix_done], name="small_wait")
    vec_early = _sum_slots(lax.dynamic_update_index_in_dim(recv[0], own[0], device, 0), F32, name="small_sum")
    vec_late = _pack([g["norm_mix_g"]])
    vec_late = _sum_slots(_all_exchange(vec_late, name="small_exchange_late"), F32, name="small_sum_late")
    early_rows = _pack([wt[n] for n in _SMALL[:-1]]).shape[0]
    vec = jnp.concatenate([vec_early[:early_rows], vec_late], axis=0)
    for n, a in zip(_SMALL, _unpack(vec, [wt[n].shape for n in _SMALL])):
        grads[n] = a
    conv_w_grad = _unpack(vec_early[early_rows:], [(3, 2 * D_FF)])[0]
    grads["conv_w"] = lax.dynamic_slice_in_dim(conv_w_grad, chip * conv_w.shape[2], conv_w.shape[2], axis=1).reshape(conv_w.shape)

    shp = conv_w.shape
    outs = _adamw(conv_w.reshape(shp[-2:]), grads["conv_w"].reshape(shp[-2:]), m_conv_w.reshape(shp[-2:]),
                  v_conv_w.reshape(shp[-2:]), name="adamw_conv_w")
    delta["conv_w"], new_m["conv_w"], new_v["conv_w"] = (o.reshape(shp) for o in outs)
    outs = _adamw(_pack([wt[n] for n in _SMALL]), vec, _pack([mom[n] for n in _SMALL]),
                  _pack([var[n] for n in _SMALL]), name="adamw_small", rows=2048)
    for d, o in zip((delta, new_m, new_v), outs):
        for n, a in zip(_SMALL, _unpack(o, [wt[n].shape for n in _SMALL])):
            d[n] = a

    total_loss = lax.psum(loss[0, 0], ("x", "y", "c"))
    return (total_loss, grad_x.reshape(x.shape), *[grads[n] for n in _WEIGHTS], *[delta[n] for n in _WEIGHTS],
            *[new_m[n] for n in _WEIGHTS], *[new_v[n] for n in _WEIGHTS])
```

```python
import functools
import math

import jax
import jax.numpy as jnp
from jax import lax
from jax.experimental import pallas as pl
from jax.experimental.pallas import tpu as pltpu

F32 = jnp.float32
BF16 = jnp.bfloat16

D_MODEL = 1024
EPS = 1e-6
CHUNK = 64
GMLP_GROUPS = 8
GMLP_BLOCK = 128
GMLP_WIDTH = 1024
FOX_HEADS = 16
FOX_HEAD_DIM = 64
FOX_WIDTH = 1024
HEAD_PAIRS = FOX_HEADS // 2
ATT_BLOCK = 128
D_FF = 2816
PLE_DIM = 256
LANES = 128
BF16_TILE_ROWS = 16
N_CHIPS = 4

ADAM_LR = 0.001
ADAM_B1 = 0.9
ADAM_B2 = 0.999
ADAM_EPS = 1e-08
ADAM_WD = 0.01
ADAM_STEP = 10

VMEM_LIMIT = 56 * 1024 * 1024
MESH = pl.DeviceIdType.MESH

_NN = (((1,), (0,)), ((), ()))
_NT = (((1,), (1,)), ((), ()))
_TN = (((0,), (0,)), ((), ()))


def _params(**kw):
    return pltpu.CompilerParams(vmem_limit_bytes=VMEM_LIMIT, **kw)


def _tile(dim, pref):
    if dim <= pref:
        return dim
    t = (pref // LANES) * LANES
    while t >= LANES:
        if dim % t == 0:
            return t
        t -= LANES
    return dim


def _dot(a, b, dn):
    return lax.dot_general(a.astype(BF16), b.astype(BF16), dn, preferred_element_type=F32)


def _gelu(x):
    c = math.sqrt(2.0 / math.pi)
    t = jnp.tanh(c * (x + 0.044715 * x * x * x))
    return 0.5 * x * (1.0 + t)


def _gelu_and_grad(x):
    c = math.sqrt(2.0 / math.pi)
    x2 = x * x
    t = jnp.tanh(c * (x + 0.044715 * x2 * x))
    g = 0.5 * x * (1.0 + t)
    dg = 0.5 * (1.0 + t) + 0.5 * x * (1.0 - t * t) * c * (1.0 + 3.0 * 0.044715 * x2)
    return g, dg


def _sigmoid(x):
    return 1.0 / (1.0 + jnp.exp(-x))


def _mm(a, b, *, mode, out_dtype, name, add=None, tm=512, tn=512, dep=None):
    if mode == "nn":
        m, k = a.shape
        k2, n = b.shape
    elif mode == "nt":
        m, k = a.shape
        n, k2 = b.shape
    else:
        k, m = a.shape
        k2, n = b.shape
    assert k == k2, (name, a.shape, b.shape)
    tm = _tile(m, tm)
    tn = _tile(n, tn)
    dn = {"nn": _NN, "nt": _NT, "tn": _TN}[mode]

    def body(a_ref, b_ref, *rest):
        o_ref = rest[-1]
        acc = _dot(a_ref[...], b_ref[...], dn)
        if add is not None:
            acc = acc + rest[0][...].astype(F32)
        o_ref[...] = acc.astype(o_ref.dtype)

    a_spec = pl.BlockSpec((k, tm), lambda i, j: (0, i)) if mode == "tn" else pl.BlockSpec((tm, k), lambda i, j: (i, 0))
    b_spec = pl.BlockSpec((tn, k), lambda i, j: (j, 0)) if mode == "nt" else pl.BlockSpec((k, tn), lambda i, j: (0, j))
    o_spec = pl.BlockSpec((tm, tn), lambda i, j: (i, j))
    in_specs = [a_spec, b_spec]
    args = [a, b]
    if add is not None:
        in_specs.append(o_spec)
        args.append(add)
    if dep is not None:
        in_specs.append(pl.BlockSpec(memory_space=pl.ANY))
        args.append(dep)
    return pl.pallas_call(
        body, name=name, grid=(m // tm, n // tn), in_specs=in_specs, out_specs=o_spec,
        out_shape=jax.ShapeDtypeStruct((m, n), out_dtype), compiler_params=_params(),
    )(*args)


def _mm_nt_sum(pairs, *, out_dtype, name, tm=256, dep=None):
    m, n = pairs[0][0].shape[0], pairs[0][1].shape[0]
    tm = _tile(m, tm)
    np_ = len(pairs)

    def body(*refs):
        o_ref = refs[-1] if dep is None else refs[-1]
        acc = None
        for p in range(np_):
            part = _dot(refs[2 * p][...], refs[2 * p + 1][...], _NT)
            acc = part if acc is None else acc + part
        o_ref[...] = acc.astype(o_ref.dtype)

    in_specs, args = [], []
    for a, b in pairs:
        assert a.shape[0] == m and b.shape[0] == n and a.shape[1] == b.shape[1], (name, a.shape, b.shape)
        in_specs += [pl.BlockSpec((tm, a.shape[1]), lambda i: (i, 0)), pl.BlockSpec(b.shape, lambda i: (0, 0))]
        args += [a, b]
    if dep is not None:
        in_specs.append(pl.BlockSpec(memory_space=pl.ANY))
        args.append(dep)
    return pl.pallas_call(
        body, name=name, grid=(m // tm,), in_specs=in_specs, out_specs=pl.BlockSpec((tm, n), lambda i: (i, 0)),
        out_shape=jax.ShapeDtypeStruct((m, n), out_dtype), compiler_params=_params(),
    )(*args)


def _rms_fwd(x, g, *, name, tm=256, dep=None):
    s, d = x.shape
    tm = _tile(s, tm)

    def body(x_ref, g_ref, *rest):
        h_ref = rest[-1]
        xv = x_ref[...]
        r = lax.rsqrt(jnp.mean(xv * xv, axis=-1, keepdims=True) + EPS)
        h_ref[...] = (xv * r * g_ref[...]).astype(h_ref.dtype)

    deps = [] if dep is None else [dep]
    return pl.pallas_call(
        body, name=name, grid=(s // tm,),
        in_specs=[pl.BlockSpec((tm, d), lambda i: (i, 0)), pl.BlockSpec((1, d), lambda i: (0, 0))]
                 + [pl.BlockSpec(memory_space=pl.ANY)] * len(deps),
        out_specs=pl.BlockSpec((tm, d), lambda i: (i, 0)),
        out_shape=jax.ShapeDtypeStruct((s, d), BF16), compiler_params=_params(),
    )(x, g, *deps)


def _rms_bwd(x, g, dh, dres, *, name, tm=256):
    s, d = x.shape
    tm = _tile(s, tm)

    def body(x_ref, g_ref, dh_ref, dres_ref, dx_ref, dxb_ref, dg_ref):
        xv = x_ref[...]
        r = lax.rsqrt(jnp.mean(xv * xv, axis=-1, keepdims=True) + EPS)
        xhat = xv * r
        dhv = dh_ref[...].astype(F32)
        dyg = dhv * g_ref[...]
        dx = dres_ref[...] + r * (dyg - xhat * jnp.mean(dyg * xhat, axis=-1, keepdims=True))
        dx_ref[...] = dx
        dxb_ref[...] = dx.astype(dxb_ref.dtype)

        @pl.when(pl.program_id(0) == 0)
        def _():
            dg_ref[...] = jnp.zeros_like(dg_ref)

        dg_ref[...] += jnp.sum(dhv * xhat, axis=0, keepdims=True)

    row = pl.BlockSpec((tm, d), lambda i: (i, 0))
    vec = pl.BlockSpec((1, d), lambda i: (0, 0))
    return pl.pallas_call(
        body, name=name, grid=(s // tm,), in_specs=[row, vec, row, row], out_specs=[row, row, vec],
        out_shape=[jax.ShapeDtypeStruct((s, d), F32), jax.ShapeDtypeStruct((s, d), BF16),
                   jax.ShapeDtypeStruct((1, d), F32)],
        compiler_params=_params(),
    )(x, g, dh, dres)


def _gmlp_mask():
    t = lax.broadcasted_iota(jnp.int32, (GMLP_BLOCK, GMLP_BLOCK), 0)
    s_ = lax.broadcasted_iota(jnp.int32, (GMLP_BLOCK, GMLP_BLOCK), 1)
    return (s_ // CHUNK) <= (t // CHUNK)


def _gmlp_norm(zv, ln_g, ln_b):
    vv, dvv = _gelu_and_grad(zv)
    mu = jnp.mean(vv, axis=-1, keepdims=True)
    xc = vv - mu
    rstd = lax.rsqrt(jnp.mean(xc * xc, axis=-1, keepdims=True) + EPS)
    vhat = xc * rstd
    return vhat * ln_g + ln_b, vhat, rstd, dvv


def _gmlp_fwd(z_uv, ln_g, ln_b, w_s, b_s_t, *, name):
    s = z_uv.shape[0]
    w = GMLP_WIDTH
    gd = w // GMLP_GROUPS

    def body(z_ref, lg_ref, lb_ref, ws_ref, bs_ref, a_ref):
        u = _gelu(z_ref[:, :w].astype(F32))
        vn, _, _, _ = _gmlp_norm(z_ref[:, w:].astype(F32), lg_ref[...], lb_ref[...])
        mask = _gmlp_mask()
        for g in range(GMLP_GROUPS):
            wm = jnp.where(mask, ws_ref[g], 0.0)
            mixed = _dot(wm, vn[:, g * gd:(g + 1) * gd], _NN) + bs_ref[:, g:g + 1]
            a_ref[:, g * gd:(g + 1) * gd] = (u[:, g * gd:(g + 1) * gd] * mixed).astype(a_ref.dtype)

    full = lambda shape: pl.BlockSpec(shape, lambda i: (0,) * len(shape))
    return pl.pallas_call(
        body, name=name, grid=(s // GMLP_BLOCK,),
        in_specs=[pl.BlockSpec((GMLP_BLOCK, 2 * w), lambda i: (i, 0)), full((1, w)), full((1, w)),
                  full((GMLP_GROUPS, GMLP_BLOCK, GMLP_BLOCK)), full((GMLP_BLOCK, LANES))],
        out_specs=pl.BlockSpec((GMLP_BLOCK, w), lambda i: (i, 0)),
        out_shape=jax.ShapeDtypeStruct((s, w), BF16), compiler_params=_params(),
    )(z_uv, ln_g, ln_b, w_s, b_s_t)


def _gmlp_bwd(z_uv, da, ln_g, ln_b, w_s, b_s_t, *, name):
    s = z_uv.shape[0]
    w = GMLP_WIDTH
    gd = w // GMLP_GROUPS

    def body(z_ref, da_ref, lg_ref, lb_ref, ws_ref, bs_ref, dz_ref, dws_ref, dbs_ref, dlg_ref, dlb_ref):
        @pl.when(pl.program_id(0) == 0)
        def _():
            dws_ref[...] = jnp.zeros_like(dws_ref)
            dbs_ref[...] = jnp.zeros_like(dbs_ref)
            dlg_ref[...] = jnp.zeros_like(dlg_ref)
            dlb_ref[...] = jnp.zeros_like(dlb_ref)

        u, du_dz = _gelu_and_grad(z_ref[:, :w].astype(F32))
        lg = lg_ref[...]
        vn, vhat, rstd, dvv_dz = _gmlp_norm(z_ref[:, w:].astype(F32), lg, lb_ref[...])
        dav = da_ref[...].astype(F32)
        mask = _gmlp_mask()
        lane = lax.broadcasted_iota(jnp.int32, (GMLP_BLOCK, LANES), 1)
        dvn_parts = []
        dbs = jnp.zeros((GMLP_BLOCK, LANES), F32)
        for g in range(GMLP_GROUPS):
            sl = slice(g * gd, (g + 1) * gd)
            wm = jnp.where(mask, ws_ref[g], 0.0)
            vn_g = vn[:, sl]
            mixed = _dot(wm, vn_g, _NN) + bs_ref[:, g:g + 1]
            dmixed = dav[:, sl] * u[:, sl]
            dz_ref[:, sl] = (dav[:, sl] * mixed * du_dz[:, sl]).astype(dz_ref.dtype)
            dvn_parts.append(_dot(wm, dmixed, _TN))
            dws_ref[g] += jnp.where(mask, _dot(dmixed, vn_g, _NT), 0.0)
            dbs = dbs + jnp.where(lane == g, jnp.sum(dmixed, axis=-1, keepdims=True), 0.0)
        dbs_ref[...] += dbs
        dvn = jnp.concatenate(dvn_parts, axis=-1)
        dlg_ref[...] += jnp.sum(dvn * vhat, axis=0, keepdims=True)
        dlb_ref[...] += jnp.sum(dvn, axis=0, keepdims=True)
        dyg = dvn * lg
        dvv = rstd * (dyg - jnp.mean(dyg, axis=-1, keepdims=True)
                      - vhat * jnp.mean(dyg * vhat, axis=-1, keepdims=True))
        dz_ref[:, w:] = (dvv * dvv_dz).astype(dz_ref.dtype)

    full = lambda shape: pl.BlockSpec(shape, lambda i: (0,) * len(shape))
    return pl.pallas_call(
        body, name=name, grid=(s // GMLP_BLOCK,),
        in_specs=[pl.BlockSpec((GMLP_BLOCK, 2 * w), lambda i: (i, 0)),
                  pl.BlockSpec((GMLP_BLOCK, w), lambda i: (i, 0)), full((1, w)), full((1, w)),
                  full((GMLP_GROUPS, GMLP_BLOCK, GMLP_BLOCK)), full((GMLP_BLOCK, LANES))],
        out_specs=[pl.BlockSpec((GMLP_BLOCK, 2 * w), lambda i: (i, 0)),
                   full((GMLP_GROUPS, GMLP_BLOCK, GMLP_BLOCK)), full((GMLP_BLOCK, LANES)),
                   full((1, w)), full((1, w))],
        out_shape=[jax.ShapeDtypeStruct((s, 2 * w), BF16),
                   jax.ShapeDtypeStruct((GMLP_GROUPS, GMLP_BLOCK, GMLP_BLOCK), F32),
                   jax.ShapeDtypeStruct((GMLP_BLOCK, LANES), F32),
                   jax.ShapeDtypeStruct((1, w), F32), jax.ShapeDtypeStruct((1, w), F32)],
        compiler_params=_params(),
    )(z_uv, da, ln_g, ln_b, w_s, b_s_t)


def _tri(lower):
    r = lax.broadcasted_iota(jnp.int32, (ATT_BLOCK, ATT_BLOCK), 0)
    c = lax.broadcasted_iota(jnp.int32, (ATT_BLOCK, ATT_BLOCK), 1)
    return jnp.where((c <= r) if lower else (c >= r), 1.0, 0.0).astype(F32)


def _log_sigmoid(x):
    return jnp.minimum(x, 0.0) - jnp.log(1.0 + jnp.exp(-jnp.abs(x)))


def _fox_cum(f, b_f, *, name):
    s = f.shape[0]
    nb = s // ATT_BLOCK

    def body(f_ref, b_ref, cb_ref, ct_ref, carry):
        @pl.when(pl.program_id(0) == 0)
        def _():
            carry[...] = jnp.zeros_like(carry)

        lf = _log_sigmoid(f_ref[...] + b_ref[...])
        cum = lax.dot_general(_tri(True), lf, _NN, precision=lax.Precision.HIGHEST,
                              preferred_element_type=F32) + carry[...]
        carry[...] = cum[ATT_BLOCK - 1:ATT_BLOCK, :]
        for h in range(FOX_HEADS):
            cb_ref[h] = jnp.broadcast_to(cum[:, h:h + 1], (ATT_BLOCK, LANES))
        ct_ref[...] = cum.T

    return pl.pallas_call(
        body, name=name, grid=(nb,),
        in_specs=[pl.BlockSpec((ATT_BLOCK, LANES), lambda i: (i, 0)), pl.BlockSpec((1, LANES), lambda i: (0, 0))],
        out_specs=[pl.BlockSpec((FOX_HEADS, ATT_BLOCK, LANES), lambda i: (0, i, 0)),
                   pl.BlockSpec((LANES, ATT_BLOCK), lambda i: (0, i))],
        out_shape=[jax.ShapeDtypeStruct((FOX_HEADS, s, LANES), F32), jax.ShapeDtypeStruct((LANES, s), F32)],
        scratch_shapes=[pltpu.VMEM((1, LANES), F32)], compiler_params=_params(),
    )(f, b_f)


def _fox_dlogit(dcum_t, f, b_f, *, name):
    s = f.shape[0]
    nb = s // ATT_BLOCK

    def body(dc_ref, f_ref, b_ref, df_ref, db_ref, carry):
        @pl.when(pl.program_id(0) == 0)
        def _():
            carry[...] = jnp.zeros_like(carry)
            db_ref[...] = jnp.zeros_like(db_ref)

        d = dc_ref[...].T
        dlog = lax.dot_general(_tri(False), d, _NN, precision=lax.Precision.HIGHEST,
                               preferred_element_type=F32) + carry[...]
        carry[...] = dlog[0:1, :]
        df = dlog * (1.0 - _sigmoid(f_ref[...] + b_ref[...]))
        df_ref[...] = df
        db_ref[...] += jnp.sum(df, axis=0, keepdims=True)

    rev = lambda i: nb - 1 - i
    return pl.pallas_call(
        body, name=name, grid=(nb,),
        in_specs=[pl.BlockSpec((LANES, ATT_BLOCK), lambda i: (0, rev(i))),
                  pl.BlockSpec((ATT_BLOCK, LANES), lambda i: (rev(i), 0)),
                  pl.BlockSpec((1, LANES), lambda i: (0, 0))],
        out_specs=[pl.BlockSpec((ATT_BLOCK, LANES), lambda i: (rev(i), 0)),
                   pl.BlockSpec((1, LANES), lambda i: (0, 0))],
        out_shape=[jax.ShapeDtypeStruct((s, LANES), F32), jax.ShapeDtypeStruct((1, LANES), F32)],
        scratch_shapes=[pltpu.VMEM((1, LANES), F32)], compiler_params=_params(),
    )(dcum_t, f, b_f)


def _causal(qi, ki):
    r = lax.broadcasted_iota(jnp.int32, (ATT_BLOCK, ATT_BLOCK), 0) + qi * ATT_BLOCK
    c = lax.broadcasted_iota(jnp.int32, (ATT_BLOCK, ATT_BLOCK), 1) + ki * ATT_BLOCK
    return c <= r


def _head_mask():
    return lax.broadcasted_iota(jnp.int32, (1, LANES), 1) < FOX_HEAD_DIM


def _attn_fwd(qkv, cum_b, cum_r, *, name):
    s = qkv.shape[0]
    nq = s // ATT_BLOCK
    scale = FOX_HEAD_DIM ** -0.5
    npair = HEAD_PAIRS

    def body(q_ref, k_ref, v_ref, cq_ref, ck_ref, o_ref, l_ref):
        qi = pl.program_id(1)
        m0 = _head_mask()
        q2 = q_ref[...]
        zero = jnp.zeros_like(q2)
        qs = (jnp.where(m0, q2, zero), jnp.where(m0, zero, q2))
        cqs = (cq_ref[0], cq_ref[1])

        def step(ki, carry, masked):
            off = pl.multiple_of(ki * ATT_BLOCK, ATT_BLOCK)
            k2 = k_ref[pl.ds(off, ATT_BLOCK), :]
            v2 = v_ref[pl.ds(off, ATT_BLOCK), :]
            out = []
            for hh in range(2):
                m, l, acc = carry[hh]
                sc = _dot(qs[hh], k2, _NT) * scale + (cqs[hh] - ck_ref[hh:hh + 1, pl.ds(off, ATT_BLOCK)])
                if masked:
                    sc = jnp.where(_causal(qi, ki), sc, -1e30)
                m_new = jnp.maximum(m, jnp.max(sc, axis=-1, keepdims=True))
                alpha = jnp.exp(m - m_new)
                p = jnp.exp(sc - m_new)
                l = alpha * l + jnp.sum(p, axis=-1, keepdims=True)
                acc = alpha * acc + _dot(p, v2, _NN)
                out.append((m_new, l, acc))
            return tuple(out)

        init = tuple((jnp.full((ATT_BLOCK, 1), -1e30, F32), jnp.zeros((ATT_BLOCK, 1), F32),
                      jnp.zeros((ATT_BLOCK, LANES), F32)) for _ in range(2))
        carry = lax.fori_loop(0, qi, lambda ki, c: step(ki, c, False), init)
        (ma, la, acca), (mb, lb, accb) = step(qi, carry, True)
        o_ref[...] = jnp.where(m0, acca / la, accb / lb).astype(o_ref.dtype)
        l_ref[0] = jnp.broadcast_to(ma + jnp.log(la), (ATT_BLOCK, LANES))
        l_ref[1] = jnp.broadcast_to(mb + jnp.log(lb), (ATT_BLOCK, LANES))

    stat = pl.BlockSpec((None, 2, ATT_BLOCK, LANES), lambda j, i: (j, 0, i, 0))
    row = pl.BlockSpec((None, 2, s), lambda j, i: (j, 0, 0))
    return pl.pallas_call(
        body, name=name, grid=(npair, nq),
        in_specs=[pl.BlockSpec((ATT_BLOCK, LANES), lambda j, i: (i, j)),
                  pl.BlockSpec((s, LANES), lambda j, i: (0, npair + j)),
                  pl.BlockSpec((s, LANES), lambda j, i: (0, 2 * npair + j)),
                  stat, row],
        out_specs=[pl.BlockSpec((ATT_BLOCK, LANES), lambda j, i: (i, j)), stat],
        out_shape=[jax.ShapeDtypeStruct((s, FOX_WIDTH), BF16),
                   jax.ShapeDtypeStruct((npair, 2, s, LANES), F32)],
        compiler_params=_params(),
    )(qkv, qkv, qkv, cum_b, cum_r)


def _attn_delta(qkv, do, lse_b, cum_b, cum_r, *, name):
    s = qkv.shape[0]
    nq = s // ATT_BLOCK
    scale = FOX_HEAD_DIM ** -0.5
    npair = HEAD_PAIRS

    def body(q_ref, k_ref, v_ref, do_ref, l_ref, cq_ref, ck_ref, d_ref):
        qi = pl.program_id(1)
        m0 = _head_mask()
        q2 = q_ref[...]
        do2 = do_ref[...]
        qs = (jnp.where(m0, q2, jnp.zeros_like(q2)), jnp.where(m0, jnp.zeros_like(q2), q2))
        dos = (jnp.where(m0, do2, jnp.zeros_like(do2)), jnp.where(m0, jnp.zeros_like(do2), do2))

        def step(ki, carry, masked):
            off = pl.multiple_of(ki * ATT_BLOCK, ATT_BLOCK)
            k2 = k_ref[pl.ds(off, ATT_BLOCK), :]
            v2 = v_ref[pl.ds(off, ATT_BLOCK), :]
            out = []
            for hh in range(2):
                sc = _dot(qs[hh], k2, _NT) * scale + (cq_ref[hh] - ck_ref[hh:hh + 1, pl.ds(off, ATT_BLOCK)])
                p = jnp.exp(sc - l_ref[hh])
                if masked:
                    p = jnp.where(_causal(qi, ki), p, 0.0)
                out.append(carry[hh] + jnp.sum(p * _dot(dos[hh], v2, _NT), axis=-1, keepdims=True))
            return tuple(out)

        init = (jnp.zeros((ATT_BLOCK, 1), F32), jnp.zeros((ATT_BLOCK, 1), F32))
        carry = lax.fori_loop(0, qi, lambda ki, c: step(ki, c, False), init)
        da, db = step(qi, carry, True)
        d_ref[0] = jnp.broadcast_to(da, (ATT_BLOCK, LANES))
        d_ref[1] = jnp.broadcast_to(db, (ATT_BLOCK, LANES))

    stat = pl.BlockSpec((None, 2, ATT_BLOCK, LANES), lambda j, i: (j, 0, i, 0))
    return pl.pallas_call(
        body, name=name, grid=(npair, nq),
        in_specs=[pl.BlockSpec((ATT_BLOCK, LANES), lambda j, i: (i, j)),
                  pl.BlockSpec((s, LANES), lambda j, i: (0, npair + j)),
                  pl.BlockSpec((s, LANES), lambda j, i: (0, 2 * npair + j)),
                  pl.BlockSpec((ATT_BLOCK, LANES), lambda j, i: (i, j)),
                  stat, stat, pl.BlockSpec((None, 2, s), lambda j, i: (j, 0, 0))],
        out_specs=stat,
        out_shape=jax.ShapeDtypeStruct((npair, 2, s, LANES), F32), compiler_params=_params(),
    )(qkv, qkv, qkv, do, lse_b, cum_b, cum_r)


def _attn_bwd(qkv, do, lse_b, delta_b, cum_b, cum_r, *, name):
    s = qkv.shape[0]
    nq = s // ATT_BLOCK
    scale = FOX_HEAD_DIM ** -0.5
    npair = HEAD_PAIRS

    def body(q_ref, k_ref, v_ref, do_ref, l_ref, dl_ref, cq_ref, ck_ref, dq_ref, dk_ref, dv_ref, dc_ref):
        ki = pl.program_id(1)
        m0 = _head_mask()
        k2 = k_ref[...]
        v2 = v_ref[...]
        koff = pl.multiple_of(ki * ATT_BLOCK, ATT_BLOCK)

        @pl.when(ki == 0)
        def _():
            dq_ref[...] = jnp.zeros_like(dq_ref)

        def step(qi, carry, masked):
            off = pl.multiple_of(qi * ATT_BLOCK, ATT_BLOCK)
            q2 = q_ref[pl.ds(off, ATT_BLOCK), :]
            do2 = do_ref[pl.ds(off, ATT_BLOCK), :]
            qzero = jnp.zeros_like(q2)
            dzero = jnp.zeros_like(do2)
            out = []
            dqs = []
            for hh in range(2):
                dk_acc, dv_acc, dc_acc = carry[hh]
                keep = m0 if hh == 0 else jnp.logical_not(m0)
                qh = jnp.where(keep, q2, qzero)
                doh = jnp.where(keep, do2, dzero)
                sc = _dot(qh, k2, _NT) * scale + (cq_ref[hh, pl.ds(off, ATT_BLOCK), :]
                                                 - ck_ref[hh:hh + 1, pl.ds(koff, ATT_BLOCK)])
                p = jnp.exp(sc - l_ref[hh, pl.ds(off, ATT_BLOCK), :])
                if masked:
                    p = jnp.where(_causal(qi, ki), p, 0.0)
                dp = _dot(doh, v2, _NT)
                ds = p * (dp - dl_ref[hh, pl.ds(off, ATT_BLOCK), :])
                dv_acc = dv_acc + _dot(p, do2, _TN)
                dk_acc = dk_acc + _dot(ds, q2, _TN)
                dc_acc = dc_acc - jnp.sum(ds, axis=0, keepdims=True)
                dqs.append(_dot(ds, k2, _NN))
                out.append((dk_acc, dv_acc, dc_acc))
            dq_ref[pl.ds(off, ATT_BLOCK), :] += jnp.where(m0, dqs[0], dqs[1]) * scale
            return tuple(out)

        init = tuple((jnp.zeros((ATT_BLOCK, LANES), F32), jnp.zeros((ATT_BLOCK, LANES), F32),
                      jnp.zeros((1, ATT_BLOCK), F32)) for _ in range(2))
        carry = step(ki, init, True)
        (dka, dva, dca), (dkb, dvb, dcb) = lax.fori_loop(ki + 1, nq, lambda qi, c: step(qi, c, False), carry)
        dk_ref[...] = (jnp.where(m0, dka, dkb) * scale).astype(dk_ref.dtype)
        dv_ref[...] = jnp.where(m0, dva, dvb).astype(dv_ref.dtype)
        dc_ref[0:1, :] = dca
        dc_ref[1:2, :] = dcb

    stat = pl.BlockSpec((None, 2, s, LANES), lambda j, i: (j, 0, 0, 0))
    colfull = lambda base: pl.BlockSpec((s, LANES), lambda j, i: (0, base + j))
    colblk = lambda base: pl.BlockSpec((ATT_BLOCK, LANES), lambda j, i: (i, base + j))
    return pl.pallas_call(
        body, name=name, grid=(npair, nq),
        in_specs=[colfull(0), colblk(npair), colblk(2 * npair), colfull(0), stat, stat, stat,
                  pl.BlockSpec((None, 2, s), lambda j, i: (j, 0, 0))],
        out_specs=[colfull(0), colblk(0), colblk(0), pl.BlockSpec((None, 2, ATT_BLOCK), lambda j, i: (j, 0, i))],
        out_shape=[jax.ShapeDtypeStruct((s, FOX_WIDTH), F32), jax.ShapeDtypeStruct((s, FOX_WIDTH), BF16),
                   jax.ShapeDtypeStruct((s, FOX_WIDTH), BF16), jax.ShapeDtypeStruct((npair, 2, s), F32)],
        compiler_params=_params(),
    )(qkv, qkv, qkv, do, lse_b, delta_b, cum_b, cum_r)


ATT_TQ = 256
ATT_TK = 256
ATT_SCALE = FOX_HEAD_DIM ** -0.5
assert ATT_SCALE == 0.125 and ATT_TQ == ATT_TK


def _causal_t(qi, ki):
    kpos = lax.broadcasted_iota(jnp.int32, (ATT_TK, ATT_TQ), 0) + ki * ATT_TK
    qpos = lax.broadcasted_iota(jnp.int32, (ATT_TK, ATT_TQ), 1) + qi * ATT_TQ
    return kpos <= qpos


def _row_mask():
    return lax.broadcasted_iota(jnp.int32, (LANES, 1), 0) < FOX_HEAD_DIM


def _lane_tile(a, width):
    return a if a.shape[1] == width else jnp.tile(a, (1, width // a.shape[1]))


def _transpose_bf16(a):
    return a.astype(F32).T.astype(BF16)


def _attn_fwd_t(qkv, cum_b, cum_r, *, name):
    s = qkv.shape[0]
    nq = s // ATT_TQ
    npair = HEAD_PAIRS

    def body(q_ref, k_ref, v_ref, cq_ref, ck_ref, o_ref, ot_ref, l_ref, vt_ref):
        qi = pl.program_id(1)
        rows = _row_mask()

        @pl.when(qi == 0)
        def _():
            vt_ref[...] = _transpose_bf16(v_ref[...])

        qt = _transpose_bf16(q_ref[...]) * ATT_SCALE
        zero = jnp.zeros_like(qt)
        qts = (jnp.where(rows, qt, zero), jnp.where(rows, zero, qt))

        def step(ki, carry, masked):
            off = pl.multiple_of(ki * ATT_TK, ATT_TK)
            k2 = k_ref[pl.ds(off, ATT_TK), :]
            vt = vt_ref[:, pl.ds(off, ATT_TK)]
            out = []
            for hh in range(2):
                m, l, acc = carry[hh]
                bias = cq_ref[hh:hh + 1, :] - _lane_tile(ck_ref[hh, pl.ds(off, ATT_TK), :], ATT_TQ)
                sc = _dot(k2, qts[hh], _NN) + bias
                if masked:
                    sc = jnp.where(_causal_t(qi, ki), sc, -1e30)
                m_new = jnp.maximum(m, jnp.max(sc, axis=0, keepdims=True))
                alpha = jnp.exp(m - m_new)
                p = jnp.exp(sc - m_new)
                l = alpha * l + jnp.sum(p, axis=0, keepdims=True)
                p_hi = p.astype(BF16)
                p_lo = (p - p_hi.astype(F32)).astype(BF16)
                acc = alpha * acc + (_dot(vt, p_hi, _NN) + _dot(vt, p_lo, _NN))
                out.append((m_new, l, acc))
            return tuple(out)

        init = tuple((jnp.full((1, ATT_TQ), -1e30, F32), jnp.zeros((1, ATT_TQ), F32),
                      jnp.zeros((LANES, ATT_TQ), F32)) for _ in range(2))
        carry = lax.fori_loop(0, qi // 2, lambda kk, c: step(2 * kk + 1, step(2 * kk, c, False), False), init)
        carry = lax.cond(qi % 2 == 1, lambda c: step(qi - 1, c, False), lambda c: c, carry)
        (ma, la, acca), (mb, lb, accb) = step(qi, carry, True)
        ot = jnp.where(rows, acca / la, accb / lb)
        ot_ref[...] = ot
        o_ref[...] = ot.T.astype(o_ref.dtype)
        l_ref[0:1, :] = ma + jnp.log(la)
        l_ref[1:2, :] = mb + jnp.log(lb)

    row = pl.BlockSpec((None, 2, ATT_TQ), lambda j, i: (j, 0, i))
    return pl.pallas_call(
        body, name=name, grid=(npair, nq),
        in_specs=[pl.BlockSpec((ATT_TQ, LANES), lambda j, i: (i, j)),
                  pl.BlockSpec((s, LANES), lambda j, i: (0, npair + j)),
                  pl.BlockSpec((s, LANES), lambda j, i: (0, 2 * npair + j)),
                  row, pl.BlockSpec((None, 2, s, LANES), lambda j, i: (j, 0, 0, 0))],
        out_specs=[pl.BlockSpec((ATT_TQ, LANES), lambda j, i: (i, j)),
                   pl.BlockSpec((LANES, ATT_TQ), lambda j, i: (j, i)), row],
        out_shape=[jax.ShapeDtypeStruct((s, FOX_WIDTH), BF16), jax.ShapeDtypeStruct((FOX_WIDTH, s), F32),
                   jax.ShapeDtypeStruct((npair, 2, s), F32)],
        scratch_shapes=[pltpu.VMEM((LANES, s), BF16)],
        compiler_params=_params(),
    )(qkv, qkv, qkv, cum_r, cum_b)


def _attn_delta_t(do_t, o_t, *, name):
    s = o_t.shape[1]
    ts = _tile(s, 512)

    def body(do_ref, o_ref, d_ref):
        prod = do_ref[...].astype(F32) * o_ref[...]
        d_ref[0:1, :] = jnp.sum(prod[:FOX_HEAD_DIM], axis=0, keepdims=True)
        d_ref[1:2, :] = jnp.sum(prod[FOX_HEAD_DIM:], axis=0, keepdims=True)

    blk = pl.BlockSpec((LANES, ts), lambda j, i: (j, i))
    return pl.pallas_call(
        body, name=name, grid=(HEAD_PAIRS, s // ts), in_specs=[blk, blk],
        out_specs=pl.BlockSpec((None, 2, ts), lambda j, i: (j, 0, i)),
        out_shape=jax.ShapeDtypeStruct((HEAD_PAIRS, 2, s), F32), compiler_params=_params(),
    )(do_t, o_t)


def _attn_bwd_t(qkv, do, o_t, lse, cum_b, cum_r, *, name, dep=None):
    s = qkv.shape[0]
    nq = s // ATT_TQ
    npair = HEAD_PAIRS

    deps = [] if dep is None else [dep]

    def body(q_ref, k_ref, v_ref, do_ref, ot_ref, l_ref, cq_ref, ck_ref, *rest):
        dq_ref, dk_ref, dv_ref, dc_ref, qt_ref, dot_ref, dqt_ref, dl_ref = rest[len(deps):]
        ki = pl.program_id(1)
        m0 = _head_mask()
        rows = _row_mask()
        k2 = k_ref[...]
        v2 = v_ref[...]
        kt = _transpose_bf16(k2)
        ks = k2 * ATT_SCALE
        kz, vz = jnp.zeros_like(k2), jnp.zeros_like(v2)
        khs = (jnp.where(m0, ks, kz), jnp.where(m0, kz, ks))
        vhs = (jnp.where(m0, v2, vz), jnp.where(m0, vz, v2))
        cks = tuple(_lane_tile(ck_ref[hh], ATT_TQ) for hh in range(2))

        @pl.when(ki == 0)
        def _():
            dqt_ref[...] = jnp.zeros_like(dqt_ref)
            qt_ref[...] = _transpose_bf16(q_ref[...])
            do_t = do_ref[...].astype(F32).T
            dot_ref[...] = do_t.astype(BF16)
            prod = do_t * ot_ref[...]
            dl_ref[0:1, :] = jnp.sum(prod[:FOX_HEAD_DIM], axis=0, keepdims=True)
            dl_ref[1:2, :] = jnp.sum(prod[FOX_HEAD_DIM:], axis=0, keepdims=True)

        def step(qi, carry, masked):
            off = pl.multiple_of(qi * ATT_TQ, ATT_TQ)
            q2 = q_ref[pl.ds(off, ATT_TQ), :]
            do2 = do_ref[pl.ds(off, ATT_TQ), :]
            qt = qt_ref[:, pl.ds(off, ATT_TQ)]
            dot_ = dot_ref[:, pl.ds(off, ATT_TQ)]
            out, dqs = [], []
            for hh in range(2):
                dk_acc, dv_acc, dc_acc = carry[hh]
                sc = _dot(khs[hh], qt, _NN) + (cq_ref[hh:hh + 1, pl.ds(off, ATT_TQ)] - cks[hh])
                p = jnp.exp(sc - l_ref[hh:hh + 1, pl.ds(off, ATT_TQ)])
                if masked:
                    p = jnp.where(_causal_t(qi, ki), p, 0.0)
                dp = _dot(vhs[hh], dot_, _NN)
                ds = p * (dp - dl_ref[hh:hh + 1, pl.ds(off, ATT_TQ)])
                dc_acc = dc_acc - jnp.sum(ds, axis=1, keepdims=True)
                dss = (ds * ATT_SCALE).astype(BF16)
                dv_acc = dv_acc + _dot(p, do2, _NN)
                dk_acc = dk_acc + _dot(dss, q2, _NN)
                dqs.append(_dot(kt, dss, _NN))
                out.append((dk_acc, dv_acc, dc_acc))
            dqt_ref[:, pl.ds(off, ATT_TQ)] += jnp.where(rows, dqs[0], dqs[1])
            return tuple(out)

        init = tuple((jnp.zeros((ATT_TK, LANES), F32), jnp.zeros((ATT_TK, LANES), F32),
                      jnp.zeros((ATT_TK, 1), F32)) for _ in range(2))
        carry = step(ki, init, True)
        rest = nq - 1 - ki
        carry = lax.fori_loop(
            0, rest // 2, lambda t, c: step(ki + 2 + 2 * t, step(ki + 1 + 2 * t, c, False), False), carry)
        carry = lax.cond(rest % 2 == 1, lambda c: step(nq - 1, c, False), lambda c: c, carry)
        (dka, dva, dca), (dkb, dvb, dcb) = carry
        dk_ref[...] = jnp.where(m0, dka, dkb).astype(dk_ref.dtype)
        dv_ref[...] = jnp.where(m0, dva, dvb).astype(dv_ref.dtype)
        dc_ref[0] = jnp.broadcast_to(dca, (ATT_TK, LANES))
        dc_ref[1] = jnp.broadcast_to(dcb, (ATT_TK, LANES))

        @pl.when(ki == nq - 1)
        def _():
            dq_ref[...] = dqt_ref[...].T.astype(dq_ref.dtype)

    colfull = lambda base: pl.BlockSpec((s, LANES), lambda j, i: (0, base + j))
    colblk = lambda base: pl.BlockSpec((ATT_TK, LANES), lambda j, i: (i, base + j))
    stat = pl.BlockSpec((None, 2, s), lambda j, i: (j, 0, 0))
    bcast = pl.BlockSpec((None, 2, ATT_TK, LANES), lambda j, i: (j, 0, i, 0))
    grad = jax.ShapeDtypeStruct((s, FOX_WIDTH), BF16)
    return pl.pallas_call(
        body, name=name, grid=(npair, nq),
        in_specs=[colfull(0), colblk(npair), colblk(2 * npair), colfull(0),
                  pl.BlockSpec((LANES, s), lambda j, i: (j, 0)), stat, stat, bcast]
                 + [pl.BlockSpec(memory_space=pl.ANY)] * len(deps),
        out_specs=[colfull(0), colblk(0), colblk(0), bcast],
        out_shape=[grad, grad, grad, jax.ShapeDtypeStruct((npair, 2, s, LANES), F32)],
        scratch_shapes=[pltpu.VMEM((LANES, s), BF16), pltpu.VMEM((LANES, s), BF16), pltpu.VMEM((LANES, s), F32),
                        pltpu.VMEM((2, s), F32)],
        compiler_params=_params(),
    )(qkv, qkv, qkv, do, o_t, lse, cum_r, cum_b, *deps)


def _merge_fwd(zg, ya, yb, *, name, tm=256):
    s, d = ya.shape
    tm = _tile(s, tm)

    def body(zg_ref, ya_ref, yb_ref, m_ref):
        ga = _sigmoid(zg_ref[:, :d].astype(F32))
        gb = _sigmoid(zg_ref[:, d:].astype(F32))
        m_ref[...] = (ga * ya_ref[...].astype(F32) + gb * yb_ref[...].astype(F32)).astype(m_ref.dtype)

    row = pl.BlockSpec((tm, d), lambda i: (i, 0))
    row2 = pl.BlockSpec((tm, 2 * d), lambda i: (i, 0))
    return pl.pallas_call(
        body, name=name, grid=(s // tm,), in_specs=[row2, row, row], out_specs=row,
        out_shape=jax.ShapeDtypeStruct((s, d), BF16), compiler_params=_params(),
    )(zg, ya, yb)


def _merge_bwd(dm, zg, ya, yb, *, name, tm=256):
    s, d = ya.shape
    tm = _tile(s, tm)

    def body(dm_ref, zg_ref, ya_ref, yb_ref, dzg_ref, dya_ref, dyb_ref):
        dmv = dm_ref[...].astype(F32)
        ga = _sigmoid(zg_ref[:, :d].astype(F32))
        gb = _sigmoid(zg_ref[:, d:].astype(F32))
        dzg_ref[:, :d] = (dmv * ya_ref[...].astype(F32) * ga * (1.0 - ga)).astype(dzg_ref.dtype)
        dzg_ref[:, d:] = (dmv * yb_ref[...].astype(F32) * gb * (1.0 - gb)).astype(dzg_ref.dtype)
        dya_ref[...] = (dmv * ga).astype(dya_ref.dtype)
        dyb_ref[...] = (dmv * gb).astype(dyb_ref.dtype)

    row = pl.BlockSpec((tm, d), lambda i: (i, 0))
    row2 = pl.BlockSpec((tm, 2 * d), lambda i: (i, 0))
    return pl.pallas_call(
        body, name=name, grid=(s // tm,), in_specs=[row, row2, row, row], out_specs=[row2, row, row],
        out_shape=[jax.ShapeDtypeStruct((s, 2 * d), BF16), jax.ShapeDtypeStruct((s, d), BF16),
                   jax.ShapeDtypeStruct((s, d), BF16)],
        compiler_params=_params(),
    )(dm, zg, ya, yb)


def _shift_down(u, k, row):
    return jnp.where(row >= k, pltpu.roll(u, k, 0), 0.0)


def _shift_up(u, k, row):
    n = u.shape[0]
    return jnp.where(row < n - k, pltpu.roll(u, n - k, 0), 0.0)


def _conv_act_fwd(up_a, up_b, cw_a, cw_b, cb_a, cb_b, *, name, tc=128):
    s, f = up_a.shape
    tc = _tile(f, tc)

    def body(ua_ref, ub_ref, wa_ref, wb_ref, ba_ref, bb_ref, act_ref):
        row = lax.broadcasted_iota(jnp.int32, (s, tc), 0)

        def conv(u_ref, w_ref, b_ref):
            u = u_ref[...].astype(F32)
            return (b_ref[...] + w_ref[0:1, :] * _shift_down(u, 2, row)
                    + w_ref[1:2, :] * _shift_down(u, 1, row) + w_ref[2:3, :] * u)

        ca = conv(ua_ref, wa_ref, ba_ref)
        cb = conv(ub_ref, wb_ref, bb_ref)
        act_ref[...] = (_gelu(ca) * cb).astype(act_ref.dtype)

    col = pl.BlockSpec((s, tc), lambda j: (0, j))
    w3 = pl.BlockSpec((3, tc), lambda j: (0, j))
    b1 = pl.BlockSpec((1, tc), lambda j: (0, j))
    return pl.pallas_call(
        body, name=name, grid=(f // tc,), in_specs=[col, col, w3, w3, b1, b1], out_specs=col,
        out_shape=jax.ShapeDtypeStruct((s, f), BF16), compiler_params=_params(),
    )(up_a, up_b, cw_a, cw_b, cb_a, cb_b)


def _conv_act_bwd(up_a, up_b, dact, cw_a, cw_b, cb_a, cb_b, *, name, tc=128):
    s, f = up_a.shape
    tc = _tile(f, tc)

    def body(ua_ref, ub_ref, da_ref, wa_ref, wb_ref, ba_ref, bb_ref, dua_ref, dub_ref, dwa_ref, dwb_ref):
        row = lax.broadcasted_iota(jnp.int32, (s, tc), 0)

        def conv(u_ref, w_ref, b_ref):
            u = u_ref[...].astype(F32)
            u1 = _shift_down(u, 1, row)
            u2 = _shift_down(u, 2, row)
            return u, u1, u2, b_ref[...] + w_ref[0:1, :] * u2 + w_ref[1:2, :] * u1 + w_ref[2:3, :] * u

        def back(dc, taps, w_ref, du_ref, dw_ref):
            u, u1, u2 = taps
            dw_ref[0:1, :] = jnp.sum(dc * u2, axis=0, keepdims=True)
            dw_ref[1:2, :] = jnp.sum(dc * u1, axis=0, keepdims=True)
            dw_ref[2:3, :] = jnp.sum(dc * u, axis=0, keepdims=True)
            dw_ref[3:4, :] = jnp.sum(dc, axis=0, keepdims=True)
            du = (w_ref[2:3, :] * dc + w_ref[1:2, :] * _shift_up(dc, 1, row)
                  + w_ref[0:1, :] * _shift_up(dc, 2, row))
            du_ref[...] = du.astype(du_ref.dtype)

        ua, ua1, ua2, ca = conv(ua_ref, wa_ref, ba_ref)
        ub, ub1, ub2, cb = conv(ub_ref, wb_ref, bb_ref)
        g, dg = _gelu_and_grad(ca)
        dact_v = da_ref[...].astype(F32)
        back(dact_v * cb * dg, (ua, ua1, ua2), wa_ref, dua_ref, dwa_ref)
        back(dact_v * g, (ub, ub1, ub2), wb_ref, dub_ref, dwb_ref)

    col = pl.BlockSpec((s, tc), lambda j: (0, j))
    w3 = pl.BlockSpec((3, tc), lambda j: (0, j))
    w4 = pl.BlockSpec((4, tc), lambda j: (0, j))
    b1 = pl.BlockSpec((1, tc), lambda j: (0, j))
    return pl.pallas_call(
        body, name=name, grid=(f // tc,), in_specs=[col, col, col, w3, w3, b1, b1],
        out_specs=[col, col, w4, w4],
        out_shape=[jax.ShapeDtypeStruct((s, f), BF16), jax.ShapeDtypeStruct((s, f), BF16),
                   jax.ShapeDtypeStruct((4, f), F32), jax.ShapeDtypeStruct((4, f), F32)],
        compiler_params=_params(),
    )(up_a, up_b, dact, cw_a, cw_b, cb_a, cb_b)


def _ple_final(x2, ple, zp, target, g_final, *, name, tm=256):
    s, d = x2.shape
    tm = _tile(s, tm)

    def body(x_ref, ple_ref, zp_ref, t_ref, g_ref, dx_ref, dple_ref, dzp_ref, dg_ref, loss_ref):
        @pl.when(pl.program_id(0) == 0)
        def _():
            dg_ref[...] = jnp.zeros_like(dg_ref)
            loss_ref[...] = jnp.zeros_like(loss_ref)

        gp = _sigmoid(zp_ref[...].astype(F32))
        plev = ple_ref[...].astype(F32)
        x3 = x_ref[...] + plev * gp
        r = lax.rsqrt(jnp.mean(x3 * x3, axis=-1, keepdims=True) + EPS)
        xhat = x3 * r
        gv = g_ref[...]
        diff = xhat * gv - t_ref[...]
        loss_ref[...] += 0.5 * jnp.sum(jnp.mean(diff * diff, axis=-1, keepdims=True), axis=0, keepdims=True)
        dy = diff * (1.0 / d)
        dg_ref[...] += jnp.sum(dy * xhat, axis=0, keepdims=True)
        dyg = dy * gv
        dx3 = r * (dyg - xhat * jnp.mean(dyg * xhat, axis=-1, keepdims=True))
        dx_ref[...] = dx3
        dple_ref[...] = (dx3 * gp).astype(dple_ref.dtype)
        dzp_ref[...] = (dx3 * plev * gp * (1.0 - gp)).astype(dzp_ref.dtype)

    row = pl.BlockSpec((tm, d), lambda i: (i, 0))
    vec = pl.BlockSpec((1, d), lambda i: (0, 0))
    return pl.pallas_call(
        body, name=name, grid=(s // tm,), in_specs=[row, row, row, row, vec],
        out_specs=[row, row, row, vec, pl.BlockSpec((1, LANES), lambda i: (0, 0))],
        out_shape=[jax.ShapeDtypeStruct((s, d), F32), jax.ShapeDtypeStruct((s, d), BF16),
                   jax.ShapeDtypeStruct((s, d), BF16), jax.ShapeDtypeStruct((1, d), F32),
                   jax.ShapeDtypeStruct((1, LANES), F32)],
        compiler_params=_params(),
    )(x2, ple, zp, target, g_final)


def _device_step(x, p, target, w, get_w_in=None, get_w_rest=None, on_grads_ffn=None, on_grads_small=None,
                 on_grads_mix=None):
    s = x.shape[0]
    g = {}
    w = dict(w)

    h = _rms_fwd(x, w["norm_mix_g"], name="rms_mix", dep=w.get("first_dep"))
    if get_w_in is not None:
        w.update(get_w_in(h))
    z_uv = _mm(h, w["w_uv"], mode="nn", out_dtype=BF16, name="proj_uv", tm=1024, dep=w.get("proj_dep"))
    qkv = _mm(h, w["w_qkv"], mode="nn", out_dtype=BF16, name="proj_qkv", tm=1024)
    zg = _mm(h, w["w_g"], mode="nn", out_dtype=BF16, name="proj_gate", tm=1024)
    f = _mm(h, w["w_f"], mode="nn", out_dtype=F32, name="proj_f", tm=1024)

    a = _gmlp_fwd(z_uv, w["gmlp_ln_g"], w["gmlp_ln_b"], w["gmlp_w_s"], w["gmlp_b_s_t"], name="gmlp_fwd")

    cum_b, cum_t = _fox_cum(f, w["b_f"], name="fox_cum")
    cum_b = cum_b.reshape(HEAD_PAIRS, 2, s, LANES)
    cum_r = cum_t[:FOX_HEADS].reshape(HEAD_PAIRS, 2, s)
    b, o_t, lse = _attn_fwd_t(qkv, cum_b, cum_r, name="attn_fwd")
    if get_w_rest is not None:
        w.update(get_w_rest(b))

    ya = _mm(a, w["w_branch_a"], mode="nn", out_dtype=BF16, name="branch_a", tm=1024)
    yb = _mm(b, w["w_branch_b"], mode="nn", out_dtype=BF16, name="branch_b", tm=1024)
    merged = _merge_fwd(zg, ya, yb, name="merge_fwd")
    x1 = _mm(merged, w["w_out"], mode="nn", out_dtype=F32, name="proj_out", add=x, tm=1024)

    h2 = _rms_fwd(x1, w["norm_ffn_g"], name="rms_ffn")
    up_a = _mm(h2, w["w_up_a"], mode="nn", out_dtype=BF16, name="up_a", tm=1024, tn=D_FF // 2)
    up_b = _mm(h2, w["w_up_b"], mode="nn", out_dtype=BF16, name="up_b", tm=1024, tn=D_FF // 2)
    cw, cb = w["conv_w"], w["conv_b"]
    conv_args = (cw[:, :D_FF], cw[:, D_FF:], cb[:, :D_FF], cb[:, D_FF:])
    act = _conv_act_fwd(up_a, up_b, *conv_args, name="conv_act_fwd")
    x2 = _mm(act, w["w_down"], mode="nn", out_dtype=F32, name="down", add=x1, tm=512)

    h3 = _rms_fwd(x2, w["norm_ple_g"], name="rms_ple")
    ple = _mm(p, w["w_ple"], mode="nn", out_dtype=BF16, name="ple_proj", tm=1024)
    zp = _mm(h3, w["w_ple_gate"], mode="nn", out_dtype=BF16, name="ple_gate", tm=1024)
    dx3, dple, dzp, g["norm_final_g"], loss = _ple_final(x2, ple, zp, target, w["norm_final_g"], name="ple_final")

    g["w_ple"] = _mm(p, dple, mode="tn", out_dtype=BF16, name="dw_ple")
    g["w_ple_gate"] = _mm(h3, dzp, mode="tn", out_dtype=BF16, name="dw_ple_gate")
    dh3 = _mm(dzp, w["w_ple_gate"], mode="nt", out_dtype=BF16, name="dh3")
    dx2, dx2_b, g["norm_ple_g"] = _rms_bwd(x2, w["norm_ple_g"], dh3, dx3, name="rms_ple_bwd")

    g["w_down"] = _mm(act, dx2_b, mode="tn", out_dtype=BF16, name="dw_down", tm=D_FF // 2)
    dact = _mm(dx2_b, w["w_down"], mode="nt", out_dtype=BF16, name="dact", tn=D_FF // 2)
    dup_a, dup_b, dcw_a, dcw_b = _conv_act_bwd(up_a, up_b, dact, *conv_args, name="conv_act_bwd")
    g["conv_w"] = jnp.concatenate([dcw_a[:3], dcw_b[:3]], axis=1)
    g["conv_b"] = jnp.concatenate([dcw_a[3:], dcw_b[3:]], axis=1)
    g["w_up_a"] = _mm(h2, dup_a, mode="tn", out_dtype=BF16, name="dw_up_a", tn=D_FF // 2)
    g["w_up_b"] = _mm(h2, dup_b, mode="tn", out_dtype=BF16, name="dw_up_b", tn=D_FF // 2)
    dh2 = _mm_nt_sum([(dup_a, w["w_up_a"]), (dup_b, w["w_up_b"])], out_dtype=BF16, name="dh2")
    dx1, dx1_b, g["norm_ffn_g"] = _rms_bwd(x1, w["norm_ffn_g"], dh2, dx2, name="rms_ffn_bwd")
    dep = on_grads_ffn(g) if on_grads_ffn is not None else None

    g["w_out"] = _mm(merged, dx1_b, mode="tn", out_dtype=BF16, name="dw_out")
    dmerged = _mm(dx1_b, w["w_out"], mode="nt", out_dtype=BF16, name="dmerged", dep=dep)
    dzg, dya, dyb = _merge_bwd(dmerged, zg, ya, yb, name="merge_bwd")
    g["w_branch_a"] = _mm(a, dya, mode="tn", out_dtype=BF16, name="dw_branch_a")
    g["w_branch_b"] = _mm(b, dyb, mode="tn", out_dtype=BF16, name="dw_branch_b")
    da = _mm(dya, w["w_branch_a"], mode="nt", out_dtype=BF16, name="da")
    db = _mm(dyb, w["w_branch_b"], mode="nt", out_dtype=BF16, name="db")

    dz_uv, g["gmlp_w_s"], dbs_t, g["gmlp_ln_g"], g["gmlp_ln_b"] = _gmlp_bwd(
        z_uv, da, w["gmlp_ln_g"], w["gmlp_ln_b"], w["gmlp_w_s"], w["gmlp_b_s_t"], name="gmlp_bwd")
    g["gmlp_b_s"] = dbs_t[:, :GMLP_GROUPS].T
    dep = on_grads_small(g) if on_grads_small is not None else None

    dq, dk, dv, dcum_b = _attn_bwd_t(qkv, db, o_t, lse, cum_b, cum_r, name="attn_bwd", dep=dep)
    dcum_t = jnp.pad(dcum_b[..., 0].reshape(FOX_HEADS, s), ((0, LANES - FOX_HEADS), (0, 0)))
    df, g["b_f"] = _fox_dlogit(dcum_t, f, w["b_f"], name="fox_dlogit")
    dqkv = jnp.concatenate([dq, dk, dv], axis=1)

    g["w_uv"] = _mm(h, dz_uv, mode="tn", out_dtype=BF16, name="dw_uv")
    g["w_qkv"] = _mm(h, dqkv, mode="tn", out_dtype=BF16, name="dw_qkv")
    g["w_f"] = _mm(h, df, mode="tn", out_dtype=BF16, name="dw_f")
    g["w_g"] = _mm(h, dzg, mode="tn", out_dtype=BF16, name="dw_g")
    dep = on_grads_mix(g) if on_grads_mix is not None else None
    dh = _mm_nt_sum([(dz_uv, w["w_uv"]), (dqkv, w["w_qkv"]), (df, w["w_f"]), (dzg, w["w_g"])],
                    out_dtype=BF16, name="dh", dep=dep)
    dx0, _, g["norm_mix_g"] = _rms_bwd(x, w["norm_mix_g"], dh, dx1, name="rms_mix_bwd")
    return loss, dx0, g


def _coords():
    return lax.axis_index("x"), lax.axis_index("y"), lax.axis_index("c")


def _other_chips(x, y):
    return [(1 - x, y), (x, 1 - y), (1 - x, 1 - y)]


def _remote(src, dst, send_sem, recv_sem, dev):
    return pltpu.make_async_remote_copy(src_ref=src, dst_ref=dst, send_sem=send_sem, recv_sem=recv_sem,
                                        device_id=dev, device_id_type=MESH)


_ANY = pl.BlockSpec(memory_space=pl.ANY)


def _gather_weights(halved, whole, *, name):
    nh, n = len(halved), len(halved) + len(whole)
    arrays = list(halved) + list(whole)

    def body(*refs):
        ins, outs = refs[:n], refs[n:2 * n]
        send_sems, recv_sems = refs[2 * n:]
        x, y, c = _coords()
        me, sib = 2 * x + y, (x, y, 1 - c)
        chips = _other_chips(x, y)

        def half(i, which):
            h = ins[i].shape[0] // 2
            return pl.ds(pl.multiple_of(which * h, 16), h)

        sends = []
        for i in range(n):
            src, dst = (ins[i].at[half(i, c)], outs[i].at[me, half(i, c)]) if i < nh else (ins[i], outs[i].at[me])
            for k, (cx, cy) in enumerate(chips):
                cp = _remote(src, dst, send_sems.at[i, k], recv_sems.at[i, k], (cx, cy, c))
                cp.start()
                sends.append(cp)
        for i in range(n):
            for k, (cx, cy) in enumerate(chips):
                got = outs[i].at[2 * cx + cy, half(i, c)] if i < nh else outs[i].at[2 * cx + cy]
                _remote(got, got, send_sems.at[i, k], recv_sems.at[i, k], sib).wait_recv()
                if i < nh:
                    cp = _remote(got, got, send_sems.at[i, 3 + k], recv_sems.at[i, 3 + k], sib)
                    cp.start()
                    sends.append(cp)
        for i in range(nh):
            for k, (cx, cy) in enumerate(chips):
                got = outs[i].at[2 * cx + cy, half(i, 1 - c)]
                _remote(got, got, send_sems.at[i, 3 + k], recv_sems.at[i, 3 + k], sib).wait_recv()
        for cp in sends:
            cp.wait_send()

    outs = pl.pallas_call(
        body, name=name, in_specs=[_ANY] * n, out_specs=[_ANY] * n,
        out_shape=[jax.ShapeDtypeStruct((N_CHIPS,) + a.shape, a.dtype) for a in arrays],
        scratch_shapes=[pltpu.SemaphoreType.DMA((n, 6)), pltpu.SemaphoreType.DMA((n, 6))],
        compiler_params=_params(),
    )(*arrays)
    chip = 2 * lax.axis_index("x") + lax.axis_index("y")
    return [lax.dynamic_update_index_in_dim(o, a, chip, 0) for o, a in zip(outs, arrays)]


def _pair_exchange(gs, *, name):
    n = len(gs)

    def body(*refs):
        ins, outs = refs[:n], refs[n:2 * n]
        send_sems, recv_sems = refs[2 * n:]
        x, y, c = _coords()
        copies = []
        for i in range(n):
            for j in range(N_CHIPS):
                cp = _remote(ins[i].at[j, 1 - c], outs[i].at[j], send_sems.at[i, j], recv_sems.at[i, j], (x, y, 1 - c))
                cp.start()
                copies.append(cp)
        for cp in copies:
            cp.wait()

    return pl.pallas_call(
        body, name=name, in_specs=[_ANY] * n, out_specs=[_ANY] * n,
        out_shape=[jax.ShapeDtypeStruct((N_CHIPS,) + a.shape[2:], a.dtype) for a in gs],
        scratch_shapes=[pltpu.SemaphoreType.DMA((n, N_CHIPS)), pltpu.SemaphoreType.DMA((n, N_CHIPS))],
        compiler_params=_params(),
    )(*gs)


def _chip_exchange(ss, *, name):
    n = len(ss)

    def body(*refs):
        ins, outs = refs[:n], refs[n:2 * n]
        send_sems, recv_sems = refs[2 * n:]
        x, y, c = _coords()
        me = 2 * x + y
        chips = _other_chips(x, y)
        sends = []
        for i in range(n):
            for k, (cx, cy) in enumerate(chips):
                cp = _remote(ins[i].at[2 * cx + cy], outs[i].at[me], send_sems.at[i, k], recv_sems.at[i, k], (cx, cy, c))
                cp.start()
                sends.append(cp)
        for i in range(n):
            for k, (cx, cy) in enumerate(chips):
                got = outs[i].at[2 * cx + cy]
                _remote(got, got, send_sems.at[i, k], recv_sems.at[i, k], (cx, cy, c)).wait_recv()
        for cp in sends:
            cp.wait_send()

    return pl.pallas_call(
        body, name=name, in_specs=[_ANY] * n, out_specs=[_ANY] * n,
        out_shape=[jax.ShapeDtypeStruct(a.shape, a.dtype) for a in ss],
        scratch_shapes=[pltpu.SemaphoreType.DMA((n, 3)), pltpu.SemaphoreType.DMA((n, 3))],
        compiler_params=_params(),
    )(*ss)


def _pair_share(hs, *, name):
    n = len(hs)

    def body(*refs):
        ins, outs = refs[:n], refs[n:2 * n]
        send_sems, recv_sems = refs[2 * n:]
        x, y, c = _coords()
        copies = []
        for i in range(n):
            cp = _remote(ins[i], outs[i], send_sems.at[i], recv_sems.at[i], (x, y, 1 - c))
            cp.start()
            copies.append(cp)
        for cp in copies:
            cp.wait()

    return pl.pallas_call(
        body, name=name, in_specs=[_ANY] * n, out_specs=[_ANY] * n,
        out_shape=[jax.ShapeDtypeStruct(a.shape, a.dtype) for a in hs],
        scratch_shapes=[pltpu.SemaphoreType.DMA((n,)), pltpu.SemaphoreType.DMA((n,))],
        compiler_params=_params(),
    )(*hs)


def _all_exchange(vec, *, name):
    def body(v_ref, o_ref, send_sems, recv_sems, local_sem):
        x, y, c = _coords()
        me = 4 * x + 2 * y + c
        local = pltpu.make_async_copy(v_ref, o_ref.at[me], local_sem)
        local.start()
        copies = []
        k = 0
        for dx in (0, 1):
            for dy in (0, 1):
                for dc in (0, 1):
                    if dx or dy or dc:
                        peer = (1 - x if dx else x, 1 - y if dy else y, 1 - c if dc else c)
                        cp = _remote(v_ref, o_ref.at[me], send_sems.at[k], recv_sems.at[k], peer)
                        cp.start()
                        copies.append(cp)
                        k += 1
        for cp in copies:
            cp.wait()
        local.wait()

    return pl.pallas_call(
        body, name=name, in_specs=[_ANY], out_specs=_ANY,
        out_shape=jax.ShapeDtypeStruct((8,) + vec.shape, vec.dtype),
        scratch_shapes=[pltpu.SemaphoreType.DMA((7,)), pltpu.SemaphoreType.DMA((7,)), pltpu.SemaphoreType.DMA(())],
        compiler_params=_params(),
    )(vec)


_HBM = pl.BlockSpec(memory_space=pltpu.HBM)
_SEM = pl.BlockSpec(memory_space=pltpu.SEMAPHORE)
_EFFECT = pltpu.SideEffectType.DATAFLOW_SIDE_EFFECTING


def _copies_start(srcs, lands, plan, n_copies, *, name, after=()):
    ns, n = len(srcs), len(srcs) + len(lands)
    na = len(after)

    def body(*refs):
        send_sems, recv_sems = refs[n + na], refs[n + na + 1]
        token = refs[-1]
        for k, (src, dst, dev) in enumerate(plan(refs[:ns], refs[ns:n])):
            _remote(src, dst, send_sems.at[k], recv_sems.at[k], dev).start()
        token[...] = jnp.zeros_like(token)

    arrays = list(srcs) + list(lands)
    outs = pl.pallas_call(
        body, name=name,
        out_shape=(pltpu.SemaphoreType.DMA((n_copies,)), pltpu.SemaphoreType.DMA((n_copies,)),
                   *[pltpu.HBM(a.shape, a.dtype) for a in arrays], jax.ShapeDtypeStruct((8, LANES), F32)),
        in_specs=[_HBM] * n + [_ANY] * na,
        out_specs=(_SEM, _SEM, *[_HBM] * n, pl.BlockSpec(memory_space=pltpu.VMEM)),
        input_output_aliases={i: 2 + i for i in range(n)},
        compiler_params=pltpu.CompilerParams(has_side_effects=_EFFECT),
    )(*[pltpu.with_memory_space_constraint(a, pltpu.HBM) for a in arrays], *after)
    return outs[0], outs[1], list(outs[2:2 + ns]), list(outs[2 + ns:2 + n]), outs[-1]


def _copies_wait(send_sems, recv_sems, srcs, lands, plan, first, after, *, name):
    ns, n = len(srcs), len(srcs) + len(lands)

    def body(*refs):
        send, recv = refs[n], refs[n + 1]
        for k, (src, dst, dev) in enumerate(plan(refs[:ns], refs[ns:n])):
            cp = _remote(src, dst, send.at[first + k], recv.at[first + k], dev)
            cp.wait_send()
            cp.wait_recv()

    arrays = list(srcs) + list(lands)
    outs = pl.pallas_call(
        body, name=name, out_shape=tuple(pltpu.HBM(a.shape, a.dtype) for a in arrays),
        in_specs=[_HBM] * n + [_SEM, _SEM] + [_ANY] * len(after), out_specs=tuple([_HBM] * n),
        input_output_aliases={i: i for i in range(n)},
        compiler_params=pltpu.CompilerParams(has_side_effects=_EFFECT),
    )(*arrays, send_sems, recv_sems, *after)
    return list(outs[:ns]), list(outs[ns:])


def _gather_plan(halved):
    def plan(srcs, lands):
        x, y, c = _coords()
        me = 2 * x + y
        out = []
        for i, (src, land) in enumerate(zip(srcs, lands)):
            if halved[i]:
                h = src.shape[0] // 2
                rows = pl.ds(pl.multiple_of(c * h, 16), h)
                src, dst = src.at[rows], land.at[me, rows]
            else:
                dst = land.at[me]
            out += [(src, dst, (cx, cy, c)) for cx, cy in _other_chips(x, y)]
        return out
    return plan


def _forward_halves(lands, *, name):
    n = len(lands)

    def body(*refs):
        ins, outs = refs[:n], refs[n:2 * n]
        send_sems, recv_sems = refs[2 * n:]
        x, y, c = _coords()
        copies = []
        for i in range(n):
            h = ins[i].shape[1] // 2
            rows = pl.ds(pl.multiple_of(c * h, 16), h)
            for k, (cx, cy) in enumerate(_other_chips(x, y)):
                cp = _remote(ins[i].at[2 * cx + cy, rows], outs[i].at[2 * cx + cy, rows],
                             send_sems.at[i, k], recv_sems.at[i, k], (x, y, 1 - c))
                cp.start()
                copies.append(cp)
        for cp in copies:
            cp.wait()

    return pl.pallas_call(
        body, name=name, in_specs=[_ANY] * n, out_specs=[_ANY] * n,
        out_shape=[jax.ShapeDtypeStruct(a.shape, a.dtype) for a in lands],
        input_output_aliases={i: i for i in range(n)},
        scratch_shapes=[pltpu.SemaphoreType.DMA((n, 3)), pltpu.SemaphoreType.DMA((n, 3))],
        compiler_params=_params(),
    )(*lands)


def _all_plan(srcs, lands):
    x, y, c = _coords()
    me = 4 * x + 2 * y + c
    out = []
    for src, land in zip(srcs, lands):
        for dx in (0, 1):
            for dy in (0, 1):
                for dc in (0, 1):
                    if dx or dy or dc:
                        out.append((src, land.at[me], (1 - x if dx else x, 1 - y if dy else y, 1 - c if dc else c)))
    return out


def _chip_plan(srcs, lands):
    x, y, c = _coords()
    me = 2 * x + y
    out = []
    for src, land in zip(srcs, lands):
        out += [(src.at[2 * cx + cy], land.at[me], (cx, cy, c)) for cx, cy in _other_chips(x, y)]
    return out


ROW_BLOCK_BYTES = 2 * 1024 * 1024


def _rtile(r, pref, mult, row_bytes=None):
    if row_bytes is not None:
        pref = max(pref, ROW_BLOCK_BYTES // row_bytes)
    t = (min(r, pref) // mult) * mult
    while t >= mult:
        if r % t == 0:
            return t
        t -= mult
    return r


def _pair_add(g, recv, core, *, name):
    _, _, r2, cols = g.shape
    tr = _rtile(r2, 256, 16, row_bytes=2 * cols)

    def body(c_ref, g_ref, r_ref, o_ref):
        o_ref[...] = (g_ref[...].astype(F32) + r_ref[...].astype(F32)).astype(o_ref.dtype)

    blk = pl.BlockSpec((None, tr, cols), lambda j, i, c_ref: (j, i, 0))
    return pl.pallas_call(
        body, name=name,
        grid_spec=pltpu.PrefetchScalarGridSpec(
            num_scalar_prefetch=1, grid=(N_CHIPS, r2 // tr),
            in_specs=[pl.BlockSpec((None, None, tr, cols), lambda j, i, c_ref: (j, c_ref[0], i, 0)), blk],
            out_specs=blk),
        out_shape=jax.ShapeDtypeStruct(recv.shape, recv.dtype), compiler_params=_params(),
    )(core, g, recv)


def _sum_slots(a, out_dtype, *, name):
    n, r, cols = a.shape
    whole = n * r * cols * a.dtype.itemsize <= 4 * ROW_BLOCK_BYTES
    tr = r if whole else _rtile(r, 256, 16)

    def body(a_ref, o_ref):
        acc = a_ref[0].astype(F32)
        for j in range(1, n):
            acc = acc + a_ref[j].astype(F32)
        o_ref[...] = acc.astype(o_ref.dtype)

    return pl.pallas_call(
        body, name=name, grid=(r // tr,),
        in_specs=[pl.BlockSpec((n, tr, cols), lambda i: (0, i, 0))],
        out_specs=pl.BlockSpec((tr, cols), lambda i: (i, 0)),
        out_shape=jax.ShapeDtypeStruct((r, cols), out_dtype), compiler_params=_params(),
    )(a)


def _chip_sum(own, recv, chip, *, name):
    _, r2, cols = own.shape
    tr = _rtile(r2, 256, 16, row_bytes=2 * cols)

    def body(chip_ref, own_ref, *rest):
        o_ref = rest[-1]
        acc = None
        for j in range(N_CHIPS):
            term = jnp.where(chip_ref[0] == j, own_ref[...], rest[j][...]).astype(F32)
            acc = term if acc is None else acc + term
        o_ref[...] = acc

    def slot(j):
        return pl.BlockSpec((None, tr, cols),
                            lambda i, chip_ref: (jnp.where(chip_ref[0] == j, (j + 1) % N_CHIPS, j), i, 0))

    return pl.pallas_call(
        body, name=name,
        grid_spec=pltpu.PrefetchScalarGridSpec(
            num_scalar_prefetch=1, grid=(r2 // tr,),
            in_specs=[pl.BlockSpec((None, tr, cols), lambda i, chip_ref: (chip_ref[0], i, 0))]
                     + [slot(j) for j in range(N_CHIPS)],
            out_specs=pl.BlockSpec((tr, cols), lambda i, chip_ref: (i, 0))),
        out_shape=jax.ShapeDtypeStruct((r2, cols), F32), compiler_params=_params(),
    )(chip, own, *([recv] * N_CHIPS))


def _adam_update(w, gv, m, v):
    c1 = 1.0 / (1.0 - ADAM_B1 ** ADAM_STEP)
    c2 = 1.0 / (1.0 - ADAM_B2 ** ADAM_STEP)
    nm = ADAM_B1 * m + (1.0 - ADAM_B1) * gv
    nv = ADAM_B2 * v + (1.0 - ADAM_B2) * gv * gv
    return -ADAM_LR * ((nm * c1) / (jnp.sqrt(nv * c2) + ADAM_EPS) + ADAM_WD * w), nm, nv


def _adamw_halves(w, g_mine, g_other, m, v, core, *, name):
    r, cols = w.shape
    r2 = r // 2
    tr = _rtile(r2, 256, 8, row_bytes=4 * cols)
    nt = r2 // tr

    def body(core_ref, w_ref, gm_ref, go_ref, m_ref, v_ref, g_ref, d_ref, nm_ref, nv_ref):
        gv = jnp.where(pl.program_id(0) == core_ref[0], gm_ref[...], go_ref[...])
        g_ref[...] = gv
        d_ref[...], nm_ref[...], nv_ref[...] = _adam_update(w_ref[...], gv, m_ref[...], v_ref[...])

    full = pl.BlockSpec((tr, cols), lambda hf, i, core_ref: (hf * nt + i, 0))
    half = pl.BlockSpec((tr, cols), lambda hf, i, core_ref: (i, 0))
    shape = jax.ShapeDtypeStruct((r, cols), F32)
    return pl.pallas_call(
        body, name=name,
        grid_spec=pltpu.PrefetchScalarGridSpec(
            num_scalar_prefetch=1, grid=(2, nt), in_specs=[full, half, half, full, full], out_specs=[full] * 4),
        out_shape=[shape] * 4, compiler_params=_params(),
    )(core, w, g_mine, g_other, m, v)


def _adamw_split_rows(w, g_mine, g_other, m, v, core, *, name, tc=256):
    r, cols = w.shape
    r2 = g_mine.shape[0]
    tc = _tile(cols, tc)

    def body(core_ref, w_ref, gm_ref, go_ref, m_ref, v_ref, g_ref, d_ref, nm_ref, nv_ref):
        mine_first = core_ref[0] == 0
        for lo, hi, first in ((0, r2, True), (r2, r, False)):
            n = hi - lo
            gm, go = gm_ref[0:n, :], go_ref[0:n, :]
            gv = jnp.where(mine_first, gm, go) if first else jnp.where(mine_first, go, gm)
            g_ref[lo:hi, :] = gv
            d_ref[lo:hi, :], nm_ref[lo:hi, :], nv_ref[lo:hi, :] = _adam_update(
                w_ref[lo:hi, :], gv, m_ref[lo:hi, :], v_ref[lo:hi, :])

    full = pl.BlockSpec((r, tc), lambda j, core_ref: (0, j))
    half = pl.BlockSpec((r2, tc), lambda j, core_ref: (0, j))
    shape = jax.ShapeDtypeStruct((r, cols), F32)
    return pl.pallas_call(
        body, name=name,
        grid_spec=pltpu.PrefetchScalarGridSpec(
            num_scalar_prefetch=1, grid=(cols // tc,), in_specs=[full, half, half, full, full],
            out_specs=[full] * 4),
        out_shape=[shape] * 4, compiler_params=_params(),
    )(core, w, g_mine, g_other, m, v)


def _adamw(w, g, m, v, *, name, rows=256):
    r, cols = w.shape
    tr = _rtile(r, rows, 8)

    def body(w_ref, g_ref, m_ref, v_ref, d_ref, nm_ref, nv_ref):
        d_ref[...], nm_ref[...], nv_ref[...] = _adam_update(w_ref[...], g_ref[...], m_ref[...], v_ref[...])

    blk = pl.BlockSpec((tr, cols), lambda i: (i, 0))
    shape = jax.ShapeDtypeStruct((r, cols), F32)
    return pl.pallas_call(
        body, name=name, grid=(r // tr,), in_specs=[blk] * 4, out_specs=[blk] * 3,
        out_shape=[shape] * 3, compiler_params=_params(),
    )(w, g, m, v)


_BIG = (("w_in", 1), ("w_branch_a", 0), ("w_branch_b", 0), ("w_out", 0), ("w_up", 1), ("w_down", 0),
        ("w_ple", 1), ("w_ple_gate", 0))
_SMALL = ("gmlp_ln_g", "gmlp_ln_b", "gmlp_w_s", "gmlp_b_s", "norm_ffn_g", "conv_b", "norm_ple_g", "norm_final_g",
          "b_f", "norm_mix_g")
N_LATE = 2
_WEIGHTS = ("norm_mix_g", "w_in", "b_f", "gmlp_ln_g", "gmlp_ln_b", "gmlp_w_s", "gmlp_b_s", "w_branch_a",
            "w_branch_b", "w_out", "norm_ffn_g", "w_up", "conv_w", "conv_b", "w_down", "norm_ple_g", "w_ple",
            "w_ple_gate", "norm_final_g")
_PACK_ROWS = 8


def _pack(arrays):
    parts = []
    for a in arrays:
        flat = a.reshape(-1)
        unit = _PACK_ROWS * LANES
        flat = jnp.pad(flat, (0, (-flat.shape[0]) % unit))
        parts.append(flat.reshape(-1, LANES))
    return jnp.concatenate(parts, axis=0)


def _unpack(packed, shapes):
    out, row = [], 0
    for shp in shapes:
        size = math.prod(shp)
        rows = -(-size // (_PACK_ROWS * LANES)) * _PACK_ROWS
        out.append(packed[row:row + rows].reshape(-1)[:size].reshape(shp))
        row += rows
    return out


def _take_cols(parts, lo, hi):
    out, start = [], 0
    for a in parts:
        width = a.shape[1]
        a0, a1 = max(lo, start) - start, min(hi, start + width) - start
        if a1 > a0:
            out.append(a if (a0, a1) == (0, width) else a[:, a0:a1])
        start += width
    return out[0] if len(out) == 1 else jnp.concatenate(out, axis=1)


def _take_rows(parts, lo, hi):
    out, start = [], 0
    for a in parts:
        height = a.shape[0]
        a0, a1 = max(lo, start) - start, min(hi, start + height) - start
        if a1 > a0:
            out.append(a if (a0, a1) == (0, height) else a[a0:a1])
        start += height
    return out[0] if len(out) == 1 else jnp.concatenate(out, axis=0)


def _assemble(gathered, axis):
    n, r, cols = gathered.shape
    if axis == 0:
        return gathered.reshape(n * r, cols)
    return _take_cols([gathered[j] for j in range(n)], 0, n * cols)


def _to_chunks(parts, axis):
    rows, total = parts[0].shape[0], sum(a.shape[1] for a in parts)
    if axis == 0:
        r, cols = rows // N_CHIPS, total
        chunks = _take_cols(parts, 0, total).reshape(N_CHIPS, r, cols)
    else:
        r, cols = rows, total // N_CHIPS
        chunks = jnp.stack([_take_cols(parts, j * cols, (j + 1) * cols) for j in range(N_CHIPS)])
    return chunks.reshape(N_CHIPS, 2, r // 2, cols)


def kernel(x, p, norm_mix_g, w_in, b_f, gmlp_ln_g, gmlp_ln_b, gmlp_w_s, gmlp_b_s, w_branch_a, w_branch_b, w_out, norm_ffn_g, w_up, conv_w, conv_b, w_down, norm_ple_g, w_ple, w_ple_gate, norm_final_g, loss_target, m_norm_mix_g, m_w_in, m_b_f, m_gmlp_ln_g, m_gmlp_ln_b, m_gmlp_w_s, m_gmlp_b_s, m_w_branch_a, m_w_branch_b, m_w_out, m_norm_ffn_g, m_w_up, m_conv_w, m_conv_b, m_w_down, m_norm_ple_g, m_w_ple, m_w_ple_gate, m_norm_final_g, v_norm_mix_g, v_w_in, v_b_f, v_gmlp_ln_g, v_gmlp_ln_b, v_gmlp_w_s, v_gmlp_b_s, v_w_branch_a, v_w_branch_b, v_w_out, v_norm_ffn_g, v_w_up, v_conv_w, v_conv_b, v_w_down, v_norm_ple_g, v_w_ple, v_w_ple_gate, v_norm_final_g):
    args = dict(locals())
    wt = {n: args[n] for n in _WEIGHTS}
    mom = {n: args["m_" + n] for n in _WEIGHTS}
    var = {n: args["v_" + n] for n in _WEIGHTS}
    chip = 2 * lax.axis_index("x") + lax.axis_index("y")
    core = lax.axis_index("c").astype(jnp.int32).reshape(1)

    chip1 = chip.astype(jnp.int32).reshape(1)
    device = 2 * chip + lax.axis_index("c")
    axis_of = dict(_BIG)
    names = [n for n, _ in _BIG]
    put_mine = lambda land, mine: lax.dynamic_update_index_in_dim(land, mine, chip, 0)

    shard_in = w_in[0].astype(BF16)
    sems_in = _copies_start([shard_in], [lax.empty((N_CHIPS,) + shard_in.shape, BF16)], _gather_plan([True]), 3,
                            name="gather_start_in")
    shards = [wt[n][0].astype(BF16) for n in names[1:]] + [conv_w[0]]
    halved = [True] * len(names[1:]) + [False]
    lands = [lax.empty((N_CHIPS,) + a.shape, a.dtype) for a in shards]
    send_sems, recv_sems, srcs, lands, rest_token = _copies_start(
        shards, lands, _gather_plan(halved), 3 * len(shards), name="gather_start_rest", after=[sems_in[4]])
    o1 = 2 * GMLP_WIDTH
    o2 = o1 + 3 * FOX_WIDTH
    o3 = o2 + FOX_HEADS
    fpad = ((0, 0), (0, LANES - FOX_HEADS))
    w = {
        "conv_b": conv_b, "norm_mix_g": norm_mix_g, "norm_ffn_g": norm_ffn_g, "norm_ple_g": norm_ple_g,
        "norm_final_g": norm_final_g.reshape(1, D_MODEL), "b_f": jnp.pad(b_f, fpad),
        "gmlp_ln_g": gmlp_ln_g, "gmlp_ln_b": gmlp_ln_b, "gmlp_w_s": gmlp_w_s[0],
        "gmlp_b_s_t": jnp.pad(gmlp_b_s[0].T, ((0, 0), (0, LANES - GMLP_GROUPS))),
        "first_dep": rest_token,
    }

    def get_w_in(after):
        _, got = _copies_wait(sems_in[0], sems_in[1], sems_in[2], sems_in[3], _gather_plan([True]), 0, [after],
                              name="gather_wait_in")
        got = _forward_halves(got, name="gather_forward_in")
        slots = put_mine(got[0], shard_in)
        slots = [slots[j] for j in range(N_CHIPS)]
        return {"w_uv": _take_cols(slots, 0, o1), "w_qkv": _take_cols(slots, o1, o2),
                "w_f": jnp.pad(_take_cols(slots, o2, o3), fpad), "w_g": _take_cols(slots, o3, o3 + 2 * D_MODEL)}

    def get_w_rest(after):
        _, got = _copies_wait(send_sems, recv_sems, srcs, lands, _gather_plan(halved), 0, [after],
                              name="gather_wait_rest")
        got = list(_forward_halves(got[:-1], name="gather_forward_rest")) + got[-1:]
        slots = {n: put_mine(got[i], shards[i]) for i, n in enumerate(names[1:])}
        full = {n: _assemble(slots[n], axis_of[n]) for n in names[1:] if n != "w_up"}
        up = [slots["w_up"][j] for j in range(N_CHIPS)]
        return {"w_branch_a": full["w_branch_a"], "w_branch_b": full["w_branch_b"], "w_out": full["w_out"],
                "w_up_a": _take_cols(up, 0, D_FF), "w_up_b": _take_cols(up, D_FF, 2 * D_FF),
                "w_down": full["w_down"], "w_ple": full["w_ple"], "w_ple_gate": full["w_ple_gate"],
                "conv_w": _assemble(put_mine(got[-1], shards[-1]), 1)}

    grads, delta, new_m, new_v = {}, {}, {}, {}
    pending = {}

    def to_chunks(n, gr):
        return _to_chunks(gr if isinstance(gr, list) else [gr], axis_of[n])

    def reduce_start(group, gfull, tag):
        chunks = [to_chunks(n, gfull[n]) for n in group]
        from_sibling = _pair_exchange(chunks, name="grad_pair_exchange_" + tag)
        pair_sums = [_pair_add(chunks[i], from_sibling[i], core, name="grad_pair_add_" + n) for i, n in enumerate(group)]
        empty = [lax.empty(a.shape, a.dtype) for a in pair_sums]
        ssem, rsem, own, recv, token = _copies_start(pair_sums, empty, _chip_plan, 3 * len(group),
                                                     name="grad_chip_start_" + tag)
        pending[tag] = (ssem, rsem, own, recv)
        return token

    def reduce_finish(group, tag, after):
        ssem, rsem, own, recv = pending[tag]
        own, recv = _copies_wait(ssem, rsem, own, recv, _chip_plan, 0, after, name="grad_chip_wait_" + tag)
        halves = [_chip_sum(own[i], recv[i], chip1, name="grad_chip_sum_" + n) for i, n in enumerate(group)]
        other_halves = _pair_share(halves, name="grad_pair_share_" + tag)
        for i, n in enumerate(group):
            shp = wt[n].shape
            outs = _adamw_halves(wt[n].reshape(shp[-2:]), halves[i], other_halves[i], mom[n].reshape(shp[-2:]),
                                 var[n].reshape(shp[-2:]), core, name="adamw_" + n)
            grads[n], delta[n], new_m[n], new_v[n] = (o.reshape(shp) for o in outs)
        return new_v[group[-1]]

    ffn_group = ("w_up", "w_down", "w_ple", "w_ple_gate")
    mix_group = ("w_in", "w_branch_a", "w_branch_b", "w_out")

    def on_grads_ffn(g):
        gfull = dict(g)
        gfull["w_up"] = [g["w_up_a"], g["w_up_b"]]
        return reduce_start(ffn_group, gfull, "ffn")

    def on_grads_small(g):
        vec = _pack([g[n] for n in _SMALL[:-N_LATE]] + [g["conv_w"]])
        ssem, rsem, own, recv, token = _copies_start(
            [vec], [lax.empty((8,) + vec.shape, F32)], _all_plan, 7, name="small_start")
        pending["small"] = (ssem, rsem, own, recv)
        return token

    def on_grads_mix(g):
        gfull = dict(g)
        gfull["w_in"] = [g["w_uv"], g["w_qkv"], g["w_f"][:, :FOX_HEADS], g["w_g"]]
        token = reduce_start(mix_group, gfull, "mix")
        pending["ffn_done"] = reduce_finish(ffn_group, "ffn", [token])
        return token

    loss, grad_x, g = _device_step(x[0], p[0, 0], loss_target[0], w, get_w_in, get_w_rest, on_grads_ffn,
                                   on_grads_small, on_grads_mix)

    mix_done = reduce_finish(mix_group, "mix", [grad_x, pending["ffn_done"]])
    ssem, rsem, own, recv = pending["small"]
    own, recv = _copies_wait(ssem, rsem, own, recv, _all_plan, 0, [mix_done], name="small_wait")
    vec_early = _sum_slots(lax.dynamic_update_index_in_dim(recv[0], own[0], device, 0), F32, name="small_sum")
    vec_late = _pack([g["b_f"][:, :FOX_HEADS], g["norm_mix_g"]])
    vec_late = _sum_slots(_all_exchange(vec_late, name="small_exchange_late"), F32, name="small_sum_late")
    early_rows = _pack([wt[n] for n in _SMALL[:-N_LATE]]).shape[0]
    vec = jnp.concatenate([vec_early[:early_rows], vec_late], axis=0)
    for n, a in zip(_SMALL, _unpack(vec, [wt[n].shape for n in _SMALL])):
        grads[n] = a
    conv_w_grad = _unpack(vec_early[early_rows:], [(3, 2 * D_FF)])[0]
    grads["conv_w"] = lax.dynamic_slice_in_dim(conv_w_grad, chip * conv_w.shape[2], conv_w.shape[2], axis=1).reshape(conv_w.shape)

    shp = conv_w.shape
    outs = _adamw(conv_w.reshape(shp[-2:]), grads["conv_w"].reshape(shp[-2:]), m_conv_w.reshape(shp[-2:]),
                  v_conv_w.reshape(shp[-2:]), name="adamw_conv_w")
    delta["conv_w"], new_m["conv_w"], new_v["conv_w"] = (o.reshape(shp) for o in outs)
    outs = _adamw(_pack([wt[n] for n in _SMALL]), vec, _pack([mom[n] for n in _SMALL]),
                  _pack([var[n] for n in _SMALL]), name="adamw_small", rows=2048)
    for d, o in zip((delta, new_m, new_v), outs):
        for n, a in zip(_SMALL, _unpack(o, [wt[n].shape for n in _SMALL])):
            d[n] = a

    total_loss = lax.psum(loss[0, 0], ("x", "y", "c"))
    return (total_loss, grad_x.reshape(x.shape), *[grads[n] for n in _WEIGHTS], *[delta[n] for n in _WEIGHTS],
            *[new_m[n] for n in _WEIGHTS], *[new_v[n] for n in _WEIGHTS])
```

```python
import functools
import math

import jax
import jax.numpy as jnp
from jax import lax
from jax.experimental import pallas as pl
from jax.experimental.pallas import tpu as pltpu

F32 = jnp.float32
BF16 = jnp.bfloat16

D_MODEL = 1024
EPS = 1e-6
CHUNK = 64
GMLP_GROUPS = 8
GMLP_BLOCK = 128
GMLP_WIDTH = 1024
FOX_HEADS = 16
FOX_HEAD_DIM = 64
FOX_WIDTH = 1024
HEAD_PAIRS = FOX_HEADS // 2
ATT_BLOCK = 128
D_FF = 2816
PLE_DIM = 256
LANES = 128
BF16_TILE_ROWS = 16
N_CHIPS = 4

ADAM_LR = 0.001
ADAM_B1 = 0.9
ADAM_B2 = 0.999
ADAM_EPS = 1e-08
ADAM_WD = 0.01
ADAM_STEP = 10

VMEM_LIMIT = 56 * 1024 * 1024
MESH = pl.DeviceIdType.MESH

_NN = (((1,), (0,)), ((), ()))
_NT = (((1,), (1,)), ((), ()))
_TN = (((0,), (0,)), ((), ()))


def _params(**kw):
    return pltpu.CompilerParams(vmem_limit_bytes=VMEM_LIMIT, **kw)


def _tile(dim, pref):
    if dim <= pref:
        return dim
    t = (pref // LANES) * LANES
    while t >= LANES:
        if dim % t == 0:
            return t
        t -= LANES
    return dim


def _dot(a, b, dn):
    return lax.dot_general(a.astype(BF16), b.astype(BF16), dn, preferred_element_type=F32)


def _gelu(x):
    c = math.sqrt(2.0 / math.pi)
    t = jnp.tanh(c * (x + 0.044715 * x * x * x))
    return 0.5 * x * (1.0 + t)


def _gelu_and_grad(x):
    c = math.sqrt(2.0 / math.pi)
    x2 = x * x
    t = jnp.tanh(c * (x + 0.044715 * x2 * x))
    g = 0.5 * x * (1.0 + t)
    dg = 0.5 * (1.0 + t) + 0.5 * x * (1.0 - t * t) * c * (1.0 + 3.0 * 0.044715 * x2)
    return g, dg


def _sigmoid(x):
    return 1.0 / (1.0 + jnp.exp(-x))


def _mm(a, b, *, mode, out_dtype, name, add=None, tm=512, tn=512, dep=None):
    if mode == "nn":
        m, k = a.shape
        k2, n = b.shape
    elif mode == "nt":
        m, k = a.shape
        n, k2 = b.shape
    else:
        k, m = a.shape
        k2, n = b.shape
    assert k == k2, (name, a.shape, b.shape)
    tm = _tile(m, tm)
    tn = _tile(n, tn)
    dn = {"nn": _NN, "nt": _NT, "tn": _TN}[mode]

    def body(a_ref, b_ref, *rest):
        o_ref = rest[-1]
        acc = _dot(a_ref[...], b_ref[...], dn)
        if add is not None:
            acc = acc + rest[0][...].astype(F32)
        o_ref[...] = acc.astype(o_ref.dtype)

    a_spec = pl.BlockSpec((k, tm), lambda i, j: (0, i)) if mode == "tn" else pl.BlockSpec((tm, k), lambda i, j: (i, 0))
    b_spec = pl.BlockSpec((tn, k), lambda i, j: (j, 0)) if mode == "nt" else pl.BlockSpec((k, tn), lambda i, j: (0, j))
    o_spec = pl.BlockSpec((tm, tn), lambda i, j: (i, j))
    in_specs = [a_spec, b_spec]
    args = [a, b]
    if add is not None:
        in_specs.append(o_spec)
        args.append(add)
    if dep is not None:
        in_specs.append(pl.BlockSpec(memory_space=pl.ANY))
        args.append(dep)
    return pl.pallas_call(
        body, name=name, grid=(m // tm, n // tn), in_specs=in_specs, out_specs=o_spec,
        out_shape=jax.ShapeDtypeStruct((m, n), out_dtype), compiler_params=_params(),
    )(*args)


def _mm_nt_sum(pairs, *, out_dtype, name, tm=256, dep=None):
    m, n = pairs[0][0].shape[0], pairs[0][1].shape[0]
    tm = _tile(m, tm)
    np_ = len(pairs)

    def body(*refs):
        o_ref = refs[-1] if dep is None else refs[-1]
        acc = None
        for p in range(np_):
            part = _dot(refs[2 * p][...], refs[2 * p + 1][...], _NT)
            acc = part if acc is None else acc + part
        o_ref[...] = acc.astype(o_ref.dtype)

    in_specs, args = [], []
    for a, b in pairs:
        assert a.shape[0] == m and b.shape[0] == n and a.shape[1] == b.shape[1], (name, a.shape, b.shape)
        in_specs += [pl.BlockSpec((tm, a.shape[1]), lambda i: (i, 0)), pl.BlockSpec(b.shape, lambda i: (0, 0))]
        args += [a, b]
    if dep is not None:
        in_specs.append(pl.BlockSpec(memory_space=pl.ANY))
        args.append(dep)
    return pl.pallas_call(
        body, name=name, grid=(m // tm,), in_specs=in_specs, out_specs=pl.BlockSpec((tm, n), lambda i: (i, 0)),
        out_shape=jax.ShapeDtypeStruct((m, n), out_dtype), compiler_params=_params(),
    )(*args)


def _rms_fwd(x, g, *, name, tm=256, dep=None):
    s, d = x.shape
    tm = _tile(s, tm)

    def body(x_ref, g_ref, *rest):
        h_ref = rest[-1]
        xv = x_ref[...]
        r = lax.rsqrt(jnp.mean(xv * xv, axis=-1, keepdims=True) + EPS)
        h_ref[...] = (xv * r * g_ref[...]).astype(h_ref.dtype)

    deps = [] if dep is None else [dep]
    return pl.pallas_call(
        body, name=name, grid=(s // tm,),
        in_specs=[pl.BlockSpec((tm, d), lambda i: (i, 0)), pl.BlockSpec((1, d), lambda i: (0, 0))]
                 + [pl.BlockSpec(memory_space=pl.ANY)] * len(deps),
        out_specs=pl.BlockSpec((tm, d), lambda i: (i, 0)),
        out_shape=jax.ShapeDtypeStruct((s, d), BF16), compiler_params=_params(),
    )(x, g, *deps)


def _rms_bwd(x, g, dh, dres, *, name, tm=256):
    s, d = x.shape
    tm = _tile(s, tm)

    def body(x_ref, g_ref, dh_ref, dres_ref, dx_ref, dxb_ref, dg_ref):
        xv = x_ref[...]
        r = lax.rsqrt(jnp.mean(xv * xv, axis=-1, keepdims=True) + EPS)
        xhat = xv * r
        dhv = dh_ref[...].astype(F32)
        dyg = dhv * g_ref[...]
        dx = dres_ref[...] + r * (dyg - xhat * jnp.mean(dyg * xhat, axis=-1, keepdims=True))
        dx_ref[...] = dx
        dxb_ref[...] = dx.astype(dxb_ref.dtype)

        @pl.when(pl.program_id(0) == 0)
        def _():
            dg_ref[...] = jnp.zeros_like(dg_ref)

        dg_ref[...] += jnp.sum(dhv * xhat, axis=0, keepdims=True)

    row = pl.BlockSpec((tm, d), lambda i: (i, 0))
    vec = pl.BlockSpec((1, d), lambda i: (0, 0))
    return pl.pallas_call(
        body, name=name, grid=(s // tm,), in_specs=[row, vec, row, row], out_specs=[row, row, vec],
        out_shape=[jax.ShapeDtypeStruct((s, d), F32), jax.ShapeDtypeStruct((s, d), BF16),
                   jax.ShapeDtypeStruct((1, d), F32)],
        compiler_params=_params(),
    )(x, g, dh, dres)


def _gmlp_mask():
    t = lax.broadcasted_iota(jnp.int32, (GMLP_BLOCK, GMLP_BLOCK), 0)
    s_ = lax.broadcasted_iota(jnp.int32, (GMLP_BLOCK, GMLP_BLOCK), 1)
    return (s_ // CHUNK) <= (t // CHUNK)


def _gmlp_norm(zv, ln_g, ln_b):
    vv, dvv = _gelu_and_grad(zv)
    mu = jnp.mean(vv, axis=-1, keepdims=True)
    xc = vv - mu
    rstd = lax.rsqrt(jnp.mean(xc * xc, axis=-1, keepdims=True) + EPS)
    vhat = xc * rstd
    return vhat * ln_g + ln_b, vhat, rstd, dvv


def _gmlp_fwd(z_uv, ln_g, ln_b, w_s, b_s_t, *, name):
    s = z_uv.shape[0]
    w = GMLP_WIDTH
    gd = w // GMLP_GROUPS

    def body(z_ref, lg_ref, lb_ref, ws_ref, bs_ref, a_ref):
        u = _gelu(z_ref[:, :w].astype(F32))
        vn, _, _, _ = _gmlp_norm(z_ref[:, w:].astype(F32), lg_ref[...], lb_ref[...])
        mask = _gmlp_mask()
        for g in range(GMLP_GROUPS):
            wm = jnp.where(mask, ws_ref[g], 0.0)
            mixed = _dot(wm, vn[:, g * gd:(g + 1) * gd], _NN) + bs_ref[:, g:g + 1]
            a_ref[:, g * gd:(g + 1) * gd] = (u[:, g * gd:(g + 1) * gd] * mixed).astype(a_ref.dtype)

    full = lambda shape: pl.BlockSpec(shape, lambda i: (0,) * len(shape))
    return pl.pallas_call(
        body, name=name, grid=(s // GMLP_BLOCK,),
        in_specs=[pl.BlockSpec((GMLP_BLOCK, 2 * w), lambda i: (i, 0)), full((1, w)), full((1, w)),
                  full((GMLP_GROUPS, GMLP_BLOCK, GMLP_BLOCK)), full((GMLP_BLOCK, LANES))],
        out_specs=pl.BlockSpec((GMLP_BLOCK, w), lambda i: (i, 0)),
        out_shape=jax.ShapeDtypeStruct((s, w), BF16), compiler_params=_params(),
    )(z_uv, ln_g, ln_b, w_s, b_s_t)


def _gmlp_bwd(z_uv, da, ln_g, ln_b, w_s, b_s_t, *, name):
    s = z_uv.shape[0]
    w = GMLP_WIDTH
    gd = w // GMLP_GROUPS

    def body(z_ref, da_ref, lg_ref, lb_ref, ws_ref, bs_ref, dz_ref, dws_ref, dbs_ref, dlg_ref, dlb_ref):
        @pl.when(pl.program_id(0) == 0)
        def _():
            dws_ref[...] = jnp.zeros_like(dws_ref)
            dbs_ref[...] = jnp.zeros_like(dbs_ref)
            dlg_ref[...] = jnp.zeros_like(dlg_ref)
            dlb_ref[...] = jnp.zeros_like(dlb_ref)

        u, du_dz = _gelu_and_grad(z_ref[:, :w].astype(F32))
        lg = lg_ref[...]
        vn, vhat, rstd, dvv_dz = _gmlp_norm(z_ref[:, w:].astype(F32), lg, lb_ref[...])
        dav = da_ref[...].astype(F32)
        mask = _gmlp_mask()
        lane = lax.broadcasted_iota(jnp.int32, (GMLP_BLOCK, LANES), 1)
        dvn_parts = []
        dbs = jnp.zeros((GMLP_BLOCK, LANES), F32)
        for g in range(GMLP_GROUPS):
            sl = slice(g * gd, (g + 1) * gd)
            wm = jnp.where(mask, ws_ref[g], 0.0)
            vn_g = vn[:, sl]
            mixed = _dot(wm, vn_g, _NN) + bs_ref[:, g:g + 1]
            dmixed = dav[:, sl] * u[:, sl]
            dz_ref[:, sl] = (dav[:, sl] * mixed * du_dz[:, sl]).astype(dz_ref.dtype)
            dvn_parts.append(_dot(wm, dmixed, _TN))
            dws_ref[g] += jnp.where(mask, _dot(dmixed, vn_g, _NT), 0.0)
            dbs = dbs + jnp.where(lane == g, jnp.sum(dmixed, axis=-1, keepdims=True), 0.0)
        dbs_ref[...] += dbs
        dvn = jnp.concatenate(dvn_parts, axis=-1)
        dlg_ref[...] += jnp.sum(dvn * vhat, axis=0, keepdims=True)
        dlb_ref[...] += jnp.sum(dvn, axis=0, keepdims=True)
        dyg = dvn * lg
        dvv = rstd * (dyg - jnp.mean(dyg, axis=-1, keepdims=True)
                      - vhat * jnp.mean(dyg * vhat, axis=-1, keepdims=True))
        dz_ref[:, w:] = (dvv * dvv_dz).astype(dz_ref.dtype)

    full = lambda shape: pl.BlockSpec(shape, lambda i: (0,) * len(shape))
    return pl.pallas_call(
        body, name=name, grid=(s // GMLP_BLOCK,),
        in_specs=[pl.BlockSpec((GMLP_BLOCK, 2 * w), lambda i: (i, 0)),
                  pl.BlockSpec((GMLP_BLOCK, w), lambda i: (i, 0)), full((1, w)), full((1, w)),
                  full((GMLP_GROUPS, GMLP_BLOCK, GMLP_BLOCK)), full((GMLP_BLOCK, LANES))],
        out_specs=[pl.BlockSpec((GMLP_BLOCK, 2 * w), lambda i: (i, 0)),
                   full((GMLP_GROUPS, GMLP_BLOCK, GMLP_BLOCK)), full((GMLP_BLOCK, LANES)),
                   full((1, w)), full((1, w))],
        out_shape=[jax.ShapeDtypeStruct((s, 2 * w), BF16),
                   jax.ShapeDtypeStruct((GMLP_GROUPS, GMLP_BLOCK, GMLP_BLOCK), F32),
                   jax.ShapeDtypeStruct((GMLP_BLOCK, LANES), F32),
                   jax.ShapeDtypeStruct((1, w), F32), jax.ShapeDtypeStruct((1, w), F32)],
        compiler_params=_params(),
    )(z_uv, da, ln_g, ln_b, w_s, b_s_t)


def _tri(lower):
    r = lax.broadcasted_iota(jnp.int32, (ATT_BLOCK, ATT_BLOCK), 0)
    c = lax.broadcasted_iota(jnp.int32, (ATT_BLOCK, ATT_BLOCK), 1)
    return jnp.where((c <= r) if lower else (c >= r), 1.0, 0.0).astype(F32)


def _log_sigmoid(x):
    return jnp.minimum(x, 0.0) - jnp.log(1.0 + jnp.exp(-jnp.abs(x)))


def _fox_cum(f, b_f, *, name):
    s = f.shape[0]
    nb = s // ATT_BLOCK

    def body(f_ref, b_ref, cb_ref, ct_ref, carry):
        @pl.when(pl.program_id(0) == 0)
        def _():
            carry[...] = jnp.zeros_like(carry)

        lf = _log_sigmoid(f_ref[...] + b_ref[...])
        cum = lax.dot_general(_tri(True), lf, _NN, precision=lax.Precision.HIGHEST,
                              preferred_element_type=F32) + carry[...]
        carry[...] = cum[ATT_BLOCK - 1:ATT_BLOCK, :]
        for h in range(FOX_HEADS):
            cb_ref[h] = jnp.broadcast_to(cum[:, h:h + 1], (ATT_BLOCK, LANES))
        ct_ref[...] = cum.T

    return pl.pallas_call(
        body, name=name, grid=(nb,),
        in_specs=[pl.BlockSpec((ATT_BLOCK, LANES), lambda i: (i, 0)), pl.BlockSpec((1, LANES), lambda i: (0, 0))],
        out_specs=[pl.BlockSpec((FOX_HEADS, ATT_BLOCK, LANES), lambda i: (0, i, 0)),
                   pl.BlockSpec((LANES, ATT_BLOCK), lambda i: (0, i))],
        out_shape=[jax.ShapeDtypeStruct((FOX_HEADS, s, LANES), F32), jax.ShapeDtypeStruct((LANES, s), F32)],
        scratch_shapes=[pltpu.VMEM((1, LANES), F32)], compiler_params=_params(),
    )(f, b_f)


def _fox_dlogit(dcum_t, f, b_f, *, name):
    s = f.shape[0]
    nb = s // ATT_BLOCK

    def body(dc_ref, f_ref, b_ref, df_ref, db_ref, carry):
        @pl.when(pl.program_id(0) == 0)
        def _():
            carry[...] = jnp.zeros_like(carry)
            db_ref[...] = jnp.zeros_like(db_ref)

        d = dc_ref[...].T
        dlog = lax.dot_general(_tri(False), d, _NN, precision=lax.Precision.HIGHEST,
                               preferred_element_type=F32) + carry[...]
        carry[...] = dlog[0:1, :]
        df = dlog * (1.0 - _sigmoid(f_ref[...] + b_ref[...]))
        df_ref[...] = df
        db_ref[...] += jnp.sum(df, axis=0, keepdims=True)

    rev = lambda i: nb - 1 - i
    return pl.pallas_call(
        body, name=name, grid=(nb,),
        in_specs=[pl.BlockSpec((LANES, ATT_BLOCK), lambda i: (0, rev(i))),
                  pl.BlockSpec((ATT_BLOCK, LANES), lambda i: (rev(i), 0)),
                  pl.BlockSpec((1, LANES), lambda i: (0, 0))],
        out_specs=[pl.BlockSpec((ATT_BLOCK, LANES), lambda i: (rev(i), 0)),
                   pl.BlockSpec((1, LANES), lambda i: (0, 0))],
        out_shape=[jax.ShapeDtypeStruct((s, LANES), F32), jax.ShapeDtypeStruct((1, LANES), F32)],
        scratch_shapes=[pltpu.VMEM((1, LANES), F32)], compiler_params=_params(),
    )(dcum_t, f, b_f)


def _causal(qi, ki):
    r = lax.broadcasted_iota(jnp.int32, (ATT_BLOCK, ATT_BLOCK), 0) + qi * ATT_BLOCK
    c = lax.broadcasted_iota(jnp.int32, (ATT_BLOCK, ATT_BLOCK), 1) + ki * ATT_BLOCK
    return c <= r


def _head_mask():
    return lax.broadcasted_iota(jnp.int32, (1, LANES), 1) < FOX_HEAD_DIM


def _attn_fwd(qkv, cum_b, cum_r, *, name):
    s = qkv.shape[0]
    nq = s // ATT_BLOCK
    scale = FOX_HEAD_DIM ** -0.5
    npair = HEAD_PAIRS

    def body(q_ref, k_ref, v_ref, cq_ref, ck_ref, o_ref, l_ref):
        qi = pl.program_id(1)
        m0 = _head_mask()
        q2 = q_ref[...]
        zero = jnp.zeros_like(q2)
        qs = (jnp.where(m0, q2, zero), jnp.where(m0, zero, q2))
        cqs = (cq_ref[0], cq_ref[1])

        def step(ki, carry, masked):
            off = pl.multiple_of(ki * ATT_BLOCK, ATT_BLOCK)
            k2 = k_ref[pl.ds(off, ATT_BLOCK), :]
            v2 = v_ref[pl.ds(off, ATT_BLOCK), :]
            out = []
            for hh in range(2):
                m, l, acc = carry[hh]
                sc = _dot(qs[hh], k2, _NT) * scale + (cqs[hh] - ck_ref[hh:hh + 1, pl.ds(off, ATT_BLOCK)])
                if masked:
                    sc = jnp.where(_causal(qi, ki), sc, -1e30)
                m_new = jnp.maximum(m, jnp.max(sc, axis=-1, keepdims=True))
                alpha = jnp.exp(m - m_new)
                p = jnp.exp(sc - m_new)
                l = alpha * l + jnp.sum(p, axis=-1, keepdims=True)
                acc = alpha * acc + _dot(p, v2, _NN)
                out.append((m_new, l, acc))
            return tuple(out)

        init = tuple((jnp.full((ATT_BLOCK, 1), -1e30, F32), jnp.zeros((ATT_BLOCK, 1), F32),
                      jnp.zeros((ATT_BLOCK, LANES), F32)) for _ in range(2))
        carry = lax.fori_loop(0, qi, lambda ki, c: step(ki, c, False), init)
        (ma, la, acca), (mb, lb, accb) = step(qi, carry, True)
        o_ref[...] = jnp.where(m0, acca / la, accb / lb).astype(o_ref.dtype)
        l_ref[0] = jnp.broadcast_to(ma + jnp.log(la), (ATT_BLOCK, LANES))
        l_ref[1] = jnp.broadcast_to(mb + jnp.log(lb), (ATT_BLOCK, LANES))

    stat = pl.BlockSpec((None, 2, ATT_BLOCK, LANES), lambda j, i: (j, 0, i, 0))
    row = pl.BlockSpec((None, 2, s), lambda j, i: (j, 0, 0))
    return pl.pallas_call(
        body, name=name, grid=(npair, nq),
        in_specs=[pl.BlockSpec((ATT_BLOCK, LANES), lambda j, i: (i, j)),
                  pl.BlockSpec((s, LANES), lambda j, i: (0, npair + j)),
                  pl.BlockSpec((s, LANES), lambda j, i: (0, 2 * npair + j)),
                  stat, row],
        out_specs=[pl.BlockSpec((ATT_BLOCK, LANES), lambda j, i: (i, j)), stat],
        out_shape=[jax.ShapeDtypeStruct((s, FOX_WIDTH), BF16),
                   jax.ShapeDtypeStruct((npair, 2, s, LANES), F32)],
        compiler_params=_params(),
    )(qkv, qkv, qkv, cum_b, cum_r)


def _attn_delta(qkv, do, lse_b, cum_b, cum_r, *, name):
    s = qkv.shape[0]
    nq = s // ATT_BLOCK
    scale = FOX_HEAD_DIM ** -0.5
    npair = HEAD_PAIRS

    def body(q_ref, k_ref, v_ref, do_ref, l_ref, cq_ref, ck_ref, d_ref):
        qi = pl.program_id(1)
        m0 = _head_mask()
        q2 = q_ref[...]
        do2 = do_ref[...]
        qs = (jnp.where(m0, q2, jnp.zeros_like(q2)), jnp.where(m0, jnp.zeros_like(q2), q2))
        dos = (jnp.where(m0, do2, jnp.zeros_like(do2)), jnp.where(m0, jnp.zeros_like(do2), do2))

        def step(ki, carry, masked):
            off = pl.multiple_of(ki * ATT_BLOCK, ATT_BLOCK)
            k2 = k_ref[pl.ds(off, ATT_BLOCK), :]
            v2 = v_ref[pl.ds(off, ATT_BLOCK), :]
            out = []
            for hh in range(2):
                sc = _dot(qs[hh], k2, _NT) * scale + (cq_ref[hh] - ck_ref[hh:hh + 1, pl.ds(off, ATT_BLOCK)])
                p = jnp.exp(sc - l_ref[hh])
                if masked:
                    p = jnp.where(_causal(qi, ki), p, 0.0)
                out.append(carry[hh] + jnp.sum(p * _dot(dos[hh], v2, _NT), axis=-1, keepdims=True))
            return tuple(out)

        init = (jnp.zeros((ATT_BLOCK, 1), F32), jnp.zeros((ATT_BLOCK, 1), F32))
        carry = lax.fori_loop(0, qi, lambda ki, c: step(ki, c, False), init)
        da, db = step(qi, carry, True)
        d_ref[0] = jnp.broadcast_to(da, (ATT_BLOCK, LANES))
        d_ref[1] = jnp.broadcast_to(db, (ATT_BLOCK, LANES))

    stat = pl.BlockSpec((None, 2, ATT_BLOCK, LANES), lambda j, i: (j, 0, i, 0))
    return pl.pallas_call(
        body, name=name, grid=(npair, nq),
        in_specs=[pl.BlockSpec((ATT_BLOCK, LANES), lambda j, i: (i, j)),
                  pl.BlockSpec((s, LANES), lambda j, i: (0, npair + j)),
                  pl.BlockSpec((s, LANES), lambda j, i: (0, 2 * npair + j)),
                  pl.BlockSpec((ATT_BLOCK, LANES), lambda j, i: (i, j)),
                  stat, stat, pl.BlockSpec((None, 2, s), lambda j, i: (j, 0, 0))],
        out_specs=stat,
        out_shape=jax.ShapeDtypeStruct((npair, 2, s, LANES), F32), compiler_params=_params(),
    )(qkv, qkv, qkv, do, lse_b, cum_b, cum_r)


def _attn_bwd(qkv, do, lse_b, delta_b, cum_b, cum_r, *, name):
    s = qkv.shape[0]
    nq = s // ATT_BLOCK
    scale = FOX_HEAD_DIM ** -0.5
    npair = HEAD_PAIRS

    def body(q_ref, k_ref, v_ref, do_ref, l_ref, dl_ref, cq_ref, ck_ref, dq_ref, dk_ref, dv_ref, dc_ref):
        ki = pl.program_id(1)
        m0 = _head_mask()
        k2 = k_ref[...]
        v2 = v_ref[...]
        koff = pl.multiple_of(ki * ATT_BLOCK, ATT_BLOCK)

        @pl.when(ki == 0)
        def _():
            dq_ref[...] = jnp.zeros_like(dq_ref)

        def step(qi, carry, masked):
            off = pl.multiple_of(qi * ATT_BLOCK, ATT_BLOCK)
            q2 = q_ref[pl.ds(off, ATT_BLOCK), :]
            do2 = do_ref[pl.ds(off, ATT_BLOCK), :]
            qzero = jnp.zeros_like(q2)
            dzero = jnp.zeros_like(do2)
            out = []
            dqs = []
            for hh in range(2):
                dk_acc, dv_acc, dc_acc = carry[hh]
                keep = m0 if hh == 0 else jnp.logical_not(m0)
                qh = jnp.where(keep, q2, qzero)
                doh = jnp.where(keep, do2, dzero)
                sc = _dot(qh, k2, _NT) * scale + (cq_ref[hh, pl.ds(off, ATT_BLOCK), :]
                                                 - ck_ref[hh:hh + 1, pl.ds(koff, ATT_BLOCK)])
                p = jnp.exp(sc - l_ref[hh, pl.ds(off, ATT_BLOCK), :])
                if masked:
                    p = jnp.where(_causal(qi, ki), p, 0.0)
                dp = _dot(doh, v2, _NT)
                ds = p * (dp - dl_ref[hh, pl.ds(off, ATT_BLOCK), :])
                dv_acc = dv_acc + _dot(p, do2, _TN)
                dk_acc = dk_acc + _dot(ds, q2, _TN)
                dc_acc = dc_acc - jnp.sum(ds, axis=0, keepdims=True)
                dqs.append(_dot(ds, k2, _NN))
                out.append((dk_acc, dv_acc, dc_acc))
            dq_ref[pl.ds(off, ATT_BLOCK), :] += jnp.where(m0, dqs[0], dqs[1]) * scale
            return tuple(out)

        init = tuple((jnp.zeros((ATT_BLOCK, LANES), F32), jnp.zeros((ATT_BLOCK, LANES), F32),
                      jnp.zeros((1, ATT_BLOCK), F32)) for _ in range(2))
        carry = step(ki, init, True)
        (dka, dva, dca), (dkb, dvb, dcb) = lax.fori_loop(ki + 1, nq, lambda qi, c: step(qi, c, False), carry)
        dk_ref[...] = (jnp.where(m0, dka, dkb) * scale).astype(dk_ref.dtype)
        dv_ref[...] = jnp.where(m0, dva, dvb).astype(dv_ref.dtype)
        dc_ref[0:1, :] = dca
        dc_ref[1:2, :] = dcb

    stat = pl.BlockSpec((None, 2, s, LANES), lambda j, i: (j, 0, 0, 0))
    colfull = lambda base: pl.BlockSpec((s, LANES), lambda j, i: (0, base + j))
    colblk = lambda base: pl.BlockSpec((ATT_BLOCK, LANES), lambda j, i: (i, base + j))
    return pl.pallas_call(
        body, name=name, grid=(npair, nq),
        in_specs=[colfull(0), colblk(npair), colblk(2 * npair), colfull(0), stat, stat, stat,
                  pl.BlockSpec((None, 2, s), lambda j, i: (j, 0, 0))],
        out_specs=[colfull(0), colblk(0), colblk(0), pl.BlockSpec((None, 2, ATT_BLOCK), lambda j, i: (j, 0, i))],
        out_shape=[jax.ShapeDtypeStruct((s, FOX_WIDTH), F32), jax.ShapeDtypeStruct((s, FOX_WIDTH), BF16),
                   jax.ShapeDtypeStruct((s, FOX_WIDTH), BF16), jax.ShapeDtypeStruct((npair, 2, s), F32)],
        compiler_params=_params(),
    )(qkv, qkv, qkv, do, lse_b, delta_b, cum_b, cum_r)


ATT_TQ = 256
ATT_TK = 256
ATT_SCALE = FOX_HEAD_DIM ** -0.5
assert ATT_SCALE == 0.125 and ATT_TQ == ATT_TK


def _causal_t(qi, ki):
    kpos = lax.broadcasted_iota(jnp.int32, (ATT_TK, ATT_TQ), 0) + ki * ATT_TK
    qpos = lax.broadcasted_iota(jnp.int32, (ATT_TK, ATT_TQ), 1) + qi * ATT_TQ
    return kpos <= qpos


def _row_mask():
    return lax.broadcasted_iota(jnp.int32, (LANES, 1), 0) < FOX_HEAD_DIM


def _lane_tile(a, width):
    return a if a.shape[1] == width else jnp.tile(a, (1, width // a.shape[1]))


def _transpose_bf16(a):
    return a.astype(F32).T.astype(BF16)


def _attn_fwd_t(qkv, cum_b, cum_r, *, name):
    s = qkv.shape[0]
    nq = s // ATT_TQ
    npair = HEAD_PAIRS

    def body(q_ref, k_ref, v_ref, cq_ref, ck_ref, o_ref, ot_ref, l_ref, vt_ref):
        qi = pl.program_id(1)
        rows = _row_mask()

        @pl.when(qi == 0)
        def _():
            vt_ref[...] = _transpose_bf16(v_ref[...])

        qt = _transpose_bf16(q_ref[...]) * ATT_SCALE
        zero = jnp.zeros_like(qt)
        qts = (jnp.where(rows, qt, zero), jnp.where(rows, zero, qt))

        def step(ki, carry, masked):
            off = pl.multiple_of(ki * ATT_TK, ATT_TK)
            k2 = k_ref[pl.ds(off, ATT_TK), :]
            vt = vt_ref[:, pl.ds(off, ATT_TK)]
            out = []
            for hh in range(2):
                m, l, acc = carry[hh]
                bias = cq_ref[hh:hh + 1, :] - _lane_tile(ck_ref[hh, pl.ds(off, ATT_TK), :], ATT_TQ)
                sc = _dot(k2, qts[hh], _NN) + bias
                if masked:
                    sc = jnp.where(_causal_t(qi, ki), sc, -1e30)
                m_new = jnp.maximum(m, jnp.max(sc, axis=0, keepdims=True))
                alpha = jnp.exp(m - m_new)
                p = jnp.exp(sc - m_new)
                l = alpha * l + jnp.sum(p, axis=0, keepdims=True)
                p_hi = p.astype(BF16)
                p_lo = (p - p_hi.astype(F32)).astype(BF16)
                acc = alpha * acc + (_dot(vt, p_hi, _NN) + _dot(vt, p_lo, _NN))
                out.append((m_new, l, acc))
            return tuple(out)

        init = tuple((jnp.full((1, ATT_TQ), -1e30, F32), jnp.zeros((1, ATT_TQ), F32),
                      jnp.zeros((LANES, ATT_TQ), F32)) for _ in range(2))
        carry = lax.fori_loop(0, qi // 2, lambda kk, c: step(2 * kk + 1, step(2 * kk, c, False), False), init)
        carry = lax.cond(qi % 2 == 1, lambda c: step(qi - 1, c, False), lambda c: c, carry)
        (ma, la, acca), (mb, lb, accb) = step(qi, carry, True)
        ot = jnp.where(rows, acca / la, accb / lb)
        ot_ref[...] = ot
        o_ref[...] = ot.T.astype(o_ref.dtype)
        l_ref[0:1, :] = ma + jnp.log(la)
        l_ref[1:2, :] = mb + jnp.log(lb)

    row = pl.BlockSpec((None, 2, ATT_TQ), lambda j, i: (j, 0, i))
    return pl.pallas_call(
        body, name=name, grid=(npair, nq),
        in_specs=[pl.BlockSpec((ATT_TQ, LANES), lambda j, i: (i, j)),
                  pl.BlockSpec((s, LANES), lambda j, i: (0, npair + j)),
                  pl.BlockSpec((s, LANES), lambda j, i: (0, 2 * npair + j)),
                  row, pl.BlockSpec((None, 2, s, LANES), lambda j, i: (j, 0, 0, 0))],
        out_specs=[pl.BlockSpec((ATT_TQ, LANES), lambda j, i: (i, j)),
                   pl.BlockSpec((LANES, ATT_TQ), lambda j, i: (j, i)), row],
        out_shape=[jax.ShapeDtypeStruct((s, FOX_WIDTH), BF16), jax.ShapeDtypeStruct((FOX_WIDTH, s), F32),
                   jax.ShapeDtypeStruct((npair, 2, s), F32)],
        scratch_shapes=[pltpu.VMEM((LANES, s), BF16)],
        compiler_params=_params(),
    )(qkv, qkv, qkv, cum_r, cum_b)


def _attn_delta_t(do_t, o_t, *, name):
    s = o_t.shape[1]
    ts = _tile(s, 512)

    def body(do_ref, o_ref, d_ref):
        prod = do_ref[...].astype(F32) * o_ref[...]
        d_ref[0:1, :] = jnp.sum(prod[:FOX_HEAD_DIM], axis=0, keepdims=True)
        d_ref[1:2, :] = jnp.sum(prod[FOX_HEAD_DIM:], axis=0, keepdims=True)

    blk = pl.BlockSpec((LANES, ts), lambda j, i: (j, i))
    return pl.pallas_call(
        body, name=name, grid=(HEAD_PAIRS, s // ts), in_specs=[blk, blk],
        out_specs=pl.BlockSpec((None, 2, ts), lambda j, i: (j, 0, i)),
        out_shape=jax.ShapeDtypeStruct((HEAD_PAIRS, 2, s), F32), compiler_params=_params(),
    )(do_t, o_t)


def _attn_bwd_t(qkv, do, o_t, lse, cum_b, cum_r, *, name, dep=None):
    s = qkv.shape[0]
    nq = s // ATT_TQ
    npair = HEAD_PAIRS

    deps = [] if dep is None else [dep]

    def body(q_ref, k_ref, v_ref, do_ref, ot_ref, l_ref, cq_ref, ck_ref, *rest):
        dq_ref, dk_ref, dv_ref, dc_ref, qt_ref, dot_ref, dqt_ref, dl_ref = rest[len(deps):]
        ki = pl.program_id(1)
        m0 = _head_mask()
        rows = _row_mask()
        k2 = k_ref[...]
        v2 = v_ref[...]
        kt = _transpose_bf16(k2)
        ks = k2 * ATT_SCALE
        kz, vz = jnp.zeros_like(k2), jnp.zeros_like(v2)
        khs = (jnp.where(m0, ks, kz), jnp.where(m0, kz, ks))
        vhs = (jnp.where(m0, v2, vz), jnp.where(m0, vz, v2))
        cks = tuple(_lane_tile(ck_ref[hh], ATT_TQ) for hh in range(2))

        @pl.when(ki == 0)
        def _():
            dqt_ref[...] = jnp.zeros_like(dqt_ref)
            qt_ref[...] = _transpose_bf16(q_ref[...])
            do_t = do_ref[...].astype(F32).T
            dot_ref[...] = do_t.astype(BF16)
            prod = do_t * ot_ref[...]
            dl_ref[0:1, :] = jnp.sum(prod[:FOX_HEAD_DIM], axis=0, keepdims=True)
            dl_ref[1:2, :] = jnp.sum(prod[FOX_HEAD_DIM:], axis=0, keepdims=True)

        def step(qi, carry, masked):
            off = pl.multiple_of(qi * ATT_TQ, ATT_TQ)
            q2 = q_ref[pl.ds(off, ATT_TQ), :]
            do2 = do_ref[pl.ds(off, ATT_TQ), :]
            qt = qt_ref[:, pl.ds(off, ATT_TQ)]
            dot_ = dot_ref[:, pl.ds(off, ATT_TQ)]
            out, dqs = [], []
            for hh in range(2):
                dk_acc, dv_acc, dc_acc = carry[hh]
                sc = _dot(khs[hh], qt, _NN) + (cq_ref[hh:hh + 1, pl.ds(off, ATT_TQ)] - cks[hh])
                p = jnp.exp(sc - l_ref[hh:hh + 1, pl.ds(off, ATT_TQ)])
                if masked:
                    p = jnp.where(_causal_t(qi, ki), p, 0.0)
                dp = _dot(vhs[hh], dot_, _NN)
                ds = p * (dp - dl_ref[hh:hh + 1, pl.ds(off, ATT_TQ)])
                dc_acc = dc_acc - jnp.sum(ds, axis=1, keepdims=True)
                dss = (ds * ATT_SCALE).astype(BF16)
                dv_acc = dv_acc + _dot(p, do2, _NN)
                dk_acc = dk_acc + _dot(dss, q2, _NN)
                dqs.append(_dot(kt, dss, _NN))
                out.append((dk_acc, dv_acc, dc_acc))
            dqt_ref[:, pl.ds(off, ATT_TQ)] += jnp.where(rows, dqs[0], dqs[1])
            return tuple(out)

        init = tuple((jnp.zeros((ATT_TK, LANES), F32), jnp.zeros((ATT_TK, LANES), F32),
                      jnp.zeros((ATT_TK, 1), F32)) for _ in range(2))
        carry = step(ki, init, True)
        rest = nq - 1 - ki
        carry = lax.fori_loop(
            0, rest // 2, lambda t, c: step(ki + 2 + 2 * t, step(ki + 1 + 2 * t, c, False), False), carry)
        carry = lax.cond(rest % 2 == 1, lambda c: step(nq - 1, c, False), lambda c: c, carry)
        (dka, dva, dca), (dkb, dvb, dcb) = carry
        dk_ref[...] = jnp.where(m0, dka, dkb).astype(dk_ref.dtype)
        dv_ref[...] = jnp.where(m0, dva, dvb).astype(dv_ref.dtype)
        dc_ref[0] = jnp.broadcast_to(dca, (ATT_TK, LANES))
        dc_ref[1] = jnp.broadcast_to(dcb, (ATT_TK, LANES))

        @pl.when(ki == nq - 1)
        def _():
            dq_ref[...] = dqt_ref[...].T.astype(dq_ref.dtype)

    colfull = lambda base: pl.BlockSpec((s, LANES), lambda j, i: (0, base + j))
    colblk = lambda base: pl.BlockSpec((ATT_TK, LANES), lambda j, i: (i, base + j))
    stat = pl.BlockSpec((None, 2, s), lambda j, i: (j, 0, 0))
    bcast = pl.BlockSpec((None, 2, ATT_TK, LANES), lambda j, i: (j, 0, i, 0))
    grad = jax.ShapeDtypeStruct((s, FOX_WIDTH), BF16)
    return pl.pallas_call(
        body, name=name, grid=(npair, nq),
        in_specs=[colfull(0), colblk(npair), colblk(2 * npair), colfull(0),
                  pl.BlockSpec((LANES, s), lambda j, i: (j, 0)), stat, stat, bcast]
                 + [pl.BlockSpec(memory_space=pl.ANY)] * len(deps),
        out_specs=[colfull(0), colblk(0), colblk(0), bcast],
        out_shape=[grad, grad, grad, jax.ShapeDtypeStruct((npair, 2, s, LANES), F32)],
        scratch_shapes=[pltpu.VMEM((LANES, s), BF16), pltpu.VMEM((LANES, s), BF16), pltpu.VMEM((LANES, s), F32),
                        pltpu.VMEM((2, s), F32)],
        compiler_params=_params(),
    )(qkv, qkv, qkv, do, o_t, lse, cum_r, cum_b, *deps)


def _merge_fwd(zg, ya, yb, *, name, tm=256):
    s, d = ya.shape
    tm = _tile(s, tm)

    def body(zg_ref, ya_ref, yb_ref, m_ref):
        ga = _sigmoid(zg_ref[:, :d].astype(F32))
        gb = _sigmoid(zg_ref[:, d:].astype(F32))
        m_ref[...] = (ga * ya_ref[...].astype(F32) + gb * yb_ref[...].astype(F32)).astype(m_ref.dtype)

    row = pl.BlockSpec((tm, d), lambda i: (i, 0))
    row2 = pl.BlockSpec((tm, 2 * d), lambda i: (i, 0))
    return pl.pallas_call(
        body, name=name, grid=(s // tm,), in_specs=[row2, row, row], out_specs=row,
        out_shape=jax.ShapeDtypeStruct((s, d), BF16), compiler_params=_params(),
    )(zg, ya, yb)


def _merge_bwd(dm, zg, ya, yb, *, name, tm=256):
    s, d = ya.shape
    tm = _tile(s, tm)

    def body(dm_ref, zg_ref, ya_ref, yb_ref, dzg_ref, dya_ref, dyb_ref):
        dmv = dm_ref[...].astype(F32)
        ga = _sigmoid(zg_ref[:, :d].astype(F32))
        gb = _sigmoid(zg_ref[:, d:].astype(F32))
        dzg_ref[:, :d] = (dmv * ya_ref[...].astype(F32) * ga * (1.0 - ga)).astype(dzg_ref.dtype)
        dzg_ref[:, d:] = (dmv * yb_ref[...].astype(F32) * gb * (1.0 - gb)).astype(dzg_ref.dtype)
        dya_ref[...] = (dmv * ga).astype(dya_ref.dtype)
        dyb_ref[...] = (dmv * gb).astype(dyb_ref.dtype)

    row = pl.BlockSpec((tm, d), lambda i: (i, 0))
    row2 = pl.BlockSpec((tm, 2 * d), lambda i: (i, 0))
    return pl.pallas_call(
        body, name=name, grid=(s // tm,), in_specs=[row, row2, row, row], out_specs=[row2, row, row],
        out_shape=[jax.ShapeDtypeStruct((s, 2 * d), BF16), jax.ShapeDtypeStruct((s, d), BF16),
                   jax.ShapeDtypeStruct((s, d), BF16)],
        compiler_params=_params(),
    )(dm, zg, ya, yb)


def _shift_down(u, k, row):
    return jnp.where(row >= k, pltpu.roll(u, k, 0), 0.0)


def _shift_up(u, k, row):
    n = u.shape[0]
    return jnp.where(row < n - k, pltpu.roll(u, n - k, 0), 0.0)


def _conv_act_fwd(up_a, up_b, cw_a, cw_b, cb_a, cb_b, *, name, tc=128):
    s, f = up_a.shape
    tc = _tile(f, tc)

    def body(ua_ref, ub_ref, wa_ref, wb_ref, ba_ref, bb_ref, act_ref):
        row = lax.broadcasted_iota(jnp.int32, (s, tc), 0)

        def conv(u_ref, w_ref, b_ref):
            u = u_ref[...].astype(F32)
            return (b_ref[...] + w_ref[0:1, :] * _shift_down(u, 2, row)
                    + w_ref[1:2, :] * _shift_down(u, 1, row) + w_ref[2:3, :] * u)

        ca = conv(ua_ref, wa_ref, ba_ref)
        cb = conv(ub_ref, wb_ref, bb_ref)
        act_ref[...] = (_gelu(ca) * cb).astype(act_ref.dtype)

    col = pl.BlockSpec((s, tc), lambda j: (0, j))
    w3 = pl.BlockSpec((3, tc), lambda j: (0, j))
    b1 = pl.BlockSpec((1, tc), lambda j: (0, j))
    return pl.pallas_call(
        body, name=name, grid=(f // tc,), in_specs=[col, col, w3, w3, b1, b1], out_specs=col,
        out_shape=jax.ShapeDtypeStruct((s, f), BF16), compiler_params=_params(),
    )(up_a, up_b, cw_a, cw_b, cb_a, cb_b)


def _conv_act_bwd(up_a, up_b, dact, cw_a, cw_b, cb_a, cb_b, *, name, tc=128):
    s, f = up_a.shape
    tc = _tile(f, tc)

    def body(ua_ref, ub_ref, da_ref, wa_ref, wb_ref, ba_ref, bb_ref, dua_ref, dub_ref, dwa_ref, dwb_ref):
        row = lax.broadcasted_iota(jnp.int32, (s, tc), 0)

        def conv(u_ref, w_ref, b_ref):
            u = u_ref[...].astype(F32)
            u1 = _shift_down(u, 1, row)
            u2 = _shift_down(u, 2, row)
            return u, u1, u2, b_ref[...] + w_ref[0:1, :] * u2 + w_ref[1:2, :] * u1 + w_ref[2:3, :] * u

        def back(dc, taps, w_ref, du_ref, dw_ref):
            u, u1, u2 = taps
            dw_ref[0:1, :] = jnp.sum(dc * u2, axis=0, keepdims=True)
            dw_ref[1:2, :] = jnp.sum(dc * u1, axis=0, keepdims=True)
            dw_ref[2:3, :] = jnp.sum(dc * u, axis=0, keepdims=True)
            dw_ref[3:4, :] = jnp.sum(dc, axis=0, keepdims=True)
            du = (w_ref[2:3, :] * dc + w_ref[1:2, :] * _shift_up(dc, 1, row)
                  + w_ref[0:1, :] * _shift_up(dc, 2, row))
            du_ref[...] = du.astype(du_ref.dtype)

        ua, ua1, ua2, ca = conv(ua_ref, wa_ref, ba_ref)
        ub, ub1, ub2, cb = conv(ub_ref, wb_ref, bb_ref)
        g, dg = _gelu_and_grad(ca)
        dact_v = da_ref[...].astype(F32)
        back(dact_v * cb * dg, (ua, ua1, ua2), wa_ref, dua_ref, dwa_ref)
        back(dact_v * g, (ub, ub1, ub2), wb_ref, dub_ref, dwb_ref)

    col = pl.BlockSpec((s, tc), lambda j: (0, j))
    w3 = pl.BlockSpec((3, tc), lambda j: (0, j))
    w4 = pl.BlockSpec((4, tc), lambda j: (0, j))
    b1 = pl.BlockSpec((1, tc), lambda j: (0, j))
    return pl.pallas_call(
        body, name=name, grid=(f // tc,), in_specs=[col, col, col, w3, w3, b1, b1],
        out_specs=[col, col, w4, w4],
        out_shape=[jax.ShapeDtypeStruct((s, f), BF16), jax.ShapeDtypeStruct((s, f), BF16),
                   jax.ShapeDtypeStruct((4, f), F32), jax.ShapeDtypeStruct((4, f), F32)],
        compiler_params=_params(),
    )(up_a, up_b, dact, cw_a, cw_b, cb_a, cb_b)


def _ple_final(x2, ple, zp, target, g_final, *, name, tm=256):
    s, d = x2.shape
    tm = _tile(s, tm)

    def body(x_ref, ple_ref, zp_ref, t_ref, g_ref, dx_ref, dple_ref, dzp_ref, dg_ref, loss_ref):
        @pl.when(pl.program_id(0) == 0)
        def _():
            dg_ref[...] = jnp.zeros_like(dg_ref)
            loss_ref[...] = jnp.zeros_like(loss_ref)

        gp = _sigmoid(zp_ref[...].astype(F32))
        plev = ple_ref[...].astype(F32)
        x3 = x_ref[...] + plev * gp
        r = lax.rsqrt(jnp.mean(x3 * x3, axis=-1, keepdims=True) + EPS)
        xhat = x3 * r
        gv = g_ref[...]
        diff = xhat * gv - t_ref[...]
        loss_ref[...] += 0.5 * jnp.sum(jnp.mean(diff * diff, axis=-1, keepdims=True), axis=0, keepdims=True)
        dy = diff * (1.0 / d)
        dg_ref[...] += jnp.sum(dy * xhat, axis=0, keepdims=True)
        dyg = dy * gv
        dx3 = r * (dyg - xhat * jnp.mean(dyg * xhat, axis=-1, keepdims=True))
        dx_ref[...] = dx3
        dple_ref[...] = (dx3 * gp).astype(dple_ref.dtype)
        dzp_ref[...] = (dx3 * plev * gp * (1.0 - gp)).astype(dzp_ref.dtype)

    row = pl.BlockSpec((tm, d), lambda i: (i, 0))
    vec = pl.BlockSpec((1, d), lambda i: (0, 0))
    return pl.pallas_call(
        body, name=name, grid=(s // tm,), in_specs=[row, row, row, row, vec],
        out_specs=[row, row, row, vec, pl.BlockSpec((1, LANES), lambda i: (0, 0))],
        out_shape=[jax.ShapeDtypeStruct((s, d), F32), jax.ShapeDtypeStruct((s, d), BF16),
                   jax.ShapeDtypeStruct((s, d), BF16), jax.ShapeDtypeStruct((1, d), F32),
                   jax.ShapeDtypeStruct((1, LANES), F32)],
        compiler_params=_params(),
    )(x2, ple, zp, target, g_final)


def _device_step(x, p, target, w, get_w_in=None, get_w_rest=None, on_grads_ffn=None, on_grads_small=None,
                 on_grads_mix=None):
    s = x.shape[0]
    g = {}
    w = dict(w)

    h = _rms_fwd(x, w["norm_mix_g"], name="rms_mix", dep=w.get("first_dep"))
    if get_w_in is not None:
        w.update(get_w_in(h))
    z_uv = _mm(h, w["w_uv"], mode="nn", out_dtype=BF16, name="proj_uv", tm=1024, dep=w.get("proj_dep"))
    qkv = _mm(h, w["w_qkv"], mode="nn", out_dtype=BF16, name="proj_qkv", tm=1024)
    zg = _mm(h, w["w_g"], mode="nn", out_dtype=BF16, name="proj_gate", tm=1024)
    f = _mm(h, w["w_f"], mode="nn", out_dtype=F32, name="proj_f", tm=1024)

    a = _gmlp_fwd(z_uv, w["gmlp_ln_g"], w["gmlp_ln_b"], w["gmlp_w_s"], w["gmlp_b_s_t"], name="gmlp_fwd")

    cum_b, cum_t = _fox_cum(f, w["b_f"], name="fox_cum")
    cum_b = cum_b.reshape(HEAD_PAIRS, 2, s, LANES)
    cum_r = cum_t[:FOX_HEADS].reshape(HEAD_PAIRS, 2, s)
    b, o_t, lse = _attn_fwd_t(qkv, cum_b, cum_r, name="attn_fwd")
    if get_w_rest is not None:
        w.update(get_w_rest(b))

    ya = _mm(a, w["w_branch_a"], mode="nn", out_dtype=BF16, name="branch_a", tm=1024)
    yb = _mm(b, w["w_branch_b"], mode="nn", out_dtype=BF16, name="branch_b", tm=1024)
    merged = _merge_fwd(zg, ya, yb, name="merge_fwd")
    x1 = _mm(merged, w["w_out"], mode="nn", out_dtype=F32, name="proj_out", add=x, tm=1024)

    h2 = _rms_fwd(x1, w["norm_ffn_g"], name="rms_ffn")
    up_a = _mm(h2, w["w_up_a"], mode="nn", out_dtype=BF16, name="up_a", tm=1024, tn=D_FF // 2)
    up_b = _mm(h2, w["w_up_b"], mode="nn", out_dtype=BF16, name="up_b", tm=1024, tn=D_FF // 2)
    cw, cb = w["conv_w"], w["conv_b"]
    conv_args = (cw[:, :D_FF], cw[:, D_FF:], cb[:, :D_FF], cb[:, D_FF:])
    act = _conv_act_fwd(up_a, up_b, *conv_args, name="conv_act_fwd")
    x2 = _mm(act, w["w_down"], mode="nn", out_dtype=F32, name="down", add=x1, tm=512)

    h3 = _rms_fwd(x2, w["norm_ple_g"], name="rms_ple")
    ple = _mm(p, w["w_ple"], mode="nn", out_dtype=BF16, name="ple_proj", tm=1024)
    zp = _mm(h3, w["w_ple_gate"], mode="nn", out_dtype=BF16, name="ple_gate", tm=1024)
    dx3, dple, dzp, g["norm_final_g"], loss = _ple_final(x2, ple, zp, target, w["norm_final_g"], name="ple_final")

    g["w_ple"] = _mm(p, dple, mode="tn", out_dtype=BF16, name="dw_ple")
    g["w_ple_gate"] = _mm(h3, dzp, mode="tn", out_dtype=BF16, name="dw_ple_gate")
    dh3 = _mm(dzp, w["w_ple_gate"], mode="nt", out_dtype=BF16, name="dh3")
    dx2, dx2_b, g["norm_ple_g"] = _rms_bwd(x2, w["norm_ple_g"], dh3, dx3, name="rms_ple_bwd")

    g["w_down"] = _mm(act, dx2_b, mode="tn", out_dtype=BF16, name="dw_down", tm=D_FF // 2)
    dact = _mm(dx2_b, w["w_down"], mode="nt", out_dtype=BF16, name="dact", tn=D_FF // 2)
    dup_a, dup_b, dcw_a, dcw_b = _conv_act_bwd(up_a, up_b, dact, *conv_args, name="conv_act_bwd")
    g["conv_w"] = jnp.concatenate([dcw_a[:3], dcw_b[:3]], axis=1)
    g["conv_b"] = jnp.concatenate([dcw_a[3:], dcw_b[3:]], axis=1)
    g["w_up_a"] = _mm(h2, dup_a, mode="tn", out_dtype=BF16, name="dw_up_a", tn=D_FF // 2)
    g["w_up_b"] = _mm(h2, dup_b, mode="tn", out_dtype=BF16, name="dw_up_b", tn=D_FF // 2)
    dh2 = _mm_nt_sum([(dup_a, w["w_up_a"]), (dup_b, w["w_up_b"])], out_dtype=BF16, name="dh2")
    dx1, dx1_b, g["norm_ffn_g"] = _rms_bwd(x1, w["norm_ffn_g"], dh2, dx2, name="rms_ffn_bwd")
    dep = on_grads_ffn(g) if on_grads_ffn is not None else None

    g["w_out"] = _mm(merged, dx1_b, mode="tn", out_dtype=BF16, name="dw_out")
    dmerged = _mm(dx1_b, w["w_out"], mode="nt", out_dtype=BF16, name="dmerged", dep=dep)
    dzg, dya, dyb = _merge_bwd(dmerged, zg, ya, yb, name="merge_bwd")
    g["w_branch_a"] = _mm(a, dya, mode="tn", out_dtype=BF16, name="dw_branch_a")
    g["w_branch_b"] = _mm(b, dyb, mode="tn", out_dtype=BF16, name="dw_branch_b")
    da = _mm(dya, w["w_branch_a"], mode="nt", out_dtype=BF16, name="da")
    db = _mm(dyb, w["w_branch_b"], mode="nt", out_dtype=BF16, name="db")

    dz_uv, g["gmlp_w_s"], dbs_t, g["gmlp_ln_g"], g["gmlp_ln_b"] = _gmlp_bwd(
        z_uv, da, w["gmlp_ln_g"], w["gmlp_ln_b"], w["gmlp_w_s"], w["gmlp_b_s_t"], name="gmlp_bwd")
    g["gmlp_b_s"] = dbs_t[:, :GMLP_GROUPS].T
    dep = on_grads_small(g) if on_grads_small is not None else None

    dq, dk, dv, dcum_b = _attn_bwd_t(qkv, db, o_t, lse, cum_b, cum_r, name="attn_bwd", dep=dep)
    dcum_t = jnp.pad(dcum_b[..., 0].reshape(FOX_HEADS, s), ((0, LANES - FOX_HEADS), (0, 0)))
    df, g["b_f"] = _fox_dlogit(dcum_t, f, w["b_f"], name="fox_dlogit")
    dqkv = jnp.concatenate([dq, dk, dv], axis=1)

    g["w_uv"] = _mm(h, dz_uv, mode="tn", out_dtype=BF16, name="dw_uv")
    g["w_qkv"] = _mm(h, dqkv, mode="tn", out_dtype=BF16, name="dw_qkv")
    g["w_f"] = _mm(h, df, mode="tn", out_dtype=BF16, name="dw_f")
    g["w_g"] = _mm(h, dzg, mode="tn", out_dtype=BF16, name="dw_g")
    dep = on_grads_mix(g) if on_grads_mix is not None else None
    dh = _mm_nt_sum([(dz_uv, w["w_uv"]), (dqkv, w["w_qkv"]), (df, w["w_f"]), (dzg, w["w_g"])],
                    out_dtype=BF16, name="dh", dep=dep)
    dx0, _, g["norm_mix_g"] = _rms_bwd(x, w["norm_mix_g"], dh, dx1, name="rms_mix_bwd")
    return loss, dx0, g


def _coords():
    return lax.axis_index("x"), lax.axis_index("y"), lax.axis_index("c")


def _other_chips(x, y):
    return [(1 - x, y), (x, 1 - y), (1 - x, 1 - y)]


def _remote(src, dst, send_sem, recv_sem, dev):
    return pltpu.make_async_remote_copy(src_ref=src, dst_ref=dst, send_sem=send_sem, recv_sem=recv_sem,
                                        device_id=dev, device_id_type=MESH)


_ANY = pl.BlockSpec(memory_space=pl.ANY)


def _gather_weights(halved, whole, *, name):
    nh, n = len(halved), len(halved) + len(whole)
    arrays = list(halved) + list(whole)

    def body(*refs):
        ins, outs = refs[:n], refs[n:2 * n]
        send_sems, recv_sems = refs[2 * n:]
        x, y, c = _coords()
        me, sib = 2 * x + y, (x, y, 1 - c)
        chips = _other_chips(x, y)

        def half(i, which):
            h = ins[i].shape[0] // 2
            return pl.ds(pl.multiple_of(which * h, 16), h)

        sends = []
        for i in range(n):
            src, dst = (ins[i].at[half(i, c)], outs[i].at[me, half(i, c)]) if i < nh else (ins[i], outs[i].at[me])
            for k, (cx, cy) in enumerate(chips):
                cp = _remote(src, dst, send_sems.at[i, k], recv_sems.at[i, k], (cx, cy, c))
                cp.start()
                sends.append(cp)
        for i in range(n):
            for k, (cx, cy) in enumerate(chips):
                got = outs[i].at[2 * cx + cy, half(i, c)] if i < nh else outs[i].at[2 * cx + cy]
                _remote(got, got, send_sems.at[i, k], recv_sems.at[i, k], sib).wait_recv()
                if i < nh:
                    cp = _remote(got, got, send_sems.at[i, 3 + k], recv_sems.at[i, 3 + k], sib)
                    cp.start()
                    sends.append(cp)
        for i in range(nh):
            for k, (cx, cy) in enumerate(chips):
                got = outs[i].at[2 * cx + cy, half(i, 1 - c)]
                _remote(got, got, send_sems.at[i, 3 + k], recv_sems.at[i, 3 + k], sib).wait_recv()
        for cp in sends:
            cp.wait_send()

    outs = pl.pallas_call(
        body, name=name, in_specs=[_ANY] * n, out_specs=[_ANY] * n,
        out_shape=[jax.ShapeDtypeStruct((N_CHIPS,) + a.shape, a.dtype) for a in arrays],
        scratch_shapes=[pltpu.SemaphoreType.DMA((n, 6)), pltpu.SemaphoreType.DMA((n, 6))],
        compiler_params=_params(),
    )(*arrays)
    chip = 2 * lax.axis_index("x") + lax.axis_index("y")
    return [lax.dynamic_update_index_in_dim(o, a, chip, 0) for o, a in zip(outs, arrays)]


def _pair_exchange(gs, *, name):
    n = len(gs)

    def body(*refs):
        ins, outs = refs[:n], refs[n:2 * n]
        send_sems, recv_sems = refs[2 * n:]
        x, y, c = _coords()
        copies = []
        for i in range(n):
            for j in range(N_CHIPS):
                cp = _remote(ins[i].at[j, 1 - c], outs[i].at[j], send_sems.at[i, j], recv_sems.at[i, j], (x, y, 1 - c))
                cp.start()
                copies.append(cp)
        for cp in copies:
            cp.wait()

    return pl.pallas_call(
        body, name=name, in_specs=[_ANY] * n, out_specs=[_ANY] * n,
        out_shape=[jax.ShapeDtypeStruct((N_CHIPS,) + a.shape[2:], a.dtype) for a in gs],
        scratch_shapes=[pltpu.SemaphoreType.DMA((n, N_CHIPS)), pltpu.SemaphoreType.DMA((n, N_CHIPS))],
        compiler_params=_params(),
    )(*gs)


def _chip_exchange(ss, *, name):
    n = len(ss)

    def body(*refs):
        ins, outs = refs[:n], refs[n:2 * n]
        send_sems, recv_sems = refs[2 * n:]
        x, y, c = _coords()
        me = 2 * x + y
        chips = _other_chips(x, y)
        sends = []
        for i in range(n):
            for k, (cx, cy) in enumerate(chips):
                cp = _remote(ins[i].at[2 * cx + cy], outs[i].at[me], send_sems.at[i, k], recv_sems.at[i, k], (cx, cy, c))
                cp.start()
                sends.append(cp)
        for i in range(n):
            for k, (cx, cy) in enumerate(chips):
                got = outs[i].at[2 * cx + cy]
                _remote(got, got, send_sems.at[i, k], recv_sems.at[i, k], (cx, cy, c)).wait_recv()
        for cp in sends:
            cp.wait_send()

    return pl.pallas_call(
        body, name=name, in_specs=[_ANY] * n, out_specs=[_ANY] * n,
        out_shape=[jax.ShapeDtypeStruct(a.shape, a.dtype) for a in ss],
        scratch_shapes=[pltpu.SemaphoreType.DMA((n, 3)), pltpu.SemaphoreType.DMA((n, 3))],
        compiler_params=_params(),
    )(*ss)


def _pair_share(hs, *, name):
    n = len(hs)

    def body(*refs):
        ins, outs = refs[:n], refs[n:2 * n]
        send_sems, recv_sems = refs[2 * n:]
        x, y, c = _coords()
        copies = []
        for i in range(n):
            cp = _remote(ins[i], outs[i], send_sems.at[i], recv_sems.at[i], (x, y, 1 - c))
            cp.start()
            copies.append(cp)
        for cp in copies:
            cp.wait()

    return pl.pallas_call(
        body, name=name, in_specs=[_ANY] * n, out_specs=[_ANY] * n,
        out_shape=[jax.ShapeDtypeStruct(a.shape, a.dtype) for a in hs],
        scratch_shapes=[pltpu.SemaphoreType.DMA((n,)), pltpu.SemaphoreType.DMA((n,))],
        compiler_params=_params(),
    )(*hs)


def _all_exchange(vec, *, name):
    def body(v_ref, o_ref, send_sems, recv_sems, local_sem):
        x, y, c = _coords()
        me = 4 * x + 2 * y + c
        local = pltpu.make_async_copy(v_ref, o_ref.at[me], local_sem)
        local.start()
        copies = []
        k = 0
        for dx in (0, 1):
            for dy in (0, 1):
                for dc in (0, 1):
                    if dx or dy or dc:
                        peer = (1 - x if dx else x, 1 - y if dy else y, 1 - c if dc else c)
                        cp = _remote(v_ref, o_ref.at[me], send_sems.at[k], recv_sems.at[k], peer)
                        cp.start()
                        copies.append(cp)
                        k += 1
        for cp in copies:
            cp.wait()
        local.wait()

    return pl.pallas_call(
        body, name=name, in_specs=[_ANY], out_specs=_ANY,
        out_shape=jax.ShapeDtypeStruct((8,) + vec.shape, vec.dtype),
        scratch_shapes=[pltpu.SemaphoreType.DMA((7,)), pltpu.SemaphoreType.DMA((7,)), pltpu.SemaphoreType.DMA(())],
        compiler_params=_params(),
    )(vec)


_HBM = pl.BlockSpec(memory_space=pltpu.HBM)
_SEM = pl.BlockSpec(memory_space=pltpu.SEMAPHORE)
_EFFECT = pltpu.SideEffectType.DATAFLOW_SIDE_EFFECTING


def _copies_start(srcs, lands, plan, n_copies, *, name, after=()):
    ns, n = len(srcs), len(srcs) + len(lands)
    na = len(after)

    def body(*refs):
        send_sems, recv_sems = refs[n + na], refs[n + na + 1]
        token = refs[-1]
        for k, (src, dst, dev) in enumerate(plan(refs[:ns], refs[ns:n])):
            _remote(src, dst, send_sems.at[k], recv_sems.at[k], dev).start()
        token[...] = jnp.zeros_like(token)

    arrays = list(srcs) + list(lands)
    outs = pl.pallas_call(
        body, name=name,
        out_shape=(pltpu.SemaphoreType.DMA((n_copies,)), pltpu.SemaphoreType.DMA((n_copies,)),
                   *[pltpu.HBM(a.shape, a.dtype) for a in arrays], jax.ShapeDtypeStruct((8, LANES), F32)),
        in_specs=[_HBM] * n + [_ANY] * na,
        out_specs=(_SEM, _SEM, *[_HBM] * n, pl.BlockSpec(memory_space=pltpu.VMEM)),
        input_output_aliases={i: 2 + i for i in range(n)},
        compiler_params=pltpu.CompilerParams(has_side_effects=_EFFECT),
    )(*[pltpu.with_memory_space_constraint(a, pltpu.HBM) for a in arrays], *after)
    return outs[0], outs[1], list(outs[2:2 + ns]), list(outs[2 + ns:2 + n]), outs[-1]


def _copies_wait(send_sems, recv_sems, srcs, lands, plan, first, after, *, name):
    ns, n = len(srcs), len(srcs) + len(lands)

    def body(*refs):
        send, recv = refs[n], refs[n + 1]
        for k, (src, dst, dev) in enumerate(plan(refs[:ns], refs[ns:n])):
            cp = _remote(src, dst, send.at[first + k], recv.at[first + k], dev)
            cp.wait_send()
            cp.wait_recv()

    arrays = list(srcs) + list(lands)
    outs = pl.pallas_call(
        body, name=name, out_shape=tuple(pltpu.HBM(a.shape, a.dtype) for a in arrays),
        in_specs=[_HBM] * n + [_SEM, _SEM] + [_ANY] * len(after), out_specs=tuple([_HBM] * n),
        input_output_aliases={i: i for i in range(n)},
        compiler_params=pltpu.CompilerParams(has_side_effects=_EFFECT),
    )(*arrays, send_sems, recv_sems, *after)
    return list(outs[:ns]), list(outs[ns:])


def _gather_plan(halved):
    def plan(srcs, lands):
        x, y, c = _coords()
        me = 2 * x + y
        out = []
        for i, (src, land) in enumerate(zip(srcs, lands)):
            if halved[i]:
                h = src.shape[0] // 2
                rows = pl.ds(pl.multiple_of(c * h, 16), h)
                src, dst = src.at[rows], land.at[me, rows]
            else:
                dst = land.at[me]
            out += [(src, dst, (cx, cy, c)) for cx, cy in _other_chips(x, y)]
        return out
    return plan


def _forward_halves(lands, *, name):
    n = len(lands)

    def body(*refs):
        ins, outs = refs[:n], refs[n:2 * n]
        send_sems, recv_sems = refs[2 * n:]
        x, y, c = _coords()
        copies = []
        for i in range(n):
            h = ins[i].shape[1] // 2
            rows = pl.ds(pl.multiple_of(c * h, 16), h)
            for k, (cx, cy) in enumerate(_other_chips(x, y)):
                cp = _remote(ins[i].at[2 * cx + cy, rows], outs[i].at[2 * cx + cy, rows],
                             send_sems.at[i, k], recv_sems.at[i, k], (x, y, 1 - c))
                cp.start()
                copies.append(cp)
        for cp in copies:
            cp.wait()

    return pl.pallas_call(
        body, name=name, in_specs=[_ANY] * n, out_specs=[_ANY] * n,
        out_shape=[jax.ShapeDtypeStruct(a.shape, a.dtype) for a in lands],
        input_output_aliases={i: i for i in range(n)},
        scratch_shapes=[pltpu.SemaphoreType.DMA((n, 3)), pltpu.SemaphoreType.DMA((n, 3))],
        compiler_params=_params(),
    )(*lands)


def _all_plan(srcs, lands):
    x, y, c = _coords()
    me = 4 * x + 2 * y + c
    out = []
    for src, land in zip(srcs, lands):
        for dx in (0, 1):
            for dy in (0, 1):
                for dc in (0, 1):
                    if dx or dy or dc:
                        out.append((src, land.at[me], (1 - x if dx else x, 1 - y if dy else y, 1 - c if dc else c)))
    return out


def _chip_plan(srcs, lands):
    x, y, c = _coords()
    me = 2 * x + y
    out = []
    for src, land in zip(srcs, lands):
        out += [(src.at[2 * cx + cy], land.at[me], (cx, cy, c)) for cx, cy in _other_chips(x, y)]
    return out


ROW_BLOCK_BYTES = 2 * 1024 * 1024


def _rtile(r, pref, mult, row_bytes=None):
    if row_bytes is not None:
        pref = max(pref, ROW_BLOCK_BYTES // row_bytes)
    t = (min(r, pref) // mult) * mult
    while t >= mult:
        if r % t == 0:
            return t
        t -= mult
    return r


def _pair_add(g, recv, core, *, name):
    _, _, r2, cols = g.shape
    tr = _rtile(r2, 256, 16, row_bytes=2 * cols)

    def body(c_ref, g_ref, r_ref, o_ref):
        o_ref[...] = (g_ref[...].astype(F32) + r_ref[...].astype(F32)).astype(o_ref.dtype)

    blk = pl.BlockSpec((None, tr, cols), lambda j, i, c_ref: (j, i, 0))
    return pl.pallas_call(
        body, name=name,
        grid_spec=pltpu.PrefetchScalarGridSpec(
            num_scalar_prefetch=1, grid=(N_CHIPS, r2 // tr),
            in_specs=[pl.BlockSpec((None, None, tr, cols), lambda j, i, c_ref: (j, c_ref[0], i, 0)), blk],
            out_specs=blk),
        out_shape=jax.ShapeDtypeStruct(recv.shape, recv.dtype), compiler_params=_params(),
    )(core, g, recv)


def _sum_slots(a, out_dtype, *, name):
    n, r, cols = a.shape
    whole = n * r * cols * a.dtype.itemsize <= 4 * ROW_BLOCK_BYTES
    tr = r if whole else _rtile(r, 256, 16)

    def body(a_ref, o_ref):
        acc = a_ref[0].astype(F32)
        for j in range(1, n):
            acc = acc + a_ref[j].astype(F32)
        o_ref[...] = acc.astype(o_ref.dtype)

    return pl.pallas_call(
        body, name=name, grid=(r // tr,),
        in_specs=[pl.BlockSpec((n, tr, cols), lambda i: (0, i, 0))],
        out_specs=pl.BlockSpec((tr, cols), lambda i: (i, 0)),
        out_shape=jax.ShapeDtypeStruct((r, cols), out_dtype), compiler_params=_params(),
    )(a)


def _chip_sum(own, recv, chip, *, name):
    _, r2, cols = own.shape
    tr = _rtile(r2, 256, 16, row_bytes=2 * cols)

    def body(chip_ref, own_ref, *rest):
        o_ref = rest[-1]
        acc = None
        for j in range(N_CHIPS):
            term = jnp.where(chip_ref[0] == j, own_ref[...], rest[j][...]).astype(F32)
            acc = term if acc is None else acc + term
        o_ref[...] = acc

    def slot(j):
        return pl.BlockSpec((None, tr, cols),
                            lambda i, chip_ref: (jnp.where(chip_ref[0] == j, (j + 1) % N_CHIPS, j), i, 0))

    return pl.pallas_call(
        body, name=name,
        grid_spec=pltpu.PrefetchScalarGridSpec(
            num_scalar_prefetch=1, grid=(r2 // tr,),
            in_specs=[pl.BlockSpec((None, tr, cols), lambda i, chip_ref: (chip_ref[0], i, 0))]
                     + [slot(j) for j in range(N_CHIPS)],
            out_specs=pl.BlockSpec((tr, cols), lambda i, chip_ref: (i, 0))),
        out_shape=jax.ShapeDtypeStruct((r2, cols), F32), compiler_params=_params(),
    )(chip, own, *([recv] * N_CHIPS))


def _adam_update(w, gv, m, v):
    c1 = 1.0 / (1.0 - ADAM_B1 ** ADAM_STEP)
    c2 = 1.0 / (1.0 - ADAM_B2 ** ADAM_STEP)
    nm = ADAM_B1 * m + (1.0 - ADAM_B1) * gv
    nv = ADAM_B2 * v + (1.0 - ADAM_B2) * gv * gv
    return -ADAM_LR * ((nm * c1) / (jnp.sqrt(nv * c2) + ADAM_EPS) + ADAM_WD * w), nm, nv


def _adamw_halves(w, g_mine, g_other, m, v, core, *, name):
    r, cols = w.shape
    r2 = r // 2
    tr = _rtile(r2, 256, 8, row_bytes=4 * cols)
    nt = r2 // tr

    def body(core_ref, w_ref, gm_ref, go_ref, m_ref, v_ref, g_ref, d_ref, nm_ref, nv_ref):
        gv = jnp.where(pl.program_id(0) == core_ref[0], gm_ref[...], go_ref[...])
        g_ref[...] = gv
        d_ref[...], nm_ref[...], nv_ref[...] = _adam_update(w_ref[...], gv, m_ref[...], v_ref[...])

    full = pl.BlockSpec((tr, cols), lambda hf, i, core_ref: (hf * nt + i, 0))
    half = pl.BlockSpec((tr, cols), lambda hf, i, core_ref: (i, 0))
    shape = jax.ShapeDtypeStruct((r, cols), F32)
    return pl.pallas_call(
        body, name=name,
        grid_spec=pltpu.PrefetchScalarGridSpec(
            num_scalar_prefetch=1, grid=(2, nt), in_specs=[full, half, half, full, full], out_specs=[full] * 4),
        out_shape=[shape] * 4, compiler_params=_params(),
    )(core, w, g_mine, g_other, m, v)


def _adamw_split_rows(w, g_mine, g_other, m, v, core, *, name, tc=256):
    r, cols = w.shape
    r2 = g_mine.shape[0]
    tc = _tile(cols, tc)

    def body(core_ref, w_ref, gm_ref, go_ref, m_ref, v_ref, g_ref, d_ref, nm_ref, nv_ref):
        mine_first = core_ref[0] == 0
        for lo, hi, first in ((0, r2, True), (r2, r, False)):
            n = hi - lo
            gm, go = gm_ref[0:n, :], go_ref[0:n, :]
            gv = jnp.where(mine_first, gm, go) if first else jnp.where(mine_first, go, gm)
            g_ref[lo:hi, :] = gv
            d_ref[lo:hi, :], nm_ref[lo:hi, :], nv_ref[lo:hi, :] = _adam_update(
                w_ref[lo:hi, :], gv, m_ref[lo:hi, :], v_ref[lo:hi, :])

    full = pl.BlockSpec((r, tc), lambda j, core_ref: (0, j))
    half = pl.BlockSpec((r2, tc), lambda j, core_ref: (0, j))
    shape = jax.ShapeDtypeStruct((r, cols), F32)
    return pl.pallas_call(
        body, name=name,
        grid_spec=pltpu.PrefetchScalarGridSpec(
            num_scalar_prefetch=1, grid=(cols // tc,), in_specs=[full, half, half, full, full],
            out_specs=[full] * 4),
        out_shape=[shape] * 4, compiler_params=_params(),
    )(core, w, g_mine, g_other, m, v)


def _adamw(w, g, m, v, *, name, rows=256):
    r, cols = w.shape
    tr = _rtile(r, rows, 8)

    def body(w_ref, g_ref, m_ref, v_ref, d_ref, nm_ref, nv_ref):
        d_ref[...], nm_ref[...], nv_ref[...] = _adam_update(w_ref[...], g_ref[...], m_ref[...], v_ref[...])

    blk = pl.BlockSpec((tr, cols), lambda i: (i, 0))
    shape = jax.ShapeDtypeStruct((r, cols), F32)
    return pl.pallas_call(
        body, name=name, grid=(r // tr,), in_specs=[blk] * 4, out_specs=[blk] * 3,
        out_shape=[shape] * 3, compiler_params=_params(),
    )(w, g, m, v)


_BIG = (("w_in", 1), ("w_branch_a", 0), ("w_branch_b", 0), ("w_out", 0), ("w_up", 1), ("w_down", 0),
        ("w_ple", 1), ("w_ple_gate", 0))
_SMALL = ("gmlp_ln_g", "gmlp_ln_b", "gmlp_w_s", "gmlp_b_s", "norm_ffn_g", "conv_b", "norm_ple_g", "norm_final_g",
          "b_f", "norm_mix_g")
N_LATE = 2
_WEIGHTS = ("norm_mix_g", "w_in", "b_f", "gmlp_ln_g", "gmlp_ln_b", "gmlp_w_s", "gmlp_b_s", "w_branch_a",
            "w_branch_b", "w_out", "norm_ffn_g", "w_up", "conv_w", "conv_b", "w_down", "norm_ple_g", "w_ple",
            "w_ple_gate", "norm_final_g")
_PACK_ROWS = 8


def _pack(arrays):
    parts = []
    for a in arrays:
        flat = a.reshape(-1)
        unit = _PACK_ROWS * LANES
        flat = jnp.pad(flat, (0, (-flat.shape[0]) % unit))
        parts.append(flat.reshape(-1, LANES))
    return jnp.concatenate(parts, axis=0)


def _unpack(packed, shapes):
    out, row = [], 0
    for shp in shapes:
        size = math.prod(shp)
        rows = -(-size // (_PACK_ROWS * LANES)) * _PACK_ROWS
        out.append(packed[row:row + rows].reshape(-1)[:size].reshape(shp))
        row += rows
    return out


def _take_cols(parts, lo, hi):
    out, start = [], 0
    for a in parts:
        width = a.shape[1]
        a0, a1 = max(lo, start) - start, min(hi, start + width) - start
        if a1 > a0:
            out.append(a if (a0, a1) == (0, width) else a[:, a0:a1])
        start += width
    return out[0] if len(out) == 1 else jnp.concatenate(out, axis=1)


def _take_rows(parts, lo, hi):
    out, start = [], 0
    for a in parts:
        height = a.shape[0]
        a0, a1 = max(lo, start) - start, min(hi, start + height) - start
        if a1 > a0:
            out.append(a if (a0, a1) == (0, height) else a[a0:a1])
        start += height
    return out[0] if len(out) == 1 else jnp.concatenate(out, axis=0)


def _assemble(gathered, axis):
    n, r, cols = gathered.shape
    if axis == 0:
        return gathered.reshape(n * r, cols)
    return _take_cols([gathered[j] for j in range(n)], 0, n * cols)


def _to_chunks(parts, axis):
    rows, total = parts[0].shape[0], sum(a.shape[1] for a in parts)
    if axis == 0:
        r, cols = rows // N_CHIPS, total
        chunks = _take_cols(parts, 0, total).reshape(N_CHIPS, r, cols)
    else:
        r, cols = rows, total // N_CHIPS
        chunks = jnp.stack([_take_cols(parts, j * cols, (j + 1) * cols) for j in range(N_CHIPS)])
    return chunks.reshape(N_CHIPS, 2, r // 2, cols)


def kernel(x, p, norm_mix_g, w_in, b_f, gmlp_ln_g, gmlp_ln_b, gmlp_w_s, gmlp_b_s, w_branch_a, w_branch_b, w_out, norm_ffn_g, w_up, conv_w, conv_b, w_down, norm_ple_g, w_ple, w_ple_gate, norm_final_g, loss_target, m_norm_mix_g, m_w_in, m_b_f, m_gmlp_ln_g, m_gmlp_ln_b, m_gmlp_w_s, m_gmlp_b_s, m_w_branch_a, m_w_branch_b, m_w_out, m_norm_ffn_g, m_w_up, m_conv_w, m_conv_b, m_w_down, m_norm_ple_g, m_w_ple, m_w_ple_gate, m_norm_final_g, v_norm_mix_g, v_w_in, v_b_f, v_gmlp_ln_g, v_gmlp_ln_b, v_gmlp_w_s, v_gmlp_b_s, v_w_branch_a, v_w_branch_b, v_w_out, v_norm_ffn_g, v_w_up, v_conv_w, v_conv_b, v_w_down, v_norm_ple_g, v_w_ple, v_w_ple_gate, v_norm_final_g):
    args = dict(locals())
    wt = {n: args[n] for n in _WEIGHTS}
    mom = {n: args["m_" + n] for n in _WEIGHTS}
    var = {n: args["v_" + n] for n in _WEIGHTS}
    chip = 2 * lax.axis_index("x") + lax.axis_index("y")
    core = lax.axis_index("c").astype(jnp.int32).reshape(1)

    chip1 = chip.astype(jnp.int32).reshape(1)
    device = 2 * chip + lax.axis_index("c")
    axis_of = dict(_BIG)
    names = [n for n, _ in _BIG]
    put_mine = lambda land, mine: lax.dynamic_update_index_in_dim(land, mine, chip, 0)

    shard_in = w_in[0].astype(BF16)
    sems_in = _copies_start([shard_in], [lax.empty((N_CHIPS,) + shard_in.shape, BF16)], _gather_plan([True]), 3,
                            name="gather_start_in")
    shards = [wt[n][0].astype(BF16) for n in names[1:]] + [conv_w[0]]
    halved = [True] * len(names[1:]) + [False]
    lands = [lax.empty((N_CHIPS,) + a.shape, a.dtype) for a in shards]
    send_sems, recv_sems, srcs, lands, rest_token = _copies_start(
        shards, lands, _gather_plan(halved), 3 * len(shards), name="gather_start_rest", after=[sems_in[4]])
    o1 = 2 * GMLP_WIDTH
    o2 = o1 + 3 * FOX_WIDTH
    o3 = o2 + FOX_HEADS
    fpad = ((0, 0), (0, LANES - FOX_HEADS))
    w = {
        "conv_b": conv_b, "norm_mix_g": norm_mix_g, "norm_ffn_g": norm_ffn_g, "norm_ple_g": norm_ple_g,
        "norm_final_g": norm_final_g.reshape(1, D_MODEL), "b_f": jnp.pad(b_f, fpad),
        "gmlp_ln_g": gmlp_ln_g, "gmlp_ln_b": gmlp_ln_b, "gmlp_w_s": gmlp_w_s[0],
        "gmlp_b_s_t": jnp.pad(gmlp_b_s[0].T, ((0, 0), (0, LANES - GMLP_GROUPS))),
        "first_dep": rest_token,
    }

    def get_w_in(after):
        early = [a.reshape(a.shape[-2:]) for a in (w_in, m_w_in, v_w_in)]
        _, got = _copies_wait(sems_in[0], sems_in[1], sems_in[2], sems_in[3], _gather_plan([True]), 0,
                              [after] + early, name="gather_wait_in")
        got = _forward_halves(got, name="gather_forward_in")
        slots = put_mine(got[0], shard_in)
        slots = [slots[j] for j in range(N_CHIPS)]
        return {"w_uv": _take_cols(slots, 0, o1), "w_qkv": _take_cols(slots, o1, o2),
                "w_f": jnp.pad(_take_cols(slots, o2, o3), fpad), "w_g": _take_cols(slots, o3, o3 + 2 * D_MODEL)}

    def get_w_rest(after):
        _, got = _copies_wait(send_sems, recv_sems, srcs, lands, _gather_plan(halved), 0, [after],
                              name="gather_wait_rest")
        got = list(_forward_halves(got[:-1], name="gather_forward_rest")) + got[-1:]
        slots = {n: put_mine(got[i], shards[i]) for i, n in enumerate(names[1:])}
        full = {n: _assemble(slots[n], axis_of[n]) for n in names[1:] if n != "w_up"}
        up = [slots["w_up"][j] for j in range(N_CHIPS)]
        return {"w_branch_a": full["w_branch_a"], "w_branch_b": full["w_branch_b"], "w_out": full["w_out"],
                "w_up_a": _take_cols(up, 0, D_FF), "w_up_b": _take_cols(up, D_FF, 2 * D_FF),
                "w_down": full["w_down"], "w_ple": full["w_ple"], "w_ple_gate": full["w_ple_gate"],
                "conv_w": _assemble(put_mine(got[-1], shards[-1]), 1)}

    grads, delta, new_m, new_v = {}, {}, {}, {}
    pending = {}

    def to_chunks(n, gr):
        return _to_chunks(gr if isinstance(gr, list) else [gr], axis_of[n])

    def reduce_start(group, gfull, tag):
        chunks = [to_chunks(n, gfull[n]) for n in group]
        from_sibling = _pair_exchange(chunks, name="grad_pair_exchange_" + tag)
        pair_sums = [_pair_add(chunks[i], from_sibling[i], core, name="grad_pair_add_" + n) for i, n in enumerate(group)]
        empty = [lax.empty(a.shape, a.dtype) for a in pair_sums]
        ssem, rsem, own, recv, token = _copies_start(pair_sums, empty, _chip_plan, 3 * len(group),
                                                     name="grad_chip_start_" + tag)
        pending[tag] = (ssem, rsem, own, recv)
        return token

    def reduce_finish(group, tag, after):
        ssem, rsem, own, recv = pending[tag]
        own, recv = _copies_wait(ssem, rsem, own, recv, _chip_plan, 0, after, name="grad_chip_wait_" + tag)
        halves = [_chip_sum(own[i], recv[i], chip1, name="grad_chip_sum_" + n) for i, n in enumerate(group)]
        other_halves = _pair_share(halves, name="grad_pair_share_" + tag)
        for i, n in enumerate(group):
            shp = wt[n].shape
            outs = _adamw_halves(wt[n].reshape(shp[-2:]), halves[i], other_halves[i], mom[n].reshape(shp[-2:]),
                                 var[n].reshape(shp[-2:]), core, name="adamw_" + n)
            grads[n], delta[n], new_m[n], new_v[n] = (o.reshape(shp) for o in outs)
        return new_v[group[-1]]

    ffn_group = ("w_up", "w_down", "w_ple", "w_ple_gate")
    mix_group = ("w_in", "w_branch_a", "w_branch_b", "w_out")

    def on_grads_ffn(g):
        gfull = dict(g)
        gfull["w_up"] = [g["w_up_a"], g["w_up_b"]]
        return reduce_start(ffn_group, gfull, "ffn")

    def on_grads_small(g):
        vec = _pack([g[n] for n in _SMALL[:-N_LATE]] + [g["conv_w"]])
        ssem, rsem, own, recv, token = _copies_start(
            [vec], [lax.empty((8,) + vec.shape, F32)], _all_plan, 7, name="small_start")
        pending["small"] = (ssem, rsem, own, recv)
        return token

    def on_grads_mix(g):
        gfull = dict(g)
        gfull["w_in"] = [g["w_uv"], g["w_qkv"], g["w_f"][:, :FOX_HEADS], g["w_g"]]
        token = reduce_start(mix_group, gfull, "mix")
        pending["ffn_done"] = reduce_finish(ffn_group, "ffn", [token])
        return token

    loss, grad_x, g = _device_step(x[0], p[0, 0], loss_target[0], w, get_w_in, get_w_rest, on_grads_ffn,
                                   on_grads_small, on_grads_mix)

    mix_done = reduce_finish(mix_group, "mix", [grad_x, pending["ffn_done"]])
    ssem, rsem, own, recv = pending["small"]
    own, recv = _copies_wait(ssem, rsem, own, recv, _all_plan, 0, [mix_done], name="small_wait")
    vec_early = _sum_slots(lax.dynamic_update_index_in_dim(recv[0], own[0], device, 0), F32, name="small_sum")
    vec_late = _pack([g["b_f"][:, :FOX_HEADS], g["norm_mix_g"]])
    vec_late = _sum_slots(_all_exchange(vec_late, name="small_exchange_late"), F32, name="small_sum_late")
    early_rows = _pack([wt[n] for n in _SMALL[:-N_LATE]]).shape[0]
    vec = jnp.concatenate([vec_early[:early_rows], vec_late], axis=0)
    for n, a in zip(_SMALL, _unpack(vec, [wt[n].shape for n in _SMALL])):
        grads[n] = a
    conv_w_grad = _unpack(vec_early[early_rows:], [(3, 2 * D_FF)])[0]
    grads["conv_w"] = lax.dynamic_slice_in_dim(conv_w_grad, chip * conv_w.shape[2], conv_w.shape[2], axis=1).reshape(conv_w.shape)

    shp = conv_w.shape
    outs = _adamw(conv_w.reshape(shp[-2:]), grads["conv_w"].reshape(shp[-2:]), m_conv_w.reshape(shp[-2:]),
                  v_conv_w.reshape(shp[-2:]), name="adamw_conv_w")
    delta["conv_w"], new_m["conv_w"], new_v["conv_w"] = (o.reshape(shp) for o in outs)
    outs = _adamw(_pack([wt[n] for n in _SMALL]), vec, _pack([mom[n] for n in _SMALL]),
                  _pack([var[n] for n in _SMALL]), name="adamw_small", rows=2048)
    for d, o in zip((delta, new_m, new_v), outs):
        for n, a in zip(_SMALL, _unpack(o, [wt[n].shape for n in _SMALL])):
            d[n] = a

    total_loss = lax.psum(loss[0, 0], ("x", "y", "c"))
    return (total_loss, grad_x.reshape(x.shape), *[grads[n] for n in _WEIGHTS], *[delta[n] for n in _WEIGHTS],
            *[new_m[n] for n in _WEIGHTS], *[new_v[n] for n in _WEIGHTS])
```

```python
import functools
import math

import jax
import jax.numpy as jnp
from jax import lax
from jax.experimental import pallas as pl
from jax.experimental.pallas import tpu as pltpu

F32 = jnp.float32
BF16 = jnp.bfloat16

D_MODEL = 1024
EPS = 1e-6
CHUNK = 64
GMLP_GROUPS = 8
GMLP_BLOCK = 128
GMLP_WIDTH = 1024
FOX_HEADS = 16
FOX_HEAD_DIM = 64
FOX_WIDTH = 1024
HEAD_PAIRS = FOX_HEADS // 2
ATT_BLOCK = 128
D_FF = 2816
PLE_DIM = 256
LANES = 128
BF16_TILE_ROWS = 16
N_CHIPS = 4

ADAM_LR = 0.001
ADAM_B1 = 0.9
ADAM_B2 = 0.999
ADAM_EPS = 1e-08
ADAM_WD = 0.01
ADAM_STEP = 10

VMEM_LIMIT = 56 * 1024 * 1024
MESH = pl.DeviceIdType.MESH

_NN = (((1,), (0,)), ((), ()))
_NT = (((1,), (1,)), ((), ()))
_TN = (((0,), (0,)), ((), ()))


def _params(**kw):
    return pltpu.CompilerParams(vmem_limit_bytes=VMEM_LIMIT, **kw)


def _tile(dim, pref):
    if dim <= pref:
        return dim
    t = (pref // LANES) * LANES
    while t >= LANES:
        if dim % t == 0:
            return t
        t -= LANES
    return dim


def _dot(a, b, dn):
    return lax.dot_general(a.astype(BF16), b.astype(BF16), dn, preferred_element_type=F32)


def _gelu(x):
    c = math.sqrt(2.0 / math.pi)
    t = jnp.tanh(c * (x + 0.044715 * x * x * x))
    return 0.5 * x * (1.0 + t)


def _gelu_and_grad(x):
    c = math.sqrt(2.0 / math.pi)
    x2 = x * x
    t = jnp.tanh(c * (x + 0.044715 * x2 * x))
    g = 0.5 * x * (1.0 + t)
    dg = 0.5 * (1.0 + t) + 0.5 * x * (1.0 - t * t) * c * (1.0 + 3.0 * 0.044715 * x2)
    return g, dg


def _sigmoid(x):
    return 1.0 / (1.0 + jnp.exp(-x))


def _mm(a, b, *, mode, out_dtype, name, add=None, tm=512, tn=512, dep=None):
    if mode == "nn":
        m, k = a.shape
        k2, n = b.shape
    elif mode == "nt":
        m, k = a.shape
        n, k2 = b.shape
    else:
        k, m = a.shape
        k2, n = b.shape
    assert k == k2, (name, a.shape, b.shape)
    tm = _tile(m, tm)
    tn = _tile(n, tn)
    dn = {"nn": _NN, "nt": _NT, "tn": _TN}[mode]

    def body(a_ref, b_ref, *rest):
        o_ref = rest[-1]
        acc = _dot(a_ref[...], b_ref[...], dn)
        if add is not None:
            acc = acc + rest[0][...].astype(F32)
        o_ref[...] = acc.astype(o_ref.dtype)

    a_spec = pl.BlockSpec((k, tm), lambda i, j: (0, i)) if mode == "tn" else pl.BlockSpec((tm, k), lambda i, j: (i, 0))
    b_spec = pl.BlockSpec((tn, k), lambda i, j: (j, 0)) if mode == "nt" else pl.BlockSpec((k, tn), lambda i, j: (0, j))
    o_spec = pl.BlockSpec((tm, tn), lambda i, j: (i, j))
    in_specs = [a_spec, b_spec]
    args = [a, b]
    if add is not None:
        in_specs.append(o_spec)
        args.append(add)
    if dep is not None:
        in_specs.append(pl.BlockSpec(memory_space=pl.ANY))
        args.append(dep)
    return pl.pallas_call(
        body, name=name, grid=(m // tm, n // tn), in_specs=in_specs, out_specs=o_spec,
        out_shape=jax.ShapeDtypeStruct((m, n), out_dtype), compiler_params=_params(),
    )(*args)


def _mm_nt_sum(pairs, *, out_dtype, name, tm=256, dep=None):
    m, n = pairs[0][0].shape[0], pairs[0][1].shape[0]
    tm = _tile(m, tm)
    np_ = len(pairs)

    def body(*refs):
        o_ref = refs[-1] if dep is None else refs[-1]
        acc = None
        for p in range(np_):
            part = _dot(refs[2 * p][...], refs[2 * p + 1][...], _NT)
            acc = part if acc is None else acc + part
        o_ref[...] = acc.astype(o_ref.dtype)

    in_specs, args = [], []
    for a, b in pairs:
        assert a.shape[0] == m and b.shape[0] == n and a.shape[1] == b.shape[1], (name, a.shape, b.shape)
        in_specs += [pl.BlockSpec((tm, a.shape[1]), lambda i: (i, 0)), pl.BlockSpec(b.shape, lambda i: (0, 0))]
        args += [a, b]
    if dep is not None:
        in_specs.append(pl.BlockSpec(memory_space=pl.ANY))
        args.append(dep)
    return pl.pallas_call(
        body, name=name, grid=(m // tm,), in_specs=in_specs, out_specs=pl.BlockSpec((tm, n), lambda i: (i, 0)),
        out_shape=jax.ShapeDtypeStruct((m, n), out_dtype), compiler_params=_params(),
    )(*args)


def _rms_fwd(x, g, *, name, tm=256, dep=None):
    s, d = x.shape
    tm = _tile(s, tm)

    def body(x_ref, g_ref, *rest):
        h_ref = rest[-1]
        xv = x_ref[...]
        r = lax.rsqrt(jnp.mean(xv * xv, axis=-1, keepdims=True) + EPS)
        h_ref[...] = (xv * r * g_ref[...]).astype(h_ref.dtype)

    deps = [] if dep is None else [dep]
    return pl.pallas_call(
        body, name=name, grid=(s // tm,),
        in_specs=[pl.BlockSpec((tm, d), lambda i: (i, 0)), pl.BlockSpec((1, d), lambda i: (0, 0))]
                 + [pl.BlockSpec(memory_space=pl.ANY)] * len(deps),
        out_specs=pl.BlockSpec((tm, d), lambda i: (i, 0)),
        out_shape=jax.ShapeDtypeStruct((s, d), BF16), compiler_params=_params(),
    )(x, g, *deps)


def _rms_bwd(x, g, dh, dres, *, name, tm=256):
    s, d = x.shape
    tm = _tile(s, tm)

    def body(x_ref, g_ref, dh_ref, dres_ref, dx_ref, dxb_ref, dg_ref):
        xv = x_ref[...]
        r = lax.rsqrt(jnp.mean(xv * xv, axis=-1, keepdims=True) + EPS)
        xhat = xv * r
        dhv = dh_ref[...].astype(F32)
        dyg = dhv * g_ref[...]
        dx = dres_ref[...] + r * (dyg - xhat * jnp.mean(dyg * xhat, axis=-1, keepdims=True))
        dx_ref[...] = dx
        dxb_ref[...] = dx.astype(dxb_ref.dtype)

        @pl.when(pl.program_id(0) == 0)
        def _():
            dg_ref[...] = jnp.zeros_like(dg_ref)

        dg_ref[...] += jnp.sum(dhv * xhat, axis=0, keepdims=True)

    row = pl.BlockSpec((tm, d), lambda i: (i, 0))
    vec = pl.BlockSpec((1, d), lambda i: (0, 0))
    return pl.pallas_call(
        body, name=name, grid=(s // tm,), in_specs=[row, vec, row, row], out_specs=[row, row, vec],
        out_shape=[jax.ShapeDtypeStruct((s, d), F32), jax.ShapeDtypeStruct((s, d), BF16),
                   jax.ShapeDtypeStruct((1, d), F32)],
        compiler_params=_params(),
    )(x, g, dh, dres)


def _gmlp_mask():
    t = lax.broadcasted_iota(jnp.int32, (GMLP_BLOCK, GMLP_BLOCK), 0)
    s_ = lax.broadcasted_iota(jnp.int32, (GMLP_BLOCK, GMLP_BLOCK), 1)
    return (s_ // CHUNK) <= (t // CHUNK)


def _gmlp_norm(zv, ln_g, ln_b):
    vv, dvv = _gelu_and_grad(zv)
    mu = jnp.mean(vv, axis=-1, keepdims=True)
    xc = vv - mu
    rstd = lax.rsqrt(jnp.mean(xc * xc, axis=-1, keepdims=True) + EPS)
    vhat = xc * rstd
    return vhat * ln_g + ln_b, vhat, rstd, dvv


def _gmlp_fwd(z_uv, ln_g, ln_b, w_s, b_s_t, *, name):
    s = z_uv.shape[0]
    w = GMLP_WIDTH
    gd = w // GMLP_GROUPS

    def body(z_ref, lg_ref, lb_ref, ws_ref, bs_ref, a_ref):
        u = _gelu(z_ref[:, :w].astype(F32))
        vn, _, _, _ = _gmlp_norm(z_ref[:, w:].astype(F32), lg_ref[...], lb_ref[...])
        mask = _gmlp_mask()
        for g in range(GMLP_GROUPS):
            wm = jnp.where(mask, ws_ref[g], 0.0)
            mixed = _dot(wm, vn[:, g * gd:(g + 1) * gd], _NN) + bs_ref[:, g:g + 1]
            a_ref[:, g * gd:(g + 1) * gd] = (u[:, g * gd:(g + 1) * gd] * mixed).astype(a_ref.dtype)

    full = lambda shape: pl.BlockSpec(shape, lambda i: (0,) * len(shape))
    return pl.pallas_call(
        body, name=name, grid=(s // GMLP_BLOCK,),
        in_specs=[pl.BlockSpec((GMLP_BLOCK, 2 * w), lambda i: (i, 0)), full((1, w)), full((1, w)),
                  full((GMLP_GROUPS, GMLP_BLOCK, GMLP_BLOCK)), full((GMLP_BLOCK, LANES))],
        out_specs=pl.BlockSpec((GMLP_BLOCK, w), lambda i: (i, 0)),
        out_shape=jax.ShapeDtypeStruct((s, w), BF16), compiler_params=_params(),
    )(z_uv, ln_g, ln_b, w_s, b_s_t)


def _gmlp_bwd(z_uv, da, ln_g, ln_b, w_s, b_s_t, *, name):
    s = z_uv.shape[0]
    w = GMLP_WIDTH
    gd = w // GMLP_GROUPS

    def body(z_ref, da_ref, lg_ref, lb_ref, ws_ref, bs_ref, dz_ref, dws_ref, dbs_ref, dlg_ref, dlb_ref):
        @pl.when(pl.program_id(0) == 0)
        def _():
            dws_ref[...] = jnp.zeros_like(dws_ref)
            dbs_ref[...] = jnp.zeros_like(dbs_ref)
            dlg_ref[...] = jnp.zeros_like(dlg_ref)
            dlb_ref[...] = jnp.zeros_like(dlb_ref)

        u, du_dz = _gelu_and_grad(z_ref[:, :w].astype(F32))
        lg = lg_ref[...]
        vn, vhat, rstd, dvv_dz = _gmlp_norm(z_ref[:, w:].astype(F32), lg, lb_ref[...])
        dav = da_ref[...].astype(F32)
        mask = _gmlp_mask()
        lane = lax.broadcasted_iota(jnp.int32, (GMLP_BLOCK, LANES), 1)
        dvn_parts = []
        dbs = jnp.zeros((GMLP_BLOCK, LANES), F32)
        for g in range(GMLP_GROUPS):
            sl = slice(g * gd, (g + 1) * gd)
            wm = jnp.where(mask, ws_ref[g], 0.0)
            vn_g = vn[:, sl]
            mixed = _dot(wm, vn_g, _NN) + bs_ref[:, g:g + 1]
            dmixed = dav[:, sl] * u[:, sl]
            dz_ref[:, sl] = (dav[:, sl] * mixed * du_dz[:, sl]).astype(dz_ref.dtype)
            dvn_parts.append(_dot(wm, dmixed, _TN))
            dws_ref[g] += jnp.where(mask, _dot(dmixed, vn_g, _NT), 0.0)
            dbs = dbs + jnp.where(lane == g, jnp.sum(dmixed, axis=-1, keepdims=True), 0.0)
        dbs_ref[...] += dbs
        dvn = jnp.concatenate(dvn_parts, axis=-1)
        dlg_ref[...] += jnp.sum(dvn * vhat, axis=0, keepdims=True)
        dlb_ref[...] += jnp.sum(dvn, axis=0, keepdims=True)
        dyg = dvn * lg
        dvv = rstd * (dyg - jnp.mean(dyg, axis=-1, keepdims=True)
                      - vhat * jnp.mean(dyg * vhat, axis=-1, keepdims=True))
        dz_ref[:, w:] = (dvv * dvv_dz).astype(dz_ref.dtype)

    full = lambda shape: pl.BlockSpec(shape, lambda i: (0,) * len(shape))
    return pl.pallas_call(
        body, name=name, grid=(s // GMLP_BLOCK,),
        in_specs=[pl.BlockSpec((GMLP_BLOCK, 2 * w), lambda i: (i, 0)),
                  pl.BlockSpec((GMLP_BLOCK, w), lambda i: (i, 0)), full((1, w)), full((1, w)),
                  full((GMLP_GROUPS, GMLP_BLOCK, GMLP_BLOCK)), full((GMLP_BLOCK, LANES))],
        out_specs=[pl.BlockSpec((GMLP_BLOCK, 2 * w), lambda i: (i, 0)),
                   full((GMLP_GROUPS, GMLP_BLOCK, GMLP_BLOCK)), full((GMLP_BLOCK, LANES)),
                   full((1, w)), full((1, w))],
        out_shape=[jax.ShapeDtypeStruct((s, 2 * w), BF16),
                   jax.ShapeDtypeStruct((GMLP_GROUPS, GMLP_BLOCK, GMLP_BLOCK), F32),
                   jax.ShapeDtypeStruct((GMLP_BLOCK, LANES), F32),
                   jax.ShapeDtypeStruct((1, w), F32), jax.ShapeDtypeStruct((1, w), F32)],
        compiler_params=_params(),
    )(z_uv, da, ln_g, ln_b, w_s, b_s_t)


def _tri(lower):
    r = lax.broadcasted_iota(jnp.int32, (ATT_BLOCK, ATT_BLOCK), 0)
    c = lax.broadcasted_iota(jnp.int32, (ATT_BLOCK, ATT_BLOCK), 1)
    return jnp.where((c <= r) if lower else (c >= r), 1.0, 0.0).astype(F32)


def _log_sigmoid(x):
    return jnp.minimum(x, 0.0) - jnp.log(1.0 + jnp.exp(-jnp.abs(x)))


def _fox_cum(f, b_f, *, name):
    s = f.shape[0]
    nb = s // ATT_BLOCK

    def body(f_ref, b_ref, cb_ref, ct_ref, carry):
        @pl.when(pl.program_id(0) == 0)
        def _():
            carry[...] = jnp.zeros_like(carry)

        lf = _log_sigmoid(f_ref[...] + b_ref[...])
        cum = lax.dot_general(_tri(True), lf, _NN, precision=lax.Precision.HIGHEST,
                              preferred_element_type=F32) + carry[...]
        carry[...] = cum[ATT_BLOCK - 1:ATT_BLOCK, :]
        for h in range(FOX_HEADS):
            cb_ref[h] = jnp.broadcast_to(cum[:, h:h + 1], (ATT_BLOCK, LANES))
        ct_ref[...] = cum.T

    return pl.pallas_call(
        body, name=name, grid=(nb,),
        in_specs=[pl.BlockSpec((ATT_BLOCK, LANES), lambda i: (i, 0)), pl.BlockSpec((1, LANES), lambda i: (0, 0))],
        out_specs=[pl.BlockSpec((FOX_HEADS, ATT_BLOCK, LANES), lambda i: (0, i, 0)),
                   pl.BlockSpec((LANES, ATT_BLOCK), lambda i: (0, i))],
        out_shape=[jax.ShapeDtypeStruct((FOX_HEADS, s, LANES), F32), jax.ShapeDtypeStruct((LANES, s), F32)],
        scratch_shapes=[pltpu.VMEM((1, LANES), F32)], compiler_params=_params(),
    )(f, b_f)


def _fox_dlogit(dcum_t, f, b_f, *, name):
    s = f.shape[0]
    nb = s // ATT_BLOCK

    def body(dc_ref, f_ref, b_ref, df_ref, db_ref, carry):
        @pl.when(pl.program_id(0) == 0)
        def _():
            carry[...] = jnp.zeros_like(carry)
            db_ref[...] = jnp.zeros_like(db_ref)

        d = dc_ref[...].T
        dlog = lax.dot_general(_tri(False), d, _NN, precision=lax.Precision.HIGHEST,
                               preferred_element_type=F32) + carry[...]
        carry[...] = dlog[0:1, :]
        df = dlog * (1.0 - _sigmoid(f_ref[...] + b_ref[...]))
        df_ref[...] = df
        db_ref[...] += jnp.sum(df, axis=0, keepdims=True)

    rev = lambda i: nb - 1 - i
    return pl.pallas_call(
        body, name=name, grid=(nb,),
        in_specs=[pl.BlockSpec((LANES, ATT_BLOCK), lambda i: (0, rev(i))),
                  pl.BlockSpec((ATT_BLOCK, LANES), lambda i: (rev(i), 0)),
                  pl.BlockSpec((1, LANES), lambda i: (0, 0))],
        out_specs=[pl.BlockSpec((ATT_BLOCK, LANES), lambda i: (rev(i), 0)),
                   pl.BlockSpec((1, LANES), lambda i: (0, 0))],
        out_shape=[jax.ShapeDtypeStruct((s, LANES), F32), jax.ShapeDtypeStruct((1, LANES), F32)],
        scratch_shapes=[pltpu.VMEM((1, LANES), F32)], compiler_params=_params(),
    )(dcum_t, f, b_f)


def _causal(qi, ki):
    r = lax.broadcasted_iota(jnp.int32, (ATT_BLOCK, ATT_BLOCK), 0) + qi * ATT_BLOCK
    c = lax.broadcasted_iota(jnp.int32, (ATT_BLOCK, ATT_BLOCK), 1) + ki * ATT_BLOCK
    return c <= r


def _head_mask():
    return lax.broadcasted_iota(jnp.int32, (1, LANES), 1) < FOX_HEAD_DIM


def _attn_fwd(qkv, cum_b, cum_r, *, name):
    s = qkv.shape[0]
    nq = s // ATT_BLOCK
    scale = FOX_HEAD_DIM ** -0.5
    npair = HEAD_PAIRS

    def body(q_ref, k_ref, v_ref, cq_ref, ck_ref, o_ref, l_ref):
        qi = pl.program_id(1)
        m0 = _head_mask()
        q2 = q_ref[...]
        zero = jnp.zeros_like(q2)
        qs = (jnp.where(m0, q2, zero), jnp.where(m0, zero, q2))
        cqs = (cq_ref[0], cq_ref[1])

        def step(ki, carry, masked):
            off = pl.multiple_of(ki * ATT_BLOCK, ATT_BLOCK)
            k2 = k_ref[pl.ds(off, ATT_BLOCK), :]
            v2 = v_ref[pl.ds(off, ATT_BLOCK), :]
            out = []
            for hh in range(2):
                m, l, acc = carry[hh]
                sc = _dot(qs[hh], k2, _NT) * scale + (cqs[hh] - ck_ref[hh:hh + 1, pl.ds(off, ATT_BLOCK)])
                if masked:
                    sc = jnp.where(_causal(qi, ki), sc, -1e30)
                m_new = jnp.maximum(m, jnp.max(sc, axis=-1, keepdims=True))
                alpha = jnp.exp(m - m_new)
                p = jnp.exp(sc - m_new)
                l = alpha * l + jnp.sum(p, axis=-1, keepdims=True)
                acc = alpha * acc + _dot(p, v2, _NN)
                out.append((m_new, l, acc))
            return tuple(out)

        init = tuple((jnp.full((ATT_BLOCK, 1), -1e30, F32), jnp.zeros((ATT_BLOCK, 1), F32),
                      jnp.zeros((ATT_BLOCK, LANES), F32)) for _ in range(2))
        carry = lax.fori_loop(0, qi, lambda ki, c: step(ki, c, False), init)
        (ma, la, acca), (mb, lb, accb) = step(qi, carry, True)
        o_ref[...] = jnp.where(m0, acca / la, accb / lb).astype(o_ref.dtype)
        l_ref[0] = jnp.broadcast_to(ma + jnp.log(la), (ATT_BLOCK, LANES))
        l_ref[1] = jnp.broadcast_to(mb + jnp.log(lb), (ATT_BLOCK, LANES))

    stat = pl.BlockSpec((None, 2, ATT_BLOCK, LANES), lambda j, i: (j, 0, i, 0))
    row = pl.BlockSpec((None, 2, s), lambda j, i: (j, 0, 0))
    return pl.pallas_call(
        body, name=name, grid=(npair, nq),
        in_specs=[pl.BlockSpec((ATT_BLOCK, LANES), lambda j, i: (i, j)),
                  pl.BlockSpec((s, LANES), lambda j, i: (0, npair + j)),
                  pl.BlockSpec((s, LANES), lambda j, i: (0, 2 * npair + j)),
                  stat, row],
        out_specs=[pl.BlockSpec((ATT_BLOCK, LANES), lambda j, i: (i, j)), stat],
        out_shape=[jax.ShapeDtypeStruct((s, FOX_WIDTH), BF16),
                   jax.ShapeDtypeStruct((npair, 2, s, LANES), F32)],
        compiler_params=_params(),
    )(qkv, qkv, qkv, cum_b, cum_r)


def _attn_delta(qkv, do, lse_b, cum_b, cum_r, *, name):
    s = qkv.shape[0]
    nq = s // ATT_BLOCK
    scale = FOX_HEAD_DIM ** -0.5
    npair = HEAD_PAIRS

    def body(q_ref, k_ref, v_ref, do_ref, l_ref, cq_ref, ck_ref, d_ref):
        qi = pl.program_id(1)
        m0 = _head_mask()
        q2 = q_ref[...]
        do2 = do_ref[...]
        qs = (jnp.where(m0, q2, jnp.zeros_like(q2)), jnp.where(m0, jnp.zeros_like(q2), q2))
        dos = (jnp.where(m0, do2, jnp.zeros_like(do2)), jnp.where(m0, jnp.zeros_like(do2), do2))

        def step(ki, carry, masked):
            off = pl.multiple_of(ki * ATT_BLOCK, ATT_BLOCK)
            k2 = k_ref[pl.ds(off, ATT_BLOCK), :]
            v2 = v_ref[pl.ds(off, ATT_BLOCK), :]
            out = []
            for hh in range(2):
                sc = _dot(qs[hh], k2, _NT) * scale + (cq_ref[hh] - ck_ref[hh:hh + 1, pl.ds(off, ATT_BLOCK)])
                p = jnp.exp(sc - l_ref[hh])
                if masked:
                    p = jnp.where(_causal(qi, ki), p, 0.0)
                out.append(carry[hh] + jnp.sum(p * _dot(dos[hh], v2, _NT), axis=-1, keepdims=True))
            return tuple(out)

        init = (jnp.zeros((ATT_BLOCK, 1), F32), jnp.zeros((ATT_BLOCK, 1), F32))
        carry = lax.fori_loop(0, qi, lambda ki, c: step(ki, c, False), init)
        da, db = step(qi, carry, True)
        d_ref[0] = jnp.broadcast_to(da, (ATT_BLOCK, LANES))
        d_ref[1] = jnp.broadcast_to(db, (ATT_BLOCK, LANES))

    stat = pl.BlockSpec((None, 2, ATT_BLOCK, LANES), lambda j, i: (j, 0, i, 0))
    return pl.pallas_call(
        body, name=name, grid=(npair, nq),
        in_specs=[pl.BlockSpec((ATT_BLOCK, LANES), lambda j, i: (i, j)),
                  pl.BlockSpec((s, LANES), lambda j, i: (0, npair + j)),
                  pl.BlockSpec((s, LANES), lambda j, i: (0, 2 * npair + j)),
                  pl.BlockSpec((ATT_BLOCK, LANES), lambda j, i: (i, j)),
                  stat, stat, pl.BlockSpec((None, 2, s), lambda j, i: (j, 0, 0))],
        out_specs=stat,
        out_shape=jax.ShapeDtypeStruct((npair, 2, s, LANES), F32), compiler_params=_params(),
    )(qkv, qkv, qkv, do, lse_b, cum_b, cum_r)


def _attn_bwd(qkv, do, lse_b, delta_b, cum_b, cum_r, *, name):
    s = qkv.shape[0]
    nq = s // ATT_BLOCK
    scale = FOX_HEAD_DIM ** -0.5
    npair = HEAD_PAIRS

    def body(q_ref, k_ref, v_ref, do_ref, l_ref, dl_ref, cq_ref, ck_ref, dq_ref, dk_ref, dv_ref, dc_ref):
        ki = pl.program_id(1)
        m0 = _head_mask()
        k2 = k_ref[...]
        v2 = v_ref[...]
        koff = pl.multiple_of(ki * ATT_BLOCK, ATT_BLOCK)

        @pl.when(ki == 0)
        def _():
            dq_ref[...] = jnp.zeros_like(dq_ref)

        def step(qi, carry, masked):
            off = pl.multiple_of(qi * ATT_BLOCK, ATT_BLOCK)
            q2 = q_ref[pl.ds(off, ATT_BLOCK), :]
            do2 = do_ref[pl.ds(off, ATT_BLOCK), :]
            qzero = jnp.zeros_like(q2)
            dzero = jnp.zeros_like(do2)
            out = []
            dqs = []
            for hh in range(2):
                dk_acc, dv_acc, dc_acc = carry[hh]
                keep = m0 if hh == 0 else jnp.logical_not(m0)
                qh = jnp.where(keep, q2, qzero)
                doh = jnp.where(keep, do2, dzero)
                sc = _dot(qh, k2, _NT) * scale + (cq_ref[hh, pl.ds(off, ATT_BLOCK), :]
                                                 - ck_ref[hh:hh + 1, pl.ds(koff, ATT_BLOCK)])
                p = jnp.exp(sc - l_ref[hh, pl.ds(off, ATT_BLOCK), :])
                if masked:
                    p = jnp.where(_causal(qi, ki), p, 0.0)
                dp = _dot(doh, v2, _NT)
                ds = p * (dp - dl_ref[hh, pl.ds(off, ATT_BLOCK), :])
                dv_acc = dv_acc + _dot(p, do2, _TN)
                dk_acc = dk_acc + _dot(ds, q2, _TN)
                dc_acc = dc_acc - jnp.sum(ds, axis=0, keepdims=True)
                dqs.append(_dot(ds, k2, _NN))
                out.append((dk_acc, dv_acc, dc_acc))
            dq_ref[pl.ds(off, ATT_BLOCK), :] += jnp.where(m0, dqs[0], dqs[1]) * scale
            return tuple(out)

        init = tuple((jnp.zeros((ATT_BLOCK, LANES), F32), jnp.zeros((ATT_BLOCK, LANES), F32),
                      jnp.zeros((1, ATT_BLOCK), F32)) for _ in range(2))
        carry = step(ki, init, True)
        (dka, dva, dca), (dkb, dvb, dcb) = lax.fori_loop(ki + 1, nq, lambda qi, c: step(qi, c, False), carry)
        dk_ref[...] = (jnp.where(m0, dka, dkb) * scale).astype(dk_ref.dtype)
        dv_ref[...] = jnp.where(m0, dva, dvb).astype(dv_ref.dtype)
        dc_ref[0:1, :] = dca
        dc_ref[1:2, :] = dcb

    stat = pl.BlockSpec((None, 2, s, LANES), lambda j, i: (j, 0, 0, 0))
    colfull = lambda base: pl.BlockSpec((s, LANES), lambda j, i: (0, base + j))
    colblk = lambda base: pl.BlockSpec((ATT_BLOCK, LANES), lambda j, i: (i, base + j))
    return pl.pallas_call(
        body, name=name, grid=(npair, nq),
        in_specs=[colfull(0), colblk(npair), colblk(2 * npair), colfull(0), stat, stat, stat,
                  pl.BlockSpec((None, 2, s), lambda j, i: (j, 0, 0))],
        out_specs=[colfull(0), colblk(0), colblk(0), pl.BlockSpec((None, 2, ATT_BLOCK), lambda j, i: (j, 0, i))],
        out_shape=[jax.ShapeDtypeStruct((s, FOX_WIDTH), F32), jax.ShapeDtypeStruct((s, FOX_WIDTH), BF16),
                   jax.ShapeDtypeStruct((s, FOX_WIDTH), BF16), jax.ShapeDtypeStruct((npair, 2, s), F32)],
        compiler_params=_params(),
    )(qkv, qkv, qkv, do, lse_b, delta_b, cum_b, cum_r)


ATT_TQ = 256
ATT_TK = 256
ATT_SCALE = FOX_HEAD_DIM ** -0.5
assert ATT_SCALE == 0.125 and ATT_TQ == ATT_TK


def _causal_t(qi, ki):
    kpos = lax.broadcasted_iota(jnp.int32, (ATT_TK, ATT_TQ), 0) + ki * ATT_TK
    qpos = lax.broadcasted_iota(jnp.int32, (ATT_TK, ATT_TQ), 1) + qi * ATT_TQ
    return kpos <= qpos


def _row_mask():
    return lax.broadcasted_iota(jnp.int32, (LANES, 1), 0) < FOX_HEAD_DIM


def _lane_tile(a, width):
    return a if a.shape[1] == width else jnp.tile(a, (1, width // a.shape[1]))


def _transpose_bf16(a):
    return a.astype(F32).T.astype(BF16)


def _attn_fwd_t(qkv, cum_b, cum_r, *, name):
    s = qkv.shape[0]
    nq = s // ATT_TQ
    npair = HEAD_PAIRS

    def body(q_ref, k_ref, v_ref, cq_ref, ck_ref, o_ref, ot_ref, l_ref, vt_ref):
        qi = pl.program_id(1)
        rows = _row_mask()

        @pl.when(qi == 0)
        def _():
            vt_ref[...] = _transpose_bf16(v_ref[...])

        qt = _transpose_bf16(q_ref[...]) * ATT_SCALE
        zero = jnp.zeros_like(qt)
        qts = (jnp.where(rows, qt, zero), jnp.where(rows, zero, qt))

        def step(ki, carry, masked):
            off = pl.multiple_of(ki * ATT_TK, ATT_TK)
            k2 = k_ref[pl.ds(off, ATT_TK), :]
            vt = vt_ref[:, pl.ds(off, ATT_TK)]
            out = []
            for hh in range(2):
                m, l, acc = carry[hh]
                bias = cq_ref[hh:hh + 1, :] - _lane_tile(ck_ref[hh, pl.ds(off, ATT_TK), :], ATT_TQ)
                sc = _dot(k2, qts[hh], _NN) + bias
                if masked:
                    sc = jnp.where(_causal_t(qi, ki), sc, -1e30)
                m_new = jnp.maximum(m, jnp.max(sc, axis=0, keepdims=True))
                alpha = jnp.exp(m - m_new)
                p = jnp.exp(sc - m_new)
                l = alpha * l + jnp.sum(p, axis=0, keepdims=True)
                p_hi = p.astype(BF16)
                p_lo = (p - p_hi.astype(F32)).astype(BF16)
                acc = alpha * acc + (_dot(vt, p_hi, _NN) + _dot(vt, p_lo, _NN))
                out.append((m_new, l, acc))
            return tuple(out)

        init = tuple((jnp.full((1, ATT_TQ), -1e30, F32), jnp.zeros((1, ATT_TQ), F32),
                      jnp.zeros((LANES, ATT_TQ), F32)) for _ in range(2))
        carry = lax.fori_loop(0, qi // 2, lambda kk, c: step(2 * kk + 1, step(2 * kk, c, False), False), init)
        carry = lax.cond(qi % 2 == 1, lambda c: step(qi - 1, c, False), lambda c: c, carry)
        (ma, la, acca), (mb, lb, accb) = step(qi, carry, True)
        ot = jnp.where(rows, acca / la, accb / lb)
        ot_ref[...] = ot
        o_ref[...] = ot.T.astype(o_ref.dtype)
        l_ref[0:1, :] = ma + jnp.log(la)
        l_ref[1:2, :] = mb + jnp.log(lb)

    row = pl.BlockSpec((None, 2, ATT_TQ), lambda j, i: (j, 0, i))
    return pl.pallas_call(
        body, name=name, grid=(npair, nq),
        in_specs=[pl.BlockSpec((ATT_TQ, LANES), lambda j, i: (i, j)),
                  pl.BlockSpec((s, LANES), lambda j, i: (0, npair + j)),
                  pl.BlockSpec((s, LANES), lambda j, i: (0, 2 * npair + j)),
                  row, pl.BlockSpec((None, 2, s, LANES), lambda j, i: (j, 0, 0, 0))],
        out_specs=[pl.BlockSpec((ATT_TQ, LANES), lambda j, i: (i, j)),
                   pl.BlockSpec((LANES, ATT_TQ), lambda j, i: (j, i)), row],
        out_shape=[jax.ShapeDtypeStruct((s, FOX_WIDTH), BF16), jax.ShapeDtypeStruct((FOX_WIDTH, s), F32),
                   jax.ShapeDtypeStruct((npair, 2, s), F32)],
        scratch_shapes=[pltpu.VMEM((LANES, s), BF16)],
        compiler_params=_params(),
    )(qkv, qkv, qkv, cum_r, cum_b)


def _attn_delta_t(do_t, o_t, *, name):
    s = o_t.shape[1]
    ts = _tile(s, 512)

    def body(do_ref, o_ref, d_ref):
        prod = do_ref[...].astype(F32) * o_ref[...]
        d_ref[0:1, :] = jnp.sum(prod[:FOX_HEAD_DIM], axis=0, keepdims=True)
        d_ref[1:2, :] = jnp.sum(prod[FOX_HEAD_DIM:], axis=0, keepdims=True)

    blk = pl.BlockSpec((LANES, ts), lambda j, i: (j, i))
    return pl.pallas_call(
        body, name=name, grid=(HEAD_PAIRS, s // ts), in_specs=[blk, blk],
        out_specs=pl.BlockSpec((None, 2, ts), lambda j, i: (j, 0, i)),
        out_shape=jax.ShapeDtypeStruct((HEAD_PAIRS, 2, s), F32), compiler_params=_params(),
    )(do_t, o_t)


def _attn_bwd_t(qkv, do, o_t, lse, cum_b, cum_r, *, name, dep=None):
    s = qkv.shape[0]
    nq = s // ATT_TQ
    npair = HEAD_PAIRS

    deps = [] if dep is None else [dep]

    def body(q_ref, k_ref, v_ref, do_ref, ot_ref, l_ref, cq_ref, ck_ref, *rest):
        dq_ref, dk_ref, dv_ref, dc_ref, qt_ref, dot_ref, dqt_ref, dl_ref = rest[len(deps):]
        ki = pl.program_id(1)
        m0 = _head_mask()
        rows = _row_mask()
        k2 = k_ref[...]
        v2 = v_ref[...]
        kt = _transpose_bf16(k2)
        ks = k2 * ATT_SCALE
        kz, vz = jnp.zeros_like(k2), jnp.zeros_like(v2)
        khs = (jnp.where(m0, ks, kz), jnp.where(m0, kz, ks))
        vhs = (jnp.where(m0, v2, vz), jnp.where(m0, vz, v2))
        cks = tuple(_lane_tile(ck_ref[hh], ATT_TQ) for hh in range(2))

        @pl.when(ki == 0)
        def _():
            dqt_ref[...] = jnp.zeros_like(dqt_ref)
            qt_ref[...] = _transpose_bf16(q_ref[...])
            do_t = do_ref[...].astype(F32).T
            dot_ref[...] = do_t.astype(BF16)
            prod = do_t * ot_ref[...]
            dl_ref[0:1, :] = jnp.sum(prod[:FOX_HEAD_DIM], axis=0, keepdims=True)
            dl_ref[1:2, :] = jnp.sum(prod[FOX_HEAD_DIM:], axis=0, keepdims=True)

        def step(qi, carry, masked):
            off = pl.multiple_of(qi * ATT_TQ, ATT_TQ)
            q2 = q_ref[pl.ds(off, ATT_TQ), :]
            do2 = do_ref[pl.ds(off, ATT_TQ), :]
            qt = qt_ref[:, pl.ds(off, ATT_TQ)]
            dot_ = dot_ref[:, pl.ds(off, ATT_TQ)]
            out, dqs = [], []
            for hh in range(2):
                dk_acc, dv_acc, dc_acc = carry[hh]
                sc = _dot(khs[hh], qt, _NN) + (cq_ref[hh:hh + 1, pl.ds(off, ATT_TQ)] - cks[hh])
                p = jnp.exp(sc - l_ref[hh:hh + 1, pl.ds(off, ATT_TQ)])
                if masked:
                    p = jnp.where(_causal_t(qi, ki), p, 0.0)
                dp = _dot(vhs[hh], dot_, _NN)
                ds = p * (dp - dl_ref[hh:hh + 1, pl.ds(off, ATT_TQ)])
                dc_acc = dc_acc - jnp.sum(ds, axis=1, keepdims=True)
                dss = (ds * ATT_SCALE).astype(BF16)
                dv_acc = dv_acc + _dot(p, do2, _NN)
                dk_acc = dk_acc + _dot(dss, q2, _NN)
                dqs.append(_dot(kt, dss, _NN))
                out.append((dk_acc, dv_acc, dc_acc))
            dqt_ref[:, pl.ds(off, ATT_TQ)] += jnp.where(rows, dqs[0], dqs[1])
            return tuple(out)

        init = tuple((jnp.zeros((ATT_TK, LANES), F32), jnp.zeros((ATT_TK, LANES), F32),
                      jnp.zeros((ATT_TK, 1), F32)) for _ in range(2))
        carry = step(ki, init, True)
        rest = nq - 1 - ki
        carry = lax.fori_loop(
            0, rest // 2, lambda t, c: step(ki + 2 + 2 * t, step(ki + 1 + 2 * t, c, False), False), carry)
        carry = lax.cond(rest % 2 == 1, lambda c: step(nq - 1, c, False), lambda c: c, carry)
        (dka, dva, dca), (dkb, dvb, dcb) = carry
        dk_ref[...] = jnp.where(m0, dka, dkb).astype(dk_ref.dtype)
        dv_ref[...] = jnp.where(m0, dva, dvb).astype(dv_ref.dtype)
        dc_ref[0] = jnp.broadcast_to(dca, (ATT_TK, LANES))
        dc_ref[1] = jnp.broadcast_to(dcb, (ATT_TK, LANES))

        @pl.when(ki == nq - 1)
        def _():
            dq_ref[...] = dqt_ref[...].T.astype(dq_ref.dtype)

    colfull = lambda base: pl.BlockSpec((s, LANES), lambda j, i: (0, base + j))
    colblk = lambda base: pl.BlockSpec((ATT_TK, LANES), lambda j, i: (i, base + j))
    stat = pl.BlockSpec((None, 2, s), lambda j, i: (j, 0, 0))
    bcast = pl.BlockSpec((None, 2, ATT_TK, LANES), lambda j, i: (j, 0, i, 0))
    grad = jax.ShapeDtypeStruct((s, FOX_WIDTH), BF16)
    return pl.pallas_call(
        body, name=name, grid=(npair, nq),
        in_specs=[colfull(0), colblk(npair), colblk(2 * npair), colfull(0),
                  pl.BlockSpec((LANES, s), lambda j, i: (j, 0)), stat, stat, bcast]
                 + [pl.BlockSpec(memory_space=pl.ANY)] * len(deps),
        out_specs=[colfull(0), colblk(0), colblk(0), bcast],
        out_shape=[grad, grad, grad, jax.ShapeDtypeStruct((npair, 2, s, LANES), F32)],
        scratch_shapes=[pltpu.VMEM((LANES, s), BF16), pltpu.VMEM((LANES, s), BF16), pltpu.VMEM((LANES, s), F32),
                        pltpu.VMEM((2, s), F32)],
        compiler_params=_params(),
    )(qkv, qkv, qkv, do, o_t, lse, cum_r, cum_b, *deps)


def _merge_fwd(zg, ya, yb, *, name, tm=256):
    s, d = ya.shape
    tm = _tile(s, tm)

    def body(zg_ref, ya_ref, yb_ref, m_ref):
        ga = _sigmoid(zg_ref[:, :d].astype(F32))
        gb = _sigmoid(zg_ref[:, d:].astype(F32))
        m_ref[...] = (ga * ya_ref[...].astype(F32) + gb * yb_ref[...].astype(F32)).astype(m_ref.dtype)

    row = pl.BlockSpec((tm, d), lambda i: (i, 0))
    row2 = pl.BlockSpec((tm, 2 * d), lambda i: (i, 0))
    return pl.pallas_call(
        body, name=name, grid=(s // tm,), in_specs=[row2, row, row], out_specs=row,
        out_shape=jax.ShapeDtypeStruct((s, d), BF16), compiler_params=_params(),
    )(zg, ya, yb)


def _merge_bwd(dm, zg, ya, yb, *, name, tm=256):
    s, d = ya.shape
    tm = _tile(s, tm)

    def body(dm_ref, zg_ref, ya_ref, yb_ref, dzg_ref, dya_ref, dyb_ref):
        dmv = dm_ref[...].astype(F32)
        ga = _sigmoid(zg_ref[:, :d].astype(F32))
        gb = _sigmoid(zg_ref[:, d:].astype(F32))
        dzg_ref[:, :d] = (dmv * ya_ref[...].astype(F32) * ga * (1.0 - ga)).astype(dzg_ref.dtype)
        dzg_ref[:, d:] = (dmv * yb_ref[...].astype(F32) * gb * (1.0 - gb)).astype(dzg_ref.dtype)
        dya_ref[...] = (dmv * ga).astype(dya_ref.dtype)
        dyb_ref[...] = (dmv * gb).astype(dyb_ref.dtype)

    row = pl.BlockSpec((tm, d), lambda i: (i, 0))
    row2 = pl.BlockSpec((tm, 2 * d), lambda i: (i, 0))
    return pl.pallas_call(
        body, name=name, grid=(s // tm,), in_specs=[row, row2, row, row], out_specs=[row2, row, row],
        out_shape=[jax.ShapeDtypeStruct((s, 2 * d), BF16), jax.ShapeDtypeStruct((s, d), BF16),
                   jax.ShapeDtypeStruct((s, d), BF16)],
        compiler_params=_params(),
    )(dm, zg, ya, yb)


SUBLANES = 8


def _shift_down(u, k, row):
    rolled = pltpu.roll(u, k, 0)
    head = jnp.where(row[:SUBLANES] >= k, rolled[:SUBLANES], 0.0)
    return jnp.concatenate([head, rolled[SUBLANES:]], axis=0)


def _shift_up(u, k, row):
    n = u.shape[0]
    rolled = pltpu.roll(u, n - k, 0)
    tail = jnp.where(row[n - SUBLANES:] < n - k, rolled[n - SUBLANES:], 0.0)
    return jnp.concatenate([rolled[:n - SUBLANES], tail], axis=0)


def _conv_act_fwd(up_a, up_b, cw_a, cw_b, cb_a, cb_b, *, name, tc=128):
    s, f = up_a.shape
    tc = _tile(f, tc)

    def body(ua_ref, ub_ref, wa_ref, wb_ref, ba_ref, bb_ref, act_ref):
        row = lax.broadcasted_iota(jnp.int32, (s, tc), 0)

        def conv(u_ref, w_ref, b_ref):
            u = u_ref[...].astype(F32)
            return (b_ref[...] + w_ref[0:1, :] * _shift_down(u, 2, row)
                    + w_ref[1:2, :] * _shift_down(u, 1, row) + w_ref[2:3, :] * u)

        ca = conv(ua_ref, wa_ref, ba_ref)
        cb = conv(ub_ref, wb_ref, bb_ref)
        act_ref[...] = (_gelu(ca) * cb).astype(act_ref.dtype)

    col = pl.BlockSpec((s, tc), lambda j: (0, j))
    w3 = pl.BlockSpec((3, tc), lambda j: (0, j))
    b1 = pl.BlockSpec((1, tc), lambda j: (0, j))
    return pl.pallas_call(
        body, name=name, grid=(f // tc,), in_specs=[col, col, w3, w3, b1, b1], out_specs=col,
        out_shape=jax.ShapeDtypeStruct((s, f), BF16), compiler_params=_params(),
    )(up_a, up_b, cw_a, cw_b, cb_a, cb_b)


def _conv_act_bwd(up_a, up_b, dact, cw_a, cw_b, cb_a, cb_b, *, name, tc=128):
    s, f = up_a.shape
    tc = _tile(f, tc)

    def body(ua_ref, ub_ref, da_ref, wa_ref, wb_ref, ba_ref, bb_ref, dua_ref, dub_ref, dwa_ref, dwb_ref):
        row = lax.broadcasted_iota(jnp.int32, (s, tc), 0)

        def conv(u_ref, w_ref, b_ref):
            u = u_ref[...].astype(F32)
            u1 = _shift_down(u, 1, row)
            u2 = _shift_down(u, 2, row)
            return u, u1, u2, b_ref[...] + w_ref[0:1, :] * u2 + w_ref[1:2, :] * u1 + w_ref[2:3, :] * u

        def back(dc, taps, w_ref, du_ref, dw_ref):
            u, u1, u2 = taps
            dw_ref[0:1, :] = jnp.sum(dc * u2, axis=0, keepdims=True)
            dw_ref[1:2, :] = jnp.sum(dc * u1, axis=0, keepdims=True)
            dw_ref[2:3, :] = jnp.sum(dc * u, axis=0, keepdims=True)
            dw_ref[3:4, :] = jnp.sum(dc, axis=0, keepdims=True)
            du = (w_ref[2:3, :] * dc + w_ref[1:2, :] * _shift_up(dc, 1, row)
                  + w_ref[0:1, :] * _shift_up(dc, 2, row))
            du_ref[...] = du.astype(du_ref.dtype)

        ua, ua1, ua2, ca = conv(ua_ref, wa_ref, ba_ref)
        ub, ub1, ub2, cb = conv(ub_ref, wb_ref, bb_ref)
        g, dg = _gelu_and_grad(ca)
        dact_v = da_ref[...].astype(F32)
        back(dact_v * cb * dg, (ua, ua1, ua2), wa_ref, dua_ref, dwa_ref)
        back(dact_v * g, (ub, ub1, ub2), wb_ref, dub_ref, dwb_ref)

    col = pl.BlockSpec((s, tc), lambda j: (0, j))
    w3 = pl.BlockSpec((3, tc), lambda j: (0, j))
    w4 = pl.BlockSpec((4, tc), lambda j: (0, j))
    b1 = pl.BlockSpec((1, tc), lambda j: (0, j))
    return pl.pallas_call(
        body, name=name, grid=(f // tc,), in_specs=[col, col, col, w3, w3, b1, b1],
        out_specs=[col, col, w4, w4],
        out_shape=[jax.ShapeDtypeStruct((s, f), BF16), jax.ShapeDtypeStruct((s, f), BF16),
                   jax.ShapeDtypeStruct((4, f), F32), jax.ShapeDtypeStruct((4, f), F32)],
        compiler_params=_params(),
    )(up_a, up_b, dact, cw_a, cw_b, cb_a, cb_b)


def _ple_final(x2, ple, zp, target, g_final, *, name, tm=256):
    s, d = x2.shape
    tm = _tile(s, tm)

    def body(x_ref, ple_ref, zp_ref, t_ref, g_ref, dx_ref, dple_ref, dzp_ref, dg_ref, loss_ref):
        @pl.when(pl.program_id(0) == 0)
        def _():
            dg_ref[...] = jnp.zeros_like(dg_ref)
            loss_ref[...] = jnp.zeros_like(loss_ref)

        gp = _sigmoid(zp_ref[...].astype(F32))
        plev = ple_ref[...].astype(F32)
        x3 = x_ref[...] + plev * gp
        r = lax.rsqrt(jnp.mean(x3 * x3, axis=-1, keepdims=True) + EPS)
        xhat = x3 * r
        gv = g_ref[...]
        diff = xhat * gv - t_ref[...]
        loss_ref[...] += 0.5 * jnp.sum(jnp.mean(diff * diff, axis=-1, keepdims=True), axis=0, keepdims=True)
        dy = diff * (1.0 / d)
        dg_ref[...] += jnp.sum(dy * xhat, axis=0, keepdims=True)
        dyg = dy * gv
        dx3 = r * (dyg - xhat * jnp.mean(dyg * xhat, axis=-1, keepdims=True))
        dx_ref[...] = dx3
        dple_ref[...] = (dx3 * gp).astype(dple_ref.dtype)
        dzp_ref[...] = (dx3 * plev * gp * (1.0 - gp)).astype(dzp_ref.dtype)

    row = pl.BlockSpec((tm, d), lambda i: (i, 0))
    vec = pl.BlockSpec((1, d), lambda i: (0, 0))
    return pl.pallas_call(
        body, name=name, grid=(s // tm,), in_specs=[row, row, row, row, vec],
        out_specs=[row, row, row, vec, pl.BlockSpec((1, LANES), lambda i: (0, 0))],
        out_shape=[jax.ShapeDtypeStruct((s, d), F32), jax.ShapeDtypeStruct((s, d), BF16),
                   jax.ShapeDtypeStruct((s, d), BF16), jax.ShapeDtypeStruct((1, d), F32),
                   jax.ShapeDtypeStruct((1, LANES), F32)],
        compiler_params=_params(),
    )(x2, ple, zp, target, g_final)


def _device_step(x, p, target, w, get_w_in=None, get_w_rest=None, on_grads_ffn=None, on_grads_small=None,
                 on_grads_mix=None):
    s = x.shape[0]
    g = {}
    w = dict(w)

    h = _rms_fwd(x, w["norm_mix_g"], name="rms_mix", dep=w.get("first_dep"))
    if get_w_in is not None:
        w.update(get_w_in(h))
    z_uv = _mm(h, w["w_uv"], mode="nn", out_dtype=BF16, name="proj_uv", tm=1024, dep=w.get("proj_dep"))
    qkv = _mm(h, w["w_qkv"], mode="nn", out_dtype=BF16, name="proj_qkv", tm=1024)
    zg = _mm(h, w["w_g"], mode="nn", out_dtype=BF16, name="proj_gate", tm=1024)
    f = _mm(h, w["w_f"], mode="nn", out_dtype=F32, name="proj_f", tm=1024)

    a = _gmlp_fwd(z_uv, w["gmlp_ln_g"], w["gmlp_ln_b"], w["gmlp_w_s"], w["gmlp_b_s_t"], name="gmlp_fwd")

    cum_b, cum_t = _fox_cum(f, w["b_f"], name="fox_cum")
    cum_b = cum_b.reshape(HEAD_PAIRS, 2, s, LANES)
    cum_r = cum_t[:FOX_HEADS].reshape(HEAD_PAIRS, 2, s)
    b, o_t, lse = _attn_fwd_t(qkv, cum_b, cum_r, name="attn_fwd")
    if get_w_rest is not None:
        w.update(get_w_rest(b))

    ya = _mm(a, w["w_branch_a"], mode="nn", out_dtype=BF16, name="branch_a", tm=1024)
    yb = _mm(b, w["w_branch_b"], mode="nn", out_dtype=BF16, name="branch_b", tm=1024)
    merged = _merge_fwd(zg, ya, yb, name="merge_fwd")
    x1 = _mm(merged, w["w_out"], mode="nn", out_dtype=F32, name="proj_out", add=x, tm=1024)

    h2 = _rms_fwd(x1, w["norm_ffn_g"], name="rms_ffn")
    up_a = _mm(h2, w["w_up_a"], mode="nn", out_dtype=BF16, name="up_a", tm=1024, tn=D_FF // 2)
    up_b = _mm(h2, w["w_up_b"], mode="nn", out_dtype=BF16, name="up_b", tm=1024, tn=D_FF // 2)
    cw, cb = w["conv_w"], w["conv_b"]
    conv_args = (cw[:, :D_FF], cw[:, D_FF:], cb[:, :D_FF], cb[:, D_FF:])
    act = _conv_act_fwd(up_a, up_b, *conv_args, name="conv_act_fwd")
    x2 = _mm(act, w["w_down"], mode="nn", out_dtype=F32, name="down", add=x1, tm=512)

    h3 = _rms_fwd(x2, w["norm_ple_g"], name="rms_ple")
    ple = _mm(p, w["w_ple"], mode="nn", out_dtype=BF16, name="ple_proj", tm=1024)
    zp = _mm(h3, w["w_ple_gate"], mode="nn", out_dtype=BF16, name="ple_gate", tm=1024)
    dx3, dple, dzp, g["norm_final_g"], loss = _ple_final(x2, ple, zp, target, w["norm_final_g"], name="ple_final")

    g["w_ple"] = _mm(p, dple, mode="tn", out_dtype=BF16, name="dw_ple")
    g["w_ple_gate"] = _mm(h3, dzp, mode="tn", out_dtype=BF16, name="dw_ple_gate")
    dh3 = _mm(dzp, w["w_ple_gate"], mode="nt", out_dtype=BF16, name="dh3")
    dx2, dx2_b, g["norm_ple_g"] = _rms_bwd(x2, w["norm_ple_g"], dh3, dx3, name="rms_ple_bwd")

    g["w_down"] = _mm(act, dx2_b, mode="tn", out_dtype=BF16, name="dw_down", tm=D_FF // 2)
    dact = _mm(dx2_b, w["w_down"], mode="nt", out_dtype=BF16, name="dact", tn=D_FF // 2)
    dup_a, dup_b, dcw_a, dcw_b = _conv_act_bwd(up_a, up_b, dact, *conv_args, name="conv_act_bwd")
    g["conv_w"] = jnp.concatenate([dcw_a[:3], dcw_b[:3]], axis=1)
    g["conv_b"] = jnp.concatenate([dcw_a[3:], dcw_b[3:]], axis=1)
    g["w_up_a"] = _mm(h2, dup_a, mode="tn", out_dtype=BF16, name="dw_up_a", tn=D_FF // 2)
    g["w_up_b"] = _mm(h2, dup_b, mode="tn", out_dtype=BF16, name="dw_up_b", tn=D_FF // 2)
    dh2 = _mm_nt_sum([(dup_a, w["w_up_a"]), (dup_b, w["w_up_b"])], out_dtype=BF16, name="dh2")
    dx1, dx1_b, g["norm_ffn_g"] = _rms_bwd(x1, w["norm_ffn_g"], dh2, dx2, name="rms_ffn_bwd")
    dep = on_grads_ffn(g) if on_grads_ffn is not None else None

    g["w_out"] = _mm(merged, dx1_b, mode="tn", out_dtype=BF16, name="dw_out")
    dmerged = _mm(dx1_b, w["w_out"], mode="nt", out_dtype=BF16, name="dmerged", dep=dep)
    dzg, dya, dyb = _merge_bwd(dmerged, zg, ya, yb, name="merge_bwd")
    g["w_branch_a"] = _mm(a, dya, mode="tn", out_dtype=BF16, name="dw_branch_a")
    g["w_branch_b"] = _mm(b, dyb, mode="tn", out_dtype=BF16, name="dw_branch_b")
    da = _mm(dya, w["w_branch_a"], mode="nt", out_dtype=BF16, name="da")
    db = _mm(dyb, w["w_branch_b"], mode="nt", out_dtype=BF16, name="db")

    dz_uv, g["gmlp_w_s"], dbs_t, g["gmlp_ln_g"], g["gmlp_ln_b"] = _gmlp_bwd(
        z_uv, da, w["gmlp_ln_g"], w["gmlp_ln_b"], w["gmlp_w_s"], w["gmlp_b_s_t"], name="gmlp_bwd")
    g["gmlp_b_s"] = dbs_t[:, :GMLP_GROUPS].T
    dep = on_grads_small(g) if on_grads_small is not None else None

    dq, dk, dv, dcum_b = _attn_bwd_t(qkv, db, o_t, lse, cum_b, cum_r, name="attn_bwd", dep=dep)
    dcum_t = jnp.pad(dcum_b[..., 0].reshape(FOX_HEADS, s), ((0, LANES - FOX_HEADS), (0, 0)))
    df, g["b_f"] = _fox_dlogit(dcum_t, f, w["b_f"], name="fox_dlogit")
    dqkv = jnp.concatenate([dq, dk, dv], axis=1)

    g["w_uv"] = _mm(h, dz_uv, mode="tn", out_dtype=BF16, name="dw_uv")
    g["w_qkv"] = _mm(h, dqkv, mode="tn", out_dtype=BF16, name="dw_qkv")
    g["w_f"] = _mm(h, df, mode="tn", out_dtype=BF16, name="dw_f")
    g["w_g"] = _mm(h, dzg, mode="tn", out_dtype=BF16, name="dw_g")
    dep = on_grads_mix(g) if on_grads_mix is not None else None
    dh = _mm_nt_sum([(dz_uv, w["w_uv"]), (dqkv, w["w_qkv"]), (df, w["w_f"]), (dzg, w["w_g"])],
                    out_dtype=BF16, name="dh", dep=dep)
    dx0, _, g["norm_mix_g"] = _rms_bwd(x, w["norm_mix_g"], dh, dx1, name="rms_mix_bwd")
    return loss, dx0, g


def _coords():
    return lax.axis_index("x"), lax.axis_index("y"), lax.axis_index("c")


def _other_chips(x, y):
    return [(1 - x, y), (x, 1 - y), (1 - x, 1 - y)]


def _remote(src, dst, send_sem, recv_sem, dev):
    return pltpu.make_async_remote_copy(src_ref=src, dst_ref=dst, send_sem=send_sem, recv_sem=recv_sem,
                                        device_id=dev, device_id_type=MESH)


_ANY = pl.BlockSpec(memory_space=pl.ANY)


def _gather_weights(halved, whole, *, name):
    nh, n = len(halved), len(halved) + len(whole)
    arrays = list(halved) + list(whole)

    def body(*refs):
        ins, outs = refs[:n], refs[n:2 * n]
        send_sems, recv_sems = refs[2 * n:]
        x, y, c = _coords()
        me, sib = 2 * x + y, (x, y, 1 - c)
        chips = _other_chips(x, y)

        def half(i, which):
            h = ins[i].shape[0] // 2
            return pl.ds(pl.multiple_of(which * h, 16), h)

        sends = []
        for i in range(n):
            src, dst = (ins[i].at[half(i, c)], outs[i].at[me, half(i, c)]) if i < nh else (ins[i], outs[i].at[me])
            for k, (cx, cy) in enumerate(chips):
                cp = _remote(src, dst, send_sems.at[i, k], recv_sems.at[i, k], (cx, cy, c))
                cp.start()
                sends.append(cp)
        for i in range(n):
            for k, (cx, cy) in enumerate(chips):
                got = outs[i].at[2 * cx + cy, half(i, c)] if i < nh else outs[i].at[2 * cx + cy]
                _remote(got, got, send_sems.at[i, k], recv_sems.at[i, k], sib).wait_recv()
                if i < nh:
                    cp = _remote(got, got, send_sems.at[i, 3 + k], recv_sems.at[i, 3 + k], sib)
                    cp.start()
                    sends.append(cp)
        for i in range(nh):
            for k, (cx, cy) in enumerate(chips):
                got = outs[i].at[2 * cx + cy, half(i, 1 - c)]
                _remote(got, got, send_sems.at[i, 3 + k], recv_sems.at[i, 3 + k], sib).wait_recv()
        for cp in sends:
            cp.wait_send()

    outs = pl.pallas_call(
        body, name=name, in_specs=[_ANY] * n, out_specs=[_ANY] * n,
        out_shape=[jax.ShapeDtypeStruct((N_CHIPS,) + a.shape, a.dtype) for a in arrays],
        scratch_shapes=[pltpu.SemaphoreType.DMA((n, 6)), pltpu.SemaphoreType.DMA((n, 6))],
        compiler_params=_params(),
    )(*arrays)
    chip = 2 * lax.axis_index("x") + lax.axis_index("y")
    return [lax.dynamic_update_index_in_dim(o, a, chip, 0) for o, a in zip(outs, arrays)]


def _pair_exchange(gs, *, name):
    n = len(gs)

    def body(*refs):
        ins, outs = refs[:n], refs[n:2 * n]
        send_sems, recv_sems = refs[2 * n:]
        x, y, c = _coords()
        copies = []
        for i in range(n):
            for j in range(N_CHIPS):
                cp = _remote(ins[i].at[j, 1 - c], outs[i].at[j], send_sems.at[i, j], recv_sems.at[i, j], (x, y, 1 - c))
                cp.start()
                copies.append(cp)
        for cp in copies:
            cp.wait()

    return pl.pallas_call(
        body, name=name, in_specs=[_ANY] * n, out_specs=[_ANY] * n,
        out_shape=[jax.ShapeDtypeStruct((N_CHIPS,) + a.shape[2:], a.dtype) for a in gs],
        scratch_shapes=[pltpu.SemaphoreType.DMA((n, N_CHIPS)), pltpu.SemaphoreType.DMA((n, N_CHIPS))],
        compiler_params=_params(),
    )(*gs)


def _chip_exchange(ss, *, name):
    n = len(ss)

    def body(*refs):
        ins, outs = refs[:n], refs[n:2 * n]
        send_sems, recv_sems = refs[2 * n:]
        x, y, c = _coords()
        me = 2 * x + y
        chips = _other_chips(x, y)
        sends = []
        for i in range(n):
            for k, (cx, cy) in enumerate(chips):
                cp = _remote(ins[i].at[2 * cx + cy], outs[i].at[me], send_sems.at[i, k], recv_sems.at[i, k], (cx, cy, c))
                cp.start()
                sends.append(cp)
        for i in range(n):
            for k, (cx, cy) in enumerate(chips):
                got = outs[i].at[2 * cx + cy]
                _remote(got, got, send_sems.at[i, k], recv_sems.at[i, k], (cx, cy, c)).wait_recv()
        for cp in sends:
            cp.wait_send()

    return pl.pallas_call(
        body, name=name, in_specs=[_ANY] * n, out_specs=[_ANY] * n,
        out_shape=[jax.ShapeDtypeStruct(a.shape, a.dtype) for a in ss],
        scratch_shapes=[pltpu.SemaphoreType.DMA((n, 3)), pltpu.SemaphoreType.DMA((n, 3))],
        compiler_params=_params(),
    )(*ss)


def _pair_share(hs, *, name):
    n = len(hs)

    def body(*refs):
        ins, outs = refs[:n], refs[n:2 * n]
        send_sems, recv_sems = refs[2 * n:]
        x, y, c = _coords()
        copies = []
        for i in range(n):
            cp = _remote(ins[i], outs[i], send_sems.at[i], recv_sems.at[i], (x, y, 1 - c))
            cp.start()
            copies.append(cp)
        for cp in copies:
            cp.wait()

    return pl.pallas_call(
        body, name=name, in_specs=[_ANY] * n, out_specs=[_ANY] * n,
        out_shape=[jax.ShapeDtypeStruct(a.shape, a.dtype) for a in hs],
        scratch_shapes=[pltpu.SemaphoreType.DMA((n,)), pltpu.SemaphoreType.DMA((n,))],
        compiler_params=_params(),
    )(*hs)


def _all_exchange(vec, *, name):
    def body(v_ref, o_ref, send_sems, recv_sems, local_sem):
        x, y, c = _coords()
        me = 4 * x + 2 * y + c
        local = pltpu.make_async_copy(v_ref, o_ref.at[me], local_sem)
        local.start()
        copies = []
        k = 0
        for dx in (0, 1):
            for dy in (0, 1):
                for dc in (0, 1):
                    if dx or dy or dc:
                        peer = (1 - x if dx else x, 1 - y if dy else y, 1 - c if dc else c)
                        cp = _remote(v_ref, o_ref.at[me], send_sems.at[k], recv_sems.at[k], peer)
                        cp.start()
                        copies.append(cp)
                        k += 1
        for cp in copies:
            cp.wait()
        local.wait()

    return pl.pallas_call(
        body, name=name, in_specs=[_ANY], out_specs=_ANY,
        out_shape=jax.ShapeDtypeStruct((8,) + vec.shape, vec.dtype),
        scratch_shapes=[pltpu.SemaphoreType.DMA((7,)), pltpu.SemaphoreType.DMA((7,)), pltpu.SemaphoreType.DMA(())],
        compiler_params=_params(),
    )(vec)


_HBM = pl.BlockSpec(memory_space=pltpu.HBM)
_SEM = pl.BlockSpec(memory_space=pltpu.SEMAPHORE)
_EFFECT = pltpu.SideEffectType.DATAFLOW_SIDE_EFFECTING


def _copies_start(srcs, lands, plan, n_copies, *, name, after=()):
    ns, n = len(srcs), len(srcs) + len(lands)
    na = len(after)

    def body(*refs):
        send_sems, recv_sems = refs[n + na], refs[n + na + 1]
        token = refs[-1]
        for k, (src, dst, dev) in enumerate(plan(refs[:ns], refs[ns:n])):
            _remote(src, dst, send_sems.at[k], recv_sems.at[k], dev).start()
        token[...] = jnp.zeros_like(token)

    arrays = list(srcs) + list(lands)
    outs = pl.pallas_call(
        body, name=name,
        out_shape=(pltpu.SemaphoreType.DMA((n_copies,)), pltpu.SemaphoreType.DMA((n_copies,)),
                   *[pltpu.HBM(a.shape, a.dtype) for a in arrays], jax.ShapeDtypeStruct((8, LANES), F32)),
        in_specs=[_HBM] * n + [_ANY] * na,
        out_specs=(_SEM, _SEM, *[_HBM] * n, pl.BlockSpec(memory_space=pltpu.VMEM)),
        input_output_aliases={i: 2 + i for i in range(n)},
        compiler_params=pltpu.CompilerParams(has_side_effects=_EFFECT),
    )(*[pltpu.with_memory_space_constraint(a, pltpu.HBM) for a in arrays], *after)
    return outs[0], outs[1], list(outs[2:2 + ns]), list(outs[2 + ns:2 + n]), outs[-1]


def _copies_wait(send_sems, recv_sems, srcs, lands, plan, first, after, *, name):
    ns, n = len(srcs), len(srcs) + len(lands)

    def body(*refs):
        send, recv = refs[n], refs[n + 1]
        for k, (src, dst, dev) in enumerate(plan(refs[:ns], refs[ns:n])):
            cp = _remote(src, dst, send.at[first + k], recv.at[first + k], dev)
            cp.wait_send()
            cp.wait_recv()

    arrays = list(srcs) + list(lands)
    outs = pl.pallas_call(
        body, name=name, out_shape=tuple(pltpu.HBM(a.shape, a.dtype) for a in arrays),
        in_specs=[_HBM] * n + [_SEM, _SEM] + [_ANY] * len(after), out_specs=tuple([_HBM] * n),
        input_output_aliases={i: i for i in range(n)},
        compiler_params=pltpu.CompilerParams(has_side_effects=_EFFECT),
    )(*arrays, send_sems, recv_sems, *after)
    return list(outs[:ns]), list(outs[ns:])


def _gather_plan(halved):
    def plan(srcs, lands):
        x, y, c = _coords()
        me = 2 * x + y
        out = []
        for i, (src, land) in enumerate(zip(srcs, lands)):
            if halved[i]:
                h = src.shape[0] // 2
                rows = pl.ds(pl.multiple_of(c * h, 16), h)
                src, dst = src.at[rows], land.at[me, rows]
            else:
                dst = land.at[me]
            out += [(src, dst, (cx, cy, c)) for cx, cy in _other_chips(x, y)]
        return out
    return plan


def _forward_halves(lands, *, name):
    n = len(lands)

    def body(*refs):
        ins, outs = refs[:n], refs[n:2 * n]
        send_sems, recv_sems = refs[2 * n:]
        x, y, c = _coords()
        copies = []
        for i in range(n):
            h = ins[i].shape[1] // 2
            rows = pl.ds(pl.multiple_of(c * h, 16), h)
            for k, (cx, cy) in enumerate(_other_chips(x, y)):
                cp = _remote(ins[i].at[2 * cx + cy, rows], outs[i].at[2 * cx + cy, rows],
                             send_sems.at[i, k], recv_sems.at[i, k], (x, y, 1 - c))
                cp.start()
                copies.append(cp)
        for cp in copies:
            cp.wait()

    return pl.pallas_call(
        body, name=name, in_specs=[_ANY] * n, out_specs=[_ANY] * n,
        out_shape=[jax.ShapeDtypeStruct(a.shape, a.dtype) for a in lands],
        input_output_aliases={i: i for i in range(n)},
        scratch_shapes=[pltpu.SemaphoreType.DMA((n, 3)), pltpu.SemaphoreType.DMA((n, 3))],
        compiler_params=_params(),
    )(*lands)


def _all_plan(srcs, lands):
    x, y, c = _coords()
    me = 4 * x + 2 * y + c
    out = []
    for src, land in zip(srcs, lands):
        for dx in (0, 1):
            for dy in (0, 1):
                for dc in (0, 1):
                    if dx or dy or dc:
                        out.append((src, land.at[me], (1 - x if dx else x, 1 - y if dy else y, 1 - c if dc else c)))
    return out


def _chip_plan(srcs, lands):
    x, y, c = _coords()
    me = 2 * x + y
    out = []
    for src, land in zip(srcs, lands):
        out += [(src.at[2 * cx + cy], land.at[me], (cx, cy, c)) for cx, cy in _other_chips(x, y)]
    return out


ROW_BLOCK_BYTES = 2 * 1024 * 1024


def _rtile(r, pref, mult, row_bytes=None):
    if row_bytes is not None:
        pref = max(pref, ROW_BLOCK_BYTES // row_bytes)
    t = (min(r, pref) // mult) * mult
    while t >= mult:
        if r % t == 0:
            return t
        t -= mult
    return r


def _pair_add(g, recv, core, *, name):
    _, _, r2, cols = g.shape
    tr = _rtile(r2, 256, 16, row_bytes=2 * cols)

    def body(c_ref, g_ref, r_ref, o_ref):
        o_ref[...] = (g_ref[...].astype(F32) + r_ref[...].astype(F32)).astype(o_ref.dtype)

    blk = pl.BlockSpec((None, tr, cols), lambda j, i, c_ref: (j, i, 0))
    return pl.pallas_call(
        body, name=name,
        grid_spec=pltpu.PrefetchScalarGridSpec(
            num_scalar_prefetch=1, grid=(N_CHIPS, r2 // tr),
            in_specs=[pl.BlockSpec((None, None, tr, cols), lambda j, i, c_ref: (j, c_ref[0], i, 0)), blk],
            out_specs=blk),
        out_shape=jax.ShapeDtypeStruct(recv.shape, recv.dtype), compiler_params=_params(),
    )(core, g, recv)


def _sum_slots(a, out_dtype, *, name):
    n, r, cols = a.shape
    whole = n * r * cols * a.dtype.itemsize <= 4 * ROW_BLOCK_BYTES
    tr = r if whole else _rtile(r, 256, 16)

    def body(a_ref, o_ref):
        acc = a_ref[0].astype(F32)
        for j in range(1, n):
            acc = acc + a_ref[j].astype(F32)
        o_ref[...] = acc.astype(o_ref.dtype)

    return pl.pallas_call(
        body, name=name, grid=(r // tr,),
        in_specs=[pl.BlockSpec((n, tr, cols), lambda i: (0, i, 0))],
        out_specs=pl.BlockSpec((tr, cols), lambda i: (i, 0)),
        out_shape=jax.ShapeDtypeStruct((r, cols), out_dtype), compiler_params=_params(),
    )(a)


def _chip_sum(own, recv, chip, *, name):
    _, r2, cols = own.shape
    tr = _rtile(r2, 256, 16, row_bytes=2 * cols)

    def body(chip_ref, own_ref, *rest):
        o_ref = rest[-1]
        acc = None
        for j in range(N_CHIPS):
            term = jnp.where(chip_ref[0] == j, own_ref[...], rest[j][...]).astype(F32)
            acc = term if acc is None else acc + term
        o_ref[...] = acc

    def slot(j):
        return pl.BlockSpec((None, tr, cols),
                            lambda i, chip_ref: (jnp.where(chip_ref[0] == j, (j + 1) % N_CHIPS, j), i, 0))

    return pl.pallas_call(
        body, name=name,
        grid_spec=pltpu.PrefetchScalarGridSpec(
            num_scalar_prefetch=1, grid=(r2 // tr,),
            in_specs=[pl.BlockSpec((None, tr, cols), lambda i, chip_ref: (chip_ref[0], i, 0))]
                     + [slot(j) for j in range(N_CHIPS)],
            out_specs=pl.BlockSpec((tr, cols), lambda i, chip_ref: (i, 0))),
        out_shape=jax.ShapeDtypeStruct((r2, cols), F32), compiler_params=_params(),
    )(chip, own, *([recv] * N_CHIPS))


def _adam_update(w, gv, m, v):
    c1 = 1.0 / (1.0 - ADAM_B1 ** ADAM_STEP)
    c2 = 1.0 / (1.0 - ADAM_B2 ** ADAM_STEP)
    nm = ADAM_B1 * m + (1.0 - ADAM_B1) * gv
    nv = ADAM_B2 * v + (1.0 - ADAM_B2) * gv * gv
    return -ADAM_LR * ((nm * c1) / (jnp.sqrt(nv * c2) + ADAM_EPS) + ADAM_WD * w), nm, nv


def _adamw_halves(w, g_mine, g_other, m, v, core, *, name):
    r, cols = w.shape
    r2 = r // 2
    tr = _rtile(r2, 256, 8, row_bytes=4 * cols)
    nt = r2 // tr

    def body(core_ref, w_ref, gm_ref, go_ref, m_ref, v_ref, g_ref, d_ref, nm_ref, nv_ref):
        gv = jnp.where(pl.program_id(0) == core_ref[0], gm_ref[...], go_ref[...])
        g_ref[...] = gv
        d_ref[...], nm_ref[...], nv_ref[...] = _adam_update(w_ref[...], gv, m_ref[...], v_ref[...])

    full = pl.BlockSpec((tr, cols), lambda hf, i, core_ref: (hf * nt + i, 0))
    half = pl.BlockSpec((tr, cols), lambda hf, i, core_ref: (i, 0))
    shape = jax.ShapeDtypeStruct((r, cols), F32)
    return pl.pallas_call(
        body, name=name,
        grid_spec=pltpu.PrefetchScalarGridSpec(
            num_scalar_prefetch=1, grid=(2, nt), in_specs=[full, half, half, full, full], out_specs=[full] * 4),
        out_shape=[shape] * 4, compiler_params=_params(),
    )(core, w, g_mine, g_other, m, v)


def _adamw_split_rows(w, g_mine, g_other, m, v, core, *, name, tc=256):
    r, cols = w.shape
    r2 = g_mine.shape[0]
    tc = _tile(cols, tc)

    def body(core_ref, w_ref, gm_ref, go_ref, m_ref, v_ref, g_ref, d_ref, nm_ref, nv_ref):
        mine_first = core_ref[0] == 0
        for lo, hi, first in ((0, r2, True), (r2, r, False)):
            n = hi - lo
            gm, go = gm_ref[0:n, :], go_ref[0:n, :]
            gv = jnp.where(mine_first, gm, go) if first else jnp.where(mine_first, go, gm)
            g_ref[lo:hi, :] = gv
            d_ref[lo:hi, :], nm_ref[lo:hi, :], nv_ref[lo:hi, :] = _adam_update(
                w_ref[lo:hi, :], gv, m_ref[lo:hi, :], v_ref[lo:hi, :])

    full = pl.BlockSpec((r, tc), lambda j, core_ref: (0, j))
    half = pl.BlockSpec((r2, tc), lambda j, core_ref: (0, j))
    shape = jax.ShapeDtypeStruct((r, cols), F32)
    return pl.pallas_call(
        body, name=name,
        grid_spec=pltpu.PrefetchScalarGridSpec(
            num_scalar_prefetch=1, grid=(cols // tc,), in_specs=[full, half, half, full, full],
            out_specs=[full] * 4),
        out_shape=[shape] * 4, compiler_params=_params(),
    )(core, w, g_mine, g_other, m, v)


def _adamw(w, g, m, v, *, name, rows=256):
    r, cols = w.shape
    tr = _rtile(r, rows, 8)

    def body(w_ref, g_ref, m_ref, v_ref, d_ref, nm_ref, nv_ref):
        d_ref[...], nm_ref[...], nv_ref[...] = _adam_update(w_ref[...], g_ref[...], m_ref[...], v_ref[...])

    blk = pl.BlockSpec((tr, cols), lambda i: (i, 0))
    shape = jax.ShapeDtypeStruct((r, cols), F32)
    return pl.pallas_call(
        body, name=name, grid=(r // tr,), in_specs=[blk] * 4, out_specs=[blk] * 3,
        out_shape=[shape] * 3, compiler_params=_params(),
    )(w, g, m, v)


_BIG = (("w_in", 1), ("w_branch_a", 0), ("w_branch_b", 0), ("w_out", 0), ("w_up", 1), ("w_down", 0),
        ("w_ple", 1), ("w_ple_gate", 0))
_SMALL = ("gmlp_ln_g", "gmlp_ln_b", "gmlp_w_s", "gmlp_b_s", "norm_ffn_g", "conv_b", "norm_ple_g", "norm_final_g",
          "b_f", "norm_mix_g")
N_LATE = 2
_WEIGHTS = ("norm_mix_g", "w_in", "b_f", "gmlp_ln_g", "gmlp_ln_b", "gmlp_w_s", "gmlp_b_s", "w_branch_a",
            "w_branch_b", "w_out", "norm_ffn_g", "w_up", "conv_w", "conv_b", "w_down", "norm_ple_g", "w_ple",
            "w_ple_gate", "norm_final_g")
_PACK_ROWS = 8


def _pack(arrays):
    parts = []
    for a in arrays:
        flat = a.reshape(-1)
        unit = _PACK_ROWS * LANES
        flat = jnp.pad(flat, (0, (-flat.shape[0]) % unit))
        parts.append(flat.reshape(-1, LANES))
    return jnp.concatenate(parts, axis=0)


def _unpack(packed, shapes):
    out, row = [], 0
    for shp in shapes:
        size = math.prod(shp)
        rows = -(-size // (_PACK_ROWS * LANES)) * _PACK_ROWS
        out.append(packed[row:row + rows].reshape(-1)[:size].reshape(shp))
        row += rows
    return out


def _take_cols(parts, lo, hi):
    out, start = [], 0
    for a in parts:
        width = a.shape[1]
        a0, a1 = max(lo, start) - start, min(hi, start + width) - start
        if a1 > a0:
            out.append(a if (a0, a1) == (0, width) else a[:, a0:a1])
        start += width
    return out[0] if len(out) == 1 else jnp.concatenate(out, axis=1)


def _take_rows(parts, lo, hi):
    out, start = [], 0
    for a in parts:
        height = a.shape[0]
        a0, a1 = max(lo, start) - start, min(hi, start + height) - start
        if a1 > a0:
            out.append(a if (a0, a1) == (0, height) else a[a0:a1])
        start += height
    return out[0] if len(out) == 1 else jnp.concatenate(out, axis=0)


def _assemble(gathered, axis):
    n, r, cols = gathered.shape
    if axis == 0:
        return gathered.reshape(n * r, cols)
    return _take_cols([gathered[j] for j in range(n)], 0, n * cols)


def _to_chunks(parts, axis):
    rows, total = parts[0].shape[0], sum(a.shape[1] for a in parts)
    if axis == 0:
        r, cols = rows // N_CHIPS, total
        chunks = _take_cols(parts, 0, total).reshape(N_CHIPS, r, cols)
    else:
        r, cols = rows, total // N_CHIPS
        chunks = jnp.stack([_take_cols(parts, j * cols, (j + 1) * cols) for j in range(N_CHIPS)])
    return chunks.reshape(N_CHIPS, 2, r // 2, cols)


def kernel(x, p, norm_mix_g, w_in, b_f, gmlp_ln_g, gmlp_ln_b, gmlp_w_s, gmlp_b_s, w_branch_a, w_branch_b, w_out, norm_ffn_g, w_up, conv_w, conv_b, w_down, norm_ple_g, w_ple, w_ple_gate, norm_final_g, loss_target, m_norm_mix_g, m_w_in, m_b_f, m_gmlp_ln_g, m_gmlp_ln_b, m_gmlp_w_s, m_gmlp_b_s, m_w_branch_a, m_w_branch_b, m_w_out, m_norm_ffn_g, m_w_up, m_conv_w, m_conv_b, m_w_down, m_norm_ple_g, m_w_ple, m_w_ple_gate, m_norm_final_g, v_norm_mix_g, v_w_in, v_b_f, v_gmlp_ln_g, v_gmlp_ln_b, v_gmlp_w_s, v_gmlp_b_s, v_w_branch_a, v_w_branch_b, v_w_out, v_norm_ffn_g, v_w_up, v_conv_w, v_conv_b, v_w_down, v_norm_ple_g, v_w_ple, v_w_ple_gate, v_norm_final_g):
    args = dict(locals())
    wt = {n: args[n] for n in _WEIGHTS}
    mom = {n: args["m_" + n] for n in _WEIGHTS}
    var = {n: args["v_" + n] for n in _WEIGHTS}
    chip = 2 * lax.axis_index("x") + lax.axis_index("y")
    core = lax.axis_index("c").astype(jnp.int32).reshape(1)

    chip1 = chip.astype(jnp.int32).reshape(1)
    device = 2 * chip + lax.axis_index("c")
    axis_of = dict(_BIG)
    names = [n for n, _ in _BIG]
    put_mine = lambda land, mine: lax.dynamic_update_index_in_dim(land, mine, chip, 0)

    shard_in = w_in[0].astype(BF16)
    sems_in = _copies_start([shard_in], [lax.empty((N_CHIPS,) + shard_in.shape, BF16)], _gather_plan([True]), 3,
                            name="gather_start_in")
    _, wt["w_in"], mom["w_in"], var["w_in"] = lax.optimization_barrier((sems_in[4], w_in, m_w_in, v_w_in))
    shards = [wt[n][0].astype(BF16) for n in names[1:]] + [conv_w[0]]
    halved = [True] * len(names[1:]) + [False]
    lands = [lax.empty((N_CHIPS,) + a.shape, a.dtype) for a in shards]
    send_sems, recv_sems, srcs, lands, rest_token = _copies_start(
        shards, lands, _gather_plan(halved), 3 * len(shards), name="gather_start_rest", after=[sems_in[4]])
    o1 = 2 * GMLP_WIDTH
    o2 = o1 + 3 * FOX_WIDTH
    o3 = o2 + FOX_HEADS
    fpad = ((0, 0), (0, LANES - FOX_HEADS))
    w = {
        "conv_b": conv_b, "norm_mix_g": norm_mix_g, "norm_ffn_g": norm_ffn_g, "norm_ple_g": norm_ple_g,
        "norm_final_g": norm_final_g.reshape(1, D_MODEL), "b_f": jnp.pad(b_f, fpad),
        "gmlp_ln_g": gmlp_ln_g, "gmlp_ln_b": gmlp_ln_b, "gmlp_w_s": gmlp_w_s[0],
        "gmlp_b_s_t": jnp.pad(gmlp_b_s[0].T, ((0, 0), (0, LANES - GMLP_GROUPS))),
        "first_dep": rest_token,
    }

    def get_w_in(after):
        early = [a.reshape(a.shape[-2:]) for a in (wt["w_in"], mom["w_in"], var["w_in"])]
        _, got = _copies_wait(sems_in[0], sems_in[1], sems_in[2], sems_in[3], _gather_plan([True]), 0,
                              [after] + early, name="gather_wait_in")
        got = _forward_halves(got, name="gather_forward_in")
        slots = put_mine(got[0], shard_in)
        slots = [slots[j] for j in range(N_CHIPS)]
        return {"w_uv": _take_cols(slots, 0, o1), "w_qkv": _take_cols(slots, o1, o2),
                "w_f": jnp.pad(_take_cols(slots, o2, o3), fpad), "w_g": _take_cols(slots, o3, o3 + 2 * D_MODEL)}

    def get_w_rest(after):
        _, got = _copies_wait(send_sems, recv_sems, srcs, lands, _gather_plan(halved), 0, [after],
                              name="gather_wait_rest")
        got = list(_forward_halves(got[:-1], name="gather_forward_rest")) + got[-1:]
        slots = {n: put_mine(got[i], shards[i]) for i, n in enumerate(names[1:])}
        full = {n: _assemble(slots[n], axis_of[n]) for n in names[1:] if n != "w_up"}
        up = [slots["w_up"][j] for j in range(N_CHIPS)]
        return {"w_branch_a": full["w_branch_a"], "w_branch_b": full["w_branch_b"], "w_out": full["w_out"],
                "w_up_a": _take_cols(up, 0, D_FF), "w_up_b": _take_cols(up, D_FF, 2 * D_FF),
                "w_down": full["w_down"], "w_ple": full["w_ple"], "w_ple_gate": full["w_ple_gate"],
                "conv_w": _assemble(put_mine(got[-1], shards[-1]), 1)}

    grads, delta, new_m, new_v = {}, {}, {}, {}
    pending = {}

    def to_chunks(n, gr):
        return _to_chunks(gr if isinstance(gr, list) else [gr], axis_of[n])

    def reduce_start(group, gfull, tag):
        chunks = [to_chunks(n, gfull[n]) for n in group]
        from_sibling = _pair_exchange(chunks, name="grad_pair_exchange_" + tag)
        pair_sums = [_pair_add(chunks[i], from_sibling[i], core, name="grad_pair_add_" + n) for i, n in enumerate(group)]
        empty = [lax.empty(a.shape, a.dtype) for a in pair_sums]
        ssem, rsem, own, recv, token = _copies_start(pair_sums, empty, _chip_plan, 3 * len(group),
                                                     name="grad_chip_start_" + tag)
        pending[tag] = (ssem, rsem, own, recv)
        return token

    def reduce_finish(group, tag, after):
        ssem, rsem, own, recv = pending[tag]
        own, recv = _copies_wait(ssem, rsem, own, recv, _chip_plan, 0, after, name="grad_chip_wait_" + tag)
        halves = [_chip_sum(own[i], recv[i], chip1, name="grad_chip_sum_" + n) for i, n in enumerate(group)]
        other_halves = _pair_share(halves, name="grad_pair_share_" + tag)
        for i, n in enumerate(group):
            shp = wt[n].shape
            outs = _adamw_halves(wt[n].reshape(shp[-2:]), halves[i], other_halves[i], mom[n].reshape(shp[-2:]),
                                 var[n].reshape(shp[-2:]), core, name="adamw_" + n)
            grads[n], delta[n], new_m[n], new_v[n] = (o.reshape(shp) for o in outs)
        return new_v[group[-1]]

    ffn_group = ("w_up", "w_down", "w_ple", "w_ple_gate")
    mix_group = ("w_in", "w_branch_a", "w_branch_b", "w_out")

    def on_grads_ffn(g):
        gfull = dict(g)
        gfull["w_up"] = [g["w_up_a"], g["w_up_b"]]
        return reduce_start(ffn_group, gfull, "ffn")

    def on_grads_small(g):
        vec = _pack([g[n] for n in _SMALL[:-N_LATE]] + [g["conv_w"]])
        ssem, rsem, own, recv, token = _copies_start(
            [vec], [lax.empty((8,) + vec.shape, F32)], _all_plan, 7, name="small_start")
        pending["small"] = (ssem, rsem, own, recv)
        return token

    def on_grads_mix(g):
        gfull = dict(g)
        gfull["w_in"] = [g["w_uv"], g["w_qkv"], g["w_f"][:, :FOX_HEADS], g["w_g"]]
        token = reduce_start(mix_group, gfull, "mix")
        pending["ffn_done"] = reduce_finish(ffn_group, "ffn", [token])
        return token

    loss, grad_x, g = _device_step(x[0], p[0, 0], loss_target[0], w, get_w_in, get_w_rest, on_grads_ffn,
                                   on_grads_small, on_grads_mix)

    mix_done = reduce_finish(mix_group, "mix", [grad_x, pending["ffn_done"]])
    ssem, rsem, own, recv = pending["small"]
    own, recv = _copies_wait(ssem, rsem, own, recv, _all_plan, 0, [mix_done], name="small_wait")
    vec_early = _sum_slots(lax.dynamic_update_index_in_dim(recv[0], own[0], device, 0), F32, name="small_sum")
    vec_late = _pack([g["b_f"][:, :FOX_HEADS], g["norm_mix_g"]])
    vec_late = _sum_slots(_all_exchange(vec_late, name="small_exchange_late"), F32, name="small_sum_late")
    early_rows = _pack([wt[n] for n in _SMALL[:-N_LATE]]).shape[0]
    vec = jnp.concatenate([vec_early[:early_rows], vec_late], axis=0)
    for n, a in zip(_SMALL, _unpack(vec, [wt[n].shape for n in _SMALL])):
        grads[n] = a
    conv_w_grad = _unpack(vec_early[early_rows:], [(3, 2 * D_FF)])[0]
    grads["conv_w"] = lax.dynamic_slice_in_dim(conv_w_grad, chip * conv_w.shape[2], conv_w.shape[2], axis=1).reshape(conv_w.shape)

    shp = conv_w.shape
    outs = _adamw(conv_w.reshape(shp[-2:]), grads["conv_w"].reshape(shp[-2:]), m_conv_w.reshape(shp[-2:]),
                  v_conv_w.reshape(shp[-2:]), name="adamw_conv_w")
    delta["conv_w"], new_m["conv_w"], new_v["conv_w"] = (o.reshape(shp) for o in outs)
    outs = _adamw(_pack([wt[n] for n in _SMALL]), vec, _pack([mom[n] for n in _SMALL]),
                  _pack([var[n] for n in _SMALL]), name="adamw_small", rows=2048)
    for d, o in zip((delta, new_m, new_v), outs):
        for n, a in zip(_SMALL, _unpack(o, [wt[n].shape for n in _SMALL])):
            d[n] = a

    total_loss = lax.psum(loss[0, 0], ("x", "y", "c"))
    return (total_loss, grad_x.reshape(x.shape), *[grads[n] for n in _WEIGHTS], *[delta[n] for n in _WEIGHTS],
            *[new_m[n] for n in _WEIGHTS], *[new_v[n] for n in _WEIGHTS])
```

```python
import functools
import math

import jax
import jax.numpy as jnp
from jax import lax
from jax.experimental import pallas as pl
from jax.experimental.pallas import tpu as pltpu

F32 = jnp.float32
BF16 = jnp.bfloat16

D_MODEL = 1024
EPS = 1e-6
CHUNK = 64
GMLP_GROUPS = 8
GMLP_BLOCK = 128
GMLP_WIDTH = 1024
FOX_HEADS = 16
FOX_HEAD_DIM = 64
FOX_WIDTH = 1024
HEAD_PAIRS = FOX_HEADS // 2
ATT_BLOCK = 128
D_FF = 2816
PLE_DIM = 256
LANES = 128
BF16_TILE_ROWS = 16
N_CHIPS = 4

ADAM_LR = 0.001
ADAM_B1 = 0.9
ADAM_B2 = 0.999
ADAM_EPS = 1e-08
ADAM_WD = 0.01
ADAM_STEP = 10

VMEM_LIMIT = 56 * 1024 * 1024
MESH = pl.DeviceIdType.MESH

_NN = (((1,), (0,)), ((), ()))
_NT = (((1,), (1,)), ((), ()))
_TN = (((0,), (0,)), ((), ()))


def _params(**kw):
    return pltpu.CompilerParams(vmem_limit_bytes=VMEM_LIMIT, **kw)


def _tile(dim, pref):
    if dim <= pref:
        return dim
    t = (pref // LANES) * LANES
    while t >= LANES:
        if dim % t == 0:
            return t
        t -= LANES
    return dim


def _dot(a, b, dn):
    return lax.dot_general(a.astype(BF16), b.astype(BF16), dn, preferred_element_type=F32)


def _gelu(x):
    c = math.sqrt(2.0 / math.pi)
    t = jnp.tanh(c * (x + 0.044715 * x * x * x))
    return 0.5 * x * (1.0 + t)


def _gelu_and_grad(x):
    c = math.sqrt(2.0 / math.pi)
    x2 = x * x
    t = jnp.tanh(c * (x + 0.044715 * x2 * x))
    g = 0.5 * x * (1.0 + t)
    dg = 0.5 * (1.0 + t) + 0.5 * x * (1.0 - t * t) * c * (1.0 + 3.0 * 0.044715 * x2)
    return g, dg


def _sigmoid(x):
    return 1.0 / (1.0 + jnp.exp(-x))


def _mm(a, b, *, mode, out_dtype, name, add=None, tm=512, tn=512, dep=None):
    if mode == "nn":
        m, k = a.shape
        k2, n = b.shape
    elif mode == "nt":
        m, k = a.shape
        n, k2 = b.shape
    else:
        k, m = a.shape
        k2, n = b.shape
    assert k == k2, (name, a.shape, b.shape)
    tm = _tile(m, tm)
    tn = _tile(n, tn)
    dn = {"nn": _NN, "nt": _NT, "tn": _TN}[mode]

    def body(a_ref, b_ref, *rest):
        o_ref = rest[-1]
        acc = _dot(a_ref[...], b_ref[...], dn)
        if add is not None:
            acc = acc + rest[0][...].astype(F32)
        o_ref[...] = acc.astype(o_ref.dtype)

    a_spec = pl.BlockSpec((k, tm), lambda i, j: (0, i)) if mode == "tn" else pl.BlockSpec((tm, k), lambda i, j: (i, 0))
    b_spec = pl.BlockSpec((tn, k), lambda i, j: (j, 0)) if mode == "nt" else pl.BlockSpec((k, tn), lambda i, j: (0, j))
    o_spec = pl.BlockSpec((tm, tn), lambda i, j: (i, j))
    in_specs = [a_spec, b_spec]
    args = [a, b]
    if add is not None:
        in_specs.append(o_spec)
        args.append(add)
    if dep is not None:
        in_specs.append(pl.BlockSpec(memory_space=pl.ANY))
        args.append(dep)
    return pl.pallas_call(
        body, name=name, grid=(m // tm, n // tn), in_specs=in_specs, out_specs=o_spec,
        out_shape=jax.ShapeDtypeStruct((m, n), out_dtype), compiler_params=_params(),
    )(*args)


def _mm_nt_sum(pairs, *, out_dtype, name, tm=256, dep=None):
    m, n = pairs[0][0].shape[0], pairs[0][1].shape[0]
    tm = _tile(m, tm)
    np_ = len(pairs)

    def body(*refs):
        o_ref = refs[-1] if dep is None else refs[-1]
        acc = None
        for p in range(np_):
            part = _dot(refs[2 * p][...], refs[2 * p + 1][...], _NT)
            acc = part if acc is None else acc + part
        o_ref[...] = acc.astype(o_ref.dtype)

    in_specs, args = [], []
    for a, b in pairs:
        assert a.shape[0] == m and b.shape[0] == n and a.shape[1] == b.shape[1], (name, a.shape, b.shape)
        in_specs += [pl.BlockSpec((tm, a.shape[1]), lambda i: (i, 0)), pl.BlockSpec(b.shape, lambda i: (0, 0))]
        args += [a, b]
    if dep is not None:
        in_specs.append(pl.BlockSpec(memory_space=pl.ANY))
        args.append(dep)
    return pl.pallas_call(
        body, name=name, grid=(m // tm,), in_specs=in_specs, out_specs=pl.BlockSpec((tm, n), lambda i: (i, 0)),
        out_shape=jax.ShapeDtypeStruct((m, n), out_dtype), compiler_params=_params(),
    )(*args)


def _rms_fwd(x, g, *, name, tm=256, dep=None):
    s, d = x.shape
    tm = _tile(s, tm)

    def body(x_ref, g_ref, *rest):
        h_ref = rest[-1]
        xv = x_ref[...]
        r = lax.rsqrt(jnp.mean(xv * xv, axis=-1, keepdims=True) + EPS)
        h_ref[...] = (xv * r * g_ref[...]).astype(h_ref.dtype)

    deps = [] if dep is None else [dep]
    return pl.pallas_call(
        body, name=name, grid=(s // tm,),
        in_specs=[pl.BlockSpec((tm, d), lambda i: (i, 0)), pl.BlockSpec((1, d), lambda i: (0, 0))]
                 + [pl.BlockSpec(memory_space=pl.ANY)] * len(deps),
        out_specs=pl.BlockSpec((tm, d), lambda i: (i, 0)),
        out_shape=jax.ShapeDtypeStruct((s, d), BF16), compiler_params=_params(),
    )(x, g, *deps)


def _rms_bwd(x, g, dh, dres, *, name, tm=256):
    s, d = x.shape
    tm = _tile(s, tm)

    def body(x_ref, g_ref, dh_ref, dres_ref, dx_ref, dxb_ref, dg_ref):
        xv = x_ref[...]
        r = lax.rsqrt(jnp.mean(xv * xv, axis=-1, keepdims=True) + EPS)
        xhat = xv * r
        dhv = dh_ref[...].astype(F32)
        dyg = dhv * g_ref[...]
        dx = dres_ref[...] + r * (dyg - xhat * jnp.mean(dyg * xhat, axis=-1, keepdims=True))
        dx_ref[...] = dx
        dxb_ref[...] = dx.astype(dxb_ref.dtype)

        @pl.when(pl.program_id(0) == 0)
        def _():
            dg_ref[...] = jnp.zeros_like(dg_ref)

        dg_ref[...] += jnp.sum(dhv * xhat, axis=0, keepdims=True)

    row = pl.BlockSpec((tm, d), lambda i: (i, 0))
    vec = pl.BlockSpec((1, d), lambda i: (0, 0))
    return pl.pallas_call(
        body, name=name, grid=(s // tm,), in_specs=[row, vec, row, row], out_specs=[row, row, vec],
        out_shape=[jax.ShapeDtypeStruct((s, d), F32), jax.ShapeDtypeStruct((s, d), BF16),
                   jax.ShapeDtypeStruct((1, d), F32)],
        compiler_params=_params(),
    )(x, g, dh, dres)


def _gmlp_mask():
    t = lax.broadcasted_iota(jnp.int32, (GMLP_BLOCK, GMLP_BLOCK), 0)
    s_ = lax.broadcasted_iota(jnp.int32, (GMLP_BLOCK, GMLP_BLOCK), 1)
    return (s_ // CHUNK) <= (t // CHUNK)


def _gmlp_norm(zv, ln_g, ln_b):
    vv, dvv = _gelu_and_grad(zv)
    mu = jnp.mean(vv, axis=-1, keepdims=True)
    xc = vv - mu
    rstd = lax.rsqrt(jnp.mean(xc * xc, axis=-1, keepdims=True) + EPS)
    vhat = xc * rstd
    return vhat * ln_g + ln_b, vhat, rstd, dvv


def _gmlp_fwd(z_uv, ln_g, ln_b, w_s, b_s_t, *, name):
    s = z_uv.shape[0]
    w = GMLP_WIDTH
    gd = w // GMLP_GROUPS

    def body(z_ref, lg_ref, lb_ref, ws_ref, bs_ref, a_ref):
        u = _gelu(z_ref[:, :w].astype(F32))
        vn, _, _, _ = _gmlp_norm(z_ref[:, w:].astype(F32), lg_ref[...], lb_ref[...])
        mask = _gmlp_mask()
        for g in range(GMLP_GROUPS):
            wm = jnp.where(mask, ws_ref[g], 0.0)
            mixed = _dot(wm, vn[:, g * gd:(g + 1) * gd], _NN) + bs_ref[:, g:g + 1]
            a_ref[:, g * gd:(g + 1) * gd] = (u[:, g * gd:(g + 1) * gd] * mixed).astype(a_ref.dtype)

    full = lambda shape: pl.BlockSpec(shape, lambda i: (0,) * len(shape))
    return pl.pallas_call(
        body, name=name, grid=(s // GMLP_BLOCK,),
        in_specs=[pl.BlockSpec((GMLP_BLOCK, 2 * w), lambda i: (i, 0)), full((1, w)), full((1, w)),
                  full((GMLP_GROUPS, GMLP_BLOCK, GMLP_BLOCK)), full((GMLP_BLOCK, LANES))],
        out_specs=pl.BlockSpec((GMLP_BLOCK, w), lambda i: (i, 0)),
        out_shape=jax.ShapeDtypeStruct((s, w), BF16), compiler_params=_params(),
    )(z_uv, ln_g, ln_b, w_s, b_s_t)


def _gmlp_bwd(z_uv, da, ln_g, ln_b, w_s, b_s_t, *, name):
    s = z_uv.shape[0]
    w = GMLP_WIDTH
    gd = w // GMLP_GROUPS

    def body(z_ref, da_ref, lg_ref, lb_ref, ws_ref, bs_ref, dz_ref, dws_ref, dbs_ref, dlg_ref, dlb_ref):
        @pl.when(pl.program_id(0) == 0)
        def _():
            dws_ref[...] = jnp.zeros_like(dws_ref)
            dbs_ref[...] = jnp.zeros_like(dbs_ref)
            dlg_ref[...] = jnp.zeros_like(dlg_ref)
            dlb_ref[...] = jnp.zeros_like(dlb_ref)

        u, du_dz = _gelu_and_grad(z_ref[:, :w].astype(F32))
        lg = lg_ref[...]
        vn, vhat, rstd, dvv_dz = _gmlp_norm(z_ref[:, w:].astype(F32), lg, lb_ref[...])
        dav = da_ref[...].astype(F32)
        mask = _gmlp_mask()
        lane = lax.broadcasted_iota(jnp.int32, (GMLP_BLOCK, LANES), 1)
        dvn_parts = []
        dbs = jnp.zeros((GMLP_BLOCK, LANES), F32)
        for g in range(GMLP_GROUPS):
            sl = slice(g * gd, (g + 1) * gd)
            wm = jnp.where(mask, ws_ref[g], 0.0)
            vn_g = vn[:, sl]
            mixed = _dot(wm, vn_g, _NN) + bs_ref[:, g:g + 1]
            dmixed = dav[:, sl] * u[:, sl]
            dz_ref[:, sl] = (dav[:, sl] * mixed * du_dz[:, sl]).astype(dz_ref.dtype)
            dvn_parts.append(_dot(wm, dmixed, _TN))
            dws_ref[g] += jnp.where(mask, _dot(dmixed, vn_g, _NT), 0.0)
            dbs = dbs + jnp.where(lane == g, jnp.sum(dmixed, axis=-1, keepdims=True), 0.0)
        dbs_ref[...] += dbs
        dvn = jnp.concatenate(dvn_parts, axis=-1)
        dlg_ref[...] += jnp.sum(dvn * vhat, axis=0, keepdims=True)
        dlb_ref[...] += jnp.sum(dvn, axis=0, keepdims=True)
        dyg = dvn * lg
        dvv = rstd * (dyg - jnp.mean(dyg, axis=-1, keepdims=True)
                      - vhat * jnp.mean(dyg * vhat, axis=-1, keepdims=True))
        dz_ref[:, w:] = (dvv * dvv_dz).astype(dz_ref.dtype)

    full = lambda shape: pl.BlockSpec(shape, lambda i: (0,) * len(shape))
    return pl.pallas_call(
        body, name=name, grid=(s // GMLP_BLOCK,),
        in_specs=[pl.BlockSpec((GMLP_BLOCK, 2 * w), lambda i: (i, 0)),
                  pl.BlockSpec((GMLP_BLOCK, w), lambda i: (i, 0)), full((1, w)), full((1, w)),
                  full((GMLP_GROUPS, GMLP_BLOCK, GMLP_BLOCK)), full((GMLP_BLOCK, LANES))],
        out_specs=[pl.BlockSpec((GMLP_BLOCK, 2 * w), lambda i: (i, 0)),
                   full((GMLP_GROUPS, GMLP_BLOCK, GMLP_BLOCK)), full((GMLP_BLOCK, LANES)),
                   full((1, w)), full((1, w))],
        out_shape=[jax.ShapeDtypeStruct((s, 2 * w), BF16),
                   jax.ShapeDtypeStruct((GMLP_GROUPS, GMLP_BLOCK, GMLP_BLOCK), F32),
                   jax.ShapeDtypeStruct((GMLP_BLOCK, LANES), F32),
                   jax.ShapeDtypeStruct((1, w), F32), jax.ShapeDtypeStruct((1, w), F32)],
        compiler_params=_params(),
    )(z_uv, da, ln_g, ln_b, w_s, b_s_t)


def _tri(lower):
    r = lax.broadcasted_iota(jnp.int32, (ATT_BLOCK, ATT_BLOCK), 0)
    c = lax.broadcasted_iota(jnp.int32, (ATT_BLOCK, ATT_BLOCK), 1)
    return jnp.where((c <= r) if lower else (c >= r), 1.0, 0.0).astype(F32)


def _log_sigmoid(x):
    return jnp.minimum(x, 0.0) - jnp.log(1.0 + jnp.exp(-jnp.abs(x)))


def _fox_cum(f, b_f, *, name):
    s = f.shape[0]
    nb = s // ATT_BLOCK

    def body(f_ref, b_ref, cb_ref, ct_ref, carry):
        @pl.when(pl.program_id(0) == 0)
        def _():
            carry[...] = jnp.zeros_like(carry)

        lf = _log_sigmoid(f_ref[...] + b_ref[...])
        cum = lax.dot_general(_tri(True), lf, _NN, precision=lax.Precision.HIGHEST,
                              preferred_element_type=F32) + carry[...]
        carry[...] = cum[ATT_BLOCK - 1:ATT_BLOCK, :]
        for h in range(FOX_HEADS):
            cb_ref[h] = jnp.broadcast_to(cum[:, h:h + 1], (ATT_BLOCK, LANES))
        ct_ref[...] = cum.T

    return pl.pallas_call(
        body, name=name, grid=(nb,),
        in_specs=[pl.BlockSpec((ATT_BLOCK, LANES), lambda i: (i, 0)), pl.BlockSpec((1, LANES), lambda i: (0, 0))],
        out_specs=[pl.BlockSpec((FOX_HEADS, ATT_BLOCK, LANES), lambda i: (0, i, 0)),
                   pl.BlockSpec((LANES, ATT_BLOCK), lambda i: (0, i))],
        out_shape=[jax.ShapeDtypeStruct((FOX_HEADS, s, LANES), F32), jax.ShapeDtypeStruct((LANES, s), F32)],
        scratch_shapes=[pltpu.VMEM((1, LANES), F32)], compiler_params=_params(),
    )(f, b_f)


def _fox_dlogit(dcum_t, f, b_f, *, name):
    s = f.shape[0]
    nb = s // ATT_BLOCK

    def body(dc_ref, f_ref, b_ref, df_ref, db_ref, carry):
        @pl.when(pl.program_id(0) == 0)
        def _():
            carry[...] = jnp.zeros_like(carry)
            db_ref[...] = jnp.zeros_like(db_ref)

        d = dc_ref[...].T
        dlog = lax.dot_general(_tri(False), d, _NN, precision=lax.Precision.HIGHEST,
                               preferred_element_type=F32) + carry[...]
        carry[...] = dlog[0:1, :]
        df = dlog * (1.0 - _sigmoid(f_ref[...] + b_ref[...]))
        df_ref[...] = df
        db_ref[...] += jnp.sum(df, axis=0, keepdims=True)

    rev = lambda i: nb - 1 - i
    return pl.pallas_call(
        body, name=name, grid=(nb,),
        in_specs=[pl.BlockSpec((LANES, ATT_BLOCK), lambda i: (0, rev(i))),
                  pl.BlockSpec((ATT_BLOCK, LANES), lambda i: (rev(i), 0)),
                  pl.BlockSpec((1, LANES), lambda i: (0, 0))],
        out_specs=[pl.BlockSpec((ATT_BLOCK, LANES), lambda i: (rev(i), 0)),
                   pl.BlockSpec((1, LANES), lambda i: (0, 0))],
        out_shape=[jax.ShapeDtypeStruct((s, LANES), F32), jax.ShapeDtypeStruct((1, LANES), F32)],
        scratch_shapes=[pltpu.VMEM((1, LANES), F32)], compiler_params=_params(),
    )(dcum_t, f, b_f)


def _causal(qi, ki):
    r = lax.broadcasted_iota(jnp.int32, (ATT_BLOCK, ATT_BLOCK), 0) + qi * ATT_BLOCK
    c = lax.broadcasted_iota(jnp.int32, (ATT_BLOCK, ATT_BLOCK), 1) + ki * ATT_BLOCK
    return c <= r


def _head_mask():
    return lax.broadcasted_iota(jnp.int32, (1, LANES), 1) < FOX_HEAD_DIM


def _attn_fwd(qkv, cum_b, cum_r, *, name):
    s = qkv.shape[0]
    nq = s // ATT_BLOCK
    scale = FOX_HEAD_DIM ** -0.5
    npair = HEAD_PAIRS

    def body(q_ref, k_ref, v_ref, cq_ref, ck_ref, o_ref, l_ref):
        qi = pl.program_id(1)
        m0 = _head_mask()
        q2 = q_ref[...]
        zero = jnp.zeros_like(q2)
        qs = (jnp.where(m0, q2, zero), jnp.where(m0, zero, q2))
        cqs = (cq_ref[0], cq_ref[1])

        def step(ki, carry, masked):
            off = pl.multiple_of(ki * ATT_BLOCK, ATT_BLOCK)
            k2 = k_ref[pl.ds(off, ATT_BLOCK), :]
            v2 = v_ref[pl.ds(off, ATT_BLOCK), :]
            out = []
            for hh in range(2):
                m, l, acc = carry[hh]
                sc = _dot(qs[hh], k2, _NT) * scale + (cqs[hh] - ck_ref[hh:hh + 1, pl.ds(off, ATT_BLOCK)])
                if masked:
                    sc = jnp.where(_causal(qi, ki), sc, -1e30)
                m_new = jnp.maximum(m, jnp.max(sc, axis=-1, keepdims=True))
                alpha = jnp.exp(m - m_new)
                p = jnp.exp(sc - m_new)
                l = alpha * l + jnp.sum(p, axis=-1, keepdims=True)
                acc = alpha * acc + _dot(p, v2, _NN)
                out.append((m_new, l, acc))
            return tuple(out)

        init = tuple((jnp.full((ATT_BLOCK, 1), -1e30, F32), jnp.zeros((ATT_BLOCK, 1), F32),
                      jnp.zeros((ATT_BLOCK, LANES), F32)) for _ in range(2))
        carry = lax.fori_loop(0, qi, lambda ki, c: step(ki, c, False), init)
        (ma, la, acca), (mb, lb, accb) = step(qi, carry, True)
        o_ref[...] = jnp.where(m0, acca / la, accb / lb).astype(o_ref.dtype)
        l_ref[0] = jnp.broadcast_to(ma + jnp.log(la), (ATT_BLOCK, LANES))
        l_ref[1] = jnp.broadcast_to(mb + jnp.log(lb), (ATT_BLOCK, LANES))

    stat = pl.BlockSpec((None, 2, ATT_BLOCK, LANES), lambda j, i: (j, 0, i, 0))
    row = pl.BlockSpec((None, 2, s), lambda j, i: (j, 0, 0))
    return pl.pallas_call(
        body, name=name, grid=(npair, nq),
        in_specs=[pl.BlockSpec((ATT_BLOCK, LANES), lambda j, i: (i, j)),
                  pl.BlockSpec((s, LANES), lambda j, i: (0, npair + j)),
                  pl.BlockSpec((s, LANES), lambda j, i: (0, 2 * npair + j)),
                  stat, row],
        out_specs=[pl.BlockSpec((ATT_BLOCK, LANES), lambda j, i: (i, j)), stat],
        out_shape=[jax.ShapeDtypeStruct((s, FOX_WIDTH), BF16),
                   jax.ShapeDtypeStruct((npair, 2, s, LANES), F32)],
        compiler_params=_params(),
    )(qkv, qkv, qkv, cum_b, cum_r)


def _attn_delta(qkv, do, lse_b, cum_b, cum_r, *, name):
    s = qkv.shape[0]
    nq = s // ATT_BLOCK
    scale = FOX_HEAD_DIM ** -0.5
    npair = HEAD_PAIRS

    def body(q_ref, k_ref, v_ref, do_ref, l_ref, cq_ref, ck_ref, d_ref):
        qi = pl.program_id(1)
        m0 = _head_mask()
        q2 = q_ref[...]
        do2 = do_ref[...]
        qs = (jnp.where(m0, q2, jnp.zeros_like(q2)), jnp.where(m0, jnp.zeros_like(q2), q2))
        dos = (jnp.where(m0, do2, jnp.zeros_like(do2)), jnp.where(m0, jnp.zeros_like(do2), do2))

        def step(ki, carry, masked):
            off = pl.multiple_of(ki * ATT_BLOCK, ATT_BLOCK)
            k2 = k_ref[pl.ds(off, ATT_BLOCK), :]
            v2 = v_ref[pl.ds(off, ATT_BLOCK), :]
            out = []
            for hh in range(2):
                sc = _dot(qs[hh], k2, _NT) * scale + (cq_ref[hh] - ck_ref[hh:hh + 1, pl.ds(off, ATT_BLOCK)])
                p = jnp.exp(sc - l_ref[hh])
                if masked:
                    p = jnp.where(_causal(qi, ki), p, 0.0)
                out.append(carry[hh] + jnp.sum(p * _dot(dos[hh], v2, _NT), axis=-1, keepdims=True))
            return tuple(out)

        init = (jnp.zeros((ATT_BLOCK, 1), F32), jnp.zeros((ATT_BLOCK, 1), F32))
        carry = lax.fori_loop(0, qi, lambda ki, c: step(ki, c, False), init)
        da, db = step(qi, carry, True)
        d_ref[0] = jnp.broadcast_to(da, (ATT_BLOCK, LANES))
        d_ref[1] = jnp.broadcast_to(db, (ATT_BLOCK, LANES))

    stat = pl.BlockSpec((None, 2, ATT_BLOCK, LANES), lambda j, i: (j, 0, i, 0))
    return pl.pallas_call(
        body, name=name, grid=(npair, nq),
        in_specs=[pl.BlockSpec((ATT_BLOCK, LANES), lambda j, i: (i, j)),
                  pl.BlockSpec((s, LANES), lambda j, i: (0, npair + j)),
                  pl.BlockSpec((s, LANES), lambda j, i: (0, 2 * npair + j)),
                  pl.BlockSpec((ATT_BLOCK, LANES), lambda j, i: (i, j)),
                  stat, stat, pl.BlockSpec((None, 2, s), lambda j, i: (j, 0, 0))],
        out_specs=stat,
        out_shape=jax.ShapeDtypeStruct((npair, 2, s, LANES), F32), compiler_params=_params(),
    )(qkv, qkv, qkv, do, lse_b, cum_b, cum_r)


def _attn_bwd(qkv, do, lse_b, delta_b, cum_b, cum_r, *, name):
    s = qkv.shape[0]
    nq = s // ATT_BLOCK
    scale = FOX_HEAD_DIM ** -0.5
    npair = HEAD_PAIRS

    def body(q_ref, k_ref, v_ref, do_ref, l_ref, dl_ref, cq_ref, ck_ref, dq_ref, dk_ref, dv_ref, dc_ref):
        ki = pl.program_id(1)
        m0 = _head_mask()
        k2 = k_ref[...]
        v2 = v_ref[...]
        koff = pl.multiple_of(ki * ATT_BLOCK, ATT_BLOCK)

        @pl.when(ki == 0)
        def _():
            dq_ref[...] = jnp.zeros_like(dq_ref)

        def step(qi, carry, masked):
            off = pl.multiple_of(qi * ATT_BLOCK, ATT_BLOCK)
            q2 = q_ref[pl.ds(off, ATT_BLOCK), :]
            do2 = do_ref[pl.ds(off, ATT_BLOCK), :]
            qzero = jnp.zeros_like(q2)
            dzero = jnp.zeros_like(do2)
            out = []
            dqs = []
            for hh in range(2):
                dk_acc, dv_acc, dc_acc = carry[hh]
                keep = m0 if hh == 0 else jnp.logical_not(m0)
                qh = jnp.where(keep, q2, qzero)
                doh = jnp.where(keep, do2, dzero)
                sc = _dot(qh, k2, _NT) * scale + (cq_ref[hh, pl.ds(off, ATT_BLOCK), :]
                                                 - ck_ref[hh:hh + 1, pl.ds(koff, ATT_BLOCK)])
                p = jnp.exp(sc - l_ref[hh, pl.ds(off, ATT_BLOCK), :])
                if masked:
                    p = jnp.where(_causal(qi, ki), p, 0.0)
                dp = _dot(doh, v2, _NT)
                ds = p * (dp - dl_ref[hh, pl.ds(off, ATT_BLOCK), :])
                dv_acc = dv_acc + _dot(p, do2, _TN)
                dk_acc = dk_acc + _dot(ds, q2, _TN)
                dc_acc = dc_acc - jnp.sum(ds, axis=0, keepdims=True)
                dqs.append(_dot(ds, k2, _NN))
                out.append((dk_acc, dv_acc, dc_acc))
            dq_ref[pl.ds(off, ATT_BLOCK), :] += jnp.where(m0, dqs[0], dqs[1]) * scale
            return tuple(out)

        init = tuple((jnp.zeros((ATT_BLOCK, LANES), F32), jnp.zeros((ATT_BLOCK, LANES), F32),
                      jnp.zeros((1, ATT_BLOCK), F32)) for _ in range(2))
        carry = step(ki, init, True)
        (dka, dva, dca), (dkb, dvb, dcb) = lax.fori_loop(ki + 1, nq, lambda qi, c: step(qi, c, False), carry)
        dk_ref[...] = (jnp.where(m0, dka, dkb) * scale).astype(dk_ref.dtype)
        dv_ref[...] = jnp.where(m0, dva, dvb).astype(dv_ref.dtype)
        dc_ref[0:1, :] = dca
        dc_ref[1:2, :] = dcb

    stat = pl.BlockSpec((None, 2, s, LANES), lambda j, i: (j, 0, 0, 0))
    colfull = lambda base: pl.BlockSpec((s, LANES), lambda j, i: (0, base + j))
    colblk = lambda base: pl.BlockSpec((ATT_BLOCK, LANES), lambda j, i: (i, base + j))
    return pl.pallas_call(
        body, name=name, grid=(npair, nq),
        in_specs=[colfull(0), colblk(npair), colblk(2 * npair), colfull(0), stat, stat, stat,
                  pl.BlockSpec((None, 2, s), lambda j, i: (j, 0, 0))],
        out_specs=[colfull(0), colblk(0), colblk(0), pl.BlockSpec((None, 2, ATT_BLOCK), lambda j, i: (j, 0, i))],
        out_shape=[jax.ShapeDtypeStruct((s, FOX_WIDTH), F32), jax.ShapeDtypeStruct((s, FOX_WIDTH), BF16),
                   jax.ShapeDtypeStruct((s, FOX_WIDTH), BF16), jax.ShapeDtypeStruct((npair, 2, s), F32)],
        compiler_params=_params(),
    )(qkv, qkv, qkv, do, lse_b, delta_b, cum_b, cum_r)


ATT_TQ = 256
ATT_TK = 256
ATT_SCALE = FOX_HEAD_DIM ** -0.5
assert ATT_SCALE == 0.125 and ATT_TQ == ATT_TK


def _causal_t(qi, ki):
    kpos = lax.broadcasted_iota(jnp.int32, (ATT_TK, ATT_TQ), 0) + ki * ATT_TK
    qpos = lax.broadcasted_iota(jnp.int32, (ATT_TK, ATT_TQ), 1) + qi * ATT_TQ
    return kpos <= qpos


def _row_mask():
    return lax.broadcasted_iota(jnp.int32, (LANES, 1), 0) < FOX_HEAD_DIM


def _lane_tile(a, width):
    return a if a.shape[1] == width else jnp.tile(a, (1, width // a.shape[1]))


def _transpose_bf16(a):
    return a.astype(F32).T.astype(BF16)


def _attn_fwd_t(qkv, cum_b, cum_r, *, name):
    s = qkv.shape[0]
    nq = s // ATT_TQ
    npair = HEAD_PAIRS

    def body(q_ref, k_ref, v_ref, cq_ref, ck_ref, o_ref, ot_ref, l_ref, vt_ref):
        qi = pl.program_id(1)
        rows = _row_mask()

        @pl.when(qi == 0)
        def _():
            vt_ref[...] = _transpose_bf16(v_ref[...])

        qt = _transpose_bf16(q_ref[...]) * ATT_SCALE
        zero = jnp.zeros_like(qt)
        qts = (jnp.where(rows, qt, zero), jnp.where(rows, zero, qt))

        def step(ki, carry, masked):
            off = pl.multiple_of(ki * ATT_TK, ATT_TK)
            k2 = k_ref[pl.ds(off, ATT_TK), :]
            vt = vt_ref[:, pl.ds(off, ATT_TK)]
            out = []
            for hh in range(2):
                m, l, acc = carry[hh]
                bias = cq_ref[hh:hh + 1, :] - _lane_tile(ck_ref[hh, pl.ds(off, ATT_TK), :], ATT_TQ)
                sc = _dot(k2, qts[hh], _NN) + bias
                if masked:
                    sc = jnp.where(_causal_t(qi, ki), sc, -1e30)
                m_new = jnp.maximum(m, jnp.max(sc, axis=0, keepdims=True))
                alpha = jnp.exp(m - m_new)
                p = jnp.exp(sc - m_new)
                l = alpha * l + jnp.sum(p, axis=0, keepdims=True)
                p_hi = p.astype(BF16)
                p_lo = (p - p_hi.astype(F32)).astype(BF16)
                acc = alpha * acc + (_dot(vt, p_hi, _NN) + _dot(vt, p_lo, _NN))
                out.append((m_new, l, acc))
            return tuple(out)

        init = tuple((jnp.full((1, ATT_TQ), -1e30, F32), jnp.zeros((1, ATT_TQ), F32),
                      jnp.zeros((LANES, ATT_TQ), F32)) for _ in range(2))
        carry = lax.fori_loop(0, qi // 2, lambda kk, c: step(2 * kk + 1, step(2 * kk, c, False), False), init)
        carry = lax.cond(qi % 2 == 1, lambda c: step(qi - 1, c, False), lambda c: c, carry)
        (ma, la, acca), (mb, lb, accb) = step(qi, carry, True)
        ot = jnp.where(rows, acca / la, accb / lb)
        ot_ref[...] = ot
        o_ref[...] = ot.T.astype(o_ref.dtype)
        l_ref[0:1, :] = ma + jnp.log(la)
        l_ref[1:2, :] = mb + jnp.log(lb)

    row = pl.BlockSpec((None, 2, ATT_TQ), lambda j, i: (j, 0, i))
    return pl.pallas_call(
        body, name=name, grid=(npair, nq),
        in_specs=[pl.BlockSpec((ATT_TQ, LANES), lambda j, i: (i, j)),
                  pl.BlockSpec((s, LANES), lambda j, i: (0, npair + j)),
                  pl.BlockSpec((s, LANES), lambda j, i: (0, 2 * npair + j)),
                  row, pl.BlockSpec((None, 2, s, LANES), lambda j, i: (j, 0, 0, 0))],
        out_specs=[pl.BlockSpec((ATT_TQ, LANES), lambda j, i: (i, j)),
                   pl.BlockSpec((LANES, ATT_TQ), lambda j, i: (j, i)), row],
        out_shape=[jax.ShapeDtypeStruct((s, FOX_WIDTH), BF16), jax.ShapeDtypeStruct((FOX_WIDTH, s), F32),
                   jax.ShapeDtypeStruct((npair, 2, s), F32)],
        scratch_shapes=[pltpu.VMEM((LANES, s), BF16)],
        compiler_params=_params(),
    )(qkv, qkv, qkv, cum_r, cum_b)


def _attn_delta_t(do_t, o_t, *, name):
    s = o_t.shape[1]
    ts = _tile(s, 512)

    def body(do_ref, o_ref, d_ref):
        prod = do_ref[...].astype(F32) * o_ref[...]
        d_ref[0:1, :] = jnp.sum(prod[:FOX_HEAD_DIM], axis=0, keepdims=True)
        d_ref[1:2, :] = jnp.sum(prod[FOX_HEAD_DIM:], axis=0, keepdims=True)

    blk = pl.BlockSpec((LANES, ts), lambda j, i: (j, i))
    return pl.pallas_call(
        body, name=name, grid=(HEAD_PAIRS, s // ts), in_specs=[blk, blk],
        out_specs=pl.BlockSpec((None, 2, ts), lambda j, i: (j, 0, i)),
        out_shape=jax.ShapeDtypeStruct((HEAD_PAIRS, 2, s), F32), compiler_params=_params(),
    )(do_t, o_t)


def _attn_bwd_t(qkv, do, o_t, lse, cum_b, cum_r, *, name, dep=None):
    s = qkv.shape[0]
    nq = s // ATT_TQ
    npair = HEAD_PAIRS

    deps = [] if dep is None else [dep]

    def body(q_ref, k_ref, v_ref, do_ref, ot_ref, l_ref, cq_ref, ck_ref, *rest):
        dq_ref, dk_ref, dv_ref, dc_ref, qt_ref, dot_ref, dqt_ref, dl_ref = rest[len(deps):]
        ki = pl.program_id(1)
        m0 = _head_mask()
        rows = _row_mask()
        k2 = k_ref[...]
        v2 = v_ref[...]
        kt = _transpose_bf16(k2)
        ks = k2 * ATT_SCALE
        kz, vz = jnp.zeros_like(k2), jnp.zeros_like(v2)
        khs = (jnp.where(m0, ks, kz), jnp.where(m0, kz, ks))
        vhs = (jnp.where(m0, v2, vz), jnp.where(m0, vz, v2))
        cks = tuple(_lane_tile(ck_ref[hh], ATT_TQ) for hh in range(2))

        @pl.when(ki == 0)
        def _():
            dqt_ref[...] = jnp.zeros_like(dqt_ref)
            qt_ref[...] = _transpose_bf16(q_ref[...])
            do_t = do_ref[...].astype(F32).T
            dot_ref[...] = do_t.astype(BF16)
            prod = do_t * ot_ref[...]
            dl_ref[0:1, :] = jnp.sum(prod[:FOX_HEAD_DIM], axis=0, keepdims=True)
            dl_ref[1:2, :] = jnp.sum(prod[FOX_HEAD_DIM:], axis=0, keepdims=True)

        def step(qi, carry, masked):
            off = pl.multiple_of(qi * ATT_TQ, ATT_TQ)
            q2 = q_ref[pl.ds(off, ATT_TQ), :]
            do2 = do_ref[pl.ds(off, ATT_TQ), :]
            qt = qt_ref[:, pl.ds(off, ATT_TQ)]
            dot_ = dot_ref[:, pl.ds(off, ATT_TQ)]
            out, dqs = [], []
            for hh in range(2):
                dk_acc, dv_acc, dc_acc = carry[hh]
                sc = _dot(khs[hh], qt, _NN) + (cq_ref[hh:hh + 1, pl.ds(off, ATT_TQ)] - cks[hh])
                p = jnp.exp(sc - l_ref[hh:hh + 1, pl.ds(off, ATT_TQ)])
                if masked:
                    p = jnp.where(_causal_t(qi, ki), p, 0.0)
                dp = _dot(vhs[hh], dot_, _NN)
                ds = p * (dp - dl_ref[hh:hh + 1, pl.ds(off, ATT_TQ)])
                dc_acc = dc_acc - jnp.sum(ds, axis=1, keepdims=True)
                dss = (ds * ATT_SCALE).astype(BF16)
                dv_acc = dv_acc + _dot(p, do2, _NN)
                dk_acc = dk_acc + _dot(dss, q2, _NN)
                dqs.append(_dot(kt, dss, _NN))
                out.append((dk_acc, dv_acc, dc_acc))
            dqt_ref[:, pl.ds(off, ATT_TQ)] += jnp.where(rows, dqs[0], dqs[1])
            return tuple(out)

        init = tuple((jnp.zeros((ATT_TK, LANES), F32), jnp.zeros((ATT_TK, LANES), F32),
                      jnp.zeros((ATT_TK, 1), F32)) for _ in range(2))
        carry = step(ki, init, True)
        rest = nq - 1 - ki
        carry = lax.fori_loop(
            0, rest // 2, lambda t, c: step(ki + 2 + 2 * t, step(ki + 1 + 2 * t, c, False), False), carry)
        carry = lax.cond(rest % 2 == 1, lambda c: step(nq - 1, c, False), lambda c: c, carry)
        (dka, dva, dca), (dkb, dvb, dcb) = carry
        dk_ref[...] = jnp.where(m0, dka, dkb).astype(dk_ref.dtype)
        dv_ref[...] = jnp.where(m0, dva, dvb).astype(dv_ref.dtype)
        dc_ref[0] = jnp.broadcast_to(dca, (ATT_TK, LANES))
        dc_ref[1] = jnp.broadcast_to(dcb, (ATT_TK, LANES))

        @pl.when(ki == nq - 1)
        def _():
            dq_ref[...] = dqt_ref[...].T.astype(dq_ref.dtype)

    colfull = lambda base: pl.BlockSpec((s, LANES), lambda j, i: (0, base + j))
    colblk = lambda base: pl.BlockSpec((ATT_TK, LANES), lambda j, i: (i, base + j))
    stat = pl.BlockSpec((None, 2, s), lambda j, i: (j, 0, 0))
    bcast = pl.BlockSpec((None, 2, ATT_TK, LANES), lambda j, i: (j, 0, i, 0))
    grad = jax.ShapeDtypeStruct((s, FOX_WIDTH), BF16)
    return pl.pallas_call(
        body, name=name, grid=(npair, nq),
        in_specs=[colfull(0), colblk(npair), colblk(2 * npair), colfull(0),
                  pl.BlockSpec((LANES, s), lambda j, i: (j, 0)), stat, stat, bcast]
                 + [pl.BlockSpec(memory_space=pl.ANY)] * len(deps),
        out_specs=[colfull(0), colblk(0), colblk(0), bcast],
        out_shape=[grad, grad, grad, jax.ShapeDtypeStruct((npair, 2, s, LANES), F32)],
        scratch_shapes=[pltpu.VMEM((LANES, s), BF16), pltpu.VMEM((LANES, s), BF16), pltpu.VMEM((LANES, s), F32),
                        pltpu.VMEM((2, s), F32)],
        compiler_params=_params(),
    )(qkv, qkv, qkv, do, o_t, lse, cum_r, cum_b, *deps)


def _merge_fwd(zg, ya, yb, *, name, tm=256):
    s, d = ya.shape
    tm = _tile(s, tm)

    def body(zg_ref, ya_ref, yb_ref, m_ref):
        ga = _sigmoid(zg_ref[:, :d].astype(F32))
        gb = _sigmoid(zg_ref[:, d:].astype(F32))
        m_ref[...] = (ga * ya_ref[...].astype(F32) + gb * yb_ref[...].astype(F32)).astype(m_ref.dtype)

    row = pl.BlockSpec((tm, d), lambda i: (i, 0))
    row2 = pl.BlockSpec((tm, 2 * d), lambda i: (i, 0))
    return pl.pallas_call(
        body, name=name, grid=(s // tm,), in_specs=[row2, row, row], out_specs=row,
        out_shape=jax.ShapeDtypeStruct((s, d), BF16), compiler_params=_params(),
    )(zg, ya, yb)


def _merge_bwd(dm, zg, ya, yb, *, name, tm=256):
    s, d = ya.shape
    tm = _tile(s, tm)

    def body(dm_ref, zg_ref, ya_ref, yb_ref, dzg_ref, dya_ref, dyb_ref):
        dmv = dm_ref[...].astype(F32)
        ga = _sigmoid(zg_ref[:, :d].astype(F32))
        gb = _sigmoid(zg_ref[:, d:].astype(F32))
        dzg_ref[:, :d] = (dmv * ya_ref[...].astype(F32) * ga * (1.0 - ga)).astype(dzg_ref.dtype)
        dzg_ref[:, d:] = (dmv * yb_ref[...].astype(F32) * gb * (1.0 - gb)).astype(dzg_ref.dtype)
        dya_ref[...] = (dmv * ga).astype(dya_ref.dtype)
        dyb_ref[...] = (dmv * gb).astype(dyb_ref.dtype)

    row = pl.BlockSpec((tm, d), lambda i: (i, 0))
    row2 = pl.BlockSpec((tm, 2 * d), lambda i: (i, 0))
    return pl.pallas_call(
        body, name=name, grid=(s // tm,), in_specs=[row, row2, row, row], out_specs=[row2, row, row],
        out_shape=[jax.ShapeDtypeStruct((s, 2 * d), BF16), jax.ShapeDtypeStruct((s, d), BF16),
                   jax.ShapeDtypeStruct((s, d), BF16)],
        compiler_params=_params(),
    )(dm, zg, ya, yb)


SUBLANES = 8


def _shift_down(u, k, row):
    rolled = pltpu.roll(u, k, 0)
    head = jnp.where(row[:SUBLANES] >= k, rolled[:SUBLANES], 0.0)
    return jnp.concatenate([head, rolled[SUBLANES:]], axis=0)


def _shift_up(u, k, row):
    n = u.shape[0]
    rolled = pltpu.roll(u, n - k, 0)
    tail = jnp.where(row[n - SUBLANES:] < n - k, rolled[n - SUBLANES:], 0.0)
    return jnp.concatenate([rolled[:n - SUBLANES], tail], axis=0)


def _conv_act_fwd(up_a, up_b, cw_a, cw_b, cb_a, cb_b, *, name, tc=128):
    s, f = up_a.shape
    tc = _tile(f, tc)

    def body(ua_ref, ub_ref, wa_ref, wb_ref, ba_ref, bb_ref, act_ref):
        row = lax.broadcasted_iota(jnp.int32, (s, tc), 0)

        def conv(u_ref, w_ref, b_ref):
            u = u_ref[...].astype(F32)
            return (b_ref[...] + w_ref[0:1, :] * _shift_down(u, 2, row)
                    + w_ref[1:2, :] * _shift_down(u, 1, row) + w_ref[2:3, :] * u)

        ca = conv(ua_ref, wa_ref, ba_ref)
        cb = conv(ub_ref, wb_ref, bb_ref)
        act_ref[...] = (_gelu(ca) * cb).astype(act_ref.dtype)

    col = pl.BlockSpec((s, tc), lambda j: (0, j))
    w3 = pl.BlockSpec((3, tc), lambda j: (0, j))
    b1 = pl.BlockSpec((1, tc), lambda j: (0, j))
    return pl.pallas_call(
        body, name=name, grid=(f // tc,), in_specs=[col, col, w3, w3, b1, b1], out_specs=col,
        out_shape=jax.ShapeDtypeStruct((s, f), BF16), compiler_params=_params(),
    )(up_a, up_b, cw_a, cw_b, cb_a, cb_b)


def _conv_act_bwd(up_a, up_b, dact, cw_a, cw_b, cb_a, cb_b, *, name, tc=128):
    s, f = up_a.shape
    tc = _tile(f, tc)

    def body(ua_ref, ub_ref, da_ref, wa_ref, wb_ref, ba_ref, bb_ref, dua_ref, dub_ref, dwa_ref, dwb_ref):
        row = lax.broadcasted_iota(jnp.int32, (s, tc), 0)

        def conv(u_ref, w_ref, b_ref):
            u = u_ref[...].astype(F32)
            u1 = _shift_down(u, 1, row)
            u2 = _shift_down(u, 2, row)
            return u, u1, u2, b_ref[...] + w_ref[0:1, :] * u2 + w_ref[1:2, :] * u1 + w_ref[2:3, :] * u

        def back(dc, taps, w_ref, du_ref, dw_ref):
            u, u1, u2 = taps
            dw_ref[0:1, :] = jnp.sum(dc * u2, axis=0, keepdims=True)
            dw_ref[1:2, :] = jnp.sum(dc * u1, axis=0, keepdims=True)
            dw_ref[2:3, :] = jnp.sum(dc * u, axis=0, keepdims=True)
            dw_ref[3:4, :] = jnp.sum(dc, axis=0, keepdims=True)
            du = (w_ref[2:3, :] * dc + w_ref[1:2, :] * _shift_up(dc, 1, row)
                  + w_ref[0:1, :] * _shift_up(dc, 2, row))
            du_ref[...] = du.astype(du_ref.dtype)

        ua, ua1, ua2, ca = conv(ua_ref, wa_ref, ba_ref)
        ub, ub1, ub2, cb = conv(ub_ref, wb_ref, bb_ref)
        g, dg = _gelu_and_grad(ca)
        dact_v = da_ref[...].astype(F32)
        back(dact_v * cb * dg, (ua, ua1, ua2), wa_ref, dua_ref, dwa_ref)
        back(dact_v * g, (ub, ub1, ub2), wb_ref, dub_ref, dwb_ref)

    col = pl.BlockSpec((s, tc), lambda j: (0, j))
    w3 = pl.BlockSpec((3, tc), lambda j: (0, j))
    w4 = pl.BlockSpec((4, tc), lambda j: (0, j))
    b1 = pl.BlockSpec((1, tc), lambda j: (0, j))
    return pl.pallas_call(
        body, name=name, grid=(f // tc,), in_specs=[col, col, col, w3, w3, b1, b1],
        out_specs=[col, col, w4, w4],
        out_shape=[jax.ShapeDtypeStruct((s, f), BF16), jax.ShapeDtypeStruct((s, f), BF16),
                   jax.ShapeDtypeStruct((4, f), F32), jax.ShapeDtypeStruct((4, f), F32)],
        compiler_params=_params(),
    )(up_a, up_b, dact, cw_a, cw_b, cb_a, cb_b)


def _ple_final(x2, ple, zp, target, g_final, *, name, tm=256):
    s, d = x2.shape
    tm = _tile(s, tm)

    def body(x_ref, ple_ref, zp_ref, t_ref, g_ref, dx_ref, dple_ref, dzp_ref, dg_ref, loss_ref):
        @pl.when(pl.program_id(0) == 0)
        def _():
            dg_ref[...] = jnp.zeros_like(dg_ref)
            loss_ref[...] = jnp.zeros_like(loss_ref)

        gp = _sigmoid(zp_ref[...].astype(F32))
        plev = ple_ref[...].astype(F32)
        x3 = x_ref[...] + plev * gp
        r = lax.rsqrt(jnp.mean(x3 * x3, axis=-1, keepdims=True) + EPS)
        xhat = x3 * r
        gv = g_ref[...]
        diff = xhat * gv - t_ref[...]
        loss_ref[...] += 0.5 * jnp.sum(jnp.mean(diff * diff, axis=-1, keepdims=True), axis=0, keepdims=True)
        dy = diff * (1.0 / d)
        dg_ref[...] += jnp.sum(dy * xhat, axis=0, keepdims=True)
        dyg = dy * gv
        dx3 = r * (dyg - xhat * jnp.mean(dyg * xhat, axis=-1, keepdims=True))
        dx_ref[...] = dx3
        dple_ref[...] = (dx3 * gp).astype(dple_ref.dtype)
        dzp_ref[...] = (dx3 * plev * gp * (1.0 - gp)).astype(dzp_ref.dtype)

    row = pl.BlockSpec((tm, d), lambda i: (i, 0))
    vec = pl.BlockSpec((1, d), lambda i: (0, 0))
    return pl.pallas_call(
        body, name=name, grid=(s // tm,), in_specs=[row, row, row, row, vec],
        out_specs=[row, row, row, vec, pl.BlockSpec((1, LANES), lambda i: (0, 0))],
        out_shape=[jax.ShapeDtypeStruct((s, d), F32), jax.ShapeDtypeStruct((s, d), BF16),
                   jax.ShapeDtypeStruct((s, d), BF16), jax.ShapeDtypeStruct((1, d), F32),
                   jax.ShapeDtypeStruct((1, LANES), F32)],
        compiler_params=_params(),
    )(x2, ple, zp, target, g_final)


def _device_step(x, p, target, w, get_w_in=None, get_w_rest=None, on_grads_ffn=None, on_grads_small=None,
                 on_grads_mix=None):
    s = x.shape[0]
    g = {}
    w = dict(w)

    h = _rms_fwd(x, w["norm_mix_g"], name="rms_mix", dep=w.get("first_dep"))
    if get_w_in is not None:
        w.update(get_w_in(h))
    z_uv = _mm(h, w["w_uv"], mode="nn", out_dtype=BF16, name="proj_uv", tm=1024, dep=w.get("proj_dep"))
    qkv = _mm(h, w["w_qkv"], mode="nn", out_dtype=BF16, name="proj_qkv", tm=1024)
    zg = _mm(h, w["w_g"], mode="nn", out_dtype=BF16, name="proj_gate", tm=1024)
    f = _mm(h, w["w_f"], mode="nn", out_dtype=F32, name="proj_f", tm=1024)

    a = _gmlp_fwd(z_uv, w["gmlp_ln_g"], w["gmlp_ln_b"], w["gmlp_w_s"], w["gmlp_b_s_t"], name="gmlp_fwd")

    cum_b, cum_t = _fox_cum(f, w["b_f"], name="fox_cum")
    cum_b = cum_b.reshape(HEAD_PAIRS, 2, s, LANES)
    cum_r = cum_t[:FOX_HEADS].reshape(HEAD_PAIRS, 2, s)
    b, o_t, lse = _attn_fwd_t(qkv, cum_b, cum_r, name="attn_fwd")
    if get_w_rest is not None:
        w.update(get_w_rest(b))

    ya = _mm(a, w["w_branch_a"], mode="nn", out_dtype=BF16, name="branch_a", tm=1024)
    yb = _mm(b, w["w_branch_b"], mode="nn", out_dtype=BF16, name="branch_b", tm=1024)
    merged = _merge_fwd(zg, ya, yb, name="merge_fwd")
    x1 = _mm(merged, w["w_out"], mode="nn", out_dtype=F32, name="proj_out", add=x, tm=1024)

    h2 = _rms_fwd(x1, w["norm_ffn_g"], name="rms_ffn")
    up_a = _mm(h2, w["w_up_a"], mode="nn", out_dtype=BF16, name="up_a", tm=1024, tn=D_FF // 2)
    up_b = _mm(h2, w["w_up_b"], mode="nn", out_dtype=BF16, name="up_b", tm=1024, tn=D_FF // 2)
    cw, cb = w["conv_w"], w["conv_b"]
    conv_args = (cw[:, :D_FF], cw[:, D_FF:], cb[:, :D_FF], cb[:, D_FF:])
    act = _conv_act_fwd(up_a, up_b, *conv_args, name="conv_act_fwd")
    x2 = _mm(act, w["w_down"], mode="nn", out_dtype=F32, name="down", add=x1, tm=512)

    h3 = _rms_fwd(x2, w["norm_ple_g"], name="rms_ple")
    ple = _mm(p, w["w_ple"], mode="nn", out_dtype=BF16, name="ple_proj", tm=1024)
    zp = _mm(h3, w["w_ple_gate"], mode="nn", out_dtype=BF16, name="ple_gate", tm=1024)
    dx3, dple, dzp, g["norm_final_g"], loss = _ple_final(x2, ple, zp, target, w["norm_final_g"], name="ple_final")

    g["w_ple"] = _mm(p, dple, mode="tn", out_dtype=BF16, name="dw_ple")
    g["w_ple_gate"] = _mm(h3, dzp, mode="tn", out_dtype=BF16, name="dw_ple_gate")
    dh3 = _mm(dzp, w["w_ple_gate"], mode="nt", out_dtype=BF16, name="dh3")
    dx2, dx2_b, g["norm_ple_g"] = _rms_bwd(x2, w["norm_ple_g"], dh3, dx3, name="rms_ple_bwd")

    g["w_down"] = _mm(act, dx2_b, mode="tn", out_dtype=BF16, name="dw_down", tm=D_FF // 2)
    dact = _mm(dx2_b, w["w_down"], mode="nt", out_dtype=BF16, name="dact", tn=D_FF // 2)
    dup_a, dup_b, dcw_a, dcw_b = _conv_act_bwd(up_a, up_b, dact, *conv_args, name="conv_act_bwd")
    g["conv_w"] = jnp.concatenate([dcw_a[:3], dcw_b[:3]], axis=1)
    g["conv_b"] = jnp.concatenate([dcw_a[3:], dcw_b[3:]], axis=1)
    g["w_up_a"] = _mm(h2, dup_a, mode="tn", out_dtype=BF16, name="dw_up_a", tn=D_FF // 2)
    g["w_up_b"] = _mm(h2, dup_b, mode="tn", out_dtype=BF16, name="dw_up_b", tn=D_FF // 2)
    dh2 = _mm_nt_sum([(dup_a, w["w_up_a"]), (dup_b, w["w_up_b"])], out_dtype=BF16, name="dh2")
    dx1, dx1_b, g["norm_ffn_g"] = _rms_bwd(x1, w["norm_ffn_g"], dh2, dx2, name="rms_ffn_bwd")

    g["w_out"] = _mm(merged, dx1_b, mode="tn", out_dtype=BF16, name="dw_out")
    dmerged = _mm(dx1_b, w["w_out"], mode="nt", out_dtype=BF16, name="dmerged")
    dzg, dya, dyb = _merge_bwd(dmerged, zg, ya, yb, name="merge_bwd")
    g["w_branch_a"] = _mm(a, dya, mode="tn", out_dtype=BF16, name="dw_branch_a")
    g["w_branch_b"] = _mm(b, dyb, mode="tn", out_dtype=BF16, name="dw_branch_b")
    dep = on_grads_ffn(g) if on_grads_ffn is not None else None
    da = _mm(dya, w["w_branch_a"], mode="nt", out_dtype=BF16, name="da", dep=dep)
    db = _mm(dyb, w["w_branch_b"], mode="nt", out_dtype=BF16, name="db")

    dz_uv, g["gmlp_w_s"], dbs_t, g["gmlp_ln_g"], g["gmlp_ln_b"] = _gmlp_bwd(
        z_uv, da, w["gmlp_ln_g"], w["gmlp_ln_b"], w["gmlp_w_s"], w["gmlp_b_s_t"], name="gmlp_bwd")
    g["gmlp_b_s"] = dbs_t[:, :GMLP_GROUPS].T
    dep = on_grads_small(g) if on_grads_small is not None else None

    dq, dk, dv, dcum_b = _attn_bwd_t(qkv, db, o_t, lse, cum_b, cum_r, name="attn_bwd", dep=dep)
    dcum_t = jnp.pad(dcum_b[..., 0].reshape(FOX_HEADS, s), ((0, LANES - FOX_HEADS), (0, 0)))
    df, g["b_f"] = _fox_dlogit(dcum_t, f, w["b_f"], name="fox_dlogit")
    dqkv = jnp.concatenate([dq, dk, dv], axis=1)

    g["w_uv"] = _mm(h, dz_uv, mode="tn", out_dtype=BF16, name="dw_uv")
    g["w_qkv"] = _mm(h, dqkv, mode="tn", out_dtype=BF16, name="dw_qkv")
    g["w_f"] = _mm(h, df, mode="tn", out_dtype=BF16, name="dw_f")
    g["w_g"] = _mm(h, dzg, mode="tn", out_dtype=BF16, name="dw_g")
    dep = on_grads_mix(g) if on_grads_mix is not None else None
    dh = _mm_nt_sum([(dz_uv, w["w_uv"]), (dqkv, w["w_qkv"]), (df, w["w_f"]), (dzg, w["w_g"])],
                    out_dtype=BF16, name="dh", dep=dep)
    dx0, _, g["norm_mix_g"] = _rms_bwd(x, w["norm_mix_g"], dh, dx1, name="rms_mix_bwd")
    return loss, dx0, g


def _coords():
    return lax.axis_index("x"), lax.axis_index("y"), lax.axis_index("c")


def _other_chips(x, y):
    return [(1 - x, y), (x, 1 - y), (1 - x, 1 - y)]


def _remote(src, dst, send_sem, recv_sem, dev):
    return pltpu.make_async_remote_copy(src_ref=src, dst_ref=dst, send_sem=send_sem, recv_sem=recv_sem,
                                        device_id=dev, device_id_type=MESH)


_ANY = pl.BlockSpec(memory_space=pl.ANY)


def _gather_weights(halved, whole, *, name):
    nh, n = len(halved), len(halved) + len(whole)
    arrays = list(halved) + list(whole)

    def body(*refs):
        ins, outs = refs[:n], refs[n:2 * n]
        send_sems, recv_sems = refs[2 * n:]
        x, y, c = _coords()
        me, sib = 2 * x + y, (x, y, 1 - c)
        chips = _other_chips(x, y)

        def half(i, which):
            h = ins[i].shape[0] // 2
            return pl.ds(pl.multiple_of(which * h, 16), h)

        sends = []
        for i in range(n):
            src, dst = (ins[i].at[half(i, c)], outs[i].at[me, half(i, c)]) if i < nh else (ins[i], outs[i].at[me])
            for k, (cx, cy) in enumerate(chips):
                cp = _remote(src, dst, send_sems.at[i, k], recv_sems.at[i, k], (cx, cy, c))
                cp.start()
                sends.append(cp)
        for i in range(n):
            for k, (cx, cy) in enumerate(chips):
                got = outs[i].at[2 * cx + cy, half(i, c)] if i < nh else outs[i].at[2 * cx + cy]
                _remote(got, got, send_sems.at[i, k], recv_sems.at[i, k], sib).wait_recv()
                if i < nh:
                    cp = _remote(got, got, send_sems.at[i, 3 + k], recv_sems.at[i, 3 + k], sib)
                    cp.start()
                    sends.append(cp)
        for i in range(nh):
            for k, (cx, cy) in enumerate(chips):
                got = outs[i].at[2 * cx + cy, half(i, 1 - c)]
                _remote(got, got, send_sems.at[i, 3 + k], recv_sems.at[i, 3 + k], sib).wait_recv()
        for cp in sends:
            cp.wait_send()

    outs = pl.pallas_call(
        body, name=name, in_specs=[_ANY] * n, out_specs=[_ANY] * n,
        out_shape=[jax.ShapeDtypeStruct((N_CHIPS,) + a.shape, a.dtype) for a in arrays],
        scratch_shapes=[pltpu.SemaphoreType.DMA((n, 6)), pltpu.SemaphoreType.DMA((n, 6))],
        compiler_params=_params(),
    )(*arrays)
    chip = 2 * lax.axis_index("x") + lax.axis_index("y")
    return [lax.dynamic_update_index_in_dim(o, a, chip, 0) for o, a in zip(outs, arrays)]


def _pair_exchange(gs, *, name):
    n = len(gs)

    def body(*refs):
        ins, outs = refs[:n], refs[n:2 * n]
        send_sems, recv_sems = refs[2 * n:]
        x, y, c = _coords()
        copies = []
        for i in range(n):
            for j in range(N_CHIPS):
                cp = _remote(ins[i].at[j, 1 - c], outs[i].at[j], send_sems.at[i, j], recv_sems.at[i, j], (x, y, 1 - c))
                cp.start()
                copies.append(cp)
        for cp in copies:
            cp.wait()

    return pl.pallas_call(
        body, name=name, in_specs=[_ANY] * n, out_specs=[_ANY] * n,
        out_shape=[jax.ShapeDtypeStruct((N_CHIPS,) + a.shape[2:], a.dtype) for a in gs],
        scratch_shapes=[pltpu.SemaphoreType.DMA((n, N_CHIPS)), pltpu.SemaphoreType.DMA((n, N_CHIPS))],
        compiler_params=_params(),
    )(*gs)


def _chip_exchange(ss, *, name):
    n = len(ss)

    def body(*refs):
        ins, outs = refs[:n], refs[n:2 * n]
        send_sems, recv_sems = refs[2 * n:]
        x, y, c = _coords()
        me = 2 * x + y
        chips = _other_chips(x, y)
        sends = []
        for i in range(n):
            for k, (cx, cy) in enumerate(chips):
                cp = _remote(ins[i].at[2 * cx + cy], outs[i].at[me], send_sems.at[i, k], recv_sems.at[i, k], (cx, cy, c))
                cp.start()
                sends.append(cp)
        for i in range(n):
            for k, (cx, cy) in enumerate(chips):
                got = outs[i].at[2 * cx + cy]
                _remote(got, got, send_sems.at[i, k], recv_sems.at[i, k], (cx, cy, c)).wait_recv()
        for cp in sends:
            cp.wait_send()

    return pl.pallas_call(
        body, name=name, in_specs=[_ANY] * n, out_specs=[_ANY] * n,
        out_shape=[jax.ShapeDtypeStruct(a.shape, a.dtype) for a in ss],
        scratch_shapes=[pltpu.SemaphoreType.DMA((n, 3)), pltpu.SemaphoreType.DMA((n, 3))],
        compiler_params=_params(),
    )(*ss)


def _pair_share(hs, *, name):
    n = len(hs)

    def body(*refs):
        ins, outs = refs[:n], refs[n:2 * n]
        send_sems, recv_sems = refs[2 * n:]
        x, y, c = _coords()
        copies = []
        for i in range(n):
            cp = _remote(ins[i], outs[i], send_sems.at[i], recv_sems.at[i], (x, y, 1 - c))
            cp.start()
            copies.append(cp)
        for cp in copies:
            cp.wait()

    return pl.pallas_call(
        body, name=name, in_specs=[_ANY] * n, out_specs=[_ANY] * n,
        out_shape=[jax.ShapeDtypeStruct(a.shape, a.dtype) for a in hs],
        scratch_shapes=[pltpu.SemaphoreType.DMA((n,)), pltpu.SemaphoreType.DMA((n,))],
        compiler_params=_params(),
    )(*hs)


def _all_exchange(vec, *, name):
    def body(v_ref, o_ref, send_sems, recv_sems, local_sem):
        x, y, c = _coords()
        me = 4 * x + 2 * y + c
        local = pltpu.make_async_copy(v_ref, o_ref.at[me], local_sem)
        local.start()
        copies = []
        k = 0
        for dx in (0, 1):
            for dy in (0, 1):
                for dc in (0, 1):
                    if dx or dy or dc:
                        peer = (1 - x if dx else x, 1 - y if dy else y, 1 - c if dc else c)
                        cp = _remote(v_ref, o_ref.at[me], send_sems.at[k], recv_sems.at[k], peer)
                        cp.start()
                        copies.append(cp)
                        k += 1
        for cp in copies:
            cp.wait()
        local.wait()

    return pl.pallas_call(
        body, name=name, in_specs=[_ANY], out_specs=_ANY,
        out_shape=jax.ShapeDtypeStruct((8,) + vec.shape, vec.dtype),
        scratch_shapes=[pltpu.SemaphoreType.DMA((7,)), pltpu.SemaphoreType.DMA((7,)), pltpu.SemaphoreType.DMA(())],
        compiler_params=_params(),
    )(vec)


_HBM = pl.BlockSpec(memory_space=pltpu.HBM)
_SEM = pl.BlockSpec(memory_space=pltpu.SEMAPHORE)
_EFFECT = pltpu.SideEffectType.DATAFLOW_SIDE_EFFECTING


def _copies_start(srcs, lands, plan, n_copies, *, name, after=()):
    ns, n = len(srcs), len(srcs) + len(lands)
    na = len(after)

    def body(*refs):
        send_sems, recv_sems = refs[n + na], refs[n + na + 1]
        token = refs[-1]
        for k, (src, dst, dev) in enumerate(plan(refs[:ns], refs[ns:n])):
            _remote(src, dst, send_sems.at[k], recv_sems.at[k], dev).start()
        token[...] = jnp.zeros_like(token)

    arrays = list(srcs) + list(lands)
    outs = pl.pallas_call(
        body, name=name,
        out_shape=(pltpu.SemaphoreType.DMA((n_copies,)), pltpu.SemaphoreType.DMA((n_copies,)),
                   *[pltpu.HBM(a.shape, a.dtype) for a in arrays], jax.ShapeDtypeStruct((8, LANES), F32)),
        in_specs=[_HBM] * n + [_ANY] * na,
        out_specs=(_SEM, _SEM, *[_HBM] * n, pl.BlockSpec(memory_space=pltpu.VMEM)),
        input_output_aliases={i: 2 + i for i in range(n)},
        compiler_params=pltpu.CompilerParams(has_side_effects=_EFFECT),
    )(*[pltpu.with_memory_space_constraint(a, pltpu.HBM) for a in arrays], *after)
    return outs[0], outs[1], list(outs[2:2 + ns]), list(outs[2 + ns:2 + n]), outs[-1]


def _copies_wait(send_sems, recv_sems, srcs, lands, plan, first, after, *, name):
    ns, n = len(srcs), len(srcs) + len(lands)

    def body(*refs):
        send, recv = refs[n], refs[n + 1]
        for k, (src, dst, dev) in enumerate(plan(refs[:ns], refs[ns:n])):
            cp = _remote(src, dst, send.at[first + k], recv.at[first + k], dev)
            cp.wait_send()
            cp.wait_recv()

    arrays = list(srcs) + list(lands)
    outs = pl.pallas_call(
        body, name=name, out_shape=tuple(pltpu.HBM(a.shape, a.dtype) for a in arrays),
        in_specs=[_HBM] * n + [_SEM, _SEM] + [_ANY] * len(after), out_specs=tuple([_HBM] * n),
        input_output_aliases={i: i for i in range(n)},
        compiler_params=pltpu.CompilerParams(has_side_effects=_EFFECT),
    )(*arrays, send_sems, recv_sems, *after)
    return list(outs[:ns]), list(outs[ns:])


def _gather_plan(halved):
    def plan(srcs, lands):
        x, y, c = _coords()
        me = 2 * x + y
        out = []
        for i, (src, land) in enumerate(zip(srcs, lands)):
            if halved[i]:
                h = src.shape[0] // 2
                rows = pl.ds(pl.multiple_of(c * h, 16), h)
                src, dst = src.at[rows], land.at[me, rows]
            else:
                dst = land.at[me]
            out += [(src, dst, (cx, cy, c)) for cx, cy in _other_chips(x, y)]
        return out
    return plan


def _forward_halves(lands, *, name):
    n = len(lands)

    def body(*refs):
        ins, outs = refs[:n], refs[n:2 * n]
        send_sems, recv_sems = refs[2 * n:]
        x, y, c = _coords()
        copies = []
        for i in range(n):
            h = ins[i].shape[1] // 2
            rows = pl.ds(pl.multiple_of(c * h, 16), h)
            for k, (cx, cy) in enumerate(_other_chips(x, y)):
                cp = _remote(ins[i].at[2 * cx + cy, rows], outs[i].at[2 * cx + cy, rows],
                             send_sems.at[i, k], recv_sems.at[i, k], (x, y, 1 - c))
                cp.start()
                copies.append(cp)
        for cp in copies:
            cp.wait()

    return pl.pallas_call(
        body, name=name, in_specs=[_ANY] * n, out_specs=[_ANY] * n,
        out_shape=[jax.ShapeDtypeStruct(a.shape, a.dtype) for a in lands],
        input_output_aliases={i: i for i in range(n)},
        scratch_shapes=[pltpu.SemaphoreType.DMA((n, 3)), pltpu.SemaphoreType.DMA((n, 3))],
        compiler_params=_params(),
    )(*lands)


def _all_plan(srcs, lands):
    x, y, c = _coords()
    me = 4 * x + 2 * y + c
    out = []
    for src, land in zip(srcs, lands):
        for dx in (0, 1):
            for dy in (0, 1):
                for dc in (0, 1):
                    if dx or dy or dc:
                        out.append((src, land.at[me], (1 - x if dx else x, 1 - y if dy else y, 1 - c if dc else c)))
    return out


def _chip_plan(srcs, lands):
    x, y, c = _coords()
    me = 2 * x + y
    out = []
    for src, land in zip(srcs, lands):
        out += [(src.at[2 * cx + cy], land.at[me], (cx, cy, c)) for cx, cy in _other_chips(x, y)]
    return out


ROW_BLOCK_BYTES = 2 * 1024 * 1024


def _rtile(r, pref, mult, row_bytes=None):
    if row_bytes is not None:
        pref = max(pref, ROW_BLOCK_BYTES // row_bytes)
    t = (min(r, pref) // mult) * mult
    while t >= mult:
        if r % t == 0:
            return t
        t -= mult
    return r


def _pair_add(g, recv, core, *, name):
    _, _, r2, cols = g.shape
    tr = _rtile(r2, 256, 16, row_bytes=2 * cols)

    def body(c_ref, g_ref, r_ref, o_ref):
        o_ref[...] = (g_ref[...].astype(F32) + r_ref[...].astype(F32)).astype(o_ref.dtype)

    blk = pl.BlockSpec((None, tr, cols), lambda j, i, c_ref: (j, i, 0))
    return pl.pallas_call(
        body, name=name,
        grid_spec=pltpu.PrefetchScalarGridSpec(
            num_scalar_prefetch=1, grid=(N_CHIPS, r2 // tr),
            in_specs=[pl.BlockSpec((None, None, tr, cols), lambda j, i, c_ref: (j, c_ref[0], i, 0)), blk],
            out_specs=blk),
        out_shape=jax.ShapeDtypeStruct(recv.shape, recv.dtype), compiler_params=_params(),
    )(core, g, recv)


def _sum_slots(a, out_dtype, *, name):
    n, r, cols = a.shape
    whole = n * r * cols * a.dtype.itemsize <= 4 * ROW_BLOCK_BYTES
    tr = r if whole else _rtile(r, 256, 16)

    def body(a_ref, o_ref):
        acc = a_ref[0].astype(F32)
        for j in range(1, n):
            acc = acc + a_ref[j].astype(F32)
        o_ref[...] = acc.astype(o_ref.dtype)

    return pl.pallas_call(
        body, name=name, grid=(r // tr,),
        in_specs=[pl.BlockSpec((n, tr, cols), lambda i: (0, i, 0))],
        out_specs=pl.BlockSpec((tr, cols), lambda i: (i, 0)),
        out_shape=jax.ShapeDtypeStruct((r, cols), out_dtype), compiler_params=_params(),
    )(a)


def _chip_sum(own, recv, chip, *, name):
    _, r2, cols = own.shape
    tr = _rtile(r2, 256, 16, row_bytes=2 * cols)

    def body(chip_ref, own_ref, *rest):
        o_ref = rest[-1]
        acc = None
        for j in range(N_CHIPS):
            term = jnp.where(chip_ref[0] == j, own_ref[...], rest[j][...]).astype(F32)
            acc = term if acc is None else acc + term
        o_ref[...] = acc

    def slot(j):
        return pl.BlockSpec((None, tr, cols),
                            lambda i, chip_ref: (jnp.where(chip_ref[0] == j, (j + 1) % N_CHIPS, j), i, 0))

    return pl.pallas_call(
        body, name=name,
        grid_spec=pltpu.PrefetchScalarGridSpec(
            num_scalar_prefetch=1, grid=(r2 // tr,),
            in_specs=[pl.BlockSpec((None, tr, cols), lambda i, chip_ref: (chip_ref[0], i, 0))]
                     + [slot(j) for j in range(N_CHIPS)],
            out_specs=pl.BlockSpec((tr, cols), lambda i, chip_ref: (i, 0))),
        out_shape=jax.ShapeDtypeStruct((r2, cols), F32), compiler_params=_params(),
    )(chip, own, *([recv] * N_CHIPS))


def _adam_update(w, gv, m, v):
    c1 = 1.0 / (1.0 - ADAM_B1 ** ADAM_STEP)
    c2 = 1.0 / (1.0 - ADAM_B2 ** ADAM_STEP)
    nm = ADAM_B1 * m + (1.0 - ADAM_B1) * gv
    nv = ADAM_B2 * v + (1.0 - ADAM_B2) * gv * gv
    return -ADAM_LR * ((nm * c1) / (jnp.sqrt(nv * c2) + ADAM_EPS) + ADAM_WD * w), nm, nv


def _adamw_halves(w, g_mine, g_other, m, v, core, *, name):
    r, cols = w.shape
    r2 = r // 2
    tr = _rtile(r2, 256, 8, row_bytes=4 * cols)
    nt = r2 // tr

    def body(core_ref, w_ref, gm_ref, go_ref, m_ref, v_ref, g_ref, d_ref, nm_ref, nv_ref):
        gv = jnp.where(pl.program_id(0) == core_ref[0], gm_ref[...], go_ref[...])
        g_ref[...] = gv
        d_ref[...], nm_ref[...], nv_ref[...] = _adam_update(w_ref[...], gv, m_ref[...], v_ref[...])

    full = pl.BlockSpec((tr, cols), lambda hf, i, core_ref: (hf * nt + i, 0))
    half = pl.BlockSpec((tr, cols), lambda hf, i, core_ref: (i, 0))
    shape = jax.ShapeDtypeStruct((r, cols), F32)
    return pl.pallas_call(
        body, name=name,
        grid_spec=pltpu.PrefetchScalarGridSpec(
            num_scalar_prefetch=1, grid=(2, nt), in_specs=[full, half, half, full, full], out_specs=[full] * 4),
        out_shape=[shape] * 4, compiler_params=_params(),
    )(core, w, g_mine, g_other, m, v)


def _adamw_split_rows(w, g_mine, g_other, m, v, core, *, name, tc=256):
    r, cols = w.shape
    r2 = g_mine.shape[0]
    tc = _tile(cols, tc)

    def body(core_ref, w_ref, gm_ref, go_ref, m_ref, v_ref, g_ref, d_ref, nm_ref, nv_ref):
        mine_first = core_ref[0] == 0
        for lo, hi, first in ((0, r2, True), (r2, r, False)):
            n = hi - lo
            gm, go = gm_ref[0:n, :], go_ref[0:n, :]
            gv = jnp.where(mine_first, gm, go) if first else jnp.where(mine_first, go, gm)
            g_ref[lo:hi, :] = gv
            d_ref[lo:hi, :], nm_ref[lo:hi, :], nv_ref[lo:hi, :] = _adam_update(
                w_ref[lo:hi, :], gv, m_ref[lo:hi, :], v_ref[lo:hi, :])

    full = pl.BlockSpec((r, tc), lambda j, core_ref: (0, j))
    half = pl.BlockSpec((r2, tc), lambda j, core_ref: (0, j))
    shape = jax.ShapeDtypeStruct((r, cols), F32)
    return pl.pallas_call(
        body, name=name,
        grid_spec=pltpu.PrefetchScalarGridSpec(
            num_scalar_prefetch=1, grid=(cols // tc,), in_specs=[full, half, half, full, full],
            out_specs=[full] * 4),
        out_shape=[shape] * 4, compiler_params=_params(),
    )(core, w, g_mine, g_other, m, v)


def _adamw(w, g, m, v, *, name, rows=256):
    r, cols = w.shape
    tr = _rtile(r, rows, 8)

    def body(w_ref, g_ref, m_ref, v_ref, d_ref, nm_ref, nv_ref):
        d_ref[...], nm_ref[...], nv_ref[...] = _adam_update(w_ref[...], g_ref[...], m_ref[...], v_ref[...])

    blk = pl.BlockSpec((tr, cols), lambda i: (i, 0))
    shape = jax.ShapeDtypeStruct((r, cols), F32)
    return pl.pallas_call(
        body, name=name, grid=(r // tr,), in_specs=[blk] * 4, out_specs=[blk] * 3,
        out_shape=[shape] * 3, compiler_params=_params(),
    )(w, g, m, v)


_BIG = (("w_in", 1), ("w_branch_a", 0), ("w_branch_b", 0), ("w_out", 0), ("w_up", 1), ("w_down", 0),
        ("w_ple", 1), ("w_ple_gate", 0))
_SMALL = ("gmlp_ln_g", "gmlp_ln_b", "gmlp_w_s", "gmlp_b_s", "norm_ffn_g", "conv_b", "norm_ple_g", "norm_final_g",
          "b_f", "norm_mix_g")
N_LATE = 2
_WEIGHTS = ("norm_mix_g", "w_in", "b_f", "gmlp_ln_g", "gmlp_ln_b", "gmlp_w_s", "gmlp_b_s", "w_branch_a",
            "w_branch_b", "w_out", "norm_ffn_g", "w_up", "conv_w", "conv_b", "w_down", "norm_ple_g", "w_ple",
            "w_ple_gate", "norm_final_g")
_PACK_ROWS = 8


def _pack(arrays):
    parts = []
    for a in arrays:
        flat = a.reshape(-1)
        unit = _PACK_ROWS * LANES
        flat = jnp.pad(flat, (0, (-flat.shape[0]) % unit))
        parts.append(flat.reshape(-1, LANES))
    return jnp.concatenate(parts, axis=0)


def _unpack(packed, shapes):
    out, row = [], 0
    for shp in shapes:
        size = math.prod(shp)
        rows = -(-size // (_PACK_ROWS * LANES)) * _PACK_ROWS
        out.append(packed[row:row + rows].reshape(-1)[:size].reshape(shp))
        row += rows
    return out


def _take_cols(parts, lo, hi):
    out, start = [], 0
    for a in parts:
        width = a.shape[1]
        a0, a1 = max(lo, start) - start, min(hi, start + width) - start
        if a1 > a0:
            out.append(a if (a0, a1) == (0, width) else a[:, a0:a1])
        start += width
    return out[0] if len(out) == 1 else jnp.concatenate(out, axis=1)


def _take_rows(parts, lo, hi):
    out, start = [], 0
    for a in parts:
        height = a.shape[0]
        a0, a1 = max(lo, start) - start, min(hi, start + height) - start
        if a1 > a0:
            out.append(a if (a0, a1) == (0, height) else a[a0:a1])
        start += height
    return out[0] if len(out) == 1 else jnp.concatenate(out, axis=0)


def _assemble(gathered, axis):
    n, r, cols = gathered.shape
    if axis == 0:
        return gathered.reshape(n * r, cols)
    return _take_cols([gathered[j] for j in range(n)], 0, n * cols)


def _to_chunks(parts, axis):
    rows, total = parts[0].shape[0], sum(a.shape[1] for a in parts)
    if axis == 0:
        r, cols = rows // N_CHIPS, total
        chunks = _take_cols(parts, 0, total).reshape(N_CHIPS, r, cols)
    else:
        r, cols = rows, total // N_CHIPS
        chunks = jnp.stack([_take_cols(parts, j * cols, (j + 1) * cols) for j in range(N_CHIPS)])
    return chunks.reshape(N_CHIPS, 2, r // 2, cols)


def kernel(x, p, norm_mix_g, w_in, b_f, gmlp_ln_g, gmlp_ln_b, gmlp_w_s, gmlp_b_s, w_branch_a, w_branch_b, w_out, norm_ffn_g, w_up, conv_w, conv_b, w_down, norm_ple_g, w_ple, w_ple_gate, norm_final_g, loss_target, m_norm_mix_g, m_w_in, m_b_f, m_gmlp_ln_g, m_gmlp_ln_b, m_gmlp_w_s, m_gmlp_b_s, m_w_branch_a, m_w_branch_b, m_w_out, m_norm_ffn_g, m_w_up, m_conv_w, m_conv_b, m_w_down, m_norm_ple_g, m_w_ple, m_w_ple_gate, m_norm_final_g, v_norm_mix_g, v_w_in, v_b_f, v_gmlp_ln_g, v_gmlp_ln_b, v_gmlp_w_s, v_gmlp_b_s, v_w_branch_a, v_w_branch_b, v_w_out, v_norm_ffn_g, v_w_up, v_conv_w, v_conv_b, v_w_down, v_norm_ple_g, v_w_ple, v_w_ple_gate, v_norm_final_g):
    args = dict(locals())
    wt = {n: args[n] for n in _WEIGHTS}
    mom = {n: args["m_" + n] for n in _WEIGHTS}
    var = {n: args["v_" + n] for n in _WEIGHTS}
    chip = 2 * lax.axis_index("x") + lax.axis_index("y")
    core = lax.axis_index("c").astype(jnp.int32).reshape(1)

    chip1 = chip.astype(jnp.int32).reshape(1)
    device = 2 * chip + lax.axis_index("c")
    axis_of = dict(_BIG)
    names = [n for n, _ in _BIG]
    put_mine = lambda land, mine: lax.dynamic_update_index_in_dim(land, mine, chip, 0)

    shard_in = w_in[0].astype(BF16)
    sems_in = _copies_start([shard_in], [lax.empty((N_CHIPS,) + shard_in.shape, BF16)], _gather_plan([True]), 3,
                            name="gather_start_in")
    _, wt["w_in"], mom["w_in"], var["w_in"] = lax.optimization_barrier((sems_in[4], w_in, m_w_in, v_w_in))
    shards = [wt[n][0].astype(BF16) for n in names[1:]] + [conv_w[0]]
    halved = [True] * len(names[1:]) + [False]
    lands = [lax.empty((N_CHIPS,) + a.shape, a.dtype) for a in shards]
    send_sems, recv_sems, srcs, lands, rest_token = _copies_start(
        shards, lands, _gather_plan(halved), 3 * len(shards), name="gather_start_rest", after=[sems_in[4]])
    o1 = 2 * GMLP_WIDTH
    o2 = o1 + 3 * FOX_WIDTH
    o3 = o2 + FOX_HEADS
    fpad = ((0, 0), (0, LANES - FOX_HEADS))
    w = {
        "conv_b": conv_b, "norm_mix_g": norm_mix_g, "norm_ffn_g": norm_ffn_g, "norm_ple_g": norm_ple_g,
        "norm_final_g": norm_final_g.reshape(1, D_MODEL), "b_f": jnp.pad(b_f, fpad),
        "gmlp_ln_g": gmlp_ln_g, "gmlp_ln_b": gmlp_ln_b, "gmlp_w_s": gmlp_w_s[0],
        "gmlp_b_s_t": jnp.pad(gmlp_b_s[0].T, ((0, 0), (0, LANES - GMLP_GROUPS))),
        "first_dep": rest_token,
    }

    def get_w_in(after):
        early = [a.reshape(a.shape[-2:]) for a in (wt["w_in"], mom["w_in"], var["w_in"])]
        _, got = _copies_wait(sems_in[0], sems_in[1], sems_in[2], sems_in[3], _gather_plan([True]), 0,
                              [after] + early, name="gather_wait_in")
        got = _forward_halves(got, name="gather_forward_in")
        slots = put_mine(got[0], shard_in)
        slots = [slots[j] for j in range(N_CHIPS)]
        return {"w_uv": _take_cols(slots, 0, o1), "w_qkv": _take_cols(slots, o1, o2),
                "w_f": jnp.pad(_take_cols(slots, o2, o3), fpad), "w_g": _take_cols(slots, o3, o3 + 2 * D_MODEL)}

    def get_w_rest(after):
        _, got = _copies_wait(send_sems, recv_sems, srcs, lands, _gather_plan(halved), 0, [after],
                              name="gather_wait_rest")
        got = list(_forward_halves(got[:-1], name="gather_forward_rest")) + got[-1:]
        slots = {n: put_mine(got[i], shards[i]) for i, n in enumerate(names[1:])}
        full = {n: _assemble(slots[n], axis_of[n]) for n in names[1:] if n != "w_up"}
        up = [slots["w_up"][j] for j in range(N_CHIPS)]
        return {"w_branch_a": full["w_branch_a"], "w_branch_b": full["w_branch_b"], "w_out": full["w_out"],
                "w_up_a": _take_cols(up, 0, D_FF), "w_up_b": _take_cols(up, D_FF, 2 * D_FF),
                "w_down": full["w_down"], "w_ple": full["w_ple"], "w_ple_gate": full["w_ple_gate"],
                "conv_w": _assemble(put_mine(got[-1], shards[-1]), 1)}

    grads, delta, new_m, new_v = {}, {}, {}, {}
    pending = {}

    def to_chunks(n, gr):
        return _to_chunks(gr if isinstance(gr, list) else [gr], axis_of[n])

    def reduce_start(group, gfull, tag):
        chunks = [to_chunks(n, gfull[n]) for n in group]
        from_sibling = _pair_exchange(chunks, name="grad_pair_exchange_" + tag)
        pair_sums = [_pair_add(chunks[i], from_sibling[i], core, name="grad_pair_add_" + n) for i, n in enumerate(group)]
        empty = [lax.empty(a.shape, a.dtype) for a in pair_sums]
        ssem, rsem, own, recv, token = _copies_start(pair_sums, empty, _chip_plan, 3 * len(group),
                                                     name="grad_chip_start_" + tag)
        pending[tag] = (ssem, rsem, own, recv)
        return token

    def reduce_finish(group, tag, after):
        ssem, rsem, own, recv = pending[tag]
        own, recv = _copies_wait(ssem, rsem, own, recv, _chip_plan, 0, after, name="grad_chip_wait_" + tag)
        halves = [_chip_sum(own[i], recv[i], chip1, name="grad_chip_sum_" + n) for i, n in enumerate(group)]
        other_halves = _pair_share(halves, name="grad_pair_share_" + tag)
        for i, n in enumerate(group):
            shp = wt[n].shape
            outs = _adamw_halves(wt[n].reshape(shp[-2:]), halves[i], other_halves[i], mom[n].reshape(shp[-2:]),
                                 var[n].reshape(shp[-2:]), core, name="adamw_" + n)
            grads[n], delta[n], new_m[n], new_v[n] = (o.reshape(shp) for o in outs)
        return new_v[group[-1]]

    ffn_group = ("w_up", "w_down", "w_ple", "w_ple_gate", "w_branch_a", "w_branch_b", "w_out")
    mix_group = ("w_in",)

    def on_grads_ffn(g):
        gfull = dict(g)
        gfull["w_up"] = [g["w_up_a"], g["w_up_b"]]
        return reduce_start(ffn_group, gfull, "ffn")

    def on_grads_small(g):
        vec = _pack([g[n] for n in _SMALL[:-N_LATE]] + [g["conv_w"]])
        ssem, rsem, own, recv, token = _copies_start(
            [vec], [lax.empty((8,) + vec.shape, F32)], _all_plan, 7, name="small_start")
        pending["small"] = (ssem, rsem, own, recv)
        return token

    def on_grads_mix(g):
        gfull = dict(g)
        gfull["w_in"] = [g["w_uv"], g["w_qkv"], g["w_f"][:, :FOX_HEADS], g["w_g"]]
        token = reduce_start(mix_group, gfull, "mix")
        pending["ffn_done"] = reduce_finish(ffn_group, "ffn", [token])
        return token

    loss, grad_x, g = _device_step(x[0], p[0, 0], loss_target[0], w, get_w_in, get_w_rest, on_grads_ffn,
                                   on_grads_small, on_grads_mix)

    mix_done = reduce_finish(mix_group, "mix", [grad_x, pending["ffn_done"]])
    ssem, rsem, own, recv = pending["small"]
    own, recv = _copies_wait(ssem, rsem, own, recv, _all_plan, 0, [mix_done], name="small_wait")
    vec_early = _sum_slots(lax.dynamic_update_index_in_dim(recv[0], own[0], device, 0), F32, name="small_sum")
    vec_late = _pack([g["b_f"][:, :FOX_HEADS], g["norm_mix_g"]])
    vec_late = _sum_slots(_all_exchange(vec_late, name="small_exchange_late"), F32, name="small_sum_late")
    early_rows = _pack([wt[n] for n in _SMALL[:-N_LATE]]).shape[0]
    vec = jnp.concatenate([vec_early[:early_rows], vec_late], axis=0)
    for n, a in zip(_SMALL, _unpack(vec, [wt[n].shape for n in _SMALL])):
        grads[n] = a
    conv_w_grad = _unpack(vec_early[early_rows:], [(3, 2 * D_FF)])[0]
    grads["conv_w"] = lax.dynamic_slice_in_dim(conv_w_grad, chip * conv_w.shape[2], conv_w.shape[2], axis=1).reshape(conv_w.shape)

    shp = conv_w.shape
    outs = _adamw(conv_w.reshape(shp[-2:]), grads["conv_w"].reshape(shp[-2:]), m_conv_w.reshape(shp[-2:]),
                  v_conv_w.reshape(shp[-2:]), name="adamw_conv_w")
    delta["conv_w"], new_m["conv_w"], new_v["conv_w"] = (o.reshape(shp) for o in outs)
    outs = _adamw(_pack([wt[n] for n in _SMALL]), vec, _pack([mom[n] for n in _SMALL]),
                  _pack([var[n] for n in _SMALL]), name="adamw_small", rows=2048)
    for d, o in zip((delta, new_m, new_v), outs):
        for n, a in zip(_SMALL, _unpack(o, [wt[n].shape for n in _SMALL])):
            d[n] = a

    total_loss = lax.psum(loss[0, 0], ("x", "y", "c"))
    return (total_loss, grad_x.reshape(x.shape), *[grads[n] for n in _WEIGHTS], *[delta[n] for n in _WEIGHTS],
            *[new_m[n] for n in _WEIGHTS], *[new_v[n] for n in _WEIGHTS])
```

```python
import functools
import math

import jax
import jax.numpy as jnp
from jax import lax
from jax.experimental import pallas as pl
from jax.experimental.pallas import tpu as pltpu

F32 = jnp.float32
BF16 = jnp.bfloat16

D_MODEL = 1024
EPS = 1e-6
CHUNK = 64
GMLP_GROUPS = 8
GMLP_BLOCK = 128
GMLP_WIDTH = 1024
FOX_HEADS = 16
FOX_HEAD_DIM = 64
FOX_WIDTH = 1024
HEAD_PAIRS = FOX_HEADS // 2
ATT_BLOCK = 128
D_FF = 2816
PLE_DIM = 256
LANES = 128
BF16_TILE_ROWS = 16
N_CHIPS = 4

ADAM_LR = 0.001
ADAM_B1 = 0.9
ADAM_B2 = 0.999
ADAM_EPS = 1e-08
ADAM_WD = 0.01
ADAM_STEP = 10

VMEM_LIMIT = 56 * 1024 * 1024
MESH = pl.DeviceIdType.MESH

_NN = (((1,), (0,)), ((), ()))
_NT = (((1,), (1,)), ((), ()))
_TN = (((0,), (0,)), ((), ()))


def _params(**kw):
    return pltpu.CompilerParams(vmem_limit_bytes=VMEM_LIMIT, **kw)


def _tile(dim, pref):
    if dim <= pref:
        return dim
    t = (pref // LANES) * LANES
    while t >= LANES:
        if dim % t == 0:
            return t
        t -= LANES
    return dim


def _dot(a, b, dn):
    return lax.dot_general(a.astype(BF16), b.astype(BF16), dn, preferred_element_type=F32)


def _gelu(x):
    c = math.sqrt(2.0 / math.pi)
    t = jnp.tanh(c * (x + 0.044715 * x * x * x))
    return 0.5 * x * (1.0 + t)


def _gelu_and_grad(x):
    c = math.sqrt(2.0 / math.pi)
    x2 = x * x
    t = jnp.tanh(c * (x + 0.044715 * x2 * x))
    g = 0.5 * x * (1.0 + t)
    dg = 0.5 * (1.0 + t) + 0.5 * x * (1.0 - t * t) * c * (1.0 + 3.0 * 0.044715 * x2)
    return g, dg


def _sigmoid(x):
    return 1.0 / (1.0 + jnp.exp(-x))


def _mm(a, b, *, mode, out_dtype, name, add=None, tm=512, tn=512, dep=None):
    if mode == "nn":
        m, k = a.shape
        k2, n = b.shape
    elif mode == "nt":
        m, k = a.shape
        n, k2 = b.shape
    else:
        k, m = a.shape
        k2, n = b.shape
    assert k == k2, (name, a.shape, b.shape)
    tm = _tile(m, tm)
    tn = _tile(n, tn)
    dn = {"nn": _NN, "nt": _NT, "tn": _TN}[mode]

    def body(a_ref, b_ref, *rest):
        o_ref = rest[-1]
        acc = _dot(a_ref[...], b_ref[...], dn)
        if add is not None:
            acc = acc + rest[0][...].astype(F32)
        o_ref[...] = acc.astype(o_ref.dtype)

    a_spec = pl.BlockSpec((k, tm), lambda i, j: (0, i)) if mode == "tn" else pl.BlockSpec((tm, k), lambda i, j: (i, 0))
    b_spec = pl.BlockSpec((tn, k), lambda i, j: (j, 0)) if mode == "nt" else pl.BlockSpec((k, tn), lambda i, j: (0, j))
    o_spec = pl.BlockSpec((tm, tn), lambda i, j: (i, j))
    in_specs = [a_spec, b_spec]
    args = [a, b]
    if add is not None:
        in_specs.append(o_spec)
        args.append(add)
    if dep is not None:
        in_specs.append(pl.BlockSpec(memory_space=pl.ANY))
        args.append(dep)
    return pl.pallas_call(
        body, name=name, grid=(m // tm, n // tn), in_specs=in_specs, out_specs=o_spec,
        out_shape=jax.ShapeDtypeStruct((m, n), out_dtype), compiler_params=_params(),
    )(*args)


def _mm_nt_sum(pairs, *, out_dtype, name, tm=256, dep=None):
    m, n = pairs[0][0].shape[0], pairs[0][1].shape[0]
    tm = _tile(m, tm)
    np_ = len(pairs)

    def body(*refs):
        o_ref = refs[-1] if dep is None else refs[-1]
        acc = None
        for p in range(np_):
            part = _dot(refs[2 * p][...], refs[2 * p + 1][...], _NT)
            acc = part if acc is None else acc + part
        o_ref[...] = acc.astype(o_ref.dtype)

    in_specs, args = [], []
    for a, b in pairs:
        assert a.shape[0] == m and b.shape[0] == n and a.shape[1] == b.shape[1], (name, a.shape, b.shape)
        in_specs += [pl.BlockSpec((tm, a.shape[1]), lambda i: (i, 0)), pl.BlockSpec(b.shape, lambda i: (0, 0))]
        args += [a, b]
    if dep is not None:
        in_specs.append(pl.BlockSpec(memory_space=pl.ANY))
        args.append(dep)
    return pl.pallas_call(
        body, name=name, grid=(m // tm,), in_specs=in_specs, out_specs=pl.BlockSpec((tm, n), lambda i: (i, 0)),
        out_shape=jax.ShapeDtypeStruct((m, n), out_dtype), compiler_params=_params(),
    )(*args)


def _rms_fwd(x, g, *, name, tm=256, dep=None):
    s, d = x.shape
    tm = _tile(s, tm)

    def body(x_ref, g_ref, *rest):
        h_ref = rest[-1]
        xv = x_ref[...]
        r = lax.rsqrt(jnp.mean(xv * xv, axis=-1, keepdims=True) + EPS)
        h_ref[...] = (xv * r * g_ref[...]).astype(h_ref.dtype)

    deps = [] if dep is None else [dep]
    return pl.pallas_call(
        body, name=name, grid=(s // tm,),
        in_specs=[pl.BlockSpec((tm, d), lambda i: (i, 0)), pl.BlockSpec((1, d), lambda i: (0, 0))]
                 + [pl.BlockSpec(memory_space=pl.ANY)] * len(deps),
        out_specs=pl.BlockSpec((tm, d), lambda i: (i, 0)),
        out_shape=jax.ShapeDtypeStruct((s, d), BF16), compiler_params=_params(),
    )(x, g, *deps)


def _rms_bwd(x, g, dh, dres, *, name, tm=256):
    s, d = x.shape
    tm = _tile(s, tm)

    def body(x_ref, g_ref, dh_ref, dres_ref, dx_ref, dxb_ref, dg_ref):
        xv = x_ref[...]
        r = lax.rsqrt(jnp.mean(xv * xv, axis=-1, keepdims=True) + EPS)
        xhat = xv * r
        dhv = dh_ref[...].astype(F32)
        dyg = dhv * g_ref[...]
        dx = dres_ref[...] + r * (dyg - xhat * jnp.mean(dyg * xhat, axis=-1, keepdims=True))
        dx_ref[...] = dx
        dxb_ref[...] = dx.astype(dxb_ref.dtype)

        @pl.when(pl.program_id(0) == 0)
        def _():
            dg_ref[...] = jnp.zeros_like(dg_ref)

        dg_ref[...] += jnp.sum(dhv * xhat, axis=0, keepdims=True)

    row = pl.BlockSpec((tm, d), lambda i: (i, 0))
    vec = pl.BlockSpec((1, d), lambda i: (0, 0))
    return pl.pallas_call(
        body, name=name, grid=(s // tm,), in_specs=[row, vec, row, row], out_specs=[row, row, vec],
        out_shape=[jax.ShapeDtypeStruct((s, d), F32), jax.ShapeDtypeStruct((s, d), BF16),
                   jax.ShapeDtypeStruct((1, d), F32)],
        compiler_params=_params(),
    )(x, g, dh, dres)


def _gmlp_mask():
    t = lax.broadcasted_iota(jnp.int32, (GMLP_BLOCK, GMLP_BLOCK), 0)
    s_ = lax.broadcasted_iota(jnp.int32, (GMLP_BLOCK, GMLP_BLOCK), 1)
    return (s_ // CHUNK) <= (t // CHUNK)


def _gmlp_norm(zv, ln_g, ln_b):
    vv, dvv = _gelu_and_grad(zv)
    mu = jnp.mean(vv, axis=-1, keepdims=True)
    xc = vv - mu
    rstd = lax.rsqrt(jnp.mean(xc * xc, axis=-1, keepdims=True) + EPS)
    vhat = xc * rstd
    return vhat * ln_g + ln_b, vhat, rstd, dvv


def _gmlp_fwd(z_uv, ln_g, ln_b, w_s, b_s_t, *, name):
    s = z_uv.shape[0]
    w = GMLP_WIDTH
    gd = w // GMLP_GROUPS

    def body(z_ref, lg_ref, lb_ref, ws_ref, bs_ref, a_ref):
        u = _gelu(z_ref[:, :w].astype(F32))
        vn, _, _, _ = _gmlp_norm(z_ref[:, w:].astype(F32), lg_ref[...], lb_ref[...])
        mask = _gmlp_mask()
        for g in range(GMLP_GROUPS):
            wm = jnp.where(mask, ws_ref[g], 0.0)
            mixed = _dot(wm, vn[:, g * gd:(g + 1) * gd], _NN) + bs_ref[:, g:g + 1]
            a_ref[:, g * gd:(g + 1) * gd] = (u[:, g * gd:(g + 1) * gd] * mixed).astype(a_ref.dtype)

    full = lambda shape: pl.BlockSpec(shape, lambda i: (0,) * len(shape))
    return pl.pallas_call(
        body, name=name, grid=(s // GMLP_BLOCK,),
        in_specs=[pl.BlockSpec((GMLP_BLOCK, 2 * w), lambda i: (i, 0)), full((1, w)), full((1, w)),
                  full((GMLP_GROUPS, GMLP_BLOCK, GMLP_BLOCK)), full((GMLP_BLOCK, LANES))],
        out_specs=pl.BlockSpec((GMLP_BLOCK, w), lambda i: (i, 0)),
        out_shape=jax.ShapeDtypeStruct((s, w), BF16), compiler_params=_params(),
    )(z_uv, ln_g, ln_b, w_s, b_s_t)


def _gmlp_bwd(z_uv, da, ln_g, ln_b, w_s, b_s_t, *, name):
    s = z_uv.shape[0]
    w = GMLP_WIDTH
    gd = w // GMLP_GROUPS

    def body(z_ref, da_ref, lg_ref, lb_ref, ws_ref, bs_ref, dz_ref, dws_ref, dbs_ref, dlg_ref, dlb_ref):
        @pl.when(pl.program_id(0) == 0)
        def _():
            dws_ref[...] = jnp.zeros_like(dws_ref)
            dbs_ref[...] = jnp.zeros_like(dbs_ref)
            dlg_ref[...] = jnp.zeros_like(dlg_ref)
            dlb_ref[...] = jnp.zeros_like(dlb_ref)

        u, du_dz = _gelu_and_grad(z_ref[:, :w].astype(F32))
        lg = lg_ref[...]
        vn, vhat, rstd, dvv_dz = _gmlp_norm(z_ref[:, w:].astype(F32), lg, lb_ref[...])
        dav = da_ref[...].astype(F32)
        mask = _gmlp_mask()
        lane = lax.broadcasted_iota(jnp.int32, (GMLP_BLOCK, LANES), 1)
        dvn_parts = []
        dbs = jnp.zeros((GMLP_BLOCK, LANES), F32)
        for g in range(GMLP_GROUPS):
            sl = slice(g * gd, (g + 1) * gd)
            wm = jnp.where(mask, ws_ref[g], 0.0)
            vn_g = vn[:, sl]
            mixed = _dot(wm, vn_g, _NN) + bs_ref[:, g:g + 1]
            dmixed = dav[:, sl] * u[:, sl]
            dz_ref[:, sl] = (dav[:, sl] * mixed * du_dz[:, sl]).astype(dz_ref.dtype)
            dvn_parts.append(_dot(wm, dmixed, _TN))
            dws_ref[g] += jnp.where(mask, _dot(dmixed, vn_g, _NT), 0.0)
            dbs = dbs + jnp.where(lane == g, jnp.sum(dmixed, axis=-1, keepdims=True), 0.0)
        dbs_ref[...] += dbs
        dvn = jnp.concatenate(dvn_parts, axis=-1)
        dlg_ref[...] += jnp.sum(dvn * vhat, axis=0, keepdims=True)
        dlb_ref[...] += jnp.sum(dvn, axis=0, keepdims=True)
        dyg = dvn * lg
        dvv = rstd * (dyg - jnp.mean(dyg, axis=-1, keepdims=True)
                      - vhat * jnp.mean(dyg * vhat, axis=-1, keepdims=True))
        dz_ref[:, w:] = (dvv * dvv_dz).astype(dz_ref.dtype)

    full = lambda shape: pl.BlockSpec(shape, lambda i: (0,) * len(shape))
    return pl.pallas_call(
        body, name=name, grid=(s // GMLP_BLOCK,),
        in_specs=[pl.BlockSpec((GMLP_BLOCK, 2 * w), lambda i: (i, 0)),
                  pl.BlockSpec((GMLP_BLOCK, w), lambda i: (i, 0)), full((1, w)), full((1, w)),
                  full((GMLP_GROUPS, GMLP_BLOCK, GMLP_BLOCK)), full((GMLP_BLOCK, LANES))],
        out_specs=[pl.BlockSpec((GMLP_BLOCK, 2 * w), lambda i: (i, 0)),
                   full((GMLP_GROUPS, GMLP_BLOCK, GMLP_BLOCK)), full((GMLP_BLOCK, LANES)),
                   full((1, w)), full((1, w))],
        out_shape=[jax.ShapeDtypeStruct((s, 2 * w), BF16),
                   jax.ShapeDtypeStruct((GMLP_GROUPS, GMLP_BLOCK, GMLP_BLOCK), F32),
                   jax.ShapeDtypeStruct((GMLP_BLOCK, LANES), F32),
                   jax.ShapeDtypeStruct((1, w), F32), jax.ShapeDtypeStruct((1, w), F32)],
        compiler_params=_params(),
    )(z_uv, da, ln_g, ln_b, w_s, b_s_t)


def _tri(lower):
    r = lax.broadcasted_iota(jnp.int32, (ATT_BLOCK, ATT_BLOCK), 0)
    c = lax.broadcasted_iota(jnp.int32, (ATT_BLOCK, ATT_BLOCK), 1)
    return jnp.where((c <= r) if lower else (c >= r), 1.0, 0.0).astype(F32)


def _log_sigmoid(x):
    return jnp.minimum(x, 0.0) - jnp.log(1.0 + jnp.exp(-jnp.abs(x)))


def _fox_cum(f, b_f, *, name):
    s = f.shape[0]
    nb = s // ATT_BLOCK

    def body(f_ref, b_ref, cb_ref, ct_ref, carry):
        @pl.when(pl.program_id(0) == 0)
        def _():
            carry[...] = jnp.zeros_like(carry)

        lf = _log_sigmoid(f_ref[...] + b_ref[...])
        cum = lax.dot_general(_tri(True), lf, _NN, precision=lax.Precision.HIGHEST,
                              preferred_element_type=F32) + carry[...]
        carry[...] = cum[ATT_BLOCK - 1:ATT_BLOCK, :]
        for h in range(FOX_HEADS):
            cb_ref[h] = jnp.broadcast_to(cum[:, h:h + 1], (ATT_BLOCK, LANES))
        ct_ref[...] = cum.T

    return pl.pallas_call(
        body, name=name, grid=(nb,),
        in_specs=[pl.BlockSpec((ATT_BLOCK, LANES), lambda i: (i, 0)), pl.BlockSpec((1, LANES), lambda i: (0, 0))],
        out_specs=[pl.BlockSpec((FOX_HEADS, ATT_BLOCK, LANES), lambda i: (0, i, 0)),
                   pl.BlockSpec((LANES, ATT_BLOCK), lambda i: (0, i))],
        out_shape=[jax.ShapeDtypeStruct((FOX_HEADS, s, LANES), F32), jax.ShapeDtypeStruct((LANES, s), F32)],
        scratch_shapes=[pltpu.VMEM((1, LANES), F32)], compiler_params=_params(),
    )(f, b_f)


def _fox_dlogit(dcum_t, f, b_f, *, name):
    s = f.shape[0]
    nb = s // ATT_BLOCK

    def body(dc_ref, f_ref, b_ref, df_ref, db_ref, carry):
        @pl.when(pl.program_id(0) == 0)
        def _():
            carry[...] = jnp.zeros_like(carry)
            db_ref[...] = jnp.zeros_like(db_ref)

        d = dc_ref[...].T
        dlog = lax.dot_general(_tri(False), d, _NN, precision=lax.Precision.HIGHEST,
                               preferred_element_type=F32) + carry[...]
        carry[...] = dlog[0:1, :]
        df = dlog * (1.0 - _sigmoid(f_ref[...] + b_ref[...]))
        df_ref[...] = df
        db_ref[...] += jnp.sum(df, axis=0, keepdims=True)

    rev = lambda i: nb - 1 - i
    return pl.pallas_call(
        body, name=name, grid=(nb,),
        in_specs=[pl.BlockSpec((LANES, ATT_BLOCK), lambda i: (0, rev(i))),
                  pl.BlockSpec((ATT_BLOCK, LANES), lambda i: (rev(i), 0)),
                  pl.BlockSpec((1, LANES), lambda i: (0, 0))],
        out_specs=[pl.BlockSpec((ATT_BLOCK, LANES), lambda i: (rev(i), 0)),
                   pl.BlockSpec((1, LANES), lambda i: (0, 0))],
        out_shape=[jax.ShapeDtypeStruct((s, LANES), F32), jax.ShapeDtypeStruct((1, LANES), F32)],
        scratch_shapes=[pltpu.VMEM((1, LANES), F32)], compiler_params=_params(),
    )(dcum_t, f, b_f)


def _causal(qi, ki):
    r = lax.broadcasted_iota(jnp.int32, (ATT_BLOCK, ATT_BLOCK), 0) + qi * ATT_BLOCK
    c = lax.broadcasted_iota(jnp.int32, (ATT_BLOCK, ATT_BLOCK), 1) + ki * ATT_BLOCK
    return c <= r


def _head_mask():
    return lax.broadcasted_iota(jnp.int32, (1, LANES), 1) < FOX_HEAD_DIM


def _attn_fwd(qkv, cum_b, cum_r, *, name):
    s = qkv.shape[0]
    nq = s // ATT_BLOCK
    scale = FOX_HEAD_DIM ** -0.5
    npair = HEAD_PAIRS

    def body(q_ref, k_ref, v_ref, cq_ref, ck_ref, o_ref, l_ref):
        qi = pl.program_id(1)
        m0 = _head_mask()
        q2 = q_ref[...]
        zero = jnp.zeros_like(q2)
        qs = (jnp.where(m0, q2, zero), jnp.where(m0, zero, q2))
        cqs = (cq_ref[0], cq_ref[1])

        def step(ki, carry, masked):
            off = pl.multiple_of(ki * ATT_BLOCK, ATT_BLOCK)
            k2 = k_ref[pl.ds(off, ATT_BLOCK), :]
            v2 = v_ref[pl.ds(off, ATT_BLOCK), :]
            out = []
            for hh in range(2):
                m, l, acc = carry[hh]
                sc = _dot(qs[hh], k2, _NT) * scale + (cqs[hh] - ck_ref[hh:hh + 1, pl.ds(off, ATT_BLOCK)])
                if masked:
                    sc = jnp.where(_causal(qi, ki), sc, -1e30)
                m_new = jnp.maximum(m, jnp.max(sc, axis=-1, keepdims=True))
                alpha = jnp.exp(m - m_new)
                p = jnp.exp(sc - m_new)
                l = alpha * l + jnp.sum(p, axis=-1, keepdims=True)
                acc = alpha * acc + _dot(p, v2, _NN)
                out.append((m_new, l, acc))
            return tuple(out)

        init = tuple((jnp.full((ATT_BLOCK, 1), -1e30, F32), jnp.zeros((ATT_BLOCK, 1), F32),
                      jnp.zeros((ATT_BLOCK, LANES), F32)) for _ in range(2))
        carry = lax.fori_loop(0, qi, lambda ki, c: step(ki, c, False), init)
        (ma, la, acca), (mb, lb, accb) = step(qi, carry, True)
        o_ref[...] = jnp.where(m0, acca / la, accb / lb).astype(o_ref.dtype)
        l_ref[0] = jnp.broadcast_to(ma + jnp.log(la), (ATT_BLOCK, LANES))
        l_ref[1] = jnp.broadcast_to(mb + jnp.log(lb), (ATT_BLOCK, LANES))

    stat = pl.BlockSpec((None, 2, ATT_BLOCK, LANES), lambda j, i: (j, 0, i, 0))
    row = pl.BlockSpec((None, 2, s), lambda j, i: (j, 0, 0))
    return pl.pallas_call(
        body, name=name, grid=(npair, nq),
        in_specs=[pl.BlockSpec((ATT_BLOCK, LANES), lambda j, i: (i, j)),
                  pl.BlockSpec((s, LANES), lambda j, i: (0, npair + j)),
                  pl.BlockSpec((s, LANES), lambda j, i: (0, 2 * npair + j)),
                  stat, row],
        out_specs=[pl.BlockSpec((ATT_BLOCK, LANES), lambda j, i: (i, j)), stat],
        out_shape=[jax.ShapeDtypeStruct((s, FOX_WIDTH), BF16),
                   jax.ShapeDtypeStruct((npair, 2, s, LANES), F32)],
        compiler_params=_params(),
    )(qkv, qkv, qkv, cum_b, cum_r)


def _attn_delta(qkv, do, lse_b, cum_b, cum_r, *, name):
    s = qkv.shape[0]
    nq = s // ATT_BLOCK
    scale = FOX_HEAD_DIM ** -0.5
    npair = HEAD_PAIRS

    def body(q_ref, k_ref, v_ref, do_ref, l_ref, cq_ref, ck_ref, d_ref):
        qi = pl.program_id(1)
        m0 = _head_mask()
        q2 = q_ref[...]
        do2 = do_ref[...]
        qs = (jnp.where(m0, q2, jnp.zeros_like(q2)), jnp.where(m0, jnp.zeros_like(q2), q2))
        dos = (jnp.where(m0, do2, jnp.zeros_like(do2)), jnp.where(m0, jnp.zeros_like(do2), do2))

        def step(ki, carry, masked):
            off = pl.multiple_of(ki * ATT_BLOCK, ATT_BLOCK)
            k2 = k_ref[pl.ds(off, ATT_BLOCK), :]
            v2 = v_ref[pl.ds(off, ATT_BLOCK), :]
            out = []
            for hh in range(2):
                sc = _dot(qs[hh], k2, _NT) * scale + (cq_ref[hh] - ck_ref[hh:hh + 1, pl.ds(off, ATT_BLOCK)])
                p = jnp.exp(sc - l_ref[hh])
                if masked:
                    p = jnp.where(_causal(qi, ki), p, 0.0)
                out.append(carry[hh] + jnp.sum(p * _dot(dos[hh], v2, _NT), axis=-1, keepdims=True))
            return tuple(out)

        init = (jnp.zeros((ATT_BLOCK, 1), F32), jnp.zeros((ATT_BLOCK, 1), F32))
        carry = lax.fori_loop(0, qi, lambda ki, c: step(ki, c, False), init)
        da, db = step(qi, carry, True)
        d_ref[0] = jnp.broadcast_to(da, (ATT_BLOCK, LANES))
        d_ref[1] = jnp.broadcast_to(db, (ATT_BLOCK, LANES))

    stat = pl.BlockSpec((None, 2, ATT_BLOCK, LANES), lambda j, i: (j, 0, i, 0))
    return pl.pallas_call(
        body, name=name, grid=(npair, nq),
        in_specs=[pl.BlockSpec((ATT_BLOCK, LANES), lambda j, i: (i, j)),
                  pl.BlockSpec((s, LANES), lambda j, i: (0, npair + j)),
                  pl.BlockSpec((s, LANES), lambda j, i: (0, 2 * npair + j)),
                  pl.BlockSpec((ATT_BLOCK, LANES), lambda j, i: (i, j)),
                  stat, stat, pl.BlockSpec((None, 2, s), lambda j, i: (j, 0, 0))],
        out_specs=stat,
        out_shape=jax.ShapeDtypeStruct((npair, 2, s, LANES), F32), compiler_params=_params(),
    )(qkv, qkv, qkv, do, lse_b, cum_b, cum_r)


def _attn_bwd(qkv, do, lse_b, delta_b, cum_b, cum_r, *, name):
    s = qkv.shape[0]
    nq = s // ATT_BLOCK
    scale = FOX_HEAD_DIM ** -0.5
    npair = HEAD_PAIRS

    def body(q_ref, k_ref, v_ref, do_ref, l_ref, dl_ref, cq_ref, ck_ref, dq_ref, dk_ref, dv_ref, dc_ref):
        ki = pl.program_id(1)
        m0 = _head_mask()
        k2 = k_ref[...]
        v2 = v_ref[...]
        koff = pl.multiple_of(ki * ATT_BLOCK, ATT_BLOCK)

        @pl.when(ki == 0)
        def _():
            dq_ref[...] = jnp.zeros_like(dq_ref)

        def step(qi, carry, masked):
            off = pl.multiple_of(qi * ATT_BLOCK, ATT_BLOCK)
            q2 = q_ref[pl.ds(off, ATT_BLOCK), :]
            do2 = do_ref[pl.ds(off, ATT_BLOCK), :]
            qzero = jnp.zeros_like(q2)
            dzero = jnp.zeros_like(do2)
            out = []
            dqs = []
            for hh in range(2):
                dk_acc, dv_acc, dc_acc = carry[hh]
                keep = m0 if hh == 0 else jnp.logical_not(m0)
                qh = jnp.where(keep, q2, qzero)
                doh = jnp.where(keep, do2, dzero)
                sc = _dot(qh, k2, _NT) * scale + (cq_ref[hh, pl.ds(off, ATT_BLOCK), :]
                                                 - ck_ref[hh:hh + 1, pl.ds(koff, ATT_BLOCK)])
                p = jnp.exp(sc - l_ref[hh, pl.ds(off, ATT_BLOCK), :])
                if masked:
                    p = jnp.where(_causal(qi, ki), p, 0.0)
                dp = _dot(doh, v2, _NT)
                ds = p * (dp - dl_ref[hh, pl.ds(off, ATT_BLOCK), :])
                dv_acc = dv_acc + _dot(p, do2, _TN)
                dk_acc = dk_acc + _dot(ds, q2, _TN)
                dc_acc = dc_acc - jnp.sum(ds, axis=0, keepdims=True)
                dqs.append(_dot(ds, k2, _NN))
                out.append((dk_acc, dv_acc, dc_acc))
            dq_ref[pl.ds(off, ATT_BLOCK), :] += jnp.where(m0, dqs[0], dqs[1]) * scale
            return tuple(out)

        init = tuple((jnp.zeros((ATT_BLOCK, LANES), F32), jnp.zeros((ATT_BLOCK, LANES), F32),
                      jnp.zeros((1, ATT_BLOCK), F32)) for _ in range(2))
        carry = step(ki, init, True)
        (dka, dva, dca), (dkb, dvb, dcb) = lax.fori_loop(ki + 1, nq, lambda qi, c: step(qi, c, False), carry)
        dk_ref[...] = (jnp.where(m0, dka, dkb) * scale).astype(dk_ref.dtype)
        dv_ref[...] = jnp.where(m0, dva, dvb).astype(dv_ref.dtype)
        dc_ref[0:1, :] = dca
        dc_ref[1:2, :] = dcb

    stat = pl.BlockSpec((None, 2, s, LANES), lambda j, i: (j, 0, 0, 0))
    colfull = lambda base: pl.BlockSpec((s, LANES), lambda j, i: (0, base + j))
    colblk = lambda base: pl.BlockSpec((ATT_BLOCK, LANES), lambda j, i: (i, base + j))
    return pl.pallas_call(
        body, name=name, grid=(npair, nq),
        in_specs=[colfull(0), colblk(npair), colblk(2 * npair), colfull(0), stat, stat, stat,
                  pl.BlockSpec((None, 2, s), lambda j, i: (j, 0, 0))],
        out_specs=[colfull(0), colblk(0), colblk(0), pl.BlockSpec((None, 2, ATT_BLOCK), lambda j, i: (j, 0, i))],
        out_shape=[jax.ShapeDtypeStruct((s, FOX_WIDTH), F32), jax.ShapeDtypeStruct((s, FOX_WIDTH), BF16),
                   jax.ShapeDtypeStruct((s, FOX_WIDTH), BF16), jax.ShapeDtypeStruct((npair, 2, s), F32)],
        compiler_params=_params(),
    )(qkv, qkv, qkv, do, lse_b, delta_b, cum_b, cum_r)


ATT_TQ = 256
ATT_TK = 256
ATT_SCALE = FOX_HEAD_DIM ** -0.5
assert ATT_SCALE == 0.125 and ATT_TQ == ATT_TK


def _causal_t(qi, ki):
    kpos = lax.broadcasted_iota(jnp.int32, (ATT_TK, ATT_TQ), 0) + ki * ATT_TK
    qpos = lax.broadcasted_iota(jnp.int32, (ATT_TK, ATT_TQ), 1) + qi * ATT_TQ
    return kpos <= qpos


def _row_mask():
    return lax.broadcasted_iota(jnp.int32, (LANES, 1), 0) < FOX_HEAD_DIM


def _lane_tile(a, width):
    return a if a.shape[1] == width else jnp.tile(a, (1, width // a.shape[1]))


def _transpose_bf16(a):
    return a.astype(F32).T.astype(BF16)


def _attn_fwd_t(qkv, cum_b, cum_r, *, name):
    s = qkv.shape[0]
    nq = s // ATT_TQ
    npair = HEAD_PAIRS

    def body(q_ref, k_ref, v_ref, cq_ref, ck_ref, o_ref, ot_ref, l_ref, vt_ref):
        qi = pl.program_id(1)
        rows = _row_mask()

        @pl.when(qi == 0)
        def _():
            vt_ref[...] = _transpose_bf16(v_ref[...])

        qt = _transpose_bf16(q_ref[...]) * ATT_SCALE
        zero = jnp.zeros_like(qt)
        qts = (jnp.where(rows, qt, zero), jnp.where(rows, zero, qt))

        def step(ki, carry, masked):
            off = pl.multiple_of(ki * ATT_TK, ATT_TK)
            k2 = k_ref[pl.ds(off, ATT_TK), :]
            vt = vt_ref[:, pl.ds(off, ATT_TK)]
            out = []
            for hh in range(2):
                m, l, acc = carry[hh]
                bias = cq_ref[hh:hh + 1, :] - _lane_tile(ck_ref[hh, pl.ds(off, ATT_TK), :], ATT_TQ)
                sc = _dot(k2, qts[hh], _NN) + bias
                if masked:
                    sc = jnp.where(_causal_t(qi, ki), sc, -1e30)
                m_new = jnp.maximum(m, jnp.max(sc, axis=0, keepdims=True))
                alpha = jnp.exp(m - m_new)
                p = jnp.exp(sc - m_new)
                l = alpha * l + jnp.sum(p, axis=0, keepdims=True)
                p_hi = p.astype(BF16)
                p_lo = (p - p_hi.astype(F32)).astype(BF16)
                acc = alpha * acc + (_dot(vt, p_hi, _NN) + _dot(vt, p_lo, _NN))
                out.append((m_new, l, acc))
            return tuple(out)

        init = tuple((jnp.full((1, ATT_TQ), -1e30, F32), jnp.zeros((1, ATT_TQ), F32),
                      jnp.zeros((LANES, ATT_TQ), F32)) for _ in range(2))
        carry = lax.fori_loop(0, qi // 2, lambda kk, c: step(2 * kk + 1, step(2 * kk, c, False), False), init)
        carry = lax.cond(qi % 2 == 1, lambda c: step(qi - 1, c, False), lambda c: c, carry)
        (ma, la, acca), (mb, lb, accb) = step(qi, carry, True)
        ot = jnp.where(rows, acca / la, accb / lb)
        ot_ref[...] = ot
        o_ref[...] = ot.T.astype(o_ref.dtype)
        l_ref[0:1, :] = ma + jnp.log(la)
        l_ref[1:2, :] = mb + jnp.log(lb)

    row = pl.BlockSpec((None, 2, ATT_TQ), lambda j, i: (j, 0, i))
    return pl.pallas_call(
        body, name=name, grid=(npair, nq),
        in_specs=[pl.BlockSpec((ATT_TQ, LANES), lambda j, i: (i, j)),
                  pl.BlockSpec((s, LANES), lambda j, i: (0, npair + j)),
                  pl.BlockSpec((s, LANES), lambda j, i: (0, 2 * npair + j)),
                  row, pl.BlockSpec((None, 2, s, LANES), lambda j, i: (j, 0, 0, 0))],
        out_specs=[pl.BlockSpec((ATT_TQ, LANES), lambda j, i: (i, j)),
                   pl.BlockSpec((LANES, ATT_TQ), lambda j, i: (j, i)), row],
        out_shape=[jax.ShapeDtypeStruct((s, FOX_WIDTH), BF16), jax.ShapeDtypeStruct((FOX_WIDTH, s), F32),
                   jax.ShapeDtypeStruct((npair, 2, s), F32)],
        scratch_shapes=[pltpu.VMEM((LANES, s), BF16)],
        compiler_params=_params(),
    )(qkv, qkv, qkv, cum_r, cum_b)


def _attn_delta_t(do_t, o_t, *, name):
    s = o_t.shape[1]
    ts = _tile(s, 512)

    def body(do_ref, o_ref, d_ref):
        prod = do_ref[...].astype(F32) * o_ref[...]
        d_ref[0:1, :] = jnp.sum(prod[:FOX_HEAD_DIM], axis=0, keepdims=True)
        d_ref[1:2, :] = jnp.sum(prod[FOX_HEAD_DIM:], axis=0, keepdims=True)

    blk = pl.BlockSpec((LANES, ts), lambda j, i: (j, i))
    return pl.pallas_call(
        body, name=name, grid=(HEAD_PAIRS, s // ts), in_specs=[blk, blk],
        out_specs=pl.BlockSpec((None, 2, ts), lambda j, i: (j, 0, i)),
        out_shape=jax.ShapeDtypeStruct((HEAD_PAIRS, 2, s), F32), compiler_params=_params(),
    )(do_t, o_t)


def _attn_bwd_t(qkv, do, o_t, lse, cum_b, cum_r, *, name, dep=None):
    s = qkv.shape[0]
    nq = s // ATT_TQ
    npair = HEAD_PAIRS

    deps = [] if dep is None else [dep]

    def body(q_ref, k_ref, v_ref, do_ref, ot_ref, l_ref, cq_ref, ck_ref, *rest):
        dq_ref, dk_ref, dv_ref, dc_ref, qt_ref, dot_ref, dqt_ref, dl_ref = rest[len(deps):]
        ki = pl.program_id(1)
        m0 = _head_mask()
        rows = _row_mask()
        k2 = k_ref[...]
        v2 = v_ref[...]
        kt = _transpose_bf16(k2)
        ks = k2 * ATT_SCALE
        kz, vz = jnp.zeros_like(k2), jnp.zeros_like(v2)
        khs = (jnp.where(m0, ks, kz), jnp.where(m0, kz, ks))
        vhs = (jnp.where(m0, v2, vz), jnp.where(m0, vz, v2))
        cks = tuple(_lane_tile(ck_ref[hh], ATT_TQ) for hh in range(2))

        @pl.when(ki == 0)
        def _():
            dqt_ref[...] = jnp.zeros_like(dqt_ref)
            qt_ref[...] = _transpose_bf16(q_ref[...])
            do_t = do_ref[...].astype(F32).T
            dot_ref[...] = do_t.astype(BF16)
            prod = do_t * ot_ref[...]
            dl_ref[0:1, :] = jnp.sum(prod[:FOX_HEAD_DIM], axis=0, keepdims=True)
            dl_ref[1:2, :] = jnp.sum(prod[FOX_HEAD_DIM:], axis=0, keepdims=True)

        def step(qi, carry, masked):
            off = pl.multiple_of(qi * ATT_TQ, ATT_TQ)
            q2 = q_ref[pl.ds(off, ATT_TQ), :]
            do2 = do_ref[pl.ds(off, ATT_TQ), :]
            qt = qt_ref[:, pl.ds(off, ATT_TQ)]
            dot_ = dot_ref[:, pl.ds(off, ATT_TQ)]
            out, dqs = [], []
            for hh in range(2):
                dk_acc, dv_acc, dc_acc = carry[hh]
                sc = _dot(khs[hh], qt, _NN) + (cq_ref[hh:hh + 1, pl.ds(off, ATT_TQ)] - cks[hh])
                p = jnp.exp(sc - l_ref[hh:hh + 1, pl.ds(off, ATT_TQ)])
                if masked:
                    p = jnp.where(_causal_t(qi, ki), p, 0.0)
                dp = _dot(vhs[hh], dot_, _NN)
                ds = p * (dp - dl_ref[hh:hh + 1, pl.ds(off, ATT_TQ)])
                dc_acc = dc_acc - jnp.sum(ds, axis=1, keepdims=True)
                dss = (ds * ATT_SCALE).astype(BF16)
                dv_acc = dv_acc + _dot(p, do2, _NN)
                dk_acc = dk_acc + _dot(dss, q2, _NN)
                dqs.append(_dot(kt, dss, _NN))
                out.append((dk_acc, dv_acc, dc_acc))
            dqt_ref[:, pl.ds(off, ATT_TQ)] += jnp.where(rows, dqs[0], dqs[1])
            return tuple(out)

        init = tuple((jnp.zeros((ATT_TK, LANES), F32), jnp.zeros((ATT_TK, LANES), F32),
                      jnp.zeros((ATT_TK, 1), F32)) for _ in range(2))
        carry = step(ki, init, True)
        rest = nq - 1 - ki
        carry = lax.fori_loop(
            0, rest // 2, lambda t, c: step(ki + 2 + 2 * t, step(ki + 1 + 2 * t, c, False), False), carry)
        carry = lax.cond(rest % 2 == 1, lambda c: step(nq - 1, c, False), lambda c: c, carry)
        (dka, dva, dca), (dkb, dvb, dcb) = carry
        dk_ref[...] = jnp.where(m0, dka, dkb).astype(dk_ref.dtype)
        dv_ref[...] = jnp.where(m0, dva, dvb).astype(dv_ref.dtype)
        dc_ref[0] = jnp.broadcast_to(dca, (ATT_TK, LANES))
        dc_ref[1] = jnp.broadcast_to(dcb, (ATT_TK, LANES))

        @pl.when(ki == nq - 1)
        def _():
            dq_ref[...] = dqt_ref[...].T.astype(dq_ref.dtype)

    colfull = lambda base: pl.BlockSpec((s, LANES), lambda j, i: (0, base + j))
    colblk = lambda base: pl.BlockSpec((ATT_TK, LANES), lambda j, i: (i, base + j))
    stat = pl.BlockSpec((None, 2, s), lambda j, i: (j, 0, 0))
    bcast = pl.BlockSpec((None, 2, ATT_TK, LANES), lambda j, i: (j, 0, i, 0))
    grad = jax.ShapeDtypeStruct((s, FOX_WIDTH), BF16)
    return pl.pallas_call(
        body, name=name, grid=(npair, nq),
        in_specs=[colfull(0), colblk(npair), colblk(2 * npair), colfull(0),
                  pl.BlockSpec((LANES, s), lambda j, i: (j, 0)), stat, stat, bcast]
                 + [pl.BlockSpec(memory_space=pl.ANY)] * len(deps),
        out_specs=[colfull(0), colblk(0), colblk(0), bcast],
        out_shape=[grad, grad, grad, jax.ShapeDtypeStruct((npair, 2, s, LANES), F32)],
        scratch_shapes=[pltpu.VMEM((LANES, s), BF16), pltpu.VMEM((LANES, s), BF16), pltpu.VMEM((LANES, s), F32),
                        pltpu.VMEM((2, s), F32)],
        compiler_params=_params(),
    )(qkv, qkv, qkv, do, o_t, lse, cum_r, cum_b, *deps)


def _merge_fwd(zg, ya, yb, *, name, tm=256):
    s, d = ya.shape
    tm = _tile(s, tm)

    def body(zg_ref, ya_ref, yb_ref, m_ref):
        ga = _sigmoid(zg_ref[:, :d].astype(F32))
        gb = _sigmoid(zg_ref[:, d:].astype(F32))
        m_ref[...] = (ga * ya_ref[...].astype(F32) + gb * yb_ref[...].astype(F32)).astype(m_ref.dtype)

    row = pl.BlockSpec((tm, d), lambda i: (i, 0))
    row2 = pl.BlockSpec((tm, 2 * d), lambda i: (i, 0))
    return pl.pallas_call(
        body, name=name, grid=(s // tm,), in_specs=[row2, row, row], out_specs=row,
        out_shape=jax.ShapeDtypeStruct((s, d), BF16), compiler_params=_params(),
    )(zg, ya, yb)


def _merge_bwd(dm, zg, ya, yb, *, name, tm=256):
    s, d = ya.shape
    tm = _tile(s, tm)

    def body(dm_ref, zg_ref, ya_ref, yb_ref, dzg_ref, dya_ref, dyb_ref):
        dmv = dm_ref[...].astype(F32)
        ga = _sigmoid(zg_ref[:, :d].astype(F32))
        gb = _sigmoid(zg_ref[:, d:].astype(F32))
        dzg_ref[:, :d] = (dmv * ya_ref[...].astype(F32) * ga * (1.0 - ga)).astype(dzg_ref.dtype)
        dzg_ref[:, d:] = (dmv * yb_ref[...].astype(F32) * gb * (1.0 - gb)).astype(dzg_ref.dtype)
        dya_ref[...] = (dmv * ga).astype(dya_ref.dtype)
        dyb_ref[...] = (dmv * gb).astype(dyb_ref.dtype)

    row = pl.BlockSpec((tm, d), lambda i: (i, 0))
    row2 = pl.BlockSpec((tm, 2 * d), lambda i: (i, 0))
    return pl.pallas_call(
        body, name=name, grid=(s // tm,), in_specs=[row, row2, row, row], out_specs=[row2, row, row],
        out_shape=[jax.ShapeDtypeStruct((s, 2 * d), BF16), jax.ShapeDtypeStruct((s, d), BF16),
                   jax.ShapeDtypeStruct((s, d), BF16)],
        compiler_params=_params(),
    )(dm, zg, ya, yb)


SUBLANES = 8


def _shift_down(u, k, row):
    rolled = pltpu.roll(u, k, 0)
    head = jnp.where(row[:SUBLANES] >= k, rolled[:SUBLANES], 0.0)
    return jnp.concatenate([head, rolled[SUBLANES:]], axis=0)


def _shift_up(u, k, row):
    n = u.shape[0]
    rolled = pltpu.roll(u, n - k, 0)
    tail = jnp.where(row[n - SUBLANES:] < n - k, rolled[n - SUBLANES:], 0.0)
    return jnp.concatenate([rolled[:n - SUBLANES], tail], axis=0)


def _conv_act_fwd(up_a, up_b, cw_a, cw_b, cb_a, cb_b, *, name, tc=128):
    s, f = up_a.shape
    tc = _tile(f, tc)

    def body(ua_ref, ub_ref, wa_ref, wb_ref, ba_ref, bb_ref, act_ref):
        row = lax.broadcasted_iota(jnp.int32, (s, tc), 0)

        def conv(u_ref, w_ref, b_ref):
            u = u_ref[...].astype(F32)
            return (b_ref[...] + w_ref[0:1, :] * _shift_down(u, 2, row)
                    + w_ref[1:2, :] * _shift_down(u, 1, row) + w_ref[2:3, :] * u)

        ca = conv(ua_ref, wa_ref, ba_ref)
        cb = conv(ub_ref, wb_ref, bb_ref)
        act_ref[...] = (_gelu(ca) * cb).astype(act_ref.dtype)

    col = pl.BlockSpec((s, tc), lambda j: (0, j))
    w3 = pl.BlockSpec((3, tc), lambda j: (0, j))
    b1 = pl.BlockSpec((1, tc), lambda j: (0, j))
    return pl.pallas_call(
        body, name=name, grid=(f // tc,), in_specs=[col, col, w3, w3, b1, b1], out_specs=col,
        out_shape=jax.ShapeDtypeStruct((s, f), BF16), compiler_params=_params(),
    )(up_a, up_b, cw_a, cw_b, cb_a, cb_b)


def _conv_act_bwd(up_a, up_b, dact, cw_a, cw_b, cb_a, cb_b, *, name, tc=128):
    s, f = up_a.shape
    tc = _tile(f, tc)

    def body(ua_ref, ub_ref, da_ref, wa_ref, wb_ref, ba_ref, bb_ref, dua_ref, dub_ref, dwa_ref, dwb_ref):
        row = lax.broadcasted_iota(jnp.int32, (s, tc), 0)

        def conv(u_ref, w_ref, b_ref):
            u = u_ref[...].astype(F32)
            u1 = _shift_down(u, 1, row)
            u2 = _shift_down(u, 2, row)
            return u, u1, u2, b_ref[...] + w_ref[0:1, :] * u2 + w_ref[1:2, :] * u1 + w_ref[2:3, :] * u

        def back(dc, taps, w_ref, du_ref, dw_ref):
            u, u1, u2 = taps
            dw_ref[0:1, :] = jnp.sum(dc * u2, axis=0, keepdims=True)
            dw_ref[1:2, :] = jnp.sum(dc * u1, axis=0, keepdims=True)
            dw_ref[2:3, :] = jnp.sum(dc * u, axis=0, keepdims=True)
            dw_ref[3:4, :] = jnp.sum(dc, axis=0, keepdims=True)
            du = (w_ref[2:3, :] * dc + w_ref[1:2, :] * _shift_up(dc, 1, row)
                  + w_ref[0:1, :] * _shift_up(dc, 2, row))
            du_ref[...] = du.astype(du_ref.dtype)

        ua, ua1, ua2, ca = conv(ua_ref, wa_ref, ba_ref)
        ub, ub1, ub2, cb = conv(ub_ref, wb_ref, bb_ref)
        g, dg = _gelu_and_grad(ca)
        dact_v = da_ref[...].astype(F32)
        back(dact_v * cb * dg, (ua, ua1, ua2), wa_ref, dua_ref, dwa_ref)
        back(dact_v * g, (ub, ub1, ub2), wb_ref, dub_ref, dwb_ref)

    col = pl.BlockSpec((s, tc), lambda j: (0, j))
    w3 = pl.BlockSpec((3, tc), lambda j: (0, j))
    w4 = pl.BlockSpec((4, tc), lambda j: (0, j))
    b1 = pl.BlockSpec((1, tc), lambda j: (0, j))
    return pl.pallas_call(
        body, name=name, grid=(f // tc,), in_specs=[col, col, col, w3, w3, b1, b1],
        out_specs=[col, col, w4, w4],
        out_shape=[jax.ShapeDtypeStruct((s, f), BF16), jax.ShapeDtypeStruct((s, f), BF16),
                   jax.ShapeDtypeStruct((4, f), F32), jax.ShapeDtypeStruct((4, f), F32)],
        compiler_params=_params(),
    )(up_a, up_b, dact, cw_a, cw_b, cb_a, cb_b)


def _ple_final(x2, ple, zp, target, g_final, *, name, tm=256):
    s, d = x2.shape
    tm = _tile(s, tm)

    def body(x_ref, ple_ref, zp_ref, t_ref, g_ref, dx_ref, dple_ref, dzp_ref, dg_ref, loss_ref):
        @pl.when(pl.program_id(0) == 0)
        def _():
            dg_ref[...] = jnp.zeros_like(dg_ref)
            loss_ref[...] = jnp.zeros_like(loss_ref)

        gp = _sigmoid(zp_ref[...].astype(F32))
        plev = ple_ref[...].astype(F32)
        x3 = x_ref[...] + plev * gp
        r = lax.rsqrt(jnp.mean(x3 * x3, axis=-1, keepdims=True) + EPS)
        xhat = x3 * r
        gv = g_ref[...]
        diff = xhat * gv - t_ref[...]
        loss_ref[...] += 0.5 * jnp.sum(jnp.mean(diff * diff, axis=-1, keepdims=True), axis=0, keepdims=True)
        dy = diff * (1.0 / d)
        dg_ref[...] += jnp.sum(dy * xhat, axis=0, keepdims=True)
        dyg = dy * gv
        dx3 = r * (dyg - xhat * jnp.mean(dyg * xhat, axis=-1, keepdims=True))
        dx_ref[...] = dx3
        dple_ref[...] = (dx3 * gp).astype(dple_ref.dtype)
        dzp_ref[...] = (dx3 * plev * gp * (1.0 - gp)).astype(dzp_ref.dtype)

    row = pl.BlockSpec((tm, d), lambda i: (i, 0))
    vec = pl.BlockSpec((1, d), lambda i: (0, 0))
    return pl.pallas_call(
        body, name=name, grid=(s // tm,), in_specs=[row, row, row, row, vec],
        out_specs=[row, row, row, vec, pl.BlockSpec((1, LANES), lambda i: (0, 0))],
        out_shape=[jax.ShapeDtypeStruct((s, d), F32), jax.ShapeDtypeStruct((s, d), BF16),
                   jax.ShapeDtypeStruct((s, d), BF16), jax.ShapeDtypeStruct((1, d), F32),
                   jax.ShapeDtypeStruct((1, LANES), F32)],
        compiler_params=_params(),
    )(x2, ple, zp, target, g_final)


def _device_step(x, p, target, w, get_w_in=None, get_w_rest=None, on_grads_ffn=None, on_grads_small=None,
                 on_grads_mix=None):
    s = x.shape[0]
    g = {}
    w = dict(w)

    h = _rms_fwd(x, w["norm_mix_g"], name="rms_mix", dep=w.get("first_dep"))
    if get_w_in is not None:
        w.update(get_w_in(h))
    qkv = _mm(h, w["w_qkv"], mode="nn", out_dtype=BF16, name="proj_qkv", tm=1024)
    f = _mm(h, w["w_f"], mode="nn", out_dtype=F32, name="proj_f", tm=1024)

    cum_b, cum_t = _fox_cum(f, w["b_f"], name="fox_cum")
    cum_b = cum_b.reshape(HEAD_PAIRS, 2, s, LANES)
    cum_r = cum_t[:FOX_HEADS].reshape(HEAD_PAIRS, 2, s)
    b, o_t, lse = _attn_fwd_t(qkv, cum_b, cum_r, name="attn_fwd")

    dep = get_w_rest[0](b) if get_w_rest is not None else None
    z_uv = _mm(h, w["w_uv"], mode="nn", out_dtype=BF16, name="proj_uv", tm=1024, dep=dep)
    zg = _mm(h, w["w_g"], mode="nn", out_dtype=BF16, name="proj_gate", tm=1024, dep=dep)
    a = _gmlp_fwd(z_uv, w["gmlp_ln_g"], w["gmlp_ln_b"], w["gmlp_w_s"], w["gmlp_b_s_t"], name="gmlp_fwd")
    if get_w_rest is not None:
        w.update(get_w_rest[1]([a, zg]))

    ya = _mm(a, w["w_branch_a"], mode="nn", out_dtype=BF16, name="branch_a", tm=1024)
    yb = _mm(b, w["w_branch_b"], mode="nn", out_dtype=BF16, name="branch_b", tm=1024)
    merged = _merge_fwd(zg, ya, yb, name="merge_fwd")
    x1 = _mm(merged, w["w_out"], mode="nn", out_dtype=F32, name="proj_out", add=x, tm=1024)

    h2 = _rms_fwd(x1, w["norm_ffn_g"], name="rms_ffn")
    up_a = _mm(h2, w["w_up_a"], mode="nn", out_dtype=BF16, name="up_a", tm=1024, tn=D_FF // 2)
    up_b = _mm(h2, w["w_up_b"], mode="nn", out_dtype=BF16, name="up_b", tm=1024, tn=D_FF // 2)
    cw, cb = w["conv_w"], w["conv_b"]
    conv_args = (cw[:, :D_FF], cw[:, D_FF:], cb[:, :D_FF], cb[:, D_FF:])
    act = _conv_act_fwd(up_a, up_b, *conv_args, name="conv_act_fwd")
    x2 = _mm(act, w["w_down"], mode="nn", out_dtype=F32, name="down", add=x1, tm=512)

    h3 = _rms_fwd(x2, w["norm_ple_g"], name="rms_ple")
    ple = _mm(p, w["w_ple"], mode="nn", out_dtype=BF16, name="ple_proj", tm=1024)
    zp = _mm(h3, w["w_ple_gate"], mode="nn", out_dtype=BF16, name="ple_gate", tm=1024)
    dx3, dple, dzp, g["norm_final_g"], loss = _ple_final(x2, ple, zp, target, w["norm_final_g"], name="ple_final")

    g["w_ple"] = _mm(p, dple, mode="tn", out_dtype=BF16, name="dw_ple")
    g["w_ple_gate"] = _mm(h3, dzp, mode="tn", out_dtype=BF16, name="dw_ple_gate")
    dh3 = _mm(dzp, w["w_ple_gate"], mode="nt", out_dtype=BF16, name="dh3")
    dx2, dx2_b, g["norm_ple_g"] = _rms_bwd(x2, w["norm_ple_g"], dh3, dx3, name="rms_ple_bwd")

    g["w_down"] = _mm(act, dx2_b, mode="tn", out_dtype=BF16, name="dw_down", tm=D_FF // 2)
    dact = _mm(dx2_b, w["w_down"], mode="nt", out_dtype=BF16, name="dact", tn=D_FF // 2)
    dup_a, dup_b, dcw_a, dcw_b = _conv_act_bwd(up_a, up_b, dact, *conv_args, name="conv_act_bwd")
    g["conv_w"] = jnp.concatenate([dcw_a[:3], dcw_b[:3]], axis=1)
    g["conv_b"] = jnp.concatenate([dcw_a[3:], dcw_b[3:]], axis=1)
    g["w_up_a"] = _mm(h2, dup_a, mode="tn", out_dtype=BF16, name="dw_up_a", tn=D_FF // 2)
    g["w_up_b"] = _mm(h2, dup_b, mode="tn", out_dtype=BF16, name="dw_up_b", tn=D_FF // 2)
    dh2 = _mm_nt_sum([(dup_a, w["w_up_a"]), (dup_b, w["w_up_b"])], out_dtype=BF16, name="dh2")
    dx1, dx1_b, g["norm_ffn_g"] = _rms_bwd(x1, w["norm_ffn_g"], dh2, dx2, name="rms_ffn_bwd")

    g["w_out"] = _mm(merged, dx1_b, mode="tn", out_dtype=BF16, name="dw_out")
    dmerged = _mm(dx1_b, w["w_out"], mode="nt", out_dtype=BF16, name="dmerged")
    dzg, dya, dyb = _merge_bwd(dmerged, zg, ya, yb, name="merge_bwd")
    g["w_branch_a"] = _mm(a, dya, mode="tn", out_dtype=BF16, name="dw_branch_a")
    g["w_branch_b"] = _mm(b, dyb, mode="tn", out_dtype=BF16, name="dw_branch_b")
    dep = on_grads_ffn(g) if on_grads_ffn is not None else None
    da = _mm(dya, w["w_branch_a"], mode="nt", out_dtype=BF16, name="da", dep=dep)
    db = _mm(dyb, w["w_branch_b"], mode="nt", out_dtype=BF16, name="db")

    dz_uv, g["gmlp_w_s"], dbs_t, g["gmlp_ln_g"], g["gmlp_ln_b"] = _gmlp_bwd(
        z_uv, da, w["gmlp_ln_g"], w["gmlp_ln_b"], w["gmlp_w_s"], w["gmlp_b_s_t"], name="gmlp_bwd")
    g["gmlp_b_s"] = dbs_t[:, :GMLP_GROUPS].T
    dep = on_grads_small(g) if on_grads_small is not None else None

    dq, dk, dv, dcum_b = _attn_bwd_t(qkv, db, o_t, lse, cum_b, cum_r, name="attn_bwd", dep=dep)
    dcum_t = jnp.pad(dcum_b[..., 0].reshape(FOX_HEADS, s), ((0, LANES - FOX_HEADS), (0, 0)))
    df, g["b_f"] = _fox_dlogit(dcum_t, f, w["b_f"], name="fox_dlogit")
    dqkv = jnp.concatenate([dq, dk, dv], axis=1)

    g["w_uv"] = _mm(h, dz_uv, mode="tn", out_dtype=BF16, name="dw_uv")
    g["w_qkv"] = _mm(h, dqkv, mode="tn", out_dtype=BF16, name="dw_qkv")
    g["w_f"] = _mm(h, df, mode="tn", out_dtype=BF16, name="dw_f")
    g["w_g"] = _mm(h, dzg, mode="tn", out_dtype=BF16, name="dw_g")
    dep = on_grads_mix(g) if on_grads_mix is not None else None
    dh = _mm_nt_sum([(dz_uv, w["w_uv"]), (dqkv, w["w_qkv"]), (df, w["w_f"]), (dzg, w["w_g"])],
                    out_dtype=BF16, name="dh", dep=dep)
    dx0, _, g["norm_mix_g"] = _rms_bwd(x, w["norm_mix_g"], dh, dx1, name="rms_mix_bwd")
    return loss, dx0, g


def _coords():
    return lax.axis_index("x"), lax.axis_index("y"), lax.axis_index("c")


def _other_chips(x, y):
    return [(1 - x, y), (x, 1 - y), (1 - x, 1 - y)]


def _remote(src, dst, send_sem, recv_sem, dev):
    return pltpu.make_async_remote_copy(src_ref=src, dst_ref=dst, send_sem=send_sem, recv_sem=recv_sem,
                                        device_id=dev, device_id_type=MESH)


_ANY = pl.BlockSpec(memory_space=pl.ANY)


def _gather_weights(halved, whole, *, name):
    nh, n = len(halved), len(halved) + len(whole)
    arrays = list(halved) + list(whole)

    def body(*refs):
        ins, outs = refs[:n], refs[n:2 * n]
        send_sems, recv_sems = refs[2 * n:]
        x, y, c = _coords()
        me, sib = 2 * x + y, (x, y, 1 - c)
        chips = _other_chips(x, y)

        def half(i, which):
            h = ins[i].shape[0] // 2
            return pl.ds(pl.multiple_of(which * h, 16), h)

        sends = []
        for i in range(n):
            src, dst = (ins[i].at[half(i, c)], outs[i].at[me, half(i, c)]) if i < nh else (ins[i], outs[i].at[me])
            for k, (cx, cy) in enumerate(chips):
                cp = _remote(src, dst, send_sems.at[i, k], recv_sems.at[i, k], (cx, cy, c))
                cp.start()
                sends.append(cp)
        for i in range(n):
            for k, (cx, cy) in enumerate(chips):
                got = outs[i].at[2 * cx + cy, half(i, c)] if i < nh else outs[i].at[2 * cx + cy]
                _remote(got, got, send_sems.at[i, k], recv_sems.at[i, k], sib).wait_recv()
                if i < nh:
                    cp = _remote(got, got, send_sems.at[i, 3 + k], recv_sems.at[i, 3 + k], sib)
                    cp.start()
                    sends.append(cp)
        for i in range(nh):
            for k, (cx, cy) in enumerate(chips):
                got = outs[i].at[2 * cx + cy, half(i, 1 - c)]
                _remote(got, got, send_sems.at[i, 3 + k], recv_sems.at[i, 3 + k], sib).wait_recv()
        for cp in sends:
            cp.wait_send()

    outs = pl.pallas_call(
        body, name=name, in_specs=[_ANY] * n, out_specs=[_ANY] * n,
        out_shape=[jax.ShapeDtypeStruct((N_CHIPS,) + a.shape, a.dtype) for a in arrays],
        scratch_shapes=[pltpu.SemaphoreType.DMA((n, 6)), pltpu.SemaphoreType.DMA((n, 6))],
        compiler_params=_params(),
    )(*arrays)
    chip = 2 * lax.axis_index("x") + lax.axis_index("y")
    return [lax.dynamic_update_index_in_dim(o, a, chip, 0) for o, a in zip(outs, arrays)]


def _pair_exchange(gs, *, name):
    n = len(gs)

    def body(*refs):
        ins, outs = refs[:n], refs[n:2 * n]
        send_sems, recv_sems = refs[2 * n:]
        x, y, c = _coords()
        copies = []
        for i in range(n):
            for j in range(N_CHIPS):
                cp = _remote(ins[i].at[j, 1 - c], outs[i].at[j], send_sems.at[i, j], recv_sems.at[i, j], (x, y, 1 - c))
                cp.start()
                copies.append(cp)
        for cp in copies:
            cp.wait()

    return pl.pallas_call(
        body, name=name, in_specs=[_ANY] * n, out_specs=[_ANY] * n,
        out_shape=[jax.ShapeDtypeStruct((N_CHIPS,) + a.shape[2:], a.dtype) for a in gs],
        scratch_shapes=[pltpu.SemaphoreType.DMA((n, N_CHIPS)), pltpu.SemaphoreType.DMA((n, N_CHIPS))],
        compiler_params=_params(),
    )(*gs)


def _chip_exchange(ss, *, name):
    n = len(ss)

    def body(*refs):
        ins, outs = refs[:n], refs[n:2 * n]
        send_sems, recv_sems = refs[2 * n:]
        x, y, c = _coords()
        me = 2 * x + y
        chips = _other_chips(x, y)
        sends = []
        for i in range(n):
            for k, (cx, cy) in enumerate(chips):
                cp = _remote(ins[i].at[2 * cx + cy], outs[i].at[me], send_sems.at[i, k], recv_sems.at[i, k], (cx, cy, c))
                cp.start()
                sends.append(cp)
        for i in range(n):
            for k, (cx, cy) in enumerate(chips):
                got = outs[i].at[2 * cx + cy]
                _remote(got, got, send_sems.at[i, k], recv_sems.at[i, k], (cx, cy, c)).wait_recv()
        for cp in sends:
            cp.wait_send()

    return pl.pallas_call(
        body, name=name, in_specs=[_ANY] * n, out_specs=[_ANY] * n,
        out_shape=[jax.ShapeDtypeStruct(a.shape, a.dtype) for a in ss],
        scratch_shapes=[pltpu.SemaphoreType.DMA((n, 3)), pltpu.SemaphoreType.DMA((n, 3))],
        compiler_params=_params(),
    )(*ss)


def _pair_share(hs, *, name):
    n = len(hs)

    def body(*refs):
        ins, outs = refs[:n], refs[n:2 * n]
        send_sems, recv_sems = refs[2 * n:]
        x, y, c = _coords()
        copies = []
        for i in range(n):
            cp = _remote(ins[i], outs[i], send_sems.at[i], recv_sems.at[i], (x, y, 1 - c))
            cp.start()
            copies.append(cp)
        for cp in copies:
            cp.wait()

    return pl.pallas_call(
        body, name=name, in_specs=[_ANY] * n, out_specs=[_ANY] * n,
        out_shape=[jax.ShapeDtypeStruct(a.shape, a.dtype) for a in hs],
        scratch_shapes=[pltpu.SemaphoreType.DMA((n,)), pltpu.SemaphoreType.DMA((n,))],
        compiler_params=_params(),
    )(*hs)


def _all_exchange(vec, *, name):
    def body(v_ref, o_ref, send_sems, recv_sems, local_sem):
        x, y, c = _coords()
        me = 4 * x + 2 * y + c
        local = pltpu.make_async_copy(v_ref, o_ref.at[me], local_sem)
        local.start()
        copies = []
        k = 0
        for dx in (0, 1):
            for dy in (0, 1):
                for dc in (0, 1):
                    if dx or dy or dc:
                        peer = (1 - x if dx else x, 1 - y if dy else y, 1 - c if dc else c)
                        cp = _remote(v_ref, o_ref.at[me], send_sems.at[k], recv_sems.at[k], peer)
                        cp.start()
                        copies.append(cp)
                        k += 1
        for cp in copies:
            cp.wait()
        local.wait()

    return pl.pallas_call(
        body, name=name, in_specs=[_ANY], out_specs=_ANY,
        out_shape=jax.ShapeDtypeStruct((8,) + vec.shape, vec.dtype),
        scratch_shapes=[pltpu.SemaphoreType.DMA((7,)), pltpu.SemaphoreType.DMA((7,)), pltpu.SemaphoreType.DMA(())],
        compiler_params=_params(),
    )(vec)


_HBM = pl.BlockSpec(memory_space=pltpu.HBM)
_SEM = pl.BlockSpec(memory_space=pltpu.SEMAPHORE)
_EFFECT = pltpu.SideEffectType.DATAFLOW_SIDE_EFFECTING


def _copies_start(srcs, lands, plan, n_copies, *, name, after=()):
    ns, n = len(srcs), len(srcs) + len(lands)
    na = len(after)

    def body(*refs):
        send_sems, recv_sems = refs[n + na], refs[n + na + 1]
        token = refs[-1]
        for k, (src, dst, dev) in enumerate(plan(refs[:ns], refs[ns:n])):
            _remote(src, dst, send_sems.at[k], recv_sems.at[k], dev).start()
        token[...] = jnp.zeros_like(token)

    arrays = list(srcs) + list(lands)
    outs = pl.pallas_call(
        body, name=name,
        out_shape=(pltpu.SemaphoreType.DMA((n_copies,)), pltpu.SemaphoreType.DMA((n_copies,)),
                   *[pltpu.HBM(a.shape, a.dtype) for a in arrays], jax.ShapeDtypeStruct((8, LANES), F32)),
        in_specs=[_HBM] * n + [_ANY] * na,
        out_specs=(_SEM, _SEM, *[_HBM] * n, pl.BlockSpec(memory_space=pltpu.VMEM)),
        input_output_aliases={i: 2 + i for i in range(n)},
        compiler_params=pltpu.CompilerParams(has_side_effects=_EFFECT),
    )(*[pltpu.with_memory_space_constraint(a, pltpu.HBM) for a in arrays], *after)
    return outs[0], outs[1], list(outs[2:2 + ns]), list(outs[2 + ns:2 + n]), outs[-1]


def _copies_wait(send_sems, recv_sems, srcs, lands, plan, first, after, *, name):
    ns, n = len(srcs), len(srcs) + len(lands)

    def body(*refs):
        send, recv = refs[n], refs[n + 1]
        for k, (src, dst, dev) in enumerate(plan(refs[:ns], refs[ns:n])):
            cp = _remote(src, dst, send.at[first + k], recv.at[first + k], dev)
            cp.wait_send()
            cp.wait_recv()

    arrays = list(srcs) + list(lands)
    outs = pl.pallas_call(
        body, name=name, out_shape=tuple(pltpu.HBM(a.shape, a.dtype) for a in arrays),
        in_specs=[_HBM] * n + [_SEM, _SEM] + [_ANY] * len(after), out_specs=tuple([_HBM] * n),
        input_output_aliases={i: i for i in range(n)},
        compiler_params=pltpu.CompilerParams(has_side_effects=_EFFECT),
    )(*arrays, send_sems, recv_sems, *after)
    return list(outs[:ns]), list(outs[ns:])


def _gather_plan(halved):
    def plan(srcs, lands):
        x, y, c = _coords()
        me = 2 * x + y
        out = []
        for i, (src, land) in enumerate(zip(srcs, lands)):
            if halved[i]:
                h = src.shape[0] // 2
                rows = pl.ds(pl.multiple_of(c * h, 16), h)
                src, dst = src.at[rows], land.at[me, rows]
            else:
                dst = land.at[me]
            out += [(src, dst, (cx, cy, c)) for cx, cy in _other_chips(x, y)]
        return out
    return plan


def _forward_halves(lands, *, name):
    n = len(lands)

    def body(*refs):
        ins, outs = refs[:n], refs[n:2 * n]
        send_sems, recv_sems = refs[2 * n:]
        x, y, c = _coords()
        copies = []
        for i in range(n):
            h = ins[i].shape[1] // 2
            rows = pl.ds(pl.multiple_of(c * h, 16), h)
            for k, (cx, cy) in enumerate(_other_chips(x, y)):
                cp = _remote(ins[i].at[2 * cx + cy, rows], outs[i].at[2 * cx + cy, rows],
                             send_sems.at[i, k], recv_sems.at[i, k], (x, y, 1 - c))
                cp.start()
                copies.append(cp)
        for cp in copies:
            cp.wait()

    return pl.pallas_call(
        body, name=name, in_specs=[_ANY] * n, out_specs=[_ANY] * n,
        out_shape=[jax.ShapeDtypeStruct(a.shape, a.dtype) for a in lands],
        input_output_aliases={i: i for i in range(n)},
        scratch_shapes=[pltpu.SemaphoreType.DMA((n, 3)), pltpu.SemaphoreType.DMA((n, 3))],
        compiler_params=_params(),
    )(*lands)


def _forward_plan(srcs, lands):
    x, y, c = _coords()
    out = []
    for land in lands:
        h = land.shape[1] // 2
        rows = pl.ds(pl.multiple_of(c * h, 16), h)
        for cx, cy in _other_chips(x, y):
            view = land.at[2 * cx + cy, rows]
            out.append((view, view, (x, y, 1 - c)))
    return out


def _pair_plan(srcs, lands):
    x, y, c = _coords()
    out = []
    for src, land in zip(srcs, lands):
        out += [(src.at[j, 1 - c], land.at[j], (x, y, 1 - c)) for j in range(N_CHIPS)]
    return out


def _all_plan(srcs, lands):
    x, y, c = _coords()
    me = 4 * x + 2 * y + c
    out = []
    for src, land in zip(srcs, lands):
        for dx in (0, 1):
            for dy in (0, 1):
                for dc in (0, 1):
                    if dx or dy or dc:
                        out.append((src, land.at[me], (1 - x if dx else x, 1 - y if dy else y, 1 - c if dc else c)))
    return out


def _chip_plan(srcs, lands):
    x, y, c = _coords()
    me = 2 * x + y
    out = []
    for src, land in zip(srcs, lands):
        out += [(src.at[2 * cx + cy], land.at[me], (cx, cy, c)) for cx, cy in _other_chips(x, y)]
    return out


ROW_BLOCK_BYTES = 2 * 1024 * 1024


def _rtile(r, pref, mult, row_bytes=None):
    if row_bytes is not None:
        pref = max(pref, ROW_BLOCK_BYTES // row_bytes)
    t = (min(r, pref) // mult) * mult
    while t >= mult:
        if r % t == 0:
            return t
        t -= mult
    return r


def _pair_add(g, recv, core, *, name):
    _, _, r2, cols = g.shape
    tr = _rtile(r2, 256, 16, row_bytes=2 * cols)

    def body(c_ref, g_ref, r_ref, o_ref):
        o_ref[...] = (g_ref[...].astype(F32) + r_ref[...].astype(F32)).astype(o_ref.dtype)

    blk = pl.BlockSpec((None, tr, cols), lambda j, i, c_ref: (j, i, 0))
    return pl.pallas_call(
        body, name=name,
        grid_spec=pltpu.PrefetchScalarGridSpec(
            num_scalar_prefetch=1, grid=(N_CHIPS, r2 // tr),
            in_specs=[pl.BlockSpec((None, None, tr, cols), lambda j, i, c_ref: (j, c_ref[0], i, 0)), blk],
            out_specs=blk),
        out_shape=jax.ShapeDtypeStruct(recv.shape, recv.dtype), compiler_params=_params(),
    )(core, g, recv)


def _sum_slots(a, out_dtype, *, name):
    n, r, cols = a.shape
    whole = n * r * cols * a.dtype.itemsize <= 4 * ROW_BLOCK_BYTES
    tr = r if whole else _rtile(r, 256, 16)

    def body(a_ref, o_ref):
        acc = a_ref[0].astype(F32)
        for j in range(1, n):
            acc = acc + a_ref[j].astype(F32)
        o_ref[...] = acc.astype(o_ref.dtype)

    return pl.pallas_call(
        body, name=name, grid=(r // tr,),
        in_specs=[pl.BlockSpec((n, tr, cols), lambda i: (0, i, 0))],
        out_specs=pl.BlockSpec((tr, cols), lambda i: (i, 0)),
        out_shape=jax.ShapeDtypeStruct((r, cols), out_dtype), compiler_params=_params(),
    )(a)


def _chip_sum(own, recv, chip, *, name):
    _, r2, cols = own.shape
    tr = _rtile(r2, 256, 16, row_bytes=2 * cols)

    def body(chip_ref, own_ref, *rest):
        o_ref = rest[-1]
        acc = None
        for j in range(N_CHIPS):
            term = jnp.where(chip_ref[0] == j, own_ref[...], rest[j][...]).astype(F32)
            acc = term if acc is None else acc + term
        o_ref[...] = acc

    def slot(j):
        return pl.BlockSpec((None, tr, cols),
                            lambda i, chip_ref: (jnp.where(chip_ref[0] == j, (j + 1) % N_CHIPS, j), i, 0))

    return pl.pallas_call(
        body, name=name,
        grid_spec=pltpu.PrefetchScalarGridSpec(
            num_scalar_prefetch=1, grid=(r2 // tr,),
            in_specs=[pl.BlockSpec((None, tr, cols), lambda i, chip_ref: (chip_ref[0], i, 0))]
                     + [slot(j) for j in range(N_CHIPS)],
            out_specs=pl.BlockSpec((tr, cols), lambda i, chip_ref: (i, 0))),
        out_shape=jax.ShapeDtypeStruct((r2, cols), F32), compiler_params=_params(),
    )(chip, own, *([recv] * N_CHIPS))


def _adam_update(w, gv, m, v):
    c1 = 1.0 / (1.0 - ADAM_B1 ** ADAM_STEP)
    c2 = 1.0 / (1.0 - ADAM_B2 ** ADAM_STEP)
    nm = ADAM_B1 * m + (1.0 - ADAM_B1) * gv
    nv = ADAM_B2 * v + (1.0 - ADAM_B2) * gv * gv
    return -ADAM_LR * ((nm * c1) / (jnp.sqrt(nv * c2) + ADAM_EPS) + ADAM_WD * w), nm, nv


def _adamw_halves(w, g_mine, g_other, m, v, core, *, name):
    r, cols = w.shape
    r2 = r // 2
    tr = _rtile(r2, 256, 8, row_bytes=4 * cols)
    nt = r2 // tr

    def body(core_ref, w_ref, gm_ref, go_ref, m_ref, v_ref, g_ref, d_ref, nm_ref, nv_ref):
        gv = jnp.where(pl.program_id(0) == core_ref[0], gm_ref[...], go_ref[...])
        g_ref[...] = gv
        d_ref[...], nm_ref[...], nv_ref[...] = _adam_update(w_ref[...], gv, m_ref[...], v_ref[...])

    full = pl.BlockSpec((tr, cols), lambda hf, i, core_ref: (hf * nt + i, 0))
    half = pl.BlockSpec((tr, cols), lambda hf, i, core_ref: (i, 0))
    shape = jax.ShapeDtypeStruct((r, cols), F32)
    return pl.pallas_call(
        body, name=name,
        grid_spec=pltpu.PrefetchScalarGridSpec(
            num_scalar_prefetch=1, grid=(2, nt), in_specs=[full, half, half, full, full], out_specs=[full] * 4),
        out_shape=[shape] * 4, compiler_params=_params(),
    )(core, w, g_mine, g_other, m, v)


def _adamw_split_rows(w, g_mine, g_other, m, v, core, *, name, tc=256):
    r, cols = w.shape
    r2 = g_mine.shape[0]
    tc = _tile(cols, tc)

    def body(core_ref, w_ref, gm_ref, go_ref, m_ref, v_ref, g_ref, d_ref, nm_ref, nv_ref):
        mine_first = core_ref[0] == 0
        for lo, hi, first in ((0, r2, True), (r2, r, False)):
            n = hi - lo
            gm, go = gm_ref[0:n, :], go_ref[0:n, :]
            gv = jnp.where(mine_first, gm, go) if first else jnp.where(mine_first, go, gm)
            g_ref[lo:hi, :] = gv
            d_ref[lo:hi, :], nm_ref[lo:hi, :], nv_ref[lo:hi, :] = _adam_update(
                w_ref[lo:hi, :], gv, m_ref[lo:hi, :], v_ref[lo:hi, :])

    full = pl.BlockSpec((r, tc), lambda j, core_ref: (0, j))
    half = pl.BlockSpec((r2, tc), lambda j, core_ref: (0, j))
    shape = jax.ShapeDtypeStruct((r, cols), F32)
    return pl.pallas_call(
        body, name=name,
        grid_spec=pltpu.PrefetchScalarGridSpec(
            num_scalar_prefetch=1, grid=(cols // tc,), in_specs=[full, half, half, full, full],
            out_specs=[full] * 4),
        out_shape=[shape] * 4, compiler_params=_params(),
    )(core, w, g_mine, g_other, m, v)


def _adamw(w, g, m, v, *, name, rows=256):
    r, cols = w.shape
    tr = _rtile(r, rows, 8)

    def body(w_ref, g_ref, m_ref, v_ref, d_ref, nm_ref, nv_ref):
        d_ref[...], nm_ref[...], nv_ref[...] = _adam_update(w_ref[...], g_ref[...], m_ref[...], v_ref[...])

    blk = pl.BlockSpec((tr, cols), lambda i: (i, 0))
    shape = jax.ShapeDtypeStruct((r, cols), F32)
    return pl.pallas_call(
        body, name=name, grid=(r // tr,), in_specs=[blk] * 4, out_specs=[blk] * 3,
        out_shape=[shape] * 3, compiler_params=_params(),
    )(w, g, m, v)


_BIG = (("w_in", 1), ("w_branch_a", 0), ("w_branch_b", 0), ("w_out", 0), ("w_up", 1), ("w_down", 0),
        ("w_ple", 1), ("w_ple_gate", 0))
_SMALL = ("gmlp_ln_g", "gmlp_ln_b", "gmlp_w_s", "gmlp_b_s", "norm_ffn_g", "conv_b", "norm_ple_g", "norm_final_g",
          "b_f", "norm_mix_g")
N_LATE = 2
_WEIGHTS = ("norm_mix_g", "w_in", "b_f", "gmlp_ln_g", "gmlp_ln_b", "gmlp_w_s", "gmlp_b_s", "w_branch_a",
            "w_branch_b", "w_out", "norm_ffn_g", "w_up", "conv_w", "conv_b", "w_down", "norm_ple_g", "w_ple",
            "w_ple_gate", "norm_final_g")
_PACK_ROWS = 8


def _pack(arrays):
    parts = []
    for a in arrays:
        flat = a.reshape(-1)
        unit = _PACK_ROWS * LANES
        flat = jnp.pad(flat, (0, (-flat.shape[0]) % unit))
        parts.append(flat.reshape(-1, LANES))
    return jnp.concatenate(parts, axis=0)


def _unpack(packed, shapes):
    out, row = [], 0
    for shp in shapes:
        size = math.prod(shp)
        rows = -(-size // (_PACK_ROWS * LANES)) * _PACK_ROWS
        out.append(packed[row:row + rows].reshape(-1)[:size].reshape(shp))
        row += rows
    return out


def _take_cols(parts, lo, hi):
    out, start = [], 0
    for a in parts:
        width = a.shape[1]
        a0, a1 = max(lo, start) - start, min(hi, start + width) - start
        if a1 > a0:
            out.append(a if (a0, a1) == (0, width) else a[:, a0:a1])
        start += width
    return out[0] if len(out) == 1 else jnp.concatenate(out, axis=1)


def _take_rows(parts, lo, hi):
    out, start = [], 0
    for a in parts:
        height = a.shape[0]
        a0, a1 = max(lo, start) - start, min(hi, start + height) - start
        if a1 > a0:
            out.append(a if (a0, a1) == (0, height) else a[a0:a1])
        start += height
    return out[0] if len(out) == 1 else jnp.concatenate(out, axis=0)


def _assemble(gathered, axis):
    n, r, cols = gathered.shape
    if axis == 0:
        return gathered.reshape(n * r, cols)
    return _take_cols([gathered[j] for j in range(n)], 0, n * cols)


def _to_chunks(parts, axis):
    rows, total = parts[0].shape[0], sum(a.shape[1] for a in parts)
    if axis == 0:
        r, cols = rows // N_CHIPS, total
        chunks = _take_cols(parts, 0, total).reshape(N_CHIPS, r, cols)
    else:
        r, cols = rows, total // N_CHIPS
        chunks = jnp.stack([_take_cols(parts, j * cols, (j + 1) * cols) for j in range(N_CHIPS)])
    return chunks.reshape(N_CHIPS, 2, r // 2, cols)


def kernel(x, p, norm_mix_g, w_in, b_f, gmlp_ln_g, gmlp_ln_b, gmlp_w_s, gmlp_b_s, w_branch_a, w_branch_b, w_out, norm_ffn_g, w_up, conv_w, conv_b, w_down, norm_ple_g, w_ple, w_ple_gate, norm_final_g, loss_target, m_norm_mix_g, m_w_in, m_b_f, m_gmlp_ln_g, m_gmlp_ln_b, m_gmlp_w_s, m_gmlp_b_s, m_w_branch_a, m_w_branch_b, m_w_out, m_norm_ffn_g, m_w_up, m_conv_w, m_conv_b, m_w_down, m_norm_ple_g, m_w_ple, m_w_ple_gate, m_norm_final_g, v_norm_mix_g, v_w_in, v_b_f, v_gmlp_ln_g, v_gmlp_ln_b, v_gmlp_w_s, v_gmlp_b_s, v_w_branch_a, v_w_branch_b, v_w_out, v_norm_ffn_g, v_w_up, v_conv_w, v_conv_b, v_w_down, v_norm_ple_g, v_w_ple, v_w_ple_gate, v_norm_final_g):
    args = dict(locals())
    wt = {n: args[n] for n in _WEIGHTS}
    mom = {n: args["m_" + n] for n in _WEIGHTS}
    var = {n: args["v_" + n] for n in _WEIGHTS}
    chip = 2 * lax.axis_index("x") + lax.axis_index("y")
    core = lax.axis_index("c").astype(jnp.int32).reshape(1)

    chip1 = chip.astype(jnp.int32).reshape(1)
    device = 2 * chip + lax.axis_index("c")
    axis_of = dict(_BIG)
    names = [n for n, _ in _BIG]
    put_mine = lambda land, mine: lax.dynamic_update_index_in_dim(land, mine, chip, 0)

    shard_in = w_in[0].astype(BF16)
    sems_in = _copies_start([shard_in], [lax.empty((N_CHIPS,) + shard_in.shape, BF16)], _gather_plan([True]), 3,
                            name="gather_start_in")
    _, wt["w_in"], mom["w_in"], var["w_in"] = lax.optimization_barrier((sems_in[4], w_in, m_w_in, v_w_in))
    shards = [wt[n][0].astype(BF16) for n in names[1:]] + [conv_w[0]]
    halved = [True] * len(names[1:]) + [False]
    lands = [lax.empty((N_CHIPS,) + a.shape, a.dtype) for a in shards]
    send_sems, recv_sems, srcs, lands, rest_token = _copies_start(
        shards, lands, _gather_plan(halved), 3 * len(shards), name="gather_start_rest", after=[sems_in[4]])
    o1 = 2 * GMLP_WIDTH
    o2 = o1 + 3 * FOX_WIDTH
    o3 = o2 + FOX_HEADS
    fpad = ((0, 0), (0, LANES - FOX_HEADS))
    w = {
        "conv_b": conv_b, "norm_mix_g": norm_mix_g, "norm_ffn_g": norm_ffn_g, "norm_ple_g": norm_ple_g,
        "norm_final_g": norm_final_g.reshape(1, D_MODEL), "b_f": jnp.pad(b_f, fpad),
        "gmlp_ln_g": gmlp_ln_g, "gmlp_ln_b": gmlp_ln_b, "gmlp_w_s": gmlp_w_s[0],
        "gmlp_b_s_t": jnp.pad(gmlp_b_s[0].T, ((0, 0), (0, LANES - GMLP_GROUPS))),
        "first_dep": rest_token,
    }

    def get_w_in(after):
        early = [a.reshape(a.shape[-2:]) for a in (wt["w_in"], mom["w_in"], var["w_in"])]
        _, got = _copies_wait(sems_in[0], sems_in[1], sems_in[2], sems_in[3], _gather_plan([True]), 0,
                              [after] + early, name="gather_wait_in")
        got = _forward_halves(got, name="gather_forward_in")
        slots = put_mine(got[0], shard_in)
        slots = [slots[j] for j in range(N_CHIPS)]
        return {"w_uv": _take_cols(slots, 0, o1), "w_qkv": _take_cols(slots, o1, o2),
                "w_f": jnp.pad(_take_cols(slots, o2, o3), fpad), "w_g": _take_cols(slots, o3, o3 + 2 * D_MODEL)}

    def start_w_rest(after):
        _, got = _copies_wait(send_sems, recv_sems, srcs, lands, _gather_plan(halved), 0, [after],
                              name="gather_wait_rest")
        ssem, rsem, _, fwd, token = _copies_start([], got[:-1], _forward_plan, 3 * len(got[:-1]),
                                                  name="gather_forward_start")
        pending["forward"] = (ssem, rsem, fwd, got[-1])
        return token

    def get_w_rest(after):
        ssem, rsem, fwd, whole = pending["forward"]
        _, fwd = _copies_wait(ssem, rsem, [], fwd, _forward_plan, 0, after, name="gather_forward_wait")
        got = fwd + [whole]
        slots = {n: put_mine(got[i], shards[i]) for i, n in enumerate(names[1:])}
        full = {n: _assemble(slots[n], axis_of[n]) for n in names[1:] if n != "w_up"}
        up = [slots["w_up"][j] for j in range(N_CHIPS)]
        return {"w_branch_a": full["w_branch_a"], "w_branch_b": full["w_branch_b"], "w_out": full["w_out"],
                "w_up_a": _take_cols(up, 0, D_FF), "w_up_b": _take_cols(up, D_FF, 2 * D_FF),
                "w_down": full["w_down"], "w_ple": full["w_ple"], "w_ple_gate": full["w_ple_gate"],
                "conv_w": _assemble(put_mine(got[-1], shards[-1]), 1)}

    grads, delta, new_m, new_v = {}, {}, {}, {}
    pending = {}

    def to_chunks(n, gr):
        return _to_chunks(gr if isinstance(gr, list) else [gr], axis_of[n])

    def pair_start(group, gfull, tag):
        chunks = [to_chunks(n, gfull[n]) for n in group]
        empty = [lax.empty((N_CHIPS,) + a.shape[2:], a.dtype) for a in chunks]
        ssem, rsem, own, recv, token = _copies_start(chunks, empty, _pair_plan, N_CHIPS * len(group),
                                                     name="grad_pair_start_" + tag)
        pending["pair_" + tag] = (ssem, rsem, own, recv)
        return token

    def reduce_start(group, gfull, tag, after=None):
        if after is None:
            chunks = [to_chunks(n, gfull[n]) for n in group]
            from_sibling = _pair_exchange(chunks, name="grad_pair_exchange_" + tag)
        else:
            ssem, rsem, own, recv = pending["pair_" + tag]
            chunks, from_sibling = _copies_wait(ssem, rsem, own, recv, _pair_plan, 0, after,
                                                name="grad_pair_wait_" + tag)
        pair_sums = [_pair_add(chunks[i], from_sibling[i], core, name="grad_pair_add_" + n) for i, n in enumerate(group)]
        empty = [lax.empty(a.shape, a.dtype) for a in pair_sums]
        ssem, rsem, own, recv, token = _copies_start(pair_sums, empty, _chip_plan, 3 * len(group),
                                                     name="grad_chip_start_" + tag)
        pending[tag] = (ssem, rsem, own, recv)
        return token

    def reduce_finish(group, tag, after):
        ssem, rsem, own, recv = pending[tag]
        own, recv = _copies_wait(ssem, rsem, own, recv, _chip_plan, 0, after, name="grad_chip_wait_" + tag)
        halves = [_chip_sum(own[i], recv[i], chip1, name="grad_chip_sum_" + n) for i, n in enumerate(group)]
        other_halves = _pair_share(halves, name="grad_pair_share_" + tag)
        for i, n in enumerate(group):
            shp = wt[n].shape
            outs = _adamw_halves(wt[n].reshape(shp[-2:]), halves[i], other_halves[i], mom[n].reshape(shp[-2:]),
                                 var[n].reshape(shp[-2:]), core, name="adamw_" + n)
            grads[n], delta[n], new_m[n], new_v[n] = (o.reshape(shp) for o in outs)
        return new_v[group[-1]]

    ffn_group = ("w_up", "w_down", "w_ple", "w_ple_gate", "w_branch_a", "w_branch_b", "w_out")
    mix_group = ("w_in",)

    def on_grads_ffn(g):
        gfull = dict(g)
        gfull["w_up"] = [g["w_up_a"], g["w_up_b"]]
        return pair_start(ffn_group, gfull, "ffn")

    def on_grads_small(g):
        chip_token = reduce_start(ffn_group, None, "ffn", after=[g["gmlp_w_s"]])
        vec = _pack([g[n] for n in _SMALL[:-N_LATE]] + [g["conv_w"]])
        ssem, rsem, own, recv, token = _copies_start(
            [vec], [lax.empty((8,) + vec.shape, F32)], _all_plan, 7, name="small_start", after=[chip_token])
        pending["small"] = (ssem, rsem, own, recv)
        return token

    def on_grads_mix(g):
        gfull = dict(g)
        gfull["w_in"] = [g["w_uv"], g["w_qkv"], g["w_f"][:, :FOX_HEADS], g["w_g"]]
        token = reduce_start(mix_group, gfull, "mix")
        pending["ffn_done"] = reduce_finish(ffn_group, "ffn", [token])
        return token

    loss, grad_x, g = _device_step(x[0], p[0, 0], loss_target[0], w, get_w_in, (start_w_rest, get_w_rest), on_grads_ffn,
                                   on_grads_small, on_grads_mix)

    mix_done = reduce_finish(mix_group, "mix", [grad_x, pending["ffn_done"]])
    ssem, rsem, own, recv = pending["small"]
    own, recv = _copies_wait(ssem, rsem, own, recv, _all_plan, 0, [mix_done], name="small_wait")
    vec_early = _sum_slots(lax.dynamic_update_index_in_dim(recv[0], own[0], device, 0), F32, name="small_sum")
    vec_late = _pack([g["b_f"][:, :FOX_HEADS], g["norm_mix_g"]])
    vec_late = _sum_slots(_all_exchange(vec_late, name="small_exchange_late"), F32, name="small_sum_late")
    early_rows = _pack([wt[n] for n in _SMALL[:-N_LATE]]).shape[0]
    vec = jnp.concatenate([vec_early[:early_rows], vec_late], axis=0)
    for n, a in zip(_SMALL, _unpack(vec, [wt[n].shape for n in _SMALL])):
        grads[n] = a
    conv_w_grad = _unpack(vec_early[early_rows:], [(3, 2 * D_FF)])[0]
    grads["conv_w"] = lax.dynamic_slice_in_dim(conv_w_grad, chip * conv_w.shape[2], conv_w.shape[2], axis=1).reshape(conv_w.shape)

    shp = conv_w.shape
    outs = _adamw(conv_w.reshape(shp[-2:]), grads["conv_w"].reshape(shp[-2:]), m_conv_w.reshape(shp[-2:]),
                  v_conv_w.reshape(shp[-2:]), name="adamw_conv_w")
    delta["conv_w"], new_m["conv_w"], new_v["conv_w"] = (o.reshape(shp) for o in outs)
    outs = _adamw(_pack([wt[n] for n in _SMALL]), vec, _pack([mom[n] for n in _SMALL]),
                  _pack([var[n] for n in _SMALL]), name="adamw_small", rows=2048)
    for d, o in zip((delta, new_m, new_v), outs):
        for n, a in zip(_SMALL, _unpack(o, [wt[n].shape for n in _SMALL])):
            d[n] = a

    total_loss = lax.psum(loss[0, 0], ("x", "y", "c"))
    return (total_loss, grad_x.reshape(x.shape), *[grads[n] for n in _WEIGHTS], *[delta[n] for n in _WEIGHTS],
            *[new_m[n] for n in _WEIGHTS], *[new_v[n] for n in _WEIGHTS])
```

```python
import functools
import math

import jax
import jax.numpy as jnp
from jax import lax
from jax.experimental import pallas as pl
from jax.experimental.pallas import tpu as pltpu

F32 = jnp.float32
BF16 = jnp.bfloat16

D_MODEL = 1024
EPS = 1e-6
CHUNK = 64
GMLP_GROUPS = 8
GMLP_BLOCK = 128
GMLP_WIDTH = 1024
FOX_HEADS = 16
FOX_HEAD_DIM = 64
FOX_WIDTH = 1024
HEAD_PAIRS = FOX_HEADS // 2
ATT_BLOCK = 128
D_FF = 2816
PLE_DIM = 256
LANES = 128
BF16_TILE_ROWS = 16
N_CHIPS = 4

ADAM_LR = 0.001
ADAM_B1 = 0.9
ADAM_B2 = 0.999
ADAM_EPS = 1e-08
ADAM_WD = 0.01
ADAM_STEP = 10

VMEM_LIMIT = 56 * 1024 * 1024
MESH = pl.DeviceIdType.MESH

_NN = (((1,), (0,)), ((), ()))
_NT = (((1,), (1,)), ((), ()))
_TN = (((0,), (0,)), ((), ()))


def _params(**kw):
    return pltpu.CompilerParams(vmem_limit_bytes=VMEM_LIMIT, **kw)


def _tile(dim, pref):
    if dim <= pref:
        return dim
    t = (pref // LANES) * LANES
    while t >= LANES:
        if dim % t == 0:
            return t
        t -= LANES
    return dim


def _dot(a, b, dn):
    return lax.dot_general(a.astype(BF16), b.astype(BF16), dn, preferred_element_type=F32)


def _gelu(x):
    c = math.sqrt(2.0 / math.pi)
    t = jnp.tanh(c * (x + 0.044715 * x * x * x))
    return 0.5 * x * (1.0 + t)


def _gelu_and_grad(x):
    c = math.sqrt(2.0 / math.pi)
    x2 = x * x
    t = jnp.tanh(c * (x + 0.044715 * x2 * x))
    g = 0.5 * x * (1.0 + t)
    dg = 0.5 * (1.0 + t) + 0.5 * x * (1.0 - t * t) * c * (1.0 + 3.0 * 0.044715 * x2)
    return g, dg


def _sigmoid(x):
    return 1.0 / (1.0 + jnp.exp(-x))


def _mm(a, b, *, mode, out_dtype, name, add=None, tm=512, tn=512, dep=None):
    if mode == "nn":
        m, k = a.shape
        k2, n = b.shape
    elif mode == "nt":
        m, k = a.shape
        n, k2 = b.shape
    else:
        k, m = a.shape
        k2, n = b.shape
    assert k == k2, (name, a.shape, b.shape)
    tm = _tile(m, tm)
    tn = _tile(n, tn)
    dn = {"nn": _NN, "nt": _NT, "tn": _TN}[mode]

    def body(a_ref, b_ref, *rest):
        o_ref = rest[-1]
        acc = _dot(a_ref[...], b_ref[...], dn)
        if add is not None:
            acc = acc + rest[0][...].astype(F32)
        o_ref[...] = acc.astype(o_ref.dtype)

    a_spec = pl.BlockSpec((k, tm), lambda i, j: (0, i)) if mode == "tn" else pl.BlockSpec((tm, k), lambda i, j: (i, 0))
    b_spec = pl.BlockSpec((tn, k), lambda i, j: (j, 0)) if mode == "nt" else pl.BlockSpec((k, tn), lambda i, j: (0, j))
    o_spec = pl.BlockSpec((tm, tn), lambda i, j: (i, j))
    in_specs = [a_spec, b_spec]
    args = [a, b]
    if add is not None:
        in_specs.append(o_spec)
        args.append(add)
    if dep is not None:
        in_specs.append(pl.BlockSpec(memory_space=pl.ANY))
        args.append(dep)
    return pl.pallas_call(
        body, name=name, grid=(m // tm, n // tn), in_specs=in_specs, out_specs=o_spec,
        out_shape=jax.ShapeDtypeStruct((m, n), out_dtype), compiler_params=_params(),
    )(*args)


def _mm_nt_sum(pairs, *, out_dtype, name, tm=256, dep=None):
    m, n = pairs[0][0].shape[0], pairs[0][1].shape[0]
    tm = _tile(m, tm)
    np_ = len(pairs)

    def body(*refs):
        o_ref = refs[-1] if dep is None else refs[-1]
        acc = None
        for p in range(np_):
            part = _dot(refs[2 * p][...], refs[2 * p + 1][...], _NT)
            acc = part if acc is None else acc + part
        o_ref[...] = acc.astype(o_ref.dtype)

    in_specs, args = [], []
    for a, b in pairs:
        assert a.shape[0] == m and b.shape[0] == n and a.shape[1] == b.shape[1], (name, a.shape, b.shape)
        in_specs += [pl.BlockSpec((tm, a.shape[1]), lambda i: (i, 0)), pl.BlockSpec(b.shape, lambda i: (0, 0))]
        args += [a, b]
    if dep is not None:
        in_specs.append(pl.BlockSpec(memory_space=pl.ANY))
        args.append(dep)
    return pl.pallas_call(
        body, name=name, grid=(m // tm,), in_specs=in_specs, out_specs=pl.BlockSpec((tm, n), lambda i: (i, 0)),
        out_shape=jax.ShapeDtypeStruct((m, n), out_dtype), compiler_params=_params(),
    )(*args)


def _rms_fwd(x, g, *, name, tm=256, dep=None):
    s, d = x.shape
    tm = _tile(s, tm)

    def body(x_ref, g_ref, *rest):
        h_ref = rest[-1]
        xv = x_ref[...]
        r = lax.rsqrt(jnp.mean(xv * xv, axis=-1, keepdims=True) + EPS)
        h_ref[...] = (xv * r * g_ref[...]).astype(h_ref.dtype)

    deps = [] if dep is None else [dep]
    return pl.pallas_call(
        body, name=name, grid=(s // tm,),
        in_specs=[pl.BlockSpec((tm, d), lambda i: (i, 0)), pl.BlockSpec((1, d), lambda i: (0, 0))]
                 + [pl.BlockSpec(memory_space=pl.ANY)] * len(deps),
        out_specs=pl.BlockSpec((tm, d), lambda i: (i, 0)),
        out_shape=jax.ShapeDtypeStruct((s, d), BF16), compiler_params=_params(),
    )(x, g, *deps)


def _rms_bwd(x, g, dh, dres, *, name, tm=256):
    s, d = x.shape
    tm = _tile(s, tm)

    def body(x_ref, g_ref, dh_ref, dres_ref, dx_ref, dxb_ref, dg_ref):
        xv = x_ref[...]
        r = lax.rsqrt(jnp.mean(xv * xv, axis=-1, keepdims=True) + EPS)
        xhat = xv * r
        dhv = dh_ref[...].astype(F32)
        dyg = dhv * g_ref[...]
        dx = dres_ref[...] + r * (dyg - xhat * jnp.mean(dyg * xhat, axis=-1, keepdims=True))
        dx_ref[...] = dx
        dxb_ref[...] = dx.astype(dxb_ref.dtype)

        @pl.when(pl.program_id(0) == 0)
        def _():
            dg_ref[...] = jnp.zeros_like(dg_ref)

        dg_ref[...] += jnp.sum(dhv * xhat, axis=0, keepdims=True)

    row = pl.BlockSpec((tm, d), lambda i: (i, 0))
    vec = pl.BlockSpec((1, d), lambda i: (0, 0))
    return pl.pallas_call(
        body, name=name, grid=(s // tm,), in_specs=[row, vec, row, row], out_specs=[row, row, vec],
        out_shape=[jax.ShapeDtypeStruct((s, d), F32), jax.ShapeDtypeStruct((s, d), BF16),
                   jax.ShapeDtypeStruct((1, d), F32)],
        compiler_params=_params(),
    )(x, g, dh, dres)


def _gmlp_mask():
    t = lax.broadcasted_iota(jnp.int32, (GMLP_BLOCK, GMLP_BLOCK), 0)
    s_ = lax.broadcasted_iota(jnp.int32, (GMLP_BLOCK, GMLP_BLOCK), 1)
    return (s_ // CHUNK) <= (t // CHUNK)


def _gmlp_norm(zv, ln_g, ln_b):
    vv, dvv = _gelu_and_grad(zv)
    mu = jnp.mean(vv, axis=-1, keepdims=True)
    xc = vv - mu
    rstd = lax.rsqrt(jnp.mean(xc * xc, axis=-1, keepdims=True) + EPS)
    vhat = xc * rstd
    return vhat * ln_g + ln_b, vhat, rstd, dvv


def _gmlp_fwd(z_uv, ln_g, ln_b, w_s, b_s_t, *, name):
    s = z_uv.shape[0]
    w = GMLP_WIDTH
    gd = w // GMLP_GROUPS

    def body(z_ref, lg_ref, lb_ref, ws_ref, bs_ref, a_ref):
        u = _gelu(z_ref[:, :w].astype(F32))
        vn, _, _, _ = _gmlp_norm(z_ref[:, w:].astype(F32), lg_ref[...], lb_ref[...])
        mask = _gmlp_mask()
        for g in range(GMLP_GROUPS):
            wm = jnp.where(mask, ws_ref[g], 0.0)
            mixed = _dot(wm, vn[:, g * gd:(g + 1) * gd], _NN) + bs_ref[:, g:g + 1]
            a_ref[:, g * gd:(g + 1) * gd] = (u[:, g * gd:(g + 1) * gd] * mixed).astype(a_ref.dtype)

    full = lambda shape: pl.BlockSpec(shape, lambda i: (0,) * len(shape))
    return pl.pallas_call(
        body, name=name, grid=(s // GMLP_BLOCK,),
        in_specs=[pl.BlockSpec((GMLP_BLOCK, 2 * w), lambda i: (i, 0)), full((1, w)), full((1, w)),
                  full((GMLP_GROUPS, GMLP_BLOCK, GMLP_BLOCK)), full((GMLP_BLOCK, LANES))],
        out_specs=pl.BlockSpec((GMLP_BLOCK, w), lambda i: (i, 0)),
        out_shape=jax.ShapeDtypeStruct((s, w), BF16), compiler_params=_params(),
    )(z_uv, ln_g, ln_b, w_s, b_s_t)


def _gmlp_bwd(z_uv, da, ln_g, ln_b, w_s, b_s_t, *, name):
    s = z_uv.shape[0]
    w = GMLP_WIDTH
    gd = w // GMLP_GROUPS

    def body(z_ref, da_ref, lg_ref, lb_ref, ws_ref, bs_ref, dz_ref, dws_ref, dbs_ref, dlg_ref, dlb_ref):
        @pl.when(pl.program_id(0) == 0)
        def _():
            dws_ref[...] = jnp.zeros_like(dws_ref)
            dbs_ref[...] = jnp.zeros_like(dbs_ref)
            dlg_ref[...] = jnp.zeros_like(dlg_ref)
            dlb_ref[...] = jnp.zeros_like(dlb_ref)

        u, du_dz = _gelu_and_grad(z_ref[:, :w].astype(F32))
        lg = lg_ref[...]
        vn, vhat, rstd, dvv_dz = _gmlp_norm(z_ref[:, w:].astype(F32), lg, lb_ref[...])
        dav = da_ref[...].astype(F32)
        mask = _gmlp_mask()
        lane = lax.broadcasted_iota(jnp.int32, (GMLP_BLOCK, LANES), 1)
        dvn_parts = []
        dbs = jnp.zeros((GMLP_BLOCK, LANES), F32)
        for g in range(GMLP_GROUPS):
            sl = slice(g * gd, (g + 1) * gd)
            wm = jnp.where(mask, ws_ref[g], 0.0)
            vn_g = vn[:, sl]
            mixed = _dot(wm, vn_g, _NN) + bs_ref[:, g:g + 1]
            dmixed = dav[:, sl] * u[:, sl]
            dz_ref[:, sl] = (dav[:, sl] * mixed * du_dz[:, sl]).astype(dz_ref.dtype)
            dvn_parts.append(_dot(wm, dmixed, _TN))
            dws_ref[g] += jnp.where(mask, _dot(dmixed, vn_g, _NT), 0.0)
            dbs = dbs + jnp.where(lane == g, jnp.sum(dmixed, axis=-1, keepdims=True), 0.0)
        dbs_ref[...] += dbs
        dvn = jnp.concatenate(dvn_parts, axis=-1)
        dlg_ref[...] += jnp.sum(dvn * vhat, axis=0, keepdims=True)
        dlb_ref[...] += jnp.sum(dvn, axis=0, keepdims=True)
        dyg = dvn * lg
        dvv = rstd * (dyg - jnp.mean(dyg, axis=-1, keepdims=True)
                      - vhat * jnp.mean(dyg * vhat, axis=-1, keepdims=True))
        dz_ref[:, w:] = (dvv * dvv_dz).astype(dz_ref.dtype)

    full = lambda shape: pl.BlockSpec(shape, lambda i: (0,) * len(shape))
    return pl.pallas_call(
        body, name=name, grid=(s // GMLP_BLOCK,),
        in_specs=[pl.BlockSpec((GMLP_BLOCK, 2 * w), lambda i: (i, 0)),
                  pl.BlockSpec((GMLP_BLOCK, w), lambda i: (i, 0)), full((1, w)), full((1, w)),
                  full((GMLP_GROUPS, GMLP_BLOCK, GMLP_BLOCK)), full((GMLP_BLOCK, LANES))],
        out_specs=[pl.BlockSpec((GMLP_BLOCK, 2 * w), lambda i: (i, 0)),
                   full((GMLP_GROUPS, GMLP_BLOCK, GMLP_BLOCK)), full((GMLP_BLOCK, LANES)),
                   full((1, w)), full((1, w))],
        out_shape=[jax.ShapeDtypeStruct((s, 2 * w), BF16),
                   jax.ShapeDtypeStruct((GMLP_GROUPS, GMLP_BLOCK, GMLP_BLOCK), F32),
                   jax.ShapeDtypeStruct((GMLP_BLOCK, LANES), F32),
                   jax.ShapeDtypeStruct((1, w), F32), jax.ShapeDtypeStruct((1, w), F32)],
        compiler_params=_params(),
    )(z_uv, da, ln_g, ln_b, w_s, b_s_t)


def _tri(lower):
    r = lax.broadcasted_iota(jnp.int32, (ATT_BLOCK, ATT_BLOCK), 0)
    c = lax.broadcasted_iota(jnp.int32, (ATT_BLOCK, ATT_BLOCK), 1)
    return jnp.where((c <= r) if lower else (c >= r), 1.0, 0.0).astype(F32)


def _log_sigmoid(x):
    return jnp.minimum(x, 0.0) - jnp.log(1.0 + jnp.exp(-jnp.abs(x)))


def _fox_cum(f, b_f, *, name):
    s = f.shape[0]
    nb = s // ATT_BLOCK

    def body(f_ref, b_ref, cb_ref, ct_ref, carry):
        @pl.when(pl.program_id(0) == 0)
        def _():
            carry[...] = jnp.zeros_like(carry)

        lf = _log_sigmoid(f_ref[...] + b_ref[...])
        cum = lax.dot_general(_tri(True), lf, _NN, precision=lax.Precision.HIGHEST,
                              preferred_element_type=F32) + carry[...]
        carry[...] = cum[ATT_BLOCK - 1:ATT_BLOCK, :]
        for h in range(FOX_HEADS):
            cb_ref[h] = jnp.broadcast_to(cum[:, h:h + 1], (ATT_BLOCK, LANES))
        ct_ref[...] = cum.T

    return pl.pallas_call(
        body, name=name, grid=(nb,),
        in_specs=[pl.BlockSpec((ATT_BLOCK, LANES), lambda i: (i, 0)), pl.BlockSpec((1, LANES), lambda i: (0, 0))],
        out_specs=[pl.BlockSpec((FOX_HEADS, ATT_BLOCK, LANES), lambda i: (0, i, 0)),
                   pl.BlockSpec((LANES, ATT_BLOCK), lambda i: (0, i))],
        out_shape=[jax.ShapeDtypeStruct((FOX_HEADS, s, LANES), F32), jax.ShapeDtypeStruct((LANES, s), F32)],
        scratch_shapes=[pltpu.VMEM((1, LANES), F32)], compiler_params=_params(),
    )(f, b_f)


def _fox_dlogit(dcum_t, f, b_f, *, name):
    s = f.shape[0]
    nb = s // ATT_BLOCK

    def body(dc_ref, f_ref, b_ref, df_ref, db_ref, carry):
        @pl.when(pl.program_id(0) == 0)
        def _():
            carry[...] = jnp.zeros_like(carry)
            db_ref[...] = jnp.zeros_like(db_ref)

        d = dc_ref[...].T
        dlog = lax.dot_general(_tri(False), d, _NN, precision=lax.Precision.HIGHEST,
                               preferred_element_type=F32) + carry[...]
        carry[...] = dlog[0:1, :]
        df = dlog * (1.0 - _sigmoid(f_ref[...] + b_ref[...]))
        df_ref[...] = df
        db_ref[...] += jnp.sum(df, axis=0, keepdims=True)

    rev = lambda i: nb - 1 - i
    return pl.pallas_call(
        body, name=name, grid=(nb,),
        in_specs=[pl.BlockSpec((LANES, ATT_BLOCK), lambda i: (0, rev(i))),
                  pl.BlockSpec((ATT_BLOCK, LANES), lambda i: (rev(i), 0)),
                  pl.BlockSpec((1, LANES), lambda i: (0, 0))],
        out_specs=[pl.BlockSpec((ATT_BLOCK, LANES), lambda i: (rev(i), 0)),
                   pl.BlockSpec((1, LANES), lambda i: (0, 0))],
        out_shape=[jax.ShapeDtypeStruct((s, LANES), F32), jax.ShapeDtypeStruct((1, LANES), F32)],
        scratch_shapes=[pltpu.VMEM((1, LANES), F32)], compiler_params=_params(),
    )(dcum_t, f, b_f)


def _causal(qi, ki):
    r = lax.broadcasted_iota(jnp.int32, (ATT_BLOCK, ATT_BLOCK), 0) + qi * ATT_BLOCK
    c = lax.broadcasted_iota(jnp.int32, (ATT_BLOCK, ATT_BLOCK), 1) + ki * ATT_BLOCK
    return c <= r


def _head_mask():
    return lax.broadcasted_iota(jnp.int32, (1, LANES), 1) < FOX_HEAD_DIM


def _attn_fwd(qkv, cum_b, cum_r, *, name):
    s = qkv.shape[0]
    nq = s // ATT_BLOCK
    scale = FOX_HEAD_DIM ** -0.5
    npair = HEAD_PAIRS

    def body(q_ref, k_ref, v_ref, cq_ref, ck_ref, o_ref, l_ref):
        qi = pl.program_id(1)
        m0 = _head_mask()
        q2 = q_ref[...]
        zero = jnp.zeros_like(q2)
        qs = (jnp.where(m0, q2, zero), jnp.where(m0, zero, q2))
        cqs = (cq_ref[0], cq_ref[1])

        def step(ki, carry, masked):
            off = pl.multiple_of(ki * ATT_BLOCK, ATT_BLOCK)
            k2 = k_ref[pl.ds(off, ATT_BLOCK), :]
            v2 = v_ref[pl.ds(off, ATT_BLOCK), :]
            out = []
            for hh in range(2):
                m, l, acc = carry[hh]
                sc = _dot(qs[hh], k2, _NT) * scale + (cqs[hh] - ck_ref[hh:hh + 1, pl.ds(off, ATT_BLOCK)])
                if masked:
                    sc = jnp.where(_causal(qi, ki), sc, -1e30)
                m_new = jnp.maximum(m, jnp.max(sc, axis=-1, keepdims=True))
                alpha = jnp.exp(m - m_new)
                p = jnp.exp(sc - m_new)
                l = alpha * l + jnp.sum(p, axis=-1, keepdims=True)
                acc = alpha * acc + _dot(p, v2, _NN)
                out.append((m_new, l, acc))
            return tuple(out)

        init = tuple((jnp.full((ATT_BLOCK, 1), -1e30, F32), jnp.zeros((ATT_BLOCK, 1), F32),
                      jnp.zeros((ATT_BLOCK, LANES), F32)) for _ in range(2))
        carry = lax.fori_loop(0, qi, lambda ki, c: step(ki, c, False), init)
        (ma, la, acca), (mb, lb, accb) = step(qi, carry, True)
        o_ref[...] = jnp.where(m0, acca / la, accb / lb).astype(o_ref.dtype)
        l_ref[0] = jnp.broadcast_to(ma + jnp.log(la), (ATT_BLOCK, LANES))
        l_ref[1] = jnp.broadcast_to(mb + jnp.log(lb), (ATT_BLOCK, LANES))

    stat = pl.BlockSpec((None, 2, ATT_BLOCK, LANES), lambda j, i: (j, 0, i, 0))
    row = pl.BlockSpec((None, 2, s), lambda j, i: (j, 0, 0))
    return pl.pallas_call(
        body, name=name, grid=(npair, nq),
        in_specs=[pl.BlockSpec((ATT_BLOCK, LANES), lambda j, i: (i, j)),
                  pl.BlockSpec((s, LANES), lambda j, i: (0, npair + j)),
                  pl.BlockSpec((s, LANES), lambda j, i: (0, 2 * npair + j)),
                  stat, row],
        out_specs=[pl.BlockSpec((ATT_BLOCK, LANES), lambda j, i: (i, j)), stat],
        out_shape=[jax.ShapeDtypeStruct((s, FOX_WIDTH), BF16),
                   jax.ShapeDtypeStruct((npair, 2, s, LANES), F32)],
        compiler_params=_params(),
    )(qkv, qkv, qkv, cum_b, cum_r)


def _attn_delta(qkv, do, lse_b, cum_b, cum_r, *, name):
    s = qkv.shape[0]
    nq = s // ATT_BLOCK
    scale = FOX_HEAD_DIM ** -0.5
    npair = HEAD_PAIRS

    def body(q_ref, k_ref, v_ref, do_ref, l_ref, cq_ref, ck_ref, d_ref):
        qi = pl.program_id(1)
        m0 = _head_mask()
        q2 = q_ref[...]
        do2 = do_ref[...]
        qs = (jnp.where(m0, q2, jnp.zeros_like(q2)), jnp.where(m0, jnp.zeros_like(q2), q2))
        dos = (jnp.where(m0, do2, jnp.zeros_like(do2)), jnp.where(m0, jnp.zeros_like(do2), do2))

        def step(ki, carry, masked):
            off = pl.multiple_of(ki * ATT_BLOCK, ATT_BLOCK)
            k2 = k_ref[pl.ds(off, ATT_BLOCK), :]
            v2 = v_ref[pl.ds(off, ATT_BLOCK), :]
            out = []
            for hh in range(2):
                sc = _dot(qs[hh], k2, _NT) * scale + (cq_ref[hh] - ck_ref[hh:hh + 1, pl.ds(off, ATT_BLOCK)])
                p = jnp.exp(sc - l_ref[hh])
                if masked:
                    p = jnp.where(_causal(qi, ki), p, 0.0)
                out.append(carry[hh] + jnp.sum(p * _dot(dos[hh], v2, _NT), axis=-1, keepdims=True))
            return tuple(out)

        init = (jnp.zeros((ATT_BLOCK, 1), F32), jnp.zeros((ATT_BLOCK, 1), F32))
        carry = lax.fori_loop(0, qi, lambda ki, c: step(ki, c, False), init)
        da, db = step(qi, carry, True)
        d_ref[0] = jnp.broadcast_to(da, (ATT_BLOCK, LANES))
        d_ref[1] = jnp.broadcast_to(db, (ATT_BLOCK, LANES))

    stat = pl.BlockSpec((None, 2, ATT_BLOCK, LANES), lambda j, i: (j, 0, i, 0))
    return pl.pallas_call(
        body, name=name, grid=(npair, nq),
        in_specs=[pl.BlockSpec((ATT_BLOCK, LANES), lambda j, i: (i, j)),
                  pl.BlockSpec((s, LANES), lambda j, i: (0, npair + j)),
                  pl.BlockSpec((s, LANES), lambda j, i: (0, 2 * npair + j)),
                  pl.BlockSpec((ATT_BLOCK, LANES), lambda j, i: (i, j)),
                  stat, stat, pl.BlockSpec((None, 2, s), lambda j, i: (j, 0, 0))],
        out_specs=stat,
        out_shape=jax.ShapeDtypeStruct((npair, 2, s, LANES), F32), compiler_params=_params(),
    )(qkv, qkv, qkv, do, lse_b, cum_b, cum_r)


def _attn_bwd(qkv, do, lse_b, delta_b, cum_b, cum_r, *, name):
    s = qkv.shape[0]
    nq = s // ATT_BLOCK
    scale = FOX_HEAD_DIM ** -0.5
    npair = HEAD_PAIRS

    def body(q_ref, k_ref, v_ref, do_ref, l_ref, dl_ref, cq_ref, ck_ref, dq_ref, dk_ref, dv_ref, dc_ref):
        ki = pl.program_id(1)
        m0 = _head_mask()
        k2 = k_ref[...]
        v2 = v_ref[...]
        koff = pl.multiple_of(ki * ATT_BLOCK, ATT_BLOCK)

        @pl.when(ki == 0)
        def _():
            dq_ref[...] = jnp.zeros_like(dq_ref)

        def step(qi, carry, masked):
            off = pl.multiple_of(qi * ATT_BLOCK, ATT_BLOCK)
            q2 = q_ref[pl.ds(off, ATT_BLOCK), :]
            do2 = do_ref[pl.ds(off, ATT_BLOCK), :]
            qzero = jnp.zeros_like(q2)
            dzero = jnp.zeros_like(do2)
            out = []
            dqs = []
            for hh in range(2):
                dk_acc, dv_acc, dc_acc = carry[hh]
                keep = m0 if hh == 0 else jnp.logical_not(m0)
                qh = jnp.where(keep, q2, qzero)
                doh = jnp.where(keep, do2, dzero)
                sc = _dot(qh, k2, _NT) * scale + (cq_ref[hh, pl.ds(off, ATT_BLOCK), :]
                                                 - ck_ref[hh:hh + 1, pl.ds(koff, ATT_BLOCK)])
                p = jnp.exp(sc - l_ref[hh, pl.ds(off, ATT_BLOCK), :])
                if masked:
                    p = jnp.where(_causal(qi, ki), p, 0.0)
                dp = _dot(doh, v2, _NT)
                ds = p * (dp - dl_ref[hh, pl.ds(off, ATT_BLOCK), :])
                dv_acc = dv_acc + _dot(p, do2, _TN)
                dk_acc = dk_acc + _dot(ds, q2, _TN)
                dc_acc = dc_acc - jnp.sum(ds, axis=0, keepdims=True)
                dqs.append(_dot(ds, k2, _NN))
                out.append((dk_acc, dv_acc, dc_acc))
            dq_ref[pl.ds(off, ATT_BLOCK), :] += jnp.where(m0, dqs[0], dqs[1]) * scale
            return tuple(out)

        init = tuple((jnp.zeros((ATT_BLOCK, LANES), F32), jnp.zeros((ATT_BLOCK, LANES), F32),
                      jnp.zeros((1, ATT_BLOCK), F32)) for _ in range(2))
        carry = step(ki, init, True)
        (dka, dva, dca), (dkb, dvb, dcb) = lax.fori_loop(ki + 1, nq, lambda qi, c: step(qi, c, False), carry)
        dk_ref[...] = (jnp.where(m0, dka, dkb) * scale).astype(dk_ref.dtype)
        dv_ref[...] = jnp.where(m0, dva, dvb).astype(dv_ref.dtype)
        dc_ref[0:1, :] = dca
        dc_ref[1:2, :] = dcb

    stat = pl.BlockSpec((None, 2, s, LANES), lambda j, i: (j, 0, 0, 0))
    colfull = lambda base: pl.BlockSpec((s, LANES), lambda j, i: (0, base + j))
    colblk = lambda base: pl.BlockSpec((ATT_BLOCK, LANES), lambda j, i: (i, base + j))
    return pl.pallas_call(
        body, name=name, grid=(npair, nq),
        in_specs=[colfull(0), colblk(npair), colblk(2 * npair), colfull(0), stat, stat, stat,
                  pl.BlockSpec((None, 2, s), lambda j, i: (j, 0, 0))],
        out_specs=[colfull(0), colblk(0), colblk(0), pl.BlockSpec((None, 2, ATT_BLOCK), lambda j, i: (j, 0, i))],
        out_shape=[jax.ShapeDtypeStruct((s, FOX_WIDTH), F32), jax.ShapeDtypeStruct((s, FOX_WIDTH), BF16),
                   jax.ShapeDtypeStruct((s, FOX_WIDTH), BF16), jax.ShapeDtypeStruct((npair, 2, s), F32)],
        compiler_params=_params(),
    )(qkv, qkv, qkv, do, lse_b, delta_b, cum_b, cum_r)


ATT_TQ = 256
ATT_TK = 256
ATT_SCALE = FOX_HEAD_DIM ** -0.5
assert ATT_SCALE == 0.125 and ATT_TQ == ATT_TK


def _causal_t(qi, ki):
    kpos = lax.broadcasted_iota(jnp.int32, (ATT_TK, ATT_TQ), 0) + ki * ATT_TK
    qpos = lax.broadcasted_iota(jnp.int32, (ATT_TK, ATT_TQ), 1) + qi * ATT_TQ
    return kpos <= qpos


def _row_mask():
    return lax.broadcasted_iota(jnp.int32, (LANES, 1), 0) < FOX_HEAD_DIM


def _lane_tile(a, width):
    return a if a.shape[1] == width else jnp.tile(a, (1, width // a.shape[1]))


def _transpose_bf16(a):
    return a.astype(F32).T.astype(BF16)


def _attn_fwd_t(qkv, cum_b, cum_r, *, name):
    s = qkv.shape[0]
    nq = s // ATT_TQ
    npair = HEAD_PAIRS

    def body(q_ref, k_ref, v_ref, cq_ref, ck_ref, o_ref, ot_ref, l_ref, vt_ref):
        qi = pl.program_id(1)
        rows = _row_mask()

        @pl.when(qi == 0)
        def _():
            vt_ref[...] = _transpose_bf16(v_ref[...])

        qt = _transpose_bf16(q_ref[...]) * ATT_SCALE
        zero = jnp.zeros_like(qt)
        qts = (jnp.where(rows, qt, zero), jnp.where(rows, zero, qt))

        def step(ki, carry, masked):
            off = pl.multiple_of(ki * ATT_TK, ATT_TK)
            k2 = k_ref[pl.ds(off, ATT_TK), :]
            vt = vt_ref[:, pl.ds(off, ATT_TK)]
            out = []
            for hh in range(2):
                m, l, acc = carry[hh]
                bias = cq_ref[hh:hh + 1, :] - _lane_tile(ck_ref[hh, pl.ds(off, ATT_TK), :], ATT_TQ)
                sc = _dot(k2, qts[hh], _NN) + bias
                if masked:
                    sc = jnp.where(_causal_t(qi, ki), sc, -1e30)
                m_new = jnp.maximum(m, jnp.max(sc, axis=0, keepdims=True))
                alpha = jnp.exp(m - m_new)
                p = jnp.exp(sc - m_new)
                l = alpha * l + jnp.sum(p, axis=0, keepdims=True)
                p_hi = p.astype(BF16)
                p_lo = (p - p_hi.astype(F32)).astype(BF16)
                acc = alpha * acc + (_dot(vt, p_hi, _NN) + _dot(vt, p_lo, _NN))
                out.append((m_new, l, acc))
            return tuple(out)

        init = tuple((jnp.full((1, ATT_TQ), -1e30, F32), jnp.zeros((1, ATT_TQ), F32),
                      jnp.zeros((LANES, ATT_TQ), F32)) for _ in range(2))
        carry = lax.fori_loop(0, qi // 2, lambda kk, c: step(2 * kk + 1, step(2 * kk, c, False), False), init)
        carry = lax.cond(qi % 2 == 1, lambda c: step(qi - 1, c, False), lambda c: c, carry)
        (ma, la, acca), (mb, lb, accb) = step(qi, carry, True)
        ot = jnp.where(rows, acca / la, accb / lb)
        ot_ref[...] = ot
        o_ref[...] = ot.T.astype(o_ref.dtype)
        l_ref[0:1, :] = ma + jnp.log(la)
        l_ref[1:2, :] = mb + jnp.log(lb)

    row = pl.BlockSpec((None, 2, ATT_TQ), lambda j, i: (j, 0, i))
    return pl.pallas_call(
        body, name=name, grid=(npair, nq),
        in_specs=[pl.BlockSpec((ATT_TQ, LANES), lambda j, i: (i, j)),
                  pl.BlockSpec((s, LANES), lambda j, i: (0, npair + j)),
                  pl.BlockSpec((s, LANES), lambda j, i: (0, 2 * npair + j)),
                  row, pl.BlockSpec((None, 2, s, LANES), lambda j, i: (j, 0, 0, 0))],
        out_specs=[pl.BlockSpec((ATT_TQ, LANES), lambda j, i: (i, j)),
                   pl.BlockSpec((LANES, ATT_TQ), lambda j, i: (j, i)), row],
        out_shape=[jax.ShapeDtypeStruct((s, FOX_WIDTH), BF16), jax.ShapeDtypeStruct((FOX_WIDTH, s), F32),
                   jax.ShapeDtypeStruct((npair, 2, s), F32)],
        scratch_shapes=[pltpu.VMEM((LANES, s), BF16)],
        compiler_params=_params(),
    )(qkv, qkv, qkv, cum_r, cum_b)


def _attn_delta_t(do_t, o_t, *, name):
    s = o_t.shape[1]
    ts = _tile(s, 512)

    def body(do_ref, o_ref, d_ref):
        prod = do_ref[...].astype(F32) * o_ref[...]
        d_ref[0:1, :] = jnp.sum(prod[:FOX_HEAD_DIM], axis=0, keepdims=True)
        d_ref[1:2, :] = jnp.sum(prod[FOX_HEAD_DIM:], axis=0, keepdims=True)

    blk = pl.BlockSpec((LANES, ts), lambda j, i: (j, i))
    return pl.pallas_call(
        body, name=name, grid=(HEAD_PAIRS, s // ts), in_specs=[blk, blk],
        out_specs=pl.BlockSpec((None, 2, ts), lambda j, i: (j, 0, i)),
        out_shape=jax.ShapeDtypeStruct((HEAD_PAIRS, 2, s), F32), compiler_params=_params(),
    )(do_t, o_t)


def _attn_bwd_t(qkv, do, o_t, lse, cum_b, cum_r, *, name, dep=None):
    s = qkv.shape[0]
    nq = s // ATT_TQ
    npair = HEAD_PAIRS

    deps = [] if dep is None else [dep]

    def body(q_ref, k_ref, v_ref, do_ref, ot_ref, l_ref, cq_ref, ck_ref, *rest):
        dq_ref, dk_ref, dv_ref, dc_ref, qt_ref, dot_ref, dqt_ref, dl_ref = rest[len(deps):]
        ki = pl.program_id(1)
        m0 = _head_mask()
        rows = _row_mask()
        k2 = k_ref[...]
        v2 = v_ref[...]
        kt = _transpose_bf16(k2)
        ks = k2 * ATT_SCALE
        kz, vz = jnp.zeros_like(k2), jnp.zeros_like(v2)
        khs = (jnp.where(m0, ks, kz), jnp.where(m0, kz, ks))
        vhs = (jnp.where(m0, v2, vz), jnp.where(m0, vz, v2))
        cks = tuple(_lane_tile(ck_ref[hh], ATT_TQ) for hh in range(2))

        @pl.when(ki == 0)
        def _():
            dqt_ref[...] = jnp.zeros_like(dqt_ref)
            qt_ref[...] = _transpose_bf16(q_ref[...])
            do_t = do_ref[...].astype(F32).T
            dot_ref[...] = do_t.astype(BF16)
            prod = do_t * ot_ref[...]
            dl_ref[0:1, :] = jnp.sum(prod[:FOX_HEAD_DIM], axis=0, keepdims=True)
            dl_ref[1:2, :] = jnp.sum(prod[FOX_HEAD_DIM:], axis=0, keepdims=True)

        def step(qi, carry, masked):
            off = pl.multiple_of(qi * ATT_TQ, ATT_TQ)
            q2 = q_ref[pl.ds(off, ATT_TQ), :]
            do2 = do_ref[pl.ds(off, ATT_TQ), :]
            qt = qt_ref[:, pl.ds(off, ATT_TQ)]
            dot_ = dot_ref[:, pl.ds(off, ATT_TQ)]
            out, dqs = [], []
            for hh in range(2):
                dk_acc, dv_acc, dc_acc = carry[hh]
                sc = _dot(khs[hh], qt, _NN) + (cq_ref[hh:hh + 1, pl.ds(off, ATT_TQ)] - cks[hh])
                p = jnp.exp(sc - l_ref[hh:hh + 1, pl.ds(off, ATT_TQ)])
                if masked:
                    p = jnp.where(_causal_t(qi, ki), p, 0.0)
                dp = _dot(vhs[hh], dot_, _NN)
                ds = p * (dp - dl_ref[hh:hh + 1, pl.ds(off, ATT_TQ)])
                dc_acc = dc_acc - jnp.sum(ds, axis=1, keepdims=True)
                dss = (ds * ATT_SCALE).astype(BF16)
                dv_acc = dv_acc + _dot(p, do2, _NN)
                dk_acc = dk_acc + _dot(dss, q2, _NN)
                dqs.append(_dot(kt, dss, _NN))
                out.append((dk_acc, dv_acc, dc_acc))
            dqt_ref[:, pl.ds(off, ATT_TQ)] += jnp.where(rows, dqs[0], dqs[1])
            return tuple(out)

        init = tuple((jnp.zeros((ATT_TK, LANES), F32), jnp.zeros((ATT_TK, LANES), F32),
                      jnp.zeros((ATT_TK, 1), F32)) for _ in range(2))
        carry = step(ki, init, True)
        rest = nq - 1 - ki
        carry = lax.fori_loop(
            0, rest // 2, lambda t, c: step(ki + 2 + 2 * t, step(ki + 1 + 2 * t, c, False), False), carry)
        carry = lax.cond(rest % 2 == 1, lambda c: step(nq - 1, c, False), lambda c: c, carry)
        (dka, dva, dca), (dkb, dvb, dcb) = carry
        dk_ref[...] = jnp.where(m0, dka, dkb).astype(dk_ref.dtype)
        dv_ref[...] = jnp.where(m0, dva, dvb).astype(dv_ref.dtype)
        dc_ref[0] = jnp.broadcast_to(dca, (ATT_TK, LANES))
        dc_ref[1] = jnp.broadcast_to(dcb, (ATT_TK, LANES))

        @pl.when(ki == nq - 1)
        def _():
            dq_ref[...] = dqt_ref[...].T.astype(dq_ref.dtype)

    colfull = lambda base: pl.BlockSpec((s, LANES), lambda j, i: (0, base + j))
    colblk = lambda base: pl.BlockSpec((ATT_TK, LANES), lambda j, i: (i, base + j))
    stat = pl.BlockSpec((None, 2, s), lambda j, i: (j, 0, 0))
    bcast = pl.BlockSpec((None, 2, ATT_TK, LANES), lambda j, i: (j, 0, i, 0))
    grad = jax.ShapeDtypeStruct((s, FOX_WIDTH), BF16)
    return pl.pallas_call(
        body, name=name, grid=(npair, nq),
        in_specs=[colfull(0), colblk(npair), colblk(2 * npair), colfull(0),
                  pl.BlockSpec((LANES, s), lambda j, i: (j, 0)), stat, stat, bcast]
                 + [pl.BlockSpec(memory_space=pl.ANY)] * len(deps),
        out_specs=[colfull(0), colblk(0), colblk(0), bcast],
        out_shape=[grad, grad, grad, jax.ShapeDtypeStruct((npair, 2, s, LANES), F32)],
        scratch_shapes=[pltpu.VMEM((LANES, s), BF16), pltpu.VMEM((LANES, s), BF16), pltpu.VMEM((LANES, s), F32),
                        pltpu.VMEM((2, s), F32)],
        compiler_params=_params(),
    )(qkv, qkv, qkv, do, o_t, lse, cum_r, cum_b, *deps)


def _merge_fwd(zg, ya, yb, *, name, tm=256):
    s, d = ya.shape
    tm = _tile(s, tm)

    def body(zg_ref, ya_ref, yb_ref, m_ref):
        ga = _sigmoid(zg_ref[:, :d].astype(F32))
        gb = _sigmoid(zg_ref[:, d:].astype(F32))
        m_ref[...] = (ga * ya_ref[...].astype(F32) + gb * yb_ref[...].astype(F32)).astype(m_ref.dtype)

    row = pl.BlockSpec((tm, d), lambda i: (i, 0))
    row2 = pl.BlockSpec((tm, 2 * d), lambda i: (i, 0))
    return pl.pallas_call(
        body, name=name, grid=(s // tm,), in_specs=[row2, row, row], out_specs=row,
        out_shape=jax.ShapeDtypeStruct((s, d), BF16), compiler_params=_params(),
    )(zg, ya, yb)


def _merge_bwd(dm, zg, ya, yb, *, name, tm=256):
    s, d = ya.shape
    tm = _tile(s, tm)

    def body(dm_ref, zg_ref, ya_ref, yb_ref, dzg_ref, dya_ref, dyb_ref):
        dmv = dm_ref[...].astype(F32)
        ga = _sigmoid(zg_ref[:, :d].astype(F32))
        gb = _sigmoid(zg_ref[:, d:].astype(F32))
        dzg_ref[:, :d] = (dmv * ya_ref[...].astype(F32) * ga * (1.0 - ga)).astype(dzg_ref.dtype)
        dzg_ref[:, d:] = (dmv * yb_ref[...].astype(F32) * gb * (1.0 - gb)).astype(dzg_ref.dtype)
        dya_ref[...] = (dmv * ga).astype(dya_ref.dtype)
        dyb_ref[...] = (dmv * gb).astype(dyb_ref.dtype)

    row = pl.BlockSpec((tm, d), lambda i: (i, 0))
    row2 = pl.BlockSpec((tm, 2 * d), lambda i: (i, 0))
    return pl.pallas_call(
        body, name=name, grid=(s // tm,), in_specs=[row, row2, row, row], out_specs=[row2, row, row],
        out_shape=[jax.ShapeDtypeStruct((s, 2 * d), BF16), jax.ShapeDtypeStruct((s, d), BF16),
                   jax.ShapeDtypeStruct((s, d), BF16)],
        compiler_params=_params(),
    )(dm, zg, ya, yb)


SUBLANES = 8


def _shift_down(u, k, row):
    rolled = pltpu.roll(u, k, 0)
    head = jnp.where(row[:SUBLANES] >= k, rolled[:SUBLANES], 0.0)
    return jnp.concatenate([head, rolled[SUBLANES:]], axis=0)


def _shift_up(u, k, row):
    n = u.shape[0]
    rolled = pltpu.roll(u, n - k, 0)
    tail = jnp.where(row[n - SUBLANES:] < n - k, rolled[n - SUBLANES:], 0.0)
    return jnp.concatenate([rolled[:n - SUBLANES], tail], axis=0)


def _conv_act_fwd(up_a, up_b, cw_a, cw_b, cb_a, cb_b, *, name, tc=128):
    s, f = up_a.shape
    tc = _tile(f, tc)

    def body(ua_ref, ub_ref, wa_ref, wb_ref, ba_ref, bb_ref, act_ref):
        row = lax.broadcasted_iota(jnp.int32, (s, tc), 0)

        def conv(u_ref, w_ref, b_ref):
            u = u_ref[...].astype(F32)
            return (b_ref[...] + w_ref[0:1, :] * _shift_down(u, 2, row)
                    + w_ref[1:2, :] * _shift_down(u, 1, row) + w_ref[2:3, :] * u)

        ca = conv(ua_ref, wa_ref, ba_ref)
        cb = conv(ub_ref, wb_ref, bb_ref)
        act_ref[...] = (_gelu(ca) * cb).astype(act_ref.dtype)

    col = pl.BlockSpec((s, tc), lambda j: (0, j))
    w3 = pl.BlockSpec((3, tc), lambda j: (0, j))
    b1 = pl.BlockSpec((1, tc), lambda j: (0, j))
    return pl.pallas_call(
        body, name=name, grid=(f // tc,), in_specs=[col, col, w3, w3, b1, b1], out_specs=col,
        out_shape=jax.ShapeDtypeStruct((s, f), BF16), compiler_params=_params(),
    )(up_a, up_b, cw_a, cw_b, cb_a, cb_b)


def _conv_act_bwd(up_a, up_b, dact, cw_a, cw_b, cb_a, cb_b, *, name, tc=128):
    s, f = up_a.shape
    tc = _tile(f, tc)

    def body(ua_ref, ub_ref, da_ref, wa_ref, wb_ref, ba_ref, bb_ref, dua_ref, dub_ref, dwa_ref, dwb_ref):
        row = lax.broadcasted_iota(jnp.int32, (s, tc), 0)

        def conv(u_ref, w_ref, b_ref):
            u = u_ref[...].astype(F32)
            u1 = _shift_down(u, 1, row)
            u2 = _shift_down(u, 2, row)
            return u, u1, u2, b_ref[...] + w_ref[0:1, :] * u2 + w_ref[1:2, :] * u1 + w_ref[2:3, :] * u

        def back(dc, taps, w_ref, du_ref, dw_ref):
            u, u1, u2 = taps
            dw_ref[0:1, :] = jnp.sum(dc * u2, axis=0, keepdims=True)
            dw_ref[1:2, :] = jnp.sum(dc * u1, axis=0, keepdims=True)
            dw_ref[2:3, :] = jnp.sum(dc * u, axis=0, keepdims=True)
            dw_ref[3:4, :] = jnp.sum(dc, axis=0, keepdims=True)
            du = (w_ref[2:3, :] * dc + w_ref[1:2, :] * _shift_up(dc, 1, row)
                  + w_ref[0:1, :] * _shift_up(dc, 2, row))
            du_ref[...] = du.astype(du_ref.dtype)

        ua, ua1, ua2, ca = conv(ua_ref, wa_ref, ba_ref)
        ub, ub1, ub2, cb = conv(ub_ref, wb_ref, bb_ref)
        g, dg = _gelu_and_grad(ca)
        dact_v = da_ref[...].astype(F32)
        back(dact_v * cb * dg, (ua, ua1, ua2), wa_ref, dua_ref, dwa_ref)
        back(dact_v * g, (ub, ub1, ub2), wb_ref, dub_ref, dwb_ref)

    col = pl.BlockSpec((s, tc), lambda j: (0, j))
    w3 = pl.BlockSpec((3, tc), lambda j: (0, j))
    w4 = pl.BlockSpec((4, tc), lambda j: (0, j))
    b1 = pl.BlockSpec((1, tc), lambda j: (0, j))
    return pl.pallas_call(
        body, name=name, grid=(f // tc,), in_specs=[col, col, col, w3, w3, b1, b1],
        out_specs=[col, col, w4, w4],
        out_shape=[jax.ShapeDtypeStruct((s, f), BF16), jax.ShapeDtypeStruct((s, f), BF16),
                   jax.ShapeDtypeStruct((4, f), F32), jax.ShapeDtypeStruct((4, f), F32)],
        compiler_params=_params(),
    )(up_a, up_b, dact, cw_a, cw_b, cb_a, cb_b)


def _ple_final(x2, ple, zp, target, g_final, *, name, tm=256):
    s, d = x2.shape
    tm = _tile(s, tm)

    def body(x_ref, ple_ref, zp_ref, t_ref, g_ref, dx_ref, dple_ref, dzp_ref, dg_ref, loss_ref):
        @pl.when(pl.program_id(0) == 0)
        def _():
            dg_ref[...] = jnp.zeros_like(dg_ref)
            loss_ref[...] = jnp.zeros_like(loss_ref)

        gp = _sigmoid(zp_ref[...].astype(F32))
        plev = ple_ref[...].astype(F32)
        x3 = x_ref[...] + plev * gp
        r = lax.rsqrt(jnp.mean(x3 * x3, axis=-1, keepdims=True) + EPS)
        xhat = x3 * r
        gv = g_ref[...]
        diff = xhat * gv - t_ref[...]
        loss_ref[...] += 0.5 * jnp.sum(jnp.mean(diff * diff, axis=-1, keepdims=True), axis=0, keepdims=True)
        dy = diff * (1.0 / d)
        dg_ref[...] += jnp.sum(dy * xhat, axis=0, keepdims=True)
        dyg = dy * gv
        dx3 = r * (dyg - xhat * jnp.mean(dyg * xhat, axis=-1, keepdims=True))
        dx_ref[...] = dx3
        dple_ref[...] = (dx3 * gp).astype(dple_ref.dtype)
        dzp_ref[...] = (dx3 * plev * gp * (1.0 - gp)).astype(dzp_ref.dtype)

    row = pl.BlockSpec((tm, d), lambda i: (i, 0))
    vec = pl.BlockSpec((1, d), lambda i: (0, 0))
    return pl.pallas_call(
        body, name=name, grid=(s // tm,), in_specs=[row, row, row, row, vec],
        out_specs=[row, row, row, vec, pl.BlockSpec((1, LANES), lambda i: (0, 0))],
        out_shape=[jax.ShapeDtypeStruct((s, d), F32), jax.ShapeDtypeStruct((s, d), BF16),
                   jax.ShapeDtypeStruct((s, d), BF16), jax.ShapeDtypeStruct((1, d), F32),
                   jax.ShapeDtypeStruct((1, LANES), F32)],
        compiler_params=_params(),
    )(x2, ple, zp, target, g_final)


def _device_step(x, p, target, w, get_w_in=None, get_w_rest=None, on_grads_ffn=None, on_grads_small=None,
                 on_grads_mix=None):
    s = x.shape[0]
    g = {}
    w = dict(w)

    h = _rms_fwd(x, w["norm_mix_g"], name="rms_mix", dep=w.get("first_dep"))
    if get_w_in is not None:
        w.update(get_w_in(h))
    qkv = _mm(h, w["w_qkv"], mode="nn", out_dtype=BF16, name="proj_qkv", tm=1024)
    f = _mm(h, w["w_f"], mode="nn", out_dtype=F32, name="proj_f", tm=1024)

    cum_b, cum_t = _fox_cum(f, w["b_f"], name="fox_cum")
    cum_b = cum_b.reshape(HEAD_PAIRS, 2, s, LANES)
    cum_r = cum_t[:FOX_HEADS].reshape(HEAD_PAIRS, 2, s)
    b, o_t, lse = _attn_fwd_t(qkv, cum_b, cum_r, name="attn_fwd")

    dep = get_w_rest[0](b) if get_w_rest is not None else None
    z_uv = _mm(h, w["w_uv"], mode="nn", out_dtype=BF16, name="proj_uv", tm=1024, dep=dep)
    zg = _mm(h, w["w_g"], mode="nn", out_dtype=BF16, name="proj_gate", tm=1024, dep=dep)
    a = _gmlp_fwd(z_uv, w["gmlp_ln_g"], w["gmlp_ln_b"], w["gmlp_w_s"], w["gmlp_b_s_t"], name="gmlp_fwd")
    if get_w_rest is not None:
        w.update(get_w_rest[1]([a, zg]))

    ya = _mm(a, w["w_branch_a"], mode="nn", out_dtype=BF16, name="branch_a", tm=1024)
    yb = _mm(b, w["w_branch_b"], mode="nn", out_dtype=BF16, name="branch_b", tm=1024)
    merged = _merge_fwd(zg, ya, yb, name="merge_fwd")
    x1 = _mm(merged, w["w_out"], mode="nn", out_dtype=F32, name="proj_out", add=x, tm=1024)

    h2 = _rms_fwd(x1, w["norm_ffn_g"], name="rms_ffn")
    up_a = _mm(h2, w["w_up_a"], mode="nn", out_dtype=BF16, name="up_a", tm=1024, tn=D_FF // 2)
    up_b = _mm(h2, w["w_up_b"], mode="nn", out_dtype=BF16, name="up_b", tm=1024, tn=D_FF // 2)
    cw, cb = w["conv_w"], w["conv_b"]
    conv_args = (cw[:, :D_FF], cw[:, D_FF:], cb[:, :D_FF], cb[:, D_FF:])
    act = _conv_act_fwd(up_a, up_b, *conv_args, name="conv_act_fwd")
    x2 = _mm(act, w["w_down"], mode="nn", out_dtype=F32, name="down", add=x1, tm=512)

    h3 = _rms_fwd(x2, w["norm_ple_g"], name="rms_ple")
    ple = _mm(p, w["w_ple"], mode="nn", out_dtype=BF16, name="ple_proj", tm=1024)
    zp = _mm(h3, w["w_ple_gate"], mode="nn", out_dtype=BF16, name="ple_gate", tm=1024)
    dx3, dple, dzp, g["norm_final_g"], loss = _ple_final(x2, ple, zp, target, w["norm_final_g"], name="ple_final")

    g["w_ple"] = _mm(p, dple, mode="tn", out_dtype=BF16, name="dw_ple")
    g["w_ple_gate"] = _mm(h3, dzp, mode="tn", out_dtype=BF16, name="dw_ple_gate")
    dh3 = _mm(dzp, w["w_ple_gate"], mode="nt", out_dtype=BF16, name="dh3")
    dx2, dx2_b, g["norm_ple_g"] = _rms_bwd(x2, w["norm_ple_g"], dh3, dx3, name="rms_ple_bwd")

    g["w_down"] = _mm(act, dx2_b, mode="tn", out_dtype=BF16, name="dw_down", tm=D_FF // 2)
    dact = _mm(dx2_b, w["w_down"], mode="nt", out_dtype=BF16, name="dact", tn=D_FF // 2)
    dup_a, dup_b, dcw_a, dcw_b = _conv_act_bwd(up_a, up_b, dact, *conv_args, name="conv_act_bwd")
    g["conv_w"] = jnp.concatenate([dcw_a[:3], dcw_b[:3]], axis=1)
    g["conv_b"] = jnp.concatenate([dcw_a[3:], dcw_b[3:]], axis=1)
    g["w_up_a"] = _mm(h2, dup_a, mode="tn", out_dtype=BF16, name="dw_up_a", tn=D_FF // 2)
    g["w_up_b"] = _mm(h2, dup_b, mode="tn", out_dtype=BF16, name="dw_up_b", tn=D_FF // 2)
    dh2 = _mm_nt_sum([(dup_a, w["w_up_a"]), (dup_b, w["w_up_b"])], out_dtype=BF16, name="dh2")
    dx1, dx1_b, g["norm_ffn_g"] = _rms_bwd(x1, w["norm_ffn_g"], dh2, dx2, name="rms_ffn_bwd")

    g["w_out"] = _mm(merged, dx1_b, mode="tn", out_dtype=BF16, name="dw_out")
    dmerged = _mm(dx1_b, w["w_out"], mode="nt", out_dtype=BF16, name="dmerged")
    dzg, dya, dyb = _merge_bwd(dmerged, zg, ya, yb, name="merge_bwd")
    g["w_branch_a"] = _mm(a, dya, mode="tn", out_dtype=BF16, name="dw_branch_a")
    g["w_branch_b"] = _mm(b, dyb, mode="tn", out_dtype=BF16, name="dw_branch_b")
    dep = on_grads_ffn(g) if on_grads_ffn is not None else None
    da = _mm(dya, w["w_branch_a"], mode="nt", out_dtype=BF16, name="da", dep=dep)
    db = _mm(dyb, w["w_branch_b"], mode="nt", out_dtype=BF16, name="db")

    dz_uv, g["gmlp_w_s"], dbs_t, g["gmlp_ln_g"], g["gmlp_ln_b"] = _gmlp_bwd(
        z_uv, da, w["gmlp_ln_g"], w["gmlp_ln_b"], w["gmlp_w_s"], w["gmlp_b_s_t"], name="gmlp_bwd")
    g["gmlp_b_s"] = dbs_t[:, :GMLP_GROUPS].T
    dep = on_grads_small(g) if on_grads_small is not None else None

    dq, dk, dv, dcum_b = _attn_bwd_t(qkv, db, o_t, lse, cum_b, cum_r, name="attn_bwd", dep=dep)
    dcum_t = jnp.pad(dcum_b[..., 0].reshape(FOX_HEADS, s), ((0, LANES - FOX_HEADS), (0, 0)))
    df, g["b_f"] = _fox_dlogit(dcum_t, f, w["b_f"], name="fox_dlogit")
    dqkv = jnp.concatenate([dq, dk, dv], axis=1)

    g["w_uv"] = _mm(h, dz_uv, mode="tn", out_dtype=BF16, name="dw_uv")
    g["w_qkv"] = _mm(h, dqkv, mode="tn", out_dtype=BF16, name="dw_qkv")
    g["w_f"] = _mm(h, df, mode="tn", out_dtype=BF16, name="dw_f")
    g["w_g"] = _mm(h, dzg, mode="tn", out_dtype=BF16, name="dw_g")
    dep = on_grads_mix(g) if on_grads_mix is not None else None
    dh = _mm_nt_sum([(dz_uv, w["w_uv"]), (dqkv, w["w_qkv"]), (df, w["w_f"]), (dzg, w["w_g"])],
                    out_dtype=BF16, name="dh", dep=dep)
    dx0, _, g["norm_mix_g"] = _rms_bwd(x, w["norm_mix_g"], dh, dx1, name="rms_mix_bwd")
    return loss, dx0, g


def _coords():
    return lax.axis_index("x"), lax.axis_index("y"), lax.axis_index("c")


def _other_chips(x, y):
    return [(1 - x, y), (x, 1 - y), (1 - x, 1 - y)]


def _remote(src, dst, send_sem, recv_sem, dev):
    return pltpu.make_async_remote_copy(src_ref=src, dst_ref=dst, send_sem=send_sem, recv_sem=recv_sem,
                                        device_id=dev, device_id_type=MESH)


_ANY = pl.BlockSpec(memory_space=pl.ANY)


def _gather_weights(halved, whole, *, name):
    nh, n = len(halved), len(halved) + len(whole)
    arrays = list(halved) + list(whole)

    def body(*refs):
        ins, outs = refs[:n], refs[n:2 * n]
        send_sems, recv_sems = refs[2 * n:]
        x, y, c = _coords()
        me, sib = 2 * x + y, (x, y, 1 - c)
        chips = _other_chips(x, y)

        def half(i, which):
            h = ins[i].shape[0] // 2
            return pl.ds(pl.multiple_of(which * h, 16), h)

        sends = []
        for i in range(n):
            src, dst = (ins[i].at[half(i, c)], outs[i].at[me, half(i, c)]) if i < nh else (ins[i], outs[i].at[me])
            for k, (cx, cy) in enumerate(chips):
                cp = _remote(src, dst, send_sems.at[i, k], recv_sems.at[i, k], (cx, cy, c))
                cp.start()
                sends.append(cp)
        for i in range(n):
            for k, (cx, cy) in enumerate(chips):
                got = outs[i].at[2 * cx + cy, half(i, c)] if i < nh else outs[i].at[2 * cx + cy]
                _remote(got, got, send_sems.at[i, k], recv_sems.at[i, k], sib).wait_recv()
                if i < nh:
                    cp = _remote(got, got, send_sems.at[i, 3 + k], recv_sems.at[i, 3 + k], sib)
                    cp.start()
                    sends.append(cp)
        for i in range(nh):
            for k, (cx, cy) in enumerate(chips):
                got = outs[i].at[2 * cx + cy, half(i, 1 - c)]
                _remote(got, got, send_sems.at[i, 3 + k], recv_sems.at[i, 3 + k], sib).wait_recv()
        for cp in sends:
            cp.wait_send()

    outs = pl.pallas_call(
        body, name=name, in_specs=[_ANY] * n, out_specs=[_ANY] * n,
        out_shape=[jax.ShapeDtypeStruct((N_CHIPS,) + a.shape, a.dtype) for a in arrays],
        scratch_shapes=[pltpu.SemaphoreType.DMA((n, 6)), pltpu.SemaphoreType.DMA((n, 6))],
        compiler_params=_params(),
    )(*arrays)
    chip = 2 * lax.axis_index("x") + lax.axis_index("y")
    return [lax.dynamic_update_index_in_dim(o, a, chip, 0) for o, a in zip(outs, arrays)]


def _pair_exchange(gs, *, name):
    n = len(gs)

    def body(*refs):
        ins, outs = refs[:n], refs[n:2 * n]
        send_sems, recv_sems = refs[2 * n:]
        x, y, c = _coords()
        copies = []
        for i in range(n):
            for j in range(N_CHIPS):
                cp = _remote(ins[i].at[j, 1 - c], outs[i].at[j], send_sems.at[i, j], recv_sems.at[i, j], (x, y, 1 - c))
                cp.start()
                copies.append(cp)
        for cp in copies:
            cp.wait()

    return pl.pallas_call(
        body, name=name, in_specs=[_ANY] * n, out_specs=[_ANY] * n,
        out_shape=[jax.ShapeDtypeStruct((N_CHIPS,) + a.shape[2:], a.dtype) for a in gs],
        scratch_shapes=[pltpu.SemaphoreType.DMA((n, N_CHIPS)), pltpu.SemaphoreType.DMA((n, N_CHIPS))],
        compiler_params=_params(),
    )(*gs)


def _chip_exchange(ss, *, name):
    n = len(ss)

    def body(*refs):
        ins, outs = refs[:n], refs[n:2 * n]
        send_sems, recv_sems = refs[2 * n:]
        x, y, c = _coords()
        me = 2 * x + y
        chips = _other_chips(x, y)
        sends = []
        for i in range(n):
            for k, (cx, cy) in enumerate(chips):
                cp = _remote(ins[i].at[2 * cx + cy], outs[i].at[me], send_sems.at[i, k], recv_sems.at[i, k], (cx, cy, c))
                cp.start()
                sends.append(cp)
        for i in range(n):
            for k, (cx, cy) in enumerate(chips):
                got = outs[i].at[2 * cx + cy]
                _remote(got, got, send_sems.at[i, k], recv_sems.at[i, k], (cx, cy, c)).wait_recv()
        for cp in sends:
            cp.wait_send()

    return pl.pallas_call(
        body, name=name, in_specs=[_ANY] * n, out_specs=[_ANY] * n,
        out_shape=[jax.ShapeDtypeStruct(a.shape, a.dtype) for a in ss],
        scratch_shapes=[pltpu.SemaphoreType.DMA((n, 3)), pltpu.SemaphoreType.DMA((n, 3))],
        compiler_params=_params(),
    )(*ss)


def _pair_share(hs, *, name):
    n = len(hs)

    def body(*refs):
        ins, outs = refs[:n], refs[n:2 * n]
        send_sems, recv_sems = refs[2 * n:]
        x, y, c = _coords()
        copies = []
        for i in range(n):
            cp = _remote(ins[i], outs[i], send_sems.at[i], recv_sems.at[i], (x, y, 1 - c))
            cp.start()
            copies.append(cp)
        for cp in copies:
            cp.wait()

    return pl.pallas_call(
        body, name=name, in_specs=[_ANY] * n, out_specs=[_ANY] * n,
        out_shape=[jax.ShapeDtypeStruct(a.shape, a.dtype) for a in hs],
        scratch_shapes=[pltpu.SemaphoreType.DMA((n,)), pltpu.SemaphoreType.DMA((n,))],
        compiler_params=_params(),
    )(*hs)


def _all_exchange(vec, *, name):
    def body(v_ref, o_ref, send_sems, recv_sems, local_sem):
        x, y, c = _coords()
        me = 4 * x + 2 * y + c
        local = pltpu.make_async_copy(v_ref, o_ref.at[me], local_sem)
        local.start()
        copies = []
        k = 0
        for dx in (0, 1):
            for dy in (0, 1):
                for dc in (0, 1):
                    if dx or dy or dc:
                        peer = (1 - x if dx else x, 1 - y if dy else y, 1 - c if dc else c)
                        cp = _remote(v_ref, o_ref.at[me], send_sems.at[k], recv_sems.at[k], peer)
                        cp.start()
                        copies.append(cp)
                        k += 1
        for cp in copies:
            cp.wait()
        local.wait()

    return pl.pallas_call(
        body, name=name, in_specs=[_ANY], out_specs=_ANY,
        out_shape=jax.ShapeDtypeStruct((8,) + vec.shape, vec.dtype),
        scratch_shapes=[pltpu.SemaphoreType.DMA((7,)), pltpu.SemaphoreType.DMA((7,)), pltpu.SemaphoreType.DMA(())],
        compiler_params=_params(),
    )(vec)


_HBM = pl.BlockSpec(memory_space=pltpu.HBM)
_SEM = pl.BlockSpec(memory_space=pltpu.SEMAPHORE)
_EFFECT = pltpu.SideEffectType.DATAFLOW_SIDE_EFFECTING


def _copies_start(srcs, lands, plan, n_copies, *, name, after=()):
    ns, n = len(srcs), len(srcs) + len(lands)
    na = len(after)

    def body(*refs):
        send_sems, recv_sems = refs[n + na], refs[n + na + 1]
        token = refs[-1]
        for k, (src, dst, dev) in enumerate(plan(refs[:ns], refs[ns:n])):
            _remote(src, dst, send_sems.at[k], recv_sems.at[k], dev).start()
        token[...] = jnp.zeros_like(token)

    arrays = list(srcs) + list(lands)
    outs = pl.pallas_call(
        body, name=name,
        out_shape=(pltpu.SemaphoreType.DMA((n_copies,)), pltpu.SemaphoreType.DMA((n_copies,)),
                   *[pltpu.HBM(a.shape, a.dtype) for a in arrays], jax.ShapeDtypeStruct((8, LANES), F32)),
        in_specs=[_HBM] * n + [_ANY] * na,
        out_specs=(_SEM, _SEM, *[_HBM] * n, pl.BlockSpec(memory_space=pltpu.VMEM)),
        input_output_aliases={i: 2 + i for i in range(n)},
        compiler_params=pltpu.CompilerParams(has_side_effects=_EFFECT),
    )(*[pltpu.with_memory_space_constraint(a, pltpu.HBM) for a in arrays], *after)
    return outs[0], outs[1], list(outs[2:2 + ns]), list(outs[2 + ns:2 + n]), outs[-1]


def _copies_wait(send_sems, recv_sems, srcs, lands, plan, first, after, *, name):
    ns, n = len(srcs), len(srcs) + len(lands)

    def body(*refs):
        send, recv = refs[n], refs[n + 1]
        for k, (src, dst, dev) in enumerate(plan(refs[:ns], refs[ns:n])):
            cp = _remote(src, dst, send.at[first + k], recv.at[first + k], dev)
            cp.wait_send()
            cp.wait_recv()

    arrays = list(srcs) + list(lands)
    outs = pl.pallas_call(
        body, name=name, out_shape=tuple(pltpu.HBM(a.shape, a.dtype) for a in arrays),
        in_specs=[_HBM] * n + [_SEM, _SEM] + [_ANY] * len(after), out_specs=tuple([_HBM] * n),
        input_output_aliases={i: i for i in range(n)},
        compiler_params=pltpu.CompilerParams(has_side_effects=_EFFECT),
    )(*arrays, send_sems, recv_sems, *after)
    return list(outs[:ns]), list(outs[ns:])


def _gather_plan(halved):
    def plan(srcs, lands):
        x, y, c = _coords()
        me = 2 * x + y
        out = []
        for i, (src, land) in enumerate(zip(srcs, lands)):
            if halved[i]:
                h = src.shape[0] // 2
                rows = pl.ds(pl.multiple_of(c * h, 16), h)
                src, dst = src.at[rows], land.at[me, rows]
            else:
                dst = land.at[me]
            out += [(src, dst, (cx, cy, c)) for cx, cy in _other_chips(x, y)]
        return out
    return plan


def _forward_halves(lands, *, name):
    n = len(lands)

    def body(*refs):
        ins, outs = refs[:n], refs[n:2 * n]
        send_sems, recv_sems = refs[2 * n:]
        x, y, c = _coords()
        copies = []
        for i in range(n):
            h = ins[i].shape[1] // 2
            rows = pl.ds(pl.multiple_of(c * h, 16), h)
            for k, (cx, cy) in enumerate(_other_chips(x, y)):
                cp = _remote(ins[i].at[2 * cx + cy, rows], outs[i].at[2 * cx + cy, rows],
                             send_sems.at[i, k], recv_sems.at[i, k], (x, y, 1 - c))
                cp.start()
                copies.append(cp)
        for cp in copies:
            cp.wait()

    return pl.pallas_call(
        body, name=name, in_specs=[_ANY] * n, out_specs=[_ANY] * n,
        out_shape=[jax.ShapeDtypeStruct(a.shape, a.dtype) for a in lands],
        input_output_aliases={i: i for i in range(n)},
        scratch_shapes=[pltpu.SemaphoreType.DMA((n, 3)), pltpu.SemaphoreType.DMA((n, 3))],
        compiler_params=_params(),
    )(*lands)


def _forward_plan(srcs, lands):
    x, y, c = _coords()
    out = []
    for land in lands:
        h = land.shape[1] // 2
        rows = pl.ds(pl.multiple_of(c * h, 16), h)
        for cx, cy in _other_chips(x, y):
            view = land.at[2 * cx + cy, rows]
            out.append((view, view, (x, y, 1 - c)))
    return out


def _share_plan(srcs, lands):
    x, y, c = _coords()
    return [(src, land, (x, y, 1 - c)) for src, land in zip(srcs, lands)]


def _pair_plan(srcs, lands):
    x, y, c = _coords()
    out = []
    for src, land in zip(srcs, lands):
        out += [(src.at[j, 1 - c], land.at[j], (x, y, 1 - c)) for j in range(N_CHIPS)]
    return out


def _all_plan(srcs, lands):
    x, y, c = _coords()
    me = 4 * x + 2 * y + c
    out = []
    for src, land in zip(srcs, lands):
        for dx in (0, 1):
            for dy in (0, 1):
                for dc in (0, 1):
                    if dx or dy or dc:
                        out.append((src, land.at[me], (1 - x if dx else x, 1 - y if dy else y, 1 - c if dc else c)))
    return out


def _chip_plan(srcs, lands):
    x, y, c = _coords()
    me = 2 * x + y
    out = []
    for src, land in zip(srcs, lands):
        out += [(src.at[2 * cx + cy], land.at[me], (cx, cy, c)) for cx, cy in _other_chips(x, y)]
    return out


ROW_BLOCK_BYTES = 2 * 1024 * 1024


def _rtile(r, pref, mult, row_bytes=None):
    if row_bytes is not None:
        pref = max(pref, ROW_BLOCK_BYTES // row_bytes)
    t = (min(r, pref) // mult) * mult
    while t >= mult:
        if r % t == 0:
            return t
        t -= mult
    return r


def _pair_add(g, recv, core, *, name):
    _, _, r2, cols = g.shape
    tr = _rtile(r2, 256, 16, row_bytes=2 * cols)

    def body(c_ref, g_ref, r_ref, o_ref):
        o_ref[...] = (g_ref[...].astype(F32) + r_ref[...].astype(F32)).astype(o_ref.dtype)

    blk = pl.BlockSpec((None, tr, cols), lambda j, i, c_ref: (j, i, 0))
    return pl.pallas_call(
        body, name=name,
        grid_spec=pltpu.PrefetchScalarGridSpec(
            num_scalar_prefetch=1, grid=(N_CHIPS, r2 // tr),
            in_specs=[pl.BlockSpec((None, None, tr, cols), lambda j, i, c_ref: (j, c_ref[0], i, 0)), blk],
            out_specs=blk),
        out_shape=jax.ShapeDtypeStruct(recv.shape, recv.dtype), compiler_params=_params(),
    )(core, g, recv)


def _sum_slots(a, out_dtype, *, name):
    n, r, cols = a.shape
    whole = n * r * cols * a.dtype.itemsize <= 4 * ROW_BLOCK_BYTES
    tr = r if whole else _rtile(r, 256, 16)

    def body(a_ref, o_ref):
        acc = a_ref[0].astype(F32)
        for j in range(1, n):
            acc = acc + a_ref[j].astype(F32)
        o_ref[...] = acc.astype(o_ref.dtype)

    return pl.pallas_call(
        body, name=name, grid=(r // tr,),
        in_specs=[pl.BlockSpec((n, tr, cols), lambda i: (0, i, 0))],
        out_specs=pl.BlockSpec((tr, cols), lambda i: (i, 0)),
        out_shape=jax.ShapeDtypeStruct((r, cols), out_dtype), compiler_params=_params(),
    )(a)


def _chip_sum(own, recv, chip, *, name):
    _, r2, cols = own.shape
    tr = _rtile(r2, 256, 16, row_bytes=2 * cols)

    def body(chip_ref, own_ref, *rest):
        o_ref = rest[-1]
        acc = None
        for j in range(N_CHIPS):
            term = jnp.where(chip_ref[0] == j, own_ref[...], rest[j][...]).astype(F32)
            acc = term if acc is None else acc + term
        o_ref[...] = acc

    def slot(j):
        return pl.BlockSpec((None, tr, cols),
                            lambda i, chip_ref: (jnp.where(chip_ref[0] == j, (j + 1) % N_CHIPS, j), i, 0))

    return pl.pallas_call(
        body, name=name,
        grid_spec=pltpu.PrefetchScalarGridSpec(
            num_scalar_prefetch=1, grid=(r2 // tr,),
            in_specs=[pl.BlockSpec((None, tr, cols), lambda i, chip_ref: (chip_ref[0], i, 0))]
                     + [slot(j) for j in range(N_CHIPS)],
            out_specs=pl.BlockSpec((tr, cols), lambda i, chip_ref: (i, 0))),
        out_shape=jax.ShapeDtypeStruct((r2, cols), F32), compiler_params=_params(),
    )(chip, own, *([recv] * N_CHIPS))


def _adam_update(w, gv, m, v):
    c1 = 1.0 / (1.0 - ADAM_B1 ** ADAM_STEP)
    c2 = 1.0 / (1.0 - ADAM_B2 ** ADAM_STEP)
    nm = ADAM_B1 * m + (1.0 - ADAM_B1) * gv
    nv = ADAM_B2 * v + (1.0 - ADAM_B2) * gv * gv
    return -ADAM_LR * ((nm * c1) / (jnp.sqrt(nv * c2) + ADAM_EPS) + ADAM_WD * w), nm, nv


def _adamw_halves(w, g_mine, g_other, m, v, core, *, name):
    r, cols = w.shape
    r2 = r // 2
    tr = _rtile(r2, 256, 8, row_bytes=4 * cols)
    nt = r2 // tr

    def body(core_ref, w_ref, gm_ref, go_ref, m_ref, v_ref, g_ref, d_ref, nm_ref, nv_ref):
        gv = jnp.where(pl.program_id(0) == core_ref[0], gm_ref[...], go_ref[...])
        g_ref[...] = gv
        d_ref[...], nm_ref[...], nv_ref[...] = _adam_update(w_ref[...], gv, m_ref[...], v_ref[...])

    full = pl.BlockSpec((tr, cols), lambda hf, i, core_ref: (hf * nt + i, 0))
    half = pl.BlockSpec((tr, cols), lambda hf, i, core_ref: (i, 0))
    shape = jax.ShapeDtypeStruct((r, cols), F32)
    return pl.pallas_call(
        body, name=name,
        grid_spec=pltpu.PrefetchScalarGridSpec(
            num_scalar_prefetch=1, grid=(2, nt), in_specs=[full, half, half, full, full], out_specs=[full] * 4),
        out_shape=[shape] * 4, compiler_params=_params(),
    )(core, w, g_mine, g_other, m, v)


def _adamw_split_rows(w, g_mine, g_other, m, v, core, *, name, tc=256):
    r, cols = w.shape
    r2 = g_mine.shape[0]
    tc = _tile(cols, tc)

    def body(core_ref, w_ref, gm_ref, go_ref, m_ref, v_ref, g_ref, d_ref, nm_ref, nv_ref):
        mine_first = core_ref[0] == 0
        for lo, hi, first in ((0, r2, True), (r2, r, False)):
            n = hi - lo
            gm, go = gm_ref[0:n, :], go_ref[0:n, :]
            gv = jnp.where(mine_first, gm, go) if first else jnp.where(mine_first, go, gm)
            g_ref[lo:hi, :] = gv
            d_ref[lo:hi, :], nm_ref[lo:hi, :], nv_ref[lo:hi, :] = _adam_update(
                w_ref[lo:hi, :], gv, m_ref[lo:hi, :], v_ref[lo:hi, :])

    full = pl.BlockSpec((r, tc), lambda j, core_ref: (0, j))
    half = pl.BlockSpec((r2, tc), lambda j, core_ref: (0, j))
    shape = jax.ShapeDtypeStruct((r, cols), F32)
    return pl.pallas_call(
        body, name=name,
        grid_spec=pltpu.PrefetchScalarGridSpec(
            num_scalar_prefetch=1, grid=(cols // tc,), in_specs=[full, half, half, full, full],
            out_specs=[full] * 4),
        out_shape=[shape] * 4, compiler_params=_params(),
    )(core, w, g_mine, g_other, m, v)


def _adamw(w, g, m, v, *, name, rows=256):
    r, cols = w.shape
    tr = _rtile(r, rows, 8)

    def body(w_ref, g_ref, m_ref, v_ref, d_ref, nm_ref, nv_ref):
        d_ref[...], nm_ref[...], nv_ref[...] = _adam_update(w_ref[...], g_ref[...], m_ref[...], v_ref[...])

    blk = pl.BlockSpec((tr, cols), lambda i: (i, 0))
    shape = jax.ShapeDtypeStruct((r, cols), F32)
    return pl.pallas_call(
        body, name=name, grid=(r // tr,), in_specs=[blk] * 4, out_specs=[blk] * 3,
        out_shape=[shape] * 3, compiler_params=_params(),
    )(w, g, m, v)


_BIG = (("w_in", 1), ("w_branch_a", 0), ("w_branch_b", 0), ("w_out", 0), ("w_up", 1), ("w_down", 0),
        ("w_ple", 1), ("w_ple_gate", 0))
_SMALL = ("gmlp_ln_g", "gmlp_ln_b", "gmlp_w_s", "gmlp_b_s", "norm_ffn_g", "conv_b", "norm_ple_g", "norm_final_g",
          "b_f", "norm_mix_g")
N_LATE = 2
_WEIGHTS = ("norm_mix_g", "w_in", "b_f", "gmlp_ln_g", "gmlp_ln_b", "gmlp_w_s", "gmlp_b_s", "w_branch_a",
            "w_branch_b", "w_out", "norm_ffn_g", "w_up", "conv_w", "conv_b", "w_down", "norm_ple_g", "w_ple",
            "w_ple_gate", "norm_final_g")
_PACK_ROWS = 8


def _pack(arrays):
    parts = []
    for a in arrays:
        flat = a.reshape(-1)
        unit = _PACK_ROWS * LANES
        flat = jnp.pad(flat, (0, (-flat.shape[0]) % unit))
        parts.append(flat.reshape(-1, LANES))
    return jnp.concatenate(parts, axis=0)


def _unpack(packed, shapes):
    out, row = [], 0
    for shp in shapes:
        size = math.prod(shp)
        rows = -(-size // (_PACK_ROWS * LANES)) * _PACK_ROWS
        out.append(packed[row:row + rows].reshape(-1)[:size].reshape(shp))
        row += rows
    return out


def _take_cols(parts, lo, hi):
    out, start = [], 0
    for a in parts:
        width = a.shape[1]
        a0, a1 = max(lo, start) - start, min(hi, start + width) - start
        if a1 > a0:
            out.append(a if (a0, a1) == (0, width) else a[:, a0:a1])
        start += width
    return out[0] if len(out) == 1 else jnp.concatenate(out, axis=1)


def _take_rows(parts, lo, hi):
    out, start = [], 0
    for a in parts:
        height = a.shape[0]
        a0, a1 = max(lo, start) - start, min(hi, start + height) - start
        if a1 > a0:
            out.append(a if (a0, a1) == (0, height) else a[a0:a1])
        start += height
    return out[0] if len(out) == 1 else jnp.concatenate(out, axis=0)


def _assemble(gathered, axis):
    n, r, cols = gathered.shape
    if axis == 0:
        return gathered.reshape(n * r, cols)
    return _take_cols([gathered[j] for j in range(n)], 0, n * cols)


def _to_chunks(parts, axis):
    rows, total = parts[0].shape[0], sum(a.shape[1] for a in parts)
    if axis == 0:
        r, cols = rows // N_CHIPS, total
        chunks = _take_cols(parts, 0, total).reshape(N_CHIPS, r, cols)
    else:
        r, cols = rows, total // N_CHIPS
        chunks = jnp.stack([_take_cols(parts, j * cols, (j + 1) * cols) for j in range(N_CHIPS)])
    return chunks.reshape(N_CHIPS, 2, r // 2, cols)


def kernel(x, p, norm_mix_g, w_in, b_f, gmlp_ln_g, gmlp_ln_b, gmlp_w_s, gmlp_b_s, w_branch_a, w_branch_b, w_out, norm_ffn_g, w_up, conv_w, conv_b, w_down, norm_ple_g, w_ple, w_ple_gate, norm_final_g, loss_target, m_norm_mix_g, m_w_in, m_b_f, m_gmlp_ln_g, m_gmlp_ln_b, m_gmlp_w_s, m_gmlp_b_s, m_w_branch_a, m_w_branch_b, m_w_out, m_norm_ffn_g, m_w_up, m_conv_w, m_conv_b, m_w_down, m_norm_ple_g, m_w_ple, m_w_ple_gate, m_norm_final_g, v_norm_mix_g, v_w_in, v_b_f, v_gmlp_ln_g, v_gmlp_ln_b, v_gmlp_w_s, v_gmlp_b_s, v_w_branch_a, v_w_branch_b, v_w_out, v_norm_ffn_g, v_w_up, v_conv_w, v_conv_b, v_w_down, v_norm_ple_g, v_w_ple, v_w_ple_gate, v_norm_final_g):
    args = dict(locals())
    wt = {n: args[n] for n in _WEIGHTS}
    mom = {n: args["m_" + n] for n in _WEIGHTS}
    var = {n: args["v_" + n] for n in _WEIGHTS}
    chip = 2 * lax.axis_index("x") + lax.axis_index("y")
    core = lax.axis_index("c").astype(jnp.int32).reshape(1)

    chip1 = chip.astype(jnp.int32).reshape(1)
    device = 2 * chip + lax.axis_index("c")
    axis_of = dict(_BIG)
    names = [n for n, _ in _BIG]
    put_mine = lambda land, mine: lax.dynamic_update_index_in_dim(land, mine, chip, 0)

    shard_in = w_in[0].astype(BF16)
    sems_in = _copies_start([shard_in], [lax.empty((N_CHIPS,) + shard_in.shape, BF16)], _gather_plan([True]), 3,
                            name="gather_start_in")
    _, wt["w_in"], mom["w_in"], var["w_in"] = lax.optimization_barrier((sems_in[4], w_in, m_w_in, v_w_in))
    shards = [wt[n][0].astype(BF16) for n in names[1:]] + [conv_w[0]]
    halved = [True] * len(names[1:]) + [False]
    lands = [lax.empty((N_CHIPS,) + a.shape, a.dtype) for a in shards]
    send_sems, recv_sems, srcs, lands, rest_token = _copies_start(
        shards, lands, _gather_plan(halved), 3 * len(shards), name="gather_start_rest", after=[sems_in[4]])
    o1 = 2 * GMLP_WIDTH
    o2 = o1 + 3 * FOX_WIDTH
    o3 = o2 + FOX_HEADS
    fpad = ((0, 0), (0, LANES - FOX_HEADS))
    w = {
        "conv_b": conv_b, "norm_mix_g": norm_mix_g, "norm_ffn_g": norm_ffn_g, "norm_ple_g": norm_ple_g,
        "norm_final_g": norm_final_g.reshape(1, D_MODEL), "b_f": jnp.pad(b_f, fpad),
        "gmlp_ln_g": gmlp_ln_g, "gmlp_ln_b": gmlp_ln_b, "gmlp_w_s": gmlp_w_s[0],
        "gmlp_b_s_t": jnp.pad(gmlp_b_s[0].T, ((0, 0), (0, LANES - GMLP_GROUPS))),
        "first_dep": rest_token,
    }

    def get_w_in(after):
        early = [a.reshape(a.shape[-2:]) for a in (wt["w_in"], mom["w_in"], var["w_in"])]
        _, got = _copies_wait(sems_in[0], sems_in[1], sems_in[2], sems_in[3], _gather_plan([True]), 0,
                              [after] + early, name="gather_wait_in")
        got = _forward_halves(got, name="gather_forward_in")
        slots = put_mine(got[0], shard_in)
        slots = [slots[j] for j in range(N_CHIPS)]
        return {"w_uv": _take_cols(slots, 0, o1), "w_qkv": _take_cols(slots, o1, o2),
                "w_f": jnp.pad(_take_cols(slots, o2, o3), fpad), "w_g": _take_cols(slots, o3, o3 + 2 * D_MODEL)}

    def start_w_rest(after):
        _, got = _copies_wait(send_sems, recv_sems, srcs, lands, _gather_plan(halved), 0, [after],
                              name="gather_wait_rest")
        ssem, rsem, _, fwd, token = _copies_start([], got[:-1], _forward_plan, 3 * len(got[:-1]),
                                                  name="gather_forward_start")
        pending["forward"] = (ssem, rsem, fwd, got[-1])
        return token

    def get_w_rest(after):
        ssem, rsem, fwd, whole = pending["forward"]
        _, fwd = _copies_wait(ssem, rsem, [], fwd, _forward_plan, 0, after, name="gather_forward_wait")
        got = fwd + [whole]
        slots = {n: put_mine(got[i], shards[i]) for i, n in enumerate(names[1:])}
        full = {n: _assemble(slots[n], axis_of[n]) for n in names[1:] if n != "w_up"}
        up = [slots["w_up"][j] for j in range(N_CHIPS)]
        return {"w_branch_a": full["w_branch_a"], "w_branch_b": full["w_branch_b"], "w_out": full["w_out"],
                "w_up_a": _take_cols(up, 0, D_FF), "w_up_b": _take_cols(up, D_FF, 2 * D_FF),
                "w_down": full["w_down"], "w_ple": full["w_ple"], "w_ple_gate": full["w_ple_gate"],
                "conv_w": _assemble(put_mine(got[-1], shards[-1]), 1)}

    grads, delta, new_m, new_v = {}, {}, {}, {}
    pending = {}

    def to_chunks(n, gr):
        return _to_chunks(gr if isinstance(gr, list) else [gr], axis_of[n])

    def pair_start(group, gfull, tag):
        chunks = [to_chunks(n, gfull[n]) for n in group]
        empty = [lax.empty((N_CHIPS,) + a.shape[2:], a.dtype) for a in chunks]
        ssem, rsem, own, recv, token = _copies_start(chunks, empty, _pair_plan, N_CHIPS * len(group),
                                                     name="grad_pair_start_" + tag)
        pending["pair_" + tag] = (ssem, rsem, own, recv)
        return token

    def reduce_start(group, gfull, tag, after=None):
        if after is None:
            chunks = [to_chunks(n, gfull[n]) for n in group]
            from_sibling = _pair_exchange(chunks, name="grad_pair_exchange_" + tag)
        else:
            ssem, rsem, own, recv = pending["pair_" + tag]
            chunks, from_sibling = _copies_wait(ssem, rsem, own, recv, _pair_plan, 0, after,
                                                name="grad_pair_wait_" + tag)
        pair_sums = [_pair_add(chunks[i], from_sibling[i], core, name="grad_pair_add_" + n) for i, n in enumerate(group)]
        empty = [lax.empty(a.shape, a.dtype) for a in pair_sums]
        ssem, rsem, own, recv, token = _copies_start(pair_sums, empty, _chip_plan, 3 * len(group),
                                                     name="grad_chip_start_" + tag)
        pending[tag] = (ssem, rsem, own, recv)
        return token

    def reduce_sum(group, tag, after):
        ssem, rsem, own, recv = pending[tag]
        own, recv = _copies_wait(ssem, rsem, own, recv, _chip_plan, 0, after, name="grad_chip_wait_" + tag)
        halves = [_chip_sum(own[i], recv[i], chip1, name="grad_chip_sum_" + n) for i, n in enumerate(group)]
        empty = [lax.empty(a.shape, a.dtype) for a in halves]
        ssem, rsem, halves, other, token = _copies_start(halves, empty, _share_plan, len(group),
                                                        name="grad_share_start_" + tag)
        pending["share_" + tag] = (ssem, rsem, halves, other)
        return token

    def reduce_update(group, tag, after):
        ssem, rsem, halves, other = pending["share_" + tag]
        halves, other_halves = _copies_wait(ssem, rsem, halves, other, _share_plan, 0, after,
                                            name="grad_share_wait_" + tag)
        for i, n in enumerate(group):
            shp = wt[n].shape
            outs = _adamw_halves(wt[n].reshape(shp[-2:]), halves[i], other_halves[i], mom[n].reshape(shp[-2:]),
                                 var[n].reshape(shp[-2:]), core, name="adamw_" + n)
            grads[n], delta[n], new_m[n], new_v[n] = (o.reshape(shp) for o in outs)
        return new_v[group[-1]]

    def reduce_finish(group, tag, after):
        ssem, rsem, own, recv = pending[tag]
        own, recv = _copies_wait(ssem, rsem, own, recv, _chip_plan, 0, after, name="grad_chip_wait_" + tag)
        halves = [_chip_sum(own[i], recv[i], chip1, name="grad_chip_sum_" + n) for i, n in enumerate(group)]
        other_halves = _pair_share(halves, name="grad_pair_share_" + tag)
        for i, n in enumerate(group):
            shp = wt[n].shape
            outs = _adamw_halves(wt[n].reshape(shp[-2:]), halves[i], other_halves[i], mom[n].reshape(shp[-2:]),
                                 var[n].reshape(shp[-2:]), core, name="adamw_" + n)
            grads[n], delta[n], new_m[n], new_v[n] = (o.reshape(shp) for o in outs)
        return new_v[group[-1]]

    ffn_group = ("w_up", "w_down", "w_ple", "w_ple_gate", "w_branch_a", "w_branch_b", "w_out")
    mix_group = ("w_in",)

    def on_grads_ffn(g):
        gfull = dict(g)
        gfull["w_up"] = [g["w_up_a"], g["w_up_b"]]
        return pair_start(ffn_group, gfull, "ffn")

    def on_grads_small(g):
        chip_token = reduce_start(ffn_group, None, "ffn", after=[g["gmlp_w_s"]])
        vec = _pack([g[n] for n in _SMALL[:-N_LATE]] + [g["conv_w"]])
        ssem, rsem, own, recv, token = _copies_start(
            [vec], [lax.empty((8,) + vec.shape, F32)], _all_plan, 7, name="small_start", after=[chip_token])
        pending["small"] = (ssem, rsem, own, recv)
        return token

    def on_grads_mix(g):
        gfull = dict(g)
        gfull["w_in"] = [g["w_uv"], g["w_qkv"], g["w_f"][:, :FOX_HEADS], g["w_g"]]
        return pair_start(mix_group, gfull, "mix")

    loss, grad_x, g = _device_step(x[0], p[0, 0], loss_target[0], w, get_w_in, (start_w_rest, get_w_rest), on_grads_ffn,
                                   on_grads_small, on_grads_mix)

    mix_token = reduce_start(mix_group, None, "mix", after=[grad_x])
    share_token = reduce_sum(ffn_group, "ffn", [mix_token])
    ssem, rsem, own, recv = pending["small"]
    own, recv = _copies_wait(ssem, rsem, own, recv, _all_plan, 0, [share_token], name="small_wait")
    vec_early = _sum_slots(lax.dynamic_update_index_in_dim(recv[0], own[0], device, 0), F32, name="small_sum")
    vec_late = _pack([g["b_f"][:, :FOX_HEADS], g["norm_mix_g"]])
    vec_late = _sum_slots(_all_exchange(vec_late, name="small_exchange_late"), F32, name="small_sum_late")
    early_rows = _pack([wt[n] for n in _SMALL[:-N_LATE]]).shape[0]
    vec = jnp.concatenate([vec_early[:early_rows], vec_late], axis=0)
    for n, a in zip(_SMALL, _unpack(vec, [wt[n].shape for n in _SMALL])):
        grads[n] = a
    conv_w_grad = _unpack(vec_early[early_rows:], [(3, 2 * D_FF)])[0]
    grads["conv_w"] = lax.dynamic_slice_in_dim(conv_w_grad, chip * conv_w.shape[2], conv_w.shape[2], axis=1).reshape(conv_w.shape)

    shp = conv_w.shape
    outs = _adamw(conv_w.reshape(shp[-2:]), grads["conv_w"].reshape(shp[-2:]), m_conv_w.reshape(shp[-2:]),
                  v_conv_w.reshape(shp[-2:]), name="adamw_conv_w")
    delta["conv_w"], new_m["conv_w"], new_v["conv_w"] = (o.reshape(shp) for o in outs)
    outs = _adamw(_pack([wt[n] for n in _SMALL]), vec, _pack([mom[n] for n in _SMALL]),
                  _pack([var[n] for n in _SMALL]), name="adamw_small", rows=2048)
    small_done = outs[2]
    for d, o in zip((delta, new_m, new_v), outs):
        for n, a in zip(_SMALL, _unpack(o, [wt[n].shape for n in _SMALL])):
            d[n] = a

    ffn_done = reduce_update(ffn_group, "ffn", [small_done])
    reduce_finish(mix_group, "mix", [ffn_done])

    total_loss = lax.psum(loss[0, 0], ("x", "y", "c"))
    return (total_loss, grad_x.reshape(x.shape), *[grads[n] for n in _WEIGHTS], *[delta[n] for n in _WEIGHTS],
            *[new_m[n] for n in _WEIGHTS], *[new_v[n] for n in _WEIGHTS])
```

```python
import functools
import math

import jax
import jax.numpy as jnp
from jax import lax
from jax.experimental import pallas as pl
from jax.experimental.pallas import tpu as pltpu

F32 = jnp.float32
BF16 = jnp.bfloat16

D_MODEL = 1024
EPS = 1e-6
CHUNK = 64
GMLP_GROUPS = 8
GMLP_BLOCK = 128
GMLP_WIDTH = 1024
FOX_HEADS = 16
FOX_HEAD_DIM = 64
FOX_WIDTH = 1024
HEAD_PAIRS = FOX_HEADS // 2
ATT_BLOCK = 128
D_FF = 2816
PLE_DIM = 256
LANES = 128
BF16_TILE_ROWS = 16
N_CHIPS = 4

ADAM_LR = 0.001
ADAM_B1 = 0.9
ADAM_B2 = 0.999
ADAM_EPS = 1e-08
ADAM_WD = 0.01
ADAM_STEP = 10

VMEM_LIMIT = 56 * 1024 * 1024
MESH = pl.DeviceIdType.MESH

_NN = (((1,), (0,)), ((), ()))
_NT = (((1,), (1,)), ((), ()))
_TN = (((0,), (0,)), ((), ()))


def _params(**kw):
    return pltpu.CompilerParams(vmem_limit_bytes=VMEM_LIMIT, **kw)


def _tile(dim, pref):
    if dim <= pref:
        return dim
    t = (pref // LANES) * LANES
    while t >= LANES:
        if dim % t == 0:
            return t
        t -= LANES
    return dim


def _dot(a, b, dn):
    return lax.dot_general(a.astype(BF16), b.astype(BF16), dn, preferred_element_type=F32)


def _gelu(x):
    c = math.sqrt(2.0 / math.pi)
    t = jnp.tanh(c * (x + 0.044715 * x * x * x))
    return 0.5 * x * (1.0 + t)


def _gelu_and_grad(x):
    c = math.sqrt(2.0 / math.pi)
    x2 = x * x
    t = jnp.tanh(c * (x + 0.044715 * x2 * x))
    g = 0.5 * x * (1.0 + t)
    dg = 0.5 * (1.0 + t) + 0.5 * x * (1.0 - t * t) * c * (1.0 + 3.0 * 0.044715 * x2)
    return g, dg


def _sigmoid(x):
    return 1.0 / (1.0 + jnp.exp(-x))


def _mm(a, b, *, mode, out_dtype, name, add=None, tm=512, tn=512, dep=None):
    if mode == "nn":
        m, k = a.shape
        k2, n = b.shape
    elif mode == "nt":
        m, k = a.shape
        n, k2 = b.shape
    else:
        k, m = a.shape
        k2, n = b.shape
    assert k == k2, (name, a.shape, b.shape)
    tm = _tile(m, tm)
    tn = _tile(n, tn)
    dn = {"nn": _NN, "nt": _NT, "tn": _TN}[mode]

    def body(a_ref, b_ref, *rest):
        o_ref = rest[-1]
        acc = _dot(a_ref[...], b_ref[...], dn)
        if add is not None:
            acc = acc + rest[0][...].astype(F32)
        o_ref[...] = acc.astype(o_ref.dtype)

    a_spec = pl.BlockSpec((k, tm), lambda i, j: (0, i)) if mode == "tn" else pl.BlockSpec((tm, k), lambda i, j: (i, 0))
    b_spec = pl.BlockSpec((tn, k), lambda i, j: (j, 0)) if mode == "nt" else pl.BlockSpec((k, tn), lambda i, j: (0, j))
    o_spec = pl.BlockSpec((tm, tn), lambda i, j: (i, j))
    in_specs = [a_spec, b_spec]
    args = [a, b]
    if add is not None:
        in_specs.append(o_spec)
        args.append(add)
    if dep is not None:
        in_specs.append(pl.BlockSpec(memory_space=pl.ANY))
        args.append(dep)
    return pl.pallas_call(
        body, name=name, grid=(m // tm, n // tn), in_specs=in_specs, out_specs=o_spec,
        out_shape=jax.ShapeDtypeStruct((m, n), out_dtype), compiler_params=_params(),
    )(*args)


def _mm_nt_sum(pairs, *, out_dtype, name, tm=256, dep=None):
    m, n = pairs[0][0].shape[0], pairs[0][1].shape[0]
    tm = _tile(m, tm)
    np_ = len(pairs)

    def body(*refs):
        o_ref = refs[-1] if dep is None else refs[-1]
        acc = None
        for p in range(np_):
            part = _dot(refs[2 * p][...], refs[2 * p + 1][...], _NT)
            acc = part if acc is None else acc + part
        o_ref[...] = acc.astype(o_ref.dtype)

    in_specs, args = [], []
    for a, b in pairs:
        assert a.shape[0] == m and b.shape[0] == n and a.shape[1] == b.shape[1], (name, a.shape, b.shape)
        in_specs += [pl.BlockSpec((tm, a.shape[1]), lambda i: (i, 0)), pl.BlockSpec(b.shape, lambda i: (0, 0))]
        args += [a, b]
    if dep is not None:
        in_specs.append(pl.BlockSpec(memory_space=pl.ANY))
        args.append(dep)
    return pl.pallas_call(
        body, name=name, grid=(m // tm,), in_specs=in_specs, out_specs=pl.BlockSpec((tm, n), lambda i: (i, 0)),
        out_shape=jax.ShapeDtypeStruct((m, n), out_dtype), compiler_params=_params(),
    )(*args)


def _rms_fwd(x, g, *, name, tm=256, dep=None):
    s, d = x.shape
    tm = _tile(s, tm)

    def body(x_ref, g_ref, *rest):
        h_ref = rest[-1]
        xv = x_ref[...]
        r = lax.rsqrt(jnp.mean(xv * xv, axis=-1, keepdims=True) + EPS)
        h_ref[...] = (xv * r * g_ref[...]).astype(h_ref.dtype)

    deps = [] if dep is None else [dep]
    return pl.pallas_call(
        body, name=name, grid=(s // tm,),
        in_specs=[pl.BlockSpec((tm, d), lambda i: (i, 0)), pl.BlockSpec((1, d), lambda i: (0, 0))]
                 + [pl.BlockSpec(memory_space=pl.ANY)] * len(deps),
        out_specs=pl.BlockSpec((tm, d), lambda i: (i, 0)),
        out_shape=jax.ShapeDtypeStruct((s, d), BF16), compiler_params=_params(),
    )(x, g, *deps)


def _rms_bwd(x, g, dh, dres, *, name, tm=256, dep=None):
    s, d = x.shape
    tm = _tile(s, tm)
    deps = [] if dep is None else [dep]

    def body(x_ref, g_ref, dh_ref, dres_ref, *rest):
        dx_ref, dxb_ref, dg_ref = rest[len(deps):]
        xv = x_ref[...]
        r = lax.rsqrt(jnp.mean(xv * xv, axis=-1, keepdims=True) + EPS)
        xhat = xv * r
        dhv = dh_ref[...].astype(F32)
        dyg = dhv * g_ref[...]
        dx = dres_ref[...] + r * (dyg - xhat * jnp.mean(dyg * xhat, axis=-1, keepdims=True))
        dx_ref[...] = dx
        dxb_ref[...] = dx.astype(dxb_ref.dtype)

        @pl.when(pl.program_id(0) == 0)
        def _():
            dg_ref[...] = jnp.zeros_like(dg_ref)

        dg_ref[...] += jnp.sum(dhv * xhat, axis=0, keepdims=True)

    row = pl.BlockSpec((tm, d), lambda i: (i, 0))
    vec = pl.BlockSpec((1, d), lambda i: (0, 0))
    return pl.pallas_call(
        body, name=name, grid=(s // tm,),
        in_specs=[row, vec, row, row] + [pl.BlockSpec(memory_space=pl.ANY)] * len(deps), out_specs=[row, row, vec],
        out_shape=[jax.ShapeDtypeStruct((s, d), F32), jax.ShapeDtypeStruct((s, d), BF16),
                   jax.ShapeDtypeStruct((1, d), F32)],
        compiler_params=_params(),
    )(x, g, dh, dres, *deps)


def _gmlp_mask():
    t = lax.broadcasted_iota(jnp.int32, (GMLP_BLOCK, GMLP_BLOCK), 0)
    s_ = lax.broadcasted_iota(jnp.int32, (GMLP_BLOCK, GMLP_BLOCK), 1)
    return (s_ // CHUNK) <= (t // CHUNK)


def _gmlp_norm(zv, ln_g, ln_b):
    vv, dvv = _gelu_and_grad(zv)
    mu = jnp.mean(vv, axis=-1, keepdims=True)
    xc = vv - mu
    rstd = lax.rsqrt(jnp.mean(xc * xc, axis=-1, keepdims=True) + EPS)
    vhat = xc * rstd
    return vhat * ln_g + ln_b, vhat, rstd, dvv


def _gmlp_fwd(z_uv, ln_g, ln_b, w_s, b_s_t, *, name):
    s = z_uv.shape[0]
    w = GMLP_WIDTH
    gd = w // GMLP_GROUPS

    def body(z_ref, lg_ref, lb_ref, ws_ref, bs_ref, a_ref):
        u = _gelu(z_ref[:, :w].astype(F32))
        vn, _, _, _ = _gmlp_norm(z_ref[:, w:].astype(F32), lg_ref[...], lb_ref[...])
        mask = _gmlp_mask()
        for g in range(GMLP_GROUPS):
            wm = jnp.where(mask, ws_ref[g], 0.0)
            mixed = _dot(wm, vn[:, g * gd:(g + 1) * gd], _NN) + bs_ref[:, g:g + 1]
            a_ref[:, g * gd:(g + 1) * gd] = (u[:, g * gd:(g + 1) * gd] * mixed).astype(a_ref.dtype)

    full = lambda shape: pl.BlockSpec(shape, lambda i: (0,) * len(shape))
    return pl.pallas_call(
        body, name=name, grid=(s // GMLP_BLOCK,),
        in_specs=[pl.BlockSpec((GMLP_BLOCK, 2 * w), lambda i: (i, 0)), full((1, w)), full((1, w)),
                  full((GMLP_GROUPS, GMLP_BLOCK, GMLP_BLOCK)), full((GMLP_BLOCK, LANES))],
        out_specs=pl.BlockSpec((GMLP_BLOCK, w), lambda i: (i, 0)),
        out_shape=jax.ShapeDtypeStruct((s, w), BF16), compiler_params=_params(),
    )(z_uv, ln_g, ln_b, w_s, b_s_t)


def _gmlp_bwd(z_uv, da, ln_g, ln_b, w_s, b_s_t, *, name):
    s = z_uv.shape[0]
    w = GMLP_WIDTH
    gd = w // GMLP_GROUPS

    def body(z_ref, da_ref, lg_ref, lb_ref, ws_ref, bs_ref, dz_ref, dws_ref, dbs_ref, dlg_ref, dlb_ref):
        @pl.when(pl.program_id(0) == 0)
        def _():
            dws_ref[...] = jnp.zeros_like(dws_ref)
            dbs_ref[...] = jnp.zeros_like(dbs_ref)
            dlg_ref[...] = jnp.zeros_like(dlg_ref)
            dlb_ref[...] = jnp.zeros_like(dlb_ref)

        u, du_dz = _gelu_and_grad(z_ref[:, :w].astype(F32))
        lg = lg_ref[...]
        vn, vhat, rstd, dvv_dz = _gmlp_norm(z_ref[:, w:].astype(F32), lg, lb_ref[...])
        dav = da_ref[...].astype(F32)
        mask = _gmlp_mask()
        lane = lax.broadcasted_iota(jnp.int32, (GMLP_BLOCK, LANES), 1)
        dvn_parts = []
        dbs = jnp.zeros((GMLP_BLOCK, LANES), F32)
        for g in range(GMLP_GROUPS):
            sl = slice(g * gd, (g + 1) * gd)
            wm = jnp.where(mask, ws_ref[g], 0.0)
            vn_g = vn[:, sl]
            mixed = _dot(wm, vn_g, _NN) + bs_ref[:, g:g + 1]
            dmixed = dav[:, sl] * u[:, sl]
            dz_ref[:, sl] = (dav[:, sl] * mixed * du_dz[:, sl]).astype(dz_ref.dtype)
            dvn_parts.append(_dot(wm, dmixed, _TN))
            dws_ref[g] += jnp.where(mask, _dot(dmixed, vn_g, _NT), 0.0)
            dbs = dbs + jnp.where(lane == g, jnp.sum(dmixed, axis=-1, keepdims=True), 0.0)
        dbs_ref[...] += dbs
        dvn = jnp.concatenate(dvn_parts, axis=-1)
        dlg_ref[...] += jnp.sum(dvn * vhat, axis=0, keepdims=True)
        dlb_ref[...] += jnp.sum(dvn, axis=0, keepdims=True)
        dyg = dvn * lg
        dvv = rstd * (dyg - jnp.mean(dyg, axis=-1, keepdims=True)
                      - vhat * jnp.mean(dyg * vhat, axis=-1, keepdims=True))
        dz_ref[:, w:] = (dvv * dvv_dz).astype(dz_ref.dtype)

    full = lambda shape: pl.BlockSpec(shape, lambda i: (0,) * len(shape))
    return pl.pallas_call(
        body, name=name, grid=(s // GMLP_BLOCK,),
        in_specs=[pl.BlockSpec((GMLP_BLOCK, 2 * w), lambda i: (i, 0)),
                  pl.BlockSpec((GMLP_BLOCK, w), lambda i: (i, 0)), full((1, w)), full((1, w)),
                  full((GMLP_GROUPS, GMLP_BLOCK, GMLP_BLOCK)), full((GMLP_BLOCK, LANES))],
        out_specs=[pl.BlockSpec((GMLP_BLOCK, 2 * w), lambda i: (i, 0)),
                   full((GMLP_GROUPS, GMLP_BLOCK, GMLP_BLOCK)), full((GMLP_BLOCK, LANES)),
                   full((1, w)), full((1, w))],
        out_shape=[jax.ShapeDtypeStruct((s, 2 * w), BF16),
                   jax.ShapeDtypeStruct((GMLP_GROUPS, GMLP_BLOCK, GMLP_BLOCK), F32),
                   jax.ShapeDtypeStruct((GMLP_BLOCK, LANES), F32),
                   jax.ShapeDtypeStruct((1, w), F32), jax.ShapeDtypeStruct((1, w), F32)],
        compiler_params=_params(),
    )(z_uv, da, ln_g, ln_b, w_s, b_s_t)


def _tri(lower):
    r = lax.broadcasted_iota(jnp.int32, (ATT_BLOCK, ATT_BLOCK), 0)
    c = lax.broadcasted_iota(jnp.int32, (ATT_BLOCK, ATT_BLOCK), 1)
    return jnp.where((c <= r) if lower else (c >= r), 1.0, 0.0).astype(F32)


def _log_sigmoid(x):
    return jnp.minimum(x, 0.0) - jnp.log(1.0 + jnp.exp(-jnp.abs(x)))


def _fox_cum(f, b_f, *, name):
    s = f.shape[0]
    nb = s // ATT_BLOCK

    def body(f_ref, b_ref, cb_ref, ct_ref, carry):
        @pl.when(pl.program_id(0) == 0)
        def _():
            carry[...] = jnp.zeros_like(carry)

        lf = _log_sigmoid(f_ref[...] + b_ref[...])
        cum = lax.dot_general(_tri(True), lf, _NN, precision=lax.Precision.HIGHEST,
                              preferred_element_type=F32) + carry[...]
        carry[...] = cum[ATT_BLOCK - 1:ATT_BLOCK, :]
        for h in range(FOX_HEADS):
            cb_ref[h] = jnp.broadcast_to(cum[:, h:h + 1], (ATT_BLOCK, LANES))
        ct_ref[...] = cum.T

    return pl.pallas_call(
        body, name=name, grid=(nb,),
        in_specs=[pl.BlockSpec((ATT_BLOCK, LANES), lambda i: (i, 0)), pl.BlockSpec((1, LANES), lambda i: (0, 0))],
        out_specs=[pl.BlockSpec((FOX_HEADS, ATT_BLOCK, LANES), lambda i: (0, i, 0)),
                   pl.BlockSpec((LANES, ATT_BLOCK), lambda i: (0, i))],
        out_shape=[jax.ShapeDtypeStruct((FOX_HEADS, s, LANES), F32), jax.ShapeDtypeStruct((LANES, s), F32)],
        scratch_shapes=[pltpu.VMEM((1, LANES), F32)], compiler_params=_params(),
    )(f, b_f)


def _fox_dlogit(dcum_t, f, b_f, *, name):
    s = f.shape[0]
    nb = s // ATT_BLOCK

    def body(dc_ref, f_ref, b_ref, df_ref, db_ref, carry):
        @pl.when(pl.program_id(0) == 0)
        def _():
            carry[...] = jnp.zeros_like(carry)
            db_ref[...] = jnp.zeros_like(db_ref)

        d = dc_ref[...].T
        dlog = lax.dot_general(_tri(False), d, _NN, precision=lax.Precision.HIGHEST,
                               preferred_element_type=F32) + carry[...]
        carry[...] = dlog[0:1, :]
        df = dlog * (1.0 - _sigmoid(f_ref[...] + b_ref[...]))
        df_ref[...] = df
        db_ref[...] += jnp.sum(df, axis=0, keepdims=True)

    rev = lambda i: nb - 1 - i
    return pl.pallas_call(
        body, name=name, grid=(nb,),
        in_specs=[pl.BlockSpec((LANES, ATT_BLOCK), lambda i: (0, rev(i))),
                  pl.BlockSpec((ATT_BLOCK, LANES), lambda i: (rev(i), 0)),
                  pl.BlockSpec((1, LANES), lambda i: (0, 0))],
        out_specs=[pl.BlockSpec((ATT_BLOCK, LANES), lambda i: (rev(i), 0)),
                   pl.BlockSpec((1, LANES), lambda i: (0, 0))],
        out_shape=[jax.ShapeDtypeStruct((s, LANES), F32), jax.ShapeDtypeStruct((1, LANES), F32)],
        scratch_shapes=[pltpu.VMEM((1, LANES), F32)], compiler_params=_params(),
    )(dcum_t, f, b_f)


def _causal(qi, ki):
    r = lax.broadcasted_iota(jnp.int32, (ATT_BLOCK, ATT_BLOCK), 0) + qi * ATT_BLOCK
    c = lax.broadcasted_iota(jnp.int32, (ATT_BLOCK, ATT_BLOCK), 1) + ki * ATT_BLOCK
    return c <= r


def _head_mask():
    return lax.broadcasted_iota(jnp.int32, (1, LANES), 1) < FOX_HEAD_DIM


def _attn_fwd(qkv, cum_b, cum_r, *, name):
    s = qkv.shape[0]
    nq = s // ATT_BLOCK
    scale = FOX_HEAD_DIM ** -0.5
    npair = HEAD_PAIRS

    def body(q_ref, k_ref, v_ref, cq_ref, ck_ref, o_ref, l_ref):
        qi = pl.program_id(1)
        m0 = _head_mask()
        q2 = q_ref[...]
        zero = jnp.zeros_like(q2)
        qs = (jnp.where(m0, q2, zero), jnp.where(m0, zero, q2))
        cqs = (cq_ref[0], cq_ref[1])

        def step(ki, carry, masked):
            off = pl.multiple_of(ki * ATT_BLOCK, ATT_BLOCK)
            k2 = k_ref[pl.ds(off, ATT_BLOCK), :]
            v2 = v_ref[pl.ds(off, ATT_BLOCK), :]
            out = []
            for hh in range(2):
                m, l, acc = carry[hh]
                sc = _dot(qs[hh], k2, _NT) * scale + (cqs[hh] - ck_ref[hh:hh + 1, pl.ds(off, ATT_BLOCK)])
                if masked:
                    sc = jnp.where(_causal(qi, ki), sc, -1e30)
                m_new = jnp.maximum(m, jnp.max(sc, axis=-1, keepdims=True))
                alpha = jnp.exp(m - m_new)
                p = jnp.exp(sc - m_new)
                l = alpha * l + jnp.sum(p, axis=-1, keepdims=True)
                acc = alpha * acc + _dot(p, v2, _NN)
                out.append((m_new, l, acc))
            return tuple(out)

        init = tuple((jnp.full((ATT_BLOCK, 1), -1e30, F32), jnp.zeros((ATT_BLOCK, 1), F32),
                      jnp.zeros((ATT_BLOCK, LANES), F32)) for _ in range(2))
        carry = lax.fori_loop(0, qi, lambda ki, c: step(ki, c, False), init)
        (ma, la, acca), (mb, lb, accb) = step(qi, carry, True)
        o_ref[...] = jnp.where(m0, acca / la, accb / lb).astype(o_ref.dtype)
        l_ref[0] = jnp.broadcast_to(ma + jnp.log(la), (ATT_BLOCK, LANES))
        l_ref[1] = jnp.broadcast_to(mb + jnp.log(lb), (ATT_BLOCK, LANES))

    stat = pl.BlockSpec((None, 2, ATT_BLOCK, LANES), lambda j, i: (j, 0, i, 0))
    row = pl.BlockSpec((None, 2, s), lambda j, i: (j, 0, 0))
    return pl.pallas_call(
        body, name=name, grid=(npair, nq),
        in_specs=[pl.BlockSpec((ATT_BLOCK, LANES), lambda j, i: (i, j)),
                  pl.BlockSpec((s, LANES), lambda j, i: (0, npair + j)),
                  pl.BlockSpec((s, LANES), lambda j, i: (0, 2 * npair + j)),
                  stat, row],
        out_specs=[pl.BlockSpec((ATT_BLOCK, LANES), lambda j, i: (i, j)), stat],
        out_shape=[jax.ShapeDtypeStruct((s, FOX_WIDTH), BF16),
                   jax.ShapeDtypeStruct((npair, 2, s, LANES), F32)],
        compiler_params=_params(),
    )(qkv, qkv, qkv, cum_b, cum_r)


def _attn_delta(qkv, do, lse_b, cum_b, cum_r, *, name):
    s = qkv.shape[0]
    nq = s // ATT_BLOCK
    scale = FOX_HEAD_DIM ** -0.5
    npair = HEAD_PAIRS

    def body(q_ref, k_ref, v_ref, do_ref, l_ref, cq_ref, ck_ref, d_ref):
        qi = pl.program_id(1)
        m0 = _head_mask()
        q2 = q_ref[...]
        do2 = do_ref[...]
        qs = (jnp.where(m0, q2, jnp.zeros_like(q2)), jnp.where(m0, jnp.zeros_like(q2), q2))
        dos = (jnp.where(m0, do2, jnp.zeros_like(do2)), jnp.where(m0, jnp.zeros_like(do2), do2))

        def step(ki, carry, masked):
            off = pl.multiple_of(ki * ATT_BLOCK, ATT_BLOCK)
            k2 = k_ref[pl.ds(off, ATT_BLOCK), :]
            v2 = v_ref[pl.ds(off, ATT_BLOCK), :]
            out = []
            for hh in range(2):
                sc = _dot(qs[hh], k2, _NT) * scale + (cq_ref[hh] - ck_ref[hh:hh + 1, pl.ds(off, ATT_BLOCK)])
                p = jnp.exp(sc - l_ref[hh])
                if masked:
                    p = jnp.where(_causal(qi, ki), p, 0.0)
                out.append(carry[hh] + jnp.sum(p * _dot(dos[hh], v2, _NT), axis=-1, keepdims=True))
            return tuple(out)

        init = (jnp.zeros((ATT_BLOCK, 1), F32), jnp.zeros((ATT_BLOCK, 1), F32))
        carry = lax.fori_loop(0, qi, lambda ki, c: step(ki, c, False), init)
        da, db = step(qi, carry, True)
        d_ref[0] = jnp.broadcast_to(da, (ATT_BLOCK, LANES))
        d_ref[1] = jnp.broadcast_to(db, (ATT_BLOCK, LANES))

    stat = pl.BlockSpec((None, 2, ATT_BLOCK, LANES), lambda j, i: (j, 0, i, 0))
    return pl.pallas_call(
        body, name=name, grid=(npair, nq),
        in_specs=[pl.BlockSpec((ATT_BLOCK, LANES), lambda j, i: (i, j)),
                  pl.BlockSpec((s, LANES), lambda j, i: (0, npair + j)),
                  pl.BlockSpec((s, LANES), lambda j, i: (0, 2 * npair + j)),
                  pl.BlockSpec((ATT_BLOCK, LANES), lambda j, i: (i, j)),
                  stat, stat, pl.BlockSpec((None, 2, s), lambda j, i: (j, 0, 0))],
        out_specs=stat,
        out_shape=jax.ShapeDtypeStruct((npair, 2, s, LANES), F32), compiler_params=_params(),
    )(qkv, qkv, qkv, do, lse_b, cum_b, cum_r)


def _attn_bwd(qkv, do, lse_b, delta_b, cum_b, cum_r, *, name):
    s = qkv.shape[0]
    nq = s // ATT_BLOCK
    scale = FOX_HEAD_DIM ** -0.5
    npair = HEAD_PAIRS

    def body(q_ref, k_ref, v_ref, do_ref, l_ref, dl_ref, cq_ref, ck_ref, dq_ref, dk_ref, dv_ref, dc_ref):
        ki = pl.program_id(1)
        m0 = _head_mask()
        k2 = k_ref[...]
        v2 = v_ref[...]
        koff = pl.multiple_of(ki * ATT_BLOCK, ATT_BLOCK)

        @pl.when(ki == 0)
        def _():
            dq_ref[...] = jnp.zeros_like(dq_ref)

        def step(qi, carry, masked):
            off = pl.multiple_of(qi * ATT_BLOCK, ATT_BLOCK)
            q2 = q_ref[pl.ds(off, ATT_BLOCK), :]
            do2 = do_ref[pl.ds(off, ATT_BLOCK), :]
            qzero = jnp.zeros_like(q2)
            dzero = jnp.zeros_like(do2)
            out = []
            dqs = []
            for hh in range(2):
                dk_acc, dv_acc, dc_acc = carry[hh]
                keep = m0 if hh == 0 else jnp.logical_not(m0)
                qh = jnp.where(keep, q2, qzero)
                doh = jnp.where(keep, do2, dzero)
                sc = _dot(qh, k2, _NT) * scale + (cq_ref[hh, pl.ds(off, ATT_BLOCK), :]
                                                 - ck_ref[hh:hh + 1, pl.ds(koff, ATT_BLOCK)])
                p = jnp.exp(sc - l_ref[hh, pl.ds(off, ATT_BLOCK), :])
                if masked:
                    p = jnp.where(_causal(qi, ki), p, 0.0)
                dp = _dot(doh, v2, _NT)
                ds = p * (dp - dl_ref[hh, pl.ds(off, ATT_BLOCK), :])
                dv_acc = dv_acc + _dot(p, do2, _TN)
                dk_acc = dk_acc + _dot(ds, q2, _TN)
                dc_acc = dc_acc - jnp.sum(ds, axis=0, keepdims=True)
                dqs.append(_dot(ds, k2, _NN))
                out.append((dk_acc, dv_acc, dc_acc))
            dq_ref[pl.ds(off, ATT_BLOCK), :] += jnp.where(m0, dqs[0], dqs[1]) * scale
            return tuple(out)

        init = tuple((jnp.zeros((ATT_BLOCK, LANES), F32), jnp.zeros((ATT_BLOCK, LANES), F32),
                      jnp.zeros((1, ATT_BLOCK), F32)) for _ in range(2))
        carry = step(ki, init, True)
        (dka, dva, dca), (dkb, dvb, dcb) = lax.fori_loop(ki + 1, nq, lambda qi, c: step(qi, c, False), carry)
        dk_ref[...] = (jnp.where(m0, dka, dkb) * scale).astype(dk_ref.dtype)
        dv_ref[...] = jnp.where(m0, dva, dvb).astype(dv_ref.dtype)
        dc_ref[0:1, :] = dca
        dc_ref[1:2, :] = dcb

    stat = pl.BlockSpec((None, 2, s, LANES), lambda j, i: (j, 0, 0, 0))
    colfull = lambda base: pl.BlockSpec((s, LANES), lambda j, i: (0, base + j))
    colblk = lambda base: pl.BlockSpec((ATT_BLOCK, LANES), lambda j, i: (i, base + j))
    return pl.pallas_call(
        body, name=name, grid=(npair, nq),
        in_specs=[colfull(0), colblk(npair), colblk(2 * npair), colfull(0), stat, stat, stat,
                  pl.BlockSpec((None, 2, s), lambda j, i: (j, 0, 0))],
        out_specs=[colfull(0), colblk(0), colblk(0), pl.BlockSpec((None, 2, ATT_BLOCK), lambda j, i: (j, 0, i))],
        out_shape=[jax.ShapeDtypeStruct((s, FOX_WIDTH), F32), jax.ShapeDtypeStruct((s, FOX_WIDTH), BF16),
                   jax.ShapeDtypeStruct((s, FOX_WIDTH), BF16), jax.ShapeDtypeStruct((npair, 2, s), F32)],
        compiler_params=_params(),
    )(qkv, qkv, qkv, do, lse_b, delta_b, cum_b, cum_r)


ATT_TQ = 256
ATT_TK = 256
ATT_SCALE = FOX_HEAD_DIM ** -0.5
assert ATT_SCALE == 0.125 and ATT_TQ == ATT_TK


def _causal_t(qi, ki):
    kpos = lax.broadcasted_iota(jnp.int32, (ATT_TK, ATT_TQ), 0) + ki * ATT_TK
    qpos = lax.broadcasted_iota(jnp.int32, (ATT_TK, ATT_TQ), 1) + qi * ATT_TQ
    return kpos <= qpos


def _row_mask():
    return lax.broadcasted_iota(jnp.int32, (LANES, 1), 0) < FOX_HEAD_DIM


def _lane_tile(a, width):
    return a if a.shape[1] == width else jnp.tile(a, (1, width // a.shape[1]))


def _transpose_bf16(a):
    return a.astype(F32).T.astype(BF16)


def _attn_fwd_t(qkv, cum_b, cum_r, *, name):
    s = qkv.shape[0]
    nq = s // ATT_TQ
    npair = HEAD_PAIRS

    def body(q_ref, k_ref, v_ref, cq_ref, ck_ref, o_ref, ot_ref, l_ref, vt_ref):
        qi = pl.program_id(1)
        rows = _row_mask()

        @pl.when(qi == 0)
        def _():
            vt_ref[...] = _transpose_bf16(v_ref[...])

        qt = _transpose_bf16(q_ref[...]) * ATT_SCALE
        zero = jnp.zeros_like(qt)
        qts = (jnp.where(rows, qt, zero), jnp.where(rows, zero, qt))

        def step(ki, carry, masked):
            off = pl.multiple_of(ki * ATT_TK, ATT_TK)
            k2 = k_ref[pl.ds(off, ATT_TK), :]
            vt = vt_ref[:, pl.ds(off, ATT_TK)]
            out = []
            for hh in range(2):
                m, l, acc = carry[hh]
                bias = cq_ref[hh:hh + 1, :] - _lane_tile(ck_ref[hh, pl.ds(off, ATT_TK), :], ATT_TQ)
                sc = _dot(k2, qts[hh], _NN) + bias
                if masked:
                    sc = jnp.where(_causal_t(qi, ki), sc, -1e30)
                m_new = jnp.maximum(m, jnp.max(sc, axis=0, keepdims=True))
                alpha = jnp.exp(m - m_new)
                p = jnp.exp(sc - m_new)
                l = alpha * l + jnp.sum(p, axis=0, keepdims=True)
                p_hi = p.astype(BF16)
                p_lo = (p - p_hi.astype(F32)).astype(BF16)
                acc = alpha * acc + (_dot(vt, p_hi, _NN) + _dot(vt, p_lo, _NN))
                out.append((m_new, l, acc))
            return tuple(out)

        init = tuple((jnp.full((1, ATT_TQ), -1e30, F32), jnp.zeros((1, ATT_TQ), F32),
                      jnp.zeros((LANES, ATT_TQ), F32)) for _ in range(2))
        carry = lax.fori_loop(0, qi // 2, lambda kk, c: step(2 * kk + 1, step(2 * kk, c, False), False), init)
        carry = lax.cond(qi % 2 == 1, lambda c: step(qi - 1, c, False), lambda c: c, carry)
        (ma, la, acca), (mb, lb, accb) = step(qi, carry, True)
        ot = jnp.where(rows, acca / la, accb / lb)
        ot_ref[...] = ot
        o_ref[...] = ot.T.astype(o_ref.dtype)
        l_ref[0:1, :] = ma + jnp.log(la)
        l_ref[1:2, :] = mb + jnp.log(lb)

    row = pl.BlockSpec((None, 2, ATT_TQ), lambda j, i: (j, 0, i))
    return pl.pallas_call(
        body, name=name, grid=(npair, nq),
        in_specs=[pl.BlockSpec((ATT_TQ, LANES), lambda j, i: (i, j)),
                  pl.BlockSpec((s, LANES), lambda j, i: (0, npair + j)),
                  pl.BlockSpec((s, LANES), lambda j, i: (0, 2 * npair + j)),
                  row, pl.BlockSpec((None, 2, s, LANES), lambda j, i: (j, 0, 0, 0))],
        out_specs=[pl.BlockSpec((ATT_TQ, LANES), lambda j, i: (i, j)),
                   pl.BlockSpec((LANES, ATT_TQ), lambda j, i: (j, i)), row],
        out_shape=[jax.ShapeDtypeStruct((s, FOX_WIDTH), BF16), jax.ShapeDtypeStruct((FOX_WIDTH, s), F32),
                   jax.ShapeDtypeStruct((npair, 2, s), F32)],
        scratch_shapes=[pltpu.VMEM((LANES, s), BF16)],
        compiler_params=_params(),
    )(qkv, qkv, qkv, cum_r, cum_b)


def _attn_delta_t(do_t, o_t, *, name):
    s = o_t.shape[1]
    ts = _tile(s, 512)

    def body(do_ref, o_ref, d_ref):
        prod = do_ref[...].astype(F32) * o_ref[...]
        d_ref[0:1, :] = jnp.sum(prod[:FOX_HEAD_DIM], axis=0, keepdims=True)
        d_ref[1:2, :] = jnp.sum(prod[FOX_HEAD_DIM:], axis=0, keepdims=True)

    blk = pl.BlockSpec((LANES, ts), lambda j, i: (j, i))
    return pl.pallas_call(
        body, name=name, grid=(HEAD_PAIRS, s // ts), in_specs=[blk, blk],
        out_specs=pl.BlockSpec((None, 2, ts), lambda j, i: (j, 0, i)),
        out_shape=jax.ShapeDtypeStruct((HEAD_PAIRS, 2, s), F32), compiler_params=_params(),
    )(do_t, o_t)


def _attn_bwd_t(qkv, do, o_t, lse, cum_b, cum_r, *, name, dep=None):
    s = qkv.shape[0]
    nq = s // ATT_TQ
    npair = HEAD_PAIRS

    deps = [] if dep is None else [dep]

    def body(q_ref, k_ref, v_ref, do_ref, ot_ref, l_ref, cq_ref, ck_ref, *rest):
        dq_ref, dk_ref, dv_ref, dc_ref, qt_ref, dot_ref, dqt_ref, dl_ref = rest[len(deps):]
        ki = pl.program_id(1)
        m0 = _head_mask()
        rows = _row_mask()
        k2 = k_ref[...]
        v2 = v_ref[...]
        kt = _transpose_bf16(k2)
        ks = k2 * ATT_SCALE
        kz, vz = jnp.zeros_like(k2), jnp.zeros_like(v2)
        khs = (jnp.where(m0, ks, kz), jnp.where(m0, kz, ks))
        vhs = (jnp.where(m0, v2, vz), jnp.where(m0, vz, v2))
        cks = tuple(_lane_tile(ck_ref[hh], ATT_TQ) for hh in range(2))

        @pl.when(ki == 0)
        def _():
            dqt_ref[...] = jnp.zeros_like(dqt_ref)
            qt_ref[...] = _transpose_bf16(q_ref[...])
            do_t = do_ref[...].astype(F32).T
            dot_ref[...] = do_t.astype(BF16)
            prod = do_t * ot_ref[...]
            dl_ref[0:1, :] = jnp.sum(prod[:FOX_HEAD_DIM], axis=0, keepdims=True)
            dl_ref[1:2, :] = jnp.sum(prod[FOX_HEAD_DIM:], axis=0, keepdims=True)

        def step(qi, carry, masked):
            off = pl.multiple_of(qi * ATT_TQ, ATT_TQ)
            q2 = q_ref[pl.ds(off, ATT_TQ), :]
            do2 = do_ref[pl.ds(off, ATT_TQ), :]
            qt = qt_ref[:, pl.ds(off, ATT_TQ)]
            dot_ = dot_ref[:, pl.ds(off, ATT_TQ)]
            out, dqs = [], []
            for hh in range(2):
                dk_acc, dv_acc, dc_acc = carry[hh]
                sc = _dot(khs[hh], qt, _NN) + (cq_ref[hh:hh + 1, pl.ds(off, ATT_TQ)] - cks[hh])
                p = jnp.exp(sc - l_ref[hh:hh + 1, pl.ds(off, ATT_TQ)])
                if masked:
                    p = jnp.where(_causal_t(qi, ki), p, 0.0)
                dp = _dot(vhs[hh], dot_, _NN)
                ds = p * (dp - dl_ref[hh:hh + 1, pl.ds(off, ATT_TQ)])
                dc_acc = dc_acc - jnp.sum(ds, axis=1, keepdims=True)
                dss = (ds * ATT_SCALE).astype(BF16)
                dv_acc = dv_acc + _dot(p, do2, _NN)
                dk_acc = dk_acc + _dot(dss, q2, _NN)
                dqs.append(_dot(kt, dss, _NN))
                out.append((dk_acc, dv_acc, dc_acc))
            dqt_ref[:, pl.ds(off, ATT_TQ)] += jnp.where(rows, dqs[0], dqs[1])
            return tuple(out)

        init = tuple((jnp.zeros((ATT_TK, LANES), F32), jnp.zeros((ATT_TK, LANES), F32),
                      jnp.zeros((ATT_TK, 1), F32)) for _ in range(2))
        carry = step(ki, init, True)
        rest = nq - 1 - ki
        carry = lax.fori_loop(
            0, rest // 2, lambda t, c: step(ki + 2 + 2 * t, step(ki + 1 + 2 * t, c, False), False), carry)
        carry = lax.cond(rest % 2 == 1, lambda c: step(nq - 1, c, False), lambda c: c, carry)
        (dka, dva, dca), (dkb, dvb, dcb) = carry
        dk_ref[...] = jnp.where(m0, dka, dkb).astype(dk_ref.dtype)
        dv_ref[...] = jnp.where(m0, dva, dvb).astype(dv_ref.dtype)
        dc_ref[0] = jnp.broadcast_to(dca, (ATT_TK, LANES))
        dc_ref[1] = jnp.broadcast_to(dcb, (ATT_TK, LANES))

        @pl.when(ki == nq - 1)
        def _():
            dq_ref[...] = dqt_ref[...].T.astype(dq_ref.dtype)

    colfull = lambda base: pl.BlockSpec((s, LANES), lambda j, i: (0, base + j))
    colblk = lambda base: pl.BlockSpec((ATT_TK, LANES), lambda j, i: (i, base + j))
    stat = pl.BlockSpec((None, 2, s), lambda j, i: (j, 0, 0))
    bcast = pl.BlockSpec((None, 2, ATT_TK, LANES), lambda j, i: (j, 0, i, 0))
    grad = jax.ShapeDtypeStruct((s, FOX_WIDTH), BF16)
    return pl.pallas_call(
        body, name=name, grid=(npair, nq),
        in_specs=[colfull(0), colblk(npair), colblk(2 * npair), colfull(0),
                  pl.BlockSpec((LANES, s), lambda j, i: (j, 0)), stat, stat, bcast]
                 + [pl.BlockSpec(memory_space=pl.ANY)] * len(deps),
        out_specs=[colfull(0), colblk(0), colblk(0), bcast],
        out_shape=[grad, grad, grad, jax.ShapeDtypeStruct((npair, 2, s, LANES), F32)],
        scratch_shapes=[pltpu.VMEM((LANES, s), BF16), pltpu.VMEM((LANES, s), BF16), pltpu.VMEM((LANES, s), F32),
                        pltpu.VMEM((2, s), F32)],
        compiler_params=_params(),
    )(qkv, qkv, qkv, do, o_t, lse, cum_r, cum_b, *deps)


def _merge_fwd(zg, ya, yb, *, name, tm=256):
    s, d = ya.shape
    tm = _tile(s, tm)

    def body(zg_ref, ya_ref, yb_ref, m_ref):
        ga = _sigmoid(zg_ref[:, :d].astype(F32))
        gb = _sigmoid(zg_ref[:, d:].astype(F32))
        m_ref[...] = (ga * ya_ref[...].astype(F32) + gb * yb_ref[...].astype(F32)).astype(m_ref.dtype)

    row = pl.BlockSpec((tm, d), lambda i: (i, 0))
    row2 = pl.BlockSpec((tm, 2 * d), lambda i: (i, 0))
    return pl.pallas_call(
        body, name=name, grid=(s // tm,), in_specs=[row2, row, row], out_specs=row,
        out_shape=jax.ShapeDtypeStruct((s, d), BF16), compiler_params=_params(),
    )(zg, ya, yb)


def _merge_bwd(dm, zg, ya, yb, *, name, tm=256):
    s, d = ya.shape
    tm = _tile(s, tm)

    def body(dm_ref, zg_ref, ya_ref, yb_ref, dzg_ref, dya_ref, dyb_ref):
        dmv = dm_ref[...].astype(F32)
        ga = _sigmoid(zg_ref[:, :d].astype(F32))
        gb = _sigmoid(zg_ref[:, d:].astype(F32))
        dzg_ref[:, :d] = (dmv * ya_ref[...].astype(F32) * ga * (1.0 - ga)).astype(dzg_ref.dtype)
        dzg_ref[:, d:] = (dmv * yb_ref[...].astype(F32) * gb * (1.0 - gb)).astype(dzg_ref.dtype)
        dya_ref[...] = (dmv * ga).astype(dya_ref.dtype)
        dyb_ref[...] = (dmv * gb).astype(dyb_ref.dtype)

    row = pl.BlockSpec((tm, d), lambda i: (i, 0))
    row2 = pl.BlockSpec((tm, 2 * d), lambda i: (i, 0))
    return pl.pallas_call(
        body, name=name, grid=(s // tm,), in_specs=[row, row2, row, row], out_specs=[row2, row, row],
        out_shape=[jax.ShapeDtypeStruct((s, 2 * d), BF16), jax.ShapeDtypeStruct((s, d), BF16),
                   jax.ShapeDtypeStruct((s, d), BF16)],
        compiler_params=_params(),
    )(dm, zg, ya, yb)


SUBLANES = 8


def _shift_down(u, k, row):
    rolled = pltpu.roll(u, k, 0)
    head = jnp.where(row[:SUBLANES] >= k, rolled[:SUBLANES], 0.0)
    return jnp.concatenate([head, rolled[SUBLANES:]], axis=0)


def _shift_up(u, k, row):
    n = u.shape[0]
    rolled = pltpu.roll(u, n - k, 0)
    tail = jnp.where(row[n - SUBLANES:] < n - k, rolled[n - SUBLANES:], 0.0)
    return jnp.concatenate([rolled[:n - SUBLANES], tail], axis=0)


def _conv_act_fwd(up_a, up_b, cw_a, cw_b, cb_a, cb_b, *, name, tc=128):
    s, f = up_a.shape
    tc = _tile(f, tc)

    def body(ua_ref, ub_ref, wa_ref, wb_ref, ba_ref, bb_ref, act_ref):
        row = lax.broadcasted_iota(jnp.int32, (s, tc), 0)

        def conv(u_ref, w_ref, b_ref):
            u = u_ref[...].astype(F32)
            return (b_ref[...] + w_ref[0:1, :] * _shift_down(u, 2, row)
                    + w_ref[1:2, :] * _shift_down(u, 1, row) + w_ref[2:3, :] * u)

        ca = conv(ua_ref, wa_ref, ba_ref)
        cb = conv(ub_ref, wb_ref, bb_ref)
        act_ref[...] = (_gelu(ca) * cb).astype(act_ref.dtype)

    col = pl.BlockSpec((s, tc), lambda j: (0, j))
    w3 = pl.BlockSpec((3, tc), lambda j: (0, j))
    b1 = pl.BlockSpec((1, tc), lambda j: (0, j))
    return pl.pallas_call(
        body, name=name, grid=(f // tc,), in_specs=[col, col, w3, w3, b1, b1], out_specs=col,
        out_shape=jax.ShapeDtypeStruct((s, f), BF16), compiler_params=_params(),
    )(up_a, up_b, cw_a, cw_b, cb_a, cb_b)


def _conv_act_bwd(up_a, up_b, dact, cw_a, cw_b, cb_a, cb_b, *, name, tc=128):
    s, f = up_a.shape
    tc = _tile(f, tc)

    def body(ua_ref, ub_ref, da_ref, wa_ref, wb_ref, ba_ref, bb_ref, dua_ref, dub_ref, dwa_ref, dwb_ref):
        row = lax.broadcasted_iota(jnp.int32, (s, tc), 0)

        def conv(u_ref, w_ref, b_ref):
            u = u_ref[...].astype(F32)
            u1 = _shift_down(u, 1, row)
            u2 = _shift_down(u, 2, row)
            return u, u1, u2, b_ref[...] + w_ref[0:1, :] * u2 + w_ref[1:2, :] * u1 + w_ref[2:3, :] * u

        def back(dc, taps, w_ref, du_ref, dw_ref):
            u, u1, u2 = taps
            dw_ref[0:1, :] = jnp.sum(dc * u2, axis=0, keepdims=True)
            dw_ref[1:2, :] = jnp.sum(dc * u1, axis=0, keepdims=True)
            dw_ref[2:3, :] = jnp.sum(dc * u, axis=0, keepdims=True)
            dw_ref[3:4, :] = jnp.sum(dc, axis=0, keepdims=True)
            du = (w_ref[2:3, :] * dc + w_ref[1:2, :] * _shift_up(dc, 1, row)
                  + w_ref[0:1, :] * _shift_up(dc, 2, row))
            du_ref[...] = du.astype(du_ref.dtype)

        ua, ua1, ua2, ca = conv(ua_ref, wa_ref, ba_ref)
        ub, ub1, ub2, cb = conv(ub_ref, wb_ref, bb_ref)
        g, dg = _gelu_and_grad(ca)
        dact_v = da_ref[...].astype(F32)
        back(dact_v * cb * dg, (ua, ua1, ua2), wa_ref, dua_ref, dwa_ref)
        back(dact_v * g, (ub, ub1, ub2), wb_ref, dub_ref, dwb_ref)

    col = pl.BlockSpec((s, tc), lambda j: (0, j))
    w3 = pl.BlockSpec((3, tc), lambda j: (0, j))
    w4 = pl.BlockSpec((4, tc), lambda j: (0, j))
    b1 = pl.BlockSpec((1, tc), lambda j: (0, j))
    return pl.pallas_call(
        body, name=name, grid=(f // tc,), in_specs=[col, col, col, w3, w3, b1, b1],
        out_specs=[col, col, w4, w4],
        out_shape=[jax.ShapeDtypeStruct((s, f), BF16), jax.ShapeDtypeStruct((s, f), BF16),
                   jax.ShapeDtypeStruct((4, f), F32), jax.ShapeDtypeStruct((4, f), F32)],
        compiler_params=_params(),
    )(up_a, up_b, dact, cw_a, cw_b, cb_a, cb_b)


def _ple_final(x2, ple, zp, target, g_final, *, name, tm=256):
    s, d = x2.shape
    tm = _tile(s, tm)

    def body(x_ref, ple_ref, zp_ref, t_ref, g_ref, dx_ref, dple_ref, dzp_ref, dg_ref, loss_ref):
        @pl.when(pl.program_id(0) == 0)
        def _():
            dg_ref[...] = jnp.zeros_like(dg_ref)
            loss_ref[...] = jnp.zeros_like(loss_ref)

        gp = _sigmoid(zp_ref[...].astype(F32))
        plev = ple_ref[...].astype(F32)
        x3 = x_ref[...] + plev * gp
        r = lax.rsqrt(jnp.mean(x3 * x3, axis=-1, keepdims=True) + EPS)
        xhat = x3 * r
        gv = g_ref[...]
        diff = xhat * gv - t_ref[...]
        loss_ref[...] += 0.5 * jnp.sum(jnp.mean(diff * diff, axis=-1, keepdims=True), axis=0, keepdims=True)
        dy = diff * (1.0 / d)
        dg_ref[...] += jnp.sum(dy * xhat, axis=0, keepdims=True)
        dyg = dy * gv
        dx3 = r * (dyg - xhat * jnp.mean(dyg * xhat, axis=-1, keepdims=True))
        dx_ref[...] = dx3
        dple_ref[...] = (dx3 * gp).astype(dple_ref.dtype)
        dzp_ref[...] = (dx3 * plev * gp * (1.0 - gp)).astype(dzp_ref.dtype)

    row = pl.BlockSpec((tm, d), lambda i: (i, 0))
    vec = pl.BlockSpec((1, d), lambda i: (0, 0))
    return pl.pallas_call(
        body, name=name, grid=(s // tm,), in_specs=[row, row, row, row, vec],
        out_specs=[row, row, row, vec, pl.BlockSpec((1, LANES), lambda i: (0, 0))],
        out_shape=[jax.ShapeDtypeStruct((s, d), F32), jax.ShapeDtypeStruct((s, d), BF16),
                   jax.ShapeDtypeStruct((s, d), BF16), jax.ShapeDtypeStruct((1, d), F32),
                   jax.ShapeDtypeStruct((1, LANES), F32)],
        compiler_params=_params(),
    )(x2, ple, zp, target, g_final)


def _device_step(x, p, target, w, get_w_in=None, get_w_rest=None, on_grads_ffn=None, on_grads_small=None,
                 on_grads_mix=None, on_after_dh=None):
    s = x.shape[0]
    g = {}
    w = dict(w)

    h = _rms_fwd(x, w["norm_mix_g"], name="rms_mix", dep=w.get("first_dep"))
    if get_w_in is not None:
        w.update(get_w_in(h))
    qkv = _mm(h, w["w_qkv"], mode="nn", out_dtype=BF16, name="proj_qkv", tm=1024)
    f = _mm(h, w["w_f"], mode="nn", out_dtype=F32, name="proj_f", tm=1024)

    cum_b, cum_t = _fox_cum(f, w["b_f"], name="fox_cum")
    cum_b = cum_b.reshape(HEAD_PAIRS, 2, s, LANES)
    cum_r = cum_t[:FOX_HEADS].reshape(HEAD_PAIRS, 2, s)
    b, o_t, lse = _attn_fwd_t(qkv, cum_b, cum_r, name="attn_fwd")

    dep = get_w_rest[0](b) if get_w_rest is not None else None
    z_uv = _mm(h, w["w_uv"], mode="nn", out_dtype=BF16, name="proj_uv", tm=1024, dep=dep)
    zg = _mm(h, w["w_g"], mode="nn", out_dtype=BF16, name="proj_gate", tm=1024, dep=dep)
    a = _gmlp_fwd(z_uv, w["gmlp_ln_g"], w["gmlp_ln_b"], w["gmlp_w_s"], w["gmlp_b_s_t"], name="gmlp_fwd")
    if get_w_rest is not None:
        w.update(get_w_rest[1]([a, zg]))

    ya = _mm(a, w["w_branch_a"], mode="nn", out_dtype=BF16, name="branch_a", tm=1024)
    yb = _mm(b, w["w_branch_b"], mode="nn", out_dtype=BF16, name="branch_b", tm=1024)
    merged = _merge_fwd(zg, ya, yb, name="merge_fwd")
    x1 = _mm(merged, w["w_out"], mode="nn", out_dtype=F32, name="proj_out", add=x, tm=1024)

    h2 = _rms_fwd(x1, w["norm_ffn_g"], name="rms_ffn")
    up_a = _mm(h2, w["w_up_a"], mode="nn", out_dtype=BF16, name="up_a", tm=1024, tn=D_FF // 2)
    up_b = _mm(h2, w["w_up_b"], mode="nn", out_dtype=BF16, name="up_b", tm=1024, tn=D_FF // 2)
    cw, cb = w["conv_w"], w["conv_b"]
    conv_args = (cw[:, :D_FF], cw[:, D_FF:], cb[:, :D_FF], cb[:, D_FF:])
    act = _conv_act_fwd(up_a, up_b, *conv_args, name="conv_act_fwd")
    x2 = _mm(act, w["w_down"], mode="nn", out_dtype=F32, name="down", add=x1, tm=512)

    h3 = _rms_fwd(x2, w["norm_ple_g"], name="rms_ple")
    ple = _mm(p, w["w_ple"], mode="nn", out_dtype=BF16, name="ple_proj", tm=1024)
    zp = _mm(h3, w["w_ple_gate"], mode="nn", out_dtype=BF16, name="ple_gate", tm=1024)
    dx3, dple, dzp, g["norm_final_g"], loss = _ple_final(x2, ple, zp, target, w["norm_final_g"], name="ple_final")

    g["w_ple"] = _mm(p, dple, mode="tn", out_dtype=BF16, name="dw_ple")
    g["w_ple_gate"] = _mm(h3, dzp, mode="tn", out_dtype=BF16, name="dw_ple_gate")
    dh3 = _mm(dzp, w["w_ple_gate"], mode="nt", out_dtype=BF16, name="dh3")
    dx2, dx2_b, g["norm_ple_g"] = _rms_bwd(x2, w["norm_ple_g"], dh3, dx3, name="rms_ple_bwd")

    g["w_down"] = _mm(act, dx2_b, mode="tn", out_dtype=BF16, name="dw_down", tm=D_FF // 2)
    dact = _mm(dx2_b, w["w_down"], mode="nt", out_dtype=BF16, name="dact", tn=D_FF // 2)
    dup_a, dup_b, dcw_a, dcw_b = _conv_act_bwd(up_a, up_b, dact, *conv_args, name="conv_act_bwd")
    g["conv_w"] = jnp.concatenate([dcw_a[:3], dcw_b[:3]], axis=1)
    g["conv_b"] = jnp.concatenate([dcw_a[3:], dcw_b[3:]], axis=1)
    g["w_up_a"] = _mm(h2, dup_a, mode="tn", out_dtype=BF16, name="dw_up_a", tn=D_FF // 2)
    g["w_up_b"] = _mm(h2, dup_b, mode="tn", out_dtype=BF16, name="dw_up_b", tn=D_FF // 2)
    dh2 = _mm_nt_sum([(dup_a, w["w_up_a"]), (dup_b, w["w_up_b"])], out_dtype=BF16, name="dh2")
    dx1, dx1_b, g["norm_ffn_g"] = _rms_bwd(x1, w["norm_ffn_g"], dh2, dx2, name="rms_ffn_bwd")

    g["w_out"] = _mm(merged, dx1_b, mode="tn", out_dtype=BF16, name="dw_out")
    dmerged = _mm(dx1_b, w["w_out"], mode="nt", out_dtype=BF16, name="dmerged")
    dzg, dya, dyb = _merge_bwd(dmerged, zg, ya, yb, name="merge_bwd")
    g["w_branch_a"] = _mm(a, dya, mode="tn", out_dtype=BF16, name="dw_branch_a")
    g["w_branch_b"] = _mm(b, dyb, mode="tn", out_dtype=BF16, name="dw_branch_b")
    dep = on_grads_ffn(g) if on_grads_ffn is not None else None
    da = _mm(dya, w["w_branch_a"], mode="nt", out_dtype=BF16, name="da", dep=dep)
    db = _mm(dyb, w["w_branch_b"], mode="nt", out_dtype=BF16, name="db")

    dz_uv, g["gmlp_w_s"], dbs_t, g["gmlp_ln_g"], g["gmlp_ln_b"] = _gmlp_bwd(
        z_uv, da, w["gmlp_ln_g"], w["gmlp_ln_b"], w["gmlp_w_s"], w["gmlp_b_s_t"], name="gmlp_bwd")
    g["gmlp_b_s"] = dbs_t[:, :GMLP_GROUPS].T
    dep = on_grads_small(g) if on_grads_small is not None else None

    dq, dk, dv, dcum_b = _attn_bwd_t(qkv, db, o_t, lse, cum_b, cum_r, name="attn_bwd", dep=dep)
    dcum_t = jnp.pad(dcum_b[..., 0].reshape(FOX_HEADS, s), ((0, LANES - FOX_HEADS), (0, 0)))
    df, g["b_f"] = _fox_dlogit(dcum_t, f, w["b_f"], name="fox_dlogit")
    dqkv = jnp.concatenate([dq, dk, dv], axis=1)

    g["w_uv"] = _mm(h, dz_uv, mode="tn", out_dtype=BF16, name="dw_uv")
    g["w_qkv"] = _mm(h, dqkv, mode="tn", out_dtype=BF16, name="dw_qkv")
    g["w_f"] = _mm(h, df, mode="tn", out_dtype=BF16, name="dw_f")
    g["w_g"] = _mm(h, dzg, mode="tn", out_dtype=BF16, name="dw_g")
    dep = on_grads_mix(g) if on_grads_mix is not None else None
    dh = _mm_nt_sum([(dz_uv, w["w_uv"]), (dqkv, w["w_qkv"]), (df, w["w_f"]), (dzg, w["w_g"])],
                    out_dtype=BF16, name="dh", dep=dep)
    dep = on_after_dh(dh) if on_after_dh is not None else None
    dx0, _, g["norm_mix_g"] = _rms_bwd(x, w["norm_mix_g"], dh, dx1, name="rms_mix_bwd", dep=dep)
    return loss, dx0, g


def _coords():
    return lax.axis_index("x"), lax.axis_index("y"), lax.axis_index("c")


def _other_chips(x, y):
    return [(1 - x, y), (x, 1 - y), (1 - x, 1 - y)]


def _remote(src, dst, send_sem, recv_sem, dev):
    return pltpu.make_async_remote_copy(src_ref=src, dst_ref=dst, send_sem=send_sem, recv_sem=recv_sem,
                                        device_id=dev, device_id_type=MESH)


_ANY = pl.BlockSpec(memory_space=pl.ANY)


def _gather_weights(halved, whole, *, name):
    nh, n = len(halved), len(halved) + len(whole)
    arrays = list(halved) + list(whole)

    def body(*refs):
        ins, outs = refs[:n], refs[n:2 * n]
        send_sems, recv_sems = refs[2 * n:]
        x, y, c = _coords()
        me, sib = 2 * x + y, (x, y, 1 - c)
        chips = _other_chips(x, y)

        def half(i, which):
            h = ins[i].shape[0] // 2
            return pl.ds(pl.multiple_of(which * h, 16), h)

        sends = []
        for i in range(n):
            src, dst = (ins[i].at[half(i, c)], outs[i].at[me, half(i, c)]) if i < nh else (ins[i], outs[i].at[me])
            for k, (cx, cy) in enumerate(chips):
                cp = _remote(src, dst, send_sems.at[i, k], recv_sems.at[i, k], (cx, cy, c))
                cp.start()
                sends.append(cp)
        for i in range(n):
            for k, (cx, cy) in enumerate(chips):
                got = outs[i].at[2 * cx + cy, half(i, c)] if i < nh else outs[i].at[2 * cx + cy]
                _remote(got, got, send_sems.at[i, k], recv_sems.at[i, k], sib).wait_recv()
                if i < nh:
                    cp = _remote(got, got, send_sems.at[i, 3 + k], recv_sems.at[i, 3 + k], sib)
                    cp.start()
                    sends.append(cp)
        for i in range(nh):
            for k, (cx, cy) in enumerate(chips):
                got = outs[i].at[2 * cx + cy, half(i, 1 - c)]
                _remote(got, got, send_sems.at[i, 3 + k], recv_sems.at[i, 3 + k], sib).wait_recv()
        for cp in sends:
            cp.wait_send()

    outs = pl.pallas_call(
        body, name=name, in_specs=[_ANY] * n, out_specs=[_ANY] * n,
        out_shape=[jax.ShapeDtypeStruct((N_CHIPS,) + a.shape, a.dtype) for a in arrays],
        scratch_shapes=[pltpu.SemaphoreType.DMA((n, 6)), pltpu.SemaphoreType.DMA((n, 6))],
        compiler_params=_params(),
    )(*arrays)
    chip = 2 * lax.axis_index("x") + lax.axis_index("y")
    return [lax.dynamic_update_index_in_dim(o, a, chip, 0) for o, a in zip(outs, arrays)]


def _pair_exchange(gs, *, name):
    n = len(gs)

    def body(*refs):
        ins, outs = refs[:n], refs[n:2 * n]
        send_sems, recv_sems = refs[2 * n:]
        x, y, c = _coords()
        copies = []
        for i in range(n):
            for j in range(N_CHIPS):
                cp = _remote(ins[i].at[j, 1 - c], outs[i].at[j], send_sems.at[i, j], recv_sems.at[i, j], (x, y, 1 - c))
                cp.start()
                copies.append(cp)
        for cp in copies:
            cp.wait()

    return pl.pallas_call(
        body, name=name, in_specs=[_ANY] * n, out_specs=[_ANY] * n,
        out_shape=[jax.ShapeDtypeStruct((N_CHIPS,) + a.shape[2:], a.dtype) for a in gs],
        scratch_shapes=[pltpu.SemaphoreType.DMA((n, N_CHIPS)), pltpu.SemaphoreType.DMA((n, N_CHIPS))],
        compiler_params=_params(),
    )(*gs)


def _chip_exchange(ss, *, name):
    n = len(ss)

    def body(*refs):
        ins, outs = refs[:n], refs[n:2 * n]
        send_sems, recv_sems = refs[2 * n:]
        x, y, c = _coords()
        me = 2 * x + y
        chips = _other_chips(x, y)
        sends = []
        for i in range(n):
            for k, (cx, cy) in enumerate(chips):
                cp = _remote(ins[i].at[2 * cx + cy], outs[i].at[me], send_sems.at[i, k], recv_sems.at[i, k], (cx, cy, c))
                cp.start()
                sends.append(cp)
        for i in range(n):
            for k, (cx, cy) in enumerate(chips):
                got = outs[i].at[2 * cx + cy]
                _remote(got, got, send_sems.at[i, k], recv_sems.at[i, k], (cx, cy, c)).wait_recv()
        for cp in sends:
            cp.wait_send()

    return pl.pallas_call(
        body, name=name, in_specs=[_ANY] * n, out_specs=[_ANY] * n,
        out_shape=[jax.ShapeDtypeStruct(a.shape, a.dtype) for a in ss],
        scratch_shapes=[pltpu.SemaphoreType.DMA((n, 3)), pltpu.SemaphoreType.DMA((n, 3))],
        compiler_params=_params(),
    )(*ss)


def _pair_share(hs, *, name):
    n = len(hs)

    def body(*refs):
        ins, outs = refs[:n], refs[n:2 * n]
        send_sems, recv_sems = refs[2 * n:]
        x, y, c = _coords()
        copies = []
        for i in range(n):
            cp = _remote(ins[i], outs[i], send_sems.at[i], recv_sems.at[i], (x, y, 1 - c))
            cp.start()
            copies.append(cp)
        for cp in copies:
            cp.wait()

    return pl.pallas_call(
        body, name=name, in_specs=[_ANY] * n, out_specs=[_ANY] * n,
        out_shape=[jax.ShapeDtypeStruct(a.shape, a.dtype) for a in hs],
        scratch_shapes=[pltpu.SemaphoreType.DMA((n,)), pltpu.SemaphoreType.DMA((n,))],
        compiler_params=_params(),
    )(*hs)


def _all_exchange(vec, *, name):
    def body(v_ref, o_ref, send_sems, recv_sems, local_sem):
        x, y, c = _coords()
        me = 4 * x + 2 * y + c
        local = pltpu.make_async_copy(v_ref, o_ref.at[me], local_sem)
        local.start()
        copies = []
        k = 0
        for dx in (0, 1):
            for dy in (0, 1):
                for dc in (0, 1):
                    if dx or dy or dc:
                        peer = (1 - x if dx else x, 1 - y if dy else y, 1 - c if dc else c)
                        cp = _remote(v_ref, o_ref.at[me], send_sems.at[k], recv_sems.at[k], peer)
                        cp.start()
                        copies.append(cp)
                        k += 1
        for cp in copies:
            cp.wait()
        local.wait()

    return pl.pallas_call(
        body, name=name, in_specs=[_ANY], out_specs=_ANY,
        out_shape=jax.ShapeDtypeStruct((8,) + vec.shape, vec.dtype),
        scratch_shapes=[pltpu.SemaphoreType.DMA((7,)), pltpu.SemaphoreType.DMA((7,)), pltpu.SemaphoreType.DMA(())],
        compiler_params=_params(),
    )(vec)


_HBM = pl.BlockSpec(memory_space=pltpu.HBM)
_SEM = pl.BlockSpec(memory_space=pltpu.SEMAPHORE)
_EFFECT = pltpu.SideEffectType.DATAFLOW_SIDE_EFFECTING


def _copies_start(srcs, lands, plan, n_copies, *, name, after=()):
    ns, n = len(srcs), len(srcs) + len(lands)
    na = len(after)

    def body(*refs):
        send_sems, recv_sems = refs[n + na], refs[n + na + 1]
        token = refs[-1]
        for k, (src, dst, dev) in enumerate(plan(refs[:ns], refs[ns:n])):
            _remote(src, dst, send_sems.at[k], recv_sems.at[k], dev).start()
        token[...] = jnp.zeros_like(token)

    arrays = list(srcs) + list(lands)
    outs = pl.pallas_call(
        body, name=name,
        out_shape=(pltpu.SemaphoreType.DMA((n_copies,)), pltpu.SemaphoreType.DMA((n_copies,)),
                   *[pltpu.HBM(a.shape, a.dtype) for a in arrays], jax.ShapeDtypeStruct((8, LANES), F32)),
        in_specs=[_HBM] * n + [_ANY] * na,
        out_specs=(_SEM, _SEM, *[_HBM] * n, pl.BlockSpec(memory_space=pltpu.VMEM)),
        input_output_aliases={i: 2 + i for i in range(n)},
        compiler_params=pltpu.CompilerParams(has_side_effects=_EFFECT),
    )(*[pltpu.with_memory_space_constraint(a, pltpu.HBM) for a in arrays], *after)
    return outs[0], outs[1], list(outs[2:2 + ns]), list(outs[2 + ns:2 + n]), outs[-1]


def _copies_wait(send_sems, recv_sems, srcs, lands, plan, first, after, *, name):
    ns, n = len(srcs), len(srcs) + len(lands)

    def body(*refs):
        send, recv = refs[n], refs[n + 1]
        for k, (src, dst, dev) in enumerate(plan(refs[:ns], refs[ns:n])):
            cp = _remote(src, dst, send.at[first + k], recv.at[first + k], dev)
            cp.wait_send()
            cp.wait_recv()

    arrays = list(srcs) + list(lands)
    outs = pl.pallas_call(
        body, name=name, out_shape=tuple(pltpu.HBM(a.shape, a.dtype) for a in arrays),
        in_specs=[_HBM] * n + [_SEM, _SEM] + [_ANY] * len(after), out_specs=tuple([_HBM] * n),
        input_output_aliases={i: i for i in range(n)},
        compiler_params=pltpu.CompilerParams(has_side_effects=_EFFECT),
    )(*arrays, send_sems, recv_sems, *after)
    return list(outs[:ns]), list(outs[ns:])


def _gather_plan(halved):
    def plan(srcs, lands):
        x, y, c = _coords()
        me = 2 * x + y
        out = []
        for i, (src, land) in enumerate(zip(srcs, lands)):
            if halved[i]:
                h = src.shape[0] // 2
                rows = pl.ds(pl.multiple_of(c * h, 16), h)
                src, dst = src.at[rows], land.at[me, rows]
            else:
                dst = land.at[me]
            out += [(src, dst, (cx, cy, c)) for cx, cy in _other_chips(x, y)]
        return out
    return plan


def _forward_halves(lands, *, name):
    n = len(lands)

    def body(*refs):
        ins, outs = refs[:n], refs[n:2 * n]
        send_sems, recv_sems = refs[2 * n:]
        x, y, c = _coords()
        copies = []
        for i in range(n):
            h = ins[i].shape[1] // 2
            rows = pl.ds(pl.multiple_of(c * h, 16), h)
            for k, (cx, cy) in enumerate(_other_chips(x, y)):
                cp = _remote(ins[i].at[2 * cx + cy, rows], outs[i].at[2 * cx + cy, rows],
                             send_sems.at[i, k], recv_sems.at[i, k], (x, y, 1 - c))
                cp.start()
                copies.append(cp)
        for cp in copies:
            cp.wait()

    return pl.pallas_call(
        body, name=name, in_specs=[_ANY] * n, out_specs=[_ANY] * n,
        out_shape=[jax.ShapeDtypeStruct(a.shape, a.dtype) for a in lands],
        input_output_aliases={i: i for i in range(n)},
        scratch_shapes=[pltpu.SemaphoreType.DMA((n, 3)), pltpu.SemaphoreType.DMA((n, 3))],
        compiler_params=_params(),
    )(*lands)


def _forward_plan(srcs, lands):
    x, y, c = _coords()
    out = []
    for land in lands:
        h = land.shape[1] // 2
        rows = pl.ds(pl.multiple_of(c * h, 16), h)
        for cx, cy in _other_chips(x, y):
            view = land.at[2 * cx + cy, rows]
            out.append((view, view, (x, y, 1 - c)))
    return out


def _share_plan(srcs, lands):
    x, y, c = _coords()
    return [(src, land, (x, y, 1 - c)) for src, land in zip(srcs, lands)]


def _pair_plan(srcs, lands):
    x, y, c = _coords()
    out = []
    for src, land in zip(srcs, lands):
        out += [(src.at[j, 1 - c], land.at[j], (x, y, 1 - c)) for j in range(N_CHIPS)]
    return out


def _all_plan(srcs, lands):
    x, y, c = _coords()
    me = 4 * x + 2 * y + c
    out = []
    for src, land in zip(srcs, lands):
        for dx in (0, 1):
            for dy in (0, 1):
                for dc in (0, 1):
                    if dx or dy or dc:
                        out.append((src, land.at[me], (1 - x if dx else x, 1 - y if dy else y, 1 - c if dc else c)))
    return out


def _chip_plan(srcs, lands):
    x, y, c = _coords()
    me = 2 * x + y
    out = []
    for src, land in zip(srcs, lands):
        out += [(src.at[2 * cx + cy], land.at[me], (cx, cy, c)) for cx, cy in _other_chips(x, y)]
    return out


ROW_BLOCK_BYTES = 2 * 1024 * 1024


def _rtile(r, pref, mult, row_bytes=None):
    if row_bytes is not None:
        pref = max(pref, ROW_BLOCK_BYTES // row_bytes)
    t = (min(r, pref) // mult) * mult
    while t >= mult:
        if r % t == 0:
            return t
        t -= mult
    return r


def _pair_add(g, recv, core, *, name):
    _, _, r2, cols = g.shape
    tr = _rtile(r2, 256, 16, row_bytes=2 * cols)

    def body(c_ref, g_ref, r_ref, o_ref):
        o_ref[...] = (g_ref[...].astype(F32) + r_ref[...].astype(F32)).astype(o_ref.dtype)

    blk = pl.BlockSpec((None, tr, cols), lambda j, i, c_ref: (j, i, 0))
    return pl.pallas_call(
        body, name=name,
        grid_spec=pltpu.PrefetchScalarGridSpec(
            num_scalar_prefetch=1, grid=(N_CHIPS, r2 // tr),
            in_specs=[pl.BlockSpec((None, None, tr, cols), lambda j, i, c_ref: (j, c_ref[0], i, 0)), blk],
            out_specs=blk),
        out_shape=jax.ShapeDtypeStruct(recv.shape, recv.dtype), compiler_params=_params(),
    )(core, g, recv)


def _sum_slots(a, out_dtype, *, name):
    n, r, cols = a.shape
    whole = n * r * cols * a.dtype.itemsize <= 4 * ROW_BLOCK_BYTES
    tr = r if whole else _rtile(r, 256, 16)

    def body(a_ref, o_ref):
        acc = a_ref[0].astype(F32)
        for j in range(1, n):
            acc = acc + a_ref[j].astype(F32)
        o_ref[...] = acc.astype(o_ref.dtype)

    return pl.pallas_call(
        body, name=name, grid=(r // tr,),
        in_specs=[pl.BlockSpec((n, tr, cols), lambda i: (0, i, 0))],
        out_specs=pl.BlockSpec((tr, cols), lambda i: (i, 0)),
        out_shape=jax.ShapeDtypeStruct((r, cols), out_dtype), compiler_params=_params(),
    )(a)


def _chip_sum(own, recv, chip, *, name):
    _, r2, cols = own.shape
    tr = _rtile(r2, 256, 16, row_bytes=2 * cols)

    def body(chip_ref, own_ref, *rest):
        o_ref = rest[-1]
        acc = None
        for j in range(N_CHIPS):
            term = jnp.where(chip_ref[0] == j, own_ref[...], rest[j][...]).astype(F32)
            acc = term if acc is None else acc + term
        o_ref[...] = acc

    def slot(j):
        return pl.BlockSpec((None, tr, cols),
                            lambda i, chip_ref: (jnp.where(chip_ref[0] == j, (j + 1) % N_CHIPS, j), i, 0))

    return pl.pallas_call(
        body, name=name,
        grid_spec=pltpu.PrefetchScalarGridSpec(
            num_scalar_prefetch=1, grid=(r2 // tr,),
            in_specs=[pl.BlockSpec((None, tr, cols), lambda i, chip_ref: (chip_ref[0], i, 0))]
                     + [slot(j) for j in range(N_CHIPS)],
            out_specs=pl.BlockSpec((tr, cols), lambda i, chip_ref: (i, 0))),
        out_shape=jax.ShapeDtypeStruct((r2, cols), F32), compiler_params=_params(),
    )(chip, own, *([recv] * N_CHIPS))


def _adam_update(w, gv, m, v):
    c1 = 1.0 / (1.0 - ADAM_B1 ** ADAM_STEP)
    c2 = 1.0 / (1.0 - ADAM_B2 ** ADAM_STEP)
    nm = ADAM_B1 * m + (1.0 - ADAM_B1) * gv
    nv = ADAM_B2 * v + (1.0 - ADAM_B2) * gv * gv
    return -ADAM_LR * ((nm * c1) / (jnp.sqrt(nv * c2) + ADAM_EPS) + ADAM_WD * w), nm, nv


def _adamw_halves(w, g_mine, g_other, m, v, core, *, name):
    r, cols = w.shape
    r2 = r // 2
    tr = _rtile(r2, 256, 8, row_bytes=4 * cols)
    nt = r2 // tr

    def body(core_ref, w_ref, gm_ref, go_ref, m_ref, v_ref, g_ref, d_ref, nm_ref, nv_ref):
        gv = jnp.where(pl.program_id(0) == core_ref[0], gm_ref[...], go_ref[...])
        g_ref[...] = gv
        d_ref[...], nm_ref[...], nv_ref[...] = _adam_update(w_ref[...], gv, m_ref[...], v_ref[...])

    full = pl.BlockSpec((tr, cols), lambda hf, i, core_ref: (hf * nt + i, 0))
    half = pl.BlockSpec((tr, cols), lambda hf, i, core_ref: (i, 0))
    shape = jax.ShapeDtypeStruct((r, cols), F32)
    return pl.pallas_call(
        body, name=name,
        grid_spec=pltpu.PrefetchScalarGridSpec(
            num_scalar_prefetch=1, grid=(2, nt), in_specs=[full, half, half, full, full], out_specs=[full] * 4),
        out_shape=[shape] * 4, compiler_params=_params(),
    )(core, w, g_mine, g_other, m, v)


def _adamw_split_rows(w, g_mine, g_other, m, v, core, *, name, tc=256):
    r, cols = w.shape
    r2 = g_mine.shape[0]
    tc = _tile(cols, tc)

    def body(core_ref, w_ref, gm_ref, go_ref, m_ref, v_ref, g_ref, d_ref, nm_ref, nv_ref):
        mine_first = core_ref[0] == 0
        for lo, hi, first in ((0, r2, True), (r2, r, False)):
            n = hi - lo
            gm, go = gm_ref[0:n, :], go_ref[0:n, :]
            gv = jnp.where(mine_first, gm, go) if first else jnp.where(mine_first, go, gm)
            g_ref[lo:hi, :] = gv
            d_ref[lo:hi, :], nm_ref[lo:hi, :], nv_ref[lo:hi, :] = _adam_update(
                w_ref[lo:hi, :], gv, m_ref[lo:hi, :], v_ref[lo:hi, :])

    full = pl.BlockSpec((r, tc), lambda j, core_ref: (0, j))
    half = pl.BlockSpec((r2, tc), lambda j, core_ref: (0, j))
    shape = jax.ShapeDtypeStruct((r, cols), F32)
    return pl.pallas_call(
        body, name=name,
        grid_spec=pltpu.PrefetchScalarGridSpec(
            num_scalar_prefetch=1, grid=(cols // tc,), in_specs=[full, half, half, full, full],
            out_specs=[full] * 4),
        out_shape=[shape] * 4, compiler_params=_params(),
    )(core, w, g_mine, g_other, m, v)


def _adamw(w, g, m, v, *, name, rows=256):
    r, cols = w.shape
    tr = _rtile(r, rows, 8)

    def body(w_ref, g_ref, m_ref, v_ref, d_ref, nm_ref, nv_ref):
        d_ref[...], nm_ref[...], nv_ref[...] = _adam_update(w_ref[...], g_ref[...], m_ref[...], v_ref[...])

    blk = pl.BlockSpec((tr, cols), lambda i: (i, 0))
    shape = jax.ShapeDtypeStruct((r, cols), F32)
    return pl.pallas_call(
        body, name=name, grid=(r // tr,), in_specs=[blk] * 4, out_specs=[blk] * 3,
        out_shape=[shape] * 3, compiler_params=_params(),
    )(w, g, m, v)


_BIG = (("w_in", 1), ("w_branch_a", 0), ("w_branch_b", 0), ("w_out", 0), ("w_up", 1), ("w_down", 0),
        ("w_ple", 1), ("w_ple_gate", 0))
_SMALL = ("gmlp_ln_g", "gmlp_ln_b", "gmlp_w_s", "gmlp_b_s", "norm_ffn_g", "conv_b", "norm_ple_g", "norm_final_g",
          "b_f", "norm_mix_g")
N_LATE = 2
_WEIGHTS = ("norm_mix_g", "w_in", "b_f", "gmlp_ln_g", "gmlp_ln_b", "gmlp_w_s", "gmlp_b_s", "w_branch_a",
            "w_branch_b", "w_out", "norm_ffn_g", "w_up", "conv_w", "conv_b", "w_down", "norm_ple_g", "w_ple",
            "w_ple_gate", "norm_final_g")
_PACK_ROWS = 8


def _pack(arrays):
    parts = []
    for a in arrays:
        flat = a.reshape(-1)
        unit = _PACK_ROWS * LANES
        flat = jnp.pad(flat, (0, (-flat.shape[0]) % unit))
        parts.append(flat.reshape(-1, LANES))
    return jnp.concatenate(parts, axis=0)


def _unpack(packed, shapes):
    out, row = [], 0
    for shp in shapes:
        size = math.prod(shp)
        rows = -(-size // (_PACK_ROWS * LANES)) * _PACK_ROWS
        out.append(packed[row:row + rows].reshape(-1)[:size].reshape(shp))
        row += rows
    return out


def _take_cols(parts, lo, hi):
    out, start = [], 0
    for a in parts:
        width = a.shape[1]
        a0, a1 = max(lo, start) - start, min(hi, start + width) - start
        if a1 > a0:
            out.append(a if (a0, a1) == (0, width) else a[:, a0:a1])
        start += width
    return out[0] if len(out) == 1 else jnp.concatenate(out, axis=1)


def _take_rows(parts, lo, hi):
    out, start = [], 0
    for a in parts:
        height = a.shape[0]
        a0, a1 = max(lo, start) - start, min(hi, start + height) - start
        if a1 > a0:
            out.append(a if (a0, a1) == (0, height) else a[a0:a1])
        start += height
    return out[0] if len(out) == 1 else jnp.concatenate(out, axis=0)


def _assemble(gathered, axis):
    n, r, cols = gathered.shape
    if axis == 0:
        return gathered.reshape(n * r, cols)
    return _take_cols([gathered[j] for j in range(n)], 0, n * cols)


def _to_chunks(parts, axis):
    rows, total = parts[0].shape[0], sum(a.shape[1] for a in parts)
    if axis == 0:
        r, cols = rows // N_CHIPS, total
        chunks = _take_cols(parts, 0, total).reshape(N_CHIPS, r, cols)
    else:
        r, cols = rows, total // N_CHIPS
        chunks = jnp.stack([_take_cols(parts, j * cols, (j + 1) * cols) for j in range(N_CHIPS)])
    return chunks.reshape(N_CHIPS, 2, r // 2, cols)


def kernel(x, p, norm_mix_g, w_in, b_f, gmlp_ln_g, gmlp_ln_b, gmlp_w_s, gmlp_b_s, w_branch_a, w_branch_b, w_out, norm_ffn_g, w_up, conv_w, conv_b, w_down, norm_ple_g, w_ple, w_ple_gate, norm_final_g, loss_target, m_norm_mix_g, m_w_in, m_b_f, m_gmlp_ln_g, m_gmlp_ln_b, m_gmlp_w_s, m_gmlp_b_s, m_w_branch_a, m_w_branch_b, m_w_out, m_norm_ffn_g, m_w_up, m_conv_w, m_conv_b, m_w_down, m_norm_ple_g, m_w_ple, m_w_ple_gate, m_norm_final_g, v_norm_mix_g, v_w_in, v_b_f, v_gmlp_ln_g, v_gmlp_ln_b, v_gmlp_w_s, v_gmlp_b_s, v_w_branch_a, v_w_branch_b, v_w_out, v_norm_ffn_g, v_w_up, v_conv_w, v_conv_b, v_w_down, v_norm_ple_g, v_w_ple, v_w_ple_gate, v_norm_final_g):
    args = dict(locals())
    wt = {n: args[n] for n in _WEIGHTS}
    mom = {n: args["m_" + n] for n in _WEIGHTS}
    var = {n: args["v_" + n] for n in _WEIGHTS}
    chip = 2 * lax.axis_index("x") + lax.axis_index("y")
    core = lax.axis_index("c").astype(jnp.int32).reshape(1)

    chip1 = chip.astype(jnp.int32).reshape(1)
    device = 2 * chip + lax.axis_index("c")
    axis_of = dict(_BIG)
    names = [n for n, _ in _BIG]
    put_mine = lambda land, mine: lax.dynamic_update_index_in_dim(land, mine, chip, 0)

    shard_in = w_in[0].astype(BF16)
    sems_in = _copies_start([shard_in], [lax.empty((N_CHIPS,) + shard_in.shape, BF16)], _gather_plan([True]), 3,
                            name="gather_start_in")
    _, wt["w_in"], mom["w_in"], var["w_in"] = lax.optimization_barrier((sems_in[4], w_in, m_w_in, v_w_in))
    shards = [wt[n][0].astype(BF16) for n in names[1:]] + [conv_w[0]]
    halved = [True] * len(names[1:]) + [False]
    lands = [lax.empty((N_CHIPS,) + a.shape, a.dtype) for a in shards]
    send_sems, recv_sems, srcs, lands, rest_token = _copies_start(
        shards, lands, _gather_plan(halved), 3 * len(shards), name="gather_start_rest", after=[sems_in[4]])
    o1 = 2 * GMLP_WIDTH
    o2 = o1 + 3 * FOX_WIDTH
    o3 = o2 + FOX_HEADS
    fpad = ((0, 0), (0, LANES - FOX_HEADS))
    w = {
        "conv_b": conv_b, "norm_mix_g": norm_mix_g, "norm_ffn_g": norm_ffn_g, "norm_ple_g": norm_ple_g,
        "norm_final_g": norm_final_g.reshape(1, D_MODEL), "b_f": jnp.pad(b_f, fpad),
        "gmlp_ln_g": gmlp_ln_g, "gmlp_ln_b": gmlp_ln_b, "gmlp_w_s": gmlp_w_s[0],
        "gmlp_b_s_t": jnp.pad(gmlp_b_s[0].T, ((0, 0), (0, LANES - GMLP_GROUPS))),
        "first_dep": rest_token,
    }

    def get_w_in(after):
        early = [a.reshape(a.shape[-2:]) for a in (wt["w_in"], mom["w_in"], var["w_in"])]
        _, got = _copies_wait(sems_in[0], sems_in[1], sems_in[2], sems_in[3], _gather_plan([True]), 0,
                              [after] + early, name="gather_wait_in")
        got = _forward_halves(got, name="gather_forward_in")
        slots = put_mine(got[0], shard_in)
        slots = [slots[j] for j in range(N_CHIPS)]
        return {"w_uv": _take_cols(slots, 0, o1), "w_qkv": _take_cols(slots, o1, o2),
                "w_f": jnp.pad(_take_cols(slots, o2, o3), fpad), "w_g": _take_cols(slots, o3, o3 + 2 * D_MODEL)}

    def start_w_rest(after):
        _, got = _copies_wait(send_sems, recv_sems, srcs, lands, _gather_plan(halved), 0, [after],
                              name="gather_wait_rest")
        ssem, rsem, _, fwd, token = _copies_start([], got[:-1], _forward_plan, 3 * len(got[:-1]),
                                                  name="gather_forward_start")
        pending["forward"] = (ssem, rsem, fwd, got[-1])
        return token

    def get_w_rest(after):
        ssem, rsem, fwd, whole = pending["forward"]
        _, fwd = _copies_wait(ssem, rsem, [], fwd, _forward_plan, 0, after, name="gather_forward_wait")
        got = fwd + [whole]
        slots = {n: put_mine(got[i], shards[i]) for i, n in enumerate(names[1:])}
        full = {n: _assemble(slots[n], axis_of[n]) for n in names[1:] if n != "w_up"}
        up = [slots["w_up"][j] for j in range(N_CHIPS)]
        return {"w_branch_a": full["w_branch_a"], "w_branch_b": full["w_branch_b"], "w_out": full["w_out"],
                "w_up_a": _take_cols(up, 0, D_FF), "w_up_b": _take_cols(up, D_FF, 2 * D_FF),
                "w_down": full["w_down"], "w_ple": full["w_ple"], "w_ple_gate": full["w_ple_gate"],
                "conv_w": _assemble(put_mine(got[-1], shards[-1]), 1)}

    grads, delta, new_m, new_v = {}, {}, {}, {}
    pending = {}

    def to_chunks(n, gr):
        return _to_chunks(gr if isinstance(gr, list) else [gr], axis_of[n])

    def pair_start(group, gfull, tag):
        chunks = [to_chunks(n, gfull[n]) for n in group]
        empty = [lax.empty((N_CHIPS,) + a.shape[2:], a.dtype) for a in chunks]
        ssem, rsem, own, recv, token = _copies_start(chunks, empty, _pair_plan, N_CHIPS * len(group),
                                                     name="grad_pair_start_" + tag)
        pending["pair_" + tag] = (ssem, rsem, own, recv)
        return token

    def reduce_start(group, gfull, tag, after=None):
        if after is None:
            chunks = [to_chunks(n, gfull[n]) for n in group]
            from_sibling = _pair_exchange(chunks, name="grad_pair_exchange_" + tag)
        else:
            ssem, rsem, own, recv = pending["pair_" + tag]
            chunks, from_sibling = _copies_wait(ssem, rsem, own, recv, _pair_plan, 0, after,
                                                name="grad_pair_wait_" + tag)
        pair_sums = [_pair_add(chunks[i], from_sibling[i], core, name="grad_pair_add_" + n) for i, n in enumerate(group)]
        empty = [lax.empty(a.shape, a.dtype) for a in pair_sums]
        ssem, rsem, own, recv, token = _copies_start(pair_sums, empty, _chip_plan, 3 * len(group),
                                                     name="grad_chip_start_" + tag)
        pending[tag] = (ssem, rsem, own, recv)
        return token

    def reduce_sum(group, tag, after):
        ssem, rsem, own, recv = pending[tag]
        own, recv = _copies_wait(ssem, rsem, own, recv, _chip_plan, 0, after, name="grad_chip_wait_" + tag)
        halves = [_chip_sum(own[i], recv[i], chip1, name="grad_chip_sum_" + n) for i, n in enumerate(group)]
        empty = [lax.empty(a.shape, a.dtype) for a in halves]
        ssem, rsem, halves, other, token = _copies_start(halves, empty, _share_plan, len(group),
                                                        name="grad_share_start_" + tag)
        pending["share_" + tag] = (ssem, rsem, halves, other)
        return token

    def reduce_update(group, tag, after):
        ssem, rsem, halves, other = pending["share_" + tag]
        halves, other_halves = _copies_wait(ssem, rsem, halves, other, _share_plan, 0, after,
                                            name="grad_share_wait_" + tag)
        for i, n in enumerate(group):
            shp = wt[n].shape
            outs = _adamw_halves(wt[n].reshape(shp[-2:]), halves[i], other_halves[i], mom[n].reshape(shp[-2:]),
                                 var[n].reshape(shp[-2:]), core, name="adamw_" + n)
            grads[n], delta[n], new_m[n], new_v[n] = (o.reshape(shp) for o in outs)
        return new_v[group[-1]]

    def reduce_finish(group, tag, after):
        ssem, rsem, own, recv = pending[tag]
        own, recv = _copies_wait(ssem, rsem, own, recv, _chip_plan, 0, after, name="grad_chip_wait_" + tag)
        halves = [_chip_sum(own[i], recv[i], chip1, name="grad_chip_sum_" + n) for i, n in enumerate(group)]
        other_halves = _pair_share(halves, name="grad_pair_share_" + tag)
        for i, n in enumerate(group):
            shp = wt[n].shape
            outs = _adamw_halves(wt[n].reshape(shp[-2:]), halves[i], other_halves[i], mom[n].reshape(shp[-2:]),
                                 var[n].reshape(shp[-2:]), core, name="adamw_" + n)
            grads[n], delta[n], new_m[n], new_v[n] = (o.reshape(shp) for o in outs)
        return new_v[group[-1]]

    ffn_group = ("w_up", "w_down", "w_ple", "w_ple_gate", "w_branch_a", "w_branch_b", "w_out")
    mix_group = ("w_in",)

    def on_grads_ffn(g):
        gfull = dict(g)
        gfull["w_up"] = [g["w_up_a"], g["w_up_b"]]
        return pair_start(ffn_group, gfull, "ffn")

    def on_grads_small(g):
        chip_token = reduce_start(ffn_group, None, "ffn", after=[g["gmlp_w_s"]])
        vec = _pack([g[n] for n in _SMALL[:-N_LATE]] + [g["conv_w"]])
        ssem, rsem, own, recv, token = _copies_start(
            [vec], [lax.empty((8,) + vec.shape, F32)], _all_plan, 7, name="small_start", after=[chip_token])
        pending["small"] = (ssem, rsem, own, recv)
        return token

    def on_grads_mix(g):
        gfull = dict(g)
        gfull["w_in"] = [g["w_uv"], g["w_qkv"], g["w_f"][:, :FOX_HEADS], g["w_g"]]
        return pair_start(mix_group, gfull, "mix")

    def on_after_dh(dh):
        mix_token = reduce_start(mix_group, None, "mix", after=[dh])
        return reduce_sum(ffn_group, "ffn", [mix_token])

    loss, grad_x, g = _device_step(x[0], p[0, 0], loss_target[0], w, get_w_in, (start_w_rest, get_w_rest), on_grads_ffn,
                                   on_grads_small, on_grads_mix, on_after_dh)

    ffn_done = reduce_update(ffn_group, "ffn", [grad_x])
    mix_done = reduce_finish(mix_group, "mix", [ffn_done])
    ssem, rsem, own, recv = pending["small"]
    own, recv = _copies_wait(ssem, rsem, own, recv, _all_plan, 0, [mix_done], name="small_wait")
    vec_early = _sum_slots(lax.dynamic_update_index_in_dim(recv[0], own[0], device, 0), F32, name="small_sum")
    vec_late = _pack([g["b_f"][:, :FOX_HEADS], g["norm_mix_g"]])
    vec_late = _sum_slots(_all_exchange(vec_late, name="small_exchange_late"), F32, name="small_sum_late")
    early_rows = _pack([wt[n] for n in _SMALL[:-N_LATE]]).shape[0]
    vec = jnp.concatenate([vec_early[:early_rows], vec_late], axis=0)
    for n, a in zip(_SMALL, _unpack(vec, [wt[n].shape for n in _SMALL])):
        grads[n] = a
    conv_w_grad = _unpack(vec_early[early_rows:], [(3, 2 * D_FF)])[0]
    grads["conv_w"] = lax.dynamic_slice_in_dim(conv_w_grad, chip * conv_w.shape[2], conv_w.shape[2], axis=1).reshape(conv_w.shape)

    shp = conv_w.shape
    outs = _adamw(conv_w.reshape(shp[-2:]), grads["conv_w"].reshape(shp[-2:]), m_conv_w.reshape(shp[-2:]),
                  v_conv_w.reshape(shp[-2:]), name="adamw_conv_w")
    delta["conv_w"], new_m["conv_w"], new_v["conv_w"] = (o.reshape(shp) for o in outs)
    outs = _adamw(_pack([wt[n] for n in _SMALL]), vec, _pack([mom[n] for n in _SMALL]),
                  _pack([var[n] for n in _SMALL]), name="adamw_small", rows=2048)
    for d, o in zip((delta, new_m, new_v), outs):
        for n, a in zip(_SMALL, _unpack(o, [wt[n].shape for n in _SMALL])):
            d[n] = a

    total_loss = lax.psum(loss[0, 0], ("x", "y", "c"))
    return (total_loss, grad_x.reshape(x.shape), *[grads[n] for n in _WEIGHTS], *[delta[n] for n in _WEIGHTS],
            *[new_m[n] for n in _WEIGHTS], *[new_v[n] for n in _WEIGHTS])
```

```python
import functools
import math

import jax
import jax.numpy as jnp
from jax import lax
from jax.experimental import pallas as pl
from jax.experimental.pallas import tpu as pltpu

F32 = jnp.float32
BF16 = jnp.bfloat16

D_MODEL = 1024
EPS = 1e-6
CHUNK = 64
GMLP_GROUPS = 8
GMLP_BLOCK = 128
GMLP_WIDTH = 1024
FOX_HEADS = 16
FOX_HEAD_DIM = 64
FOX_WIDTH = 1024
HEAD_PAIRS = FOX_HEADS // 2
ATT_BLOCK = 128
D_FF = 2816
PLE_DIM = 256
LANES = 128
BF16_TILE_ROWS = 16
N_CHIPS = 4

ADAM_LR = 0.001
ADAM_B1 = 0.9
ADAM_B2 = 0.999
ADAM_EPS = 1e-08
ADAM_WD = 0.01
ADAM_STEP = 10

VMEM_LIMIT = 56 * 1024 * 1024
MESH = pl.DeviceIdType.MESH

_NN = (((1,), (0,)), ((), ()))
_NT = (((1,), (1,)), ((), ()))
_TN = (((0,), (0,)), ((), ()))


def _params(**kw):
    return pltpu.CompilerParams(vmem_limit_bytes=VMEM_LIMIT, **kw)


def _tile(dim, pref):
    if dim <= pref:
        return dim
    t = (pref // LANES) * LANES
    while t >= LANES:
        if dim % t == 0:
            return t
        t -= LANES
    return dim


def _dot(a, b, dn):
    return lax.dot_general(a.astype(BF16), b.astype(BF16), dn, preferred_element_type=F32)


def _gelu(x):
    c = math.sqrt(2.0 / math.pi)
    t = jnp.tanh(c * (x + 0.044715 * x * x * x))
    return 0.5 * x * (1.0 + t)


def _gelu_and_grad(x):
    c = math.sqrt(2.0 / math.pi)
    x2 = x * x
    t = jnp.tanh(c * (x + 0.044715 * x2 * x))
    g = 0.5 * x * (1.0 + t)
    dg = 0.5 * (1.0 + t) + 0.5 * x * (1.0 - t * t) * c * (1.0 + 3.0 * 0.044715 * x2)
    return g, dg


def _sigmoid(x):
    return 1.0 / (1.0 + jnp.exp(-x))


def _mm(a, b, *, mode, out_dtype, name, add=None, tm=512, tn=512, dep=None):
    if mode == "nn":
        m, k = a.shape
        k2, n = b.shape
    elif mode == "nt":
        m, k = a.shape
        n, k2 = b.shape
    else:
        k, m = a.shape
        k2, n = b.shape
    assert k == k2, (name, a.shape, b.shape)
    tm = _tile(m, tm)
    tn = _tile(n, tn)
    dn = {"nn": _NN, "nt": _NT, "tn": _TN}[mode]

    def body(a_ref, b_ref, *rest):
        o_ref = rest[-1]
        acc = _dot(a_ref[...], b_ref[...], dn)
        if add is not None:
            acc = acc + rest[0][...].astype(F32)
        o_ref[...] = acc.astype(o_ref.dtype)

    a_spec = pl.BlockSpec((k, tm), lambda i, j: (0, i)) if mode == "tn" else pl.BlockSpec((tm, k), lambda i, j: (i, 0))
    b_spec = pl.BlockSpec((tn, k), lambda i, j: (j, 0)) if mode == "nt" else pl.BlockSpec((k, tn), lambda i, j: (0, j))
    o_spec = pl.BlockSpec((tm, tn), lambda i, j: (i, j))
    in_specs = [a_spec, b_spec]
    args = [a, b]
    if add is not None:
        in_specs.append(o_spec)
        args.append(add)
    if dep is not None:
        in_specs.append(pl.BlockSpec(memory_space=pl.ANY))
        args.append(dep)
    return pl.pallas_call(
        body, name=name, grid=(m // tm, n // tn), in_specs=in_specs, out_specs=o_spec,
        out_shape=jax.ShapeDtypeStruct((m, n), out_dtype), compiler_params=_params(),
    )(*args)


def _mm_nt_sum(pairs, *, out_dtype, name, tm=256, dep=None):
    m, n = pairs[0][0].shape[0], pairs[0][1].shape[0]
    tm = _tile(m, tm)
    np_ = len(pairs)

    def body(*refs):
        o_ref = refs[-1] if dep is None else refs[-1]
        acc = None
        for p in range(np_):
            part = _dot(refs[2 * p][...], refs[2 * p + 1][...], _NT)
            acc = part if acc is None else acc + part
        o_ref[...] = acc.astype(o_ref.dtype)

    in_specs, args = [], []
    for a, b in pairs:
        assert a.shape[0] == m and b.shape[0] == n and a.shape[1] == b.shape[1], (name, a.shape, b.shape)
        in_specs += [pl.BlockSpec((tm, a.shape[1]), lambda i: (i, 0)), pl.BlockSpec(b.shape, lambda i: (0, 0))]
        args += [a, b]
    if dep is not None:
        in_specs.append(pl.BlockSpec(memory_space=pl.ANY))
        args.append(dep)
    return pl.pallas_call(
        body, name=name, grid=(m // tm,), in_specs=in_specs, out_specs=pl.BlockSpec((tm, n), lambda i: (i, 0)),
        out_shape=jax.ShapeDtypeStruct((m, n), out_dtype), compiler_params=_params(),
    )(*args)


def _rms_fwd(x, g, *, name, tm=256, dep=None):
    s, d = x.shape
    tm = _tile(s, tm)

    def body(x_ref, g_ref, *rest):
        h_ref = rest[-1]
        xv = x_ref[...]
        r = lax.rsqrt(jnp.mean(xv * xv, axis=-1, keepdims=True) + EPS)
        h_ref[...] = (xv * r * g_ref[...]).astype(h_ref.dtype)

    deps = [] if dep is None else [dep]
    return pl.pallas_call(
        body, name=name, grid=(s // tm,),
        in_specs=[pl.BlockSpec((tm, d), lambda i: (i, 0)), pl.BlockSpec((1, d), lambda i: (0, 0))]
                 + [pl.BlockSpec(memory_space=pl.ANY)] * len(deps),
        out_specs=pl.BlockSpec((tm, d), lambda i: (i, 0)),
        out_shape=jax.ShapeDtypeStruct((s, d), BF16), compiler_params=_params(),
    )(x, g, *deps)


def _rms_bwd(x, g, dh, dres, *, name, tm=256, dep=None):
    s, d = x.shape
    tm = _tile(s, tm)
    deps = [] if dep is None else [dep]

    def body(x_ref, g_ref, dh_ref, dres_ref, *rest):
        dx_ref, dxb_ref, dg_ref = rest[len(deps):]
        xv = x_ref[...]
        r = lax.rsqrt(jnp.mean(xv * xv, axis=-1, keepdims=True) + EPS)
        xhat = xv * r
        dhv = dh_ref[...].astype(F32)
        dyg = dhv * g_ref[...]
        dx = dres_ref[...] + r * (dyg - xhat * jnp.mean(dyg * xhat, axis=-1, keepdims=True))
        dx_ref[...] = dx
        dxb_ref[...] = dx.astype(dxb_ref.dtype)

        @pl.when(pl.program_id(0) == 0)
        def _():
            dg_ref[...] = jnp.zeros_like(dg_ref)

        dg_ref[...] += jnp.sum(dhv * xhat, axis=0, keepdims=True)

    row = pl.BlockSpec((tm, d), lambda i: (i, 0))
    vec = pl.BlockSpec((1, d), lambda i: (0, 0))
    return pl.pallas_call(
        body, name=name, grid=(s // tm,),
        in_specs=[row, vec, row, row] + [pl.BlockSpec(memory_space=pl.ANY)] * len(deps), out_specs=[row, row, vec],
        out_shape=[jax.ShapeDtypeStruct((s, d), F32), jax.ShapeDtypeStruct((s, d), BF16),
                   jax.ShapeDtypeStruct((1, d), F32)],
        compiler_params=_params(),
    )(x, g, dh, dres, *deps)


def _gmlp_mask():
    t = lax.broadcasted_iota(jnp.int32, (GMLP_BLOCK, GMLP_BLOCK), 0)
    s_ = lax.broadcasted_iota(jnp.int32, (GMLP_BLOCK, GMLP_BLOCK), 1)
    return (s_ // CHUNK) <= (t // CHUNK)


def _gmlp_norm(zv, ln_g, ln_b):
    vv, dvv = _gelu_and_grad(zv)
    mu = jnp.mean(vv, axis=-1, keepdims=True)
    xc = vv - mu
    rstd = lax.rsqrt(jnp.mean(xc * xc, axis=-1, keepdims=True) + EPS)
    vhat = xc * rstd
    return vhat * ln_g + ln_b, vhat, rstd, dvv


def _gmlp_fwd(z_uv, ln_g, ln_b, w_s, b_s_t, *, name):
    s = z_uv.shape[0]
    w = GMLP_WIDTH
    gd = w // GMLP_GROUPS

    def body(z_ref, lg_ref, lb_ref, ws_ref, bs_ref, a_ref):
        u = _gelu(z_ref[:, :w].astype(F32))
        vn, _, _, _ = _gmlp_norm(z_ref[:, w:].astype(F32), lg_ref[...], lb_ref[...])
        mask = _gmlp_mask()
        for g in range(GMLP_GROUPS):
            wm = jnp.where(mask, ws_ref[g], 0.0)
            mixed = _dot(wm, vn[:, g * gd:(g + 1) * gd], _NN) + bs_ref[:, g:g + 1]
            a_ref[:, g * gd:(g + 1) * gd] = (u[:, g * gd:(g + 1) * gd] * mixed).astype(a_ref.dtype)

    full = lambda shape: pl.BlockSpec(shape, lambda i: (0,) * len(shape))
    return pl.pallas_call(
        body, name=name, grid=(s // GMLP_BLOCK,),
        in_specs=[pl.BlockSpec((GMLP_BLOCK, 2 * w), lambda i: (i, 0)), full((1, w)), full((1, w)),
                  full((GMLP_GROUPS, GMLP_BLOCK, GMLP_BLOCK)), full((GMLP_BLOCK, LANES))],
        out_specs=pl.BlockSpec((GMLP_BLOCK, w), lambda i: (i, 0)),
        out_shape=jax.ShapeDtypeStruct((s, w), BF16), compiler_params=_params(),
    )(z_uv, ln_g, ln_b, w_s, b_s_t)


def _gmlp_bwd(z_uv, da, ln_g, ln_b, w_s, b_s_t, *, name):
    s = z_uv.shape[0]
    w = GMLP_WIDTH
    gd = w // GMLP_GROUPS

    def body(z_ref, da_ref, lg_ref, lb_ref, ws_ref, bs_ref, dz_ref, dws_ref, dbs_ref, dlg_ref, dlb_ref):
        @pl.when(pl.program_id(0) == 0)
        def _():
            dws_ref[...] = jnp.zeros_like(dws_ref)
            dbs_ref[...] = jnp.zeros_like(dbs_ref)
            dlg_ref[...] = jnp.zeros_like(dlg_ref)
            dlb_ref[...] = jnp.zeros_like(dlb_ref)

        u, du_dz = _gelu_and_grad(z_ref[:, :w].astype(F32))
        lg = lg_ref[...]
        vn, vhat, rstd, dvv_dz = _gmlp_norm(z_ref[:, w:].astype(F32), lg, lb_ref[...])
        dav = da_ref[...].astype(F32)
        mask = _gmlp_mask()
        lane = lax.broadcasted_iota(jnp.int32, (GMLP_BLOCK, LANES), 1)
        dvn_parts = []
        dbs = jnp.zeros((GMLP_BLOCK, LANES), F32)
        for g in range(GMLP_GROUPS):
            sl = slice(g * gd, (g + 1) * gd)
            wm = jnp.where(mask, ws_ref[g], 0.0)
            vn_g = vn[:, sl]
            mixed = _dot(wm, vn_g, _NN) + bs_ref[:, g:g + 1]
            dmixed = dav[:, sl] * u[:, sl]
            dz_ref[:, sl] = (dav[:, sl] * mixed * du_dz[:, sl]).astype(dz_ref.dtype)
            dvn_parts.append(_dot(wm, dmixed, _TN))
            dws_ref[g] += jnp.where(mask, _dot(dmixed, vn_g, _NT), 0.0)
            dbs = dbs + jnp.where(lane == g, jnp.sum(dmixed, axis=-1, keepdims=True), 0.0)
        dbs_ref[...] += dbs
        dvn = jnp.concatenate(dvn_parts, axis=-1)
        dlg_ref[...] += jnp.sum(dvn * vhat, axis=0, keepdims=True)
        dlb_ref[...] += jnp.sum(dvn, axis=0, keepdims=True)
        dyg = dvn * lg
        dvv = rstd * (dyg - jnp.mean(dyg, axis=-1, keepdims=True)
                      - vhat * jnp.mean(dyg * vhat, axis=-1, keepdims=True))
        dz_ref[:, w:] = (dvv * dvv_dz).astype(dz_ref.dtype)

    full = lambda shape: pl.BlockSpec(shape, lambda i: (0,) * len(shape))
    return pl.pallas_call(
        body, name=name, grid=(s // GMLP_BLOCK,),
        in_specs=[pl.BlockSpec((GMLP_BLOCK, 2 * w), lambda i: (i, 0)),
                  pl.BlockSpec((GMLP_BLOCK, w), lambda i: (i, 0)), full((1, w)), full((1, w)),
                  full((GMLP_GROUPS, GMLP_BLOCK, GMLP_BLOCK)), full((GMLP_BLOCK, LANES))],
        out_specs=[pl.BlockSpec((GMLP_BLOCK, 2 * w), lambda i: (i, 0)),
                   full((GMLP_GROUPS, GMLP_BLOCK, GMLP_BLOCK)), full((GMLP_BLOCK, LANES)),
                   full((1, w)), full((1, w))],
        out_shape=[jax.ShapeDtypeStruct((s, 2 * w), BF16),
                   jax.ShapeDtypeStruct((GMLP_GROUPS, GMLP_BLOCK, GMLP_BLOCK), F32),
                   jax.ShapeDtypeStruct((GMLP_BLOCK, LANES), F32),
                   jax.ShapeDtypeStruct((1, w), F32), jax.ShapeDtypeStruct((1, w), F32)],
        compiler_params=_params(),
    )(z_uv, da, ln_g, ln_b, w_s, b_s_t)


def _tri(lower):
    r = lax.broadcasted_iota(jnp.int32, (ATT_BLOCK, ATT_BLOCK), 0)
    c = lax.broadcasted_iota(jnp.int32, (ATT_BLOCK, ATT_BLOCK), 1)
    return jnp.where((c <= r) if lower else (c >= r), 1.0, 0.0).astype(F32)


def _log_sigmoid(x):
    return jnp.minimum(x, 0.0) - jnp.log(1.0 + jnp.exp(-jnp.abs(x)))


def _fox_cum(f, b_f, *, name):
    s = f.shape[0]
    nb = s // ATT_BLOCK

    def body(f_ref, b_ref, cb_ref, ct_ref, carry):
        @pl.when(pl.program_id(0) == 0)
        def _():
            carry[...] = jnp.zeros_like(carry)

        lf = _log_sigmoid(f_ref[...] + b_ref[...])
        cum = lax.dot_general(_tri(True), lf, _NN, precision=lax.Precision.HIGHEST,
                              preferred_element_type=F32) + carry[...]
        carry[...] = cum[ATT_BLOCK - 1:ATT_BLOCK, :]
        for h in range(FOX_HEADS):
            cb_ref[h] = jnp.broadcast_to(cum[:, h:h + 1], (ATT_BLOCK, LANES))
        ct_ref[...] = cum.T

    return pl.pallas_call(
        body, name=name, grid=(nb,),
        in_specs=[pl.BlockSpec((ATT_BLOCK, LANES), lambda i: (i, 0)), pl.BlockSpec((1, LANES), lambda i: (0, 0))],
        out_specs=[pl.BlockSpec((FOX_HEADS, ATT_BLOCK, LANES), lambda i: (0, i, 0)),
                   pl.BlockSpec((LANES, ATT_BLOCK), lambda i: (0, i))],
        out_shape=[jax.ShapeDtypeStruct((FOX_HEADS, s, LANES), F32), jax.ShapeDtypeStruct((LANES, s), F32)],
        scratch_shapes=[pltpu.VMEM((1, LANES), F32)], compiler_params=_params(),
    )(f, b_f)


def _fox_dlogit(dcum_t, f, b_f, *, name):
    s = f.shape[0]
    nb = s // ATT_BLOCK

    def body(dc_ref, f_ref, b_ref, df_ref, db_ref, carry):
        @pl.when(pl.program_id(0) == 0)
        def _():
            carry[...] = jnp.zeros_like(carry)
            db_ref[...] = jnp.zeros_like(db_ref)

        d = dc_ref[...].T
        dlog = lax.dot_general(_tri(False), d, _NN, precision=lax.Precision.HIGHEST,
                               preferred_element_type=F32) + carry[...]
        carry[...] = dlog[0:1, :]
        df = dlog * (1.0 - _sigmoid(f_ref[...] + b_ref[...]))
        df_ref[...] = df
        db_ref[...] += jnp.sum(df, axis=0, keepdims=True)

    rev = lambda i: nb - 1 - i
    return pl.pallas_call(
        body, name=name, grid=(nb,),
        in_specs=[pl.BlockSpec((LANES, ATT_BLOCK), lambda i: (0, rev(i))),
                  pl.BlockSpec((ATT_BLOCK, LANES), lambda i: (rev(i), 0)),
                  pl.BlockSpec((1, LANES), lambda i: (0, 0))],
        out_specs=[pl.BlockSpec((ATT_BLOCK, LANES), lambda i: (rev(i), 0)),
                   pl.BlockSpec((1, LANES), lambda i: (0, 0))],
        out_shape=[jax.ShapeDtypeStruct((s, LANES), F32), jax.ShapeDtypeStruct((1, LANES), F32)],
        scratch_shapes=[pltpu.VMEM((1, LANES), F32)], compiler_params=_params(),
    )(dcum_t, f, b_f)


def _causal(qi, ki):
    r = lax.broadcasted_iota(jnp.int32, (ATT_BLOCK, ATT_BLOCK), 0) + qi * ATT_BLOCK
    c = lax.broadcasted_iota(jnp.int32, (ATT_BLOCK, ATT_BLOCK), 1) + ki * ATT_BLOCK
    return c <= r


def _head_mask():
    return lax.broadcasted_iota(jnp.int32, (1, LANES), 1) < FOX_HEAD_DIM


def _attn_fwd(qkv, cum_b, cum_r, *, name):
    s = qkv.shape[0]
    nq = s // ATT_BLOCK
    scale = FOX_HEAD_DIM ** -0.5
    npair = HEAD_PAIRS

    def body(q_ref, k_ref, v_ref, cq_ref, ck_ref, o_ref, l_ref):
        qi = pl.program_id(1)
        m0 = _head_mask()
        q2 = q_ref[...]
        zero = jnp.zeros_like(q2)
        qs = (jnp.where(m0, q2, zero), jnp.where(m0, zero, q2))
        cqs = (cq_ref[0], cq_ref[1])

        def step(ki, carry, masked):
            off = pl.multiple_of(ki * ATT_BLOCK, ATT_BLOCK)
            k2 = k_ref[pl.ds(off, ATT_BLOCK), :]
            v2 = v_ref[pl.ds(off, ATT_BLOCK), :]
            out = []
            for hh in range(2):
                m, l, acc = carry[hh]
                sc = _dot(qs[hh], k2, _NT) * scale + (cqs[hh] - ck_ref[hh:hh + 1, pl.ds(off, ATT_BLOCK)])
                if masked:
                    sc = jnp.where(_causal(qi, ki), sc, -1e30)
                m_new = jnp.maximum(m, jnp.max(sc, axis=-1, keepdims=True))
                alpha = jnp.exp(m - m_new)
                p = jnp.exp(sc - m_new)
                l = alpha * l + jnp.sum(p, axis=-1, keepdims=True)
                acc = alpha * acc + _dot(p, v2, _NN)
                out.append((m_new, l, acc))
            return tuple(out)

        init = tuple((jnp.full((ATT_BLOCK, 1), -1e30, F32), jnp.zeros((ATT_BLOCK, 1), F32),
                      jnp.zeros((ATT_BLOCK, LANES), F32)) for _ in range(2))
        carry = lax.fori_loop(0, qi, lambda ki, c: step(ki, c, False), init)
        (ma, la, acca), (mb, lb, accb) = step(qi, carry, True)
        o_ref[...] = jnp.where(m0, acca / la, accb / lb).astype(o_ref.dtype)
        l_ref[0] = jnp.broadcast_to(ma + jnp.log(la), (ATT_BLOCK, LANES))
        l_ref[1] = jnp.broadcast_to(mb + jnp.log(lb), (ATT_BLOCK, LANES))

    stat = pl.BlockSpec((None, 2, ATT_BLOCK, LANES), lambda j, i: (j, 0, i, 0))
    row = pl.BlockSpec((None, 2, s), lambda j, i: (j, 0, 0))
    return pl.pallas_call(
        body, name=name, grid=(npair, nq),
        in_specs=[pl.BlockSpec((ATT_BLOCK, LANES), lambda j, i: (i, j)),
                  pl.BlockSpec((s, LANES), lambda j, i: (0, npair + j)),
                  pl.BlockSpec((s, LANES), lambda j, i: (0, 2 * npair + j)),
                  stat, row],
        out_specs=[pl.BlockSpec((ATT_BLOCK, LANES), lambda j, i: (i, j)), stat],
        out_shape=[jax.ShapeDtypeStruct((s, FOX_WIDTH), BF16),
                   jax.ShapeDtypeStruct((npair, 2, s, LANES), F32)],
        compiler_params=_params(),
    )(qkv, qkv, qkv, cum_b, cum_r)


def _attn_delta(qkv, do, lse_b, cum_b, cum_r, *, name):
    s = qkv.shape[0]
    nq = s // ATT_BLOCK
    scale = FOX_HEAD_DIM ** -0.5
    npair = HEAD_PAIRS

    def body(q_ref, k_ref, v_ref, do_ref, l_ref, cq_ref, ck_ref, d_ref):
        qi = pl.program_id(1)
        m0 = _head_mask()
        q2 = q_ref[...]
        do2 = do_ref[...]
        qs = (jnp.where(m0, q2, jnp.zeros_like(q2)), jnp.where(m0, jnp.zeros_like(q2), q2))
        dos = (jnp.where(m0, do2, jnp.zeros_like(do2)), jnp.where(m0, jnp.zeros_like(do2), do2))

        def step(ki, carry, masked):
            off = pl.multiple_of(ki * ATT_BLOCK, ATT_BLOCK)
            k2 = k_ref[pl.ds(off, ATT_BLOCK), :]
            v2 = v_ref[pl.ds(off, ATT_BLOCK), :]
            out = []
            for hh in range(2):
                sc = _dot(qs[hh], k2, _NT) * scale + (cq_ref[hh] - ck_ref[hh:hh + 1, pl.ds(off, ATT_BLOCK)])
                p = jnp.exp(sc - l_ref[hh])
                if masked:
                    p = jnp.where(_causal(qi, ki), p, 0.0)
                out.append(carry[hh] + jnp.sum(p * _dot(dos[hh], v2, _NT), axis=-1, keepdims=True))
            return tuple(out)

        init = (jnp.zeros((ATT_BLOCK, 1), F32), jnp.zeros((ATT_BLOCK, 1), F32))
        carry = lax.fori_loop(0, qi, lambda ki, c: step(ki, c, False), init)
        da, db = step(qi, carry, True)
        d_ref[0] = jnp.broadcast_to(da, (ATT_BLOCK, LANES))
        d_ref[1] = jnp.broadcast_to(db, (ATT_BLOCK, LANES))

    stat = pl.BlockSpec((None, 2, ATT_BLOCK, LANES), lambda j, i: (j, 0, i, 0))
    return pl.pallas_call(
        body, name=name, grid=(npair, nq),
        in_specs=[pl.BlockSpec((ATT_BLOCK, LANES), lambda j, i: (i, j)),
                  pl.BlockSpec((s, LANES), lambda j, i: (0, npair + j)),
                  pl.BlockSpec((s, LANES), lambda j, i: (0, 2 * npair + j)),
                  pl.BlockSpec((ATT_BLOCK, LANES), lambda j, i: (i, j)),
                  stat, stat, pl.BlockSpec((None, 2, s), lambda j, i: (j, 0, 0))],
        out_specs=stat,
        out_shape=jax.ShapeDtypeStruct((npair, 2, s, LANES), F32), compiler_params=_params(),
    )(qkv, qkv, qkv, do, lse_b, cum_b, cum_r)


def _attn_bwd(qkv, do, lse_b, delta_b, cum_b, cum_r, *, name):
    s = qkv.shape[0]
    nq = s // ATT_BLOCK
    scale = FOX_HEAD_DIM ** -0.5
    npair = HEAD_PAIRS

    def body(q_ref, k_ref, v_ref, do_ref, l_ref, dl_ref, cq_ref, ck_ref, dq_ref, dk_ref, dv_ref, dc_ref):
        ki = pl.program_id(1)
        m0 = _head_mask()
        k2 = k_ref[...]
        v2 = v_ref[...]
        koff = pl.multiple_of(ki * ATT_BLOCK, ATT_BLOCK)

        @pl.when(ki == 0)
        def _():
            dq_ref[...] = jnp.zeros_like(dq_ref)

        def step(qi, carry, masked):
            off = pl.multiple_of(qi * ATT_BLOCK, ATT_BLOCK)
            q2 = q_ref[pl.ds(off, ATT_BLOCK), :]
            do2 = do_ref[pl.ds(off, ATT_BLOCK), :]
            qzero = jnp.zeros_like(q2)
            dzero = jnp.zeros_like(do2)
            out = []
            dqs = []
            for hh in range(2):
                dk_acc, dv_acc, dc_acc = carry[hh]
                keep = m0 if hh == 0 else jnp.logical_not(m0)
                qh = jnp.where(keep, q2, qzero)
                doh = jnp.where(keep, do2, dzero)
                sc = _dot(qh, k2, _NT) * scale + (cq_ref[hh, pl.ds(off, ATT_BLOCK), :]
                                                 - ck_ref[hh:hh + 1, pl.ds(koff, ATT_BLOCK)])
                p = jnp.exp(sc - l_ref[hh, pl.ds(off, ATT_BLOCK), :])
                if masked:
                    p = jnp.where(_causal(qi, ki), p, 0.0)
                dp = _dot(doh, v2, _NT)
                ds = p * (dp - dl_ref[hh, pl.ds(off, ATT_BLOCK), :])
                dv_acc = dv_acc + _dot(p, do2, _TN)
                dk_acc = dk_acc + _dot(ds, q2, _TN)
                dc_acc = dc_acc - jnp.sum(ds, axis=0, keepdims=True)
                dqs.append(_dot(ds, k2, _NN))
                out.append((dk_acc, dv_acc, dc_acc))
            dq_ref[pl.ds(off, ATT_BLOCK), :] += jnp.where(m0, dqs[0], dqs[1]) * scale
            return tuple(out)

        init = tuple((jnp.zeros((ATT_BLOCK, LANES), F32), jnp.zeros((ATT_BLOCK, LANES), F32),
                      jnp.zeros((1, ATT_BLOCK), F32)) for _ in range(2))
        carry = step(ki, init, True)
        (dka, dva, dca), (dkb, dvb, dcb) = lax.fori_loop(ki + 1, nq, lambda qi, c: step(qi, c, False), carry)
        dk_ref[...] = (jnp.where(m0, dka, dkb) * scale).astype(dk_ref.dtype)
        dv_ref[...] = jnp.where(m0, dva, dvb).astype(dv_ref.dtype)
        dc_ref[0:1, :] = dca
        dc_ref[1:2, :] = dcb

    stat = pl.BlockSpec((None, 2, s, LANES), lambda j, i: (j, 0, 0, 0))
    colfull = lambda base: pl.BlockSpec((s, LANES), lambda j, i: (0, base + j))
    colblk = lambda base: pl.BlockSpec((ATT_BLOCK, LANES), lambda j, i: (i, base + j))
    return pl.pallas_call(
        body, name=name, grid=(npair, nq),
        in_specs=[colfull(0), colblk(npair), colblk(2 * npair), colfull(0), stat, stat, stat,
                  pl.BlockSpec((None, 2, s), lambda j, i: (j, 0, 0))],
        out_specs=[colfull(0), colblk(0), colblk(0), pl.BlockSpec((None, 2, ATT_BLOCK), lambda j, i: (j, 0, i))],
        out_shape=[jax.ShapeDtypeStruct((s, FOX_WIDTH), F32), jax.ShapeDtypeStruct((s, FOX_WIDTH), BF16),
                   jax.ShapeDtypeStruct((s, FOX_WIDTH), BF16), jax.ShapeDtypeStruct((npair, 2, s), F32)],
        compiler_params=_params(),
    )(qkv, qkv, qkv, do, lse_b, delta_b, cum_b, cum_r)


ATT_TQ = 256
ATT_TK = 256
ATT_SCALE = FOX_HEAD_DIM ** -0.5
assert ATT_SCALE == 0.125 and ATT_TQ == ATT_TK


def _causal_t(qi, ki):
    kpos = lax.broadcasted_iota(jnp.int32, (ATT_TK, ATT_TQ), 0) + ki * ATT_TK
    qpos = lax.broadcasted_iota(jnp.int32, (ATT_TK, ATT_TQ), 1) + qi * ATT_TQ
    return kpos <= qpos


def _row_mask():
    return lax.broadcasted_iota(jnp.int32, (LANES, 1), 0) < FOX_HEAD_DIM


def _lane_tile(a, width):
    return a if a.shape[1] == width else jnp.tile(a, (1, width // a.shape[1]))


def _transpose_bf16(a):
    return a.astype(F32).T.astype(BF16)


def _attn_fwd_t(qkv, cum_b, cum_r, *, name):
    s = qkv.shape[0]
    nq = s // ATT_TQ
    npair = HEAD_PAIRS

    def body(q_ref, k_ref, v_ref, cq_ref, ck_ref, o_ref, ot_ref, l_ref, vt_ref):
        qi = pl.program_id(1)
        rows = _row_mask()

        @pl.when(qi == 0)
        def _():
            vt_ref[...] = _transpose_bf16(v_ref[...])

        qt = _transpose_bf16(q_ref[...]) * ATT_SCALE
        zero = jnp.zeros_like(qt)
        qts = (jnp.where(rows, qt, zero), jnp.where(rows, zero, qt))

        def step(ki, carry, masked):
            off = pl.multiple_of(ki * ATT_TK, ATT_TK)
            k2 = k_ref[pl.ds(off, ATT_TK), :]
            vt = vt_ref[:, pl.ds(off, ATT_TK)]
            out = []
            for hh in range(2):
                m, l, acc = carry[hh]
                bias = cq_ref[hh:hh + 1, :] - _lane_tile(ck_ref[hh, pl.ds(off, ATT_TK), :], ATT_TQ)
                sc = _dot(k2, qts[hh], _NN) + bias
                if masked:
                    sc = jnp.where(_causal_t(qi, ki), sc, -1e30)
                m_new = jnp.maximum(m, jnp.max(sc, axis=0, keepdims=True))
                alpha = jnp.exp(m - m_new)
                p = jnp.exp(sc - m_new)
                l = alpha * l + jnp.sum(p, axis=0, keepdims=True)
                p_hi = p.astype(BF16)
                p_lo = (p - p_hi.astype(F32)).astype(BF16)
                acc = alpha * acc + (_dot(vt, p_hi, _NN) + _dot(vt, p_lo, _NN))
                out.append((m_new, l, acc))
            return tuple(out)

        init = tuple((jnp.full((1, ATT_TQ), -1e30, F32), jnp.zeros((1, ATT_TQ), F32),
                      jnp.zeros((LANES, ATT_TQ), F32)) for _ in range(2))
        carry = lax.fori_loop(0, qi // 2, lambda kk, c: step(2 * kk + 1, step(2 * kk, c, False), False), init)
        carry = lax.cond(qi % 2 == 1, lambda c: step(qi - 1, c, False), lambda c: c, carry)
        (ma, la, acca), (mb, lb, accb) = step(qi, carry, True)
        ot = jnp.where(rows, acca / la, accb / lb)
        ot_ref[...] = ot
        o_ref[...] = ot.T.astype(o_ref.dtype)
        l_ref[0:1, :] = ma + jnp.log(la)
        l_ref[1:2, :] = mb + jnp.log(lb)

    row = pl.BlockSpec((None, 2, ATT_TQ), lambda j, i: (j, 0, i))
    return pl.pallas_call(
        body, name=name, grid=(npair, nq),
        in_specs=[pl.BlockSpec((ATT_TQ, LANES), lambda j, i: (i, j)),
                  pl.BlockSpec((s, LANES), lambda j, i: (0, npair + j)),
                  pl.BlockSpec((s, LANES), lambda j, i: (0, 2 * npair + j)),
                  row, pl.BlockSpec((None, 2, s, LANES), lambda j, i: (j, 0, 0, 0))],
        out_specs=[pl.BlockSpec((ATT_TQ, LANES), lambda j, i: (i, j)),
                   pl.BlockSpec((LANES, ATT_TQ), lambda j, i: (j, i)), row],
        out_shape=[jax.ShapeDtypeStruct((s, FOX_WIDTH), BF16), jax.ShapeDtypeStruct((FOX_WIDTH, s), F32),
                   jax.ShapeDtypeStruct((npair, 2, s), F32)],
        scratch_shapes=[pltpu.VMEM((LANES, s), BF16)],
        compiler_params=_params(),
    )(qkv, qkv, qkv, cum_r, cum_b)


def _attn_delta_t(do_t, o_t, *, name):
    s = o_t.shape[1]
    ts = _tile(s, 512)

    def body(do_ref, o_ref, d_ref):
        prod = do_ref[...].astype(F32) * o_ref[...]
        d_ref[0:1, :] = jnp.sum(prod[:FOX_HEAD_DIM], axis=0, keepdims=True)
        d_ref[1:2, :] = jnp.sum(prod[FOX_HEAD_DIM:], axis=0, keepdims=True)

    blk = pl.BlockSpec((LANES, ts), lambda j, i: (j, i))
    return pl.pallas_call(
        body, name=name, grid=(HEAD_PAIRS, s // ts), in_specs=[blk, blk],
        out_specs=pl.BlockSpec((None, 2, ts), lambda j, i: (j, 0, i)),
        out_shape=jax.ShapeDtypeStruct((HEAD_PAIRS, 2, s), F32), compiler_params=_params(),
    )(do_t, o_t)


def _attn_bwd_t(qkv, do, o_t, lse, cum_b, cum_r, *, name, dep=None):
    s = qkv.shape[0]
    nq = s // ATT_TQ
    npair = HEAD_PAIRS

    deps = [] if dep is None else [dep]

    def body(q_ref, k_ref, v_ref, do_ref, ot_ref, l_ref, cq_ref, ck_ref, *rest):
        dq_ref, dk_ref, dv_ref, dc_ref, qt_ref, dot_ref, dqt_ref, dl_ref = rest[len(deps):]
        ki = pl.program_id(1)
        m0 = _head_mask()
        rows = _row_mask()
        k2 = k_ref[...]
        v2 = v_ref[...]
        kt = _transpose_bf16(k2)
        ks = k2 * ATT_SCALE
        kz, vz = jnp.zeros_like(k2), jnp.zeros_like(v2)
        khs = (jnp.where(m0, ks, kz), jnp.where(m0, kz, ks))
        vhs = (jnp.where(m0, v2, vz), jnp.where(m0, vz, v2))
        cks = tuple(_lane_tile(ck_ref[hh], ATT_TQ) for hh in range(2))

        @pl.when(ki == 0)
        def _():
            dqt_ref[...] = jnp.zeros_like(dqt_ref)
            qt_ref[...] = _transpose_bf16(q_ref[...])
            do_t = do_ref[...].astype(F32).T
            dot_ref[...] = do_t.astype(BF16)
            prod = do_t * ot_ref[...]
            dl_ref[0:1, :] = jnp.sum(prod[:FOX_HEAD_DIM], axis=0, keepdims=True)
            dl_ref[1:2, :] = jnp.sum(prod[FOX_HEAD_DIM:], axis=0, keepdims=True)

        def step(qi, carry, masked):
            off = pl.multiple_of(qi * ATT_TQ, ATT_TQ)
            q2 = q_ref[pl.ds(off, ATT_TQ), :]
            do2 = do_ref[pl.ds(off, ATT_TQ), :]
            qt = qt_ref[:, pl.ds(off, ATT_TQ)]
            dot_ = dot_ref[:, pl.ds(off, ATT_TQ)]
            out, dqs = [], []
            for hh in range(2):
                dk_acc, dv_acc, dc_acc = carry[hh]
                sc = _dot(khs[hh], qt, _NN) + (cq_ref[hh:hh + 1, pl.ds(off, ATT_TQ)] - cks[hh])
                p = jnp.exp(sc - l_ref[hh:hh + 1, pl.ds(off, ATT_TQ)])
                if masked:
                    p = jnp.where(_causal_t(qi, ki), p, 0.0)
                dp = _dot(vhs[hh], dot_, _NN)
                ds = p * (dp - dl_ref[hh:hh + 1, pl.ds(off, ATT_TQ)])
                dc_acc = dc_acc - jnp.sum(ds, axis=1, keepdims=True)
                dss = (ds * ATT_SCALE).astype(BF16)
                dv_acc = dv_acc + _dot(p, do2, _NN)
                dk_acc = dk_acc + _dot(dss, q2, _NN)
                dqs.append(_dot(kt, dss, _NN))
                out.append((dk_acc, dv_acc, dc_acc))
            dqt_ref[:, pl.ds(off, ATT_TQ)] += jnp.where(rows, dqs[0], dqs[1])
            return tuple(out)

        init = tuple((jnp.zeros((ATT_TK, LANES), F32), jnp.zeros((ATT_TK, LANES), F32),
                      jnp.zeros((ATT_TK, 1), F32)) for _ in range(2))
        carry = step(ki, init, True)
        rest = nq - 1 - ki
        carry = lax.fori_loop(
            0, rest // 2, lambda t, c: step(ki + 2 + 2 * t, step(ki + 1 + 2 * t, c, False), False), carry)
        carry = lax.cond(rest % 2 == 1, lambda c: step(nq - 1, c, False), lambda c: c, carry)
        (dka, dva, dca), (dkb, dvb, dcb) = carry
        dk_ref[...] = jnp.where(m0, dka, dkb).astype(dk_ref.dtype)
        dv_ref[...] = jnp.where(m0, dva, dvb).astype(dv_ref.dtype)
        dc_ref[0] = jnp.broadcast_to(dca, (ATT_TK, LANES))
        dc_ref[1] = jnp.broadcast_to(dcb, (ATT_TK, LANES))

        @pl.when(ki == nq - 1)
        def _():
            dq_ref[...] = dqt_ref[...].T.astype(dq_ref.dtype)

    colfull = lambda base: pl.BlockSpec((s, LANES), lambda j, i: (0, base + j))
    colblk = lambda base: pl.BlockSpec((ATT_TK, LANES), lambda j, i: (i, base + j))
    stat = pl.BlockSpec((None, 2, s), lambda j, i: (j, 0, 0))
    bcast = pl.BlockSpec((None, 2, ATT_TK, LANES), lambda j, i: (j, 0, i, 0))
    grad = jax.ShapeDtypeStruct((s, FOX_WIDTH), BF16)
    return pl.pallas_call(
        body, name=name, grid=(npair, nq),
        in_specs=[colfull(0), colblk(npair), colblk(2 * npair), colfull(0),
                  pl.BlockSpec((LANES, s), lambda j, i: (j, 0)), stat, stat, bcast]
                 + [pl.BlockSpec(memory_space=pl.ANY)] * len(deps),
        out_specs=[colfull(0), colblk(0), colblk(0), bcast],
        out_shape=[grad, grad, grad, jax.ShapeDtypeStruct((npair, 2, s, LANES), F32)],
        scratch_shapes=[pltpu.VMEM((LANES, s), BF16), pltpu.VMEM((LANES, s), BF16), pltpu.VMEM((LANES, s), F32),
                        pltpu.VMEM((2, s), F32)],
        compiler_params=_params(),
    )(qkv, qkv, qkv, do, o_t, lse, cum_r, cum_b, *deps)


def _merge_fwd(zg, ya, yb, *, name, tm=256):
    s, d = ya.shape
    tm = _tile(s, tm)

    def body(zg_ref, ya_ref, yb_ref, m_ref):
        ga = _sigmoid(zg_ref[:, :d].astype(F32))
        gb = _sigmoid(zg_ref[:, d:].astype(F32))
        m_ref[...] = (ga * ya_ref[...].astype(F32) + gb * yb_ref[...].astype(F32)).astype(m_ref.dtype)

    row = pl.BlockSpec((tm, d), lambda i: (i, 0))
    row2 = pl.BlockSpec((tm, 2 * d), lambda i: (i, 0))
    return pl.pallas_call(
        body, name=name, grid=(s // tm,), in_specs=[row2, row, row], out_specs=row,
        out_shape=jax.ShapeDtypeStruct((s, d), BF16), compiler_params=_params(),
    )(zg, ya, yb)


def _merge_bwd(dm, zg, ya, yb, *, name, tm=256):
    s, d = ya.shape
    tm = _tile(s, tm)

    def body(dm_ref, zg_ref, ya_ref, yb_ref, dzg_ref, dya_ref, dyb_ref):
        dmv = dm_ref[...].astype(F32)
        ga = _sigmoid(zg_ref[:, :d].astype(F32))
        gb = _sigmoid(zg_ref[:, d:].astype(F32))
        dzg_ref[:, :d] = (dmv * ya_ref[...].astype(F32) * ga * (1.0 - ga)).astype(dzg_ref.dtype)
        dzg_ref[:, d:] = (dmv * yb_ref[...].astype(F32) * gb * (1.0 - gb)).astype(dzg_ref.dtype)
        dya_ref[...] = (dmv * ga).astype(dya_ref.dtype)
        dyb_ref[...] = (dmv * gb).astype(dyb_ref.dtype)

    row = pl.BlockSpec((tm, d), lambda i: (i, 0))
    row2 = pl.BlockSpec((tm, 2 * d), lambda i: (i, 0))
    return pl.pallas_call(
        body, name=name, grid=(s // tm,), in_specs=[row, row2, row, row], out_specs=[row2, row, row],
        out_shape=[jax.ShapeDtypeStruct((s, 2 * d), BF16), jax.ShapeDtypeStruct((s, d), BF16),
                   jax.ShapeDtypeStruct((s, d), BF16)],
        compiler_params=_params(),
    )(dm, zg, ya, yb)


SUBLANES = 8


def _shift_down(u, k, row):
    rolled = pltpu.roll(u, k, 0)
    head = jnp.where(row[:SUBLANES] >= k, rolled[:SUBLANES], 0.0)
    return jnp.concatenate([head, rolled[SUBLANES:]], axis=0)


def _shift_up(u, k, row):
    n = u.shape[0]
    rolled = pltpu.roll(u, n - k, 0)
    tail = jnp.where(row[n - SUBLANES:] < n - k, rolled[n - SUBLANES:], 0.0)
    return jnp.concatenate([rolled[:n - SUBLANES], tail], axis=0)


def _conv_act_fwd(up_a, up_b, cw_a, cw_b, cb_a, cb_b, *, name, tc=128):
    s, f = up_a.shape
    tc = _tile(f, tc)

    def body(ua_ref, ub_ref, wa_ref, wb_ref, ba_ref, bb_ref, act_ref):
        row = lax.broadcasted_iota(jnp.int32, (s, tc), 0)

        def conv(u_ref, w_ref, b_ref):
            u = u_ref[...].astype(F32)
            return (b_ref[...] + w_ref[0:1, :] * _shift_down(u, 2, row)
                    + w_ref[1:2, :] * _shift_down(u, 1, row) + w_ref[2:3, :] * u)

        ca = conv(ua_ref, wa_ref, ba_ref)
        cb = conv(ub_ref, wb_ref, bb_ref)
        act_ref[...] = (_gelu(ca) * cb).astype(act_ref.dtype)

    col = pl.BlockSpec((s, tc), lambda j: (0, j))
    w3 = pl.BlockSpec((3, tc), lambda j: (0, j))
    b1 = pl.BlockSpec((1, tc), lambda j: (0, j))
    return pl.pallas_call(
        body, name=name, grid=(f // tc,), in_specs=[col, col, w3, w3, b1, b1], out_specs=col,
        out_shape=jax.ShapeDtypeStruct((s, f), BF16), compiler_params=_params(),
    )(up_a, up_b, cw_a, cw_b, cb_a, cb_b)


def _conv_act_bwd(up_a, up_b, dact, cw_a, cw_b, cb_a, cb_b, *, name, tc=128):
    s, f = up_a.shape
    tc = _tile(f, tc)

    def body(ua_ref, ub_ref, da_ref, wa_ref, wb_ref, ba_ref, bb_ref, dua_ref, dub_ref, dwa_ref, dwb_ref):
        row = lax.broadcasted_iota(jnp.int32, (s, tc), 0)

        def conv(u_ref, w_ref, b_ref):
            u = u_ref[...].astype(F32)
            u1 = _shift_down(u, 1, row)
            u2 = _shift_down(u, 2, row)
            return u, u1, u2, b_ref[...] + w_ref[0:1, :] * u2 + w_ref[1:2, :] * u1 + w_ref[2:3, :] * u

        def back(dc, taps, w_ref, du_ref, dw_ref):
            u, u1, u2 = taps
            dw_ref[0:1, :] = jnp.sum(dc * u2, axis=0, keepdims=True)
            dw_ref[1:2, :] = jnp.sum(dc * u1, axis=0, keepdims=True)
            dw_ref[2:3, :] = jnp.sum(dc * u, axis=0, keepdims=True)
            dw_ref[3:4, :] = jnp.sum(dc, axis=0, keepdims=True)
            du = (w_ref[2:3, :] * dc + w_ref[1:2, :] * _shift_up(dc, 1, row)
                  + w_ref[0:1, :] * _shift_up(dc, 2, row))
            du_ref[...] = du.astype(du_ref.dtype)

        ua, ua1, ua2, ca = conv(ua_ref, wa_ref, ba_ref)
        ub, ub1, ub2, cb = conv(ub_ref, wb_ref, bb_ref)
        g, dg = _gelu_and_grad(ca)
        dact_v = da_ref[...].astype(F32)
        back(dact_v * cb * dg, (ua, ua1, ua2), wa_ref, dua_ref, dwa_ref)
        back(dact_v * g, (ub, ub1, ub2), wb_ref, dub_ref, dwb_ref)

    col = pl.BlockSpec((s, tc), lambda j: (0, j))
    w3 = pl.BlockSpec((3, tc), lambda j: (0, j))
    w4 = pl.BlockSpec((4, tc), lambda j: (0, j))
    b1 = pl.BlockSpec((1, tc), lambda j: (0, j))
    return pl.pallas_call(
        body, name=name, grid=(f // tc,), in_specs=[col, col, col, w3, w3, b1, b1],
        out_specs=[col, col, w4, w4],
        out_shape=[jax.ShapeDtypeStruct((s, f), BF16), jax.ShapeDtypeStruct((s, f), BF16),
                   jax.ShapeDtypeStruct((4, f), F32), jax.ShapeDtypeStruct((4, f), F32)],
        compiler_params=_params(),
    )(up_a, up_b, dact, cw_a, cw_b, cb_a, cb_b)


def _ple_final(x2, ple, zp, target, g_final, *, name, tm=256):
    s, d = x2.shape
    tm = _tile(s, tm)

    def body(x_ref, ple_ref, zp_ref, t_ref, g_ref, dx_ref, dple_ref, dzp_ref, dg_ref, loss_ref):
        @pl.when(pl.program_id(0) == 0)
        def _():
            dg_ref[...] = jnp.zeros_like(dg_ref)
            loss_ref[...] = jnp.zeros_like(loss_ref)

        gp = _sigmoid(zp_ref[...].astype(F32))
        plev = ple_ref[...].astype(F32)
        x3 = x_ref[...] + plev * gp
        r = lax.rsqrt(jnp.mean(x3 * x3, axis=-1, keepdims=True) + EPS)
        xhat = x3 * r
        gv = g_ref[...]
        diff = xhat * gv - t_ref[...]
        loss_ref[...] += 0.5 * jnp.sum(jnp.mean(diff * diff, axis=-1, keepdims=True), axis=0, keepdims=True)
        dy = diff * (1.0 / d)
        dg_ref[...] += jnp.sum(dy * xhat, axis=0, keepdims=True)
        dyg = dy * gv
        dx3 = r * (dyg - xhat * jnp.mean(dyg * xhat, axis=-1, keepdims=True))
        dx_ref[...] = dx3
        dple_ref[...] = (dx3 * gp).astype(dple_ref.dtype)
        dzp_ref[...] = (dx3 * plev * gp * (1.0 - gp)).astype(dzp_ref.dtype)

    row = pl.BlockSpec((tm, d), lambda i: (i, 0))
    vec = pl.BlockSpec((1, d), lambda i: (0, 0))
    return pl.pallas_call(
        body, name=name, grid=(s // tm,), in_specs=[row, row, row, row, vec],
        out_specs=[row, row, row, vec, pl.BlockSpec((1, LANES), lambda i: (0, 0))],
        out_shape=[jax.ShapeDtypeStruct((s, d), F32), jax.ShapeDtypeStruct((s, d), BF16),
                   jax.ShapeDtypeStruct((s, d), BF16), jax.ShapeDtypeStruct((1, d), F32),
                   jax.ShapeDtypeStruct((1, LANES), F32)],
        compiler_params=_params(),
    )(x2, ple, zp, target, g_final)


def _device_step(x, p, target, w, get_w_in=None, get_w_rest=None, on_grads_ffn=None, on_grads_small=None,
                 on_grads_mix=None, on_after_dh=None):
    s = x.shape[0]
    g = {}
    w = dict(w)

    h = _rms_fwd(x, w["norm_mix_g"], name="rms_mix", dep=w.get("first_dep"))
    if get_w_in is not None:
        w.update(get_w_in(h))
    qkv = _mm(h, w["w_qkv"], mode="nn", out_dtype=BF16, name="proj_qkv", tm=1024)
    f = _mm(h, w["w_f"], mode="nn", out_dtype=F32, name="proj_f", tm=1024)

    cum_b, cum_t = _fox_cum(f, w["b_f"], name="fox_cum")
    cum_b = cum_b.reshape(HEAD_PAIRS, 2, s, LANES)
    cum_r = cum_t[:FOX_HEADS].reshape(HEAD_PAIRS, 2, s)
    b, o_t, lse = _attn_fwd_t(qkv, cum_b, cum_r, name="attn_fwd")

    dep = get_w_rest[0](b) if get_w_rest is not None else None
    z_uv = _mm(h, w["w_uv"], mode="nn", out_dtype=BF16, name="proj_uv", tm=1024, dep=dep)
    zg = _mm(h, w["w_g"], mode="nn", out_dtype=BF16, name="proj_gate", tm=1024, dep=dep)
    a = _gmlp_fwd(z_uv, w["gmlp_ln_g"], w["gmlp_ln_b"], w["gmlp_w_s"], w["gmlp_b_s_t"], name="gmlp_fwd")
    if get_w_rest is not None:
        w.update(get_w_rest[1]([a, zg]))

    ya = _mm(a, w["w_branch_a"], mode="nn", out_dtype=BF16, name="branch_a", tm=1024)
    yb = _mm(b, w["w_branch_b"], mode="nn", out_dtype=BF16, name="branch_b", tm=1024)
    merged = _merge_fwd(zg, ya, yb, name="merge_fwd")
    x1 = _mm(merged, w["w_out"], mode="nn", out_dtype=F32, name="proj_out", add=x, tm=1024)

    h2 = _rms_fwd(x1, w["norm_ffn_g"], name="rms_ffn")
    up_a = _mm(h2, w["w_up_a"], mode="nn", out_dtype=BF16, name="up_a", tm=1024, tn=D_FF // 2)
    up_b = _mm(h2, w["w_up_b"], mode="nn", out_dtype=BF16, name="up_b", tm=1024, tn=D_FF // 2)
    cw, cb = w["conv_w"], w["conv_b"]
    conv_args = (cw[:, :D_FF], cw[:, D_FF:], cb[:, :D_FF], cb[:, D_FF:])
    act = _conv_act_fwd(up_a, up_b, *conv_args, name="conv_act_fwd")
    x2 = _mm(act, w["w_down"], mode="nn", out_dtype=F32, name="down", add=x1, tm=512)

    h3 = _rms_fwd(x2, w["norm_ple_g"], name="rms_ple")
    ple = _mm(p, w["w_ple"], mode="nn", out_dtype=BF16, name="ple_proj", tm=1024)
    zp = _mm(h3, w["w_ple_gate"], mode="nn", out_dtype=BF16, name="ple_gate", tm=1024)
    dx3, dple, dzp, g["norm_final_g"], loss = _ple_final(x2, ple, zp, target, w["norm_final_g"], name="ple_final")

    g["w_ple"] = _mm(p, dple, mode="tn", out_dtype=BF16, name="dw_ple")
    g["w_ple_gate"] = _mm(h3, dzp, mode="tn", out_dtype=BF16, name="dw_ple_gate")
    dh3 = _mm(dzp, w["w_ple_gate"], mode="nt", out_dtype=BF16, name="dh3")
    dx2, dx2_b, g["norm_ple_g"] = _rms_bwd(x2, w["norm_ple_g"], dh3, dx3, name="rms_ple_bwd")

    g["w_down"] = _mm(act, dx2_b, mode="tn", out_dtype=BF16, name="dw_down", tm=D_FF // 2)
    dact = _mm(dx2_b, w["w_down"], mode="nt", out_dtype=BF16, name="dact", tn=D_FF // 2)
    dup_a, dup_b, dcw_a, dcw_b = _conv_act_bwd(up_a, up_b, dact, *conv_args, name="conv_act_bwd")
    g["conv_w"] = jnp.concatenate([dcw_a[:3], dcw_b[:3]], axis=1)
    g["conv_b"] = jnp.concatenate([dcw_a[3:], dcw_b[3:]], axis=1)
    g["w_up_a"] = _mm(h2, dup_a, mode="tn", out_dtype=BF16, name="dw_up_a", tn=D_FF // 2)
    g["w_up_b"] = _mm(h2, dup_b, mode="tn", out_dtype=BF16, name="dw_up_b", tn=D_FF // 2)
    dh2 = _mm_nt_sum([(dup_a, w["w_up_a"]), (dup_b, w["w_up_b"])], out_dtype=BF16, name="dh2")
    dx1, dx1_b, g["norm_ffn_g"] = _rms_bwd(x1, w["norm_ffn_g"], dh2, dx2, name="rms_ffn_bwd")

    g["w_out"] = _mm(merged, dx1_b, mode="tn", out_dtype=BF16, name="dw_out")
    dmerged = _mm(dx1_b, w["w_out"], mode="nt", out_dtype=BF16, name="dmerged")
    dzg, dya, dyb = _merge_bwd(dmerged, zg, ya, yb, name="merge_bwd")
    g["w_branch_a"] = _mm(a, dya, mode="tn", out_dtype=BF16, name="dw_branch_a")
    g["w_branch_b"] = _mm(b, dyb, mode="tn", out_dtype=BF16, name="dw_branch_b")
    dep = on_grads_ffn(g) if on_grads_ffn is not None else None
    da = _mm(dya, w["w_branch_a"], mode="nt", out_dtype=BF16, name="da", dep=dep)
    db = _mm(dyb, w["w_branch_b"], mode="nt", out_dtype=BF16, name="db")

    dz_uv, g["gmlp_w_s"], dbs_t, g["gmlp_ln_g"], g["gmlp_ln_b"] = _gmlp_bwd(
        z_uv, da, w["gmlp_ln_g"], w["gmlp_ln_b"], w["gmlp_w_s"], w["gmlp_b_s_t"], name="gmlp_bwd")
    g["gmlp_b_s"] = dbs_t[:, :GMLP_GROUPS].T
    dep = on_grads_small(g) if on_grads_small is not None else None

    dq, dk, dv, dcum_b = _attn_bwd_t(qkv, db, o_t, lse, cum_b, cum_r, name="attn_bwd", dep=dep)
    dcum_t = jnp.pad(dcum_b[..., 0].reshape(FOX_HEADS, s), ((0, LANES - FOX_HEADS), (0, 0)))
    df, g["b_f"] = _fox_dlogit(dcum_t, f, w["b_f"], name="fox_dlogit")
    dqkv = jnp.concatenate([dq, dk, dv], axis=1)

    g["w_uv"] = _mm(h, dz_uv, mode="tn", out_dtype=BF16, name="dw_uv")
    g["w_qkv"] = _mm(h, dqkv, mode="tn", out_dtype=BF16, name="dw_qkv")
    g["w_f"] = _mm(h, df, mode="tn", out_dtype=BF16, name="dw_f")
    g["w_g"] = _mm(h, dzg, mode="tn", out_dtype=BF16, name="dw_g")
    dep = on_grads_mix(g) if on_grads_mix is not None else None
    dh = _mm_nt_sum([(dz_uv, w["w_uv"]), (dqkv, w["w_qkv"]), (df, w["w_f"]), (dzg, w["w_g"])],
                    out_dtype=BF16, name="dh", dep=dep)
    dep = on_after_dh(dh) if on_after_dh is not None else None
    dx0, _, g["norm_mix_g"] = _rms_bwd(x, w["norm_mix_g"], dh, dx1, name="rms_mix_bwd", dep=dep)
    return loss, dx0, g


def _coords():
    return lax.axis_index("x"), lax.axis_index("y"), lax.axis_index("c")


def _other_chips(x, y):
    return [(1 - x, y), (x, 1 - y), (1 - x, 1 - y)]


def _remote(src, dst, send_sem, recv_sem, dev):
    return pltpu.make_async_remote_copy(src_ref=src, dst_ref=dst, send_sem=send_sem, recv_sem=recv_sem,
                                        device_id=dev, device_id_type=MESH)


_ANY = pl.BlockSpec(memory_space=pl.ANY)


def _gather_weights(halved, whole, *, name):
    nh, n = len(halved), len(halved) + len(whole)
    arrays = list(halved) + list(whole)

    def body(*refs):
        ins, outs = refs[:n], refs[n:2 * n]
        send_sems, recv_sems = refs[2 * n:]
        x, y, c = _coords()
        me, sib = 2 * x + y, (x, y, 1 - c)
        chips = _other_chips(x, y)

        def half(i, which):
            h = ins[i].shape[0] // 2
            return pl.ds(pl.multiple_of(which * h, 16), h)

        sends = []
        for i in range(n):
            src, dst = (ins[i].at[half(i, c)], outs[i].at[me, half(i, c)]) if i < nh else (ins[i], outs[i].at[me])
            for k, (cx, cy) in enumerate(chips):
                cp = _remote(src, dst, send_sems.at[i, k], recv_sems.at[i, k], (cx, cy, c))
                cp.start()
                sends.append(cp)
        for i in range(n):
            for k, (cx, cy) in enumerate(chips):
                got = outs[i].at[2 * cx + cy, half(i, c)] if i < nh else outs[i].at[2 * cx + cy]
                _remote(got, got, send_sems.at[i, k], recv_sems.at[i, k], sib).wait_recv()
                if i < nh:
                    cp = _remote(got, got, send_sems.at[i, 3 + k], recv_sems.at[i, 3 + k], sib)
                    cp.start()
                    sends.append(cp)
        for i in range(nh):
            for k, (cx, cy) in enumerate(chips):
                got = outs[i].at[2 * cx + cy, half(i, 1 - c)]
                _remote(got, got, send_sems.at[i, 3 + k], recv_sems.at[i, 3 + k], sib).wait_recv()
        for cp in sends:
            cp.wait_send()

    outs = pl.pallas_call(
        body, name=name, in_specs=[_ANY] * n, out_specs=[_ANY] * n,
        out_shape=[jax.ShapeDtypeStruct((N_CHIPS,) + a.shape, a.dtype) for a in arrays],
        scratch_shapes=[pltpu.SemaphoreType.DMA((n, 6)), pltpu.SemaphoreType.DMA((n, 6))],
        compiler_params=_params(),
    )(*arrays)
    chip = 2 * lax.axis_index("x") + lax.axis_index("y")
    return [lax.dynamic_update_index_in_dim(o, a, chip, 0) for o, a in zip(outs, arrays)]


def _pair_exchange(gs, *, name):
    n = len(gs)

    def body(*refs):
        ins, outs = refs[:n], refs[n:2 * n]
        send_sems, recv_sems = refs[2 * n:]
        x, y, c = _coords()
        copies = []
        for i in range(n):
            for j in range(N_CHIPS):
                cp = _remote(ins[i].at[j, 1 - c], outs[i].at[j], send_sems.at[i, j], recv_sems.at[i, j], (x, y, 1 - c))
                cp.start()
                copies.append(cp)
        for cp in copies:
            cp.wait()

    return pl.pallas_call(
        body, name=name, in_specs=[_ANY] * n, out_specs=[_ANY] * n,
        out_shape=[jax.ShapeDtypeStruct((N_CHIPS,) + a.shape[2:], a.dtype) for a in gs],
        scratch_shapes=[pltpu.SemaphoreType.DMA((n, N_CHIPS)), pltpu.SemaphoreType.DMA((n, N_CHIPS))],
        compiler_params=_params(),
    )(*gs)


def _chip_exchange(ss, *, name):
    n = len(ss)

    def body(*refs):
        ins, outs = refs[:n], refs[n:2 * n]
        send_sems, recv_sems = refs[2 * n:]
        x, y, c = _coords()
        me = 2 * x + y
        chips = _other_chips(x, y)
        sends = []
        for i in range(n):
            for k, (cx, cy) in enumerate(chips):
                cp = _remote(ins[i].at[2 * cx + cy], outs[i].at[me], send_sems.at[i, k], recv_sems.at[i, k], (cx, cy, c))
                cp.start()
                sends.append(cp)
        for i in range(n):
            for k, (cx, cy) in enumerate(chips):
                got = outs[i].at[2 * cx + cy]
                _remote(got, got, send_sems.at[i, k], recv_sems.at[i, k], (cx, cy, c)).wait_recv()
        for cp in sends:
            cp.wait_send()

    return pl.pallas_call(
        body, name=name, in_specs=[_ANY] * n, out_specs=[_ANY] * n,
        out_shape=[jax.ShapeDtypeStruct(a.shape, a.dtype) for a in ss],
        scratch_shapes=[pltpu.SemaphoreType.DMA((n, 3)), pltpu.SemaphoreType.DMA((n, 3))],
        compiler_params=_params(),
    )(*ss)


def _pair_share(hs, *, name):
    n = len(hs)

    def body(*refs):
        ins, outs = refs[:n], refs[n:2 * n]
        send_sems, recv_sems = refs[2 * n:]
        x, y, c = _coords()
        copies = []
        for i in range(n):
            cp = _remote(ins[i], outs[i], send_sems.at[i], recv_sems.at[i], (x, y, 1 - c))
            cp.start()
            copies.append(cp)
        for cp in copies:
            cp.wait()

    return pl.pallas_call(
        body, name=name, in_specs=[_ANY] * n, out_specs=[_ANY] * n,
        out_shape=[jax.ShapeDtypeStruct(a.shape, a.dtype) for a in hs],
        scratch_shapes=[pltpu.SemaphoreType.DMA((n,)), pltpu.SemaphoreType.DMA((n,))],
        compiler_params=_params(),
    )(*hs)


def _all_exchange(vec, *, name):
    def body(v_ref, o_ref, send_sems, recv_sems, local_sem):
        x, y, c = _coords()
        me = 4 * x + 2 * y + c
        local = pltpu.make_async_copy(v_ref, o_ref.at[me], local_sem)
        local.start()
        copies = []
        k = 0
        for dx in (0, 1):
            for dy in (0, 1):
                for dc in (0, 1):
                    if dx or dy or dc:
                        peer = (1 - x if dx else x, 1 - y if dy else y, 1 - c if dc else c)
                        cp = _remote(v_ref, o_ref.at[me], send_sems.at[k], recv_sems.at[k], peer)
                        cp.start()
                        copies.append(cp)
                        k += 1
        for cp in copies:
            cp.wait()
        local.wait()

    return pl.pallas_call(
        body, name=name, in_specs=[_ANY], out_specs=_ANY,
        out_shape=jax.ShapeDtypeStruct((8,) + vec.shape, vec.dtype),
        scratch_shapes=[pltpu.SemaphoreType.DMA((7,)), pltpu.SemaphoreType.DMA((7,)), pltpu.SemaphoreType.DMA(())],
        compiler_params=_params(),
    )(vec)


_HBM = pl.BlockSpec(memory_space=pltpu.HBM)
_SEM = pl.BlockSpec(memory_space=pltpu.SEMAPHORE)
_EFFECT = pltpu.SideEffectType.DATAFLOW_SIDE_EFFECTING


def _copies_start(srcs, lands, plan, n_copies, *, name, after=()):
    ns, n = len(srcs), len(srcs) + len(lands)
    na = len(after)

    def body(*refs):
        send_sems, recv_sems = refs[n + na], refs[n + na + 1]
        token = refs[-1]
        for k, (src, dst, dev) in enumerate(plan(refs[:ns], refs[ns:n])):
            _remote(src, dst, send_sems.at[k], recv_sems.at[k], dev).start()
        token[...] = jnp.zeros_like(token)

    arrays = list(srcs) + list(lands)
    outs = pl.pallas_call(
        body, name=name,
        out_shape=(pltpu.SemaphoreType.DMA((n_copies,)), pltpu.SemaphoreType.DMA((n_copies,)),
                   *[pltpu.HBM(a.shape, a.dtype) for a in arrays], jax.ShapeDtypeStruct((8, LANES), F32)),
        in_specs=[_HBM] * n + [_ANY] * na,
        out_specs=(_SEM, _SEM, *[_HBM] * n, pl.BlockSpec(memory_space=pltpu.VMEM)),
        input_output_aliases={i: 2 + i for i in range(n)},
        compiler_params=pltpu.CompilerParams(has_side_effects=_EFFECT),
    )(*[pltpu.with_memory_space_constraint(a, pltpu.HBM) for a in arrays], *after)
    return outs[0], outs[1], list(outs[2:2 + ns]), list(outs[2 + ns:2 + n]), outs[-1]


def _copies_wait(send_sems, recv_sems, srcs, lands, plan, first, after, *, name):
    ns, n = len(srcs), len(srcs) + len(lands)

    def body(*refs):
        send, recv = refs[n], refs[n + 1]
        for k, (src, dst, dev) in enumerate(plan(refs[:ns], refs[ns:n])):
            cp = _remote(src, dst, send.at[first + k], recv.at[first + k], dev)
            cp.wait_send()
            cp.wait_recv()

    arrays = list(srcs) + list(lands)
    outs = pl.pallas_call(
        body, name=name, out_shape=tuple(pltpu.HBM(a.shape, a.dtype) for a in arrays),
        in_specs=[_HBM] * n + [_SEM, _SEM] + [_ANY] * len(after), out_specs=tuple([_HBM] * n),
        input_output_aliases={i: i for i in range(n)},
        compiler_params=pltpu.CompilerParams(has_side_effects=_EFFECT),
    )(*arrays, send_sems, recv_sems, *after)
    return list(outs[:ns]), list(outs[ns:])


def _gather_plan(halved):
    def plan(srcs, lands):
        x, y, c = _coords()
        me = 2 * x + y
        out = []
        for i, (src, land) in enumerate(zip(srcs, lands)):
            if halved[i]:
                h = src.shape[0] // 2
                rows = pl.ds(pl.multiple_of(c * h, 16), h)
                src, dst = src.at[rows], land.at[me, rows]
            else:
                dst = land.at[me]
            out += [(src, dst, (cx, cy, c)) for cx, cy in _other_chips(x, y)]
        return out
    return plan


def _forward_halves(lands, *, name):
    n = len(lands)

    def body(*refs):
        ins, outs = refs[:n], refs[n:2 * n]
        send_sems, recv_sems = refs[2 * n:]
        x, y, c = _coords()
        copies = []
        for i in range(n):
            h = ins[i].shape[1] // 2
            rows = pl.ds(pl.multiple_of(c * h, 16), h)
            for k, (cx, cy) in enumerate(_other_chips(x, y)):
                cp = _remote(ins[i].at[2 * cx + cy, rows], outs[i].at[2 * cx + cy, rows],
                             send_sems.at[i, k], recv_sems.at[i, k], (x, y, 1 - c))
                cp.start()
                copies.append(cp)
        for cp in copies:
            cp.wait()

    return pl.pallas_call(
        body, name=name, in_specs=[_ANY] * n, out_specs=[_ANY] * n,
        out_shape=[jax.ShapeDtypeStruct(a.shape, a.dtype) for a in lands],
        input_output_aliases={i: i for i in range(n)},
        scratch_shapes=[pltpu.SemaphoreType.DMA((n, 3)), pltpu.SemaphoreType.DMA((n, 3))],
        compiler_params=_params(),
    )(*lands)


def _forward_plan(srcs, lands):
    x, y, c = _coords()
    out = []
    for land in lands:
        h = land.shape[1] // 2
        rows = pl.ds(pl.multiple_of(c * h, 16), h)
        for cx, cy in _other_chips(x, y):
            view = land.at[2 * cx + cy, rows]
            out.append((view, view, (x, y, 1 - c)))
    return out


def _share_plan(srcs, lands):
    x, y, c = _coords()
    return [(src, land, (x, y, 1 - c)) for src, land in zip(srcs, lands)]


def _pair_plan(srcs, lands):
    x, y, c = _coords()
    out = []
    for src, land in zip(srcs, lands):
        out += [(src.at[j, 1 - c], land.at[j], (x, y, 1 - c)) for j in range(N_CHIPS)]
    return out


def _all_plan(srcs, lands):
    x, y, c = _coords()
    me = 4 * x + 2 * y + c
    out = []
    for src, land in zip(srcs, lands):
        for dx in (0, 1):
            for dy in (0, 1):
                for dc in (0, 1):
                    if dx or dy or dc:
                        out.append((src, land.at[me], (1 - x if dx else x, 1 - y if dy else y, 1 - c if dc else c)))
    return out


def _chip_plan(srcs, lands):
    x, y, c = _coords()
    me = 2 * x + y
    out = []
    for src, land in zip(srcs, lands):
        out += [(src.at[2 * cx + cy], land.at[me], (cx, cy, c)) for cx, cy in _other_chips(x, y)]
    return out


ROW_BLOCK_BYTES = 2 * 1024 * 1024


def _rtile(r, pref, mult, row_bytes=None):
    if row_bytes is not None:
        pref = max(pref, ROW_BLOCK_BYTES // row_bytes)
    t = (min(r, pref) // mult) * mult
    while t >= mult:
        if r % t == 0:
            return t
        t -= mult
    return r


def _pair_add(g, recv, core, *, name):
    _, _, r2, cols = g.shape
    tr = _rtile(r2, 256, 16, row_bytes=2 * cols)

    def body(c_ref, g_ref, r_ref, o_ref):
        o_ref[...] = (g_ref[...].astype(F32) + r_ref[...].astype(F32)).astype(o_ref.dtype)

    blk = pl.BlockSpec((None, tr, cols), lambda j, i, c_ref: (j, i, 0))
    return pl.pallas_call(
        body, name=name,
        grid_spec=pltpu.PrefetchScalarGridSpec(
            num_scalar_prefetch=1, grid=(N_CHIPS, r2 // tr),
            in_specs=[pl.BlockSpec((None, None, tr, cols), lambda j, i, c_ref: (j, c_ref[0], i, 0)), blk],
            out_specs=blk),
        out_shape=jax.ShapeDtypeStruct(recv.shape, recv.dtype), compiler_params=_params(),
    )(core, g, recv)


def _sum_slots(a, out_dtype, *, name):
    n, r, cols = a.shape
    whole = n * r * cols * a.dtype.itemsize <= 4 * ROW_BLOCK_BYTES
    tr = r if whole else _rtile(r, 256, 16)

    def body(a_ref, o_ref):
        acc = a_ref[0].astype(F32)
        for j in range(1, n):
            acc = acc + a_ref[j].astype(F32)
        o_ref[...] = acc.astype(o_ref.dtype)

    return pl.pallas_call(
        body, name=name, grid=(r // tr,),
        in_specs=[pl.BlockSpec((n, tr, cols), lambda i: (0, i, 0))],
        out_specs=pl.BlockSpec((tr, cols), lambda i: (i, 0)),
        out_shape=jax.ShapeDtypeStruct((r, cols), out_dtype), compiler_params=_params(),
    )(a)


def _chip_sum(own, recv, chip, *, name):
    _, r2, cols = own.shape
    tr = _rtile(r2, 256, 16, row_bytes=2 * cols)

    def body(chip_ref, own_ref, *rest):
        o_ref = rest[-1]
        acc = None
        for j in range(N_CHIPS):
            term = jnp.where(chip_ref[0] == j, own_ref[...], rest[j][...]).astype(F32)
            acc = term if acc is None else acc + term
        o_ref[...] = acc

    def slot(j):
        return pl.BlockSpec((None, tr, cols),
                            lambda i, chip_ref: (jnp.where(chip_ref[0] == j, (j + 1) % N_CHIPS, j), i, 0))

    return pl.pallas_call(
        body, name=name,
        grid_spec=pltpu.PrefetchScalarGridSpec(
            num_scalar_prefetch=1, grid=(r2 // tr,),
            in_specs=[pl.BlockSpec((None, tr, cols), lambda i, chip_ref: (chip_ref[0], i, 0))]
                     + [slot(j) for j in range(N_CHIPS)],
            out_specs=pl.BlockSpec((tr, cols), lambda i, chip_ref: (i, 0))),
        out_shape=jax.ShapeDtypeStruct((r2, cols), F32), compiler_params=_params(),
    )(chip, own, *([recv] * N_CHIPS))


def _adam_update(w, gv, m, v):
    c1 = 1.0 / (1.0 - ADAM_B1 ** ADAM_STEP)
    c2 = 1.0 / (1.0 - ADAM_B2 ** ADAM_STEP)
    nm = ADAM_B1 * m + (1.0 - ADAM_B1) * gv
    nv = ADAM_B2 * v + (1.0 - ADAM_B2) * gv * gv
    return -ADAM_LR * ((nm * c1) / (jnp.sqrt(nv * c2) + ADAM_EPS) + ADAM_WD * w), nm, nv


def _adamw_halves(w, g_mine, g_other, m, v, core, *, name):
    r, cols = w.shape
    r2 = r // 2
    tr = _rtile(r2, 256, 8, row_bytes=4 * cols)
    nt = r2 // tr

    def body(core_ref, w_ref, gm_ref, go_ref, m_ref, v_ref, g_ref, d_ref, nm_ref, nv_ref):
        gv = jnp.where(pl.program_id(0) == core_ref[0], gm_ref[...], go_ref[...])
        g_ref[...] = gv
        d_ref[...], nm_ref[...], nv_ref[...] = _adam_update(w_ref[...], gv, m_ref[...], v_ref[...])

    full = pl.BlockSpec((tr, cols), lambda hf, i, core_ref: (hf * nt + i, 0))
    half = pl.BlockSpec((tr, cols), lambda hf, i, core_ref: (i, 0))
    shape = jax.ShapeDtypeStruct((r, cols), F32)
    return pl.pallas_call(
        body, name=name,
        grid_spec=pltpu.PrefetchScalarGridSpec(
            num_scalar_prefetch=1, grid=(2, nt), in_specs=[full, half, half, full, full], out_specs=[full] * 4),
        out_shape=[shape] * 4, compiler_params=_params(),
    )(core, w, g_mine, g_other, m, v)


def _adamw_split_rows(w, g_mine, g_other, m, v, core, *, name, tc=256):
    r, cols = w.shape
    r2 = g_mine.shape[0]
    tc = _tile(cols, tc)

    def body(core_ref, w_ref, gm_ref, go_ref, m_ref, v_ref, g_ref, d_ref, nm_ref, nv_ref):
        mine_first = core_ref[0] == 0
        for lo, hi, first in ((0, r2, True), (r2, r, False)):
            n = hi - lo
            gm, go = gm_ref[0:n, :], go_ref[0:n, :]
            gv = jnp.where(mine_first, gm, go) if first else jnp.where(mine_first, go, gm)
            g_ref[lo:hi, :] = gv
            d_ref[lo:hi, :], nm_ref[lo:hi, :], nv_ref[lo:hi, :] = _adam_update(
                w_ref[lo:hi, :], gv, m_ref[lo:hi, :], v_ref[lo:hi, :])

    full = pl.BlockSpec((r, tc), lambda j, core_ref: (0, j))
    half = pl.BlockSpec((r2, tc), lambda j, core_ref: (0, j))
    shape = jax.ShapeDtypeStruct((r, cols), F32)
    return pl.pallas_call(
        body, name=name,
        grid_spec=pltpu.PrefetchScalarGridSpec(
            num_scalar_prefetch=1, grid=(cols // tc,), in_specs=[full, half, half, full, full],
            out_specs=[full] * 4),
        out_shape=[shape] * 4, compiler_params=_params(),
    )(core, w, g_mine, g_other, m, v)


def _adamw(w, g, m, v, *, name, rows=256):
    r, cols = w.shape
    tr = _rtile(r, rows, 8)

    def body(w_ref, g_ref, m_ref, v_ref, d_ref, nm_ref, nv_ref):
        d_ref[...], nm_ref[...], nv_ref[...] = _adam_update(w_ref[...], g_ref[...], m_ref[...], v_ref[...])

    blk = pl.BlockSpec((tr, cols), lambda i: (i, 0))
    shape = jax.ShapeDtypeStruct((r, cols), F32)
    return pl.pallas_call(
        body, name=name, grid=(r // tr,), in_specs=[blk] * 4, out_specs=[blk] * 3,
        out_shape=[shape] * 3, compiler_params=_params(),
    )(w, g, m, v)


_BIG = (("w_in", 1), ("w_branch_a", 0), ("w_branch_b", 0), ("w_out", 0), ("w_up", 1), ("w_down", 0),
        ("w_ple", 1), ("w_ple_gate", 0))
_SMALL = ("gmlp_ln_g", "gmlp_ln_b", "gmlp_w_s", "gmlp_b_s", "norm_ffn_g", "conv_b", "norm_ple_g", "norm_final_g",
          "b_f", "norm_mix_g")
N_LATE = 2
_WEIGHTS = ("norm_mix_g", "w_in", "b_f", "gmlp_ln_g", "gmlp_ln_b", "gmlp_w_s", "gmlp_b_s", "w_branch_a",
            "w_branch_b", "w_out", "norm_ffn_g", "w_up", "conv_w", "conv_b", "w_down", "norm_ple_g", "w_ple",
            "w_ple_gate", "norm_final_g")
_PACK_ROWS = 8


def _pack(arrays):
    parts = []
    for a in arrays:
        flat = a.reshape(-1)
        unit = _PACK_ROWS * LANES
        flat = jnp.pad(flat, (0, (-flat.shape[0]) % unit))
        parts.append(flat.reshape(-1, LANES))
    return jnp.concatenate(parts, axis=0)


def _unpack(packed, shapes):
    out, row = [], 0
    for shp in shapes:
        size = math.prod(shp)
        rows = -(-size // (_PACK_ROWS * LANES)) * _PACK_ROWS
        out.append(packed[row:row + rows].reshape(-1)[:size].reshape(shp))
        row += rows
    return out


def _take_cols(parts, lo, hi):
    out, start = [], 0
    for a in parts:
        width = a.shape[1]
        a0, a1 = max(lo, start) - start, min(hi, start + width) - start
        if a1 > a0:
            out.append(a if (a0, a1) == (0, width) else a[:, a0:a1])
        start += width
    return out[0] if len(out) == 1 else jnp.concatenate(out, axis=1)


def _take_rows(parts, lo, hi):
    out, start = [], 0
    for a in parts:
        height = a.shape[0]
        a0, a1 = max(lo, start) - start, min(hi, start + height) - start
        if a1 > a0:
            out.append(a if (a0, a1) == (0, height) else a[a0:a1])
        start += height
    return out[0] if len(out) == 1 else jnp.concatenate(out, axis=0)


def _assemble(gathered, axis):
    n, r, cols = gathered.shape
    if axis == 0:
        return gathered.reshape(n * r, cols)
    return _take_cols([gathered[j] for j in range(n)], 0, n * cols)


def _to_chunks(parts, axis):
    rows, total = parts[0].shape[0], sum(a.shape[1] for a in parts)
    if axis == 0:
        r, cols = rows // N_CHIPS, total
        chunks = _take_cols(parts, 0, total).reshape(N_CHIPS, r, cols)
    else:
        r, cols = rows, total // N_CHIPS
        chunks = jnp.stack([_take_cols(parts, j * cols, (j + 1) * cols) for j in range(N_CHIPS)])
    return chunks.reshape(N_CHIPS, 2, r // 2, cols)


def kernel(x, p, norm_mix_g, w_in, b_f, gmlp_ln_g, gmlp_ln_b, gmlp_w_s, gmlp_b_s, w_branch_a, w_branch_b, w_out, norm_ffn_g, w_up, conv_w, conv_b, w_down, norm_ple_g, w_ple, w_ple_gate, norm_final_g, loss_target, m_norm_mix_g, m_w_in, m_b_f, m_gmlp_ln_g, m_gmlp_ln_b, m_gmlp_w_s, m_gmlp_b_s, m_w_branch_a, m_w_branch_b, m_w_out, m_norm_ffn_g, m_w_up, m_conv_w, m_conv_b, m_w_down, m_norm_ple_g, m_w_ple, m_w_ple_gate, m_norm_final_g, v_norm_mix_g, v_w_in, v_b_f, v_gmlp_ln_g, v_gmlp_ln_b, v_gmlp_w_s, v_gmlp_b_s, v_w_branch_a, v_w_branch_b, v_w_out, v_norm_ffn_g, v_w_up, v_conv_w, v_conv_b, v_w_down, v_norm_ple_g, v_w_ple, v_w_ple_gate, v_norm_final_g):
    args = dict(locals())
    wt = {n: args[n] for n in _WEIGHTS}
    mom = {n: args["m_" + n] for n in _WEIGHTS}
    var = {n: args["v_" + n] for n in _WEIGHTS}
    chip = 2 * lax.axis_index("x") + lax.axis_index("y")
    core = lax.axis_index("c").astype(jnp.int32).reshape(1)

    chip1 = chip.astype(jnp.int32).reshape(1)
    device = 2 * chip + lax.axis_index("c")
    axis_of = dict(_BIG)
    names = [n for n, _ in _BIG]
    put_mine = lambda land, mine: lax.dynamic_update_index_in_dim(land, mine, chip, 0)

    shard_in = w_in[0].astype(BF16)
    sems_in = _copies_start([shard_in], [lax.empty((N_CHIPS,) + shard_in.shape, BF16)], _gather_plan([True]), 3,
                            name="gather_start_in")
    _, wt["w_in"], mom["w_in"], var["w_in"] = lax.optimization_barrier((sems_in[4], w_in, m_w_in, v_w_in))
    shards = [wt[n][0].astype(BF16) for n in names[1:]] + [conv_w[0]]
    halved = [True] * len(names[1:]) + [False]
    lands = [lax.empty((N_CHIPS,) + a.shape, a.dtype) for a in shards]
    send_sems, recv_sems, srcs, lands, rest_token = _copies_start(
        shards, lands, _gather_plan(halved), 3 * len(shards), name="gather_start_rest", after=[sems_in[4]])
    o1 = 2 * GMLP_WIDTH
    o2 = o1 + 3 * FOX_WIDTH
    o3 = o2 + FOX_HEADS
    fpad = ((0, 0), (0, LANES - FOX_HEADS))
    w = {
        "conv_b": conv_b, "norm_mix_g": norm_mix_g, "norm_ffn_g": norm_ffn_g, "norm_ple_g": norm_ple_g,
        "norm_final_g": norm_final_g.reshape(1, D_MODEL), "b_f": jnp.pad(b_f, fpad),
        "gmlp_ln_g": gmlp_ln_g, "gmlp_ln_b": gmlp_ln_b, "gmlp_w_s": gmlp_w_s[0],
        "gmlp_b_s_t": jnp.pad(gmlp_b_s[0].T, ((0, 0), (0, LANES - GMLP_GROUPS))),
        "first_dep": rest_token,
    }

    def get_w_in(after):
        early = [a.reshape(a.shape[-2:]) for a in (wt["w_in"], mom["w_in"], var["w_in"])]
        _, got = _copies_wait(sems_in[0], sems_in[1], sems_in[2], sems_in[3], _gather_plan([True]), 0,
                              [after] + early, name="gather_wait_in")
        got = _forward_halves(got, name="gather_forward_in")
        slots = put_mine(got[0], shard_in)
        slots = [slots[j] for j in range(N_CHIPS)]
        return {"w_uv": _take_cols(slots, 0, o1), "w_qkv": _take_cols(slots, o1, o2),
                "w_f": jnp.pad(_take_cols(slots, o2, o3), fpad), "w_g": _take_cols(slots, o3, o3 + 2 * D_MODEL)}

    def start_w_rest(after):
        _, got = _copies_wait(send_sems, recv_sems, srcs, lands, _gather_plan(halved), 0, [after],
                              name="gather_wait_rest")
        ssem, rsem, _, fwd, token = _copies_start([], got[:-1], _forward_plan, 3 * len(got[:-1]),
                                                  name="gather_forward_start")
        pending["forward"] = (ssem, rsem, fwd, got[-1])
        return token

    def get_w_rest(after):
        ssem, rsem, fwd, whole = pending["forward"]
        _, fwd = _copies_wait(ssem, rsem, [], fwd, _forward_plan, 0, after, name="gather_forward_wait")
        got = fwd + [whole]
        slots = {n: put_mine(got[i], shards[i]) for i, n in enumerate(names[1:])}
        full = {n: _assemble(slots[n], axis_of[n]) for n in names[1:] if n != "w_up"}
        up = [slots["w_up"][j] for j in range(N_CHIPS)]
        return {"w_branch_a": full["w_branch_a"], "w_branch_b": full["w_branch_b"], "w_out": full["w_out"],
                "w_up_a": _take_cols(up, 0, D_FF), "w_up_b": _take_cols(up, D_FF, 2 * D_FF),
                "w_down": full["w_down"], "w_ple": full["w_ple"], "w_ple_gate": full["w_ple_gate"],
                "conv_w": _assemble(put_mine(got[-1], shards[-1]), 1)}

    grads, delta, new_m, new_v = {}, {}, {}, {}
    pending = {}

    def to_chunks(n, gr):
        return _to_chunks(gr if isinstance(gr, list) else [gr], axis_of[n])

    def pair_start(group, gfull, tag):
        chunks = [to_chunks(n, gfull[n]) for n in group]
        empty = [lax.empty((N_CHIPS,) + a.shape[2:], a.dtype) for a in chunks]
        ssem, rsem, own, recv, token = _copies_start(chunks, empty, _pair_plan, N_CHIPS * len(group),
                                                     name="grad_pair_start_" + tag)
        pending["pair_" + tag] = (ssem, rsem, own, recv)
        return token

    def reduce_start(group, gfull, tag, after=None):
        if after is None:
            chunks = [to_chunks(n, gfull[n]) for n in group]
            from_sibling = _pair_exchange(chunks, name="grad_pair_exchange_" + tag)
        else:
            ssem, rsem, own, recv = pending["pair_" + tag]
            chunks, from_sibling = _copies_wait(ssem, rsem, own, recv, _pair_plan, 0, after,
                                                name="grad_pair_wait_" + tag)
        pair_sums = [_pair_add(chunks[i], from_sibling[i], core, name="grad_pair_add_" + n) for i, n in enumerate(group)]
        empty = [lax.empty(a.shape, a.dtype) for a in pair_sums]
        ssem, rsem, own, recv, token = _copies_start(pair_sums, empty, _chip_plan, 3 * len(group),
                                                     name="grad_chip_start_" + tag)
        pending[tag] = (ssem, rsem, own, recv)
        return token

    def reduce_sum(group, tag, after):
        ssem, rsem, own, recv = pending[tag]
        own, recv = _copies_wait(ssem, rsem, own, recv, _chip_plan, 0, after, name="grad_chip_wait_" + tag)
        halves = [_chip_sum(own[i], recv[i], chip1, name="grad_chip_sum_" + n) for i, n in enumerate(group)]
        empty = [lax.empty(a.shape, a.dtype) for a in halves]
        ssem, rsem, halves, other, token = _copies_start(halves, empty, _share_plan, len(group),
                                                        name="grad_share_start_" + tag)
        pending["share_" + tag] = (ssem, rsem, halves, other)
        return token

    def reduce_update(group, tag, after):
        ssem, rsem, halves, other = pending["share_" + tag]
        halves, other_halves = _copies_wait(ssem, rsem, halves, other, _share_plan, 0, after,
                                            name="grad_share_wait_" + tag)
        for i, n in enumerate(group):
            shp = wt[n].shape
            outs = _adamw_halves(wt[n].reshape(shp[-2:]), halves[i], other_halves[i], mom[n].reshape(shp[-2:]),
                                 var[n].reshape(shp[-2:]), core, name="adamw_" + n)
            grads[n], delta[n], new_m[n], new_v[n] = (o.reshape(shp) for o in outs)
        return new_v[group[-1]]

    def reduce_finish(group, tag, after):
        ssem, rsem, own, recv = pending[tag]
        own, recv = _copies_wait(ssem, rsem, own, recv, _chip_plan, 0, after, name="grad_chip_wait_" + tag)
        halves = [_chip_sum(own[i], recv[i], chip1, name="grad_chip_sum_" + n) for i, n in enumerate(group)]
        other_halves = _pair_share(halves, name="grad_pair_share_" + tag)
        for i, n in enumerate(group):
            shp = wt[n].shape
            outs = _adamw_halves(wt[n].reshape(shp[-2:]), halves[i], other_halves[i], mom[n].reshape(shp[-2:]),
                                 var[n].reshape(shp[-2:]), core, name="adamw_" + n)
            grads[n], delta[n], new_m[n], new_v[n] = (o.reshape(shp) for o in outs)
        return new_v[group[-1]]

    ffn_group = ("w_up", "w_down", "w_ple", "w_ple_gate", "w_branch_a", "w_branch_b", "w_out")
    mix_group = ("w_in",)

    def on_grads_ffn(g):
        gfull = dict(g)
        gfull["w_up"] = [g["w_up_a"], g["w_up_b"]]
        return pair_start(ffn_group, gfull, "ffn")

    def on_grads_small(g):
        chip_token = reduce_start(ffn_group, None, "ffn", after=[g["gmlp_w_s"]])
        vec = _pack([g[n] for n in _SMALL[:-N_LATE]] + [g["conv_w"]])
        ssem, rsem, own, recv, token = _copies_start(
            [vec], [lax.empty((8,) + vec.shape, F32)], _all_plan, 7, name="small_start", after=[chip_token])
        pending["small"] = (ssem, rsem, own, recv)
        return token

    def on_grads_mix(g):
        gfull = dict(g)
        gfull["w_in"] = [g["w_uv"], g["w_qkv"], g["w_f"][:, :FOX_HEADS], g["w_g"]]
        return reduce_start(mix_group, gfull, "mix")

    def on_after_dh(dh):
        return reduce_sum(ffn_group, "ffn", [dh])

    loss, grad_x, g = _device_step(x[0], p[0, 0], loss_target[0], w, get_w_in, (start_w_rest, get_w_rest), on_grads_ffn,
                                   on_grads_small, on_grads_mix, on_after_dh)

    ffn_done = reduce_update(ffn_group, "ffn", [grad_x])
    mix_done = reduce_finish(mix_group, "mix", [ffn_done])
    ssem, rsem, own, recv = pending["small"]
    own, recv = _copies_wait(ssem, rsem, own, recv, _all_plan, 0, [mix_done], name="small_wait")
    vec_early = _sum_slots(lax.dynamic_update_index_in_dim(recv[0], own[0], device, 0), F32, name="small_sum")
    vec_late = _pack([g["b_f"][:, :FOX_HEADS], g["norm_mix_g"]])
    vec_late = _sum_slots(_all_exchange(vec_late, name="small_exchange_late"), F32, name="small_sum_late")
    early_rows = _pack([wt[n] for n in _SMALL[:-N_LATE]]).shape[0]
    vec = jnp.concatenate([vec_early[:early_rows], vec_late], axis=0)
    for n, a in zip(_SMALL, _unpack(vec, [wt[n].shape for n in _SMALL])):
        grads[n] = a
    conv_w_grad = _unpack(vec_early[early_rows:], [(3, 2 * D_FF)])[0]
    grads["conv_w"] = lax.dynamic_slice_in_dim(conv_w_grad, chip * conv_w.shape[2], conv_w.shape[2], axis=1).reshape(conv_w.shape)

    shp = conv_w.shape
    outs = _adamw(conv_w.reshape(shp[-2:]), grads["conv_w"].reshape(shp[-2:]), m_conv_w.reshape(shp[-2:]),
                  v_conv_w.reshape(shp[-2:]), name="adamw_conv_w")
    delta["conv_w"], new_m["conv_w"], new_v["conv_w"] = (o.reshape(shp) for o in outs)
    outs = _adamw(_pack([wt[n] for n in _SMALL]), vec, _pack([mom[n] for n in _SMALL]),
                  _pack([var[n] for n in _SMALL]), name="adamw_small", rows=2048)
    for d, o in zip((delta, new_m, new_v), outs):
        for n, a in zip(_SMALL, _unpack(o, [wt[n].shape for n in _SMALL])):
            d[n] = a

    total_loss = lax.psum(loss[0, 0], ("x", "y", "c"))
    return (total_loss, grad_x.reshape(x.shape), *[grads[n] for n in _WEIGHTS], *[delta[n] for n in _WEIGHTS],
            *[new_m[n] for n in _WEIGHTS], *[new_v[n] for n in _WEIGHTS])
```

```python
import math

import jax
import jax.numpy as jnp
from jax import lax
from jax.experimental import pallas as pl
from jax.experimental.pallas import tpu as pltpu

F32 = jnp.float32
BF16 = jnp.bfloat16

D_MODEL = 1024
EPS = 1e-6
CHUNK = 64
GMLP_GROUPS = 8
GMLP_BLOCK = 128
GMLP_WIDTH = 1024
FOX_HEADS = 16
FOX_HEAD_DIM = 64
FOX_WIDTH = 1024
HEAD_PAIRS = FOX_HEADS // 2
ATT_BLOCK = 128
D_FF = 2816
PLE_DIM = 256
LANES = 128
BF16_TILE_ROWS = 16
N_CHIPS = 4

ADAM_LR = 0.001
ADAM_B1 = 0.9
ADAM_B2 = 0.999
ADAM_EPS = 1e-08
ADAM_WD = 0.01
ADAM_STEP = 10

VMEM_LIMIT = 56 * 1024 * 1024
MESH = pl.DeviceIdType.MESH

_NN = (((1,), (0,)), ((), ()))
_NT = (((1,), (1,)), ((), ()))
_TN = (((0,), (0,)), ((), ()))


def _params(**kw):
    return pltpu.CompilerParams(vmem_limit_bytes=VMEM_LIMIT, **kw)


def _tile(dim, pref):
    if dim <= pref:
        return dim
    t = (pref // LANES) * LANES
    while t >= LANES:
        if dim % t == 0:
            return t
        t -= LANES
    return dim


def _dot(a, b, dn):
    return lax.dot_general(a.astype(BF16), b.astype(BF16), dn, preferred_element_type=F32)


def _gelu(x):
    c = math.sqrt(2.0 / math.pi)
    t = jnp.tanh(c * (x + 0.044715 * x * x * x))
    return 0.5 * x * (1.0 + t)


def _gelu_and_grad(x):
    c = math.sqrt(2.0 / math.pi)
    x2 = x * x
    t = jnp.tanh(c * (x + 0.044715 * x2 * x))
    g = 0.5 * x * (1.0 + t)
    dg = 0.5 * (1.0 + t) + 0.5 * x * (1.0 - t * t) * c * (1.0 + 3.0 * 0.044715 * x2)
    return g, dg


def _sigmoid(x):
    return 1.0 / (1.0 + jnp.exp(-x))


def _mm(a, b, *, mode, out_dtype, name, add=None, tm=512, tn=512, dep=None):
    if mode == "nn":
        m, k = a.shape
        k2, n = b.shape
    elif mode == "nt":
        m, k = a.shape
        n, k2 = b.shape
    else:
        k, m = a.shape
        k2, n = b.shape
    assert k == k2, (name, a.shape, b.shape)
    tm = _tile(m, tm)
    tn = _tile(n, tn)
    dn = {"nn": _NN, "nt": _NT, "tn": _TN}[mode]

    def body(a_ref, b_ref, *rest):
        o_ref = rest[-1]
        acc = _dot(a_ref[...], b_ref[...], dn)
        if add is not None:
            acc = acc + rest[0][...].astype(F32)
        o_ref[...] = acc.astype(o_ref.dtype)

    a_spec = pl.BlockSpec((k, tm), lambda i, j: (0, i)) if mode == "tn" else pl.BlockSpec((tm, k), lambda i, j: (i, 0))
    b_spec = pl.BlockSpec((tn, k), lambda i, j: (j, 0)) if mode == "nt" else pl.BlockSpec((k, tn), lambda i, j: (0, j))
    o_spec = pl.BlockSpec((tm, tn), lambda i, j: (i, j))
    in_specs = [a_spec, b_spec]
    args = [a, b]
    if add is not None:
        in_specs.append(o_spec)
        args.append(add)
    if dep is not None:
        in_specs.append(pl.BlockSpec(memory_space=pl.ANY))
        args.append(dep)
    return pl.pallas_call(
        body, name=name, grid=(m // tm, n // tn), in_specs=in_specs, out_specs=o_spec,
        out_shape=jax.ShapeDtypeStruct((m, n), out_dtype), compiler_params=_params(),
    )(*args)


def _mm_nt_sum(pairs, *, out_dtype, name, tm=256, dep=None):
    m, n = pairs[0][0].shape[0], pairs[0][1].shape[0]
    tm = _tile(m, tm)
    np_ = len(pairs)

    def body(*refs):
        o_ref = refs[-1]
        acc = None
        for p in range(np_):
            part = _dot(refs[2 * p][...], refs[2 * p + 1][...], _NT)
            acc = part if acc is None else acc + part
        o_ref[...] = acc.astype(o_ref.dtype)

    in_specs, args = [], []
    for a, b in pairs:
        assert a.shape[0] == m and b.shape[0] == n and a.shape[1] == b.shape[1], (name, a.shape, b.shape)
        in_specs += [pl.BlockSpec((tm, a.shape[1]), lambda i: (i, 0)), pl.BlockSpec(b.shape, lambda i: (0, 0))]
        args += [a, b]
    if dep is not None:
        in_specs.append(pl.BlockSpec(memory_space=pl.ANY))
        args.append(dep)
    return pl.pallas_call(
        body, name=name, grid=(m // tm,), in_specs=in_specs, out_specs=pl.BlockSpec((tm, n), lambda i: (i, 0)),
        out_shape=jax.ShapeDtypeStruct((m, n), out_dtype), compiler_params=_params(),
    )(*args)


def _rms_fwd(x, g, *, name, tm=512, dep=None):
    s, d = x.shape
    tm = _tile(s, tm)

    def body(x_ref, g_ref, *rest):
        h_ref = rest[-1]
        xv = x_ref[...]
        r = lax.rsqrt(jnp.mean(xv * xv, axis=-1, keepdims=True) + EPS)
        h_ref[...] = (xv * r * g_ref[...]).astype(h_ref.dtype)

    deps = [] if dep is None else [dep]
    return pl.pallas_call(
        body, name=name, grid=(s // tm,),
        in_specs=[pl.BlockSpec((tm, d), lambda i: (i, 0)), pl.BlockSpec((1, d), lambda i: (0, 0))]
                 + [pl.BlockSpec(memory_space=pl.ANY)] * len(deps),
        out_specs=pl.BlockSpec((tm, d), lambda i: (i, 0)),
        out_shape=jax.ShapeDtypeStruct((s, d), BF16), compiler_params=_params(),
    )(x, g, *deps)


def _rms_bwd(x, g, dh, dres, *, name, tm=512, dep=None):
    s, d = x.shape
    tm = _tile(s, tm)
    deps = [] if dep is None else [dep]

    def body(x_ref, g_ref, dh_ref, dres_ref, *rest):
        dx_ref, dxb_ref, dg_ref = rest[len(deps):]
        xv = x_ref[...]
        r = lax.rsqrt(jnp.mean(xv * xv, axis=-1, keepdims=True) + EPS)
        xhat = xv * r
        dhv = dh_ref[...].astype(F32)
        dyg = dhv * g_ref[...]
        dx = dres_ref[...] + r * (dyg - xhat * jnp.mean(dyg * xhat, axis=-1, keepdims=True))
        dx_ref[...] = dx
        dxb_ref[...] = dx.astype(dxb_ref.dtype)

        @pl.when(pl.program_id(0) == 0)
        def _():
            dg_ref[...] = jnp.zeros_like(dg_ref)

        dg_ref[...] += jnp.sum(dhv * xhat, axis=0, keepdims=True)

    row = pl.BlockSpec((tm, d), lambda i: (i, 0))
    vec = pl.BlockSpec((1, d), lambda i: (0, 0))
    return pl.pallas_call(
        body, name=name, grid=(s // tm,),
        in_specs=[row, vec, row, row] + [pl.BlockSpec(memory_space=pl.ANY)] * len(deps), out_specs=[row, row, vec],
        out_shape=[jax.ShapeDtypeStruct((s, d), F32), jax.ShapeDtypeStruct((s, d), BF16),
                   jax.ShapeDtypeStruct((1, d), F32)],
        compiler_params=_params(),
    )(x, g, dh, dres, *deps)


def _gmlp_mask():
    t = lax.broadcasted_iota(jnp.int32, (GMLP_BLOCK, GMLP_BLOCK), 0)
    s_ = lax.broadcasted_iota(jnp.int32, (GMLP_BLOCK, GMLP_BLOCK), 1)
    return (s_ // CHUNK) <= (t // CHUNK)


def _gmlp_norm(zv, ln_g, ln_b):
    vv, dvv = _gelu_and_grad(zv)
    mu = jnp.mean(vv, axis=-1, keepdims=True)
    xc = vv - mu
    rstd = lax.rsqrt(jnp.mean(xc * xc, axis=-1, keepdims=True) + EPS)
    vhat = xc * rstd
    return vhat * ln_g + ln_b, vhat, rstd, dvv


def _gmlp_fwd(z_uv, ln_g, ln_b, w_s, b_s_t, *, name):
    s = z_uv.shape[0]
    w = GMLP_WIDTH
    gd = w // GMLP_GROUPS

    def body(z_ref, lg_ref, lb_ref, ws_ref, bs_ref, a_ref):
        u = _gelu(z_ref[:, :w].astype(F32))
        vn, _, _, _ = _gmlp_norm(z_ref[:, w:].astype(F32), lg_ref[...], lb_ref[...])
        mask = _gmlp_mask()
        for g in range(GMLP_GROUPS):
            wm = jnp.where(mask, ws_ref[g], 0.0)
            mixed = _dot(wm, vn[:, g * gd:(g + 1) * gd], _NN) + bs_ref[:, g:g + 1]
            a_ref[:, g * gd:(g + 1) * gd] = (u[:, g * gd:(g + 1) * gd] * mixed).astype(a_ref.dtype)

    full = lambda shape: pl.BlockSpec(shape, lambda i: (0,) * len(shape))
    return pl.pallas_call(
        body, name=name, grid=(s // GMLP_BLOCK,),
        in_specs=[pl.BlockSpec((GMLP_BLOCK, 2 * w), lambda i: (i, 0)), full((1, w)), full((1, w)),
                  full((GMLP_GROUPS, GMLP_BLOCK, GMLP_BLOCK)), full((GMLP_BLOCK, LANES))],
        out_specs=pl.BlockSpec((GMLP_BLOCK, w), lambda i: (i, 0)),
        out_shape=jax.ShapeDtypeStruct((s, w), BF16), compiler_params=_params(),
    )(z_uv, ln_g, ln_b, w_s, b_s_t)


def _gmlp_bwd(z_uv, da, ln_g, ln_b, w_s, b_s_t, *, name):
    s = z_uv.shape[0]
    w = GMLP_WIDTH
    gd = w // GMLP_GROUPS

    def body(z_ref, da_ref, lg_ref, lb_ref, ws_ref, bs_ref, dz_ref, dws_ref, dbs_ref, dlg_ref, dlb_ref):
        @pl.when(pl.program_id(0) == 0)
        def _():
            dws_ref[...] = jnp.zeros_like(dws_ref)
            dbs_ref[...] = jnp.zeros_like(dbs_ref)
            dlg_ref[...] = jnp.zeros_like(dlg_ref)
            dlb_ref[...] = jnp.zeros_like(dlb_ref)

        u, du_dz = _gelu_and_grad(z_ref[:, :w].astype(F32))
        lg = lg_ref[...]
        vn, vhat, rstd, dvv_dz = _gmlp_norm(z_ref[:, w:].astype(F32), lg, lb_ref[...])
        dav = da_ref[...].astype(F32)
        mask = _gmlp_mask()
        lane = lax.broadcasted_iota(jnp.int32, (GMLP_BLOCK, LANES), 1)
        dvn_parts = []
        dbs = jnp.zeros((GMLP_BLOCK, LANES), F32)
        for g in range(GMLP_GROUPS):
            sl = slice(g * gd, (g + 1) * gd)
            wm = jnp.where(mask, ws_ref[g], 0.0)
            vn_g = vn[:, sl]
            mixed = _dot(wm, vn_g, _NN) + bs_ref[:, g:g + 1]
            dmixed = dav[:, sl] * u[:, sl]
            dz_ref[:, sl] = (dav[:, sl] * mixed * du_dz[:, sl]).astype(dz_ref.dtype)
            dvn_parts.append(_dot(wm, dmixed, _TN))
            dws_ref[g] += jnp.where(mask, _dot(dmixed, vn_g, _NT), 0.0)
            dbs = dbs + jnp.where(lane == g, jnp.sum(dmixed, axis=-1, keepdims=True), 0.0)
        dbs_ref[...] += dbs
        dvn = jnp.concatenate(dvn_parts, axis=-1)
        dlg_ref[...] += jnp.sum(dvn * vhat, axis=0, keepdims=True)
        dlb_ref[...] += jnp.sum(dvn, axis=0, keepdims=True)
        dyg = dvn * lg
        dvv = rstd * (dyg - jnp.mean(dyg, axis=-1, keepdims=True)
                      - vhat * jnp.mean(dyg * vhat, axis=-1, keepdims=True))
        dz_ref[:, w:] = (dvv * dvv_dz).astype(dz_ref.dtype)

    full = lambda shape: pl.BlockSpec(shape, lambda i: (0,) * len(shape))
    return pl.pallas_call(
        body, name=name, grid=(s // GMLP_BLOCK,),
        in_specs=[pl.BlockSpec((GMLP_BLOCK, 2 * w), lambda i: (i, 0)),
                  pl.BlockSpec((GMLP_BLOCK, w), lambda i: (i, 0)), full((1, w)), full((1, w)),
                  full((GMLP_GROUPS, GMLP_BLOCK, GMLP_BLOCK)), full((GMLP_BLOCK, LANES))],
        out_specs=[pl.BlockSpec((GMLP_BLOCK, 2 * w), lambda i: (i, 0)),
                   full((GMLP_GROUPS, GMLP_BLOCK, GMLP_BLOCK)), full((GMLP_BLOCK, LANES)),
                   full((1, w)), full((1, w))],
        out_shape=[jax.ShapeDtypeStruct((s, 2 * w), BF16),
                   jax.ShapeDtypeStruct((GMLP_GROUPS, GMLP_BLOCK, GMLP_BLOCK), F32),
                   jax.ShapeDtypeStruct((GMLP_BLOCK, LANES), F32),
                   jax.ShapeDtypeStruct((1, w), F32), jax.ShapeDtypeStruct((1, w), F32)],
        compiler_params=_params(),
    )(z_uv, da, ln_g, ln_b, w_s, b_s_t)


def _tri(lower):
    r = lax.broadcasted_iota(jnp.int32, (ATT_BLOCK, ATT_BLOCK), 0)
    c = lax.broadcasted_iota(jnp.int32, (ATT_BLOCK, ATT_BLOCK), 1)
    return jnp.where((c <= r) if lower else (c >= r), 1.0, 0.0).astype(F32)


def _log_sigmoid(x):
    return jnp.minimum(x, 0.0) - jnp.log(1.0 + jnp.exp(-jnp.abs(x)))


def _fox_cum(f, b_f, *, name):
    s = f.shape[0]
    nb = s // ATT_BLOCK

    def body(f_ref, b_ref, cb_ref, ct_ref, carry):
        @pl.when(pl.program_id(0) == 0)
        def _():
            carry[...] = jnp.zeros_like(carry)

        lf = _log_sigmoid(f_ref[...] + b_ref[...])
        cum = lax.dot_general(_tri(True), lf, _NN, precision=lax.Precision.HIGHEST,
                              preferred_element_type=F32) + carry[...]
        carry[...] = cum[ATT_BLOCK - 1:ATT_BLOCK, :]
        for h in range(FOX_HEADS):
            cb_ref[h] = jnp.broadcast_to(cum[:, h:h + 1], (ATT_BLOCK, LANES))
        ct_ref[...] = cum.T

    return pl.pallas_call(
        body, name=name, grid=(nb,),
        in_specs=[pl.BlockSpec((ATT_BLOCK, LANES), lambda i: (i, 0)), pl.BlockSpec((1, LANES), lambda i: (0, 0))],
        out_specs=[pl.BlockSpec((FOX_HEADS, ATT_BLOCK, LANES), lambda i: (0, i, 0)),
                   pl.BlockSpec((LANES, ATT_BLOCK), lambda i: (0, i))],
        out_shape=[jax.ShapeDtypeStruct((FOX_HEADS, s, LANES), F32), jax.ShapeDtypeStruct((LANES, s), F32)],
        scratch_shapes=[pltpu.VMEM((1, LANES), F32)], compiler_params=_params(),
    )(f, b_f)


def _fox_dlogit(dcum_t, f, b_f, *, name):
    s = f.shape[0]
    nb = s // ATT_BLOCK

    def body(dc_ref, f_ref, b_ref, df_ref, db_ref, carry):
        @pl.when(pl.program_id(0) == 0)
        def _():
            carry[...] = jnp.zeros_like(carry)
            db_ref[...] = jnp.zeros_like(db_ref)

        d = dc_ref[...].T
        dlog = lax.dot_general(_tri(False), d, _NN, precision=lax.Precision.HIGHEST,
                               preferred_element_type=F32) + carry[...]
        carry[...] = dlog[0:1, :]
        df = dlog * (1.0 - _sigmoid(f_ref[...] + b_ref[...]))
        df_ref[...] = df
        db_ref[...] += jnp.sum(df, axis=0, keepdims=True)

    rev = lambda i: nb - 1 - i
    return pl.pallas_call(
        body, name=name, grid=(nb,),
        in_specs=[pl.BlockSpec((LANES, ATT_BLOCK), lambda i: (0, rev(i))),
                  pl.BlockSpec((ATT_BLOCK, LANES), lambda i: (rev(i), 0)),
                  pl.BlockSpec((1, LANES), lambda i: (0, 0))],
        out_specs=[pl.BlockSpec((ATT_BLOCK, LANES), lambda i: (rev(i), 0)),
                   pl.BlockSpec((1, LANES), lambda i: (0, 0))],
        out_shape=[jax.ShapeDtypeStruct((s, LANES), F32), jax.ShapeDtypeStruct((1, LANES), F32)],
        scratch_shapes=[pltpu.VMEM((1, LANES), F32)], compiler_params=_params(),
    )(dcum_t, f, b_f)


def _head_mask():
    return lax.broadcasted_iota(jnp.int32, (1, LANES), 1) < FOX_HEAD_DIM


ATT_TQ = 256
ATT_TK = 256
ATT_SCALE = FOX_HEAD_DIM ** -0.5
assert ATT_SCALE == 0.125 and ATT_TQ == ATT_TK


def _causal_t(qi, ki):
    kpos = lax.broadcasted_iota(jnp.int32, (ATT_TK, ATT_TQ), 0) + ki * ATT_TK
    qpos = lax.broadcasted_iota(jnp.int32, (ATT_TK, ATT_TQ), 1) + qi * ATT_TQ
    return kpos <= qpos


def _row_mask():
    return lax.broadcasted_iota(jnp.int32, (LANES, 1), 0) < FOX_HEAD_DIM


def _lane_tile(a, width):
    return a if a.shape[1] == width else jnp.tile(a, (1, width // a.shape[1]))


def _transpose_bf16(a):
    return a.astype(F32).T.astype(BF16)


def _attn_fwd_t(qkv, cum_b, cum_r, *, name):
    s = qkv.shape[0]
    nq = s // ATT_TQ
    npair = HEAD_PAIRS

    def body(q_ref, k_ref, v_ref, cq_ref, ck_ref, o_ref, ot_ref, l_ref, vt_ref):
        qi = pl.program_id(1)
        rows = _row_mask()

        @pl.when(qi == 0)
        def _():
            vt_ref[...] = _transpose_bf16(v_ref[...])

        qt = _transpose_bf16(q_ref[...]) * ATT_SCALE
        zero = jnp.zeros_like(qt)
        qts = (jnp.where(rows, qt, zero), jnp.where(rows, zero, qt))

        def step(ki, carry, masked):
            off = pl.multiple_of(ki * ATT_TK, ATT_TK)
            k2 = k_ref[pl.ds(off, ATT_TK), :]
            vt = vt_ref[:, pl.ds(off, ATT_TK)]
            out = []
            for hh in range(2):
                m, l, acc = carry[hh]
                bias = cq_ref[hh:hh + 1, :] - _lane_tile(ck_ref[hh, pl.ds(off, ATT_TK), :], ATT_TQ)
                sc = _dot(k2, qts[hh], _NN) + bias
                if masked:
                    sc = jnp.where(_causal_t(qi, ki), sc, -1e30)
                m_new = jnp.maximum(m, jnp.max(sc, axis=0, keepdims=True))
                alpha = jnp.exp(m - m_new)
                p = jnp.exp(sc - m_new)
                l = alpha * l + jnp.sum(p, axis=0, keepdims=True)
                p_hi = p.astype(BF16)
                p_lo = (p - p_hi.astype(F32)).astype(BF16)
                acc = alpha * acc + (_dot(vt, p_hi, _NN) + _dot(vt, p_lo, _NN))
                out.append((m_new, l, acc))
            return tuple(out)

        init = tuple((jnp.full((1, ATT_TQ), -1e30, F32), jnp.zeros((1, ATT_TQ), F32),
                      jnp.zeros((LANES, ATT_TQ), F32)) for _ in range(2))
        carry = lax.fori_loop(0, qi // 2, lambda kk, c: step(2 * kk + 1, step(2 * kk, c, False), False), init)
        carry = lax.cond(qi % 2 == 1, lambda c: step(qi - 1, c, False), lambda c: c, carry)
        (ma, la, acca), (mb, lb, accb) = step(qi, carry, True)
        ot = jnp.where(rows, acca / la, accb / lb)
        ot_ref[...] = ot
        o_ref[...] = ot.T.astype(o_ref.dtype)
        l_ref[0:1, :] = ma + jnp.log(la)
        l_ref[1:2, :] = mb + jnp.log(lb)

    row = pl.BlockSpec((None, 2, ATT_TQ), lambda j, i: (j, 0, i))
    return pl.pallas_call(
        body, name=name, grid=(npair, nq),
        in_specs=[pl.BlockSpec((ATT_TQ, LANES), lambda j, i: (i, j)),
                  pl.BlockSpec((s, LANES), lambda j, i: (0, npair + j)),
                  pl.BlockSpec((s, LANES), lambda j, i: (0, 2 * npair + j)),
                  row, pl.BlockSpec((None, 2, s, LANES), lambda j, i: (j, 0, 0, 0))],
        out_specs=[pl.BlockSpec((ATT_TQ, LANES), lambda j, i: (i, j)),
                   pl.BlockSpec((LANES, ATT_TQ), lambda j, i: (j, i)), row],
        out_shape=[jax.ShapeDtypeStruct((s, FOX_WIDTH), BF16), jax.ShapeDtypeStruct((FOX_WIDTH, s), F32),
                   jax.ShapeDtypeStruct((npair, 2, s), F32)],
        scratch_shapes=[pltpu.VMEM((LANES, s), BF16)],
        compiler_params=_params(),
    )(qkv, qkv, qkv, cum_r, cum_b)


def _attn_bwd_t(qkv, do, o_t, lse, cum_b, cum_r, *, name, dep=None):
    s = qkv.shape[0]
    nq = s // ATT_TQ
    npair = HEAD_PAIRS

    deps = [] if dep is None else [dep]

    def body(q_ref, k_ref, v_ref, do_ref, ot_ref, l_ref, cq_ref, ck_ref, *rest):
        dq_ref, dk_ref, dv_ref, dc_ref, qt_ref, dot_ref, dqt_ref, dl_ref = rest[len(deps):]
        ki = pl.program_id(1)
        m0 = _head_mask()
        rows = _row_mask()
        k2 = k_ref[...]
        v2 = v_ref[...]
        kt = _transpose_bf16(k2)
        ks = k2 * ATT_SCALE
        kz, vz = jnp.zeros_like(k2), jnp.zeros_like(v2)
        khs = (jnp.where(m0, ks, kz), jnp.where(m0, kz, ks))
        vhs = (jnp.where(m0, v2, vz), jnp.where(m0, vz, v2))
        cks = tuple(_lane_tile(ck_ref[hh], ATT_TQ) for hh in range(2))

        @pl.when(ki == 0)
        def _():
            dqt_ref[...] = jnp.zeros_like(dqt_ref)
            qt_ref[...] = _transpose_bf16(q_ref[...])
            do_t = do_ref[...].astype(F32).T
            dot_ref[...] = do_t.astype(BF16)
            prod = do_t * ot_ref[...]
            dl_ref[0:1, :] = jnp.sum(prod[:FOX_HEAD_DIM], axis=0, keepdims=True)
            dl_ref[1:2, :] = jnp.sum(prod[FOX_HEAD_DIM:], axis=0, keepdims=True)

        def step(qi, carry, masked):
            off = pl.multiple_of(qi * ATT_TQ, ATT_TQ)
            q2 = q_ref[pl.ds(off, ATT_TQ), :]
            do2 = do_ref[pl.ds(off, ATT_TQ), :]
            qt = qt_ref[:, pl.ds(off, ATT_TQ)]
            dot_ = dot_ref[:, pl.ds(off, ATT_TQ)]
            out, dqs = [], []
            for hh in range(2):
                dk_acc, dv_acc, dc_acc = carry[hh]
                sc = _dot(khs[hh], qt, _NN) + (cq_ref[hh:hh + 1, pl.ds(off, ATT_TQ)] - cks[hh])
                p = jnp.exp(sc - l_ref[hh:hh + 1, pl.ds(off, ATT_TQ)])
                if masked:
                    p = jnp.where(_causal_t(qi, ki), p, 0.0)
                dp = _dot(vhs[hh], dot_, _NN)
                ds = p * (dp - dl_ref[hh:hh + 1, pl.ds(off, ATT_TQ)])
                dc_acc = dc_acc - jnp.sum(ds, axis=1, keepdims=True)
                dss = (ds * ATT_SCALE).astype(BF16)
                dv_acc = dv_acc + _dot(p, do2, _NN)
                dk_acc = dk_acc + _dot(dss, q2, _NN)
                dqs.append(_dot(kt, dss, _NN))
                out.append((dk_acc, dv_acc, dc_acc))
            dqt_ref[:, pl.ds(off, ATT_TQ)] += jnp.where(rows, dqs[0], dqs[1])
            return tuple(out)

        init = tuple((jnp.zeros((ATT_TK, LANES), F32), jnp.zeros((ATT_TK, LANES), F32),
                      jnp.zeros((ATT_TK, 1), F32)) for _ in range(2))
        carry = step(ki, init, True)
        rest = nq - 1 - ki
        carry = lax.fori_loop(
            0, rest // 2, lambda t, c: step(ki + 2 + 2 * t, step(ki + 1 + 2 * t, c, False), False), carry)
        carry = lax.cond(rest % 2 == 1, lambda c: step(nq - 1, c, False), lambda c: c, carry)
        (dka, dva, dca), (dkb, dvb, dcb) = carry
        dk_ref[...] = jnp.where(m0, dka, dkb).astype(dk_ref.dtype)
        dv_ref[...] = jnp.where(m0, dva, dvb).astype(dv_ref.dtype)
        dc_ref[0] = jnp.broadcast_to(dca, (ATT_TK, LANES))
        dc_ref[1] = jnp.broadcast_to(dcb, (ATT_TK, LANES))

        @pl.when(ki == nq - 1)
        def _():
            dq_ref[...] = dqt_ref[...].T.astype(dq_ref.dtype)

    colfull = lambda base: pl.BlockSpec((s, LANES), lambda j, i: (0, base + j))
    colblk = lambda base: pl.BlockSpec((ATT_TK, LANES), lambda j, i: (i, base + j))
    stat = pl.BlockSpec((None, 2, s), lambda j, i: (j, 0, 0))
    bcast = pl.BlockSpec((None, 2, ATT_TK, LANES), lambda j, i: (j, 0, i, 0))
    grad = jax.ShapeDtypeStruct((s, FOX_WIDTH), BF16)
    return pl.pallas_call(
        body, name=name, grid=(npair, nq),
        in_specs=[colfull(0), colblk(npair), colblk(2 * npair), colfull(0),
                  pl.BlockSpec((LANES, s), lambda j, i: (j, 0)), stat, stat, bcast]
                 + [pl.BlockSpec(memory_space=pl.ANY)] * len(deps),
        out_specs=[colfull(0), colblk(0), colblk(0), bcast],
        out_shape=[grad, grad, grad, jax.ShapeDtypeStruct((npair, 2, s, LANES), F32)],
        scratch_shapes=[pltpu.VMEM((LANES, s), BF16), pltpu.VMEM((LANES, s), BF16), pltpu.VMEM((LANES, s), F32),
                        pltpu.VMEM((2, s), F32)],
        compiler_params=_params(),
    )(qkv, qkv, qkv, do, o_t, lse, cum_r, cum_b, *deps)


def _merge_fwd(zg, ya, yb, *, name, tm=512):
    s, d = ya.shape
    tm = _tile(s, tm)

    def body(zg_ref, ya_ref, yb_ref, m_ref):
        ga = _sigmoid(zg_ref[:, :d].astype(F32))
        gb = _sigmoid(zg_ref[:, d:].astype(F32))
        m_ref[...] = (ga * ya_ref[...].astype(F32) + gb * yb_ref[...].astype(F32)).astype(m_ref.dtype)

    row = pl.BlockSpec((tm, d), lambda i: (i, 0))
    row2 = pl.BlockSpec((tm, 2 * d), lambda i: (i, 0))
    return pl.pallas_call(
        body, name=name, grid=(s // tm,), in_specs=[row2, row, row], out_specs=row,
        out_shape=jax.ShapeDtypeStruct((s, d), BF16), compiler_params=_params(),
    )(zg, ya, yb)


def _merge_bwd(dm, zg, ya, yb, *, name, tm=512):
    s, d = ya.shape
    tm = _tile(s, tm)

    def body(dm_ref, zg_ref, ya_ref, yb_ref, dzg_ref, dya_ref, dyb_ref):
        dmv = dm_ref[...].astype(F32)
        ga = _sigmoid(zg_ref[:, :d].astype(F32))
        gb = _sigmoid(zg_ref[:, d:].astype(F32))
        dzg_ref[:, :d] = (dmv * ya_ref[...].astype(F32) * ga * (1.0 - ga)).astype(dzg_ref.dtype)
        dzg_ref[:, d:] = (dmv * yb_ref[...].astype(F32) * gb * (1.0 - gb)).astype(dzg_ref.dtype)
        dya_ref[...] = (dmv * ga).astype(dya_ref.dtype)
        dyb_ref[...] = (dmv * gb).astype(dyb_ref.dtype)

    row = pl.BlockSpec((tm, d), lambda i: (i, 0))
    row2 = pl.BlockSpec((tm, 2 * d), lambda i: (i, 0))
    return pl.pallas_call(
        body, name=name, grid=(s // tm,), in_specs=[row, row2, row, row], out_specs=[row2, row, row],
        out_shape=[jax.ShapeDtypeStruct((s, 2 * d), BF16), jax.ShapeDtypeStruct((s, d), BF16),
                   jax.ShapeDtypeStruct((s, d), BF16)],
        compiler_params=_params(),
    )(dm, zg, ya, yb)


SUBLANES = 8


def _shift_down(u, k, row):
    rolled = pltpu.roll(u, k, 0)
    head = jnp.where(row[:SUBLANES] >= k, rolled[:SUBLANES], 0.0)
    return jnp.concatenate([head, rolled[SUBLANES:]], axis=0)


def _shift_up(u, k, row):
    n = u.shape[0]
    rolled = pltpu.roll(u, n - k, 0)
    tail = jnp.where(row[n - SUBLANES:] < n - k, rolled[n - SUBLANES:], 0.0)
    return jnp.concatenate([rolled[:n - SUBLANES], tail], axis=0)


def _conv_act_fwd(up_a, up_b, cw_a, cw_b, cb_a, cb_b, *, name, tc=128):
    s, f = up_a.shape
    tc = _tile(f, tc)

    def body(ua_ref, ub_ref, wa_ref, wb_ref, ba_ref, bb_ref, act_ref):
        row = lax.broadcasted_iota(jnp.int32, (s, tc), 0)

        def conv(u_ref, w_ref, b_ref):
            u = u_ref[...].astype(F32)
            return (b_ref[...] + w_ref[0:1, :] * _shift_down(u, 2, row)
                    + w_ref[1:2, :] * _shift_down(u, 1, row) + w_ref[2:3, :] * u)

        ca = conv(ua_ref, wa_ref, ba_ref)
        cb = conv(ub_ref, wb_ref, bb_ref)
        act_ref[...] = (_gelu(ca) * cb).astype(act_ref.dtype)

    col = pl.BlockSpec((s, tc), lambda j: (0, j))
    w3 = pl.BlockSpec((3, tc), lambda j: (0, j))
    b1 = pl.BlockSpec((1, tc), lambda j: (0, j))
    return pl.pallas_call(
        body, name=name, grid=(f // tc,), in_specs=[col, col, w3, w3, b1, b1], out_specs=col,
        out_shape=jax.ShapeDtypeStruct((s, f), BF16), compiler_params=_params(),
    )(up_a, up_b, cw_a, cw_b, cb_a, cb_b)


def _conv_act_bwd(up_a, up_b, dact, cw_a, cw_b, cb_a, cb_b, *, name, tc=128):
    s, f = up_a.shape
    tc = _tile(f, tc)

    def body(ua_ref, ub_ref, da_ref, wa_ref, wb_ref, ba_ref, bb_ref, dua_ref, dub_ref, dwa_ref, dwb_ref):
        row = lax.broadcasted_iota(jnp.int32, (s, tc), 0)

        def conv(u_ref, w_ref, b_ref):
            u = u_ref[...].astype(F32)
            u1 = _shift_down(u, 1, row)
            u2 = _shift_down(u, 2, row)
            return u, u1, u2, b_ref[...] + w_ref[0:1, :] * u2 + w_ref[1:2, :] * u1 + w_ref[2:3, :] * u

        def back(dc, taps, w_ref, du_ref, dw_ref):
            u, u1, u2 = taps
            dw_ref[0:1, :] = jnp.sum(dc * u2, axis=0, keepdims=True)
            dw_ref[1:2, :] = jnp.sum(dc * u1, axis=0, keepdims=True)
            dw_ref[2:3, :] = jnp.sum(dc * u, axis=0, keepdims=True)
            dw_ref[3:4, :] = jnp.sum(dc, axis=0, keepdims=True)
            du = (w_ref[2:3, :] * dc + w_ref[1:2, :] * _shift_up(dc, 1, row)
                  + w_ref[0:1, :] * _shift_up(dc, 2, row))
            du_ref[...] = du.astype(du_ref.dtype)

        ua, ua1, ua2, ca = conv(ua_ref, wa_ref, ba_ref)
        ub, ub1, ub2, cb = conv(ub_ref, wb_ref, bb_ref)
        g, dg = _gelu_and_grad(ca)
        dact_v = da_ref[...].astype(F32)
        back(dact_v * cb * dg, (ua, ua1, ua2), wa_ref, dua_ref, dwa_ref)
        back(dact_v * g, (ub, ub1, ub2), wb_ref, dub_ref, dwb_ref)

    col = pl.BlockSpec((s, tc), lambda j: (0, j))
    w3 = pl.BlockSpec((3, tc), lambda j: (0, j))
    w4 = pl.BlockSpec((4, tc), lambda j: (0, j))
    b1 = pl.BlockSpec((1, tc), lambda j: (0, j))
    return pl.pallas_call(
        body, name=name, grid=(f // tc,), in_specs=[col, col, col, w3, w3, b1, b1],
        out_specs=[col, col, w4, w4],
        out_shape=[jax.ShapeDtypeStruct((s, f), BF16), jax.ShapeDtypeStruct((s, f), BF16),
                   jax.ShapeDtypeStruct((4, f), F32), jax.ShapeDtypeStruct((4, f), F32)],
        compiler_params=_params(),
    )(up_a, up_b, dact, cw_a, cw_b, cb_a, cb_b)


def _ple_final(x2, ple, zp, target, g_final, *, name, tm=512):
    s, d = x2.shape
    tm = _tile(s, tm)

    def body(x_ref, ple_ref, zp_ref, t_ref, g_ref, dx_ref, dple_ref, dzp_ref, dg_ref, loss_ref):
        @pl.when(pl.program_id(0) == 0)
        def _():
            dg_ref[...] = jnp.zeros_like(dg_ref)
            loss_ref[...] = jnp.zeros_like(loss_ref)

        gp = _sigmoid(zp_ref[...].astype(F32))
        plev = ple_ref[...].astype(F32)
        x3 = x_ref[...] + plev * gp
        r = lax.rsqrt(jnp.mean(x3 * x3, axis=-1, keepdims=True) + EPS)
        xhat = x3 * r
        gv = g_ref[...]
        diff = xhat * gv - t_ref[...]
        loss_ref[...] += 0.5 * jnp.sum(jnp.mean(diff * diff, axis=-1, keepdims=True), axis=0, keepdims=True)
        dy = diff * (1.0 / d)
        dg_ref[...] += jnp.sum(dy * xhat, axis=0, keepdims=True)
        dyg = dy * gv
        dx3 = r * (dyg - xhat * jnp.mean(dyg * xhat, axis=-1, keepdims=True))
        dx_ref[...] = dx3
        dple_ref[...] = (dx3 * gp).astype(dple_ref.dtype)
        dzp_ref[...] = (dx3 * plev * gp * (1.0 - gp)).astype(dzp_ref.dtype)

    row = pl.BlockSpec((tm, d), lambda i: (i, 0))
    vec = pl.BlockSpec((1, d), lambda i: (0, 0))
    return pl.pallas_call(
        body, name=name, grid=(s // tm,), in_specs=[row, row, row, row, vec],
        out_specs=[row, row, row, vec, pl.BlockSpec((1, LANES), lambda i: (0, 0))],
        out_shape=[jax.ShapeDtypeStruct((s, d), F32), jax.ShapeDtypeStruct((s, d), BF16),
                   jax.ShapeDtypeStruct((s, d), BF16), jax.ShapeDtypeStruct((1, d), F32),
                   jax.ShapeDtypeStruct((1, LANES), F32)],
        compiler_params=_params(),
    )(x2, ple, zp, target, g_final)


def _device_step(x, p, target, w, get_w_in=None, get_w_rest=None, on_grads_ffn=None, on_grads_small=None,
                 on_grads_mix=None, on_after_dh=None):
    s = x.shape[0]
    g = {}
    w = dict(w)

    h = _rms_fwd(x, w["norm_mix_g"], name="rms_mix", dep=w.get("first_dep"))
    if get_w_in is not None:
        w.update(get_w_in(h))
    qkv = _mm(h, w["w_qkv"], mode="nn", out_dtype=BF16, name="proj_qkv", tm=1024)
    f = _mm(h, w["w_f"], mode="nn", out_dtype=F32, name="proj_f", tm=1024)

    cum_b, cum_t = _fox_cum(f, w["b_f"], name="fox_cum")
    cum_b = cum_b.reshape(HEAD_PAIRS, 2, s, LANES)
    cum_r = cum_t[:FOX_HEADS].reshape(HEAD_PAIRS, 2, s)
    b, o_t, lse = _attn_fwd_t(qkv, cum_b, cum_r, name="attn_fwd")

    dep = get_w_rest[0](b) if get_w_rest is not None else None
    z_uv = _mm(h, w["w_uv"], mode="nn", out_dtype=BF16, name="proj_uv", tm=1024, dep=dep)
    zg = _mm(h, w["w_g"], mode="nn", out_dtype=BF16, name="proj_gate", tm=1024, dep=dep)
    a = _gmlp_fwd(z_uv, w["gmlp_ln_g"], w["gmlp_ln_b"], w["gmlp_w_s"], w["gmlp_b_s_t"], name="gmlp_fwd")
    if get_w_rest is not None:
        w.update(get_w_rest[1]([a, zg]))

    ya = _mm(a, w["w_branch_a"], mode="nn", out_dtype=BF16, name="branch_a", tm=1024)
    yb = _mm(b, w["w_branch_b"], mode="nn", out_dtype=BF16, name="branch_b", tm=1024)
    merged = _merge_fwd(zg, ya, yb, name="merge_fwd")
    x1 = _mm(merged, w["w_out"], mode="nn", out_dtype=F32, name="proj_out", add=x, tm=1024)

    h2 = _rms_fwd(x1, w["norm_ffn_g"], name="rms_ffn")
    up_a = _mm(h2, w["w_up_a"], mode="nn", out_dtype=BF16, name="up_a", tm=1024, tn=D_FF // 2)
    up_b = _mm(h2, w["w_up_b"], mode="nn", out_dtype=BF16, name="up_b", tm=1024, tn=D_FF // 2)
    cw, cb = w["conv_w"], w["conv_b"]
    conv_args = (cw[:, :D_FF], cw[:, D_FF:], cb[:, :D_FF], cb[:, D_FF:])
    act = _conv_act_fwd(up_a, up_b, *conv_args, name="conv_act_fwd")
    x2 = _mm(act, w["w_down"], mode="nn", out_dtype=F32, name="down", add=x1, tm=512)

    h3 = _rms_fwd(x2, w["norm_ple_g"], name="rms_ple")
    ple = _mm(p, w["w_ple"], mode="nn", out_dtype=BF16, name="ple_proj", tm=1024)
    zp = _mm(h3, w["w_ple_gate"], mode="nn", out_dtype=BF16, name="ple_gate", tm=1024)
    dx3, dple, dzp, g["norm_final_g"], loss = _ple_final(x2, ple, zp, target, w["norm_final_g"], name="ple_final")

    g["w_ple"] = _mm(p, dple, mode="tn", out_dtype=BF16, name="dw_ple")
    g["w_ple_gate"] = _mm(h3, dzp, mode="tn", out_dtype=BF16, name="dw_ple_gate")
    dh3 = _mm(dzp, w["w_ple_gate"], mode="nt", out_dtype=BF16, name="dh3")
    dx2, dx2_b, g["norm_ple_g"] = _rms_bwd(x2, w["norm_ple_g"], dh3, dx3, name="rms_ple_bwd")

    g["w_down"] = _mm(act, dx2_b, mode="tn", out_dtype=BF16, name="dw_down", tm=D_FF // 2)
    dact = _mm(dx2_b, w["w_down"], mode="nt", out_dtype=BF16, name="dact", tn=D_FF // 2)
    dup_a, dup_b, dcw_a, dcw_b = _conv_act_bwd(up_a, up_b, dact, *conv_args, name="conv_act_bwd")
    g["conv_w"] = jnp.concatenate([dcw_a[:3], dcw_b[:3]], axis=1)
    g["conv_b"] = jnp.concatenate([dcw_a[3:], dcw_b[3:]], axis=1)
    g["w_up_a"] = _mm(h2, dup_a, mode="tn", out_dtype=BF16, name="dw_up_a", tn=D_FF // 2)
    g["w_up_b"] = _mm(h2, dup_b, mode="tn", out_dtype=BF16, name="dw_up_b", tn=D_FF // 2)
    dh2 = _mm_nt_sum([(dup_a, w["w_up_a"]), (dup_b, w["w_up_b"])], out_dtype=BF16, name="dh2")
    dx1, dx1_b, g["norm_ffn_g"] = _rms_bwd(x1, w["norm_ffn_g"], dh2, dx2, name="rms_ffn_bwd")

    g["w_out"] = _mm(merged, dx1_b, mode="tn", out_dtype=BF16, name="dw_out")
    dmerged = _mm(dx1_b, w["w_out"], mode="nt", out_dtype=BF16, name="dmerged")
    dzg, dya, dyb = _merge_bwd(dmerged, zg, ya, yb, name="merge_bwd")
    g["w_branch_a"] = _mm(a, dya, mode="tn", out_dtype=BF16, name="dw_branch_a")
    g["w_branch_b"] = _mm(b, dyb, mode="tn", out_dtype=BF16, name="dw_branch_b")
    dep = on_grads_ffn(g) if on_grads_ffn is not None else None
    da = _mm(dya, w["w_branch_a"], mode="nt", out_dtype=BF16, name="da", dep=dep)
    db = _mm(dyb, w["w_branch_b"], mode="nt", out_dtype=BF16, name="db")

    dz_uv, g["gmlp_w_s"], dbs_t, g["gmlp_ln_g"], g["gmlp_ln_b"] = _gmlp_bwd(
        z_uv, da, w["gmlp_ln_g"], w["gmlp_ln_b"], w["gmlp_w_s"], w["gmlp_b_s_t"], name="gmlp_bwd")
    g["gmlp_b_s"] = dbs_t[:, :GMLP_GROUPS].T
    dep = on_grads_small(g) if on_grads_small is not None else None

    dq, dk, dv, dcum_b = _attn_bwd_t(qkv, db, o_t, lse, cum_b, cum_r, name="attn_bwd", dep=dep)
    dcum_t = jnp.pad(dcum_b[..., 0].reshape(FOX_HEADS, s), ((0, LANES - FOX_HEADS), (0, 0)))
    df, g["b_f"] = _fox_dlogit(dcum_t, f, w["b_f"], name="fox_dlogit")
    dqkv = jnp.concatenate([dq, dk, dv], axis=1)

    g["w_uv"] = _mm(h, dz_uv, mode="tn", out_dtype=BF16, name="dw_uv")
    g["w_qkv"] = _mm(h, dqkv, mode="tn", out_dtype=BF16, name="dw_qkv")
    g["w_f"] = _mm(h, df, mode="tn", out_dtype=BF16, name="dw_f")
    g["w_g"] = _mm(h, dzg, mode="tn", out_dtype=BF16, name="dw_g")
    dep = on_grads_mix(g) if on_grads_mix is not None else None
    dh = _mm_nt_sum([(dz_uv, w["w_uv"]), (dqkv, w["w_qkv"]), (df, w["w_f"]), (dzg, w["w_g"])],
                    out_dtype=BF16, name="dh", dep=dep)
    dep = on_after_dh(dh) if on_after_dh is not None else None
    dx0, _, g["norm_mix_g"] = _rms_bwd(x, w["norm_mix_g"], dh, dx1, name="rms_mix_bwd", dep=dep)
    return loss, dx0, g


def _coords():
    return lax.axis_index("x"), lax.axis_index("y"), lax.axis_index("c")


def _other_chips(x, y):
    return [(1 - x, y), (x, 1 - y), (1 - x, 1 - y)]


def _remote(src, dst, send_sem, recv_sem, dev):
    return pltpu.make_async_remote_copy(src_ref=src, dst_ref=dst, send_sem=send_sem, recv_sem=recv_sem,
                                        device_id=dev, device_id_type=MESH)


_ANY = pl.BlockSpec(memory_space=pl.ANY)


def _pair_exchange(gs, *, name):
    n = len(gs)

    def body(*refs):
        ins, outs = refs[:n], refs[n:2 * n]
        send_sems, recv_sems = refs[2 * n:]
        x, y, c = _coords()
        copies = []
        for i in range(n):
            for j in range(N_CHIPS):
                cp = _remote(ins[i].at[j, 1 - c], outs[i].at[j], send_sems.at[i, j], recv_sems.at[i, j], (x, y, 1 - c))
                cp.start()
                copies.append(cp)
        for cp in copies:
            cp.wait()

    return pl.pallas_call(
        body, name=name, in_specs=[_ANY] * n, out_specs=[_ANY] * n,
        out_shape=[jax.ShapeDtypeStruct((N_CHIPS,) + a.shape[2:], a.dtype) for a in gs],
        scratch_shapes=[pltpu.SemaphoreType.DMA((n, N_CHIPS)), pltpu.SemaphoreType.DMA((n, N_CHIPS))],
        compiler_params=_params(),
    )(*gs)


def _pair_share(hs, *, name):
    n = len(hs)

    def body(*refs):
        ins, outs = refs[:n], refs[n:2 * n]
        send_sems, recv_sems = refs[2 * n:]
        x, y, c = _coords()
        copies = []
        for i in range(n):
            cp = _remote(ins[i], outs[i], send_sems.at[i], recv_sems.at[i], (x, y, 1 - c))
            cp.start()
            copies.append(cp)
        for cp in copies:
            cp.wait()

    return pl.pallas_call(
        body, name=name, in_specs=[_ANY] * n, out_specs=[_ANY] * n,
        out_shape=[jax.ShapeDtypeStruct(a.shape, a.dtype) for a in hs],
        scratch_shapes=[pltpu.SemaphoreType.DMA((n,)), pltpu.SemaphoreType.DMA((n,))],
        compiler_params=_params(),
    )(*hs)


def _all_exchange(vec, *, name):
    def body(v_ref, o_ref, send_sems, recv_sems, local_sem):
        x, y, c = _coords()
        me = 4 * x + 2 * y + c
        local = pltpu.make_async_copy(v_ref, o_ref.at[me], local_sem)
        local.start()
        copies = []
        k = 0
        for dx in (0, 1):
            for dy in (0, 1):
                for dc in (0, 1):
                    if dx or dy or dc:
                        peer = (1 - x if dx else x, 1 - y if dy else y, 1 - c if dc else c)
                        cp = _remote(v_ref, o_ref.at[me], send_sems.at[k], recv_sems.at[k], peer)
                        cp.start()
                        copies.append(cp)
                        k += 1
        for cp in copies:
            cp.wait()
        local.wait()

    return pl.pallas_call(
        body, name=name, in_specs=[_ANY], out_specs=_ANY,
        out_shape=jax.ShapeDtypeStruct((8,) + vec.shape, vec.dtype),
        scratch_shapes=[pltpu.SemaphoreType.DMA((7,)), pltpu.SemaphoreType.DMA((7,)), pltpu.SemaphoreType.DMA(())],
        compiler_params=_params(),
    )(vec)


_HBM = pl.BlockSpec(memory_space=pltpu.HBM)
_SEM = pl.BlockSpec(memory_space=pltpu.SEMAPHORE)
_EFFECT = pltpu.SideEffectType.DATAFLOW_SIDE_EFFECTING


def _copies_start(srcs, lands, plan, n_copies, *, name, after=()):
    ns, n = len(srcs), len(srcs) + len(lands)
    na = len(after)

    def body(*refs):
        send_sems, recv_sems = refs[n + na], refs[n + na + 1]
        token = refs[-1]
        for k, (src, dst, dev) in enumerate(plan(refs[:ns], refs[ns:n])):
            _remote(src, dst, send_sems.at[k], recv_sems.at[k], dev).start()
        token[...] = jnp.zeros_like(token)

    arrays = list(srcs) + list(lands)
    outs = pl.pallas_call(
        body, name=name,
        out_shape=(pltpu.SemaphoreType.DMA((n_copies,)), pltpu.SemaphoreType.DMA((n_copies,)),
                   *[pltpu.HBM(a.shape, a.dtype) for a in arrays], jax.ShapeDtypeStruct((8, LANES), F32)),
        in_specs=[_HBM] * n + [_ANY] * na,
        out_specs=(_SEM, _SEM, *[_HBM] * n, pl.BlockSpec(memory_space=pltpu.VMEM)),
        input_output_aliases={i: 2 + i for i in range(n)},
        compiler_params=pltpu.CompilerParams(has_side_effects=_EFFECT),
    )(*[pltpu.with_memory_space_constraint(a, pltpu.HBM) for a in arrays], *after)
    return outs[0], outs[1], list(outs[2:2 + ns]), list(outs[2 + ns:2 + n]), outs[-1]


def _copies_wait(send_sems, recv_sems, srcs, lands, plan, first, after, *, name):
    ns, n = len(srcs), len(srcs) + len(lands)

    def body(*refs):
        send, recv = refs[n], refs[n + 1]
        for k, (src, dst, dev) in enumerate(plan(refs[:ns], refs[ns:n])):
            cp = _remote(src, dst, send.at[first + k], recv.at[first + k], dev)
            cp.wait_send()
            cp.wait_recv()

    arrays = list(srcs) + list(lands)
    outs = pl.pallas_call(
        body, name=name, out_shape=tuple(pltpu.HBM(a.shape, a.dtype) for a in arrays),
        in_specs=[_HBM] * n + [_SEM, _SEM] + [_ANY] * len(after), out_specs=tuple([_HBM] * n),
        input_output_aliases={i: i for i in range(n)},
        compiler_params=pltpu.CompilerParams(has_side_effects=_EFFECT),
    )(*arrays, send_sems, recv_sems, *after)
    return list(outs[:ns]), list(outs[ns:])


def _gather_plan(halved):
    def plan(srcs, lands):
        x, y, c = _coords()
        me = 2 * x + y
        out = []
        for i, (src, land) in enumerate(zip(srcs, lands)):
            if halved[i]:
                h = src.shape[0] // 2
                rows = pl.ds(pl.multiple_of(c * h, 16), h)
                src, dst = src.at[rows], land.at[me, rows]
            else:
                dst = land.at[me]
            out += [(src, dst, (cx, cy, c)) for cx, cy in _other_chips(x, y)]
        return out
    return plan


def _forward_halves(lands, *, name):
    n = len(lands)

    def body(*refs):
        ins, outs = refs[:n], refs[n:2 * n]
        send_sems, recv_sems = refs[2 * n:]
        x, y, c = _coords()
        copies = []
        for i in range(n):
            h = ins[i].shape[1] // 2
            rows = pl.ds(pl.multiple_of(c * h, 16), h)
            for k, (cx, cy) in enumerate(_other_chips(x, y)):
                cp = _remote(ins[i].at[2 * cx + cy, rows], outs[i].at[2 * cx + cy, rows],
                             send_sems.at[i, k], recv_sems.at[i, k], (x, y, 1 - c))
                cp.start()
                copies.append(cp)
        for cp in copies:
            cp.wait()

    return pl.pallas_call(
        body, name=name, in_specs=[_ANY] * n, out_specs=[_ANY] * n,
        out_shape=[jax.ShapeDtypeStruct(a.shape, a.dtype) for a in lands],
        input_output_aliases={i: i for i in range(n)},
        scratch_shapes=[pltpu.SemaphoreType.DMA((n, 3)), pltpu.SemaphoreType.DMA((n, 3))],
        compiler_params=_params(),
    )(*lands)


def _forward_plan(srcs, lands):
    x, y, c = _coords()
    out = []
    for land in lands:
        h = land.shape[1] // 2
        rows = pl.ds(pl.multiple_of(c * h, 16), h)
        for cx, cy in _other_chips(x, y):
            view = land.at[2 * cx + cy, rows]
            out.append((view, view, (x, y, 1 - c)))
    return out


def _share_plan(srcs, lands):
    x, y, c = _coords()
    return [(src, land, (x, y, 1 - c)) for src, land in zip(srcs, lands)]


def _pair_plan(srcs, lands):
    x, y, c = _coords()
    out = []
    for src, land in zip(srcs, lands):
        out += [(src.at[j, 1 - c], land.at[j], (x, y, 1 - c)) for j in range(N_CHIPS)]
    return out


def _all_plan(srcs, lands):
    x, y, c = _coords()
    me = 4 * x + 2 * y + c
    out = []
    for src, land in zip(srcs, lands):
        for dx in (0, 1):
            for dy in (0, 1):
                for dc in (0, 1):
                    if dx or dy or dc:
                        out.append((src, land.at[me], (1 - x if dx else x, 1 - y if dy else y, 1 - c if dc else c)))
    return out


def _chip_plan(srcs, lands):
    x, y, c = _coords()
    me = 2 * x + y
    out = []
    for src, land in zip(srcs, lands):
        out += [(src.at[2 * cx + cy], land.at[me], (cx, cy, c)) for cx, cy in _other_chips(x, y)]
    return out


ROW_BLOCK_BYTES = 2 * 1024 * 1024


def _rtile(r, pref, mult, row_bytes=None):
    if row_bytes is not None:
        pref = max(pref, ROW_BLOCK_BYTES // row_bytes)
    t = (min(r, pref) // mult) * mult
    while t >= mult:
        if r % t == 0:
            return t
        t -= mult
    return r


def _pair_add(g, recv, core, *, name):
    _, _, r2, cols = g.shape
    tr = _rtile(r2, 256, 16, row_bytes=2 * cols)

    def body(c_ref, g_ref, r_ref, o_ref):
        o_ref[...] = (g_ref[...].astype(F32) + r_ref[...].astype(F32)).astype(o_ref.dtype)

    blk = pl.BlockSpec((None, tr, cols), lambda j, i, c_ref: (j, i, 0))
    return pl.pallas_call(
        body, name=name,
        grid_spec=pltpu.PrefetchScalarGridSpec(
            num_scalar_prefetch=1, grid=(N_CHIPS, r2 // tr),
            in_specs=[pl.BlockSpec((None, None, tr, cols), lambda j, i, c_ref: (j, c_ref[0], i, 0)), blk],
            out_specs=blk),
        out_shape=jax.ShapeDtypeStruct(recv.shape, recv.dtype), compiler_params=_params(),
    )(core, g, recv)


def _sum_slots(a, out_dtype, *, name):
    n, r, cols = a.shape
    whole = n * r * cols * a.dtype.itemsize <= 4 * ROW_BLOCK_BYTES
    tr = r if whole else _rtile(r, 256, 16)

    def body(a_ref, o_ref):
        acc = a_ref[0].astype(F32)
        for j in range(1, n):
            acc = acc + a_ref[j].astype(F32)
        o_ref[...] = acc.astype(o_ref.dtype)

    return pl.pallas_call(
        body, name=name, grid=(r // tr,),
        in_specs=[pl.BlockSpec((n, tr, cols), lambda i: (0, i, 0))],
        out_specs=pl.BlockSpec((tr, cols), lambda i: (i, 0)),
        out_shape=jax.ShapeDtypeStruct((r, cols), out_dtype), compiler_params=_params(),
    )(a)


def _chip_sum(own, recv, chip, *, name):
    _, r2, cols = own.shape
    tr = _rtile(r2, 256, 16, row_bytes=2 * cols)

    def body(chip_ref, own_ref, *rest):
        o_ref = rest[-1]
        acc = None
        for j in range(N_CHIPS):
            term = jnp.where(chip_ref[0] == j, own_ref[...], rest[j][...]).astype(F32)
            acc = term if acc is None else acc + term
        o_ref[...] = acc

    def slot(j):
        return pl.BlockSpec((None, tr, cols),
                            lambda i, chip_ref: (jnp.where(chip_ref[0] == j, (j + 1) % N_CHIPS, j), i, 0))

    return pl.pallas_call(
        body, name=name,
        grid_spec=pltpu.PrefetchScalarGridSpec(
            num_scalar_prefetch=1, grid=(r2 // tr,),
            in_specs=[pl.BlockSpec((None, tr, cols), lambda i, chip_ref: (chip_ref[0], i, 0))]
                     + [slot(j) for j in range(N_CHIPS)],
            out_specs=pl.BlockSpec((tr, cols), lambda i, chip_ref: (i, 0))),
        out_shape=jax.ShapeDtypeStruct((r2, cols), F32), compiler_params=_params(),
    )(chip, own, *([recv] * N_CHIPS))


def _adam_update(w, gv, m, v):
    c1 = 1.0 / (1.0 - ADAM_B1 ** ADAM_STEP)
    c2 = 1.0 / (1.0 - ADAM_B2 ** ADAM_STEP)
    nm = ADAM_B1 * m + (1.0 - ADAM_B1) * gv
    nv = ADAM_B2 * v + (1.0 - ADAM_B2) * gv * gv
    return -ADAM_LR * ((nm * c1) / (jnp.sqrt(nv * c2) + ADAM_EPS) + ADAM_WD * w), nm, nv


def _adamw_halves(w, g_mine, g_other, m, v, core, *, name):
    r, cols = w.shape
    r2 = r // 2
    tr = _rtile(r2, 256, 8, row_bytes=4 * cols)
    nt = r2 // tr

    def body(core_ref, w_ref, gm_ref, go_ref, m_ref, v_ref, g_ref, d_ref, nm_ref, nv_ref):
        gv = jnp.where(pl.program_id(0) == core_ref[0], gm_ref[...], go_ref[...])
        g_ref[...] = gv
        d_ref[...], nm_ref[...], nv_ref[...] = _adam_update(w_ref[...], gv, m_ref[...], v_ref[...])

    full = pl.BlockSpec((tr, cols), lambda hf, i, core_ref: (hf * nt + i, 0))
    half = pl.BlockSpec((tr, cols), lambda hf, i, core_ref: (i, 0))
    shape = jax.ShapeDtypeStruct((r, cols), F32)
    return pl.pallas_call(
        body, name=name,
        grid_spec=pltpu.PrefetchScalarGridSpec(
            num_scalar_prefetch=1, grid=(2, nt), in_specs=[full, half, half, full, full], out_specs=[full] * 4),
        out_shape=[shape] * 4, compiler_params=_params(),
    )(core, w, g_mine, g_other, m, v)


def _adamw(w, g, m, v, *, name, rows=256):
    r, cols = w.shape
    tr = _rtile(r, rows, 8)

    def body(w_ref, g_ref, m_ref, v_ref, d_ref, nm_ref, nv_ref):
        d_ref[...], nm_ref[...], nv_ref[...] = _adam_update(w_ref[...], g_ref[...], m_ref[...], v_ref[...])

    blk = pl.BlockSpec((tr, cols), lambda i: (i, 0))
    shape = jax.ShapeDtypeStruct((r, cols), F32)
    return pl.pallas_call(
        body, name=name, grid=(r // tr,), in_specs=[blk] * 4, out_specs=[blk] * 3,
        out_shape=[shape] * 3, compiler_params=_params(),
    )(w, g, m, v)


_BIG = (("w_in", 1), ("w_branch_a", 0), ("w_branch_b", 0), ("w_out", 0), ("w_up", 1), ("w_down", 0),
        ("w_ple", 1), ("w_ple_gate", 0))
_SMALL = ("gmlp_ln_g", "gmlp_ln_b", "gmlp_w_s", "gmlp_b_s", "norm_ffn_g", "conv_b", "norm_ple_g", "norm_final_g",
          "b_f", "norm_mix_g")
N_LATE = 2
_WEIGHTS = ("norm_mix_g", "w_in", "b_f", "gmlp_ln_g", "gmlp_ln_b", "gmlp_w_s", "gmlp_b_s", "w_branch_a",
            "w_branch_b", "w_out", "norm_ffn_g", "w_up", "conv_w", "conv_b", "w_down", "norm_ple_g", "w_ple",
            "w_ple_gate", "norm_final_g")
_PACK_ROWS = 8


def _pack(arrays):
    parts = []
    for a in arrays:
        flat = a.reshape(-1)
        unit = _PACK_ROWS * LANES
        flat = jnp.pad(flat, (0, (-flat.shape[0]) % unit))
        parts.append(flat.reshape(-1, LANES))
    return jnp.concatenate(parts, axis=0)


def _unpack(packed, shapes):
    out, row = [], 0
    for shp in shapes:
        size = math.prod(shp)
        rows = -(-size // (_PACK_ROWS * LANES)) * _PACK_ROWS
        out.append(packed[row:row + rows].reshape(-1)[:size].reshape(shp))
        row += rows
    return out


def _take_cols(parts, lo, hi):
    out, start = [], 0
    for a in parts:
        width = a.shape[1]
        a0, a1 = max(lo, start) - start, min(hi, start + width) - start
        if a1 > a0:
            out.append(a if (a0, a1) == (0, width) else a[:, a0:a1])
        start += width
    return out[0] if len(out) == 1 else jnp.concatenate(out, axis=1)


def _assemble(gathered, axis):
    n, r, cols = gathered.shape
    if axis == 0:
        return gathered.reshape(n * r, cols)
    return _take_cols([gathered[j] for j in range(n)], 0, n * cols)


def _to_chunks(parts, axis):
    rows, total = parts[0].shape[0], sum(a.shape[1] for a in parts)
    if axis == 0:
        r, cols = rows // N_CHIPS, total
        chunks = _take_cols(parts, 0, total).reshape(N_CHIPS, r, cols)
    else:
        r, cols = rows, total // N_CHIPS
        chunks = jnp.stack([_take_cols(parts, j * cols, (j + 1) * cols) for j in range(N_CHIPS)])
    return chunks.reshape(N_CHIPS, 2, r // 2, cols)


def kernel(x, p, norm_mix_g, w_in, b_f, gmlp_ln_g, gmlp_ln_b, gmlp_w_s, gmlp_b_s, w_branch_a, w_branch_b, w_out, norm_ffn_g, w_up, conv_w, conv_b, w_down, norm_ple_g, w_ple, w_ple_gate, norm_final_g, loss_target, m_norm_mix_g, m_w_in, m_b_f, m_gmlp_ln_g, m_gmlp_ln_b, m_gmlp_w_s, m_gmlp_b_s, m_w_branch_a, m_w_branch_b, m_w_out, m_norm_ffn_g, m_w_up, m_conv_w, m_conv_b, m_w_down, m_norm_ple_g, m_w_ple, m_w_ple_gate, m_norm_final_g, v_norm_mix_g, v_w_in, v_b_f, v_gmlp_ln_g, v_gmlp_ln_b, v_gmlp_w_s, v_gmlp_b_s, v_w_branch_a, v_w_branch_b, v_w_out, v_norm_ffn_g, v_w_up, v_conv_w, v_conv_b, v_w_down, v_norm_ple_g, v_w_ple, v_w_ple_gate, v_norm_final_g):
    args = dict(locals())
    wt = {n: args[n] for n in _WEIGHTS}
    mom = {n: args["m_" + n] for n in _WEIGHTS}
    var = {n: args["v_" + n] for n in _WEIGHTS}
    chip = 2 * lax.axis_index("x") + lax.axis_index("y")
    core = lax.axis_index("c").astype(jnp.int32).reshape(1)

    chip1 = chip.astype(jnp.int32).reshape(1)
    device = 2 * chip + lax.axis_index("c")
    axis_of = dict(_BIG)
    names = [n for n, _ in _BIG]
    put_mine = lambda land, mine: lax.dynamic_update_index_in_dim(land, mine, chip, 0)

    shard_in = w_in[0].astype(BF16)
    sems_in = _copies_start([shard_in], [lax.empty((N_CHIPS,) + shard_in.shape, BF16)], _gather_plan([True]), 3,
                            name="gather_start_in")
    _, wt["w_in"], mom["w_in"], var["w_in"] = lax.optimization_barrier((sems_in[4], w_in, m_w_in, v_w_in))
    shards = [wt[n][0].astype(BF16) for n in names[1:]] + [conv_w[0]]
    halved = [True] * len(names[1:]) + [False]
    lands = [lax.empty((N_CHIPS,) + a.shape, a.dtype) for a in shards]
    send_sems, recv_sems, srcs, lands, rest_token = _copies_start(
        shards, lands, _gather_plan(halved), 3 * len(shards), name="gather_start_rest", after=[sems_in[4]])
    o1 = 2 * GMLP_WIDTH
    o2 = o1 + 3 * FOX_WIDTH
    o3 = o2 + FOX_HEADS
    fpad = ((0, 0), (0, LANES - FOX_HEADS))
    w = {
        "conv_b": conv_b, "norm_mix_g": norm_mix_g, "norm_ffn_g": norm_ffn_g, "norm_ple_g": norm_ple_g,
        "norm_final_g": norm_final_g.reshape(1, D_MODEL), "b_f": jnp.pad(b_f, fpad),
        "gmlp_ln_g": gmlp_ln_g, "gmlp_ln_b": gmlp_ln_b, "gmlp_w_s": gmlp_w_s[0],
        "gmlp_b_s_t": jnp.pad(gmlp_b_s[0].T, ((0, 0), (0, LANES - GMLP_GROUPS))),
        "first_dep": rest_token,
    }

    def get_w_in(after):
        early = [a.reshape(a.shape[-2:]) for a in (wt["w_in"], mom["w_in"], var["w_in"])]
        _, got = _copies_wait(sems_in[0], sems_in[1], sems_in[2], sems_in[3], _gather_plan([True]), 0,
                              [after] + early, name="gather_wait_in")
        got = _forward_halves(got, name="gather_forward_in")
        slots = put_mine(got[0], shard_in)
        slots = [slots[j] for j in range(N_CHIPS)]
        return {"w_uv": _take_cols(slots, 0, o1), "w_qkv": _take_cols(slots, o1, o2),
                "w_f": jnp.pad(_take_cols(slots, o2, o3), fpad), "w_g": _take_cols(slots, o3, o3 + 2 * D_MODEL)}

    def start_w_rest(after):
        _, got = _copies_wait(send_sems, recv_sems, srcs, lands, _gather_plan(halved), 0, [after],
                              name="gather_wait_rest")
        ssem, rsem, _, fwd, token = _copies_start([], got[:-1], _forward_plan, 3 * len(got[:-1]),
                                                  name="gather_forward_start")
        pending["forward"] = (ssem, rsem, fwd, got[-1])
        return token

    def get_w_rest(after):
        ssem, rsem, fwd, whole = pending["forward"]
        _, fwd = _copies_wait(ssem, rsem, [], fwd, _forward_plan, 0, after, name="gather_forward_wait")
        got = fwd + [whole]
        slots = {n: put_mine(got[i], shards[i]) for i, n in enumerate(names[1:])}
        full = {n: _assemble(slots[n], axis_of[n]) for n in names[1:] if n != "w_up"}
        up = [slots["w_up"][j] for j in range(N_CHIPS)]
        return {"w_branch_a": full["w_branch_a"], "w_branch_b": full["w_branch_b"], "w_out": full["w_out"],
                "w_up_a": _take_cols(up, 0, D_FF), "w_up_b": _take_cols(up, D_FF, 2 * D_FF),
                "w_down": full["w_down"], "w_ple": full["w_ple"], "w_ple_gate": full["w_ple_gate"],
                "conv_w": _assemble(put_mine(got[-1], shards[-1]), 1)}

    grads, delta, new_m, new_v = {}, {}, {}, {}
    pending = {}

    def to_chunks(n, gr):
        return _to_chunks(gr if isinstance(gr, list) else [gr], axis_of[n])

    def pair_start(group, gfull, tag):
        chunks = [to_chunks(n, gfull[n]) for n in group]
        empty = [lax.empty((N_CHIPS,) + a.shape[2:], a.dtype) for a in chunks]
        ssem, rsem, own, recv, token = _copies_start(chunks, empty, _pair_plan, N_CHIPS * len(group),
                                                     name="grad_pair_start_" + tag)
        pending["pair_" + tag] = (ssem, rsem, own, recv)
        return token

    def reduce_start(group, gfull, tag, after=None):
        if after is None:
            chunks = [to_chunks(n, gfull[n]) for n in group]
            from_sibling = _pair_exchange(chunks, name="grad_pair_exchange_" + tag)
        else:
            ssem, rsem, own, recv = pending["pair_" + tag]
            chunks, from_sibling = _copies_wait(ssem, rsem, own, recv, _pair_plan, 0, after,
                                                name="grad_pair_wait_" + tag)
        pair_sums = [_pair_add(chunks[i], from_sibling[i], core, name="grad_pair_add_" + n) for i, n in enumerate(group)]
        empty = [lax.empty(a.shape, a.dtype) for a in pair_sums]
        ssem, rsem, own, recv, token = _copies_start(pair_sums, empty, _chip_plan, 3 * len(group),
                                                     name="grad_chip_start_" + tag)
        pending[tag] = (ssem, rsem, own, recv)
        return token

    def reduce_sum(group, tag, after):
        ssem, rsem, own, recv = pending[tag]
        own, recv = _copies_wait(ssem, rsem, own, recv, _chip_plan, 0, after, name="grad_chip_wait_" + tag)
        halves = [_chip_sum(own[i], recv[i], chip1, name="grad_chip_sum_" + n) for i, n in enumerate(group)]
        empty = [lax.empty(a.shape, a.dtype) for a in halves]
        ssem, rsem, halves, other, token = _copies_start(halves, empty, _share_plan, len(group),
                                                        name="grad_share_start_" + tag)
        pending["share_" + tag] = (ssem, rsem, halves, other)
        return token

    def reduce_update(group, tag, after):
        ssem, rsem, halves, other = pending["share_" + tag]
        halves, other_halves = _copies_wait(ssem, rsem, halves, other, _share_plan, 0, after,
                                            name="grad_share_wait_" + tag)
        for i, n in enumerate(group):
            shp = wt[n].shape
            outs = _adamw_halves(wt[n].reshape(shp[-2:]), halves[i], other_halves[i], mom[n].reshape(shp[-2:]),
                                 var[n].reshape(shp[-2:]), core, name="adamw_" + n)
            grads[n], delta[n], new_m[n], new_v[n] = (o.reshape(shp) for o in outs)
        return new_v[group[-1]]

    def reduce_finish(group, tag, after):
        ssem, rsem, own, recv = pending[tag]
        own, recv = _copies_wait(ssem, rsem, own, recv, _chip_plan, 0, after, name="grad_chip_wait_" + tag)
        halves = [_chip_sum(own[i], recv[i], chip1, name="grad_chip_sum_" + n) for i, n in enumerate(group)]
        other_halves = _pair_share(halves, name="grad_pair_share_" + tag)
        for i, n in enumerate(group):
            shp = wt[n].shape
            outs = _adamw_halves(wt[n].reshape(shp[-2:]), halves[i], other_halves[i], mom[n].reshape(shp[-2:]),
                                 var[n].reshape(shp[-2:]), core, name="adamw_" + n)
            grads[n], delta[n], new_m[n], new_v[n] = (o.reshape(shp) for o in outs)
        return new_v[group[-1]]

    ffn_group = ("w_up", "w_down", "w_ple", "w_ple_gate", "w_branch_a", "w_branch_b", "w_out")
    mix_group = ("w_in",)

    def on_grads_ffn(g):
        gfull = dict(g)
        gfull["w_up"] = [g["w_up_a"], g["w_up_b"]]
        return pair_start(ffn_group, gfull, "ffn")

    def on_grads_small(g):
        chip_token = reduce_start(ffn_group, None, "ffn", after=[g["gmlp_w_s"]])
        vec = _pack([g[n] for n in _SMALL[:-N_LATE]] + [g["conv_w"]])
        ssem, rsem, own, recv, token = _copies_start(
            [vec], [lax.empty((8,) + vec.shape, F32)], _all_plan, 7, name="small_start", after=[chip_token])
        pending["small"] = (ssem, rsem, own, recv)
        return token

    def on_grads_mix(g):
        gfull = dict(g)
        gfull["w_in"] = [g["w_uv"], g["w_qkv"], g["w_f"][:, :FOX_HEADS], g["w_g"]]
        return reduce_start(mix_group, gfull, "mix")

    def on_after_dh(dh):
        return reduce_sum(ffn_group, "ffn", [dh])

    loss, grad_x, g = _device_step(x[0], p[0, 0], loss_target[0], w, get_w_in, (start_w_rest, get_w_rest), on_grads_ffn,
                                   on_grads_small, on_grads_mix, on_after_dh)

    ffn_done = reduce_update(ffn_group, "ffn", [grad_x])
    mix_done = reduce_finish(mix_group, "mix", [ffn_done])
    ssem, rsem, own, recv = pending["small"]
    own, recv = _copies_wait(ssem, rsem, own, recv, _all_plan, 0, [mix_done], name="small_wait")
    vec_early = _sum_slots(lax.dynamic_update_index_in_dim(recv[0], own[0], device, 0), F32, name="small_sum")
    vec_late = _pack([g["b_f"][:, :FOX_HEADS], g["norm_mix_g"]])
    vec_late = _sum_slots(_all_exchange(vec_late, name="small_exchange_late"), F32, name="small_sum_late")
    early_rows = _pack([wt[n] for n in _SMALL[:-N_LATE]]).shape[0]
    vec = jnp.concatenate([vec_early[:early_rows], vec_late], axis=0)
    for n, a in zip(_SMALL, _unpack(vec, [wt[n].shape for n in _SMALL])):
        grads[n] = a
    conv_w_grad = _unpack(vec_early[early_rows:], [(3, 2 * D_FF)])[0]
    grads["conv_w"] = lax.dynamic_slice_in_dim(conv_w_grad, chip * conv_w.shape[2], conv_w.shape[2], axis=1).reshape(conv_w.shape)

    shp = conv_w.shape
    outs = _adamw(conv_w.reshape(shp[-2:]), grads["conv_w"].reshape(shp[-2:]), m_conv_w.reshape(shp[-2:]),
                  v_conv_w.reshape(shp[-2:]), name="adamw_conv_w")
    delta["conv_w"], new_m["conv_w"], new_v["conv_w"] = (o.reshape(shp) for o in outs)
    outs = _adamw(_pack([wt[n] for n in _SMALL]), vec, _pack([mom[n] for n in _SMALL]),
                  _pack([var[n] for n in _SMALL]), name="adamw_small", rows=2048)
    for d, o in zip((delta, new_m, new_v), outs):
        for n, a in zip(_SMALL, _unpack(o, [wt[n].shape for n in _SMALL])):
            d[n] = a

    total_loss = lax.psum(loss[0, 0], ("x", "y", "c"))
    return (total_loss, grad_x.reshape(x.shape), *[grads[n] for n in _WEIGHTS], *[delta[n] for n in _WEIGHTS],
            *[new_m[n] for n in _WEIGHTS], *[new_v[n] for n in _WEIGHTS])
```

```python
import math

import jax
import jax.numpy as jnp
from jax import lax
from jax.experimental import pallas as pl
from jax.experimental.pallas import tpu as pltpu

F32 = jnp.float32
BF16 = jnp.bfloat16

D_MODEL = 1024
EPS = 1e-6
CHUNK = 64
GMLP_GROUPS = 8
GMLP_BLOCK = 128
GMLP_WIDTH = 1024
FOX_HEADS = 16
FOX_HEAD_DIM = 64
FOX_WIDTH = 1024
HEAD_PAIRS = FOX_HEADS // 2
ATT_BLOCK = 128
D_FF = 2816
PLE_DIM = 256
LANES = 128
BF16_TILE_ROWS = 16
N_CHIPS = 4

ADAM_LR = 0.001
ADAM_B1 = 0.9
ADAM_B2 = 0.999
ADAM_EPS = 1e-08
ADAM_WD = 0.01
ADAM_STEP = 10

VMEM_LIMIT = 56 * 1024 * 1024
MESH = pl.DeviceIdType.MESH

_NN = (((1,), (0,)), ((), ()))
_NT = (((1,), (1,)), ((), ()))
_TN = (((0,), (0,)), ((), ()))


def _params(**kw):
    return pltpu.CompilerParams(vmem_limit_bytes=VMEM_LIMIT, **kw)


def _tile(dim, pref):
    if dim <= pref:
        return dim
    t = (pref // LANES) * LANES
    while t >= LANES:
        if dim % t == 0:
            return t
        t -= LANES
    return dim


def _dot(a, b, dn):
    return lax.dot_general(a.astype(BF16), b.astype(BF16), dn, preferred_element_type=F32)


def _gelu(x):
    c = math.sqrt(2.0 / math.pi)
    t = jnp.tanh(c * (x + 0.044715 * x * x * x))
    return 0.5 * x * (1.0 + t)


def _gelu_and_grad(x):
    c = math.sqrt(2.0 / math.pi)
    x2 = x * x
    t = jnp.tanh(c * (x + 0.044715 * x2 * x))
    g = 0.5 * x * (1.0 + t)
    dg = 0.5 * (1.0 + t) + 0.5 * x * (1.0 - t * t) * c * (1.0 + 3.0 * 0.044715 * x2)
    return g, dg


def _sigmoid(x):
    return 1.0 / (1.0 + jnp.exp(-x))


def _mm(a, b, *, mode, out_dtype, name, add=None, tm=512, tn=1024, dep=None):
    if mode == "nn":
        m, k = a.shape
        k2, n = b.shape
    elif mode == "nt":
        m, k = a.shape
        n, k2 = b.shape
    else:
        k, m = a.shape
        k2, n = b.shape
    assert k == k2, (name, a.shape, b.shape)
    tm = _tile(m, tm)
    tn = _tile(n, tn)
    dn = {"nn": _NN, "nt": _NT, "tn": _TN}[mode]

    def body(a_ref, b_ref, *rest):
        o_ref = rest[-1]
        acc = _dot(a_ref[...], b_ref[...], dn)
        if add is not None:
            acc = acc + rest[0][...].astype(F32)
        o_ref[...] = acc.astype(o_ref.dtype)

    a_spec = pl.BlockSpec((k, tm), lambda i, j: (0, i)) if mode == "tn" else pl.BlockSpec((tm, k), lambda i, j: (i, 0))
    b_spec = pl.BlockSpec((tn, k), lambda i, j: (j, 0)) if mode == "nt" else pl.BlockSpec((k, tn), lambda i, j: (0, j))
    o_spec = pl.BlockSpec((tm, tn), lambda i, j: (i, j))
    in_specs = [a_spec, b_spec]
    args = [a, b]
    if add is not None:
        in_specs.append(o_spec)
        args.append(add)
    if dep is not None:
        in_specs.append(pl.BlockSpec(memory_space=pl.ANY))
        args.append(dep)
    return pl.pallas_call(
        body, name=name, grid=(m // tm, n // tn), in_specs=in_specs, out_specs=o_spec,
        out_shape=jax.ShapeDtypeStruct((m, n), out_dtype), compiler_params=_params(),
    )(*args)


def _mm_nt_sum(pairs, *, out_dtype, name, tm=256, dep=None):
    m, n = pairs[0][0].shape[0], pairs[0][1].shape[0]
    tm = _tile(m, tm)
    np_ = len(pairs)

    def body(*refs):
        o_ref = refs[-1]
        acc = None
        for p in range(np_):
            part = _dot(refs[2 * p][...], refs[2 * p + 1][...], _NT)
            acc = part if acc is None else acc + part
        o_ref[...] = acc.astype(o_ref.dtype)

    in_specs, args = [], []
    for a, b in pairs:
        assert a.shape[0] == m and b.shape[0] == n and a.shape[1] == b.shape[1], (name, a.shape, b.shape)
        in_specs += [pl.BlockSpec((tm, a.shape[1]), lambda i: (i, 0)), pl.BlockSpec(b.shape, lambda i: (0, 0))]
        args += [a, b]
    if dep is not None:
        in_specs.append(pl.BlockSpec(memory_space=pl.ANY))
        args.append(dep)
    return pl.pallas_call(
        body, name=name, grid=(m // tm,), in_specs=in_specs, out_specs=pl.BlockSpec((tm, n), lambda i: (i, 0)),
        out_shape=jax.ShapeDtypeStruct((m, n), out_dtype), compiler_params=_params(),
    )(*args)


def _rms_fwd(x, g, *, name, tm=512, dep=None):
    s, d = x.shape
    tm = _tile(s, tm)

    def body(x_ref, g_ref, *rest):
        h_ref = rest[-1]
        xv = x_ref[...]
        r = lax.rsqrt(jnp.mean(xv * xv, axis=-1, keepdims=True) + EPS)
        h_ref[...] = (xv * r * g_ref[...]).astype(h_ref.dtype)

    deps = [] if dep is None else [dep]
    return pl.pallas_call(
        body, name=name, grid=(s // tm,),
        in_specs=[pl.BlockSpec((tm, d), lambda i: (i, 0)), pl.BlockSpec((1, d), lambda i: (0, 0))]
                 + [pl.BlockSpec(memory_space=pl.ANY)] * len(deps),
        out_specs=pl.BlockSpec((tm, d), lambda i: (i, 0)),
        out_shape=jax.ShapeDtypeStruct((s, d), BF16), compiler_params=_params(),
    )(x, g, *deps)


def _rms_bwd(x, g, dh, dres, *, name, tm=512, dep=None):
    s, d = x.shape
    tm = _tile(s, tm)
    deps = [] if dep is None else [dep]

    def body(x_ref, g_ref, dh_ref, dres_ref, *rest):
        dx_ref, dxb_ref, dg_ref = rest[len(deps):]
        xv = x_ref[...]
        r = lax.rsqrt(jnp.mean(xv * xv, axis=-1, keepdims=True) + EPS)
        xhat = xv * r
        dhv = dh_ref[...].astype(F32)
        dyg = dhv * g_ref[...]
        dx = dres_ref[...] + r * (dyg - xhat * jnp.mean(dyg * xhat, axis=-1, keepdims=True))
        dx_ref[...] = dx
        dxb_ref[...] = dx.astype(dxb_ref.dtype)

        @pl.when(pl.program_id(0) == 0)
        def _():
            dg_ref[...] = jnp.zeros_like(dg_ref)

        dg_ref[...] += jnp.sum(dhv * xhat, axis=0, keepdims=True)

    row = pl.BlockSpec((tm, d), lambda i: (i, 0))
    vec = pl.BlockSpec((1, d), lambda i: (0, 0))
    return pl.pallas_call(
        body, name=name, grid=(s // tm,),
        in_specs=[row, vec, row, row] + [pl.BlockSpec(memory_space=pl.ANY)] * len(deps), out_specs=[row, row, vec],
        out_shape=[jax.ShapeDtypeStruct((s, d), F32), jax.ShapeDtypeStruct((s, d), BF16),
                   jax.ShapeDtypeStruct((1, d), F32)],
        compiler_params=_params(),
    )(x, g, dh, dres, *deps)


def _gmlp_mask():
    t = lax.broadcasted_iota(jnp.int32, (GMLP_BLOCK, GMLP_BLOCK), 0)
    s_ = lax.broadcasted_iota(jnp.int32, (GMLP_BLOCK, GMLP_BLOCK), 1)
    return (s_ // CHUNK) <= (t // CHUNK)


def _gmlp_norm(zv, ln_g, ln_b):
    vv, dvv = _gelu_and_grad(zv)
    mu = jnp.mean(vv, axis=-1, keepdims=True)
    xc = vv - mu
    rstd = lax.rsqrt(jnp.mean(xc * xc, axis=-1, keepdims=True) + EPS)
    vhat = xc * rstd
    return vhat * ln_g + ln_b, vhat, rstd, dvv


def _gmlp_fwd(z_uv, ln_g, ln_b, w_s, b_s_t, *, name):
    s = z_uv.shape[0]
    w = GMLP_WIDTH
    gd = w // GMLP_GROUPS

    def body(z_ref, lg_ref, lb_ref, ws_ref, bs_ref, a_ref):
        u = _gelu(z_ref[:, :w].astype(F32))
        vn, _, _, _ = _gmlp_norm(z_ref[:, w:].astype(F32), lg_ref[...], lb_ref[...])
        mask = _gmlp_mask()
        for g in range(GMLP_GROUPS):
            wm = jnp.where(mask, ws_ref[g], 0.0)
            mixed = _dot(wm, vn[:, g * gd:(g + 1) * gd], _NN) + bs_ref[:, g:g + 1]
            a_ref[:, g * gd:(g + 1) * gd] = (u[:, g * gd:(g + 1) * gd] * mixed).astype(a_ref.dtype)

    full = lambda shape: pl.BlockSpec(shape, lambda i: (0,) * len(shape))
    return pl.pallas_call(
        body, name=name, grid=(s // GMLP_BLOCK,),
        in_specs=[pl.BlockSpec((GMLP_BLOCK, 2 * w), lambda i: (i, 0)), full((1, w)), full((1, w)),
                  full((GMLP_GROUPS, GMLP_BLOCK, GMLP_BLOCK)), full((GMLP_BLOCK, LANES))],
        out_specs=pl.BlockSpec((GMLP_BLOCK, w), lambda i: (i, 0)),
        out_shape=jax.ShapeDtypeStruct((s, w), BF16), compiler_params=_params(),
    )(z_uv, ln_g, ln_b, w_s, b_s_t)


def _gmlp_bwd(z_uv, da, ln_g, ln_b, w_s, b_s_t, *, name):
    s = z_uv.shape[0]
    w = GMLP_WIDTH
    gd = w // GMLP_GROUPS

    def body(z_ref, da_ref, lg_ref, lb_ref, ws_ref, bs_ref, dz_ref, dws_ref, dbs_ref, dlg_ref, dlb_ref):
        @pl.when(pl.program_id(0) == 0)
        def _():
            dws_ref[...] = jnp.zeros_like(dws_ref)
            dbs_ref[...] = jnp.zeros_like(dbs_ref)
            dlg_ref[...] = jnp.zeros_like(dlg_ref)
            dlb_ref[...] = jnp.zeros_like(dlb_ref)

        u, du_dz = _gelu_and_grad(z_ref[:, :w].astype(F32))
        lg = lg_ref[...]
        vn, vhat, rstd, dvv_dz = _gmlp_norm(z_ref[:, w:].astype(F32), lg, lb_ref[...])
        dav = da_ref[...].astype(F32)
        mask = _gmlp_mask()
        lane = lax.broadcasted_iota(jnp.int32, (GMLP_BLOCK, LANES), 1)
        dvn_parts = []
        dbs = jnp.zeros((GMLP_BLOCK, LANES), F32)
        for g in range(GMLP_GROUPS):
            sl = slice(g * gd, (g + 1) * gd)
            wm = jnp.where(mask, ws_ref[g], 0.0)
            vn_g = vn[:, sl]
            mixed = _dot(wm, vn_g, _NN) + bs_ref[:, g:g + 1]
            dmixed = dav[:, sl] * u[:, sl]
            dz_ref[:, sl] = (dav[:, sl] * mixed * du_dz[:, sl]).astype(dz_ref.dtype)
            dvn_parts.append(_dot(wm, dmixed, _TN))
            dws_ref[g] += jnp.where(mask, _dot(dmixed, vn_g, _NT), 0.0)
            dbs = dbs + jnp.where(lane == g, jnp.sum(dmixed, axis=-1, keepdims=True), 0.0)
        dbs_ref[...] += dbs
        dvn = jnp.concatenate(dvn_parts, axis=-1)
        dlg_ref[...] += jnp.sum(dvn * vhat, axis=0, keepdims=True)
        dlb_ref[...] += jnp.sum(dvn, axis=0, keepdims=True)
        dyg = dvn * lg
        dvv = rstd * (dyg - jnp.mean(dyg, axis=-1, keepdims=True)
                      - vhat * jnp.mean(dyg * vhat, axis=-1, keepdims=True))
        dz_ref[:, w:] = (dvv * dvv_dz).astype(dz_ref.dtype)

    full = lambda shape: pl.BlockSpec(shape, lambda i: (0,) * len(shape))
    return pl.pallas_call(
        body, name=name, grid=(s // GMLP_BLOCK,),
        in_specs=[pl.BlockSpec((GMLP_BLOCK, 2 * w), lambda i: (i, 0)),
                  pl.BlockSpec((GMLP_BLOCK, w), lambda i: (i, 0)), full((1, w)), full((1, w)),
                  full((GMLP_GROUPS, GMLP_BLOCK, GMLP_BLOCK)), full((GMLP_BLOCK, LANES))],
        out_specs=[pl.BlockSpec((GMLP_BLOCK, 2 * w), lambda i: (i, 0)),
                   full((GMLP_GROUPS, GMLP_BLOCK, GMLP_BLOCK)), full((GMLP_BLOCK, LANES)),
                   full((1, w)), full((1, w))],
        out_shape=[jax.ShapeDtypeStruct((s, 2 * w), BF16),
                   jax.ShapeDtypeStruct((GMLP_GROUPS, GMLP_BLOCK, GMLP_BLOCK), F32),
                   jax.ShapeDtypeStruct((GMLP_BLOCK, LANES), F32),
                   jax.ShapeDtypeStruct((1, w), F32), jax.ShapeDtypeStruct((1, w), F32)],
        compiler_params=_params(),
    )(z_uv, da, ln_g, ln_b, w_s, b_s_t)


def _tri(lower):
    r = lax.broadcasted_iota(jnp.int32, (ATT_BLOCK, ATT_BLOCK), 0)
    c = lax.broadcasted_iota(jnp.int32, (ATT_BLOCK, ATT_BLOCK), 1)
    return jnp.where((c <= r) if lower else (c >= r), 1.0, 0.0).astype(F32)


def _log_sigmoid(x):
    return jnp.minimum(x, 0.0) - jnp.log(1.0 + jnp.exp(-jnp.abs(x)))


def _fox_cum(f, b_f, *, name):
    s = f.shape[0]
    nb = s // ATT_BLOCK

    def body(f_ref, b_ref, cb_ref, ct_ref, carry):
        @pl.when(pl.program_id(0) == 0)
        def _():
            carry[...] = jnp.zeros_like(carry)

        lf = _log_sigmoid(f_ref[...] + b_ref[...])
        cum = lax.dot_general(_tri(True), lf, _NN, precision=lax.Precision.HIGHEST,
                              preferred_element_type=F32) + carry[...]
        carry[...] = cum[ATT_BLOCK - 1:ATT_BLOCK, :]
        for h in range(FOX_HEADS):
            cb_ref[h] = jnp.broadcast_to(cum[:, h:h + 1], (ATT_BLOCK, LANES))
        ct_ref[...] = cum.T

    return pl.pallas_call(
        body, name=name, grid=(nb,),
        in_specs=[pl.BlockSpec((ATT_BLOCK, LANES), lambda i: (i, 0)), pl.BlockSpec((1, LANES), lambda i: (0, 0))],
        out_specs=[pl.BlockSpec((FOX_HEADS, ATT_BLOCK, LANES), lambda i: (0, i, 0)),
                   pl.BlockSpec((LANES, ATT_BLOCK), lambda i: (0, i))],
        out_shape=[jax.ShapeDtypeStruct((FOX_HEADS, s, LANES), F32), jax.ShapeDtypeStruct((LANES, s), F32)],
        scratch_shapes=[pltpu.VMEM((1, LANES), F32)], compiler_params=_params(),
    )(f, b_f)


def _fox_dlogit(dcum_t, f, b_f, *, name):
    s = f.shape[0]
    nb = s // ATT_BLOCK

    def body(dc_ref, f_ref, b_ref, df_ref, db_ref, carry):
        @pl.when(pl.program_id(0) == 0)
        def _():
            carry[...] = jnp.zeros_like(carry)
            db_ref[...] = jnp.zeros_like(db_ref)

        d = dc_ref[...].T
        dlog = lax.dot_general(_tri(False), d, _NN, precision=lax.Precision.HIGHEST,
                               preferred_element_type=F32) + carry[...]
        carry[...] = dlog[0:1, :]
        df = dlog * (1.0 - _sigmoid(f_ref[...] + b_ref[...]))
        df_ref[...] = df
        db_ref[...] += jnp.sum(df, axis=0, keepdims=True)

    rev = lambda i: nb - 1 - i
    return pl.pallas_call(
        body, name=name, grid=(nb,),
        in_specs=[pl.BlockSpec((LANES, ATT_BLOCK), lambda i: (0, rev(i))),
                  pl.BlockSpec((ATT_BLOCK, LANES), lambda i: (rev(i), 0)),
                  pl.BlockSpec((1, LANES), lambda i: (0, 0))],
        out_specs=[pl.BlockSpec((ATT_BLOCK, LANES), lambda i: (rev(i), 0)),
                   pl.BlockSpec((1, LANES), lambda i: (0, 0))],
        out_shape=[jax.ShapeDtypeStruct((s, LANES), F32), jax.ShapeDtypeStruct((1, LANES), F32)],
        scratch_shapes=[pltpu.VMEM((1, LANES), F32)], compiler_params=_params(),
    )(dcum_t, f, b_f)


def _head_mask():
    return lax.broadcasted_iota(jnp.int32, (1, LANES), 1) < FOX_HEAD_DIM


ATT_TQ = 256
ATT_TK = 256
ATT_SCALE = FOX_HEAD_DIM ** -0.5
assert ATT_SCALE == 0.125 and ATT_TQ == ATT_TK


def _causal_t(qi, ki):
    kpos = lax.broadcasted_iota(jnp.int32, (ATT_TK, ATT_TQ), 0) + ki * ATT_TK
    qpos = lax.broadcasted_iota(jnp.int32, (ATT_TK, ATT_TQ), 1) + qi * ATT_TQ
    return kpos <= qpos


def _row_mask():
    return lax.broadcasted_iota(jnp.int32, (LANES, 1), 0) < FOX_HEAD_DIM


def _lane_tile(a, width):
    return a if a.shape[1] == width else jnp.tile(a, (1, width // a.shape[1]))


def _transpose_bf16(a):
    return a.astype(F32).T.astype(BF16)


def _attn_fwd_t(qkv, cum_b, cum_r, *, name):
    s = qkv.shape[0]
    nq = s // ATT_TQ
    npair = HEAD_PAIRS

    def body(q_ref, k_ref, v_ref, cq_ref, ck_ref, o_ref, ot_ref, l_ref, vt_ref):
        qi = pl.program_id(1)
        rows = _row_mask()

        @pl.when(qi == 0)
        def _():
            vt_ref[...] = _transpose_bf16(v_ref[...])

        qt = _transpose_bf16(q_ref[...]) * ATT_SCALE
        zero = jnp.zeros_like(qt)
        qts = (jnp.where(rows, qt, zero), jnp.where(rows, zero, qt))

        def step(ki, carry, masked):
            off = pl.multiple_of(ki * ATT_TK, ATT_TK)
            k2 = k_ref[pl.ds(off, ATT_TK), :]
            vt = vt_ref[:, pl.ds(off, ATT_TK)]
            out = []
            for hh in range(2):
                m, l, acc = carry[hh]
                bias = cq_ref[hh:hh + 1, :] - _lane_tile(ck_ref[hh, pl.ds(off, ATT_TK), :], ATT_TQ)
                sc = _dot(k2, qts[hh], _NN) + bias
                if masked:
                    sc = jnp.where(_causal_t(qi, ki), sc, -1e30)
                m_new = jnp.maximum(m, jnp.max(sc, axis=0, keepdims=True))
                alpha = jnp.exp(m - m_new)
                p = jnp.exp(sc - m_new)
                l = alpha * l + jnp.sum(p, axis=0, keepdims=True)
                p_hi = p.astype(BF16)
                p_lo = (p - p_hi.astype(F32)).astype(BF16)
                acc = alpha * acc + (_dot(vt, p_hi, _NN) + _dot(vt, p_lo, _NN))
                out.append((m_new, l, acc))
            return tuple(out)

        init = tuple((jnp.full((1, ATT_TQ), -1e30, F32), jnp.zeros((1, ATT_TQ), F32),
                      jnp.zeros((LANES, ATT_TQ), F32)) for _ in range(2))
        carry = lax.fori_loop(0, qi // 2, lambda kk, c: step(2 * kk + 1, step(2 * kk, c, False), False), init)
        carry = lax.cond(qi % 2 == 1, lambda c: step(qi - 1, c, False), lambda c: c, carry)
        (ma, la, acca), (mb, lb, accb) = step(qi, carry, True)
        ot = jnp.where(rows, acca / la, accb / lb)
        ot_ref[...] = ot
        o_ref[...] = ot.T.astype(o_ref.dtype)
        l_ref[0:1, :] = ma + jnp.log(la)
        l_ref[1:2, :] = mb + jnp.log(lb)

    row = pl.BlockSpec((None, 2, ATT_TQ), lambda j, i: (j, 0, i))
    return pl.pallas_call(
        body, name=name, grid=(npair, nq),
        in_specs=[pl.BlockSpec((ATT_TQ, LANES), lambda j, i: (i, j)),
                  pl.BlockSpec((s, LANES), lambda j, i: (0, npair + j)),
                  pl.BlockSpec((s, LANES), lambda j, i: (0, 2 * npair + j)),
                  row, pl.BlockSpec((None, 2, s, LANES), lambda j, i: (j, 0, 0, 0))],
        out_specs=[pl.BlockSpec((ATT_TQ, LANES), lambda j, i: (i, j)),
                   pl.BlockSpec((LANES, ATT_TQ), lambda j, i: (j, i)), row],
        out_shape=[jax.ShapeDtypeStruct((s, FOX_WIDTH), BF16), jax.ShapeDtypeStruct((FOX_WIDTH, s), F32),
                   jax.ShapeDtypeStruct((npair, 2, s), F32)],
        scratch_shapes=[pltpu.VMEM((LANES, s), BF16)],
        compiler_params=_params(),
    )(qkv, qkv, qkv, cum_r, cum_b)


def _attn_bwd_t(qkv, do, o_t, lse, cum_b, cum_r, *, name, dep=None):
    s = qkv.shape[0]
    nq = s // ATT_TQ
    npair = HEAD_PAIRS

    deps = [] if dep is None else [dep]

    def body(q_ref, k_ref, v_ref, do_ref, ot_ref, l_ref, cq_ref, ck_ref, *rest):
        dq_ref, dk_ref, dv_ref, dc_ref, qt_ref, dot_ref, dqt_ref, dl_ref = rest[len(deps):]
        ki = pl.program_id(1)
        m0 = _head_mask()
        rows = _row_mask()
        k2 = k_ref[...]
        v2 = v_ref[...]
        kt = _transpose_bf16(k2)
        ks = k2 * ATT_SCALE
        kz, vz = jnp.zeros_like(k2), jnp.zeros_like(v2)
        khs = (jnp.where(m0, ks, kz), jnp.where(m0, kz, ks))
        vhs = (jnp.where(m0, v2, vz), jnp.where(m0, vz, v2))
        cks = tuple(_lane_tile(ck_ref[hh], ATT_TQ) for hh in range(2))

        @pl.when(ki == 0)
        def _():
            dqt_ref[...] = jnp.zeros_like(dqt_ref)
            qt_ref[...] = _transpose_bf16(q_ref[...])
            do_t = do_ref[...].astype(F32).T
            dot_ref[...] = do_t.astype(BF16)
            prod = do_t * ot_ref[...]
            dl_ref[0:1, :] = jnp.sum(prod[:FOX_HEAD_DIM], axis=0, keepdims=True)
            dl_ref[1:2, :] = jnp.sum(prod[FOX_HEAD_DIM:], axis=0, keepdims=True)

        def step(qi, carry, masked):
            off = pl.multiple_of(qi * ATT_TQ, ATT_TQ)
            q2 = q_ref[pl.ds(off, ATT_TQ), :]
            do2 = do_ref[pl.ds(off, ATT_TQ), :]
            qt = qt_ref[:, pl.ds(off, ATT_TQ)]
            dot_ = dot_ref[:, pl.ds(off, ATT_TQ)]
            out, dqs = [], []
            for hh in range(2):
                dk_acc, dv_acc, dc_acc = carry[hh]
                sc = _dot(khs[hh], qt, _NN) + (cq_ref[hh:hh + 1, pl.ds(off, ATT_TQ)] - cks[hh])
                p = jnp.exp(sc - l_ref[hh:hh + 1, pl.ds(off, ATT_TQ)])
                if masked:
                    p = jnp.where(_causal_t(qi, ki), p, 0.0)
                dp = _dot(vhs[hh], dot_, _NN)
                ds = p * (dp - dl_ref[hh:hh + 1, pl.ds(off, ATT_TQ)])
                dc_acc = dc_acc - jnp.sum(ds, axis=1, keepdims=True)
                dss = (ds * ATT_SCALE).astype(BF16)
                dv_acc = dv_acc + _dot(p, do2, _NN)
                dk_acc = dk_acc + _dot(dss, q2, _NN)
                dqs.append(_dot(kt, dss, _NN))
                out.append((dk_acc, dv_acc, dc_acc))
            dqt_ref[:, pl.ds(off, ATT_TQ)] += jnp.where(rows, dqs[0], dqs[1])
            return tuple(out)

        init = tuple((jnp.zeros((ATT_TK, LANES), F32), jnp.zeros((ATT_TK, LANES), F32),
                      jnp.zeros((ATT_TK, 1), F32)) for _ in range(2))
        carry = step(ki, init, True)
        rest = nq - 1 - ki
        carry = lax.fori_loop(
            0, rest // 2, lambda t, c: step(ki + 2 + 2 * t, step(ki + 1 + 2 * t, c, False), False), carry)
        carry = lax.cond(rest % 2 == 1, lambda c: step(nq - 1, c, False), lambda c: c, carry)
        (dka, dva, dca), (dkb, dvb, dcb) = carry
        dk_ref[...] = jnp.where(m0, dka, dkb).astype(dk_ref.dtype)
        dv_ref[...] = jnp.where(m0, dva, dvb).astype(dv_ref.dtype)
        dc_ref[0] = jnp.broadcast_to(dca, (ATT_TK, LANES))
        dc_ref[1] = jnp.broadcast_to(dcb, (ATT_TK, LANES))

        @pl.when(ki == nq - 1)
        def _():
            dq_ref[...] = dqt_ref[...].T.astype(dq_ref.dtype)

    colfull = lambda base: pl.BlockSpec((s, LANES), lambda j, i: (0, base + j))
    colblk = lambda base: pl.BlockSpec((ATT_TK, LANES), lambda j, i: (i, base + j))
    stat = pl.BlockSpec((None, 2, s), lambda j, i: (j, 0, 0))
    bcast = pl.BlockSpec((None, 2, ATT_TK, LANES), lambda j, i: (j, 0, i, 0))
    grad = jax.ShapeDtypeStruct((s, FOX_WIDTH), BF16)
    return pl.pallas_call(
        body, name=name, grid=(npair, nq),
        in_specs=[colfull(0), colblk(npair), colblk(2 * npair), colfull(0),
                  pl.BlockSpec((LANES, s), lambda j, i: (j, 0)), stat, stat, bcast]
                 + [pl.BlockSpec(memory_space=pl.ANY)] * len(deps),
        out_specs=[colfull(0), colblk(0), colblk(0), bcast],
        out_shape=[grad, grad, grad, jax.ShapeDtypeStruct((npair, 2, s, LANES), F32)],
        scratch_shapes=[pltpu.VMEM((LANES, s), BF16), pltpu.VMEM((LANES, s), BF16), pltpu.VMEM((LANES, s), F32),
                        pltpu.VMEM((2, s), F32)],
        compiler_params=_params(),
    )(qkv, qkv, qkv, do, o_t, lse, cum_r, cum_b, *deps)


def _merge_fwd(zg, ya, yb, *, name, tm=512):
    s, d = ya.shape
    tm = _tile(s, tm)

    def body(zg_ref, ya_ref, yb_ref, m_ref):
        ga = _sigmoid(zg_ref[:, :d].astype(F32))
        gb = _sigmoid(zg_ref[:, d:].astype(F32))
        m_ref[...] = (ga * ya_ref[...].astype(F32) + gb * yb_ref[...].astype(F32)).astype(m_ref.dtype)

    row = pl.BlockSpec((tm, d), lambda i: (i, 0))
    row2 = pl.BlockSpec((tm, 2 * d), lambda i: (i, 0))
    return pl.pallas_call(
        body, name=name, grid=(s // tm,), in_specs=[row2, row, row], out_specs=row,
        out_shape=jax.ShapeDtypeStruct((s, d), BF16), compiler_params=_params(),
    )(zg, ya, yb)


def _merge_bwd(dm, zg, ya, yb, *, name, tm=512):
    s, d = ya.shape
    tm = _tile(s, tm)

    def body(dm_ref, zg_ref, ya_ref, yb_ref, dzg_ref, dya_ref, dyb_ref):
        dmv = dm_ref[...].astype(F32)
        ga = _sigmoid(zg_ref[:, :d].astype(F32))
        gb = _sigmoid(zg_ref[:, d:].astype(F32))
        dzg_ref[:, :d] = (dmv * ya_ref[...].astype(F32) * ga * (1.0 - ga)).astype(dzg_ref.dtype)
        dzg_ref[:, d:] = (dmv * yb_ref[...].astype(F32) * gb * (1.0 - gb)).astype(dzg_ref.dtype)
        dya_ref[...] = (dmv * ga).astype(dya_ref.dtype)
        dyb_ref[...] = (dmv * gb).astype(dyb_ref.dtype)

    row = pl.BlockSpec((tm, d), lambda i: (i, 0))
    row2 = pl.BlockSpec((tm, 2 * d), lambda i: (i, 0))
    return pl.pallas_call(
        body, name=name, grid=(s // tm,), in_specs=[row, row2, row, row], out_specs=[row2, row, row],
        out_shape=[jax.ShapeDtypeStruct((s, 2 * d), BF16), jax.ShapeDtypeStruct((s, d), BF16),
                   jax.ShapeDtypeStruct((s, d), BF16)],
        compiler_params=_params(),
    )(dm, zg, ya, yb)


SUBLANES = 8


def _shift_down(u, k, row):
    rolled = pltpu.roll(u, k, 0)
    head = jnp.where(row[:SUBLANES] >= k, rolled[:SUBLANES], 0.0)
    return jnp.concatenate([head, rolled[SUBLANES:]], axis=0)


def _shift_up(u, k, row):
    n = u.shape[0]
    rolled = pltpu.roll(u, n - k, 0)
    tail = jnp.where(row[n - SUBLANES:] < n - k, rolled[n - SUBLANES:], 0.0)
    return jnp.concatenate([rolled[:n - SUBLANES], tail], axis=0)


def _conv_act_fwd(up_a, up_b, cw_a, cw_b, cb_a, cb_b, *, name, tc=128):
    s, f = up_a.shape
    tc = _tile(f, tc)

    def body(ua_ref, ub_ref, wa_ref, wb_ref, ba_ref, bb_ref, act_ref):
        row = lax.broadcasted_iota(jnp.int32, (s, tc), 0)

        def conv(u_ref, w_ref, b_ref):
            u = u_ref[...].astype(F32)
            return (b_ref[...] + w_ref[0:1, :] * _shift_down(u, 2, row)
                    + w_ref[1:2, :] * _shift_down(u, 1, row) + w_ref[2:3, :] * u)

        ca = conv(ua_ref, wa_ref, ba_ref)
        cb = conv(ub_ref, wb_ref, bb_ref)
        act_ref[...] = (_gelu(ca) * cb).astype(act_ref.dtype)

    col = pl.BlockSpec((s, tc), lambda j: (0, j))
    w3 = pl.BlockSpec((3, tc), lambda j: (0, j))
    b1 = pl.BlockSpec((1, tc), lambda j: (0, j))
    return pl.pallas_call(
        body, name=name, grid=(f // tc,), in_specs=[col, col, w3, w3, b1, b1], out_specs=col,
        out_shape=jax.ShapeDtypeStruct((s, f), BF16), compiler_params=_params(),
    )(up_a, up_b, cw_a, cw_b, cb_a, cb_b)


def _conv_act_bwd(up_a, up_b, dact, cw_a, cw_b, cb_a, cb_b, *, name, tc=128):
    s, f = up_a.shape
    tc = _tile(f, tc)

    def body(ua_ref, ub_ref, da_ref, wa_ref, wb_ref, ba_ref, bb_ref, dua_ref, dub_ref, dwa_ref, dwb_ref):
        row = lax.broadcasted_iota(jnp.int32, (s, tc), 0)

        def conv(u_ref, w_ref, b_ref):
            u = u_ref[...].astype(F32)
            u1 = _shift_down(u, 1, row)
            u2 = _shift_down(u, 2, row)
            return u, u1, u2, b_ref[...] + w_ref[0:1, :] * u2 + w_ref[1:2, :] * u1 + w_ref[2:3, :] * u

        def back(dc, taps, w_ref, du_ref, dw_ref):
            u, u1, u2 = taps
            dw_ref[0:1, :] = jnp.sum(dc * u2, axis=0, keepdims=True)
            dw_ref[1:2, :] = jnp.sum(dc * u1, axis=0, keepdims=True)
            dw_ref[2:3, :] = jnp.sum(dc * u, axis=0, keepdims=True)
            dw_ref[3:4, :] = jnp.sum(dc, axis=0, keepdims=True)
            du = (w_ref[2:3, :] * dc + w_ref[1:2, :] * _shift_up(dc, 1, row)
                  + w_ref[0:1, :] * _shift_up(dc, 2, row))
            du_ref[...] = du.astype(du_ref.dtype)

        ua, ua1, ua2, ca = conv(ua_ref, wa_ref, ba_ref)
        ub, ub1, ub2, cb = conv(ub_ref, wb_ref, bb_ref)
        g, dg = _gelu_and_grad(ca)
        dact_v = da_ref[...].astype(F32)
        back(dact_v * cb * dg, (ua, ua1, ua2), wa_ref, dua_ref, dwa_ref)
        back(dact_v * g, (ub, ub1, ub2), wb_ref, dub_ref, dwb_ref)

    col = pl.BlockSpec((s, tc), lambda j: (0, j))
    w3 = pl.BlockSpec((3, tc), lambda j: (0, j))
    w4 = pl.BlockSpec((4, tc), lambda j: (0, j))
    b1 = pl.BlockSpec((1, tc), lambda j: (0, j))
    return pl.pallas_call(
        body, name=name, grid=(f // tc,), in_specs=[col, col, col, w3, w3, b1, b1],
        out_specs=[col, col, w4, w4],
        out_shape=[jax.ShapeDtypeStruct((s, f), BF16), jax.ShapeDtypeStruct((s, f), BF16),
                   jax.ShapeDtypeStruct((4, f), F32), jax.ShapeDtypeStruct((4, f), F32)],
        compiler_params=_params(),
    )(up_a, up_b, dact, cw_a, cw_b, cb_a, cb_b)


def _ple_final(x2, ple, zp, target, g_final, *, name, tm=512):
    s, d = x2.shape
    tm = _tile(s, tm)

    def body(x_ref, ple_ref, zp_ref, t_ref, g_ref, dx_ref, dple_ref, dzp_ref, dg_ref, loss_ref):
        @pl.when(pl.program_id(0) == 0)
        def _():
            dg_ref[...] = jnp.zeros_like(dg_ref)
            loss_ref[...] = jnp.zeros_like(loss_ref)

        gp = _sigmoid(zp_ref[...].astype(F32))
        plev = ple_ref[...].astype(F32)
        x3 = x_ref[...] + plev * gp
        r = lax.rsqrt(jnp.mean(x3 * x3, axis=-1, keepdims=True) + EPS)
        xhat = x3 * r
        gv = g_ref[...]
        diff = xhat * gv - t_ref[...]
        loss_ref[...] += 0.5 * jnp.sum(jnp.mean(diff * diff, axis=-1, keepdims=True), axis=0, keepdims=True)
        dy = diff * (1.0 / d)
        dg_ref[...] += jnp.sum(dy * xhat, axis=0, keepdims=True)
        dyg = dy * gv
        dx3 = r * (dyg - xhat * jnp.mean(dyg * xhat, axis=-1, keepdims=True))
        dx_ref[...] = dx3
        dple_ref[...] = (dx3 * gp).astype(dple_ref.dtype)
        dzp_ref[...] = (dx3 * plev * gp * (1.0 - gp)).astype(dzp_ref.dtype)

    row = pl.BlockSpec((tm, d), lambda i: (i, 0))
    vec = pl.BlockSpec((1, d), lambda i: (0, 0))
    return pl.pallas_call(
        body, name=name, grid=(s // tm,), in_specs=[row, row, row, row, vec],
        out_specs=[row, row, row, vec, pl.BlockSpec((1, LANES), lambda i: (0, 0))],
        out_shape=[jax.ShapeDtypeStruct((s, d), F32), jax.ShapeDtypeStruct((s, d), BF16),
                   jax.ShapeDtypeStruct((s, d), BF16), jax.ShapeDtypeStruct((1, d), F32),
                   jax.ShapeDtypeStruct((1, LANES), F32)],
        compiler_params=_params(),
    )(x2, ple, zp, target, g_final)


def _device_step(x, p, target, w, get_w_in=None, get_w_rest=None, on_grads_ffn=None, on_grads_small=None,
                 on_grads_mix=None, on_after_dh=None):
    s = x.shape[0]
    g = {}
    w = dict(w)

    h = _rms_fwd(x, w["norm_mix_g"], name="rms_mix", dep=w.get("first_dep"))
    if get_w_in is not None:
        w.update(get_w_in(h))
    qkv = _mm(h, w["w_qkv"], mode="nn", out_dtype=BF16, name="proj_qkv", tm=1024)
    f = _mm(h, w["w_f"], mode="nn", out_dtype=F32, name="proj_f", tm=1024)

    cum_b, cum_t = _fox_cum(f, w["b_f"], name="fox_cum")
    cum_b = cum_b.reshape(HEAD_PAIRS, 2, s, LANES)
    cum_r = cum_t[:FOX_HEADS].reshape(HEAD_PAIRS, 2, s)
    b, o_t, lse = _attn_fwd_t(qkv, cum_b, cum_r, name="attn_fwd")

    dep = get_w_rest[0](b) if get_w_rest is not None else None
    z_uv = _mm(h, w["w_uv"], mode="nn", out_dtype=BF16, name="proj_uv", tm=1024, dep=dep)
    zg = _mm(h, w["w_g"], mode="nn", out_dtype=BF16, name="proj_gate", tm=1024, dep=dep)
    a = _gmlp_fwd(z_uv, w["gmlp_ln_g"], w["gmlp_ln_b"], w["gmlp_w_s"], w["gmlp_b_s_t"], name="gmlp_fwd")
    if get_w_rest is not None:
        w.update(get_w_rest[1]([a, zg]))

    ya = _mm(a, w["w_branch_a"], mode="nn", out_dtype=BF16, name="branch_a", tm=1024)
    yb = _mm(b, w["w_branch_b"], mode="nn", out_dtype=BF16, name="branch_b", tm=1024)
    merged = _merge_fwd(zg, ya, yb, name="merge_fwd")
    x1 = _mm(merged, w["w_out"], mode="nn", out_dtype=F32, name="proj_out", add=x, tm=1024)

    h2 = _rms_fwd(x1, w["norm_ffn_g"], name="rms_ffn")
    up_a = _mm(h2, w["w_up_a"], mode="nn", out_dtype=BF16, name="up_a", tm=1024, tn=D_FF // 2)
    up_b = _mm(h2, w["w_up_b"], mode="nn", out_dtype=BF16, name="up_b", tm=1024, tn=D_FF // 2)
    cw, cb = w["conv_w"], w["conv_b"]
    conv_args = (cw[:, :D_FF], cw[:, D_FF:], cb[:, :D_FF], cb[:, D_FF:])
    act = _conv_act_fwd(up_a, up_b, *conv_args, name="conv_act_fwd")
    x2 = _mm(act, w["w_down"], mode="nn", out_dtype=F32, name="down", add=x1, tm=512)

    h3 = _rms_fwd(x2, w["norm_ple_g"], name="rms_ple")
    ple = _mm(p, w["w_ple"], mode="nn", out_dtype=BF16, name="ple_proj", tm=1024)
    zp = _mm(h3, w["w_ple_gate"], mode="nn", out_dtype=BF16, name="ple_gate", tm=1024)
    dx3, dple, dzp, g["norm_final_g"], loss = _ple_final(x2, ple, zp, target, w["norm_final_g"], name="ple_final")

    g["w_ple"] = _mm(p, dple, mode="tn", out_dtype=BF16, name="dw_ple")
    g["w_ple_gate"] = _mm(h3, dzp, mode="tn", out_dtype=BF16, name="dw_ple_gate")
    dh3 = _mm(dzp, w["w_ple_gate"], mode="nt", out_dtype=BF16, name="dh3")
    dx2, dx2_b, g["norm_ple_g"] = _rms_bwd(x2, w["norm_ple_g"], dh3, dx3, name="rms_ple_bwd")

    g["w_down"] = _mm(act, dx2_b, mode="tn", out_dtype=BF16, name="dw_down", tm=D_FF // 2)
    dact = _mm(dx2_b, w["w_down"], mode="nt", out_dtype=BF16, name="dact", tn=D_FF // 2)
    dup_a, dup_b, dcw_a, dcw_b = _conv_act_bwd(up_a, up_b, dact, *conv_args, name="conv_act_bwd")
    g["conv_w"] = jnp.concatenate([dcw_a[:3], dcw_b[:3]], axis=1)
    g["conv_b"] = jnp.concatenate([dcw_a[3:], dcw_b[3:]], axis=1)
    g["w_up_a"] = _mm(h2, dup_a, mode="tn", out_dtype=BF16, name="dw_up_a", tn=D_FF // 2)
    g["w_up_b"] = _mm(h2, dup_b, mode="tn", out_dtype=BF16, name="dw_up_b", tn=D_FF // 2)
    dh2 = _mm_nt_sum([(dup_a, w["w_up_a"]), (dup_b, w["w_up_b"])], out_dtype=BF16, name="dh2")
    dx1, dx1_b, g["norm_ffn_g"] = _rms_bwd(x1, w["norm_ffn_g"], dh2, dx2, name="rms_ffn_bwd")

    g["w_out"] = _mm(merged, dx1_b, mode="tn", out_dtype=BF16, name="dw_out")
    dmerged = _mm(dx1_b, w["w_out"], mode="nt", out_dtype=BF16, name="dmerged")
    dzg, dya, dyb = _merge_bwd(dmerged, zg, ya, yb, name="merge_bwd")
    g["w_branch_a"] = _mm(a, dya, mode="tn", out_dtype=BF16, name="dw_branch_a")
    g["w_branch_b"] = _mm(b, dyb, mode="tn", out_dtype=BF16, name="dw_branch_b")
    dep = on_grads_ffn(g) if on_grads_ffn is not None else None
    da = _mm(dya, w["w_branch_a"], mode="nt", out_dtype=BF16, name="da", dep=dep)
    db = _mm(dyb, w["w_branch_b"], mode="nt", out_dtype=BF16, name="db")

    dz_uv, g["gmlp_w_s"], dbs_t, g["gmlp_ln_g"], g["gmlp_ln_b"] = _gmlp_bwd(
        z_uv, da, w["gmlp_ln_g"], w["gmlp_ln_b"], w["gmlp_w_s"], w["gmlp_b_s_t"], name="gmlp_bwd")
    g["gmlp_b_s"] = dbs_t[:, :GMLP_GROUPS].T
    dep = on_grads_small(g) if on_grads_small is not None else None

    dq, dk, dv, dcum_b = _attn_bwd_t(qkv, db, o_t, lse, cum_b, cum_r, name="attn_bwd", dep=dep)
    dcum_t = jnp.pad(dcum_b[..., 0].reshape(FOX_HEADS, s), ((0, LANES - FOX_HEADS), (0, 0)))
    df, g["b_f"] = _fox_dlogit(dcum_t, f, w["b_f"], name="fox_dlogit")
    dqkv = jnp.concatenate([dq, dk, dv], axis=1)

    g["w_uv"] = _mm(h, dz_uv, mode="tn", out_dtype=BF16, name="dw_uv")
    g["w_qkv"] = _mm(h, dqkv, mode="tn", out_dtype=BF16, name="dw_qkv")
    g["w_f"] = _mm(h, df, mode="tn", out_dtype=BF16, name="dw_f")
    g["w_g"] = _mm(h, dzg, mode="tn", out_dtype=BF16, name="dw_g")
    dep = on_grads_mix(g) if on_grads_mix is not None else None
    dh = _mm_nt_sum([(dz_uv, w["w_uv"]), (dqkv, w["w_qkv"]), (df, w["w_f"]), (dzg, w["w_g"])],
                    out_dtype=BF16, name="dh", dep=dep)
    dep = on_after_dh(dh) if on_after_dh is not None else None
    dx0, _, g["norm_mix_g"] = _rms_bwd(x, w["norm_mix_g"], dh, dx1, name="rms_mix_bwd", dep=dep)
    return loss, dx0, g


def _coords():
    return lax.axis_index("x"), lax.axis_index("y"), lax.axis_index("c")


def _other_chips(x, y):
    return [(1 - x, y), (x, 1 - y), (1 - x, 1 - y)]


def _remote(src, dst, send_sem, recv_sem, dev):
    return pltpu.make_async_remote_copy(src_ref=src, dst_ref=dst, send_sem=send_sem, recv_sem=recv_sem,
                                        device_id=dev, device_id_type=MESH)


_ANY = pl.BlockSpec(memory_space=pl.ANY)


def _pair_exchange(gs, *, name):
    n = len(gs)

    def body(*refs):
        ins, outs = refs[:n], refs[n:2 * n]
        send_sems, recv_sems = refs[2 * n:]
        x, y, c = _coords()
        copies = []
        for i in range(n):
            for j in range(N_CHIPS):
                cp = _remote(ins[i].at[j, 1 - c], outs[i].at[j], send_sems.at[i, j], recv_sems.at[i, j], (x, y, 1 - c))
                cp.start()
                copies.append(cp)
        for cp in copies:
            cp.wait()

    return pl.pallas_call(
        body, name=name, in_specs=[_ANY] * n, out_specs=[_ANY] * n,
        out_shape=[jax.ShapeDtypeStruct((N_CHIPS,) + a.shape[2:], a.dtype) for a in gs],
        scratch_shapes=[pltpu.SemaphoreType.DMA((n, N_CHIPS)), pltpu.SemaphoreType.DMA((n, N_CHIPS))],
        compiler_params=_params(),
    )(*gs)


def _pair_share(hs, *, name):
    n = len(hs)

    def body(*refs):
        ins, outs = refs[:n], refs[n:2 * n]
        send_sems, recv_sems = refs[2 * n:]
        x, y, c = _coords()
        copies = []
        for i in range(n):
            cp = _remote(ins[i], outs[i], send_sems.at[i], recv_sems.at[i], (x, y, 1 - c))
            cp.start()
            copies.append(cp)
        for cp in copies:
            cp.wait()

    return pl.pallas_call(
        body, name=name, in_specs=[_ANY] * n, out_specs=[_ANY] * n,
        out_shape=[jax.ShapeDtypeStruct(a.shape, a.dtype) for a in hs],
        scratch_shapes=[pltpu.SemaphoreType.DMA((n,)), pltpu.SemaphoreType.DMA((n,))],
        compiler_params=_params(),
    )(*hs)


def _all_exchange(vec, *, name):
    def body(v_ref, o_ref, send_sems, recv_sems, local_sem):
        x, y, c = _coords()
        me = 4 * x + 2 * y + c
        local = pltpu.make_async_copy(v_ref, o_ref.at[me], local_sem)
        local.start()
        copies = []
        k = 0
        for dx in (0, 1):
            for dy in (0, 1):
                for dc in (0, 1):
                    if dx or dy or dc:
                        peer = (1 - x if dx else x, 1 - y if dy else y, 1 - c if dc else c)
                        cp = _remote(v_ref, o_ref.at[me], send_sems.at[k], recv_sems.at[k], peer)
                        cp.start()
                        copies.append(cp)
                        k += 1
        for cp in copies:
            cp.wait()
        local.wait()

    return pl.pallas_call(
        body, name=name, in_specs=[_ANY], out_specs=_ANY,
        out_shape=jax.ShapeDtypeStruct((8,) + vec.shape, vec.dtype),
        scratch_shapes=[pltpu.SemaphoreType.DMA((7,)), pltpu.SemaphoreType.DMA((7,)), pltpu.SemaphoreType.DMA(())],
        compiler_params=_params(),
    )(vec)


_HBM = pl.BlockSpec(memory_space=pltpu.HBM)
_SEM = pl.BlockSpec(memory_space=pltpu.SEMAPHORE)
_EFFECT = pltpu.SideEffectType.DATAFLOW_SIDE_EFFECTING


def _copies_start(srcs, lands, plan, n_copies, *, name, after=()):
    ns, n = len(srcs), len(srcs) + len(lands)
    na = len(after)

    def body(*refs):
        send_sems, recv_sems = refs[n + na], refs[n + na + 1]
        token = refs[-1]
        for k, (src, dst, dev) in enumerate(plan(refs[:ns], refs[ns:n])):
            _remote(src, dst, send_sems.at[k], recv_sems.at[k], dev).start()
        token[...] = jnp.zeros_like(token)

    arrays = list(srcs) + list(lands)
    outs = pl.pallas_call(
        body, name=name,
        out_shape=(pltpu.SemaphoreType.DMA((n_copies,)), pltpu.SemaphoreType.DMA((n_copies,)),
                   *[pltpu.HBM(a.shape, a.dtype) for a in arrays], jax.ShapeDtypeStruct((8, LANES), F32)),
        in_specs=[_HBM] * n + [_ANY] * na,
        out_specs=(_SEM, _SEM, *[_HBM] * n, pl.BlockSpec(memory_space=pltpu.VMEM)),
        input_output_aliases={i: 2 + i for i in range(n)},
        compiler_params=pltpu.CompilerParams(has_side_effects=_EFFECT),
    )(*[pltpu.with_memory_space_constraint(a, pltpu.HBM) for a in arrays], *after)
    return outs[0], outs[1], list(outs[2:2 + ns]), list(outs[2 + ns:2 + n]), outs[-1]


def _copies_wait(send_sems, recv_sems, srcs, lands, plan, first, after, *, name):
    ns, n = len(srcs), len(srcs) + len(lands)

    def body(*refs):
        send, recv = refs[n], refs[n + 1]
        for k, (src, dst, dev) in enumerate(plan(refs[:ns], refs[ns:n])):
            cp = _remote(src, dst, send.at[first + k], recv.at[first + k], dev)
            cp.wait_send()
            cp.wait_recv()

    arrays = list(srcs) + list(lands)
    outs = pl.pallas_call(
        body, name=name, out_shape=tuple(pltpu.HBM(a.shape, a.dtype) for a in arrays),
        in_specs=[_HBM] * n + [_SEM, _SEM] + [_ANY] * len(after), out_specs=tuple([_HBM] * n),
        input_output_aliases={i: i for i in range(n)},
        compiler_params=pltpu.CompilerParams(has_side_effects=_EFFECT),
    )(*arrays, send_sems, recv_sems, *after)
    return list(outs[:ns]), list(outs[ns:])


def _gather_plan(halved):
    def plan(srcs, lands):
        x, y, c = _coords()
        me = 2 * x + y
        out = []
        for i, (src, land) in enumerate(zip(srcs, lands)):
            if halved[i]:
                h = src.shape[0] // 2
                rows = pl.ds(pl.multiple_of(c * h, 16), h)
                src, dst = src.at[rows], land.at[me, rows]
            else:
                dst = land.at[me]
            out += [(src, dst, (cx, cy, c)) for cx, cy in _other_chips(x, y)]
        return out
    return plan


def _forward_halves(lands, *, name):
    n = len(lands)

    def body(*refs):
        ins, outs = refs[:n], refs[n:2 * n]
        send_sems, recv_sems = refs[2 * n:]
        x, y, c = _coords()
        copies = []
        for i in range(n):
            h = ins[i].shape[1] // 2
            rows = pl.ds(pl.multiple_of(c * h, 16), h)
            for k, (cx, cy) in enumerate(_other_chips(x, y)):
                cp = _remote(ins[i].at[2 * cx + cy, rows], outs[i].at[2 * cx + cy, rows],
                             send_sems.at[i, k], recv_sems.at[i, k], (x, y, 1 - c))
                cp.start()
                copies.append(cp)
        for cp in copies:
            cp.wait()

    return pl.pallas_call(
        body, name=name, in_specs=[_ANY] * n, out_specs=[_ANY] * n,
        out_shape=[jax.ShapeDtypeStruct(a.shape, a.dtype) for a in lands],
        input_output_aliases={i: i for i in range(n)},
        scratch_shapes=[pltpu.SemaphoreType.DMA((n, 3)), pltpu.SemaphoreType.DMA((n, 3))],
        compiler_params=_params(),
    )(*lands)


def _forward_plan(srcs, lands):
    x, y, c = _coords()
    out = []
    for land in lands:
        h = land.shape[1] // 2
        rows = pl.ds(pl.multiple_of(c * h, 16), h)
        for cx, cy in _other_chips(x, y):
            view = land.at[2 * cx + cy, rows]
            out.append((view, view, (x, y, 1 - c)))
    return out


def _share_plan(srcs, lands):
    x, y, c = _coords()
    return [(src, land, (x, y, 1 - c)) for src, land in zip(srcs, lands)]


def _pair_plan(srcs, lands):
    x, y, c = _coords()
    out = []
    for src, land in zip(srcs, lands):
        out += [(src.at[j, 1 - c], land.at[j], (x, y, 1 - c)) for j in range(N_CHIPS)]
    return out


def _all_plan(srcs, lands):
    x, y, c = _coords()
    me = 4 * x + 2 * y + c
    out = []
    for src, land in zip(srcs, lands):
        for dx in (0, 1):
            for dy in (0, 1):
                for dc in (0, 1):
                    if dx or dy or dc:
                        out.append((src, land.at[me], (1 - x if dx else x, 1 - y if dy else y, 1 - c if dc else c)))
    return out


def _chip_plan(srcs, lands):
    x, y, c = _coords()
    me = 2 * x + y
    out = []
    for src, land in zip(srcs, lands):
        out += [(src.at[2 * cx + cy], land.at[me], (cx, cy, c)) for cx, cy in _other_chips(x, y)]
    return out


ROW_BLOCK_BYTES = 2 * 1024 * 1024


def _rtile(r, pref, mult, row_bytes=None):
    if row_bytes is not None:
        pref = max(pref, ROW_BLOCK_BYTES // row_bytes)
    t = (min(r, pref) // mult) * mult
    while t >= mult:
        if r % t == 0:
            return t
        t -= mult
    return r


def _pair_add(g, recv, core, *, name):
    _, _, r2, cols = g.shape
    tr = _rtile(r2, 256, 16, row_bytes=2 * cols)

    def body(c_ref, g_ref, r_ref, o_ref):
        o_ref[...] = (g_ref[...].astype(F32) + r_ref[...].astype(F32)).astype(o_ref.dtype)

    blk = pl.BlockSpec((None, tr, cols), lambda j, i, c_ref: (j, i, 0))
    return pl.pallas_call(
        body, name=name,
        grid_spec=pltpu.PrefetchScalarGridSpec(
            num_scalar_prefetch=1, grid=(N_CHIPS, r2 // tr),
            in_specs=[pl.BlockSpec((None, None, tr, cols), lambda j, i, c_ref: (j, c_ref[0], i, 0)), blk],
            out_specs=blk),
        out_shape=jax.ShapeDtypeStruct(recv.shape, recv.dtype), compiler_params=_params(),
    )(core, g, recv)


def _sum_slots(a, out_dtype, *, name):
    n, r, cols = a.shape
    whole = n * r * cols * a.dtype.itemsize <= 4 * ROW_BLOCK_BYTES
    tr = r if whole else _rtile(r, 256, 16)

    def body(a_ref, o_ref):
        acc = a_ref[0].astype(F32)
        for j in range(1, n):
            acc = acc + a_ref[j].astype(F32)
        o_ref[...] = acc.astype(o_ref.dtype)

    return pl.pallas_call(
        body, name=name, grid=(r // tr,),
        in_specs=[pl.BlockSpec((n, tr, cols), lambda i: (0, i, 0))],
        out_specs=pl.BlockSpec((tr, cols), lambda i: (i, 0)),
        out_shape=jax.ShapeDtypeStruct((r, cols), out_dtype), compiler_params=_params(),
    )(a)


def _chip_sum(own, recv, chip, *, name):
    _, r2, cols = own.shape
    tr = _rtile(r2, 256, 16, row_bytes=2 * cols)

    def body(chip_ref, own_ref, *rest):
        o_ref = rest[-1]
        acc = None
        for j in range(N_CHIPS):
            term = jnp.where(chip_ref[0] == j, own_ref[...], rest[j][...]).astype(F32)
            acc = term if acc is None else acc + term
        o_ref[...] = acc

    def slot(j):
        return pl.BlockSpec((None, tr, cols),
                            lambda i, chip_ref: (jnp.where(chip_ref[0] == j, (j + 1) % N_CHIPS, j), i, 0))

    return pl.pallas_call(
        body, name=name,
        grid_spec=pltpu.PrefetchScalarGridSpec(
            num_scalar_prefetch=1, grid=(r2 // tr,),
            in_specs=[pl.BlockSpec((None, tr, cols), lambda i, chip_ref: (chip_ref[0], i, 0))]
                     + [slot(j) for j in range(N_CHIPS)],
            out_specs=pl.BlockSpec((tr, cols), lambda i, chip_ref: (i, 0))),
        out_shape=jax.ShapeDtypeStruct((r2, cols), F32), compiler_params=_params(),
    )(chip, own, *([recv] * N_CHIPS))


def _adam_update(w, gv, m, v):
    c1 = 1.0 / (1.0 - ADAM_B1 ** ADAM_STEP)
    c2 = 1.0 / (1.0 - ADAM_B2 ** ADAM_STEP)
    nm = ADAM_B1 * m + (1.0 - ADAM_B1) * gv
    nv = ADAM_B2 * v + (1.0 - ADAM_B2) * gv * gv
    return -ADAM_LR * ((nm * c1) / (jnp.sqrt(nv * c2) + ADAM_EPS) + ADAM_WD * w), nm, nv


def _adamw_halves(w, g_mine, g_other, m, v, core, *, name):
    r, cols = w.shape
    r2 = r // 2
    tr = _rtile(r2, 256, 8, row_bytes=4 * cols)
    nt = r2 // tr

    def body(core_ref, w_ref, gm_ref, go_ref, m_ref, v_ref, g_ref, d_ref, nm_ref, nv_ref):
        gv = jnp.where(pl.program_id(0) == core_ref[0], gm_ref[...], go_ref[...])
        g_ref[...] = gv
        d_ref[...], nm_ref[...], nv_ref[...] = _adam_update(w_ref[...], gv, m_ref[...], v_ref[...])

    full = pl.BlockSpec((tr, cols), lambda hf, i, core_ref: (hf * nt + i, 0))
    half = pl.BlockSpec((tr, cols), lambda hf, i, core_ref: (i, 0))
    shape = jax.ShapeDtypeStruct((r, cols), F32)
    return pl.pallas_call(
        body, name=name,
        grid_spec=pltpu.PrefetchScalarGridSpec(
            num_scalar_prefetch=1, grid=(2, nt), in_specs=[full, half, half, full, full], out_specs=[full] * 4),
        out_shape=[shape] * 4, compiler_params=_params(),
    )(core, w, g_mine, g_other, m, v)


def _adamw(w, g, m, v, *, name, rows=256):
    r, cols = w.shape
    tr = _rtile(r, rows, 8)

    def body(w_ref, g_ref, m_ref, v_ref, d_ref, nm_ref, nv_ref):
        d_ref[...], nm_ref[...], nv_ref[...] = _adam_update(w_ref[...], g_ref[...], m_ref[...], v_ref[...])

    blk = pl.BlockSpec((tr, cols), lambda i: (i, 0))
    shape = jax.ShapeDtypeStruct((r, cols), F32)
    return pl.pallas_call(
        body, name=name, grid=(r // tr,), in_specs=[blk] * 4, out_specs=[blk] * 3,
        out_shape=[shape] * 3, compiler_params=_params(),
    )(w, g, m, v)


_BIG = (("w_in", 1), ("w_branch_a", 0), ("w_branch_b", 0), ("w_out", 0), ("w_up", 1), ("w_down", 0),
        ("w_ple", 1), ("w_ple_gate", 0))
_SMALL = ("gmlp_ln_g", "gmlp_ln_b", "gmlp_w_s", "gmlp_b_s", "norm_ffn_g", "conv_b", "norm_ple_g", "norm_final_g",
          "b_f", "norm_mix_g")
N_LATE = 2
_WEIGHTS = ("norm_mix_g", "w_in", "b_f", "gmlp_ln_g", "gmlp_ln_b", "gmlp_w_s", "gmlp_b_s", "w_branch_a",
            "w_branch_b", "w_out", "norm_ffn_g", "w_up", "conv_w", "conv_b", "w_down", "norm_ple_g", "w_ple",
            "w_ple_gate", "norm_final_g")
_PACK_ROWS = 8


def _pack(arrays):
    parts = []
    for a in arrays:
        flat = a.reshape(-1)
        unit = _PACK_ROWS * LANES
        flat = jnp.pad(flat, (0, (-flat.shape[0]) % unit))
        parts.append(flat.reshape(-1, LANES))
    return jnp.concatenate(parts, axis=0)


def _unpack(packed, shapes):
    out, row = [], 0
    for shp in shapes:
        size = math.prod(shp)
        rows = -(-size // (_PACK_ROWS * LANES)) * _PACK_ROWS
        out.append(packed[row:row + rows].reshape(-1)[:size].reshape(shp))
        row += rows
    return out


def _take_cols(parts, lo, hi):
    out, start = [], 0
    for a in parts:
        width = a.shape[1]
        a0, a1 = max(lo, start) - start, min(hi, start + width) - start
        if a1 > a0:
            out.append(a if (a0, a1) == (0, width) else a[:, a0:a1])
        start += width
    return out[0] if len(out) == 1 else jnp.concatenate(out, axis=1)


def _assemble(gathered, axis):
    n, r, cols = gathered.shape
    if axis == 0:
        return gathered.reshape(n * r, cols)
    return _take_cols([gathered[j] for j in range(n)], 0, n * cols)


def _to_chunks(parts, axis):
    rows, total = parts[0].shape[0], sum(a.shape[1] for a in parts)
    if axis == 0:
        r, cols = rows // N_CHIPS, total
        chunks = _take_cols(parts, 0, total).reshape(N_CHIPS, r, cols)
    else:
        r, cols = rows, total // N_CHIPS
        chunks = jnp.stack([_take_cols(parts, j * cols, (j + 1) * cols) for j in range(N_CHIPS)])
    return chunks.reshape(N_CHIPS, 2, r // 2, cols)


def kernel(x, p, norm_mix_g, w_in, b_f, gmlp_ln_g, gmlp_ln_b, gmlp_w_s, gmlp_b_s, w_branch_a, w_branch_b, w_out, norm_ffn_g, w_up, conv_w, conv_b, w_down, norm_ple_g, w_ple, w_ple_gate, norm_final_g, loss_target, m_norm_mix_g, m_w_in, m_b_f, m_gmlp_ln_g, m_gmlp_ln_b, m_gmlp_w_s, m_gmlp_b_s, m_w_branch_a, m_w_branch_b, m_w_out, m_norm_ffn_g, m_w_up, m_conv_w, m_conv_b, m_w_down, m_norm_ple_g, m_w_ple, m_w_ple_gate, m_norm_final_g, v_norm_mix_g, v_w_in, v_b_f, v_gmlp_ln_g, v_gmlp_ln_b, v_gmlp_w_s, v_gmlp_b_s, v_w_branch_a, v_w_branch_b, v_w_out, v_norm_ffn_g, v_w_up, v_conv_w, v_conv_b, v_w_down, v_norm_ple_g, v_w_ple, v_w_ple_gate, v_norm_final_g):
    args = dict(locals())
    wt = {n: args[n] for n in _WEIGHTS}
    mom = {n: args["m_" + n] for n in _WEIGHTS}
    var = {n: args["v_" + n] for n in _WEIGHTS}
    chip = 2 * lax.axis_index("x") + lax.axis_index("y")
    core = lax.axis_index("c").astype(jnp.int32).reshape(1)

    chip1 = chip.astype(jnp.int32).reshape(1)
    device = 2 * chip + lax.axis_index("c")
    axis_of = dict(_BIG)
    names = [n for n, _ in _BIG]
    put_mine = lambda land, mine: lax.dynamic_update_index_in_dim(land, mine, chip, 0)

    shard_in = w_in[0].astype(BF16)
    sems_in = _copies_start([shard_in], [lax.empty((N_CHIPS,) + shard_in.shape, BF16)], _gather_plan([True]), 3,
                            name="gather_start_in")
    _, wt["w_in"], mom["w_in"], var["w_in"] = lax.optimization_barrier((sems_in[4], w_in, m_w_in, v_w_in))
    shards = [wt[n][0].astype(BF16) for n in names[1:]] + [conv_w[0]]
    halved = [True] * len(names[1:]) + [False]
    lands = [lax.empty((N_CHIPS,) + a.shape, a.dtype) for a in shards]
    send_sems, recv_sems, srcs, lands, rest_token = _copies_start(
        shards, lands, _gather_plan(halved), 3 * len(shards), name="gather_start_rest", after=[sems_in[4]])
    o1 = 2 * GMLP_WIDTH
    o2 = o1 + 3 * FOX_WIDTH
    o3 = o2 + FOX_HEADS
    fpad = ((0, 0), (0, LANES - FOX_HEADS))
    w = {
        "conv_b": conv_b, "norm_mix_g": norm_mix_g, "norm_ffn_g": norm_ffn_g, "norm_ple_g": norm_ple_g,
        "norm_final_g": norm_final_g.reshape(1, D_MODEL), "b_f": jnp.pad(b_f, fpad),
        "gmlp_ln_g": gmlp_ln_g, "gmlp_ln_b": gmlp_ln_b, "gmlp_w_s": gmlp_w_s[0],
        "gmlp_b_s_t": jnp.pad(gmlp_b_s[0].T, ((0, 0), (0, LANES - GMLP_GROUPS))),
        "first_dep": rest_token,
    }

    def get_w_in(after):
        early = [a.reshape(a.shape[-2:]) for a in (wt["w_in"], mom["w_in"], var["w_in"])]
        _, got = _copies_wait(sems_in[0], sems_in[1], sems_in[2], sems_in[3], _gather_plan([True]), 0,
                              [after] + early, name="gather_wait_in")
        got = _forward_halves(got, name="gather_forward_in")
        slots = put_mine(got[0], shard_in)
        slots = [slots[j] for j in range(N_CHIPS)]
        return {"w_uv": _take_cols(slots, 0, o1), "w_qkv": _take_cols(slots, o1, o2),
                "w_f": jnp.pad(_take_cols(slots, o2, o3), fpad), "w_g": _take_cols(slots, o3, o3 + 2 * D_MODEL)}

    def start_w_rest(after):
        _, got = _copies_wait(send_sems, recv_sems, srcs, lands, _gather_plan(halved), 0, [after],
                              name="gather_wait_rest")
        ssem, rsem, _, fwd, token = _copies_start([], got[:-1], _forward_plan, 3 * len(got[:-1]),
                                                  name="gather_forward_start")
        pending["forward"] = (ssem, rsem, fwd, got[-1])
        return token

    def get_w_rest(after):
        ssem, rsem, fwd, whole = pending["forward"]
        _, fwd = _copies_wait(ssem, rsem, [], fwd, _forward_plan, 0, after, name="gather_forward_wait")
        got = fwd + [whole]
        slots = {n: put_mine(got[i], shards[i]) for i, n in enumerate(names[1:])}
        full = {n: _assemble(slots[n], axis_of[n]) for n in names[1:] if n != "w_up"}
        up = [slots["w_up"][j] for j in range(N_CHIPS)]
        return {"w_branch_a": full["w_branch_a"], "w_branch_b": full["w_branch_b"], "w_out": full["w_out"],
                "w_up_a": _take_cols(up, 0, D_FF), "w_up_b": _take_cols(up, D_FF, 2 * D_FF),
                "w_down": full["w_down"], "w_ple": full["w_ple"], "w_ple_gate": full["w_ple_gate"],
                "conv_w": _assemble(put_mine(got[-1], shards[-1]), 1)}

    grads, delta, new_m, new_v = {}, {}, {}, {}
    pending = {}

    def to_chunks(n, gr):
        return _to_chunks(gr if isinstance(gr, list) else [gr], axis_of[n])

    def pair_start(group, gfull, tag):
        chunks = [to_chunks(n, gfull[n]) for n in group]
        empty = [lax.empty((N_CHIPS,) + a.shape[2:], a.dtype) for a in chunks]
        ssem, rsem, own, recv, token = _copies_start(chunks, empty, _pair_plan, N_CHIPS * len(group),
                                                     name="grad_pair_start_" + tag)
        pending["pair_" + tag] = (ssem, rsem, own, recv)
        return token

    def reduce_start(group, gfull, tag, after=None):
        if after is None:
            chunks = [to_chunks(n, gfull[n]) for n in group]
            from_sibling = _pair_exchange(chunks, name="grad_pair_exchange_" + tag)
        else:
            ssem, rsem, own, recv = pending["pair_" + tag]
            chunks, from_sibling = _copies_wait(ssem, rsem, own, recv, _pair_plan, 0, after,
                                                name="grad_pair_wait_" + tag)
        pair_sums = [_pair_add(chunks[i], from_sibling[i], core, name="grad_pair_add_" + n) for i, n in enumerate(group)]
        empty = [lax.empty(a.shape, a.dtype) for a in pair_sums]
        ssem, rsem, own, recv, token = _copies_start(pair_sums, empty, _chip_plan, 3 * len(group),
                                                     name="grad_chip_start_" + tag)
        pending[tag] = (ssem, rsem, own, recv)
        return token

    def reduce_sum(group, tag, after):
        ssem, rsem, own, recv = pending[tag]
        own, recv = _copies_wait(ssem, rsem, own, recv, _chip_plan, 0, after, name="grad_chip_wait_" + tag)
        halves = [_chip_sum(own[i], recv[i], chip1, name="grad_chip_sum_" + n) for i, n in enumerate(group)]
        empty = [lax.empty(a.shape, a.dtype) for a in halves]
        ssem, rsem, halves, other, token = _copies_start(halves, empty, _share_plan, len(group),
                                                        name="grad_share_start_" + tag)
        pending["share_" + tag] = (ssem, rsem, halves, other)
        return token

    def reduce_update(group, tag, after):
        ssem, rsem, halves, other = pending["share_" + tag]
        halves, other_halves = _copies_wait(ssem, rsem, halves, other, _share_plan, 0, after,
                                            name="grad_share_wait_" + tag)
        for i, n in enumerate(group):
            shp = wt[n].shape
            outs = _adamw_halves(wt[n].reshape(shp[-2:]), halves[i], other_halves[i], mom[n].reshape(shp[-2:]),
                                 var[n].reshape(shp[-2:]), core, name="adamw_" + n)
            grads[n], delta[n], new_m[n], new_v[n] = (o.reshape(shp) for o in outs)
        return new_v[group[-1]]

    def reduce_finish(group, tag, after):
        ssem, rsem, own, recv = pending[tag]
        own, recv = _copies_wait(ssem, rsem, own, recv, _chip_plan, 0, after, name="grad_chip_wait_" + tag)
        halves = [_chip_sum(own[i], recv[i], chip1, name="grad_chip_sum_" + n) for i, n in enumerate(group)]
        other_halves = _pair_share(halves, name="grad_pair_share_" + tag)
        for i, n in enumerate(group):
            shp = wt[n].shape
            outs = _adamw_halves(wt[n].reshape(shp[-2:]), halves[i], other_halves[i], mom[n].reshape(shp[-2:]),
                                 var[n].reshape(shp[-2:]), core, name="adamw_" + n)
            grads[n], delta[n], new_m[n], new_v[n] = (o.reshape(shp) for o in outs)
        return new_v[group[-1]]

    ffn_group = ("w_up", "w_down", "w_ple", "w_ple_gate", "w_branch_a", "w_branch_b", "w_out")
    mix_group = ("w_in",)

    def on_grads_ffn(g):
        gfull = dict(g)
        gfull["w_up"] = [g["w_up_a"], g["w_up_b"]]
        return pair_start(ffn_group, gfull, "ffn")

    def on_grads_small(g):
        chip_token = reduce_start(ffn_group, None, "ffn", after=[g["gmlp_w_s"]])
        vec = _pack([g[n] for n in _SMALL[:-N_LATE]] + [g["conv_w"]])
        ssem, rsem, own, recv, token = _copies_start(
            [vec], [lax.empty((8,) + vec.shape, F32)], _all_plan, 7, name="small_start", after=[chip_token])
        pending["small"] = (ssem, rsem, own, recv)
        return token

    def on_grads_mix(g):
        gfull = dict(g)
        gfull["w_in"] = [g["w_uv"], g["w_qkv"], g["w_f"][:, :FOX_HEADS], g["w_g"]]
        return reduce_start(mix_group, gfull, "mix")

    def on_after_dh(dh):
        return reduce_sum(ffn_group, "ffn", [dh])

    loss, grad_x, g = _device_step(x[0], p[0, 0], loss_target[0], w, get_w_in, (start_w_rest, get_w_rest), on_grads_ffn,
                                   on_grads_small, on_grads_mix, on_after_dh)

    ffn_done = reduce_update(ffn_group, "ffn", [grad_x])
    mix_done = reduce_finish(mix_group, "mix", [ffn_done])
    ssem, rsem, own, recv = pending["small"]
    own, recv = _copies_wait(ssem, rsem, own, recv, _all_plan, 0, [mix_done], name="small_wait")
    vec_early = _sum_slots(lax.dynamic_update_index_in_dim(recv[0], own[0], device, 0), F32, name="small_sum")
    vec_late = _pack([g["b_f"][:, :FOX_HEADS], g["norm_mix_g"]])
    vec_late = _sum_slots(_all_exchange(vec_late, name="small_exchange_late"), F32, name="small_sum_late")
    early_rows = _pack([wt[n] for n in _SMALL[:-N_LATE]]).shape[0]
    vec = jnp.concatenate([vec_early[:early_rows], vec_late], axis=0)
    for n, a in zip(_SMALL, _unpack(vec, [wt[n].shape for n in _SMALL])):
        grads[n] = a
    conv_w_grad = _unpack(vec_early[early_rows:], [(3, 2 * D_FF)])[0]
    grads["conv_w"] = lax.dynamic_slice_in_dim(conv_w_grad, chip * conv_w.shape[2], conv_w.shape[2], axis=1).reshape(conv_w.shape)

    shp = conv_w.shape
    outs = _adamw(conv_w.reshape(shp[-2:]), grads["conv_w"].reshape(shp[-2:]), m_conv_w.reshape(shp[-2:]),
                  v_conv_w.reshape(shp[-2:]), name="adamw_conv_w")
    delta["conv_w"], new_m["conv_w"], new_v["conv_w"] = (o.reshape(shp) for o in outs)
    outs = _adamw(_pack([wt[n] for n in _SMALL]), vec, _pack([mom[n] for n in _SMALL]),
                  _pack([var[n] for n in _SMALL]), name="adamw_small", rows=2048)
    for d, o in zip((delta, new_m, new_v), outs):
        for n, a in zip(_SMALL, _unpack(o, [wt[n].shape for n in _SMALL])):
            d[n] = a

    total_loss = lax.psum(loss[0, 0], ("x", "y", "c"))
    return (total_loss, grad_x.reshape(x.shape), *[grads[n] for n in _WEIGHTS], *[delta[n] for n in _WEIGHTS],
            *[new_m[n] for n in _WEIGHTS], *[new_v[n] for n in _WEIGHTS])
```

```python
import math

import jax
import jax.numpy as jnp
from jax import lax
from jax.experimental import pallas as pl
from jax.experimental.pallas import tpu as pltpu

F32 = jnp.float32
BF16 = jnp.bfloat16

D_MODEL = 1024
EPS = 1e-6
CHUNK = 64
GMLP_GROUPS = 8
GMLP_BLOCK = 128
GMLP_WIDTH = 1024
FOX_HEADS = 16
FOX_HEAD_DIM = 64
FOX_WIDTH = 1024
HEAD_PAIRS = FOX_HEADS // 2
ATT_BLOCK = 128
D_FF = 2816
PLE_DIM = 256
LANES = 128
BF16_TILE_ROWS = 16
N_CHIPS = 4

ADAM_LR = 0.001
ADAM_B1 = 0.9
ADAM_B2 = 0.999
ADAM_EPS = 1e-08
ADAM_WD = 0.01
ADAM_STEP = 10

VMEM_LIMIT = 56 * 1024 * 1024
MESH = pl.DeviceIdType.MESH

_NN = (((1,), (0,)), ((), ()))
_NT = (((1,), (1,)), ((), ()))
_TN = (((0,), (0,)), ((), ()))


def _params(**kw):
    return pltpu.CompilerParams(vmem_limit_bytes=VMEM_LIMIT, **kw)


def _tile(dim, pref):
    if dim <= pref:
        return dim
    t = (pref // LANES) * LANES
    while t >= LANES:
        if dim % t == 0:
            return t
        t -= LANES
    return dim


def _dot(a, b, dn):
    return lax.dot_general(a.astype(BF16), b.astype(BF16), dn, preferred_element_type=F32)


def _gelu(x):
    c = math.sqrt(2.0 / math.pi)
    t = jnp.tanh(c * (x + 0.044715 * x * x * x))
    return 0.5 * x * (1.0 + t)


def _gelu_and_grad(x):
    c = math.sqrt(2.0 / math.pi)
    x2 = x * x
    t = jnp.tanh(c * (x + 0.044715 * x2 * x))
    g = 0.5 * x * (1.0 + t)
    dg = 0.5 * (1.0 + t) + 0.5 * x * (1.0 - t * t) * c * (1.0 + 3.0 * 0.044715 * x2)
    return g, dg


def _sigmoid(x):
    return 1.0 / (1.0 + jnp.exp(-x))


def _mm(a, b, *, mode, out_dtype, name, add=None, tm=1024, tn=1024, dep=None):
    if mode == "nn":
        m, k = a.shape
        k2, n = b.shape
    elif mode == "nt":
        m, k = a.shape
        n, k2 = b.shape
    else:
        k, m = a.shape
        k2, n = b.shape
    assert k == k2, (name, a.shape, b.shape)
    tm = _tile(m, tm)
    tn = _tile(n, tn)
    dn = {"nn": _NN, "nt": _NT, "tn": _TN}[mode]

    def body(a_ref, b_ref, *rest):
        o_ref = rest[-1]
        acc = _dot(a_ref[...], b_ref[...], dn)
        if add is not None:
            acc = acc + rest[0][...].astype(F32)
        o_ref[...] = acc.astype(o_ref.dtype)

    a_spec = pl.BlockSpec((k, tm), lambda i, j: (0, i)) if mode == "tn" else pl.BlockSpec((tm, k), lambda i, j: (i, 0))
    b_spec = pl.BlockSpec((tn, k), lambda i, j: (j, 0)) if mode == "nt" else pl.BlockSpec((k, tn), lambda i, j: (0, j))
    o_spec = pl.BlockSpec((tm, tn), lambda i, j: (i, j))
    in_specs = [a_spec, b_spec]
    args = [a, b]
    if add is not None:
        in_specs.append(o_spec)
        args.append(add)
    if dep is not None:
        in_specs.append(pl.BlockSpec(memory_space=pl.ANY))
        args.append(dep)
    return pl.pallas_call(
        body, name=name, grid=(m // tm, n // tn), in_specs=in_specs, out_specs=o_spec,
        out_shape=jax.ShapeDtypeStruct((m, n), out_dtype), compiler_params=_params(),
    )(*args)


def _mm_nt_sum(pairs, *, out_dtype, name, tm=256, dep=None):
    m, n = pairs[0][0].shape[0], pairs[0][1].shape[0]
    tm = _tile(m, tm)
    np_ = len(pairs)

    def body(*refs):
        o_ref = refs[-1]
        acc = None
        for p in range(np_):
            part = _dot(refs[2 * p][...], refs[2 * p + 1][...], _NT)
            acc = part if acc is None else acc + part
        o_ref[...] = acc.astype(o_ref.dtype)

    in_specs, args = [], []
    for a, b in pairs:
        assert a.shape[0] == m and b.shape[0] == n and a.shape[1] == b.shape[1], (name, a.shape, b.shape)
        in_specs += [pl.BlockSpec((tm, a.shape[1]), lambda i: (i, 0)), pl.BlockSpec(b.shape, lambda i: (0, 0))]
        args += [a, b]
    if dep is not None:
        in_specs.append(pl.BlockSpec(memory_space=pl.ANY))
        args.append(dep)
    return pl.pallas_call(
        body, name=name, grid=(m // tm,), in_specs=in_specs, out_specs=pl.BlockSpec((tm, n), lambda i: (i, 0)),
        out_shape=jax.ShapeDtypeStruct((m, n), out_dtype), compiler_params=_params(),
    )(*args)


def _rms_fwd(x, g, *, name, tm=512, dep=None):
    s, d = x.shape
    tm = _tile(s, tm)

    def body(x_ref, g_ref, *rest):
        h_ref = rest[-1]
        xv = x_ref[...]
        r = lax.rsqrt(jnp.mean(xv * xv, axis=-1, keepdims=True) + EPS)
        h_ref[...] = (xv * r * g_ref[...]).astype(h_ref.dtype)

    deps = [] if dep is None else [dep]
    return pl.pallas_call(
        body, name=name, grid=(s // tm,),
        in_specs=[pl.BlockSpec((tm, d), lambda i: (i, 0)), pl.BlockSpec((1, d), lambda i: (0, 0))]
                 + [pl.BlockSpec(memory_space=pl.ANY)] * len(deps),
        out_specs=pl.BlockSpec((tm, d), lambda i: (i, 0)),
        out_shape=jax.ShapeDtypeStruct((s, d), BF16), compiler_params=_params(),
    )(x, g, *deps)


def _rms_bwd(x, g, dh, dres, *, name, tm=512, dep=None):
    s, d = x.shape
    tm = _tile(s, tm)
    deps = [] if dep is None else [dep]

    def body(x_ref, g_ref, dh_ref, dres_ref, *rest):
        dx_ref, dxb_ref, dg_ref = rest[len(deps):]
        xv = x_ref[...]
        r = lax.rsqrt(jnp.mean(xv * xv, axis=-1, keepdims=True) + EPS)
        xhat = xv * r
        dhv = dh_ref[...].astype(F32)
        dyg = dhv * g_ref[...]
        dx = dres_ref[...] + r * (dyg - xhat * jnp.mean(dyg * xhat, axis=-1, keepdims=True))
        dx_ref[...] = dx
        dxb_ref[...] = dx.astype(dxb_ref.dtype)

        @pl.when(pl.program_id(0) == 0)
        def _():
            dg_ref[...] = jnp.zeros_like(dg_ref)

        dg_ref[...] += jnp.sum(dhv * xhat, axis=0, keepdims=True)

    row = pl.BlockSpec((tm, d), lambda i: (i, 0))
    vec = pl.BlockSpec((1, d), lambda i: (0, 0))
    return pl.pallas_call(
        body, name=name, grid=(s // tm,),
        in_specs=[row, vec, row, row] + [pl.BlockSpec(memory_space=pl.ANY)] * len(deps), out_specs=[row, row, vec],
        out_shape=[jax.ShapeDtypeStruct((s, d), F32), jax.ShapeDtypeStruct((s, d), BF16),
                   jax.ShapeDtypeStruct((1, d), F32)],
        compiler_params=_params(),
    )(x, g, dh, dres, *deps)


def _gmlp_mask():
    t = lax.broadcasted_iota(jnp.int32, (GMLP_BLOCK, GMLP_BLOCK), 0)
    s_ = lax.broadcasted_iota(jnp.int32, (GMLP_BLOCK, GMLP_BLOCK), 1)
    return (s_ // CHUNK) <= (t // CHUNK)


def _gmlp_norm(zv, ln_g, ln_b):
    vv, dvv = _gelu_and_grad(zv)
    mu = jnp.mean(vv, axis=-1, keepdims=True)
    xc = vv - mu
    rstd = lax.rsqrt(jnp.mean(xc * xc, axis=-1, keepdims=True) + EPS)
    vhat = xc * rstd
    return vhat * ln_g + ln_b, vhat, rstd, dvv


def _gmlp_fwd(z_uv, ln_g, ln_b, w_s, b_s_t, *, name):
    s = z_uv.shape[0]
    w = GMLP_WIDTH
    gd = w // GMLP_GROUPS

    def body(z_ref, lg_ref, lb_ref, ws_ref, bs_ref, a_ref):
        u = _gelu(z_ref[:, :w].astype(F32))
        vn, _, _, _ = _gmlp_norm(z_ref[:, w:].astype(F32), lg_ref[...], lb_ref[...])
        mask = _gmlp_mask()
        for g in range(GMLP_GROUPS):
            wm = jnp.where(mask, ws_ref[g], 0.0)
            mixed = _dot(wm, vn[:, g * gd:(g + 1) * gd], _NN) + bs_ref[:, g:g + 1]
            a_ref[:, g * gd:(g + 1) * gd] = (u[:, g * gd:(g + 1) * gd] * mixed).astype(a_ref.dtype)

    full = lambda shape: pl.BlockSpec(shape, lambda i: (0,) * len(shape))
    return pl.pallas_call(
        body, name=name, grid=(s // GMLP_BLOCK,),
        in_specs=[pl.BlockSpec((GMLP_BLOCK, 2 * w), lambda i: (i, 0)), full((1, w)), full((1, w)),
                  full((GMLP_GROUPS, GMLP_BLOCK, GMLP_BLOCK)), full((GMLP_BLOCK, LANES))],
        out_specs=pl.BlockSpec((GMLP_BLOCK, w), lambda i: (i, 0)),
        out_shape=jax.ShapeDtypeStruct((s, w), BF16), compiler_params=_params(),
    )(z_uv, ln_g, ln_b, w_s, b_s_t)


def _gmlp_bwd(z_uv, da, ln_g, ln_b, w_s, b_s_t, *, name):
    s = z_uv.shape[0]
    w = GMLP_WIDTH
    gd = w // GMLP_GROUPS

    def body(z_ref, da_ref, lg_ref, lb_ref, ws_ref, bs_ref, dz_ref, dws_ref, dbs_ref, dlg_ref, dlb_ref):
        @pl.when(pl.program_id(0) == 0)
        def _():
            dws_ref[...] = jnp.zeros_like(dws_ref)
            dbs_ref[...] = jnp.zeros_like(dbs_ref)
            dlg_ref[...] = jnp.zeros_like(dlg_ref)
            dlb_ref[...] = jnp.zeros_like(dlb_ref)

        u, du_dz = _gelu_and_grad(z_ref[:, :w].astype(F32))
        lg = lg_ref[...]
        vn, vhat, rstd, dvv_dz = _gmlp_norm(z_ref[:, w:].astype(F32), lg, lb_ref[...])
        dav = da_ref[...].astype(F32)
        mask = _gmlp_mask()
        lane = lax.broadcasted_iota(jnp.int32, (GMLP_BLOCK, LANES), 1)
        dvn_parts = []
        dbs = jnp.zeros((GMLP_BLOCK, LANES), F32)
        for g in range(GMLP_GROUPS):
            sl = slice(g * gd, (g + 1) * gd)
            wm = jnp.where(mask, ws_ref[g], 0.0)
            vn_g = vn[:, sl]
            mixed = _dot(wm, vn_g, _NN) + bs_ref[:, g:g + 1]
            dmixed = dav[:, sl] * u[:, sl]
            dz_ref[:, sl] = (dav[:, sl] * mixed * du_dz[:, sl]).astype(dz_ref.dtype)
            dvn_parts.append(_dot(wm, dmixed, _TN))
            dws_ref[g] += jnp.where(mask, _dot(dmixed, vn_g, _NT), 0.0)
            dbs = dbs + jnp.where(lane == g, jnp.sum(dmixed, axis=-1, keepdims=True), 0.0)
        dbs_ref[...] += dbs
        dvn = jnp.concatenate(dvn_parts, axis=-1)
        dlg_ref[...] += jnp.sum(dvn * vhat, axis=0, keepdims=True)
        dlb_ref[...] += jnp.sum(dvn, axis=0, keepdims=True)
        dyg = dvn * lg
        dvv = rstd * (dyg - jnp.mean(dyg, axis=-1, keepdims=True)
                      - vhat * jnp.mean(dyg * vhat, axis=-1, keepdims=True))
        dz_ref[:, w:] = (dvv * dvv_dz).astype(dz_ref.dtype)

    full = lambda shape: pl.BlockSpec(shape, lambda i: (0,) * len(shape))
    return pl.pallas_call(
        body, name=name, grid=(s // GMLP_BLOCK,),
        in_specs=[pl.BlockSpec((GMLP_BLOCK, 2 * w), lambda i: (i, 0)),
                  pl.BlockSpec((GMLP_BLOCK, w), lambda i: (i, 0)), full((1, w)), full((1, w)),
                  full((GMLP_GROUPS, GMLP_BLOCK, GMLP_BLOCK)), full((GMLP_BLOCK, LANES))],
        out_specs=[pl.BlockSpec((GMLP_BLOCK, 2 * w), lambda i: (i, 0)),
                   full((GMLP_GROUPS, GMLP_BLOCK, GMLP_BLOCK)), full((GMLP_BLOCK, LANES)),
                   full((1, w)), full((1, w))],
        out_shape=[jax.ShapeDtypeStruct((s, 2 * w), BF16),
                   jax.ShapeDtypeStruct((GMLP_GROUPS, GMLP_BLOCK, GMLP_BLOCK), F32),
                   jax.ShapeDtypeStruct((GMLP_BLOCK, LANES), F32),
                   jax.ShapeDtypeStruct((1, w), F32), jax.ShapeDtypeStruct((1, w), F32)],
        compiler_params=_params(),
    )(z_uv, da, ln_g, ln_b, w_s, b_s_t)


def _tri(lower):
    r = lax.broadcasted_iota(jnp.int32, (ATT_BLOCK, ATT_BLOCK), 0)
    c = lax.broadcasted_iota(jnp.int32, (ATT_BLOCK, ATT_BLOCK), 1)
    return jnp.where((c <= r) if lower else (c >= r), 1.0, 0.0).astype(F32)


def _log_sigmoid(x):
    return jnp.minimum(x, 0.0) - jnp.log(1.0 + jnp.exp(-jnp.abs(x)))


def _fox_cum(f, b_f, *, name):
    s = f.shape[0]
    nb = s // ATT_BLOCK

    def body(f_ref, b_ref, cb_ref, ct_ref, carry):
        @pl.when(pl.program_id(0) == 0)
        def _():
            carry[...] = jnp.zeros_like(carry)

        lf = _log_sigmoid(f_ref[...] + b_ref[...])
        cum = lax.dot_general(_tri(True), lf, _NN, precision=lax.Precision.HIGHEST,
                              preferred_element_type=F32) + carry[...]
        carry[...] = cum[ATT_BLOCK - 1:ATT_BLOCK, :]
        for h in range(FOX_HEADS):
            cb_ref[h] = jnp.broadcast_to(cum[:, h:h + 1], (ATT_BLOCK, LANES))
        ct_ref[...] = cum.T

    return pl.pallas_call(
        body, name=name, grid=(nb,),
        in_specs=[pl.BlockSpec((ATT_BLOCK, LANES), lambda i: (i, 0)), pl.BlockSpec((1, LANES), lambda i: (0, 0))],
        out_specs=[pl.BlockSpec((FOX_HEADS, ATT_BLOCK, LANES), lambda i: (0, i, 0)),
                   pl.BlockSpec((LANES, ATT_BLOCK), lambda i: (0, i))],
        out_shape=[jax.ShapeDtypeStruct((FOX_HEADS, s, LANES), F32), jax.ShapeDtypeStruct((LANES, s), F32)],
        scratch_shapes=[pltpu.VMEM((1, LANES), F32)], compiler_params=_params(),
    )(f, b_f)


def _fox_dlogit(dcum_t, f, b_f, *, name):
    s = f.shape[0]
    nb = s // ATT_BLOCK

    def body(dc_ref, f_ref, b_ref, df_ref, db_ref, carry):
        @pl.when(pl.program_id(0) == 0)
        def _():
            carry[...] = jnp.zeros_like(carry)
            db_ref[...] = jnp.zeros_like(db_ref)

        d = dc_ref[...].T
        dlog = lax.dot_general(_tri(False), d, _NN, precision=lax.Precision.HIGHEST,
                               preferred_element_type=F32) + carry[...]
        carry[...] = dlog[0:1, :]
        df = dlog * (1.0 - _sigmoid(f_ref[...] + b_ref[...]))
        df_ref[...] = df
        db_ref[...] += jnp.sum(df, axis=0, keepdims=True)

    rev = lambda i: nb - 1 - i
    return pl.pallas_call(
        body, name=name, grid=(nb,),
        in_specs=[pl.BlockSpec((LANES, ATT_BLOCK), lambda i: (0, rev(i))),
                  pl.BlockSpec((ATT_BLOCK, LANES), lambda i: (rev(i), 0)),
                  pl.BlockSpec((1, LANES), lambda i: (0, 0))],
        out_specs=[pl.BlockSpec((ATT_BLOCK, LANES), lambda i: (rev(i), 0)),
                   pl.BlockSpec((1, LANES), lambda i: (0, 0))],
        out_shape=[jax.ShapeDtypeStruct((s, LANES), F32), jax.ShapeDtypeStruct((1, LANES), F32)],
        scratch_shapes=[pltpu.VMEM((1, LANES), F32)], compiler_params=_params(),
    )(dcum_t, f, b_f)


def _head_mask():
    return lax.broadcasted_iota(jnp.int32, (1, LANES), 1) < FOX_HEAD_DIM


ATT_TQ = 256
ATT_TK = 256
ATT_SCALE = FOX_HEAD_DIM ** -0.5
assert ATT_SCALE == 0.125 and ATT_TQ == ATT_TK


def _causal_t(qi, ki):
    kpos = lax.broadcasted_iota(jnp.int32, (ATT_TK, ATT_TQ), 0) + ki * ATT_TK
    qpos = lax.broadcasted_iota(jnp.int32, (ATT_TK, ATT_TQ), 1) + qi * ATT_TQ
    return kpos <= qpos


def _row_mask():
    return lax.broadcasted_iota(jnp.int32, (LANES, 1), 0) < FOX_HEAD_DIM


def _lane_tile(a, width):
    return a if a.shape[1] == width else jnp.tile(a, (1, width // a.shape[1]))


def _transpose_bf16(a):
    return a.astype(F32).T.astype(BF16)


def _attn_fwd_t(qkv, cum_b, cum_r, *, name):
    s = qkv.shape[0]
    nq = s // ATT_TQ
    npair = HEAD_PAIRS

    def body(q_ref, k_ref, v_ref, cq_ref, ck_ref, o_ref, ot_ref, l_ref, vt_ref):
        qi = pl.program_id(1)
        rows = _row_mask()

        @pl.when(qi == 0)
        def _():
            vt_ref[...] = _transpose_bf16(v_ref[...])

        qt = _transpose_bf16(q_ref[...]) * ATT_SCALE
        zero = jnp.zeros_like(qt)
        qts = (jnp.where(rows, qt, zero), jnp.where(rows, zero, qt))

        def step(ki, carry, masked):
            off = pl.multiple_of(ki * ATT_TK, ATT_TK)
            k2 = k_ref[pl.ds(off, ATT_TK), :]
            vt = vt_ref[:, pl.ds(off, ATT_TK)]
            out = []
            for hh in range(2):
                m, l, acc = carry[hh]
                bias = cq_ref[hh:hh + 1, :] - _lane_tile(ck_ref[hh, pl.ds(off, ATT_TK), :], ATT_TQ)
                sc = _dot(k2, qts[hh], _NN) + bias
                if masked:
                    sc = jnp.where(_causal_t(qi, ki), sc, -1e30)
                m_new = jnp.maximum(m, jnp.max(sc, axis=0, keepdims=True))
                alpha = jnp.exp(m - m_new)
                p = jnp.exp(sc - m_new)
                l = alpha * l + jnp.sum(p, axis=0, keepdims=True)
                p_hi = p.astype(BF16)
                p_lo = (p - p_hi.astype(F32)).astype(BF16)
                acc = alpha * acc + (_dot(vt, p_hi, _NN) + _dot(vt, p_lo, _NN))
                out.append((m_new, l, acc))
            return tuple(out)

        init = tuple((jnp.full((1, ATT_TQ), -1e30, F32), jnp.zeros((1, ATT_TQ), F32),
                      jnp.zeros((LANES, ATT_TQ), F32)) for _ in range(2))
        carry = lax.fori_loop(0, qi // 2, lambda kk, c: step(2 * kk + 1, step(2 * kk, c, False), False), init)
        carry = lax.cond(qi % 2 == 1, lambda c: step(qi - 1, c, False), lambda c: c, carry)
        (ma, la, acca), (mb, lb, accb) = step(qi, carry, True)
        ot = jnp.where(rows, acca / la, accb / lb)
        ot_ref[...] = ot
        o_ref[...] = ot.T.astype(o_ref.dtype)
        l_ref[0:1, :] = ma + jnp.log(la)
        l_ref[1:2, :] = mb + jnp.log(lb)

    row = pl.BlockSpec((None, 2, ATT_TQ), lambda j, i: (j, 0, i))
    return pl.pallas_call(
        body, name=name, grid=(npair, nq),
        in_specs=[pl.BlockSpec((ATT_TQ, LANES), lambda j, i: (i, j)),
                  pl.BlockSpec((s, LANES), lambda j, i: (0, npair + j)),
                  pl.BlockSpec((s, LANES), lambda j, i: (0, 2 * npair + j)),
                  row, pl.BlockSpec((None, 2, s, LANES), lambda j, i: (j, 0, 0, 0))],
        out_specs=[pl.BlockSpec((ATT_TQ, LANES), lambda j, i: (i, j)),
                   pl.BlockSpec((LANES, ATT_TQ), lambda j, i: (j, i)), row],
        out_shape=[jax.ShapeDtypeStruct((s, FOX_WIDTH), BF16), jax.ShapeDtypeStruct((FOX_WIDTH, s), F32),
                   jax.ShapeDtypeStruct((npair, 2, s), F32)],
        scratch_shapes=[pltpu.VMEM((LANES, s), BF16)],
        compiler_params=_params(),
    )(qkv, qkv, qkv, cum_r, cum_b)


def _attn_bwd_t(qkv, do, o_t, lse, cum_b, cum_r, *, name, dep=None):
    s = qkv.shape[0]
    nq = s // ATT_TQ
    npair = HEAD_PAIRS

    deps = [] if dep is None else [dep]

    def body(q_ref, k_ref, v_ref, do_ref, ot_ref, l_ref, cq_ref, ck_ref, *rest):
        dq_ref, dk_ref, dv_ref, dc_ref, qt_ref, dot_ref, dqt_ref, dl_ref = rest[len(deps):]
        ki = pl.program_id(1)
        m0 = _head_mask()
        rows = _row_mask()
        k2 = k_ref[...]
        v2 = v_ref[...]
        kt = _transpose_bf16(k2)
        ks = k2 * ATT_SCALE
        kz, vz = jnp.zeros_like(k2), jnp.zeros_like(v2)
        khs = (jnp.where(m0, ks, kz), jnp.where(m0, kz, ks))
        vhs = (jnp.where(m0, v2, vz), jnp.where(m0, vz, v2))
        cks = tuple(_lane_tile(ck_ref[hh], ATT_TQ) for hh in range(2))

        @pl.when(ki == 0)
        def _():
            dqt_ref[...] = jnp.zeros_like(dqt_ref)
            qt_ref[...] = _transpose_bf16(q_ref[...])
            do_t = do_ref[...].astype(F32).T
            dot_ref[...] = do_t.astype(BF16)
            prod = do_t * ot_ref[...]
            dl_ref[0:1, :] = jnp.sum(prod[:FOX_HEAD_DIM], axis=0, keepdims=True)
            dl_ref[1:2, :] = jnp.sum(prod[FOX_HEAD_DIM:], axis=0, keepdims=True)

        def step(qi, carry, masked):
            off = pl.multiple_of(qi * ATT_TQ, ATT_TQ)
            q2 = q_ref[pl.ds(off, ATT_TQ), :]
            do2 = do_ref[pl.ds(off, ATT_TQ), :]
            qt = qt_ref[:, pl.ds(off, ATT_TQ)]
            dot_ = dot_ref[:, pl.ds(off, ATT_TQ)]
            out, dqs = [], []
            for hh in range(2):
                dk_acc, dv_acc, dc_acc = carry[hh]
                sc = _dot(khs[hh], qt, _NN) + (cq_ref[hh:hh + 1, pl.ds(off, ATT_TQ)] - cks[hh])
                p = jnp.exp(sc - l_ref[hh:hh + 1, pl.ds(off, ATT_TQ)])
                if masked:
                    p = jnp.where(_causal_t(qi, ki), p, 0.0)
                dp = _dot(vhs[hh], dot_, _NN)
                ds = p * (dp - dl_ref[hh:hh + 1, pl.ds(off, ATT_TQ)])
                dc_acc = dc_acc - jnp.sum(ds, axis=1, keepdims=True)
                dss = (ds * ATT_SCALE).astype(BF16)
                dv_acc = dv_acc + _dot(p, do2, _NN)
                dk_acc = dk_acc + _dot(dss, q2, _NN)
                dqs.append(_dot(kt, dss, _NN))
                out.append((dk_acc, dv_acc, dc_acc))
            dqt_ref[:, pl.ds(off, ATT_TQ)] += jnp.where(rows, dqs[0], dqs[1])
            return tuple(out)

        init = tuple((jnp.zeros((ATT_TK, LANES), F32), jnp.zeros((ATT_TK, LANES), F32),
                      jnp.zeros((ATT_TK, 1), F32)) for _ in range(2))
        carry = step(ki, init, True)
        rest = nq - 1 - ki
        carry = lax.fori_loop(
            0, rest // 2, lambda t, c: step(ki + 2 + 2 * t, step(ki + 1 + 2 * t, c, False), False), carry)
        carry = lax.cond(rest % 2 == 1, lambda c: step(nq - 1, c, False), lambda c: c, carry)
        (dka, dva, dca), (dkb, dvb, dcb) = carry
        dk_ref[...] = jnp.where(m0, dka, dkb).astype(dk_ref.dtype)
        dv_ref[...] = jnp.where(m0, dva, dvb).astype(dv_ref.dtype)
        dc_ref[0] = jnp.broadcast_to(dca, (ATT_TK, LANES))
        dc_ref[1] = jnp.broadcast_to(dcb, (ATT_TK, LANES))

        @pl.when(ki == nq - 1)
        def _():
            dq_ref[...] = dqt_ref[...].T.astype(dq_ref.dtype)

    colfull = lambda base: pl.BlockSpec((s, LANES), lambda j, i: (0, base + j))
    colblk = lambda base: pl.BlockSpec((ATT_TK, LANES), lambda j, i: (i, base + j))
    stat = pl.BlockSpec((None, 2, s), lambda j, i: (j, 0, 0))
    bcast = pl.BlockSpec((None, 2, ATT_TK, LANES), lambda j, i: (j, 0, i, 0))
    grad = jax.ShapeDtypeStruct((s, FOX_WIDTH), BF16)
    return pl.pallas_call(
        body, name=name, grid=(npair, nq),
        in_specs=[colfull(0), colblk(npair), colblk(2 * npair), colfull(0),
                  pl.BlockSpec((LANES, s), lambda j, i: (j, 0)), stat, stat, bcast]
                 + [pl.BlockSpec(memory_space=pl.ANY)] * len(deps),
        out_specs=[colfull(0), colblk(0), colblk(0), bcast],
        out_shape=[grad, grad, grad, jax.ShapeDtypeStruct((npair, 2, s, LANES), F32)],
        scratch_shapes=[pltpu.VMEM((LANES, s), BF16), pltpu.VMEM((LANES, s), BF16), pltpu.VMEM((LANES, s), F32),
                        pltpu.VMEM((2, s), F32)],
        compiler_params=_params(),
    )(qkv, qkv, qkv, do, o_t, lse, cum_r, cum_b, *deps)


def _merge_fwd(zg, ya, yb, *, name, tm=512):
    s, d = ya.shape
    tm = _tile(s, tm)

    def body(zg_ref, ya_ref, yb_ref, m_ref):
        ga = _sigmoid(zg_ref[:, :d].astype(F32))
        gb = _sigmoid(zg_ref[:, d:].astype(F32))
        m_ref[...] = (ga * ya_ref[...].astype(F32) + gb * yb_ref[...].astype(F32)).astype(m_ref.dtype)

    row = pl.BlockSpec((tm, d), lambda i: (i, 0))
    row2 = pl.BlockSpec((tm, 2 * d), lambda i: (i, 0))
    return pl.pallas_call(
        body, name=name, grid=(s // tm,), in_specs=[row2, row, row], out_specs=row,
        out_shape=jax.ShapeDtypeStruct((s, d), BF16), compiler_params=_params(),
    )(zg, ya, yb)


def _merge_bwd(dm, zg, ya, yb, *, name, tm=512):
    s, d = ya.shape
    tm = _tile(s, tm)

    def body(dm_ref, zg_ref, ya_ref, yb_ref, dzg_ref, dya_ref, dyb_ref):
        dmv = dm_ref[...].astype(F32)
        ga = _sigmoid(zg_ref[:, :d].astype(F32))
        gb = _sigmoid(zg_ref[:, d:].astype(F32))
        dzg_ref[:, :d] = (dmv * ya_ref[...].astype(F32) * ga * (1.0 - ga)).astype(dzg_ref.dtype)
        dzg_ref[:, d:] = (dmv * yb_ref[...].astype(F32) * gb * (1.0 - gb)).astype(dzg_ref.dtype)
        dya_ref[...] = (dmv * ga).astype(dya_ref.dtype)
        dyb_ref[...] = (dmv * gb).astype(dyb_ref.dtype)

    row = pl.BlockSpec((tm, d), lambda i: (i, 0))
    row2 = pl.BlockSpec((tm, 2 * d), lambda i: (i, 0))
    return pl.pallas_call(
        body, name=name, grid=(s // tm,), in_specs=[row, row2, row, row], out_specs=[row2, row, row],
        out_shape=[jax.ShapeDtypeStruct((s, 2 * d), BF16), jax.ShapeDtypeStruct((s, d), BF16),
                   jax.ShapeDtypeStruct((s, d), BF16)],
        compiler_params=_params(),
    )(dm, zg, ya, yb)


SUBLANES = 8


def _shift_down(u, k, row):
    rolled = pltpu.roll(u, k, 0)
    head = jnp.where(row[:SUBLANES] >= k, rolled[:SUBLANES], 0.0)
    return jnp.concatenate([head, rolled[SUBLANES:]], axis=0)


def _shift_up(u, k, row):
    n = u.shape[0]
    rolled = pltpu.roll(u, n - k, 0)
    tail = jnp.where(row[n - SUBLANES:] < n - k, rolled[n - SUBLANES:], 0.0)
    return jnp.concatenate([rolled[:n - SUBLANES], tail], axis=0)


def _conv_act_fwd(up_a, up_b, cw_a, cw_b, cb_a, cb_b, *, name, tc=128):
    s, f = up_a.shape
    tc = _tile(f, tc)

    def body(ua_ref, ub_ref, wa_ref, wb_ref, ba_ref, bb_ref, act_ref):
        row = lax.broadcasted_iota(jnp.int32, (s, tc), 0)

        def conv(u_ref, w_ref, b_ref):
            u = u_ref[...].astype(F32)
            return (b_ref[...] + w_ref[0:1, :] * _shift_down(u, 2, row)
                    + w_ref[1:2, :] * _shift_down(u, 1, row) + w_ref[2:3, :] * u)

        ca = conv(ua_ref, wa_ref, ba_ref)
        cb = conv(ub_ref, wb_ref, bb_ref)
        act_ref[...] = (_gelu(ca) * cb).astype(act_ref.dtype)

    col = pl.BlockSpec((s, tc), lambda j: (0, j))
    w3 = pl.BlockSpec((3, tc), lambda j: (0, j))
    b1 = pl.BlockSpec((1, tc), lambda j: (0, j))
    return pl.pallas_call(
        body, name=name, grid=(f // tc,), in_specs=[col, col, w3, w3, b1, b1], out_specs=col,
        out_shape=jax.ShapeDtypeStruct((s, f), BF16), compiler_params=_params(),
    )(up_a, up_b, cw_a, cw_b, cb_a, cb_b)


def _conv_act_bwd(up_a, up_b, dact, cw_a, cw_b, cb_a, cb_b, *, name, tc=128):
    s, f = up_a.shape
    tc = _tile(f, tc)

    def body(ua_ref, ub_ref, da_ref, wa_ref, wb_ref, ba_ref, bb_ref, dua_ref, dub_ref, dwa_ref, dwb_ref):
        row = lax.broadcasted_iota(jnp.int32, (s, tc), 0)

        def conv(u_ref, w_ref, b_ref):
            u = u_ref[...].astype(F32)
            u1 = _shift_down(u, 1, row)
            u2 = _shift_down(u, 2, row)
            return u, u1, u2, b_ref[...] + w_ref[0:1, :] * u2 + w_ref[1:2, :] * u1 + w_ref[2:3, :] * u

        def back(dc, taps, w_ref, du_ref, dw_ref):
            u, u1, u2 = taps
            dw_ref[0:1, :] = jnp.sum(dc * u2, axis=0, keepdims=True)
            dw_ref[1:2, :] = jnp.sum(dc * u1, axis=0, keepdims=True)
            dw_ref[2:3, :] = jnp.sum(dc * u, axis=0, keepdims=True)
            dw_ref[3:4, :] = jnp.sum(dc, axis=0, keepdims=True)
            du = (w_ref[2:3, :] * dc + w_ref[1:2, :] * _shift_up(dc, 1, row)
                  + w_ref[0:1, :] * _shift_up(dc, 2, row))
            du_ref[...] = du.astype(du_ref.dtype)

        ua, ua1, ua2, ca = conv(ua_ref, wa_ref, ba_ref)
        ub, ub1, ub2, cb = conv(ub_ref, wb_ref, bb_ref)
        g, dg = _gelu_and_grad(ca)
        dact_v = da_ref[...].astype(F32)
        back(dact_v * cb * dg, (ua, ua1, ua2), wa_ref, dua_ref, dwa_ref)
        back(dact_v * g, (ub, ub1, ub2), wb_ref, dub_ref, dwb_ref)

    col = pl.BlockSpec((s, tc), lambda j: (0, j))
    w3 = pl.BlockSpec((3, tc), lambda j: (0, j))
    w4 = pl.BlockSpec((4, tc), lambda j: (0, j))
    b1 = pl.BlockSpec((1, tc), lambda j: (0, j))
    return pl.pallas_call(
        body, name=name, grid=(f // tc,), in_specs=[col, col, col, w3, w3, b1, b1],
        out_specs=[col, col, w4, w4],
        out_shape=[jax.ShapeDtypeStruct((s, f), BF16), jax.ShapeDtypeStruct((s, f), BF16),
                   jax.ShapeDtypeStruct((4, f), F32), jax.ShapeDtypeStruct((4, f), F32)],
        compiler_params=_params(),
    )(up_a, up_b, dact, cw_a, cw_b, cb_a, cb_b)


def _ple_final(x2, ple, zp, target, g_final, *, name, tm=512):
    s, d = x2.shape
    tm = _tile(s, tm)

    def body(x_ref, ple_ref, zp_ref, t_ref, g_ref, dx_ref, dple_ref, dzp_ref, dg_ref, loss_ref):
        @pl.when(pl.program_id(0) == 0)
        def _():
            dg_ref[...] = jnp.zeros_like(dg_ref)
            loss_ref[...] = jnp.zeros_like(loss_ref)

        gp = _sigmoid(zp_ref[...].astype(F32))
        plev = ple_ref[...].astype(F32)
        x3 = x_ref[...] + plev * gp
        r = lax.rsqrt(jnp.mean(x3 * x3, axis=-1, keepdims=True) + EPS)
        xhat = x3 * r
        gv = g_ref[...]
        diff = xhat * gv - t_ref[...]
        loss_ref[...] += 0.5 * jnp.sum(jnp.mean(diff * diff, axis=-1, keepdims=True), axis=0, keepdims=True)
        dy = diff * (1.0 / d)
        dg_ref[...] += jnp.sum(dy * xhat, axis=0, keepdims=True)
        dyg = dy * gv
        dx3 = r * (dyg - xhat * jnp.mean(dyg * xhat, axis=-1, keepdims=True))
        dx_ref[...] = dx3
        dple_ref[...] = (dx3 * gp).astype(dple_ref.dtype)
        dzp_ref[...] = (dx3 * plev * gp * (1.0 - gp)).astype(dzp_ref.dtype)

    row = pl.BlockSpec((tm, d), lambda i: (i, 0))
    vec = pl.BlockSpec((1, d), lambda i: (0, 0))
    return pl.pallas_call(
        body, name=name, grid=(s // tm,), in_specs=[row, row, row, row, vec],
        out_specs=[row, row, row, vec, pl.BlockSpec((1, LANES), lambda i: (0, 0))],
        out_shape=[jax.ShapeDtypeStruct((s, d), F32), jax.ShapeDtypeStruct((s, d), BF16),
                   jax.ShapeDtypeStruct((s, d), BF16), jax.ShapeDtypeStruct((1, d), F32),
                   jax.ShapeDtypeStruct((1, LANES), F32)],
        compiler_params=_params(),
    )(x2, ple, zp, target, g_final)


def _device_step(x, p, target, w, get_w_in=None, get_w_rest=None, on_grads_ffn=None, on_grads_small=None,
                 on_grads_mix=None, on_after_dh=None):
    s = x.shape[0]
    g = {}
    w = dict(w)

    h = _rms_fwd(x, w["norm_mix_g"], name="rms_mix", dep=w.get("first_dep"))
    if get_w_in is not None:
        w.update(get_w_in(h))
    qkv = _mm(h, w["w_qkv"], mode="nn", out_dtype=BF16, name="proj_qkv", tm=1024)
    f = _mm(h, w["w_f"], mode="nn", out_dtype=F32, name="proj_f", tm=1024)

    cum_b, cum_t = _fox_cum(f, w["b_f"], name="fox_cum")
    cum_b = cum_b.reshape(HEAD_PAIRS, 2, s, LANES)
    cum_r = cum_t[:FOX_HEADS].reshape(HEAD_PAIRS, 2, s)
    b, o_t, lse = _attn_fwd_t(qkv, cum_b, cum_r, name="attn_fwd")

    dep = get_w_rest[0](b) if get_w_rest is not None else None
    z_uv = _mm(h, w["w_uv"], mode="nn", out_dtype=BF16, name="proj_uv", tm=1024, dep=dep)
    zg = _mm(h, w["w_g"], mode="nn", out_dtype=BF16, name="proj_gate", tm=1024, dep=dep)
    a = _gmlp_fwd(z_uv, w["gmlp_ln_g"], w["gmlp_ln_b"], w["gmlp_w_s"], w["gmlp_b_s_t"], name="gmlp_fwd")
    if get_w_rest is not None:
        w.update(get_w_rest[1]([a, zg]))

    ya = _mm(a, w["w_branch_a"], mode="nn", out_dtype=BF16, name="branch_a", tm=1024)
    yb = _mm(b, w["w_branch_b"], mode="nn", out_dtype=BF16, name="branch_b", tm=1024)
    merged = _merge_fwd(zg, ya, yb, name="merge_fwd")
    x1 = _mm(merged, w["w_out"], mode="nn", out_dtype=F32, name="proj_out", add=x, tm=1024)

    h2 = _rms_fwd(x1, w["norm_ffn_g"], name="rms_ffn")
    up_a = _mm(h2, w["w_up_a"], mode="nn", out_dtype=BF16, name="up_a", tm=1024, tn=D_FF // 2)
    up_b = _mm(h2, w["w_up_b"], mode="nn", out_dtype=BF16, name="up_b", tm=1024, tn=D_FF // 2)
    cw, cb = w["conv_w"], w["conv_b"]
    conv_args = (cw[:, :D_FF], cw[:, D_FF:], cb[:, :D_FF], cb[:, D_FF:])
    act = _conv_act_fwd(up_a, up_b, *conv_args, name="conv_act_fwd")
    x2 = _mm(act, w["w_down"], mode="nn", out_dtype=F32, name="down", add=x1, tm=512)

    h3 = _rms_fwd(x2, w["norm_ple_g"], name="rms_ple")
    ple = _mm(p, w["w_ple"], mode="nn", out_dtype=BF16, name="ple_proj", tm=1024)
    zp = _mm(h3, w["w_ple_gate"], mode="nn", out_dtype=BF16, name="ple_gate", tm=1024)
    dx3, dple, dzp, g["norm_final_g"], loss = _ple_final(x2, ple, zp, target, w["norm_final_g"], name="ple_final")

    g["w_ple"] = _mm(p, dple, mode="tn", out_dtype=BF16, name="dw_ple")
    g["w_ple_gate"] = _mm(h3, dzp, mode="tn", out_dtype=BF16, name="dw_ple_gate")
    dh3 = _mm(dzp, w["w_ple_gate"], mode="nt", out_dtype=BF16, name="dh3")
    dx2, dx2_b, g["norm_ple_g"] = _rms_bwd(x2, w["norm_ple_g"], dh3, dx3, name="rms_ple_bwd")

    g["w_down"] = _mm(act, dx2_b, mode="tn", out_dtype=BF16, name="dw_down", tm=D_FF // 2)
    dact = _mm(dx2_b, w["w_down"], mode="nt", out_dtype=BF16, name="dact", tn=D_FF // 2)
    dup_a, dup_b, dcw_a, dcw_b = _conv_act_bwd(up_a, up_b, dact, *conv_args, name="conv_act_bwd")
    g["conv_w"] = jnp.concatenate([dcw_a[:3], dcw_b[:3]], axis=1)
    g["conv_b"] = jnp.concatenate([dcw_a[3:], dcw_b[3:]], axis=1)
    g["w_up_a"] = _mm(h2, dup_a, mode="tn", out_dtype=BF16, name="dw_up_a", tn=D_FF // 2)
    g["w_up_b"] = _mm(h2, dup_b, mode="tn", out_dtype=BF16, name="dw_up_b", tn=D_FF // 2)
    dh2 = _mm_nt_sum([(dup_a, w["w_up_a"]), (dup_b, w["w_up_b"])], out_dtype=BF16, name="dh2")
    dx1, dx1_b, g["norm_ffn_g"] = _rms_bwd(x1, w["norm_ffn_g"], dh2, dx2, name="rms_ffn_bwd")

    g["w_out"] = _mm(merged, dx1_b, mode="tn", out_dtype=BF16, name="dw_out")
    dmerged = _mm(dx1_b, w["w_out"], mode="nt", out_dtype=BF16, name="dmerged")
    dzg, dya, dyb = _merge_bwd(dmerged, zg, ya, yb, name="merge_bwd")
    g["w_branch_a"] = _mm(a, dya, mode="tn", out_dtype=BF16, name="dw_branch_a")
    g["w_branch_b"] = _mm(b, dyb, mode="tn", out_dtype=BF16, name="dw_branch_b")
    dep = on_grads_ffn(g) if on_grads_ffn is not None else None
    da = _mm(dya, w["w_branch_a"], mode="nt", out_dtype=BF16, name="da", dep=dep)
    db = _mm(dyb, w["w_branch_b"], mode="nt", out_dtype=BF16, name="db")

    dz_uv, g["gmlp_w_s"], dbs_t, g["gmlp_ln_g"], g["gmlp_ln_b"] = _gmlp_bwd(
        z_uv, da, w["gmlp_ln_g"], w["gmlp_ln_b"], w["gmlp_w_s"], w["gmlp_b_s_t"], name="gmlp_bwd")
    g["gmlp_b_s"] = dbs_t[:, :GMLP_GROUPS].T
    dep = on_grads_small(g) if on_grads_small is not None else None

    dq, dk, dv, dcum_b = _attn_bwd_t(qkv, db, o_t, lse, cum_b, cum_r, name="attn_bwd", dep=dep)
    dcum_t = jnp.pad(dcum_b[..., 0].reshape(FOX_HEADS, s), ((0, LANES - FOX_HEADS), (0, 0)))
    df, g["b_f"] = _fox_dlogit(dcum_t, f, w["b_f"], name="fox_dlogit")
    dqkv = jnp.concatenate([dq, dk, dv], axis=1)

    g["w_uv"] = _mm(h, dz_uv, mode="tn", out_dtype=BF16, name="dw_uv")
    g["w_qkv"] = _mm(h, dqkv, mode="tn", out_dtype=BF16, name="dw_qkv")
    g["w_f"] = _mm(h, df, mode="tn", out_dtype=BF16, name="dw_f")
    g["w_g"] = _mm(h, dzg, mode="tn", out_dtype=BF16, name="dw_g")
    dep = on_grads_mix(g) if on_grads_mix is not None else None
    dh = _mm_nt_sum([(dz_uv, w["w_uv"]), (dqkv, w["w_qkv"]), (df, w["w_f"]), (dzg, w["w_g"])],
                    out_dtype=BF16, name="dh", dep=dep)
    dep = on_after_dh(dh) if on_after_dh is not None else None
    dx0, _, g["norm_mix_g"] = _rms_bwd(x, w["norm_mix_g"], dh, dx1, name="rms_mix_bwd", dep=dep)
    return loss, dx0, g


def _coords():
    return lax.axis_index("x"), lax.axis_index("y"), lax.axis_index("c")


def _other_chips(x, y):
    return [(1 - x, y), (x, 1 - y), (1 - x, 1 - y)]


def _remote(src, dst, send_sem, recv_sem, dev):
    return pltpu.make_async_remote_copy(src_ref=src, dst_ref=dst, send_sem=send_sem, recv_sem=recv_sem,
                                        device_id=dev, device_id_type=MESH)


_ANY = pl.BlockSpec(memory_space=pl.ANY)


def _pair_exchange(gs, *, name):
    n = len(gs)

    def body(*refs):
        ins, outs = refs[:n], refs[n:2 * n]
        send_sems, recv_sems = refs[2 * n:]
        x, y, c = _coords()
        copies = []
        for i in range(n):
            for j in range(N_CHIPS):
                cp = _remote(ins[i].at[j, 1 - c], outs[i].at[j], send_sems.at[i, j], recv_sems.at[i, j], (x, y, 1 - c))
                cp.start()
                copies.append(cp)
        for cp in copies:
            cp.wait()

    return pl.pallas_call(
        body, name=name, in_specs=[_ANY] * n, out_specs=[_ANY] * n,
        out_shape=[jax.ShapeDtypeStruct((N_CHIPS,) + a.shape[2:], a.dtype) for a in gs],
        scratch_shapes=[pltpu.SemaphoreType.DMA((n, N_CHIPS)), pltpu.SemaphoreType.DMA((n, N_CHIPS))],
        compiler_params=_params(),
    )(*gs)


def _pair_share(hs, *, name):
    n = len(hs)

    def body(*refs):
        ins, outs = refs[:n], refs[n:2 * n]
        send_sems, recv_sems = refs[2 * n:]
        x, y, c = _coords()
        copies = []
        for i in range(n):
            cp = _remote(ins[i], outs[i], send_sems.at[i], recv_sems.at[i], (x, y, 1 - c))
            cp.start()
            copies.append(cp)
        for cp in copies:
            cp.wait()

    return pl.pallas_call(
        body, name=name, in_specs=[_ANY] * n, out_specs=[_ANY] * n,
        out_shape=[jax.ShapeDtypeStruct(a.shape, a.dtype) for a in hs],
        scratch_shapes=[pltpu.SemaphoreType.DMA((n,)), pltpu.SemaphoreType.DMA((n,))],
        compiler_params=_params(),
    )(*hs)


def _all_exchange(vec, *, name):
    def body(v_ref, o_ref, send_sems, recv_sems, local_sem):
        x, y, c = _coords()
        me = 4 * x + 2 * y + c
        local = pltpu.make_async_copy(v_ref, o_ref.at[me], local_sem)
        local.start()
        copies = []
        k = 0
        for dx in (0, 1):
            for dy in (0, 1):
                for dc in (0, 1):
                    if dx or dy or dc:
                        peer = (1 - x if dx else x, 1 - y if dy else y, 1 - c if dc else c)
                        cp = _remote(v_ref, o_ref.at[me], send_sems.at[k], recv_sems.at[k], peer)
                        cp.start()
                        copies.append(cp)
                        k += 1
        for cp in copies:
            cp.wait()
        local.wait()

    return pl.pallas_call(
        body, name=name, in_specs=[_ANY], out_specs=_ANY,
        out_shape=jax.ShapeDtypeStruct((8,) + vec.shape, vec.dtype),
        scratch_shapes=[pltpu.SemaphoreType.DMA((7,)), pltpu.SemaphoreType.DMA((7,)), pltpu.SemaphoreType.DMA(())],
        compiler_params=_params(),
    )(vec)


_HBM = pl.BlockSpec(memory_space=pltpu.HBM)
_SEM = pl.BlockSpec(memory_space=pltpu.SEMAPHORE)
_EFFECT = pltpu.SideEffectType.DATAFLOW_SIDE_EFFECTING


def _copies_start(srcs, lands, plan, n_copies, *, name, after=()):
    ns, n = len(srcs), len(srcs) + len(lands)
    na = len(after)

    def body(*refs):
        send_sems, recv_sems = refs[n + na], refs[n + na + 1]
        token = refs[-1]
        for k, (src, dst, dev) in enumerate(plan(refs[:ns], refs[ns:n])):
            _remote(src, dst, send_sems.at[k], recv_sems.at[k], dev).start()
        token[...] = jnp.zeros_like(token)

    arrays = list(srcs) + list(lands)
    outs = pl.pallas_call(
        body, name=name,
        out_shape=(pltpu.SemaphoreType.DMA((n_copies,)), pltpu.SemaphoreType.DMA((n_copies,)),
                   *[pltpu.HBM(a.shape, a.dtype) for a in arrays], jax.ShapeDtypeStruct((8, LANES), F32)),
        in_specs=[_HBM] * n + [_ANY] * na,
        out_specs=(_SEM, _SEM, *[_HBM] * n, pl.BlockSpec(memory_space=pltpu.VMEM)),
        input_output_aliases={i: 2 + i for i in range(n)},
        compiler_params=pltpu.CompilerParams(has_side_effects=_EFFECT),
    )(*[pltpu.with_memory_space_constraint(a, pltpu.HBM) for a in arrays], *after)
    return outs[0], outs[1], list(outs[2:2 + ns]), list(outs[2 + ns:2 + n]), outs[-1]


def _copies_wait(send_sems, recv_sems, srcs, lands, plan, first, after, *, name):
    ns, n = len(srcs), len(srcs) + len(lands)

    def body(*refs):
        send, recv = refs[n], refs[n + 1]
        for k, (src, dst, dev) in enumerate(plan(refs[:ns], refs[ns:n])):
            cp = _remote(src, dst, send.at[first + k], recv.at[first + k], dev)
            cp.wait_send()
            cp.wait_recv()

    arrays = list(srcs) + list(lands)
    outs = pl.pallas_call(
        body, name=name, out_shape=tuple(pltpu.HBM(a.shape, a.dtype) for a in arrays),
        in_specs=[_HBM] * n + [_SEM, _SEM] + [_ANY] * len(after), out_specs=tuple([_HBM] * n),
        input_output_aliases={i: i for i in range(n)},
        compiler_params=pltpu.CompilerParams(has_side_effects=_EFFECT),
    )(*arrays, send_sems, recv_sems, *after)
    return list(outs[:ns]), list(outs[ns:])


def _gather_plan(halved):
    def plan(srcs, lands):
        x, y, c = _coords()
        me = 2 * x + y
        out = []
        for i, (src, land) in enumerate(zip(srcs, lands)):
            if halved[i]:
                h = src.shape[0] // 2
                rows = pl.ds(pl.multiple_of(c * h, 16), h)
                src, dst = src.at[rows], land.at[me, rows]
            else:
                dst = land.at[me]
            out += [(src, dst, (cx, cy, c)) for cx, cy in _other_chips(x, y)]
        return out
    return plan


def _forward_halves(lands, *, name):
    n = len(lands)

    def body(*refs):
        ins, outs = refs[:n], refs[n:2 * n]
        send_sems, recv_sems = refs[2 * n:]
        x, y, c = _coords()
        copies = []
        for i in range(n):
            h = ins[i].shape[1] // 2
            rows = pl.ds(pl.multiple_of(c * h, 16), h)
            for k, (cx, cy) in enumerate(_other_chips(x, y)):
                cp = _remote(ins[i].at[2 * cx + cy, rows], outs[i].at[2 * cx + cy, rows],
                             send_sems.at[i, k], recv_sems.at[i, k], (x, y, 1 - c))
                cp.start()
                copies.append(cp)
        for cp in copies:
            cp.wait()

    return pl.pallas_call(
        body, name=name, in_specs=[_ANY] * n, out_specs=[_ANY] * n,
        out_shape=[jax.ShapeDtypeStruct(a.shape, a.dtype) for a in lands],
        input_output_aliases={i: i for i in range(n)},
        scratch_shapes=[pltpu.SemaphoreType.DMA((n, 3)), pltpu.SemaphoreType.DMA((n, 3))],
        compiler_params=_params(),
    )(*lands)


def _forward_plan(srcs, lands):
    x, y, c = _coords()
    out = []
    for land in lands:
        h = land.shape[1] // 2
        rows = pl.ds(pl.multiple_of(c * h, 16), h)
        for cx, cy in _other_chips(x, y):
            view = land.at[2 * cx + cy, rows]
            out.append((view, view, (x, y, 1 - c)))
    return out


def _share_plan(srcs, lands):
    x, y, c = _coords()
    return [(src, land, (x, y, 1 - c)) for src, land in zip(srcs, lands)]


def _pair_plan(srcs, lands):
    x, y, c = _coords()
    out = []
    for src, land in zip(srcs, lands):
        out += [(src.at[j, 1 - c], land.at[j], (x, y, 1 - c)) for j in range(N_CHIPS)]
    return out


def _all_plan(srcs, lands):
    x, y, c = _coords()
    me = 4 * x + 2 * y + c
    out = []
    for src, land in zip(srcs, lands):
        for dx in (0, 1):
            for dy in (0, 1):
                for dc in (0, 1):
                    if dx or dy or dc:
                        out.append((src, land.at[me], (1 - x if dx else x, 1 - y if dy else y, 1 - c if dc else c)))
    return out


def _chip_plan(srcs, lands):
    x, y, c = _coords()
    me = 2 * x + y
    out = []
    for src, land in zip(srcs, lands):
        out += [(src.at[2 * cx + cy], land.at[me], (cx, cy, c)) for cx, cy in _other_chips(x, y)]
    return out


ROW_BLOCK_BYTES = 2 * 1024 * 1024


def _rtile(r, pref, mult, row_bytes=None):
    if row_bytes is not None:
        pref = max(pref, ROW_BLOCK_BYTES // row_bytes)
    t = (min(r, pref) // mult) * mult
    while t >= mult:
        if r % t == 0:
            return t
        t -= mult
    return r


def _pair_add(g, recv, core, *, name):
    _, _, r2, cols = g.shape
    tr = _rtile(r2, 256, 16, row_bytes=2 * cols)

    def body(c_ref, g_ref, r_ref, o_ref):
        o_ref[...] = (g_ref[...].astype(F32) + r_ref[...].astype(F32)).astype(o_ref.dtype)

    blk = pl.BlockSpec((None, tr, cols), lambda j, i, c_ref: (j, i, 0))
    return pl.pallas_call(
        body, name=name,
        grid_spec=pltpu.PrefetchScalarGridSpec(
            num_scalar_prefetch=1, grid=(N_CHIPS, r2 // tr),
            in_specs=[pl.BlockSpec((None, None, tr, cols), lambda j, i, c_ref: (j, c_ref[0], i, 0)), blk],
            out_specs=blk),
        out_shape=jax.ShapeDtypeStruct(recv.shape, recv.dtype), compiler_params=_params(),
    )(core, g, recv)


def _sum_slots(a, out_dtype, *, name):
    n, r, cols = a.shape
    whole = n * r * cols * a.dtype.itemsize <= 4 * ROW_BLOCK_BYTES
    tr = r if whole else _rtile(r, 256, 16)

    def body(a_ref, o_ref):
        acc = a_ref[0].astype(F32)
        for j in range(1, n):
            acc = acc + a_ref[j].astype(F32)
        o_ref[...] = acc.astype(o_ref.dtype)

    return pl.pallas_call(
        body, name=name, grid=(r // tr,),
        in_specs=[pl.BlockSpec((n, tr, cols), lambda i: (0, i, 0))],
        out_specs=pl.BlockSpec((tr, cols), lambda i: (i, 0)),
        out_shape=jax.ShapeDtypeStruct((r, cols), out_dtype), compiler_params=_params(),
    )(a)


def _chip_sum(own, recv, chip, *, name):
    _, r2, cols = own.shape
    tr = _rtile(r2, 256, 16, row_bytes=2 * cols)

    def body(chip_ref, own_ref, *rest):
        o_ref = rest[-1]
        acc = None
        for j in range(N_CHIPS):
            term = jnp.where(chip_ref[0] == j, own_ref[...], rest[j][...]).astype(F32)
            acc = term if acc is None else acc + term
        o_ref[...] = acc

    def slot(j):
        return pl.BlockSpec((None, tr, cols),
                            lambda i, chip_ref: (jnp.where(chip_ref[0] == j, (j + 1) % N_CHIPS, j), i, 0))

    return pl.pallas_call(
        body, name=name,
        grid_spec=pltpu.PrefetchScalarGridSpec(
            num_scalar_prefetch=1, grid=(r2 // tr,),
            in_specs=[pl.BlockSpec((None, tr, cols), lambda i, chip_ref: (chip_ref[0], i, 0))]
                     + [slot(j) for j in range(N_CHIPS)],
            out_specs=pl.BlockSpec((tr, cols), lambda i, chip_ref: (i, 0))),
        out_shape=jax.ShapeDtypeStruct((r2, cols), F32), compiler_params=_params(),
    )(chip, own, *([recv] * N_CHIPS))


def _adam_update(w, gv, m, v):
    c1 = 1.0 / (1.0 - ADAM_B1 ** ADAM_STEP)
    c2 = 1.0 / (1.0 - ADAM_B2 ** ADAM_STEP)
    nm = ADAM_B1 * m + (1.0 - ADAM_B1) * gv
    nv = ADAM_B2 * v + (1.0 - ADAM_B2) * gv * gv
    return -ADAM_LR * ((nm * c1) / (jnp.sqrt(nv * c2) + ADAM_EPS) + ADAM_WD * w), nm, nv


def _adamw_halves(w, g_mine, g_other, m, v, core, *, name):
    r, cols = w.shape
    r2 = r // 2
    tr = _rtile(r2, 256, 8, row_bytes=4 * cols)
    nt = r2 // tr

    def body(core_ref, w_ref, gm_ref, go_ref, m_ref, v_ref, g_ref, d_ref, nm_ref, nv_ref):
        gv = jnp.where(pl.program_id(0) == core_ref[0], gm_ref[...], go_ref[...])
        g_ref[...] = gv
        d_ref[...], nm_ref[...], nv_ref[...] = _adam_update(w_ref[...], gv, m_ref[...], v_ref[...])

    full = pl.BlockSpec((tr, cols), lambda hf, i, core_ref: (hf * nt + i, 0))
    half = pl.BlockSpec((tr, cols), lambda hf, i, core_ref: (i, 0))
    shape = jax.ShapeDtypeStruct((r, cols), F32)
    return pl.pallas_call(
        body, name=name,
        grid_spec=pltpu.PrefetchScalarGridSpec(
            num_scalar_prefetch=1, grid=(2, nt), in_specs=[full, half, half, full, full], out_specs=[full] * 4),
        out_shape=[shape] * 4, compiler_params=_params(),
    )(core, w, g_mine, g_other, m, v)


def _adamw(w, g, m, v, *, name, rows=256):
    r, cols = w.shape
    tr = _rtile(r, rows, 8)

    def body(w_ref, g_ref, m_ref, v_ref, d_ref, nm_ref, nv_ref):
        d_ref[...], nm_ref[...], nv_ref[...] = _adam_update(w_ref[...], g_ref[...], m_ref[...], v_ref[...])

    blk = pl.BlockSpec((tr, cols), lambda i: (i, 0))
    shape = jax.ShapeDtypeStruct((r, cols), F32)
    return pl.pallas_call(
        body, name=name, grid=(r // tr,), in_specs=[blk] * 4, out_specs=[blk] * 3,
        out_shape=[shape] * 3, compiler_params=_params(),
    )(w, g, m, v)


_BIG = (("w_in", 1), ("w_branch_a", 0), ("w_branch_b", 0), ("w_out", 0), ("w_up", 1), ("w_down", 0),
        ("w_ple", 1), ("w_ple_gate", 0))
_SMALL = ("gmlp_ln_g", "gmlp_ln_b", "gmlp_w_s", "gmlp_b_s", "norm_ffn_g", "conv_b", "norm_ple_g", "norm_final_g",
          "b_f", "norm_mix_g")
N_LATE = 2
_WEIGHTS = ("norm_mix_g", "w_in", "b_f", "gmlp_ln_g", "gmlp_ln_b", "gmlp_w_s", "gmlp_b_s", "w_branch_a",
            "w_branch_b", "w_out", "norm_ffn_g", "w_up", "conv_w", "conv_b", "w_down", "norm_ple_g", "w_ple",
            "w_ple_gate", "norm_final_g")
_PACK_ROWS = 8


def _pack(arrays):
    parts = []
    for a in arrays:
        flat = a.reshape(-1)
        unit = _PACK_ROWS * LANES
        flat = jnp.pad(flat, (0, (-flat.shape[0]) % unit))
        parts.append(flat.reshape(-1, LANES))
    return jnp.concatenate(parts, axis=0)


def _unpack(packed, shapes):
    out, row = [], 0
    for shp in shapes:
        size = math.prod(shp)
        rows = -(-size // (_PACK_ROWS * LANES)) * _PACK_ROWS
        out.append(packed[row:row + rows].reshape(-1)[:size].reshape(shp))
        row += rows
    return out


def _take_cols(parts, lo, hi):
    out, start = [], 0
    for a in parts:
        width = a.shape[1]
        a0, a1 = max(lo, start) - start, min(hi, start + width) - start
        if a1 > a0:
            out.append(a if (a0, a1) == (0, width) else a[:, a0:a1])
        start += width
    return out[0] if len(out) == 1 else jnp.concatenate(out, axis=1)


def _assemble(gathered, axis):
    n, r, cols = gathered.shape
    if axis == 0:
        return gathered.reshape(n * r, cols)
    return _take_cols([gathered[j] for j in range(n)], 0, n * cols)


def _to_chunks(parts, axis):
    rows, total = parts[0].shape[0], sum(a.shape[1] for a in parts)
    if axis == 0:
        r, cols = rows // N_CHIPS, total
        chunks = _take_cols(parts, 0, total).reshape(N_CHIPS, r, cols)
    else:
        r, cols = rows, total // N_CHIPS
        chunks = jnp.stack([_take_cols(parts, j * cols, (j + 1) * cols) for j in range(N_CHIPS)])
    return chunks.reshape(N_CHIPS, 2, r // 2, cols)


def kernel(x, p, norm_mix_g, w_in, b_f, gmlp_ln_g, gmlp_ln_b, gmlp_w_s, gmlp_b_s, w_branch_a, w_branch_b, w_out, norm_ffn_g, w_up, conv_w, conv_b, w_down, norm_ple_g, w_ple, w_ple_gate, norm_final_g, loss_target, m_norm_mix_g, m_w_in, m_b_f, m_gmlp_ln_g, m_gmlp_ln_b, m_gmlp_w_s, m_gmlp_b_s, m_w_branch_a, m_w_branch_b, m_w_out, m_norm_ffn_g, m_w_up, m_conv_w, m_conv_b, m_w_down, m_norm_ple_g, m_w_ple, m_w_ple_gate, m_norm_final_g, v_norm_mix_g, v_w_in, v_b_f, v_gmlp_ln_g, v_gmlp_ln_b, v_gmlp_w_s, v_gmlp_b_s, v_w_branch_a, v_w_branch_b, v_w_out, v_norm_ffn_g, v_w_up, v_conv_w, v_conv_b, v_w_down, v_norm_ple_g, v_w_ple, v_w_ple_gate, v_norm_final_g):
    args = dict(locals())
    wt = {n: args[n] for n in _WEIGHTS}
    mom = {n: args["m_" + n] for n in _WEIGHTS}
    var = {n: args["v_" + n] for n in _WEIGHTS}
    chip = 2 * lax.axis_index("x") + lax.axis_index("y")
    core = lax.axis_index("c").astype(jnp.int32).reshape(1)

    chip1 = chip.astype(jnp.int32).reshape(1)
    device = 2 * chip + lax.axis_index("c")
    axis_of = dict(_BIG)
    names = [n for n, _ in _BIG]
    put_mine = lambda land, mine: lax.dynamic_update_index_in_dim(land, mine, chip, 0)

    shard_in = w_in[0].astype(BF16)
    sems_in = _copies_start([shard_in], [lax.empty((N_CHIPS,) + shard_in.shape, BF16)], _gather_plan([True]), 3,
                            name="gather_start_in")
    _, wt["w_in"], mom["w_in"], var["w_in"] = lax.optimization_barrier((sems_in[4], w_in, m_w_in, v_w_in))
    shards = [wt[n][0].astype(BF16) for n in names[1:]] + [conv_w[0]]
    halved = [True] * len(names[1:]) + [False]
    lands = [lax.empty((N_CHIPS,) + a.shape, a.dtype) for a in shards]
    send_sems, recv_sems, srcs, lands, rest_token = _copies_start(
        shards, lands, _gather_plan(halved), 3 * len(shards), name="gather_start_rest", after=[sems_in[4]])
    o1 = 2 * GMLP_WIDTH
    o2 = o1 + 3 * FOX_WIDTH
    o3 = o2 + FOX_HEADS
    fpad = ((0, 0), (0, LANES - FOX_HEADS))
    w = {
        "conv_b": conv_b, "norm_mix_g": norm_mix_g, "norm_ffn_g": norm_ffn_g, "norm_ple_g": norm_ple_g,
        "norm_final_g": norm_final_g.reshape(1, D_MODEL), "b_f": jnp.pad(b_f, fpad),
        "gmlp_ln_g": gmlp_ln_g, "gmlp_ln_b": gmlp_ln_b, "gmlp_w_s": gmlp_w_s[0],
        "gmlp_b_s_t": jnp.pad(gmlp_b_s[0].T, ((0, 0), (0, LANES - GMLP_GROUPS))),
        "first_dep": rest_token,
    }

    def get_w_in(after):
        early = [a.reshape(a.shape[-2:]) for a in (wt["w_in"], mom["w_in"], var["w_in"])]
        _, got = _copies_wait(sems_in[0], sems_in[1], sems_in[2], sems_in[3], _gather_plan([True]), 0,
                              [after] + early, name="gather_wait_in")
        got = _forward_halves(got, name="gather_forward_in")
        slots = put_mine(got[0], shard_in)
        slots = [slots[j] for j in range(N_CHIPS)]
        return {"w_uv": _take_cols(slots, 0, o1), "w_qkv": _take_cols(slots, o1, o2),
                "w_f": jnp.pad(_take_cols(slots, o2, o3), fpad), "w_g": _take_cols(slots, o3, o3 + 2 * D_MODEL)}

    def start_w_rest(after):
        _, got = _copies_wait(send_sems, recv_sems, srcs, lands, _gather_plan(halved), 0, [after],
                              name="gather_wait_rest")
        ssem, rsem, _, fwd, token = _copies_start([], got[:-1], _forward_plan, 3 * len(got[:-1]),
                                                  name="gather_forward_start")
        pending["forward"] = (ssem, rsem, fwd, got[-1])
        return token

    def get_w_rest(after):
        ssem, rsem, fwd, whole = pending["forward"]
        _, fwd = _copies_wait(ssem, rsem, [], fwd, _forward_plan, 0, after, name="gather_forward_wait")
        got = fwd + [whole]
        slots = {n: put_mine(got[i], shards[i]) for i, n in enumerate(names[1:])}
        full = {n: _assemble(slots[n], axis_of[n]) for n in names[1:] if n != "w_up"}
        up = [slots["w_up"][j] for j in range(N_CHIPS)]
        return {"w_branch_a": full["w_branch_a"], "w_branch_b": full["w_branch_b"], "w_out": full["w_out"],
                "w_up_a": _take_cols(up, 0, D_FF), "w_up_b": _take_cols(up, D_FF, 2 * D_FF),
                "w_down": full["w_down"], "w_ple": full["w_ple"], "w_ple_gate": full["w_ple_gate"],
                "conv_w": _assemble(put_mine(got[-1], shards[-1]), 1)}

    grads, delta, new_m, new_v = {}, {}, {}, {}
    pending = {}

    def to_chunks(n, gr):
        return _to_chunks(gr if isinstance(gr, list) else [gr], axis_of[n])

    def pair_start(group, gfull, tag):
        chunks = [to_chunks(n, gfull[n]) for n in group]
        empty = [lax.empty((N_CHIPS,) + a.shape[2:], a.dtype) for a in chunks]
        ssem, rsem, own, recv, token = _copies_start(chunks, empty, _pair_plan, N_CHIPS * len(group),
                                                     name="grad_pair_start_" + tag)
        pending["pair_" + tag] = (ssem, rsem, own, recv)
        return token

    def reduce_start(group, gfull, tag, after=None):
        if after is None:
            chunks = [to_chunks(n, gfull[n]) for n in group]
            from_sibling = _pair_exchange(chunks, name="grad_pair_exchange_" + tag)
        else:
            ssem, rsem, own, recv = pending["pair_" + tag]
            chunks, from_sibling = _copies_wait(ssem, rsem, own, recv, _pair_plan, 0, after,
                                                name="grad_pair_wait_" + tag)
        pair_sums = [_pair_add(chunks[i], from_sibling[i], core, name="grad_pair_add_" + n) for i, n in enumerate(group)]
        empty = [lax.empty(a.shape, a.dtype) for a in pair_sums]
        ssem, rsem, own, recv, token = _copies_start(pair_sums, empty, _chip_plan, 3 * len(group),
                                                     name="grad_chip_start_" + tag)
        pending[tag] = (ssem, rsem, own, recv)
        return token

    def reduce_sum(group, tag, after):
        ssem, rsem, own, recv = pending[tag]
        own, recv = _copies_wait(ssem, rsem, own, recv, _chip_plan, 0, after, name="grad_chip_wait_" + tag)
        halves = [_chip_sum(own[i], recv[i], chip1, name="grad_chip_sum_" + n) for i, n in enumerate(group)]
        empty = [lax.empty(a.shape, a.dtype) for a in halves]
        ssem, rsem, halves, other, token = _copies_start(halves, empty, _share_plan, len(group),
                                                        name="grad_share_start_" + tag)
        pending["share_" + tag] = (ssem, rsem, halves, other)
        return token

    def reduce_update(group, tag, after):
        ssem, rsem, halves, other = pending["share_" + tag]
        halves, other_halves = _copies_wait(ssem, rsem, halves, other, _share_plan, 0, after,
                                            name="grad_share_wait_" + tag)
        for i, n in enumerate(group):
            shp = wt[n].shape
            outs = _adamw_halves(wt[n].reshape(shp[-2:]), halves[i], other_halves[i], mom[n].reshape(shp[-2:]),
                                 var[n].reshape(shp[-2:]), core, name="adamw_" + n)
            grads[n], delta[n], new_m[n], new_v[n] = (o.reshape(shp) for o in outs)
        return new_v[group[-1]]

    def reduce_finish(group, tag, after):
        ssem, rsem, own, recv = pending[tag]
        own, recv = _copies_wait(ssem, rsem, own, recv, _chip_plan, 0, after, name="grad_chip_wait_" + tag)
        halves = [_chip_sum(own[i], recv[i], chip1, name="grad_chip_sum_" + n) for i, n in enumerate(group)]
        other_halves = _pair_share(halves, name="grad_pair_share_" + tag)
        for i, n in enumerate(group):
            shp = wt[n].shape
            outs = _adamw_halves(wt[n].reshape(shp[-2:]), halves[i], other_halves[i], mom[n].reshape(shp[-2:]),
                                 var[n].reshape(shp[-2:]), core, name="adamw_" + n)
            grads[n], delta[n], new_m[n], new_v[n] = (o.reshape(shp) for o in outs)
        return new_v[group[-1]]

    ffn_group = ("w_up", "w_down", "w_ple", "w_ple_gate", "w_branch_a", "w_branch_b", "w_out")
    mix_group = ("w_in",)

    def on_grads_ffn(g):
        gfull = dict(g)
        gfull["w_up"] = [g["w_up_a"], g["w_up_b"]]
        return pair_start(ffn_group, gfull, "ffn")

    def on_grads_small(g):
        chip_token = reduce_start(ffn_group, None, "ffn", after=[g["gmlp_w_s"]])
        vec = _pack([g[n] for n in _SMALL[:-N_LATE]] + [g["conv_w"]])
        ssem, rsem, own, recv, token = _copies_start(
            [vec], [lax.empty((8,) + vec.shape, F32)], _all_plan, 7, name="small_start", after=[chip_token])
        pending["small"] = (ssem, rsem, own, recv)
        return token

    def on_grads_mix(g):
        gfull = dict(g)
        gfull["w_in"] = [g["w_uv"], g["w_qkv"], g["w_f"][:, :FOX_HEADS], g["w_g"]]
        return reduce_start(mix_group, gfull, "mix")

    def on_after_dh(dh):
        return reduce_sum(ffn_group, "ffn", [dh])

    loss, grad_x, g = _device_step(x[0], p[0, 0], loss_target[0], w, get_w_in, (start_w_rest, get_w_rest), on_grads_ffn,
                                   on_grads_small, on_grads_mix, on_after_dh)

    ffn_done = reduce_update(ffn_group, "ffn", [grad_x])
    mix_done = reduce_finish(mix_group, "mix", [ffn_done])
    ssem, rsem, own, recv = pending["small"]
    own, recv = _copies_wait(ssem, rsem, own, recv, _all_plan, 0, [mix_done], name="small_wait")
    vec_early = _sum_slots(lax.dynamic_update_index_in_dim(recv[0], own[0], device, 0), F32, name="small_sum")
    vec_late = _pack([g["b_f"][:, :FOX_HEADS], g["norm_mix_g"]])
    vec_late = _sum_slots(_all_exchange(vec_late, name="small_exchange_late"), F32, name="small_sum_late")
    early_rows = _pack([wt[n] for n in _SMALL[:-N_LATE]]).shape[0]
    vec = jnp.concatenate([vec_early[:early_rows], vec_late], axis=0)
    for n, a in zip(_SMALL, _unpack(vec, [wt[n].shape for n in _SMALL])):
        grads[n] = a
    conv_w_grad = _unpack(vec_early[early_rows:], [(3, 2 * D_FF)])[0]
    grads["conv_w"] = lax.dynamic_slice_in_dim(conv_w_grad, chip * conv_w.shape[2], conv_w.shape[2], axis=1).reshape(conv_w.shape)

    shp = conv_w.shape
    outs = _adamw(conv_w.reshape(shp[-2:]), grads["conv_w"].reshape(shp[-2:]), m_conv_w.reshape(shp[-2:]),
                  v_conv_w.reshape(shp[-2:]), name="adamw_conv_w")
    delta["conv_w"], new_m["conv_w"], new_v["conv_w"] = (o.reshape(shp) for o in outs)
    outs = _adamw(_pack([wt[n] for n in _SMALL]), vec, _pack([mom[n] for n in _SMALL]),
                  _pack([var[n] for n in _SMALL]), name="adamw_small", rows=2048)
    for d, o in zip((delta, new_m, new_v), outs):
        for n, a in zip(_SMALL, _unpack(o, [wt[n].shape for n in _SMALL])):
            d[n] = a

    total_loss = lax.psum(loss[0, 0], ("x", "y", "c"))
    return (total_loss, grad_x.reshape(x.shape), *[grads[n] for n in _WEIGHTS], *[delta[n] for n in _WEIGHTS],
            *[new_m[n] for n in _WEIGHTS], *[new_v[n] for n in _WEIGHTS])
```

```python
import math

import jax
import jax.numpy as jnp
from jax import lax
from jax.experimental import pallas as pl
from jax.experimental.pallas import tpu as pltpu

F32 = jnp.float32
BF16 = jnp.bfloat16

D_MODEL = 1024
EPS = 1e-6
CHUNK = 64
GMLP_GROUPS = 8
GMLP_BLOCK = 128
GMLP_WIDTH = 1024
FOX_HEADS = 16
FOX_HEAD_DIM = 64
FOX_WIDTH = 1024
HEAD_PAIRS = FOX_HEADS // 2
ATT_BLOCK = 128
D_FF = 2816
PLE_DIM = 256
LANES = 128
BF16_TILE_ROWS = 16
N_CHIPS = 4

ADAM_LR = 0.001
ADAM_B1 = 0.9
ADAM_B2 = 0.999
ADAM_EPS = 1e-08
ADAM_WD = 0.01
ADAM_STEP = 10

VMEM_LIMIT = 56 * 1024 * 1024
MESH = pl.DeviceIdType.MESH

_NN = (((1,), (0,)), ((), ()))
_NT = (((1,), (1,)), ((), ()))
_TN = (((0,), (0,)), ((), ()))


def _params(**kw):
    return pltpu.CompilerParams(vmem_limit_bytes=VMEM_LIMIT, **kw)


def _tile(dim, pref):
    if dim <= pref:
        return dim
    t = (pref // LANES) * LANES
    while t >= LANES:
        if dim % t == 0:
            return t
        t -= LANES
    return dim


def _dot(a, b, dn):
    return lax.dot_general(a.astype(BF16), b.astype(BF16), dn, preferred_element_type=F32)


def _gelu(x):
    c = math.sqrt(2.0 / math.pi)
    t = jnp.tanh(c * (x + 0.044715 * x * x * x))
    return 0.5 * x * (1.0 + t)


def _gelu_and_grad(x):
    c = math.sqrt(2.0 / math.pi)
    x2 = x * x
    t = jnp.tanh(c * (x + 0.044715 * x2 * x))
    g = 0.5 * x * (1.0 + t)
    dg = 0.5 * (1.0 + t) + 0.5 * x * (1.0 - t * t) * c * (1.0 + 3.0 * 0.044715 * x2)
    return g, dg


def _sigmoid(x):
    return 1.0 / (1.0 + jnp.exp(-x))


def _mm(a, b, *, mode, out_dtype, name, add=None, tm=1024, tn=1024, dep=None):
    if mode == "nn":
        m, k = a.shape
        k2, n = b.shape
    elif mode == "nt":
        m, k = a.shape
        n, k2 = b.shape
    else:
        k, m = a.shape
        k2, n = b.shape
    assert k == k2, (name, a.shape, b.shape)
    tm = _tile(m, tm)
    tn = _tile(n, tn)
    dn = {"nn": _NN, "nt": _NT, "tn": _TN}[mode]

    def body(a_ref, b_ref, *rest):
        o_ref = rest[-1]
        acc = _dot(a_ref[...], b_ref[...], dn)
        if add is not None:
            acc = acc + rest[0][...].astype(F32)
        o_ref[...] = acc.astype(o_ref.dtype)

    a_spec = pl.BlockSpec((k, tm), lambda i, j: (0, i)) if mode == "tn" else pl.BlockSpec((tm, k), lambda i, j: (i, 0))
    b_spec = pl.BlockSpec((tn, k), lambda i, j: (j, 0)) if mode == "nt" else pl.BlockSpec((k, tn), lambda i, j: (0, j))
    o_spec = pl.BlockSpec((tm, tn), lambda i, j: (i, j))
    in_specs = [a_spec, b_spec]
    args = [a, b]
    if add is not None:
        in_specs.append(o_spec)
        args.append(add)
    if dep is not None:
        in_specs.append(pl.BlockSpec(memory_space=pl.ANY))
        args.append(dep)
    return pl.pallas_call(
        body, name=name, grid=(m // tm, n // tn), in_specs=in_specs, out_specs=o_spec,
        out_shape=jax.ShapeDtypeStruct((m, n), out_dtype), compiler_params=_params(),
    )(*args)


def _mm_nt_sum(pairs, *, out_dtype, name, tm=512, dep=None):
    m, n = pairs[0][0].shape[0], pairs[0][1].shape[0]
    tm = _tile(m, tm)
    np_ = len(pairs)

    def body(*refs):
        o_ref = refs[-1]
        acc = None
        for p in range(np_):
            part = _dot(refs[2 * p][...], refs[2 * p + 1][...], _NT)
            acc = part if acc is None else acc + part
        o_ref[...] = acc.astype(o_ref.dtype)

    in_specs, args = [], []
    for a, b in pairs:
        assert a.shape[0] == m and b.shape[0] == n and a.shape[1] == b.shape[1], (name, a.shape, b.shape)
        in_specs += [pl.BlockSpec((tm, a.shape[1]), lambda i: (i, 0)), pl.BlockSpec(b.shape, lambda i: (0, 0))]
        args += [a, b]
    if dep is not None:
        in_specs.append(pl.BlockSpec(memory_space=pl.ANY))
        args.append(dep)
    return pl.pallas_call(
        body, name=name, grid=(m // tm,), in_specs=in_specs, out_specs=pl.BlockSpec((tm, n), lambda i: (i, 0)),
        out_shape=jax.ShapeDtypeStruct((m, n), out_dtype), compiler_params=_params(),
    )(*args)


def _rms_fwd(x, g, *, name, tm=512, dep=None):
    s, d = x.shape
    tm = _tile(s, tm)

    def body(x_ref, g_ref, *rest):
        h_ref = rest[-1]
        xv = x_ref[...]
        r = lax.rsqrt(jnp.mean(xv * xv, axis=-1, keepdims=True) + EPS)
        h_ref[...] = (xv * r * g_ref[...]).astype(h_ref.dtype)

    deps = [] if dep is None else [dep]
    return pl.pallas_call(
        body, name=name, grid=(s // tm,),
        in_specs=[pl.BlockSpec((tm, d), lambda i: (i, 0)), pl.BlockSpec((1, d), lambda i: (0, 0))]
                 + [pl.BlockSpec(memory_space=pl.ANY)] * len(deps),
        out_specs=pl.BlockSpec((tm, d), lambda i: (i, 0)),
        out_shape=jax.ShapeDtypeStruct((s, d), BF16), compiler_params=_params(),
    )(x, g, *deps)


def _rms_bwd(x, g, dh, dres, *, name, tm=512, dep=None):
    s, d = x.shape
    tm = _tile(s, tm)
    deps = [] if dep is None else [dep]

    def body(x_ref, g_ref, dh_ref, dres_ref, *rest):
        dx_ref, dxb_ref, dg_ref = rest[len(deps):]
        xv = x_ref[...]
        r = lax.rsqrt(jnp.mean(xv * xv, axis=-1, keepdims=True) + EPS)
        xhat = xv * r
        dhv = dh_ref[...].astype(F32)
        dyg = dhv * g_ref[...]
        dx = dres_ref[...] + r * (dyg - xhat * jnp.mean(dyg * xhat, axis=-1, keepdims=True))
        dx_ref[...] = dx
        dxb_ref[...] = dx.astype(dxb_ref.dtype)

        @pl.when(pl.program_id(0) == 0)
        def _():
            dg_ref[...] = jnp.zeros_like(dg_ref)

        dg_ref[...] += jnp.sum(dhv * xhat, axis=0, keepdims=True)

    row = pl.BlockSpec((tm, d), lambda i: (i, 0))
    vec = pl.BlockSpec((1, d), lambda i: (0, 0))
    return pl.pallas_call(
        body, name=name, grid=(s // tm,),
        in_specs=[row, vec, row, row] + [pl.BlockSpec(memory_space=pl.ANY)] * len(deps), out_specs=[row, row, vec],
        out_shape=[jax.ShapeDtypeStruct((s, d), F32), jax.ShapeDtypeStruct((s, d), BF16),
                   jax.ShapeDtypeStruct((1, d), F32)],
        compiler_params=_params(),
    )(x, g, dh, dres, *deps)


def _gmlp_mask():
    t = lax.broadcasted_iota(jnp.int32, (GMLP_BLOCK, GMLP_BLOCK), 0)
    s_ = lax.broadcasted_iota(jnp.int32, (GMLP_BLOCK, GMLP_BLOCK), 1)
    return (s_ // CHUNK) <= (t // CHUNK)


def _gmlp_norm(zv, ln_g, ln_b):
    vv, dvv = _gelu_and_grad(zv)
    mu = jnp.mean(vv, axis=-1, keepdims=True)
    xc = vv - mu
    rstd = lax.rsqrt(jnp.mean(xc * xc, axis=-1, keepdims=True) + EPS)
    vhat = xc * rstd
    return vhat * ln_g + ln_b, vhat, rstd, dvv


def _gmlp_fwd(z_uv, ln_g, ln_b, w_s, b_s_t, *, name):
    s = z_uv.shape[0]
    w = GMLP_WIDTH
    gd = w // GMLP_GROUPS

    def body(z_ref, lg_ref, lb_ref, ws_ref, bs_ref, a_ref):
        u = _gelu(z_ref[:, :w].astype(F32))
        vn, _, _, _ = _gmlp_norm(z_ref[:, w:].astype(F32), lg_ref[...], lb_ref[...])
        mask = _gmlp_mask()
        for g in range(GMLP_GROUPS):
            wm = jnp.where(mask, ws_ref[g], 0.0)
            mixed = _dot(wm, vn[:, g * gd:(g + 1) * gd], _NN) + bs_ref[:, g:g + 1]
            a_ref[:, g * gd:(g + 1) * gd] = (u[:, g * gd:(g + 1) * gd] * mixed).astype(a_ref.dtype)

    full = lambda shape: pl.BlockSpec(shape, lambda i: (0,) * len(shape))
    return pl.pallas_call(
        body, name=name, grid=(s // GMLP_BLOCK,),
        in_specs=[pl.BlockSpec((GMLP_BLOCK, 2 * w), lambda i: (i, 0)), full((1, w)), full((1, w)),
                  full((GMLP_GROUPS, GMLP_BLOCK, GMLP_BLOCK)), full((GMLP_BLOCK, LANES))],
        out_specs=pl.BlockSpec((GMLP_BLOCK, w), lambda i: (i, 0)),
        out_shape=jax.ShapeDtypeStruct((s, w), BF16), compiler_params=_params(),
    )(z_uv, ln_g, ln_b, w_s, b_s_t)


def _gmlp_bwd(z_uv, da, ln_g, ln_b, w_s, b_s_t, *, name):
    s = z_uv.shape[0]
    w = GMLP_WIDTH
    gd = w // GMLP_GROUPS

    def body(z_ref, da_ref, lg_ref, lb_ref, ws_ref, bs_ref, dz_ref, dws_ref, dbs_ref, dlg_ref, dlb_ref):
        @pl.when(pl.program_id(0) == 0)
        def _():
            dws_ref[...] = jnp.zeros_like(dws_ref)
            dbs_ref[...] = jnp.zeros_like(dbs_ref)
            dlg_ref[...] = jnp.zeros_like(dlg_ref)
            dlb_ref[...] = jnp.zeros_like(dlb_ref)

        u, du_dz = _gelu_and_grad(z_ref[:, :w].astype(F32))
        lg = lg_ref[...]
        vn, vhat, rstd, dvv_dz = _gmlp_norm(z_ref[:, w:].astype(F32), lg, lb_ref[...])
        dav = da_ref[...].astype(F32)
        mask = _gmlp_mask()
        lane = lax.broadcasted_iota(jnp.int32, (GMLP_BLOCK, LANES), 1)
        dvn_parts = []
        dbs = jnp.zeros((GMLP_BLOCK, LANES), F32)
        for g in range(GMLP_GROUPS):
            sl = slice(g * gd, (g + 1) * gd)
            wm = jnp.where(mask, ws_ref[g], 0.0)
            vn_g = vn[:, sl]
            mixed = _dot(wm, vn_g, _NN) + bs_ref[:, g:g + 1]
            dmixed = dav[:, sl] * u[:, sl]
            dz_ref[:, sl] = (dav[:, sl] * mixed * du_dz[:, sl]).astype(dz_ref.dtype)
            dvn_parts.append(_dot(wm, dmixed, _TN))
            dws_ref[g] += jnp.where(mask, _dot(dmixed, vn_g, _NT), 0.0)
            dbs = dbs + jnp.where(lane == g, jnp.sum(dmixed, axis=-1, keepdims=True), 0.0)
        dbs_ref[...] += dbs
        dvn = jnp.concatenate(dvn_parts, axis=-1)
        dlg_ref[...] += jnp.sum(dvn * vhat, axis=0, keepdims=True)
        dlb_ref[...] += jnp.sum(dvn, axis=0, keepdims=True)
        dyg = dvn * lg
        dvv = rstd * (dyg - jnp.mean(dyg, axis=-1, keepdims=True)
                      - vhat * jnp.mean(dyg * vhat, axis=-1, keepdims=True))
        dz_ref[:, w:] = (dvv * dvv_dz).astype(dz_ref.dtype)

    full = lambda shape: pl.BlockSpec(shape, lambda i: (0,) * len(shape))
    return pl.pallas_call(
        body, name=name, grid=(s // GMLP_BLOCK,),
        in_specs=[pl.BlockSpec((GMLP_BLOCK, 2 * w), lambda i: (i, 0)),
                  pl.BlockSpec((GMLP_BLOCK, w), lambda i: (i, 0)), full((1, w)), full((1, w)),
                  full((GMLP_GROUPS, GMLP_BLOCK, GMLP_BLOCK)), full((GMLP_BLOCK, LANES))],
        out_specs=[pl.BlockSpec((GMLP_BLOCK, 2 * w), lambda i: (i, 0)),
                   full((GMLP_GROUPS, GMLP_BLOCK, GMLP_BLOCK)), full((GMLP_BLOCK, LANES)),
                   full((1, w)), full((1, w))],
        out_shape=[jax.ShapeDtypeStruct((s, 2 * w), BF16),
                   jax.ShapeDtypeStruct((GMLP_GROUPS, GMLP_BLOCK, GMLP_BLOCK), F32),
                   jax.ShapeDtypeStruct((GMLP_BLOCK, LANES), F32),
                   jax.ShapeDtypeStruct((1, w), F32), jax.ShapeDtypeStruct((1, w), F32)],
        compiler_params=_params(),
    )(z_uv, da, ln_g, ln_b, w_s, b_s_t)


def _tri(lower):
    r = lax.broadcasted_iota(jnp.int32, (ATT_BLOCK, ATT_BLOCK), 0)
    c = lax.broadcasted_iota(jnp.int32, (ATT_BLOCK, ATT_BLOCK), 1)
    return jnp.where((c <= r) if lower else (c >= r), 1.0, 0.0).astype(F32)


def _log_sigmoid(x):
    return jnp.minimum(x, 0.0) - jnp.log(1.0 + jnp.exp(-jnp.abs(x)))


def _fox_cum(f, b_f, *, name):
    s = f.shape[0]
    nb = s // ATT_BLOCK

    def body(f_ref, b_ref, cb_ref, ct_ref, carry):
        @pl.when(pl.program_id(0) == 0)
        def _():
            carry[...] = jnp.zeros_like(carry)

        lf = _log_sigmoid(f_ref[...] + b_ref[...])
        cum = lax.dot_general(_tri(True), lf, _NN, precision=lax.Precision.HIGHEST,
                              preferred_element_type=F32) + carry[...]
        carry[...] = cum[ATT_BLOCK - 1:ATT_BLOCK, :]
        for h in range(FOX_HEADS):
            cb_ref[h] = jnp.broadcast_to(cum[:, h:h + 1], (ATT_BLOCK, LANES))
        ct_ref[...] = cum.T

    return pl.pallas_call(
        body, name=name, grid=(nb,),
        in_specs=[pl.BlockSpec((ATT_BLOCK, LANES), lambda i: (i, 0)), pl.BlockSpec((1, LANES), lambda i: (0, 0))],
        out_specs=[pl.BlockSpec((FOX_HEADS, ATT_BLOCK, LANES), lambda i: (0, i, 0)),
                   pl.BlockSpec((LANES, ATT_BLOCK), lambda i: (0, i))],
        out_shape=[jax.ShapeDtypeStruct((FOX_HEADS, s, LANES), F32), jax.ShapeDtypeStruct((LANES, s), F32)],
        scratch_shapes=[pltpu.VMEM((1, LANES), F32)], compiler_params=_params(),
    )(f, b_f)


def _fox_dlogit(dcum_t, f, b_f, *, name):
    s = f.shape[0]
    nb = s // ATT_BLOCK

    def body(dc_ref, f_ref, b_ref, df_ref, db_ref, carry):
        @pl.when(pl.program_id(0) == 0)
        def _():
            carry[...] = jnp.zeros_like(carry)
            db_ref[...] = jnp.zeros_like(db_ref)

        d = dc_ref[...].T
        dlog = lax.dot_general(_tri(False), d, _NN, precision=lax.Precision.HIGHEST,
                               preferred_element_type=F32) + carry[...]
        carry[...] = dlog[0:1, :]
        df = dlog * (1.0 - _sigmoid(f_ref[...] + b_ref[...]))
        df_ref[...] = df
        db_ref[...] += jnp.sum(df, axis=0, keepdims=True)

    rev = lambda i: nb - 1 - i
    return pl.pallas_call(
        body, name=name, grid=(nb,),
        in_specs=[pl.BlockSpec((LANES, ATT_BLOCK), lambda i: (0, rev(i))),
                  pl.BlockSpec((ATT_BLOCK, LANES), lambda i: (rev(i), 0)),
                  pl.BlockSpec((1, LANES), lambda i: (0, 0))],
        out_specs=[pl.BlockSpec((ATT_BLOCK, LANES), lambda i: (rev(i), 0)),
                   pl.BlockSpec((1, LANES), lambda i: (0, 0))],
        out_shape=[jax.ShapeDtypeStruct((s, LANES), F32), jax.ShapeDtypeStruct((1, LANES), F32)],
        scratch_shapes=[pltpu.VMEM((1, LANES), F32)], compiler_params=_params(),
    )(dcum_t, f, b_f)


def _head_mask():
    return lax.broadcasted_iota(jnp.int32, (1, LANES), 1) < FOX_HEAD_DIM


ATT_TQ = 256
ATT_TK = 256
ATT_SCALE = FOX_HEAD_DIM ** -0.5
assert ATT_SCALE == 0.125 and ATT_TQ == ATT_TK


def _causal_t(qi, ki):
    kpos = lax.broadcasted_iota(jnp.int32, (ATT_TK, ATT_TQ), 0) + ki * ATT_TK
    qpos = lax.broadcasted_iota(jnp.int32, (ATT_TK, ATT_TQ), 1) + qi * ATT_TQ
    return kpos <= qpos


def _row_mask():
    return lax.broadcasted_iota(jnp.int32, (LANES, 1), 0) < FOX_HEAD_DIM


def _lane_tile(a, width):
    return a if a.shape[1] == width else jnp.tile(a, (1, width // a.shape[1]))


def _transpose_bf16(a):
    return a.astype(F32).T.astype(BF16)


def _attn_fwd_t(qkv, cum_b, cum_r, *, name):
    s = qkv.shape[0]
    nq = s // ATT_TQ
    npair = HEAD_PAIRS

    def body(q_ref, k_ref, v_ref, cq_ref, ck_ref, o_ref, ot_ref, l_ref, vt_ref):
        qi = pl.program_id(1)
        rows = _row_mask()

        @pl.when(qi == 0)
        def _():
            vt_ref[...] = _transpose_bf16(v_ref[...])

        qt = _transpose_bf16(q_ref[...]) * ATT_SCALE
        zero = jnp.zeros_like(qt)
        qts = (jnp.where(rows, qt, zero), jnp.where(rows, zero, qt))

        def step(ki, carry, masked):
            off = pl.multiple_of(ki * ATT_TK, ATT_TK)
            k2 = k_ref[pl.ds(off, ATT_TK), :]
            vt = vt_ref[:, pl.ds(off, ATT_TK)]
            out = []
            for hh in range(2):
                m, l, acc = carry[hh]
                bias = cq_ref[hh:hh + 1, :] - _lane_tile(ck_ref[hh, pl.ds(off, ATT_TK), :], ATT_TQ)
                sc = _dot(k2, qts[hh], _NN) + bias
                if masked:
                    sc = jnp.where(_causal_t(qi, ki), sc, -1e30)
                m_new = jnp.maximum(m, jnp.max(sc, axis=0, keepdims=True))
                alpha = jnp.exp(m - m_new)
                p = jnp.exp(sc - m_new)
                l = alpha * l + jnp.sum(p, axis=0, keepdims=True)
                p_hi = p.astype(BF16)
                p_lo = (p - p_hi.astype(F32)).astype(BF16)
                acc = alpha * acc + (_dot(vt, p_hi, _NN) + _dot(vt, p_lo, _NN))
                out.append((m_new, l, acc))
            return tuple(out)

        init = tuple((jnp.full((1, ATT_TQ), -1e30, F32), jnp.zeros((1, ATT_TQ), F32),
                      jnp.zeros((LANES, ATT_TQ), F32)) for _ in range(2))
        carry = lax.fori_loop(0, qi // 2, lambda kk, c: step(2 * kk + 1, step(2 * kk, c, False), False), init)
        carry = lax.cond(qi % 2 == 1, lambda c: step(qi - 1, c, False), lambda c: c, carry)
        (ma, la, acca), (mb, lb, accb) = step(qi, carry, True)
        ot = jnp.where(rows, acca / la, accb / lb)
        ot_ref[...] = ot
        o_ref[...] = ot.T.astype(o_ref.dtype)
        l_ref[0:1, :] = ma + jnp.log(la)
        l_ref[1:2, :] = mb + jnp.log(lb)

    row = pl.BlockSpec((None, 2, ATT_TQ), lambda j, i: (j, 0, i))
    return pl.pallas_call(
        body, name=name, grid=(npair, nq),
        in_specs=[pl.BlockSpec((ATT_TQ, LANES), lambda j, i: (i, j)),
                  pl.BlockSpec((s, LANES), lambda j, i: (0, npair + j)),
                  pl.BlockSpec((s, LANES), lambda j, i: (0, 2 * npair + j)),
                  row, pl.BlockSpec((None, 2, s, LANES), lambda j, i: (j, 0, 0, 0))],
        out_specs=[pl.BlockSpec((ATT_TQ, LANES), lambda j, i: (i, j)),
                   pl.BlockSpec((LANES, ATT_TQ), lambda j, i: (j, i)), row],
        out_shape=[jax.ShapeDtypeStruct((s, FOX_WIDTH), BF16), jax.ShapeDtypeStruct((FOX_WIDTH, s), F32),
                   jax.ShapeDtypeStruct((npair, 2, s), F32)],
        scratch_shapes=[pltpu.VMEM((LANES, s), BF16)],
        compiler_params=_params(),
    )(qkv, qkv, qkv, cum_r, cum_b)


def _attn_bwd_t(qkv, do, o_t, lse, cum_b, cum_r, *, name, dep=None):
    s = qkv.shape[0]
    nq = s // ATT_TQ
    npair = HEAD_PAIRS

    deps = [] if dep is None else [dep]

    def body(q_ref, k_ref, v_ref, do_ref, ot_ref, l_ref, cq_ref, ck_ref, *rest):
        dq_ref, dk_ref, dv_ref, dc_ref, qt_ref, dot_ref, dqt_ref, dl_ref = rest[len(deps):]
        ki = pl.program_id(1)
        m0 = _head_mask()
        rows = _row_mask()
        k2 = k_ref[...]
        v2 = v_ref[...]
        kt = _transpose_bf16(k2)
        ks = k2 * ATT_SCALE
        kz, vz = jnp.zeros_like(k2), jnp.zeros_like(v2)
        khs = (jnp.where(m0, ks, kz), jnp.where(m0, kz, ks))
        vhs = (jnp.where(m0, v2, vz), jnp.where(m0, vz, v2))
        cks = tuple(_lane_tile(ck_ref[hh], ATT_TQ) for hh in range(2))

        @pl.when(ki == 0)
        def _():
            dqt_ref[...] = jnp.zeros_like(dqt_ref)
            qt_ref[...] = _transpose_bf16(q_ref[...])
            do_t = do_ref[...].astype(F32).T
            dot_ref[...] = do_t.astype(BF16)
            prod = do_t * ot_ref[...]
            dl_ref[0:1, :] = jnp.sum(prod[:FOX_HEAD_DIM], axis=0, keepdims=True)
            dl_ref[1:2, :] = jnp.sum(prod[FOX_HEAD_DIM:], axis=0, keepdims=True)

        def step(qi, carry, masked):
            off = pl.multiple_of(qi * ATT_TQ, ATT_TQ)
            q2 = q_ref[pl.ds(off, ATT_TQ), :]
            do2 = do_ref[pl.ds(off, ATT_TQ), :]
            qt = qt_ref[:, pl.ds(off, ATT_TQ)]
            dot_ = dot_ref[:, pl.ds(off, ATT_TQ)]
            out, dqs = [], []
            for hh in range(2):
                dk_acc, dv_acc, dc_acc = carry[hh]
                sc = _dot(khs[hh], qt, _NN) + (cq_ref[hh:hh + 1, pl.ds(off, ATT_TQ)] - cks[hh])
                p = jnp.exp(sc - l_ref[hh:hh + 1, pl.ds(off, ATT_TQ)])
                if masked:
                    p = jnp.where(_causal_t(qi, ki), p, 0.0)
                dp = _dot(vhs[hh], dot_, _NN)
                ds = p * (dp - dl_ref[hh:hh + 1, pl.ds(off, ATT_TQ)])
                dc_acc = dc_acc - jnp.sum(ds, axis=1, keepdims=True)
                dss = (ds * ATT_SCALE).astype(BF16)
                dv_acc = dv_acc + _dot(p, do2, _NN)
                dk_acc = dk_acc + _dot(dss, q2, _NN)
                dqs.append(_dot(kt, dss, _NN))
                out.append((dk_acc, dv_acc, dc_acc))
            dqt_ref[:, pl.ds(off, ATT_TQ)] += jnp.where(rows, dqs[0], dqs[1])
            return tuple(out)

        init = tuple((jnp.zeros((ATT_TK, LANES), F32), jnp.zeros((ATT_TK, LANES), F32),
                      jnp.zeros((ATT_TK, 1), F32)) for _ in range(2))
        carry = step(ki, init, True)
        rest = nq - 1 - ki
        carry = lax.fori_loop(
            0, rest // 2, lambda t, c: step(ki + 2 + 2 * t, step(ki + 1 + 2 * t, c, False), False), carry)
        carry = lax.cond(rest % 2 == 1, lambda c: step(nq - 1, c, False), lambda c: c, carry)
        (dka, dva, dca), (dkb, dvb, dcb) = carry
        dk_ref[...] = jnp.where(m0, dka, dkb).astype(dk_ref.dtype)
        dv_ref[...] = jnp.where(m0, dva, dvb).astype(dv_ref.dtype)
        dc_ref[0] = jnp.broadcast_to(dca, (ATT_TK, LANES))
        dc_ref[1] = jnp.broadcast_to(dcb, (ATT_TK, LANES))

        @pl.when(ki == nq - 1)
        def _():
            dq_ref[...] = dqt_ref[...].T.astype(dq_ref.dtype)

    colfull = lambda base: pl.BlockSpec((s, LANES), lambda j, i: (0, base + j))
    colblk = lambda base: pl.BlockSpec((ATT_TK, LANES), lambda j, i: (i, base + j))
    stat = pl.BlockSpec((None, 2, s), lambda j, i: (j, 0, 0))
    bcast = pl.BlockSpec((None, 2, ATT_TK, LANES), lambda j, i: (j, 0, i, 0))
    grad = jax.ShapeDtypeStruct((s, FOX_WIDTH), BF16)
    return pl.pallas_call(
        body, name=name, grid=(npair, nq),
        in_specs=[colfull(0), colblk(npair), colblk(2 * npair), colfull(0),
                  pl.BlockSpec((LANES, s), lambda j, i: (j, 0)), stat, stat, bcast]
                 + [pl.BlockSpec(memory_space=pl.ANY)] * len(deps),
        out_specs=[colfull(0), colblk(0), colblk(0), bcast],
        out_shape=[grad, grad, grad, jax.ShapeDtypeStruct((npair, 2, s, LANES), F32)],
        scratch_shapes=[pltpu.VMEM((LANES, s), BF16), pltpu.VMEM((LANES, s), BF16), pltpu.VMEM((LANES, s), F32),
                        pltpu.VMEM((2, s), F32)],
        compiler_params=_params(),
    )(qkv, qkv, qkv, do, o_t, lse, cum_r, cum_b, *deps)


def _merge_fwd(zg, ya, yb, *, name, tm=512):
    s, d = ya.shape
    tm = _tile(s, tm)

    def body(zg_ref, ya_ref, yb_ref, m_ref):
        ga = _sigmoid(zg_ref[:, :d].astype(F32))
        gb = _sigmoid(zg_ref[:, d:].astype(F32))
        m_ref[...] = (ga * ya_ref[...].astype(F32) + gb * yb_ref[...].astype(F32)).astype(m_ref.dtype)

    row = pl.BlockSpec((tm, d), lambda i: (i, 0))
    row2 = pl.BlockSpec((tm, 2 * d), lambda i: (i, 0))
    return pl.pallas_call(
        body, name=name, grid=(s // tm,), in_specs=[row2, row, row], out_specs=row,
        out_shape=jax.ShapeDtypeStruct((s, d), BF16), compiler_params=_params(),
    )(zg, ya, yb)


def _merge_bwd(dm, zg, ya, yb, *, name, tm=512):
    s, d = ya.shape
    tm = _tile(s, tm)

    def body(dm_ref, zg_ref, ya_ref, yb_ref, dzg_ref, dya_ref, dyb_ref):
        dmv = dm_ref[...].astype(F32)
        ga = _sigmoid(zg_ref[:, :d].astype(F32))
        gb = _sigmoid(zg_ref[:, d:].astype(F32))
        dzg_ref[:, :d] = (dmv * ya_ref[...].astype(F32) * ga * (1.0 - ga)).astype(dzg_ref.dtype)
        dzg_ref[:, d:] = (dmv * yb_ref[...].astype(F32) * gb * (1.0 - gb)).astype(dzg_ref.dtype)
        dya_ref[...] = (dmv * ga).astype(dya_ref.dtype)
        dyb_ref[...] = (dmv * gb).astype(dyb_ref.dtype)

    row = pl.BlockSpec((tm, d), lambda i: (i, 0))
    row2 = pl.BlockSpec((tm, 2 * d), lambda i: (i, 0))
    return pl.pallas_call(
        body, name=name, grid=(s // tm,), in_specs=[row, row2, row, row], out_specs=[row2, row, row],
        out_shape=[jax.ShapeDtypeStruct((s, 2 * d), BF16), jax.ShapeDtypeStruct((s, d), BF16),
                   jax.ShapeDtypeStruct((s, d), BF16)],
        compiler_params=_params(),
    )(dm, zg, ya, yb)


SUBLANES = 8


def _shift_down(u, k, row):
    rolled = pltpu.roll(u, k, 0)
    head = jnp.where(row[:SUBLANES] >= k, rolled[:SUBLANES], 0.0)
    return jnp.concatenate([head, rolled[SUBLANES:]], axis=0)


def _shift_up(u, k, row):
    n = u.shape[0]
    rolled = pltpu.roll(u, n - k, 0)
    tail = jnp.where(row[n - SUBLANES:] < n - k, rolled[n - SUBLANES:], 0.0)
    return jnp.concatenate([rolled[:n - SUBLANES], tail], axis=0)


def _conv_act_fwd(up_a, up_b, cw_a, cw_b, cb_a, cb_b, *, name, tc=128):
    s, f = up_a.shape
    tc = _tile(f, tc)

    def body(ua_ref, ub_ref, wa_ref, wb_ref, ba_ref, bb_ref, act_ref):
        row = lax.broadcasted_iota(jnp.int32, (s, tc), 0)

        def conv(u_ref, w_ref, b_ref):
            u = u_ref[...].astype(F32)
            return (b_ref[...] + w_ref[0:1, :] * _shift_down(u, 2, row)
                    + w_ref[1:2, :] * _shift_down(u, 1, row) + w_ref[2:3, :] * u)

        ca = conv(ua_ref, wa_ref, ba_ref)
        cb = conv(ub_ref, wb_ref, bb_ref)
        act_ref[...] = (_gelu(ca) * cb).astype(act_ref.dtype)

    col = pl.BlockSpec((s, tc), lambda j: (0, j))
    w3 = pl.BlockSpec((3, tc), lambda j: (0, j))
    b1 = pl.BlockSpec((1, tc), lambda j: (0, j))
    return pl.pallas_call(
        body, name=name, grid=(f // tc,), in_specs=[col, col, w3, w3, b1, b1], out_specs=col,
        out_shape=jax.ShapeDtypeStruct((s, f), BF16), compiler_params=_params(),
    )(up_a, up_b, cw_a, cw_b, cb_a, cb_b)


def _conv_act_bwd(up_a, up_b, dact, cw_a, cw_b, cb_a, cb_b, *, name, tc=128):
    s, f = up_a.shape
    tc = _tile(f, tc)

    def body(ua_ref, ub_ref, da_ref, wa_ref, wb_ref, ba_ref, bb_ref, dua_ref, dub_ref, dwa_ref, dwb_ref):
        row = lax.broadcasted_iota(jnp.int32, (s, tc), 0)

        def conv(u_ref, w_ref, b_ref):
            u = u_ref[...].astype(F32)
            u1 = _shift_down(u, 1, row)
            u2 = _shift_down(u, 2, row)
            return u, u1, u2, b_ref[...] + w_ref[0:1, :] * u2 + w_ref[1:2, :] * u1 + w_ref[2:3, :] * u

        def back(dc, taps, w_ref, du_ref, dw_ref):
            u, u1, u2 = taps
            dw_ref[0:1, :] = jnp.sum(dc * u2, axis=0, keepdims=True)
            dw_ref[1:2, :] = jnp.sum(dc * u1, axis=0, keepdims=True)
            dw_ref[2:3, :] = jnp.sum(dc * u, axis=0, keepdims=True)
            dw_ref[3:4, :] = jnp.sum(dc, axis=0, keepdims=True)
            du = (w_ref[2:3, :] * dc + w_ref[1:2, :] * _shift_up(dc, 1, row)
                  + w_ref[0:1, :] * _shift_up(dc, 2, row))
            du_ref[...] = du.astype(du_ref.dtype)

        ua, ua1, ua2, ca = conv(ua_ref, wa_ref, ba_ref)
        ub, ub1, ub2, cb = conv(ub_ref, wb_ref, bb_ref)
        g, dg = _gelu_and_grad(ca)
        dact_v = da_ref[...].astype(F32)
        back(dact_v * cb * dg, (ua, ua1, ua2), wa_ref, dua_ref, dwa_ref)
        back(dact_v * g, (ub, ub1, ub2), wb_ref, dub_ref, dwb_ref)

    col = pl.BlockSpec((s, tc), lambda j: (0, j))
    w3 = pl.BlockSpec((3, tc), lambda j: (0, j))
    w4 = pl.BlockSpec((4, tc), lambda j: (0, j))
    b1 = pl.BlockSpec((1, tc), lambda j: (0, j))
    return pl.pallas_call(
        body, name=name, grid=(f // tc,), in_specs=[col, col, col, w3, w3, b1, b1],
        out_specs=[col, col, w4, w4],
        out_shape=[jax.ShapeDtypeStruct((s, f), BF16), jax.ShapeDtypeStruct((s, f), BF16),
                   jax.ShapeDtypeStruct((4, f), F32), jax.ShapeDtypeStruct((4, f), F32)],
        compiler_params=_params(),
    )(up_a, up_b, dact, cw_a, cw_b, cb_a, cb_b)


def _ple_final(x2, ple, zp, target, g_final, *, name, tm=512):
    s, d = x2.shape
    tm = _tile(s, tm)

    def body(x_ref, ple_ref, zp_ref, t_ref, g_ref, dx_ref, dple_ref, dzp_ref, dg_ref, loss_ref):
        @pl.when(pl.program_id(0) == 0)
        def _():
            dg_ref[...] = jnp.zeros_like(dg_ref)
            loss_ref[...] = jnp.zeros_like(loss_ref)

        gp = _sigmoid(zp_ref[...].astype(F32))
        plev = ple_ref[...].astype(F32)
        x3 = x_ref[...] + plev * gp
        r = lax.rsqrt(jnp.mean(x3 * x3, axis=-1, keepdims=True) + EPS)
        xhat = x3 * r
        gv = g_ref[...]
        diff = xhat * gv - t_ref[...]
        loss_ref[...] += 0.5 * jnp.sum(jnp.mean(diff * diff, axis=-1, keepdims=True), axis=0, keepdims=True)
        dy = diff * (1.0 / d)
        dg_ref[...] += jnp.sum(dy * xhat, axis=0, keepdims=True)
        dyg = dy * gv
        dx3 = r * (dyg - xhat * jnp.mean(dyg * xhat, axis=-1, keepdims=True))
        dx_ref[...] = dx3
        dple_ref[...] = (dx3 * gp).astype(dple_ref.dtype)
        dzp_ref[...] = (dx3 * plev * gp * (1.0 - gp)).astype(dzp_ref.dtype)

    row = pl.BlockSpec((tm, d), lambda i: (i, 0))
    vec = pl.BlockSpec((1, d), lambda i: (0, 0))
    return pl.pallas_call(
        body, name=name, grid=(s // tm,), in_specs=[row, row, row, row, vec],
        out_specs=[row, row, row, vec, pl.BlockSpec((1, LANES), lambda i: (0, 0))],
        out_shape=[jax.ShapeDtypeStruct((s, d), F32), jax.ShapeDtypeStruct((s, d), BF16),
                   jax.ShapeDtypeStruct((s, d), BF16), jax.ShapeDtypeStruct((1, d), F32),
                   jax.ShapeDtypeStruct((1, LANES), F32)],
        compiler_params=_params(),
    )(x2, ple, zp, target, g_final)


def _device_step(x, p, target, w, get_w_in=None, get_w_rest=None, on_grads_ffn=None, on_grads_small=None,
                 on_grads_mix=None, on_after_dh=None):
    s = x.shape[0]
    g = {}
    w = dict(w)

    h = _rms_fwd(x, w["norm_mix_g"], name="rms_mix", dep=w.get("first_dep"))
    if get_w_in is not None:
        w.update(get_w_in(h))
    qkv = _mm(h, w["w_qkv"], mode="nn", out_dtype=BF16, name="proj_qkv", tm=1024)
    f = _mm(h, w["w_f"], mode="nn", out_dtype=F32, name="proj_f", tm=1024)

    cum_b, cum_t = _fox_cum(f, w["b_f"], name="fox_cum")
    cum_b = cum_b.reshape(HEAD_PAIRS, 2, s, LANES)
    cum_r = cum_t[:FOX_HEADS].reshape(HEAD_PAIRS, 2, s)
    b, o_t, lse = _attn_fwd_t(qkv, cum_b, cum_r, name="attn_fwd")

    dep = get_w_rest[0](b) if get_w_rest is not None else None
    z_uv = _mm(h, w["w_uv"], mode="nn", out_dtype=BF16, name="proj_uv", tm=1024, dep=dep)
    zg = _mm(h, w["w_g"], mode="nn", out_dtype=BF16, name="proj_gate", tm=1024, dep=dep)
    a = _gmlp_fwd(z_uv, w["gmlp_ln_g"], w["gmlp_ln_b"], w["gmlp_w_s"], w["gmlp_b_s_t"], name="gmlp_fwd")
    if get_w_rest is not None:
        w.update(get_w_rest[1]([a, zg]))

    ya = _mm(a, w["w_branch_a"], mode="nn", out_dtype=BF16, name="branch_a", tm=1024)
    yb = _mm(b, w["w_branch_b"], mode="nn", out_dtype=BF16, name="branch_b", tm=1024)
    merged = _merge_fwd(zg, ya, yb, name="merge_fwd")
    x1 = _mm(merged, w["w_out"], mode="nn", out_dtype=F32, name="proj_out", add=x, tm=1024)

    h2 = _rms_fwd(x1, w["norm_ffn_g"], name="rms_ffn")
    up_a = _mm(h2, w["w_up_a"], mode="nn", out_dtype=BF16, name="up_a", tm=1024, tn=D_FF // 2)
    up_b = _mm(h2, w["w_up_b"], mode="nn", out_dtype=BF16, name="up_b", tm=1024, tn=D_FF // 2)
    cw, cb = w["conv_w"], w["conv_b"]
    conv_args = (cw[:, :D_FF], cw[:, D_FF:], cb[:, :D_FF], cb[:, D_FF:])
    act = _conv_act_fwd(up_a, up_b, *conv_args, name="conv_act_fwd")
    x2 = _mm(act, w["w_down"], mode="nn", out_dtype=F32, name="down", add=x1, tm=512)

    h3 = _rms_fwd(x2, w["norm_ple_g"], name="rms_ple")
    ple = _mm(p, w["w_ple"], mode="nn", out_dtype=BF16, name="ple_proj", tm=1024)
    zp = _mm(h3, w["w_ple_gate"], mode="nn", out_dtype=BF16, name="ple_gate", tm=1024)
    dx3, dple, dzp, g["norm_final_g"], loss = _ple_final(x2, ple, zp, target, w["norm_final_g"], name="ple_final")

    g["w_ple"] = _mm(p, dple, mode="tn", out_dtype=BF16, name="dw_ple")
    g["w_ple_gate"] = _mm(h3, dzp, mode="tn", out_dtype=BF16, name="dw_ple_gate")
    dh3 = _mm(dzp, w["w_ple_gate"], mode="nt", out_dtype=BF16, name="dh3")
    dx2, dx2_b, g["norm_ple_g"] = _rms_bwd(x2, w["norm_ple_g"], dh3, dx3, name="rms_ple_bwd")

    g["w_down"] = _mm(act, dx2_b, mode="tn", out_dtype=BF16, name="dw_down", tm=D_FF // 2)
    dact = _mm(dx2_b, w["w_down"], mode="nt", out_dtype=BF16, name="dact", tn=D_FF // 2)
    dup_a, dup_b, dcw_a, dcw_b = _conv_act_bwd(up_a, up_b, dact, *conv_args, name="conv_act_bwd")
    g["conv_w"] = jnp.concatenate([dcw_a[:3], dcw_b[:3]], axis=1)
    g["conv_b"] = jnp.concatenate([dcw_a[3:], dcw_b[3:]], axis=1)
    g["w_up_a"] = _mm(h2, dup_a, mode="tn", out_dtype=BF16, name="dw_up_a", tn=D_FF // 2)
    g["w_up_b"] = _mm(h2, dup_b, mode="tn", out_dtype=BF16, name="dw_up_b", tn=D_FF // 2)
    dh2 = _mm_nt_sum([(dup_a, w["w_up_a"]), (dup_b, w["w_up_b"])], out_dtype=BF16, name="dh2")
    dx1, dx1_b, g["norm_ffn_g"] = _rms_bwd(x1, w["norm_ffn_g"], dh2, dx2, name="rms_ffn_bwd")

    g["w_out"] = _mm(merged, dx1_b, mode="tn", out_dtype=BF16, name="dw_out")
    dmerged = _mm(dx1_b, w["w_out"], mode="nt", out_dtype=BF16, name="dmerged")
    dzg, dya, dyb = _merge_bwd(dmerged, zg, ya, yb, name="merge_bwd")
    g["w_branch_a"] = _mm(a, dya, mode="tn", out_dtype=BF16, name="dw_branch_a")
    g["w_branch_b"] = _mm(b, dyb, mode="tn", out_dtype=BF16, name="dw_branch_b")
    dep = on_grads_ffn(g) if on_grads_ffn is not None else None
    da = _mm(dya, w["w_branch_a"], mode="nt", out_dtype=BF16, name="da", dep=dep)
    db = _mm(dyb, w["w_branch_b"], mode="nt", out_dtype=BF16, name="db")

    dz_uv, g["gmlp_w_s"], dbs_t, g["gmlp_ln_g"], g["gmlp_ln_b"] = _gmlp_bwd(
        z_uv, da, w["gmlp_ln_g"], w["gmlp_ln_b"], w["gmlp_w_s"], w["gmlp_b_s_t"], name="gmlp_bwd")
    g["gmlp_b_s"] = dbs_t[:, :GMLP_GROUPS].T
    dep = on_grads_small(g) if on_grads_small is not None else None

    dq, dk, dv, dcum_b = _attn_bwd_t(qkv, db, o_t, lse, cum_b, cum_r, name="attn_bwd", dep=dep)
    dcum_t = jnp.pad(dcum_b[..., 0].reshape(FOX_HEADS, s), ((0, LANES - FOX_HEADS), (0, 0)))
    df, g["b_f"] = _fox_dlogit(dcum_t, f, w["b_f"], name="fox_dlogit")
    dqkv = jnp.concatenate([dq, dk, dv], axis=1)

    g["w_uv"] = _mm(h, dz_uv, mode="tn", out_dtype=BF16, name="dw_uv")
    g["w_qkv"] = _mm(h, dqkv, mode="tn", out_dtype=BF16, name="dw_qkv")
    g["w_f"] = _mm(h, df, mode="tn", out_dtype=BF16, name="dw_f")
    g["w_g"] = _mm(h, dzg, mode="tn", out_dtype=BF16, name="dw_g")
    dep = on_grads_mix(g) if on_grads_mix is not None else None
    dh = _mm_nt_sum([(dz_uv, w["w_uv"]), (dqkv, w["w_qkv"]), (df, w["w_f"]), (dzg, w["w_g"])],
                    out_dtype=BF16, name="dh", dep=dep)
    dep = on_after_dh(dh) if on_after_dh is not None else None
    dx0, _, g["norm_mix_g"] = _rms_bwd(x, w["norm_mix_g"], dh, dx1, name="rms_mix_bwd", dep=dep)
    return loss, dx0, g


def _coords():
    return lax.axis_index("x"), lax.axis_index("y"), lax.axis_index("c")


def _other_chips(x, y):
    return [(1 - x, y), (x, 1 - y), (1 - x, 1 - y)]


def _remote(src, dst, send_sem, recv_sem, dev):
    return pltpu.make_async_remote_copy(src_ref=src, dst_ref=dst, send_sem=send_sem, recv_sem=recv_sem,
                                        device_id=dev, device_id_type=MESH)


_ANY = pl.BlockSpec(memory_space=pl.ANY)


def _pair_exchange(gs, *, name):
    n = len(gs)

    def body(*refs):
        ins, outs = refs[:n], refs[n:2 * n]
        send_sems, recv_sems = refs[2 * n:]
        x, y, c = _coords()
        copies = []
        for i in range(n):
            for j in range(N_CHIPS):
                cp = _remote(ins[i].at[j, 1 - c], outs[i].at[j], send_sems.at[i, j], recv_sems.at[i, j], (x, y, 1 - c))
                cp.start()
                copies.append(cp)
        for cp in copies:
            cp.wait()

    return pl.pallas_call(
        body, name=name, in_specs=[_ANY] * n, out_specs=[_ANY] * n,
        out_shape=[jax.ShapeDtypeStruct((N_CHIPS,) + a.shape[2:], a.dtype) for a in gs],
        scratch_shapes=[pltpu.SemaphoreType.DMA((n, N_CHIPS)), pltpu.SemaphoreType.DMA((n, N_CHIPS))],
        compiler_params=_params(),
    )(*gs)


def _pair_share(hs, *, name):
    n = len(hs)

    def body(*refs):
        ins, outs = refs[:n], refs[n:2 * n]
        send_sems, recv_sems = refs[2 * n:]
        x, y, c = _coords()
        copies = []
        for i in range(n):
            cp = _remote(ins[i], outs[i], send_sems.at[i], recv_sems.at[i], (x, y, 1 - c))
            cp.start()
            copies.append(cp)
        for cp in copies:
            cp.wait()

    return pl.pallas_call(
        body, name=name, in_specs=[_ANY] * n, out_specs=[_ANY] * n,
        out_shape=[jax.ShapeDtypeStruct(a.shape, a.dtype) for a in hs],
        scratch_shapes=[pltpu.SemaphoreType.DMA((n,)), pltpu.SemaphoreType.DMA((n,))],
        compiler_params=_params(),
    )(*hs)


def _all_exchange(vec, *, name):
    def body(v_ref, o_ref, send_sems, recv_sems, local_sem):
        x, y, c = _coords()
        me = 4 * x + 2 * y + c
        local = pltpu.make_async_copy(v_ref, o_ref.at[me], local_sem)
        local.start()
        copies = []
        k = 0
        for dx in (0, 1):
            for dy in (0, 1):
                for dc in (0, 1):
                    if dx or dy or dc:
                        peer = (1 - x if dx else x, 1 - y if dy else y, 1 - c if dc else c)
                        cp = _remote(v_ref, o_ref.at[me], send_sems.at[k], recv_sems.at[k], peer)
                        cp.start()
                        copies.append(cp)
                        k += 1
        for cp in copies:
            cp.wait()
        local.wait()

    return pl.pallas_call(
        body, name=name, in_specs=[_ANY], out_specs=_ANY,
        out_shape=jax.ShapeDtypeStruct((8,) + vec.shape, vec.dtype),
        scratch_shapes=[pltpu.SemaphoreType.DMA((7,)), pltpu.SemaphoreType.DMA((7,)), pltpu.SemaphoreType.DMA(())],
        compiler_params=_params(),
    )(vec)


_HBM = pl.BlockSpec(memory_space=pltpu.HBM)
_SEM = pl.BlockSpec(memory_space=pltpu.SEMAPHORE)
_EFFECT = pltpu.SideEffectType.DATAFLOW_SIDE_EFFECTING


def _copies_start(srcs, lands, plan, n_copies, *, name, after=()):
    ns, n = len(srcs), len(srcs) + len(lands)
    na = len(after)

    def body(*refs):
        send_sems, recv_sems = refs[n + na], refs[n + na + 1]
        token = refs[-1]
        for k, (src, dst, dev) in enumerate(plan(refs[:ns], refs[ns:n])):
            _remote(src, dst, send_sems.at[k], recv_sems.at[k], dev).start()
        token[...] = jnp.zeros_like(token)

    arrays = list(srcs) + list(lands)
    outs = pl.pallas_call(
        body, name=name,
        out_shape=(pltpu.SemaphoreType.DMA((n_copies,)), pltpu.SemaphoreType.DMA((n_copies,)),
                   *[pltpu.HBM(a.shape, a.dtype) for a in arrays], jax.ShapeDtypeStruct((8, LANES), F32)),
        in_specs=[_HBM] * n + [_ANY] * na,
        out_specs=(_SEM, _SEM, *[_HBM] * n, pl.BlockSpec(memory_space=pltpu.VMEM)),
        input_output_aliases={i: 2 + i for i in range(n)},
        compiler_params=pltpu.CompilerParams(has_side_effects=_EFFECT),
    )(*[pltpu.with_memory_space_constraint(a, pltpu.HBM) for a in arrays], *after)
    return outs[0], outs[1], list(outs[2:2 + ns]), list(outs[2 + ns:2 + n]), outs[-1]


def _copies_wait(send_sems, recv_sems, srcs, lands, plan, first, after, *, name):
    ns, n = len(srcs), len(srcs) + len(lands)

    def body(*refs):
        send, recv = refs[n], refs[n + 1]
        for k, (src, dst, dev) in enumerate(plan(refs[:ns], refs[ns:n])):
            cp = _remote(src, dst, send.at[first + k], recv.at[first + k], dev)
            cp.wait_send()
            cp.wait_recv()

    arrays = list(srcs) + list(lands)
    outs = pl.pallas_call(
        body, name=name, out_shape=tuple(pltpu.HBM(a.shape, a.dtype) for a in arrays),
        in_specs=[_HBM] * n + [_SEM, _SEM] + [_ANY] * len(after), out_specs=tuple([_HBM] * n),
        input_output_aliases={i: i for i in range(n)},
        compiler_params=pltpu.CompilerParams(has_side_effects=_EFFECT),
    )(*arrays, send_sems, recv_sems, *after)
    return list(outs[:ns]), list(outs[ns:])


def _gather_plan(halved):
    def plan(srcs, lands):
        x, y, c = _coords()
        me = 2 * x + y
        out = []
        for i, (src, land) in enumerate(zip(srcs, lands)):
            if halved[i]:
                h = src.shape[0] // 2
                rows = pl.ds(pl.multiple_of(c * h, 16), h)
                src, dst = src.at[rows], land.at[me, rows]
            else:
                dst = land.at[me]
            out += [(src, dst, (cx, cy, c)) for cx, cy in _other_chips(x, y)]
        return out
    return plan


def _forward_halves(lands, *, name):
    n = len(lands)

    def body(*refs):
        ins, outs = refs[:n], refs[n:2 * n]
        send_sems, recv_sems = refs[2 * n:]
        x, y, c = _coords()
        copies = []
        for i in range(n):
            h = ins[i].shape[1] // 2
            rows = pl.ds(pl.multiple_of(c * h, 16), h)
            for k, (cx, cy) in enumerate(_other_chips(x, y)):
                cp = _remote(ins[i].at[2 * cx + cy, rows], outs[i].at[2 * cx + cy, rows],
                             send_sems.at[i, k], recv_sems.at[i, k], (x, y, 1 - c))
                cp.start()
                copies.append(cp)
        for cp in copies:
            cp.wait()

    return pl.pallas_call(
        body, name=name, in_specs=[_ANY] * n, out_specs=[_ANY] * n,
        out_shape=[jax.ShapeDtypeStruct(a.shape, a.dtype) for a in lands],
        input_output_aliases={i: i for i in range(n)},
        scratch_shapes=[pltpu.SemaphoreType.DMA((n, 3)), pltpu.SemaphoreType.DMA((n, 3))],
        compiler_params=_params(),
    )(*lands)


def _forward_plan(srcs, lands):
    x, y, c = _coords()
    out = []
    for land in lands:
        h = land.shape[1] // 2
        rows = pl.ds(pl.multiple_of(c * h, 16), h)
        for cx, cy in _other_chips(x, y):
            view = land.at[2 * cx + cy, rows]
            out.append((view, view, (x, y, 1 - c)))
    return out


def _share_plan(srcs, lands):
    x, y, c = _coords()
    return [(src, land, (x, y, 1 - c)) for src, land in zip(srcs, lands)]


def _pair_plan(srcs, lands):
    x, y, c = _coords()
    out = []
    for src, land in zip(srcs, lands):
        out += [(src.at[j, 1 - c], land.at[j], (x, y, 1 - c)) for j in range(N_CHIPS)]
    return out


def _all_plan(srcs, lands):
    x, y, c = _coords()
    me = 4 * x + 2 * y + c
    out = []
    for src, land in zip(srcs, lands):
        for dx in (0, 1):
            for dy in (0, 1):
                for dc in (0, 1):
                    if dx or dy or dc:
                        out.append((src, land.at[me], (1 - x if dx else x, 1 - y if dy else y, 1 - c if dc else c)))
    return out


def _chip_plan(srcs, lands):
    x, y, c = _coords()
    me = 2 * x + y
    out = []
    for src, land in zip(srcs, lands):
        out += [(src.at[2 * cx + cy], land.at[me], (cx, cy, c)) for cx, cy in _other_chips(x, y)]
    return out


ROW_BLOCK_BYTES = 2 * 1024 * 1024


def _rtile(r, pref, mult, row_bytes=None):
    if row_bytes is not None:
        pref = max(pref, ROW_BLOCK_BYTES // row_bytes)
    t = (min(r, pref) // mult) * mult
    while t >= mult:
        if r % t == 0:
            return t
        t -= mult
    return r


def _pair_add(g, recv, core, *, name):
    _, _, r2, cols = g.shape
    tr = _rtile(r2, 256, 16, row_bytes=2 * cols)

    def body(c_ref, g_ref, r_ref, o_ref):
        o_ref[...] = (g_ref[...].astype(F32) + r_ref[...].astype(F32)).astype(o_ref.dtype)

    blk = pl.BlockSpec((None, tr, cols), lambda j, i, c_ref: (j, i, 0))
    return pl.pallas_call(
        body, name=name,
        grid_spec=pltpu.PrefetchScalarGridSpec(
            num_scalar_prefetch=1, grid=(N_CHIPS, r2 // tr),
            in_specs=[pl.BlockSpec((None, None, tr, cols), lambda j, i, c_ref: (j, c_ref[0], i, 0)), blk],
            out_specs=blk),
        out_shape=jax.ShapeDtypeStruct(recv.shape, recv.dtype), compiler_params=_params(),
    )(core, g, recv)


def _sum_slots(a, out_dtype, *, name):
    n, r, cols = a.shape
    whole = n * r * cols * a.dtype.itemsize <= 4 * ROW_BLOCK_BYTES
    tr = r if whole else _rtile(r, 256, 16)

    def body(a_ref, o_ref):
        acc = a_ref[0].astype(F32)
        for j in range(1, n):
            acc = acc + a_ref[j].astype(F32)
        o_ref[...] = acc.astype(o_ref.dtype)

    return pl.pallas_call(
        body, name=name, grid=(r // tr,),
        in_specs=[pl.BlockSpec((n, tr, cols), lambda i: (0, i, 0))],
        out_specs=pl.BlockSpec((tr, cols), lambda i: (i, 0)),
        out_shape=jax.ShapeDtypeStruct((r, cols), out_dtype), compiler_params=_params(),
    )(a)


def _chip_sum(own, recv, chip, *, name):
    _, r2, cols = own.shape
    tr = _rtile(r2, 256, 16, row_bytes=2 * cols)

    def body(chip_ref, own_ref, *rest):
        o_ref = rest[-1]
        acc = None
        for j in range(N_CHIPS):
            term = jnp.where(chip_ref[0] == j, own_ref[...], rest[j][...]).astype(F32)
            acc = term if acc is None else acc + term
        o_ref[...] = acc

    def slot(j):
        return pl.BlockSpec((None, tr, cols),
                            lambda i, chip_ref: (jnp.where(chip_ref[0] == j, (j + 1) % N_CHIPS, j), i, 0))

    return pl.pallas_call(
        body, name=name,
        grid_spec=pltpu.PrefetchScalarGridSpec(
            num_scalar_prefetch=1, grid=(r2 // tr,),
            in_specs=[pl.BlockSpec((None, tr, cols), lambda i, chip_ref: (chip_ref[0], i, 0))]
                     + [slot(j) for j in range(N_CHIPS)],
            out_specs=pl.BlockSpec((tr, cols), lambda i, chip_ref: (i, 0))),
        out_shape=jax.ShapeDtypeStruct((r2, cols), F32), compiler_params=_params(),
    )(chip, own, *([recv] * N_CHIPS))


def _adam_update(w, gv, m, v):
    c1 = 1.0 / (1.0 - ADAM_B1 ** ADAM_STEP)
    c2 = 1.0 / (1.0 - ADAM_B2 ** ADAM_STEP)
    nm = ADAM_B1 * m + (1.0 - ADAM_B1) * gv
    nv = ADAM_B2 * v + (1.0 - ADAM_B2) * gv * gv
    return -ADAM_LR * ((nm * c1) / (jnp.sqrt(nv * c2) + ADAM_EPS) + ADAM_WD * w), nm, nv


def _adamw_halves(w, g_mine, g_other, m, v, core, *, name):
    r, cols = w.shape
    r2 = r // 2
    tr = _rtile(r2, 256, 8, row_bytes=4 * cols)
    nt = r2 // tr

    def body(core_ref, w_ref, gm_ref, go_ref, m_ref, v_ref, g_ref, d_ref, nm_ref, nv_ref):
        gv = jnp.where(pl.program_id(0) == core_ref[0], gm_ref[...], go_ref[...])
        g_ref[...] = gv
        d_ref[...], nm_ref[...], nv_ref[...] = _adam_update(w_ref[...], gv, m_ref[...], v_ref[...])

    full = pl.BlockSpec((tr, cols), lambda hf, i, core_ref: (hf * nt + i, 0))
    half = pl.BlockSpec((tr, cols), lambda hf, i, core_ref: (i, 0))
    shape = jax.ShapeDtypeStruct((r, cols), F32)
    return pl.pallas_call(
        body, name=name,
        grid_spec=pltpu.PrefetchScalarGridSpec(
            num_scalar_prefetch=1, grid=(2, nt), in_specs=[full, half, half, full, full], out_specs=[full] * 4),
        out_shape=[shape] * 4, compiler_params=_params(),
    )(core, w, g_mine, g_other, m, v)


def _adamw(w, g, m, v, *, name, rows=256):
    r, cols = w.shape
    tr = _rtile(r, rows, 8)

    def body(w_ref, g_ref, m_ref, v_ref, d_ref, nm_ref, nv_ref):
        d_ref[...], nm_ref[...], nv_ref[...] = _adam_update(w_ref[...], g_ref[...], m_ref[...], v_ref[...])

    blk = pl.BlockSpec((tr, cols), lambda i: (i, 0))
    shape = jax.ShapeDtypeStruct((r, cols), F32)
    return pl.pallas_call(
        body, name=name, grid=(r // tr,), in_specs=[blk] * 4, out_specs=[blk] * 3,
        out_shape=[shape] * 3, compiler_params=_params(),
    )(w, g, m, v)


_BIG = (("w_in", 1), ("w_branch_a", 0), ("w_branch_b", 0), ("w_out", 0), ("w_up", 1), ("w_down", 0),
        ("w_ple", 1), ("w_ple_gate", 0))
_SMALL = ("gmlp_ln_g", "gmlp_ln_b", "gmlp_w_s", "gmlp_b_s", "norm_ffn_g", "conv_b", "norm_ple_g", "norm_final_g",
          "b_f", "norm_mix_g")
N_LATE = 2
_WEIGHTS = ("norm_mix_g", "w_in", "b_f", "gmlp_ln_g", "gmlp_ln_b", "gmlp_w_s", "gmlp_b_s", "w_branch_a",
            "w_branch_b", "w_out", "norm_ffn_g", "w_up", "conv_w", "conv_b", "w_down", "norm_ple_g", "w_ple",
            "w_ple_gate", "norm_final_g")
_PACK_ROWS = 8


def _pack(arrays):
    parts = []
    for a in arrays:
        flat = a.reshape(-1)
        unit = _PACK_ROWS * LANES
        flat = jnp.pad(flat, (0, (-flat.shape[0]) % unit))
        parts.append(flat.reshape(-1, LANES))
    return jnp.concatenate(parts, axis=0)


def _unpack(packed, shapes):
    out, row = [], 0
    for shp in shapes:
        size = math.prod(shp)
        rows = -(-size // (_PACK_ROWS * LANES)) * _PACK_ROWS
        out.append(packed[row:row + rows].reshape(-1)[:size].reshape(shp))
        row += rows
    return out


def _take_cols(parts, lo, hi):
    out, start = [], 0
    for a in parts:
        width = a.shape[1]
        a0, a1 = max(lo, start) - start, min(hi, start + width) - start
        if a1 > a0:
            out.append(a if (a0, a1) == (0, width) else a[:, a0:a1])
        start += width
    return out[0] if len(out) == 1 else jnp.concatenate(out, axis=1)


def _assemble(gathered, axis):
    n, r, cols = gathered.shape
    if axis == 0:
        return gathered.reshape(n * r, cols)
    return _take_cols([gathered[j] for j in range(n)], 0, n * cols)


def _to_chunks(parts, axis):
    rows, total = parts[0].shape[0], sum(a.shape[1] for a in parts)
    if axis == 0:
        r, cols = rows // N_CHIPS, total
        chunks = _take_cols(parts, 0, total).reshape(N_CHIPS, r, cols)
    else:
        r, cols = rows, total // N_CHIPS
        chunks = jnp.stack([_take_cols(parts, j * cols, (j + 1) * cols) for j in range(N_CHIPS)])
    return chunks.reshape(N_CHIPS, 2, r // 2, cols)


def kernel(x, p, norm_mix_g, w_in, b_f, gmlp_ln_g, gmlp_ln_b, gmlp_w_s, gmlp_b_s, w_branch_a, w_branch_b, w_out, norm_ffn_g, w_up, conv_w, conv_b, w_down, norm_ple_g, w_ple, w_ple_gate, norm_final_g, loss_target, m_norm_mix_g, m_w_in, m_b_f, m_gmlp_ln_g, m_gmlp_ln_b, m_gmlp_w_s, m_gmlp_b_s, m_w_branch_a, m_w_branch_b, m_w_out, m_norm_ffn_g, m_w_up, m_conv_w, m_conv_b, m_w_down, m_norm_ple_g, m_w_ple, m_w_ple_gate, m_norm_final_g, v_norm_mix_g, v_w_in, v_b_f, v_gmlp_ln_g, v_gmlp_ln_b, v_gmlp_w_s, v_gmlp_b_s, v_w_branch_a, v_w_branch_b, v_w_out, v_norm_ffn_g, v_w_up, v_conv_w, v_conv_b, v_w_down, v_norm_ple_g, v_w_ple, v_w_ple_gate, v_norm_final_g):
    args = dict(locals())
    wt = {n: args[n] for n in _WEIGHTS}
    mom = {n: args["m_" + n] for n in _WEIGHTS}
    var = {n: args["v_" + n] for n in _WEIGHTS}
    chip = 2 * lax.axis_index("x") + lax.axis_index("y")
    core = lax.axis_index("c").astype(jnp.int32).reshape(1)

    chip1 = chip.astype(jnp.int32).reshape(1)
    device = 2 * chip + lax.axis_index("c")
    axis_of = dict(_BIG)
    names = [n for n, _ in _BIG]
    put_mine = lambda land, mine: lax.dynamic_update_index_in_dim(land, mine, chip, 0)

    shard_in = w_in[0].astype(BF16)
    sems_in = _copies_start([shard_in], [lax.empty((N_CHIPS,) + shard_in.shape, BF16)], _gather_plan([True]), 3,
                            name="gather_start_in")
    _, wt["w_in"], mom["w_in"], var["w_in"] = lax.optimization_barrier((sems_in[4], w_in, m_w_in, v_w_in))
    shards = [wt[n][0].astype(BF16) for n in names[1:]] + [conv_w[0]]
    halved = [True] * len(names[1:]) + [False]
    lands = [lax.empty((N_CHIPS,) + a.shape, a.dtype) for a in shards]
    send_sems, recv_sems, srcs, lands, rest_token = _copies_start(
        shards, lands, _gather_plan(halved), 3 * len(shards), name="gather_start_rest", after=[sems_in[4]])
    o1 = 2 * GMLP_WIDTH
    o2 = o1 + 3 * FOX_WIDTH
    o3 = o2 + FOX_HEADS
    fpad = ((0, 0), (0, LANES - FOX_HEADS))
    w = {
        "conv_b": conv_b, "norm_mix_g": norm_mix_g, "norm_ffn_g": norm_ffn_g, "norm_ple_g": norm_ple_g,
        "norm_final_g": norm_final_g.reshape(1, D_MODEL), "b_f": jnp.pad(b_f, fpad),
        "gmlp_ln_g": gmlp_ln_g, "gmlp_ln_b": gmlp_ln_b, "gmlp_w_s": gmlp_w_s[0],
        "gmlp_b_s_t": jnp.pad(gmlp_b_s[0].T, ((0, 0), (0, LANES - GMLP_GROUPS))),
        "first_dep": rest_token,
    }

    def get_w_in(after):
        early = [a.reshape(a.shape[-2:]) for a in (wt["w_in"], mom["w_in"], var["w_in"])]
        _, got = _copies_wait(sems_in[0], sems_in[1], sems_in[2], sems_in[3], _gather_plan([True]), 0,
                              [after] + early, name="gather_wait_in")
        got = _forward_halves(got, name="gather_forward_in")
        slots = put_mine(got[0], shard_in)
        slots = [slots[j] for j in range(N_CHIPS)]
        return {"w_uv": _take_cols(slots, 0, o1), "w_qkv": _take_cols(slots, o1, o2),
                "w_f": jnp.pad(_take_cols(slots, o2, o3), fpad), "w_g": _take_cols(slots, o3, o3 + 2 * D_MODEL)}

    def start_w_rest(after):
        _, got = _copies_wait(send_sems, recv_sems, srcs, lands, _gather_plan(halved), 0, [after],
                              name="gather_wait_rest")
        ssem, rsem, _, fwd, token = _copies_start([], got[:-1], _forward_plan, 3 * len(got[:-1]),
                                                  name="gather_forward_start")
        pending["forward"] = (ssem, rsem, fwd, got[-1])
        return token

    def get_w_rest(after):
        ssem, rsem, fwd, whole = pending["forward"]
        _, fwd = _copies_wait(ssem, rsem, [], fwd, _forward_plan, 0, after, name="gather_forward_wait")
        got = fwd + [whole]
        slots = {n: put_mine(got[i], shards[i]) for i, n in enumerate(names[1:])}
        full = {n: _assemble(slots[n], axis_of[n]) for n in names[1:] if n != "w_up"}
        up = [slots["w_up"][j] for j in range(N_CHIPS)]
        return {"w_branch_a": full["w_branch_a"], "w_branch_b": full["w_branch_b"], "w_out": full["w_out"],
                "w_up_a": _take_cols(up, 0, D_FF), "w_up_b": _take_cols(up, D_FF, 2 * D_FF),
                "w_down": full["w_down"], "w_ple": full["w_ple"], "w_ple_gate": full["w_ple_gate"],
                "conv_w": _assemble(put_mine(got[-1], shards[-1]), 1)}

    grads, delta, new_m, new_v = {}, {}, {}, {}
    pending = {}

    def to_chunks(n, gr):
        return _to_chunks(gr if isinstance(gr, list) else [gr], axis_of[n])

    def pair_start(group, gfull, tag):
        chunks = [to_chunks(n, gfull[n]) for n in group]
        empty = [lax.empty((N_CHIPS,) + a.shape[2:], a.dtype) for a in chunks]
        ssem, rsem, own, recv, token = _copies_start(chunks, empty, _pair_plan, N_CHIPS * len(group),
                                                     name="grad_pair_start_" + tag)
        pending["pair_" + tag] = (ssem, rsem, own, recv)
        return token

    def reduce_start(group, gfull, tag, after=None):
        if after is None:
            chunks = [to_chunks(n, gfull[n]) for n in group]
            from_sibling = _pair_exchange(chunks, name="grad_pair_exchange_" + tag)
        else:
            ssem, rsem, own, recv = pending["pair_" + tag]
            chunks, from_sibling = _copies_wait(ssem, rsem, own, recv, _pair_plan, 0, after,
                                                name="grad_pair_wait_" + tag)
        pair_sums = [_pair_add(chunks[i], from_sibling[i], core, name="grad_pair_add_" + n) for i, n in enumerate(group)]
        empty = [lax.empty(a.shape, a.dtype) for a in pair_sums]
        ssem, rsem, own, recv, token = _copies_start(pair_sums, empty, _chip_plan, 3 * len(group),
                                                     name="grad_chip_start_" + tag)
        pending[tag] = (ssem, rsem, own, recv)
        return token

    def reduce_sum(group, tag, after):
        ssem, rsem, own, recv = pending[tag]
        own, recv = _copies_wait(ssem, rsem, own, recv, _chip_plan, 0, after, name="grad_chip_wait_" + tag)
        halves = [_chip_sum(own[i], recv[i], chip1, name="grad_chip_sum_" + n) for i, n in enumerate(group)]
        empty = [lax.empty(a.shape, a.dtype) for a in halves]
        ssem, rsem, halves, other, token = _copies_start(halves, empty, _share_plan, len(group),
                                                        name="grad_share_start_" + tag)
        pending["share_" + tag] = (ssem, rsem, halves, other)
        return token

    def reduce_update(group, tag, after):
        ssem, rsem, halves, other = pending["share_" + tag]
        halves, other_halves = _copies_wait(ssem, rsem, halves, other, _share_plan, 0, after,
                                            name="grad_share_wait_" + tag)
        for i, n in enumerate(group):
            shp = wt[n].shape
            outs = _adamw_halves(wt[n].reshape(shp[-2:]), halves[i], other_halves[i], mom[n].reshape(shp[-2:]),
                                 var[n].reshape(shp[-2:]), core, name="adamw_" + n)
            grads[n], delta[n], new_m[n], new_v[n] = (o.reshape(shp) for o in outs)
        return new_v[group[-1]]

    def reduce_finish(group, tag, after):
        ssem, rsem, own, recv = pending[tag]
        own, recv = _copies_wait(ssem, rsem, own, recv, _chip_plan, 0, after, name="grad_chip_wait_" + tag)
        halves = [_chip_sum(own[i], recv[i], chip1, name="grad_chip_sum_" + n) for i, n in enumerate(group)]
        other_halves = _pair_share(halves, name="grad_pair_share_" + tag)
        for i, n in enumerate(group):
            shp = wt[n].shape
            outs = _adamw_halves(wt[n].reshape(shp[-2:]), halves[i], other_halves[i], mom[n].reshape(shp[-2:]),
                                 var[n].reshape(shp[-2:]), core, name="adamw_" + n)
            grads[n], delta[n], new_m[n], new_v[n] = (o.reshape(shp) for o in outs)
        return new_v[group[-1]]

    ffn_group = ("w_up", "w_down", "w_ple", "w_ple_gate", "w_branch_a", "w_branch_b", "w_out")
    mix_group = ("w_in",)

    def on_grads_ffn(g):
        gfull = dict(g)
        gfull["w_up"] = [g["w_up_a"], g["w_up_b"]]
        return pair_start(ffn_group, gfull, "ffn")

    def on_grads_small(g):
        chip_token = reduce_start(ffn_group, None, "ffn", after=[g["gmlp_w_s"]])
        vec = _pack([g[n] for n in _SMALL[:-N_LATE]] + [g["conv_w"]])
        ssem, rsem, own, recv, token = _copies_start(
            [vec], [lax.empty((8,) + vec.shape, F32)], _all_plan, 7, name="small_start", after=[chip_token])
        pending["small"] = (ssem, rsem, own, recv)
        return token

    def on_grads_mix(g):
        gfull = dict(g)
        gfull["w_in"] = [g["w_uv"], g["w_qkv"], g["w_f"][:, :FOX_HEADS], g["w_g"]]
        return reduce_start(mix_group, gfull, "mix")

    def on_after_dh(dh):
        return reduce_sum(ffn_group, "ffn", [dh])

    loss, grad_x, g = _device_step(x[0], p[0, 0], loss_target[0], w, get_w_in, (start_w_rest, get_w_rest), on_grads_ffn,
                                   on_grads_small, on_grads_mix, on_after_dh)

    ffn_done = reduce_update(ffn_group, "ffn", [grad_x])
    mix_done = reduce_finish(mix_group, "mix", [ffn_done])
    ssem, rsem, own, recv = pending["small"]
    own, recv = _copies_wait(ssem, rsem, own, recv, _all_plan, 0, [mix_done], name="small_wait")
    vec_early = _sum_slots(lax.dynamic_update_index_in_dim(recv[0], own[0], device, 0), F32, name="small_sum")
    vec_late = _pack([g["b_f"][:, :FOX_HEADS], g["norm_mix_g"]])
    vec_late = _sum_slots(_all_exchange(vec_late, name="small_exchange_late"), F32, name="small_sum_late")
    early_rows = _pack([wt[n] for n in _SMALL[:-N_LATE]]).shape[0]
    vec = jnp.concatenate([vec_early[:early_rows], vec_late], axis=0)
    for n, a in zip(_SMALL, _unpack(vec, [wt[n].shape for n in _SMALL])):
        grads[n] = a
    conv_w_grad = _unpack(vec_early[early_rows:], [(3, 2 * D_FF)])[0]
    grads["conv_w"] = lax.dynamic_slice_in_dim(conv_w_grad, chip * conv_w.shape[2], conv_w.shape[2], axis=1).reshape(conv_w.shape)

    shp = conv_w.shape
    outs = _adamw(conv_w.reshape(shp[-2:]), grads["conv_w"].reshape(shp[-2:]), m_conv_w.reshape(shp[-2:]),
                  v_conv_w.reshape(shp[-2:]), name="adamw_conv_w")
    delta["conv_w"], new_m["conv_w"], new_v["conv_w"] = (o.reshape(shp) for o in outs)
    outs = _adamw(_pack([wt[n] for n in _SMALL]), vec, _pack([mom[n] for n in _SMALL]),
                  _pack([var[n] for n in _SMALL]), name="adamw_small", rows=2048)
    for d, o in zip((delta, new_m, new_v), outs):
        for n, a in zip(_SMALL, _unpack(o, [wt[n].shape for n in _SMALL])):
            d[n] = a

    total_loss = lax.psum(loss[0, 0], ("x", "y", "c"))
    return (total_loss, grad_x.reshape(x.shape), *[grads[n] for n in _WEIGHTS], *[delta[n] for n in _WEIGHTS],
            *[new_m[n] for n in _WEIGHTS], *[new_v[n] for n in _WEIGHTS])
```
